```python
import jax, jax.numpy as jnp
from jax import lax
import numpy as np

D_MODEL = 1024
BATCH = 8
SEQ = 4096
DEPTH = 1

PLE_DIM = 256
ROPE_THETA = 10000.0
RMS_EPS = 1e-6
BLOCK = 128

SWA_WINDOW = 128
A_HEADS = 8
A_KV_HEADS = 2
A_HEAD_DIM = 64
A_WIDTH = A_HEADS * A_HEAD_DIM

B_HEADS = 8
Q_LORA = 256
KV_LORA = 128
NOPE_DIM = 64
ROPE_DIM = 32
V_DIM = 64
B_WIDTH = B_HEADS * V_DIM

SPLIT_SIZES = (A_HEADS * A_HEAD_DIM,
               A_KV_HEADS * A_HEAD_DIM,
               A_KV_HEADS * A_HEAD_DIM,
               Q_LORA,
               KV_LORA,
               ROPE_DIM,
               2 * D_MODEL)
IN_COLS = 768 + 416 + 2 * D_MODEL

D_FF = 2816
CONV_W = 3

kernel_name = "hybrid_swa_mla_gated_block"


def split_cols(t, sizes):
    idx = np.cumsum(np.array(sizes))[:-1].tolist()
    return jnp.split(t, idx, axis=-1)


def rmsnorm(t, g):
    tf = t.astype(jnp.float32)
    y = tf * lax.rsqrt(jnp.mean(tf * tf, axis=-1, keepdims=True) + RMS_EPS)
    return (y * g.astype(jnp.float32)).astype(t.dtype)


def rope_tables(positions, dim):
    inv = ROPE_THETA ** (-(jnp.arange(0, dim, 2, dtype=jnp.float32) / dim))
    ang = positions.astype(jnp.float32)[..., None] * inv
    return jnp.cos(ang), jnp.sin(ang)


def apply_rope(t, cos, sin):
    tf = t.astype(jnp.float32)
    t1, t2 = jnp.split(tf, 2, axis=-1)
    c, s = cos[:, :, None, :], sin[:, :, None, :]
    return jnp.concatenate([t1 * c - t2 * s, t2 * c + t1 * s], axis=-1).astype(t.dtype)


def swa_attention(q, k, v, sinks):
    B_, S_, H, d = q.shape
    G = H // A_KV_HEADS
    nblk = S_ // BLOCK
    qb = q.reshape(B_, nblk, BLOCK, A_KV_HEADS, G, d)
    pad = ((0, 0), (BLOCK, 0), (0, 0), (0, 0))
    kp = jnp.pad(k, pad).reshape(B_, nblk + 1, BLOCK, A_KV_HEADS, d)
    vp = jnp.pad(v, pad).reshape(B_, nblk + 1, BLOCK, A_KV_HEADS, d)
    kb = jnp.concatenate([kp[:, :-1], kp[:, 1:]], axis=2)
    vb = jnp.concatenate([vp[:, :-1], vp[:, 1:]], axis=2)
    s = jnp.einsum('bnqkgd,bnskd->bnkgqs', qb, kb).astype(jnp.float32) * (d ** -0.5)
    qi = jnp.arange(BLOCK)[:, None]
    kj = jnp.arange(2 * BLOCK)[None, :]
    rel = qi + BLOCK - kj
    band = (rel >= 0) & (rel < SWA_WINDOW)
    key_abs = jnp.arange(nblk)[:, None, None] * BLOCK + kj[None] - BLOCK
    valid = band[None] & (key_abs >= 0)
    s = jnp.where(valid[None, :, None, None], s, -jnp.inf)
    sink = sinks.astype(jnp.float32).reshape(1, 1, A_KV_HEADS, G, 1, 1)
    m = jnp.maximum(jnp.max(s, axis=-1, keepdims=True), sink)
    e = jnp.exp(s - m)
    pr = e / (jnp.sum(e, axis=-1, keepdims=True) + jnp.exp(sink - m))
    out = jnp.einsum('bnkgqs,bnskd->bnqkgd', pr.astype(v.dtype), vb)
    return out.reshape(B_, S_, H * d)


def mla_attention(q, k, v):
    B_, S_, H, dqk = q.shape
    dv = v.shape[-1]
    nblk = S_ // BLOCK
    scale = dqk ** -0.5
    qb = q.reshape(B_, nblk, BLOCK, H, dqk).transpose(1, 0, 2, 3, 4)
    key_pos = jnp.arange(S_)

    def one_block(args):
        qblk, n = args
        s = jnp.einsum('bqhd,bshd->bhqs', qblk, k).astype(jnp.float32) * scale
        q_pos = n * BLOCK + jnp.arange(BLOCK)
        causal = key_pos[None, :] <= q_pos[:, None]
        pr = jax.nn.softmax(jnp.where(causal, s, -jnp.inf), axis=-1)
        return jnp.einsum('bhqs,bshd->bqhd', pr.astype(v.dtype), v)

    out = lax.map(one_block, (qb, jnp.arange(nblk)))
    return out.transpose(1, 0, 2, 3, 4).reshape(B_, S_, H * dv)


def causal_dwconv(u, w, b):
    C = u.shape[-1]
    y = lax.conv_general_dilated(u, w[:, None, :].astype(u.dtype), window_strides=(1,),
                                 padding=[(CONV_W - 1, 0)],
                                 dimension_numbers=('NWC', 'WIO', 'NWC'),
                                 feature_group_count=C)
    return y + b


def _fwd_setup_inputs(seed: int = 0) -> dict:
    key = jax.random.key(seed)
    ks = jax.random.split(key, 24)
    f32 = jnp.float32

    def w(k, shape, fan_in):
        return jax.random.normal(k, shape, f32) * (fan_in ** -0.5)

    def gain(k, dim):
        return 1.0 + 0.02 * jax.random.normal(k, (DEPTH, dim), f32)

    x = jax.random.normal(ks[0], (BATCH, SEQ, D_MODEL), f32)
    p = jax.random.normal(ks[1], (DEPTH, BATCH, SEQ, PLE_DIM), f32)
    start = jax.random.randint(ks[2], (BATCH, 1), 0, 1024, dtype=jnp.int32)
    positions = (start + jnp.arange(SEQ, dtype=jnp.int32)[None, :]).astype(jnp.int32)
    return {
        "x": x,
        "p": p,
        "positions": positions,
        "attn_pre_norm": gain(ks[3], D_MODEL),
        "attn_post_norm": gain(ks[4], D_MODEL),
        "w_in": w(ks[5], (DEPTH, D_MODEL, IN_COLS), D_MODEL),
        "b_gate": 0.01 * jax.random.normal(ks[6], (DEPTH, 2 * D_MODEL), f32),
        "sinks": 0.5 * jax.random.normal(ks[7], (DEPTH, A_HEADS), f32),
        "q_a_norm": gain(ks[8], Q_LORA),
        "w_uq": w(ks[9], (DEPTH, Q_LORA, B_HEADS * (NOPE_DIM + ROPE_DIM)), Q_LORA),
        "kv_a_norm": gain(ks[10], KV_LORA),
        "w_ukv": w(ks[11], (DEPTH, KV_LORA, B_HEADS * (NOPE_DIM + V_DIM)), KV_LORA),
        "w_branch_a": w(ks[12], (DEPTH, A_WIDTH, D_MODEL), A_WIDTH),
        "w_branch_b": w(ks[13], (DEPTH, B_WIDTH, D_MODEL), B_WIDTH),
        "w_out": w(ks[14], (DEPTH, D_MODEL, D_MODEL), D_MODEL),
        "mlp_pre_norm": gain(ks[15], D_MODEL),
        "mlp_post_norm": gain(ks[16], D_MODEL),
        "w_up": w(ks[17], (DEPTH, D_MODEL, 2 * D_FF), D_MODEL),
        "conv_w": w(ks[18], (DEPTH, CONV_W, 2 * D_FF), CONV_W),
        "conv_b": 0.01 * jax.random.normal(ks[19], (DEPTH, 2 * D_FF), f32),
        "w_down": w(ks[20], (DEPTH, D_FF, D_MODEL), D_FF),
        "ple_norm": gain(ks[21], D_MODEL),
        "w_ple_gate": w(ks[22], (DEPTH, D_MODEL, D_MODEL), D_MODEL),
        "w_ple": w(ks[23], (DEPTH, PLE_DIM, D_MODEL), PLE_DIM),
    }


def _fwd_reference(x, p, positions, attn_pre_norm, attn_post_norm, w_in, b_gate, sinks,
              q_a_norm, w_uq, kv_a_norm, w_ukv, w_branch_a, w_branch_b, w_out,
              mlp_pre_norm, mlp_post_norm, w_up, conv_w, conv_b, w_down,
              ple_norm, w_ple_gate, w_ple):
    B_, S_, _ = x.shape
    cos_a, sin_a = rope_tables(positions, A_HEAD_DIM)
    cos_b, sin_b = rope_tables(positions, ROPE_DIM)
    for i in range(DEPTH):
        h = rmsnorm(x, attn_pre_norm[i])
        qa, ka, va, cq, ckv, kr, gates = split_cols(h @ w_in[i], SPLIT_SIZES)

        qa = apply_rope(qa.reshape(B_, S_, A_HEADS, A_HEAD_DIM), cos_a, sin_a)
        ka = apply_rope(ka.reshape(B_, S_, A_KV_HEADS, A_HEAD_DIM), cos_a, sin_a)
        va = va.reshape(B_, S_, A_KV_HEADS, A_HEAD_DIM)
        ya = swa_attention(qa, ka, va, sinks[i])

        qb = (rmsnorm(cq, q_a_norm[i]) @ w_uq[i]).reshape(B_, S_, B_HEADS, NOPE_DIM + ROPE_DIM)
        q_nope, q_pe = split_cols(qb, (NOPE_DIM, ROPE_DIM))
        q_pe = apply_rope(q_pe, cos_b, sin_b)
        kvb = (rmsnorm(ckv, kv_a_norm[i]) @ w_ukv[i]).reshape(B_, S_, B_HEADS, NOPE_DIM + V_DIM)
        k_nope, vb = split_cols(kvb, (NOPE_DIM, V_DIM))
        k_pe = apply_rope(kr[:, :, None, :], cos_b, sin_b)
        qb = jnp.concatenate([q_nope, q_pe], axis=-1)
        kb = jnp.concatenate([k_nope, jnp.broadcast_to(k_pe, (B_, S_, B_HEADS, ROPE_DIM))], axis=-1)
        yb = mla_attention(qb, kb, vb)

        gate_a, gate_b = jnp.split(jax.nn.sigmoid(gates + b_gate[i]), 2, axis=-1)
        mixed = gate_a * (ya @ w_branch_a[i]) + gate_b * (yb @ w_branch_b[i])
        x = x + rmsnorm(mixed @ w_out[i], attn_post_norm[i])

        h = rmsnorm(x, mlp_pre_norm[i])
        u = causal_dwconv(h @ w_up[i], conv_w[i], conv_b[i])
        u_gate, u_val = jnp.split(u, 2, axis=-1)
        ff = (jax.nn.gelu(u_gate, approximate=True) * u_val) @ w_down[i]
        x = x + rmsnorm(ff, mlp_post_norm[i])

        e = p[i] @ w_ple[i]
        x = x + jax.nn.sigmoid(rmsnorm(x, ple_norm[i]) @ w_ple_gate[i]) * e
    return x


import jax as _jax
import jax.numpy as _jnp

TWIN_FORMAT = 'train_step'
FWD_PARAMS = ['x', 'p', 'positions', 'attn_pre_norm', 'attn_post_norm', 'w_in', 'b_gate', 'sinks', 'q_a_norm', 'w_uq', 'kv_a_norm', 'w_ukv', 'w_branch_a', 'w_branch_b', 'w_out', 'mlp_pre_norm', 'mlp_post_norm', 'w_up', 'conv_w', 'conv_b', 'w_down', 'ple_norm', 'w_ple_gate', 'w_ple']
TWIN_WEIGHTS = ['attn_pre_norm', 'attn_post_norm', 'w_in', 'b_gate', 'sinks', 'q_a_norm', 'w_uq', 'kv_a_norm', 'w_ukv', 'w_branch_a', 'w_branch_b', 'w_out', 'mlp_pre_norm', 'mlp_post_norm', 'w_up', 'conv_w', 'conv_b', 'w_down', 'ple_norm', 'w_ple_gate', 'w_ple']
TWIN_DIFF_INPUT = 'x'
TWIN_INPUTS = ['x', 'p', 'positions', 'attn_pre_norm', 'attn_post_norm', 'w_in', 'b_gate', 'sinks', 'q_a_norm', 'w_uq', 'kv_a_norm', 'w_ukv', 'w_branch_a', 'w_branch_b', 'w_out', 'mlp_pre_norm', 'mlp_post_norm', 'w_up', 'conv_w', 'conv_b', 'w_down', 'ple_norm', 'w_ple_gate', 'w_ple', 'loss_target', 'm_attn_pre_norm', 'm_attn_post_norm', 'm_w_in', 'm_b_gate', 'm_sinks', 'm_q_a_norm', 'm_w_uq', 'm_kv_a_norm', 'm_w_ukv', 'm_w_branch_a', 'm_w_branch_b', 'm_w_out', 'm_mlp_pre_norm', 'm_mlp_post_norm', 'm_w_up', 'm_conv_w', 'm_conv_b', 'm_w_down', 'm_ple_norm', 'm_w_ple_gate', 'm_w_ple', 'v_attn_pre_norm', 'v_attn_post_norm', 'v_w_in', 'v_b_gate', 'v_sinks', 'v_q_a_norm', 'v_w_uq', 'v_kv_a_norm', 'v_w_ukv', 'v_w_branch_a', 'v_w_branch_b', 'v_w_out', 'v_mlp_pre_norm', 'v_mlp_post_norm', 'v_w_up', 'v_conv_w', 'v_conv_b', 'v_w_down', 'v_ple_norm', 'v_w_ple_gate', 'v_w_ple']
TWIN_OUTPUTS = ['loss', 'grad_x', 'grad_attn_pre_norm', 'grad_attn_post_norm', 'grad_w_in', 'grad_b_gate', 'grad_sinks', 'grad_q_a_norm', 'grad_w_uq', 'grad_kv_a_norm', 'grad_w_ukv', 'grad_w_branch_a', 'grad_w_branch_b', 'grad_w_out', 'grad_mlp_pre_norm', 'grad_mlp_post_norm', 'grad_w_up', 'grad_conv_w', 'grad_conv_b', 'grad_w_down', 'grad_ple_norm', 'grad_w_ple_gate', 'grad_w_ple', 'delta_attn_pre_norm', 'delta_attn_post_norm', 'delta_w_in', 'delta_b_gate', 'delta_sinks', 'delta_q_a_norm', 'delta_w_uq', 'delta_kv_a_norm', 'delta_w_ukv', 'delta_w_branch_a', 'delta_w_branch_b', 'delta_w_out', 'delta_mlp_pre_norm', 'delta_mlp_post_norm', 'delta_w_up', 'delta_conv_w', 'delta_conv_b', 'delta_w_down', 'delta_ple_norm', 'delta_w_ple_gate', 'delta_w_ple', 'new_m_attn_pre_norm', 'new_m_attn_post_norm', 'new_m_w_in', 'new_m_b_gate', 'new_m_sinks', 'new_m_q_a_norm', 'new_m_w_uq', 'new_m_kv_a_norm', 'new_m_w_ukv', 'new_m_w_branch_a', 'new_m_w_branch_b', 'new_m_w_out', 'new_m_mlp_pre_norm', 'new_m_mlp_post_norm', 'new_m_w_up', 'new_m_conv_w', 'new_m_conv_b', 'new_m_w_down', 'new_m_ple_norm', 'new_m_w_ple_gate', 'new_m_w_ple', 'new_v_attn_pre_norm', 'new_v_attn_post_norm', 'new_v_w_in', 'new_v_b_gate', 'new_v_sinks', 'new_v_q_a_norm', 'new_v_w_uq', 'new_v_kv_a_norm', 'new_v_w_ukv', 'new_v_w_branch_a', 'new_v_w_branch_b', 'new_v_w_out', 'new_v_mlp_pre_norm', 'new_v_mlp_post_norm', 'new_v_w_up', 'new_v_conv_w', 'new_v_conv_b', 'new_v_w_down', 'new_v_ple_norm', 'new_v_w_ple_gate', 'new_v_w_ple']
TWIN_LEAF_KINDS = {'loss': 'loss', 'grad_x': 'grad_x', 'grad_attn_pre_norm': 'grad_w', 'grad_attn_post_norm': 'grad_w', 'grad_w_in': 'grad_w', 'grad_b_gate': 'grad_w', 'grad_sinks': 'grad_w', 'grad_q_a_norm': 'grad_w', 'grad_w_uq': 'grad_w', 'grad_kv_a_norm': 'grad_w', 'grad_w_ukv': 'grad_w', 'grad_w_branch_a': 'grad_w', 'grad_w_branch_b': 'grad_w', 'grad_w_out': 'grad_w', 'grad_mlp_pre_norm': 'grad_w', 'grad_mlp_post_norm': 'grad_w', 'grad_w_up': 'grad_w', 'grad_conv_w': 'grad_w', 'grad_conv_b': 'grad_w', 'grad_w_down': 'grad_w', 'grad_ple_norm': 'grad_w', 'grad_w_ple_gate': 'grad_w', 'grad_w_ple': 'grad_w', 'delta_attn_pre_norm': 'delta_w', 'delta_attn_post_norm': 'delta_w', 'delta_w_in': 'delta_w', 'delta_b_gate': 'delta_w', 'delta_sinks': 'delta_w', 'delta_q_a_norm': 'delta_w', 'delta_w_uq': 'delta_w', 'delta_kv_a_norm': 'delta_w', 'delta_w_ukv': 'delta_w', 'delta_w_branch_a': 'delta_w', 'delta_w_branch_b': 'delta_w', 'delta_w_out': 'delta_w', 'delta_mlp_pre_norm': 'delta_w', 'delta_mlp_post_norm': 'delta_w', 'delta_w_up': 'delta_w', 'delta_conv_w': 'delta_w', 'delta_conv_b': 'delta_w', 'delta_w_down': 'delta_w', 'delta_ple_norm': 'delta_w', 'delta_w_ple_gate': 'delta_w', 'delta_w_ple': 'delta_w', 'new_m_attn_pre_norm': 'new_m', 'new_m_attn_post_norm': 'new_m', 'new_m_w_in': 'new_m', 'new_m_b_gate': 'new_m', 'new_m_sinks': 'new_m', 'new_m_q_a_norm': 'new_m', 'new_m_w_uq': 'new_m', 'new_m_kv_a_norm': 'new_m', 'new_m_w_ukv': 'new_m', 'new_m_w_branch_a': 'new_m', 'new_m_w_branch_b': 'new_m', 'new_m_w_out': 'new_m', 'new_m_mlp_pre_norm': 'new_m', 'new_m_mlp_post_norm': 'new_m', 'new_m_w_up': 'new_m', 'new_m_conv_w': 'new_m', 'new_m_conv_b': 'new_m', 'new_m_w_down': 'new_m', 'new_m_ple_norm': 'new_m', 'new_m_w_ple_gate': 'new_m', 'new_m_w_ple': 'new_m', 'new_v_attn_pre_norm': 'new_v', 'new_v_attn_post_norm': 'new_v', 'new_v_w_in': 'new_v', 'new_v_b_gate': 'new_v', 'new_v_sinks': 'new_v', 'new_v_q_a_norm': 'new_v', 'new_v_w_uq': 'new_v', 'new_v_kv_a_norm': 'new_v', 'new_v_w_ukv': 'new_v', 'new_v_w_branch_a': 'new_v', 'new_v_w_branch_b': 'new_v', 'new_v_w_out': 'new_v', 'new_v_mlp_pre_norm': 'new_v', 'new_v_mlp_post_norm': 'new_v', 'new_v_w_up': 'new_v', 'new_v_conv_w': 'new_v', 'new_v_conv_b': 'new_v', 'new_v_w_down': 'new_v', 'new_v_ple_norm': 'new_v', 'new_v_w_ple_gate': 'new_v', 'new_v_w_ple': 'new_v'}


def _forward(args):
    return _fwd_reference(*[args[k] for k in FWD_PARAMS])


def _output_shape():
    out = _jax.eval_shape(lambda: _forward(_fwd_setup_inputs(0)))
    return out.shape, out.dtype

N_MICROBATCH = 1
ADAM_LR = 0.001
ADAM_B1 = 0.9
ADAM_B2 = 0.999
ADAM_EPS = 1e-08
ADAM_WD = 0.01
ADAM_STEP = 10
PER_EXAMPLE_BATCH_AXIS = {'x': 0, 'p': 1, 'positions': 0, 'loss_target': 0}
SHARED_INPUTS = []
_WEIGHT_DTYPES = {'attn_pre_norm': _jnp.float32, 'attn_post_norm': _jnp.float32, 'w_in': _jnp.float32, 'b_gate': _jnp.float32, 'sinks': _jnp.float32, 'q_a_norm': _jnp.float32, 'w_uq': _jnp.float32, 'kv_a_norm': _jnp.float32, 'w_ukv': _jnp.float32, 'w_branch_a': _jnp.float32, 'w_branch_b': _jnp.float32, 'w_out': _jnp.float32, 'mlp_pre_norm': _jnp.float32, 'mlp_post_norm': _jnp.float32, 'w_up': _jnp.float32, 'conv_w': _jnp.float32, 'conv_b': _jnp.float32, 'w_down': _jnp.float32, 'ple_norm': _jnp.float32, 'w_ple_gate': _jnp.float32, 'w_ple': _jnp.float32}
MOMENT_SCALE = {'attn_pre_norm': 9.284142e-01, 'attn_post_norm': 3.154242e+01, 'w_in': 4.846365e-01, 'b_gate': 1.909364e-01, 'sinks': 1.575357e-01, 'q_a_norm': 6.294179e-01, 'w_uq': 3.676909e-01, 'kv_a_norm': 1.583362e+00, 'w_ukv': 4.884750e-01, 'w_branch_a': 5.563652e-01, 'w_branch_b': 4.128864e-01, 'w_out': 6.734481e-01, 'mlp_pre_norm': 5.882919e-01, 'mlp_post_norm': 3.206139e+01, 'w_up': 2.550109e-01, 'conv_w': 2.750466e-01, 'conv_b': 7.100417e-01, 'w_down': 4.901963e-01, 'ple_norm': 9.458268e-01, 'w_ple_gate': 1.777517e-01, 'w_ple': 4.195233e-01}


def _to_microbatches(a, axis):
    t = _jnp.moveaxis(a, axis, 0)
    t = t.reshape((N_MICROBATCH, t.shape[0] // N_MICROBATCH) + t.shape[1:])
    return _jnp.moveaxis(t, 1, axis + 1)


def setup_inputs(seed: int = 0) -> dict:
    inp = _fwd_setup_inputs(seed)
    key = _jax.random.fold_in(_jax.random.key(seed), 7919)
    shape, _ = _output_shape()
    out = dict(inp)
    out["loss_target"] = _jax.random.normal(_jax.random.fold_in(key, 0), shape, _jnp.float32)
    for i, name in enumerate(TWIN_WEIGHTS):
        w = inp[name].astype(_jnp.float32)
        if MOMENT_SCALE is None:
            s = _jnp.sqrt(_jnp.mean(_jnp.square(w)) + 1e-30)
        else:
            s = MOMENT_SCALE[name]
        km, kv = _jax.random.split(_jax.random.fold_in(key, i + 1))
        out[name] = w
        out["m_" + name] = s * _jax.random.normal(km, w.shape, _jnp.float32)
        out["v_" + name] = (s * s) * _jax.random.uniform(kv, w.shape, _jnp.float32, 0.5, 1.5)
    if N_MICROBATCH > 1:
        for name, axis in PER_EXAMPLE_BATCH_AXIS.items():
            out[name] = _to_microbatches(out[name], axis)
    return {'x': out['x'], 'p': out['p'], 'positions': out['positions'], 'attn_pre_norm': out['attn_pre_norm'], 'attn_post_norm': out['attn_post_norm'], 'w_in': out['w_in'], 'b_gate': out['b_gate'], 'sinks': out['sinks'], 'q_a_norm': out['q_a_norm'], 'w_uq': out['w_uq'], 'kv_a_norm': out['kv_a_norm'], 'w_ukv': out['w_ukv'], 'w_branch_a': out['w_branch_a'], 'w_branch_b': out['w_branch_b'], 'w_out': out['w_out'], 'mlp_pre_norm': out['mlp_pre_norm'], 'mlp_post_norm': out['mlp_post_norm'], 'w_up': out['w_up'], 'conv_w': out['conv_w'], 'conv_b': out['conv_b'], 'w_down': out['w_down'], 'ple_norm': out['ple_norm'], 'w_ple_gate': out['w_ple_gate'], 'w_ple': out['w_ple'], 'loss_target': out['loss_target'], 'm_attn_pre_norm': out['m_attn_pre_norm'], 'm_attn_post_norm': out['m_attn_post_norm'], 'm_w_in': out['m_w_in'], 'm_b_gate': out['m_b_gate'], 'm_sinks': out['m_sinks'], 'm_q_a_norm': out['m_q_a_norm'], 'm_w_uq': out['m_w_uq'], 'm_kv_a_norm': out['m_kv_a_norm'], 'm_w_ukv': out['m_w_ukv'], 'm_w_branch_a': out['m_w_branch_a'], 'm_w_branch_b': out['m_w_branch_b'], 'm_w_out': out['m_w_out'], 'm_mlp_pre_norm': out['m_mlp_pre_norm'], 'm_mlp_post_norm': out['m_mlp_post_norm'], 'm_w_up': out['m_w_up'], 'm_conv_w': out['m_conv_w'], 'm_conv_b': out['m_conv_b'], 'm_w_down': out['m_w_down'], 'm_ple_norm': out['m_ple_norm'], 'm_w_ple_gate': out['m_w_ple_gate'], 'm_w_ple': out['m_w_ple'], 'v_attn_pre_norm': out['v_attn_pre_norm'], 'v_attn_post_norm': out['v_attn_post_norm'], 'v_w_in': out['v_w_in'], 'v_b_gate': out['v_b_gate'], 'v_sinks': out['v_sinks'], 'v_q_a_norm': out['v_q_a_norm'], 'v_w_uq': out['v_w_uq'], 'v_kv_a_norm': out['v_kv_a_norm'], 'v_w_ukv': out['v_w_ukv'], 'v_w_branch_a': out['v_w_branch_a'], 'v_w_branch_b': out['v_w_branch_b'], 'v_w_out': out['v_w_out'], 'v_mlp_pre_norm': out['v_mlp_pre_norm'], 'v_mlp_post_norm': out['v_mlp_post_norm'], 'v_w_up': out['v_w_up'], 'v_conv_w': out['v_conv_w'], 'v_conv_b': out['v_conv_b'], 'v_w_down': out['v_w_down'], 'v_ple_norm': out['v_ple_norm'], 'v_w_ple_gate': out['v_w_ple_gate'], 'v_w_ple': out['v_w_ple']}


def _loss(weights, diff, rest, loss_target):
    with _jax.named_scope("forward"):
        args = {**rest, TWIN_DIFF_INPUT: diff, **{k: w.astype(_WEIGHT_DTYPES[k]) for k, w in weights.items()}}
        y = _forward(args)
    with _jax.named_scope("loss_head"):
        err = _jnp.square(y.astype(_jnp.float32) - loss_target)
        return 0.5 * _jnp.sum(_jnp.mean(err, axis=-1)) if err.ndim else 0.5 * err


def _adamw(w, g, m, v):
    m = ADAM_B1 * m + (1.0 - ADAM_B1) * g
    v = ADAM_B2 * v + (1.0 - ADAM_B2) * _jnp.square(g)
    m_hat = m / (1.0 - ADAM_B1 ** ADAM_STEP)
    v_hat = v / (1.0 - ADAM_B2 ** ADAM_STEP)
    delta = -ADAM_LR * (m_hat / (_jnp.sqrt(v_hat) + ADAM_EPS) + ADAM_WD * w)
    return delta, m, v


def reference(x, p, positions, attn_pre_norm, attn_post_norm, w_in, b_gate, sinks, q_a_norm, w_uq, kv_a_norm, w_ukv, w_branch_a, w_branch_b, w_out, mlp_pre_norm, mlp_post_norm, w_up, conv_w, conv_b, w_down, ple_norm, w_ple_gate, w_ple, loss_target, m_attn_pre_norm, m_attn_post_norm, m_w_in, m_b_gate, m_sinks, m_q_a_norm, m_w_uq, m_kv_a_norm, m_w_ukv, m_w_branch_a, m_w_branch_b, m_w_out, m_mlp_pre_norm, m_mlp_post_norm, m_w_up, m_conv_w, m_conv_b, m_w_down, m_ple_norm, m_w_ple_gate, m_w_ple, v_attn_pre_norm, v_attn_post_norm, v_w_in, v_b_gate, v_sinks, v_q_a_norm, v_w_uq, v_kv_a_norm, v_w_ukv, v_w_branch_a, v_w_branch_b, v_w_out, v_mlp_pre_norm, v_mlp_post_norm, v_w_up, v_conv_w, v_conv_b, v_w_down, v_ple_norm, v_w_ple_gate, v_w_ple):
    given = dict(x=x, p=p, positions=positions, attn_pre_norm=attn_pre_norm, attn_post_norm=attn_post_norm, w_in=w_in, b_gate=b_gate, sinks=sinks, q_a_norm=q_a_norm, w_uq=w_uq, kv_a_norm=kv_a_norm, w_ukv=w_ukv, w_branch_a=w_branch_a, w_branch_b=w_branch_b, w_out=w_out, mlp_pre_norm=mlp_pre_norm, mlp_post_norm=mlp_post_norm, w_up=w_up, conv_w=conv_w, conv_b=conv_b, w_down=w_down, ple_norm=ple_norm, w_ple_gate=w_ple_gate, w_ple=w_ple, loss_target=loss_target, m_attn_pre_norm=m_attn_pre_norm, m_attn_post_norm=m_attn_post_norm, m_w_in=m_w_in, m_b_gate=m_b_gate, m_sinks=m_sinks, m_q_a_norm=m_q_a_norm, m_w_uq=m_w_uq, m_kv_a_norm=m_kv_a_norm, m_w_ukv=m_w_ukv, m_w_branch_a=m_w_branch_a, m_w_branch_b=m_w_branch_b, m_w_out=m_w_out, m_mlp_pre_norm=m_mlp_pre_norm, m_mlp_post_norm=m_mlp_post_norm, m_w_up=m_w_up, m_conv_w=m_conv_w, m_conv_b=m_conv_b, m_w_down=m_w_down, m_ple_norm=m_ple_norm, m_w_ple_gate=m_w_ple_gate, m_w_ple=m_w_ple, v_attn_pre_norm=v_attn_pre_norm, v_attn_post_norm=v_attn_post_norm, v_w_in=v_w_in, v_b_gate=v_b_gate, v_sinks=v_sinks, v_q_a_norm=v_q_a_norm, v_w_uq=v_w_uq, v_kv_a_norm=v_kv_a_norm, v_w_ukv=v_w_ukv, v_w_branch_a=v_w_branch_a, v_w_branch_b=v_w_branch_b, v_w_out=v_w_out, v_mlp_pre_norm=v_mlp_pre_norm, v_mlp_post_norm=v_mlp_post_norm, v_w_up=v_w_up, v_conv_w=v_conv_w, v_conv_b=v_conv_b, v_w_down=v_w_down, v_ple_norm=v_ple_norm, v_w_ple_gate=v_w_ple_gate, v_w_ple=v_w_ple)
    weights = {n: given[n] for n in TWIN_WEIGHTS}
    shared = {n: given[n] for n in SHARED_INPUTS}
    per_example = {n: given[n] for n in ['x', 'p', 'positions']}
    grad_fn = _jax.value_and_grad(_loss, argnums=(0, 1))

    def one_microbatch(ex, loss_target):
        ex = dict(ex)
        diff = ex.pop(TWIN_DIFF_INPUT)
        return grad_fn(weights, diff, {**shared, **ex}, loss_target)

    if N_MICROBATCH == 1:
        loss, (grad_w, grad_x) = one_microbatch(per_example, given["loss_target"])
    else:
        def body(carry, xs):
            loss_sum, grad_sum = carry
            l_k, (gw_k, gx_k) = one_microbatch(xs[0], xs[1])
            with _jax.named_scope("update"):
                return (loss_sum + l_k, _jax.tree.map(_jnp.add, grad_sum, gw_k)), gx_k

        init = (_jnp.zeros((), _jnp.float32), _jax.tree.map(_jnp.zeros_like, weights))
        (loss, grad_w), grad_x = _jax.lax.scan(body, init, (per_example, given["loss_target"]))
    with _jax.named_scope("update"):
        delta_w, new_m, new_v = {}, {}, {}
        for n in TWIN_WEIGHTS:
            delta_w[n], new_m[n], new_v[n] = _adamw(weights[n], grad_w[n], given["m_" + n], given["v_" + n])
    return (loss, grad_x, *[grad_w[n] for n in TWIN_WEIGHTS], *[delta_w[n] for n in TWIN_WEIGHTS],
            *[new_m[n] for n in TWIN_WEIGHTS], *[new_v[n] for n in TWIN_WEIGHTS])
```

```python
import functools
import math

import numpy as np
import jax
import jax.numpy as jnp
from jax import lax
from jax.experimental import pallas as pl
from jax.experimental.pallas import tpu as pltpu

F32 = jnp.float32
BF16 = jnp.bfloat16

D_MODEL = 1024
D_FF = 2816
PLE_DIM = 256
ROPE_THETA = 10000.0
RMS_EPS = 1e-6
SWA_WINDOW = 128
HEADS = 8
A_KV_HEADS = 2
A_HEAD_DIM = 64
Q_LORA = 256
KV_LORA = 128
NOPE_DIM = 64
ROPE_DIM = 32
V_DIM = 64
LANES = 128
ZW = 4096
NEG = -1e30

ADAM_LR = 0.001
ADAM_B1 = 0.9
ADAM_B2 = 0.999
ADAM_EPS = 1e-08
ADAM_WD = 0.01
ADAM_STEP = 10

VMEM_LIMIT = 60 * 1024 * 1024
MESH_AXES = ("x", "y", "c")
MESH = pl.DeviceIdType.MESH

Z_QA, Z_KA, Z_VA, Z_CQ, Z_CKV, Z_KR, Z_GATE = 0, 1024, 1280, 1536, 1792, 1920, 2048


def _dot(a, b):
    return jnp.dot(a, b, preferred_element_type=F32)


def _dot_nt(a, b):
    return lax.dot_general(a, b, (((1,), (1,)), ((), ())), preferred_element_type=F32)


def _dot_tn(a, b):
    return lax.dot_general(a, b, (((0,), (0,)), ((), ())), preferred_element_type=F32)


def _rms_stats(x):
    r = lax.rsqrt(jnp.mean(x * x, axis=-1, keepdims=True) + RMS_EPS)
    return x * r, r


def _rms_bwd(dy, xn, r, g):
    dxn = dy * g
    dx = r * (dxn - xn * jnp.mean(dxn * xn, axis=-1, keepdims=True))
    dg = jnp.sum(dy * xn, axis=0, keepdims=True)
    return dx, dg


def _tile_lanes(t, n):
    return t if n == 1 else jnp.concatenate([t] * n, axis=1)


def _rope(x, c, s1, s2, half):
    w = x.shape[1]
    n = w // LANES
    return (x * _tile_lanes(c, n) + pltpu.roll(x, w - half, 1) * _tile_lanes(s1, n)
            + pltpu.roll(x, half, 1) * _tile_lanes(s2, n))


def _rope_t(dy, c, s1, s2, half):
    w = dy.shape[1]
    n = w // LANES
    return (dy * _tile_lanes(c, n) + pltpu.roll(dy * _tile_lanes(s1, n), half, 1)
            + pltpu.roll(dy * _tile_lanes(s2, n), w - half, 1))


def _sigmoid(x):
    return 1.0 / (1.0 + jnp.exp(-x))


_GELU_C = math.sqrt(2.0 / math.pi)


def _gelu_and_grad(x):
    x2 = x * x
    th = jnp.tanh(_GELU_C * (x + 0.044715 * x * x2))
    gel = 0.5 * x * (1.0 + th)
    dgel = 0.5 * (1.0 + th) + 0.5 * x * (1.0 - th * th) * (_GELU_C * (1.0 + 3.0 * 0.044715 * x2))
    return gel, dgel


def _conv_taps(up, h6, h7):
    rows = lax.broadcasted_iota(jnp.int32, up.shape, 0)
    r1 = pltpu.roll(up, 1, 0)
    r2 = pltpu.roll(up, 2, 0)
    xm1 = jnp.where(rows == 0, h7, r1)
    xm2 = jnp.where(rows == 0, h6, jnp.where(rows == 1, h7, r2))
    return xm1, xm2


def _conv_taps_next(du, n0, n1):
    tm = du.shape[0]
    rows = lax.broadcasted_iota(jnp.int32, du.shape, 0)
    r1 = pltpu.roll(du, tm - 1, 0)
    r2 = pltpu.roll(du, tm - 2, 0)
    xp1 = jnp.where(rows == tm - 1, n0, r1)
    xp2 = jnp.where(rows == tm - 2, n0, jnp.where(rows == tm - 1, n1, r2))
    return xp1, xp2


def _row(tm, n):
    return pl.BlockSpec((tm, n), lambda i: (i, 0))


def _full(shape):
    nd = len(shape)
    return pl.BlockSpec(tuple(shape), lambda i: (0,) * nd)


def _heads(tm, h):
    return pl.BlockSpec((h, tm, LANES), lambda i: (0, i, 0))


def _rows_call(name, body, t_rows, tm, ins, outs, scratch=()):
    return pl.pallas_call(
        body, name=name, grid=(t_rows // tm,),
        in_specs=[s for _, s in ins],
        out_specs=[s for _, s in outs],
        out_shape=[s for s, _ in outs],
        scratch_shapes=list(scratch),
        compiler_params=pltpu.CompilerParams(dimension_semantics=("arbitrary",), vmem_limit_bytes=VMEM_LIMIT),
    )(*[a for a, _ in ins])


def _sds(shape, dtype):
    return jax.ShapeDtypeStruct(tuple(shape), dtype)


def _rope_consts():
    c = np.zeros((16, LANES), np.float32)
    lane = np.arange(LANES)
    inv_a = (ROPE_THETA ** (-(np.arange(0, A_HEAD_DIM, 2, dtype=np.float32) / A_HEAD_DIM))).astype(np.float32)
    in_a = lane < A_HEAD_DIM
    c[0, in_a] = inv_a[lane[in_a] % (A_HEAD_DIM // 2)]
    c[1, in_a] = 1.0
    c[2, lane < A_HEAD_DIM // 2] = -1.0
    c[3, (lane >= A_HEAD_DIM // 2) & in_a] = 1.0
    inv_b = (ROPE_THETA ** (-(np.arange(0, ROPE_DIM, 2, dtype=np.float32) / ROPE_DIM))).astype(np.float32)
    pe = (lane >= NOPE_DIM) & (lane < NOPE_DIM + ROPE_DIM)
    c[5, pe] = inv_b[(lane[pe] - NOPE_DIM) % (ROPE_DIM // 2)]
    c[6, pe] = 1.0
    c[7, (lane >= NOPE_DIM) & (lane < NOPE_DIM + ROPE_DIM // 2)] = -1.0
    c[8, (lane >= NOPE_DIM + ROPE_DIM // 2) & (lane < NOPE_DIM + ROPE_DIM)] = 1.0
    c[9, lane < NOPE_DIM] = 1.0
    c[10, pe] = 1.0
    return jnp.asarray(c)


def _rope_tables(pos_f, consts, tm):
    t_rows = pos_f.shape[0]

    def body(pos_ref, c_ref, ca, sa1, sa2, cb, sb1, sb2):
        pos = pos_ref[...]
        ang = pos * c_ref[0:1, :]
        cs, sn = jnp.cos(ang), jnp.sin(ang)
        ca[...] = cs * c_ref[1:2, :]
        sa1[...] = sn * c_ref[2:3, :]
        sa2[...] = sn * c_ref[3:4, :]
        ang = pos * c_ref[5:6, :]
        cs, sn = jnp.cos(ang), jnp.sin(ang)
        cb[...] = cs * c_ref[6:7, :] + c_ref[9:10, :]
        sb1[...] = sn * c_ref[7:8, :]
        sb2[...] = sn * c_ref[8:9, :]

    tab = (_sds((t_rows, LANES), F32), _row(tm, LANES))
    return _rows_call("rope_tables", body, t_rows, tm,
                      [(pos_f, _row(tm, 1)), (consts, _full(consts.shape))], [tab] * 6)


def _fwd_in(x, g1, win, bg, gq, gkv, wuq, wk, wv, tabs, tm):
    t_rows = x.shape[0]

    def body(x_ref, g1_ref, win_ref, bg_ref, gq_ref, gkv_ref, wuq_ref, wk_ref, wv_ref,
             ca, sa1, sa2, cb, sb1, sb2,
             h1_ref, qs_ref, ks_ref, vs_ref, cq_ref, cqn_ref, ckv_ref, ckvn_ref, qm_ref, km_ref, vm_ref, gate_ref):
        xn, _ = _rms_stats(x_ref[...])
        hb = (xn * g1_ref[...]).astype(BF16)
        h1_ref[...] = hb
        ta = (ca[...], sa1[...], sa2[...])
        tb = (cb[...], sb1[...], sb2[...])
        qs_ref[...] = _rope(_dot(hb, win_ref[:, Z_QA:Z_KA]), *ta, A_HEAD_DIM // 2).astype(BF16)
        ks_ref[...] = _rope(_dot(hb, win_ref[:, Z_KA:Z_VA]), *ta, A_HEAD_DIM // 2).astype(BF16)
        vs_ref[...] = _dot(hb, win_ref[:, Z_VA:Z_CQ]).astype(BF16)
        cq = _dot(hb, win_ref[:, Z_CQ:Z_CKV])
        cq_ref[...] = cq
        cqn, _ = _rms_stats(cq)
        cqb = (cqn * gq_ref[...]).astype(BF16)
        cqn_ref[...] = cqb
        qm_ref[...] = _rope(_dot(cqb, wuq_ref[...]), *tb, ROPE_DIM // 2).astype(BF16)
        ckv = _dot(hb, win_ref[:, Z_CKV:Z_KR])
        ckv_ref[...] = ckv
        ckvn, _ = _rms_stats(ckv)
        ckvb = (ckvn * gkv_ref[...]).astype(BF16)
        ckvn_ref[...] = ckvb
        kpe = _rope(_dot(hb, win_ref[:, Z_KR:Z_GATE]), *tb, ROPE_DIM // 2)
        km_ref[...] = (_dot(ckvb, wk_ref[...]) + _tile_lanes(kpe, HEADS)).astype(BF16)
        vm_ref[...] = _dot(ckvb, wv_ref[...]).astype(BF16)
        gate_ref[...] = _sigmoid(_dot(hb, win_ref[:, Z_GATE:ZW]) + bg_ref[...])

    def o(n, dt):
        return (_sds((t_rows, n), dt), _row(tm, n))

    ins = [(x, _row(tm, D_MODEL)), (g1, _full(g1.shape)), (win, _full(win.shape)), (bg, _full(bg.shape)),
           (gq, _full(gq.shape)), (gkv, _full(gkv.shape)), (wuq, _full(wuq.shape)), (wk, _full(wk.shape)),
           (wv, _full(wv.shape))] + [(t, _row(tm, LANES)) for t in tabs]
    outs = [o(1024, BF16), o(1024, BF16), o(256, BF16), o(256, BF16), o(256, F32), o(256, BF16), o(128, F32),
            o(128, BF16), o(1024, BF16), o(1024, BF16), o(1024, BF16), o(2048, F32)]
    return _rows_call("fwd_in", body, t_rows, tm, ins, outs)


def _attn_tile(t_rows):
    return min(512, t_rows)


def _attn_fwd(name, q, k, v, sink, *, heads, group, scale, window, has_sink):
    t_rows = q.shape[0]
    t = _attn_tile(t_rows)
    nq = t_rows // t
    banded = window is not None
    nj = 2 if banded else nq

    def kvidx(i, j):
        return jnp.maximum(i - 1 + j, 0) if banded else jnp.minimum(j, i)

    def body(q_ref, k_ref, v_ref, sink_ref, o_ref, lse_ref, m_s, l_s, acc_s):
        i = pl.program_id(1)
        j = pl.program_id(2)

        @pl.when(j == 0)
        def _():
            if has_sink:
                m_s[...] = jnp.zeros((t, 1), F32) + sink_ref[0]
                l_s[...] = jnp.ones((t, 1), F32)
            else:
                m_s[...] = jnp.full((t, 1), NEG, F32)
                l_s[...] = jnp.zeros((t, 1), F32)
            acc_s[...] = jnp.zeros((t, LANES), F32)

        active = (i - 1 + j >= 0) if banded else (j <= i)

        @pl.when(active)
        def _():
            s = _dot_nt(q_ref[...], k_ref[...]) * scale
            qpos = i * t + lax.broadcasted_iota(jnp.int32, (t, t), 0)
            kpos = kvidx(i, j) * t + lax.broadcasted_iota(jnp.int32, (t, t), 1)
            valid = kpos <= qpos
            if banded:
                valid = valid & (qpos - kpos < window)
            s = jnp.where(valid, s, NEG)
            m_prev = m_s[...]
            m_new = jnp.maximum(m_prev, jnp.max(s, axis=1, keepdims=True))
            p = jnp.exp(s - m_new)
            alpha = jnp.exp(m_prev - m_new)
            l_s[...] = alpha * l_s[...] + jnp.sum(p, axis=1, keepdims=True)
            acc_s[...] = alpha * acc_s[...] + _dot(p.astype(BF16), v_ref[...])
            m_s[...] = m_new

        @pl.when(j == nj - 1)
        def _():
            l = l_s[...]
            o_ref[...] = (acc_s[...] / l).astype(o_ref.dtype)
            lse_ref[0] = m_s[...] + jnp.log(l)

    return pl.pallas_call(
        body, name=name, grid=(heads, nq, nj),
        in_specs=[pl.BlockSpec((t, LANES), lambda h, i, j: (i, h)),
                  pl.BlockSpec((t, LANES), lambda h, i, j: (kvidx(i, j), h // group)),
                  pl.BlockSpec((t, LANES), lambda h, i, j: (kvidx(i, j), h // group)),
                  pl.BlockSpec((1, 1, 1), lambda h, i, j: (h, 0, 0))],
        out_specs=[pl.BlockSpec((t, LANES), lambda h, i, j: (i, h)),
                   pl.BlockSpec((1, t, 1), lambda h, i, j: (h, i, 0))],
        out_shape=[_sds((t_rows, heads * LANES), BF16), _sds((heads, t_rows, 1), F32)],
        scratch_shapes=[pltpu.VMEM((t, 1), F32), pltpu.VMEM((t, 1), F32), pltpu.VMEM((t, LANES), F32)],
        compiler_params=pltpu.CompilerParams(dimension_semantics=("arbitrary",) * 3, vmem_limit_bytes=VMEM_LIMIT),
    )(q, k, v, sink)


def _attn_delta(name, o, do, lse, sink, *, heads, has_sink):
    t_rows = o.shape[0]
    t = _attn_tile(t_rows)
    nq = t_rows // t

    def body(o_ref, do_ref, lse_ref, sink_ref, dl_ref, ds_ref):
        i = pl.program_id(1)
        d = jnp.sum(o_ref[...].astype(F32) * do_ref[...].astype(F32), axis=1, keepdims=True)
        dl_ref[0] = d

        @pl.when(i == 0)
        def _():
            ds_ref[...] = jnp.zeros((1, 1, 1), F32)

        if has_sink:
            ds_ref[0] += -jnp.sum(jnp.exp(sink_ref[0] - lse_ref[0]) * d, axis=0, keepdims=True)

    return pl.pallas_call(
        body, name=name, grid=(heads, nq),
        in_specs=[pl.BlockSpec((t, LANES), lambda h, i: (i, h)),
                  pl.BlockSpec((t, LANES), lambda h, i: (i, h)),
                  pl.BlockSpec((1, t, 1), lambda h, i: (h, i, 0)),
                  pl.BlockSpec((1, 1, 1), lambda h, i: (h, 0, 0))],
        out_specs=[pl.BlockSpec((1, t, 1), lambda h, i: (h, i, 0)),
                   pl.BlockSpec((1, 1, 1), lambda h, i: (h, 0, 0))],
        out_shape=[_sds((heads, t_rows, 1), F32), _sds((heads, 1, 1), F32)],
        compiler_params=pltpu.CompilerParams(dimension_semantics=("arbitrary",) * 2, vmem_limit_bytes=VMEM_LIMIT),
    )(o, do, lse, sink)


def _attn_bwd(name, q, k, v, do, lse, delta, *, heads, group, scale, window):
    t_rows = q.shape[0]
    t = _attn_tile(t_rows)
    nq = t_rows // t
    kv_heads = heads // group
    banded = window is not None
    ni = 2 if banded else nq

    def qidx(j, ii):
        return jnp.minimum(j + ii, nq - 1) if banded else jnp.maximum(ii, j)

    def body(q_ref, k_ref, v_ref, do_ref, lse_ref, dl_ref, dq_ref, dk_ref, dv_ref):
        j = pl.program_id(1)
        g = pl.program_id(2)
        ii = pl.program_id(3)

        @pl.when((j == 0) & (g == 0) & (ii == 0))
        def _():
            dq_ref[...] = jnp.zeros(dq_ref.shape, F32)

        @pl.when((g == 0) & (ii == 0))
        def _():
            dk_ref[...] = jnp.zeros((t, LANES), F32)
            dv_ref[...] = jnp.zeros((t, LANES), F32)

        active = (j + ii <= nq - 1) if banded else (ii >= j)

        @pl.when(active)
        def _():
            qi = qidx(j, ii)
            qv = q_ref[...]
            kv = k_ref[...]
            dov = do_ref[...]
            s = _dot_nt(qv, kv) * scale
            qpos = qi * t + lax.broadcasted_iota(jnp.int32, (t, t), 0)
            kpos = j * t + lax.broadcasted_iota(jnp.int32, (t, t), 1)
            valid = kpos <= qpos
            if banded:
                valid = valid & (qpos - kpos < window)
            s = jnp.where(valid, s, NEG)
            p = jnp.exp(s - lse_ref[0])
            dv_ref[...] += _dot_tn(p.astype(BF16), dov)
            dp = _dot_nt(dov, v_ref[...])
            ds = (p * (dp - dl_ref[0]) * scale).astype(BF16)
            dk_ref[...] += _dot_tn(ds, qv)
            r0 = pl.multiple_of(qi * t, t)
            dq_ref[g, pl.ds(r0, t), :] += _dot(ds, kv)

    def qmap(kvh, j, g, ii):
        return (qidx(j, ii), kvh * group + g)

    def rowmap(kvh, j, g, ii):
        return (kvh * group + g, qidx(j, ii), 0)

    def kvmap(kvh, j, g, ii):
        return (j, kvh)

    return pl.pallas_call(
        body, name=name, grid=(kv_heads, nq, group, ni),
        in_specs=[pl.BlockSpec((t, LANES), qmap), pl.BlockSpec((t, LANES), kvmap), pl.BlockSpec((t, LANES), kvmap),
                  pl.BlockSpec((t, LANES), qmap), pl.BlockSpec((1, t, 1), rowmap), pl.BlockSpec((1, t, 1), rowmap)],
        out_specs=[pl.BlockSpec((group, t_rows, LANES), lambda kvh, j, g, ii: (kvh, 0, 0)),
                   pl.BlockSpec((t, LANES), kvmap), pl.BlockSpec((t, LANES), kvmap)],
        out_shape=[_sds((heads, t_rows, LANES), F32), _sds((t_rows, kv_heads * LANES), F32),
                   _sds((t_rows, kv_heads * LANES), F32)],
        compiler_params=pltpu.CompilerParams(dimension_semantics=("arbitrary",) * 4, vmem_limit_bytes=VMEM_LIMIT),
    )(q, k, v, do, lse, delta)


def _fwd_mix(x, ya, yb, gate, wba, wbb, wout, g2, g3, tm):
    t_rows = x.shape[0]

    def body(x_ref, ya_ref, yb_ref, gate_ref, wba_ref, wbb_ref, wout_ref, g2_ref, g3_ref,
             pa_ref, pb_ref, mixed_ref, o_ref, x1_ref, h2_ref):
        pa = _dot(ya_ref[...], wba_ref[...])
        pb = _dot(yb_ref[...], wbb_ref[...])
        pa_ref[...] = pa
        pb_ref[...] = pb
        mixed = (gate_ref[:, 0:D_MODEL] * pa + gate_ref[:, D_MODEL:2 * D_MODEL] * pb).astype(BF16)
        mixed_ref[...] = mixed
        o = _dot(mixed, wout_ref[...])
        o_ref[...] = o
        on, _ = _rms_stats(o)
        x1 = x_ref[...] + on * g2_ref[...]
        x1_ref[...] = x1
        x1n, _ = _rms_stats(x1)
        h2_ref[...] = (x1n * g3_ref[...]).astype(BF16)

    def o_(dt):
        return (_sds((t_rows, D_MODEL), dt), _row(tm, D_MODEL))

    ins = [(x, _row(tm, D_MODEL)), (ya, _row(tm, 1024)), (yb, _row(tm, 1024)), (gate, _row(tm, 2048)),
           (wba, _full(wba.shape)), (wbb, _full(wbb.shape)), (wout, _full(wout.shape)),
           (g2, _full(g2.shape)), (g3, _full(g3.shape))]
    return _rows_call("fwd_mix", body, t_rows, tm, ins, [o_(F32), o_(F32), o_(BF16), o_(F32), o_(F32), o_(BF16)])


CONV_CHUNK = 1408


def _fwd_up(h2, wup, convw8, convb, tm):
    t_rows = h2.shape[0]
    cdim = 2 * D_FF

    def body(h2_ref, wup_ref, cw_ref, cb_ref, up_ref, a_ref, carry):
        i = pl.program_id(0)

        @pl.when(i == 0)
        def _():
            carry[...] = jnp.zeros(carry.shape, F32)

        hb = h2_ref[...]

        def conv(c0):
            sl = slice(c0, c0 + CONV_CHUNK)
            up = _dot(hb, wup_ref[:, sl])
            up_ref[:, sl] = up
            xm1, xm2 = _conv_taps(up, carry[6:7, sl], carry[7:8, sl])
            u = cw_ref[0:1, sl] * xm2 + cw_ref[1:2, sl] * xm1 + cw_ref[2:3, sl] * up + cb_ref[:, sl]
            carry[:, sl] = up[tm - 8:tm, :]
            return u

        for c0 in range(0, D_FF, CONV_CHUNK):
            ug = conv(c0)
            uv = conv(D_FF + c0)
            gel, _ = _gelu_and_grad(ug)
            a_ref[:, c0:c0 + CONV_CHUNK] = (gel * uv).astype(BF16)

    ins = [(h2, _row(tm, D_MODEL)), (wup, _full(wup.shape)), (convw8, _full(convw8.shape)), (convb, _full(convb.shape))]
    outs = [(_sds((t_rows, cdim), F32), _row(tm, cdim)), (_sds((t_rows, D_FF), BF16), _row(tm, D_FF))]
    return _rows_call("fwd_up", body, t_rows, tm, ins, outs, scratch=[pltpu.VMEM((8, cdim), F32)])


def _fwd_out(a, wdown, x1, g4, p, wple, g5, wpg, tgt, tm):
    t_rows = a.shape[0]

    def body(a_ref, wdown_ref, x1_ref, g4_ref, p_ref, wple_ref, g5_ref, wpg_ref, tgt_ref,
             ff_ref, x2_ref, e_ref, n5_ref, sg_ref, dx3_ref, loss_ref):
        i = pl.program_id(0)
        ff = _dot(a_ref[...], wdown_ref[...])
        ff_ref[...] = ff
        ffn, _ = _rms_stats(ff)
        x2 = x1_ref[...] + ffn * g4_ref[...]
        x2_ref[...] = x2
        e = _dot(p_ref[...].astype(BF16), wple_ref[...])
        e_ref[...] = e
        x2n, _ = _rms_stats(x2)
        n5 = (x2n * g5_ref[...]).astype(BF16)
        n5_ref[...] = n5
        sg = _sigmoid(_dot(n5, wpg_ref[...]))
        sg_ref[...] = sg
        d = x2 + sg * e - tgt_ref[...]
        dx3_ref[...] = d * (1.0 / D_MODEL)

        @pl.when(i == 0)
        def _():
            loss_ref[...] = jnp.zeros((1, 1), F32)

        loss_ref[...] += 0.5 * jnp.sum(jnp.sum(d * d, axis=1, keepdims=True), axis=0, keepdims=True) * (1.0 / D_MODEL)

    def o_(dt):
        return (_sds((t_rows, D_MODEL), dt), _row(tm, D_MODEL))

    ins = [(a, _row(tm, D_FF)), (wdown, _full(wdown.shape)), (x1, _row(tm, D_MODEL)), (g4, _full(g4.shape)),
           (p, _row(tm, PLE_DIM)), (wple, _full(wple.shape)), (g5, _full(g5.shape)), (wpg, _full(wpg.shape)),
           (tgt, _row(tm, D_MODEL))]
    outs = [o_(F32), o_(F32), o_(F32), o_(BF16), o_(F32), o_(F32), (_sds((1, 1), F32), _full((1, 1)))]
    return _rows_call("fwd_out", body, t_rows, tm, ins, outs)


def _bwd_out(dx3, e, sg, x2, ff, g5, g4, wpg, wdown, up, convw8, convb, tm):
    t_rows = dx3.shape[0]
    cdim = 2 * D_FF
    hb = tm // 8

    def body(dx3_ref, e_ref, sg_ref, x2_ref, ff_ref, g5_ref, g4_ref, wpg_ref, wdown_ref, up_ref, halo_ref, cw_ref,
             cb_ref, dpre_ref, de_ref, dx2_ref, dff_ref, du_ref, dg5_ref, dg4_ref, dcb_ref, dcw_ref):
        i = pl.program_id(0)

        @pl.when(i == 0)
        def _():
            dg5_ref[...] = jnp.zeros(dg5_ref.shape, F32)
            dg4_ref[...] = jnp.zeros(dg4_ref.shape, F32)
            dcb_ref[...] = jnp.zeros(dcb_ref.shape, F32)
            dcw_ref[...] = jnp.zeros(dcw_ref.shape, F32)

        dx3 = dx3_ref[...]
        sg = sg_ref[...]
        dpre = (dx3 * e_ref[...] * sg * (1.0 - sg)).astype(BF16)
        dpre_ref[...] = dpre
        de_ref[...] = (dx3 * sg).astype(BF16)
        dn5 = _dot_nt(dpre, wpg_ref[...])
        x2n, r5 = _rms_stats(x2_ref[...])
        d2, dg5 = _rms_bwd(dn5, x2n, r5, g5_ref[...])
        dx2 = dx3 + d2
        dx2_ref[...] = dx2
        dg5_ref[...] += dg5
        ffn, r4 = _rms_stats(ff_ref[...])
        dff, dg4 = _rms_bwd(dx2, ffn, r4, g4_ref[...])
        dg4_ref[...] += dg4
        dffb = dff.astype(BF16)
        dff_ref[...] = dffb
        keep = jnp.where(i > 0, 1.0, 0.0)

        def conv(c0):
            sl = slice(c0, c0 + CONV_CHUNK)
            up = up_ref[:, sl]
            xm1, xm2 = _conv_taps(up, halo_ref[6:7, sl] * keep, halo_ref[7:8, sl] * keep)
            u = cw_ref[0:1, sl] * xm2 + cw_ref[1:2, sl] * xm1 + cw_ref[2:3, sl] * up + cb_ref[:, sl]
            return u, up, xm1, xm2

        def grads(c0, du, up, xm1, xm2):
            sl = slice(c0, c0 + CONV_CHUNK)
            du_ref[:, sl] = du
            dcb_ref[:, sl] += jnp.sum(du, axis=0, keepdims=True)
            dcw_ref[0:1, sl] += jnp.sum(du * xm2, axis=0, keepdims=True)
            dcw_ref[1:2, sl] += jnp.sum(du * xm1, axis=0, keepdims=True)
            dcw_ref[2:3, sl] += jnp.sum(du * up, axis=0, keepdims=True)

        for c0 in range(0, D_FF, CONV_CHUNK):
            da = _dot_nt(dffb, wdown_ref[c0:c0 + CONV_CHUNK, :])
            ug, *rg = conv(c0)
            uv, *rv = conv(D_FF + c0)
            gel, dgel = _gelu_and_grad(ug)
            grads(c0, da * uv * dgel, *rg)
            grads(D_FF + c0, da * gel, *rv)

    def o_(n, dt):
        return (_sds((t_rows, n), dt), _row(tm, n))

    def acc(r, n):
        return (_sds((r, n), F32), _full((r, n)))

    halo = pl.BlockSpec((8, cdim), lambda i: (jnp.maximum(i * hb - 1, 0), 0))
    ins = [(dx3, _row(tm, D_MODEL)), (e, _row(tm, D_MODEL)), (sg, _row(tm, D_MODEL)), (x2, _row(tm, D_MODEL)),
           (ff, _row(tm, D_MODEL)), (g5, _full(g5.shape)), (g4, _full(g4.shape)), (wpg, _full(wpg.shape)),
           (wdown, _full(wdown.shape)), (up, _row(tm, cdim)), (up, halo), (convw8, _full(convw8.shape)),
           (convb, _full(convb.shape))]
    outs = [o_(D_MODEL, BF16), o_(D_MODEL, BF16), o_(D_MODEL, F32), o_(D_MODEL, BF16), o_(cdim, F32),
            acc(1, D_MODEL), acc(1, D_MODEL), acc(1, cdim), acc(8, cdim)]
    return _rows_call("bwd_out", body, t_rows, tm, ins, outs)


def _bwd_mid(du, convw8, wup, dx2, x1, g3, o, g2, wout, gate, pa, pb, wba, wbb, tm):
    t_rows = du.shape[0]
    cdim = 2 * D_FF
    hb = tm // 8
    last_blk = t_rows // 8 - 1
    n_tiles = t_rows // tm

    def body(du_ref, halo_ref, cw_ref, wup_ref, dx2_ref, x1_ref, g3_ref, o_ref, g2_ref, wout_ref, gate_ref, pa_ref,
             pb_ref, wba_ref, wbb_ref,
             dup_ref, dx1_ref, do_ref, dpa_ref, dpb_ref, dgt_ref, dya_ref, dyb_ref, dg3_ref, dg2_ref, dbg_ref):
        i = pl.program_id(0)

        @pl.when(i == 0)
        def _():
            dg3_ref[...] = jnp.zeros(dg3_ref.shape, F32)
            dg2_ref[...] = jnp.zeros(dg2_ref.shape, F32)
            dbg_ref[...] = jnp.zeros(dbg_ref.shape, F32)

        keep = jnp.where(i < n_tiles - 1, 1.0, 0.0)
        dh2 = jnp.zeros((tm, D_MODEL), F32)
        for c0 in range(0, cdim, CONV_CHUNK):
            sl = slice(c0, c0 + CONV_CHUNK)
            du = du_ref[:, sl]
            xp1, xp2 = _conv_taps_next(du, halo_ref[0:1, sl] * keep, halo_ref[1:2, sl] * keep)
            dup = (cw_ref[2:3, sl] * du + cw_ref[1:2, sl] * xp1 + cw_ref[0:1, sl] * xp2).astype(BF16)
            dup_ref[:, sl] = dup
            dh2 = dh2 + _dot_nt(dup, wup_ref[:, sl])
        x1n, r3 = _rms_stats(x1_ref[...])
        d1, dg3 = _rms_bwd(dh2, x1n, r3, g3_ref[...])
        dx1 = dx2_ref[...] + d1
        dx1_ref[...] = dx1
        dg3_ref[...] += dg3
        on, r2 = _rms_stats(o_ref[...])
        do, dg2 = _rms_bwd(dx1, on, r2, g2_ref[...])
        dg2_ref[...] += dg2
        dob = do.astype(BF16)
        do_ref[...] = dob
        dmixed = _dot_nt(dob, wout_ref[...])
        ga = gate_ref[:, 0:D_MODEL]
        gb = gate_ref[:, D_MODEL:2 * D_MODEL]
        dpa = (dmixed * ga).astype(BF16)
        dpb = (dmixed * gb).astype(BF16)
        dpa_ref[...] = dpa
        dpb_ref[...] = dpb
        dga = dmixed * pa_ref[...] * ga * (1.0 - ga)
        dgb = dmixed * pb_ref[...] * gb * (1.0 - gb)
        dgt_ref[:, 0:D_MODEL] = dga.astype(BF16)
        dgt_ref[:, D_MODEL:2 * D_MODEL] = dgb.astype(BF16)
        dbg_ref[:, 0:D_MODEL] += jnp.sum(dga, axis=0, keepdims=True)
        dbg_ref[:, D_MODEL:2 * D_MODEL] += jnp.sum(dgb, axis=0, keepdims=True)
        dya_ref[...] = _dot_nt(dpa, wba_ref[...]).astype(BF16)
        dyb_ref[...] = _dot_nt(dpb, wbb_ref[...]).astype(BF16)

    def o_(n, dt):
        return (_sds((t_rows, n), dt), _row(tm, n))

    def acc(r, n):
        return (_sds((r, n), F32), _full((r, n)))

    halo = pl.BlockSpec((8, cdim), lambda i: (jnp.minimum((i + 1) * hb, last_blk), 0))
    ins = [(du, _row(tm, cdim)), (du, halo), (convw8, _full(convw8.shape)), (wup, _full(wup.shape)),
           (dx2, _row(tm, D_MODEL)), (x1, _row(tm, D_MODEL)), (g3, _full(g3.shape)), (o, _row(tm, D_MODEL)),
           (g2, _full(g2.shape)), (wout, _full(wout.shape)), (gate, _row(tm, 2048)), (pa, _row(tm, D_MODEL)),
           (pb, _row(tm, D_MODEL)), (wba, _full(wba.shape)), (wbb, _full(wbb.shape))]
    outs = [o_(cdim, BF16), o_(D_MODEL, F32), o_(D_MODEL, BF16), o_(D_MODEL, BF16), o_(D_MODEL, BF16),
            o_(2048, BF16), o_(1024, BF16), o_(1024, BF16), acc(1, D_MODEL), acc(1, D_MODEL), acc(1, 2048)]
    return _rows_call("bwd_mid", body, t_rows, tm, ins, outs)


def _bwd_in(dqs, dks, dvs, dqm, dkm, dvm, tabs, consts, cq, ckv, gq, gkv, wuq, wk, wv, dgates, win, x, g1, dx1, tm):
    t_rows = x.shape[0]

    def body(dqs_ref, dks_ref, dvs_ref, dqm_ref, dkm_ref, dvm_ref, ca, sa1, sa2, cb, sb1, sb2, c_ref, cq_ref,
             ckv_ref, gq_ref, gkv_ref, wuq_ref, wk_ref, wv_ref, dgt_ref, win_ref, x_ref, g1_ref, dx1_ref,
             dz_ref, dqb_ref, dx_ref, dgq_ref, dgkv_ref, dg1_ref):
        i = pl.program_id(0)

        @pl.when(i == 0)
        def _():
            dgq_ref[...] = jnp.zeros(dgq_ref.shape, F32)
            dgkv_ref[...] = jnp.zeros(dgkv_ref.shape, F32)
            dg1_ref[...] = jnp.zeros(dg1_ref.shape, F32)

        ta = (ca[...], sa1[...], sa2[...])
        tb = (cb[...], sb1[...], sb2[...])
        dqs = jnp.concatenate([dqs_ref[h] for h in range(HEADS)], axis=1)
        dz_ref[:, Z_QA:Z_KA] = _rope_t(dqs, *ta, A_HEAD_DIM // 2).astype(BF16)
        dz_ref[:, Z_KA:Z_VA] = _rope_t(dks_ref[...], *ta, A_HEAD_DIM // 2).astype(BF16)
        dz_ref[:, Z_VA:Z_CQ] = dvs_ref[...].astype(BF16)
        dqm = jnp.concatenate([dqm_ref[h] for h in range(HEADS)], axis=1)
        dqb = _rope_t(dqm, *tb, ROPE_DIM // 2).astype(BF16)
        dqb_ref[...] = dqb
        dcqn = _dot_nt(dqb, wuq_ref[...])
        cqn, rq = _rms_stats(cq_ref[...])
        dcq, dgq = _rms_bwd(dcqn, cqn, rq, gq_ref[...])
        dgq_ref[...] += dgq
        dz_ref[:, Z_CQ:Z_CKV] = dcq.astype(BF16)
        dkm = dkm_ref[...]
        dslot = dkm[:, 0:LANES]
        for h in range(1, HEADS):
            dslot = dslot + dkm[:, h * LANES:(h + 1) * LANES]
        dz_ref[:, Z_KR:Z_GATE] = _rope_t(dslot * c_ref[10:11, :], *tb, ROPE_DIM // 2).astype(BF16)
        dckvn = _dot_nt(dkm.astype(BF16), wk_ref[...]) + _dot_nt(dvm_ref[...].astype(BF16), wv_ref[...])
        ckvn, rkv = _rms_stats(ckv_ref[...])
        dckv, dgkv = _rms_bwd(dckvn, ckvn, rkv, gkv_ref[...])
        dgkv_ref[...] += dgkv
        dz_ref[:, Z_CKV:Z_KR] = dckv.astype(BF16)
        dz_ref[:, Z_GATE:ZW] = dgt_ref[...]
        dh1 = _dot_nt(dz_ref[...], win_ref[...])
        xn, r1 = _rms_stats(x_ref[...])
        d0, dg1 = _rms_bwd(dh1, xn, r1, g1_ref[...])
        dg1_ref[...] += dg1
        dx_ref[...] = dx1_ref[...] + d0

    def acc(n):
        return (_sds((1, n), F32), _full((1, n)))

    ins = [(dqs, _heads(tm, HEADS)), (dks, _row(tm, 256)), (dvs, _row(tm, 256)), (dqm, _heads(tm, HEADS)),
           (dkm, _row(tm, 1024)), (dvm, _row(tm, 1024))] + [(t, _row(tm, LANES)) for t in tabs] + [
           (consts, _full(consts.shape)), (cq, _row(tm, 256)), (ckv, _row(tm, 128)), (gq, _full(gq.shape)),
           (gkv, _full(gkv.shape)), (wuq, _full(wuq.shape)), (wk, _full(wk.shape)), (wv, _full(wv.shape)),
           (dgates, _row(tm, 2048)), (win, _full(win.shape)), (x, _row(tm, D_MODEL)), (g1, _full(g1.shape)),
           (dx1, _row(tm, D_MODEL))]
    outs = [(_sds((t_rows, ZW), BF16), _row(tm, ZW)), (_sds((t_rows, 1024), BF16), _row(tm, 1024)),
            (_sds((t_rows, D_MODEL), F32), _row(tm, D_MODEL)), acc(256), acc(128), acc(D_MODEL)]
    return _rows_call("bwd_in", body, t_rows, tm, ins, outs)


def _pick_cols(n):
    best = LANES
    for d in range(LANES, min(n, 1408) + 1, LANES):
        if n % d == 0:
            best = d
    return best


def _mm_tn(name, a, b):
    t_rows, m = a.shape
    n = b.shape[1]
    bk = min(512, t_rows)
    bm, bn = _pick_cols(m), _pick_cols(n)

    def body(a_ref, b_ref, o_ref):
        @pl.when(pl.program_id(2) == 0)
        def _():
            o_ref[...] = jnp.zeros((bm, bn), F32)

        o_ref[...] += _dot_tn(a_ref[...].astype(BF16), b_ref[...].astype(BF16))

    return pl.pallas_call(
        body, name=name, grid=(m // bm, n // bn, t_rows // bk),
        in_specs=[pl.BlockSpec((bk, bm), lambda i, j, k: (k, i)), pl.BlockSpec((bk, bn), lambda i, j, k: (k, j))],
        out_specs=pl.BlockSpec((bm, bn), lambda i, j, k: (i, j)),
        out_shape=_sds((m, n), F32),
        compiler_params=pltpu.CompilerParams(dimension_semantics=("arbitrary",) * 3, vmem_limit_bytes=VMEM_LIMIT),
    )(a, b)


PACK_ROWS = 512


def _pack_tile(rows):
    t = PACK_ROWS
    while rows % t:
        t //= 2
    return t


def _add_pair(a, b):
    _, rows, _ = a.shape
    t = _pack_tile(rows)

    def body(a_ref, b_ref, o_ref):
        o_ref[...] = (a_ref[...] + b_ref[...]).astype(BF16)

    spec = pl.BlockSpec((4, t, LANES), lambda i: (0, i, 0))
    return pl.pallas_call(body, name="rs_add_pair", grid=(rows // t,), in_specs=[spec, spec], out_specs=spec,
                          out_shape=_sds(a.shape, BF16))(a, b)


def _add_chips(parts):
    _, rows, _ = parts.shape
    t = _pack_tile(rows)

    def body(p_ref, o_ref):
        acc = p_ref[0].astype(F32)
        for j in range(1, 4):
            acc = acc + p_ref[j].astype(F32)
        o_ref[...] = acc

    return pl.pallas_call(body, name="rs_add_chips", grid=(rows // t,),
                          in_specs=[pl.BlockSpec((4, t, LANES), lambda i: (0, i, 0))],
                          out_specs=pl.BlockSpec((t, LANES), lambda i: (i, 0)),
                          out_shape=_sds((rows, LANES), F32))(parts)


def _add_devices(parts):
    n, rows, _ = parts.shape

    def body(p_ref, o_ref):
        acc = p_ref[0]
        for j in range(1, n):
            acc = acc + p_ref[j]
        o_ref[...] = acc

    return pl.pallas_call(body, name="small_add", grid=(1,),
                          in_specs=[pl.BlockSpec((n, rows, LANES), lambda i: (0, 0, 0))],
                          out_specs=pl.BlockSpec((rows, LANES), lambda i: (0, 0)),
                          out_shape=_sds((rows, LANES), F32))(parts)


def _adam_rows(k, n):
    target = max(8, (1 << 20) // (4 * n))
    if k <= target:
        return k
    best = None
    for d in range(8, target + 1, 8):
        if k % d == 0:
            best = d
    return best if best is not None else k


def _adamw(name, w, g, m, v):
    k, n = w.shape
    bk = _adam_rows(k, n)
    c1 = 1.0 - ADAM_B1 ** ADAM_STEP
    c2 = 1.0 - ADAM_B2 ** ADAM_STEP

    def body(w_ref, g_ref, m_ref, v_ref, d_ref, mo_ref, vo_ref):
        g_ = g_ref[...]
        m_ = ADAM_B1 * m_ref[...] + (1.0 - ADAM_B1) * g_
        v_ = ADAM_B2 * v_ref[...] + (1.0 - ADAM_B2) * (g_ * g_)
        mo_ref[...] = m_
        vo_ref[...] = v_
        d_ref[...] = -ADAM_LR * ((m_ / c1) / (jnp.sqrt(v_ / c2) + ADAM_EPS) + ADAM_WD * w_ref[...])

    spec = pl.BlockSpec((bk, n), lambda i: (i, 0))
    return pl.pallas_call(body, name=name, grid=(k // bk,), in_specs=[spec] * 4, out_specs=[spec] * 3,
                          out_shape=[_sds((k, n), F32)] * 3,
                          compiler_params=pltpu.CompilerParams(vmem_limit_bytes=VMEM_LIMIT))(w, g, m, v)


_HBM = pl.BlockSpec(memory_space=pltpu.HBM)


def _me():
    return lax.axis_index("x"), lax.axis_index("y"), lax.axis_index("c")


def _other_chips(x, y):
    return [(1 - x, y), (x, 1 - y), (1 - x, 1 - y)]


def _gather_weights(shard):
    def body(x_ref, out_ref, send_sems, recv_sems, local_sem):
        x, y, c = _me()
        sibling = (x, y, 1 - c)
        chips = _other_chips(x, y)

        def slot(px, py, pc):
            return out_ref.at[2 * px + py, pc]

        def copy(k, src, dst, to):
            return pltpu.make_async_remote_copy(src_ref=src, dst_ref=dst, send_sem=send_sems.at[k],
                                                recv_sem=recv_sems.at[k], device_id=to, device_id_type=MESH)

        mine = pltpu.make_async_copy(x_ref, out_ref.at[2 * x + y], local_sem)
        mine.start()
        first = [copy(j, x_ref.at[c], slot(x, y, c), (*chip, c)) for j, chip in enumerate(chips)]
        for cp in first:
            cp.start()
        passed = [copy(3 + j, slot(*chip, c), slot(*chip, c), sibling) for j, chip in enumerate(chips)]
        for j, chip in enumerate(chips):
            copy(j, x_ref.at[c], slot(*chip, c), (*chip, c)).wait_recv()
            passed[j].start()
        for j, chip in enumerate(chips):
            copy(3 + j, slot(*chip, 1 - c), slot(*chip, 1 - c), sibling).wait_recv()
        for cp in first + passed:
            cp.wait_send()
        mine.wait()

    return pl.pallas_call(
        body, name="gather_weights", out_shape=_sds((4,) + shard.shape, shard.dtype), in_specs=[_HBM], out_specs=_HBM,
        scratch_shapes=[pltpu.SemaphoreType.DMA((6,)), pltpu.SemaphoreType.DMA((6,)), pltpu.SemaphoreType.DMA],
    )(shard)


def _swap_sibling(v):
    def body(v_ref, out_ref, send_sem, recv_sem):
        x, y, c = _me()
        cp = pltpu.make_async_remote_copy(src_ref=v_ref, dst_ref=out_ref, send_sem=send_sem, recv_sem=recv_sem,
                                          device_id=(x, y, 1 - c), device_id_type=MESH)
        cp.start()
        cp.wait()

    return pl.pallas_call(
        body, name="swap_sibling", out_shape=_sds(v.shape, v.dtype), in_specs=[_HBM], out_specs=_HBM,
        scratch_shapes=[pltpu.SemaphoreType.DMA, pltpu.SemaphoreType.DMA],
    )(v)


def _scatter_chips(s):
    def body(s_ref, out_ref, send_sems, recv_sems, local_sem):
        x, y, c = _me()
        k = 2 * x + y
        chips = _other_chips(x, y)
        mine = pltpu.make_async_copy(s_ref.at[k], out_ref.at[k], local_sem)
        mine.start()
        sends = [pltpu.make_async_remote_copy(src_ref=s_ref.at[2 * cx + cy], dst_ref=out_ref.at[k],
                                              send_sem=send_sems.at[j], recv_sem=recv_sems.at[j],
                                              device_id=(cx, cy, c), device_id_type=MESH)
                 for j, (cx, cy) in enumerate(chips)]
        for cp in sends:
            cp.start()
        for j, (cx, cy) in enumerate(chips):
            pltpu.make_async_remote_copy(src_ref=s_ref.at[k], dst_ref=out_ref.at[2 * cx + cy],
                                         send_sem=send_sems.at[j], recv_sem=recv_sems.at[j],
                                         device_id=(cx, cy, c), device_id_type=MESH).wait_recv()
        for cp in sends:
            cp.wait_send()
        mine.wait()

    return pl.pallas_call(
        body, name="scatter_chips", out_shape=_sds(s.shape, s.dtype), in_specs=[_HBM], out_specs=_HBM,
        scratch_shapes=[pltpu.SemaphoreType.DMA((3,)), pltpu.SemaphoreType.DMA((3,)), pltpu.SemaphoreType.DMA],
    )(s)


def _join_sibling(f):
    def body(f_ref, out_ref, send_sem, recv_sem, local_sem):
        x, y, c = _me()
        mine = pltpu.make_async_copy(f_ref, out_ref.at[c], local_sem)
        mine.start()
        cp = pltpu.make_async_remote_copy(src_ref=f_ref, dst_ref=out_ref.at[c], send_sem=send_sem, recv_sem=recv_sem,
                                          device_id=(x, y, 1 - c), device_id_type=MESH)
        cp.start()
        pltpu.make_async_remote_copy(src_ref=f_ref, dst_ref=out_ref.at[1 - c], send_sem=send_sem, recv_sem=recv_sem,
                                     device_id=(x, y, 1 - c), device_id_type=MESH).wait_recv()
        cp.wait_send()
        mine.wait()

    return pl.pallas_call(
        body, name="join_sibling", out_shape=_sds((2,) + f.shape, f.dtype), in_specs=[_HBM], out_specs=_HBM,
        scratch_shapes=[pltpu.SemaphoreType.DMA, pltpu.SemaphoreType.DMA, pltpu.SemaphoreType.DMA],
    )(f)


def _gather_small(name, v):
    def body(v_ref, out_ref, send_sems, recv_sems, local_sem):
        x, y, c = _me()
        me = 4 * x + 2 * y + c
        mine = pltpu.make_async_copy(v_ref, out_ref.at[me], local_sem)
        mine.start()
        peers = []
        for f in range(1, 8):
            fx, fy, fc = (f >> 2) & 1, (f >> 1) & 1, f & 1
            px = 1 - x if fx else x
            py = 1 - y if fy else y
            pc = 1 - c if fc else c
            peers.append((f - 1, (px, py, pc)))
        sends = [pltpu.make_async_remote_copy(src_ref=v_ref, dst_ref=out_ref.at[me], send_sem=send_sems.at[k],
                                              recv_sem=recv_sems.at[k], device_id=peer, device_id_type=MESH)
                 for k, peer in peers]
        for cp in sends:
            cp.start()
        for k, (px, py, pc) in peers:
            pltpu.make_async_remote_copy(src_ref=v_ref, dst_ref=out_ref.at[4 * px + 2 * py + pc],
                                         send_sem=send_sems.at[k], recv_sem=recv_sems.at[k],
                                         device_id=(px, py, pc), device_id_type=MESH).wait_recv()
        for cp in sends:
            cp.wait_send()
        mine.wait()

    return pl.pallas_call(
        body, name=name, out_shape=_sds((8,) + v.shape, v.dtype), in_specs=[_HBM], out_specs=_HBM,
        scratch_shapes=[pltpu.SemaphoreType.DMA((7,)), pltpu.SemaphoreType.DMA((7,)), pltpu.SemaphoreType.DMA],
    )(v)


_BIG = (("w_in", (1024, 3232), 1), ("w_uq", (256, 768), 1), ("w_ukv", (128, 1024), 1), ("w_branch_a", (512, 1024), 1),
        ("w_branch_b", (512, 1024), 1), ("w_out", (1024, 1024), 0), ("w_up", (1024, 5632), 1),
        ("w_down", (2816, 1024), 0), ("w_ple_gate", (1024, 1024), 0), ("w_ple", (256, 1024), 1))


def _shard_shape(shape, axis):
    return (shape[0] // 4, shape[1]) if axis == 0 else (shape[0], shape[1] // 4)


def _half_rows(shape, axis):
    k, n = _shard_shape(shape, axis)
    return k * n // (2 * LANES)


def _pack_shards(shards, dtype):
    parts = [shards[name].astype(dtype).reshape(2, _half_rows(shape, axis), LANES) for name, shape, axis in _BIG]
    return jnp.concatenate(parts, axis=1)


def _unpack_gathered(g):
    out, off = {}, 0
    for name, shape, axis in _BIG:
        r = _half_rows(shape, axis)
        k, n = _shard_shape(shape, axis)
        w = g[:, :, off:off + r, :].reshape(4, k, n)
        out[name] = w.reshape(shape) if axis == 0 else w.transpose(1, 0, 2).reshape(shape)
        off += r
    return out


def _pack_grad_halves(grads, c):
    keep, send = [], []
    for name, shape, axis in _BIG:
        k, n = _shard_shape(shape, axis)
        r = _half_rows(shape, axis)
        g = grads[name]
        g4 = g.reshape(4, k, n) if axis == 0 else g.reshape(k, 4, n).transpose(1, 0, 2)
        g4 = g4.reshape(4, 2, r, LANES)
        keep.append(lax.dynamic_index_in_dim(g4, c, 1, keepdims=False))
        send.append(lax.dynamic_index_in_dim(g4, 1 - c, 1, keepdims=False))
    return jnp.concatenate(keep, axis=1), jnp.concatenate(send, axis=1)


def _unpack_shard_grads(f):
    out, off = {}, 0
    for name, shape, axis in _BIG:
        r = _half_rows(shape, axis)
        out[name] = f[:, off:off + r, :].reshape(_shard_shape(shape, axis))
        off += r
    return out


def _pad_slots(w, heads, dim, axis):
    if axis == 1:
        k = w.shape[0]
        return jnp.pad(w.reshape(k, heads, dim), ((0, 0), (0, 0), (0, LANES - dim))).reshape(k, heads * LANES)
    n = w.shape[1]
    return jnp.pad(w.reshape(heads, dim, n), ((0, 0), (0, LANES - dim), (0, 0))).reshape(heads * LANES, n)


def _unpad_slots(w, heads, dim, axis):
    if axis == 1:
        k = w.shape[0]
        return w.reshape(k, heads, LANES)[:, :, :dim].reshape(k, heads * dim)
    n = w.shape[1]
    return w.reshape(heads, LANES, n)[:, :dim, :].reshape(heads * dim, n)


def _pad_w_in(w):
    kr = jnp.pad(w[:, 1152:1184], ((0, 0), (NOPE_DIM, LANES - NOPE_DIM - ROPE_DIM)))
    return jnp.concatenate([_pad_slots(w[:, 0:512], HEADS, A_HEAD_DIM, 1),
                            _pad_slots(w[:, 512:640], A_KV_HEADS, A_HEAD_DIM, 1),
                            _pad_slots(w[:, 640:768], A_KV_HEADS, A_HEAD_DIM, 1),
                            w[:, 768:1024], w[:, 1024:1152], kr, w[:, 1184:3232]], axis=1)


def _unpad_w_in(w):
    return jnp.concatenate([_unpad_slots(w[:, Z_QA:Z_KA], HEADS, A_HEAD_DIM, 1),
                            _unpad_slots(w[:, Z_KA:Z_VA], A_KV_HEADS, A_HEAD_DIM, 1),
                            _unpad_slots(w[:, Z_VA:Z_CQ], A_KV_HEADS, A_HEAD_DIM, 1),
                            w[:, Z_CQ:Z_CKV], w[:, Z_CKV:Z_KR],
                            w[:, Z_KR + NOPE_DIM:Z_KR + NOPE_DIM + ROPE_DIM], w[:, Z_GATE:ZW]], axis=1)


_SMALL = (("attn_pre_norm", 1024), ("attn_post_norm", 1024), ("b_gate", 2048), ("sinks", 8), ("q_a_norm", 256),
          ("kv_a_norm", 128), ("mlp_pre_norm", 1024), ("mlp_post_norm", 1024), ("conv_b", 5632), ("ple_norm", 1024),
          ("conv_w", 3 * 5632))


def _small_rows(n):
    return -(-n // LANES)


def _pack_small(vals):
    parts = []
    for name, n in _SMALL:
        r = _small_rows(n)
        parts.append(jnp.pad(vals[name].reshape(-1), (0, r * LANES - n)).reshape(r, LANES))
    rows = sum(_small_rows(n) for _, n in _SMALL)
    pad = -rows % 8
    if pad:
        parts.append(jnp.zeros((pad, LANES), F32))
    return jnp.concatenate(parts, axis=0)


def _unpack_small(buf):
    out, off = {}, 0
    for name, n in _SMALL:
        r = _small_rows(n)
        out[name] = buf[off:off + r].reshape(-1)[:n]
        off += r
    return out


def kernel(x, p, positions, attn_pre_norm, attn_post_norm, w_in, b_gate, sinks, q_a_norm, w_uq, kv_a_norm, w_ukv, w_branch_a, w_branch_b, w_out, mlp_pre_norm, mlp_post_norm, w_up, conv_w, conv_b, w_down, ple_norm, w_ple_gate, w_ple, loss_target, m_attn_pre_norm, m_attn_post_norm, m_w_in, m_b_gate, m_sinks, m_q_a_norm, m_w_uq, m_kv_a_norm, m_w_ukv, m_w_branch_a, m_w_branch_b, m_w_out, m_mlp_pre_norm, m_mlp_post_norm, m_w_up, m_conv_w, m_conv_b, m_w_down, m_ple_norm, m_w_ple_gate, m_w_ple, v_attn_pre_norm, v_attn_post_norm, v_w_in, v_b_gate, v_sinks, v_q_a_norm, v_w_uq, v_kv_a_norm, v_w_ukv, v_w_branch_a, v_w_branch_b, v_w_out, v_mlp_pre_norm, v_mlp_post_norm, v_w_up, v_conv_w, v_conv_b, v_w_down, v_ple_norm, v_w_ple_gate, v_w_ple):
    names = ["attn_pre_norm", "attn_post_norm", "w_in", "b_gate", "sinks", "q_a_norm", "w_uq", "kv_a_norm", "w_ukv",
             "w_branch_a", "w_branch_b", "w_out", "mlp_pre_norm", "mlp_post_norm", "w_up", "conv_w", "conv_b",
             "w_down", "ple_norm", "w_ple_gate", "w_ple"]
    wts = dict(zip(names, [attn_pre_norm, attn_post_norm, w_in, b_gate, sinks, q_a_norm, w_uq, kv_a_norm, w_ukv,
                           w_branch_a, w_branch_b, w_out, mlp_pre_norm, mlp_post_norm, w_up, conv_w, conv_b, w_down,
                           ple_norm, w_ple_gate, w_ple]))
    moms = dict(zip(names, [m_attn_pre_norm, m_attn_post_norm, m_w_in, m_b_gate, m_sinks, m_q_a_norm, m_w_uq,
                            m_kv_a_norm, m_w_ukv, m_w_branch_a, m_w_branch_b, m_w_out, m_mlp_pre_norm,
                            m_mlp_post_norm, m_w_up, m_conv_w, m_conv_b, m_w_down, m_ple_norm, m_w_ple_gate, m_w_ple]))
    vars_ = dict(zip(names, [v_attn_pre_norm, v_attn_post_norm, v_w_in, v_b_gate, v_sinks, v_q_a_norm, v_w_uq,
                             v_kv_a_norm, v_w_ukv, v_w_branch_a, v_w_branch_b, v_w_out, v_mlp_pre_norm,
                             v_mlp_post_norm, v_w_up, v_conv_w, v_conv_b, v_w_down, v_ple_norm, v_w_ple_gate, v_w_ple]))
    w2 = {n: a.reshape(a.shape[-2:]) for n, a in wts.items()}
    m2 = {n: a.reshape(a.shape[-2:]) for n, a in moms.items()}
    v2 = {n: a.reshape(a.shape[-2:]) for n, a in vars_.items()}

    t_rows = x.shape[-2]
    tm = min(256, t_rows)
    xc, yc, cc = lax.axis_index("x"), lax.axis_index("y"), lax.axis_index("c")
    chip = 2 * xc + yc

    x2d = x.reshape(t_rows, D_MODEL)
    p2d = p.reshape(t_rows, PLE_DIM)
    tgt = loss_target.reshape(t_rows, D_MODEL)
    pos_f = positions.reshape(t_rows, 1).astype(F32)

    gathered = _gather_weights(_pack_shards(w2, BF16))
    full = _unpack_gathered(gathered)
    cw_rows = 3 * 1408 // LANES
    cw_all = _gather_small("gather_conv_w", jnp.pad(w2["conv_w"].reshape(cw_rows, LANES), ((0, 40 - cw_rows), (0, 0))))
    conv_full = cw_all[0::2, :cw_rows].reshape(4, 3, 1408).transpose(1, 0, 2).reshape(3, 2 * D_FF)
    convw8 = jnp.pad(conv_full, ((0, 5), (0, 0)))

    win = _pad_w_in(full["w_in"])
    wuq = _pad_slots(full["w_uq"], HEADS, NOPE_DIM + ROPE_DIM, 1)
    ukv = full["w_ukv"].reshape(KV_LORA, HEADS, NOPE_DIM + V_DIM)
    wk = _pad_slots(ukv[:, :, :NOPE_DIM].reshape(KV_LORA, HEADS * NOPE_DIM), HEADS, NOPE_DIM, 1)
    wv = _pad_slots(ukv[:, :, NOPE_DIM:].reshape(KV_LORA, HEADS * V_DIM), HEADS, V_DIM, 1)
    wba = _pad_slots(full["w_branch_a"], HEADS, A_HEAD_DIM, 0)
    wbb = _pad_slots(full["w_branch_b"], HEADS, V_DIM, 0)
    wout, wup, wdown, wpg, wple = full["w_out"], full["w_up"], full["w_down"], full["w_ple_gate"], full["w_ple"]
    g1, g2, g3, g4, g5 = (w2["attn_pre_norm"], w2["attn_post_norm"], w2["mlp_pre_norm"], w2["mlp_post_norm"],
                          w2["ple_norm"])
    gq, gkv, bg, convb = w2["q_a_norm"], w2["kv_a_norm"], w2["b_gate"], w2["conv_b"]
    sink3 = w2["sinks"].reshape(HEADS, 1, 1)

    consts = _rope_consts()
    tabs = _rope_tables(pos_f, consts, tm)
    h1, qs, ks, vs, cq, cqn, ckv, ckvn, qm, km, vm, gate = _fwd_in(x2d, g1, win, bg, gq, gkv, wuq, wk, wv, tabs, tm)
    swa = dict(heads=HEADS, group=HEADS // A_KV_HEADS, scale=A_HEAD_DIM ** -0.5, window=SWA_WINDOW)
    mla = dict(heads=HEADS, group=1, scale=(NOPE_DIM + ROPE_DIM) ** -0.5, window=None)
    ya, lse_a = _attn_fwd("swa_fwd", qs, ks, vs, sink3, has_sink=True, **swa)
    yb, lse_b = _attn_fwd("mla_fwd", qm, km, vm, sink3, has_sink=False, **mla)
    pa, pb, mixed, o, x1, h2 = _fwd_mix(x2d, ya, yb, gate, wba, wbb, wout, g2, g3, tm)
    up, a = _fwd_up(h2, wup, convw8, convb, tm)
    ff, x2, e, n5, sg, dx3, loss_part = _fwd_out(a, wdown, x1, g4, p2d, wple, g5, wpg, tgt, tm)

    dpre, de, dx2, dff, du, dg5, dg4, dconvb, dconvw8 = _bwd_out(dx3, e, sg, x2, ff, g5, g4, wpg, wdown, up, convw8,
                                                                 convb, tm)
    dup, dx1, do, dpa, dpb, dgates, dya, dyb, dg3, dg2, dbg = _bwd_mid(du, convw8, wup, dx2, x1, g3, o, g2, wout, gate,
                                                                       pa, pb, wba, wbb, tm)
    dl_a, dsink = _attn_delta("swa_delta", ya, dya, lse_a, sink3, heads=HEADS, has_sink=True)
    dl_b, _ = _attn_delta("mla_delta", yb, dyb, lse_b, sink3, heads=HEADS, has_sink=False)
    dqs, dks, dvs = _attn_bwd("swa_bwd", qs, ks, vs, dya, lse_a, dl_a, **swa)
    dqm, dkm, dvm = _attn_bwd("mla_bwd", qm, km, vm, dyb, lse_b, dl_b, **mla)
    dz, dqb, dx, dgq, dgkv, dg1 = _bwd_in(dqs, dks, dvs, dqm, dkm, dvm, tabs, consts, cq, ckv, gq, gkv, wuq, wk, wv,
                                           dgates, win, x2d, g1, dx1, tm)

    dwk = _unpad_slots(_mm_tn("dw_k", ckvn, dkm), HEADS, NOPE_DIM, 1).reshape(KV_LORA, HEADS, NOPE_DIM)
    dwv = _unpad_slots(_mm_tn("dw_v", ckvn, dvm), HEADS, V_DIM, 1).reshape(KV_LORA, HEADS, V_DIM)
    grads = {
        "w_in": _unpad_w_in(_mm_tn("dw_in", h1, dz)),
        "w_uq": _unpad_slots(_mm_tn("dw_uq", cqn, dqb), HEADS, NOPE_DIM + ROPE_DIM, 1),
        "w_ukv": jnp.concatenate([dwk, dwv], axis=2).reshape(KV_LORA, HEADS * (NOPE_DIM + V_DIM)),
        "w_branch_a": _unpad_slots(_mm_tn("dw_branch_a", ya, dpa), HEADS, A_HEAD_DIM, 0),
        "w_branch_b": _unpad_slots(_mm_tn("dw_branch_b", yb, dpb), HEADS, V_DIM, 0),
        "w_out": _mm_tn("dw_out", mixed, do),
        "w_up": _mm_tn("dw_up", h2, dup),
        "w_down": _mm_tn("dw_down", a, dff),
        "w_ple_gate": _mm_tn("dw_ple_gate", n5, dpre),
        "w_ple": _mm_tn("dw_ple", p2d, de),
    }

    keep, send = _pack_grad_halves(grads, cc)
    pair = _add_pair(keep, _swap_sibling(send))
    reduced = _add_chips(_scatter_chips(pair))
    shard_grads = _unpack_shard_grads(_join_sibling(reduced))

    small = {"attn_pre_norm": dg1, "attn_post_norm": dg2, "b_gate": dbg, "sinks": dsink, "q_a_norm": dgq,
             "kv_a_norm": dgkv, "mlp_pre_norm": dg3, "mlp_post_norm": dg4, "conv_b": dconvb, "ple_norm": dg5,
             "conv_w": dconvw8[0:3]}
    small_sum = _unpack_small(_add_devices(_gather_small("gather_small_grads", _pack_small(small))))
    for n in names:
        if n in small_sum and n != "conv_w":
            shard_grads[n] = small_sum[n].reshape(w2[n].shape)
    shard_grads["conv_w"] = lax.dynamic_index_in_dim(small_sum["conv_w"].reshape(3, 4, 1408), chip, 1, keepdims=False)

    loss = lax.psum(loss_part[0, 0], MESH_AXES)

    g_out, d_out, m_out, v_out = [], [], [], []
    for n in names:
        g = shard_grads[n]
        d, mn, vn = _adamw("adamw_" + n, w2[n], g, m2[n], v2[n])
        shp = wts[n].shape
        g_out.append(g.reshape(shp))
        d_out.append(d.reshape(shp))
        m_out.append(mn.reshape(shp))
        v_out.append(vn.reshape(shp))
    return (loss, dx.reshape(x.shape), *g_out, *d_out, *m_out, *v_out)
```

```python
import functools
import math

import numpy as np
import jax
import jax.numpy as jnp
from jax import lax
from jax.experimental import pallas as pl
from jax.experimental.pallas import tpu as pltpu

F32 = jnp.float32
BF16 = jnp.bfloat16

D_MODEL = 1024
D_FF = 2816
PLE_DIM = 256
ROPE_THETA = 10000.0
RMS_EPS = 1e-6
SWA_WINDOW = 128
HEADS = 8
A_KV_HEADS = 2
A_HEAD_DIM = 64
Q_LORA = 256
KV_LORA = 128
NOPE_DIM = 64
ROPE_DIM = 32
V_DIM = 64
LANES = 128
ZW = 4096
NEG = -1e30
SCALE_A = A_HEAD_DIM ** -0.5
SCALE_B = (NOPE_DIM + ROPE_DIM) ** -0.5

ADAM_LR = 0.001
ADAM_B1 = 0.9
ADAM_B2 = 0.999
ADAM_EPS = 1e-08
ADAM_WD = 0.01
ADAM_STEP = 10

VMEM_LIMIT = 60 * 1024 * 1024
MESH_AXES = ("x", "y", "c")
MESH = pl.DeviceIdType.MESH

Z_QA, Z_KA, Z_VA, Z_CQ, Z_CKV, Z_KR, Z_GATE = 0, 1024, 1280, 1536, 1792, 1920, 2048


def _dot(a, b):
    return jnp.dot(a, b, preferred_element_type=F32)


def _dot_nt(a, b):
    return lax.dot_general(a, b, (((1,), (1,)), ((), ())), preferred_element_type=F32)


def _dot_tn(a, b):
    return lax.dot_general(a, b, (((0,), (0,)), ((), ())), preferred_element_type=F32)


def _rms_stats(x):
    r = lax.rsqrt(jnp.mean(x * x, axis=-1, keepdims=True) + RMS_EPS)
    return x * r, r


def _rms_bwd(dy, xn, r, g):
    dxn = dy * g
    dx = r * (dxn - xn * jnp.mean(dxn * xn, axis=-1, keepdims=True))
    dg = jnp.sum(dy * xn, axis=0, keepdims=True)
    return dx, dg


def _tile_lanes(t, n):
    return t if n == 1 else jnp.concatenate([t] * n, axis=1)


def _rope(x, c, s1, s2, half):
    w = x.shape[1]
    n = w // LANES
    return (x * _tile_lanes(c, n) + pltpu.roll(x, w - half, 1) * _tile_lanes(s1, n)
            + pltpu.roll(x, half, 1) * _tile_lanes(s2, n))


def _rope_t(dy, c, s1, s2, half):
    w = dy.shape[1]
    n = w // LANES
    return (dy * _tile_lanes(c, n) + pltpu.roll(dy * _tile_lanes(s1, n), half, 1)
            + pltpu.roll(dy * _tile_lanes(s2, n), w - half, 1))


def _sigmoid(x):
    return 1.0 / (1.0 + jnp.exp(-x))


_GELU_C = math.sqrt(2.0 / math.pi)


def _gelu_and_grad(x):
    x2 = x * x
    th = jnp.tanh(_GELU_C * (x + 0.044715 * x * x2))
    gel = 0.5 * x * (1.0 + th)
    dgel = 0.5 * (1.0 + th) + 0.5 * x * (1.0 - th * th) * (_GELU_C * (1.0 + 3.0 * 0.044715 * x2))
    return gel, dgel


def _conv_taps(up, h6, h7):
    rows = lax.broadcasted_iota(jnp.int32, up.shape, 0)
    r1 = pltpu.roll(up, 1, 0)
    r2 = pltpu.roll(up, 2, 0)
    xm1 = jnp.where(rows == 0, h7, r1)
    xm2 = jnp.where(rows == 0, h6, jnp.where(rows == 1, h7, r2))
    return xm1, xm2


def _conv_taps_next(du, n0, n1):
    tm = du.shape[0]
    rows = lax.broadcasted_iota(jnp.int32, du.shape, 0)
    r1 = pltpu.roll(du, tm - 1, 0)
    r2 = pltpu.roll(du, tm - 2, 0)
    xp1 = jnp.where(rows == tm - 1, n0, r1)
    xp2 = jnp.where(rows == tm - 2, n0, jnp.where(rows == tm - 1, n1, r2))
    return xp1, xp2


def _row(tm, n):
    return pl.BlockSpec((tm, n), lambda i: (i, 0))


def _full(shape):
    nd = len(shape)
    return pl.BlockSpec(tuple(shape), lambda i: (0,) * nd)


def _heads(tm, h):
    return pl.BlockSpec((h, tm, LANES), lambda i: (0, i, 0))


def _rows_call(name, body, t_rows, tm, ins, outs, scratch=()):
    return pl.pallas_call(
        body, name=name, grid=(t_rows // tm,),
        in_specs=[s for _, s in ins],
        out_specs=[s for _, s in outs],
        out_shape=[s for s, _ in outs],
        scratch_shapes=list(scratch),
        compiler_params=pltpu.CompilerParams(dimension_semantics=("arbitrary",), vmem_limit_bytes=VMEM_LIMIT),
    )(*[a for a, _ in ins])


def _sds(shape, dtype):
    return jax.ShapeDtypeStruct(tuple(shape), dtype)


def _rope_consts():
    c = np.zeros((16, LANES), np.float32)
    lane = np.arange(LANES)
    inv_a = (ROPE_THETA ** (-(np.arange(0, A_HEAD_DIM, 2, dtype=np.float32) / A_HEAD_DIM))).astype(np.float32)
    in_a = lane < A_HEAD_DIM
    c[0, in_a] = inv_a[lane[in_a] % (A_HEAD_DIM // 2)]
    c[1, in_a] = 1.0
    c[2, lane < A_HEAD_DIM // 2] = -1.0
    c[3, (lane >= A_HEAD_DIM // 2) & in_a] = 1.0
    inv_b = (ROPE_THETA ** (-(np.arange(0, ROPE_DIM, 2, dtype=np.float32) / ROPE_DIM))).astype(np.float32)
    pe = (lane >= NOPE_DIM) & (lane < NOPE_DIM + ROPE_DIM)
    c[5, pe] = inv_b[(lane[pe] - NOPE_DIM) % (ROPE_DIM // 2)]
    c[6, pe] = 1.0
    c[7, (lane >= NOPE_DIM) & (lane < NOPE_DIM + ROPE_DIM // 2)] = -1.0
    c[8, (lane >= NOPE_DIM + ROPE_DIM // 2) & (lane < NOPE_DIM + ROPE_DIM)] = 1.0
    c[9, lane < NOPE_DIM] = 1.0
    c[10, pe] = 1.0
    return jnp.asarray(c)


def _rope_tables(pos_f, consts, tm):
    t_rows = pos_f.shape[0]

    def body(pos_ref, c_ref, ca, sa1, sa2, cb, sb1, sb2):
        pos = pos_ref[...]
        ang = pos * c_ref[0:1, :]
        cs, sn = jnp.cos(ang), jnp.sin(ang)
        ca[...] = cs * c_ref[1:2, :]
        sa1[...] = sn * c_ref[2:3, :]
        sa2[...] = sn * c_ref[3:4, :]
        ang = pos * c_ref[5:6, :]
        cs, sn = jnp.cos(ang), jnp.sin(ang)
        cb[...] = cs * c_ref[6:7, :] + c_ref[9:10, :]
        sb1[...] = sn * c_ref[7:8, :]
        sb2[...] = sn * c_ref[8:9, :]

    tab = (_sds((t_rows, LANES), F32), _row(tm, LANES))
    return _rows_call("rope_tables", body, t_rows, tm,
                      [(pos_f, _row(tm, 1)), (consts, _full(consts.shape))], [tab] * 6)


def _fwd_in(x, g1, win, bg, gq, gkv, wuq, wk, wv, tabs, tm):
    t_rows = x.shape[0]

    def body(x_ref, g1_ref, win_ref, bg_ref, gq_ref, gkv_ref, wuq_ref, wk_ref, wv_ref,
             ca, sa1, sa2, cb, sb1, sb2,
             h1_ref, qs_ref, ks_ref, vs_ref, cq_ref, cqn_ref, ckv_ref, ckvn_ref, qm_ref, km_ref, vm_ref, gate_ref):
        xn, _ = _rms_stats(x_ref[...])
        hb = (xn * g1_ref[...]).astype(BF16)
        h1_ref[...] = hb
        ta = (ca[...], sa1[...], sa2[...])
        tb = (cb[...], sb1[...], sb2[...])
        qs_ref[...] = (_rope(_dot(hb, win_ref[:, Z_QA:Z_KA]), *ta, A_HEAD_DIM // 2) * SCALE_A).astype(BF16)
        ks_ref[...] = _rope(_dot(hb, win_ref[:, Z_KA:Z_VA]), *ta, A_HEAD_DIM // 2).astype(BF16)
        vs_ref[...] = _dot(hb, win_ref[:, Z_VA:Z_CQ]).astype(BF16)
        cq = _dot(hb, win_ref[:, Z_CQ:Z_CKV])
        cq_ref[...] = cq
        cqn, _ = _rms_stats(cq)
        cqb = (cqn * gq_ref[...]).astype(BF16)
        cqn_ref[...] = cqb
        qm_ref[...] = (_rope(_dot(cqb, wuq_ref[...]), *tb, ROPE_DIM // 2) * SCALE_B).astype(BF16)
        ckv = _dot(hb, win_ref[:, Z_CKV:Z_KR])
        ckv_ref[...] = ckv
        ckvn, _ = _rms_stats(ckv)
        ckvb = (ckvn * gkv_ref[...]).astype(BF16)
        ckvn_ref[...] = ckvb
        kpe = _rope(_dot(hb, win_ref[:, Z_KR:Z_GATE]), *tb, ROPE_DIM // 2)
        km_ref[...] = (_dot(ckvb, wk_ref[...]) + _tile_lanes(kpe, HEADS)).astype(BF16)
        vm_ref[...] = _dot(ckvb, wv_ref[...]).astype(BF16)
        gate_ref[...] = _sigmoid(_dot(hb, win_ref[:, Z_GATE:ZW]) + bg_ref[...])

    def o(n, dt):
        return (_sds((t_rows, n), dt), _row(tm, n))

    ins = [(x, _row(tm, D_MODEL)), (g1, _full(g1.shape)), (win, _full(win.shape)), (bg, _full(bg.shape)),
           (gq, _full(gq.shape)), (gkv, _full(gkv.shape)), (wuq, _full(wuq.shape)), (wk, _full(wk.shape)),
           (wv, _full(wv.shape))] + [(t, _row(tm, LANES)) for t in tabs]
    outs = [o(1024, BF16), o(1024, BF16), o(256, BF16), o(256, BF16), o(256, F32), o(256, BF16), o(128, F32),
            o(128, BF16), o(1024, BF16), o(1024, BF16), o(1024, BF16), o(2048, F32)]
    return _rows_call("fwd_in", body, t_rows, tm, ins, outs)


def _attn_tile(t_rows):
    return min(512, t_rows)


def _attn_fwd(name, q, k, v, sink, *, heads, group, window, has_sink):
    t_rows = q.shape[0]
    t = _attn_tile(t_rows)
    nq = t_rows // t
    banded = window is not None
    nj = 2 if banded else nq

    def kvidx(i, j):
        return jnp.maximum(i - 1 + j, 0) if banded else jnp.minimum(j, i)

    def body(q_ref, k_ref, v_ref, sink_ref, o_ref, lse_ref, m_s, l_s, acc_s):
        i = pl.program_id(1)
        j = pl.program_id(2)

        @pl.when(j == 0)
        def _():
            if has_sink:
                m_s[...] = jnp.zeros((t, 1), F32) + sink_ref[0]
                l_s[...] = jnp.ones((t, 1), F32)
            else:
                m_s[...] = jnp.full((t, 1), NEG, F32)
                l_s[...] = jnp.zeros((t, 1), F32)
            acc_s[...] = jnp.zeros((t, LANES), F32)

        def step(masked):
            s = _dot_nt(q_ref[...], k_ref[...])
            if masked:
                qpos = i * t + lax.broadcasted_iota(jnp.int32, (t, t), 0)
                kpos = kvidx(i, j) * t + lax.broadcasted_iota(jnp.int32, (t, t), 1)
                valid = kpos <= qpos
                if banded:
                    valid = valid & (qpos - kpos < window)
                s = jnp.where(valid, s, NEG)
            m_prev = m_s[...]
            m_new = jnp.maximum(m_prev, jnp.max(s, axis=1, keepdims=True))
            p = jnp.exp(s - m_new)
            alpha = jnp.exp(m_prev - m_new)
            l_s[...] = alpha * l_s[...] + jnp.sum(p, axis=1, keepdims=True)
            acc_s[...] = alpha * acc_s[...] + _dot(p.astype(BF16), v_ref[...])
            m_s[...] = m_new

        if banded:
            pl.when(i - 1 + j >= 0)(lambda: step(True))
        else:
            pl.when(j < i)(lambda: step(False))
            pl.when(j == i)(lambda: step(True))

        @pl.when(j == nj - 1)
        def _():
            l = l_s[...]
            o_ref[...] = (acc_s[...] / l).astype(o_ref.dtype)
            lse_ref[0] = m_s[...] + jnp.log(l)

    return pl.pallas_call(
        body, name=name, grid=(heads, nq, nj),
        in_specs=[pl.BlockSpec((t, LANES), lambda h, i, j: (i, h)),
                  pl.BlockSpec((t, LANES), lambda h, i, j: (kvidx(i, j), h // group)),
                  pl.BlockSpec((t, LANES), lambda h, i, j: (kvidx(i, j), h // group)),
                  pl.BlockSpec((1, 1, 1), lambda h, i, j: (h, 0, 0))],
        out_specs=[pl.BlockSpec((t, LANES), lambda h, i, j: (i, h)),
                   pl.BlockSpec((1, t, 1), lambda h, i, j: (h, i, 0))],
        out_shape=[_sds((t_rows, heads * LANES), BF16), _sds((heads, t_rows, 1), F32)],
        scratch_shapes=[pltpu.VMEM((t, 1), F32), pltpu.VMEM((t, 1), F32), pltpu.VMEM((t, LANES), F32)],
        compiler_params=pltpu.CompilerParams(dimension_semantics=("arbitrary",) * 3, vmem_limit_bytes=VMEM_LIMIT),
    )(q, k, v, sink)


def _attn_delta(name, o, do, lse, sink, *, heads, has_sink):
    t_rows = o.shape[0]
    t = _attn_tile(t_rows)
    nq = t_rows // t

    def body(o_ref, do_ref, lse_ref, sink_ref, dl_ref, ds_ref):
        i = pl.program_id(1)
        d = jnp.sum(o_ref[...].astype(F32) * do_ref[...].astype(F32), axis=1, keepdims=True)
        dl_ref[0] = d

        @pl.when(i == 0)
        def _():
            ds_ref[...] = jnp.zeros((1, 1, 1), F32)

        if has_sink:
            ds_ref[0] += -jnp.sum(jnp.exp(sink_ref[0] - lse_ref[0]) * d, axis=0, keepdims=True)

    return pl.pallas_call(
        body, name=name, grid=(heads, nq),
        in_specs=[pl.BlockSpec((t, LANES), lambda h, i: (i, h)),
                  pl.BlockSpec((t, LANES), lambda h, i: (i, h)),
                  pl.BlockSpec((1, t, 1), lambda h, i: (h, i, 0)),
                  pl.BlockSpec((1, 1, 1), lambda h, i: (h, 0, 0))],
        out_specs=[pl.BlockSpec((1, t, 1), lambda h, i: (h, i, 0)),
                   pl.BlockSpec((1, 1, 1), lambda h, i: (h, 0, 0))],
        out_shape=[_sds((heads, t_rows, 1), F32), _sds((heads, 1, 1), F32)],
        compiler_params=pltpu.CompilerParams(dimension_semantics=("arbitrary",) * 2, vmem_limit_bytes=VMEM_LIMIT),
    )(o, do, lse, sink)


def _attn_bwd(name, q, k, v, do, lse, delta, *, heads, group, window):
    t_rows = q.shape[0]
    t = _attn_tile(t_rows)
    nq = t_rows // t
    kv_heads = heads // group
    banded = window is not None
    ni = 2 if banded else nq

    def qidx(j, ii):
        return jnp.minimum(j + ii, nq - 1) if banded else jnp.maximum(ii, j)

    def body(q_ref, k_ref, v_ref, do_ref, lse_ref, dl_ref, dq_ref, dk_ref, dv_ref):
        j = pl.program_id(1)
        g = pl.program_id(2)
        ii = pl.program_id(3)

        @pl.when((j == 0) & (g == 0) & (ii == 0))
        def _():
            dq_ref[...] = jnp.zeros(dq_ref.shape, F32)

        @pl.when((g == 0) & (ii == 0))
        def _():
            dk_ref[...] = jnp.zeros((t, LANES), F32)
            dv_ref[...] = jnp.zeros((t, LANES), F32)

        def step(masked):
            qi = qidx(j, ii)
            qv = q_ref[...]
            kv = k_ref[...]
            dov = do_ref[...]
            s = _dot_nt(qv, kv)
            if masked:
                qpos = qi * t + lax.broadcasted_iota(jnp.int32, (t, t), 0)
                kpos = j * t + lax.broadcasted_iota(jnp.int32, (t, t), 1)
                valid = kpos <= qpos
                if banded:
                    valid = valid & (qpos - kpos < window)
                s = jnp.where(valid, s, NEG)
            p = jnp.exp(s - lse_ref[0])
            dv_ref[...] += _dot_tn(p.astype(BF16), dov)
            dp = _dot_nt(dov, v_ref[...])
            ds = (p * (dp - dl_ref[0])).astype(BF16)
            dk_ref[...] += _dot_tn(ds, qv)
            r0 = pl.multiple_of(qi * t, t)
            dq_ref[g, pl.ds(r0, t), :] += _dot(ds, kv)

        if banded:
            pl.when(j + ii <= nq - 1)(lambda: step(True))
        else:
            pl.when(ii > j)(lambda: step(False))
            pl.when(ii == j)(lambda: step(True))

    def qmap(kvh, j, g, ii):
        return (qidx(j, ii), kvh * group + g)

    def rowmap(kvh, j, g, ii):
        return (kvh * group + g, qidx(j, ii), 0)

    def kvmap(kvh, j, g, ii):
        return (j, kvh)

    return pl.pallas_call(
        body, name=name, grid=(kv_heads, nq, group, ni),
        in_specs=[pl.BlockSpec((t, LANES), qmap), pl.BlockSpec((t, LANES), kvmap), pl.BlockSpec((t, LANES), kvmap),
                  pl.BlockSpec((t, LANES), qmap), pl.BlockSpec((1, t, 1), rowmap), pl.BlockSpec((1, t, 1), rowmap)],
        out_specs=[pl.BlockSpec((group, t_rows, LANES), lambda kvh, j, g, ii: (kvh, 0, 0)),
                   pl.BlockSpec((t, LANES), kvmap), pl.BlockSpec((t, LANES), kvmap)],
        out_shape=[_sds((heads, t_rows, LANES), F32), _sds((t_rows, kv_heads * LANES), F32),
                   _sds((t_rows, kv_heads * LANES), F32)],
        compiler_params=pltpu.CompilerParams(dimension_semantics=("arbitrary",) * 4, vmem_limit_bytes=VMEM_LIMIT),
    )(q, k, v, do, lse, delta)


def _fwd_mix(x, ya, yb, gate, wba, wbb, wout, g2, g3, tm):
    t_rows = x.shape[0]

    def body(x_ref, ya_ref, yb_ref, gate_ref, wba_ref, wbb_ref, wout_ref, g2_ref, g3_ref,
             pa_ref, pb_ref, mixed_ref, o_ref, x1_ref, h2_ref):
        pa = _dot(ya_ref[...], wba_ref[...])
        pb = _dot(yb_ref[...], wbb_ref[...])
        pa_ref[...] = pa
        pb_ref[...] = pb
        mixed = (gate_ref[:, 0:D_MODEL] * pa + gate_ref[:, D_MODEL:2 * D_MODEL] * pb).astype(BF16)
        mixed_ref[...] = mixed
        o = _dot(mixed, wout_ref[...])
        o_ref[...] = o
        on, _ = _rms_stats(o)
        x1 = x_ref[...] + on * g2_ref[...]
        x1_ref[...] = x1
        x1n, _ = _rms_stats(x1)
        h2_ref[...] = (x1n * g3_ref[...]).astype(BF16)

    def o_(dt):
        return (_sds((t_rows, D_MODEL), dt), _row(tm, D_MODEL))

    ins = [(x, _row(tm, D_MODEL)), (ya, _row(tm, 1024)), (yb, _row(tm, 1024)), (gate, _row(tm, 2048)),
           (wba, _full(wba.shape)), (wbb, _full(wbb.shape)), (wout, _full(wout.shape)),
           (g2, _full(g2.shape)), (g3, _full(g3.shape))]
    return _rows_call("fwd_mix", body, t_rows, tm, ins, [o_(F32), o_(F32), o_(BF16), o_(F32), o_(F32), o_(BF16)])


CONV_CHUNK = 1408


def _fwd_up(h2, wup, convw8, convb, tm):
    t_rows = h2.shape[0]
    cdim = 2 * D_FF

    def body(h2_ref, wup_ref, cw_ref, cb_ref, up_ref, a_ref, carry):
        i = pl.program_id(0)

        @pl.when(i == 0)
        def _():
            carry[...] = jnp.zeros(carry.shape, F32)

        hb = h2_ref[...]

        def conv(c0):
            sl = slice(c0, c0 + CONV_CHUNK)
            up = _dot(hb, wup_ref[:, sl])
            up_ref[:, sl] = up
            xm1, xm2 = _conv_taps(up, carry[6:7, sl], carry[7:8, sl])
            u = cw_ref[0:1, sl] * xm2 + cw_ref[1:2, sl] * xm1 + cw_ref[2:3, sl] * up + cb_ref[:, sl]
            carry[:, sl] = up[tm - 8:tm, :]
            return u

        for c0 in range(0, D_FF, CONV_CHUNK):
            ug = conv(c0)
            uv = conv(D_FF + c0)
            gel, _ = _gelu_and_grad(ug)
            a_ref[:, c0:c0 + CONV_CHUNK] = (gel * uv).astype(BF16)

    ins = [(h2, _row(tm, D_MODEL)), (wup, _full(wup.shape)), (convw8, _full(convw8.shape)), (convb, _full(convb.shape))]
    outs = [(_sds((t_rows, cdim), F32), _row(tm, cdim)), (_sds((t_rows, D_FF), BF16), _row(tm, D_FF))]
    return _rows_call("fwd_up", body, t_rows, tm, ins, outs, scratch=[pltpu.VMEM((8, cdim), F32)])


def _fwd_out(a, wdown, x1, g4, p, wple, g5, wpg, tgt, tm):
    t_rows = a.shape[0]

    def body(a_ref, wdown_ref, x1_ref, g4_ref, p_ref, wple_ref, g5_ref, wpg_ref, tgt_ref,
             ff_ref, x2_ref, e_ref, n5_ref, sg_ref, dx3_ref, loss_ref):
        i = pl.program_id(0)
        ff = _dot(a_ref[...], wdown_ref[...])
        ff_ref[...] = ff
        ffn, _ = _rms_stats(ff)
        x2 = x1_ref[...] + ffn * g4_ref[...]
        x2_ref[...] = x2
        e = _dot(p_ref[...].astype(BF16), wple_ref[...])
        e_ref[...] = e
        x2n, _ = _rms_stats(x2)
        n5 = (x2n * g5_ref[...]).astype(BF16)
        n5_ref[...] = n5
        sg = _sigmoid(_dot(n5, wpg_ref[...]))
        sg_ref[...] = sg
        d = x2 + sg * e - tgt_ref[...]
        dx3_ref[...] = d * (1.0 / D_MODEL)

        @pl.when(i == 0)
        def _():
            loss_ref[...] = jnp.zeros((1, 1), F32)

        loss_ref[...] += 0.5 * jnp.sum(jnp.sum(d * d, axis=1, keepdims=True), axis=0, keepdims=True) * (1.0 / D_MODEL)

    def o_(dt):
        return (_sds((t_rows, D_MODEL), dt), _row(tm, D_MODEL))

    ins = [(a, _row(tm, D_FF)), (wdown, _full(wdown.shape)), (x1, _row(tm, D_MODEL)), (g4, _full(g4.shape)),
           (p, _row(tm, PLE_DIM)), (wple, _full(wple.shape)), (g5, _full(g5.shape)), (wpg, _full(wpg.shape)),
           (tgt, _row(tm, D_MODEL))]
    outs = [o_(F32), o_(F32), o_(F32), o_(BF16), o_(F32), o_(F32), (_sds((1, 1), F32), _full((1, 1)))]
    return _rows_call("fwd_out", body, t_rows, tm, ins, outs)


def _bwd_out(dx3, e, sg, x2, ff, g5, g4, wpg, wdown, up, convw8, convb, tm):
    t_rows = dx3.shape[0]
    cdim = 2 * D_FF
    hb = tm // 8

    def body(dx3_ref, e_ref, sg_ref, x2_ref, ff_ref, g5_ref, g4_ref, wpg_ref, wdown_ref, up_ref, halo_ref, cw_ref,
             cb_ref, dpre_ref, de_ref, dx2_ref, dff_ref, du_ref, dg5_ref, dg4_ref, dcb_ref, dcw_ref):
        i = pl.program_id(0)

        @pl.when(i == 0)
        def _():
            dg5_ref[...] = jnp.zeros(dg5_ref.shape, F32)
            dg4_ref[...] = jnp.zeros(dg4_ref.shape, F32)
            dcb_ref[...] = jnp.zeros(dcb_ref.shape, F32)
            dcw_ref[...] = jnp.zeros(dcw_ref.shape, F32)

        dx3 = dx3_ref[...]
        sg = sg_ref[...]
        dpre = (dx3 * e_ref[...] * sg * (1.0 - sg)).astype(BF16)
        dpre_ref[...] = dpre
        de_ref[...] = (dx3 * sg).astype(BF16)
        dn5 = _dot_nt(dpre, wpg_ref[...])
        x2n, r5 = _rms_stats(x2_ref[...])
        d2, dg5 = _rms_bwd(dn5, x2n, r5, g5_ref[...])
        dx2 = dx3 + d2
        dx2_ref[...] = dx2
        dg5_ref[...] += dg5
        ffn, r4 = _rms_stats(ff_ref[...])
        dff, dg4 = _rms_bwd(dx2, ffn, r4, g4_ref[...])
        dg4_ref[...] += dg4
        dffb = dff.astype(BF16)
        dff_ref[...] = dffb
        keep = jnp.where(i > 0, 1.0, 0.0)

        def conv(c0):
            sl = slice(c0, c0 + CONV_CHUNK)
            up = up_ref[:, sl]
            xm1, xm2 = _conv_taps(up, halo_ref[6:7, sl] * keep, halo_ref[7:8, sl] * keep)
            u = cw_ref[0:1, sl] * xm2 + cw_ref[1:2, sl] * xm1 + cw_ref[2:3, sl] * up + cb_ref[:, sl]
            return u, up, xm1, xm2

        def grads(c0, du, up, xm1, xm2):
            sl = slice(c0, c0 + CONV_CHUNK)
            du_ref[:, sl] = du
            dcb_ref[:, sl] += jnp.sum(du, axis=0, keepdims=True)
            dcw_ref[0:1, sl] += jnp.sum(du * xm2, axis=0, keepdims=True)
            dcw_ref[1:2, sl] += jnp.sum(du * xm1, axis=0, keepdims=True)
            dcw_ref[2:3, sl] += jnp.sum(du * up, axis=0, keepdims=True)

        for c0 in range(0, D_FF, CONV_CHUNK):
            da = _dot_nt(dffb, wdown_ref[c0:c0 + CONV_CHUNK, :])
            ug, *rg = conv(c0)
            uv, *rv = conv(D_FF + c0)
            gel, dgel = _gelu_and_grad(ug)
            grads(c0, da * uv * dgel, *rg)
            grads(D_FF + c0, da * gel, *rv)

    def o_(n, dt):
        return (_sds((t_rows, n), dt), _row(tm, n))

    def acc(r, n):
        return (_sds((r, n), F32), _full((r, n)))

    halo = pl.BlockSpec((8, cdim), lambda i: (jnp.maximum(i * hb - 1, 0), 0))
    ins = [(dx3, _row(tm, D_MODEL)), (e, _row(tm, D_MODEL)), (sg, _row(tm, D_MODEL)), (x2, _row(tm, D_MODEL)),
           (ff, _row(tm, D_MODEL)), (g5, _full(g5.shape)), (g4, _full(g4.shape)), (wpg, _full(wpg.shape)),
           (wdown, _full(wdown.shape)), (up, _row(tm, cdim)), (up, halo), (convw8, _full(convw8.shape)),
           (convb, _full(convb.shape))]
    outs = [o_(D_MODEL, BF16), o_(D_MODEL, BF16), o_(D_MODEL, F32), o_(D_MODEL, BF16), o_(cdim, F32),
            acc(1, D_MODEL), acc(1, D_MODEL), acc(1, cdim), acc(8, cdim)]
    return _rows_call("bwd_out", body, t_rows, tm, ins, outs)


def _bwd_mid(du, convw8, wup, dx2, x1, g3, o, g2, wout, gate, pa, pb, wba, wbb, tm):
    t_rows = du.shape[0]
    cdim = 2 * D_FF
    hb = tm // 8
    last_blk = t_rows // 8 - 1
    n_tiles = t_rows // tm

    def body(du_ref, halo_ref, cw_ref, wup_ref, dx2_ref, x1_ref, g3_ref, o_ref, g2_ref, wout_ref, gate_ref, pa_ref,
             pb_ref, wba_ref, wbb_ref,
             dup_ref, dx1_ref, do_ref, dpa_ref, dpb_ref, dgt_ref, dya_ref, dyb_ref, dg3_ref, dg2_ref, dbg_ref):
        i = pl.program_id(0)

        @pl.when(i == 0)
        def _():
            dg3_ref[...] = jnp.zeros(dg3_ref.shape, F32)
            dg2_ref[...] = jnp.zeros(dg2_ref.shape, F32)
            dbg_ref[...] = jnp.zeros(dbg_ref.shape, F32)

        keep = jnp.where(i < n_tiles - 1, 1.0, 0.0)
        dh2 = jnp.zeros((tm, D_MODEL), F32)
        for c0 in range(0, cdim, CONV_CHUNK):
            sl = slice(c0, c0 + CONV_CHUNK)
            du = du_ref[:, sl]
            xp1, xp2 = _conv_taps_next(du, halo_ref[0:1, sl] * keep, halo_ref[1:2, sl] * keep)
            dup = (cw_ref[2:3, sl] * du + cw_ref[1:2, sl] * xp1 + cw_ref[0:1, sl] * xp2).astype(BF16)
            dup_ref[:, sl] = dup
            dh2 = dh2 + _dot_nt(dup, wup_ref[:, sl])
        x1n, r3 = _rms_stats(x1_ref[...])
        d1, dg3 = _rms_bwd(dh2, x1n, r3, g3_ref[...])
        dx1 = dx2_ref[...] + d1
        dx1_ref[...] = dx1
        dg3_ref[...] += dg3
        on, r2 = _rms_stats(o_ref[...])
        do, dg2 = _rms_bwd(dx1, on, r2, g2_ref[...])
        dg2_ref[...] += dg2
        dob = do.astype(BF16)
        do_ref[...] = dob
        dmixed = _dot_nt(dob, wout_ref[...])
        ga = gate_ref[:, 0:D_MODEL]
        gb = gate_ref[:, D_MODEL:2 * D_MODEL]
        dpa = (dmixed * ga).astype(BF16)
        dpb = (dmixed * gb).astype(BF16)
        dpa_ref[...] = dpa
        dpb_ref[...] = dpb
        dga = dmixed * pa_ref[...] * ga * (1.0 - ga)
        dgb = dmixed * pb_ref[...] * gb * (1.0 - gb)
        dgt_ref[:, 0:D_MODEL] = dga.astype(BF16)
        dgt_ref[:, D_MODEL:2 * D_MODEL] = dgb.astype(BF16)
        dbg_ref[:, 0:D_MODEL] += jnp.sum(dga, axis=0, keepdims=True)
        dbg_ref[:, D_MODEL:2 * D_MODEL] += jnp.sum(dgb, axis=0, keepdims=True)
        dya_ref[...] = _dot_nt(dpa, wba_ref[...]).astype(BF16)
        dyb_ref[...] = _dot_nt(dpb, wbb_ref[...]).astype(BF16)

    def o_(n, dt):
        return (_sds((t_rows, n), dt), _row(tm, n))

    def acc(r, n):
        return (_sds((r, n), F32), _full((r, n)))

    halo = pl.BlockSpec((8, cdim), lambda i: (jnp.minimum((i + 1) * hb, last_blk), 0))
    ins = [(du, _row(tm, cdim)), (du, halo), (convw8, _full(convw8.shape)), (wup, _full(wup.shape)),
           (dx2, _row(tm, D_MODEL)), (x1, _row(tm, D_MODEL)), (g3, _full(g3.shape)), (o, _row(tm, D_MODEL)),
           (g2, _full(g2.shape)), (wout, _full(wout.shape)), (gate, _row(tm, 2048)), (pa, _row(tm, D_MODEL)),
           (pb, _row(tm, D_MODEL)), (wba, _full(wba.shape)), (wbb, _full(wbb.shape))]
    outs = [o_(cdim, BF16), o_(D_MODEL, F32), o_(D_MODEL, BF16), o_(D_MODEL, BF16), o_(D_MODEL, BF16),
            o_(2048, BF16), o_(1024, BF16), o_(1024, BF16), acc(1, D_MODEL), acc(1, D_MODEL), acc(1, 2048)]
    return _rows_call("bwd_mid", body, t_rows, tm, ins, outs)


def _bwd_in(dqs, dks, dvs, dqm, dkm, dvm, tabs, consts, cq, ckv, gq, gkv, wuq, wk, wv, dgates, win, x, g1, dx1, tm):
    t_rows = x.shape[0]

    def body(dqs_ref, dks_ref, dvs_ref, dqm_ref, dkm_ref, dvm_ref, ca, sa1, sa2, cb, sb1, sb2, c_ref, cq_ref,
             ckv_ref, gq_ref, gkv_ref, wuq_ref, wk_ref, wv_ref, dgt_ref, win_ref, x_ref, g1_ref, dx1_ref,
             dz_ref, dqb_ref, dx_ref, dgq_ref, dgkv_ref, dg1_ref):
        i = pl.program_id(0)

        @pl.when(i == 0)
        def _():
            dgq_ref[...] = jnp.zeros(dgq_ref.shape, F32)
            dgkv_ref[...] = jnp.zeros(dgkv_ref.shape, F32)
            dg1_ref[...] = jnp.zeros(dg1_ref.shape, F32)

        ta = (ca[...], sa1[...], sa2[...])
        tb = (cb[...], sb1[...], sb2[...])
        dqs = jnp.concatenate([dqs_ref[h] for h in range(HEADS)], axis=1)
        dz_ref[:, Z_QA:Z_KA] = _rope_t(dqs * SCALE_A, *ta, A_HEAD_DIM // 2).astype(BF16)
        dz_ref[:, Z_KA:Z_VA] = _rope_t(dks_ref[...], *ta, A_HEAD_DIM // 2).astype(BF16)
        dz_ref[:, Z_VA:Z_CQ] = dvs_ref[...].astype(BF16)
        dqm = jnp.concatenate([dqm_ref[h] for h in range(HEADS)], axis=1)
        dqb = _rope_t(dqm * SCALE_B, *tb, ROPE_DIM // 2).astype(BF16)
        dqb_ref[...] = dqb
        dcqn = _dot_nt(dqb, wuq_ref[...])
        cqn, rq = _rms_stats(cq_ref[...])
        dcq, dgq = _rms_bwd(dcqn, cqn, rq, gq_ref[...])
        dgq_ref[...] += dgq
        dz_ref[:, Z_CQ:Z_CKV] = dcq.astype(BF16)
        dkm = dkm_ref[...]
        dslot = dkm[:, 0:LANES]
        for h in range(1, HEADS):
            dslot = dslot + dkm[:, h * LANES:(h + 1) * LANES]
        dz_ref[:, Z_KR:Z_GATE] = _rope_t(dslot * c_ref[10:11, :], *tb, ROPE_DIM // 2).astype(BF16)
        dckvn = _dot_nt(dkm.astype(BF16), wk_ref[...]) + _dot_nt(dvm_ref[...].astype(BF16), wv_ref[...])
        ckvn, rkv = _rms_stats(ckv_ref[...])
        dckv, dgkv = _rms_bwd(dckvn, ckvn, rkv, gkv_ref[...])
        dgkv_ref[...] += dgkv
        dz_ref[:, Z_CKV:Z_KR] = dckv.astype(BF16)
        dz_ref[:, Z_GATE:ZW] = dgt_ref[...]
        dh1 = _dot_nt(dz_ref[...], win_ref[...])
        xn, r1 = _rms_stats(x_ref[...])
        d0, dg1 = _rms_bwd(dh1, xn, r1, g1_ref[...])
        dg1_ref[...] += dg1
        dx_ref[...] = dx1_ref[...] + d0

    def acc(n):
        return (_sds((1, n), F32), _full((1, n)))

    ins = [(dqs, _heads(tm, HEADS)), (dks, _row(tm, 256)), (dvs, _row(tm, 256)), (dqm, _heads(tm, HEADS)),
           (dkm, _row(tm, 1024)), (dvm, _row(tm, 1024))] + [(t, _row(tm, LANES)) for t in tabs] + [
           (consts, _full(consts.shape)), (cq, _row(tm, 256)), (ckv, _row(tm, 128)), (gq, _full(gq.shape)),
           (gkv, _full(gkv.shape)), (wuq, _full(wuq.shape)), (wk, _full(wk.shape)), (wv, _full(wv.shape)),
           (dgates, _row(tm, 2048)), (win, _full(win.shape)), (x, _row(tm, D_MODEL)), (g1, _full(g1.shape)),
           (dx1, _row(tm, D_MODEL))]
    outs = [(_sds((t_rows, ZW), BF16), _row(tm, ZW)), (_sds((t_rows, 1024), BF16), _row(tm, 1024)),
            (_sds((t_rows, D_MODEL), F32), _row(tm, D_MODEL)), acc(256), acc(128), acc(D_MODEL)]
    return _rows_call("bwd_in", body, t_rows, tm, ins, outs)


def _pick_cols(n):
    best = LANES
    for d in range(LANES, min(n, 1408) + 1, LANES):
        if n % d == 0:
            best = d
    return best


def _mm_tn(name, a, b):
    t_rows, m = a.shape
    n = b.shape[1]
    bk = min(512, t_rows)
    bm, bn = _pick_cols(m), _pick_cols(n)

    def body(a_ref, b_ref, o_ref):
        @pl.when(pl.program_id(2) == 0)
        def _():
            o_ref[...] = jnp.zeros((bm, bn), F32)

        o_ref[...] += _dot_tn(a_ref[...].astype(BF16), b_ref[...].astype(BF16))

    return pl.pallas_call(
        body, name=name, grid=(m // bm, n // bn, t_rows // bk),
        in_specs=[pl.BlockSpec((bk, bm), lambda i, j, k: (k, i)), pl.BlockSpec((bk, bn), lambda i, j, k: (k, j))],
        out_specs=pl.BlockSpec((bm, bn), lambda i, j, k: (i, j)),
        out_shape=_sds((m, n), F32),
        compiler_params=pltpu.CompilerParams(dimension_semantics=("arbitrary",) * 3, vmem_limit_bytes=VMEM_LIMIT),
    )(a, b)


PACK_ROWS = 512


def _pack_tile(rows):
    assert rows % PACK_ROWS == 0
    return PACK_ROWS


def _add_pair(a, b):
    _, rows, _ = a.shape
    t = _pack_tile(rows)

    def body(a_ref, b_ref, o_ref):
        o_ref[...] = (a_ref[...] + b_ref[...]).astype(BF16)

    spec = pl.BlockSpec((4, t, LANES), lambda i: (0, i, 0))
    return pl.pallas_call(body, name="rs_add_pair", grid=(rows // t,), in_specs=[spec, spec], out_specs=spec,
                          out_shape=_sds(a.shape, BF16))(a, b)


def _add_chips(parts):
    _, rows, _ = parts.shape
    t = _pack_tile(rows)

    def body(p_ref, o_ref):
        acc = p_ref[0].astype(F32)
        for j in range(1, 4):
            acc = acc + p_ref[j].astype(F32)
        o_ref[...] = acc

    return pl.pallas_call(body, name="rs_add_chips", grid=(rows // t,),
                          in_specs=[pl.BlockSpec((4, t, LANES), lambda i: (0, i, 0))],
                          out_specs=pl.BlockSpec((t, LANES), lambda i: (i, 0)),
                          out_shape=_sds((rows, LANES), F32))(parts)


def _add_devices(parts):
    n, rows, _ = parts.shape

    def body(p_ref, o_ref):
        acc = p_ref[0]
        for j in range(1, n):
            acc = acc + p_ref[j]
        o_ref[...] = acc

    return pl.pallas_call(body, name="small_add", grid=(1,),
                          in_specs=[pl.BlockSpec((n, rows, LANES), lambda i: (0, 0, 0))],
                          out_specs=pl.BlockSpec((rows, LANES), lambda i: (0, 0)),
                          out_shape=_sds((rows, LANES), F32))(parts)


def _adam_rows(k, n):
    target = max(8, (1 << 20) // (4 * n))
    if k <= target:
        return k
    best = None
    for d in range(8, target + 1, 8):
        if k % d == 0:
            best = d
    return best if best is not None else k


def _adamw(name, w, g, m, v):
    k, n = w.shape
    bk = _adam_rows(k, n)
    c1 = 1.0 - ADAM_B1 ** ADAM_STEP
    c2 = 1.0 - ADAM_B2 ** ADAM_STEP

    def body(w_ref, g_ref, m_ref, v_ref, d_ref, mo_ref, vo_ref):
        g_ = g_ref[...]
        m_ = ADAM_B1 * m_ref[...] + (1.0 - ADAM_B1) * g_
        v_ = ADAM_B2 * v_ref[...] + (1.0 - ADAM_B2) * (g_ * g_)
        mo_ref[...] = m_
        vo_ref[...] = v_
        d_ref[...] = -ADAM_LR * ((m_ / c1) / (jnp.sqrt(v_ / c2) + ADAM_EPS) + ADAM_WD * w_ref[...])

    spec = pl.BlockSpec((bk, n), lambda i: (i, 0))
    return pl.pallas_call(body, name=name, grid=(k // bk,), in_specs=[spec] * 4, out_specs=[spec] * 3,
                          out_shape=[_sds((k, n), F32)] * 3,
                          compiler_params=pltpu.CompilerParams(vmem_limit_bytes=VMEM_LIMIT))(w, g, m, v)


_HBM = pl.BlockSpec(memory_space=pltpu.HBM)


def _me():
    return lax.axis_index("x"), lax.axis_index("y"), lax.axis_index("c")


def _other_chips(x, y):
    return [(1 - x, y), (x, 1 - y), (1 - x, 1 - y)]


def _gather_weights(shard):
    def body(x_ref, out_ref, send_sems, recv_sems):
        x, y, c = _me()
        sibling = (x, y, 1 - c)
        chips = _other_chips(x, y)

        def slot(px, py, pc):
            return out_ref.at[2 * px + py, pc]

        def copy(k, src, dst, to):
            return pltpu.make_async_remote_copy(src_ref=src, dst_ref=dst, send_sem=send_sems.at[k],
                                                recv_sem=recv_sems.at[k], device_id=to, device_id_type=MESH)

        first = [copy(j, x_ref.at[c], slot(x, y, c), (*chip, c)) for j, chip in enumerate(chips)]
        for cp in first:
            cp.start()
        passed = [copy(3 + j, slot(*chip, c), slot(*chip, c), sibling) for j, chip in enumerate(chips)]
        for j, chip in enumerate(chips):
            copy(j, x_ref.at[c], slot(*chip, c), (*chip, c)).wait_recv()
            passed[j].start()
        for j, chip in enumerate(chips):
            copy(3 + j, slot(*chip, 1 - c), slot(*chip, 1 - c), sibling).wait_recv()
        for cp in first + passed:
            cp.wait_send()

    return pl.pallas_call(
        body, name="gather_weights", out_shape=_sds((4,) + shard.shape, shard.dtype), in_specs=[_HBM], out_specs=_HBM,
        scratch_shapes=[pltpu.SemaphoreType.DMA((6,)), pltpu.SemaphoreType.DMA((6,))],
    )(shard)


def _swap_sibling(name, v):
    def body(v_ref, out_ref, send_sem, recv_sem):
        x, y, c = _me()
        cp = pltpu.make_async_remote_copy(src_ref=v_ref, dst_ref=out_ref, send_sem=send_sem, recv_sem=recv_sem,
                                          device_id=(x, y, 1 - c), device_id_type=MESH)
        cp.start()
        cp.wait()

    return pl.pallas_call(
        body, name=name, out_shape=_sds(v.shape, v.dtype), in_specs=[_HBM], out_specs=_HBM,
        scratch_shapes=[pltpu.SemaphoreType.DMA, pltpu.SemaphoreType.DMA],
    )(v)


def _scatter_chips(s):
    def body(s_ref, out_ref, send_sems, recv_sems):
        x, y, c = _me()
        k = 2 * x + y
        chips = _other_chips(x, y)
        sends = [pltpu.make_async_remote_copy(src_ref=s_ref.at[2 * cx + cy], dst_ref=out_ref.at[k],
                                              send_sem=send_sems.at[j], recv_sem=recv_sems.at[j],
                                              device_id=(cx, cy, c), device_id_type=MESH)
                 for j, (cx, cy) in enumerate(chips)]
        for cp in sends:
            cp.start()
        for j, (cx, cy) in enumerate(chips):
            pltpu.make_async_remote_copy(src_ref=s_ref.at[k], dst_ref=out_ref.at[2 * cx + cy],
                                         send_sem=send_sems.at[j], recv_sem=recv_sems.at[j],
                                         device_id=(cx, cy, c), device_id_type=MESH).wait_recv()
        for cp in sends:
            cp.wait_send()

    return pl.pallas_call(
        body, name="scatter_chips", out_shape=_sds(s.shape, s.dtype), in_specs=[_HBM], out_specs=_HBM,
        scratch_shapes=[pltpu.SemaphoreType.DMA((3,)), pltpu.SemaphoreType.DMA((3,))],
    )(s)


def _gather_small(name, v):
    def body(v_ref, out_ref, send_sems, recv_sems, local_sem):
        x, y, c = _me()
        me = 4 * x + 2 * y + c
        mine = pltpu.make_async_copy(v_ref, out_ref.at[me], local_sem)
        mine.start()
        peers = []
        for f in range(1, 8):
            fx, fy, fc = (f >> 2) & 1, (f >> 1) & 1, f & 1
            px = 1 - x if fx else x
            py = 1 - y if fy else y
            pc = 1 - c if fc else c
            peers.append((f - 1, (px, py, pc)))
        sends = [pltpu.make_async_remote_copy(src_ref=v_ref, dst_ref=out_ref.at[me], send_sem=send_sems.at[k],
                                              recv_sem=recv_sems.at[k], device_id=peer, device_id_type=MESH)
                 for k, peer in peers]
        for cp in sends:
            cp.start()
        for k, (px, py, pc) in peers:
            pltpu.make_async_remote_copy(src_ref=v_ref, dst_ref=out_ref.at[4 * px + 2 * py + pc],
                                         send_sem=send_sems.at[k], recv_sem=recv_sems.at[k],
                                         device_id=(px, py, pc), device_id_type=MESH).wait_recv()
        for cp in sends:
            cp.wait_send()
        mine.wait()

    return pl.pallas_call(
        body, name=name, out_shape=_sds((8,) + v.shape, v.dtype), in_specs=[_HBM], out_specs=_HBM,
        scratch_shapes=[pltpu.SemaphoreType.DMA((7,)), pltpu.SemaphoreType.DMA((7,)), pltpu.SemaphoreType.DMA],
    )(v)


_BIG = (("w_in", (1024, 3232), 1), ("w_uq", (256, 768), 1), ("w_ukv", (128, 1024), 1), ("w_branch_a", (512, 1024), 1),
        ("w_branch_b", (512, 1024), 1), ("w_out", (1024, 1024), 0), ("w_up", (1024, 5632), 1),
        ("w_down", (2816, 1024), 0), ("w_ple_gate", (1024, 1024), 0), ("w_ple", (256, 1024), 1))


def _shard_shape(shape, axis):
    return (shape[0] // 4, shape[1]) if axis == 0 else (shape[0], shape[1] // 4)


def _half_rows(shape, axis):
    k, n = _shard_shape(shape, axis)
    return k * n // (2 * LANES)


_PACK_PAD = -sum(_half_rows(shape, axis) for _, shape, axis in _BIG) % PACK_ROWS


def _pack_shards(shards, dtype):
    parts = [shards[name].astype(dtype).reshape(2, _half_rows(shape, axis), LANES) for name, shape, axis in _BIG]
    return jnp.concatenate(parts + [jnp.zeros((2, _PACK_PAD, LANES), dtype)], axis=1)


def _unpack_gathered(g):
    out, off = {}, 0
    for name, shape, axis in _BIG:
        r = _half_rows(shape, axis)
        k, n = _shard_shape(shape, axis)
        w = g[:, :, off:off + r, :].reshape(4, k, n)
        out[name] = w.reshape(shape) if axis == 0 else w.transpose(1, 0, 2).reshape(shape)
        off += r
    return out


def _pack_grad_halves(grads, c):
    keep, send = [], []
    for name, shape, axis in _BIG:
        k, n = _shard_shape(shape, axis)
        r = _half_rows(shape, axis)
        g = grads[name]
        g4 = g.reshape(4, k, n) if axis == 0 else g.reshape(k, 4, n).transpose(1, 0, 2)
        g4 = g4.reshape(4, 2, r, LANES)
        keep.append(lax.dynamic_index_in_dim(g4, c, 1, keepdims=False))
        send.append(lax.dynamic_index_in_dim(g4, 1 - c, 1, keepdims=False))
    pad = [jnp.zeros((4, _PACK_PAD, LANES), F32)]
    return jnp.concatenate(keep + pad, axis=1), jnp.concatenate(send + pad, axis=1)


def _unpack_shard_grads(f):
    out, off = {}, 0
    for name, shape, axis in _BIG:
        r = _half_rows(shape, axis)
        out[name] = f[:, off:off + r, :].reshape(_shard_shape(shape, axis))
        off += r
    return out


def _pad_slots(w, heads, dim, axis):
    if axis == 1:
        k = w.shape[0]
        return jnp.pad(w.reshape(k, heads, dim), ((0, 0), (0, 0), (0, LANES - dim))).reshape(k, heads * LANES)
    n = w.shape[1]
    return jnp.pad(w.reshape(heads, dim, n), ((0, 0), (0, LANES - dim), (0, 0))).reshape(heads * LANES, n)


def _unpad_slots(w, heads, dim, axis):
    if axis == 1:
        k = w.shape[0]
        return w.reshape(k, heads, LANES)[:, :, :dim].reshape(k, heads * dim)
    n = w.shape[1]
    return w.reshape(heads, LANES, n)[:, :dim, :].reshape(heads * dim, n)


def _pad_w_in(w):
    kr = jnp.pad(w[:, 1152:1184], ((0, 0), (NOPE_DIM, LANES - NOPE_DIM - ROPE_DIM)))
    return jnp.concatenate([_pad_slots(w[:, 0:512], HEADS, A_HEAD_DIM, 1),
                            _pad_slots(w[:, 512:640], A_KV_HEADS, A_HEAD_DIM, 1),
                            _pad_slots(w[:, 640:768], A_KV_HEADS, A_HEAD_DIM, 1),
                            w[:, 768:1024], w[:, 1024:1152], kr, w[:, 1184:3232]], axis=1)


def _unpad_w_in(w):
    return jnp.concatenate([_unpad_slots(w[:, Z_QA:Z_KA], HEADS, A_HEAD_DIM, 1),
                            _unpad_slots(w[:, Z_KA:Z_VA], A_KV_HEADS, A_HEAD_DIM, 1),
                            _unpad_slots(w[:, Z_VA:Z_CQ], A_KV_HEADS, A_HEAD_DIM, 1),
                            w[:, Z_CQ:Z_CKV], w[:, Z_CKV:Z_KR],
                            w[:, Z_KR + NOPE_DIM:Z_KR + NOPE_DIM + ROPE_DIM], w[:, Z_GATE:ZW]], axis=1)


_SMALL = (("attn_pre_norm", 1024), ("attn_post_norm", 1024), ("b_gate", 2048), ("sinks", 8), ("q_a_norm", 256),
          ("kv_a_norm", 128), ("mlp_pre_norm", 1024), ("mlp_post_norm", 1024), ("conv_b", 5632), ("ple_norm", 1024),
          ("conv_w", 3 * 5632))


def _small_rows(n):
    return -(-n // LANES)


def _pack_small(vals):
    parts = []
    for name, n in _SMALL:
        r = _small_rows(n)
        parts.append(jnp.pad(vals[name].reshape(-1), (0, r * LANES - n)).reshape(r, LANES))
    rows = sum(_small_rows(n) for _, n in _SMALL)
    pad = -rows % 8
    if pad:
        parts.append(jnp.zeros((pad, LANES), F32))
    return jnp.concatenate(parts, axis=0)


def _unpack_small(buf):
    out, off = {}, 0
    for name, n in _SMALL:
        r = _small_rows(n)
        out[name] = buf[off:off + r].reshape(-1)[:n]
        off += r
    return out


def kernel(x, p, positions, attn_pre_norm, attn_post_norm, w_in, b_gate, sinks, q_a_norm, w_uq, kv_a_norm, w_ukv, w_branch_a, w_branch_b, w_out, mlp_pre_norm, mlp_post_norm, w_up, conv_w, conv_b, w_down, ple_norm, w_ple_gate, w_ple, loss_target, m_attn_pre_norm, m_attn_post_norm, m_w_in, m_b_gate, m_sinks, m_q_a_norm, m_w_uq, m_kv_a_norm, m_w_ukv, m_w_branch_a, m_w_branch_b, m_w_out, m_mlp_pre_norm, m_mlp_post_norm, m_w_up, m_conv_w, m_conv_b, m_w_down, m_ple_norm, m_w_ple_gate, m_w_ple, v_attn_pre_norm, v_attn_post_norm, v_w_in, v_b_gate, v_sinks, v_q_a_norm, v_w_uq, v_kv_a_norm, v_w_ukv, v_w_branch_a, v_w_branch_b, v_w_out, v_mlp_pre_norm, v_mlp_post_norm, v_w_up, v_conv_w, v_conv_b, v_w_down, v_ple_norm, v_w_ple_gate, v_w_ple):
    names = ["attn_pre_norm", "attn_post_norm", "w_in", "b_gate", "sinks", "q_a_norm", "w_uq", "kv_a_norm", "w_ukv",
             "w_branch_a", "w_branch_b", "w_out", "mlp_pre_norm", "mlp_post_norm", "w_up", "conv_w", "conv_b",
             "w_down", "ple_norm", "w_ple_gate", "w_ple"]
    wts = dict(zip(names, [attn_pre_norm, attn_post_norm, w_in, b_gate, sinks, q_a_norm, w_uq, kv_a_norm, w_ukv,
                           w_branch_a, w_branch_b, w_out, mlp_pre_norm, mlp_post_norm, w_up, conv_w, conv_b, w_down,
                           ple_norm, w_ple_gate, w_ple]))
    moms = dict(zip(names, [m_attn_pre_norm, m_attn_post_norm, m_w_in, m_b_gate, m_sinks, m_q_a_norm, m_w_uq,
                            m_kv_a_norm, m_w_ukv, m_w_branch_a, m_w_branch_b, m_w_out, m_mlp_pre_norm,
                            m_mlp_post_norm, m_w_up, m_conv_w, m_conv_b, m_w_down, m_ple_norm, m_w_ple_gate, m_w_ple]))
    vars_ = dict(zip(names, [v_attn_pre_norm, v_attn_post_norm, v_w_in, v_b_gate, v_sinks, v_q_a_norm, v_w_uq,
                             v_kv_a_norm, v_w_ukv, v_w_branch_a, v_w_branch_b, v_w_out, v_mlp_pre_norm,
                             v_mlp_post_norm, v_w_up, v_conv_w, v_conv_b, v_w_down, v_ple_norm, v_w_ple_gate, v_w_ple]))
    w2 = {n: a.reshape(a.shape[-2:]) for n, a in wts.items()}
    m2 = {n: a.reshape(a.shape[-2:]) for n, a in moms.items()}
    v2 = {n: a.reshape(a.shape[-2:]) for n, a in vars_.items()}

    t_rows = x.shape[-2]
    tm = min(256, t_rows)
    xc, yc, cc = lax.axis_index("x"), lax.axis_index("y"), lax.axis_index("c")
    chip = 2 * xc + yc

    x2d = x.reshape(t_rows, D_MODEL)
    p2d = p.reshape(t_rows, PLE_DIM)
    tgt = loss_target.reshape(t_rows, D_MODEL)
    pos_f = positions.reshape(t_rows, 1).astype(F32)

    my_shard = _pack_shards(w2, BF16)
    gathered = lax.dynamic_update_slice(_gather_weights(my_shard), my_shard[None], (chip, 0, 0, 0))
    full = _unpack_gathered(gathered)
    cw_rows = 3 * 1408 // LANES
    cw_all = _gather_small("gather_conv_w", jnp.pad(w2["conv_w"].reshape(cw_rows, LANES), ((0, 40 - cw_rows), (0, 0))))
    conv_full = cw_all[0::2, :cw_rows].reshape(4, 3, 1408).transpose(1, 0, 2).reshape(3, 2 * D_FF)
    convw8 = jnp.pad(conv_full, ((0, 5), (0, 0)))

    win = _pad_w_in(full["w_in"])
    wuq = _pad_slots(full["w_uq"], HEADS, NOPE_DIM + ROPE_DIM, 1)
    ukv = full["w_ukv"].reshape(KV_LORA, HEADS, NOPE_DIM + V_DIM)
    wk = _pad_slots(ukv[:, :, :NOPE_DIM].reshape(KV_LORA, HEADS * NOPE_DIM), HEADS, NOPE_DIM, 1)
    wv = _pad_slots(ukv[:, :, NOPE_DIM:].reshape(KV_LORA, HEADS * V_DIM), HEADS, V_DIM, 1)
    wba = _pad_slots(full["w_branch_a"], HEADS, A_HEAD_DIM, 0)
    wbb = _pad_slots(full["w_branch_b"], HEADS, V_DIM, 0)
    wout, wup, wdown, wpg, wple = full["w_out"], full["w_up"], full["w_down"], full["w_ple_gate"], full["w_ple"]
    g1, g2, g3, g4, g5 = (w2["attn_pre_norm"], w2["attn_post_norm"], w2["mlp_pre_norm"], w2["mlp_post_norm"],
                          w2["ple_norm"])
    gq, gkv, bg, convb = w2["q_a_norm"], w2["kv_a_norm"], w2["b_gate"], w2["conv_b"]
    sink3 = w2["sinks"].reshape(HEADS, 1, 1)

    consts = _rope_consts()
    tabs = _rope_tables(pos_f, consts, tm)
    h1, qs, ks, vs, cq, cqn, ckv, ckvn, qm, km, vm, gate = _fwd_in(x2d, g1, win, bg, gq, gkv, wuq, wk, wv, tabs, tm)
    swa = dict(heads=HEADS, group=HEADS // A_KV_HEADS, window=SWA_WINDOW)
    mla = dict(heads=HEADS, group=1, window=None)
    ya, lse_a = _attn_fwd("swa_fwd", qs, ks, vs, sink3, has_sink=True, **swa)
    yb, lse_b = _attn_fwd("mla_fwd", qm, km, vm, sink3, has_sink=False, **mla)
    pa, pb, mixed, o, x1, h2 = _fwd_mix(x2d, ya, yb, gate, wba, wbb, wout, g2, g3, tm)
    up, a = _fwd_up(h2, wup, convw8, convb, tm)
    ff, x2, e, n5, sg, dx3, loss_part = _fwd_out(a, wdown, x1, g4, p2d, wple, g5, wpg, tgt, tm)

    dpre, de, dx2, dff, du, dg5, dg4, dconvb, dconvw8 = _bwd_out(dx3, e, sg, x2, ff, g5, g4, wpg, wdown, up, convw8,
                                                                 convb, tm)
    dup, dx1, do, dpa, dpb, dgates, dya, dyb, dg3, dg2, dbg = _bwd_mid(du, convw8, wup, dx2, x1, g3, o, g2, wout, gate,
                                                                       pa, pb, wba, wbb, tm)
    dl_a, dsink = _attn_delta("swa_delta", ya, dya, lse_a, sink3, heads=HEADS, has_sink=True)
    dl_b, _ = _attn_delta("mla_delta", yb, dyb, lse_b, sink3, heads=HEADS, has_sink=False)
    dqs, dks, dvs = _attn_bwd("swa_bwd", qs, ks, vs, dya, lse_a, dl_a, **swa)
    dqm, dkm, dvm = _attn_bwd("mla_bwd", qm, km, vm, dyb, lse_b, dl_b, **mla)
    dz, dqb, dx, dgq, dgkv, dg1 = _bwd_in(dqs, dks, dvs, dqm, dkm, dvm, tabs, consts, cq, ckv, gq, gkv, wuq, wk, wv,
                                           dgates, win, x2d, g1, dx1, tm)

    dwk = _unpad_slots(_mm_tn("dw_k", ckvn, dkm), HEADS, NOPE_DIM, 1).reshape(KV_LORA, HEADS, NOPE_DIM)
    dwv = _unpad_slots(_mm_tn("dw_v", ckvn, dvm), HEADS, V_DIM, 1).reshape(KV_LORA, HEADS, V_DIM)
    grads = {
        "w_in": _unpad_w_in(_mm_tn("dw_in", h1, dz)),
        "w_uq": _unpad_slots(_mm_tn("dw_uq", cqn, dqb), HEADS, NOPE_DIM + ROPE_DIM, 1),
        "w_ukv": jnp.concatenate([dwk, dwv], axis=2).reshape(KV_LORA, HEADS * (NOPE_DIM + V_DIM)),
        "w_branch_a": _unpad_slots(_mm_tn("dw_branch_a", ya, dpa), HEADS, A_HEAD_DIM, 0),
        "w_branch_b": _unpad_slots(_mm_tn("dw_branch_b", yb, dpb), HEADS, V_DIM, 0),
        "w_out": _mm_tn("dw_out", mixed, do),
        "w_up": _mm_tn("dw_up", h2, dup),
        "w_down": _mm_tn("dw_down", a, dff),
        "w_ple_gate": _mm_tn("dw_ple_gate", n5, dpre),
        "w_ple": _mm_tn("dw_ple", p2d, de),
    }

    keep, send = _pack_grad_halves(grads, cc)
    pair = _add_pair(keep, _swap_sibling("swap_grad_halves", send))
    own = lax.dynamic_index_in_dim(pair, chip, 0, keepdims=True)
    reduced = _add_chips(lax.dynamic_update_slice(_scatter_chips(pair), own, (chip, 0, 0)))
    other = _swap_sibling("swap_reduced_halves", reduced)
    both = jnp.stack([jnp.where(cc == 0, reduced, other), jnp.where(cc == 0, other, reduced)])
    shard_grads = _unpack_shard_grads(both)

    small = {"attn_pre_norm": dg1, "attn_post_norm": dg2, "b_gate": dbg, "sinks": dsink, "q_a_norm": dgq,
             "kv_a_norm": dgkv, "mlp_pre_norm": dg3, "mlp_post_norm": dg4, "conv_b": dconvb, "ple_norm": dg5,
             "conv_w": dconvw8[0:3]}
    small_sum = _unpack_small(_add_devices(_gather_small("gather_small_grads", _pack_small(small))))
    for n in names:
        if n in small_sum and n != "conv_w":
            shard_grads[n] = small_sum[n].reshape(w2[n].shape)
    shard_grads["conv_w"] = lax.dynamic_index_in_dim(small_sum["conv_w"].reshape(3, 4, 1408), chip, 1, keepdims=False)

    loss = lax.psum(loss_part[0, 0], MESH_AXES)

    g_out, d_out, m_out, v_out = [], [], [], []
    for n in names:
        g = shard_grads[n]
        d, mn, vn = _adamw("adamw_" + n, w2[n], g, m2[n], v2[n])
        shp = wts[n].shape
        g_out.append(g.reshape(shp))
        d_out.append(d.reshape(shp))
        m_out.append(mn.reshape(shp))
        v_out.append(vn.reshape(shp))
    return (loss, dx.reshape(x.shape), *g_out, *d_out, *m_out, *v_out)
```

```python
import functools
import math

import numpy as np
import jax
import jax.numpy as jnp
from jax import lax
from jax.experimental import pallas as pl
from jax.experimental.pallas import tpu as pltpu

F32 = jnp.float32
BF16 = jnp.bfloat16

D_MODEL = 1024
D_FF = 2816
PLE_DIM = 256
ROPE_THETA = 10000.0
RMS_EPS = 1e-6
SWA_WINDOW = 128
HEADS = 8
A_KV_HEADS = 2
A_HEAD_DIM = 64
Q_LORA = 256
KV_LORA = 128
NOPE_DIM = 64
ROPE_DIM = 32
V_DIM = 64
LANES = 128
ZW = 4096
NEG = -1e30
SCALE_A = A_HEAD_DIM ** -0.5
SCALE_B = (NOPE_DIM + ROPE_DIM) ** -0.5

ADAM_LR = 0.001
ADAM_B1 = 0.9
ADAM_B2 = 0.999
ADAM_EPS = 1e-08
ADAM_WD = 0.01
ADAM_STEP = 10

VMEM_LIMIT = 60 * 1024 * 1024
MESH_AXES = ("x", "y", "c")
MESH = pl.DeviceIdType.MESH

Z_QA, Z_KA, Z_VA, Z_CQ, Z_CKV, Z_KR, Z_GATE = 0, 1024, 1280, 1536, 1792, 1920, 2048


def _dot(a, b):
    return jnp.dot(a, b, preferred_element_type=F32)


def _dot_nt(a, b):
    return lax.dot_general(a, b, (((1,), (1,)), ((), ())), preferred_element_type=F32)


def _dot_tn(a, b):
    return lax.dot_general(a, b, (((0,), (0,)), ((), ())), preferred_element_type=F32)


def _rms_stats(x):
    r = lax.rsqrt(jnp.mean(x * x, axis=-1, keepdims=True) + RMS_EPS)
    return x * r, r


def _rms_bwd(dy, xn, r, g):
    dxn = dy * g
    dx = r * (dxn - xn * jnp.mean(dxn * xn, axis=-1, keepdims=True))
    dg = jnp.sum(dy * xn, axis=0, keepdims=True)
    return dx, dg


def _tile_lanes(t, n):
    return t if n == 1 else jnp.concatenate([t] * n, axis=1)


def _rope(x, c, s1, s2, half):
    w = x.shape[1]
    n = w // LANES
    return (x * _tile_lanes(c, n) + pltpu.roll(x, w - half, 1) * _tile_lanes(s1, n)
            + pltpu.roll(x, half, 1) * _tile_lanes(s2, n))


def _rope_t(dy, c, s1, s2, half):
    w = dy.shape[1]
    n = w // LANES
    return (dy * _tile_lanes(c, n) + pltpu.roll(dy * _tile_lanes(s1, n), half, 1)
            + pltpu.roll(dy * _tile_lanes(s2, n), w - half, 1))


def _sigmoid(x):
    return 1.0 / (1.0 + jnp.exp(-x))


_GELU_C = math.sqrt(2.0 / math.pi)


def _gelu_and_grad(x):
    x2 = x * x
    th = jnp.tanh(_GELU_C * (x + 0.044715 * x * x2))
    gel = 0.5 * x * (1.0 + th)
    dgel = 0.5 * (1.0 + th) + 0.5 * x * (1.0 - th * th) * (_GELU_C * (1.0 + 3.0 * 0.044715 * x2))
    return gel, dgel


def _conv_taps(up, h6, h7):
    rows = lax.broadcasted_iota(jnp.int32, up.shape, 0)
    r1 = pltpu.roll(up, 1, 0)
    r2 = pltpu.roll(up, 2, 0)
    xm1 = jnp.where(rows == 0, h7, r1)
    xm2 = jnp.where(rows == 0, h6, jnp.where(rows == 1, h7, r2))
    return xm1, xm2


def _conv_taps_next(du, n0, n1):
    tm = du.shape[0]
    rows = lax.broadcasted_iota(jnp.int32, du.shape, 0)
    r1 = pltpu.roll(du, tm - 1, 0)
    r2 = pltpu.roll(du, tm - 2, 0)
    xp1 = jnp.where(rows == tm - 1, n0, r1)
    xp2 = jnp.where(rows == tm - 2, n0, jnp.where(rows == tm - 1, n1, r2))
    return xp1, xp2


def _row(tm, n):
    return pl.BlockSpec((tm, n), lambda i: (i, 0))


def _full(shape):
    nd = len(shape)
    return pl.BlockSpec(tuple(shape), lambda i: (0,) * nd)


def _heads(tm, h):
    return pl.BlockSpec((h, tm, LANES), lambda i: (0, i, 0))


def _rows_call(name, body, t_rows, tm, ins, outs, scratch=()):
    return pl.pallas_call(
        body, name=name, grid=(t_rows // tm,),
        in_specs=[s for _, s in ins],
        out_specs=[s for _, s in outs],
        out_shape=[s for s, _ in outs],
        scratch_shapes=list(scratch),
        compiler_params=pltpu.CompilerParams(dimension_semantics=("arbitrary",), vmem_limit_bytes=VMEM_LIMIT),
    )(*[a for a, _ in ins])


def _sds(shape, dtype):
    return jax.ShapeDtypeStruct(tuple(shape), dtype)


def _rope_consts():
    c = np.zeros((16, LANES), np.float32)
    lane = np.arange(LANES)
    inv_a = (ROPE_THETA ** (-(np.arange(0, A_HEAD_DIM, 2, dtype=np.float32) / A_HEAD_DIM))).astype(np.float32)
    in_a = lane < A_HEAD_DIM
    c[0, in_a] = inv_a[lane[in_a] % (A_HEAD_DIM // 2)]
    c[1, in_a] = 1.0
    c[2, lane < A_HEAD_DIM // 2] = -1.0
    c[3, (lane >= A_HEAD_DIM // 2) & in_a] = 1.0
    inv_b = (ROPE_THETA ** (-(np.arange(0, ROPE_DIM, 2, dtype=np.float32) / ROPE_DIM))).astype(np.float32)
    pe = (lane >= NOPE_DIM) & (lane < NOPE_DIM + ROPE_DIM)
    c[5, pe] = inv_b[(lane[pe] - NOPE_DIM) % (ROPE_DIM // 2)]
    c[6, pe] = 1.0
    c[7, (lane >= NOPE_DIM) & (lane < NOPE_DIM + ROPE_DIM // 2)] = -1.0
    c[8, (lane >= NOPE_DIM + ROPE_DIM // 2) & (lane < NOPE_DIM + ROPE_DIM)] = 1.0
    c[9, lane < NOPE_DIM] = 1.0
    c[10, pe] = 1.0
    return jnp.asarray(c)


def _rope_tables(pos_f, consts, tm):
    t_rows = pos_f.shape[0]

    def body(pos_ref, c_ref, ca, sa1, sa2, cb, sb1, sb2):
        pos = pos_ref[...]
        ang = pos * c_ref[0:1, :]
        cs, sn = jnp.cos(ang), jnp.sin(ang)
        ca[...] = cs * c_ref[1:2, :]
        sa1[...] = sn * c_ref[2:3, :]
        sa2[...] = sn * c_ref[3:4, :]
        ang = pos * c_ref[5:6, :]
        cs, sn = jnp.cos(ang), jnp.sin(ang)
        cb[...] = cs * c_ref[6:7, :] + c_ref[9:10, :]
        sb1[...] = sn * c_ref[7:8, :]
        sb2[...] = sn * c_ref[8:9, :]

    tab = (_sds((t_rows, LANES), F32), _row(tm, LANES))
    return _rows_call("rope_tables", body, t_rows, tm,
                      [(pos_f, _row(tm, 1)), (consts, _full(consts.shape))], [tab] * 6)


def _fwd_in(x, g1, win, bg, gq, gkv, wuq, wk, wv, tabs, tm):
    t_rows = x.shape[0]

    def body(x_ref, g1_ref, win_ref, bg_ref, gq_ref, gkv_ref, wuq_ref, wk_ref, wv_ref,
             ca, sa1, sa2, cb, sb1, sb2,
             h1_ref, qs_ref, ks_ref, vs_ref, cq_ref, cqn_ref, ckv_ref, ckvn_ref, qm_ref, km_ref, vm_ref, gate_ref):
        xn, _ = _rms_stats(x_ref[...])
        hb = (xn * g1_ref[...]).astype(BF16)
        h1_ref[...] = hb
        ta = (ca[...], sa1[...], sa2[...])
        tb = (cb[...], sb1[...], sb2[...])
        qs_ref[...] = (_rope(_dot(hb, win_ref[:, Z_QA:Z_KA]), *ta, A_HEAD_DIM // 2) * SCALE_A).astype(BF16)
        ks_ref[...] = _rope(_dot(hb, win_ref[:, Z_KA:Z_VA]), *ta, A_HEAD_DIM // 2).astype(BF16)
        vs_ref[...] = _dot(hb, win_ref[:, Z_VA:Z_CQ]).astype(BF16)
        cq = _dot(hb, win_ref[:, Z_CQ:Z_CKV])
        cq_ref[...] = cq
        cqn, _ = _rms_stats(cq)
        cqb = (cqn * gq_ref[...]).astype(BF16)
        cqn_ref[...] = cqb
        qm_ref[...] = (_rope(_dot(cqb, wuq_ref[...]), *tb, ROPE_DIM // 2) * SCALE_B).astype(BF16)
        ckv = _dot(hb, win_ref[:, Z_CKV:Z_KR])
        ckv_ref[...] = ckv
        ckvn, _ = _rms_stats(ckv)
        ckvb = (ckvn * gkv_ref[...]).astype(BF16)
        ckvn_ref[...] = ckvb
        kpe = _rope(_dot(hb, win_ref[:, Z_KR:Z_GATE]), *tb, ROPE_DIM // 2)
        km_ref[...] = (_dot(ckvb, wk_ref[...]) + _tile_lanes(kpe, HEADS)).astype(BF16)
        vm_ref[...] = _dot(ckvb, wv_ref[...]).astype(BF16)
        gate_ref[...] = _sigmoid(_dot(hb, win_ref[:, Z_GATE:ZW]) + bg_ref[...])

    def o(n, dt):
        return (_sds((t_rows, n), dt), _row(tm, n))

    ins = [(x, _row(tm, D_MODEL)), (g1, _full(g1.shape)), (win, _full(win.shape)), (bg, _full(bg.shape)),
           (gq, _full(gq.shape)), (gkv, _full(gkv.shape)), (wuq, _full(wuq.shape)), (wk, _full(wk.shape)),
           (wv, _full(wv.shape))] + [(t, _row(tm, LANES)) for t in tabs]
    outs = [o(1024, BF16), o(1024, BF16), o(256, BF16), o(256, BF16), o(256, F32), o(256, BF16), o(128, F32),
            o(128, BF16), o(1024, BF16), o(1024, BF16), o(1024, BF16), o(2048, F32)]
    return _rows_call("fwd_in", body, t_rows, tm, ins, outs)


def _attn_tile(t_rows):
    return min(512, t_rows)


MLA_HEADS_PER_STEP = 2


def _causal_pairs(nq, by_kv):
    if by_kv:
        pairs = [(i, j) for j in range(nq) for i in range(j, nq)]
    else:
        pairs = [(i, j) for i in range(nq) for j in range(i + 1)]
    return (jnp.asarray([p[0] for p in pairs], jnp.int32), jnp.asarray([p[1] for p in pairs], jnp.int32))


def _mla_fwd(q, k, v):
    t_rows = q.shape[0]
    t = _attn_tile(t_rows)
    hp = MLA_HEADS_PER_STEP
    w = hp * LANES
    ii, jj = _causal_pairs(t_rows // t, by_kv=False)

    def body(i_ref, j_ref, q_ref, k_ref, v_ref, o_ref, lse_ref, m_s, l_s, acc_s):
        i = i_ref[pl.program_id(1)]
        j = j_ref[pl.program_id(1)]

        @pl.when(j == 0)
        def _():
            m_s[...] = jnp.full(m_s.shape, NEG, F32)
            l_s[...] = jnp.zeros(l_s.shape, F32)
            acc_s[...] = jnp.zeros(acc_s.shape, F32)

        def step(diagonal):
            for hh in range(hp):
                sl = slice(hh * LANES, (hh + 1) * LANES)
                s = _dot_nt(q_ref[:, sl], k_ref[:, sl])
                if diagonal:
                    valid = (lax.broadcasted_iota(jnp.int32, (t, t), 1) <= lax.broadcasted_iota(jnp.int32, (t, t), 0))
                    s = jnp.where(valid, s, NEG)
                m_prev = m_s[hh]
                m_new = jnp.maximum(m_prev, jnp.max(s, axis=1, keepdims=True))
                p = jnp.exp(s - m_new)
                alpha = jnp.exp(m_prev - m_new)
                l_new = alpha * l_s[hh] + jnp.sum(p, axis=1, keepdims=True)
                acc = alpha * acc_s[hh] + _dot(p.astype(BF16), v_ref[:, sl])
                if diagonal:
                    o_ref[:, sl] = (acc / l_new).astype(o_ref.dtype)
                    lse_ref[hh] = m_new + jnp.log(l_new)
                else:
                    m_s[hh] = m_new
                    l_s[hh] = l_new
                    acc_s[hh] = acc

        pl.when(j < i)(lambda: step(False))
        pl.when(j == i)(lambda: step(True))

    grid_spec = pltpu.PrefetchScalarGridSpec(
        num_scalar_prefetch=2, grid=(HEADS // hp, ii.shape[0]),
        in_specs=[pl.BlockSpec((t, w), lambda hb, s, ir, jr: (ir[s], hb)),
                  pl.BlockSpec((t, w), lambda hb, s, ir, jr: (jr[s], hb)),
                  pl.BlockSpec((t, w), lambda hb, s, ir, jr: (jr[s], hb))],
        out_specs=[pl.BlockSpec((t, w), lambda hb, s, ir, jr: (ir[s], hb)),
                   pl.BlockSpec((hp, t, 1), lambda hb, s, ir, jr: (hb, ir[s], 0))],
        scratch_shapes=[pltpu.VMEM((hp, t, 1), F32), pltpu.VMEM((hp, t, 1), F32), pltpu.VMEM((hp, t, LANES), F32)])
    return pl.pallas_call(
        body, name="mla_fwd", grid_spec=grid_spec,
        out_shape=[_sds((t_rows, HEADS * LANES), BF16), _sds((HEADS, t_rows, 1), F32)],
        compiler_params=pltpu.CompilerParams(dimension_semantics=("arbitrary",) * 2, vmem_limit_bytes=VMEM_LIMIT),
    )(ii, jj, q, k, v)


def _mla_delta(o, do):
    t_rows = o.shape[0]
    t = _attn_tile(t_rows)

    def body(o_ref, do_ref, dl_ref):
        dl_ref[0] = jnp.sum(o_ref[...].astype(F32) * do_ref[...].astype(F32), axis=1, keepdims=True)

    return pl.pallas_call(
        body, name="mla_delta", grid=(HEADS, t_rows // t),
        in_specs=[pl.BlockSpec((t, LANES), lambda h, i: (i, h)), pl.BlockSpec((t, LANES), lambda h, i: (i, h))],
        out_specs=pl.BlockSpec((1, t, 1), lambda h, i: (h, i, 0)),
        out_shape=_sds((HEADS, t_rows, 1), F32),
    )(o, do)


def _mla_bwd(q, k, v, do, lse, delta):
    t_rows = q.shape[0]
    t = _attn_tile(t_rows)
    hp = MLA_HEADS_PER_STEP
    w = hp * LANES
    ii, jj = _causal_pairs(t_rows // t, by_kv=True)

    def body(i_ref, j_ref, q_ref, k_ref, v_ref, do_ref, lse_ref, dl_ref, dq_ref, dk_ref, dv_ref):
        i = i_ref[pl.program_id(1)]
        j = j_ref[pl.program_id(1)]

        @pl.when(pl.program_id(1) == 0)
        def _():
            dq_ref[...] = jnp.zeros(dq_ref.shape, F32)

        def step(diagonal):
            r0 = pl.multiple_of(i * t, t)
            for hh in range(hp):
                sl = slice(hh * LANES, (hh + 1) * LANES)
                qv = q_ref[:, sl]
                kv = k_ref[:, sl]
                dov = do_ref[:, sl]
                s = _dot_nt(qv, kv)
                if diagonal:
                    valid = (lax.broadcasted_iota(jnp.int32, (t, t), 1) <= lax.broadcasted_iota(jnp.int32, (t, t), 0))
                    s = jnp.where(valid, s, NEG)
                p = jnp.exp(s - lse_ref[hh])
                dv = _dot_tn(p.astype(BF16), dov)
                dp = _dot_nt(dov, v_ref[:, sl])
                ds = (p * (dp - dl_ref[hh])).astype(BF16)
                dk = _dot_tn(ds, qv)
                if diagonal:
                    dv_ref[:, sl] = dv
                    dk_ref[:, sl] = dk
                else:
                    dv_ref[:, sl] += dv
                    dk_ref[:, sl] += dk
                dq_ref[hh, pl.ds(r0, t), :] += _dot(ds, kv)

        pl.when(i > j)(lambda: step(False))
        pl.when(i == j)(lambda: step(True))

    def qmap(hb, s, ir, jr):
        return (ir[s], hb)

    def kvmap(hb, s, ir, jr):
        return (jr[s], hb)

    def rowmap(hb, s, ir, jr):
        return (hb, ir[s], 0)

    grid_spec = pltpu.PrefetchScalarGridSpec(
        num_scalar_prefetch=2, grid=(HEADS // hp, ii.shape[0]),
        in_specs=[pl.BlockSpec((t, w), qmap), pl.BlockSpec((t, w), kvmap), pl.BlockSpec((t, w), kvmap),
                  pl.BlockSpec((t, w), qmap), pl.BlockSpec((hp, t, 1), rowmap), pl.BlockSpec((hp, t, 1), rowmap)],
        out_specs=[pl.BlockSpec((hp, t_rows, LANES), lambda hb, s, ir, jr: (hb, 0, 0)),
                   pl.BlockSpec((t, w), kvmap), pl.BlockSpec((t, w), kvmap)])
    return pl.pallas_call(
        body, name="mla_bwd", grid_spec=grid_spec,
        out_shape=[_sds((HEADS, t_rows, LANES), F32), _sds((t_rows, HEADS * LANES), F32),
                   _sds((t_rows, HEADS * LANES), F32)],
        compiler_params=pltpu.CompilerParams(dimension_semantics=("arbitrary",) * 2, vmem_limit_bytes=VMEM_LIMIT),
    )(ii, jj, q, k, v, do, lse, delta)


SWA_TILE = 2 * SWA_WINDOW
SWA_GROUP = HEADS // A_KV_HEADS


def _swa_bias(tq):
    qoff = (lax.broadcasted_iota(jnp.int32, (SWA_GROUP * tq, tq + SWA_WINDOW), 0) % tq)
    koff = lax.broadcasted_iota(jnp.int32, (SWA_GROUP * tq, tq + SWA_WINDOW), 1) - SWA_WINDOW
    band = (koff <= qoff) & (qoff - koff < SWA_WINDOW)
    return jnp.stack([jnp.where(band & (koff >= 0), 0.0, NEG), jnp.where(band, 0.0, NEG)]).astype(F32)


def _swa_specs(tq, nq):
    wb = tq // SWA_WINDOW

    def qi(i):
        return jnp.minimum(i, nq - 1)

    q = pl.BlockSpec((tq, SWA_GROUP * LANES), lambda h, i: (qi(i), h))
    cur = pl.BlockSpec((tq, LANES), lambda h, i: (qi(i), h))
    prev = pl.BlockSpec((SWA_WINDOW, LANES), lambda h, i: (jnp.maximum(qi(i) * wb - 1, 0), h))
    bias = pl.BlockSpec((1, SWA_GROUP * tq, tq + SWA_WINDOW), lambda h, i: (jnp.minimum(i, 1), 0, 0))
    rows = pl.BlockSpec((1, 1, SWA_GROUP * tq, 1), lambda h, i: (h, qi(i), 0, 0))
    sink = pl.BlockSpec((1, SWA_GROUP * tq, 1), lambda h, i: (h, 0, 0))
    return q, cur, prev, bias, rows, sink


def _stack_heads(ref):
    return jnp.concatenate([ref[:, g * LANES:(g + 1) * LANES] for g in range(SWA_GROUP)], axis=0)


def _swa_fwd(q, k, v, bias, sink_rows):
    t_rows = q.shape[0]
    tq = min(SWA_TILE, t_rows)
    nq = t_rows // tq
    qs_, cur, prev, bs, rows, sk = _swa_specs(tq, nq)

    def body(q_ref, kc_ref, kp_ref, vc_ref, vp_ref, b_ref, sink_ref, o_ref, lse_ref):
        qs = _stack_heads(q_ref)
        kk = jnp.concatenate([kp_ref[...], kc_ref[...]], axis=0)
        vv = jnp.concatenate([vp_ref[...], vc_ref[...]], axis=0)
        s = _dot_nt(qs, kk) + b_ref[0]
        sink = sink_ref[0]
        m = jnp.maximum(jnp.max(s, axis=1, keepdims=True), sink)
        p = jnp.exp(s - m)
        l = jnp.sum(p, axis=1, keepdims=True) + jnp.exp(sink - m)
        o = _dot(p.astype(BF16), vv) / l
        for g in range(SWA_GROUP):
            o_ref[:, g * LANES:(g + 1) * LANES] = o[g * tq:(g + 1) * tq].astype(o_ref.dtype)
        lse_ref[0, 0] = m + jnp.log(l)

    return pl.pallas_call(
        body, name="swa_fwd", grid=(A_KV_HEADS, nq),
        in_specs=[qs_, cur, prev, cur, prev, bs, sk],
        out_specs=[qs_, rows],
        out_shape=[_sds((t_rows, HEADS * LANES), BF16), _sds((A_KV_HEADS, nq, SWA_GROUP * tq, 1), F32)],
        compiler_params=pltpu.CompilerParams(dimension_semantics=("arbitrary",) * 2, vmem_limit_bytes=VMEM_LIMIT),
    )(q, k, k, v, v, bias, sink_rows)


def _swa_bwd(q, k, v, o, do, lse, bias, sink_rows):
    t_rows = q.shape[0]
    tq = min(SWA_TILE, t_rows)
    nq = t_rows // tq
    qs_, cur, prev, bs, rows, sk = _swa_specs(tq, nq)
    hw = SWA_WINDOW

    def body(q_ref, kc_ref, kp_ref, vc_ref, vp_ref, o_ref, do_ref, lse_ref, b_ref, sink_ref,
             dq_ref, dk_ref, dv_ref, dsink_ref, ck, cv, dsa):
        i = pl.program_id(1)

        @pl.when(i == 0)
        def _():
            dsa[...] = jnp.zeros(dsa.shape, F32)

        @pl.when(i < nq)
        def _():
            qs = _stack_heads(q_ref)
            dos = _stack_heads(do_ref)
            kk = jnp.concatenate([kp_ref[...], kc_ref[...]], axis=0)
            vv = jnp.concatenate([vp_ref[...], vc_ref[...]], axis=0)
            lse = lse_ref[0, 0]
            p = jnp.exp(_dot_nt(qs, kk) + b_ref[0] - lse)
            delta = jnp.sum(_stack_heads(o_ref).astype(F32) * dos.astype(F32), axis=1, keepdims=True)
            dsa[...] += -jnp.exp(sink_ref[0] - lse) * delta
            dv = _dot_tn(p.astype(BF16), dos)
            ds = (p * (_dot_nt(dos, vv) - delta)).astype(BF16)
            dk = _dot_tn(ds, qs)
            dq = _dot(ds, kk)
            for g in range(SWA_GROUP):
                dq_ref[:, g * LANES:(g + 1) * LANES] = dq[g * tq:(g + 1) * tq]

            @pl.when(i > 0)
            def _():
                dk_ref[0:tq - hw, :] = ck[0:tq - hw, :]
                dk_ref[tq - hw:tq, :] = ck[tq - hw:tq, :] + dk[0:hw]
                dv_ref[0:tq - hw, :] = cv[0:tq - hw, :]
                dv_ref[tq - hw:tq, :] = cv[tq - hw:tq, :] + dv[0:hw]

            ck[...] = dk[hw:hw + tq]
            cv[...] = dv[hw:hw + tq]

        @pl.when(i == nq)
        def _():
            dk_ref[...] = ck[...]
            dv_ref[...] = cv[...]
            dsink_ref[...] = jnp.zeros(dsink_ref.shape, F32)
            for g in range(SWA_GROUP):
                tot = jnp.sum(dsa[g * tq:(g + 1) * tq, :], axis=0, keepdims=True)
                dsink_ref[0, g:g + 1, :] = jnp.zeros((1, LANES), F32) + tot

    kv_out = pl.BlockSpec((tq, LANES), lambda h, i: (jnp.maximum(i - 1, 0), h))
    return pl.pallas_call(
        body, name="swa_bwd", grid=(A_KV_HEADS, nq + 1),
        in_specs=[qs_, cur, prev, cur, prev, qs_, qs_, rows, bs, sk],
        out_specs=[qs_, kv_out, kv_out, pl.BlockSpec((1, 8, LANES), lambda h, i: (h, 0, 0))],
        out_shape=[_sds((t_rows, HEADS * LANES), F32), _sds((t_rows, A_KV_HEADS * LANES), F32),
                   _sds((t_rows, A_KV_HEADS * LANES), F32), _sds((A_KV_HEADS, 8, LANES), F32)],
        scratch_shapes=[pltpu.VMEM((tq, LANES), F32), pltpu.VMEM((tq, LANES), F32),
                        pltpu.VMEM((SWA_GROUP * tq, 1), F32)],
        compiler_params=pltpu.CompilerParams(dimension_semantics=("arbitrary",) * 2, vmem_limit_bytes=VMEM_LIMIT),
    )(q, k, k, v, v, o, do, lse, bias, sink_rows)


def _fwd_mix(x, ya, yb, gate, wba, wbb, wout, g2, g3, tm):
    t_rows = x.shape[0]

    def body(x_ref, ya_ref, yb_ref, gate_ref, wba_ref, wbb_ref, wout_ref, g2_ref, g3_ref,
             pa_ref, pb_ref, mixed_ref, o_ref, x1_ref, h2_ref):
        pa = _dot(ya_ref[...], wba_ref[...])
        pb = _dot(yb_ref[...], wbb_ref[...])
        pa_ref[...] = pa
        pb_ref[...] = pb
        mixed = (gate_ref[:, 0:D_MODEL] * pa + gate_ref[:, D_MODEL:2 * D_MODEL] * pb).astype(BF16)
        mixed_ref[...] = mixed
        o = _dot(mixed, wout_ref[...])
        o_ref[...] = o
        on, _ = _rms_stats(o)
        x1 = x_ref[...] + on * g2_ref[...]
        x1_ref[...] = x1
        x1n, _ = _rms_stats(x1)
        h2_ref[...] = (x1n * g3_ref[...]).astype(BF16)

    def o_(dt):
        return (_sds((t_rows, D_MODEL), dt), _row(tm, D_MODEL))

    ins = [(x, _row(tm, D_MODEL)), (ya, _row(tm, 1024)), (yb, _row(tm, 1024)), (gate, _row(tm, 2048)),
           (wba, _full(wba.shape)), (wbb, _full(wbb.shape)), (wout, _full(wout.shape)),
           (g2, _full(g2.shape)), (g3, _full(g3.shape))]
    return _rows_call("fwd_mix", body, t_rows, tm, ins, [o_(F32), o_(F32), o_(BF16), o_(F32), o_(F32), o_(BF16)])


CONV_CHUNK = 1408


def _fwd_up(h2, wup, convw8, convb, tm):
    t_rows = h2.shape[0]
    cdim = 2 * D_FF

    def body(h2_ref, wup_ref, cw_ref, cb_ref, up_ref, a_ref, carry):
        i = pl.program_id(0)

        @pl.when(i == 0)
        def _():
            carry[...] = jnp.zeros(carry.shape, F32)

        hb = h2_ref[...]

        def conv(c0):
            sl = slice(c0, c0 + CONV_CHUNK)
            up = _dot(hb, wup_ref[:, sl])
            up_ref[:, sl] = up
            xm1, xm2 = _conv_taps(up, carry[6:7, sl], carry[7:8, sl])
            u = cw_ref[0:1, sl] * xm2 + cw_ref[1:2, sl] * xm1 + cw_ref[2:3, sl] * up + cb_ref[:, sl]
            carry[:, sl] = up[tm - 8:tm, :]
            return u

        for c0 in range(0, D_FF, CONV_CHUNK):
            ug = conv(c0)
            uv = conv(D_FF + c0)
            gel, _ = _gelu_and_grad(ug)
            a_ref[:, c0:c0 + CONV_CHUNK] = (gel * uv).astype(BF16)

    ins = [(h2, _row(tm, D_MODEL)), (wup, _full(wup.shape)), (convw8, _full(convw8.shape)), (convb, _full(convb.shape))]
    outs = [(_sds((t_rows, cdim), F32), _row(tm, cdim)), (_sds((t_rows, D_FF), BF16), _row(tm, D_FF))]
    return _rows_call("fwd_up", body, t_rows, tm, ins, outs, scratch=[pltpu.VMEM((8, cdim), F32)])


def _fwd_out(a, wdown, x1, g4, p, wple, g5, wpg, tgt, tm):
    t_rows = a.shape[0]

    def body(a_ref, wdown_ref, x1_ref, g4_ref, p_ref, wple_ref, g5_ref, wpg_ref, tgt_ref,
             ff_ref, x2_ref, e_ref, n5_ref, sg_ref, dx3_ref, loss_ref):
        i = pl.program_id(0)
        ff = _dot(a_ref[...], wdown_ref[...])
        ff_ref[...] = ff
        ffn, _ = _rms_stats(ff)
        x2 = x1_ref[...] + ffn * g4_ref[...]
        x2_ref[...] = x2
        e = _dot(p_ref[...].astype(BF16), wple_ref[...])
        e_ref[...] = e
        x2n, _ = _rms_stats(x2)
        n5 = (x2n * g5_ref[...]).astype(BF16)
        n5_ref[...] = n5
        sg = _sigmoid(_dot(n5, wpg_ref[...]))
        sg_ref[...] = sg
        d = x2 + sg * e - tgt_ref[...]
        dx3_ref[...] = d * (1.0 / D_MODEL)

        @pl.when(i == 0)
        def _():
            loss_ref[...] = jnp.zeros((1, 1), F32)

        loss_ref[...] += 0.5 * jnp.sum(jnp.sum(d * d, axis=1, keepdims=True), axis=0, keepdims=True) * (1.0 / D_MODEL)

    def o_(dt):
        return (_sds((t_rows, D_MODEL), dt), _row(tm, D_MODEL))

    ins = [(a, _row(tm, D_FF)), (wdown, _full(wdown.shape)), (x1, _row(tm, D_MODEL)), (g4, _full(g4.shape)),
           (p, _row(tm, PLE_DIM)), (wple, _full(wple.shape)), (g5, _full(g5.shape)), (wpg, _full(wpg.shape)),
           (tgt, _row(tm, D_MODEL))]
    outs = [o_(F32), o_(F32), o_(F32), o_(BF16), o_(F32), o_(F32), (_sds((1, 1), F32), _full((1, 1)))]
    return _rows_call("fwd_out", body, t_rows, tm, ins, outs)


def _bwd_out(dx3, e, sg, x2, ff, g5, g4, wpg, wdown, up, convw8, convb, tm):
    t_rows = dx3.shape[0]
    cdim = 2 * D_FF
    hb = tm // 8

    def body(dx3_ref, e_ref, sg_ref, x2_ref, ff_ref, g5_ref, g4_ref, wpg_ref, wdown_ref, up_ref, halo_ref, cw_ref,
             cb_ref, dpre_ref, de_ref, dx2_ref, dff_ref, du_ref, dg5_ref, dg4_ref, dcb_ref, dcw_ref):
        i = pl.program_id(0)

        @pl.when(i == 0)
        def _():
            dg5_ref[...] = jnp.zeros(dg5_ref.shape, F32)
            dg4_ref[...] = jnp.zeros(dg4_ref.shape, F32)
            dcb_ref[...] = jnp.zeros(dcb_ref.shape, F32)
            dcw_ref[...] = jnp.zeros(dcw_ref.shape, F32)

        dx3 = dx3_ref[...]
        sg = sg_ref[...]
        dpre = (dx3 * e_ref[...] * sg * (1.0 - sg)).astype(BF16)
        dpre_ref[...] = dpre
        de_ref[...] = (dx3 * sg).astype(BF16)
        dn5 = _dot_nt(dpre, wpg_ref[...])
        x2n, r5 = _rms_stats(x2_ref[...])
        d2, dg5 = _rms_bwd(dn5, x2n, r5, g5_ref[...])
        dx2 = dx3 + d2
        dx2_ref[...] = dx2
        dg5_ref[...] += dg5
        ffn, r4 = _rms_stats(ff_ref[...])
        dff, dg4 = _rms_bwd(dx2, ffn, r4, g4_ref[...])
        dg4_ref[...] += dg4
        dffb = dff.astype(BF16)
        dff_ref[...] = dffb
        keep = jnp.where(i > 0, 1.0, 0.0)

        def conv(c0):
            sl = slice(c0, c0 + CONV_CHUNK)
            up = up_ref[:, sl]
            xm1, xm2 = _conv_taps(up, halo_ref[6:7, sl] * keep, halo_ref[7:8, sl] * keep)
            u = cw_ref[0:1, sl] * xm2 + cw_ref[1:2, sl] * xm1 + cw_ref[2:3, sl] * up + cb_ref[:, sl]
            return u, up, xm1, xm2

        def grads(c0, du, up, xm1, xm2):
            sl = slice(c0, c0 + CONV_CHUNK)
            du_ref[:, sl] = du
            dcb_ref[:, sl] += jnp.sum(du, axis=0, keepdims=True)
            dcw_ref[0:1, sl] += jnp.sum(du * xm2, axis=0, keepdims=True)
            dcw_ref[1:2, sl] += jnp.sum(du * xm1, axis=0, keepdims=True)
            dcw_ref[2:3, sl] += jnp.sum(du * up, axis=0, keepdims=True)

        for c0 in range(0, D_FF, CONV_CHUNK):
            da = _dot_nt(dffb, wdown_ref[c0:c0 + CONV_CHUNK, :])
            ug, *rg = conv(c0)
            uv, *rv = conv(D_FF + c0)
            gel, dgel = _gelu_and_grad(ug)
            grads(c0, da * uv * dgel, *rg)
            grads(D_FF + c0, da * gel, *rv)

    def o_(n, dt):
        return (_sds((t_rows, n), dt), _row(tm, n))

    def acc(r, n):
        return (_sds((r, n), F32), _full((r, n)))

    halo = pl.BlockSpec((8, cdim), lambda i: (jnp.maximum(i * hb - 1, 0), 0))
    ins = [(dx3, _row(tm, D_MODEL)), (e, _row(tm, D_MODEL)), (sg, _row(tm, D_MODEL)), (x2, _row(tm, D_MODEL)),
           (ff, _row(tm, D_MODEL)), (g5, _full(g5.shape)), (g4, _full(g4.shape)), (wpg, _full(wpg.shape)),
           (wdown, _full(wdown.shape)), (up, _row(tm, cdim)), (up, halo), (convw8, _full(convw8.shape)),
           (convb, _full(convb.shape))]
    outs = [o_(D_MODEL, BF16), o_(D_MODEL, BF16), o_(D_MODEL, F32), o_(D_MODEL, BF16), o_(cdim, F32),
            acc(1, D_MODEL), acc(1, D_MODEL), acc(1, cdim), acc(8, cdim)]
    return _rows_call("bwd_out", body, t_rows, tm, ins, outs)


def _bwd_mid(du, convw8, wup, dx2, x1, g3, o, g2, wout, gate, pa, pb, wba, wbb, tm):
    t_rows = du.shape[0]
    cdim = 2 * D_FF
    hb = tm // 8
    last_blk = t_rows // 8 - 1
    n_tiles = t_rows // tm

    def body(du_ref, halo_ref, cw_ref, wup_ref, dx2_ref, x1_ref, g3_ref, o_ref, g2_ref, wout_ref, gate_ref, pa_ref,
             pb_ref, wba_ref, wbb_ref,
             dup_ref, dx1_ref, do_ref, dpa_ref, dpb_ref, dgt_ref, dya_ref, dyb_ref, dg3_ref, dg2_ref, dbg_ref):
        i = pl.program_id(0)

        @pl.when(i == 0)
        def _():
            dg3_ref[...] = jnp.zeros(dg3_ref.shape, F32)
            dg2_ref[...] = jnp.zeros(dg2_ref.shape, F32)
            dbg_ref[...] = jnp.zeros(dbg_ref.shape, F32)

        keep = jnp.where(i < n_tiles - 1, 1.0, 0.0)
        dh2 = jnp.zeros((tm, D_MODEL), F32)
        for c0 in range(0, cdim, CONV_CHUNK):
            sl = slice(c0, c0 + CONV_CHUNK)
            du = du_ref[:, sl]
            xp1, xp2 = _conv_taps_next(du, halo_ref[0:1, sl] * keep, halo_ref[1:2, sl] * keep)
            dup = (cw_ref[2:3, sl] * du + cw_ref[1:2, sl] * xp1 + cw_ref[0:1, sl] * xp2).astype(BF16)
            dup_ref[:, sl] = dup
            dh2 = dh2 + _dot_nt(dup, wup_ref[:, sl])
        x1n, r3 = _rms_stats(x1_ref[...])
        d1, dg3 = _rms_bwd(dh2, x1n, r3, g3_ref[...])
        dx1 = dx2_ref[...] + d1
        dx1_ref[...] = dx1
        dg3_ref[...] += dg3
        on, r2 = _rms_stats(o_ref[...])
        do, dg2 = _rms_bwd(dx1, on, r2, g2_ref[...])
        dg2_ref[...] += dg2
        dob = do.astype(BF16)
        do_ref[...] = dob
        dmixed = _dot_nt(dob, wout_ref[...])
        ga = gate_ref[:, 0:D_MODEL]
        gb = gate_ref[:, D_MODEL:2 * D_MODEL]
        dpa = (dmixed * ga).astype(BF16)
        dpb = (dmixed * gb).astype(BF16)
        dpa_ref[...] = dpa
        dpb_ref[...] = dpb
        dga = dmixed * pa_ref[...] * ga * (1.0 - ga)
        dgb = dmixed * pb_ref[...] * gb * (1.0 - gb)
        dgt_ref[:, 0:D_MODEL] = dga.astype(BF16)
        dgt_ref[:, D_MODEL:2 * D_MODEL] = dgb.astype(BF16)
        dbg_ref[:, 0:D_MODEL] += jnp.sum(dga, axis=0, keepdims=True)
        dbg_ref[:, D_MODEL:2 * D_MODEL] += jnp.sum(dgb, axis=0, keepdims=True)
        dya_ref[...] = _dot_nt(dpa, wba_ref[...]).astype(BF16)
        dyb_ref[...] = _dot_nt(dpb, wbb_ref[...]).astype(BF16)

    def o_(n, dt):
        return (_sds((t_rows, n), dt), _row(tm, n))

    def acc(r, n):
        return (_sds((r, n), F32), _full((r, n)))

    halo = pl.BlockSpec((8, cdim), lambda i: (jnp.minimum((i + 1) * hb, last_blk), 0))
    ins = [(du, _row(tm, cdim)), (du, halo), (convw8, _full(convw8.shape)), (wup, _full(wup.shape)),
           (dx2, _row(tm, D_MODEL)), (x1, _row(tm, D_MODEL)), (g3, _full(g3.shape)), (o, _row(tm, D_MODEL)),
           (g2, _full(g2.shape)), (wout, _full(wout.shape)), (gate, _row(tm, 2048)), (pa, _row(tm, D_MODEL)),
           (pb, _row(tm, D_MODEL)), (wba, _full(wba.shape)), (wbb, _full(wbb.shape))]
    outs = [o_(cdim, BF16), o_(D_MODEL, F32), o_(D_MODEL, BF16), o_(D_MODEL, BF16), o_(D_MODEL, BF16),
            o_(2048, BF16), o_(1024, BF16), o_(1024, BF16), acc(1, D_MODEL), acc(1, D_MODEL), acc(1, 2048)]
    return _rows_call("bwd_mid", body, t_rows, tm, ins, outs)


def _bwd_in(dqs, dks, dvs, dqm, dkm, dvm, tabs, consts, cq, ckv, gq, gkv, wuq, wk, wv, dgates, win, x, g1, dx1, tm):
    t_rows = x.shape[0]

    def body(dqs_ref, dks_ref, dvs_ref, dqm_ref, dkm_ref, dvm_ref, ca, sa1, sa2, cb, sb1, sb2, c_ref, cq_ref,
             ckv_ref, gq_ref, gkv_ref, wuq_ref, wk_ref, wv_ref, dgt_ref, win_ref, x_ref, g1_ref, dx1_ref,
             dz_ref, dqb_ref, dx_ref, dgq_ref, dgkv_ref, dg1_ref):
        i = pl.program_id(0)

        @pl.when(i == 0)
        def _():
            dgq_ref[...] = jnp.zeros(dgq_ref.shape, F32)
            dgkv_ref[...] = jnp.zeros(dgkv_ref.shape, F32)
            dg1_ref[...] = jnp.zeros(dg1_ref.shape, F32)

        ta = (ca[...], sa1[...], sa2[...])
        tb = (cb[...], sb1[...], sb2[...])
        dz_ref[:, Z_QA:Z_KA] = _rope_t(dqs_ref[...] * SCALE_A, *ta, A_HEAD_DIM // 2).astype(BF16)
        dz_ref[:, Z_KA:Z_VA] = _rope_t(dks_ref[...], *ta, A_HEAD_DIM // 2).astype(BF16)
        dz_ref[:, Z_VA:Z_CQ] = dvs_ref[...].astype(BF16)
        dqm = jnp.concatenate([dqm_ref[h] for h in range(HEADS)], axis=1)
        dqb = _rope_t(dqm * SCALE_B, *tb, ROPE_DIM // 2).astype(BF16)
        dqb_ref[...] = dqb
        dcqn = _dot_nt(dqb, wuq_ref[...])
        cqn, rq = _rms_stats(cq_ref[...])
        dcq, dgq = _rms_bwd(dcqn, cqn, rq, gq_ref[...])
        dgq_ref[...] += dgq
        dz_ref[:, Z_CQ:Z_CKV] = dcq.astype(BF16)
        dkm = dkm_ref[...]
        dslot = dkm[:, 0:LANES]
        for h in range(1, HEADS):
            dslot = dslot + dkm[:, h * LANES:(h + 1) * LANES]
        dz_ref[:, Z_KR:Z_GATE] = _rope_t(dslot * c_ref[10:11, :], *tb, ROPE_DIM // 2).astype(BF16)
        dckvn = _dot_nt(dkm.astype(BF16), wk_ref[...]) + _dot_nt(dvm_ref[...].astype(BF16), wv_ref[...])
        ckvn, rkv = _rms_stats(ckv_ref[...])
        dckv, dgkv = _rms_bwd(dckvn, ckvn, rkv, gkv_ref[...])
        dgkv_ref[...] += dgkv
        dz_ref[:, Z_CKV:Z_KR] = dckv.astype(BF16)
        dz_ref[:, Z_GATE:ZW] = dgt_ref[...]
        dh1 = _dot_nt(dz_ref[...], win_ref[...])
        xn, r1 = _rms_stats(x_ref[...])
        d0, dg1 = _rms_bwd(dh1, xn, r1, g1_ref[...])
        dg1_ref[...] += dg1
        dx_ref[...] = dx1_ref[...] + d0

    def acc(n):
        return (_sds((1, n), F32), _full((1, n)))

    ins = [(dqs, _row(tm, 1024)), (dks, _row(tm, 256)), (dvs, _row(tm, 256)), (dqm, _heads(tm, HEADS)),
           (dkm, _row(tm, 1024)), (dvm, _row(tm, 1024))] + [(t, _row(tm, LANES)) for t in tabs] + [
           (consts, _full(consts.shape)), (cq, _row(tm, 256)), (ckv, _row(tm, 128)), (gq, _full(gq.shape)),
           (gkv, _full(gkv.shape)), (wuq, _full(wuq.shape)), (wk, _full(wk.shape)), (wv, _full(wv.shape)),
           (dgates, _row(tm, 2048)), (win, _full(win.shape)), (x, _row(tm, D_MODEL)), (g1, _full(g1.shape)),
           (dx1, _row(tm, D_MODEL))]
    outs = [(_sds((t_rows, ZW), BF16), _row(tm, ZW)), (_sds((t_rows, 1024), BF16), _row(tm, 1024)),
            (_sds((t_rows, D_MODEL), F32), _row(tm, D_MODEL)), acc(256), acc(128), acc(D_MODEL)]
    return _rows_call("bwd_in", body, t_rows, tm, ins, outs)


def _pick_cols(n):
    best = LANES
    for d in range(LANES, min(n, 1408) + 1, LANES):
        if n % d == 0:
            best = d
    return best


def _mm_tn(name, a, b):
    t_rows, m = a.shape
    n = b.shape[1]
    bk = min(512, t_rows)
    bm, bn = _pick_cols(m), _pick_cols(n)

    def body(a_ref, b_ref, o_ref):
        @pl.when(pl.program_id(2) == 0)
        def _():
            o_ref[...] = jnp.zeros((bm, bn), F32)

        o_ref[...] += _dot_tn(a_ref[...].astype(BF16), b_ref[...].astype(BF16))

    return pl.pallas_call(
        body, name=name, grid=(m // bm, n // bn, t_rows // bk),
        in_specs=[pl.BlockSpec((bk, bm), lambda i, j, k: (k, i)), pl.BlockSpec((bk, bn), lambda i, j, k: (k, j))],
        out_specs=pl.BlockSpec((bm, bn), lambda i, j, k: (i, j)),
        out_shape=_sds((m, n), F32),
        compiler_params=pltpu.CompilerParams(dimension_semantics=("arbitrary",) * 3, vmem_limit_bytes=VMEM_LIMIT),
    )(a, b)


PACK_ROWS = 512


def _pack_tile(rows):
    assert rows % PACK_ROWS == 0
    return PACK_ROWS


def _add_pair(a, b):
    _, rows, _ = a.shape
    t = _pack_tile(rows)

    def body(a_ref, b_ref, o_ref):
        o_ref[...] = (a_ref[...] + b_ref[...]).astype(BF16)

    spec = pl.BlockSpec((4, t, LANES), lambda i: (0, i, 0))
    return pl.pallas_call(body, name="rs_add_pair", grid=(rows // t,), in_specs=[spec, spec], out_specs=spec,
                          out_shape=_sds(a.shape, BF16))(a, b)


def _add_chips(parts):
    _, rows, _ = parts.shape
    t = _pack_tile(rows)

    def body(p_ref, o_ref):
        acc = p_ref[0].astype(F32)
        for j in range(1, 4):
            acc = acc + p_ref[j].astype(F32)
        o_ref[...] = acc

    return pl.pallas_call(body, name="rs_add_chips", grid=(rows // t,),
                          in_specs=[pl.BlockSpec((4, t, LANES), lambda i: (0, i, 0))],
                          out_specs=pl.BlockSpec((t, LANES), lambda i: (i, 0)),
                          out_shape=_sds((rows, LANES), F32))(parts)


def _add_devices(parts):
    n, rows, _ = parts.shape

    def body(p_ref, o_ref):
        acc = p_ref[0]
        for j in range(1, n):
            acc = acc + p_ref[j]
        o_ref[...] = acc

    return pl.pallas_call(body, name="small_add", grid=(1,),
                          in_specs=[pl.BlockSpec((n, rows, LANES), lambda i: (0, 0, 0))],
                          out_specs=pl.BlockSpec((rows, LANES), lambda i: (0, 0)),
                          out_shape=_sds((rows, LANES), F32))(parts)


def _adam_rows(k, n):
    target = max(8, (1 << 20) // (4 * n))
    if k <= target:
        return k
    best = None
    for d in range(8, target + 1, 8):
        if k % d == 0:
            best = d
    return best if best is not None else k


def _adamw(name, w, g, m, v):
    k, n = w.shape
    bk = _adam_rows(k, n)
    c1 = 1.0 - ADAM_B1 ** ADAM_STEP
    c2 = 1.0 - ADAM_B2 ** ADAM_STEP

    def body(w_ref, g_ref, m_ref, v_ref, d_ref, mo_ref, vo_ref):
        g_ = g_ref[...]
        m_ = ADAM_B1 * m_ref[...] + (1.0 - ADAM_B1) * g_
        v_ = ADAM_B2 * v_ref[...] + (1.0 - ADAM_B2) * (g_ * g_)
        mo_ref[...] = m_
        vo_ref[...] = v_
        d_ref[...] = -ADAM_LR * ((m_ / c1) / (jnp.sqrt(v_ / c2) + ADAM_EPS) + ADAM_WD * w_ref[...])

    spec = pl.BlockSpec((bk, n), lambda i: (i, 0))
    return pl.pallas_call(body, name=name, grid=(k // bk,), in_specs=[spec] * 4, out_specs=[spec] * 3,
                          out_shape=[_sds((k, n), F32)] * 3,
                          compiler_params=pltpu.CompilerParams(vmem_limit_bytes=VMEM_LIMIT))(w, g, m, v)


_HBM = pl.BlockSpec(memory_space=pltpu.HBM)


def _me():
    return lax.axis_index("x"), lax.axis_index("y"), lax.axis_index("c")


def _other_chips(x, y):
    return [(1 - x, y), (x, 1 - y), (1 - x, 1 - y)]


def _gather_weights(shard):
    def body(x_ref, out_ref, send_sems, recv_sems):
        x, y, c = _me()
        sibling = (x, y, 1 - c)
        chips = _other_chips(x, y)

        def slot(px, py, pc):
            return out_ref.at[2 * px + py, pc]

        def copy(k, src, dst, to):
            return pltpu.make_async_remote_copy(src_ref=src, dst_ref=dst, send_sem=send_sems.at[k],
                                                recv_sem=recv_sems.at[k], device_id=to, device_id_type=MESH)

        first = [copy(j, x_ref.at[c], slot(x, y, c), (*chip, c)) for j, chip in enumerate(chips)]
        for cp in first:
            cp.start()
        passed = [copy(3 + j, slot(*chip, c), slot(*chip, c), sibling) for j, chip in enumerate(chips)]
        for j, chip in enumerate(chips):
            copy(j, x_ref.at[c], slot(*chip, c), (*chip, c)).wait_recv()
            passed[j].start()
        for j, chip in enumerate(chips):
            copy(3 + j, slot(*chip, 1 - c), slot(*chip, 1 - c), sibling).wait_recv()
        for cp in first + passed:
            cp.wait_send()

    return pl.pallas_call(
        body, name="gather_weights", out_shape=_sds((4,) + shard.shape, shard.dtype), in_specs=[_HBM], out_specs=_HBM,
        scratch_shapes=[pltpu.SemaphoreType.DMA((6,)), pltpu.SemaphoreType.DMA((6,))],
    )(shard)


def _swap_sibling(name, v):
    def body(v_ref, out_ref, send_sem, recv_sem):
        x, y, c = _me()
        cp = pltpu.make_async_remote_copy(src_ref=v_ref, dst_ref=out_ref, send_sem=send_sem, recv_sem=recv_sem,
                                          device_id=(x, y, 1 - c), device_id_type=MESH)
        cp.start()
        cp.wait()

    return pl.pallas_call(
        body, name=name, out_shape=_sds(v.shape, v.dtype), in_specs=[_HBM], out_specs=_HBM,
        scratch_shapes=[pltpu.SemaphoreType.DMA, pltpu.SemaphoreType.DMA],
    )(v)


def _scatter_chips(s):
    def body(s_ref, out_ref, send_sems, recv_sems):
        x, y, c = _me()
        k = 2 * x + y
        chips = _other_chips(x, y)
        sends = [pltpu.make_async_remote_copy(src_ref=s_ref.at[2 * cx + cy], dst_ref=out_ref.at[k],
                                              send_sem=send_sems.at[j], recv_sem=recv_sems.at[j],
                                              device_id=(cx, cy, c), device_id_type=MESH)
                 for j, (cx, cy) in enumerate(chips)]
        for cp in sends:
            cp.start()
        for j, (cx, cy) in enumerate(chips):
            pltpu.make_async_remote_copy(src_ref=s_ref.at[k], dst_ref=out_ref.at[2 * cx + cy],
                                         send_sem=send_sems.at[j], recv_sem=recv_sems.at[j],
                                         device_id=(cx, cy, c), device_id_type=MESH).wait_recv()
        for cp in sends:
            cp.wait_send()

    return pl.pallas_call(
        body, name="scatter_chips", out_shape=_sds(s.shape, s.dtype), in_specs=[_HBM], out_specs=_HBM,
        scratch_shapes=[pltpu.SemaphoreType.DMA((3,)), pltpu.SemaphoreType.DMA((3,))],
    )(s)


def _gather_small(name, v):
    def body(v_ref, out_ref, send_sems, recv_sems, local_sem):
        x, y, c = _me()
        me = 4 * x + 2 * y + c
        mine = pltpu.make_async_copy(v_ref, out_ref.at[me], local_sem)
        mine.start()
        peers = []
        for f in range(1, 8):
            fx, fy, fc = (f >> 2) & 1, (f >> 1) & 1, f & 1
            px = 1 - x if fx else x
            py = 1 - y if fy else y
            pc = 1 - c if fc else c
            peers.append((f - 1, (px, py, pc)))
        sends = [pltpu.make_async_remote_copy(src_ref=v_ref, dst_ref=out_ref.at[me], send_sem=send_sems.at[k],
                                              recv_sem=recv_sems.at[k], device_id=peer, device_id_type=MESH)
                 for k, peer in peers]
        for cp in sends:
            cp.start()
        for k, (px, py, pc) in peers:
            pltpu.make_async_remote_copy(src_ref=v_ref, dst_ref=out_ref.at[4 * px + 2 * py + pc],
                                         send_sem=send_sems.at[k], recv_sem=recv_sems.at[k],
                                         device_id=(px, py, pc), device_id_type=MESH).wait_recv()
        for cp in sends:
            cp.wait_send()
        mine.wait()

    return pl.pallas_call(
        body, name=name, out_shape=_sds((8,) + v.shape, v.dtype), in_specs=[_HBM], out_specs=_HBM,
        scratch_shapes=[pltpu.SemaphoreType.DMA((7,)), pltpu.SemaphoreType.DMA((7,)), pltpu.SemaphoreType.DMA],
    )(v)


_BIG = (("w_in", (1024, 3232), 1), ("w_uq", (256, 768), 1), ("w_ukv", (128, 1024), 1), ("w_branch_a", (512, 1024), 1),
        ("w_branch_b", (512, 1024), 1), ("w_out", (1024, 1024), 0), ("w_up", (1024, 5632), 1),
        ("w_down", (2816, 1024), 0), ("w_ple_gate", (1024, 1024), 0), ("w_ple", (256, 1024), 1))


def _shard_shape(shape, axis):
    return (shape[0] // 4, shape[1]) if axis == 0 else (shape[0], shape[1] // 4)


def _half_rows(shape, axis):
    k, n = _shard_shape(shape, axis)
    return k * n // (2 * LANES)


_PACK_PAD = -sum(_half_rows(shape, axis) for _, shape, axis in _BIG) % PACK_ROWS


def _pack_shards(shards, dtype):
    parts = [shards[name].astype(dtype).reshape(2, _half_rows(shape, axis), LANES) for name, shape, axis in _BIG]
    return jnp.concatenate(parts + [jnp.zeros((2, _PACK_PAD, LANES), dtype)], axis=1)


def _unpack_gathered(g):
    out, off = {}, 0
    for name, shape, axis in _BIG:
        r = _half_rows(shape, axis)
        k, n = _shard_shape(shape, axis)
        w = g[:, :, off:off + r, :].reshape(4, k, n)
        out[name] = w.reshape(shape) if axis == 0 else w.transpose(1, 0, 2).reshape(shape)
        off += r
    return out


def _pack_grad_halves(grads, c):
    keep, send = [], []
    for name, shape, axis in _BIG:
        k, n = _shard_shape(shape, axis)
        r = _half_rows(shape, axis)
        g = grads[name]
        g4 = g.reshape(4, k, n) if axis == 0 else g.reshape(k, 4, n).transpose(1, 0, 2)
        g4 = g4.reshape(4, 2, r, LANES)
        keep.append(lax.dynamic_index_in_dim(g4, c, 1, keepdims=False))
        send.append(lax.dynamic_index_in_dim(g4, 1 - c, 1, keepdims=False))
    pad = [jnp.zeros((4, _PACK_PAD, LANES), F32)]
    return jnp.concatenate(keep + pad, axis=1), jnp.concatenate(send + pad, axis=1)


def _unpack_shard_grads(f):
    out, off = {}, 0
    for name, shape, axis in _BIG:
        r = _half_rows(shape, axis)
        out[name] = f[:, off:off + r, :].reshape(_shard_shape(shape, axis))
        off += r
    return out


def _pad_slots(w, heads, dim, axis):
    if axis == 1:
        k = w.shape[0]
        return jnp.pad(w.reshape(k, heads, dim), ((0, 0), (0, 0), (0, LANES - dim))).reshape(k, heads * LANES)
    n = w.shape[1]
    return jnp.pad(w.reshape(heads, dim, n), ((0, 0), (0, LANES - dim), (0, 0))).reshape(heads * LANES, n)


def _unpad_slots(w, heads, dim, axis):
    if axis == 1:
        k = w.shape[0]
        return w.reshape(k, heads, LANES)[:, :, :dim].reshape(k, heads * dim)
    n = w.shape[1]
    return w.reshape(heads, LANES, n)[:, :dim, :].reshape(heads * dim, n)


def _pad_w_in(w):
    kr = jnp.pad(w[:, 1152:1184], ((0, 0), (NOPE_DIM, LANES - NOPE_DIM - ROPE_DIM)))
    return jnp.concatenate([_pad_slots(w[:, 0:512], HEADS, A_HEAD_DIM, 1),
                            _pad_slots(w[:, 512:640], A_KV_HEADS, A_HEAD_DIM, 1),
                            _pad_slots(w[:, 640:768], A_KV_HEADS, A_HEAD_DIM, 1),
                            w[:, 768:1024], w[:, 1024:1152], kr, w[:, 1184:3232]], axis=1)


def _unpad_w_in(w):
    return jnp.concatenate([_unpad_slots(w[:, Z_QA:Z_KA], HEADS, A_HEAD_DIM, 1),
                            _unpad_slots(w[:, Z_KA:Z_VA], A_KV_HEADS, A_HEAD_DIM, 1),
                            _unpad_slots(w[:, Z_VA:Z_CQ], A_KV_HEADS, A_HEAD_DIM, 1),
                            w[:, Z_CQ:Z_CKV], w[:, Z_CKV:Z_KR],
                            w[:, Z_KR + NOPE_DIM:Z_KR + NOPE_DIM + ROPE_DIM], w[:, Z_GATE:ZW]], axis=1)


_SMALL = (("attn_pre_norm", 1024), ("attn_post_norm", 1024), ("b_gate", 2048), ("sinks", 8), ("q_a_norm", 256),
          ("kv_a_norm", 128), ("mlp_pre_norm", 1024), ("mlp_post_norm", 1024), ("conv_b", 5632), ("ple_norm", 1024),
          ("conv_w", 3 * 5632))


def _small_rows(n):
    return -(-n // LANES)


def _pack_small(vals):
    parts = []
    for name, n in _SMALL:
        r = _small_rows(n)
        parts.append(jnp.pad(vals[name].reshape(-1), (0, r * LANES - n)).reshape(r, LANES))
    rows = sum(_small_rows(n) for _, n in _SMALL)
    pad = -rows % 8
    if pad:
        parts.append(jnp.zeros((pad, LANES), F32))
    return jnp.concatenate(parts, axis=0)


def _unpack_small(buf):
    out, off = {}, 0
    for name, n in _SMALL:
        r = _small_rows(n)
        out[name] = buf[off:off + r].reshape(-1)[:n]
        off += r
    return out


def kernel(x, p, positions, attn_pre_norm, attn_post_norm, w_in, b_gate, sinks, q_a_norm, w_uq, kv_a_norm, w_ukv, w_branch_a, w_branch_b, w_out, mlp_pre_norm, mlp_post_norm, w_up, conv_w, conv_b, w_down, ple_norm, w_ple_gate, w_ple, loss_target, m_attn_pre_norm, m_attn_post_norm, m_w_in, m_b_gate, m_sinks, m_q_a_norm, m_w_uq, m_kv_a_norm, m_w_ukv, m_w_branch_a, m_w_branch_b, m_w_out, m_mlp_pre_norm, m_mlp_post_norm, m_w_up, m_conv_w, m_conv_b, m_w_down, m_ple_norm, m_w_ple_gate, m_w_ple, v_attn_pre_norm, v_attn_post_norm, v_w_in, v_b_gate, v_sinks, v_q_a_norm, v_w_uq, v_kv_a_norm, v_w_ukv, v_w_branch_a, v_w_branch_b, v_w_out, v_mlp_pre_norm, v_mlp_post_norm, v_w_up, v_conv_w, v_conv_b, v_w_down, v_ple_norm, v_w_ple_gate, v_w_ple):
    names = ["attn_pre_norm", "attn_post_norm", "w_in", "b_gate", "sinks", "q_a_norm", "w_uq", "kv_a_norm", "w_ukv",
             "w_branch_a", "w_branch_b", "w_out", "mlp_pre_norm", "mlp_post_norm", "w_up", "conv_w", "conv_b",
             "w_down", "ple_norm", "w_ple_gate", "w_ple"]
    wts = dict(zip(names, [attn_pre_norm, attn_post_norm, w_in, b_gate, sinks, q_a_norm, w_uq, kv_a_norm, w_ukv,
                           w_branch_a, w_branch_b, w_out, mlp_pre_norm, mlp_post_norm, w_up, conv_w, conv_b, w_down,
                           ple_norm, w_ple_gate, w_ple]))
    moms = dict(zip(names, [m_attn_pre_norm, m_attn_post_norm, m_w_in, m_b_gate, m_sinks, m_q_a_norm, m_w_uq,
                            m_kv_a_norm, m_w_ukv, m_w_branch_a, m_w_branch_b, m_w_out, m_mlp_pre_norm,
                            m_mlp_post_norm, m_w_up, m_conv_w, m_conv_b, m_w_down, m_ple_norm, m_w_ple_gate, m_w_ple]))
    vars_ = dict(zip(names, [v_attn_pre_norm, v_attn_post_norm, v_w_in, v_b_gate, v_sinks, v_q_a_norm, v_w_uq,
                             v_kv_a_norm, v_w_ukv, v_w_branch_a, v_w_branch_b, v_w_out, v_mlp_pre_norm,
                             v_mlp_post_norm, v_w_up, v_conv_w, v_conv_b, v_w_down, v_ple_norm, v_w_ple_gate, v_w_ple]))
    w2 = {n: a.reshape(a.shape[-2:]) for n, a in wts.items()}
    m2 = {n: a.reshape(a.shape[-2:]) for n, a in moms.items()}
    v2 = {n: a.reshape(a.shape[-2:]) for n, a in vars_.items()}

    t_rows = x.shape[-2]
    tm = min(256, t_rows)
    xc, yc, cc = lax.axis_index("x"), lax.axis_index("y"), lax.axis_index("c")
    chip = 2 * xc + yc

    x2d = x.reshape(t_rows, D_MODEL)
    p2d = p.reshape(t_rows, PLE_DIM)
    tgt = loss_target.reshape(t_rows, D_MODEL)
    pos_f = positions.reshape(t_rows, 1).astype(F32)

    my_shard = _pack_shards(w2, BF16)
    gathered = lax.dynamic_update_slice(_gather_weights(my_shard), my_shard[None], (chip, 0, 0, 0))
    full = _unpack_gathered(gathered)
    cw_rows = 3 * 1408 // LANES
    cw_all = _gather_small("gather_conv_w", jnp.pad(w2["conv_w"].reshape(cw_rows, LANES), ((0, 40 - cw_rows), (0, 0))))
    conv_full = cw_all[0::2, :cw_rows].reshape(4, 3, 1408).transpose(1, 0, 2).reshape(3, 2 * D_FF)
    convw8 = jnp.pad(conv_full, ((0, 5), (0, 0)))

    win = _pad_w_in(full["w_in"])
    wuq = _pad_slots(full["w_uq"], HEADS, NOPE_DIM + ROPE_DIM, 1)
    ukv = full["w_ukv"].reshape(KV_LORA, HEADS, NOPE_DIM + V_DIM)
    wk = _pad_slots(ukv[:, :, :NOPE_DIM].reshape(KV_LORA, HEADS * NOPE_DIM), HEADS, NOPE_DIM, 1)
    wv = _pad_slots(ukv[:, :, NOPE_DIM:].reshape(KV_LORA, HEADS * V_DIM), HEADS, V_DIM, 1)
    wba = _pad_slots(full["w_branch_a"], HEADS, A_HEAD_DIM, 0)
    wbb = _pad_slots(full["w_branch_b"], HEADS, V_DIM, 0)
    wout, wup, wdown, wpg, wple = full["w_out"], full["w_up"], full["w_down"], full["w_ple_gate"], full["w_ple"]
    g1, g2, g3, g4, g5 = (w2["attn_pre_norm"], w2["attn_post_norm"], w2["mlp_pre_norm"], w2["mlp_post_norm"],
                          w2["ple_norm"])
    gq, gkv, bg, convb = w2["q_a_norm"], w2["kv_a_norm"], w2["b_gate"], w2["conv_b"]
    swa_tile = min(SWA_TILE, t_rows)
    sink_rows = jnp.repeat(w2["sinks"].reshape(A_KV_HEADS, SWA_GROUP, 1), swa_tile, axis=2).reshape(
        A_KV_HEADS, SWA_GROUP * swa_tile, 1)
    swa_bias = _swa_bias(swa_tile)

    consts = _rope_consts()
    tabs = _rope_tables(pos_f, consts, tm)
    h1, qs, ks, vs, cq, cqn, ckv, ckvn, qm, km, vm, gate = _fwd_in(x2d, g1, win, bg, gq, gkv, wuq, wk, wv, tabs, tm)
    ya, lse_a = _swa_fwd(qs, ks, vs, swa_bias, sink_rows)
    yb, lse_b = _mla_fwd(qm, km, vm)
    pa, pb, mixed, o, x1, h2 = _fwd_mix(x2d, ya, yb, gate, wba, wbb, wout, g2, g3, tm)
    up, a = _fwd_up(h2, wup, convw8, convb, tm)
    ff, x2, e, n5, sg, dx3, loss_part = _fwd_out(a, wdown, x1, g4, p2d, wple, g5, wpg, tgt, tm)

    dpre, de, dx2, dff, du, dg5, dg4, dconvb, dconvw8 = _bwd_out(dx3, e, sg, x2, ff, g5, g4, wpg, wdown, up, convw8,
                                                                 convb, tm)
    dup, dx1, do, dpa, dpb, dgates, dya, dyb, dg3, dg2, dbg = _bwd_mid(du, convw8, wup, dx2, x1, g3, o, g2, wout, gate,
                                                                       pa, pb, wba, wbb, tm)
    dqs, dks, dvs, dsink_rows = _swa_bwd(qs, ks, vs, ya, dya, lse_a, swa_bias, sink_rows)
    dsink = dsink_rows[:, 0:SWA_GROUP, 0]
    dqm, dkm, dvm = _mla_bwd(qm, km, vm, dyb, lse_b, _mla_delta(yb, dyb))
    dz, dqb, dx, dgq, dgkv, dg1 = _bwd_in(dqs, dks, dvs, dqm, dkm, dvm, tabs, consts, cq, ckv, gq, gkv, wuq, wk, wv,
                                           dgates, win, x2d, g1, dx1, tm)

    dwk = _unpad_slots(_mm_tn("dw_k", ckvn, dkm), HEADS, NOPE_DIM, 1).reshape(KV_LORA, HEADS, NOPE_DIM)
    dwv = _unpad_slots(_mm_tn("dw_v", ckvn, dvm), HEADS, V_DIM, 1).reshape(KV_LORA, HEADS, V_DIM)
    grads = {
        "w_in": _unpad_w_in(_mm_tn("dw_in", h1, dz)),
        "w_uq": _unpad_slots(_mm_tn("dw_uq", cqn, dqb), HEADS, NOPE_DIM + ROPE_DIM, 1),
        "w_ukv": jnp.concatenate([dwk, dwv], axis=2).reshape(KV_LORA, HEADS * (NOPE_DIM + V_DIM)),
        "w_branch_a": _unpad_slots(_mm_tn("dw_branch_a", ya, dpa), HEADS, A_HEAD_DIM, 0),
        "w_branch_b": _unpad_slots(_mm_tn("dw_branch_b", yb, dpb), HEADS, V_DIM, 0),
        "w_out": _mm_tn("dw_out", mixed, do),
        "w_up": _mm_tn("dw_up", h2, dup),
        "w_down": _mm_tn("dw_down", a, dff),
        "w_ple_gate": _mm_tn("dw_ple_gate", n5, dpre),
        "w_ple": _mm_tn("dw_ple", p2d, de),
    }

    keep, send = _pack_grad_halves(grads, cc)
    pair = _add_pair(keep, _swap_sibling("swap_grad_halves", send))
    own = lax.dynamic_index_in_dim(pair, chip, 0, keepdims=True)
    reduced = _add_chips(lax.dynamic_update_slice(_scatter_chips(pair), own, (chip, 0, 0)))
    other = _swap_sibling("swap_reduced_halves", reduced)
    both = jnp.stack([jnp.where(cc == 0, reduced, other), jnp.where(cc == 0, other, reduced)])
    shard_grads = _unpack_shard_grads(both)

    small = {"attn_pre_norm": dg1, "attn_post_norm": dg2, "b_gate": dbg, "sinks": dsink, "q_a_norm": dgq,
             "kv_a_norm": dgkv, "mlp_pre_norm": dg3, "mlp_post_norm": dg4, "conv_b": dconvb, "ple_norm": dg5,
             "conv_w": dconvw8[0:3]}
    small_sum = _unpack_small(_add_devices(_gather_small("gather_small_grads", _pack_small(small))))
    for n in names:
        if n in small_sum and n != "conv_w":
            shard_grads[n] = small_sum[n].reshape(w2[n].shape)
    shard_grads["conv_w"] = lax.dynamic_index_in_dim(small_sum["conv_w"].reshape(3, 4, 1408), chip, 1, keepdims=False)

    loss = lax.psum(loss_part[0, 0], MESH_AXES)

    g_out, d_out, m_out, v_out = [], [], [], []
    for n in names:
        g = shard_grads[n]
        d, mn, vn = _adamw("adamw_" + n, w2[n], g, m2[n], v2[n])
        shp = wts[n].shape
        g_out.append(g.reshape(shp))
        d_out.append(d.reshape(shp))
        m_out.append(mn.reshape(shp))
        v_out.append(vn.reshape(shp))
    return (loss, dx.reshape(x.shape), *g_out, *d_out, *m_out, *v_out)
```

```python
import functools
import math

import numpy as np
import jax
import jax.numpy as jnp
from jax import lax
from jax.experimental import pallas as pl
from jax.experimental.pallas import tpu as pltpu

F32 = jnp.float32
BF16 = jnp.bfloat16

D_MODEL = 1024
D_FF = 2816
PLE_DIM = 256
ROPE_THETA = 10000.0
RMS_EPS = 1e-6
SWA_WINDOW = 128
HEADS = 8
A_KV_HEADS = 2
A_HEAD_DIM = 64
Q_LORA = 256
KV_LORA = 128
NOPE_DIM = 64
ROPE_DIM = 32
V_DIM = 64
LANES = 128
ZW = 4096
NEG = -1e30
SCALE_A = A_HEAD_DIM ** -0.5
SCALE_B = (NOPE_DIM + ROPE_DIM) ** -0.5

ADAM_LR = 0.001
ADAM_B1 = 0.9
ADAM_B2 = 0.999
ADAM_EPS = 1e-08
ADAM_WD = 0.01
ADAM_STEP = 10

VMEM_LIMIT = 60 * 1024 * 1024
MESH_AXES = ("x", "y", "c")
MESH = pl.DeviceIdType.MESH

Z_QA, Z_KA, Z_VA, Z_CQ, Z_CKV, Z_KR, Z_GATE = 0, 1024, 1280, 1536, 1792, 1920, 2048


def _dot(a, b):
    return jnp.dot(a, b, preferred_element_type=F32)


def _dot_nt(a, b):
    return lax.dot_general(a, b, (((1,), (1,)), ((), ())), preferred_element_type=F32)


def _dot_tn(a, b):
    return lax.dot_general(a, b, (((0,), (0,)), ((), ())), preferred_element_type=F32)


def _rms_stats(x):
    r = lax.rsqrt(jnp.mean(x * x, axis=-1, keepdims=True) + RMS_EPS)
    return x * r, r


def _rms_bwd(dy, xn, r, g):
    dxn = dy * g
    dx = r * (dxn - xn * jnp.mean(dxn * xn, axis=-1, keepdims=True))
    dg = jnp.sum(dy * xn, axis=0, keepdims=True)
    return dx, dg


def _tile_lanes(t, n):
    return t if n == 1 else jnp.concatenate([t] * n, axis=1)


def _rope(x, c, s1, s2, half):
    w = x.shape[1]
    n = w // LANES
    return (x * _tile_lanes(c, n) + pltpu.roll(x, w - half, 1) * _tile_lanes(s1, n)
            + pltpu.roll(x, half, 1) * _tile_lanes(s2, n))


def _rope_t(dy, c, s1, s2, half):
    w = dy.shape[1]
    n = w // LANES
    return (dy * _tile_lanes(c, n) + pltpu.roll(dy * _tile_lanes(s1, n), half, 1)
            + pltpu.roll(dy * _tile_lanes(s2, n), w - half, 1))


def _sigmoid(x):
    return 1.0 / (1.0 + jnp.exp(-x))


_GELU_C = math.sqrt(2.0 / math.pi)


def _gelu_and_grad(x):
    x2 = x * x
    th = jnp.tanh(_GELU_C * (x + 0.044715 * x * x2))
    gel = 0.5 * x * (1.0 + th)
    dgel = 0.5 * (1.0 + th) + 0.5 * x * (1.0 - th * th) * (_GELU_C * (1.0 + 3.0 * 0.044715 * x2))
    return gel, dgel


def _conv_taps(up, h6, h7):
    rows = lax.broadcasted_iota(jnp.int32, up.shape, 0)
    r1 = pltpu.roll(up, 1, 0)
    r2 = pltpu.roll(up, 2, 0)
    xm1 = jnp.where(rows == 0, h7, r1)
    xm2 = jnp.where(rows == 0, h6, jnp.where(rows == 1, h7, r2))
    return xm1, xm2


def _conv_taps_next(du, n0, n1):
    tm = du.shape[0]
    rows = lax.broadcasted_iota(jnp.int32, du.shape, 0)
    r1 = pltpu.roll(du, tm - 1, 0)
    r2 = pltpu.roll(du, tm - 2, 0)
    xp1 = jnp.where(rows == tm - 1, n0, r1)
    xp2 = jnp.where(rows == tm - 2, n0, jnp.where(rows == tm - 1, n1, r2))
    return xp1, xp2


def _row(tm, n):
    return pl.BlockSpec((tm, n), lambda i: (i, 0))


def _full(shape):
    nd = len(shape)
    return pl.BlockSpec(tuple(shape), lambda i: (0,) * nd)


def _heads(tm, h):
    return pl.BlockSpec((h, tm, LANES), lambda i: (0, i, 0))


def _rows_call(name, body, t_rows, tm, ins, outs, scratch=()):
    return pl.pallas_call(
        body, name=name, grid=(t_rows // tm,),
        in_specs=[s for _, s in ins],
        out_specs=[s for _, s in outs],
        out_shape=[s for s, _ in outs],
        scratch_shapes=list(scratch),
        compiler_params=pltpu.CompilerParams(dimension_semantics=("arbitrary",), vmem_limit_bytes=VMEM_LIMIT),
    )(*[a for a, _ in ins])


def _sds(shape, dtype):
    return jax.ShapeDtypeStruct(tuple(shape), dtype)


def _rope_consts():
    c = np.zeros((16, LANES), np.float32)
    lane = np.arange(LANES)
    inv_a = (ROPE_THETA ** (-(np.arange(0, A_HEAD_DIM, 2, dtype=np.float32) / A_HEAD_DIM))).astype(np.float32)
    in_a = lane < A_HEAD_DIM
    c[0, in_a] = inv_a[lane[in_a] % (A_HEAD_DIM // 2)]
    c[1, in_a] = 1.0
    c[2, lane < A_HEAD_DIM // 2] = -1.0
    c[3, (lane >= A_HEAD_DIM // 2) & in_a] = 1.0
    inv_b = (ROPE_THETA ** (-(np.arange(0, ROPE_DIM, 2, dtype=np.float32) / ROPE_DIM))).astype(np.float32)
    pe = (lane >= NOPE_DIM) & (lane < NOPE_DIM + ROPE_DIM)
    c[5, pe] = inv_b[(lane[pe] - NOPE_DIM) % (ROPE_DIM // 2)]
    c[6, pe] = 1.0
    c[7, (lane >= NOPE_DIM) & (lane < NOPE_DIM + ROPE_DIM // 2)] = -1.0
    c[8, (lane >= NOPE_DIM + ROPE_DIM // 2) & (lane < NOPE_DIM + ROPE_DIM)] = 1.0
    c[9, lane < NOPE_DIM] = 1.0
    c[10, pe] = 1.0
    return jnp.asarray(c)


def _rope_tables(pos_f, consts, tm):
    t_rows = pos_f.shape[0]

    def body(pos_ref, c_ref, ca, sa1, sa2, cb, sb1, sb2):
        pos = pos_ref[...]
        ang = pos * c_ref[0:1, :]
        cs, sn = jnp.cos(ang), jnp.sin(ang)
        ca[...] = cs * c_ref[1:2, :]
        sa1[...] = sn * c_ref[2:3, :]
        sa2[...] = sn * c_ref[3:4, :]
        ang = pos * c_ref[5:6, :]
        cs, sn = jnp.cos(ang), jnp.sin(ang)
        cb[...] = cs * c_ref[6:7, :] + c_ref[9:10, :]
        sb1[...] = sn * c_ref[7:8, :]
        sb2[...] = sn * c_ref[8:9, :]

    tab = (_sds((t_rows, LANES), F32), _row(tm, LANES))
    return _rows_call("rope_tables", body, t_rows, tm,
                      [(pos_f, _row(tm, 1)), (consts, _full(consts.shape))], [tab] * 6)


def _fwd_in(x, g1, win, bg, gq, gkv, wuq, wk, wv, tabs, tm):
    t_rows = x.shape[0]

    def body(x_ref, g1_ref, win_ref, bg_ref, gq_ref, gkv_ref, wuq_ref, wk_ref, wv_ref,
             ca, sa1, sa2, cb, sb1, sb2,
             h1_ref, qs_ref, ks_ref, vs_ref, cq_ref, cqn_ref, ckv_ref, ckvn_ref, qm_ref, km_ref, vm_ref, gate_ref):
        xn, _ = _rms_stats(x_ref[...])
        hb = (xn * g1_ref[...]).astype(BF16)
        h1_ref[...] = hb
        ta = (ca[...], sa1[...], sa2[...])
        tb = (cb[...], sb1[...], sb2[...])
        qs_ref[...] = (_rope(_dot(hb, win_ref[:, Z_QA:Z_KA]), *ta, A_HEAD_DIM // 2) * SCALE_A).astype(BF16)
        ks_ref[...] = _rope(_dot(hb, win_ref[:, Z_KA:Z_VA]), *ta, A_HEAD_DIM // 2).astype(BF16)
        vs_ref[...] = _dot(hb, win_ref[:, Z_VA:Z_CQ]).astype(BF16)
        cq = _dot(hb, win_ref[:, Z_CQ:Z_CKV])
        cq_ref[...] = cq
        cqn, _ = _rms_stats(cq)
        cqb = (cqn * gq_ref[...]).astype(BF16)
        cqn_ref[...] = cqb
        qm_ref[...] = (_rope(_dot(cqb, wuq_ref[...]), *tb, ROPE_DIM // 2) * SCALE_B).astype(BF16)
        ckv = _dot(hb, win_ref[:, Z_CKV:Z_KR])
        ckv_ref[...] = ckv
        ckvn, _ = _rms_stats(ckv)
        ckvb = (ckvn * gkv_ref[...]).astype(BF16)
        ckvn_ref[...] = ckvb
        kpe = _rope(_dot(hb, win_ref[:, Z_KR:Z_GATE]), *tb, ROPE_DIM // 2)
        km_ref[...] = (_dot(ckvb, wk_ref[...]) + _tile_lanes(kpe, HEADS)).astype(BF16)
        vm_ref[...] = _dot(ckvb, wv_ref[...]).astype(BF16)
        gate_ref[...] = _sigmoid(_dot(hb, win_ref[:, Z_GATE:ZW]) + bg_ref[...])

    def o(n, dt):
        return (_sds((t_rows, n), dt), _row(tm, n))

    ins = [(x, _row(tm, D_MODEL)), (g1, _full(g1.shape)), (win, _full(win.shape)), (bg, _full(bg.shape)),
           (gq, _full(gq.shape)), (gkv, _full(gkv.shape)), (wuq, _full(wuq.shape)), (wk, _full(wk.shape)),
           (wv, _full(wv.shape))] + [(t, _row(tm, LANES)) for t in tabs]
    outs = [o(1024, BF16), o(1024, BF16), o(256, BF16), o(256, BF16), o(256, F32), o(256, BF16), o(128, F32),
            o(128, BF16), o(1024, BF16), o(1024, BF16), o(1024, BF16), o(2048, F32)]
    return _rows_call("fwd_in", body, t_rows, tm, ins, outs)


def _attn_tile(t_rows):
    return min(512, t_rows)


MLA_HEADS_PER_STEP = 2


def _causal_pairs(nq, by_kv):
    if by_kv:
        pairs = [(i, j) for j in range(nq) for i in range(j, nq)]
    else:
        pairs = [(i, j) for i in range(nq) for j in range(i + 1)]
    return (jnp.asarray([p[0] for p in pairs], jnp.int32), jnp.asarray([p[1] for p in pairs], jnp.int32))


def _mla_fwd(q, k, v):
    t_rows = q.shape[0]
    t = _attn_tile(t_rows)
    hp = MLA_HEADS_PER_STEP
    w = hp * LANES
    ii, jj = _causal_pairs(t_rows // t, by_kv=False)

    def body(i_ref, j_ref, q_ref, k_ref, v_ref, o_ref, lse_ref, m_s, l_s, acc_s):
        i = i_ref[pl.program_id(1)]
        j = j_ref[pl.program_id(1)]

        @pl.when(j == 0)
        def _():
            m_s[...] = jnp.full(m_s.shape, NEG, F32)
            l_s[...] = jnp.zeros(l_s.shape, F32)
            acc_s[...] = jnp.zeros(acc_s.shape, F32)

        def step(diagonal):
            for hh in range(hp):
                sl = slice(hh * LANES, (hh + 1) * LANES)
                s = _dot_nt(k_ref[:, sl], q_ref[:, sl])
                if diagonal:
                    valid = (lax.broadcasted_iota(jnp.int32, (t, t), 0) <= lax.broadcasted_iota(jnp.int32, (t, t), 1))
                    s = jnp.where(valid, s, NEG)
                m_prev = m_s[hh]
                m_new = jnp.maximum(m_prev, jnp.max(s, axis=0, keepdims=True))
                p = jnp.exp(s - m_new)
                alpha = jnp.exp(m_prev - m_new)
                l_new = alpha * l_s[hh] + jnp.sum(p, axis=0, keepdims=True)
                acc = alpha * acc_s[hh] + _dot_tn(v_ref[:, sl], p.astype(BF16))
                if diagonal:
                    o_ref[:, sl] = (acc / l_new).T.astype(o_ref.dtype)
                    lse_ref[hh] = m_new + jnp.log(l_new)
                else:
                    m_s[hh] = m_new
                    l_s[hh] = l_new
                    acc_s[hh] = acc

        pl.when(j < i)(lambda: step(False))
        pl.when(j == i)(lambda: step(True))

    grid_spec = pltpu.PrefetchScalarGridSpec(
        num_scalar_prefetch=2, grid=(HEADS // hp, ii.shape[0]),
        in_specs=[pl.BlockSpec((t, w), lambda hb, s, ir, jr: (ir[s], hb)),
                  pl.BlockSpec((t, w), lambda hb, s, ir, jr: (jr[s], hb)),
                  pl.BlockSpec((t, w), lambda hb, s, ir, jr: (jr[s], hb))],
        out_specs=[pl.BlockSpec((t, w), lambda hb, s, ir, jr: (ir[s], hb)),
                   pl.BlockSpec((hp, 1, t), lambda hb, s, ir, jr: (hb, 0, ir[s]))],
        scratch_shapes=[pltpu.VMEM((hp, 1, t), F32), pltpu.VMEM((hp, 1, t), F32), pltpu.VMEM((hp, LANES, t), F32)])
    return pl.pallas_call(
        body, name="mla_fwd", grid_spec=grid_spec,
        out_shape=[_sds((t_rows, HEADS * LANES), BF16), _sds((HEADS, 1, t_rows), F32)],
        compiler_params=pltpu.CompilerParams(dimension_semantics=("arbitrary",) * 2, vmem_limit_bytes=VMEM_LIMIT),
    )(ii, jj, q, k, v)


def _mla_delta(o, do):
    t_rows = o.shape[0]
    t = _attn_tile(t_rows)

    def body(o_ref, do_ref, dl_ref):
        prod = o_ref[...].astype(F32) * do_ref[...].astype(F32)
        dl_ref[0] = jnp.sum(prod.T, axis=0, keepdims=True)

    return pl.pallas_call(
        body, name="mla_delta", grid=(HEADS, t_rows // t),
        in_specs=[pl.BlockSpec((t, LANES), lambda h, i: (i, h)), pl.BlockSpec((t, LANES), lambda h, i: (i, h))],
        out_specs=pl.BlockSpec((1, 1, t), lambda h, i: (h, 0, i)),
        out_shape=_sds((HEADS, 1, t_rows), F32),
    )(o, do)


def _mla_bwd(q, k, v, do, lse, delta):
    t_rows = q.shape[0]
    t = _attn_tile(t_rows)
    hp = MLA_HEADS_PER_STEP
    w = hp * LANES
    ii, jj = _causal_pairs(t_rows // t, by_kv=True)

    def body(i_ref, j_ref, q_ref, k_ref, v_ref, do_ref, lse_ref, dl_ref, dq_ref, dk_ref, dv_ref):
        i = i_ref[pl.program_id(1)]
        j = j_ref[pl.program_id(1)]

        @pl.when(pl.program_id(1) == 0)
        def _():
            dq_ref[...] = jnp.zeros(dq_ref.shape, F32)

        def step(diagonal):
            r0 = pl.multiple_of(i * t, t)
            for hh in range(hp):
                sl = slice(hh * LANES, (hh + 1) * LANES)
                qv = q_ref[:, sl]
                kv = k_ref[:, sl]
                dov = do_ref[:, sl]
                s = _dot_nt(kv, qv)
                if diagonal:
                    valid = (lax.broadcasted_iota(jnp.int32, (t, t), 0) <= lax.broadcasted_iota(jnp.int32, (t, t), 1))
                    s = jnp.where(valid, s, NEG)
                p = jnp.exp(s - lse_ref[hh])
                dv = _dot(p.astype(BF16), dov)
                dp = _dot_nt(v_ref[:, sl], dov)
                ds = (p * (dp - dl_ref[hh])).astype(BF16)
                dk = _dot(ds, qv)
                if diagonal:
                    dv_ref[:, sl] = dv
                    dk_ref[:, sl] = dk
                else:
                    dv_ref[:, sl] += dv
                    dk_ref[:, sl] += dk
                dq_ref[hh, pl.ds(r0, t), :] += _dot_tn(ds, kv)

        pl.when(i > j)(lambda: step(False))
        pl.when(i == j)(lambda: step(True))

    def qmap(hb, s, ir, jr):
        return (ir[s], hb)

    def kvmap(hb, s, ir, jr):
        return (jr[s], hb)

    def rowmap(hb, s, ir, jr):
        return (hb, 0, ir[s])

    grid_spec = pltpu.PrefetchScalarGridSpec(
        num_scalar_prefetch=2, grid=(HEADS // hp, ii.shape[0]),
        in_specs=[pl.BlockSpec((t, w), qmap), pl.BlockSpec((t, w), kvmap), pl.BlockSpec((t, w), kvmap),
                  pl.BlockSpec((t, w), qmap), pl.BlockSpec((hp, 1, t), rowmap), pl.BlockSpec((hp, 1, t), rowmap)],
        out_specs=[pl.BlockSpec((hp, t_rows, LANES), lambda hb, s, ir, jr: (hb, 0, 0)),
                   pl.BlockSpec((t, w), kvmap), pl.BlockSpec((t, w), kvmap)])
    return pl.pallas_call(
        body, name="mla_bwd", grid_spec=grid_spec,
        out_shape=[_sds((HEADS, t_rows, LANES), F32), _sds((t_rows, HEADS * LANES), F32),
                   _sds((t_rows, HEADS * LANES), F32)],
        compiler_params=pltpu.CompilerParams(dimension_semantics=("arbitrary",) * 2, vmem_limit_bytes=VMEM_LIMIT),
    )(ii, jj, q, k, v, do, lse, delta)


SWA_TILE = 2 * SWA_WINDOW
SWA_GROUP = HEADS // A_KV_HEADS


def _swa_bias(tq):
    koff = lax.broadcasted_iota(jnp.int32, (tq + SWA_WINDOW, SWA_GROUP * tq), 0) - SWA_WINDOW
    qoff = (lax.broadcasted_iota(jnp.int32, (tq + SWA_WINDOW, SWA_GROUP * tq), 1) % tq)
    band = (koff <= qoff) & (qoff - koff < SWA_WINDOW)
    return jnp.stack([jnp.where(band & (koff >= 0), 0.0, NEG), jnp.where(band, 0.0, NEG)]).astype(F32)


def _swa_specs(tq, nq):
    wb = tq // SWA_WINDOW

    def qi(i):
        return jnp.minimum(i, nq - 1)

    q = pl.BlockSpec((tq, SWA_GROUP * LANES), lambda h, i: (qi(i), h))
    cur = pl.BlockSpec((tq, LANES), lambda h, i: (qi(i), h))
    prev = pl.BlockSpec((SWA_WINDOW, LANES), lambda h, i: (jnp.maximum(qi(i) * wb - 1, 0), h))
    bias = pl.BlockSpec((1, tq + SWA_WINDOW, SWA_GROUP * tq), lambda h, i: (jnp.minimum(i, 1), 0, 0))
    rows = pl.BlockSpec((1, 1, 1, SWA_GROUP * tq), lambda h, i: (h, qi(i), 0, 0))
    sink = pl.BlockSpec((1, 1, SWA_GROUP * tq), lambda h, i: (h, 0, 0))
    return q, cur, prev, bias, rows, sink


def _stack_heads(ref):
    return jnp.concatenate([ref[:, g * LANES:(g + 1) * LANES] for g in range(SWA_GROUP)], axis=0)


def _swa_fwd(q, k, v, bias, sink_rows):
    t_rows = q.shape[0]
    tq = min(SWA_TILE, t_rows)
    nq = t_rows // tq
    qs_, cur, prev, bs, rows, sk = _swa_specs(tq, nq)

    def body(q_ref, kc_ref, kp_ref, vc_ref, vp_ref, b_ref, sink_ref, o_ref, lse_ref):
        qs = _stack_heads(q_ref)
        kk = jnp.concatenate([kp_ref[...], kc_ref[...]], axis=0)
        vv = jnp.concatenate([vp_ref[...], vc_ref[...]], axis=0)
        s = _dot_nt(kk, qs) + b_ref[0]
        sink = sink_ref[0]
        m = jnp.maximum(jnp.max(s, axis=0, keepdims=True), sink)
        p = jnp.exp(s - m)
        l = jnp.sum(p, axis=0, keepdims=True) + jnp.exp(sink - m)
        o = (_dot_tn(vv, p.astype(BF16)) / l).T
        for g in range(SWA_GROUP):
            o_ref[:, g * LANES:(g + 1) * LANES] = o[g * tq:(g + 1) * tq].astype(o_ref.dtype)
        lse_ref[0, 0] = m + jnp.log(l)

    return pl.pallas_call(
        body, name="swa_fwd", grid=(A_KV_HEADS, nq),
        in_specs=[qs_, cur, prev, cur, prev, bs, sk],
        out_specs=[qs_, rows],
        out_shape=[_sds((t_rows, HEADS * LANES), BF16), _sds((A_KV_HEADS, nq, 1, SWA_GROUP * tq), F32)],
        compiler_params=pltpu.CompilerParams(dimension_semantics=("arbitrary",) * 2, vmem_limit_bytes=VMEM_LIMIT),
    )(q, k, k, v, v, bias, sink_rows)


def _swa_bwd(q, k, v, o, do, lse, bias, sink_rows):
    t_rows = q.shape[0]
    tq = min(SWA_TILE, t_rows)
    nq = t_rows // tq
    qs_, cur, prev, bs, rows, sk = _swa_specs(tq, nq)
    hw = SWA_WINDOW

    def body(q_ref, kc_ref, kp_ref, vc_ref, vp_ref, o_ref, do_ref, lse_ref, b_ref, sink_ref,
             dq_ref, dk_ref, dv_ref, dsink_ref, ck, cv, dsa):
        i = pl.program_id(1)

        @pl.when(i == 0)
        def _():
            dsa[...] = jnp.zeros(dsa.shape, F32)

        @pl.when(i < nq)
        def _():
            qs = _stack_heads(q_ref)
            dos = _stack_heads(do_ref)
            kk = jnp.concatenate([kp_ref[...], kc_ref[...]], axis=0)
            vv = jnp.concatenate([vp_ref[...], vc_ref[...]], axis=0)
            lse = lse_ref[0, 0]
            p = jnp.exp(_dot_nt(kk, qs) + b_ref[0] - lse)
            delta = jnp.sum((_stack_heads(o_ref).astype(F32) * dos.astype(F32)).T, axis=0, keepdims=True)
            dsa[...] += -jnp.exp(sink_ref[0] - lse) * delta
            dv = _dot(p.astype(BF16), dos)
            ds = (p * (_dot_nt(vv, dos) - delta)).astype(BF16)
            dk = _dot(ds, qs)
            dq = _dot_tn(ds, kk)
            for g in range(SWA_GROUP):
                dq_ref[:, g * LANES:(g + 1) * LANES] = dq[g * tq:(g + 1) * tq]

            @pl.when(i > 0)
            def _():
                dk_ref[0:tq - hw, :] = ck[0:tq - hw, :]
                dk_ref[tq - hw:tq, :] = ck[tq - hw:tq, :] + dk[0:hw]
                dv_ref[0:tq - hw, :] = cv[0:tq - hw, :]
                dv_ref[tq - hw:tq, :] = cv[tq - hw:tq, :] + dv[0:hw]

            ck[...] = dk[hw:hw + tq]
            cv[...] = dv[hw:hw + tq]

        @pl.when(i == nq)
        def _():
            dk_ref[...] = ck[...]
            dv_ref[...] = cv[...]
            dsink_ref[...] = jnp.zeros(dsink_ref.shape, F32)
            for g in range(SWA_GROUP):
                tot = jnp.sum(dsa[:, g * tq:(g + 1) * tq], axis=1, keepdims=True)
                dsink_ref[0, g:g + 1, :] = jnp.zeros((1, LANES), F32) + tot

    kv_out = pl.BlockSpec((tq, LANES), lambda h, i: (jnp.maximum(i - 1, 0), h))
    return pl.pallas_call(
        body, name="swa_bwd", grid=(A_KV_HEADS, nq + 1),
        in_specs=[qs_, cur, prev, cur, prev, qs_, qs_, rows, bs, sk],
        out_specs=[qs_, kv_out, kv_out, pl.BlockSpec((1, 8, LANES), lambda h, i: (h, 0, 0))],
        out_shape=[_sds((t_rows, HEADS * LANES), F32), _sds((t_rows, A_KV_HEADS * LANES), F32),
                   _sds((t_rows, A_KV_HEADS * LANES), F32), _sds((A_KV_HEADS, 8, LANES), F32)],
        scratch_shapes=[pltpu.VMEM((tq, LANES), F32), pltpu.VMEM((tq, LANES), F32),
                        pltpu.VMEM((1, SWA_GROUP * tq), F32)],
        compiler_params=pltpu.CompilerParams(dimension_semantics=("arbitrary",) * 2, vmem_limit_bytes=VMEM_LIMIT),
    )(q, k, k, v, v, o, do, lse, bias, sink_rows)


def _fwd_mix(x, ya, yb, gate, wba, wbb, wout, g2, g3, tm):
    t_rows = x.shape[0]

    def body(x_ref, ya_ref, yb_ref, gate_ref, wba_ref, wbb_ref, wout_ref, g2_ref, g3_ref,
             pa_ref, pb_ref, mixed_ref, o_ref, x1_ref, h2_ref):
        pa = _dot(ya_ref[...], wba_ref[...])
        pb = _dot(yb_ref[...], wbb_ref[...])
        pa_ref[...] = pa
        pb_ref[...] = pb
        mixed = (gate_ref[:, 0:D_MODEL] * pa + gate_ref[:, D_MODEL:2 * D_MODEL] * pb).astype(BF16)
        mixed_ref[...] = mixed
        o = _dot(mixed, wout_ref[...])
        o_ref[...] = o
        on, _ = _rms_stats(o)
        x1 = x_ref[...] + on * g2_ref[...]
        x1_ref[...] = x1
        x1n, _ = _rms_stats(x1)
        h2_ref[...] = (x1n * g3_ref[...]).astype(BF16)

    def o_(dt):
        return (_sds((t_rows, D_MODEL), dt), _row(tm, D_MODEL))

    ins = [(x, _row(tm, D_MODEL)), (ya, _row(tm, 1024)), (yb, _row(tm, 1024)), (gate, _row(tm, 2048)),
           (wba, _full(wba.shape)), (wbb, _full(wbb.shape)), (wout, _full(wout.shape)),
           (g2, _full(g2.shape)), (g3, _full(g3.shape))]
    return _rows_call("fwd_mix", body, t_rows, tm, ins, [o_(F32), o_(F32), o_(BF16), o_(F32), o_(F32), o_(BF16)])


CONV_CHUNK = 1408


def _fwd_up(h2, wup, convw8, convb, tm):
    t_rows = h2.shape[0]
    cdim = 2 * D_FF

    def body(h2_ref, wup_ref, cw_ref, cb_ref, up_ref, a_ref, carry):
        i = pl.program_id(0)

        @pl.when(i == 0)
        def _():
            carry[...] = jnp.zeros(carry.shape, F32)

        hb = h2_ref[...]

        def conv(c0):
            sl = slice(c0, c0 + CONV_CHUNK)
            up = _dot(hb, wup_ref[:, sl])
            up_ref[:, sl] = up
            xm1, xm2 = _conv_taps(up, carry[6:7, sl], carry[7:8, sl])
            u = cw_ref[0:1, sl] * xm2 + cw_ref[1:2, sl] * xm1 + cw_ref[2:3, sl] * up + cb_ref[:, sl]
            carry[:, sl] = up[tm - 8:tm, :]
            return u

        for c0 in range(0, D_FF, CONV_CHUNK):
            ug = conv(c0)
            uv = conv(D_FF + c0)
            gel, _ = _gelu_and_grad(ug)
            a_ref[:, c0:c0 + CONV_CHUNK] = (gel * uv).astype(BF16)

    ins = [(h2, _row(tm, D_MODEL)), (wup, _full(wup.shape)), (convw8, _full(convw8.shape)), (convb, _full(convb.shape))]
    outs = [(_sds((t_rows, cdim), F32), _row(tm, cdim)), (_sds((t_rows, D_FF), BF16), _row(tm, D_FF))]
    return _rows_call("fwd_up", body, t_rows, tm, ins, outs, scratch=[pltpu.VMEM((8, cdim), F32)])


def _fwd_out(a, wdown, x1, g4, p, wple, g5, wpg, tgt, tm):
    t_rows = a.shape[0]

    def body(a_ref, wdown_ref, x1_ref, g4_ref, p_ref, wple_ref, g5_ref, wpg_ref, tgt_ref,
             ff_ref, x2_ref, e_ref, n5_ref, sg_ref, dx3_ref, loss_ref):
        i = pl.program_id(0)
        ff = _dot(a_ref[...], wdown_ref[...])
        ff_ref[...] = ff
        ffn, _ = _rms_stats(ff)
        x2 = x1_ref[...] + ffn * g4_ref[...]
        x2_ref[...] = x2
        e = _dot(p_ref[...].astype(BF16), wple_ref[...])
        e_ref[...] = e
        x2n, _ = _rms_stats(x2)
        n5 = (x2n * g5_ref[...]).astype(BF16)
        n5_ref[...] = n5
        sg = _sigmoid(_dot(n5, wpg_ref[...]))
        sg_ref[...] = sg
        d = x2 + sg * e - tgt_ref[...]
        dx3_ref[...] = d * (1.0 / D_MODEL)

        @pl.when(i == 0)
        def _():
            loss_ref[...] = jnp.zeros((1, 1), F32)

        loss_ref[...] += 0.5 * jnp.sum(jnp.sum(d * d, axis=1, keepdims=True), axis=0, keepdims=True) * (1.0 / D_MODEL)

    def o_(dt):
        return (_sds((t_rows, D_MODEL), dt), _row(tm, D_MODEL))

    ins = [(a, _row(tm, D_FF)), (wdown, _full(wdown.shape)), (x1, _row(tm, D_MODEL)), (g4, _full(g4.shape)),
           (p, _row(tm, PLE_DIM)), (wple, _full(wple.shape)), (g5, _full(g5.shape)), (wpg, _full(wpg.shape)),
           (tgt, _row(tm, D_MODEL))]
    outs = [o_(F32), o_(F32), o_(F32), o_(BF16), o_(F32), o_(F32), (_sds((1, 1), F32), _full((1, 1)))]
    return _rows_call("fwd_out", body, t_rows, tm, ins, outs)


def _bwd_out(dx3, e, sg, x2, ff, g5, g4, wpg, wdown, up, convw8, convb, tm):
    t_rows = dx3.shape[0]
    cdim = 2 * D_FF
    hb = tm // 8

    def body(dx3_ref, e_ref, sg_ref, x2_ref, ff_ref, g5_ref, g4_ref, wpg_ref, wdown_ref, up_ref, halo_ref, cw_ref,
             cb_ref, dpre_ref, de_ref, dx2_ref, dff_ref, du_ref, dg5_ref, dg4_ref, dcb_ref, dcw_ref):
        i = pl.program_id(0)

        @pl.when(i == 0)
        def _():
            dg5_ref[...] = jnp.zeros(dg5_ref.shape, F32)
            dg4_ref[...] = jnp.zeros(dg4_ref.shape, F32)
            dcb_ref[...] = jnp.zeros(dcb_ref.shape, F32)
            dcw_ref[...] = jnp.zeros(dcw_ref.shape, F32)

        dx3 = dx3_ref[...]
        sg = sg_ref[...]
        dpre = (dx3 * e_ref[...] * sg * (1.0 - sg)).astype(BF16)
        dpre_ref[...] = dpre
        de_ref[...] = (dx3 * sg).astype(BF16)
        dn5 = _dot_nt(dpre, wpg_ref[...])
        x2n, r5 = _rms_stats(x2_ref[...])
        d2, dg5 = _rms_bwd(dn5, x2n, r5, g5_ref[...])
        dx2 = dx3 + d2
        dx2_ref[...] = dx2
        dg5_ref[...] += dg5
        ffn, r4 = _rms_stats(ff_ref[...])
        dff, dg4 = _rms_bwd(dx2, ffn, r4, g4_ref[...])
        dg4_ref[...] += dg4
        dffb = dff.astype(BF16)
        dff_ref[...] = dffb
        keep = jnp.where(i > 0, 1.0, 0.0)

        def conv(c0):
            sl = slice(c0, c0 + CONV_CHUNK)
            up = up_ref[:, sl]
            xm1, xm2 = _conv_taps(up, halo_ref[6:7, sl] * keep, halo_ref[7:8, sl] * keep)
            u = cw_ref[0:1, sl] * xm2 + cw_ref[1:2, sl] * xm1 + cw_ref[2:3, sl] * up + cb_ref[:, sl]
            return u, up, xm1, xm2

        def grads(c0, du, up, xm1, xm2):
            sl = slice(c0, c0 + CONV_CHUNK)
            du_ref[:, sl] = du
            dcb_ref[:, sl] += jnp.sum(du, axis=0, keepdims=True)
            dcw_ref[0:1, sl] += jnp.sum(du * xm2, axis=0, keepdims=True)
            dcw_ref[1:2, sl] += jnp.sum(du * xm1, axis=0, keepdims=True)
            dcw_ref[2:3, sl] += jnp.sum(du * up, axis=0, keepdims=True)

        for c0 in range(0, D_FF, CONV_CHUNK):
            da = _dot_nt(dffb, wdown_ref[c0:c0 + CONV_CHUNK, :])
            ug, *rg = conv(c0)
            uv, *rv = conv(D_FF + c0)
            gel, dgel = _gelu_and_grad(ug)
            grads(c0, da * uv * dgel, *rg)
            grads(D_FF + c0, da * gel, *rv)

    def o_(n, dt):
        return (_sds((t_rows, n), dt), _row(tm, n))

    def acc(r, n):
        return (_sds((r, n), F32), _full((r, n)))

    halo = pl.BlockSpec((8, cdim), lambda i: (jnp.maximum(i * hb - 1, 0), 0))
    ins = [(dx3, _row(tm, D_MODEL)), (e, _row(tm, D_MODEL)), (sg, _row(tm, D_MODEL)), (x2, _row(tm, D_MODEL)),
           (ff, _row(tm, D_MODEL)), (g5, _full(g5.shape)), (g4, _full(g4.shape)), (wpg, _full(wpg.shape)),
           (wdown, _full(wdown.shape)), (up, _row(tm, cdim)), (up, halo), (convw8, _full(convw8.shape)),
           (convb, _full(convb.shape))]
    outs = [o_(D_MODEL, BF16), o_(D_MODEL, BF16), o_(D_MODEL, F32), o_(D_MODEL, BF16), o_(cdim, F32),
            acc(1, D_MODEL), acc(1, D_MODEL), acc(1, cdim), acc(8, cdim)]
    return _rows_call("bwd_out", body, t_rows, tm, ins, outs)


def _bwd_mid(du, convw8, wup, dx2, x1, g3, o, g2, wout, gate, pa, pb, wba, wbb, tm):
    t_rows = du.shape[0]
    cdim = 2 * D_FF
    hb = tm // 8
    last_blk = t_rows // 8 - 1
    n_tiles = t_rows // tm

    def body(du_ref, halo_ref, cw_ref, wup_ref, dx2_ref, x1_ref, g3_ref, o_ref, g2_ref, wout_ref, gate_ref, pa_ref,
             pb_ref, wba_ref, wbb_ref,
             dup_ref, dx1_ref, do_ref, dpa_ref, dpb_ref, dgt_ref, dya_ref, dyb_ref, dg3_ref, dg2_ref, dbg_ref):
        i = pl.program_id(0)

        @pl.when(i == 0)
        def _():
            dg3_ref[...] = jnp.zeros(dg3_ref.shape, F32)
            dg2_ref[...] = jnp.zeros(dg2_ref.shape, F32)
            dbg_ref[...] = jnp.zeros(dbg_ref.shape, F32)

        keep = jnp.where(i < n_tiles - 1, 1.0, 0.0)
        dh2 = jnp.zeros((tm, D_MODEL), F32)
        for c0 in range(0, cdim, CONV_CHUNK):
            sl = slice(c0, c0 + CONV_CHUNK)
            du = du_ref[:, sl]
            xp1, xp2 = _conv_taps_next(du, halo_ref[0:1, sl] * keep, halo_ref[1:2, sl] * keep)
            dup = (cw_ref[2:3, sl] * du + cw_ref[1:2, sl] * xp1 + cw_ref[0:1, sl] * xp2).astype(BF16)
            dup_ref[:, sl] = dup
            dh2 = dh2 + _dot_nt(dup, wup_ref[:, sl])
        x1n, r3 = _rms_stats(x1_ref[...])
        d1, dg3 = _rms_bwd(dh2, x1n, r3, g3_ref[...])
        dx1 = dx2_ref[...] + d1
        dx1_ref[...] = dx1
        dg3_ref[...] += dg3
        on, r2 = _rms_stats(o_ref[...])
        do, dg2 = _rms_bwd(dx1, on, r2, g2_ref[...])
        dg2_ref[...] += dg2
        dob = do.astype(BF16)
        do_ref[...] = dob
        dmixed = _dot_nt(dob, wout_ref[...])
        ga = gate_ref[:, 0:D_MODEL]
        gb = gate_ref[:, D_MODEL:2 * D_MODEL]
        dpa = (dmixed * ga).astype(BF16)
        dpb = (dmixed * gb).astype(BF16)
        dpa_ref[...] = dpa
        dpb_ref[...] = dpb
        dga = dmixed * pa_ref[...] * ga * (1.0 - ga)
        dgb = dmixed * pb_ref[...] * gb * (1.0 - gb)
        dgt_ref[:, 0:D_MODEL] = dga.astype(BF16)
        dgt_ref[:, D_MODEL:2 * D_MODEL] = dgb.astype(BF16)
        dbg_ref[:, 0:D_MODEL] += jnp.sum(dga, axis=0, keepdims=True)
        dbg_ref[:, D_MODEL:2 * D_MODEL] += jnp.sum(dgb, axis=0, keepdims=True)
        dya_ref[...] = _dot_nt(dpa, wba_ref[...]).astype(BF16)
        dyb_ref[...] = _dot_nt(dpb, wbb_ref[...]).astype(BF16)

    def o_(n, dt):
        return (_sds((t_rows, n), dt), _row(tm, n))

    def acc(r, n):
        return (_sds((r, n), F32), _full((r, n)))

    halo = pl.BlockSpec((8, cdim), lambda i: (jnp.minimum((i + 1) * hb, last_blk), 0))
    ins = [(du, _row(tm, cdim)), (du, halo), (convw8, _full(convw8.shape)), (wup, _full(wup.shape)),
           (dx2, _row(tm, D_MODEL)), (x1, _row(tm, D_MODEL)), (g3, _full(g3.shape)), (o, _row(tm, D_MODEL)),
           (g2, _full(g2.shape)), (wout, _full(wout.shape)), (gate, _row(tm, 2048)), (pa, _row(tm, D_MODEL)),
           (pb, _row(tm, D_MODEL)), (wba, _full(wba.shape)), (wbb, _full(wbb.shape))]
    outs = [o_(cdim, BF16), o_(D_MODEL, F32), o_(D_MODEL, BF16), o_(D_MODEL, BF16), o_(D_MODEL, BF16),
            o_(2048, BF16), o_(1024, BF16), o_(1024, BF16), acc(1, D_MODEL), acc(1, D_MODEL), acc(1, 2048)]
    return _rows_call("bwd_mid", body, t_rows, tm, ins, outs)


def _bwd_in(dqs, dks, dvs, dqm, dkm, dvm, tabs, consts, cq, ckv, gq, gkv, wuq, wk, wv, dgates, win, x, g1, dx1, tm):
    t_rows = x.shape[0]

    def body(dqs_ref, dks_ref, dvs_ref, dqm_ref, dkm_ref, dvm_ref, ca, sa1, sa2, cb, sb1, sb2, c_ref, cq_ref,
             ckv_ref, gq_ref, gkv_ref, wuq_ref, wk_ref, wv_ref, dgt_ref, win_ref, x_ref, g1_ref, dx1_ref,
             dz_ref, dqb_ref, dx_ref, dgq_ref, dgkv_ref, dg1_ref):
        i = pl.program_id(0)

        @pl.when(i == 0)
        def _():
            dgq_ref[...] = jnp.zeros(dgq_ref.shape, F32)
            dgkv_ref[...] = jnp.zeros(dgkv_ref.shape, F32)
            dg1_ref[...] = jnp.zeros(dg1_ref.shape, F32)

        ta = (ca[...], sa1[...], sa2[...])
        tb = (cb[...], sb1[...], sb2[...])
        dz_ref[:, Z_QA:Z_KA] = _rope_t(dqs_ref[...] * SCALE_A, *ta, A_HEAD_DIM // 2).astype(BF16)
        dz_ref[:, Z_KA:Z_VA] = _rope_t(dks_ref[...], *ta, A_HEAD_DIM // 2).astype(BF16)
        dz_ref[:, Z_VA:Z_CQ] = dvs_ref[...].astype(BF16)
        dqm = jnp.concatenate([dqm_ref[h] for h in range(HEADS)], axis=1)
        dqb = _rope_t(dqm * SCALE_B, *tb, ROPE_DIM // 2).astype(BF16)
        dqb_ref[...] = dqb
        dcqn = _dot_nt(dqb, wuq_ref[...])
        cqn, rq = _rms_stats(cq_ref[...])
        dcq, dgq = _rms_bwd(dcqn, cqn, rq, gq_ref[...])
        dgq_ref[...] += dgq
        dz_ref[:, Z_CQ:Z_CKV] = dcq.astype(BF16)
        dkm = dkm_ref[...]
        dslot = dkm[:, 0:LANES]
        for h in range(1, HEADS):
            dslot = dslot + dkm[:, h * LANES:(h + 1) * LANES]
        dz_ref[:, Z_KR:Z_GATE] = _rope_t(dslot * c_ref[10:11, :], *tb, ROPE_DIM // 2).astype(BF16)
        dckvn = _dot_nt(dkm.astype(BF16), wk_ref[...]) + _dot_nt(dvm_ref[...].astype(BF16), wv_ref[...])
        ckvn, rkv = _rms_stats(ckv_ref[...])
        dckv, dgkv = _rms_bwd(dckvn, ckvn, rkv, gkv_ref[...])
        dgkv_ref[...] += dgkv
        dz_ref[:, Z_CKV:Z_KR] = dckv.astype(BF16)
        dz_ref[:, Z_GATE:ZW] = dgt_ref[...]
        dh1 = _dot_nt(dz_ref[...], win_ref[...])
        xn, r1 = _rms_stats(x_ref[...])
        d0, dg1 = _rms_bwd(dh1, xn, r1, g1_ref[...])
        dg1_ref[...] += dg1
        dx_ref[...] = dx1_ref[...] + d0

    def acc(n):
        return (_sds((1, n), F32), _full((1, n)))

    ins = [(dqs, _row(tm, 1024)), (dks, _row(tm, 256)), (dvs, _row(tm, 256)), (dqm, _heads(tm, HEADS)),
           (dkm, _row(tm, 1024)), (dvm, _row(tm, 1024))] + [(t, _row(tm, LANES)) for t in tabs] + [
           (consts, _full(consts.shape)), (cq, _row(tm, 256)), (ckv, _row(tm, 128)), (gq, _full(gq.shape)),
           (gkv, _full(gkv.shape)), (wuq, _full(wuq.shape)), (wk, _full(wk.shape)), (wv, _full(wv.shape)),
           (dgates, _row(tm, 2048)), (win, _full(win.shape)), (x, _row(tm, D_MODEL)), (g1, _full(g1.shape)),
           (dx1, _row(tm, D_MODEL))]
    outs = [(_sds((t_rows, ZW), BF16), _row(tm, ZW)), (_sds((t_rows, 1024), BF16), _row(tm, 1024)),
            (_sds((t_rows, D_MODEL), F32), _row(tm, D_MODEL)), acc(256), acc(128), acc(D_MODEL)]
    return _rows_call("bwd_in", body, t_rows, tm, ins, outs)


def _pick_cols(n):
    best = LANES
    for d in range(LANES, min(n, 1408) + 1, LANES):
        if n % d == 0:
            best = d
    return best


def _mm_tn(name, a, b):
    t_rows, m = a.shape
    n = b.shape[1]
    bk = min(512, t_rows)
    bm, bn = _pick_cols(m), _pick_cols(n)

    def body(a_ref, b_ref, o_ref):
        @pl.when(pl.program_id(2) == 0)
        def _():
            o_ref[...] = jnp.zeros((bm, bn), F32)

        o_ref[...] += _dot_tn(a_ref[...].astype(BF16), b_ref[...].astype(BF16))

    return pl.pallas_call(
        body, name=name, grid=(m // bm, n // bn, t_rows // bk),
        in_specs=[pl.BlockSpec((bk, bm), lambda i, j, k: (k, i)), pl.BlockSpec((bk, bn), lambda i, j, k: (k, j))],
        out_specs=pl.BlockSpec((bm, bn), lambda i, j, k: (i, j)),
        out_shape=_sds((m, n), F32),
        compiler_params=pltpu.CompilerParams(dimension_semantics=("arbitrary",) * 3, vmem_limit_bytes=VMEM_LIMIT),
    )(a, b)


PACK_ROWS = 512


def _pack_tile(rows):
    assert rows % PACK_ROWS == 0
    return PACK_ROWS


def _add_pair(a, b):
    _, rows, _ = a.shape
    t = _pack_tile(rows)

    def body(a_ref, b_ref, o_ref):
        o_ref[...] = (a_ref[...] + b_ref[...]).astype(BF16)

    spec = pl.BlockSpec((4, t, LANES), lambda i: (0, i, 0))
    return pl.pallas_call(body, name="rs_add_pair", grid=(rows // t,), in_specs=[spec, spec], out_specs=spec,
                          out_shape=_sds(a.shape, BF16))(a, b)


def _add_chips(parts):
    _, rows, _ = parts.shape
    t = _pack_tile(rows)

    def body(p_ref, o_ref):
        acc = p_ref[0].astype(F32)
        for j in range(1, 4):
            acc = acc + p_ref[j].astype(F32)
        o_ref[...] = acc

    return pl.pallas_call(body, name="rs_add_chips", grid=(rows // t,),
                          in_specs=[pl.BlockSpec((4, t, LANES), lambda i: (0, i, 0))],
                          out_specs=pl.BlockSpec((t, LANES), lambda i: (i, 0)),
                          out_shape=_sds((rows, LANES), F32))(parts)


def _add_devices(parts):
    n, rows, _ = parts.shape

    def body(p_ref, o_ref):
        acc = p_ref[0]
        for j in range(1, n):
            acc = acc + p_ref[j]
        o_ref[...] = acc

    return pl.pallas_call(body, name="small_add", grid=(1,),
                          in_specs=[pl.BlockSpec((n, rows, LANES), lambda i: (0, 0, 0))],
                          out_specs=pl.BlockSpec((rows, LANES), lambda i: (0, 0)),
                          out_shape=_sds((rows, LANES), F32))(parts)


def _adam_rows(k, n):
    target = max(8, (1 << 20) // (4 * n))
    if k <= target:
        return k
    best = None
    for d in range(8, target + 1, 8):
        if k % d == 0:
            best = d
    return best if best is not None else k


def _adamw(name, w, g, m, v):
    k, n = w.shape
    bk = _adam_rows(k, n)
    c1 = 1.0 - ADAM_B1 ** ADAM_STEP
    c2 = 1.0 - ADAM_B2 ** ADAM_STEP

    def body(w_ref, g_ref, m_ref, v_ref, d_ref, mo_ref, vo_ref):
        g_ = g_ref[...]
        m_ = ADAM_B1 * m_ref[...] + (1.0 - ADAM_B1) * g_
        v_ = ADAM_B2 * v_ref[...] + (1.0 - ADAM_B2) * (g_ * g_)
        mo_ref[...] = m_
        vo_ref[...] = v_
        d_ref[...] = -ADAM_LR * ((m_ / c1) / (jnp.sqrt(v_ / c2) + ADAM_EPS) + ADAM_WD * w_ref[...])

    spec = pl.BlockSpec((bk, n), lambda i: (i, 0))
    return pl.pallas_call(body, name=name, grid=(k // bk,), in_specs=[spec] * 4, out_specs=[spec] * 3,
                          out_shape=[_sds((k, n), F32)] * 3,
                          compiler_params=pltpu.CompilerParams(vmem_limit_bytes=VMEM_LIMIT))(w, g, m, v)


_HBM = pl.BlockSpec(memory_space=pltpu.HBM)


def _me():
    return lax.axis_index("x"), lax.axis_index("y"), lax.axis_index("c")


def _other_chips(x, y):
    return [(1 - x, y), (x, 1 - y), (1 - x, 1 - y)]


def _gather_weights(shard):
    def body(x_ref, out_ref, send_sems, recv_sems):
        x, y, c = _me()
        sibling = (x, y, 1 - c)
        chips = _other_chips(x, y)

        def slot(px, py, pc):
            return out_ref.at[2 * px + py, pc]

        def copy(k, src, dst, to):
            return pltpu.make_async_remote_copy(src_ref=src, dst_ref=dst, send_sem=send_sems.at[k],
                                                recv_sem=recv_sems.at[k], device_id=to, device_id_type=MESH)

        first = [copy(j, x_ref.at[c], slot(x, y, c), (*chip, c)) for j, chip in enumerate(chips)]
        for cp in first:
            cp.start()
        passed = [copy(3 + j, slot(*chip, c), slot(*chip, c), sibling) for j, chip in enumerate(chips)]
        for j, chip in enumerate(chips):
            copy(j, x_ref.at[c], slot(*chip, c), (*chip, c)).wait_recv()
            passed[j].start()
        for j, chip in enumerate(chips):
            copy(3 + j, slot(*chip, 1 - c), slot(*chip, 1 - c), sibling).wait_recv()
        for cp in first + passed:
            cp.wait_send()

    return pl.pallas_call(
        body, name="gather_weights", out_shape=_sds((4,) + shard.shape, shard.dtype), in_specs=[_HBM], out_specs=_HBM,
        scratch_shapes=[pltpu.SemaphoreType.DMA((6,)), pltpu.SemaphoreType.DMA((6,))],
    )(shard)


def _swap_sibling(name, v):
    def body(v_ref, out_ref, send_sem, recv_sem):
        x, y, c = _me()
        cp = pltpu.make_async_remote_copy(src_ref=v_ref, dst_ref=out_ref, send_sem=send_sem, recv_sem=recv_sem,
                                          device_id=(x, y, 1 - c), device_id_type=MESH)
        cp.start()
        cp.wait()

    return pl.pallas_call(
        body, name=name, out_shape=_sds(v.shape, v.dtype), in_specs=[_HBM], out_specs=_HBM,
        scratch_shapes=[pltpu.SemaphoreType.DMA, pltpu.SemaphoreType.DMA],
    )(v)


def _scatter_chips(s):
    def body(s_ref, out_ref, send_sems, recv_sems):
        x, y, c = _me()
        k = 2 * x + y
        chips = _other_chips(x, y)
        sends = [pltpu.make_async_remote_copy(src_ref=s_ref.at[2 * cx + cy], dst_ref=out_ref.at[k],
                                              send_sem=send_sems.at[j], recv_sem=recv_sems.at[j],
                                              device_id=(cx, cy, c), device_id_type=MESH)
                 for j, (cx, cy) in enumerate(chips)]
        for cp in sends:
            cp.start()
        for j, (cx, cy) in enumerate(chips):
            pltpu.make_async_remote_copy(src_ref=s_ref.at[k], dst_ref=out_ref.at[2 * cx + cy],
                                         send_sem=send_sems.at[j], recv_sem=recv_sems.at[j],
                                         device_id=(cx, cy, c), device_id_type=MESH).wait_recv()
        for cp in sends:
            cp.wait_send()

    return pl.pallas_call(
        body, name="scatter_chips", out_shape=_sds(s.shape, s.dtype), in_specs=[_HBM], out_specs=_HBM,
        scratch_shapes=[pltpu.SemaphoreType.DMA((3,)), pltpu.SemaphoreType.DMA((3,))],
    )(s)


def _gather_small(name, v):
    def body(v_ref, out_ref, send_sems, recv_sems, local_sem):
        x, y, c = _me()
        me = 4 * x + 2 * y + c
        mine = pltpu.make_async_copy(v_ref, out_ref.at[me], local_sem)
        mine.start()
        peers = []
        for f in range(1, 8):
            fx, fy, fc = (f >> 2) & 1, (f >> 1) & 1, f & 1
            px = 1 - x if fx else x
            py = 1 - y if fy else y
            pc = 1 - c if fc else c
            peers.append((f - 1, (px, py, pc)))
        sends = [pltpu.make_async_remote_copy(src_ref=v_ref, dst_ref=out_ref.at[me], send_sem=send_sems.at[k],
                                              recv_sem=recv_sems.at[k], device_id=peer, device_id_type=MESH)
                 for k, peer in peers]
        for cp in sends:
            cp.start()
        for k, (px, py, pc) in peers:
            pltpu.make_async_remote_copy(src_ref=v_ref, dst_ref=out_ref.at[4 * px + 2 * py + pc],
                                         send_sem=send_sems.at[k], recv_sem=recv_sems.at[k],
                                         device_id=(px, py, pc), device_id_type=MESH).wait_recv()
        for cp in sends:
            cp.wait_send()
        mine.wait()

    return pl.pallas_call(
        body, name=name, out_shape=_sds((8,) + v.shape, v.dtype), in_specs=[_HBM], out_specs=_HBM,
        scratch_shapes=[pltpu.SemaphoreType.DMA((7,)), pltpu.SemaphoreType.DMA((7,)), pltpu.SemaphoreType.DMA],
    )(v)


_BIG = (("w_in", (1024, 3232), 1), ("w_uq", (256, 768), 1), ("w_ukv", (128, 1024), 1), ("w_branch_a", (512, 1024), 1),
        ("w_branch_b", (512, 1024), 1), ("w_out", (1024, 1024), 0), ("w_up", (1024, 5632), 1),
        ("w_down", (2816, 1024), 0), ("w_ple_gate", (1024, 1024), 0), ("w_ple", (256, 1024), 1))


def _shard_shape(shape, axis):
    return (shape[0] // 4, shape[1]) if axis == 0 else (shape[0], shape[1] // 4)


def _half_rows(shape, axis):
    k, n = _shard_shape(shape, axis)
    return k * n // (2 * LANES)


_PACK_PAD = -sum(_half_rows(shape, axis) for _, shape, axis in _BIG) % PACK_ROWS


def _pack_shards(shards, dtype):
    parts = [shards[name].astype(dtype).reshape(2, _half_rows(shape, axis), LANES) for name, shape, axis in _BIG]
    return jnp.concatenate(parts + [jnp.zeros((2, _PACK_PAD, LANES), dtype)], axis=1)


def _unpack_gathered(g):
    out, off = {}, 0
    for name, shape, axis in _BIG:
        r = _half_rows(shape, axis)
        k, n = _shard_shape(shape, axis)
        w = g[:, :, off:off + r, :].reshape(4, k, n)
        out[name] = w.reshape(shape) if axis == 0 else w.transpose(1, 0, 2).reshape(shape)
        off += r
    return out


def _pack_grad_halves(grads, c):
    keep, send = [], []
    for name, shape, axis in _BIG:
        k, n = _shard_shape(shape, axis)
        r = _half_rows(shape, axis)
        g = grads[name]
        g4 = g.reshape(4, k, n) if axis == 0 else g.reshape(k, 4, n).transpose(1, 0, 2)
        g4 = g4.reshape(4, 2, r, LANES)
        keep.append(lax.dynamic_index_in_dim(g4, c, 1, keepdims=False))
        send.append(lax.dynamic_index_in_dim(g4, 1 - c, 1, keepdims=False))
    pad = [jnp.zeros((4, _PACK_PAD, LANES), F32)]
    return jnp.concatenate(keep + pad, axis=1), jnp.concatenate(send + pad, axis=1)


def _unpack_shard_grads(f):
    out, off = {}, 0
    for name, shape, axis in _BIG:
        r = _half_rows(shape, axis)
        out[name] = f[:, off:off + r, :].reshape(_shard_shape(shape, axis))
        off += r
    return out


def _pad_slots(w, heads, dim, axis):
    if axis == 1:
        k = w.shape[0]
        return jnp.pad(w.reshape(k, heads, dim), ((0, 0), (0, 0), (0, LANES - dim))).reshape(k, heads * LANES)
    n = w.shape[1]
    return jnp.pad(w.reshape(heads, dim, n), ((0, 0), (0, LANES - dim), (0, 0))).reshape(heads * LANES, n)


def _unpad_slots(w, heads, dim, axis):
    if axis == 1:
        k = w.shape[0]
        return w.reshape(k, heads, LANES)[:, :, :dim].reshape(k, heads * dim)
    n = w.shape[1]
    return w.reshape(heads, LANES, n)[:, :dim, :].reshape(heads * dim, n)


def _pad_w_in(w):
    kr = jnp.pad(w[:, 1152:1184], ((0, 0), (NOPE_DIM, LANES - NOPE_DIM - ROPE_DIM)))
    return jnp.concatenate([_pad_slots(w[:, 0:512], HEADS, A_HEAD_DIM, 1),
                            _pad_slots(w[:, 512:640], A_KV_HEADS, A_HEAD_DIM, 1),
                            _pad_slots(w[:, 640:768], A_KV_HEADS, A_HEAD_DIM, 1),
                            w[:, 768:1024], w[:, 1024:1152], kr, w[:, 1184:3232]], axis=1)


def _unpad_w_in(w):
    return jnp.concatenate([_unpad_slots(w[:, Z_QA:Z_KA], HEADS, A_HEAD_DIM, 1),
                            _unpad_slots(w[:, Z_KA:Z_VA], A_KV_HEADS, A_HEAD_DIM, 1),
                            _unpad_slots(w[:, Z_VA:Z_CQ], A_KV_HEADS, A_HEAD_DIM, 1),
                            w[:, Z_CQ:Z_CKV], w[:, Z_CKV:Z_KR],
                            w[:, Z_KR + NOPE_DIM:Z_KR + NOPE_DIM + ROPE_DIM], w[:, Z_GATE:ZW]], axis=1)


_SMALL = (("attn_pre_norm", 1024), ("attn_post_norm", 1024), ("b_gate", 2048), ("sinks", 8), ("q_a_norm", 256),
          ("kv_a_norm", 128), ("mlp_pre_norm", 1024), ("mlp_post_norm", 1024), ("conv_b", 5632), ("ple_norm", 1024),
          ("conv_w", 3 * 5632), ("loss", 1))


def _small_rows(n):
    return -(-n // LANES)


def _pack_small(vals):
    parts = []
    for name, n in _SMALL:
        r = _small_rows(n)
        parts.append(jnp.pad(vals[name].reshape(-1), (0, r * LANES - n)).reshape(r, LANES))
    rows = sum(_small_rows(n) for _, n in _SMALL)
    pad = -rows % 8
    if pad:
        parts.append(jnp.zeros((pad, LANES), F32))
    return jnp.concatenate(parts, axis=0)


def _unpack_small(buf):
    out, off = {}, 0
    for name, n in _SMALL:
        r = _small_rows(n)
        out[name] = buf[off:off + r].reshape(-1)[:n]
        off += r
    return out


def kernel(x, p, positions, attn_pre_norm, attn_post_norm, w_in, b_gate, sinks, q_a_norm, w_uq, kv_a_norm, w_ukv, w_branch_a, w_branch_b, w_out, mlp_pre_norm, mlp_post_norm, w_up, conv_w, conv_b, w_down, ple_norm, w_ple_gate, w_ple, loss_target, m_attn_pre_norm, m_attn_post_norm, m_w_in, m_b_gate, m_sinks, m_q_a_norm, m_w_uq, m_kv_a_norm, m_w_ukv, m_w_branch_a, m_w_branch_b, m_w_out, m_mlp_pre_norm, m_mlp_post_norm, m_w_up, m_conv_w, m_conv_b, m_w_down, m_ple_norm, m_w_ple_gate, m_w_ple, v_attn_pre_norm, v_attn_post_norm, v_w_in, v_b_gate, v_sinks, v_q_a_norm, v_w_uq, v_kv_a_norm, v_w_ukv, v_w_branch_a, v_w_branch_b, v_w_out, v_mlp_pre_norm, v_mlp_post_norm, v_w_up, v_conv_w, v_conv_b, v_w_down, v_ple_norm, v_w_ple_gate, v_w_ple):
    names = ["attn_pre_norm", "attn_post_norm", "w_in", "b_gate", "sinks", "q_a_norm", "w_uq", "kv_a_norm", "w_ukv",
             "w_branch_a", "w_branch_b", "w_out", "mlp_pre_norm", "mlp_post_norm", "w_up", "conv_w", "conv_b",
             "w_down", "ple_norm", "w_ple_gate", "w_ple"]
    wts = dict(zip(names, [attn_pre_norm, attn_post_norm, w_in, b_gate, sinks, q_a_norm, w_uq, kv_a_norm, w_ukv,
                           w_branch_a, w_branch_b, w_out, mlp_pre_norm, mlp_post_norm, w_up, conv_w, conv_b, w_down,
                           ple_norm, w_ple_gate, w_ple]))
    moms = dict(zip(names, [m_attn_pre_norm, m_attn_post_norm, m_w_in, m_b_gate, m_sinks, m_q_a_norm, m_w_uq,
                            m_kv_a_norm, m_w_ukv, m_w_branch_a, m_w_branch_b, m_w_out, m_mlp_pre_norm,
                            m_mlp_post_norm, m_w_up, m_conv_w, m_conv_b, m_w_down, m_ple_norm, m_w_ple_gate, m_w_ple]))
    vars_ = dict(zip(names, [v_attn_pre_norm, v_attn_post_norm, v_w_in, v_b_gate, v_sinks, v_q_a_norm, v_w_uq,
                             v_kv_a_norm, v_w_ukv, v_w_branch_a, v_w_branch_b, v_w_out, v_mlp_pre_norm,
                             v_mlp_post_norm, v_w_up, v_conv_w, v_conv_b, v_w_down, v_ple_norm, v_w_ple_gate, v_w_ple]))
    w2 = {n: a.reshape(a.shape[-2:]) for n, a in wts.items()}
    m2 = {n: a.reshape(a.shape[-2:]) for n, a in moms.items()}
    v2 = {n: a.reshape(a.shape[-2:]) for n, a in vars_.items()}

    t_rows = x.shape[-2]
    tm = min(256, t_rows)
    xc, yc, cc = lax.axis_index("x"), lax.axis_index("y"), lax.axis_index("c")
    chip = 2 * xc + yc

    x2d = x.reshape(t_rows, D_MODEL)
    p2d = p.reshape(t_rows, PLE_DIM)
    tgt = loss_target.reshape(t_rows, D_MODEL)
    pos_f = positions.reshape(t_rows, 1).astype(F32)

    my_shard = _pack_shards(w2, BF16)
    gathered = lax.dynamic_update_slice(_gather_weights(my_shard), my_shard[None], (chip, 0, 0, 0))
    full = _unpack_gathered(gathered)
    cw_rows = 3 * 1408 // LANES
    cw_all = _gather_small("gather_conv_w", jnp.pad(w2["conv_w"].reshape(cw_rows, LANES), ((0, 40 - cw_rows), (0, 0))))
    conv_full = cw_all[0::2, :cw_rows].reshape(4, 3, 1408).transpose(1, 0, 2).reshape(3, 2 * D_FF)
    convw8 = jnp.pad(conv_full, ((0, 5), (0, 0)))

    win = _pad_w_in(full["w_in"])
    wuq = _pad_slots(full["w_uq"], HEADS, NOPE_DIM + ROPE_DIM, 1)
    ukv = full["w_ukv"].reshape(KV_LORA, HEADS, NOPE_DIM + V_DIM)
    wk = _pad_slots(ukv[:, :, :NOPE_DIM].reshape(KV_LORA, HEADS * NOPE_DIM), HEADS, NOPE_DIM, 1)
    wv = _pad_slots(ukv[:, :, NOPE_DIM:].reshape(KV_LORA, HEADS * V_DIM), HEADS, V_DIM, 1)
    wba = _pad_slots(full["w_branch_a"], HEADS, A_HEAD_DIM, 0)
    wbb = _pad_slots(full["w_branch_b"], HEADS, V_DIM, 0)
    wout, wup, wdown, wpg, wple = full["w_out"], full["w_up"], full["w_down"], full["w_ple_gate"], full["w_ple"]
    g1, g2, g3, g4, g5 = (w2["attn_pre_norm"], w2["attn_post_norm"], w2["mlp_pre_norm"], w2["mlp_post_norm"],
                          w2["ple_norm"])
    gq, gkv, bg, convb = w2["q_a_norm"], w2["kv_a_norm"], w2["b_gate"], w2["conv_b"]
    swa_tile = min(SWA_TILE, t_rows)
    sink_rows = jnp.repeat(w2["sinks"].reshape(A_KV_HEADS, SWA_GROUP, 1), swa_tile, axis=2).reshape(
        A_KV_HEADS, 1, SWA_GROUP * swa_tile)
    swa_bias = _swa_bias(swa_tile)

    consts = _rope_consts()
    tabs = _rope_tables(pos_f, consts, tm)
    h1, qs, ks, vs, cq, cqn, ckv, ckvn, qm, km, vm, gate = _fwd_in(x2d, g1, win, bg, gq, gkv, wuq, wk, wv, tabs, tm)
    ya, lse_a = _swa_fwd(qs, ks, vs, swa_bias, sink_rows)
    yb, lse_b = _mla_fwd(qm, km, vm)
    pa, pb, mixed, o, x1, h2 = _fwd_mix(x2d, ya, yb, gate, wba, wbb, wout, g2, g3, tm)
    up, a = _fwd_up(h2, wup, convw8, convb, tm)
    ff, x2, e, n5, sg, dx3, loss_part = _fwd_out(a, wdown, x1, g4, p2d, wple, g5, wpg, tgt, tm)

    dpre, de, dx2, dff, du, dg5, dg4, dconvb, dconvw8 = _bwd_out(dx3, e, sg, x2, ff, g5, g4, wpg, wdown, up, convw8,
                                                                 convb, tm)
    dup, dx1, do, dpa, dpb, dgates, dya, dyb, dg3, dg2, dbg = _bwd_mid(du, convw8, wup, dx2, x1, g3, o, g2, wout, gate,
                                                                       pa, pb, wba, wbb, tm)
    dqs, dks, dvs, dsink_rows = _swa_bwd(qs, ks, vs, ya, dya, lse_a, swa_bias, sink_rows)
    dsink = dsink_rows[:, 0:SWA_GROUP, 0]
    dqm, dkm, dvm = _mla_bwd(qm, km, vm, dyb, lse_b, _mla_delta(yb, dyb))
    dz, dqb, dx, dgq, dgkv, dg1 = _bwd_in(dqs, dks, dvs, dqm, dkm, dvm, tabs, consts, cq, ckv, gq, gkv, wuq, wk, wv,
                                           dgates, win, x2d, g1, dx1, tm)

    dwk = _unpad_slots(_mm_tn("dw_k", ckvn, dkm), HEADS, NOPE_DIM, 1).reshape(KV_LORA, HEADS, NOPE_DIM)
    dwv = _unpad_slots(_mm_tn("dw_v", ckvn, dvm), HEADS, V_DIM, 1).reshape(KV_LORA, HEADS, V_DIM)
    grads = {
        "w_in": _unpad_w_in(_mm_tn("dw_in", h1, dz)),
        "w_uq": _unpad_slots(_mm_tn("dw_uq", cqn, dqb), HEADS, NOPE_DIM + ROPE_DIM, 1),
        "w_ukv": jnp.concatenate([dwk, dwv], axis=2).reshape(KV_LORA, HEADS * (NOPE_DIM + V_DIM)),
        "w_branch_a": _unpad_slots(_mm_tn("dw_branch_a", ya, dpa), HEADS, A_HEAD_DIM, 0),
        "w_branch_b": _unpad_slots(_mm_tn("dw_branch_b", yb, dpb), HEADS, V_DIM, 0),
        "w_out": _mm_tn("dw_out", mixed, do),
        "w_up": _mm_tn("dw_up", h2, dup),
        "w_down": _mm_tn("dw_down", a, dff),
        "w_ple_gate": _mm_tn("dw_ple_gate", n5, dpre),
        "w_ple": _mm_tn("dw_ple", p2d, de),
    }

    keep, send = _pack_grad_halves(grads, cc)
    pair = _add_pair(keep, _swap_sibling("swap_grad_halves", send))
    own = lax.dynamic_index_in_dim(pair, chip, 0, keepdims=True)
    reduced = _add_chips(lax.dynamic_update_slice(_scatter_chips(pair), own, (chip, 0, 0)))
    other = _swap_sibling("swap_reduced_halves", reduced)
    both = jnp.stack([jnp.where(cc == 0, reduced, other), jnp.where(cc == 0, other, reduced)])
    shard_grads = _unpack_shard_grads(both)

    small = {"attn_pre_norm": dg1, "attn_post_norm": dg2, "b_gate": dbg, "sinks": dsink, "q_a_norm": dgq,
             "kv_a_norm": dgkv, "mlp_pre_norm": dg3, "mlp_post_norm": dg4, "conv_b": dconvb, "ple_norm": dg5,
             "conv_w": dconvw8[0:3], "loss": loss_part}
    small_sum = _unpack_small(_add_devices(_gather_small("gather_small_grads", _pack_small(small))))
    for n in names:
        if n in small_sum and n != "conv_w":
            shard_grads[n] = small_sum[n].reshape(w2[n].shape)
    shard_grads["conv_w"] = lax.dynamic_index_in_dim(small_sum["conv_w"].reshape(3, 4, 1408), chip, 1, keepdims=False)

    loss = small_sum["loss"][0]

    g_out, d_out, m_out, v_out = [], [], [], []
    for n in names:
        g = shard_grads[n]
        d, mn, vn = _adamw("adamw_" + n, w2[n], g, m2[n], v2[n])
        shp = wts[n].shape
        g_out.append(g.reshape(shp))
        d_out.append(d.reshape(shp))
        m_out.append(mn.reshape(shp))
        v_out.append(vn.reshape(shp))
    return (loss, dx.reshape(x.shape), *g_out, *d_out, *m_out, *v_out)
```

```python
import functools
import math

import numpy as np
import jax
import jax.numpy as jnp
from jax import lax
from jax.experimental import pallas as pl
from jax.experimental.pallas import tpu as pltpu

F32 = jnp.float32
BF16 = jnp.bfloat16

D_MODEL = 1024
D_FF = 2816
PLE_DIM = 256
ROPE_THETA = 10000.0
RMS_EPS = 1e-6
SWA_WINDOW = 128
HEADS = 8
A_KV_HEADS = 2
A_HEAD_DIM = 64
Q_LORA = 256
KV_LORA = 128
NOPE_DIM = 64
ROPE_DIM = 32
V_DIM = 64
LANES = 128
ZW = 4096
NEG = -1e30
SCALE_A = A_HEAD_DIM ** -0.5
SCALE_B = (NOPE_DIM + ROPE_DIM) ** -0.5

ADAM_LR = 0.001
ADAM_B1 = 0.9
ADAM_B2 = 0.999
ADAM_EPS = 1e-08
ADAM_WD = 0.01
ADAM_STEP = 10

VMEM_LIMIT = 60 * 1024 * 1024
MESH_AXES = ("x", "y", "c")
MESH = pl.DeviceIdType.MESH

Z_QA, Z_KA, Z_VA, Z_CQ, Z_CKV, Z_KR, Z_GATE = 0, 1024, 1280, 1536, 1792, 1920, 2048


def _dot(a, b):
    return jnp.dot(a, b, preferred_element_type=F32)


def _dot_nt(a, b):
    return lax.dot_general(a, b, (((1,), (1,)), ((), ())), preferred_element_type=F32)


def _dot_tn(a, b):
    return lax.dot_general(a, b, (((0,), (0,)), ((), ())), preferred_element_type=F32)


def _rms_stats(x):
    r = lax.rsqrt(jnp.mean(x * x, axis=-1, keepdims=True) + RMS_EPS)
    return x * r, r


def _rms_bwd(dy, xn, r, g):
    dxn = dy * g
    dx = r * (dxn - xn * jnp.mean(dxn * xn, axis=-1, keepdims=True))
    dg = jnp.sum(dy * xn, axis=0, keepdims=True)
    return dx, dg


def _tile_lanes(t, n):
    return t if n == 1 else jnp.concatenate([t] * n, axis=1)


def _rope(x, c, s1, s2, half):
    w = x.shape[1]
    n = w // LANES
    return (x * _tile_lanes(c, n) + pltpu.roll(x, w - half, 1) * _tile_lanes(s1, n)
            + pltpu.roll(x, half, 1) * _tile_lanes(s2, n))


def _rope_t(dy, c, s1, s2, half):
    w = dy.shape[1]
    n = w // LANES
    return (dy * _tile_lanes(c, n) + pltpu.roll(dy * _tile_lanes(s1, n), half, 1)
            + pltpu.roll(dy * _tile_lanes(s2, n), w - half, 1))


def _sigmoid(x):
    return 1.0 / (1.0 + jnp.exp(-x))


_GELU_C = math.sqrt(2.0 / math.pi)


def _gelu_and_grad(x):
    x2 = x * x
    th = jnp.tanh(_GELU_C * (x + 0.044715 * x * x2))
    gel = 0.5 * x * (1.0 + th)
    dgel = 0.5 * (1.0 + th) + 0.5 * x * (1.0 - th * th) * (_GELU_C * (1.0 + 3.0 * 0.044715 * x2))
    return gel, dgel


def _conv_taps(up, h6, h7):
    rows = lax.broadcasted_iota(jnp.int32, up.shape, 0)
    r1 = pltpu.roll(up, 1, 0)
    r2 = pltpu.roll(up, 2, 0)
    xm1 = jnp.where(rows == 0, h7, r1)
    xm2 = jnp.where(rows == 0, h6, jnp.where(rows == 1, h7, r2))
    return xm1, xm2


def _conv_taps_next(du, n0, n1):
    tm = du.shape[0]
    rows = lax.broadcasted_iota(jnp.int32, du.shape, 0)
    r1 = pltpu.roll(du, tm - 1, 0)
    r2 = pltpu.roll(du, tm - 2, 0)
    xp1 = jnp.where(rows == tm - 1, n0, r1)
    xp2 = jnp.where(rows == tm - 2, n0, jnp.where(rows == tm - 1, n1, r2))
    return xp1, xp2


def _row(tm, n):
    return pl.BlockSpec((tm, n), lambda i: (i, 0))


def _full(shape):
    nd = len(shape)
    return pl.BlockSpec(tuple(shape), lambda i: (0,) * nd)


def _heads(tm, h):
    return pl.BlockSpec((h, tm, LANES), lambda i: (0, i, 0))


def _rows_call(name, body, t_rows, tm, ins, outs, scratch=()):
    return pl.pallas_call(
        body, name=name, grid=(t_rows // tm,),
        in_specs=[s for _, s in ins],
        out_specs=[s for _, s in outs],
        out_shape=[s for s, _ in outs],
        scratch_shapes=list(scratch),
        compiler_params=pltpu.CompilerParams(dimension_semantics=("arbitrary",), vmem_limit_bytes=VMEM_LIMIT),
    )(*[a for a, _ in ins])


def _sds(shape, dtype):
    return jax.ShapeDtypeStruct(tuple(shape), dtype)


def _rope_consts():
    c = np.zeros((16, LANES), np.float32)
    lane = np.arange(LANES)
    inv_a = (ROPE_THETA ** (-(np.arange(0, A_HEAD_DIM, 2, dtype=np.float32) / A_HEAD_DIM))).astype(np.float32)
    in_a = lane < A_HEAD_DIM
    c[0, in_a] = inv_a[lane[in_a] % (A_HEAD_DIM // 2)]
    c[1, in_a] = 1.0
    c[2, lane < A_HEAD_DIM // 2] = -1.0
    c[3, (lane >= A_HEAD_DIM // 2) & in_a] = 1.0
    inv_b = (ROPE_THETA ** (-(np.arange(0, ROPE_DIM, 2, dtype=np.float32) / ROPE_DIM))).astype(np.float32)
    pe = (lane >= NOPE_DIM) & (lane < NOPE_DIM + ROPE_DIM)
    c[5, pe] = inv_b[(lane[pe] - NOPE_DIM) % (ROPE_DIM // 2)]
    c[6, pe] = 1.0
    c[7, (lane >= NOPE_DIM) & (lane < NOPE_DIM + ROPE_DIM // 2)] = -1.0
    c[8, (lane >= NOPE_DIM + ROPE_DIM // 2) & (lane < NOPE_DIM + ROPE_DIM)] = 1.0
    c[9, lane < NOPE_DIM] = 1.0
    c[10, pe] = 1.0
    return jnp.asarray(c)


def _rope_tables(pos_f, consts, tm):
    t_rows = pos_f.shape[0]

    def body(pos_ref, c_ref, ca, sa1, sa2, cb, sb1, sb2):
        pos = pos_ref[...]
        ang = pos * c_ref[0:1, :]
        cs, sn = jnp.cos(ang), jnp.sin(ang)
        ca[...] = cs * c_ref[1:2, :]
        sa1[...] = sn * c_ref[2:3, :]
        sa2[...] = sn * c_ref[3:4, :]
        ang = pos * c_ref[5:6, :]
        cs, sn = jnp.cos(ang), jnp.sin(ang)
        cb[...] = cs * c_ref[6:7, :] + c_ref[9:10, :]
        sb1[...] = sn * c_ref[7:8, :]
        sb2[...] = sn * c_ref[8:9, :]

    tab = (_sds((t_rows, LANES), F32), _row(tm, LANES))
    return _rows_call("rope_tables", body, t_rows, tm,
                      [(pos_f, _row(tm, 1)), (consts, _full(consts.shape))], [tab] * 6)


def _fwd_in(x, g1, win, bg, gq, gkv, wuq, wk, wv, tabs, tm):
    t_rows = x.shape[0]

    def body(x_ref, g1_ref, win_ref, bg_ref, gq_ref, gkv_ref, wuq_ref, wk_ref, wv_ref,
             ca, sa1, sa2, cb, sb1, sb2,
             h1_ref, qs_ref, ks_ref, vs_ref, cq_ref, cqn_ref, ckv_ref, ckvn_ref, qm_ref, km_ref, vm_ref, gate_ref):
        xn, _ = _rms_stats(x_ref[...])
        hb = (xn * g1_ref[...]).astype(BF16)
        h1_ref[...] = hb
        ta = (ca[...], sa1[...], sa2[...])
        tb = (cb[...], sb1[...], sb2[...])
        qs_ref[...] = (_rope(_dot(hb, win_ref[:, Z_QA:Z_KA]), *ta, A_HEAD_DIM // 2) * SCALE_A).astype(BF16)
        ks_ref[...] = _rope(_dot(hb, win_ref[:, Z_KA:Z_VA]), *ta, A_HEAD_DIM // 2).astype(BF16)
        vs_ref[...] = _dot(hb, win_ref[:, Z_VA:Z_CQ]).astype(BF16)
        cq = _dot(hb, win_ref[:, Z_CQ:Z_CKV])
        cq_ref[...] = cq
        cqn, _ = _rms_stats(cq)
        cqb = (cqn * gq_ref[...]).astype(BF16)
        cqn_ref[...] = cqb
        qm_ref[...] = (_rope(_dot(cqb, wuq_ref[...]), *tb, ROPE_DIM // 2) * SCALE_B).astype(BF16)
        ckv = _dot(hb, win_ref[:, Z_CKV:Z_KR])
        ckv_ref[...] = ckv
        ckvn, _ = _rms_stats(ckv)
        ckvb = (ckvn * gkv_ref[...]).astype(BF16)
        ckvn_ref[...] = ckvb
        kpe = _rope(_dot(hb, win_ref[:, Z_KR:Z_GATE]), *tb, ROPE_DIM // 2)
        km_ref[...] = (_dot(ckvb, wk_ref[...]) + _tile_lanes(kpe, HEADS)).astype(BF16)
        vm_ref[...] = _dot(ckvb, wv_ref[...]).astype(BF16)
        gate_ref[...] = _sigmoid(_dot(hb, win_ref[:, Z_GATE:ZW]) + bg_ref[...])

    def o(n, dt):
        return (_sds((t_rows, n), dt), _row(tm, n))

    ins = [(x, _row(tm, D_MODEL)), (g1, _full(g1.shape)), (win, _full(win.shape)), (bg, _full(bg.shape)),
           (gq, _full(gq.shape)), (gkv, _full(gkv.shape)), (wuq, _full(wuq.shape)), (wk, _full(wk.shape)),
           (wv, _full(wv.shape))] + [(t, _row(tm, LANES)) for t in tabs]
    outs = [o(1024, BF16), o(1024, BF16), o(256, BF16), o(256, BF16), o(256, F32), o(256, BF16), o(128, F32),
            o(128, BF16), o(1024, BF16), o(1024, BF16), o(1024, BF16), o(2048, F32)]
    return _rows_call("fwd_in", body, t_rows, tm, ins, outs)


def _attn_tile(t_rows):
    return min(512, t_rows)


MLA_HEADS_PER_STEP = 2


def _causal_pairs(nq, by_kv):
    if by_kv:
        pairs = [(i, j) for j in range(nq) for i in range(j, nq)]
    else:
        pairs = [(i, j) for i in range(nq) for j in range(i + 1)]
    return (jnp.asarray([p[0] for p in pairs], jnp.int32), jnp.asarray([p[1] for p in pairs], jnp.int32))


def _mla_fwd(q, k, v):
    t_rows = q.shape[0]
    t = _attn_tile(t_rows)
    hp = MLA_HEADS_PER_STEP
    w = hp * LANES
    ii, jj = _causal_pairs(t_rows // t, by_kv=False)

    def body(i_ref, j_ref, q_ref, k_ref, v_ref, o_ref, lse_ref, m_s, l_s, acc_s):
        i = i_ref[pl.program_id(1)]
        j = j_ref[pl.program_id(1)]

        @pl.when(j == 0)
        def _():
            m_s[...] = jnp.full(m_s.shape, NEG, F32)
            l_s[...] = jnp.zeros(l_s.shape, F32)
            acc_s[...] = jnp.zeros(acc_s.shape, F32)

        def step(diagonal):
            for hh in range(hp):
                sl = slice(hh * LANES, (hh + 1) * LANES)
                s = _dot_nt(k_ref[:, sl], q_ref[:, sl])
                if diagonal:
                    valid = (lax.broadcasted_iota(jnp.int32, (t, t), 0) <= lax.broadcasted_iota(jnp.int32, (t, t), 1))
                    s = jnp.where(valid, s, NEG)
                m_prev = m_s[hh]
                m_new = jnp.maximum(m_prev, jnp.max(s, axis=0, keepdims=True))
                p = jnp.exp(s - m_new)
                alpha = jnp.exp(m_prev - m_new)
                l_new = alpha * l_s[hh] + jnp.sum(p, axis=0, keepdims=True)
                acc = alpha * acc_s[hh] + _dot_tn(v_ref[:, sl], p.astype(BF16))
                if diagonal:
                    o_ref[:, sl] = (acc / l_new).T.astype(o_ref.dtype)
                    lse_ref[hh] = m_new + jnp.log(l_new)
                else:
                    m_s[hh] = m_new
                    l_s[hh] = l_new
                    acc_s[hh] = acc

        pl.when(j < i)(lambda: step(False))
        pl.when(j == i)(lambda: step(True))

    grid_spec = pltpu.PrefetchScalarGridSpec(
        num_scalar_prefetch=2, grid=(HEADS // hp, ii.shape[0]),
        in_specs=[pl.BlockSpec((t, w), lambda hb, s, ir, jr: (ir[s], hb)),
                  pl.BlockSpec((t, w), lambda hb, s, ir, jr: (jr[s], hb)),
                  pl.BlockSpec((t, w), lambda hb, s, ir, jr: (jr[s], hb))],
        out_specs=[pl.BlockSpec((t, w), lambda hb, s, ir, jr: (ir[s], hb)),
                   pl.BlockSpec((hp, 1, t), lambda hb, s, ir, jr: (hb, 0, ir[s]))],
        scratch_shapes=[pltpu.VMEM((hp, 1, t), F32), pltpu.VMEM((hp, 1, t), F32), pltpu.VMEM((hp, LANES, t), F32)])
    return pl.pallas_call(
        body, name="mla_fwd", grid_spec=grid_spec,
        out_shape=[_sds((t_rows, HEADS * LANES), BF16), _sds((HEADS, 1, t_rows), F32)],
        compiler_params=pltpu.CompilerParams(dimension_semantics=("arbitrary",) * 2, vmem_limit_bytes=VMEM_LIMIT),
    )(ii, jj, q, k, v)


def _mla_delta(o, do):
    t_rows = o.shape[0]
    t = _attn_tile(t_rows)

    def body(o_ref, do_ref, dl_ref):
        prod = o_ref[...].astype(F32) * do_ref[...].astype(F32)
        dl_ref[0] = jnp.sum(prod.T, axis=0, keepdims=True)

    return pl.pallas_call(
        body, name="mla_delta", grid=(HEADS, t_rows // t),
        in_specs=[pl.BlockSpec((t, LANES), lambda h, i: (i, h)), pl.BlockSpec((t, LANES), lambda h, i: (i, h))],
        out_specs=pl.BlockSpec((1, 1, t), lambda h, i: (h, 0, i)),
        out_shape=_sds((HEADS, 1, t_rows), F32),
    )(o, do)


def _mla_bwd(q, k, v, do, lse, delta):
    t_rows = q.shape[0]
    t = _attn_tile(t_rows)
    hp = MLA_HEADS_PER_STEP
    w = hp * LANES
    ii, jj = _causal_pairs(t_rows // t, by_kv=True)

    def body(i_ref, j_ref, q_ref, k_ref, v_ref, do_ref, lse_ref, dl_ref, dq_ref, dk_ref, dv_ref):
        i = i_ref[pl.program_id(1)]
        j = j_ref[pl.program_id(1)]

        @pl.when(pl.program_id(1) == 0)
        def _():
            dq_ref[...] = jnp.zeros(dq_ref.shape, F32)

        def step(diagonal):
            r0 = pl.multiple_of(i * t, t)
            for hh in range(hp):
                sl = slice(hh * LANES, (hh + 1) * LANES)
                qv = q_ref[:, sl]
                kv = k_ref[:, sl]
                dov = do_ref[:, sl]
                s = _dot_nt(kv, qv)
                if diagonal:
                    valid = (lax.broadcasted_iota(jnp.int32, (t, t), 0) <= lax.broadcasted_iota(jnp.int32, (t, t), 1))
                    s = jnp.where(valid, s, NEG)
                p = jnp.exp(s - lse_ref[hh])
                dv = _dot(p.astype(BF16), dov)
                dp = _dot_nt(v_ref[:, sl], dov)
                ds = (p * (dp - dl_ref[hh])).astype(BF16)
                dk = _dot(ds, qv)
                if diagonal:
                    dv_ref[:, sl] = dv
                    dk_ref[:, sl] = dk
                else:
                    dv_ref[:, sl] += dv
                    dk_ref[:, sl] += dk
                dq_ref[hh, pl.ds(r0, t), :] += _dot_tn(ds, kv)

        pl.when(i > j)(lambda: step(False))
        pl.when(i == j)(lambda: step(True))

    def qmap(hb, s, ir, jr):
        return (ir[s], hb)

    def kvmap(hb, s, ir, jr):
        return (jr[s], hb)

    def rowmap(hb, s, ir, jr):
        return (hb, 0, ir[s])

    grid_spec = pltpu.PrefetchScalarGridSpec(
        num_scalar_prefetch=2, grid=(HEADS // hp, ii.shape[0]),
        in_specs=[pl.BlockSpec((t, w), qmap), pl.BlockSpec((t, w), kvmap), pl.BlockSpec((t, w), kvmap),
                  pl.BlockSpec((t, w), qmap), pl.BlockSpec((hp, 1, t), rowmap), pl.BlockSpec((hp, 1, t), rowmap)],
        out_specs=[pl.BlockSpec((hp, t_rows, LANES), lambda hb, s, ir, jr: (hb, 0, 0)),
                   pl.BlockSpec((t, w), kvmap), pl.BlockSpec((t, w), kvmap)])
    return pl.pallas_call(
        body, name="mla_bwd", grid_spec=grid_spec,
        out_shape=[_sds((HEADS, t_rows, LANES), F32), _sds((t_rows, HEADS * LANES), F32),
                   _sds((t_rows, HEADS * LANES), F32)],
        compiler_params=pltpu.CompilerParams(dimension_semantics=("arbitrary",) * 2, vmem_limit_bytes=VMEM_LIMIT),
    )(ii, jj, q, k, v, do, lse, delta)


SWA_TILE = 2 * SWA_WINDOW
SWA_GROUP = HEADS // A_KV_HEADS


def _swa_bias(tq):
    koff = lax.broadcasted_iota(jnp.int32, (tq + SWA_WINDOW, SWA_GROUP * tq), 0) - SWA_WINDOW
    qoff = (lax.broadcasted_iota(jnp.int32, (tq + SWA_WINDOW, SWA_GROUP * tq), 1) % tq)
    band = (koff <= qoff) & (qoff - koff < SWA_WINDOW)
    return jnp.stack([jnp.where(band & (koff >= 0), 0.0, NEG), jnp.where(band, 0.0, NEG)]).astype(F32)


def _swa_specs(tq, nq):
    wb = tq // SWA_WINDOW

    def qi(i):
        return jnp.minimum(i, nq - 1)

    q = pl.BlockSpec((tq, SWA_GROUP * LANES), lambda h, i: (qi(i), h))
    cur = pl.BlockSpec((tq, LANES), lambda h, i: (qi(i), h))
    prev = pl.BlockSpec((SWA_WINDOW, LANES), lambda h, i: (jnp.maximum(qi(i) * wb - 1, 0), h))
    bias = pl.BlockSpec((1, tq + SWA_WINDOW, SWA_GROUP * tq), lambda h, i: (jnp.minimum(i, 1), 0, 0))
    rows = pl.BlockSpec((1, 1, 1, SWA_GROUP * tq), lambda h, i: (h, qi(i), 0, 0))
    sink = pl.BlockSpec((1, 1, SWA_GROUP * tq), lambda h, i: (h, 0, 0))
    return q, cur, prev, bias, rows, sink


def _stack_heads(ref):
    return jnp.concatenate([ref[:, g * LANES:(g + 1) * LANES] for g in range(SWA_GROUP)], axis=0)


def _swa_fwd(q, k, v, bias, sink_rows):
    t_rows = q.shape[0]
    tq = min(SWA_TILE, t_rows)
    nq = t_rows // tq
    qs_, cur, prev, bs, rows, sk = _swa_specs(tq, nq)

    def body(q_ref, kc_ref, kp_ref, vc_ref, vp_ref, b_ref, sink_ref, o_ref, lse_ref):
        qs = _stack_heads(q_ref)
        kk = jnp.concatenate([kp_ref[...], kc_ref[...]], axis=0)
        vv = jnp.concatenate([vp_ref[...], vc_ref[...]], axis=0)
        s = _dot_nt(kk, qs) + b_ref[0]
        sink = sink_ref[0]
        m = jnp.maximum(jnp.max(s, axis=0, keepdims=True), sink)
        p = jnp.exp(s - m)
        l = jnp.sum(p, axis=0, keepdims=True) + jnp.exp(sink - m)
        o = (_dot_tn(vv, p.astype(BF16)) / l).T
        for g in range(SWA_GROUP):
            o_ref[:, g * LANES:(g + 1) * LANES] = o[g * tq:(g + 1) * tq].astype(o_ref.dtype)
        lse_ref[0, 0] = m + jnp.log(l)

    return pl.pallas_call(
        body, name="swa_fwd", grid=(A_KV_HEADS, nq),
        in_specs=[qs_, cur, prev, cur, prev, bs, sk],
        out_specs=[qs_, rows],
        out_shape=[_sds((t_rows, HEADS * LANES), BF16), _sds((A_KV_HEADS, nq, 1, SWA_GROUP * tq), F32)],
        compiler_params=pltpu.CompilerParams(dimension_semantics=("arbitrary",) * 2, vmem_limit_bytes=VMEM_LIMIT),
    )(q, k, k, v, v, bias, sink_rows)


def _swa_bwd(q, k, v, o, do, lse, bias, sink_rows):
    t_rows = q.shape[0]
    tq = min(SWA_TILE, t_rows)
    nq = t_rows // tq
    qs_, cur, prev, bs, rows, sk = _swa_specs(tq, nq)
    hw = SWA_WINDOW

    def body(q_ref, kc_ref, kp_ref, vc_ref, vp_ref, o_ref, do_ref, lse_ref, b_ref, sink_ref,
             dq_ref, dk_ref, dv_ref, dsink_ref, ck, cv, dsa):
        i = pl.program_id(1)

        @pl.when(i == 0)
        def _():
            dsa[...] = jnp.zeros(dsa.shape, F32)

        @pl.when(i < nq)
        def _():
            qs = _stack_heads(q_ref)
            dos = _stack_heads(do_ref)
            kk = jnp.concatenate([kp_ref[...], kc_ref[...]], axis=0)
            vv = jnp.concatenate([vp_ref[...], vc_ref[...]], axis=0)
            lse = lse_ref[0, 0]
            p = jnp.exp(_dot_nt(kk, qs) + b_ref[0] - lse)
            delta = jnp.sum((_stack_heads(o_ref).astype(F32) * dos.astype(F32)).T, axis=0, keepdims=True)
            dsa[...] += -jnp.exp(sink_ref[0] - lse) * delta
            dv = _dot(p.astype(BF16), dos)
            ds = (p * (_dot_nt(vv, dos) - delta)).astype(BF16)
            dk = _dot(ds, qs)
            dq = _dot_tn(ds, kk)
            for g in range(SWA_GROUP):
                dq_ref[:, g * LANES:(g + 1) * LANES] = dq[g * tq:(g + 1) * tq]

            @pl.when(i > 0)
            def _():
                dk_ref[0:tq - hw, :] = ck[0:tq - hw, :]
                dk_ref[tq - hw:tq, :] = ck[tq - hw:tq, :] + dk[0:hw]
                dv_ref[0:tq - hw, :] = cv[0:tq - hw, :]
                dv_ref[tq - hw:tq, :] = cv[tq - hw:tq, :] + dv[0:hw]

            ck[...] = dk[hw:hw + tq]
            cv[...] = dv[hw:hw + tq]

        @pl.when(i == nq)
        def _():
            dk_ref[...] = ck[...]
            dv_ref[...] = cv[...]
            dsink_ref[...] = jnp.zeros(dsink_ref.shape, F32)
            for g in range(SWA_GROUP):
                tot = jnp.sum(dsa[:, g * tq:(g + 1) * tq], axis=1, keepdims=True)
                dsink_ref[0, g:g + 1, :] = jnp.zeros((1, LANES), F32) + tot

    kv_out = pl.BlockSpec((tq, LANES), lambda h, i: (jnp.maximum(i - 1, 0), h))
    return pl.pallas_call(
        body, name="swa_bwd", grid=(A_KV_HEADS, nq + 1),
        in_specs=[qs_, cur, prev, cur, prev, qs_, qs_, rows, bs, sk],
        out_specs=[qs_, kv_out, kv_out, pl.BlockSpec((1, 8, LANES), lambda h, i: (h, 0, 0))],
        out_shape=[_sds((t_rows, HEADS * LANES), F32), _sds((t_rows, A_KV_HEADS * LANES), F32),
                   _sds((t_rows, A_KV_HEADS * LANES), F32), _sds((A_KV_HEADS, 8, LANES), F32)],
        scratch_shapes=[pltpu.VMEM((tq, LANES), F32), pltpu.VMEM((tq, LANES), F32),
                        pltpu.VMEM((1, SWA_GROUP * tq), F32)],
        compiler_params=pltpu.CompilerParams(dimension_semantics=("arbitrary",) * 2, vmem_limit_bytes=VMEM_LIMIT),
    )(q, k, k, v, v, o, do, lse, bias, sink_rows)


def _fwd_mix(x, ya, yb, gate, wba, wbb, wout, g2, g3, tm):
    t_rows = x.shape[0]

    def body(x_ref, ya_ref, yb_ref, gate_ref, wba_ref, wbb_ref, wout_ref, g2_ref, g3_ref,
             pa_ref, pb_ref, mixed_ref, o_ref, x1_ref, h2_ref):
        pa = _dot(ya_ref[...], wba_ref[...])
        pb = _dot(yb_ref[...], wbb_ref[...])
        pa_ref[...] = pa
        pb_ref[...] = pb
        mixed = (gate_ref[:, 0:D_MODEL] * pa + gate_ref[:, D_MODEL:2 * D_MODEL] * pb).astype(BF16)
        mixed_ref[...] = mixed
        o = _dot(mixed, wout_ref[...])
        o_ref[...] = o
        on, _ = _rms_stats(o)
        x1 = x_ref[...] + on * g2_ref[...]
        x1_ref[...] = x1
        x1n, _ = _rms_stats(x1)
        h2_ref[...] = (x1n * g3_ref[...]).astype(BF16)

    def o_(dt):
        return (_sds((t_rows, D_MODEL), dt), _row(tm, D_MODEL))

    ins = [(x, _row(tm, D_MODEL)), (ya, _row(tm, 1024)), (yb, _row(tm, 1024)), (gate, _row(tm, 2048)),
           (wba, _full(wba.shape)), (wbb, _full(wbb.shape)), (wout, _full(wout.shape)),
           (g2, _full(g2.shape)), (g3, _full(g3.shape))]
    return _rows_call("fwd_mix", body, t_rows, tm, ins, [o_(F32), o_(F32), o_(BF16), o_(F32), o_(F32), o_(BF16)])


CONV_CHUNK = 1408


def _fwd_up(h2, wup, convw8, convb, tm):
    t_rows = h2.shape[0]
    cdim = 2 * D_FF

    def body(h2_ref, wup_ref, cw_ref, cb_ref, up_ref, a_ref, carry):
        i = pl.program_id(0)

        @pl.when(i == 0)
        def _():
            carry[...] = jnp.zeros(carry.shape, F32)

        hb = h2_ref[...]

        def conv(c0):
            sl = slice(c0, c0 + CONV_CHUNK)
            up = _dot(hb, wup_ref[:, sl])
            up_ref[:, sl] = up
            xm1, xm2 = _conv_taps(up, carry[6:7, sl], carry[7:8, sl])
            u = cw_ref[0:1, sl] * xm2 + cw_ref[1:2, sl] * xm1 + cw_ref[2:3, sl] * up + cb_ref[:, sl]
            carry[:, sl] = up[tm - 8:tm, :]
            return u

        for c0 in range(0, D_FF, CONV_CHUNK):
            ug = conv(c0)
            uv = conv(D_FF + c0)
            gel, _ = _gelu_and_grad(ug)
            a_ref[:, c0:c0 + CONV_CHUNK] = (gel * uv).astype(BF16)

    ins = [(h2, _row(tm, D_MODEL)), (wup, _full(wup.shape)), (convw8, _full(convw8.shape)), (convb, _full(convb.shape))]
    outs = [(_sds((t_rows, cdim), F32), _row(tm, cdim)), (_sds((t_rows, D_FF), BF16), _row(tm, D_FF))]
    return _rows_call("fwd_up", body, t_rows, tm, ins, outs, scratch=[pltpu.VMEM((8, cdim), F32)])


def _fwd_out(a, wdown, x1, g4, p, wple, g5, wpg, tgt, tm):
    t_rows = a.shape[0]

    def body(a_ref, wdown_ref, x1_ref, g4_ref, p_ref, wple_ref, g5_ref, wpg_ref, tgt_ref,
             ff_ref, x2_ref, e_ref, n5_ref, sg_ref, dx3_ref, loss_ref):
        i = pl.program_id(0)
        ff = _dot(a_ref[...], wdown_ref[...])
        ff_ref[...] = ff
        ffn, _ = _rms_stats(ff)
        x2 = x1_ref[...] + ffn * g4_ref[...]
        x2_ref[...] = x2
        e = _dot(p_ref[...].astype(BF16), wple_ref[...])
        e_ref[...] = e
        x2n, _ = _rms_stats(x2)
        n5 = (x2n * g5_ref[...]).astype(BF16)
        n5_ref[...] = n5
        sg = _sigmoid(_dot(n5, wpg_ref[...]))
        sg_ref[...] = sg
        d = x2 + sg * e - tgt_ref[...]
        dx3_ref[...] = d * (1.0 / D_MODEL)

        @pl.when(i == 0)
        def _():
            loss_ref[...] = jnp.zeros((1, 1), F32)

        loss_ref[...] += 0.5 * jnp.sum(jnp.sum(d * d, axis=1, keepdims=True), axis=0, keepdims=True) * (1.0 / D_MODEL)

    def o_(dt):
        return (_sds((t_rows, D_MODEL), dt), _row(tm, D_MODEL))

    ins = [(a, _row(tm, D_FF)), (wdown, _full(wdown.shape)), (x1, _row(tm, D_MODEL)), (g4, _full(g4.shape)),
           (p, _row(tm, PLE_DIM)), (wple, _full(wple.shape)), (g5, _full(g5.shape)), (wpg, _full(wpg.shape)),
           (tgt, _row(tm, D_MODEL))]
    outs = [o_(F32), o_(F32), o_(F32), o_(BF16), o_(F32), o_(F32), (_sds((1, 1), F32), _full((1, 1)))]
    return _rows_call("fwd_out", body, t_rows, tm, ins, outs)


def _bwd_out(dx3, e, sg, x2, ff, g5, g4, wpg, wdown, up, convw8, convb, tm):
    t_rows = dx3.shape[0]
    cdim = 2 * D_FF
    hb = tm // 8

    def body(dx3_ref, e_ref, sg_ref, x2_ref, ff_ref, g5_ref, g4_ref, wpg_ref, wdown_ref, up_ref, halo_ref, cw_ref,
             cb_ref, dpre_ref, de_ref, dx2_ref, dff_ref, du_ref, dg5_ref, dg4_ref, dcb_ref, dcw_ref):
        i = pl.program_id(0)

        @pl.when(i == 0)
        def _():
            dg5_ref[...] = jnp.zeros(dg5_ref.shape, F32)
            dg4_ref[...] = jnp.zeros(dg4_ref.shape, F32)
            dcb_ref[...] = jnp.zeros(dcb_ref.shape, F32)
            dcw_ref[...] = jnp.zeros(dcw_ref.shape, F32)

        dx3 = dx3_ref[...]
        sg = sg_ref[...]
        dpre = (dx3 * e_ref[...] * sg * (1.0 - sg)).astype(BF16)
        dpre_ref[...] = dpre
        de_ref[...] = (dx3 * sg).astype(BF16)
        dn5 = _dot_nt(dpre, wpg_ref[...])
        x2n, r5 = _rms_stats(x2_ref[...])
        d2, dg5 = _rms_bwd(dn5, x2n, r5, g5_ref[...])
        dx2 = dx3 + d2
        dx2_ref[...] = dx2
        dg5_ref[...] += dg5
        ffn, r4 = _rms_stats(ff_ref[...])
        dff, dg4 = _rms_bwd(dx2, ffn, r4, g4_ref[...])
        dg4_ref[...] += dg4
        dffb = dff.astype(BF16)
        dff_ref[...] = dffb
        keep = jnp.where(i > 0, 1.0, 0.0)

        def conv(c0):
            sl = slice(c0, c0 + CONV_CHUNK)
            up = up_ref[:, sl]
            xm1, xm2 = _conv_taps(up, halo_ref[6:7, sl] * keep, halo_ref[7:8, sl] * keep)
            u = cw_ref[0:1, sl] * xm2 + cw_ref[1:2, sl] * xm1 + cw_ref[2:3, sl] * up + cb_ref[:, sl]
            return u, up, xm1, xm2

        def grads(c0, du, up, xm1, xm2):
            sl = slice(c0, c0 + CONV_CHUNK)
            du_ref[:, sl] = du
            dcb_ref[:, sl] += jnp.sum(du, axis=0, keepdims=True)
            dcw_ref[0:1, sl] += jnp.sum(du * xm2, axis=0, keepdims=True)
            dcw_ref[1:2, sl] += jnp.sum(du * xm1, axis=0, keepdims=True)
            dcw_ref[2:3, sl] += jnp.sum(du * up, axis=0, keepdims=True)

        for c0 in range(0, D_FF, CONV_CHUNK):
            da = _dot_nt(dffb, wdown_ref[c0:c0 + CONV_CHUNK, :])
            ug, *rg = conv(c0)
            uv, *rv = conv(D_FF + c0)
            gel, dgel = _gelu_and_grad(ug)
            grads(c0, da * uv * dgel, *rg)
            grads(D_FF + c0, da * gel, *rv)

    def o_(n, dt):
        return (_sds((t_rows, n), dt), _row(tm, n))

    def acc(r, n):
        return (_sds((r, n), F32), _full((r, n)))

    halo = pl.BlockSpec((8, cdim), lambda i: (jnp.maximum(i * hb - 1, 0), 0))
    ins = [(dx3, _row(tm, D_MODEL)), (e, _row(tm, D_MODEL)), (sg, _row(tm, D_MODEL)), (x2, _row(tm, D_MODEL)),
           (ff, _row(tm, D_MODEL)), (g5, _full(g5.shape)), (g4, _full(g4.shape)), (wpg, _full(wpg.shape)),
           (wdown, _full(wdown.shape)), (up, _row(tm, cdim)), (up, halo), (convw8, _full(convw8.shape)),
           (convb, _full(convb.shape))]
    outs = [o_(D_MODEL, BF16), o_(D_MODEL, BF16), o_(D_MODEL, F32), o_(D_MODEL, BF16), o_(cdim, F32),
            acc(1, D_MODEL), acc(1, D_MODEL), acc(1, cdim), acc(8, cdim)]
    return _rows_call("bwd_out", body, t_rows, tm, ins, outs)


def _bwd_mid(du, convw8, wup, dx2, x1, g3, o, g2, wout, gate, pa, pb, wba, wbb, tm):
    t_rows = du.shape[0]
    cdim = 2 * D_FF
    hb = tm // 8
    last_blk = t_rows // 8 - 1
    n_tiles = t_rows // tm

    def body(du_ref, halo_ref, cw_ref, wup_ref, dx2_ref, x1_ref, g3_ref, o_ref, g2_ref, wout_ref, gate_ref, pa_ref,
             pb_ref, wba_ref, wbb_ref,
             dup_ref, dx1_ref, do_ref, dpa_ref, dpb_ref, dgt_ref, dya_ref, dyb_ref, dg3_ref, dg2_ref, dbg_ref):
        i = pl.program_id(0)

        @pl.when(i == 0)
        def _():
            dg3_ref[...] = jnp.zeros(dg3_ref.shape, F32)
            dg2_ref[...] = jnp.zeros(dg2_ref.shape, F32)
            dbg_ref[...] = jnp.zeros(dbg_ref.shape, F32)

        keep = jnp.where(i < n_tiles - 1, 1.0, 0.0)
        dh2 = jnp.zeros((tm, D_MODEL), F32)
        for c0 in range(0, cdim, CONV_CHUNK):
            sl = slice(c0, c0 + CONV_CHUNK)
            du = du_ref[:, sl]
            xp1, xp2 = _conv_taps_next(du, halo_ref[0:1, sl] * keep, halo_ref[1:2, sl] * keep)
            dup = (cw_ref[2:3, sl] * du + cw_ref[1:2, sl] * xp1 + cw_ref[0:1, sl] * xp2).astype(BF16)
            dup_ref[:, sl] = dup
            dh2 = dh2 + _dot_nt(dup, wup_ref[:, sl])
        x1n, r3 = _rms_stats(x1_ref[...])
        d1, dg3 = _rms_bwd(dh2, x1n, r3, g3_ref[...])
        dx1 = dx2_ref[...] + d1
        dx1_ref[...] = dx1
        dg3_ref[...] += dg3
        on, r2 = _rms_stats(o_ref[...])
        do, dg2 = _rms_bwd(dx1, on, r2, g2_ref[...])
        dg2_ref[...] += dg2
        dob = do.astype(BF16)
        do_ref[...] = dob
        dmixed = _dot_nt(dob, wout_ref[...])
        ga = gate_ref[:, 0:D_MODEL]
        gb = gate_ref[:, D_MODEL:2 * D_MODEL]
        dpa = (dmixed * ga).astype(BF16)
        dpb = (dmixed * gb).astype(BF16)
        dpa_ref[...] = dpa
        dpb_ref[...] = dpb
        dga = dmixed * pa_ref[...] * ga * (1.0 - ga)
        dgb = dmixed * pb_ref[...] * gb * (1.0 - gb)
        dgt_ref[:, 0:D_MODEL] = dga.astype(BF16)
        dgt_ref[:, D_MODEL:2 * D_MODEL] = dgb.astype(BF16)
        dbg_ref[:, 0:D_MODEL] += jnp.sum(dga, axis=0, keepdims=True)
        dbg_ref[:, D_MODEL:2 * D_MODEL] += jnp.sum(dgb, axis=0, keepdims=True)
        dya_ref[...] = _dot_nt(dpa, wba_ref[...]).astype(BF16)
        dyb_ref[...] = _dot_nt(dpb, wbb_ref[...]).astype(BF16)

    def o_(n, dt):
        return (_sds((t_rows, n), dt), _row(tm, n))

    def acc(r, n):
        return (_sds((r, n), F32), _full((r, n)))

    halo = pl.BlockSpec((8, cdim), lambda i: (jnp.minimum((i + 1) * hb, last_blk), 0))
    ins = [(du, _row(tm, cdim)), (du, halo), (convw8, _full(convw8.shape)), (wup, _full(wup.shape)),
           (dx2, _row(tm, D_MODEL)), (x1, _row(tm, D_MODEL)), (g3, _full(g3.shape)), (o, _row(tm, D_MODEL)),
           (g2, _full(g2.shape)), (wout, _full(wout.shape)), (gate, _row(tm, 2048)), (pa, _row(tm, D_MODEL)),
           (pb, _row(tm, D_MODEL)), (wba, _full(wba.shape)), (wbb, _full(wbb.shape))]
    outs = [o_(cdim, BF16), o_(D_MODEL, F32), o_(D_MODEL, BF16), o_(D_MODEL, BF16), o_(D_MODEL, BF16),
            o_(2048, BF16), o_(1024, BF16), o_(1024, BF16), acc(1, D_MODEL), acc(1, D_MODEL), acc(1, 2048)]
    return _rows_call("bwd_mid", body, t_rows, tm, ins, outs)


def _bwd_in(dqs, dks, dvs, dqm, dkm, dvm, tabs, consts, cq, ckv, gq, gkv, wuq, wk, wv, dgates, win, x, g1, dx1, tm):
    t_rows = x.shape[0]

    def body(dqs_ref, dks_ref, dvs_ref, dqm_ref, dkm_ref, dvm_ref, ca, sa1, sa2, cb, sb1, sb2, c_ref, cq_ref,
             ckv_ref, gq_ref, gkv_ref, wuq_ref, wk_ref, wv_ref, dgt_ref, win_ref, x_ref, g1_ref, dx1_ref,
             dz_ref, dqb_ref, dx_ref, dgq_ref, dgkv_ref, dg1_ref):
        i = pl.program_id(0)

        @pl.when(i == 0)
        def _():
            dgq_ref[...] = jnp.zeros(dgq_ref.shape, F32)
            dgkv_ref[...] = jnp.zeros(dgkv_ref.shape, F32)
            dg1_ref[...] = jnp.zeros(dg1_ref.shape, F32)

        ta = (ca[...], sa1[...], sa2[...])
        tb = (cb[...], sb1[...], sb2[...])
        dz_ref[:, Z_QA:Z_KA] = _rope_t(dqs_ref[...] * SCALE_A, *ta, A_HEAD_DIM // 2).astype(BF16)
        dz_ref[:, Z_KA:Z_VA] = _rope_t(dks_ref[...], *ta, A_HEAD_DIM // 2).astype(BF16)
        dz_ref[:, Z_VA:Z_CQ] = dvs_ref[...].astype(BF16)
        dqm = jnp.concatenate([dqm_ref[h] for h in range(HEADS)], axis=1)
        dqb = _rope_t(dqm * SCALE_B, *tb, ROPE_DIM // 2).astype(BF16)
        dqb_ref[...] = dqb
        dcqn = _dot_nt(dqb, wuq_ref[...])
        cqn, rq = _rms_stats(cq_ref[...])
        dcq, dgq = _rms_bwd(dcqn, cqn, rq, gq_ref[...])
        dgq_ref[...] += dgq
        dz_ref[:, Z_CQ:Z_CKV] = dcq.astype(BF16)
        dkm = dkm_ref[...]
        dslot = dkm[:, 0:LANES]
        for h in range(1, HEADS):
            dslot = dslot + dkm[:, h * LANES:(h + 1) * LANES]
        dz_ref[:, Z_KR:Z_GATE] = _rope_t(dslot * c_ref[10:11, :], *tb, ROPE_DIM // 2).astype(BF16)
        dckvn = _dot_nt(dkm.astype(BF16), wk_ref[...]) + _dot_nt(dvm_ref[...].astype(BF16), wv_ref[...])
        ckvn, rkv = _rms_stats(ckv_ref[...])
        dckv, dgkv = _rms_bwd(dckvn, ckvn, rkv, gkv_ref[...])
        dgkv_ref[...] += dgkv
        dz_ref[:, Z_CKV:Z_KR] = dckv.astype(BF16)
        dz_ref[:, Z_GATE:ZW] = dgt_ref[...]
        dh1 = _dot_nt(dz_ref[...], win_ref[...])
        xn, r1 = _rms_stats(x_ref[...])
        d0, dg1 = _rms_bwd(dh1, xn, r1, g1_ref[...])
        dg1_ref[...] += dg1
        dx_ref[...] = dx1_ref[...] + d0

    def acc(n):
        return (_sds((1, n), F32), _full((1, n)))

    ins = [(dqs, _row(tm, 1024)), (dks, _row(tm, 256)), (dvs, _row(tm, 256)), (dqm, _heads(tm, HEADS)),
           (dkm, _row(tm, 1024)), (dvm, _row(tm, 1024))] + [(t, _row(tm, LANES)) for t in tabs] + [
           (consts, _full(consts.shape)), (cq, _row(tm, 256)), (ckv, _row(tm, 128)), (gq, _full(gq.shape)),
           (gkv, _full(gkv.shape)), (wuq, _full(wuq.shape)), (wk, _full(wk.shape)), (wv, _full(wv.shape)),
           (dgates, _row(tm, 2048)), (win, _full(win.shape)), (x, _row(tm, D_MODEL)), (g1, _full(g1.shape)),
           (dx1, _row(tm, D_MODEL))]
    outs = [(_sds((t_rows, ZW), BF16), _row(tm, ZW)), (_sds((t_rows, 1024), BF16), _row(tm, 1024)),
            (_sds((t_rows, D_MODEL), F32), _row(tm, D_MODEL)), acc(256), acc(128), acc(D_MODEL)]
    return _rows_call("bwd_in", body, t_rows, tm, ins, outs)


def _pick_cols(n):
    best = LANES
    for d in range(LANES, min(n, 1408) + 1, LANES):
        if n % d == 0:
            best = d
    return best


def _mm_tn(name, a, b):
    t_rows, m = a.shape
    n = b.shape[1]
    bk = min(512, t_rows)
    bm, bn = _pick_cols(m), _pick_cols(n)

    def body(a_ref, b_ref, o_ref):
        @pl.when(pl.program_id(2) == 0)
        def _():
            o_ref[...] = jnp.zeros((bm, bn), F32)

        o_ref[...] += _dot_tn(a_ref[...].astype(BF16), b_ref[...].astype(BF16))

    return pl.pallas_call(
        body, name=name, grid=(m // bm, n // bn, t_rows // bk),
        in_specs=[pl.BlockSpec((bk, bm), lambda i, j, k: (k, i)), pl.BlockSpec((bk, bn), lambda i, j, k: (k, j))],
        out_specs=pl.BlockSpec((bm, bn), lambda i, j, k: (i, j)),
        out_shape=_sds((m, n), F32),
        compiler_params=pltpu.CompilerParams(dimension_semantics=("arbitrary",) * 3, vmem_limit_bytes=VMEM_LIMIT),
    )(a, b)


PACK_ROWS = 512


def _pack_tile(rows):
    assert rows % PACK_ROWS == 0
    return PACK_ROWS


def _add_pair(name, a, b):
    _, rows, _ = a.shape
    t = _pack_tile(rows)

    def body(a_ref, b_ref, o_ref):
        o_ref[...] = (a_ref[...] + b_ref[...]).astype(BF16)

    spec = pl.BlockSpec((4, t, LANES), lambda i: (0, i, 0))
    return pl.pallas_call(body, name=name, grid=(rows // t,), in_specs=[spec, spec], out_specs=spec,
                          out_shape=_sds(a.shape, BF16))(a, b)


def _add_chips(name, parts):
    _, rows, _ = parts.shape
    t = _pack_tile(rows)

    def body(p_ref, o_ref):
        acc = p_ref[0].astype(F32)
        for j in range(1, 4):
            acc = acc + p_ref[j].astype(F32)
        o_ref[...] = acc

    return pl.pallas_call(body, name=name, grid=(rows // t,),
                          in_specs=[pl.BlockSpec((4, t, LANES), lambda i: (0, i, 0))],
                          out_specs=pl.BlockSpec((t, LANES), lambda i: (i, 0)),
                          out_shape=_sds((rows, LANES), F32))(parts)


def _add_devices(parts):
    n, rows, _ = parts.shape

    def body(p_ref, o_ref):
        acc = p_ref[0]
        for j in range(1, n):
            acc = acc + p_ref[j]
        o_ref[...] = acc

    return pl.pallas_call(body, name="small_add", grid=(1,),
                          in_specs=[pl.BlockSpec((n, rows, LANES), lambda i: (0, 0, 0))],
                          out_specs=pl.BlockSpec((rows, LANES), lambda i: (0, 0)),
                          out_shape=_sds((rows, LANES), F32))(parts)


def _adam_rows(k, n):
    target = max(8, (1 << 20) // (4 * n))
    if k <= target:
        return k
    best = None
    for d in range(8, target + 1, 8):
        if k % d == 0:
            best = d
    return best if best is not None else k


def _adamw(name, w, g, m, v):
    k, n = w.shape
    bk = _adam_rows(k, n)
    c1 = 1.0 - ADAM_B1 ** ADAM_STEP
    c2 = 1.0 - ADAM_B2 ** ADAM_STEP

    def body(w_ref, g_ref, m_ref, v_ref, d_ref, mo_ref, vo_ref):
        g_ = g_ref[...]
        m_ = ADAM_B1 * m_ref[...] + (1.0 - ADAM_B1) * g_
        v_ = ADAM_B2 * v_ref[...] + (1.0 - ADAM_B2) * (g_ * g_)
        mo_ref[...] = m_
        vo_ref[...] = v_
        d_ref[...] = -ADAM_LR * ((m_ / c1) / (jnp.sqrt(v_ / c2) + ADAM_EPS) + ADAM_WD * w_ref[...])

    spec = pl.BlockSpec((bk, n), lambda i: (i, 0))
    return pl.pallas_call(body, name=name, grid=(k // bk,), in_specs=[spec] * 4, out_specs=[spec] * 3,
                          out_shape=[_sds((k, n), F32)] * 3,
                          compiler_params=pltpu.CompilerParams(vmem_limit_bytes=VMEM_LIMIT))(w, g, m, v)


_HBM = pl.BlockSpec(memory_space=pltpu.HBM)


def _me():
    return lax.axis_index("x"), lax.axis_index("y"), lax.axis_index("c")


def _other_chips(x, y):
    return [(1 - x, y), (x, 1 - y), (1 - x, 1 - y)]


def _gather_weights(shard):
    def body(x_ref, out_ref, send_sems, recv_sems):
        x, y, c = _me()
        sibling = (x, y, 1 - c)
        chips = _other_chips(x, y)

        def slot(px, py, pc):
            return out_ref.at[2 * px + py, pc]

        def copy(k, src, dst, to):
            return pltpu.make_async_remote_copy(src_ref=src, dst_ref=dst, send_sem=send_sems.at[k],
                                                recv_sem=recv_sems.at[k], device_id=to, device_id_type=MESH)

        first = [copy(j, x_ref.at[c], slot(x, y, c), (*chip, c)) for j, chip in enumerate(chips)]
        for cp in first:
            cp.start()
        passed = [copy(3 + j, slot(*chip, c), slot(*chip, c), sibling) for j, chip in enumerate(chips)]
        for j, chip in enumerate(chips):
            copy(j, x_ref.at[c], slot(*chip, c), (*chip, c)).wait_recv()
            passed[j].start()
        for j, chip in enumerate(chips):
            copy(3 + j, slot(*chip, 1 - c), slot(*chip, 1 - c), sibling).wait_recv()
        for cp in first + passed:
            cp.wait_send()

    return pl.pallas_call(
        body, name="gather_weights", out_shape=_sds((4,) + shard.shape, shard.dtype), in_specs=[_HBM], out_specs=_HBM,
        scratch_shapes=[pltpu.SemaphoreType.DMA((6,)), pltpu.SemaphoreType.DMA((6,))],
    )(shard)


def _swap_sibling(name, v):
    def body(v_ref, out_ref, send_sem, recv_sem):
        x, y, c = _me()
        cp = pltpu.make_async_remote_copy(src_ref=v_ref, dst_ref=out_ref, send_sem=send_sem, recv_sem=recv_sem,
                                          device_id=(x, y, 1 - c), device_id_type=MESH)
        cp.start()
        cp.wait()

    return pl.pallas_call(
        body, name=name, out_shape=_sds(v.shape, v.dtype), in_specs=[_HBM], out_specs=_HBM,
        scratch_shapes=[pltpu.SemaphoreType.DMA, pltpu.SemaphoreType.DMA],
    )(v)


def _scatter_chips(s):
    def body(s_ref, out_ref, send_sems, recv_sems):
        x, y, c = _me()
        k = 2 * x + y
        chips = _other_chips(x, y)
        sends = [pltpu.make_async_remote_copy(src_ref=s_ref.at[2 * cx + cy], dst_ref=out_ref.at[k],
                                              send_sem=send_sems.at[j], recv_sem=recv_sems.at[j],
                                              device_id=(cx, cy, c), device_id_type=MESH)
                 for j, (cx, cy) in enumerate(chips)]
        for cp in sends:
            cp.start()
        for j, (cx, cy) in enumerate(chips):
            pltpu.make_async_remote_copy(src_ref=s_ref.at[k], dst_ref=out_ref.at[2 * cx + cy],
                                         send_sem=send_sems.at[j], recv_sem=recv_sems.at[j],
                                         device_id=(cx, cy, c), device_id_type=MESH).wait_recv()
        for cp in sends:
            cp.wait_send()

    return pl.pallas_call(
        body, name="scatter_chips", out_shape=_sds(s.shape, s.dtype), in_specs=[_HBM], out_specs=_HBM,
        scratch_shapes=[pltpu.SemaphoreType.DMA((3,)), pltpu.SemaphoreType.DMA((3,))],
    )(s)


_SEM = pl.BlockSpec(memory_space=pltpu.SEMAPHORE)
_EFFECT = pltpu.SideEffectType.DATAFLOW_SIDE_EFFECTING


def _chip_copies(v_ref, land_ref, send_sems, recv_sems, per_chip_piece):
    x, y, c = _me()
    k = 2 * x + y
    out = []
    for j, (cx, cy) in enumerate(_other_chips(x, y)):
        src = v_ref.at[2 * cx + cy] if per_chip_piece else v_ref
        send = pltpu.make_async_remote_copy(src_ref=src, dst_ref=land_ref.at[k], send_sem=send_sems.at[j],
                                            recv_sem=recv_sems.at[j], device_id=(cx, cy, c), device_id_type=MESH)
        recv = pltpu.make_async_remote_copy(src_ref=src, dst_ref=land_ref.at[2 * cx + cy], send_sem=send_sems.at[j],
                                            recv_sem=recv_sems.at[j], device_id=(cx, cy, c), device_id_type=MESH)
        out.append((send, recv))
    return out


def _chips_start(name, v, land_shape, per_chip_piece, after=None):
    def body(*refs):
        v_ref, land_ref = refs[0], refs[1]
        send_sems, recv_sems, _, _, token = refs[-5:]
        for send, _ in _chip_copies(v_ref, land_ref, send_sems, recv_sems, per_chip_piece):
            send.start()
        token[...] = jnp.zeros_like(token)

    extra = () if after is None else (after,)
    return pl.pallas_call(
        body, name=name,
        out_shape=(pltpu.SemaphoreType.DMA((3,)), pltpu.SemaphoreType.DMA((3,)), pltpu.HBM(v.shape, v.dtype),
                   pltpu.HBM(land_shape, v.dtype), _sds((8, LANES), F32)),
        in_specs=(_HBM, _HBM) + (pl.BlockSpec(memory_space=pl.ANY),) * len(extra),
        out_specs=(_SEM, _SEM, _HBM, _HBM, pl.BlockSpec(memory_space=pltpu.VMEM)),
        input_output_aliases={0: 2, 1: 3},
        compiler_params=pltpu.CompilerParams(has_side_effects=_EFFECT),
    )(pltpu.with_memory_space_constraint(v, pltpu.HBM),
      pltpu.with_memory_space_constraint(lax.empty(land_shape, v.dtype), pltpu.HBM), *extra)


def _chips_wait(name, send_sems, recv_sems, v_thru, land_thru, per_chip_piece, after):
    def body(v_ref, land_ref, send_sems, recv_sems, after_ref, v_dead, got_ref):
        for send, recv in _chip_copies(v_ref, land_ref, send_sems, recv_sems, per_chip_piece):
            send.wait_send()
            recv.wait_recv()

    return pl.pallas_call(
        body, name=name,
        out_shape=(pltpu.HBM(v_thru.shape, v_thru.dtype), pltpu.HBM(land_thru.shape, land_thru.dtype)),
        in_specs=(_HBM, _HBM, _SEM, _SEM, pl.BlockSpec(memory_space=pl.ANY)), out_specs=(_HBM, _HBM),
        input_output_aliases={0: 0, 1: 1},
        compiler_params=pltpu.CompilerParams(has_side_effects=_EFFECT),
    )(v_thru, land_thru, send_sems, recv_sems, after)[1]


def _gather_small(name, v):
    def body(v_ref, out_ref, send_sems, recv_sems, local_sem):
        x, y, c = _me()
        me = 4 * x + 2 * y + c
        mine = pltpu.make_async_copy(v_ref, out_ref.at[me], local_sem)
        mine.start()
        peers = []
        for f in range(1, 8):
            fx, fy, fc = (f >> 2) & 1, (f >> 1) & 1, f & 1
            px = 1 - x if fx else x
            py = 1 - y if fy else y
            pc = 1 - c if fc else c
            peers.append((f - 1, (px, py, pc)))
        sends = [pltpu.make_async_remote_copy(src_ref=v_ref, dst_ref=out_ref.at[me], send_sem=send_sems.at[k],
                                              recv_sem=recv_sems.at[k], device_id=peer, device_id_type=MESH)
                 for k, peer in peers]
        for cp in sends:
            cp.start()
        for k, (px, py, pc) in peers:
            pltpu.make_async_remote_copy(src_ref=v_ref, dst_ref=out_ref.at[4 * px + 2 * py + pc],
                                         send_sem=send_sems.at[k], recv_sem=recv_sems.at[k],
                                         device_id=(px, py, pc), device_id_type=MESH).wait_recv()
        for cp in sends:
            cp.wait_send()
        mine.wait()

    return pl.pallas_call(
        body, name=name, out_shape=_sds((8,) + v.shape, v.dtype), in_specs=[_HBM], out_specs=_HBM,
        scratch_shapes=[pltpu.SemaphoreType.DMA((7,)), pltpu.SemaphoreType.DMA((7,)), pltpu.SemaphoreType.DMA],
    )(v)


_BIG = (("w_in", (1024, 3232), 1), ("w_uq", (256, 768), 1), ("w_ukv", (128, 1024), 1), ("w_branch_a", (512, 1024), 1),
        ("w_branch_b", (512, 1024), 1), ("w_out", (1024, 1024), 0), ("w_up", (1024, 5632), 1),
        ("w_down", (2816, 1024), 0), ("w_ple_gate", (1024, 1024), 0), ("w_ple", (256, 1024), 1))


def _shard_shape(shape, axis):
    return (shape[0] // 4, shape[1]) if axis == 0 else (shape[0], shape[1] // 4)


def _half_rows(shape, axis):
    k, n = _shard_shape(shape, axis)
    return k * n // (2 * LANES)


_EARLY = tuple(b for b in _BIG if b[0] in ("w_in", "w_uq", "w_ukv"))
_LATE = tuple(b for b in _BIG if b[0] not in ("w_in", "w_uq", "w_ukv"))


def _pack_pad(group):
    return -sum(_half_rows(shape, axis) for _, shape, axis in group) % PACK_ROWS


def _pack_shards(shards, dtype, group):
    parts = [shards[name].astype(dtype).reshape(2, _half_rows(shape, axis), LANES) for name, shape, axis in group]
    return jnp.concatenate(parts + [jnp.zeros((2, _pack_pad(group), LANES), dtype)], axis=1)


def _unpack_gathered(g, group):
    out, off = {}, 0
    for name, shape, axis in group:
        r = _half_rows(shape, axis)
        k, n = _shard_shape(shape, axis)
        w = g[:, :, off:off + r, :].reshape(4, k, n)
        out[name] = w.reshape(shape) if axis == 0 else w.transpose(1, 0, 2).reshape(shape)
        off += r
    return out


def _pack_grad_halves(grads, c, group):
    keep, send = [], []
    for name, shape, axis in group:
        k, n = _shard_shape(shape, axis)
        r = _half_rows(shape, axis)
        g = grads[name]
        g4 = g.reshape(4, k, n) if axis == 0 else g.reshape(k, 4, n).transpose(1, 0, 2)
        g4 = g4.reshape(4, 2, r, LANES)
        keep.append(lax.dynamic_index_in_dim(g4, c, 1, keepdims=False))
        send.append(lax.dynamic_index_in_dim(g4, 1 - c, 1, keepdims=False))
    pad = [jnp.zeros((4, _pack_pad(group), LANES), F32)]
    return jnp.concatenate(keep + pad, axis=1), jnp.concatenate(send + pad, axis=1)


def _unpack_shard_grads(f, group):
    out, off = {}, 0
    for name, shape, axis in group:
        r = _half_rows(shape, axis)
        out[name] = f[:, off:off + r, :].reshape(_shard_shape(shape, axis))
        off += r
    return out


def _pad_slots(w, heads, dim, axis):
    if axis == 1:
        k = w.shape[0]
        return jnp.pad(w.reshape(k, heads, dim), ((0, 0), (0, 0), (0, LANES - dim))).reshape(k, heads * LANES)
    n = w.shape[1]
    return jnp.pad(w.reshape(heads, dim, n), ((0, 0), (0, LANES - dim), (0, 0))).reshape(heads * LANES, n)


def _unpad_slots(w, heads, dim, axis):
    if axis == 1:
        k = w.shape[0]
        return w.reshape(k, heads, LANES)[:, :, :dim].reshape(k, heads * dim)
    n = w.shape[1]
    return w.reshape(heads, LANES, n)[:, :dim, :].reshape(heads * dim, n)


def _pad_w_in(w):
    kr = jnp.pad(w[:, 1152:1184], ((0, 0), (NOPE_DIM, LANES - NOPE_DIM - ROPE_DIM)))
    return jnp.concatenate([_pad_slots(w[:, 0:512], HEADS, A_HEAD_DIM, 1),
                            _pad_slots(w[:, 512:640], A_KV_HEADS, A_HEAD_DIM, 1),
                            _pad_slots(w[:, 640:768], A_KV_HEADS, A_HEAD_DIM, 1),
                            w[:, 768:1024], w[:, 1024:1152], kr, w[:, 1184:3232]], axis=1)


def _unpad_w_in(w):
    return jnp.concatenate([_unpad_slots(w[:, Z_QA:Z_KA], HEADS, A_HEAD_DIM, 1),
                            _unpad_slots(w[:, Z_KA:Z_VA], A_KV_HEADS, A_HEAD_DIM, 1),
                            _unpad_slots(w[:, Z_VA:Z_CQ], A_KV_HEADS, A_HEAD_DIM, 1),
                            w[:, Z_CQ:Z_CKV], w[:, Z_CKV:Z_KR],
                            w[:, Z_KR + NOPE_DIM:Z_KR + NOPE_DIM + ROPE_DIM], w[:, Z_GATE:ZW]], axis=1)


_SMALL = (("attn_pre_norm", 1024), ("attn_post_norm", 1024), ("b_gate", 2048), ("sinks", 8), ("q_a_norm", 256),
          ("kv_a_norm", 128), ("mlp_pre_norm", 1024), ("mlp_post_norm", 1024), ("conv_b", 5632), ("ple_norm", 1024),
          ("conv_w", 3 * 5632), ("loss", 1))


def _small_rows(n):
    return 8 * -(-n // (8 * LANES))


def _pack_small(vals):
    parts = []
    for name, n in _SMALL:
        r = _small_rows(n)
        parts.append(jnp.pad(vals[name].reshape(-1), (0, r * LANES - n)).reshape(r, LANES))
    return jnp.concatenate(parts, axis=0)


def _unpack_small(buf):
    out, off = {}, 0
    for name, n in _SMALL:
        r = _small_rows(n)
        out[name] = buf[off:off + r].reshape(-1)[:n]
        off += r
    return out


def kernel(x, p, positions, attn_pre_norm, attn_post_norm, w_in, b_gate, sinks, q_a_norm, w_uq, kv_a_norm, w_ukv, w_branch_a, w_branch_b, w_out, mlp_pre_norm, mlp_post_norm, w_up, conv_w, conv_b, w_down, ple_norm, w_ple_gate, w_ple, loss_target, m_attn_pre_norm, m_attn_post_norm, m_w_in, m_b_gate, m_sinks, m_q_a_norm, m_w_uq, m_kv_a_norm, m_w_ukv, m_w_branch_a, m_w_branch_b, m_w_out, m_mlp_pre_norm, m_mlp_post_norm, m_w_up, m_conv_w, m_conv_b, m_w_down, m_ple_norm, m_w_ple_gate, m_w_ple, v_attn_pre_norm, v_attn_post_norm, v_w_in, v_b_gate, v_sinks, v_q_a_norm, v_w_uq, v_kv_a_norm, v_w_ukv, v_w_branch_a, v_w_branch_b, v_w_out, v_mlp_pre_norm, v_mlp_post_norm, v_w_up, v_conv_w, v_conv_b, v_w_down, v_ple_norm, v_w_ple_gate, v_w_ple):
    names = ["attn_pre_norm", "attn_post_norm", "w_in", "b_gate", "sinks", "q_a_norm", "w_uq", "kv_a_norm", "w_ukv",
             "w_branch_a", "w_branch_b", "w_out", "mlp_pre_norm", "mlp_post_norm", "w_up", "conv_w", "conv_b",
             "w_down", "ple_norm", "w_ple_gate", "w_ple"]
    wts = dict(zip(names, [attn_pre_norm, attn_post_norm, w_in, b_gate, sinks, q_a_norm, w_uq, kv_a_norm, w_ukv,
                           w_branch_a, w_branch_b, w_out, mlp_pre_norm, mlp_post_norm, w_up, conv_w, conv_b, w_down,
                           ple_norm, w_ple_gate, w_ple]))
    moms = dict(zip(names, [m_attn_pre_norm, m_attn_post_norm, m_w_in, m_b_gate, m_sinks, m_q_a_norm, m_w_uq,
                            m_kv_a_norm, m_w_ukv, m_w_branch_a, m_w_branch_b, m_w_out, m_mlp_pre_norm,
                            m_mlp_post_norm, m_w_up, m_conv_w, m_conv_b, m_w_down, m_ple_norm, m_w_ple_gate, m_w_ple]))
    vars_ = dict(zip(names, [v_attn_pre_norm, v_attn_post_norm, v_w_in, v_b_gate, v_sinks, v_q_a_norm, v_w_uq,
                             v_kv_a_norm, v_w_ukv, v_w_branch_a, v_w_branch_b, v_w_out, v_mlp_pre_norm,
                             v_mlp_post_norm, v_w_up, v_conv_w, v_conv_b, v_w_down, v_ple_norm, v_w_ple_gate, v_w_ple]))
    w2 = {n: a.reshape(a.shape[-2:]) for n, a in wts.items()}
    m2 = {n: a.reshape(a.shape[-2:]) for n, a in moms.items()}
    v2 = {n: a.reshape(a.shape[-2:]) for n, a in vars_.items()}

    t_rows = x.shape[-2]
    tm = min(256, t_rows)
    xc, yc, cc = lax.axis_index("x"), lax.axis_index("y"), lax.axis_index("c")
    chip = 2 * xc + yc

    x2d = x.reshape(t_rows, D_MODEL)
    p2d = p.reshape(t_rows, PLE_DIM)
    tgt = loss_target.reshape(t_rows, D_MODEL)
    pos_f = positions.reshape(t_rows, 1).astype(F32)

    early_shard = _pack_shards(w2, BF16, _EARLY)
    late_shard = _pack_shards(w2, BF16, _LATE)
    early = lax.dynamic_update_slice(_gather_weights(early_shard), early_shard[None], (chip, 0, 0, 0))
    late_sems = _chips_start("gather_late_start", late_shard, (4,) + late_shard.shape, False, after=early)
    late_token = late_sems[4][0:1, 0:1]
    full = _unpack_gathered(early, _EARLY)
    cw_rows = 3 * 1408 // LANES
    cw_all = _gather_small("gather_conv_w", jnp.pad(w2["conv_w"].reshape(cw_rows, LANES), ((0, 40 - cw_rows), (0, 0))))
    conv_full = cw_all[0::2, :cw_rows].reshape(4, 3, 1408).transpose(1, 0, 2).reshape(3, 2 * D_FF)
    convw8 = jnp.pad(conv_full, ((0, 5), (0, 0)))

    win = _pad_w_in(full["w_in"])
    wuq = _pad_slots(full["w_uq"], HEADS, NOPE_DIM + ROPE_DIM, 1)
    ukv = full["w_ukv"].reshape(KV_LORA, HEADS, NOPE_DIM + V_DIM)
    wk = _pad_slots(ukv[:, :, :NOPE_DIM].reshape(KV_LORA, HEADS * NOPE_DIM), HEADS, NOPE_DIM, 1)
    wv = _pad_slots(ukv[:, :, NOPE_DIM:].reshape(KV_LORA, HEADS * V_DIM), HEADS, V_DIM, 1)
    g1, g2, g3, g4, g5 = (w2["attn_pre_norm"], w2["attn_post_norm"], w2["mlp_pre_norm"], w2["mlp_post_norm"],
                          w2["ple_norm"])
    gq, gkv, bg, convb = w2["q_a_norm"], w2["kv_a_norm"], w2["b_gate"], w2["conv_b"]
    swa_tile = min(SWA_TILE, t_rows)
    sink_rows = jnp.repeat(w2["sinks"].reshape(A_KV_HEADS, SWA_GROUP, 1), swa_tile, axis=2).reshape(
        A_KV_HEADS, 1, SWA_GROUP * swa_tile)
    swa_bias = _swa_bias(swa_tile)

    consts = _rope_consts()
    tabs = _rope_tables(pos_f, consts, tm)
    h1, qs, ks, vs, cq, cqn, ckv, ckvn, qm, km, vm, gate = _fwd_in(x2d, g1, win, bg + late_token, gq, gkv, wuq, wk, wv,
                                                                   tabs, tm)
    ya, lse_a = _swa_fwd(qs, ks, vs, swa_bias, sink_rows)
    yb, lse_b = _mla_fwd(qm, km, vm)
    late = _chips_wait("gather_late_wait", *late_sems[:4], False, after=yb)
    full = _unpack_gathered(lax.dynamic_update_slice(late, late_shard[None], (chip, 0, 0, 0)), _LATE)
    wba = _pad_slots(full["w_branch_a"], HEADS, A_HEAD_DIM, 0)
    wbb = _pad_slots(full["w_branch_b"], HEADS, V_DIM, 0)
    wout, wup, wdown, wpg, wple = full["w_out"], full["w_up"], full["w_down"], full["w_ple_gate"], full["w_ple"]
    pa, pb, mixed, o, x1, h2 = _fwd_mix(x2d, ya, yb, gate, wba, wbb, wout, g2, g3, tm)
    up, a = _fwd_up(h2, wup, convw8, convb, tm)
    ff, x2, e, n5, sg, dx3, loss_part = _fwd_out(a, wdown, x1, g4, p2d, wple, g5, wpg, tgt, tm)

    dpre, de, dx2, dff, du, dg5, dg4, dconvb, dconvw8 = _bwd_out(dx3, e, sg, x2, ff, g5, g4, wpg, wdown, up, convw8,
                                                                 convb, tm)
    dup, dx1, do, dpa, dpb, dgates, dya, dyb, dg3, dg2, dbg = _bwd_mid(du, convw8, wup, dx2, x1, g3, o, g2, wout, gate,
                                                                       pa, pb, wba, wbb, tm)
    late_grads = {
        "w_branch_a": _unpad_slots(_mm_tn("dw_branch_a", ya, dpa), HEADS, A_HEAD_DIM, 0),
        "w_branch_b": _unpad_slots(_mm_tn("dw_branch_b", yb, dpb), HEADS, V_DIM, 0),
        "w_out": _mm_tn("dw_out", mixed, do),
        "w_up": _mm_tn("dw_up", h2, dup),
        "w_down": _mm_tn("dw_down", a, dff),
        "w_ple_gate": _mm_tn("dw_ple_gate", n5, dpre),
        "w_ple": _mm_tn("dw_ple", p2d, de),
    }
    keep, send = _pack_grad_halves(late_grads, cc, _LATE)
    late_pair = _add_pair("rs_late_add_pair", keep, _swap_sibling("swap_late_grad_halves", send))
    rs_sems = _chips_start("scatter_late_start", late_pair, late_pair.shape, True)
    rs_token = rs_sems[4][0:1, 0:1]

    dqs, dks, dvs, dsink_rows = _swa_bwd(qs, ks, vs, ya, dya, lse_a, swa_bias, sink_rows + rs_token)
    dsink = dsink_rows[:, 0:SWA_GROUP, 0]
    dqm, dkm, dvm = _mla_bwd(qm, km, vm, dyb, lse_b, _mla_delta(yb, dyb))
    dz, dqb, dx, dgq, dgkv, dg1 = _bwd_in(dqs, dks, dvs, dqm, dkm, dvm, tabs, consts, cq, ckv, gq, gkv, wuq, wk, wv,
                                           dgates, win, x2d, g1, dx1, tm)

    dwk = _unpad_slots(_mm_tn("dw_k", ckvn, dkm), HEADS, NOPE_DIM, 1).reshape(KV_LORA, HEADS, NOPE_DIM)
    dwv = _unpad_slots(_mm_tn("dw_v", ckvn, dvm), HEADS, V_DIM, 1).reshape(KV_LORA, HEADS, V_DIM)
    early_grads = {
        "w_in": _unpad_w_in(_mm_tn("dw_in", h1, dz)),
        "w_uq": _unpad_slots(_mm_tn("dw_uq", cqn, dqb), HEADS, NOPE_DIM + ROPE_DIM, 1),
        "w_ukv": jnp.concatenate([dwk, dwv], axis=2).reshape(KV_LORA, HEADS * (NOPE_DIM + V_DIM)),
    }

    def finish(tag, pair, landed, group):
        own = lax.dynamic_index_in_dim(pair, chip, 0, keepdims=True)
        reduced = _add_chips("rs_%s_add_chips" % tag, lax.dynamic_update_slice(landed, own, (chip, 0, 0)))
        other = _swap_sibling("swap_%s_reduced_halves" % tag, reduced)
        both = jnp.stack([jnp.where(cc == 0, reduced, other), jnp.where(cc == 0, other, reduced)])
        return _unpack_shard_grads(both, group)

    keep, send = _pack_grad_halves(early_grads, cc, _EARLY)
    early_pair = _add_pair("rs_early_add_pair", keep, _swap_sibling("swap_early_grad_halves", send))
    shard_grads = finish("early", early_pair, _scatter_chips(early_pair), _EARLY)
    late_landed = _chips_wait("scatter_late_wait", *rs_sems[:4], True, after=early_pair)
    shard_grads.update(finish("late", late_pair, late_landed, _LATE))

    small = {"attn_pre_norm": dg1, "attn_post_norm": dg2, "b_gate": dbg, "sinks": dsink, "q_a_norm": dgq,
             "kv_a_norm": dgkv, "mlp_pre_norm": dg3, "mlp_post_norm": dg4, "conv_b": dconvb, "ple_norm": dg5,
             "conv_w": dconvw8[0:3], "loss": loss_part}
    small_sum = _unpack_small(_add_devices(_gather_small("gather_small_grads", _pack_small(small))))
    for n in names:
        if n in small_sum and n != "conv_w":
            shard_grads[n] = small_sum[n].reshape(w2[n].shape)
    shard_grads["conv_w"] = lax.dynamic_index_in_dim(small_sum["conv_w"].reshape(3, 4, 1408), chip, 1, keepdims=False)

    loss = small_sum["loss"][0]

    g_out, d_out, m_out, v_out = [], [], [], []
    for n in names:
        g = shard_grads[n]
        d, mn, vn = _adamw("adamw_" + n, w2[n], g, m2[n], v2[n])
        shp = wts[n].shape
        g_out.append(g.reshape(shp))
        d_out.append(d.reshape(shp))
        m_out.append(mn.reshape(shp))
        v_out.append(vn.reshape(shp))
    return (loss, dx.reshape(x.shape), *g_out, *d_out, *m_out, *v_out)
```

```python
import functools
import math

import numpy as np
import jax
import jax.numpy as jnp
from jax import lax
from jax.experimental import pallas as pl
from jax.experimental.pallas import tpu as pltpu

F32 = jnp.float32
BF16 = jnp.bfloat16

D_MODEL = 1024
D_FF = 2816
PLE_DIM = 256
ROPE_THETA = 10000.0
RMS_EPS = 1e-6
SWA_WINDOW = 128
HEADS = 8
A_KV_HEADS = 2
A_HEAD_DIM = 64
Q_LORA = 256
KV_LORA = 128
NOPE_DIM = 64
ROPE_DIM = 32
V_DIM = 64
LANES = 128
ZW = 4096
NEG = -1e30
SCALE_A = A_HEAD_DIM ** -0.5
SCALE_B = (NOPE_DIM + ROPE_DIM) ** -0.5

ADAM_LR = 0.001
ADAM_B1 = 0.9
ADAM_B2 = 0.999
ADAM_EPS = 1e-08
ADAM_WD = 0.01
ADAM_STEP = 10

VMEM_LIMIT = 60 * 1024 * 1024
MESH_AXES = ("x", "y", "c")
MESH = pl.DeviceIdType.MESH

Z_QA, Z_KA, Z_VA, Z_CQ, Z_CKV, Z_KR, Z_GATE = 0, 1024, 1280, 1536, 1792, 1920, 2048


def _dot(a, b):
    return jnp.dot(a, b, preferred_element_type=F32)


def _dot_nt(a, b):
    return lax.dot_general(a, b, (((1,), (1,)), ((), ())), preferred_element_type=F32)


def _dot_tn(a, b):
    return lax.dot_general(a, b, (((0,), (0,)), ((), ())), preferred_element_type=F32)


def _rms_stats(x):
    r = lax.rsqrt(jnp.mean(x * x, axis=-1, keepdims=True) + RMS_EPS)
    return x * r, r


def _rms_bwd(dy, xn, r, g):
    dxn = dy * g
    dx = r * (dxn - xn * jnp.mean(dxn * xn, axis=-1, keepdims=True))
    dg = jnp.sum(dy * xn, axis=0, keepdims=True)
    return dx, dg


def _tile_lanes(t, n):
    return t if n == 1 else jnp.concatenate([t] * n, axis=1)


def _rope(x, c, s1, s2, half):
    w = x.shape[1]
    n = w // LANES
    return (x * _tile_lanes(c, n) + pltpu.roll(x, w - half, 1) * _tile_lanes(s1, n)
            + pltpu.roll(x, half, 1) * _tile_lanes(s2, n))


def _rope_t(dy, c, s1, s2, half):
    w = dy.shape[1]
    n = w // LANES
    return (dy * _tile_lanes(c, n) + pltpu.roll(dy * _tile_lanes(s1, n), half, 1)
            + pltpu.roll(dy * _tile_lanes(s2, n), w - half, 1))


def _sigmoid(x):
    return 1.0 / (1.0 + jnp.exp(-x))


_GELU_C = math.sqrt(2.0 / math.pi)


def _gelu_and_grad(x):
    x2 = x * x
    th = jnp.tanh(_GELU_C * (x + 0.044715 * x * x2))
    gel = 0.5 * x * (1.0 + th)
    dgel = 0.5 * (1.0 + th) + 0.5 * x * (1.0 - th * th) * (_GELU_C * (1.0 + 3.0 * 0.044715 * x2))
    return gel, dgel


def _conv_taps(up, h6, h7):
    rows = lax.broadcasted_iota(jnp.int32, up.shape, 0)
    r1 = pltpu.roll(up, 1, 0)
    r2 = pltpu.roll(up, 2, 0)
    xm1 = jnp.where(rows == 0, h7, r1)
    xm2 = jnp.where(rows == 0, h6, jnp.where(rows == 1, h7, r2))
    return xm1, xm2


def _conv_taps_next(du, n0, n1):
    tm = du.shape[0]
    rows = lax.broadcasted_iota(jnp.int32, du.shape, 0)
    r1 = pltpu.roll(du, tm - 1, 0)
    r2 = pltpu.roll(du, tm - 2, 0)
    xp1 = jnp.where(rows == tm - 1, n0, r1)
    xp2 = jnp.where(rows == tm - 2, n0, jnp.where(rows == tm - 1, n1, r2))
    return xp1, xp2


def _row(tm, n):
    return pl.BlockSpec((tm, n), lambda i: (i, 0))


def _full(shape):
    nd = len(shape)
    return pl.BlockSpec(tuple(shape), lambda i: (0,) * nd)


def _heads(tm, h):
    return pl.BlockSpec((h, tm, LANES), lambda i: (0, i, 0))


def _rows_call(name, body, t_rows, tm, ins, outs, scratch=()):
    return pl.pallas_call(
        body, name=name, grid=(t_rows // tm,),
        in_specs=[s for _, s in ins],
        out_specs=[s for _, s in outs],
        out_shape=[s for s, _ in outs],
        scratch_shapes=list(scratch),
        compiler_params=pltpu.CompilerParams(dimension_semantics=("arbitrary",), vmem_limit_bytes=VMEM_LIMIT),
    )(*[a for a, _ in ins])


def _sds(shape, dtype):
    return jax.ShapeDtypeStruct(tuple(shape), dtype)


def _rope_consts():
    c = np.zeros((16, LANES), np.float32)
    lane = np.arange(LANES)
    inv_a = (ROPE_THETA ** (-(np.arange(0, A_HEAD_DIM, 2, dtype=np.float32) / A_HEAD_DIM))).astype(np.float32)
    in_a = lane < A_HEAD_DIM
    c[0, in_a] = inv_a[lane[in_a] % (A_HEAD_DIM // 2)]
    c[1, in_a] = 1.0
    c[2, lane < A_HEAD_DIM // 2] = -1.0
    c[3, (lane >= A_HEAD_DIM // 2) & in_a] = 1.0
    inv_b = (ROPE_THETA ** (-(np.arange(0, ROPE_DIM, 2, dtype=np.float32) / ROPE_DIM))).astype(np.float32)
    pe = (lane >= NOPE_DIM) & (lane < NOPE_DIM + ROPE_DIM)
    c[5, pe] = inv_b[(lane[pe] - NOPE_DIM) % (ROPE_DIM // 2)]
    c[6, pe] = 1.0
    c[7, (lane >= NOPE_DIM) & (lane < NOPE_DIM + ROPE_DIM // 2)] = -1.0
    c[8, (lane >= NOPE_DIM + ROPE_DIM // 2) & (lane < NOPE_DIM + ROPE_DIM)] = 1.0
    c[9, lane < NOPE_DIM] = 1.0
    c[10, pe] = 1.0
    return jnp.asarray(c)


def _rope_tables(pos_f, consts, tm):
    t_rows = pos_f.shape[0]

    def body(pos_ref, c_ref, ca, sa1, sa2, cb, sb1, sb2):
        pos = pos_ref[...]
        ang = pos * c_ref[0:1, :]
        cs, sn = jnp.cos(ang), jnp.sin(ang)
        ca[...] = cs * c_ref[1:2, :]
        sa1[...] = sn * c_ref[2:3, :]
        sa2[...] = sn * c_ref[3:4, :]
        ang = pos * c_ref[5:6, :]
        cs, sn = jnp.cos(ang), jnp.sin(ang)
        cb[...] = cs * c_ref[6:7, :] + c_ref[9:10, :]
        sb1[...] = sn * c_ref[7:8, :]
        sb2[...] = sn * c_ref[8:9, :]

    tab = (_sds((t_rows, LANES), F32), _row(tm, LANES))
    return _rows_call("rope_tables", body, t_rows, tm,
                      [(pos_f, _row(tm, 1)), (consts, _full(consts.shape))], [tab] * 6)


def _fwd_in(x, g1, win, bg, gq, gkv, wuq, wk, wv, tabs, tm):
    t_rows = x.shape[0]

    def body(x_ref, g1_ref, win_ref, bg_ref, gq_ref, gkv_ref, wuq_ref, wk_ref, wv_ref,
             ca, sa1, sa2, cb, sb1, sb2,
             h1_ref, qs_ref, ks_ref, vs_ref, cq_ref, cqn_ref, ckv_ref, ckvn_ref, qm_ref, km_ref, vm_ref, gate_ref):
        xn, _ = _rms_stats(x_ref[...])
        hb = (xn * g1_ref[...]).astype(BF16)
        h1_ref[...] = hb
        ta = (ca[...], sa1[...], sa2[...])
        tb = (cb[...], sb1[...], sb2[...])
        qs_ref[...] = (_rope(_dot(hb, win_ref[:, Z_QA:Z_KA]), *ta, A_HEAD_DIM // 2) * SCALE_A).astype(BF16)
        ks_ref[...] = _rope(_dot(hb, win_ref[:, Z_KA:Z_VA]), *ta, A_HEAD_DIM // 2).astype(BF16)
        vs_ref[...] = _dot(hb, win_ref[:, Z_VA:Z_CQ]).astype(BF16)
        cq = _dot(hb, win_ref[:, Z_CQ:Z_CKV])
        cq_ref[...] = cq
        cqn, _ = _rms_stats(cq)
        cqb = (cqn * gq_ref[...]).astype(BF16)
        cqn_ref[...] = cqb
        qm_ref[...] = (_rope(_dot(cqb, wuq_ref[...]), *tb, ROPE_DIM // 2) * SCALE_B).astype(BF16)
        ckv = _dot(hb, win_ref[:, Z_CKV:Z_KR])
        ckv_ref[...] = ckv
        ckvn, _ = _rms_stats(ckv)
        ckvb = (ckvn * gkv_ref[...]).astype(BF16)
        ckvn_ref[...] = ckvb
        kpe = _rope(_dot(hb, win_ref[:, Z_KR:Z_GATE]), *tb, ROPE_DIM // 2)
        km_ref[...] = (_dot(ckvb, wk_ref[...]) + _tile_lanes(kpe, HEADS)).astype(BF16)
        vm_ref[...] = _dot(ckvb, wv_ref[...]).astype(BF16)
        gate_ref[...] = _sigmoid(_dot(hb, win_ref[:, Z_GATE:ZW]) + bg_ref[...])

    def o(n, dt):
        return (_sds((t_rows, n), dt), _row(tm, n))

    ins = [(x, _row(tm, D_MODEL)), (g1, _full(g1.shape)), (win, _full(win.shape)), (bg, _full(bg.shape)),
           (gq, _full(gq.shape)), (gkv, _full(gkv.shape)), (wuq, _full(wuq.shape)), (wk, _full(wk.shape)),
           (wv, _full(wv.shape))] + [(t, _row(tm, LANES)) for t in tabs]
    outs = [o(1024, BF16), o(1024, BF16), o(256, BF16), o(256, BF16), o(256, F32), o(256, BF16), o(128, F32),
            o(128, BF16), o(1024, BF16), o(1024, BF16), o(1024, BF16), o(2048, F32)]
    return _rows_call("fwd_in", body, t_rows, tm, ins, outs)


def _attn_tile(t_rows):
    return min(512, t_rows)


MLA_HEADS_PER_STEP = 2


def _causal_pairs(nq, by_kv):
    if by_kv:
        pairs = [(i, j) for j in range(nq) for i in range(j, nq)]
    else:
        pairs = [(i, j) for i in range(nq) for j in range(i + 1)]
    return (jnp.asarray([p[0] for p in pairs], jnp.int32), jnp.asarray([p[1] for p in pairs], jnp.int32))


def _mla_fwd(q, k, v):
    t_rows = q.shape[0]
    t = _attn_tile(t_rows)
    hp = MLA_HEADS_PER_STEP
    w = hp * LANES
    ii, jj = _causal_pairs(t_rows // t, by_kv=False)

    def body(i_ref, j_ref, q_ref, k_ref, v_ref, o_ref, lse_ref, m_s, l_s, acc_s):
        i = i_ref[pl.program_id(1)]
        j = j_ref[pl.program_id(1)]

        @pl.when(j == 0)
        def _():
            m_s[...] = jnp.full(m_s.shape, NEG, F32)
            l_s[...] = jnp.zeros(l_s.shape, F32)
            acc_s[...] = jnp.zeros(acc_s.shape, F32)

        def step(diagonal):
            for hh in range(hp):
                sl = slice(hh * LANES, (hh + 1) * LANES)
                s = _dot_nt(k_ref[:, sl], q_ref[:, sl])
                if diagonal:
                    valid = (lax.broadcasted_iota(jnp.int32, (t, t), 0) <= lax.broadcasted_iota(jnp.int32, (t, t), 1))
                    s = jnp.where(valid, s, NEG)
                m_prev = m_s[hh]
                m_new = jnp.maximum(m_prev, jnp.max(s, axis=0, keepdims=True))
                p = jnp.exp(s - m_new)
                alpha = jnp.exp(m_prev - m_new)
                l_new = alpha * l_s[hh] + jnp.sum(p, axis=0, keepdims=True)
                acc = alpha * acc_s[hh] + _dot_tn(v_ref[:, sl], p.astype(BF16))
                if diagonal:
                    o_ref[:, sl] = (acc / l_new).T.astype(o_ref.dtype)
                    lse_ref[hh] = m_new + jnp.log(l_new)
                else:
                    m_s[hh] = m_new
                    l_s[hh] = l_new
                    acc_s[hh] = acc

        pl.when(j < i)(lambda: step(False))
        pl.when(j == i)(lambda: step(True))

    grid_spec = pltpu.PrefetchScalarGridSpec(
        num_scalar_prefetch=2, grid=(HEADS // hp, ii.shape[0]),
        in_specs=[pl.BlockSpec((t, w), lambda hb, s, ir, jr: (ir[s], hb)),
                  pl.BlockSpec((t, w), lambda hb, s, ir, jr: (jr[s], hb)),
                  pl.BlockSpec((t, w), lambda hb, s, ir, jr: (jr[s], hb))],
        out_specs=[pl.BlockSpec((t, w), lambda hb, s, ir, jr: (ir[s], hb)),
                   pl.BlockSpec((hp, 1, t), lambda hb, s, ir, jr: (hb, 0, ir[s]))],
        scratch_shapes=[pltpu.VMEM((hp, 1, t), F32), pltpu.VMEM((hp, 1, t), F32), pltpu.VMEM((hp, LANES, t), F32)])
    return pl.pallas_call(
        body, name="mla_fwd", grid_spec=grid_spec,
        out_shape=[_sds((t_rows, HEADS * LANES), BF16), _sds((HEADS, 1, t_rows), F32)],
        compiler_params=pltpu.CompilerParams(dimension_semantics=("arbitrary",) * 2, vmem_limit_bytes=VMEM_LIMIT),
    )(ii, jj, q, k, v)


def _mla_delta(o, do):
    t_rows = o.shape[0]
    t = _attn_tile(t_rows)

    def body(o_ref, do_ref, dl_ref):
        prod = o_ref[...].astype(F32) * do_ref[...].astype(F32)
        dl_ref[0] = jnp.sum(prod.T, axis=0, keepdims=True)

    return pl.pallas_call(
        body, name="mla_delta", grid=(HEADS, t_rows // t),
        in_specs=[pl.BlockSpec((t, LANES), lambda h, i: (i, h)), pl.BlockSpec((t, LANES), lambda h, i: (i, h))],
        out_specs=pl.BlockSpec((1, 1, t), lambda h, i: (h, 0, i)),
        out_shape=_sds((HEADS, 1, t_rows), F32),
    )(o, do)


def _mla_bwd(q, k, v, do, lse, delta):
    t_rows = q.shape[0]
    t = _attn_tile(t_rows)
    hp = MLA_HEADS_PER_STEP
    w = hp * LANES
    ii, jj = _causal_pairs(t_rows // t, by_kv=True)

    def body(i_ref, j_ref, q_ref, k_ref, v_ref, do_ref, lse_ref, dl_ref, dq_ref, dk_ref, dv_ref):
        i = i_ref[pl.program_id(1)]
        j = j_ref[pl.program_id(1)]

        @pl.when(pl.program_id(1) == 0)
        def _():
            dq_ref[...] = jnp.zeros(dq_ref.shape, F32)

        def step(diagonal):
            r0 = pl.multiple_of(i * t, t)
            for hh in range(hp):
                sl = slice(hh * LANES, (hh + 1) * LANES)
                qv = q_ref[:, sl]
                kv = k_ref[:, sl]
                dov = do_ref[:, sl]
                s = _dot_nt(kv, qv)
                if diagonal:
                    valid = (lax.broadcasted_iota(jnp.int32, (t, t), 0) <= lax.broadcasted_iota(jnp.int32, (t, t), 1))
                    s = jnp.where(valid, s, NEG)
                p = jnp.exp(s - lse_ref[hh])
                dv = _dot(p.astype(BF16), dov)
                dp = _dot_nt(v_ref[:, sl], dov)
                ds = (p * (dp - dl_ref[hh])).astype(BF16)
                dk = _dot(ds, qv)
                if diagonal:
                    dv_ref[:, sl] = dv
                    dk_ref[:, sl] = dk
                else:
                    dv_ref[:, sl] += dv
                    dk_ref[:, sl] += dk
                dq_ref[hh, pl.ds(r0, t), :] += _dot_tn(ds, kv)

        pl.when(i > j)(lambda: step(False))
        pl.when(i == j)(lambda: step(True))

    def qmap(hb, s, ir, jr):
        return (ir[s], hb)

    def kvmap(hb, s, ir, jr):
        return (jr[s], hb)

    def rowmap(hb, s, ir, jr):
        return (hb, 0, ir[s])

    grid_spec = pltpu.PrefetchScalarGridSpec(
        num_scalar_prefetch=2, grid=(HEADS // hp, ii.shape[0]),
        in_specs=[pl.BlockSpec((t, w), qmap), pl.BlockSpec((t, w), kvmap), pl.BlockSpec((t, w), kvmap),
                  pl.BlockSpec((t, w), qmap), pl.BlockSpec((hp, 1, t), rowmap), pl.BlockSpec((hp, 1, t), rowmap)],
        out_specs=[pl.BlockSpec((hp, t_rows, LANES), lambda hb, s, ir, jr: (hb, 0, 0)),
                   pl.BlockSpec((t, w), kvmap), pl.BlockSpec((t, w), kvmap)])
    return pl.pallas_call(
        body, name="mla_bwd", grid_spec=grid_spec,
        out_shape=[_sds((HEADS, t_rows, LANES), F32), _sds((t_rows, HEADS * LANES), F32),
                   _sds((t_rows, HEADS * LANES), F32)],
        compiler_params=pltpu.CompilerParams(dimension_semantics=("arbitrary",) * 2, vmem_limit_bytes=VMEM_LIMIT),
    )(ii, jj, q, k, v, do, lse, delta)


SWA_TILE = 2 * SWA_WINDOW
SWA_GROUP = HEADS // A_KV_HEADS


def _swa_bias(tq):
    koff = lax.broadcasted_iota(jnp.int32, (tq + SWA_WINDOW, SWA_GROUP * tq), 0) - SWA_WINDOW
    qoff = (lax.broadcasted_iota(jnp.int32, (tq + SWA_WINDOW, SWA_GROUP * tq), 1) % tq)
    band = (koff <= qoff) & (qoff - koff < SWA_WINDOW)
    return jnp.stack([jnp.where(band & (koff >= 0), 0.0, NEG), jnp.where(band, 0.0, NEG)]).astype(F32)


def _swa_specs(tq, nq):
    wb = tq // SWA_WINDOW

    def qi(i):
        return jnp.minimum(i, nq - 1)

    q = pl.BlockSpec((tq, SWA_GROUP * LANES), lambda h, i: (qi(i), h))
    cur = pl.BlockSpec((tq, LANES), lambda h, i: (qi(i), h))
    prev = pl.BlockSpec((SWA_WINDOW, LANES), lambda h, i: (jnp.maximum(qi(i) * wb - 1, 0), h))
    bias = pl.BlockSpec((1, tq + SWA_WINDOW, SWA_GROUP * tq), lambda h, i: (jnp.minimum(i, 1), 0, 0))
    rows = pl.BlockSpec((1, 1, 1, SWA_GROUP * tq), lambda h, i: (h, qi(i), 0, 0))
    sink = pl.BlockSpec((1, 1, SWA_GROUP * tq), lambda h, i: (h, 0, 0))
    return q, cur, prev, bias, rows, sink


def _stack_heads(ref):
    return jnp.concatenate([ref[:, g * LANES:(g + 1) * LANES] for g in range(SWA_GROUP)], axis=0)


def _swa_fwd(q, k, v, bias, sink_rows):
    t_rows = q.shape[0]
    tq = min(SWA_TILE, t_rows)
    nq = t_rows // tq
    qs_, cur, prev, bs, rows, sk = _swa_specs(tq, nq)

    def body(q_ref, kc_ref, kp_ref, vc_ref, vp_ref, b_ref, sink_ref, o_ref, lse_ref):
        qs = _stack_heads(q_ref)
        kk = jnp.concatenate([kp_ref[...], kc_ref[...]], axis=0)
        vv = jnp.concatenate([vp_ref[...], vc_ref[...]], axis=0)
        s = _dot_nt(kk, qs) + b_ref[0]
        sink = sink_ref[0]
        m = jnp.maximum(jnp.max(s, axis=0, keepdims=True), sink)
        p = jnp.exp(s - m)
        l = jnp.sum(p, axis=0, keepdims=True) + jnp.exp(sink - m)
        o = (_dot_tn(vv, p.astype(BF16)) / l).T
        for g in range(SWA_GROUP):
            o_ref[:, g * LANES:(g + 1) * LANES] = o[g * tq:(g + 1) * tq].astype(o_ref.dtype)
        lse_ref[0, 0] = m + jnp.log(l)

    return pl.pallas_call(
        body, name="swa_fwd", grid=(A_KV_HEADS, nq),
        in_specs=[qs_, cur, prev, cur, prev, bs, sk],
        out_specs=[qs_, rows],
        out_shape=[_sds((t_rows, HEADS * LANES), BF16), _sds((A_KV_HEADS, nq, 1, SWA_GROUP * tq), F32)],
        compiler_params=pltpu.CompilerParams(dimension_semantics=("arbitrary",) * 2, vmem_limit_bytes=VMEM_LIMIT),
    )(q, k, k, v, v, bias, sink_rows)


def _swa_bwd(q, k, v, o, do, lse, bias, sink_rows):
    t_rows = q.shape[0]
    tq = min(SWA_TILE, t_rows)
    nq = t_rows // tq
    qs_, cur, prev, bs, rows, sk = _swa_specs(tq, nq)
    hw = SWA_WINDOW

    def body(q_ref, kc_ref, kp_ref, vc_ref, vp_ref, o_ref, do_ref, lse_ref, b_ref, sink_ref,
             dq_ref, dk_ref, dv_ref, dsink_ref, ck, cv, dsa):
        i = pl.program_id(1)

        @pl.when(i == 0)
        def _():
            dsa[...] = jnp.zeros(dsa.shape, F32)

        @pl.when(i < nq)
        def _():
            qs = _stack_heads(q_ref)
            dos = _stack_heads(do_ref)
            kk = jnp.concatenate([kp_ref[...], kc_ref[...]], axis=0)
            vv = jnp.concatenate([vp_ref[...], vc_ref[...]], axis=0)
            lse = lse_ref[0, 0]
            p = jnp.exp(_dot_nt(kk, qs) + b_ref[0] - lse)
            delta = jnp.sum((_stack_heads(o_ref).astype(F32) * dos.astype(F32)).T, axis=0, keepdims=True)
            dsa[...] += -jnp.exp(sink_ref[0] - lse) * delta
            dv = _dot(p.astype(BF16), dos)
            ds = (p * (_dot_nt(vv, dos) - delta)).astype(BF16)
            dk = _dot(ds, qs)
            dq = _dot_tn(ds, kk)
            for g in range(SWA_GROUP):
                dq_ref[:, g * LANES:(g + 1) * LANES] = dq[g * tq:(g + 1) * tq]

            @pl.when(i > 0)
            def _():
                dk_ref[0:tq - hw, :] = ck[0:tq - hw, :]
                dk_ref[tq - hw:tq, :] = ck[tq - hw:tq, :] + dk[0:hw]
                dv_ref[0:tq - hw, :] = cv[0:tq - hw, :]
                dv_ref[tq - hw:tq, :] = cv[tq - hw:tq, :] + dv[0:hw]

            ck[...] = dk[hw:hw + tq]
            cv[...] = dv[hw:hw + tq]

        @pl.when(i == nq)
        def _():
            dk_ref[...] = ck[...]
            dv_ref[...] = cv[...]
            dsink_ref[...] = jnp.zeros(dsink_ref.shape, F32)
            for g in range(SWA_GROUP):
                tot = jnp.sum(dsa[:, g * tq:(g + 1) * tq], axis=1, keepdims=True)
                dsink_ref[0, g:g + 1, :] = jnp.zeros((1, LANES), F32) + tot

    kv_out = pl.BlockSpec((tq, LANES), lambda h, i: (jnp.maximum(i - 1, 0), h))
    return pl.pallas_call(
        body, name="swa_bwd", grid=(A_KV_HEADS, nq + 1),
        in_specs=[qs_, cur, prev, cur, prev, qs_, qs_, rows, bs, sk],
        out_specs=[qs_, kv_out, kv_out, pl.BlockSpec((1, 8, LANES), lambda h, i: (h, 0, 0))],
        out_shape=[_sds((t_rows, HEADS * LANES), F32), _sds((t_rows, A_KV_HEADS * LANES), F32),
                   _sds((t_rows, A_KV_HEADS * LANES), F32), _sds((A_KV_HEADS, 8, LANES), F32)],
        scratch_shapes=[pltpu.VMEM((tq, LANES), F32), pltpu.VMEM((tq, LANES), F32),
                        pltpu.VMEM((1, SWA_GROUP * tq), F32)],
        compiler_params=pltpu.CompilerParams(dimension_semantics=("arbitrary",) * 2, vmem_limit_bytes=VMEM_LIMIT),
    )(q, k, k, v, v, o, do, lse, bias, sink_rows)


def _fwd_mix(x, ya, yb, gate, wba, wbb, wout, g2, g3, tm):
    t_rows = x.shape[0]

    def body(x_ref, ya_ref, yb_ref, gate_ref, wba_ref, wbb_ref, wout_ref, g2_ref, g3_ref,
             pa_ref, pb_ref, mixed_ref, o_ref, x1_ref, h2_ref):
        pa = _dot(ya_ref[...], wba_ref[...])
        pb = _dot(yb_ref[...], wbb_ref[...])
        pa_ref[...] = pa
        pb_ref[...] = pb
        mixed = (gate_ref[:, 0:D_MODEL] * pa + gate_ref[:, D_MODEL:2 * D_MODEL] * pb).astype(BF16)
        mixed_ref[...] = mixed
        o = _dot(mixed, wout_ref[...])
        o_ref[...] = o
        on, _ = _rms_stats(o)
        x1 = x_ref[...] + on * g2_ref[...]
        x1_ref[...] = x1
        x1n, _ = _rms_stats(x1)
        h2_ref[...] = (x1n * g3_ref[...]).astype(BF16)

    def o_(dt):
        return (_sds((t_rows, D_MODEL), dt), _row(tm, D_MODEL))

    ins = [(x, _row(tm, D_MODEL)), (ya, _row(tm, 1024)), (yb, _row(tm, 1024)), (gate, _row(tm, 2048)),
           (wba, _full(wba.shape)), (wbb, _full(wbb.shape)), (wout, _full(wout.shape)),
           (g2, _full(g2.shape)), (g3, _full(g3.shape))]
    return _rows_call("fwd_mix", body, t_rows, tm, ins, [o_(F32), o_(F32), o_(BF16), o_(F32), o_(F32), o_(BF16)])


CONV_CHUNK = 1408


def _fwd_up(h2, wup, convw8, convb, tm):
    t_rows = h2.shape[0]
    cdim = 2 * D_FF

    def body(h2_ref, wup_ref, cw_ref, cb_ref, up_ref, a_ref, carry):
        i = pl.program_id(0)

        @pl.when(i == 0)
        def _():
            carry[...] = jnp.zeros(carry.shape, F32)

        hb = h2_ref[...]

        def conv(c0):
            sl = slice(c0, c0 + CONV_CHUNK)
            up = _dot(hb, wup_ref[c0 // CONV_CHUNK])
            up_ref[:, sl] = up
            xm1, xm2 = _conv_taps(up, carry[6:7, sl], carry[7:8, sl])
            u = cw_ref[0:1, sl] * xm2 + cw_ref[1:2, sl] * xm1 + cw_ref[2:3, sl] * up + cb_ref[:, sl]
            carry[:, sl] = up[tm - 8:tm, :]
            return u

        for c0 in range(0, D_FF, CONV_CHUNK):
            ug = conv(c0)
            uv = conv(D_FF + c0)
            gel, _ = _gelu_and_grad(ug)
            a_ref[:, c0:c0 + CONV_CHUNK] = (gel * uv).astype(BF16)

    ins = [(h2, _row(tm, D_MODEL)), (wup, _full(wup.shape)), (convw8, _full(convw8.shape)), (convb, _full(convb.shape))]
    outs = [(_sds((t_rows, cdim), F32), _row(tm, cdim)), (_sds((t_rows, D_FF), BF16), _row(tm, D_FF))]
    return _rows_call("fwd_up", body, t_rows, tm, ins, outs, scratch=[pltpu.VMEM((8, cdim), F32)])


def _fwd_out(a, wdown, x1, g4, p, wple, g5, wpg, tgt, tm):
    t_rows = a.shape[0]

    def body(a_ref, wdown_ref, x1_ref, g4_ref, p_ref, wple_ref, g5_ref, wpg_ref, tgt_ref,
             ff_ref, x2_ref, e_ref, n5_ref, sg_ref, dx3_ref, loss_ref):
        i = pl.program_id(0)
        ff = _dot(a_ref[...], wdown_ref[...])
        ff_ref[...] = ff
        ffn, _ = _rms_stats(ff)
        x2 = x1_ref[...] + ffn * g4_ref[...]
        x2_ref[...] = x2
        e = _dot(p_ref[...].astype(BF16), wple_ref[...])
        e_ref[...] = e
        x2n, _ = _rms_stats(x2)
        n5 = (x2n * g5_ref[...]).astype(BF16)
        n5_ref[...] = n5
        sg = _sigmoid(_dot(n5, wpg_ref[...]))
        sg_ref[...] = sg
        d = x2 + sg * e - tgt_ref[...]
        dx3_ref[...] = d * (1.0 / D_MODEL)

        @pl.when(i == 0)
        def _():
            loss_ref[...] = jnp.zeros((1, 1), F32)

        loss_ref[...] += 0.5 * jnp.sum(jnp.sum(d * d, axis=1, keepdims=True), axis=0, keepdims=True) * (1.0 / D_MODEL)

    def o_(dt):
        return (_sds((t_rows, D_MODEL), dt), _row(tm, D_MODEL))

    ins = [(a, _row(tm, D_FF)), (wdown, _full(wdown.shape)), (x1, _row(tm, D_MODEL)), (g4, _full(g4.shape)),
           (p, _row(tm, PLE_DIM)), (wple, _full(wple.shape)), (g5, _full(g5.shape)), (wpg, _full(wpg.shape)),
           (tgt, _row(tm, D_MODEL))]
    outs = [o_(F32), o_(F32), o_(F32), o_(BF16), o_(F32), o_(F32), (_sds((1, 1), F32), _full((1, 1)))]
    return _rows_call("fwd_out", body, t_rows, tm, ins, outs)


def _bwd_out(dx3, e, sg, x2, ff, g5, g4, wpg, wdown, up, convw8, convb, tm):
    t_rows = dx3.shape[0]
    cdim = 2 * D_FF
    hb = tm // 8

    def body(dx3_ref, e_ref, sg_ref, x2_ref, ff_ref, g5_ref, g4_ref, wpg_ref, wdown_ref, up_ref, halo_ref, cw_ref,
             cb_ref, dpre_ref, de_ref, dx2_ref, dff_ref, du_ref, dg5_ref, dg4_ref, dcb_ref, dcw_ref):
        i = pl.program_id(0)

        @pl.when(i == 0)
        def _():
            dg5_ref[...] = jnp.zeros(dg5_ref.shape, F32)
            dg4_ref[...] = jnp.zeros(dg4_ref.shape, F32)
            dcb_ref[...] = jnp.zeros(dcb_ref.shape, F32)
            dcw_ref[...] = jnp.zeros(dcw_ref.shape, F32)

        dx3 = dx3_ref[...]
        sg = sg_ref[...]
        dpre = (dx3 * e_ref[...] * sg * (1.0 - sg)).astype(BF16)
        dpre_ref[...] = dpre
        de_ref[...] = (dx3 * sg).astype(BF16)
        dn5 = _dot_nt(dpre, wpg_ref[...])
        x2n, r5 = _rms_stats(x2_ref[...])
        d2, dg5 = _rms_bwd(dn5, x2n, r5, g5_ref[...])
        dx2 = dx3 + d2
        dx2_ref[...] = dx2
        dg5_ref[...] += dg5
        ffn, r4 = _rms_stats(ff_ref[...])
        dff, dg4 = _rms_bwd(dx2, ffn, r4, g4_ref[...])
        dg4_ref[...] += dg4
        dffb = dff.astype(BF16)
        dff_ref[...] = dffb
        keep = jnp.where(i > 0, 1.0, 0.0)

        def conv(c0):
            sl = slice(c0, c0 + CONV_CHUNK)
            up = up_ref[:, sl]
            xm1, xm2 = _conv_taps(up, halo_ref[6:7, sl] * keep, halo_ref[7:8, sl] * keep)
            u = cw_ref[0:1, sl] * xm2 + cw_ref[1:2, sl] * xm1 + cw_ref[2:3, sl] * up + cb_ref[:, sl]
            return u, up, xm1, xm2

        def grads(c0, du, up, xm1, xm2):
            sl = slice(c0, c0 + CONV_CHUNK)
            du_ref[:, sl] = du
            dcb_ref[:, sl] += jnp.sum(du, axis=0, keepdims=True)
            dcw_ref[0:1, sl] += jnp.sum(du * xm2, axis=0, keepdims=True)
            dcw_ref[1:2, sl] += jnp.sum(du * xm1, axis=0, keepdims=True)
            dcw_ref[2:3, sl] += jnp.sum(du * up, axis=0, keepdims=True)

        for c0 in range(0, D_FF, CONV_CHUNK):
            da = _dot_nt(dffb, wdown_ref[c0:c0 + CONV_CHUNK, :])
            ug, *rg = conv(c0)
            uv, *rv = conv(D_FF + c0)
            gel, dgel = _gelu_and_grad(ug)
            grads(c0, da * uv * dgel, *rg)
            grads(D_FF + c0, da * gel, *rv)

    def o_(n, dt):
        return (_sds((t_rows, n), dt), _row(tm, n))

    def acc(r, n):
        return (_sds((r, n), F32), _full((r, n)))

    halo = pl.BlockSpec((8, cdim), lambda i: (jnp.maximum(i * hb - 1, 0), 0))
    ins = [(dx3, _row(tm, D_MODEL)), (e, _row(tm, D_MODEL)), (sg, _row(tm, D_MODEL)), (x2, _row(tm, D_MODEL)),
           (ff, _row(tm, D_MODEL)), (g5, _full(g5.shape)), (g4, _full(g4.shape)), (wpg, _full(wpg.shape)),
           (wdown, _full(wdown.shape)), (up, _row(tm, cdim)), (up, halo), (convw8, _full(convw8.shape)),
           (convb, _full(convb.shape))]
    outs = [o_(D_MODEL, BF16), o_(D_MODEL, BF16), o_(D_MODEL, F32), o_(D_MODEL, BF16), o_(cdim, F32),
            acc(1, D_MODEL), acc(1, D_MODEL), acc(1, cdim), acc(8, cdim)]
    return _rows_call("bwd_out", body, t_rows, tm, ins, outs)


def _bwd_mid(du, convw8, wup, dx2, x1, g3, o, g2, wout, gate, pa, pb, wba, wbb, tm):
    t_rows = du.shape[0]
    cdim = 2 * D_FF
    hb = tm // 8
    last_blk = t_rows // 8 - 1
    n_tiles = t_rows // tm

    def body(du_ref, halo_ref, cw_ref, wup_ref, dx2_ref, x1_ref, g3_ref, o_ref, g2_ref, wout_ref, gate_ref, pa_ref,
             pb_ref, wba_ref, wbb_ref,
             dup_ref, dx1_ref, do_ref, dpa_ref, dpb_ref, dgt_ref, dya_ref, dyb_ref, dg3_ref, dg2_ref, dbg_ref):
        i = pl.program_id(0)

        @pl.when(i == 0)
        def _():
            dg3_ref[...] = jnp.zeros(dg3_ref.shape, F32)
            dg2_ref[...] = jnp.zeros(dg2_ref.shape, F32)
            dbg_ref[...] = jnp.zeros(dbg_ref.shape, F32)

        keep = jnp.where(i < n_tiles - 1, 1.0, 0.0)
        dh2 = jnp.zeros((tm, D_MODEL), F32)
        for c0 in range(0, cdim, CONV_CHUNK):
            sl = slice(c0, c0 + CONV_CHUNK)
            du = du_ref[:, sl]
            xp1, xp2 = _conv_taps_next(du, halo_ref[0:1, sl] * keep, halo_ref[1:2, sl] * keep)
            dup = (cw_ref[2:3, sl] * du + cw_ref[1:2, sl] * xp1 + cw_ref[0:1, sl] * xp2).astype(BF16)
            dup_ref[:, sl] = dup
            dh2 = dh2 + _dot_nt(dup, wup_ref[c0 // CONV_CHUNK])
        x1n, r3 = _rms_stats(x1_ref[...])
        d1, dg3 = _rms_bwd(dh2, x1n, r3, g3_ref[...])
        dx1 = dx2_ref[...] + d1
        dx1_ref[...] = dx1
        dg3_ref[...] += dg3
        on, r2 = _rms_stats(o_ref[...])
        do, dg2 = _rms_bwd(dx1, on, r2, g2_ref[...])
        dg2_ref[...] += dg2
        dob = do.astype(BF16)
        do_ref[...] = dob
        dmixed = _dot_nt(dob, wout_ref[...])
        ga = gate_ref[:, 0:D_MODEL]
        gb = gate_ref[:, D_MODEL:2 * D_MODEL]
        dpa = (dmixed * ga).astype(BF16)
        dpb = (dmixed * gb).astype(BF16)
        dpa_ref[...] = dpa
        dpb_ref[...] = dpb
        dga = dmixed * pa_ref[...] * ga * (1.0 - ga)
        dgb = dmixed * pb_ref[...] * gb * (1.0 - gb)
        dgt_ref[:, 0:D_MODEL] = dga.astype(BF16)
        dgt_ref[:, D_MODEL:2 * D_MODEL] = dgb.astype(BF16)
        dbg_ref[:, 0:D_MODEL] += jnp.sum(dga, axis=0, keepdims=True)
        dbg_ref[:, D_MODEL:2 * D_MODEL] += jnp.sum(dgb, axis=0, keepdims=True)
        dya_ref[...] = _dot_nt(dpa, wba_ref[...]).astype(BF16)
        dyb_ref[...] = _dot_nt(dpb, wbb_ref[...]).astype(BF16)

    def o_(n, dt):
        return (_sds((t_rows, n), dt), _row(tm, n))

    def acc(r, n):
        return (_sds((r, n), F32), _full((r, n)))

    halo = pl.BlockSpec((8, cdim), lambda i: (jnp.minimum((i + 1) * hb, last_blk), 0))
    ins = [(du, _row(tm, cdim)), (du, halo), (convw8, _full(convw8.shape)), (wup, _full(wup.shape)),
           (dx2, _row(tm, D_MODEL)), (x1, _row(tm, D_MODEL)), (g3, _full(g3.shape)), (o, _row(tm, D_MODEL)),
           (g2, _full(g2.shape)), (wout, _full(wout.shape)), (gate, _row(tm, 2048)), (pa, _row(tm, D_MODEL)),
           (pb, _row(tm, D_MODEL)), (wba, _full(wba.shape)), (wbb, _full(wbb.shape))]
    outs = [o_(cdim, BF16), o_(D_MODEL, F32), o_(D_MODEL, BF16), o_(D_MODEL, BF16), o_(D_MODEL, BF16),
            o_(2048, BF16), o_(1024, BF16), o_(1024, BF16), acc(1, D_MODEL), acc(1, D_MODEL), acc(1, 2048)]
    return _rows_call("bwd_mid", body, t_rows, tm, ins, outs)


def _bwd_in(dqs, dks, dvs, dqm, dkm, dvm, tabs, consts, cq, ckv, gq, gkv, wuq, wk, wv, dgates, win, x, g1, dx1, tm):
    t_rows = x.shape[0]

    def body(dqs_ref, dks_ref, dvs_ref, dqm_ref, dkm_ref, dvm_ref, ca, sa1, sa2, cb, sb1, sb2, c_ref, cq_ref,
             ckv_ref, gq_ref, gkv_ref, wuq_ref, wk_ref, wv_ref, dgt_ref, win_ref, x_ref, g1_ref, dx1_ref,
             dz_ref, dqb_ref, dx_ref, dgq_ref, dgkv_ref, dg1_ref):
        i = pl.program_id(0)

        @pl.when(i == 0)
        def _():
            dgq_ref[...] = jnp.zeros(dgq_ref.shape, F32)
            dgkv_ref[...] = jnp.zeros(dgkv_ref.shape, F32)
            dg1_ref[...] = jnp.zeros(dg1_ref.shape, F32)

        ta = (ca[...], sa1[...], sa2[...])
        tb = (cb[...], sb1[...], sb2[...])
        dz_ref[:, Z_QA:Z_KA] = _rope_t(dqs_ref[...] * SCALE_A, *ta, A_HEAD_DIM // 2).astype(BF16)
        dz_ref[:, Z_KA:Z_VA] = _rope_t(dks_ref[...], *ta, A_HEAD_DIM // 2).astype(BF16)
        dz_ref[:, Z_VA:Z_CQ] = dvs_ref[...].astype(BF16)
        dqm = jnp.concatenate([dqm_ref[h] for h in range(HEADS)], axis=1)
        dqb = _rope_t(dqm * SCALE_B, *tb, ROPE_DIM // 2).astype(BF16)
        dqb_ref[...] = dqb
        dcqn = _dot_nt(dqb, wuq_ref[...])
        cqn, rq = _rms_stats(cq_ref[...])
        dcq, dgq = _rms_bwd(dcqn, cqn, rq, gq_ref[...])
        dgq_ref[...] += dgq
        dz_ref[:, Z_CQ:Z_CKV] = dcq.astype(BF16)
        dkm = dkm_ref[...]
        dslot = dkm[:, 0:LANES]
        for h in range(1, HEADS):
            dslot = dslot + dkm[:, h * LANES:(h + 1) * LANES]
        dz_ref[:, Z_KR:Z_GATE] = _rope_t(dslot * c_ref[10:11, :], *tb, ROPE_DIM // 2).astype(BF16)
        dckvn = _dot_nt(dkm.astype(BF16), wk_ref[...]) + _dot_nt(dvm_ref[...].astype(BF16), wv_ref[...])
        ckvn, rkv = _rms_stats(ckv_ref[...])
        dckv, dgkv = _rms_bwd(dckvn, ckvn, rkv, gkv_ref[...])
        dgkv_ref[...] += dgkv
        dz_ref[:, Z_CKV:Z_KR] = dckv.astype(BF16)
        dz_ref[:, Z_GATE:ZW] = dgt_ref[...]
        dh1 = _dot_nt(dz_ref[...], win_ref[...])
        xn, r1 = _rms_stats(x_ref[...])
        d0, dg1 = _rms_bwd(dh1, xn, r1, g1_ref[...])
        dg1_ref[...] += dg1
        dx_ref[...] = dx1_ref[...] + d0

    def acc(n):
        return (_sds((1, n), F32), _full((1, n)))

    ins = [(dqs, _row(tm, 1024)), (dks, _row(tm, 256)), (dvs, _row(tm, 256)), (dqm, _heads(tm, HEADS)),
           (dkm, _row(tm, 1024)), (dvm, _row(tm, 1024))] + [(t, _row(tm, LANES)) for t in tabs] + [
           (consts, _full(consts.shape)), (cq, _row(tm, 256)), (ckv, _row(tm, 128)), (gq, _full(gq.shape)),
           (gkv, _full(gkv.shape)), (wuq, _full(wuq.shape)), (wk, _full(wk.shape)), (wv, _full(wv.shape)),
           (dgates, _row(tm, 2048)), (win, _full(win.shape)), (x, _row(tm, D_MODEL)), (g1, _full(g1.shape)),
           (dx1, _row(tm, D_MODEL))]
    outs = [(_sds((t_rows, ZW), BF16), _row(tm, ZW)), (_sds((t_rows, 1024), BF16), _row(tm, 1024)),
            (_sds((t_rows, D_MODEL), F32), _row(tm, D_MODEL)), acc(256), acc(128), acc(D_MODEL)]
    return _rows_call("bwd_in", body, t_rows, tm, ins, outs)


def _pick_cols(n):
    best = LANES
    for d in range(LANES, min(n, 1408) + 1, LANES):
        if n % d == 0:
            best = d
    return best


def _mm_tn(name, a, b, column_shards=1):
    t_rows, m = a.shape
    n = b.shape[1]
    bk = min(512, t_rows)
    bm, bn = _pick_cols(m), _pick_cols(n // column_shards)
    per_shard = n // column_shards // bn

    def body(a_ref, b_ref, o_ref):
        @pl.when(pl.program_id(2) == 0)
        def _():
            o_ref[...] = jnp.zeros((bm, bn), F32)

        o_ref[...] += _dot_tn(a_ref[...].astype(BF16), b_ref[...].astype(BF16))

    return pl.pallas_call(
        body, name=name, grid=(m // bm, n // bn, t_rows // bk),
        in_specs=[pl.BlockSpec((bk, bm), lambda i, j, k: (k, i)), pl.BlockSpec((bk, bn), lambda i, j, k: (k, j))],
        out_specs=(pl.BlockSpec((bm, bn), lambda i, j, k: (i, j)) if column_shards == 1 else
                   pl.BlockSpec((None, bm, bn), lambda i, j, k: (j // per_shard, i, j % per_shard))),
        out_shape=_sds((m, n) if column_shards == 1 else (column_shards, m, n // column_shards), F32),
        compiler_params=pltpu.CompilerParams(dimension_semantics=("arbitrary",) * 3, vmem_limit_bytes=VMEM_LIMIT),
    )(a, b)


PACK_ROWS = 512


ADD_TILE_ELEMS = 1 << 17


def _add_rows(rows, cols):
    best = 16
    for d in range(16, rows + 1, 16):
        if rows % d == 0 and d * cols <= ADD_TILE_ELEMS:
            best = d
    assert rows % best == 0
    return best


def _add_pair(name, g, recv, half):
    _, _, rows, cols = g.shape
    t = _add_rows(rows, cols)

    def body(h_ref, g_ref, r_ref, o_ref):
        o_ref[...] = (g_ref[:, 0] + r_ref[...]).astype(BF16)

    spec = pl.BlockSpec((4, t, cols), lambda i, h: (0, i, 0))
    grid_spec = pltpu.PrefetchScalarGridSpec(
        num_scalar_prefetch=1, grid=(rows // t,),
        in_specs=[pl.BlockSpec((4, 1, t, cols), lambda i, h: (0, h[0], i, 0)), spec], out_specs=spec)
    return pl.pallas_call(body, name=name, grid_spec=grid_spec,
                          out_shape=_sds(recv.shape, BF16))(jnp.reshape(half, (1,)).astype(jnp.int32), g, recv)


def _add_chips(name, parts):
    _, rows, cols = parts.shape
    t = _add_rows(rows, cols)

    def body(p_ref, o_ref):
        acc = p_ref[0].astype(F32)
        for j in range(1, 4):
            acc = acc + p_ref[j].astype(F32)
        o_ref[...] = acc

    return pl.pallas_call(body, name=name, grid=(rows // t,),
                          in_specs=[pl.BlockSpec((4, t, cols), lambda i: (0, i, 0))],
                          out_specs=pl.BlockSpec((t, cols), lambda i: (i, 0)),
                          out_shape=_sds((rows, cols), F32))(parts)


def _add_devices(parts):
    n, rows, _ = parts.shape

    def body(p_ref, o_ref):
        acc = p_ref[0]
        for j in range(1, n):
            acc = acc + p_ref[j]
        o_ref[...] = acc

    return pl.pallas_call(body, name="small_add", grid=(1,),
                          in_specs=[pl.BlockSpec((n, rows, LANES), lambda i: (0, 0, 0))],
                          out_specs=pl.BlockSpec((rows, LANES), lambda i: (0, 0)),
                          out_shape=_sds((rows, LANES), F32))(parts)


def _adam_rows(k, n):
    target = max(8, (1 << 20) // (4 * n))
    if k <= target:
        return k
    best = None
    for d in range(8, target + 1, 8):
        if k % d == 0:
            best = d
    return best if best is not None else k


def _adamw(name, w, g, m, v):
    k, n = w.shape
    bk = _adam_rows(k, n)
    c1 = 1.0 - ADAM_B1 ** ADAM_STEP
    c2 = 1.0 - ADAM_B2 ** ADAM_STEP

    def body(w_ref, g_ref, m_ref, v_ref, d_ref, mo_ref, vo_ref):
        g_ = g_ref[...]
        m_ = ADAM_B1 * m_ref[...] + (1.0 - ADAM_B1) * g_
        v_ = ADAM_B2 * v_ref[...] + (1.0 - ADAM_B2) * (g_ * g_)
        mo_ref[...] = m_
        vo_ref[...] = v_
        d_ref[...] = -ADAM_LR * ((m_ / c1) / (jnp.sqrt(v_ / c2) + ADAM_EPS) + ADAM_WD * w_ref[...])

    spec = pl.BlockSpec((bk, n), lambda i: (i, 0))
    return pl.pallas_call(body, name=name, grid=(k // bk,), in_specs=[spec] * 4, out_specs=[spec] * 3,
                          out_shape=[_sds((k, n), F32)] * 3,
                          compiler_params=pltpu.CompilerParams(vmem_limit_bytes=VMEM_LIMIT))(w, g, m, v)


_HBM = pl.BlockSpec(memory_space=pltpu.HBM)


def _me():
    return lax.axis_index("x"), lax.axis_index("y"), lax.axis_index("c")


def _other_chips(x, y):
    return [(1 - x, y), (x, 1 - y), (1 - x, 1 - y)]


def _gather_weights(shards):
    n = len(shards)

    def body(*refs):
        x_refs, out_refs = refs[:n], refs[n:2 * n]
        send_sems, recv_sems = refs[2 * n:]
        x, y, c = _me()
        sibling = (x, y, 1 - c)
        chips = _other_chips(x, y)

        def copy(k, src, dst, to):
            return pltpu.make_async_remote_copy(src_ref=src, dst_ref=dst, send_sem=send_sems.at[k],
                                                recv_sem=recv_sems.at[k], device_id=to, device_id_type=MESH)

        first, passed = [], []
        for a, (x_ref, out_ref) in enumerate(zip(x_refs, out_refs)):
            for j, (cx, cy) in enumerate(chips):
                first.append(copy(6 * a + j, x_ref.at[c], out_ref.at[2 * x + y, c], (cx, cy, c)))
        for cp in first:
            cp.start()
        for a, (x_ref, out_ref) in enumerate(zip(x_refs, out_refs)):
            for j, (cx, cy) in enumerate(chips):
                landed = out_ref.at[2 * cx + cy, c]
                copy(6 * a + j, x_ref.at[c], landed, (cx, cy, c)).wait_recv()
                passed.append(copy(6 * a + 3 + j, landed, landed, sibling))
                passed[-1].start()
        for a, (x_ref, out_ref) in enumerate(zip(x_refs, out_refs)):
            for j, (cx, cy) in enumerate(chips):
                theirs = out_ref.at[2 * cx + cy, 1 - c]
                copy(6 * a + 3 + j, theirs, theirs, sibling).wait_recv()
        for cp in first + passed:
            cp.wait_send()

    return pl.pallas_call(
        body, name="gather_weights", out_shape=[_sds((4,) + s.shape, s.dtype) for s in shards],
        in_specs=[_HBM] * n, out_specs=[_HBM] * n,
        scratch_shapes=[pltpu.SemaphoreType.DMA((6 * n,)), pltpu.SemaphoreType.DMA((6 * n,))],
    )(*shards)


def _swap_sibling(name, vs, other_half=False):
    n = len(vs)

    def body(*refs):
        v_refs, out_refs = refs[:n], refs[n:2 * n]
        send_sems, recv_sems = refs[2 * n:]
        x, y, c = _me()
        cps = [pltpu.make_async_remote_copy(src_ref=v_ref.at[:, 1 - c] if other_half else v_ref, dst_ref=out_ref,
                                            send_sem=send_sems.at[a], recv_sem=recv_sems.at[a],
                                            device_id=(x, y, 1 - c), device_id_type=MESH)
               for a, (v_ref, out_ref) in enumerate(zip(v_refs, out_refs))]
        for cp in cps:
            cp.start()
        for cp in cps:
            cp.wait()

    def landing(v):
        return _sds((v.shape[0],) + v.shape[2:] if other_half else v.shape, v.dtype)

    return pl.pallas_call(
        body, name=name, out_shape=[landing(v) for v in vs], in_specs=[_HBM] * n, out_specs=[_HBM] * n,
        scratch_shapes=[pltpu.SemaphoreType.DMA((n,)), pltpu.SemaphoreType.DMA((n,))],
    )(*vs)


def _scatter_chips(ss):
    n = len(ss)

    def body(*refs):
        s_refs, out_refs = refs[:n], refs[n:2 * n]
        send_sems, recv_sems = refs[2 * n:]
        pairs = []
        for a, (s_ref, out_ref) in enumerate(zip(s_refs, out_refs)):
            pairs += _chip_copies(s_ref, out_ref, send_sems, recv_sems, True, 3 * a)
        for send, _ in pairs:
            send.start()
        for _, recv in pairs:
            recv.wait_recv()
        for send, _ in pairs:
            send.wait_send()

    return pl.pallas_call(
        body, name="scatter_chips", out_shape=[_sds(s.shape, s.dtype) for s in ss],
        in_specs=[_HBM] * n, out_specs=[_HBM] * n,
        scratch_shapes=[pltpu.SemaphoreType.DMA((3 * n,)), pltpu.SemaphoreType.DMA((3 * n,))],
    )(*ss)


_SEM = pl.BlockSpec(memory_space=pltpu.SEMAPHORE)
_EFFECT = pltpu.SideEffectType.DATAFLOW_SIDE_EFFECTING


def _chip_copies(v_ref, land_ref, send_sems, recv_sems, per_chip_piece, sem0=0):
    x, y, c = _me()
    k = 2 * x + y
    out = []
    for j, (cx, cy) in enumerate(_other_chips(x, y)):
        src = v_ref.at[2 * cx + cy] if per_chip_piece else v_ref
        sems = dict(send_sem=send_sems.at[sem0 + j], recv_sem=recv_sems.at[sem0 + j], device_id=(cx, cy, c),
                    device_id_type=MESH)
        send = pltpu.make_async_remote_copy(src_ref=src, dst_ref=land_ref.at[k], **sems)
        recv = pltpu.make_async_remote_copy(src_ref=src, dst_ref=land_ref.at[2 * cx + cy], **sems)
        out.append((send, recv))
    return out


def _chips_start(name, vs, per_chip_piece, after=None):
    n = len(vs)
    lands = [v.shape if per_chip_piece else (4,) + v.shape for v in vs]

    def body(*refs):
        v_refs, land_refs = refs[:n], refs[n:2 * n]
        send_sems, recv_sems = refs[-2 * n - 3], refs[-2 * n - 2]
        token = refs[-1]
        for a in range(n):
            for send, _ in _chip_copies(v_refs[a], land_refs[a], send_sems, recv_sems, per_chip_piece, 3 * a):
                send.start()
        token[...] = jnp.zeros_like(token)

    extra = () if after is None else (after,)
    hbm = [pltpu.with_memory_space_constraint(v, pltpu.HBM) for v in vs]
    zones = [pltpu.with_memory_space_constraint(lax.empty(s, v.dtype), pltpu.HBM) for s, v in zip(lands, vs)]
    out = pl.pallas_call(
        body, name=name,
        out_shape=(pltpu.SemaphoreType.DMA((3 * n,)), pltpu.SemaphoreType.DMA((3 * n,)),
                   *[pltpu.HBM(v.shape, v.dtype) for v in vs], *[pltpu.HBM(s, v.dtype) for s, v in zip(lands, vs)],
                   _sds((8, LANES), F32)),
        in_specs=(_HBM,) * (2 * n) + (pl.BlockSpec(memory_space=pl.ANY),) * len(extra),
        out_specs=(_SEM, _SEM) + (_HBM,) * (2 * n) + (pl.BlockSpec(memory_space=pltpu.VMEM),),
        input_output_aliases={i: 2 + i for i in range(2 * n)},
        compiler_params=pltpu.CompilerParams(has_side_effects=_EFFECT),
    )(*hbm, *zones, *extra)
    return out[0], out[1], list(out[2:2 + n]), list(out[2 + n:2 + 2 * n]), out[-1]


def _chips_wait(name, send_sems, recv_sems, v_thru, land_thru, per_chip_piece, after):
    n = len(v_thru)

    def body(*refs):
        v_refs, land_refs = refs[:n], refs[n:2 * n]
        send_sems, recv_sems = refs[2 * n], refs[2 * n + 1]
        for a in range(n):
            for send, recv in _chip_copies(v_refs[a], land_refs[a], send_sems, recv_sems, per_chip_piece, 3 * a):
                send.wait_send()
                recv.wait_recv()

    out = pl.pallas_call(
        body, name=name,
        out_shape=tuple(pltpu.HBM(a.shape, a.dtype) for a in list(v_thru) + list(land_thru)),
        in_specs=(_HBM,) * (2 * n) + (_SEM, _SEM, pl.BlockSpec(memory_space=pl.ANY)), out_specs=(_HBM,) * (2 * n),
        input_output_aliases={i: i for i in range(2 * n)},
        compiler_params=pltpu.CompilerParams(has_side_effects=_EFFECT),
    )(*v_thru, *land_thru, send_sems, recv_sems, after)
    return list(out[n:])


def _gather_small(name, v):
    def body(v_ref, out_ref, send_sems, recv_sems, local_sem):
        x, y, c = _me()
        me = 4 * x + 2 * y + c
        mine = pltpu.make_async_copy(v_ref, out_ref.at[me], local_sem)
        mine.start()
        peers = []
        for f in range(1, 8):
            fx, fy, fc = (f >> 2) & 1, (f >> 1) & 1, f & 1
            px = 1 - x if fx else x
            py = 1 - y if fy else y
            pc = 1 - c if fc else c
            peers.append((f - 1, (px, py, pc)))
        sends = [pltpu.make_async_remote_copy(src_ref=v_ref, dst_ref=out_ref.at[me], send_sem=send_sems.at[k],
                                              recv_sem=recv_sems.at[k], device_id=peer, device_id_type=MESH)
                 for k, peer in peers]
        for cp in sends:
            cp.start()
        for k, (px, py, pc) in peers:
            pltpu.make_async_remote_copy(src_ref=v_ref, dst_ref=out_ref.at[4 * px + 2 * py + pc],
                                         send_sem=send_sems.at[k], recv_sem=recv_sems.at[k],
                                         device_id=(px, py, pc), device_id_type=MESH).wait_recv()
        for cp in sends:
            cp.wait_send()
        mine.wait()

    return pl.pallas_call(
        body, name=name, out_shape=_sds((8,) + v.shape, v.dtype), in_specs=[_HBM], out_specs=_HBM,
        scratch_shapes=[pltpu.SemaphoreType.DMA((7,)), pltpu.SemaphoreType.DMA((7,)), pltpu.SemaphoreType.DMA],
    )(v)


_BIG = (("w_in", (1024, 3232), 1), ("w_uq", (256, 768), 1), ("w_ukv", (128, 1024), 1), ("w_branch_a", (512, 1024), 1),
        ("w_branch_b", (512, 1024), 1), ("w_out", (1024, 1024), 0), ("w_up", (1024, 5632), 1),
        ("w_down", (2816, 1024), 0), ("w_ple_gate", (1024, 1024), 0), ("w_ple", (256, 1024), 1))


def _shard_shape(shape, axis):
    return (shape[0] // 4, shape[1]) if axis == 0 else (shape[0], shape[1] // 4)


def _half_rows(shape, axis):
    k, n = _shard_shape(shape, axis)
    return k * n // (2 * LANES)


_EARLY = ("w_in", "w_uq", "w_ukv")
_LATE = ("w_branch_a", "w_branch_b", "w_out", "w_up", "w_down", "w_ple_gate", "w_ple")
_NATURAL = ("w_in", "w_up", "w_down", "w_out", "w_ple_gate")
_EARLY_PACKED = tuple(b for b in _BIG if b[0] in _EARLY and b[0] not in _NATURAL)
_LATE_PACKED = tuple(b for b in _BIG if b[0] in _LATE and b[0] not in _NATURAL)
_SHARD = {name: _shard_shape(shape, axis) for name, shape, axis in _BIG}


def _halves(a):
    return a.reshape(a.shape[:-2] + (2, a.shape[-2] // 2, a.shape[-1]))


def _rows_joined(a):
    return a.reshape(a.shape[:-3] + (a.shape[-3] * a.shape[-2], a.shape[-1]))


def _pack_pad(group):
    return -sum(_half_rows(shape, axis) for _, shape, axis in group) % PACK_ROWS


def _pack_shards(shards, dtype, group):
    parts = [shards[name].astype(dtype).reshape(2, _half_rows(shape, axis), LANES) for name, shape, axis in group]
    return jnp.concatenate(parts + [jnp.zeros((2, _pack_pad(group), LANES), dtype)], axis=1)


def _unpack_gathered(g, group):
    out, off = {}, 0
    for name, shape, axis in group:
        r = _half_rows(shape, axis)
        k, n = _shard_shape(shape, axis)
        w = g[:, :, off:off + r, :].reshape(4, k, n)
        out[name] = w.reshape(shape) if axis == 0 else w.transpose(1, 0, 2).reshape(shape)
        off += r
    return out


def _pack_grads(grads, group):
    parts = []
    for name, shape, axis in group:
        k, n = _shard_shape(shape, axis)
        g = grads[name]
        g4 = g.reshape(4, k, n) if axis == 0 else g.reshape(k, 4, n).transpose(1, 0, 2)
        parts.append(g4.reshape(4, 2, _half_rows(shape, axis), LANES))
    return jnp.concatenate(parts + [jnp.zeros((4, 2, _pack_pad(group), LANES), F32)], axis=2)


def _unpack_shard_grads(f, group):
    out, off = {}, 0
    for name, shape, axis in group:
        r = _half_rows(shape, axis)
        out[name] = f[:, off:off + r, :].reshape(_shard_shape(shape, axis))
        off += r
    return out


def _pad_slots(w, heads, dim, axis):
    if axis == 1:
        k = w.shape[0]
        return jnp.pad(w.reshape(k, heads, dim), ((0, 0), (0, 0), (0, LANES - dim))).reshape(k, heads * LANES)
    n = w.shape[1]
    return jnp.pad(w.reshape(heads, dim, n), ((0, 0), (0, LANES - dim), (0, 0))).reshape(heads * LANES, n)


def _unpad_slots(w, heads, dim, axis):
    if axis == 1:
        k = w.shape[0]
        return w.reshape(k, heads, LANES)[:, :, :dim].reshape(k, heads * dim)
    n = w.shape[1]
    return w.reshape(heads, LANES, n)[:, :dim, :].reshape(heads * dim, n)


def _pad_w_in(w):
    kr = jnp.pad(w[:, 1152:1184], ((0, 0), (NOPE_DIM, LANES - NOPE_DIM - ROPE_DIM)))
    return jnp.concatenate([_pad_slots(w[:, 0:512], HEADS, A_HEAD_DIM, 1),
                            _pad_slots(w[:, 512:640], A_KV_HEADS, A_HEAD_DIM, 1),
                            _pad_slots(w[:, 640:768], A_KV_HEADS, A_HEAD_DIM, 1),
                            w[:, 768:1024], w[:, 1024:1152], kr, w[:, 1184:3232]], axis=1)


def _unpad_w_in(w):
    return jnp.concatenate([_unpad_slots(w[:, Z_QA:Z_KA], HEADS, A_HEAD_DIM, 1),
                            _unpad_slots(w[:, Z_KA:Z_VA], A_KV_HEADS, A_HEAD_DIM, 1),
                            _unpad_slots(w[:, Z_VA:Z_CQ], A_KV_HEADS, A_HEAD_DIM, 1),
                            w[:, Z_CQ:Z_CKV], w[:, Z_CKV:Z_KR],
                            w[:, Z_KR + NOPE_DIM:Z_KR + NOPE_DIM + ROPE_DIM], w[:, Z_GATE:ZW]], axis=1)


_SMALL = (("attn_pre_norm", 1024), ("attn_post_norm", 1024), ("b_gate", 2048), ("sinks", 8), ("q_a_norm", 256),
          ("kv_a_norm", 128), ("mlp_pre_norm", 1024), ("mlp_post_norm", 1024), ("conv_b", 5632), ("ple_norm", 1024),
          ("conv_w", 3 * 5632), ("loss", 1))


def _small_rows(n):
    return 8 * -(-n // (8 * LANES))


def _pack_small(vals):
    parts = []
    for name, n in _SMALL:
        r = _small_rows(n)
        parts.append(jnp.pad(vals[name].reshape(-1), (0, r * LANES - n)).reshape(r, LANES))
    return jnp.concatenate(parts, axis=0)


def _unpack_small(buf):
    out, off = {}, 0
    for name, n in _SMALL:
        r = _small_rows(n)
        out[name] = buf[off:off + r].reshape(-1)[:n]
        off += r
    return out


def kernel(x, p, positions, attn_pre_norm, attn_post_norm, w_in, b_gate, sinks, q_a_norm, w_uq, kv_a_norm, w_ukv, w_branch_a, w_branch_b, w_out, mlp_pre_norm, mlp_post_norm, w_up, conv_w, conv_b, w_down, ple_norm, w_ple_gate, w_ple, loss_target, m_attn_pre_norm, m_attn_post_norm, m_w_in, m_b_gate, m_sinks, m_q_a_norm, m_w_uq, m_kv_a_norm, m_w_ukv, m_w_branch_a, m_w_branch_b, m_w_out, m_mlp_pre_norm, m_mlp_post_norm, m_w_up, m_conv_w, m_conv_b, m_w_down, m_ple_norm, m_w_ple_gate, m_w_ple, v_attn_pre_norm, v_attn_post_norm, v_w_in, v_b_gate, v_sinks, v_q_a_norm, v_w_uq, v_kv_a_norm, v_w_ukv, v_w_branch_a, v_w_branch_b, v_w_out, v_mlp_pre_norm, v_mlp_post_norm, v_w_up, v_conv_w, v_conv_b, v_w_down, v_ple_norm, v_w_ple_gate, v_w_ple):
    names = ["attn_pre_norm", "attn_post_norm", "w_in", "b_gate", "sinks", "q_a_norm", "w_uq", "kv_a_norm", "w_ukv",
             "w_branch_a", "w_branch_b", "w_out", "mlp_pre_norm", "mlp_post_norm", "w_up", "conv_w", "conv_b",
             "w_down", "ple_norm", "w_ple_gate", "w_ple"]
    wts = dict(zip(names, [attn_pre_norm, attn_post_norm, w_in, b_gate, sinks, q_a_norm, w_uq, kv_a_norm, w_ukv,
                           w_branch_a, w_branch_b, w_out, mlp_pre_norm, mlp_post_norm, w_up, conv_w, conv_b, w_down,
                           ple_norm, w_ple_gate, w_ple]))
    moms = dict(zip(names, [m_attn_pre_norm, m_attn_post_norm, m_w_in, m_b_gate, m_sinks, m_q_a_norm, m_w_uq,
                            m_kv_a_norm, m_w_ukv, m_w_branch_a, m_w_branch_b, m_w_out, m_mlp_pre_norm,
                            m_mlp_post_norm, m_w_up, m_conv_w, m_conv_b, m_w_down, m_ple_norm, m_w_ple_gate, m_w_ple]))
    vars_ = dict(zip(names, [v_attn_pre_norm, v_attn_post_norm, v_w_in, v_b_gate, v_sinks, v_q_a_norm, v_w_uq,
                             v_kv_a_norm, v_w_ukv, v_w_branch_a, v_w_branch_b, v_w_out, v_mlp_pre_norm,
                             v_mlp_post_norm, v_w_up, v_conv_w, v_conv_b, v_w_down, v_ple_norm, v_w_ple_gate, v_w_ple]))
    w2 = {n: a.reshape(a.shape[-2:]) for n, a in wts.items()}
    m2 = {n: a.reshape(a.shape[-2:]) for n, a in moms.items()}
    v2 = {n: a.reshape(a.shape[-2:]) for n, a in vars_.items()}

    t_rows = x.shape[-2]
    tm = min(256, t_rows)
    xc, yc, cc = lax.axis_index("x"), lax.axis_index("y"), lax.axis_index("c")
    chip = 2 * xc + yc

    x2d = x.reshape(t_rows, D_MODEL)
    p2d = p.reshape(t_rows, PLE_DIM)
    tgt = loss_target.reshape(t_rows, D_MODEL)
    pos_f = positions.reshape(t_rows, 1).astype(F32)

    def own_slot_filled(gathered, mine):
        return [lax.dynamic_update_slice(g, m[None], (chip, 0, 0, 0)) for g, m in zip(gathered, mine)]

    def shard_lists(group, packed_group):
        return ([_halves(w2[n].astype(BF16)) for n in group if n in _NATURAL]
                + [_pack_shards(w2, BF16, packed_group)])

    early_mine = shard_lists(_EARLY, _EARLY_PACKED)
    late_mine = shard_lists(_LATE, _LATE_PACKED)
    early = own_slot_filled(_gather_weights(early_mine), early_mine)
    late_sems = _chips_start("gather_late_start", late_mine, False, after=early[0])
    late_token = late_sems[4][0:1, 0:1]
    full = _unpack_gathered(early[-1], _EARLY_PACKED)
    full["w_in"] = _rows_joined(early[0]).transpose(1, 0, 2).reshape(D_MODEL, 3232)
    cw_rows = 3 * 1408 // LANES
    cw_all = _gather_small("gather_conv_w", jnp.pad(w2["conv_w"].reshape(cw_rows, LANES), ((0, 40 - cw_rows), (0, 0))))
    conv_full = cw_all[0::2, :cw_rows].reshape(4, 3, 1408).transpose(1, 0, 2).reshape(3, 2 * D_FF)
    convw8 = jnp.pad(conv_full, ((0, 5), (0, 0)))

    win = _pad_w_in(full["w_in"])
    wuq = _pad_slots(full["w_uq"], HEADS, NOPE_DIM + ROPE_DIM, 1)
    ukv = full["w_ukv"].reshape(KV_LORA, HEADS, NOPE_DIM + V_DIM)
    wk = _pad_slots(ukv[:, :, :NOPE_DIM].reshape(KV_LORA, HEADS * NOPE_DIM), HEADS, NOPE_DIM, 1)
    wv = _pad_slots(ukv[:, :, NOPE_DIM:].reshape(KV_LORA, HEADS * V_DIM), HEADS, V_DIM, 1)
    g1, g2, g3, g4, g5 = (w2["attn_pre_norm"], w2["attn_post_norm"], w2["mlp_pre_norm"], w2["mlp_post_norm"],
                          w2["ple_norm"])
    gq, gkv, bg, convb = w2["q_a_norm"], w2["kv_a_norm"], w2["b_gate"], w2["conv_b"]
    swa_tile = min(SWA_TILE, t_rows)
    sink_rows = jnp.repeat(w2["sinks"].reshape(A_KV_HEADS, SWA_GROUP, 1), swa_tile, axis=2).reshape(
        A_KV_HEADS, 1, SWA_GROUP * swa_tile)
    swa_bias = _swa_bias(swa_tile)

    consts = _rope_consts()
    tabs = _rope_tables(pos_f, consts, tm)
    h1, qs, ks, vs, cq, cqn, ckv, ckvn, qm, km, vm, gate = _fwd_in(x2d, g1, win, bg + late_token, gq, gkv, wuq, wk, wv,
                                                                   tabs, tm)
    ya, lse_a = _swa_fwd(qs, ks, vs, swa_bias, sink_rows)
    yb, lse_b = _mla_fwd(qm, km, vm)
    late = own_slot_filled(_chips_wait("gather_late_wait", *late_sems[:4], False, after=yb), late_mine)
    full = _unpack_gathered(late[-1], _LATE_PACKED)
    wba = _pad_slots(full["w_branch_a"], HEADS, A_HEAD_DIM, 0)
    wbb = _pad_slots(full["w_branch_b"], HEADS, V_DIM, 0)
    wple = full["w_ple"]
    natural = dict(zip([n for n in _LATE if n in _NATURAL], late))
    wup = _rows_joined(natural["w_up"])
    wout, wdown, wpg = (_rows_joined(natural[n]).reshape(-1, D_MODEL) for n in ("w_out", "w_down", "w_ple_gate"))
    pa, pb, mixed, o, x1, h2 = _fwd_mix(x2d, ya, yb, gate, wba, wbb, wout, g2, g3, tm)
    up, a = _fwd_up(h2, wup, convw8, convb, tm)
    ff, x2, e, n5, sg, dx3, loss_part = _fwd_out(a, wdown, x1, g4, p2d, wple, g5, wpg, tgt, tm)

    dpre, de, dx2, dff, du, dg5, dg4, dconvb, dconvw8 = _bwd_out(dx3, e, sg, x2, ff, g5, g4, wpg, wdown, up, convw8,
                                                                 convb, tm)
    dup, dx1, do, dpa, dpb, dgates, dya, dyb, dg3, dg2, dbg = _bwd_mid(du, convw8, wup, dx2, x1, g3, o, g2, wout, gate,
                                                                       pa, pb, wba, wbb, tm)
    late_grads = {
        "w_branch_a": _unpad_slots(_mm_tn("dw_branch_a", ya, dpa), HEADS, A_HEAD_DIM, 0),
        "w_branch_b": _unpad_slots(_mm_tn("dw_branch_b", yb, dpb), HEADS, V_DIM, 0),
        "w_out": _mm_tn("dw_out", mixed, do).reshape(4, D_MODEL // 4, D_MODEL),
        "w_up": _mm_tn("dw_up", h2, dup, column_shards=4),
        "w_down": _mm_tn("dw_down", a, dff).reshape(4, D_FF // 4, D_MODEL),
        "w_ple_gate": _mm_tn("dw_ple_gate", n5, dpre).reshape(4, D_MODEL // 4, D_MODEL),
        "w_ple": _mm_tn("dw_ple", p2d, de),
    }

    def pair_sums(tag, grads, group, packed_group):
        views = [_halves(grads[n]) for n in group if n in _NATURAL] + [_pack_grads(grads, packed_group)]
        theirs = _swap_sibling("swap_%s_grad_halves" % tag, views, other_half=True)
        return [_add_pair("rs_%s_add_pair_%d" % (tag, i), g, r, cc) for i, (g, r) in enumerate(zip(views, theirs))]

    late_pairs = pair_sums("late", late_grads, _LATE, _LATE_PACKED)
    rs_sems = _chips_start("scatter_late_start", late_pairs, True)
    rs_token = rs_sems[4][0:1, 0:1]

    dqs, dks, dvs, dsink_rows = _swa_bwd(qs, ks, vs, ya, dya, lse_a, swa_bias, sink_rows + rs_token)
    dsink = dsink_rows[:, 0:SWA_GROUP, 0]
    dqm, dkm, dvm = _mla_bwd(qm, km, vm, dyb, lse_b, _mla_delta(yb, dyb))
    dz, dqb, dx, dgq, dgkv, dg1 = _bwd_in(dqs, dks, dvs, dqm, dkm, dvm, tabs, consts, cq, ckv, gq, gkv, wuq, wk, wv,
                                           dgates, win, x2d, g1, dx1, tm)

    dwk = _unpad_slots(_mm_tn("dw_k", ckvn, dkm), HEADS, NOPE_DIM, 1).reshape(KV_LORA, HEADS, NOPE_DIM)
    dwv = _unpad_slots(_mm_tn("dw_v", ckvn, dvm), HEADS, V_DIM, 1).reshape(KV_LORA, HEADS, V_DIM)
    early_grads = {
        "w_in": _unpad_w_in(_mm_tn("dw_in", h1, dz)).reshape(D_MODEL, 4, 808).transpose(1, 0, 2),
        "w_uq": _unpad_slots(_mm_tn("dw_uq", cqn, dqb), HEADS, NOPE_DIM + ROPE_DIM, 1),
        "w_ukv": jnp.concatenate([dwk, dwv], axis=2).reshape(KV_LORA, HEADS * (NOPE_DIM + V_DIM)),
    }

    def finish(tag, pairs, landed, group, packed_group):
        reduced = []
        for i, (pair, land) in enumerate(zip(pairs, landed)):
            own = lax.dynamic_index_in_dim(pair, chip, 0, keepdims=True)
            reduced.append(_add_chips("rs_%s_add_chips_%d" % (tag, i),
                                      lax.dynamic_update_slice(land, own, (chip, 0, 0))))
        others = _swap_sibling("swap_%s_reduced_halves" % tag, reduced)
        both = [jnp.where(cc == 0, jnp.stack([r, o]), jnp.stack([o, r])) for r, o in zip(reduced, others)]
        out = _unpack_shard_grads(both[-1], packed_group)
        out.update({n: _rows_joined(b) for n, b in zip([n for n in group if n in _NATURAL], both)})
        return out

    early_pairs = pair_sums("early", early_grads, _EARLY, _EARLY_PACKED)
    shard_grads = finish("early", early_pairs, _scatter_chips(early_pairs), _EARLY, _EARLY_PACKED)
    late_landed = _chips_wait("scatter_late_wait", *rs_sems[:4], True, after=early_pairs[0])
    shard_grads.update(finish("late", late_pairs, late_landed, _LATE, _LATE_PACKED))

    small = {"attn_pre_norm": dg1, "attn_post_norm": dg2, "b_gate": dbg, "sinks": dsink, "q_a_norm": dgq,
             "kv_a_norm": dgkv, "mlp_pre_norm": dg3, "mlp_post_norm": dg4, "conv_b": dconvb, "ple_norm": dg5,
             "conv_w": dconvw8[0:3], "loss": loss_part}
    small_sum = _unpack_small(_add_devices(_gather_small("gather_small_grads", _pack_small(small))))
    for n in names:
        if n in small_sum and n != "conv_w":
            shard_grads[n] = small_sum[n].reshape(w2[n].shape)
    shard_grads["conv_w"] = lax.dynamic_index_in_dim(small_sum["conv_w"].reshape(3, 4, 1408), chip, 1, keepdims=False)

    loss = small_sum["loss"][0]

    g_out, d_out, m_out, v_out = [], [], [], []
    for n in names:
        g = shard_grads[n]
        d, mn, vn = _adamw("adamw_" + n, w2[n], g, m2[n], v2[n])
        shp = wts[n].shape
        g_out.append(g.reshape(shp))
        d_out.append(d.reshape(shp))
        m_out.append(mn.reshape(shp))
        v_out.append(vn.reshape(shp))
    return (loss, dx.reshape(x.shape), *g_out, *d_out, *m_out, *v_out)
```

```python
import functools
import math

import numpy as np
import jax
import jax.numpy as jnp
from jax import lax
from jax.experimental import pallas as pl
from jax.experimental.pallas import tpu as pltpu

F32 = jnp.float32
BF16 = jnp.bfloat16

D_MODEL = 1024
D_FF = 2816
PLE_DIM = 256
ROPE_THETA = 10000.0
RMS_EPS = 1e-6
SWA_WINDOW = 128
HEADS = 8
A_KV_HEADS = 2
A_HEAD_DIM = 64
Q_LORA = 256
KV_LORA = 128
NOPE_DIM = 64
ROPE_DIM = 32
V_DIM = 64
LANES = 128
ZW = 4096
NEG = -1e30
SCALE_A = A_HEAD_DIM ** -0.5
SCALE_B = (NOPE_DIM + ROPE_DIM) ** -0.5

ADAM_LR = 0.001
ADAM_B1 = 0.9
ADAM_B2 = 0.999
ADAM_EPS = 1e-08
ADAM_WD = 0.01
ADAM_STEP = 10

VMEM_LIMIT = 60 * 1024 * 1024
MESH_AXES = ("x", "y", "c")
MESH = pl.DeviceIdType.MESH

Z_QA, Z_KA, Z_VA, Z_CQ, Z_CKV, Z_KR, Z_GATE = 0, 1024, 1280, 1536, 1792, 1920, 2048


def _dot(a, b):
    return jnp.dot(a, b, preferred_element_type=F32)


def _dot_nt(a, b):
    return lax.dot_general(a, b, (((1,), (1,)), ((), ())), preferred_element_type=F32)


def _dot_tn(a, b):
    return lax.dot_general(a, b, (((0,), (0,)), ((), ())), preferred_element_type=F32)


def _rms_stats(x):
    r = lax.rsqrt(jnp.mean(x * x, axis=-1, keepdims=True) + RMS_EPS)
    return x * r, r


def _rms_bwd(dy, xn, r, g):
    dxn = dy * g
    dx = r * (dxn - xn * jnp.mean(dxn * xn, axis=-1, keepdims=True))
    dg = jnp.sum(dy * xn, axis=0, keepdims=True)
    return dx, dg


def _tile_lanes(t, n):
    return t if n == 1 else jnp.concatenate([t] * n, axis=1)


def _rope(x, c, s1, s2, half):
    w = x.shape[1]
    n = w // LANES
    return (x * _tile_lanes(c, n) + pltpu.roll(x, w - half, 1) * _tile_lanes(s1, n)
            + pltpu.roll(x, half, 1) * _tile_lanes(s2, n))


def _rope_t(dy, c, s1, s2, half):
    w = dy.shape[1]
    n = w // LANES
    return (dy * _tile_lanes(c, n) + pltpu.roll(dy * _tile_lanes(s1, n), half, 1)
            + pltpu.roll(dy * _tile_lanes(s2, n), w - half, 1))


def _sigmoid(x):
    return 1.0 / (1.0 + jnp.exp(-x))


_GELU_C = math.sqrt(2.0 / math.pi)


def _gelu_and_grad(x):
    x2 = x * x
    th = jnp.tanh(_GELU_C * (x + 0.044715 * x * x2))
    gel = 0.5 * x * (1.0 + th)
    dgel = 0.5 * (1.0 + th) + 0.5 * x * (1.0 - th * th) * (_GELU_C * (1.0 + 3.0 * 0.044715 * x2))
    return gel, dgel


def _conv_taps(up, h6, h7):
    rows = lax.broadcasted_iota(jnp.int32, up.shape, 0)
    r1 = pltpu.roll(up, 1, 0)
    r2 = pltpu.roll(up, 2, 0)
    xm1 = jnp.where(rows == 0, h7, r1)
    xm2 = jnp.where(rows == 0, h6, jnp.where(rows == 1, h7, r2))
    return xm1, xm2


def _conv_taps_next(du, n0, n1):
    tm = du.shape[0]
    rows = lax.broadcasted_iota(jnp.int32, du.shape, 0)
    r1 = pltpu.roll(du, tm - 1, 0)
    r2 = pltpu.roll(du, tm - 2, 0)
    xp1 = jnp.where(rows == tm - 1, n0, r1)
    xp2 = jnp.where(rows == tm - 2, n0, jnp.where(rows == tm - 1, n1, r2))
    return xp1, xp2


def _row(tm, n):
    return pl.BlockSpec((tm, n), lambda i: (i, 0))


def _full(shape):
    nd = len(shape)
    return pl.BlockSpec(tuple(shape), lambda i: (0,) * nd)


def _heads(tm, h):
    return pl.BlockSpec((h, tm, LANES), lambda i: (0, i, 0))


def _rows_call(name, body, t_rows, tm, ins, outs, scratch=()):
    return pl.pallas_call(
        body, name=name, grid=(t_rows // tm,),
        in_specs=[s for _, s in ins],
        out_specs=[s for _, s in outs],
        out_shape=[s for s, _ in outs],
        scratch_shapes=list(scratch),
        compiler_params=pltpu.CompilerParams(dimension_semantics=("arbitrary",), vmem_limit_bytes=VMEM_LIMIT),
    )(*[a for a, _ in ins])


def _sds(shape, dtype):
    return jax.ShapeDtypeStruct(tuple(shape), dtype)


def _rope_consts():
    c = np.zeros((16, LANES), np.float32)
    lane = np.arange(LANES)
    inv_a = (ROPE_THETA ** (-(np.arange(0, A_HEAD_DIM, 2, dtype=np.float32) / A_HEAD_DIM))).astype(np.float32)
    in_a = lane < A_HEAD_DIM
    c[0, in_a] = inv_a[lane[in_a] % (A_HEAD_DIM // 2)]
    c[1, in_a] = 1.0
    c[2, lane < A_HEAD_DIM // 2] = -1.0
    c[3, (lane >= A_HEAD_DIM // 2) & in_a] = 1.0
    inv_b = (ROPE_THETA ** (-(np.arange(0, ROPE_DIM, 2, dtype=np.float32) / ROPE_DIM))).astype(np.float32)
    pe = (lane >= NOPE_DIM) & (lane < NOPE_DIM + ROPE_DIM)
    c[5, pe] = inv_b[(lane[pe] - NOPE_DIM) % (ROPE_DIM // 2)]
    c[6, pe] = 1.0
    c[7, (lane >= NOPE_DIM) & (lane < NOPE_DIM + ROPE_DIM // 2)] = -1.0
    c[8, (lane >= NOPE_DIM + ROPE_DIM // 2) & (lane < NOPE_DIM + ROPE_DIM)] = 1.0
    c[9, lane < NOPE_DIM] = 1.0
    c[10, pe] = 1.0
    return jnp.asarray(c)


def _rope_tables(pos_f, consts, tm):
    t_rows = pos_f.shape[0]

    def body(pos_ref, c_ref, ca, sa1, sa2, cb, sb1, sb2):
        pos = pos_ref[...]
        ang = pos * c_ref[0:1, :]
        cs, sn = jnp.cos(ang), jnp.sin(ang)
        ca[...] = cs * c_ref[1:2, :]
        sa1[...] = sn * c_ref[2:3, :]
        sa2[...] = sn * c_ref[3:4, :]
        ang = pos * c_ref[5:6, :]
        cs, sn = jnp.cos(ang), jnp.sin(ang)
        cb[...] = cs * c_ref[6:7, :] + c_ref[9:10, :]
        sb1[...] = sn * c_ref[7:8, :]
        sb2[...] = sn * c_ref[8:9, :]

    tab = (_sds((t_rows, LANES), F32), _row(tm, LANES))
    return _rows_call("rope_tables", body, t_rows, tm,
                      [(pos_f, _row(tm, 1)), (consts, _full(consts.shape))], [tab] * 6)


def _fwd_in(x, g1, win, bg, gq, gkv, wuq, wk, wv, tabs, tm):
    t_rows = x.shape[0]

    def body(x_ref, g1_ref, win_ref, bg_ref, gq_ref, gkv_ref, wuq_ref, wk_ref, wv_ref,
             ca, sa1, sa2, cb, sb1, sb2,
             h1_ref, qs_ref, ks_ref, vs_ref, cq_ref, cqn_ref, ckv_ref, ckvn_ref, qm_ref, km_ref, vm_ref, gate_ref):
        xn, _ = _rms_stats(x_ref[...])
        hb = (xn * g1_ref[...]).astype(BF16)
        h1_ref[...] = hb
        ta = (ca[...], sa1[...], sa2[...])
        tb = (cb[...], sb1[...], sb2[...])
        qs_ref[...] = (_rope(_dot(hb, win_ref[:, Z_QA:Z_KA]), *ta, A_HEAD_DIM // 2) * SCALE_A).astype(BF16)
        ks_ref[...] = _rope(_dot(hb, win_ref[:, Z_KA:Z_VA]), *ta, A_HEAD_DIM // 2).astype(BF16)
        vs_ref[...] = _dot(hb, win_ref[:, Z_VA:Z_CQ]).astype(BF16)
        cq = _dot(hb, win_ref[:, Z_CQ:Z_CKV])
        cq_ref[...] = cq
        cqn, _ = _rms_stats(cq)
        cqb = (cqn * gq_ref[...]).astype(BF16)
        cqn_ref[...] = cqb
        qm_ref[...] = (_rope(_dot(cqb, wuq_ref[...]), *tb, ROPE_DIM // 2) * SCALE_B).astype(BF16)
        ckv = _dot(hb, win_ref[:, Z_CKV:Z_KR])
        ckv_ref[...] = ckv
        ckvn, _ = _rms_stats(ckv)
        ckvb = (ckvn * gkv_ref[...]).astype(BF16)
        ckvn_ref[...] = ckvb
        kpe = _rope(_dot(hb, win_ref[:, Z_KR:Z_GATE]), *tb, ROPE_DIM // 2)
        km_ref[...] = (_dot(ckvb, wk_ref[...]) + _tile_lanes(kpe, HEADS)).astype(BF16)
        vm_ref[...] = _dot(ckvb, wv_ref[...]).astype(BF16)
        gate_ref[...] = _sigmoid(_dot(hb, win_ref[:, Z_GATE:ZW]) + bg_ref[...])

    def o(n, dt):
        return (_sds((t_rows, n), dt), _row(tm, n))

    ins = [(x, _row(tm, D_MODEL)), (g1, _full(g1.shape)), (win, _full(win.shape)), (bg, _full(bg.shape)),
           (gq, _full(gq.shape)), (gkv, _full(gkv.shape)), (wuq, _full(wuq.shape)), (wk, _full(wk.shape)),
           (wv, _full(wv.shape))] + [(t, _row(tm, LANES)) for t in tabs]
    outs = [o(1024, BF16), o(1024, BF16), o(256, BF16), o(256, BF16), o(256, F32), o(256, BF16), o(128, F32),
            o(128, BF16), o(1024, BF16), o(1024, BF16), o(1024, BF16), o(2048, F32)]
    return _rows_call("fwd_in", body, t_rows, tm, ins, outs)


def _attn_tile(t_rows):
    return min(512, t_rows)


MLA_HEADS_PER_STEP = 2


def _causal_pairs(nq, by_kv):
    if by_kv:
        pairs = [(i, j) for j in range(nq) for i in range(j, nq)]
    else:
        pairs = [(i, j) for i in range(nq) for j in range(i + 1)]
    return (jnp.asarray([p[0] for p in pairs], jnp.int32), jnp.asarray([p[1] for p in pairs], jnp.int32))


def _mla_fwd(q, k, v):
    t_rows = q.shape[0]
    t = _attn_tile(t_rows)
    hp = MLA_HEADS_PER_STEP
    w = hp * LANES
    ii, jj = _causal_pairs(t_rows // t, by_kv=False)

    def body(i_ref, j_ref, q_ref, k_ref, v_ref, o_ref, lse_ref, m_s, l_s, acc_s):
        i = i_ref[pl.program_id(1)]
        j = j_ref[pl.program_id(1)]

        @pl.when(j == 0)
        def _():
            m_s[...] = jnp.full(m_s.shape, NEG, F32)
            l_s[...] = jnp.zeros(l_s.shape, F32)
            acc_s[...] = jnp.zeros(acc_s.shape, F32)

        def step(diagonal):
            for hh in range(hp):
                sl = slice(hh * LANES, (hh + 1) * LANES)
                s = _dot_nt(k_ref[:, sl], q_ref[:, sl])
                if diagonal:
                    valid = (lax.broadcasted_iota(jnp.int32, (t, t), 0) <= lax.broadcasted_iota(jnp.int32, (t, t), 1))
                    s = jnp.where(valid, s, NEG)
                m_prev = m_s[hh]
                m_new = jnp.maximum(m_prev, jnp.max(s, axis=0, keepdims=True))
                p = jnp.exp(s - m_new)
                alpha = jnp.exp(m_prev - m_new)
                l_new = alpha * l_s[hh] + jnp.sum(p, axis=0, keepdims=True)
                acc = alpha * acc_s[hh] + _dot_tn(v_ref[:, sl], p.astype(BF16))
                if diagonal:
                    o_ref[:, sl] = (acc / l_new).T.astype(o_ref.dtype)
                    lse_ref[hh] = m_new + jnp.log(l_new)
                else:
                    m_s[hh] = m_new
                    l_s[hh] = l_new
                    acc_s[hh] = acc

        pl.when(j < i)(lambda: step(False))
        pl.when(j == i)(lambda: step(True))

    grid_spec = pltpu.PrefetchScalarGridSpec(
        num_scalar_prefetch=2, grid=(HEADS // hp, ii.shape[0]),
        in_specs=[pl.BlockSpec((t, w), lambda hb, s, ir, jr: (ir[s], hb)),
                  pl.BlockSpec((t, w), lambda hb, s, ir, jr: (jr[s], hb)),
                  pl.BlockSpec((t, w), lambda hb, s, ir, jr: (jr[s], hb))],
        out_specs=[pl.BlockSpec((t, w), lambda hb, s, ir, jr: (ir[s], hb)),
                   pl.BlockSpec((hp, 1, t), lambda hb, s, ir, jr: (hb, 0, ir[s]))],
        scratch_shapes=[pltpu.VMEM((hp, 1, t), F32), pltpu.VMEM((hp, 1, t), F32), pltpu.VMEM((hp, LANES, t), F32)])
    return pl.pallas_call(
        body, name="mla_fwd", grid_spec=grid_spec,
        out_shape=[_sds((t_rows, HEADS * LANES), BF16), _sds((HEADS, 1, t_rows), F32)],
        compiler_params=pltpu.CompilerParams(dimension_semantics=("arbitrary",) * 2, vmem_limit_bytes=VMEM_LIMIT),
    )(ii, jj, q, k, v)


def _mla_delta(o, do):
    t_rows = o.shape[0]
    t = _attn_tile(t_rows)

    def body(o_ref, do_ref, dl_ref):
        prod = o_ref[...].astype(F32) * do_ref[...].astype(F32)
        dl_ref[0] = jnp.sum(prod.T, axis=0, keepdims=True)

    return pl.pallas_call(
        body, name="mla_delta", grid=(HEADS, t_rows // t),
        in_specs=[pl.BlockSpec((t, LANES), lambda h, i: (i, h)), pl.BlockSpec((t, LANES), lambda h, i: (i, h))],
        out_specs=pl.BlockSpec((1, 1, t), lambda h, i: (h, 0, i)),
        out_shape=_sds((HEADS, 1, t_rows), F32),
    )(o, do)


def _mla_bwd(q, k, v, do, lse, delta):
    t_rows = q.shape[0]
    t = _attn_tile(t_rows)
    hp = MLA_HEADS_PER_STEP
    w = hp * LANES
    ii, jj = _causal_pairs(t_rows // t, by_kv=True)

    def body(i_ref, j_ref, q_ref, k_ref, v_ref, do_ref, lse_ref, dl_ref, dq_ref, dk_ref, dv_ref):
        i = i_ref[pl.program_id(1)]
        j = j_ref[pl.program_id(1)]

        @pl.when(pl.program_id(1) == 0)
        def _():
            dq_ref[...] = jnp.zeros(dq_ref.shape, F32)

        def step(diagonal):
            r0 = pl.multiple_of(i * t, t)
            for hh in range(hp):
                sl = slice(hh * LANES, (hh + 1) * LANES)
                qv = q_ref[:, sl]
                kv = k_ref[:, sl]
                dov = do_ref[:, sl]
                s = _dot_nt(kv, qv)
                if diagonal:
                    valid = (lax.broadcasted_iota(jnp.int32, (t, t), 0) <= lax.broadcasted_iota(jnp.int32, (t, t), 1))
                    s = jnp.where(valid, s, NEG)
                p = jnp.exp(s - lse_ref[hh])
                dv = _dot(p.astype(BF16), dov)
                dp = _dot_nt(v_ref[:, sl], dov)
                ds = (p * (dp - dl_ref[hh])).astype(BF16)
                dk = _dot(ds, qv)
                if diagonal:
                    dv_ref[:, sl] = dv
                    dk_ref[:, sl] = dk
                else:
                    dv_ref[:, sl] += dv
                    dk_ref[:, sl] += dk
                dq_ref[hh, pl.ds(r0, t), :] += _dot_tn(ds, kv)

        pl.when(i > j)(lambda: step(False))
        pl.when(i == j)(lambda: step(True))

    def qmap(hb, s, ir, jr):
        return (ir[s], hb)

    def kvmap(hb, s, ir, jr):
        return (jr[s], hb)

    def rowmap(hb, s, ir, jr):
        return (hb, 0, ir[s])

    grid_spec = pltpu.PrefetchScalarGridSpec(
        num_scalar_prefetch=2, grid=(HEADS // hp, ii.shape[0]),
        in_specs=[pl.BlockSpec((t, w), qmap), pl.BlockSpec((t, w), kvmap), pl.BlockSpec((t, w), kvmap),
                  pl.BlockSpec((t, w), qmap), pl.BlockSpec((hp, 1, t), rowmap), pl.BlockSpec((hp, 1, t), rowmap)],
        out_specs=[pl.BlockSpec((hp, t_rows, LANES), lambda hb, s, ir, jr: (hb, 0, 0)),
                   pl.BlockSpec((t, w), kvmap), pl.BlockSpec((t, w), kvmap)])
    return pl.pallas_call(
        body, name="mla_bwd", grid_spec=grid_spec,
        out_shape=[_sds((HEADS, t_rows, LANES), F32), _sds((t_rows, HEADS * LANES), F32),
                   _sds((t_rows, HEADS * LANES), F32)],
        compiler_params=pltpu.CompilerParams(dimension_semantics=("arbitrary",) * 2, vmem_limit_bytes=VMEM_LIMIT),
    )(ii, jj, q, k, v, do, lse, delta)


SWA_TILE = 2 * SWA_WINDOW
SWA_GROUP = HEADS // A_KV_HEADS


def _swa_bias(tq):
    koff = lax.broadcasted_iota(jnp.int32, (tq + SWA_WINDOW, SWA_GROUP * tq), 0) - SWA_WINDOW
    qoff = (lax.broadcasted_iota(jnp.int32, (tq + SWA_WINDOW, SWA_GROUP * tq), 1) % tq)
    band = (koff <= qoff) & (qoff - koff < SWA_WINDOW)
    return jnp.stack([jnp.where(band & (koff >= 0), 0.0, NEG), jnp.where(band, 0.0, NEG)]).astype(F32)


def _swa_specs(tq, nq):
    wb = tq // SWA_WINDOW

    def qi(i):
        return jnp.minimum(i, nq - 1)

    q = pl.BlockSpec((tq, SWA_GROUP * LANES), lambda h, i: (qi(i), h))
    cur = pl.BlockSpec((tq, LANES), lambda h, i: (qi(i), h))
    prev = pl.BlockSpec((SWA_WINDOW, LANES), lambda h, i: (jnp.maximum(qi(i) * wb - 1, 0), h))
    bias = pl.BlockSpec((1, tq + SWA_WINDOW, SWA_GROUP * tq), lambda h, i: (jnp.minimum(i, 1), 0, 0))
    rows = pl.BlockSpec((1, 1, 1, SWA_GROUP * tq), lambda h, i: (h, qi(i), 0, 0))
    sink = pl.BlockSpec((1, 1, SWA_GROUP * tq), lambda h, i: (h, 0, 0))
    return q, cur, prev, bias, rows, sink


def _stack_heads(ref):
    return jnp.concatenate([ref[:, g * LANES:(g + 1) * LANES] for g in range(SWA_GROUP)], axis=0)


def _swa_fwd(q, k, v, bias, sink_rows):
    t_rows = q.shape[0]
    tq = min(SWA_TILE, t_rows)
    nq = t_rows // tq
    qs_, cur, prev, bs, rows, sk = _swa_specs(tq, nq)

    def body(q_ref, kc_ref, kp_ref, vc_ref, vp_ref, b_ref, sink_ref, o_ref, lse_ref):
        qs = _stack_heads(q_ref)
        kk = jnp.concatenate([kp_ref[...], kc_ref[...]], axis=0)
        vv = jnp.concatenate([vp_ref[...], vc_ref[...]], axis=0)
        s = _dot_nt(kk, qs) + b_ref[0]
        sink = sink_ref[0]
        m = jnp.maximum(jnp.max(s, axis=0, keepdims=True), sink)
        p = jnp.exp(s - m)
        l = jnp.sum(p, axis=0, keepdims=True) + jnp.exp(sink - m)
        o = (_dot_tn(vv, p.astype(BF16)) / l).T
        for g in range(SWA_GROUP):
            o_ref[:, g * LANES:(g + 1) * LANES] = o[g * tq:(g + 1) * tq].astype(o_ref.dtype)
        lse_ref[0, 0] = m + jnp.log(l)

    return pl.pallas_call(
        body, name="swa_fwd", grid=(A_KV_HEADS, nq),
        in_specs=[qs_, cur, prev, cur, prev, bs, sk],
        out_specs=[qs_, rows],
        out_shape=[_sds((t_rows, HEADS * LANES), BF16), _sds((A_KV_HEADS, nq, 1, SWA_GROUP * tq), F32)],
        compiler_params=pltpu.CompilerParams(dimension_semantics=("arbitrary",) * 2, vmem_limit_bytes=VMEM_LIMIT),
    )(q, k, k, v, v, bias, sink_rows)


def _swa_bwd(q, k, v, o, do, lse, bias, sink_rows):
    t_rows = q.shape[0]
    tq = min(SWA_TILE, t_rows)
    nq = t_rows // tq
    qs_, cur, prev, bs, rows, sk = _swa_specs(tq, nq)
    hw = SWA_WINDOW

    def body(q_ref, kc_ref, kp_ref, vc_ref, vp_ref, o_ref, do_ref, lse_ref, b_ref, sink_ref,
             dq_ref, dk_ref, dv_ref, dsink_ref, ck, cv, dsa):
        i = pl.program_id(1)

        @pl.when(i == 0)
        def _():
            dsa[...] = jnp.zeros(dsa.shape, F32)

        @pl.when(i < nq)
        def _():
            qs = _stack_heads(q_ref)
            dos = _stack_heads(do_ref)
            kk = jnp.concatenate([kp_ref[...], kc_ref[...]], axis=0)
            vv = jnp.concatenate([vp_ref[...], vc_ref[...]], axis=0)
            lse = lse_ref[0, 0]
            p = jnp.exp(_dot_nt(kk, qs) + b_ref[0] - lse)
            delta = jnp.sum((_stack_heads(o_ref).astype(F32) * dos.astype(F32)).T, axis=0, keepdims=True)
            dsa[...] += -jnp.exp(sink_ref[0] - lse) * delta
            dv = _dot(p.astype(BF16), dos)
            ds = (p * (_dot_nt(vv, dos) - delta)).astype(BF16)
            dk = _dot(ds, qs)
            dq = _dot_tn(ds, kk)
            for g in range(SWA_GROUP):
                dq_ref[:, g * LANES:(g + 1) * LANES] = dq[g * tq:(g + 1) * tq]

            @pl.when(i > 0)
            def _():
                dk_ref[0:tq - hw, :] = ck[0:tq - hw, :]
                dk_ref[tq - hw:tq, :] = ck[tq - hw:tq, :] + dk[0:hw]
                dv_ref[0:tq - hw, :] = cv[0:tq - hw, :]
                dv_ref[tq - hw:tq, :] = cv[tq - hw:tq, :] + dv[0:hw]

            ck[...] = dk[hw:hw + tq]
            cv[...] = dv[hw:hw + tq]

        @pl.when(i == nq)
        def _():
            dk_ref[...] = ck[...]
            dv_ref[...] = cv[...]
            dsink_ref[...] = jnp.zeros(dsink_ref.shape, F32)
            for g in range(SWA_GROUP):
                tot = jnp.sum(dsa[:, g * tq:(g + 1) * tq], axis=1, keepdims=True)
                dsink_ref[0, g:g + 1, :] = jnp.zeros((1, LANES), F32) + tot

    kv_out = pl.BlockSpec((tq, LANES), lambda h, i: (jnp.maximum(i - 1, 0), h))
    return pl.pallas_call(
        body, name="swa_bwd", grid=(A_KV_HEADS, nq + 1),
        in_specs=[qs_, cur, prev, cur, prev, qs_, qs_, rows, bs, sk],
        out_specs=[qs_, kv_out, kv_out, pl.BlockSpec((1, 8, LANES), lambda h, i: (h, 0, 0))],
        out_shape=[_sds((t_rows, HEADS * LANES), F32), _sds((t_rows, A_KV_HEADS * LANES), F32),
                   _sds((t_rows, A_KV_HEADS * LANES), F32), _sds((A_KV_HEADS, 8, LANES), F32)],
        scratch_shapes=[pltpu.VMEM((tq, LANES), F32), pltpu.VMEM((tq, LANES), F32),
                        pltpu.VMEM((1, SWA_GROUP * tq), F32)],
        compiler_params=pltpu.CompilerParams(dimension_semantics=("arbitrary",) * 2, vmem_limit_bytes=VMEM_LIMIT),
    )(q, k, k, v, v, o, do, lse, bias, sink_rows)


def _fwd_mix(x, ya, yb, gate, wba, wbb, wout, g2, g3, tm):
    t_rows = x.shape[0]

    def body(x_ref, ya_ref, yb_ref, gate_ref, wba_ref, wbb_ref, wout_ref, g2_ref, g3_ref,
             pa_ref, pb_ref, mixed_ref, o_ref, x1_ref, h2_ref):
        pa = _dot(ya_ref[...], wba_ref[...])
        pb = _dot(yb_ref[...], wbb_ref[...])
        pa_ref[...] = pa
        pb_ref[...] = pb
        mixed = (gate_ref[:, 0:D_MODEL] * pa + gate_ref[:, D_MODEL:2 * D_MODEL] * pb).astype(BF16)
        mixed_ref[...] = mixed
        o = _dot(mixed, wout_ref[...])
        o_ref[...] = o
        on, _ = _rms_stats(o)
        x1 = x_ref[...] + on * g2_ref[...]
        x1_ref[...] = x1
        x1n, _ = _rms_stats(x1)
        h2_ref[...] = (x1n * g3_ref[...]).astype(BF16)

    def o_(dt):
        return (_sds((t_rows, D_MODEL), dt), _row(tm, D_MODEL))

    ins = [(x, _row(tm, D_MODEL)), (ya, _row(tm, 1024)), (yb, _row(tm, 1024)), (gate, _row(tm, 2048)),
           (wba, _full(wba.shape)), (wbb, _full(wbb.shape)), (wout, _full(wout.shape)),
           (g2, _full(g2.shape)), (g3, _full(g3.shape))]
    return _rows_call("fwd_mix", body, t_rows, tm, ins, [o_(F32), o_(F32), o_(BF16), o_(F32), o_(F32), o_(BF16)])


CONV_CHUNK = 1408


def _fwd_up(h2, wup, convw8, convb, tm):
    t_rows = h2.shape[0]
    cdim = 2 * D_FF

    def body(h2_ref, wup_ref, cw_ref, cb_ref, up_ref, a_ref, carry):
        i = pl.program_id(0)

        @pl.when(i == 0)
        def _():
            carry[...] = jnp.zeros(carry.shape, F32)

        hb = h2_ref[...]

        def conv(c0):
            sl = slice(c0, c0 + CONV_CHUNK)
            up = _dot(hb, wup_ref[c0 // CONV_CHUNK])
            up_ref[:, sl] = up
            xm1, xm2 = _conv_taps(up, carry[6:7, sl], carry[7:8, sl])
            u = cw_ref[0:1, sl] * xm2 + cw_ref[1:2, sl] * xm1 + cw_ref[2:3, sl] * up + cb_ref[:, sl]
            carry[:, sl] = up[tm - 8:tm, :]
            return u

        for c0 in range(0, D_FF, CONV_CHUNK):
            ug = conv(c0)
            uv = conv(D_FF + c0)
            gel, _ = _gelu_and_grad(ug)
            a_ref[:, c0:c0 + CONV_CHUNK] = (gel * uv).astype(BF16)

    ins = [(h2, _row(tm, D_MODEL)), (wup, _full(wup.shape)), (convw8, _full(convw8.shape)), (convb, _full(convb.shape))]
    outs = [(_sds((t_rows, cdim), F32), _row(tm, cdim)), (_sds((t_rows, D_FF), BF16), _row(tm, D_FF))]
    return _rows_call("fwd_up", body, t_rows, tm, ins, outs, scratch=[pltpu.VMEM((8, cdim), F32)])


def _fwd_out(a, wdown, x1, g4, p, wple, g5, wpg, tgt, tm):
    t_rows = a.shape[0]

    def body(a_ref, wdown_ref, x1_ref, g4_ref, p_ref, wple_ref, g5_ref, wpg_ref, tgt_ref,
             ff_ref, x2_ref, e_ref, n5_ref, sg_ref, dx3_ref, loss_ref):
        i = pl.program_id(0)
        ff = _dot(a_ref[...], wdown_ref[...])
        ff_ref[...] = ff
        ffn, _ = _rms_stats(ff)
        x2 = x1_ref[...] + ffn * g4_ref[...]
        x2_ref[...] = x2
        e = _dot(p_ref[...].astype(BF16), wple_ref[...])
        e_ref[...] = e
        x2n, _ = _rms_stats(x2)
        n5 = (x2n * g5_ref[...]).astype(BF16)
        n5_ref[...] = n5
        sg = _sigmoid(_dot(n5, wpg_ref[...]))
        sg_ref[...] = sg
        d = x2 + sg * e - tgt_ref[...]
        dx3_ref[...] = d * (1.0 / D_MODEL)

        @pl.when(i == 0)
        def _():
            loss_ref[...] = jnp.zeros((1, 1), F32)

        loss_ref[...] += 0.5 * jnp.sum(jnp.sum(d * d, axis=1, keepdims=True), axis=0, keepdims=True) * (1.0 / D_MODEL)

    def o_(dt):
        return (_sds((t_rows, D_MODEL), dt), _row(tm, D_MODEL))

    ins = [(a, _row(tm, D_FF)), (wdown, _full(wdown.shape)), (x1, _row(tm, D_MODEL)), (g4, _full(g4.shape)),
           (p, _row(tm, PLE_DIM)), (wple, _full(wple.shape)), (g5, _full(g5.shape)), (wpg, _full(wpg.shape)),
           (tgt, _row(tm, D_MODEL))]
    outs = [o_(F32), o_(F32), o_(F32), o_(BF16), o_(F32), o_(F32), (_sds((1, 1), F32), _full((1, 1)))]
    return _rows_call("fwd_out", body, t_rows, tm, ins, outs)


def _bwd_out(dx3, e, sg, x2, ff, g5, g4, wpg, wdown, up, convw8, convb, tm):
    t_rows = dx3.shape[0]
    cdim = 2 * D_FF
    hb = tm // 8

    def body(dx3_ref, e_ref, sg_ref, x2_ref, ff_ref, g5_ref, g4_ref, wpg_ref, wdown_ref, up_ref, halo_ref, cw_ref,
             cb_ref, dpre_ref, de_ref, dx2_ref, dff_ref, du_ref, dg5_ref, dg4_ref, dcb_ref, dcw_ref):
        i = pl.program_id(0)

        @pl.when(i == 0)
        def _():
            dg5_ref[...] = jnp.zeros(dg5_ref.shape, F32)
            dg4_ref[...] = jnp.zeros(dg4_ref.shape, F32)
            dcb_ref[...] = jnp.zeros(dcb_ref.shape, F32)
            dcw_ref[...] = jnp.zeros(dcw_ref.shape, F32)

        dx3 = dx3_ref[...]
        sg = sg_ref[...]
        dpre = (dx3 * e_ref[...] * sg * (1.0 - sg)).astype(BF16)
        dpre_ref[...] = dpre
        de_ref[...] = (dx3 * sg).astype(BF16)
        dn5 = _dot_nt(dpre, wpg_ref[...])
        x2n, r5 = _rms_stats(x2_ref[...])
        d2, dg5 = _rms_bwd(dn5, x2n, r5, g5_ref[...])
        dx2 = dx3 + d2
        dx2_ref[...] = dx2
        dg5_ref[...] += dg5
        ffn, r4 = _rms_stats(ff_ref[...])
        dff, dg4 = _rms_bwd(dx2, ffn, r4, g4_ref[...])
        dg4_ref[...] += dg4
        dffb = dff.astype(BF16)
        dff_ref[...] = dffb
        keep = jnp.where(i > 0, 1.0, 0.0)

        def conv(c0):
            sl = slice(c0, c0 + CONV_CHUNK)
            up = up_ref[:, sl]
            xm1, xm2 = _conv_taps(up, halo_ref[6:7, sl] * keep, halo_ref[7:8, sl] * keep)
            u = cw_ref[0:1, sl] * xm2 + cw_ref[1:2, sl] * xm1 + cw_ref[2:3, sl] * up + cb_ref[:, sl]
            return u, up, xm1, xm2

        def grads(c0, du, up, xm1, xm2):
            sl = slice(c0, c0 + CONV_CHUNK)
            du_ref[:, sl] = du
            dcb_ref[:, sl] += jnp.sum(du, axis=0, keepdims=True)
            dcw_ref[0:1, sl] += jnp.sum(du * xm2, axis=0, keepdims=True)
            dcw_ref[1:2, sl] += jnp.sum(du * xm1, axis=0, keepdims=True)
            dcw_ref[2:3, sl] += jnp.sum(du * up, axis=0, keepdims=True)

        for c0 in range(0, D_FF, CONV_CHUNK):
            da = _dot_nt(dffb, wdown_ref[c0:c0 + CONV_CHUNK, :])
            ug, *rg = conv(c0)
            uv, *rv = conv(D_FF + c0)
            gel, dgel = _gelu_and_grad(ug)
            grads(c0, da * uv * dgel, *rg)
            grads(D_FF + c0, da * gel, *rv)

    def o_(n, dt):
        return (_sds((t_rows, n), dt), _row(tm, n))

    def acc(r, n):
        return (_sds((r, n), F32), _full((r, n)))

    halo = pl.BlockSpec((8, cdim), lambda i: (jnp.maximum(i * hb - 1, 0), 0))
    ins = [(dx3, _row(tm, D_MODEL)), (e, _row(tm, D_MODEL)), (sg, _row(tm, D_MODEL)), (x2, _row(tm, D_MODEL)),
           (ff, _row(tm, D_MODEL)), (g5, _full(g5.shape)), (g4, _full(g4.shape)), (wpg, _full(wpg.shape)),
           (wdown, _full(wdown.shape)), (up, _row(tm, cdim)), (up, halo), (convw8, _full(convw8.shape)),
           (convb, _full(convb.shape))]
    outs = [o_(D_MODEL, BF16), o_(D_MODEL, BF16), o_(D_MODEL, F32), o_(D_MODEL, BF16), o_(cdim, F32),
            acc(1, D_MODEL), acc(1, D_MODEL), acc(1, cdim), acc(8, cdim)]
    return _rows_call("bwd_out", body, t_rows, tm, ins, outs)


def _bwd_mid(du, convw8, wup, dx2, x1, g3, o, g2, wout, gate, pa, pb, wba, wbb, tm):
    t_rows = du.shape[0]
    cdim = 2 * D_FF
    hb = tm // 8
    last_blk = t_rows // 8 - 1
    n_tiles = t_rows // tm

    def body(du_ref, halo_ref, cw_ref, wup_ref, dx2_ref, x1_ref, g3_ref, o_ref, g2_ref, wout_ref, gate_ref, pa_ref,
             pb_ref, wba_ref, wbb_ref,
             dup_ref, dx1_ref, do_ref, dpa_ref, dpb_ref, dgt_ref, dya_ref, dyb_ref, dg3_ref, dg2_ref, dbg_ref):
        i = pl.program_id(0)

        @pl.when(i == 0)
        def _():
            dg3_ref[...] = jnp.zeros(dg3_ref.shape, F32)
            dg2_ref[...] = jnp.zeros(dg2_ref.shape, F32)
            dbg_ref[...] = jnp.zeros(dbg_ref.shape, F32)

        keep = jnp.where(i < n_tiles - 1, 1.0, 0.0)
        dh2 = jnp.zeros((tm, D_MODEL), F32)
        for c0 in range(0, cdim, CONV_CHUNK):
            sl = slice(c0, c0 + CONV_CHUNK)
            du = du_ref[:, sl]
            xp1, xp2 = _conv_taps_next(du, halo_ref[0:1, sl] * keep, halo_ref[1:2, sl] * keep)
            dup = (cw_ref[2:3, sl] * du + cw_ref[1:2, sl] * xp1 + cw_ref[0:1, sl] * xp2).astype(BF16)
            dup_ref[:, sl] = dup
            dh2 = dh2 + _dot_nt(dup, wup_ref[c0 // CONV_CHUNK])
        x1n, r3 = _rms_stats(x1_ref[...])
        d1, dg3 = _rms_bwd(dh2, x1n, r3, g3_ref[...])
        dx1 = dx2_ref[...] + d1
        dx1_ref[...] = dx1
        dg3_ref[...] += dg3
        on, r2 = _rms_stats(o_ref[...])
        do, dg2 = _rms_bwd(dx1, on, r2, g2_ref[...])
        dg2_ref[...] += dg2
        dob = do.astype(BF16)
        do_ref[...] = dob
        dmixed = _dot_nt(dob, wout_ref[...])
        ga = gate_ref[:, 0:D_MODEL]
        gb = gate_ref[:, D_MODEL:2 * D_MODEL]
        dpa = (dmixed * ga).astype(BF16)
        dpb = (dmixed * gb).astype(BF16)
        dpa_ref[...] = dpa
        dpb_ref[...] = dpb
        dga = dmixed * pa_ref[...] * ga * (1.0 - ga)
        dgb = dmixed * pb_ref[...] * gb * (1.0 - gb)
        dgt_ref[:, 0:D_MODEL] = dga.astype(BF16)
        dgt_ref[:, D_MODEL:2 * D_MODEL] = dgb.astype(BF16)
        dbg_ref[:, 0:D_MODEL] += jnp.sum(dga, axis=0, keepdims=True)
        dbg_ref[:, D_MODEL:2 * D_MODEL] += jnp.sum(dgb, axis=0, keepdims=True)
        dya_ref[...] = _dot_nt(dpa, wba_ref[...]).astype(BF16)
        dyb_ref[...] = _dot_nt(dpb, wbb_ref[...]).astype(BF16)

    def o_(n, dt):
        return (_sds((t_rows, n), dt), _row(tm, n))

    def acc(r, n):
        return (_sds((r, n), F32), _full((r, n)))

    halo = pl.BlockSpec((8, cdim), lambda i: (jnp.minimum((i + 1) * hb, last_blk), 0))
    ins = [(du, _row(tm, cdim)), (du, halo), (convw8, _full(convw8.shape)), (wup, _full(wup.shape)),
           (dx2, _row(tm, D_MODEL)), (x1, _row(tm, D_MODEL)), (g3, _full(g3.shape)), (o, _row(tm, D_MODEL)),
           (g2, _full(g2.shape)), (wout, _full(wout.shape)), (gate, _row(tm, 2048)), (pa, _row(tm, D_MODEL)),
           (pb, _row(tm, D_MODEL)), (wba, _full(wba.shape)), (wbb, _full(wbb.shape))]
    outs = [o_(cdim, BF16), o_(D_MODEL, F32), o_(D_MODEL, BF16), o_(D_MODEL, BF16), o_(D_MODEL, BF16),
            o_(2048, BF16), o_(1024, BF16), o_(1024, BF16), acc(1, D_MODEL), acc(1, D_MODEL), acc(1, 2048)]
    return _rows_call("bwd_mid", body, t_rows, tm, ins, outs)


def _bwd_in(dqs, dks, dvs, dqm, dkm, dvm, tabs, consts, cq, ckv, gq, gkv, wuq, wk, wv, dgates, win, x, g1, dx1, tm):
    t_rows = x.shape[0]

    def body(dqs_ref, dks_ref, dvs_ref, dqm_ref, dkm_ref, dvm_ref, ca, sa1, sa2, cb, sb1, sb2, c_ref, cq_ref,
             ckv_ref, gq_ref, gkv_ref, wuq_ref, wk_ref, wv_ref, dgt_ref, win_ref, x_ref, g1_ref, dx1_ref,
             dz_ref, dqb_ref, dx_ref, dgq_ref, dgkv_ref, dg1_ref):
        i = pl.program_id(0)

        @pl.when(i == 0)
        def _():
            dgq_ref[...] = jnp.zeros(dgq_ref.shape, F32)
            dgkv_ref[...] = jnp.zeros(dgkv_ref.shape, F32)
            dg1_ref[...] = jnp.zeros(dg1_ref.shape, F32)

        ta = (ca[...], sa1[...], sa2[...])
        tb = (cb[...], sb1[...], sb2[...])
        dz_ref[:, Z_QA:Z_KA] = _rope_t(dqs_ref[...] * SCALE_A, *ta, A_HEAD_DIM // 2).astype(BF16)
        dz_ref[:, Z_KA:Z_VA] = _rope_t(dks_ref[...], *ta, A_HEAD_DIM // 2).astype(BF16)
        dz_ref[:, Z_VA:Z_CQ] = dvs_ref[...].astype(BF16)
        dqm = jnp.concatenate([dqm_ref[h] for h in range(HEADS)], axis=1)
        dqb = _rope_t(dqm * SCALE_B, *tb, ROPE_DIM // 2).astype(BF16)
        dqb_ref[...] = dqb
        dcqn = _dot_nt(dqb, wuq_ref[...])
        cqn, rq = _rms_stats(cq_ref[...])
        dcq, dgq = _rms_bwd(dcqn, cqn, rq, gq_ref[...])
        dgq_ref[...] += dgq
        dz_ref[:, Z_CQ:Z_CKV] = dcq.astype(BF16)
        dkm = dkm_ref[...]
        dslot = dkm[:, 0:LANES]
        for h in range(1, HEADS):
            dslot = dslot + dkm[:, h * LANES:(h + 1) * LANES]
        dz_ref[:, Z_KR:Z_GATE] = _rope_t(dslot * c_ref[10:11, :], *tb, ROPE_DIM // 2).astype(BF16)
        dckvn = _dot_nt(dkm.astype(BF16), wk_ref[...]) + _dot_nt(dvm_ref[...].astype(BF16), wv_ref[...])
        ckvn, rkv = _rms_stats(ckv_ref[...])
        dckv, dgkv = _rms_bwd(dckvn, ckvn, rkv, gkv_ref[...])
        dgkv_ref[...] += dgkv
        dz_ref[:, Z_CKV:Z_KR] = dckv.astype(BF16)
        dz_ref[:, Z_GATE:ZW] = dgt_ref[...]
        dh1 = _dot_nt(dz_ref[...], win_ref[...])
        xn, r1 = _rms_stats(x_ref[...])
        d0, dg1 = _rms_bwd(dh1, xn, r1, g1_ref[...])
        dg1_ref[...] += dg1
        dx_ref[...] = dx1_ref[...] + d0

    def acc(n):
        return (_sds((1, n), F32), _full((1, n)))

    ins = [(dqs, _row(tm, 1024)), (dks, _row(tm, 256)), (dvs, _row(tm, 256)), (dqm, _heads(tm, HEADS)),
           (dkm, _row(tm, 1024)), (dvm, _row(tm, 1024))] + [(t, _row(tm, LANES)) for t in tabs] + [
           (consts, _full(consts.shape)), (cq, _row(tm, 256)), (ckv, _row(tm, 128)), (gq, _full(gq.shape)),
           (gkv, _full(gkv.shape)), (wuq, _full(wuq.shape)), (wk, _full(wk.shape)), (wv, _full(wv.shape)),
           (dgates, _row(tm, 2048)), (win, _full(win.shape)), (x, _row(tm, D_MODEL)), (g1, _full(g1.shape)),
           (dx1, _row(tm, D_MODEL))]
    outs = [(_sds((t_rows, ZW), BF16), _row(tm, ZW)), (_sds((t_rows, 1024), BF16), _row(tm, 1024)),
            (_sds((t_rows, D_MODEL), F32), _row(tm, D_MODEL)), acc(256), acc(128), acc(D_MODEL)]
    return _rows_call("bwd_in", body, t_rows, tm, ins, outs)


def _pick_cols(n):
    best = LANES
    for d in range(LANES, min(n, 1408) + 1, LANES):
        if n % d == 0:
            best = d
    return best


def _mm_tn(name, a, b, column_shards=1):
    t_rows, m = a.shape
    n = b.shape[1]
    bk = min(1024, t_rows)
    bm, bn = _pick_cols(m), _pick_cols(n // column_shards)
    per_shard = n // column_shards // bn

    def body(a_ref, b_ref, o_ref):
        @pl.when(pl.program_id(2) == 0)
        def _():
            o_ref[...] = jnp.zeros((bm, bn), F32)

        o_ref[...] += _dot_tn(a_ref[...].astype(BF16), b_ref[...].astype(BF16))

    return pl.pallas_call(
        body, name=name, grid=(m // bm, n // bn, t_rows // bk),
        in_specs=[pl.BlockSpec((bk, bm), lambda i, j, k: (k, i)), pl.BlockSpec((bk, bn), lambda i, j, k: (k, j))],
        out_specs=(pl.BlockSpec((bm, bn), lambda i, j, k: (i, j)) if column_shards == 1 else
                   pl.BlockSpec((None, bm, bn), lambda i, j, k: (j // per_shard, i, j % per_shard))),
        out_shape=_sds((m, n) if column_shards == 1 else (column_shards, m, n // column_shards), F32),
        compiler_params=pltpu.CompilerParams(dimension_semantics=("arbitrary",) * 3, vmem_limit_bytes=VMEM_LIMIT),
    )(a, b)


PACK_ROWS = 512


ADD_TILE_ELEMS = 1 << 17


def _add_rows(rows, cols):
    best = 16
    for d in range(16, rows + 1, 16):
        if rows % d == 0 and d * cols <= ADD_TILE_ELEMS:
            best = d
    assert rows % best == 0
    return best


def _add_pair(name, g, recv, half):
    _, _, rows, cols = g.shape
    t = _add_rows(rows, cols)

    def body(h_ref, g_ref, r_ref, o_ref):
        o_ref[...] = (g_ref[:, 0] + r_ref[...]).astype(BF16)

    spec = pl.BlockSpec((4, t, cols), lambda i, h: (0, i, 0))
    grid_spec = pltpu.PrefetchScalarGridSpec(
        num_scalar_prefetch=1, grid=(rows // t,),
        in_specs=[pl.BlockSpec((4, 1, t, cols), lambda i, h: (0, h[0], i, 0)), spec], out_specs=spec)
    return pl.pallas_call(body, name=name, grid_spec=grid_spec,
                          out_shape=_sds(recv.shape, BF16))(jnp.reshape(half, (1,)).astype(jnp.int32), g, recv)


def _add_chips(name, parts):
    _, rows, cols = parts.shape
    t = _add_rows(rows, cols)

    def body(p_ref, o_ref):
        acc = p_ref[0].astype(F32)
        for j in range(1, 4):
            acc = acc + p_ref[j].astype(F32)
        o_ref[...] = acc

    return pl.pallas_call(body, name=name, grid=(rows // t,),
                          in_specs=[pl.BlockSpec((4, t, cols), lambda i: (0, i, 0))],
                          out_specs=pl.BlockSpec((t, cols), lambda i: (i, 0)),
                          out_shape=_sds((rows, cols), F32))(parts)


def _add_devices(parts):
    n, rows, _ = parts.shape

    def body(p_ref, o_ref):
        acc = p_ref[0]
        for j in range(1, n):
            acc = acc + p_ref[j]
        o_ref[...] = acc

    return pl.pallas_call(body, name="small_add", grid=(1,),
                          in_specs=[pl.BlockSpec((n, rows, LANES), lambda i: (0, 0, 0))],
                          out_specs=pl.BlockSpec((rows, LANES), lambda i: (0, 0)),
                          out_shape=_sds((rows, LANES), F32))(parts)


def _adam_rows(k, n):
    target = max(8, (1 << 20) // (4 * n))
    if k <= target:
        return k
    best = None
    for d in range(8, target + 1, 8):
        if k % d == 0:
            best = d
    return best if best is not None else k


def _adamw(name, w, g, m, v):
    k, n = w.shape
    bk = _adam_rows(k, n)
    c1 = 1.0 - ADAM_B1 ** ADAM_STEP
    c2 = 1.0 - ADAM_B2 ** ADAM_STEP

    def body(w_ref, g_ref, m_ref, v_ref, d_ref, mo_ref, vo_ref):
        g_ = g_ref[...]
        m_ = ADAM_B1 * m_ref[...] + (1.0 - ADAM_B1) * g_
        v_ = ADAM_B2 * v_ref[...] + (1.0 - ADAM_B2) * (g_ * g_)
        mo_ref[...] = m_
        vo_ref[...] = v_
        d_ref[...] = -ADAM_LR * ((m_ / c1) / (jnp.sqrt(v_ / c2) + ADAM_EPS) + ADAM_WD * w_ref[...])

    spec = pl.BlockSpec((bk, n), lambda i: (i, 0))
    return pl.pallas_call(body, name=name, grid=(k // bk,), in_specs=[spec] * 4, out_specs=[spec] * 3,
                          out_shape=[_sds((k, n), F32)] * 3,
                          compiler_params=pltpu.CompilerParams(vmem_limit_bytes=VMEM_LIMIT))(w, g, m, v)


_HBM = pl.BlockSpec(memory_space=pltpu.HBM)


def _me():
    return lax.axis_index("x"), lax.axis_index("y"), lax.axis_index("c")


def _other_chips(x, y):
    return [(1 - x, y), (x, 1 - y), (1 - x, 1 - y)]


def _gather_weights(shards):
    n = len(shards)

    def body(*refs):
        x_refs, out_refs = refs[:n], refs[n:2 * n]
        send_sems, recv_sems = refs[2 * n:]
        x, y, c = _me()
        sibling = (x, y, 1 - c)
        chips = _other_chips(x, y)

        def copy(k, src, dst, to):
            return pltpu.make_async_remote_copy(src_ref=src, dst_ref=dst, send_sem=send_sems.at[k],
                                                recv_sem=recv_sems.at[k], device_id=to, device_id_type=MESH)

        first, passed = [], []
        for a, (x_ref, out_ref) in enumerate(zip(x_refs, out_refs)):
            for j, (cx, cy) in enumerate(chips):
                first.append(copy(6 * a + j, x_ref.at[c], out_ref.at[2 * x + y, c], (cx, cy, c)))
        for cp in first:
            cp.start()
        for a, (x_ref, out_ref) in enumerate(zip(x_refs, out_refs)):
            for j, (cx, cy) in enumerate(chips):
                landed = out_ref.at[2 * cx + cy, c]
                copy(6 * a + j, x_ref.at[c], landed, (cx, cy, c)).wait_recv()
                passed.append(copy(6 * a + 3 + j, landed, landed, sibling))
                passed[-1].start()
        for a, (x_ref, out_ref) in enumerate(zip(x_refs, out_refs)):
            for j, (cx, cy) in enumerate(chips):
                theirs = out_ref.at[2 * cx + cy, 1 - c]
                copy(6 * a + 3 + j, theirs, theirs, sibling).wait_recv()
        for cp in first + passed:
            cp.wait_send()

    return pl.pallas_call(
        body, name="gather_weights", out_shape=[_sds((4,) + s.shape, s.dtype) for s in shards],
        in_specs=[_HBM] * n, out_specs=[_HBM] * n,
        scratch_shapes=[pltpu.SemaphoreType.DMA((6 * n,)), pltpu.SemaphoreType.DMA((6 * n,))],
    )(*shards)


def _swap_sibling(name, vs, other_half=False):
    n = len(vs)

    def body(*refs):
        v_refs, out_refs = refs[:n], refs[n:2 * n]
        send_sems, recv_sems = refs[2 * n:]
        x, y, c = _me()
        cps = [pltpu.make_async_remote_copy(src_ref=v_ref.at[:, 1 - c] if other_half else v_ref, dst_ref=out_ref,
                                            send_sem=send_sems.at[a], recv_sem=recv_sems.at[a],
                                            device_id=(x, y, 1 - c), device_id_type=MESH)
               for a, (v_ref, out_ref) in enumerate(zip(v_refs, out_refs))]
        for cp in cps:
            cp.start()
        for cp in cps:
            cp.wait()

    def landing(v):
        return _sds((v.shape[0],) + v.shape[2:] if other_half else v.shape, v.dtype)

    return pl.pallas_call(
        body, name=name, out_shape=[landing(v) for v in vs], in_specs=[_HBM] * n, out_specs=[_HBM] * n,
        scratch_shapes=[pltpu.SemaphoreType.DMA((n,)), pltpu.SemaphoreType.DMA((n,))],
    )(*vs)


def _scatter_chips(ss):
    n = len(ss)

    def body(*refs):
        s_refs, out_refs = refs[:n], refs[n:2 * n]
        send_sems, recv_sems = refs[2 * n:]
        pairs = []
        for a, (s_ref, out_ref) in enumerate(zip(s_refs, out_refs)):
            pairs += _chip_copies(s_ref, out_ref, send_sems, recv_sems, True, 3 * a)
        for send, _ in pairs:
            send.start()
        for _, recv in pairs:
            recv.wait_recv()
        for send, _ in pairs:
            send.wait_send()

    return pl.pallas_call(
        body, name="scatter_chips", out_shape=[_sds(s.shape, s.dtype) for s in ss],
        in_specs=[_HBM] * n, out_specs=[_HBM] * n,
        scratch_shapes=[pltpu.SemaphoreType.DMA((3 * n,)), pltpu.SemaphoreType.DMA((3 * n,))],
    )(*ss)


_SEM = pl.BlockSpec(memory_space=pltpu.SEMAPHORE)
_EFFECT = pltpu.SideEffectType.DATAFLOW_SIDE_EFFECTING


def _chip_copies(v_ref, land_ref, send_sems, recv_sems, per_chip_piece, sem0=0):
    x, y, c = _me()
    k = 2 * x + y
    out = []
    for j, (cx, cy) in enumerate(_other_chips(x, y)):
        src = v_ref.at[2 * cx + cy] if per_chip_piece else v_ref
        sems = dict(send_sem=send_sems.at[sem0 + j], recv_sem=recv_sems.at[sem0 + j], device_id=(cx, cy, c),
                    device_id_type=MESH)
        send = pltpu.make_async_remote_copy(src_ref=src, dst_ref=land_ref.at[k], **sems)
        recv = pltpu.make_async_remote_copy(src_ref=src, dst_ref=land_ref.at[2 * cx + cy], **sems)
        out.append((send, recv))
    return out


def _chips_start(name, vs, per_chip_piece, after=None):
    n = len(vs)
    lands = [v.shape if per_chip_piece else (4,) + v.shape for v in vs]

    def body(*refs):
        v_refs, land_refs = refs[:n], refs[n:2 * n]
        send_sems, recv_sems = refs[-2 * n - 3], refs[-2 * n - 2]
        token = refs[-1]
        for a in range(n):
            for send, _ in _chip_copies(v_refs[a], land_refs[a], send_sems, recv_sems, per_chip_piece, 3 * a):
                send.start()
        token[...] = jnp.zeros_like(token)

    extra = () if after is None else (after,)
    hbm = [pltpu.with_memory_space_constraint(v, pltpu.HBM) for v in vs]
    zones = [pltpu.with_memory_space_constraint(lax.empty(s, v.dtype), pltpu.HBM) for s, v in zip(lands, vs)]
    out = pl.pallas_call(
        body, name=name,
        out_shape=(pltpu.SemaphoreType.DMA((3 * n,)), pltpu.SemaphoreType.DMA((3 * n,)),
                   *[pltpu.HBM(v.shape, v.dtype) for v in vs], *[pltpu.HBM(s, v.dtype) for s, v in zip(lands, vs)],
                   _sds((8, LANES), F32)),
        in_specs=(_HBM,) * (2 * n) + (pl.BlockSpec(memory_space=pl.ANY),) * len(extra),
        out_specs=(_SEM, _SEM) + (_HBM,) * (2 * n) + (pl.BlockSpec(memory_space=pltpu.VMEM),),
        input_output_aliases={i: 2 + i for i in range(2 * n)},
        compiler_params=pltpu.CompilerParams(has_side_effects=_EFFECT),
    )(*hbm, *zones, *extra)
    return out[0], out[1], list(out[2:2 + n]), list(out[2 + n:2 + 2 * n]), out[-1]


def _chips_wait(name, send_sems, recv_sems, v_thru, land_thru, per_chip_piece, after):
    n = len(v_thru)

    def body(*refs):
        v_refs, land_refs = refs[:n], refs[n:2 * n]
        send_sems, recv_sems = refs[2 * n], refs[2 * n + 1]
        for a in range(n):
            for send, recv in _chip_copies(v_refs[a], land_refs[a], send_sems, recv_sems, per_chip_piece, 3 * a):
                send.wait_send()
                recv.wait_recv()

    out = pl.pallas_call(
        body, name=name,
        out_shape=tuple(pltpu.HBM(a.shape, a.dtype) for a in list(v_thru) + list(land_thru)),
        in_specs=(_HBM,) * (2 * n) + (_SEM, _SEM, pl.BlockSpec(memory_space=pl.ANY)), out_specs=(_HBM,) * (2 * n),
        input_output_aliases={i: i for i in range(2 * n)},
        compiler_params=pltpu.CompilerParams(has_side_effects=_EFFECT),
    )(*v_thru, *land_thru, send_sems, recv_sems, after)
    return list(out[n:])


def _gather_small(name, v):
    def body(v_ref, out_ref, send_sems, recv_sems, local_sem):
        x, y, c = _me()
        me = 4 * x + 2 * y + c
        mine = pltpu.make_async_copy(v_ref, out_ref.at[me], local_sem)
        mine.start()
        peers = []
        for f in range(1, 8):
            fx, fy, fc = (f >> 2) & 1, (f >> 1) & 1, f & 1
            px = 1 - x if fx else x
            py = 1 - y if fy else y
            pc = 1 - c if fc else c
            peers.append((f - 1, (px, py, pc)))
        sends = [pltpu.make_async_remote_copy(src_ref=v_ref, dst_ref=out_ref.at[me], send_sem=send_sems.at[k],
                                              recv_sem=recv_sems.at[k], device_id=peer, device_id_type=MESH)
                 for k, peer in peers]
        for cp in sends:
            cp.start()
        for k, (px, py, pc) in peers:
            pltpu.make_async_remote_copy(src_ref=v_ref, dst_ref=out_ref.at[4 * px + 2 * py + pc],
                                         send_sem=send_sems.at[k], recv_sem=recv_sems.at[k],
                                         device_id=(px, py, pc), device_id_type=MESH).wait_recv()
        for cp in sends:
            cp.wait_send()
        mine.wait()

    return pl.pallas_call(
        body, name=name, out_shape=_sds((8,) + v.shape, v.dtype), in_specs=[_HBM], out_specs=_HBM,
        scratch_shapes=[pltpu.SemaphoreType.DMA((7,)), pltpu.SemaphoreType.DMA((7,)), pltpu.SemaphoreType.DMA],
    )(v)


_BIG = (("w_in", (1024, 3232), 1), ("w_uq", (256, 768), 1), ("w_ukv", (128, 1024), 1), ("w_branch_a", (512, 1024), 1),
        ("w_branch_b", (512, 1024), 1), ("w_out", (1024, 1024), 0), ("w_up", (1024, 5632), 1),
        ("w_down", (2816, 1024), 0), ("w_ple_gate", (1024, 1024), 0), ("w_ple", (256, 1024), 1))


def _shard_shape(shape, axis):
    return (shape[0] // 4, shape[1]) if axis == 0 else (shape[0], shape[1] // 4)


def _half_rows(shape, axis):
    k, n = _shard_shape(shape, axis)
    return k * n // (2 * LANES)


_EARLY = ("w_in", "w_uq", "w_ukv")
_LATE = ("w_branch_a", "w_branch_b", "w_out", "w_up", "w_down", "w_ple_gate", "w_ple")
_NATURAL = ("w_in", "w_up", "w_down", "w_out", "w_ple_gate")
_EARLY_PACKED = tuple(b for b in _BIG if b[0] in _EARLY and b[0] not in _NATURAL)
_LATE_PACKED = tuple(b for b in _BIG if b[0] in _LATE and b[0] not in _NATURAL)
_SHARD = {name: _shard_shape(shape, axis) for name, shape, axis in _BIG}


def _halves(a):
    return a.reshape(a.shape[:-2] + (2, a.shape[-2] // 2, a.shape[-1]))


def _rows_joined(a):
    return a.reshape(a.shape[:-3] + (a.shape[-3] * a.shape[-2], a.shape[-1]))


def _pack_pad(group):
    return -sum(_half_rows(shape, axis) for _, shape, axis in group) % PACK_ROWS


def _pack_shards(shards, dtype, group):
    parts = [shards[name].astype(dtype).reshape(2, _half_rows(shape, axis), LANES) for name, shape, axis in group]
    return jnp.concatenate(parts + [jnp.zeros((2, _pack_pad(group), LANES), dtype)], axis=1)


def _unpack_gathered(g, group):
    out, off = {}, 0
    for name, shape, axis in group:
        r = _half_rows(shape, axis)
        k, n = _shard_shape(shape, axis)
        w = g[:, :, off:off + r, :].reshape(4, k, n)
        out[name] = w.reshape(shape) if axis == 0 else w.transpose(1, 0, 2).reshape(shape)
        off += r
    return out


def _pack_grads(grads, group):
    parts = []
    for name, shape, axis in group:
        k, n = _shard_shape(shape, axis)
        g = grads[name]
        g4 = g.reshape(4, k, n) if axis == 0 else g.reshape(k, 4, n).transpose(1, 0, 2)
        parts.append(g4.reshape(4, 2, _half_rows(shape, axis), LANES))
    return jnp.concatenate(parts + [jnp.zeros((4, 2, _pack_pad(group), LANES), F32)], axis=2)


def _unpack_shard_grads(f, group):
    out, off = {}, 0
    for name, shape, axis in group:
        r = _half_rows(shape, axis)
        out[name] = f[:, off:off + r, :].reshape(_shard_shape(shape, axis))
        off += r
    return out


def _pad_slots(w, heads, dim, axis):
    if axis == 1:
        k = w.shape[0]
        return jnp.pad(w.reshape(k, heads, dim), ((0, 0), (0, 0), (0, LANES - dim))).reshape(k, heads * LANES)
    n = w.shape[1]
    return jnp.pad(w.reshape(heads, dim, n), ((0, 0), (0, LANES - dim), (0, 0))).reshape(heads * LANES, n)


def _unpad_slots(w, heads, dim, axis):
    if axis == 1:
        k = w.shape[0]
        return w.reshape(k, heads, LANES)[:, :, :dim].reshape(k, heads * dim)
    n = w.shape[1]
    return w.reshape(heads, LANES, n)[:, :dim, :].reshape(heads * dim, n)


def _pad_w_in(w):
    kr = jnp.pad(w[:, 1152:1184], ((0, 0), (NOPE_DIM, LANES - NOPE_DIM - ROPE_DIM)))
    return jnp.concatenate([_pad_slots(w[:, 0:512], HEADS, A_HEAD_DIM, 1),
                            _pad_slots(w[:, 512:640], A_KV_HEADS, A_HEAD_DIM, 1),
                            _pad_slots(w[:, 640:768], A_KV_HEADS, A_HEAD_DIM, 1),
                            w[:, 768:1024], w[:, 1024:1152], kr, w[:, 1184:3232]], axis=1)


def _unpad_w_in(w):
    return jnp.concatenate([_unpad_slots(w[:, Z_QA:Z_KA], HEADS, A_HEAD_DIM, 1),
                            _unpad_slots(w[:, Z_KA:Z_VA], A_KV_HEADS, A_HEAD_DIM, 1),
                            _unpad_slots(w[:, Z_VA:Z_CQ], A_KV_HEADS, A_HEAD_DIM, 1),
                            w[:, Z_CQ:Z_CKV], w[:, Z_CKV:Z_KR],
                            w[:, Z_KR + NOPE_DIM:Z_KR + NOPE_DIM + ROPE_DIM], w[:, Z_GATE:ZW]], axis=1)


_SMALL = (("attn_pre_norm", 1024), ("attn_post_norm", 1024), ("b_gate", 2048), ("sinks", 8), ("q_a_norm", 256),
          ("kv_a_norm", 128), ("mlp_pre_norm", 1024), ("mlp_post_norm", 1024), ("conv_b", 5632), ("ple_norm", 1024),
          ("conv_w", 3 * 5632), ("loss", 1))


def _small_rows(n):
    return 8 * -(-n // (8 * LANES))


def _pack_small(vals):
    parts = []
    for name, n in _SMALL:
        r = _small_rows(n)
        parts.append(jnp.pad(vals[name].reshape(-1), (0, r * LANES - n)).reshape(r, LANES))
    return jnp.concatenate(parts, axis=0)


def _unpack_small(buf):
    out, off = {}, 0
    for name, n in _SMALL:
        r = _small_rows(n)
        out[name] = buf[off:off + r].reshape(-1)[:n]
        off += r
    return out


def kernel(x, p, positions, attn_pre_norm, attn_post_norm, w_in, b_gate, sinks, q_a_norm, w_uq, kv_a_norm, w_ukv, w_branch_a, w_branch_b, w_out, mlp_pre_norm, mlp_post_norm, w_up, conv_w, conv_b, w_down, ple_norm, w_ple_gate, w_ple, loss_target, m_attn_pre_norm, m_attn_post_norm, m_w_in, m_b_gate, m_sinks, m_q_a_norm, m_w_uq, m_kv_a_norm, m_w_ukv, m_w_branch_a, m_w_branch_b, m_w_out, m_mlp_pre_norm, m_mlp_post_norm, m_w_up, m_conv_w, m_conv_b, m_w_down, m_ple_norm, m_w_ple_gate, m_w_ple, v_attn_pre_norm, v_attn_post_norm, v_w_in, v_b_gate, v_sinks, v_q_a_norm, v_w_uq, v_kv_a_norm, v_w_ukv, v_w_branch_a, v_w_branch_b, v_w_out, v_mlp_pre_norm, v_mlp_post_norm, v_w_up, v_conv_w, v_conv_b, v_w_down, v_ple_norm, v_w_ple_gate, v_w_ple):
    names = ["attn_pre_norm", "attn_post_norm", "w_in", "b_gate", "sinks", "q_a_norm", "w_uq", "kv_a_norm", "w_ukv",
             "w_branch_a", "w_branch_b", "w_out", "mlp_pre_norm", "mlp_post_norm", "w_up", "conv_w", "conv_b",
             "w_down", "ple_norm", "w_ple_gate", "w_ple"]
    wts = dict(zip(names, [attn_pre_norm, attn_post_norm, w_in, b_gate, sinks, q_a_norm, w_uq, kv_a_norm, w_ukv,
                           w_branch_a, w_branch_b, w_out, mlp_pre_norm, mlp_post_norm, w_up, conv_w, conv_b, w_down,
                           ple_norm, w_ple_gate, w_ple]))
    moms = dict(zip(names, [m_attn_pre_norm, m_attn_post_norm, m_w_in, m_b_gate, m_sinks, m_q_a_norm, m_w_uq,
                            m_kv_a_norm, m_w_ukv, m_w_branch_a, m_w_branch_b, m_w_out, m_mlp_pre_norm,
                            m_mlp_post_norm, m_w_up, m_conv_w, m_conv_b, m_w_down, m_ple_norm, m_w_ple_gate, m_w_ple]))
    vars_ = dict(zip(names, [v_attn_pre_norm, v_attn_post_norm, v_w_in, v_b_gate, v_sinks, v_q_a_norm, v_w_uq,
                             v_kv_a_norm, v_w_ukv, v_w_branch_a, v_w_branch_b, v_w_out, v_mlp_pre_norm,
                             v_mlp_post_norm, v_w_up, v_conv_w, v_conv_b, v_w_down, v_ple_norm, v_w_ple_gate, v_w_ple]))
    w2 = {n: a.reshape(a.shape[-2:]) for n, a in wts.items()}
    m2 = {n: a.reshape(a.shape[-2:]) for n, a in moms.items()}
    v2 = {n: a.reshape(a.shape[-2:]) for n, a in vars_.items()}

    t_rows = x.shape[-2]
    tm = min(256, t_rows)
    xc, yc, cc = lax.axis_index("x"), lax.axis_index("y"), lax.axis_index("c")
    chip = 2 * xc + yc

    x2d = x.reshape(t_rows, D_MODEL)
    p2d = p.reshape(t_rows, PLE_DIM)
    tgt = loss_target.reshape(t_rows, D_MODEL)
    pos_f = positions.reshape(t_rows, 1).astype(F32)

    def own_slot_filled(gathered, mine):
        return [lax.dynamic_update_slice(g, m[None], (chip, 0, 0, 0)) for g, m in zip(gathered, mine)]

    def shard_lists(group, packed_group):
        return ([_halves(w2[n].astype(BF16)) for n in group if n in _NATURAL]
                + [_pack_shards(w2, BF16, packed_group)])

    cw_rows = 3 * 1408 // LANES
    conv_mine = jnp.pad(w2["conv_w"].reshape(cw_rows, LANES), ((0, 48 - cw_rows), (0, 0))).reshape(2, 24, LANES)
    early_mine = shard_lists(_EARLY, _EARLY_PACKED) + [conv_mine]
    late_mine = shard_lists(_LATE, _LATE_PACKED)
    early = own_slot_filled(_gather_weights(early_mine), early_mine)
    late_sems = _chips_start("gather_late_start", late_mine, False, after=early[0])
    late_token = late_sems[4][0:1, 0:1]
    full = _unpack_gathered(early[1], _EARLY_PACKED)
    full["w_in"] = _rows_joined(early[0]).transpose(1, 0, 2).reshape(D_MODEL, 3232)
    conv_full = early[2].reshape(4, 48, LANES)[:, :cw_rows].reshape(4, 3, 1408).transpose(1, 0, 2).reshape(3, 2 * D_FF)
    convw8 = jnp.pad(conv_full, ((0, 5), (0, 0)))

    win = _pad_w_in(full["w_in"])
    wuq = _pad_slots(full["w_uq"], HEADS, NOPE_DIM + ROPE_DIM, 1)
    ukv = full["w_ukv"].reshape(KV_LORA, HEADS, NOPE_DIM + V_DIM)
    wk = _pad_slots(ukv[:, :, :NOPE_DIM].reshape(KV_LORA, HEADS * NOPE_DIM), HEADS, NOPE_DIM, 1)
    wv = _pad_slots(ukv[:, :, NOPE_DIM:].reshape(KV_LORA, HEADS * V_DIM), HEADS, V_DIM, 1)
    g1, g2, g3, g4, g5 = (w2["attn_pre_norm"], w2["attn_post_norm"], w2["mlp_pre_norm"], w2["mlp_post_norm"],
                          w2["ple_norm"])
    gq, gkv, bg, convb = w2["q_a_norm"], w2["kv_a_norm"], w2["b_gate"], w2["conv_b"]
    swa_tile = min(SWA_TILE, t_rows)
    sink_rows = jnp.repeat(w2["sinks"].reshape(A_KV_HEADS, SWA_GROUP, 1), swa_tile, axis=2).reshape(
        A_KV_HEADS, 1, SWA_GROUP * swa_tile)
    swa_bias = _swa_bias(swa_tile)

    consts = _rope_consts()
    tabs = _rope_tables(pos_f, consts, tm)
    h1, qs, ks, vs, cq, cqn, ckv, ckvn, qm, km, vm, gate = _fwd_in(x2d, g1, win, bg + late_token, gq, gkv, wuq, wk, wv,
                                                                   tabs, tm)
    ya, lse_a = _swa_fwd(qs, ks, vs, swa_bias, sink_rows)
    yb, lse_b = _mla_fwd(qm, km, vm)
    late = own_slot_filled(_chips_wait("gather_late_wait", *late_sems[:4], False, after=yb), late_mine)
    full = _unpack_gathered(late[-1], _LATE_PACKED)
    wba = _pad_slots(full["w_branch_a"], HEADS, A_HEAD_DIM, 0)
    wbb = _pad_slots(full["w_branch_b"], HEADS, V_DIM, 0)
    wple = full["w_ple"]
    natural = dict(zip([n for n in _LATE if n in _NATURAL], late))
    wup = _rows_joined(natural["w_up"])
    wout, wdown, wpg = (_rows_joined(natural[n]).reshape(-1, D_MODEL) for n in ("w_out", "w_down", "w_ple_gate"))
    pa, pb, mixed, o, x1, h2 = _fwd_mix(x2d, ya, yb, gate, wba, wbb, wout, g2, g3, tm)
    up, a = _fwd_up(h2, wup, convw8, convb, tm)
    ff, x2, e, n5, sg, dx3, loss_part = _fwd_out(a, wdown, x1, g4, p2d, wple, g5, wpg, tgt, tm)

    dpre, de, dx2, dff, du, dg5, dg4, dconvb, dconvw8 = _bwd_out(dx3, e, sg, x2, ff, g5, g4, wpg, wdown, up, convw8,
                                                                 convb, tm)
    dup, dx1, do, dpa, dpb, dgates, dya, dyb, dg3, dg2, dbg = _bwd_mid(du, convw8, wup, dx2, x1, g3, o, g2, wout, gate,
                                                                       pa, pb, wba, wbb, tm)
    late_grads = {
        "w_branch_a": _unpad_slots(_mm_tn("dw_branch_a", ya, dpa), HEADS, A_HEAD_DIM, 0),
        "w_branch_b": _unpad_slots(_mm_tn("dw_branch_b", yb, dpb), HEADS, V_DIM, 0),
        "w_out": _mm_tn("dw_out", mixed, do).reshape(4, D_MODEL // 4, D_MODEL),
        "w_up": _mm_tn("dw_up", h2, dup, column_shards=4),
        "w_down": _mm_tn("dw_down", a, dff).reshape(4, D_FF // 4, D_MODEL),
        "w_ple_gate": _mm_tn("dw_ple_gate", n5, dpre).reshape(4, D_MODEL // 4, D_MODEL),
        "w_ple": _mm_tn("dw_ple", p2d, de),
    }

    def pair_sums(tag, grads, group, packed_group):
        views = [_halves(grads[n]) for n in group if n in _NATURAL] + [_pack_grads(grads, packed_group)]
        theirs = _swap_sibling("swap_%s_grad_halves" % tag, views, other_half=True)
        return [_add_pair("rs_%s_add_pair_%d" % (tag, i), g, r, cc) for i, (g, r) in enumerate(zip(views, theirs))]

    late_pairs = pair_sums("late", late_grads, _LATE, _LATE_PACKED)
    rs_sems = _chips_start("scatter_late_start", late_pairs, True)
    rs_token = rs_sems[4][0:1, 0:1]

    dqs, dks, dvs, dsink_rows = _swa_bwd(qs, ks, vs, ya, dya, lse_a, swa_bias, sink_rows + rs_token)
    dsink = dsink_rows[:, 0:SWA_GROUP, 0]
    dqm, dkm, dvm = _mla_bwd(qm, km, vm, dyb, lse_b, _mla_delta(yb, dyb))
    dz, dqb, dx, dgq, dgkv, dg1 = _bwd_in(dqs, dks, dvs, dqm, dkm, dvm, tabs, consts, cq, ckv, gq, gkv, wuq, wk, wv,
                                           dgates, win, x2d, g1, dx1, tm)

    dwk = _unpad_slots(_mm_tn("dw_k", ckvn, dkm), HEADS, NOPE_DIM, 1).reshape(KV_LORA, HEADS, NOPE_DIM)
    dwv = _unpad_slots(_mm_tn("dw_v", ckvn, dvm), HEADS, V_DIM, 1).reshape(KV_LORA, HEADS, V_DIM)
    early_grads = {
        "w_in": _unpad_w_in(_mm_tn("dw_in", h1, dz)).reshape(D_MODEL, 4, 808).transpose(1, 0, 2),
        "w_uq": _unpad_slots(_mm_tn("dw_uq", cqn, dqb), HEADS, NOPE_DIM + ROPE_DIM, 1),
        "w_ukv": jnp.concatenate([dwk, dwv], axis=2).reshape(KV_LORA, HEADS * (NOPE_DIM + V_DIM)),
    }

    def finish(tag, pairs, landed, group, packed_group):
        reduced = []
        for i, (pair, land) in enumerate(zip(pairs, landed)):
            own = lax.dynamic_index_in_dim(pair, chip, 0, keepdims=True)
            reduced.append(_add_chips("rs_%s_add_chips_%d" % (tag, i),
                                      lax.dynamic_update_slice(land, own, (chip, 0, 0))))
        others = _swap_sibling("swap_%s_reduced_halves" % tag, reduced)
        both = [jnp.where(cc == 0, jnp.stack([r, o]), jnp.stack([o, r])) for r, o in zip(reduced, others)]
        out = _unpack_shard_grads(both[-1], packed_group)
        out.update({n: _rows_joined(b) for n, b in zip([n for n in group if n in _NATURAL], both)})
        return out

    early_pairs = pair_sums("early", early_grads, _EARLY, _EARLY_PACKED)
    shard_grads = finish("early", early_pairs, _scatter_chips(early_pairs), _EARLY, _EARLY_PACKED)
    late_landed = _chips_wait("scatter_late_wait", *rs_sems[:4], True, after=early_pairs[0])
    shard_grads.update(finish("late", late_pairs, late_landed, _LATE, _LATE_PACKED))

    small = {"attn_pre_norm": dg1, "attn_post_norm": dg2, "b_gate": dbg, "sinks": dsink, "q_a_norm": dgq,
             "kv_a_norm": dgkv, "mlp_pre_norm": dg3, "mlp_post_norm": dg4, "conv_b": dconvb, "ple_norm": dg5,
             "conv_w": dconvw8[0:3], "loss": loss_part}
    small_sum = _unpack_small(_add_devices(_gather_small("gather_small_grads", _pack_small(small))))
    for n in names:
        if n in small_sum and n != "conv_w":
            shard_grads[n] = small_sum[n].reshape(w2[n].shape)
    shard_grads["conv_w"] = lax.dynamic_index_in_dim(small_sum["conv_w"].reshape(3, 4, 1408), chip, 1, keepdims=False)

    loss = small_sum["loss"][0]

    g_out, d_out, m_out, v_out = [], [], [], []
    for n in names:
        g = shard_grads[n]
        d, mn, vn = _adamw("adamw_" + n, w2[n], g, m2[n], v2[n])
        shp = wts[n].shape
        g_out.append(g.reshape(shp))
        d_out.append(d.reshape(shp))
        m_out.append(mn.reshape(shp))
        v_out.append(vn.reshape(shp))
    return (loss, dx.reshape(x.shape), *g_out, *d_out, *m_out, *v_out)
```

```python
import functools
import math

import numpy as np
import jax
import jax.numpy as jnp
from jax import lax
from jax.experimental import pallas as pl
from jax.experimental.pallas import tpu as pltpu

F32 = jnp.float32
BF16 = jnp.bfloat16

D_MODEL = 1024
D_FF = 2816
PLE_DIM = 256
ROPE_THETA = 10000.0
RMS_EPS = 1e-6
SWA_WINDOW = 128
HEADS = 8
A_KV_HEADS = 2
A_HEAD_DIM = 64
Q_LORA = 256
KV_LORA = 128
NOPE_DIM = 64
ROPE_DIM = 32
V_DIM = 64
LANES = 128
ZW = 4096
NEG = -1e30
SCALE_A = A_HEAD_DIM ** -0.5
SCALE_B = (NOPE_DIM + ROPE_DIM) ** -0.5

ADAM_LR = 0.001
ADAM_B1 = 0.9
ADAM_B2 = 0.999
ADAM_EPS = 1e-08
ADAM_WD = 0.01
ADAM_STEP = 10

VMEM_LIMIT = 60 * 1024 * 1024
MESH_AXES = ("x", "y", "c")
MESH = pl.DeviceIdType.MESH

Z_QA, Z_KA, Z_VA, Z_CQ, Z_CKV, Z_KR, Z_GATE = 0, 1024, 1280, 1536, 1792, 1920, 2048


def _dot(a, b):
    return jnp.dot(a, b, preferred_element_type=F32)


def _dot_nt(a, b):
    return lax.dot_general(a, b, (((1,), (1,)), ((), ())), preferred_element_type=F32)


def _dot_tn(a, b):
    return lax.dot_general(a, b, (((0,), (0,)), ((), ())), preferred_element_type=F32)


def _rms_stats(x):
    r = lax.rsqrt(jnp.mean(x * x, axis=-1, keepdims=True) + RMS_EPS)
    return x * r, r


def _rms_bwd(dy, xn, r, g):
    dxn = dy * g
    dx = r * (dxn - xn * jnp.mean(dxn * xn, axis=-1, keepdims=True))
    dg = jnp.sum(dy * xn, axis=0, keepdims=True)
    return dx, dg


def _tile_lanes(t, n):
    return t if n == 1 else jnp.concatenate([t] * n, axis=1)


def _rope(x, c, s1, s2, half):
    w = x.shape[1]
    n = w // LANES
    return (x * _tile_lanes(c, n) + pltpu.roll(x, w - half, 1) * _tile_lanes(s1, n)
            + pltpu.roll(x, half, 1) * _tile_lanes(s2, n))


def _rope_t(dy, c, s1, s2, half):
    w = dy.shape[1]
    n = w // LANES
    return (dy * _tile_lanes(c, n) + pltpu.roll(dy * _tile_lanes(s1, n), half, 1)
            + pltpu.roll(dy * _tile_lanes(s2, n), w - half, 1))


def _sigmoid(x):
    return 1.0 / (1.0 + jnp.exp(-x))


_GELU_C = math.sqrt(2.0 / math.pi)


def _gelu_and_grad(x):
    x2 = x * x
    th = jnp.tanh(_GELU_C * (x + 0.044715 * x * x2))
    gel = 0.5 * x * (1.0 + th)
    dgel = 0.5 * (1.0 + th) + 0.5 * x * (1.0 - th * th) * (_GELU_C * (1.0 + 3.0 * 0.044715 * x2))
    return gel, dgel


def _conv_taps(up, h6, h7):
    rows = lax.broadcasted_iota(jnp.int32, up.shape, 0)
    r1 = pltpu.roll(up, 1, 0)
    r2 = pltpu.roll(up, 2, 0)
    xm1 = jnp.where(rows == 0, h7, r1)
    xm2 = jnp.where(rows == 0, h6, jnp.where(rows == 1, h7, r2))
    return xm1, xm2


def _conv_taps_next(du, n0, n1):
    tm = du.shape[0]
    rows = lax.broadcasted_iota(jnp.int32, du.shape, 0)
    r1 = pltpu.roll(du, tm - 1, 0)
    r2 = pltpu.roll(du, tm - 2, 0)
    xp1 = jnp.where(rows == tm - 1, n0, r1)
    xp2 = jnp.where(rows == tm - 2, n0, jnp.where(rows == tm - 1, n1, r2))
    return xp1, xp2


def _row(tm, n):
    return pl.BlockSpec((tm, n), lambda i: (i, 0))


def _full(shape):
    nd = len(shape)
    return pl.BlockSpec(tuple(shape), lambda i: (0,) * nd)


def _resident(shape):
    nd = len(shape)
    return pl.BlockSpec(tuple(shape), lambda i: (0,) * nd, pipeline_mode=pl.Buffered(1))


def _heads(tm, h):
    return pl.BlockSpec((h, tm, LANES), lambda i: (0, i, 0))


def _rows_call(name, body, t_rows, tm, ins, outs, scratch=()):
    return pl.pallas_call(
        body, name=name, grid=(t_rows // tm,),
        in_specs=[s for _, s in ins],
        out_specs=[s for _, s in outs],
        out_shape=[s for s, _ in outs],
        scratch_shapes=list(scratch),
        compiler_params=pltpu.CompilerParams(dimension_semantics=("arbitrary",), vmem_limit_bytes=VMEM_LIMIT),
    )(*[a for a, _ in ins])


def _sds(shape, dtype):
    return jax.ShapeDtypeStruct(tuple(shape), dtype)


def _rope_consts():
    c = np.zeros((16, LANES), np.float32)
    lane = np.arange(LANES)
    inv_a = (ROPE_THETA ** (-(np.arange(0, A_HEAD_DIM, 2, dtype=np.float32) / A_HEAD_DIM))).astype(np.float32)
    in_a = lane < A_HEAD_DIM
    c[0, in_a] = inv_a[lane[in_a] % (A_HEAD_DIM // 2)]
    c[1, in_a] = 1.0
    c[2, lane < A_HEAD_DIM // 2] = -1.0
    c[3, (lane >= A_HEAD_DIM // 2) & in_a] = 1.0
    inv_b = (ROPE_THETA ** (-(np.arange(0, ROPE_DIM, 2, dtype=np.float32) / ROPE_DIM))).astype(np.float32)
    pe = (lane >= NOPE_DIM) & (lane < NOPE_DIM + ROPE_DIM)
    c[5, pe] = inv_b[(lane[pe] - NOPE_DIM) % (ROPE_DIM // 2)]
    c[6, pe] = 1.0
    c[7, (lane >= NOPE_DIM) & (lane < NOPE_DIM + ROPE_DIM // 2)] = -1.0
    c[8, (lane >= NOPE_DIM + ROPE_DIM // 2) & (lane < NOPE_DIM + ROPE_DIM)] = 1.0
    c[9, lane < NOPE_DIM] = 1.0
    c[10, pe] = 1.0
    return jnp.asarray(c)


def _rope_tables(pos_f, consts, tm):
    t_rows = pos_f.shape[0]

    def body(pos_ref, c_ref, ca, sa1, sa2, cb, sb1, sb2):
        pos = pos_ref[...]
        ang = pos * c_ref[0:1, :]
        cs, sn = jnp.cos(ang), jnp.sin(ang)
        ca[...] = cs * c_ref[1:2, :]
        sa1[...] = sn * c_ref[2:3, :]
        sa2[...] = sn * c_ref[3:4, :]
        ang = pos * c_ref[5:6, :]
        cs, sn = jnp.cos(ang), jnp.sin(ang)
        cb[...] = cs * c_ref[6:7, :] + c_ref[9:10, :]
        sb1[...] = sn * c_ref[7:8, :]
        sb2[...] = sn * c_ref[8:9, :]

    tab = (_sds((t_rows, LANES), F32), _row(tm, LANES))
    return _rows_call("rope_tables", body, t_rows, tm,
                      [(pos_f, _row(tm, 1)), (consts, _full(consts.shape))], [tab] * 6)


def _fwd_in(x, g1, win, bg, gq, gkv, wuq, wk, wv, tabs, tm):
    t_rows = x.shape[0]

    def body(x_ref, g1_ref, win_ref, bg_ref, gq_ref, gkv_ref, wuq_ref, wk_ref, wv_ref,
             ca, sa1, sa2, cb, sb1, sb2,
             h1_ref, qs_ref, ks_ref, vs_ref, cq_ref, cqn_ref, ckv_ref, ckvn_ref, qm_ref, km_ref, vm_ref, gate_ref):
        xn, _ = _rms_stats(x_ref[...])
        hb = (xn * g1_ref[...]).astype(BF16)
        h1_ref[...] = hb
        ta = (ca[...], sa1[...], sa2[...])
        tb = (cb[...], sb1[...], sb2[...])
        qs_ref[...] = (_rope(_dot(hb, win_ref[:, Z_QA:Z_KA]), *ta, A_HEAD_DIM // 2) * SCALE_A).astype(BF16)
        ks_ref[...] = _rope(_dot(hb, win_ref[:, Z_KA:Z_VA]), *ta, A_HEAD_DIM // 2).astype(BF16)
        vs_ref[...] = _dot(hb, win_ref[:, Z_VA:Z_CQ]).astype(BF16)
        cq = _dot(hb, win_ref[:, Z_CQ:Z_CKV])
        cq_ref[...] = cq
        cqn, _ = _rms_stats(cq)
        cqb = (cqn * gq_ref[...]).astype(BF16)
        cqn_ref[...] = cqb
        qm_ref[...] = (_rope(_dot(cqb, wuq_ref[...]), *tb, ROPE_DIM // 2) * SCALE_B).astype(BF16)
        ckv = _dot(hb, win_ref[:, Z_CKV:Z_KR])
        ckv_ref[...] = ckv
        ckvn, _ = _rms_stats(ckv)
        ckvb = (ckvn * gkv_ref[...]).astype(BF16)
        ckvn_ref[...] = ckvb
        kpe = _rope(_dot(hb, win_ref[:, Z_KR:Z_GATE]), *tb, ROPE_DIM // 2)
        km_ref[...] = (_dot(ckvb, wk_ref[...]) + _tile_lanes(kpe, HEADS)).astype(BF16)
        vm_ref[...] = _dot(ckvb, wv_ref[...]).astype(BF16)
        gate_ref[...] = _sigmoid(_dot(hb, win_ref[:, Z_GATE:ZW]) + bg_ref[...])

    def o(n, dt):
        return (_sds((t_rows, n), dt), _row(tm, n))

    ins = [(x, _row(tm, D_MODEL)), (g1, _full(g1.shape)), (win, _resident(win.shape)), (bg, _full(bg.shape)),
           (gq, _full(gq.shape)), (gkv, _full(gkv.shape)), (wuq, _full(wuq.shape)), (wk, _full(wk.shape)),
           (wv, _full(wv.shape))] + [(t, _row(tm, LANES)) for t in tabs]
    outs = [o(1024, BF16), o(1024, BF16), o(256, BF16), o(256, BF16), o(256, F32), o(256, BF16), o(128, F32),
            o(128, BF16), o(1024, BF16), o(1024, BF16), o(1024, BF16), o(2048, F32)]
    return _rows_call("fwd_in", body, t_rows, tm, ins, outs)


def _attn_tile(t_rows):
    return min(512, t_rows)


MLA_HEADS_PER_STEP = 2


def _causal_pairs(nq, by_kv):
    if by_kv:
        pairs = [(i, j) for j in range(nq) for i in range(j, nq)]
    else:
        pairs = [(i, j) for i in range(nq) for j in range(i + 1)]
    return (jnp.asarray([p[0] for p in pairs], jnp.int32), jnp.asarray([p[1] for p in pairs], jnp.int32))


def _mla_fwd(q, k, v):
    t_rows = q.shape[0]
    t = _attn_tile(t_rows)
    hp = MLA_HEADS_PER_STEP
    w = hp * LANES
    ii, jj = _causal_pairs(t_rows // t, by_kv=False)

    def body(i_ref, j_ref, q_ref, k_ref, v_ref, o_ref, lse_ref, m_s, l_s, acc_s):
        i = i_ref[pl.program_id(1)]
        j = j_ref[pl.program_id(1)]

        @pl.when(j == 0)
        def _():
            m_s[...] = jnp.full(m_s.shape, NEG, F32)
            l_s[...] = jnp.zeros(l_s.shape, F32)
            acc_s[...] = jnp.zeros(acc_s.shape, F32)

        def step(diagonal):
            for hh in range(hp):
                sl = slice(hh * LANES, (hh + 1) * LANES)
                s = _dot_nt(k_ref[:, sl], q_ref[:, sl])
                if diagonal:
                    valid = (lax.broadcasted_iota(jnp.int32, (t, t), 0) <= lax.broadcasted_iota(jnp.int32, (t, t), 1))
                    s = jnp.where(valid, s, NEG)
                m_prev = m_s[hh]
                m_new = jnp.maximum(m_prev, jnp.max(s, axis=0, keepdims=True))
                p = jnp.exp(s - m_new)
                alpha = jnp.exp(m_prev - m_new)
                l_new = alpha * l_s[hh] + jnp.sum(p, axis=0, keepdims=True)
                acc = alpha * acc_s[hh] + _dot_tn(v_ref[:, sl], p.astype(BF16))
                if diagonal:
                    o_ref[:, sl] = (acc / l_new).T.astype(o_ref.dtype)
                    lse_ref[hh] = m_new + jnp.log(l_new)
                else:
                    m_s[hh] = m_new
                    l_s[hh] = l_new
                    acc_s[hh] = acc

        pl.when(j < i)(lambda: step(False))
        pl.when(j == i)(lambda: step(True))

    grid_spec = pltpu.PrefetchScalarGridSpec(
        num_scalar_prefetch=2, grid=(HEADS // hp, ii.shape[0]),
        in_specs=[pl.BlockSpec((t, w), lambda hb, s, ir, jr: (ir[s], hb)),
                  pl.BlockSpec((t, w), lambda hb, s, ir, jr: (jr[s], hb)),
                  pl.BlockSpec((t, w), lambda hb, s, ir, jr: (jr[s], hb))],
        out_specs=[pl.BlockSpec((t, w), lambda hb, s, ir, jr: (ir[s], hb)),
                   pl.BlockSpec((hp, 1, t), lambda hb, s, ir, jr: (hb, 0, ir[s]))],
        scratch_shapes=[pltpu.VMEM((hp, 1, t), F32), pltpu.VMEM((hp, 1, t), F32), pltpu.VMEM((hp, LANES, t), F32)])
    return pl.pallas_call(
        body, name="mla_fwd", grid_spec=grid_spec,
        out_shape=[_sds((t_rows, HEADS * LANES), BF16), _sds((HEADS, 1, t_rows), F32)],
        compiler_params=pltpu.CompilerParams(dimension_semantics=("arbitrary",) * 2, vmem_limit_bytes=VMEM_LIMIT),
    )(ii, jj, q, k, v)


def _mla_bwd(q, k, v, do, lse, delta):
    t_rows = q.shape[0]
    t = _attn_tile(t_rows)
    hp = MLA_HEADS_PER_STEP
    w = hp * LANES
    ii, jj = _causal_pairs(t_rows // t, by_kv=True)

    def body(i_ref, j_ref, q_ref, k_ref, v_ref, do_ref, lse_ref, dl_ref, dq_ref, dk_ref, dv_ref):
        i = i_ref[pl.program_id(1)]
        j = j_ref[pl.program_id(1)]

        @pl.when(pl.program_id(1) == 0)
        def _():
            dq_ref[...] = jnp.zeros(dq_ref.shape, F32)

        def step(diagonal):
            r0 = pl.multiple_of(i * t, t)
            for hh in range(hp):
                sl = slice(hh * LANES, (hh + 1) * LANES)
                qv = q_ref[:, sl]
                kv = k_ref[:, sl]
                dov = do_ref[:, sl]
                s = _dot_nt(kv, qv)
                if diagonal:
                    valid = (lax.broadcasted_iota(jnp.int32, (t, t), 0) <= lax.broadcasted_iota(jnp.int32, (t, t), 1))
                    s = jnp.where(valid, s, NEG)
                p = jnp.exp(s - lse_ref[hh])
                dv = _dot(p.astype(BF16), dov)
                dp = _dot_nt(v_ref[:, sl], dov)
                ds = (p * (dp - dl_ref[hh])).astype(BF16)
                dk = _dot(ds, qv)
                if diagonal:
                    dv_ref[:, sl] = dv
                    dk_ref[:, sl] = dk
                else:
                    dv_ref[:, sl] += dv
                    dk_ref[:, sl] += dk
                dq_ref[hh, pl.ds(r0, t), :] += _dot_tn(ds, kv)

        pl.when(i > j)(lambda: step(False))
        pl.when(i == j)(lambda: step(True))

    def qmap(hb, s, ir, jr):
        return (ir[s], hb)

    def kvmap(hb, s, ir, jr):
        return (jr[s], hb)

    def rowmap(hb, s, ir, jr):
        return (hb, 0, ir[s])

    grid_spec = pltpu.PrefetchScalarGridSpec(
        num_scalar_prefetch=2, grid=(HEADS // hp, ii.shape[0]),
        in_specs=[pl.BlockSpec((t, w), qmap), pl.BlockSpec((t, w), kvmap), pl.BlockSpec((t, w), kvmap),
                  pl.BlockSpec((t, w), qmap), pl.BlockSpec((hp, 1, t), rowmap), pl.BlockSpec((hp, 1, t), rowmap)],
        out_specs=[pl.BlockSpec((hp, t_rows, LANES), lambda hb, s, ir, jr: (hb, 0, 0)),
                   pl.BlockSpec((t, w), kvmap), pl.BlockSpec((t, w), kvmap)])
    return pl.pallas_call(
        body, name="mla_bwd", grid_spec=grid_spec,
        out_shape=[_sds((HEADS, t_rows, LANES), F32), _sds((t_rows, HEADS * LANES), F32),
                   _sds((t_rows, HEADS * LANES), F32)],
        compiler_params=pltpu.CompilerParams(dimension_semantics=("arbitrary",) * 2, vmem_limit_bytes=VMEM_LIMIT),
    )(ii, jj, q, k, v, do, lse, delta)


SWA_TILE = 2 * SWA_WINDOW
SWA_GROUP = HEADS // A_KV_HEADS


def _swa_bias(tq):
    koff = lax.broadcasted_iota(jnp.int32, (tq + SWA_WINDOW, SWA_GROUP * tq), 0) - SWA_WINDOW
    qoff = (lax.broadcasted_iota(jnp.int32, (tq + SWA_WINDOW, SWA_GROUP * tq), 1) % tq)
    band = (koff <= qoff) & (qoff - koff < SWA_WINDOW)
    return jnp.stack([jnp.where(band & (koff >= 0), 0.0, NEG), jnp.where(band, 0.0, NEG)]).astype(F32)


def _swa_specs(tq, nq):
    wb = tq // SWA_WINDOW

    def qi(i):
        return jnp.minimum(i, nq - 1)

    q = pl.BlockSpec((tq, SWA_GROUP * LANES), lambda h, i: (qi(i), h))
    cur = pl.BlockSpec((tq, LANES), lambda h, i: (qi(i), h))
    prev = pl.BlockSpec((SWA_WINDOW, LANES), lambda h, i: (jnp.maximum(qi(i) * wb - 1, 0), h))
    bias = pl.BlockSpec((1, tq + SWA_WINDOW, SWA_GROUP * tq), lambda h, i: (jnp.minimum(i, 1), 0, 0))
    rows = pl.BlockSpec((1, 1, 1, SWA_GROUP * tq), lambda h, i: (h, qi(i), 0, 0))
    sink = pl.BlockSpec((1, 1, SWA_GROUP * tq), lambda h, i: (h, 0, 0))
    return q, cur, prev, bias, rows, sink


def _stack_heads(ref):
    return jnp.concatenate([ref[:, g * LANES:(g + 1) * LANES] for g in range(SWA_GROUP)], axis=0)


def _swa_fwd(q, k, v, bias, sink_rows):
    t_rows = q.shape[0]
    tq = min(SWA_TILE, t_rows)
    nq = t_rows // tq
    qs_, cur, prev, bs, rows, sk = _swa_specs(tq, nq)

    def body(q_ref, kc_ref, kp_ref, vc_ref, vp_ref, b_ref, sink_ref, o_ref, lse_ref):
        qs = _stack_heads(q_ref)
        kk = jnp.concatenate([kp_ref[...], kc_ref[...]], axis=0)
        vv = jnp.concatenate([vp_ref[...], vc_ref[...]], axis=0)
        s = _dot_nt(kk, qs) + b_ref[0]
        sink = sink_ref[0]
        m = jnp.maximum(jnp.max(s, axis=0, keepdims=True), sink)
        p = jnp.exp(s - m)
        l = jnp.sum(p, axis=0, keepdims=True) + jnp.exp(sink - m)
        o = (_dot_tn(vv, p.astype(BF16)) / l).T
        for g in range(SWA_GROUP):
            o_ref[:, g * LANES:(g + 1) * LANES] = o[g * tq:(g + 1) * tq].astype(o_ref.dtype)
        lse_ref[0, 0] = m + jnp.log(l)

    return pl.pallas_call(
        body, name="swa_fwd", grid=(A_KV_HEADS, nq),
        in_specs=[qs_, cur, prev, cur, prev, bs, sk],
        out_specs=[qs_, rows],
        out_shape=[_sds((t_rows, HEADS * LANES), BF16), _sds((A_KV_HEADS, nq, 1, SWA_GROUP * tq), F32)],
        compiler_params=pltpu.CompilerParams(dimension_semantics=("arbitrary",) * 2, vmem_limit_bytes=VMEM_LIMIT),
    )(q, k, k, v, v, bias, sink_rows)


def _swa_bwd(q, k, v, o, do, lse, bias, sink_rows):
    t_rows = q.shape[0]
    tq = min(SWA_TILE, t_rows)
    nq = t_rows // tq
    qs_, cur, prev, bs, rows, sk = _swa_specs(tq, nq)
    hw = SWA_WINDOW

    def body(q_ref, kc_ref, kp_ref, vc_ref, vp_ref, o_ref, do_ref, lse_ref, b_ref, sink_ref,
             dq_ref, dk_ref, dv_ref, dsink_ref, ck, cv, dsa):
        i = pl.program_id(1)

        @pl.when(i == 0)
        def _():
            dsa[...] = jnp.zeros(dsa.shape, F32)

        @pl.when(i < nq)
        def _():
            qs = _stack_heads(q_ref)
            dos = _stack_heads(do_ref)
            kk = jnp.concatenate([kp_ref[...], kc_ref[...]], axis=0)
            vv = jnp.concatenate([vp_ref[...], vc_ref[...]], axis=0)
            lse = lse_ref[0, 0]
            p = jnp.exp(_dot_nt(kk, qs) + b_ref[0] - lse)
            delta = jnp.sum((_stack_heads(o_ref).astype(F32) * dos.astype(F32)).T, axis=0, keepdims=True)
            dsa[...] += -jnp.exp(sink_ref[0] - lse) * delta
            dv = _dot(p.astype(BF16), dos)
            ds = (p * (_dot_nt(vv, dos) - delta)).astype(BF16)
            dk = _dot(ds, qs)
            dq = _dot_tn(ds, kk)
            for g in range(SWA_GROUP):
                dq_ref[:, g * LANES:(g + 1) * LANES] = dq[g * tq:(g + 1) * tq]

            @pl.when(i > 0)
            def _():
                dk_ref[0:tq - hw, :] = ck[0:tq - hw, :]
                dk_ref[tq - hw:tq, :] = ck[tq - hw:tq, :] + dk[0:hw]
                dv_ref[0:tq - hw, :] = cv[0:tq - hw, :]
                dv_ref[tq - hw:tq, :] = cv[tq - hw:tq, :] + dv[0:hw]

            ck[...] = dk[hw:hw + tq]
            cv[...] = dv[hw:hw + tq]

        @pl.when(i == nq)
        def _():
            dk_ref[...] = ck[...]
            dv_ref[...] = cv[...]
            dsink_ref[...] = jnp.zeros(dsink_ref.shape, F32)
            for g in range(SWA_GROUP):
                tot = jnp.sum(dsa[:, g * tq:(g + 1) * tq], axis=1, keepdims=True)
                dsink_ref[0, g:g + 1, :] = jnp.zeros((1, LANES), F32) + tot

    kv_out = pl.BlockSpec((tq, LANES), lambda h, i: (jnp.maximum(i - 1, 0), h))
    return pl.pallas_call(
        body, name="swa_bwd", grid=(A_KV_HEADS, nq + 1),
        in_specs=[qs_, cur, prev, cur, prev, qs_, qs_, rows, bs, sk],
        out_specs=[qs_, kv_out, kv_out, pl.BlockSpec((1, 8, LANES), lambda h, i: (h, 0, 0))],
        out_shape=[_sds((t_rows, HEADS * LANES), F32), _sds((t_rows, A_KV_HEADS * LANES), F32),
                   _sds((t_rows, A_KV_HEADS * LANES), F32), _sds((A_KV_HEADS, 8, LANES), F32)],
        scratch_shapes=[pltpu.VMEM((tq, LANES), F32), pltpu.VMEM((tq, LANES), F32),
                        pltpu.VMEM((1, SWA_GROUP * tq), F32)],
        compiler_params=pltpu.CompilerParams(dimension_semantics=("arbitrary",) * 2, vmem_limit_bytes=VMEM_LIMIT),
    )(q, k, k, v, v, o, do, lse, bias, sink_rows)


def _fwd_mix(x, ya, yb, gate, wba, wbb, wout, g2, g3, tm):
    t_rows = x.shape[0]

    def body(x_ref, ya_ref, yb_ref, gate_ref, wba_ref, wbb_ref, wout_ref, g2_ref, g3_ref,
             pa_ref, pb_ref, mixed_ref, o_ref, x1_ref, h2_ref):
        pa = _dot(ya_ref[...], wba_ref[...])
        pb = _dot(yb_ref[...], wbb_ref[...])
        pa_ref[...] = pa
        pb_ref[...] = pb
        mixed = (gate_ref[:, 0:D_MODEL] * pa + gate_ref[:, D_MODEL:2 * D_MODEL] * pb).astype(BF16)
        mixed_ref[...] = mixed
        o = _dot(mixed, wout_ref[...])
        o_ref[...] = o
        on, _ = _rms_stats(o)
        x1 = x_ref[...] + on * g2_ref[...]
        x1_ref[...] = x1
        x1n, _ = _rms_stats(x1)
        h2_ref[...] = (x1n * g3_ref[...]).astype(BF16)

    def o_(dt):
        return (_sds((t_rows, D_MODEL), dt), _row(tm, D_MODEL))

    ins = [(x, _row(tm, D_MODEL)), (ya, _row(tm, 1024)), (yb, _row(tm, 1024)), (gate, _row(tm, 2048)),
           (wba, _resident(wba.shape)), (wbb, _resident(wbb.shape)), (wout, _resident(wout.shape)),
           (g2, _full(g2.shape)), (g3, _full(g3.shape))]
    return _rows_call("fwd_mix", body, t_rows, tm, ins, [o_(F32), o_(F32), o_(BF16), o_(F32), o_(F32), o_(BF16)])


CONV_CHUNK = 1408


def _fwd_up(h2, wup, convw8, convb, tm):
    t_rows = h2.shape[0]
    cdim = 2 * D_FF

    def body(h2_ref, wup_ref, cw_ref, cb_ref, up_ref, a_ref, carry):
        i = pl.program_id(0)

        @pl.when(i == 0)
        def _():
            carry[...] = jnp.zeros(carry.shape, F32)

        hb = h2_ref[...]

        def conv(c0):
            sl = slice(c0, c0 + CONV_CHUNK)
            up = _dot(hb, wup_ref[c0 // CONV_CHUNK])
            up_ref[:, sl] = up
            xm1, xm2 = _conv_taps(up, carry[6:7, sl], carry[7:8, sl])
            u = cw_ref[0:1, sl] * xm2 + cw_ref[1:2, sl] * xm1 + cw_ref[2:3, sl] * up + cb_ref[:, sl]
            carry[:, sl] = up[tm - 8:tm, :]
            return u

        for c0 in range(0, D_FF, CONV_CHUNK):
            ug = conv(c0)
            uv = conv(D_FF + c0)
            gel, _ = _gelu_and_grad(ug)
            a_ref[:, c0:c0 + CONV_CHUNK] = (gel * uv).astype(BF16)

    ins = [(h2, _row(tm, D_MODEL)), (wup, _resident(wup.shape)), (convw8, _full(convw8.shape)), (convb, _full(convb.shape))]
    outs = [(_sds((t_rows, cdim), F32), _row(tm, cdim)), (_sds((t_rows, D_FF), BF16), _row(tm, D_FF))]
    return _rows_call("fwd_up", body, t_rows, tm, ins, outs, scratch=[pltpu.VMEM((8, cdim), F32)])


def _fwd_out(a, wdown, x1, g4, p, wple, g5, wpg, tgt, tm):
    t_rows = a.shape[0]

    def body(a_ref, wdown_ref, x1_ref, g4_ref, p_ref, wple_ref, g5_ref, wpg_ref, tgt_ref,
             ff_ref, x2_ref, e_ref, n5_ref, sg_ref, dx3_ref, loss_ref):
        i = pl.program_id(0)
        ff = _dot(a_ref[...], wdown_ref[...])
        ff_ref[...] = ff
        ffn, _ = _rms_stats(ff)
        x2 = x1_ref[...] + ffn * g4_ref[...]
        x2_ref[...] = x2
        e = _dot(p_ref[...].astype(BF16), wple_ref[...])
        e_ref[...] = e
        x2n, _ = _rms_stats(x2)
        n5 = (x2n * g5_ref[...]).astype(BF16)
        n5_ref[...] = n5
        sg = _sigmoid(_dot(n5, wpg_ref[...]))
        sg_ref[...] = sg
        d = x2 + sg * e - tgt_ref[...]
        dx3_ref[...] = d * (1.0 / D_MODEL)

        @pl.when(i == 0)
        def _():
            loss_ref[...] = jnp.zeros((1, 1), F32)

        loss_ref[...] += 0.5 * jnp.sum(jnp.sum(d * d, axis=1, keepdims=True), axis=0, keepdims=True) * (1.0 / D_MODEL)

    def o_(dt):
        return (_sds((t_rows, D_MODEL), dt), _row(tm, D_MODEL))

    ins = [(a, _row(tm, D_FF)), (wdown, _resident(wdown.shape)), (x1, _row(tm, D_MODEL)), (g4, _full(g4.shape)),
           (p, _row(tm, PLE_DIM)), (wple, _full(wple.shape)), (g5, _full(g5.shape)), (wpg, _resident(wpg.shape)),
           (tgt, _row(tm, D_MODEL))]
    outs = [o_(F32), o_(F32), o_(F32), o_(BF16), o_(F32), o_(F32), (_sds((1, 1), F32), _full((1, 1)))]
    return _rows_call("fwd_out", body, t_rows, tm, ins, outs)


def _bwd_out(dx3, e, sg, x2, ff, g5, g4, wpg, wdown, up, convw8, convb, tm):
    t_rows = dx3.shape[0]
    cdim = 2 * D_FF
    hb = tm // 8

    def body(dx3_ref, e_ref, sg_ref, x2_ref, ff_ref, g5_ref, g4_ref, wpg_ref, wdown_ref, up_ref, halo_ref, cw_ref,
             cb_ref, dpre_ref, de_ref, dx2_ref, dff_ref, du_ref, dg5_ref, dg4_ref, dcb_ref, dcw_ref):
        i = pl.program_id(0)

        @pl.when(i == 0)
        def _():
            dg5_ref[...] = jnp.zeros(dg5_ref.shape, F32)
            dg4_ref[...] = jnp.zeros(dg4_ref.shape, F32)
            dcb_ref[...] = jnp.zeros(dcb_ref.shape, F32)
            dcw_ref[...] = jnp.zeros(dcw_ref.shape, F32)

        dx3 = dx3_ref[...]
        sg = sg_ref[...]
        dpre = (dx3 * e_ref[...] * sg * (1.0 - sg)).astype(BF16)
        dpre_ref[...] = dpre
        de_ref[...] = (dx3 * sg).astype(BF16)
        dn5 = _dot_nt(dpre, wpg_ref[...])
        x2n, r5 = _rms_stats(x2_ref[...])
        d2, dg5 = _rms_bwd(dn5, x2n, r5, g5_ref[...])
        dx2 = dx3 + d2
        dx2_ref[...] = dx2
        dg5_ref[...] += dg5
        ffn, r4 = _rms_stats(ff_ref[...])
        dff, dg4 = _rms_bwd(dx2, ffn, r4, g4_ref[...])
        dg4_ref[...] += dg4
        dffb = dff.astype(BF16)
        dff_ref[...] = dffb
        keep = jnp.where(i > 0, 1.0, 0.0)

        def conv(c0):
            sl = slice(c0, c0 + CONV_CHUNK)
            up = up_ref[:, sl]
            xm1, xm2 = _conv_taps(up, halo_ref[6:7, sl] * keep, halo_ref[7:8, sl] * keep)
            u = cw_ref[0:1, sl] * xm2 + cw_ref[1:2, sl] * xm1 + cw_ref[2:3, sl] * up + cb_ref[:, sl]
            return u, up, xm1, xm2

        def grads(c0, du, up, xm1, xm2):
            sl = slice(c0, c0 + CONV_CHUNK)
            du_ref[:, sl] = du.astype(BF16)
            dcb_ref[:, sl] += jnp.sum(du, axis=0, keepdims=True)
            dcw_ref[0:1, sl] += jnp.sum(du * xm2, axis=0, keepdims=True)
            dcw_ref[1:2, sl] += jnp.sum(du * xm1, axis=0, keepdims=True)
            dcw_ref[2:3, sl] += jnp.sum(du * up, axis=0, keepdims=True)

        for c0 in range(0, D_FF, CONV_CHUNK):
            da = _dot_nt(dffb, wdown_ref[c0:c0 + CONV_CHUNK, :])
            ug, *rg = conv(c0)
            uv, *rv = conv(D_FF + c0)
            gel, dgel = _gelu_and_grad(ug)
            grads(c0, da * uv * dgel, *rg)
            grads(D_FF + c0, da * gel, *rv)

    def o_(n, dt):
        return (_sds((t_rows, n), dt), _row(tm, n))

    def acc(r, n):
        return (_sds((r, n), F32), _full((r, n)))

    halo = pl.BlockSpec((8, cdim), lambda i: (jnp.maximum(i * hb - 1, 0), 0))
    ins = [(dx3, _row(tm, D_MODEL)), (e, _row(tm, D_MODEL)), (sg, _row(tm, D_MODEL)), (x2, _row(tm, D_MODEL)),
           (ff, _row(tm, D_MODEL)), (g5, _full(g5.shape)), (g4, _full(g4.shape)), (wpg, _resident(wpg.shape)),
           (wdown, _resident(wdown.shape)), (up, _row(tm, cdim)), (up, halo), (convw8, _full(convw8.shape)),
           (convb, _full(convb.shape))]
    outs = [o_(D_MODEL, BF16), o_(D_MODEL, BF16), o_(D_MODEL, F32), o_(D_MODEL, BF16), o_(cdim, BF16),
            acc(1, D_MODEL), acc(1, D_MODEL), acc(1, cdim), acc(8, cdim)]
    return _rows_call("bwd_out", body, t_rows, tm, ins, outs)


def _bwd_mid(du, convw8, wup, dx2, x1, g3, o, g2, wout, gate, pa, pb, wba, wbb, yb, tm):
    t_rows = du.shape[0]
    cdim = 2 * D_FF
    halo_rows = 16
    hb = tm // halo_rows
    last_blk = t_rows // halo_rows - 1
    n_tiles = t_rows // tm

    def body(du_ref, halo_ref, cw_ref, wup_ref, dx2_ref, x1_ref, g3_ref, o_ref, g2_ref, wout_ref, gate_ref, pa_ref,
             pb_ref, wba_ref, wbb_ref, yb_ref,
             dup_ref, dx1_ref, do_ref, dpa_ref, dpb_ref, dgt_ref, dya_ref, dyb_ref, dl_ref, dg3_ref, dg2_ref, dbg_ref):
        i = pl.program_id(0)

        @pl.when(i == 0)
        def _():
            dg3_ref[...] = jnp.zeros(dg3_ref.shape, F32)
            dg2_ref[...] = jnp.zeros(dg2_ref.shape, F32)
            dbg_ref[...] = jnp.zeros(dbg_ref.shape, F32)

        keep = jnp.where(i < n_tiles - 1, 1.0, 0.0)
        dh2 = jnp.zeros((tm, D_MODEL), F32)
        for c0 in range(0, cdim, CONV_CHUNK):
            sl = slice(c0, c0 + CONV_CHUNK)
            du = du_ref[:, sl].astype(F32)
            nxt = halo_ref[:, sl].astype(F32)
            xp1, xp2 = _conv_taps_next(du, nxt[0:1] * keep, nxt[1:2] * keep)
            dup = (cw_ref[2:3, sl] * du + cw_ref[1:2, sl] * xp1 + cw_ref[0:1, sl] * xp2).astype(BF16)
            dup_ref[:, sl] = dup
            dh2 = dh2 + _dot_nt(dup, wup_ref[c0 // CONV_CHUNK])
        x1n, r3 = _rms_stats(x1_ref[...])
        d1, dg3 = _rms_bwd(dh2, x1n, r3, g3_ref[...])
        dx1 = dx2_ref[...] + d1
        dx1_ref[...] = dx1
        dg3_ref[...] += dg3
        on, r2 = _rms_stats(o_ref[...])
        do, dg2 = _rms_bwd(dx1, on, r2, g2_ref[...])
        dg2_ref[...] += dg2
        dob = do.astype(BF16)
        do_ref[...] = dob
        dmixed = _dot_nt(dob, wout_ref[...])
        ga = gate_ref[:, 0:D_MODEL]
        gb = gate_ref[:, D_MODEL:2 * D_MODEL]
        dpa = (dmixed * ga).astype(BF16)
        dpb = (dmixed * gb).astype(BF16)
        dpa_ref[...] = dpa
        dpb_ref[...] = dpb
        dga = dmixed * pa_ref[...] * ga * (1.0 - ga)
        dgb = dmixed * pb_ref[...] * gb * (1.0 - gb)
        dgt_ref[:, 0:D_MODEL] = dga.astype(BF16)
        dgt_ref[:, D_MODEL:2 * D_MODEL] = dgb.astype(BF16)
        dbg_ref[:, 0:D_MODEL] += jnp.sum(dga, axis=0, keepdims=True)
        dbg_ref[:, D_MODEL:2 * D_MODEL] += jnp.sum(dgb, axis=0, keepdims=True)
        dya_ref[...] = _dot_nt(dpa, wba_ref[...]).astype(BF16)
        dyb = _dot_nt(dpb, wbb_ref[...]).astype(BF16)
        dyb_ref[...] = dyb
        prod = yb_ref[...].astype(F32) * dyb.astype(F32)
        lane_head = lax.broadcasted_iota(jnp.int32, (HEADS, HEADS * LANES), 1) // LANES
        sel = (lane_head == lax.broadcasted_iota(jnp.int32, (HEADS, HEADS * LANES), 0)).astype(F32)
        dl_ref[...] = lax.dot_general(sel, prod, (((1,), (1,)), ((), ())), precision=lax.Precision.HIGHEST,
                                      preferred_element_type=F32)

    def o_(n, dt):
        return (_sds((t_rows, n), dt), _row(tm, n))

    def acc(r, n):
        return (_sds((r, n), F32), _full((r, n)))

    halo = pl.BlockSpec((halo_rows, cdim), lambda i: (jnp.minimum((i + 1) * hb, last_blk), 0))
    ins = [(du, _row(tm, cdim)), (du, halo), (convw8, _full(convw8.shape)), (wup, _resident(wup.shape)),
           (dx2, _row(tm, D_MODEL)), (x1, _row(tm, D_MODEL)), (g3, _full(g3.shape)), (o, _row(tm, D_MODEL)),
           (g2, _full(g2.shape)), (wout, _resident(wout.shape)), (gate, _row(tm, 2048)), (pa, _row(tm, D_MODEL)),
           (pb, _row(tm, D_MODEL)), (wba, _resident(wba.shape)), (wbb, _resident(wbb.shape)), (yb, _row(tm, 1024))]
    outs = [o_(cdim, BF16), o_(D_MODEL, F32), o_(D_MODEL, BF16), o_(D_MODEL, BF16), o_(D_MODEL, BF16),
            o_(2048, BF16), o_(1024, BF16), o_(1024, BF16),
            (_sds((HEADS, t_rows), F32), pl.BlockSpec((HEADS, tm), lambda i: (0, i))),
            acc(1, D_MODEL), acc(1, D_MODEL), acc(1, 2048)]
    return _rows_call("bwd_mid", body, t_rows, tm, ins, outs)


def _bwd_in(dqs, dks, dvs, dqm, dkm, dvm, tabs, consts, cq, ckv, gq, gkv, wuq, wk, wv, dgates, win, x, g1, dx1, tm):
    t_rows = x.shape[0]

    def body(dqs_ref, dks_ref, dvs_ref, dqm_ref, dkm_ref, dvm_ref, ca, sa1, sa2, cb, sb1, sb2, c_ref, cq_ref,
             ckv_ref, gq_ref, gkv_ref, wuq_ref, wk_ref, wv_ref, dgt_ref, win_ref, x_ref, g1_ref, dx1_ref,
             dz_ref, dqb_ref, dx_ref, dgq_ref, dgkv_ref, dg1_ref):
        i = pl.program_id(0)

        @pl.when(i == 0)
        def _():
            dgq_ref[...] = jnp.zeros(dgq_ref.shape, F32)
            dgkv_ref[...] = jnp.zeros(dgkv_ref.shape, F32)
            dg1_ref[...] = jnp.zeros(dg1_ref.shape, F32)

        ta = (ca[...], sa1[...], sa2[...])
        tb = (cb[...], sb1[...], sb2[...])
        dz_ref[:, Z_QA:Z_KA] = _rope_t(dqs_ref[...] * SCALE_A, *ta, A_HEAD_DIM // 2).astype(BF16)
        dz_ref[:, Z_KA:Z_VA] = _rope_t(dks_ref[...], *ta, A_HEAD_DIM // 2).astype(BF16)
        dz_ref[:, Z_VA:Z_CQ] = dvs_ref[...].astype(BF16)
        dqm = jnp.concatenate([dqm_ref[h] for h in range(HEADS)], axis=1)
        dqb = _rope_t(dqm * SCALE_B, *tb, ROPE_DIM // 2).astype(BF16)
        dqb_ref[...] = dqb
        dcqn = _dot_nt(dqb, wuq_ref[...])
        cqn, rq = _rms_stats(cq_ref[...])
        dcq, dgq = _rms_bwd(dcqn, cqn, rq, gq_ref[...])
        dgq_ref[...] += dgq
        dz_ref[:, Z_CQ:Z_CKV] = dcq.astype(BF16)
        dkm = dkm_ref[...]
        dslot = dkm[:, 0:LANES]
        for h in range(1, HEADS):
            dslot = dslot + dkm[:, h * LANES:(h + 1) * LANES]
        dz_ref[:, Z_KR:Z_GATE] = _rope_t(dslot * c_ref[10:11, :], *tb, ROPE_DIM // 2).astype(BF16)
        dckvn = _dot_nt(dkm.astype(BF16), wk_ref[...]) + _dot_nt(dvm_ref[...].astype(BF16), wv_ref[...])
        ckvn, rkv = _rms_stats(ckv_ref[...])
        dckv, dgkv = _rms_bwd(dckvn, ckvn, rkv, gkv_ref[...])
        dgkv_ref[...] += dgkv
        dz_ref[:, Z_CKV:Z_KR] = dckv.astype(BF16)
        dz_ref[:, Z_GATE:ZW] = dgt_ref[...]
        dh1 = _dot_nt(dz_ref[...], win_ref[...])
        xn, r1 = _rms_stats(x_ref[...])
        d0, dg1 = _rms_bwd(dh1, xn, r1, g1_ref[...])
        dg1_ref[...] += dg1
        dx_ref[...] = dx1_ref[...] + d0

    def acc(n):
        return (_sds((1, n), F32), _full((1, n)))

    ins = [(dqs, _row(tm, 1024)), (dks, _row(tm, 256)), (dvs, _row(tm, 256)), (dqm, _heads(tm, HEADS)),
           (dkm, _row(tm, 1024)), (dvm, _row(tm, 1024))] + [(t, _row(tm, LANES)) for t in tabs] + [
           (consts, _full(consts.shape)), (cq, _row(tm, 256)), (ckv, _row(tm, 128)), (gq, _full(gq.shape)),
           (gkv, _full(gkv.shape)), (wuq, _full(wuq.shape)), (wk, _full(wk.shape)), (wv, _full(wv.shape)),
           (dgates, _row(tm, 2048)), (win, _resident(win.shape)), (x, _row(tm, D_MODEL)), (g1, _full(g1.shape)),
           (dx1, _row(tm, D_MODEL))]
    outs = [(_sds((t_rows, ZW), BF16), _row(tm, ZW)), (_sds((t_rows, 1024), BF16), _row(tm, 1024)),
            (_sds((t_rows, D_MODEL), F32), _row(tm, D_MODEL)), acc(256), acc(128), acc(D_MODEL)]
    return _rows_call("bwd_in", body, t_rows, tm, ins, outs)


def _pick_cols(n):
    best = LANES
    for d in range(LANES, min(n, 1408) + 1, LANES):
        if n % d == 0:
            best = d
    return best


def _mm_tn(name, a, b, column_shards=1):
    t_rows, m = a.shape
    n = b.shape[1]
    bk = min(1024, t_rows)
    bm, bn = _pick_cols(m), _pick_cols(n // column_shards)
    per_shard = n // column_shards // bn

    def body(a_ref, b_ref, o_ref):
        @pl.when(pl.program_id(2) == 0)
        def _():
            o_ref[...] = jnp.zeros((bm, bn), F32)

        o_ref[...] += _dot_tn(a_ref[...].astype(BF16), b_ref[...].astype(BF16))

    return pl.pallas_call(
        body, name=name, grid=(m // bm, n // bn, t_rows // bk),
        in_specs=[pl.BlockSpec((bk, bm), lambda i, j, k: (k, i)), pl.BlockSpec((bk, bn), lambda i, j, k: (k, j))],
        out_specs=(pl.BlockSpec((bm, bn), lambda i, j, k: (i, j)) if column_shards == 1 else
                   pl.BlockSpec((None, bm, bn), lambda i, j, k: (j // per_shard, i, j % per_shard))),
        out_shape=_sds((m, n) if column_shards == 1 else (column_shards, m, n // column_shards), F32),
        compiler_params=pltpu.CompilerParams(dimension_semantics=("arbitrary",) * 3, vmem_limit_bytes=VMEM_LIMIT),
    )(a, b)


PACK_ROWS = 512


ADD_TILE_ELEMS = 1 << 17


def _add_rows(rows, cols):
    best = 16
    for d in range(16, rows + 1, 16):
        if rows % d == 0 and d * cols <= ADD_TILE_ELEMS:
            best = d
    assert rows % best == 0
    return best


def _add_pair(name, g, recv, half):
    _, _, rows, cols = g.shape
    t = _add_rows(rows, cols)

    def body(h_ref, g_ref, r_ref, o_ref):
        o_ref[...] = (g_ref[:, 0] + r_ref[...]).astype(BF16)

    spec = pl.BlockSpec((4, t, cols), lambda i, h: (0, i, 0))
    grid_spec = pltpu.PrefetchScalarGridSpec(
        num_scalar_prefetch=1, grid=(rows // t,),
        in_specs=[pl.BlockSpec((4, 1, t, cols), lambda i, h: (0, h[0], i, 0)), spec], out_specs=spec)
    return pl.pallas_call(body, name=name, grid_spec=grid_spec,
                          out_shape=_sds(recv.shape, BF16))(jnp.reshape(half, (1,)).astype(jnp.int32), g, recv)


def _add_chips(name, parts):
    _, rows, cols = parts.shape
    t = _add_rows(rows, cols)

    def body(p_ref, o_ref):
        acc = p_ref[0].astype(F32)
        for j in range(1, 4):
            acc = acc + p_ref[j].astype(F32)
        o_ref[...] = acc

    return pl.pallas_call(body, name=name, grid=(rows // t,),
                          in_specs=[pl.BlockSpec((4, t, cols), lambda i: (0, i, 0))],
                          out_specs=pl.BlockSpec((t, cols), lambda i: (i, 0)),
                          out_shape=_sds((rows, cols), F32))(parts)


def _add_devices(parts):
    n, rows, _ = parts.shape

    def body(p_ref, o_ref):
        acc = p_ref[0]
        for j in range(1, n):
            acc = acc + p_ref[j]
        o_ref[...] = acc

    return pl.pallas_call(body, name="small_add", grid=(1,),
                          in_specs=[pl.BlockSpec((n, rows, LANES), lambda i: (0, 0, 0))],
                          out_specs=pl.BlockSpec((rows, LANES), lambda i: (0, 0)),
                          out_shape=_sds((rows, LANES), F32))(parts)


def _adam_rows(k, n):
    target = max(8, (1 << 20) // (4 * n))
    if k <= target:
        return k
    best = None
    for d in range(8, target + 1, 8):
        if k % d == 0:
            best = d
    return best if best is not None else k


def _adamw(name, w, g, m, v):
    k, n = w.shape
    bk = _adam_rows(k, n)
    c1 = 1.0 - ADAM_B1 ** ADAM_STEP
    c2 = 1.0 - ADAM_B2 ** ADAM_STEP

    def body(w_ref, g_ref, m_ref, v_ref, d_ref, mo_ref, vo_ref):
        g_ = g_ref[...]
        m_ = ADAM_B1 * m_ref[...] + (1.0 - ADAM_B1) * g_
        v_ = ADAM_B2 * v_ref[...] + (1.0 - ADAM_B2) * (g_ * g_)
        mo_ref[...] = m_
        vo_ref[...] = v_
        d_ref[...] = -ADAM_LR * ((m_ / c1) / (jnp.sqrt(v_ / c2) + ADAM_EPS) + ADAM_WD * w_ref[...])

    spec = pl.BlockSpec((bk, n), lambda i: (i, 0))
    return pl.pallas_call(body, name=name, grid=(k // bk,), in_specs=[spec] * 4, out_specs=[spec] * 3,
                          out_shape=[_sds((k, n), F32)] * 3,
                          compiler_params=pltpu.CompilerParams(vmem_limit_bytes=VMEM_LIMIT))(w, g, m, v)


_HBM = pl.BlockSpec(memory_space=pltpu.HBM)


def _me():
    return lax.axis_index("x"), lax.axis_index("y"), lax.axis_index("c")


def _other_chips(x, y):
    return [(1 - x, y), (x, 1 - y), (1 - x, 1 - y)]


def _gather_weights(shards):
    n = len(shards)

    def body(*refs):
        x_refs, out_refs = refs[:n], refs[n:2 * n]
        send_sems, recv_sems = refs[2 * n:]
        x, y, c = _me()
        sibling = (x, y, 1 - c)
        chips = _other_chips(x, y)

        def copy(k, src, dst, to):
            return pltpu.make_async_remote_copy(src_ref=src, dst_ref=dst, send_sem=send_sems.at[k],
                                                recv_sem=recv_sems.at[k], device_id=to, device_id_type=MESH)

        first, passed = [], []
        for a, (x_ref, out_ref) in enumerate(zip(x_refs, out_refs)):
            for j, (cx, cy) in enumerate(chips):
                first.append(copy(6 * a + j, x_ref.at[c], out_ref.at[2 * x + y, c], (cx, cy, c)))
        for cp in first:
            cp.start()
        for a, (x_ref, out_ref) in enumerate(zip(x_refs, out_refs)):
            for j, (cx, cy) in enumerate(chips):
                landed = out_ref.at[2 * cx + cy, c]
                copy(6 * a + j, x_ref.at[c], landed, (cx, cy, c)).wait_recv()
                passed.append(copy(6 * a + 3 + j, landed, landed, sibling))
                passed[-1].start()
        for a, (x_ref, out_ref) in enumerate(zip(x_refs, out_refs)):
            for j, (cx, cy) in enumerate(chips):
                theirs = out_ref.at[2 * cx + cy, 1 - c]
                copy(6 * a + 3 + j, theirs, theirs, sibling).wait_recv()
        for cp in first + passed:
            cp.wait_send()

    return pl.pallas_call(
        body, name="gather_weights", out_shape=[_sds((4,) + s.shape, s.dtype) for s in shards],
        in_specs=[_HBM] * n, out_specs=[_HBM] * n,
        scratch_shapes=[pltpu.SemaphoreType.DMA((6 * n,)), pltpu.SemaphoreType.DMA((6 * n,))],
    )(*shards)


def _swap_sibling(name, vs, other_half=False):
    n = len(vs)

    def body(*refs):
        v_refs, out_refs = refs[:n], refs[n:2 * n]
        send_sems, recv_sems = refs[2 * n:]
        x, y, c = _me()
        cps = [pltpu.make_async_remote_copy(src_ref=v_ref.at[:, 1 - c] if other_half else v_ref, dst_ref=out_ref,
                                            send_sem=send_sems.at[a], recv_sem=recv_sems.at[a],
                                            device_id=(x, y, 1 - c), device_id_type=MESH)
               for a, (v_ref, out_ref) in enumerate(zip(v_refs, out_refs))]
        for cp in cps:
            cp.start()
        for cp in cps:
            cp.wait()

    def landing(v):
        return _sds((v.shape[0],) + v.shape[2:] if other_half else v.shape, v.dtype)

    return pl.pallas_call(
        body, name=name, out_shape=[landing(v) for v in vs], in_specs=[_HBM] * n, out_specs=[_HBM] * n,
        scratch_shapes=[pltpu.SemaphoreType.DMA((n,)), pltpu.SemaphoreType.DMA((n,))],
    )(*vs)


def _scatter_chips(ss):
    n = len(ss)

    def body(*refs):
        s_refs, out_refs = refs[:n], refs[n:2 * n]
        send_sems, recv_sems = refs[2 * n:]
        pairs = []
        for a, (s_ref, out_ref) in enumerate(zip(s_refs, out_refs)):
            pairs += _chip_copies(s_ref, out_ref, send_sems, recv_sems, True, 3 * a)
        for send, _ in pairs:
            send.start()
        for _, recv in pairs:
            recv.wait_recv()
        for send, _ in pairs:
            send.wait_send()

    return pl.pallas_call(
        body, name="scatter_chips", out_shape=[_sds(s.shape, s.dtype) for s in ss],
        in_specs=[_HBM] * n, out_specs=[_HBM] * n,
        scratch_shapes=[pltpu.SemaphoreType.DMA((3 * n,)), pltpu.SemaphoreType.DMA((3 * n,))],
    )(*ss)


_SEM = pl.BlockSpec(memory_space=pltpu.SEMAPHORE)
_EFFECT = pltpu.SideEffectType.DATAFLOW_SIDE_EFFECTING


def _chip_copies(v_ref, land_ref, send_sems, recv_sems, per_chip_piece, sem0=0):
    x, y, c = _me()
    k = 2 * x + y
    out = []
    for j, (cx, cy) in enumerate(_other_chips(x, y)):
        src = v_ref.at[2 * cx + cy] if per_chip_piece else v_ref
        sems = dict(send_sem=send_sems.at[sem0 + j], recv_sem=recv_sems.at[sem0 + j], device_id=(cx, cy, c),
                    device_id_type=MESH)
        send = pltpu.make_async_remote_copy(src_ref=src, dst_ref=land_ref.at[k], **sems)
        recv = pltpu.make_async_remote_copy(src_ref=src, dst_ref=land_ref.at[2 * cx + cy], **sems)
        out.append((send, recv))
    return out


def _chips_start(name, vs, per_chip_piece, after=None):
    n = len(vs)
    lands = [v.shape if per_chip_piece else (4,) + v.shape for v in vs]

    def body(*refs):
        v_refs, land_refs = refs[:n], refs[n:2 * n]
        send_sems, recv_sems = refs[-2 * n - 3], refs[-2 * n - 2]
        token = refs[-1]
        for a in range(n):
            for send, _ in _chip_copies(v_refs[a], land_refs[a], send_sems, recv_sems, per_chip_piece, 3 * a):
                send.start()
        token[...] = jnp.zeros_like(token)

    extra = () if after is None else (after,)
    hbm = [pltpu.with_memory_space_constraint(v, pltpu.HBM) for v in vs]
    zones = [pltpu.with_memory_space_constraint(lax.empty(s, v.dtype), pltpu.HBM) for s, v in zip(lands, vs)]
    out = pl.pallas_call(
        body, name=name,
        out_shape=(pltpu.SemaphoreType.DMA((3 * n,)), pltpu.SemaphoreType.DMA((3 * n,)),
                   *[pltpu.HBM(v.shape, v.dtype) for v in vs], *[pltpu.HBM(s, v.dtype) for s, v in zip(lands, vs)],
                   _sds((8, LANES), F32)),
        in_specs=(_HBM,) * (2 * n) + (pl.BlockSpec(memory_space=pl.ANY),) * len(extra),
        out_specs=(_SEM, _SEM) + (_HBM,) * (2 * n) + (pl.BlockSpec(memory_space=pltpu.VMEM),),
        input_output_aliases={i: 2 + i for i in range(2 * n)},
        compiler_params=pltpu.CompilerParams(has_side_effects=_EFFECT),
    )(*hbm, *zones, *extra)
    return out[0], out[1], list(out[2:2 + n]), list(out[2 + n:2 + 2 * n]), out[-1]


def _chips_wait(name, send_sems, recv_sems, v_thru, land_thru, per_chip_piece, after):
    n = len(v_thru)

    def body(*refs):
        v_refs, land_refs = refs[:n], refs[n:2 * n]
        send_sems, recv_sems = refs[2 * n], refs[2 * n + 1]
        for a in range(n):
            for send, recv in _chip_copies(v_refs[a], land_refs[a], send_sems, recv_sems, per_chip_piece, 3 * a):
                send.wait_send()
                recv.wait_recv()

    out = pl.pallas_call(
        body, name=name,
        out_shape=tuple(pltpu.HBM(a.shape, a.dtype) for a in list(v_thru) + list(land_thru)),
        in_specs=(_HBM,) * (2 * n) + (_SEM, _SEM, pl.BlockSpec(memory_space=pl.ANY)), out_specs=(_HBM,) * (2 * n),
        input_output_aliases={i: i for i in range(2 * n)},
        compiler_params=pltpu.CompilerParams(has_side_effects=_EFFECT),
    )(*v_thru, *land_thru, send_sems, recv_sems, after)
    return list(out[n:])


def _gather_small(name, v):
    def body(v_ref, out_ref, send_sems, recv_sems, local_sem):
        x, y, c = _me()
        me = 4 * x + 2 * y + c
        mine = pltpu.make_async_copy(v_ref, out_ref.at[me], local_sem)
        mine.start()
        peers = []
        for f in range(1, 8):
            fx, fy, fc = (f >> 2) & 1, (f >> 1) & 1, f & 1
            px = 1 - x if fx else x
            py = 1 - y if fy else y
            pc = 1 - c if fc else c
            peers.append((f - 1, (px, py, pc)))
        sends = [pltpu.make_async_remote_copy(src_ref=v_ref, dst_ref=out_ref.at[me], send_sem=send_sems.at[k],
                                              recv_sem=recv_sems.at[k], device_id=peer, device_id_type=MESH)
                 for k, peer in peers]
        for cp in sends:
            cp.start()
        for k, (px, py, pc) in peers:
            pltpu.make_async_remote_copy(src_ref=v_ref, dst_ref=out_ref.at[4 * px + 2 * py + pc],
                                         send_sem=send_sems.at[k], recv_sem=recv_sems.at[k],
                                         device_id=(px, py, pc), device_id_type=MESH).wait_recv()
        for cp in sends:
            cp.wait_send()
        mine.wait()

    return pl.pallas_call(
        body, name=name, out_shape=_sds((8,) + v.shape, v.dtype), in_specs=[_HBM], out_specs=_HBM,
        scratch_shapes=[pltpu.SemaphoreType.DMA((7,)), pltpu.SemaphoreType.DMA((7,)), pltpu.SemaphoreType.DMA],
    )(v)


_BIG = (("w_in", (1024, 3232), 1), ("w_uq", (256, 768), 1), ("w_ukv", (128, 1024), 1), ("w_branch_a", (512, 1024), 1),
        ("w_branch_b", (512, 1024), 1), ("w_out", (1024, 1024), 0), ("w_up", (1024, 5632), 1),
        ("w_down", (2816, 1024), 0), ("w_ple_gate", (1024, 1024), 0), ("w_ple", (256, 1024), 1))


def _shard_shape(shape, axis):
    return (shape[0] // 4, shape[1]) if axis == 0 else (shape[0], shape[1] // 4)


def _half_rows(shape, axis):
    k, n = _shard_shape(shape, axis)
    return k * n // (2 * LANES)


_EARLY = ("w_in", "w_uq", "w_ukv")
_LATE = ("w_branch_a", "w_branch_b", "w_out", "w_up", "w_down", "w_ple_gate", "w_ple")
_NATURAL = ("w_in", "w_up", "w_down", "w_out", "w_ple_gate")
_EARLY_PACKED = tuple(b for b in _BIG if b[0] in _EARLY and b[0] not in _NATURAL)
_LATE_PACKED = tuple(b for b in _BIG if b[0] in _LATE and b[0] not in _NATURAL)
_SHARD = {name: _shard_shape(shape, axis) for name, shape, axis in _BIG}


def _halves(a):
    return a.reshape(a.shape[:-2] + (2, a.shape[-2] // 2, a.shape[-1]))


def _rows_joined(a):
    return a.reshape(a.shape[:-3] + (a.shape[-3] * a.shape[-2], a.shape[-1]))


def _pack_pad(group):
    return -sum(_half_rows(shape, axis) for _, shape, axis in group) % PACK_ROWS


def _pack_shards(shards, dtype, group):
    parts = [shards[name].astype(dtype).reshape(2, _half_rows(shape, axis), LANES) for name, shape, axis in group]
    return jnp.concatenate(parts + [jnp.zeros((2, _pack_pad(group), LANES), dtype)], axis=1)


def _unpack_gathered(g, group):
    out, off = {}, 0
    for name, shape, axis in group:
        r = _half_rows(shape, axis)
        k, n = _shard_shape(shape, axis)
        w = g[:, :, off:off + r, :].reshape(4, k, n)
        out[name] = w.reshape(shape) if axis == 0 else w.transpose(1, 0, 2).reshape(shape)
        off += r
    return out


def _pack_grads(grads, group):
    parts = []
    for name, shape, axis in group:
        k, n = _shard_shape(shape, axis)
        g = grads[name]
        g4 = g.reshape(4, k, n) if axis == 0 else g.reshape(k, 4, n).transpose(1, 0, 2)
        parts.append(g4.reshape(4, 2, _half_rows(shape, axis), LANES))
    return jnp.concatenate(parts + [jnp.zeros((4, 2, _pack_pad(group), LANES), F32)], axis=2)


def _unpack_shard_grads(f, group):
    out, off = {}, 0
    for name, shape, axis in group:
        r = _half_rows(shape, axis)
        out[name] = f[:, off:off + r, :].reshape(_shard_shape(shape, axis))
        off += r
    return out


def _pad_slots(w, heads, dim, axis):
    if axis == 1:
        k = w.shape[0]
        return jnp.pad(w.reshape(k, heads, dim), ((0, 0), (0, 0), (0, LANES - dim))).reshape(k, heads * LANES)
    n = w.shape[1]
    return jnp.pad(w.reshape(heads, dim, n), ((0, 0), (0, LANES - dim), (0, 0))).reshape(heads * LANES, n)


def _unpad_slots(w, heads, dim, axis):
    if axis == 1:
        k = w.shape[0]
        return w.reshape(k, heads, LANES)[:, :, :dim].reshape(k, heads * dim)
    n = w.shape[1]
    return w.reshape(heads, LANES, n)[:, :dim, :].reshape(heads * dim, n)


def _pad_w_in(w):
    kr = jnp.pad(w[:, 1152:1184], ((0, 0), (NOPE_DIM, LANES - NOPE_DIM - ROPE_DIM)))
    return jnp.concatenate([_pad_slots(w[:, 0:512], HEADS, A_HEAD_DIM, 1),
                            _pad_slots(w[:, 512:640], A_KV_HEADS, A_HEAD_DIM, 1),
                            _pad_slots(w[:, 640:768], A_KV_HEADS, A_HEAD_DIM, 1),
                            w[:, 768:1024], w[:, 1024:1152], kr, w[:, 1184:3232]], axis=1)


def _unpad_w_in(w):
    return jnp.concatenate([_unpad_slots(w[:, Z_QA:Z_KA], HEADS, A_HEAD_DIM, 1),
                            _unpad_slots(w[:, Z_KA:Z_VA], A_KV_HEADS, A_HEAD_DIM, 1),
                            _unpad_slots(w[:, Z_VA:Z_CQ], A_KV_HEADS, A_HEAD_DIM, 1),
                            w[:, Z_CQ:Z_CKV], w[:, Z_CKV:Z_KR],
                            w[:, Z_KR + NOPE_DIM:Z_KR + NOPE_DIM + ROPE_DIM], w[:, Z_GATE:ZW]], axis=1)


_SMALL = (("attn_pre_norm", 1024), ("attn_post_norm", 1024), ("b_gate", 2048), ("sinks", 8), ("q_a_norm", 256),
          ("kv_a_norm", 128), ("mlp_pre_norm", 1024), ("mlp_post_norm", 1024), ("conv_b", 5632), ("ple_norm", 1024),
          ("conv_w", 3 * 5632), ("loss", 1))


def _small_rows(n):
    return 8 * -(-n // (8 * LANES))


def _pack_small(vals):
    parts = []
    for name, n in _SMALL:
        r = _small_rows(n)
        parts.append(jnp.pad(vals[name].reshape(-1), (0, r * LANES - n)).reshape(r, LANES))
    return jnp.concatenate(parts, axis=0)


def _unpack_small(buf):
    out, off = {}, 0
    for name, n in _SMALL:
        r = _small_rows(n)
        out[name] = buf[off:off + r].reshape(-1)[:n]
        off += r
    return out


def kernel(x, p, positions, attn_pre_norm, attn_post_norm, w_in, b_gate, sinks, q_a_norm, w_uq, kv_a_norm, w_ukv, w_branch_a, w_branch_b, w_out, mlp_pre_norm, mlp_post_norm, w_up, conv_w, conv_b, w_down, ple_norm, w_ple_gate, w_ple, loss_target, m_attn_pre_norm, m_attn_post_norm, m_w_in, m_b_gate, m_sinks, m_q_a_norm, m_w_uq, m_kv_a_norm, m_w_ukv, m_w_branch_a, m_w_branch_b, m_w_out, m_mlp_pre_norm, m_mlp_post_norm, m_w_up, m_conv_w, m_conv_b, m_w_down, m_ple_norm, m_w_ple_gate, m_w_ple, v_attn_pre_norm, v_attn_post_norm, v_w_in, v_b_gate, v_sinks, v_q_a_norm, v_w_uq, v_kv_a_norm, v_w_ukv, v_w_branch_a, v_w_branch_b, v_w_out, v_mlp_pre_norm, v_mlp_post_norm, v_w_up, v_conv_w, v_conv_b, v_w_down, v_ple_norm, v_w_ple_gate, v_w_ple):
    names = ["attn_pre_norm", "attn_post_norm", "w_in", "b_gate", "sinks", "q_a_norm", "w_uq", "kv_a_norm", "w_ukv",
             "w_branch_a", "w_branch_b", "w_out", "mlp_pre_norm", "mlp_post_norm", "w_up", "conv_w", "conv_b",
             "w_down", "ple_norm", "w_ple_gate", "w_ple"]
    wts = dict(zip(names, [attn_pre_norm, attn_post_norm, w_in, b_gate, sinks, q_a_norm, w_uq, kv_a_norm, w_ukv,
                           w_branch_a, w_branch_b, w_out, mlp_pre_norm, mlp_post_norm, w_up, conv_w, conv_b, w_down,
                           ple_norm, w_ple_gate, w_ple]))
    moms = dict(zip(names, [m_attn_pre_norm, m_attn_post_norm, m_w_in, m_b_gate, m_sinks, m_q_a_norm, m_w_uq,
                            m_kv_a_norm, m_w_ukv, m_w_branch_a, m_w_branch_b, m_w_out, m_mlp_pre_norm,
                            m_mlp_post_norm, m_w_up, m_conv_w, m_conv_b, m_w_down, m_ple_norm, m_w_ple_gate, m_w_ple]))
    vars_ = dict(zip(names, [v_attn_pre_norm, v_attn_post_norm, v_w_in, v_b_gate, v_sinks, v_q_a_norm, v_w_uq,
                             v_kv_a_norm, v_w_ukv, v_w_branch_a, v_w_branch_b, v_w_out, v_mlp_pre_norm,
                             v_mlp_post_norm, v_w_up, v_conv_w, v_conv_b, v_w_down, v_ple_norm, v_w_ple_gate, v_w_ple]))
    w2 = {n: a.reshape(a.shape[-2:]) for n, a in wts.items()}
    m2 = {n: a.reshape(a.shape[-2:]) for n, a in moms.items()}
    v2 = {n: a.reshape(a.shape[-2:]) for n, a in vars_.items()}

    t_rows = x.shape[-2]
    tm = min(256, t_rows)
    xc, yc, cc = lax.axis_index("x"), lax.axis_index("y"), lax.axis_index("c")
    chip = 2 * xc + yc

    x2d = x.reshape(t_rows, D_MODEL)
    p2d = p.reshape(t_rows, PLE_DIM)
    tgt = loss_target.reshape(t_rows, D_MODEL)
    pos_f = positions.reshape(t_rows, 1).astype(F32)

    def own_slot_filled(gathered, mine):
        return [lax.dynamic_update_slice(g, m[None], (chip, 0, 0, 0)) for g, m in zip(gathered, mine)]

    def shard_lists(group, packed_group):
        return ([_halves(w2[n].astype(BF16)) for n in group if n in _NATURAL]
                + [_pack_shards(w2, BF16, packed_group)])

    cw_rows = 3 * 1408 // LANES
    conv_mine = jnp.pad(w2["conv_w"].reshape(cw_rows, LANES), ((0, 48 - cw_rows), (0, 0))).reshape(2, 24, LANES)
    early_mine = shard_lists(_EARLY, _EARLY_PACKED) + [conv_mine]
    late_mine = shard_lists(_LATE, _LATE_PACKED)
    early = own_slot_filled(_gather_weights(early_mine), early_mine)
    late_sems = _chips_start("gather_late_start", late_mine, False, after=early[0])
    late_token = late_sems[4][0:1, 0:1]
    full = _unpack_gathered(early[1], _EARLY_PACKED)
    full["w_in"] = _rows_joined(early[0]).transpose(1, 0, 2).reshape(D_MODEL, 3232)
    conv_full = early[2].reshape(4, 48, LANES)[:, :cw_rows].reshape(4, 3, 1408).transpose(1, 0, 2).reshape(3, 2 * D_FF)
    convw8 = jnp.pad(conv_full, ((0, 5), (0, 0)))

    win = _pad_w_in(full["w_in"])
    wuq = _pad_slots(full["w_uq"], HEADS, NOPE_DIM + ROPE_DIM, 1)
    ukv = full["w_ukv"].reshape(KV_LORA, HEADS, NOPE_DIM + V_DIM)
    wk = _pad_slots(ukv[:, :, :NOPE_DIM].reshape(KV_LORA, HEADS * NOPE_DIM), HEADS, NOPE_DIM, 1)
    wv = _pad_slots(ukv[:, :, NOPE_DIM:].reshape(KV_LORA, HEADS * V_DIM), HEADS, V_DIM, 1)
    g1, g2, g3, g4, g5 = (w2["attn_pre_norm"], w2["attn_post_norm"], w2["mlp_pre_norm"], w2["mlp_post_norm"],
                          w2["ple_norm"])
    gq, gkv, bg, convb = w2["q_a_norm"], w2["kv_a_norm"], w2["b_gate"], w2["conv_b"]
    swa_tile = min(SWA_TILE, t_rows)
    sink_rows = jnp.repeat(w2["sinks"].reshape(A_KV_HEADS, SWA_GROUP, 1), swa_tile, axis=2).reshape(
        A_KV_HEADS, 1, SWA_GROUP * swa_tile)
    swa_bias = _swa_bias(swa_tile)

    consts = _rope_consts()
    tabs = _rope_tables(pos_f, consts, tm)
    h1, qs, ks, vs, cq, cqn, ckv, ckvn, qm, km, vm, gate = _fwd_in(x2d, g1, win, bg + late_token, gq, gkv, wuq, wk, wv,
                                                                   tabs, tm)
    ya, lse_a = _swa_fwd(qs, ks, vs, swa_bias, sink_rows)
    yb, lse_b = _mla_fwd(qm, km, vm)
    late = own_slot_filled(_chips_wait("gather_late_wait", *late_sems[:4], False, after=yb), late_mine)
    full = _unpack_gathered(late[-1], _LATE_PACKED)
    wba = _pad_slots(full["w_branch_a"], HEADS, A_HEAD_DIM, 0)
    wbb = _pad_slots(full["w_branch_b"], HEADS, V_DIM, 0)
    wple = full["w_ple"]
    natural = dict(zip([n for n in _LATE if n in _NATURAL], late))
    wup = _rows_joined(natural["w_up"])
    wout, wdown, wpg = (_rows_joined(natural[n]).reshape(-1, D_MODEL) for n in ("w_out", "w_down", "w_ple_gate"))
    pa, pb, mixed, o, x1, h2 = _fwd_mix(x2d, ya, yb, gate, wba, wbb, wout, g2, g3, tm)
    up, a = _fwd_up(h2, wup, convw8, convb, tm)
    ff, x2, e, n5, sg, dx3, loss_part = _fwd_out(a, wdown, x1, g4, p2d, wple, g5, wpg, tgt, tm)

    dpre, de, dx2, dff, du, dg5, dg4, dconvb, dconvw8 = _bwd_out(dx3, e, sg, x2, ff, g5, g4, wpg, wdown, up, convw8,
                                                                 convb, tm)
    dup, dx1, do, dpa, dpb, dgates, dya, dyb, delta_b, dg3, dg2, dbg = _bwd_mid(
        du, convw8, wup, dx2, x1, g3, o, g2, wout, gate, pa, pb, wba, wbb, yb, tm)
    late_grads = {
        "w_branch_a": _unpad_slots(_mm_tn("dw_branch_a", ya, dpa), HEADS, A_HEAD_DIM, 0),
        "w_branch_b": _unpad_slots(_mm_tn("dw_branch_b", yb, dpb), HEADS, V_DIM, 0),
        "w_out": _mm_tn("dw_out", mixed, do).reshape(4, D_MODEL // 4, D_MODEL),
        "w_up": _mm_tn("dw_up", h2, dup, column_shards=4),
        "w_down": _mm_tn("dw_down", a, dff).reshape(4, D_FF // 4, D_MODEL),
        "w_ple_gate": _mm_tn("dw_ple_gate", n5, dpre).reshape(4, D_MODEL // 4, D_MODEL),
        "w_ple": _mm_tn("dw_ple", p2d, de),
    }

    def pair_sums(tag, grads, group, packed_group):
        views = [_halves(grads[n]) for n in group if n in _NATURAL] + [_pack_grads(grads, packed_group)]
        theirs = _swap_sibling("swap_%s_grad_halves" % tag, views, other_half=True)
        return [_add_pair("rs_%s_add_pair_%d" % (tag, i), g, r, cc) for i, (g, r) in enumerate(zip(views, theirs))]

    late_pairs = pair_sums("late", late_grads, _LATE, _LATE_PACKED)
    rs_sems = _chips_start("scatter_late_start", late_pairs, True)
    rs_token = rs_sems[4][0:1, 0:1]

    dqs, dks, dvs, dsink_rows = _swa_bwd(qs, ks, vs, ya, dya, lse_a, swa_bias, sink_rows + rs_token)
    dsink = dsink_rows[:, 0:SWA_GROUP, 0]
    dqm, dkm, dvm = _mla_bwd(qm, km, vm, dyb, lse_b, delta_b.reshape(HEADS, 1, t_rows))
    dz, dqb, dx, dgq, dgkv, dg1 = _bwd_in(dqs, dks, dvs, dqm, dkm, dvm, tabs, consts, cq, ckv, gq, gkv, wuq, wk, wv,
                                           dgates, win, x2d, g1, dx1, tm)

    dwk = _unpad_slots(_mm_tn("dw_k", ckvn, dkm), HEADS, NOPE_DIM, 1).reshape(KV_LORA, HEADS, NOPE_DIM)
    dwv = _unpad_slots(_mm_tn("dw_v", ckvn, dvm), HEADS, V_DIM, 1).reshape(KV_LORA, HEADS, V_DIM)
    early_grads = {
        "w_in": _unpad_w_in(_mm_tn("dw_in", h1, dz)).reshape(D_MODEL, 4, 808).transpose(1, 0, 2),
        "w_uq": _unpad_slots(_mm_tn("dw_uq", cqn, dqb), HEADS, NOPE_DIM + ROPE_DIM, 1),
        "w_ukv": jnp.concatenate([dwk, dwv], axis=2).reshape(KV_LORA, HEADS * (NOPE_DIM + V_DIM)),
    }

    def finish(tag, pairs, landed, group, packed_group):
        reduced = []
        for i, (pair, land) in enumerate(zip(pairs, landed)):
            own = lax.dynamic_index_in_dim(pair, chip, 0, keepdims=True)
            reduced.append(_add_chips("rs_%s_add_chips_%d" % (tag, i),
                                      lax.dynamic_update_slice(land, own, (chip, 0, 0))))
        others = _swap_sibling("swap_%s_reduced_halves" % tag, reduced)
        both = [jnp.where(cc == 0, jnp.stack([r, o]), jnp.stack([o, r])) for r, o in zip(reduced, others)]
        out = _unpack_shard_grads(both[-1], packed_group)
        out.update({n: _rows_joined(b) for n, b in zip([n for n in group if n in _NATURAL], both)})
        return out

    small = {"attn_pre_norm": dg1, "attn_post_norm": dg2, "b_gate": dbg, "sinks": dsink, "q_a_norm": dgq,
             "kv_a_norm": dgkv, "mlp_pre_norm": dg3, "mlp_post_norm": dg4, "conv_b": dconvb, "ple_norm": dg5,
             "conv_w": dconvw8[0:3], "loss": loss_part}
    small_all = _gather_small("gather_small_grads", _pack_small(small))

    updates = {}

    def adamw(n, g):
        updates[n] = (g,) + tuple(_adamw("adamw_" + n, w2[n], g, m2[n], v2[n]))

    early_pairs = pair_sums("early", early_grads, _EARLY, _EARLY_PACKED)
    early_sems = _chips_start("scatter_early_start", early_pairs, True, after=small_all)
    late_landed = _chips_wait("scatter_late_wait", *rs_sems[:4], True, after=early_sems[4])
    late_shards = finish("late", late_pairs, late_landed, _LATE, _LATE_PACKED)
    for n in _LATE:
        adamw(n, late_shards[n])
    early_landed = _chips_wait("scatter_early_wait", *early_sems[:4], True, after=updates[_LATE[-1]][1])
    early_shards = finish("early", early_pairs, early_landed, _EARLY, _EARLY_PACKED)
    for n in _EARLY:
        adamw(n, early_shards[n])

    small_sum = _unpack_small(_add_devices(small_all))
    for n in names:
        if n == "conv_w":
            adamw(n, lax.dynamic_index_in_dim(small_sum[n].reshape(3, 4, 1408), chip, 1, keepdims=False))
        elif n in small_sum:
            adamw(n, small_sum[n].reshape(w2[n].shape))
    loss = small_sum["loss"][0]

    outs = [[updates[n][i].reshape(wts[n].shape) for n in names] for i in range(4)]
    return (loss, dx.reshape(x.shape), *outs[0], *outs[1], *outs[2], *outs[3])
```

```python
import functools
import math

import numpy as np
import jax
import jax.numpy as jnp
from jax import lax
from jax.experimental import pallas as pl
from jax.experimental.pallas import tpu as pltpu

F32 = jnp.float32
BF16 = jnp.bfloat16

D_MODEL = 1024
D_FF = 2816
PLE_DIM = 256
ROPE_THETA = 10000.0
RMS_EPS = 1e-6
SWA_WINDOW = 128
HEADS = 8
A_KV_HEADS = 2
A_HEAD_DIM = 64
Q_LORA = 256
KV_LORA = 128
NOPE_DIM = 64
ROPE_DIM = 32
V_DIM = 64
LANES = 128
ZW = 4096
NEG = -1e30
SCALE_A = A_HEAD_DIM ** -0.5
SCALE_B = (NOPE_DIM + ROPE_DIM) ** -0.5

ADAM_LR = 0.001
ADAM_B1 = 0.9
ADAM_B2 = 0.999
ADAM_EPS = 1e-08
ADAM_WD = 0.01
ADAM_STEP = 10

VMEM_LIMIT = 60 * 1024 * 1024
MESH_AXES = ("x", "y", "c")
MESH = pl.DeviceIdType.MESH

Z_QA, Z_KA, Z_VA, Z_CQ, Z_CKV, Z_KR, Z_GATE = 0, 1024, 1280, 1536, 1792, 1920, 2048


def _dot(a, b):
    return jnp.dot(a, b, preferred_element_type=F32)


def _dot_nt(a, b):
    return lax.dot_general(a, b, (((1,), (1,)), ((), ())), preferred_element_type=F32)


def _dot_tn(a, b):
    return lax.dot_general(a, b, (((0,), (0,)), ((), ())), preferred_element_type=F32)


def _rms_stats(x):
    r = lax.rsqrt(jnp.mean(x * x, axis=-1, keepdims=True) + RMS_EPS)
    return x * r, r


def _rms_bwd(dy, xn, r, g):
    dxn = dy * g
    dx = r * (dxn - xn * jnp.mean(dxn * xn, axis=-1, keepdims=True))
    dg = jnp.sum(dy * xn, axis=0, keepdims=True)
    return dx, dg


def _tile_lanes(t, n):
    return t if n == 1 else jnp.concatenate([t] * n, axis=1)


def _rope(x, c, s1, s2, half):
    w = x.shape[1]
    n = w // LANES
    return (x * _tile_lanes(c, n) + pltpu.roll(x, w - half, 1) * _tile_lanes(s1, n)
            + pltpu.roll(x, half, 1) * _tile_lanes(s2, n))


def _rope_t(dy, c, s1, s2, half):
    w = dy.shape[1]
    n = w // LANES
    return (dy * _tile_lanes(c, n) + pltpu.roll(dy * _tile_lanes(s1, n), half, 1)
            + pltpu.roll(dy * _tile_lanes(s2, n), w - half, 1))


def _sigmoid(x):
    return 1.0 / (1.0 + jnp.exp(-x))


_GELU_C = math.sqrt(2.0 / math.pi)


def _gelu_and_grad(x):
    x2 = x * x
    th = jnp.tanh(_GELU_C * (x + 0.044715 * x * x2))
    gel = 0.5 * x * (1.0 + th)
    dgel = 0.5 * (1.0 + th) + 0.5 * x * (1.0 - th * th) * (_GELU_C * (1.0 + 3.0 * 0.044715 * x2))
    return gel, dgel


def _conv_taps(up, h6, h7):
    rows = lax.broadcasted_iota(jnp.int32, up.shape, 0)
    r1 = pltpu.roll(up, 1, 0)
    r2 = pltpu.roll(up, 2, 0)
    xm1 = jnp.where(rows == 0, h7, r1)
    xm2 = jnp.where(rows == 0, h6, jnp.where(rows == 1, h7, r2))
    return xm1, xm2


def _conv_taps_next(du, n0, n1):
    tm = du.shape[0]
    rows = lax.broadcasted_iota(jnp.int32, du.shape, 0)
    r1 = pltpu.roll(du, tm - 1, 0)
    r2 = pltpu.roll(du, tm - 2, 0)
    xp1 = jnp.where(rows == tm - 1, n0, r1)
    xp2 = jnp.where(rows == tm - 2, n0, jnp.where(rows == tm - 1, n1, r2))
    return xp1, xp2


def _row(tm, n):
    return pl.BlockSpec((tm, n), lambda i: (i, 0))


def _full(shape):
    nd = len(shape)
    return pl.BlockSpec(tuple(shape), lambda i: (0,) * nd)


def _resident(shape):
    nd = len(shape)
    return pl.BlockSpec(tuple(shape), lambda i: (0,) * nd, pipeline_mode=pl.Buffered(1))


def _heads(tm, h):
    return pl.BlockSpec((h, tm, LANES), lambda i: (0, i, 0))


def _rows_call(name, body, t_rows, tm, ins, outs, scratch=()):
    return pl.pallas_call(
        body, name=name, grid=(t_rows // tm,),
        in_specs=[s for _, s in ins],
        out_specs=[s for _, s in outs],
        out_shape=[s for s, _ in outs],
        scratch_shapes=list(scratch),
        compiler_params=pltpu.CompilerParams(dimension_semantics=("arbitrary",), vmem_limit_bytes=VMEM_LIMIT),
    )(*[a for a, _ in ins])


def _sds(shape, dtype):
    return jax.ShapeDtypeStruct(tuple(shape), dtype)


def _rope_consts():
    c = np.zeros((16, LANES), np.float32)
    lane = np.arange(LANES)
    inv_a = (ROPE_THETA ** (-(np.arange(0, A_HEAD_DIM, 2, dtype=np.float32) / A_HEAD_DIM))).astype(np.float32)
    in_a = lane < A_HEAD_DIM
    c[0, in_a] = inv_a[lane[in_a] % (A_HEAD_DIM // 2)]
    c[1, in_a] = 1.0
    c[2, lane < A_HEAD_DIM // 2] = -1.0
    c[3, (lane >= A_HEAD_DIM // 2) & in_a] = 1.0
    inv_b = (ROPE_THETA ** (-(np.arange(0, ROPE_DIM, 2, dtype=np.float32) / ROPE_DIM))).astype(np.float32)
    pe = (lane >= NOPE_DIM) & (lane < NOPE_DIM + ROPE_DIM)
    c[5, pe] = inv_b[(lane[pe] - NOPE_DIM) % (ROPE_DIM // 2)]
    c[6, pe] = 1.0
    c[7, (lane >= NOPE_DIM) & (lane < NOPE_DIM + ROPE_DIM // 2)] = -1.0
    c[8, (lane >= NOPE_DIM + ROPE_DIM // 2) & (lane < NOPE_DIM + ROPE_DIM)] = 1.0
    c[9, lane < NOPE_DIM] = 1.0
    c[10, pe] = 1.0
    return jnp.asarray(c)


def _rope_tables(pos_f, consts, tm):
    t_rows = pos_f.shape[0]

    def body(pos_ref, c_ref, ca, sa1, sa2, cb, sb1, sb2):
        pos = pos_ref[...]
        ang = pos * c_ref[0:1, :]
        cs, sn = jnp.cos(ang), jnp.sin(ang)
        ca[...] = cs * c_ref[1:2, :]
        sa1[...] = sn * c_ref[2:3, :]
        sa2[...] = sn * c_ref[3:4, :]
        ang = pos * c_ref[5:6, :]
        cs, sn = jnp.cos(ang), jnp.sin(ang)
        cb[...] = cs * c_ref[6:7, :] + c_ref[9:10, :]
        sb1[...] = sn * c_ref[7:8, :]
        sb2[...] = sn * c_ref[8:9, :]

    tab = (_sds((t_rows, LANES), F32), _row(tm, LANES))
    return _rows_call("rope_tables", body, t_rows, tm,
                      [(pos_f, _row(tm, 1)), (consts, _full(consts.shape))], [tab] * 6)


def _fwd_in(x, g1, win, bg, gq, gkv, wuq, wk, wv, tabs, tm):
    t_rows = x.shape[0]

    def body(x_ref, g1_ref, win_ref, bg_ref, gq_ref, gkv_ref, wuq_ref, wk_ref, wv_ref,
             ca, sa1, sa2, cb, sb1, sb2,
             h1_ref, qs_ref, ks_ref, vs_ref, cq_ref, cqn_ref, ckv_ref, ckvn_ref, qm_ref, km_ref, vm_ref, gate_ref):
        xn, _ = _rms_stats(x_ref[...])
        hb = (xn * g1_ref[...]).astype(BF16)
        h1_ref[...] = hb
        ta = (ca[...], sa1[...], sa2[...])
        tb = (cb[...], sb1[...], sb2[...])
        qs_ref[...] = (_rope(_dot(hb, win_ref[:, Z_QA:Z_KA]), *ta, A_HEAD_DIM // 2) * SCALE_A).astype(BF16)
        ks_ref[...] = _rope(_dot(hb, win_ref[:, Z_KA:Z_VA]), *ta, A_HEAD_DIM // 2).astype(BF16)
        vs_ref[...] = _dot(hb, win_ref[:, Z_VA:Z_CQ]).astype(BF16)
        cq = _dot(hb, win_ref[:, Z_CQ:Z_CKV])
        cq_ref[...] = cq
        cqn, _ = _rms_stats(cq)
        cqb = (cqn * gq_ref[...]).astype(BF16)
        cqn_ref[...] = cqb
        qm_ref[...] = (_rope(_dot(cqb, wuq_ref[...]), *tb, ROPE_DIM // 2) * SCALE_B).astype(BF16)
        ckv = _dot(hb, win_ref[:, Z_CKV:Z_KR])
        ckv_ref[...] = ckv
        ckvn, _ = _rms_stats(ckv)
        ckvb = (ckvn * gkv_ref[...]).astype(BF16)
        ckvn_ref[...] = ckvb
        kpe = _rope(_dot(hb, win_ref[:, Z_KR:Z_GATE]), *tb, ROPE_DIM // 2)
        km_ref[...] = (_dot(ckvb, wk_ref[...]) + _tile_lanes(kpe, HEADS)).astype(BF16)
        vm_ref[...] = _dot(ckvb, wv_ref[...]).astype(BF16)
        gate_ref[...] = _sigmoid(_dot(hb, win_ref[:, Z_GATE:ZW]) + bg_ref[...])

    def o(n, dt):
        return (_sds((t_rows, n), dt), _row(tm, n))

    ins = [(x, _row(tm, D_MODEL)), (g1, _full(g1.shape)), (win, _resident(win.shape)), (bg, _full(bg.shape)),
           (gq, _full(gq.shape)), (gkv, _full(gkv.shape)), (wuq, _full(wuq.shape)), (wk, _full(wk.shape)),
           (wv, _full(wv.shape))] + [(t, _row(tm, LANES)) for t in tabs]
    outs = [o(1024, BF16), o(1024, BF16), o(256, BF16), o(256, BF16), o(256, F32), o(256, BF16), o(128, F32),
            o(128, BF16), o(1024, BF16), o(1024, BF16), o(1024, BF16), o(2048, F32)]
    return _rows_call("fwd_in", body, t_rows, tm, ins, outs)


def _attn_tile(t_rows):
    return min(512, t_rows)


MLA_HEADS_PER_STEP = 2


def _causal_pairs(nq, by_kv):
    if by_kv:
        pairs = [(i, j) for j in range(nq) for i in range(j, nq)]
    else:
        pairs = [(i, j) for i in range(nq) for j in range(i + 1)]
    return (jnp.asarray([p[0] for p in pairs], jnp.int32), jnp.asarray([p[1] for p in pairs], jnp.int32))


def _mla_fwd(q, k, v):
    t_rows = q.shape[0]
    t = _attn_tile(t_rows)
    hp = MLA_HEADS_PER_STEP
    w = hp * LANES
    ii, jj = _causal_pairs(t_rows // t, by_kv=False)

    def body(i_ref, j_ref, q_ref, k_ref, v_ref, o_ref, lse_ref, m_s, l_s, acc_s):
        i = i_ref[pl.program_id(1)]
        j = j_ref[pl.program_id(1)]

        @pl.when(j == 0)
        def _():
            m_s[...] = jnp.full(m_s.shape, NEG, F32)
            l_s[...] = jnp.zeros(l_s.shape, F32)
            acc_s[...] = jnp.zeros(acc_s.shape, F32)

        def step(diagonal):
            for hh in range(hp):
                sl = slice(hh * LANES, (hh + 1) * LANES)
                s = _dot_nt(k_ref[:, sl], q_ref[:, sl])
                if diagonal:
                    valid = (lax.broadcasted_iota(jnp.int32, (t, t), 0) <= lax.broadcasted_iota(jnp.int32, (t, t), 1))
                    s = jnp.where(valid, s, NEG)
                m_prev = m_s[hh]
                m_new = jnp.maximum(m_prev, jnp.max(s, axis=0, keepdims=True))
                p = jnp.exp(s - m_new)
                alpha = jnp.exp(m_prev - m_new)
                l_new = alpha * l_s[hh] + jnp.sum(p, axis=0, keepdims=True)
                acc = alpha * acc_s[hh] + _dot_tn(v_ref[:, sl], p.astype(BF16))
                if diagonal:
                    o_ref[:, sl] = (acc / l_new).T.astype(o_ref.dtype)
                    lse_ref[hh] = m_new + jnp.log(l_new)
                else:
                    m_s[hh] = m_new
                    l_s[hh] = l_new
                    acc_s[hh] = acc

        pl.when(j < i)(lambda: step(False))
        pl.when(j == i)(lambda: step(True))

    grid_spec = pltpu.PrefetchScalarGridSpec(
        num_scalar_prefetch=2, grid=(HEADS // hp, ii.shape[0]),
        in_specs=[pl.BlockSpec((t, w), lambda hb, s, ir, jr: (ir[s], hb)),
                  pl.BlockSpec((t, w), lambda hb, s, ir, jr: (jr[s], hb)),
                  pl.BlockSpec((t, w), lambda hb, s, ir, jr: (jr[s], hb))],
        out_specs=[pl.BlockSpec((t, w), lambda hb, s, ir, jr: (ir[s], hb)),
                   pl.BlockSpec((hp, 1, t), lambda hb, s, ir, jr: (hb, 0, ir[s]))],
        scratch_shapes=[pltpu.VMEM((hp, 1, t), F32), pltpu.VMEM((hp, 1, t), F32), pltpu.VMEM((hp, LANES, t), F32)])
    return pl.pallas_call(
        body, name="mla_fwd", grid_spec=grid_spec,
        out_shape=[_sds((t_rows, HEADS * LANES), BF16), _sds((HEADS, 1, t_rows), F32)],
        compiler_params=pltpu.CompilerParams(dimension_semantics=("arbitrary",) * 2, vmem_limit_bytes=VMEM_LIMIT),
    )(ii, jj, q, k, v)


def _mla_bwd(q, k, v, do, lse, delta):
    t_rows = q.shape[0]
    t = _attn_tile(t_rows)
    hp = MLA_HEADS_PER_STEP
    w = hp * LANES
    ii, jj = _causal_pairs(t_rows // t, by_kv=True)

    def body(i_ref, j_ref, q_ref, k_ref, v_ref, do_ref, lse_ref, dl_ref, dq_ref, dk_ref, dv_ref):
        i = i_ref[pl.program_id(1)]
        j = j_ref[pl.program_id(1)]

        @pl.when(pl.program_id(1) == 0)
        def _():
            dq_ref[...] = jnp.zeros(dq_ref.shape, F32)

        def step(diagonal):
            r0 = pl.multiple_of(i * t, t)
            for hh in range(hp):
                sl = slice(hh * LANES, (hh + 1) * LANES)
                qv = q_ref[:, sl]
                kv = k_ref[:, sl]
                dov = do_ref[:, sl]
                s = _dot_nt(kv, qv)
                if diagonal:
                    valid = (lax.broadcasted_iota(jnp.int32, (t, t), 0) <= lax.broadcasted_iota(jnp.int32, (t, t), 1))
                    s = jnp.where(valid, s, NEG)
                p = jnp.exp(s - lse_ref[hh])
                dv = _dot(p.astype(BF16), dov)
                dp = _dot_nt(v_ref[:, sl], dov)
                ds = (p * (dp - dl_ref[hh])).astype(BF16)
                dk = _dot(ds, qv)
                if diagonal:
                    dv_ref[:, sl] = dv
                    dk_ref[:, sl] = dk
                else:
                    dv_ref[:, sl] += dv
                    dk_ref[:, sl] += dk
                dq_ref[hh, pl.ds(r0, t), :] += _dot_tn(ds, kv)

        pl.when(i > j)(lambda: step(False))
        pl.when(i == j)(lambda: step(True))

    def qmap(hb, s, ir, jr):
        return (ir[s], hb)

    def kvmap(hb, s, ir, jr):
        return (jr[s], hb)

    def rowmap(hb, s, ir, jr):
        return (hb, 0, ir[s])

    grid_spec = pltpu.PrefetchScalarGridSpec(
        num_scalar_prefetch=2, grid=(HEADS // hp, ii.shape[0]),
        in_specs=[pl.BlockSpec((t, w), qmap), pl.BlockSpec((t, w), kvmap), pl.BlockSpec((t, w), kvmap),
                  pl.BlockSpec((t, w), qmap), pl.BlockSpec((hp, 1, t), rowmap), pl.BlockSpec((hp, 1, t), rowmap)],
        out_specs=[pl.BlockSpec((hp, t_rows, LANES), lambda hb, s, ir, jr: (hb, 0, 0)),
                   pl.BlockSpec((t, w), kvmap), pl.BlockSpec((t, w), kvmap)])
    return pl.pallas_call(
        body, name="mla_bwd", grid_spec=grid_spec,
        out_shape=[_sds((HEADS, t_rows, LANES), F32), _sds((t_rows, HEADS * LANES), F32),
                   _sds((t_rows, HEADS * LANES), F32)],
        compiler_params=pltpu.CompilerParams(dimension_semantics=("arbitrary",) * 2, vmem_limit_bytes=VMEM_LIMIT),
    )(ii, jj, q, k, v, do, lse, delta)


SWA_TILE = 2 * SWA_WINDOW
SWA_GROUP = HEADS // A_KV_HEADS


def _swa_bias(tq):
    koff = lax.broadcasted_iota(jnp.int32, (tq + SWA_WINDOW, SWA_GROUP * tq), 0) - SWA_WINDOW
    qoff = (lax.broadcasted_iota(jnp.int32, (tq + SWA_WINDOW, SWA_GROUP * tq), 1) % tq)
    band = (koff <= qoff) & (qoff - koff < SWA_WINDOW)
    return jnp.stack([jnp.where(band & (koff >= 0), 0.0, NEG), jnp.where(band, 0.0, NEG)]).astype(F32)


def _swa_specs(tq, nq):
    wb = tq // SWA_WINDOW

    def qi(i):
        return jnp.minimum(i, nq - 1)

    q = pl.BlockSpec((tq, SWA_GROUP * LANES), lambda h, i: (qi(i), h))
    cur = pl.BlockSpec((tq, LANES), lambda h, i: (qi(i), h))
    prev = pl.BlockSpec((SWA_WINDOW, LANES), lambda h, i: (jnp.maximum(qi(i) * wb - 1, 0), h))
    bias = pl.BlockSpec((1, tq + SWA_WINDOW, SWA_GROUP * tq), lambda h, i: (jnp.minimum(i, 1), 0, 0))
    rows = pl.BlockSpec((1, 1, 1, SWA_GROUP * tq), lambda h, i: (h, qi(i), 0, 0))
    sink = pl.BlockSpec((1, 1, SWA_GROUP * tq), lambda h, i: (h, 0, 0))
    return q, cur, prev, bias, rows, sink


def _stack_heads(ref):
    return jnp.concatenate([ref[:, g * LANES:(g + 1) * LANES] for g in range(SWA_GROUP)], axis=0)


def _swa_fwd(q, k, v, bias, sink_rows):
    t_rows = q.shape[0]
    tq = min(SWA_TILE, t_rows)
    nq = t_rows // tq
    qs_, cur, prev, bs, rows, sk = _swa_specs(tq, nq)

    def body(q_ref, kc_ref, kp_ref, vc_ref, vp_ref, b_ref, sink_ref, o_ref, lse_ref):
        qs = _stack_heads(q_ref)
        kk = jnp.concatenate([kp_ref[...], kc_ref[...]], axis=0)
        vv = jnp.concatenate([vp_ref[...], vc_ref[...]], axis=0)
        s = _dot_nt(kk, qs) + b_ref[0]
        sink = sink_ref[0]
        m = jnp.maximum(jnp.max(s, axis=0, keepdims=True), sink)
        p = jnp.exp(s - m)
        l = jnp.sum(p, axis=0, keepdims=True) + jnp.exp(sink - m)
        o = (_dot_tn(vv, p.astype(BF16)) / l).T
        for g in range(SWA_GROUP):
            o_ref[:, g * LANES:(g + 1) * LANES] = o[g * tq:(g + 1) * tq].astype(o_ref.dtype)
        lse_ref[0, 0] = m + jnp.log(l)

    return pl.pallas_call(
        body, name="swa_fwd", grid=(A_KV_HEADS, nq),
        in_specs=[qs_, cur, prev, cur, prev, bs, sk],
        out_specs=[qs_, rows],
        out_shape=[_sds((t_rows, HEADS * LANES), BF16), _sds((A_KV_HEADS, nq, 1, SWA_GROUP * tq), F32)],
        compiler_params=pltpu.CompilerParams(dimension_semantics=("arbitrary",) * 2, vmem_limit_bytes=VMEM_LIMIT),
    )(q, k, k, v, v, bias, sink_rows)


def _swa_bwd(q, k, v, o, do, lse, bias, sink_rows):
    t_rows = q.shape[0]
    tq = min(SWA_TILE, t_rows)
    nq = t_rows // tq
    qs_, cur, prev, bs, rows, sk = _swa_specs(tq, nq)
    hw = SWA_WINDOW

    def body(q_ref, kc_ref, kp_ref, vc_ref, vp_ref, o_ref, do_ref, lse_ref, b_ref, sink_ref,
             dq_ref, dk_ref, dv_ref, dsink_ref, ck, cv, dsa):
        i = pl.program_id(1)

        @pl.when(i == 0)
        def _():
            dsa[...] = jnp.zeros(dsa.shape, F32)

        @pl.when(i < nq)
        def _():
            qs = _stack_heads(q_ref)
            dos = _stack_heads(do_ref)
            kk = jnp.concatenate([kp_ref[...], kc_ref[...]], axis=0)
            vv = jnp.concatenate([vp_ref[...], vc_ref[...]], axis=0)
            lse = lse_ref[0, 0]
            p = jnp.exp(_dot_nt(kk, qs) + b_ref[0] - lse)
            delta = jnp.sum((_stack_heads(o_ref).astype(F32) * dos.astype(F32)).T, axis=0, keepdims=True)
            dsa[...] += -jnp.exp(sink_ref[0] - lse) * delta
            dv = _dot(p.astype(BF16), dos)
            ds = (p * (_dot_nt(vv, dos) - delta)).astype(BF16)
            dk = _dot(ds, qs)
            dq = _dot_tn(ds, kk)
            for g in range(SWA_GROUP):
                dq_ref[:, g * LANES:(g + 1) * LANES] = dq[g * tq:(g + 1) * tq]

            @pl.when(i > 0)
            def _():
                dk_ref[0:tq - hw, :] = ck[0:tq - hw, :]
                dk_ref[tq - hw:tq, :] = ck[tq - hw:tq, :] + dk[0:hw]
                dv_ref[0:tq - hw, :] = cv[0:tq - hw, :]
                dv_ref[tq - hw:tq, :] = cv[tq - hw:tq, :] + dv[0:hw]

            ck[...] = dk[hw:hw + tq]
            cv[...] = dv[hw:hw + tq]

        @pl.when(i == nq)
        def _():
            dk_ref[...] = ck[...]
            dv_ref[...] = cv[...]
            dsink_ref[...] = jnp.zeros(dsink_ref.shape, F32)
            for g in range(SWA_GROUP):
                tot = jnp.sum(dsa[:, g * tq:(g + 1) * tq], axis=1, keepdims=True)
                dsink_ref[0, g:g + 1, :] = jnp.zeros((1, LANES), F32) + tot

    kv_out = pl.BlockSpec((tq, LANES), lambda h, i: (jnp.maximum(i - 1, 0), h))
    return pl.pallas_call(
        body, name="swa_bwd", grid=(A_KV_HEADS, nq + 1),
        in_specs=[qs_, cur, prev, cur, prev, qs_, qs_, rows, bs, sk],
        out_specs=[qs_, kv_out, kv_out, pl.BlockSpec((1, 8, LANES), lambda h, i: (h, 0, 0))],
        out_shape=[_sds((t_rows, HEADS * LANES), F32), _sds((t_rows, A_KV_HEADS * LANES), F32),
                   _sds((t_rows, A_KV_HEADS * LANES), F32), _sds((A_KV_HEADS, 8, LANES), F32)],
        scratch_shapes=[pltpu.VMEM((tq, LANES), F32), pltpu.VMEM((tq, LANES), F32),
                        pltpu.VMEM((1, SWA_GROUP * tq), F32)],
        compiler_params=pltpu.CompilerParams(dimension_semantics=("arbitrary",) * 2, vmem_limit_bytes=VMEM_LIMIT),
    )(q, k, k, v, v, o, do, lse, bias, sink_rows)


def _fwd_mix(x, ya, yb, gate, wba, wbb, wout, g2, g3, tm):
    t_rows = x.shape[0]

    def body(x_ref, ya_ref, yb_ref, gate_ref, wba_ref, wbb_ref, wout_ref, g2_ref, g3_ref,
             pa_ref, pb_ref, mixed_ref, o_ref, x1_ref, h2_ref):
        pa = _dot(ya_ref[...], wba_ref[...])
        pb = _dot(yb_ref[...], wbb_ref[...])
        pa_ref[...] = pa
        pb_ref[...] = pb
        mixed = (gate_ref[:, 0:D_MODEL] * pa + gate_ref[:, D_MODEL:2 * D_MODEL] * pb).astype(BF16)
        mixed_ref[...] = mixed
        o = _dot(mixed, wout_ref[...])
        o_ref[...] = o
        on, _ = _rms_stats(o)
        x1 = x_ref[...] + on * g2_ref[...]
        x1_ref[...] = x1
        x1n, _ = _rms_stats(x1)
        h2_ref[...] = (x1n * g3_ref[...]).astype(BF16)

    def o_(dt):
        return (_sds((t_rows, D_MODEL), dt), _row(tm, D_MODEL))

    ins = [(x, _row(tm, D_MODEL)), (ya, _row(tm, 1024)), (yb, _row(tm, 1024)), (gate, _row(tm, 2048)),
           (wba, _resident(wba.shape)), (wbb, _resident(wbb.shape)), (wout, _resident(wout.shape)),
           (g2, _full(g2.shape)), (g3, _full(g3.shape))]
    return _rows_call("fwd_mix", body, t_rows, tm, ins, [o_(F32), o_(F32), o_(BF16), o_(F32), o_(F32), o_(BF16)])


CONV_CHUNK = 1408


def _fwd_up(h2, wup, convw8, convb, tm):
    t_rows = h2.shape[0]
    cdim = 2 * D_FF

    def body(h2_ref, wup_ref, cw_ref, cb_ref, up_ref, a_ref, carry):
        i = pl.program_id(0)

        @pl.when(i == 0)
        def _():
            carry[...] = jnp.zeros(carry.shape, F32)

        hb = h2_ref[...]

        def conv(c0):
            sl = slice(c0, c0 + CONV_CHUNK)
            up = _dot(hb, wup_ref[c0 // CONV_CHUNK])
            up_ref[:, sl] = up
            xm1, xm2 = _conv_taps(up, carry[6:7, sl], carry[7:8, sl])
            u = cw_ref[0:1, sl] * xm2 + cw_ref[1:2, sl] * xm1 + cw_ref[2:3, sl] * up + cb_ref[:, sl]
            carry[:, sl] = up[tm - 8:tm, :]
            return u

        for c0 in range(0, D_FF, CONV_CHUNK):
            ug = conv(c0)
            uv = conv(D_FF + c0)
            gel, _ = _gelu_and_grad(ug)
            a_ref[:, c0:c0 + CONV_CHUNK] = (gel * uv).astype(BF16)

    ins = [(h2, _row(tm, D_MODEL)), (wup, _resident(wup.shape)), (convw8, _full(convw8.shape)), (convb, _full(convb.shape))]
    outs = [(_sds((t_rows, cdim), F32), _row(tm, cdim)), (_sds((t_rows, D_FF), BF16), _row(tm, D_FF))]
    return _rows_call("fwd_up", body, t_rows, tm, ins, outs, scratch=[pltpu.VMEM((8, cdim), F32)])


def _fwd_out(a, wdown, x1, g4, p, wple, g5, wpg, tgt, tm):
    t_rows = a.shape[0]

    def body(a_ref, wdown_ref, x1_ref, g4_ref, p_ref, wple_ref, g5_ref, wpg_ref, tgt_ref,
             ff_ref, x2_ref, e_ref, n5_ref, sg_ref, dx3_ref, loss_ref):
        i = pl.program_id(0)
        ff = _dot(a_ref[...], wdown_ref[...])
        ff_ref[...] = ff
        ffn, _ = _rms_stats(ff)
        x2 = x1_ref[...] + ffn * g4_ref[...]
        x2_ref[...] = x2
        e = _dot(p_ref[...].astype(BF16), wple_ref[...])
        e_ref[...] = e
        x2n, _ = _rms_stats(x2)
        n5 = (x2n * g5_ref[...]).astype(BF16)
        n5_ref[...] = n5
        sg = _sigmoid(_dot(n5, wpg_ref[...]))
        sg_ref[...] = sg
        d = x2 + sg * e - tgt_ref[...]
        dx3_ref[...] = d * (1.0 / D_MODEL)

        @pl.when(i == 0)
        def _():
            loss_ref[...] = jnp.zeros((1, 1), F32)

        loss_ref[...] += 0.5 * jnp.sum(jnp.sum(d * d, axis=1, keepdims=True), axis=0, keepdims=True) * (1.0 / D_MODEL)

    def o_(dt):
        return (_sds((t_rows, D_MODEL), dt), _row(tm, D_MODEL))

    ins = [(a, _row(tm, D_FF)), (wdown, _resident(wdown.shape)), (x1, _row(tm, D_MODEL)), (g4, _full(g4.shape)),
           (p, _row(tm, PLE_DIM)), (wple, _full(wple.shape)), (g5, _full(g5.shape)), (wpg, _resident(wpg.shape)),
           (tgt, _row(tm, D_MODEL))]
    outs = [o_(F32), o_(F32), o_(F32), o_(BF16), o_(F32), o_(F32), (_sds((1, 1), F32), _full((1, 1)))]
    return _rows_call("fwd_out", body, t_rows, tm, ins, outs)


def _bwd_out(dx3, e, sg, x2, ff, g5, g4, wpg, wdown, up, convw8, convb, tm):
    t_rows = dx3.shape[0]
    cdim = 2 * D_FF
    hb = tm // 8

    def body(dx3_ref, e_ref, sg_ref, x2_ref, ff_ref, g5_ref, g4_ref, wpg_ref, wdown_ref, up_ref, halo_ref, cw_ref,
             cb_ref, dpre_ref, de_ref, dx2_ref, dff_ref, du_ref, dg5_ref, dg4_ref, dcb_ref, dcw_ref):
        i = pl.program_id(0)

        @pl.when(i == 0)
        def _():
            dg5_ref[...] = jnp.zeros(dg5_ref.shape, F32)
            dg4_ref[...] = jnp.zeros(dg4_ref.shape, F32)
            dcb_ref[...] = jnp.zeros(dcb_ref.shape, F32)
            dcw_ref[...] = jnp.zeros(dcw_ref.shape, F32)

        dx3 = dx3_ref[...]
        sg = sg_ref[...]
        dpre = (dx3 * e_ref[...] * sg * (1.0 - sg)).astype(BF16)
        dpre_ref[...] = dpre
        de_ref[...] = (dx3 * sg).astype(BF16)
        dn5 = _dot_nt(dpre, wpg_ref[...])
        x2n, r5 = _rms_stats(x2_ref[...])
        d2, dg5 = _rms_bwd(dn5, x2n, r5, g5_ref[...])
        dx2 = dx3 + d2
        dx2_ref[...] = dx2
        dg5_ref[...] += dg5
        ffn, r4 = _rms_stats(ff_ref[...])
        dff, dg4 = _rms_bwd(dx2, ffn, r4, g4_ref[...])
        dg4_ref[...] += dg4
        dffb = dff.astype(BF16)
        dff_ref[...] = dffb
        keep = jnp.where(i > 0, 1.0, 0.0)

        def conv(c0):
            sl = slice(c0, c0 + CONV_CHUNK)
            up = up_ref[:, sl]
            xm1, xm2 = _conv_taps(up, halo_ref[6:7, sl] * keep, halo_ref[7:8, sl] * keep)
            u = cw_ref[0:1, sl] * xm2 + cw_ref[1:2, sl] * xm1 + cw_ref[2:3, sl] * up + cb_ref[:, sl]
            return u, up, xm1, xm2

        def grads(c0, du, up, xm1, xm2):
            sl = slice(c0, c0 + CONV_CHUNK)
            du_ref[:, sl] = du.astype(BF16)
            dcb_ref[:, sl] += jnp.sum(du, axis=0, keepdims=True)
            dcw_ref[0:1, sl] += jnp.sum(du * xm2, axis=0, keepdims=True)
            dcw_ref[1:2, sl] += jnp.sum(du * xm1, axis=0, keepdims=True)
            dcw_ref[2:3, sl] += jnp.sum(du * up, axis=0, keepdims=True)

        for c0 in range(0, D_FF, CONV_CHUNK):
            da = _dot_nt(dffb, wdown_ref[c0:c0 + CONV_CHUNK, :])
            ug, *rg = conv(c0)
            uv, *rv = conv(D_FF + c0)
            gel, dgel = _gelu_and_grad(ug)
            grads(c0, da * uv * dgel, *rg)
            grads(D_FF + c0, da * gel, *rv)

    def o_(n, dt):
        return (_sds((t_rows, n), dt), _row(tm, n))

    def acc(r, n):
        return (_sds((r, n), F32), _full((r, n)))

    halo = pl.BlockSpec((8, cdim), lambda i: (jnp.maximum(i * hb - 1, 0), 0))
    ins = [(dx3, _row(tm, D_MODEL)), (e, _row(tm, D_MODEL)), (sg, _row(tm, D_MODEL)), (x2, _row(tm, D_MODEL)),
           (ff, _row(tm, D_MODEL)), (g5, _full(g5.shape)), (g4, _full(g4.shape)), (wpg, _resident(wpg.shape)),
           (wdown, _resident(wdown.shape)), (up, _row(tm, cdim)), (up, halo), (convw8, _full(convw8.shape)),
           (convb, _full(convb.shape))]
    outs = [o_(D_MODEL, BF16), o_(D_MODEL, BF16), o_(D_MODEL, F32), o_(D_MODEL, BF16), o_(cdim, BF16),
            acc(1, D_MODEL), acc(1, D_MODEL), acc(1, cdim), acc(8, cdim)]
    return _rows_call("bwd_out", body, t_rows, tm, ins, outs)


def _bwd_mid(du, convw8, wup, dx2, x1, g3, o, g2, wout, gate, pa, pb, wba, wbb, yb, tm):
    t_rows = du.shape[0]
    cdim = 2 * D_FF
    halo_rows = 16
    hb = tm // halo_rows
    last_blk = t_rows // halo_rows - 1
    n_tiles = t_rows // tm

    def body(du_ref, halo_ref, cw_ref, wup_ref, dx2_ref, x1_ref, g3_ref, o_ref, g2_ref, wout_ref, gate_ref, pa_ref,
             pb_ref, wba_ref, wbb_ref, yb_ref,
             dup_ref, dx1_ref, do_ref, dpa_ref, dpb_ref, dgt_ref, dya_ref, dyb_ref, dl_ref, dg3_ref, dg2_ref, dbg_ref):
        i = pl.program_id(0)

        @pl.when(i == 0)
        def _():
            dg3_ref[...] = jnp.zeros(dg3_ref.shape, F32)
            dg2_ref[...] = jnp.zeros(dg2_ref.shape, F32)
            dbg_ref[...] = jnp.zeros(dbg_ref.shape, F32)

        keep = jnp.where(i < n_tiles - 1, 1.0, 0.0)
        dh2 = jnp.zeros((tm, D_MODEL), F32)
        for c0 in range(0, cdim, CONV_CHUNK):
            sl = slice(c0, c0 + CONV_CHUNK)
            du = du_ref[:, sl].astype(F32)
            nxt = halo_ref[:, sl].astype(F32)
            xp1, xp2 = _conv_taps_next(du, nxt[0:1] * keep, nxt[1:2] * keep)
            dup = (cw_ref[2:3, sl] * du + cw_ref[1:2, sl] * xp1 + cw_ref[0:1, sl] * xp2).astype(BF16)
            dup_ref[:, sl] = dup
            dh2 = dh2 + _dot_nt(dup, wup_ref[c0 // CONV_CHUNK])
        x1n, r3 = _rms_stats(x1_ref[...])
        d1, dg3 = _rms_bwd(dh2, x1n, r3, g3_ref[...])
        dx1 = dx2_ref[...] + d1
        dx1_ref[...] = dx1
        dg3_ref[...] += dg3
        on, r2 = _rms_stats(o_ref[...])
        do, dg2 = _rms_bwd(dx1, on, r2, g2_ref[...])
        dg2_ref[...] += dg2
        dob = do.astype(BF16)
        do_ref[...] = dob
        dmixed = _dot_nt(dob, wout_ref[...])
        ga = gate_ref[:, 0:D_MODEL]
        gb = gate_ref[:, D_MODEL:2 * D_MODEL]
        dpa = (dmixed * ga).astype(BF16)
        dpb = (dmixed * gb).astype(BF16)
        dpa_ref[...] = dpa
        dpb_ref[...] = dpb
        dga = dmixed * pa_ref[...] * ga * (1.0 - ga)
        dgb = dmixed * pb_ref[...] * gb * (1.0 - gb)
        dgt_ref[:, 0:D_MODEL] = dga.astype(BF16)
        dgt_ref[:, D_MODEL:2 * D_MODEL] = dgb.astype(BF16)
        dbg_ref[:, 0:D_MODEL] += jnp.sum(dga, axis=0, keepdims=True)
        dbg_ref[:, D_MODEL:2 * D_MODEL] += jnp.sum(dgb, axis=0, keepdims=True)
        dya_ref[...] = _dot_nt(dpa, wba_ref[...]).astype(BF16)
        dyb = _dot_nt(dpb, wbb_ref[...]).astype(BF16)
        dyb_ref[...] = dyb
        prod = yb_ref[...].astype(F32) * dyb.astype(F32)
        lane_head = lax.broadcasted_iota(jnp.int32, (HEADS, HEADS * LANES), 1) // LANES
        sel = (lane_head == lax.broadcasted_iota(jnp.int32, (HEADS, HEADS * LANES), 0)).astype(BF16)
        hi = prod.astype(BF16)
        lo = (prod - hi.astype(F32)).astype(BF16)
        dl_ref[...] = _dot_nt(sel, hi) + _dot_nt(sel, lo)

    def o_(n, dt):
        return (_sds((t_rows, n), dt), _row(tm, n))

    def acc(r, n):
        return (_sds((r, n), F32), _full((r, n)))

    halo = pl.BlockSpec((halo_rows, cdim), lambda i: (jnp.minimum((i + 1) * hb, last_blk), 0))
    ins = [(du, _row(tm, cdim)), (du, halo), (convw8, _full(convw8.shape)), (wup, _resident(wup.shape)),
           (dx2, _row(tm, D_MODEL)), (x1, _row(tm, D_MODEL)), (g3, _full(g3.shape)), (o, _row(tm, D_MODEL)),
           (g2, _full(g2.shape)), (wout, _resident(wout.shape)), (gate, _row(tm, 2048)), (pa, _row(tm, D_MODEL)),
           (pb, _row(tm, D_MODEL)), (wba, _resident(wba.shape)), (wbb, _resident(wbb.shape)), (yb, _row(tm, 1024))]
    outs = [o_(cdim, BF16), o_(D_MODEL, F32), o_(D_MODEL, BF16), o_(D_MODEL, BF16), o_(D_MODEL, BF16),
            o_(2048, BF16), o_(1024, BF16), o_(1024, BF16),
            (_sds((HEADS, t_rows), F32), pl.BlockSpec((HEADS, tm), lambda i: (0, i))),
            acc(1, D_MODEL), acc(1, D_MODEL), acc(1, 2048)]
    return _rows_call("bwd_mid", body, t_rows, tm, ins, outs)


def _bwd_in(dqs, dks, dvs, dqm, dkm, dvm, tabs, consts, cq, ckv, gq, gkv, wuq, wk, wv, dgates, win, x, g1, dx1, tm):
    t_rows = x.shape[0]

    def body(dqs_ref, dks_ref, dvs_ref, dqm_ref, dkm_ref, dvm_ref, ca, sa1, sa2, cb, sb1, sb2, c_ref, cq_ref,
             ckv_ref, gq_ref, gkv_ref, wuq_ref, wk_ref, wv_ref, dgt_ref, win_ref, x_ref, g1_ref, dx1_ref,
             dz_ref, dqb_ref, dx_ref, dgq_ref, dgkv_ref, dg1_ref):
        i = pl.program_id(0)

        @pl.when(i == 0)
        def _():
            dgq_ref[...] = jnp.zeros(dgq_ref.shape, F32)
            dgkv_ref[...] = jnp.zeros(dgkv_ref.shape, F32)
            dg1_ref[...] = jnp.zeros(dg1_ref.shape, F32)

        ta = (ca[...], sa1[...], sa2[...])
        tb = (cb[...], sb1[...], sb2[...])
        dz_ref[:, Z_QA:Z_KA] = _rope_t(dqs_ref[...] * SCALE_A, *ta, A_HEAD_DIM // 2).astype(BF16)
        dz_ref[:, Z_KA:Z_VA] = _rope_t(dks_ref[...], *ta, A_HEAD_DIM // 2).astype(BF16)
        dz_ref[:, Z_VA:Z_CQ] = dvs_ref[...].astype(BF16)
        dqm = jnp.concatenate([dqm_ref[h] for h in range(HEADS)], axis=1)
        dqb = _rope_t(dqm * SCALE_B, *tb, ROPE_DIM // 2).astype(BF16)
        dqb_ref[...] = dqb
        dcqn = _dot_nt(dqb, wuq_ref[...])
        cqn, rq = _rms_stats(cq_ref[...])
        dcq, dgq = _rms_bwd(dcqn, cqn, rq, gq_ref[...])
        dgq_ref[...] += dgq
        dz_ref[:, Z_CQ:Z_CKV] = dcq.astype(BF16)
        dkm = dkm_ref[...]
        dslot = dkm[:, 0:LANES]
        for h in range(1, HEADS):
            dslot = dslot + dkm[:, h * LANES:(h + 1) * LANES]
        dz_ref[:, Z_KR:Z_GATE] = _rope_t(dslot * c_ref[10:11, :], *tb, ROPE_DIM // 2).astype(BF16)
        dckvn = _dot_nt(dkm.astype(BF16), wk_ref[...]) + _dot_nt(dvm_ref[...].astype(BF16), wv_ref[...])
        ckvn, rkv = _rms_stats(ckv_ref[...])
        dckv, dgkv = _rms_bwd(dckvn, ckvn, rkv, gkv_ref[...])
        dgkv_ref[...] += dgkv
        dz_ref[:, Z_CKV:Z_KR] = dckv.astype(BF16)
        dz_ref[:, Z_GATE:ZW] = dgt_ref[...]
        dh1 = _dot_nt(dz_ref[...], win_ref[...])
        xn, r1 = _rms_stats(x_ref[...])
        d0, dg1 = _rms_bwd(dh1, xn, r1, g1_ref[...])
        dg1_ref[...] += dg1
        dx_ref[...] = dx1_ref[...] + d0

    def acc(n):
        return (_sds((1, n), F32), _full((1, n)))

    ins = [(dqs, _row(tm, 1024)), (dks, _row(tm, 256)), (dvs, _row(tm, 256)), (dqm, _heads(tm, HEADS)),
           (dkm, _row(tm, 1024)), (dvm, _row(tm, 1024))] + [(t, _row(tm, LANES)) for t in tabs] + [
           (consts, _full(consts.shape)), (cq, _row(tm, 256)), (ckv, _row(tm, 128)), (gq, _full(gq.shape)),
           (gkv, _full(gkv.shape)), (wuq, _full(wuq.shape)), (wk, _full(wk.shape)), (wv, _full(wv.shape)),
           (dgates, _row(tm, 2048)), (win, _resident(win.shape)), (x, _row(tm, D_MODEL)), (g1, _full(g1.shape)),
           (dx1, _row(tm, D_MODEL))]
    outs = [(_sds((t_rows, ZW), BF16), _row(tm, ZW)), (_sds((t_rows, 1024), BF16), _row(tm, 1024)),
            (_sds((t_rows, D_MODEL), F32), _row(tm, D_MODEL)), acc(256), acc(128), acc(D_MODEL)]
    return _rows_call("bwd_in", body, t_rows, tm, ins, outs)


def _pick_cols(n):
    best = LANES
    for d in range(LANES, min(n, 1408) + 1, LANES):
        if n % d == 0:
            best = d
    return best


def _mm_tn(name, a, b, column_shards=1):
    t_rows, m = a.shape
    n = b.shape[1]
    bk = min(1024, t_rows)
    bm, bn = _pick_cols(m), _pick_cols(n // column_shards)
    per_shard = n // column_shards // bn

    def body(a_ref, b_ref, o_ref):
        @pl.when(pl.program_id(2) == 0)
        def _():
            o_ref[...] = jnp.zeros((bm, bn), F32)

        o_ref[...] += _dot_tn(a_ref[...].astype(BF16), b_ref[...].astype(BF16))

    return pl.pallas_call(
        body, name=name, grid=(m // bm, n // bn, t_rows // bk),
        in_specs=[pl.BlockSpec((bk, bm), lambda i, j, k: (k, i)), pl.BlockSpec((bk, bn), lambda i, j, k: (k, j))],
        out_specs=(pl.BlockSpec((bm, bn), lambda i, j, k: (i, j)) if column_shards == 1 else
                   pl.BlockSpec((None, bm, bn), lambda i, j, k: (j // per_shard, i, j % per_shard))),
        out_shape=_sds((m, n) if column_shards == 1 else (column_shards, m, n // column_shards), F32),
        compiler_params=pltpu.CompilerParams(dimension_semantics=("arbitrary",) * 3, vmem_limit_bytes=VMEM_LIMIT),
    )(a, b)


PACK_ROWS = 512


ADD_TILE_ELEMS = 1 << 17


def _add_rows(rows, cols):
    best = 16
    for d in range(16, rows + 1, 16):
        if rows % d == 0 and d * cols <= ADD_TILE_ELEMS:
            best = d
    assert rows % best == 0
    return best


def _add_pair(name, g, recv, half):
    _, _, rows, cols = g.shape
    t = _add_rows(rows, cols)

    def body(h_ref, g_ref, r_ref, o_ref):
        o_ref[...] = (g_ref[:, 0] + r_ref[...]).astype(BF16)

    spec = pl.BlockSpec((4, t, cols), lambda i, h: (0, i, 0))
    grid_spec = pltpu.PrefetchScalarGridSpec(
        num_scalar_prefetch=1, grid=(rows // t,),
        in_specs=[pl.BlockSpec((4, 1, t, cols), lambda i, h: (0, h[0], i, 0)), spec], out_specs=spec)
    return pl.pallas_call(body, name=name, grid_spec=grid_spec,
                          out_shape=_sds(recv.shape, BF16))(jnp.reshape(half, (1,)).astype(jnp.int32), g, recv)


def _add_chips(name, parts):
    _, rows, cols = parts.shape
    t = _add_rows(rows, cols)

    def body(p_ref, o_ref):
        acc = p_ref[0].astype(F32)
        for j in range(1, 4):
            acc = acc + p_ref[j].astype(F32)
        o_ref[...] = acc

    return pl.pallas_call(body, name=name, grid=(rows // t,),
                          in_specs=[pl.BlockSpec((4, t, cols), lambda i: (0, i, 0))],
                          out_specs=pl.BlockSpec((t, cols), lambda i: (i, 0)),
                          out_shape=_sds((rows, cols), F32))(parts)


def _add_devices(parts):
    n, rows, _ = parts.shape

    def body(p_ref, o_ref):
        acc = p_ref[0]
        for j in range(1, n):
            acc = acc + p_ref[j]
        o_ref[...] = acc

    return pl.pallas_call(body, name="small_add", grid=(1,),
                          in_specs=[pl.BlockSpec((n, rows, LANES), lambda i: (0, 0, 0))],
                          out_specs=pl.BlockSpec((rows, LANES), lambda i: (0, 0)),
                          out_shape=_sds((rows, LANES), F32))(parts)


def _adam_rows(k, n):
    target = max(8, (1 << 20) // (4 * n))
    if k <= target:
        return k
    best = None
    for d in range(8, target + 1, 8):
        if k % d == 0:
            best = d
    return best if best is not None else k


def _adamw(name, w, g, m, v):
    k, n = w.shape
    bk = _adam_rows(k, n)
    c1 = 1.0 - ADAM_B1 ** ADAM_STEP
    c2 = 1.0 - ADAM_B2 ** ADAM_STEP

    def body(w_ref, g_ref, m_ref, v_ref, d_ref, mo_ref, vo_ref):
        g_ = g_ref[...]
        m_ = ADAM_B1 * m_ref[...] + (1.0 - ADAM_B1) * g_
        v_ = ADAM_B2 * v_ref[...] + (1.0 - ADAM_B2) * (g_ * g_)
        mo_ref[...] = m_
        vo_ref[...] = v_
        d_ref[...] = -ADAM_LR * ((m_ / c1) / (jnp.sqrt(v_ / c2) + ADAM_EPS) + ADAM_WD * w_ref[...])

    spec = pl.BlockSpec((bk, n), lambda i: (i, 0))
    return pl.pallas_call(body, name=name, grid=(k // bk,), in_specs=[spec] * 4, out_specs=[spec] * 3,
                          out_shape=[_sds((k, n), F32)] * 3,
                          compiler_params=pltpu.CompilerParams(vmem_limit_bytes=VMEM_LIMIT))(w, g, m, v)


_HBM = pl.BlockSpec(memory_space=pltpu.HBM)


def _me():
    return lax.axis_index("x"), lax.axis_index("y"), lax.axis_index("c")


def _other_chips(x, y):
    return [(1 - x, y), (x, 1 - y), (1 - x, 1 - y)]


def _gather_weights(shards):
    n = len(shards)

    def body(*refs):
        x_refs, out_refs = refs[:n], refs[n:2 * n]
        send_sems, recv_sems = refs[2 * n:]
        x, y, c = _me()
        sibling = (x, y, 1 - c)
        chips = _other_chips(x, y)

        def copy(k, src, dst, to):
            return pltpu.make_async_remote_copy(src_ref=src, dst_ref=dst, send_sem=send_sems.at[k],
                                                recv_sem=recv_sems.at[k], device_id=to, device_id_type=MESH)

        first, passed = [], []
        for a, (x_ref, out_ref) in enumerate(zip(x_refs, out_refs)):
            for j, (cx, cy) in enumerate(chips):
                first.append(copy(6 * a + j, x_ref.at[c], out_ref.at[2 * x + y, c], (cx, cy, c)))
        for cp in first:
            cp.start()
        for a, (x_ref, out_ref) in enumerate(zip(x_refs, out_refs)):
            for j, (cx, cy) in enumerate(chips):
                landed = out_ref.at[2 * cx + cy, c]
                copy(6 * a + j, x_ref.at[c], landed, (cx, cy, c)).wait_recv()
                passed.append(copy(6 * a + 3 + j, landed, landed, sibling))
                passed[-1].start()
        for a, (x_ref, out_ref) in enumerate(zip(x_refs, out_refs)):
            for j, (cx, cy) in enumerate(chips):
                theirs = out_ref.at[2 * cx + cy, 1 - c]
                copy(6 * a + 3 + j, theirs, theirs, sibling).wait_recv()
        for cp in first + passed:
            cp.wait_send()

    return pl.pallas_call(
        body, name="gather_weights", out_shape=[_sds((4,) + s.shape, s.dtype) for s in shards],
        in_specs=[_HBM] * n, out_specs=[_HBM] * n,
        scratch_shapes=[pltpu.SemaphoreType.DMA((6 * n,)), pltpu.SemaphoreType.DMA((6 * n,))],
    )(*shards)


def _swap_sibling(name, vs, other_half=False):
    n = len(vs)

    def body(*refs):
        v_refs, out_refs = refs[:n], refs[n:2 * n]
        send_sems, recv_sems = refs[2 * n:]
        x, y, c = _me()
        cps = [pltpu.make_async_remote_copy(src_ref=v_ref.at[:, 1 - c] if other_half else v_ref, dst_ref=out_ref,
                                            send_sem=send_sems.at[a], recv_sem=recv_sems.at[a],
                                            device_id=(x, y, 1 - c), device_id_type=MESH)
               for a, (v_ref, out_ref) in enumerate(zip(v_refs, out_refs))]
        for cp in cps:
            cp.start()
        for cp in cps:
            cp.wait()

    def landing(v):
        return _sds((v.shape[0],) + v.shape[2:] if other_half else v.shape, v.dtype)

    return pl.pallas_call(
        body, name=name, out_shape=[landing(v) for v in vs], in_specs=[_HBM] * n, out_specs=[_HBM] * n,
        scratch_shapes=[pltpu.SemaphoreType.DMA((n,)), pltpu.SemaphoreType.DMA((n,))],
    )(*vs)


_SEM = pl.BlockSpec(memory_space=pltpu.SEMAPHORE)
_EFFECT = pltpu.SideEffectType.DATAFLOW_SIDE_EFFECTING
WHOLE = "whole"
PIECE = "piece"
SIBLING_HALF = "sibling"


def _landing_shape(v, mode):
    return {WHOLE: (4,) + v.shape, PIECE: v.shape, SIBLING_HALF: (v.shape[0],) + v.shape[2:]}[mode]


def _chip_copies(v_ref, land_ref, send_sems, recv_sems, mode, sem0=0):
    x, y, c = _me()
    if mode == SIBLING_HALF:
        cp = pltpu.make_async_remote_copy(src_ref=v_ref.at[:, 1 - c], dst_ref=land_ref, send_sem=send_sems.at[sem0],
                                          recv_sem=recv_sems.at[sem0], device_id=(x, y, 1 - c), device_id_type=MESH)
        return [(cp, cp)]
    k = 2 * x + y
    out = []
    for j, (cx, cy) in enumerate(_other_chips(x, y)):
        src = v_ref.at[2 * cx + cy] if mode == PIECE else v_ref
        sems = dict(send_sem=send_sems.at[sem0 + j], recv_sem=recv_sems.at[sem0 + j], device_id=(cx, cy, c),
                    device_id_type=MESH)
        send = pltpu.make_async_remote_copy(src_ref=src, dst_ref=land_ref.at[k], **sems)
        recv = pltpu.make_async_remote_copy(src_ref=src, dst_ref=land_ref.at[2 * cx + cy], **sems)
        out.append((send, recv))
    return out


def _chips_start(name, vs, mode, after=None):
    n = len(vs)
    lands = [_landing_shape(v, mode) for v in vs]

    def body(*refs):
        v_refs, land_refs = refs[:n], refs[n:2 * n]
        send_sems, recv_sems = refs[-2 * n - 3], refs[-2 * n - 2]
        token = refs[-1]
        for a in range(n):
            for send, _ in _chip_copies(v_refs[a], land_refs[a], send_sems, recv_sems, mode, 3 * a):
                send.start()
        token[...] = jnp.zeros_like(token)

    extra = () if after is None else (after,)
    hbm = [pltpu.with_memory_space_constraint(v, pltpu.HBM) for v in vs]
    zones = [pltpu.with_memory_space_constraint(lax.empty(s, v.dtype), pltpu.HBM) for s, v in zip(lands, vs)]
    out = pl.pallas_call(
        body, name=name,
        out_shape=(pltpu.SemaphoreType.DMA((3 * n,)), pltpu.SemaphoreType.DMA((3 * n,)),
                   *[pltpu.HBM(v.shape, v.dtype) for v in vs], *[pltpu.HBM(s, v.dtype) for s, v in zip(lands, vs)],
                   _sds((8, LANES), F32)),
        in_specs=(_HBM,) * (2 * n) + (pl.BlockSpec(memory_space=pl.ANY),) * len(extra),
        out_specs=(_SEM, _SEM) + (_HBM,) * (2 * n) + (pl.BlockSpec(memory_space=pltpu.VMEM),),
        input_output_aliases={i: 2 + i for i in range(2 * n)},
        compiler_params=pltpu.CompilerParams(has_side_effects=_EFFECT),
    )(*hbm, *zones, *extra)
    return out[0], out[1], list(out[2:2 + n]), list(out[2 + n:2 + 2 * n]), out[-1]


def _chips_wait(name, send_sems, recv_sems, v_thru, land_thru, mode, after):
    n = len(v_thru)

    def body(*refs):
        v_refs, land_refs = refs[:n], refs[n:2 * n]
        send_sems, recv_sems = refs[2 * n], refs[2 * n + 1]
        for a in range(n):
            for send, recv in _chip_copies(v_refs[a], land_refs[a], send_sems, recv_sems, mode, 3 * a):
                send.wait_send()
                recv.wait_recv()

    out = pl.pallas_call(
        body, name=name,
        out_shape=tuple(pltpu.HBM(a.shape, a.dtype) for a in list(v_thru) + list(land_thru)),
        in_specs=(_HBM,) * (2 * n) + (_SEM, _SEM, pl.BlockSpec(memory_space=pl.ANY)), out_specs=(_HBM,) * (2 * n),
        input_output_aliases={i: i for i in range(2 * n)},
        compiler_params=pltpu.CompilerParams(has_side_effects=_EFFECT),
    )(*v_thru, *land_thru, send_sems, recv_sems, after)
    return list(out[:n]), list(out[n:])


def _gather_small(name, v):
    def body(v_ref, out_ref, send_sems, recv_sems, local_sem):
        x, y, c = _me()
        me = 4 * x + 2 * y + c
        mine = pltpu.make_async_copy(v_ref, out_ref.at[me], local_sem)
        mine.start()
        peers = []
        for f in range(1, 8):
            fx, fy, fc = (f >> 2) & 1, (f >> 1) & 1, f & 1
            px = 1 - x if fx else x
            py = 1 - y if fy else y
            pc = 1 - c if fc else c
            peers.append((f - 1, (px, py, pc)))
        sends = [pltpu.make_async_remote_copy(src_ref=v_ref, dst_ref=out_ref.at[me], send_sem=send_sems.at[k],
                                              recv_sem=recv_sems.at[k], device_id=peer, device_id_type=MESH)
                 for k, peer in peers]
        for cp in sends:
            cp.start()
        for k, (px, py, pc) in peers:
            pltpu.make_async_remote_copy(src_ref=v_ref, dst_ref=out_ref.at[4 * px + 2 * py + pc],
                                         send_sem=send_sems.at[k], recv_sem=recv_sems.at[k],
                                         device_id=(px, py, pc), device_id_type=MESH).wait_recv()
        for cp in sends:
            cp.wait_send()
        mine.wait()

    return pl.pallas_call(
        body, name=name, out_shape=_sds((8,) + v.shape, v.dtype), in_specs=[_HBM], out_specs=_HBM,
        scratch_shapes=[pltpu.SemaphoreType.DMA((7,)), pltpu.SemaphoreType.DMA((7,)), pltpu.SemaphoreType.DMA],
    )(v)


_BIG = (("w_in", (1024, 3232), 1), ("w_uq", (256, 768), 1), ("w_ukv", (128, 1024), 1), ("w_branch_a", (512, 1024), 1),
        ("w_branch_b", (512, 1024), 1), ("w_out", (1024, 1024), 0), ("w_up", (1024, 5632), 1),
        ("w_down", (2816, 1024), 0), ("w_ple_gate", (1024, 1024), 0), ("w_ple", (256, 1024), 1))


def _shard_shape(shape, axis):
    return (shape[0] // 4, shape[1]) if axis == 0 else (shape[0], shape[1] // 4)


def _half_rows(shape, axis):
    k, n = _shard_shape(shape, axis)
    return k * n // (2 * LANES)


_EARLY = ("w_in", "w_uq", "w_ukv")
_LATE = ("w_branch_a", "w_branch_b", "w_out", "w_up", "w_down", "w_ple_gate", "w_ple")
_NATURAL = ("w_in", "w_up", "w_down", "w_out", "w_ple_gate")
_EARLY_PACKED = tuple(b for b in _BIG if b[0] in _EARLY and b[0] not in _NATURAL)
_LATE_PACKED = tuple(b for b in _BIG if b[0] in _LATE and b[0] not in _NATURAL)
_SHARD = {name: _shard_shape(shape, axis) for name, shape, axis in _BIG}


def _halves(a):
    return a.reshape(a.shape[:-2] + (2, a.shape[-2] // 2, a.shape[-1]))


def _rows_joined(a):
    return a.reshape(a.shape[:-3] + (a.shape[-3] * a.shape[-2], a.shape[-1]))


def _pack_pad(group):
    return -sum(_half_rows(shape, axis) for _, shape, axis in group) % PACK_ROWS


def _pack_shards(shards, dtype, group):
    parts = [shards[name].astype(dtype).reshape(2, _half_rows(shape, axis), LANES) for name, shape, axis in group]
    return jnp.concatenate(parts + [jnp.zeros((2, _pack_pad(group), LANES), dtype)], axis=1)


def _unpack_gathered(g, group):
    out, off = {}, 0
    for name, shape, axis in group:
        r = _half_rows(shape, axis)
        k, n = _shard_shape(shape, axis)
        w = g[:, :, off:off + r, :].reshape(4, k, n)
        out[name] = w.reshape(shape) if axis == 0 else w.transpose(1, 0, 2).reshape(shape)
        off += r
    return out


def _pack_grads(grads, group):
    parts = []
    for name, shape, axis in group:
        k, n = _shard_shape(shape, axis)
        g = grads[name]
        g4 = g.reshape(4, k, n) if axis == 0 else g.reshape(k, 4, n).transpose(1, 0, 2)
        parts.append(g4.reshape(4, 2, _half_rows(shape, axis), LANES))
    return jnp.concatenate(parts + [jnp.zeros((4, 2, _pack_pad(group), LANES), F32)], axis=2)


def _unpack_shard_grads(f, group):
    out, off = {}, 0
    for name, shape, axis in group:
        r = _half_rows(shape, axis)
        out[name] = f[:, off:off + r, :].reshape(_shard_shape(shape, axis))
        off += r
    return out


def _pad_slots(w, heads, dim, axis):
    if axis == 1:
        k = w.shape[0]
        return jnp.pad(w.reshape(k, heads, dim), ((0, 0), (0, 0), (0, LANES - dim))).reshape(k, heads * LANES)
    n = w.shape[1]
    return jnp.pad(w.reshape(heads, dim, n), ((0, 0), (0, LANES - dim), (0, 0))).reshape(heads * LANES, n)


def _unpad_slots(w, heads, dim, axis):
    if axis == 1:
        k = w.shape[0]
        return w.reshape(k, heads, LANES)[:, :, :dim].reshape(k, heads * dim)
    n = w.shape[1]
    return w.reshape(heads, LANES, n)[:, :dim, :].reshape(heads * dim, n)


def _pad_w_in(w):
    kr = jnp.pad(w[:, 1152:1184], ((0, 0), (NOPE_DIM, LANES - NOPE_DIM - ROPE_DIM)))
    return jnp.concatenate([_pad_slots(w[:, 0:512], HEADS, A_HEAD_DIM, 1),
                            _pad_slots(w[:, 512:640], A_KV_HEADS, A_HEAD_DIM, 1),
                            _pad_slots(w[:, 640:768], A_KV_HEADS, A_HEAD_DIM, 1),
                            w[:, 768:1024], w[:, 1024:1152], kr, w[:, 1184:3232]], axis=1)


def _unpad_w_in(w):
    return jnp.concatenate([_unpad_slots(w[:, Z_QA:Z_KA], HEADS, A_HEAD_DIM, 1),
                            _unpad_slots(w[:, Z_KA:Z_VA], A_KV_HEADS, A_HEAD_DIM, 1),
                            _unpad_slots(w[:, Z_VA:Z_CQ], A_KV_HEADS, A_HEAD_DIM, 1),
                            w[:, Z_CQ:Z_CKV], w[:, Z_CKV:Z_KR],
                            w[:, Z_KR + NOPE_DIM:Z_KR + NOPE_DIM + ROPE_DIM], w[:, Z_GATE:ZW]], axis=1)


_SMALL = (("attn_pre_norm", 1024), ("attn_post_norm", 1024), ("b_gate", 2048), ("sinks", 8), ("q_a_norm", 256),
          ("kv_a_norm", 128), ("mlp_pre_norm", 1024), ("mlp_post_norm", 1024), ("conv_b", 5632), ("ple_norm", 1024),
          ("conv_w", 3 * 5632), ("loss", 1))


def _small_rows(n):
    return 8 * -(-n // (8 * LANES))


def _pack_small(vals):
    parts = []
    for name, n in _SMALL:
        r = _small_rows(n)
        parts.append(jnp.pad(vals[name].reshape(-1), (0, r * LANES - n)).reshape(r, LANES))
    return jnp.concatenate(parts, axis=0)


def _unpack_small(buf):
    out, off = {}, 0
    for name, n in _SMALL:
        r = _small_rows(n)
        out[name] = buf[off:off + r].reshape(-1)[:n]
        off += r
    return out


def kernel(x, p, positions, attn_pre_norm, attn_post_norm, w_in, b_gate, sinks, q_a_norm, w_uq, kv_a_norm, w_ukv, w_branch_a, w_branch_b, w_out, mlp_pre_norm, mlp_post_norm, w_up, conv_w, conv_b, w_down, ple_norm, w_ple_gate, w_ple, loss_target, m_attn_pre_norm, m_attn_post_norm, m_w_in, m_b_gate, m_sinks, m_q_a_norm, m_w_uq, m_kv_a_norm, m_w_ukv, m_w_branch_a, m_w_branch_b, m_w_out, m_mlp_pre_norm, m_mlp_post_norm, m_w_up, m_conv_w, m_conv_b, m_w_down, m_ple_norm, m_w_ple_gate, m_w_ple, v_attn_pre_norm, v_attn_post_norm, v_w_in, v_b_gate, v_sinks, v_q_a_norm, v_w_uq, v_kv_a_norm, v_w_ukv, v_w_branch_a, v_w_branch_b, v_w_out, v_mlp_pre_norm, v_mlp_post_norm, v_w_up, v_conv_w, v_conv_b, v_w_down, v_ple_norm, v_w_ple_gate, v_w_ple):
    names = ["attn_pre_norm", "attn_post_norm", "w_in", "b_gate", "sinks", "q_a_norm", "w_uq", "kv_a_norm", "w_ukv",
             "w_branch_a", "w_branch_b", "w_out", "mlp_pre_norm", "mlp_post_norm", "w_up", "conv_w", "conv_b",
             "w_down", "ple_norm", "w_ple_gate", "w_ple"]
    wts = dict(zip(names, [attn_pre_norm, attn_post_norm, w_in, b_gate, sinks, q_a_norm, w_uq, kv_a_norm, w_ukv,
                           w_branch_a, w_branch_b, w_out, mlp_pre_norm, mlp_post_norm, w_up, conv_w, conv_b, w_down,
                           ple_norm, w_ple_gate, w_ple]))
    moms = dict(zip(names, [m_attn_pre_norm, m_attn_post_norm, m_w_in, m_b_gate, m_sinks, m_q_a_norm, m_w_uq,
                            m_kv_a_norm, m_w_ukv, m_w_branch_a, m_w_branch_b, m_w_out, m_mlp_pre_norm,
                            m_mlp_post_norm, m_w_up, m_conv_w, m_conv_b, m_w_down, m_ple_norm, m_w_ple_gate, m_w_ple]))
    vars_ = dict(zip(names, [v_attn_pre_norm, v_attn_post_norm, v_w_in, v_b_gate, v_sinks, v_q_a_norm, v_w_uq,
                             v_kv_a_norm, v_w_ukv, v_w_branch_a, v_w_branch_b, v_w_out, v_mlp_pre_norm,
                             v_mlp_post_norm, v_w_up, v_conv_w, v_conv_b, v_w_down, v_ple_norm, v_w_ple_gate, v_w_ple]))
    w2 = {n: a.reshape(a.shape[-2:]) for n, a in wts.items()}
    m2 = {n: a.reshape(a.shape[-2:]) for n, a in moms.items()}
    v2 = {n: a.reshape(a.shape[-2:]) for n, a in vars_.items()}

    t_rows = x.shape[-2]
    tm = min(256, t_rows)
    tm_wide = min(512, t_rows)
    xc, yc, cc = lax.axis_index("x"), lax.axis_index("y"), lax.axis_index("c")
    chip = 2 * xc + yc

    x2d = x.reshape(t_rows, D_MODEL)
    p2d = p.reshape(t_rows, PLE_DIM)
    tgt = loss_target.reshape(t_rows, D_MODEL)
    pos_f = positions.reshape(t_rows, 1).astype(F32)

    def own_slot_filled(gathered, mine):
        return [lax.dynamic_update_slice(g, m[None], (chip, 0, 0, 0)) for g, m in zip(gathered, mine)]

    def shard_lists(group, packed_group):
        return ([_halves(w2[n].astype(BF16)) for n in group if n in _NATURAL]
                + [_pack_shards(w2, BF16, packed_group)])

    cw_rows = 3 * 1408 // LANES
    conv_mine = jnp.pad(w2["conv_w"].reshape(cw_rows, LANES), ((0, 48 - cw_rows), (0, 0))).reshape(2, 24, LANES)
    early_mine = shard_lists(_EARLY, _EARLY_PACKED) + [conv_mine]
    late_mine = shard_lists(_LATE, _LATE_PACKED)
    early = own_slot_filled(_gather_weights(early_mine), early_mine)
    late_sems = _chips_start("gather_late_start", late_mine, WHOLE, after=early[0])
    late_token = late_sems[4][0:1, 0:1]
    full = _unpack_gathered(early[1], _EARLY_PACKED)
    full["w_in"] = _rows_joined(early[0]).transpose(1, 0, 2).reshape(D_MODEL, 3232)
    conv_full = early[2].reshape(4, 48, LANES)[:, :cw_rows].reshape(4, 3, 1408).transpose(1, 0, 2).reshape(3, 2 * D_FF)
    convw8 = jnp.pad(conv_full, ((0, 5), (0, 0)))

    win = _pad_w_in(full["w_in"])
    wuq = _pad_slots(full["w_uq"], HEADS, NOPE_DIM + ROPE_DIM, 1)
    ukv = full["w_ukv"].reshape(KV_LORA, HEADS, NOPE_DIM + V_DIM)
    wk = _pad_slots(ukv[:, :, :NOPE_DIM].reshape(KV_LORA, HEADS * NOPE_DIM), HEADS, NOPE_DIM, 1)
    wv = _pad_slots(ukv[:, :, NOPE_DIM:].reshape(KV_LORA, HEADS * V_DIM), HEADS, V_DIM, 1)
    g1, g2, g3, g4, g5 = (w2["attn_pre_norm"], w2["attn_post_norm"], w2["mlp_pre_norm"], w2["mlp_post_norm"],
                          w2["ple_norm"])
    gq, gkv, bg, convb = w2["q_a_norm"], w2["kv_a_norm"], w2["b_gate"], w2["conv_b"]
    swa_tile = min(SWA_TILE, t_rows)
    sink_rows = jnp.repeat(w2["sinks"].reshape(A_KV_HEADS, SWA_GROUP, 1), swa_tile, axis=2).reshape(
        A_KV_HEADS, 1, SWA_GROUP * swa_tile)
    swa_bias = _swa_bias(swa_tile)

    consts = _rope_consts()
    tabs = _rope_tables(pos_f, consts, tm)
    h1, qs, ks, vs, cq, cqn, ckv, ckvn, qm, km, vm, gate = _fwd_in(x2d, g1, win, bg + late_token, gq, gkv, wuq, wk, wv,
                                                                   tabs, tm_wide)
    ya, lse_a = _swa_fwd(qs, ks, vs, swa_bias, sink_rows)
    yb, lse_b = _mla_fwd(qm, km, vm)
    late_sent, late_landed = _chips_wait("gather_late_wait", *late_sems[:4], WHOLE, after=yb)
    late = own_slot_filled(late_landed, late_sent)
    full = _unpack_gathered(late[-1], _LATE_PACKED)
    wba = _pad_slots(full["w_branch_a"], HEADS, A_HEAD_DIM, 0)
    wbb = _pad_slots(full["w_branch_b"], HEADS, V_DIM, 0)
    wple = full["w_ple"]
    natural = dict(zip([n for n in _LATE if n in _NATURAL], late))
    wup = _rows_joined(natural["w_up"])
    wout, wdown, wpg = (_rows_joined(natural[n]).reshape(-1, D_MODEL) for n in ("w_out", "w_down", "w_ple_gate"))
    pa, pb, mixed, o, x1, h2 = _fwd_mix(x2d, ya, yb, gate, wba, wbb, wout, g2, g3, tm_wide)
    up, a = _fwd_up(h2, wup, convw8, convb, tm)
    ff, x2, e, n5, sg, dx3, loss_part = _fwd_out(a, wdown, x1, g4, p2d, wple, g5, wpg, tgt, tm_wide)

    dpre, de, dx2, dff, du, dg5, dg4, dconvb, dconvw8 = _bwd_out(dx3, e, sg, x2, ff, g5, g4, wpg, wdown, up, convw8,
                                                                 convb, tm)
    dup, dx1, do, dpa, dpb, dgates, dya, dyb, delta_b, dg3, dg2, dbg = _bwd_mid(
        du, convw8, wup, dx2, x1, g3, o, g2, wout, gate, pa, pb, wba, wbb, yb, tm)
    late_grads = {
        "w_branch_a": _unpad_slots(_mm_tn("dw_branch_a", ya, dpa), HEADS, A_HEAD_DIM, 0),
        "w_branch_b": _unpad_slots(_mm_tn("dw_branch_b", yb, dpb), HEADS, V_DIM, 0),
        "w_out": _mm_tn("dw_out", mixed, do).reshape(4, D_MODEL // 4, D_MODEL),
        "w_up": _mm_tn("dw_up", h2, dup, column_shards=4),
        "w_down": _mm_tn("dw_down", a, dff).reshape(4, D_FF // 4, D_MODEL),
        "w_ple_gate": _mm_tn("dw_ple_gate", n5, dpre).reshape(4, D_MODEL // 4, D_MODEL),
        "w_ple": _mm_tn("dw_ple", p2d, de),
    }

    def grad_views(grads, group, packed_group):
        return [_halves(grads[n]) for n in group if n in _NATURAL] + [_pack_grads(grads, packed_group)]

    def pair_sums(tag, views, theirs):
        return [_add_pair("rs_%s_add_pair_%d" % (tag, i), g, r, cc) for i, (g, r) in enumerate(zip(views, theirs))]

    swap_sems = _chips_start("swap_late_start", grad_views(late_grads, _LATE, _LATE_PACKED), SIBLING_HALF)
    dqs, dks, dvs, dsink_rows = _swa_bwd(qs, ks, vs, ya, dya, lse_a, swa_bias, sink_rows + swap_sems[4][0:1, 0:1])
    dsink = dsink_rows[:, 0:SWA_GROUP, 0]
    late_views, late_theirs = _chips_wait("swap_late_wait", *swap_sems[:4], SIBLING_HALF, after=dqs)
    rs_sems = _chips_start("scatter_late_start", pair_sums("late", late_views, late_theirs), PIECE)
    dqm, dkm, dvm = _mla_bwd(qm, km, vm, dyb, lse_b, delta_b.reshape(HEADS, 1, t_rows) + rs_sems[4][0:1, 0:1])
    dz, dqb, dx, dgq, dgkv, dg1 = _bwd_in(dqs, dks, dvs, dqm, dkm, dvm, tabs, consts, cq, ckv, gq, gkv, wuq, wk, wv,
                                           dgates, win, x2d, g1, dx1, tm)

    dwk = _unpad_slots(_mm_tn("dw_k", ckvn, dkm), HEADS, NOPE_DIM, 1).reshape(KV_LORA, HEADS, NOPE_DIM)
    dwv = _unpad_slots(_mm_tn("dw_v", ckvn, dvm), HEADS, V_DIM, 1).reshape(KV_LORA, HEADS, V_DIM)
    early_grads = {
        "w_in": _unpad_w_in(_mm_tn("dw_in", h1, dz)).reshape(D_MODEL, 4, 808).transpose(1, 0, 2),
        "w_uq": _unpad_slots(_mm_tn("dw_uq", cqn, dqb), HEADS, NOPE_DIM + ROPE_DIM, 1),
        "w_ukv": jnp.concatenate([dwk, dwv], axis=2).reshape(KV_LORA, HEADS * (NOPE_DIM + V_DIM)),
    }

    def finish(tag, pairs, landed, group, packed_group):
        reduced = []
        for i, (pair, land) in enumerate(zip(pairs, landed)):
            own = lax.dynamic_index_in_dim(pair, chip, 0, keepdims=True)
            reduced.append(_add_chips("rs_%s_add_chips_%d" % (tag, i),
                                      lax.dynamic_update_slice(land, own, (chip, 0, 0))))
        others = _swap_sibling("swap_%s_reduced_halves" % tag, reduced)
        both = [jnp.where(cc == 0, jnp.stack([r, o]), jnp.stack([o, r])) for r, o in zip(reduced, others)]
        out = _unpack_shard_grads(both[-1], packed_group)
        out.update({n: _rows_joined(b) for n, b in zip([n for n in group if n in _NATURAL], both)})
        return out

    small = {"attn_pre_norm": dg1, "attn_post_norm": dg2, "b_gate": dbg, "sinks": dsink, "q_a_norm": dgq,
             "kv_a_norm": dgkv, "mlp_pre_norm": dg3, "mlp_post_norm": dg4, "conv_b": dconvb, "ple_norm": dg5,
             "conv_w": dconvw8[0:3], "loss": loss_part}
    small_all = _gather_small("gather_small_grads", _pack_small(small))

    updates = {}

    def adamw(n, g):
        updates[n] = (g,) + tuple(_adamw("adamw_" + n, w2[n], g, m2[n], v2[n]))

    early_views = grad_views(early_grads, _EARLY, _EARLY_PACKED)
    early_theirs = _swap_sibling("swap_early_grad_halves", early_views, other_half=True)
    early_sems = _chips_start("scatter_early_start", pair_sums("early", early_views, early_theirs), PIECE,
                              after=small_all)
    late_pairs, late_landed = _chips_wait("scatter_late_wait", *rs_sems[:4], PIECE, after=early_sems[4])
    late_shards = finish("late", late_pairs, late_landed, _LATE, _LATE_PACKED)
    for n in _LATE:
        adamw(n, late_shards[n])
    early_pairs, early_landed = _chips_wait("scatter_early_wait", *early_sems[:4], PIECE,
                                            after=updates[_LATE[-1]][1])
    early_shards = finish("early", early_pairs, early_landed, _EARLY, _EARLY_PACKED)
    for n in _EARLY:
        adamw(n, early_shards[n])

    small_sum = _unpack_small(_add_devices(small_all))
    for n in names:
        if n == "conv_w":
            adamw(n, lax.dynamic_index_in_dim(small_sum[n].reshape(3, 4, 1408), chip, 1, keepdims=False))
        elif n in small_sum:
            adamw(n, small_sum[n].reshape(w2[n].shape))
    loss = small_sum["loss"][0]

    outs = [[updates[n][i].reshape(wts[n].shape) for n in names] for i in range(4)]
    return (loss, dx.reshape(x.shape), *outs[0], *outs[1], *outs[2], *outs[3])
```

```python
import functools
import math

import numpy as np
import jax
import jax.numpy as jnp
from jax import lax
from jax.experimental import pallas as pl
from jax.experimental.pallas import tpu as pltpu

F32 = jnp.float32
BF16 = jnp.bfloat16

D_MODEL = 1024
D_FF = 2816
PLE_DIM = 256
ROPE_THETA = 10000.0
RMS_EPS = 1e-6
SWA_WINDOW = 128
HEADS = 8
A_KV_HEADS = 2
A_HEAD_DIM = 64
Q_LORA = 256
KV_LORA = 128
NOPE_DIM = 64
ROPE_DIM = 32
V_DIM = 64
LANES = 128
ZW = 4096
NEG = -1e30
SCALE_A = A_HEAD_DIM ** -0.5
SCALE_B = (NOPE_DIM + ROPE_DIM) ** -0.5

ADAM_LR = 0.001
ADAM_B1 = 0.9
ADAM_B2 = 0.999
ADAM_EPS = 1e-08
ADAM_WD = 0.01
ADAM_STEP = 10

VMEM_LIMIT = 60 * 1024 * 1024
MESH_AXES = ("x", "y", "c")
MESH = pl.DeviceIdType.MESH

Z_QA, Z_KA, Z_VA, Z_CQ, Z_CKV, Z_KR, Z_GATE = 0, 1024, 1280, 1536, 1792, 1920, 2048


def _dot(a, b):
    return jnp.dot(a, b, preferred_element_type=F32)


def _dot_nt(a, b):
    return lax.dot_general(a, b, (((1,), (1,)), ((), ())), preferred_element_type=F32)


def _dot_tn(a, b):
    return lax.dot_general(a, b, (((0,), (0,)), ((), ())), preferred_element_type=F32)


def _rms_stats(x):
    r = lax.rsqrt(jnp.mean(x * x, axis=-1, keepdims=True) + RMS_EPS)
    return x * r, r


def _rms_bwd(dy, xn, r, g):
    dxn = dy * g
    dx = r * (dxn - xn * jnp.mean(dxn * xn, axis=-1, keepdims=True))
    dg = jnp.sum(dy * xn, axis=0, keepdims=True)
    return dx, dg


def _tile_lanes(t, n):
    return t if n == 1 else jnp.concatenate([t] * n, axis=1)


def _rope(x, c, s1, s2, half):
    w = x.shape[1]
    n = w // LANES
    return (x * _tile_lanes(c, n) + pltpu.roll(x, w - half, 1) * _tile_lanes(s1, n)
            + pltpu.roll(x, half, 1) * _tile_lanes(s2, n))


def _rope_t(dy, c, s1, s2, half):
    w = dy.shape[1]
    n = w // LANES
    return (dy * _tile_lanes(c, n) + pltpu.roll(dy * _tile_lanes(s1, n), half, 1)
            + pltpu.roll(dy * _tile_lanes(s2, n), w - half, 1))


def _sigmoid(x):
    return 1.0 / (1.0 + jnp.exp(-x))


_GELU_C = math.sqrt(2.0 / math.pi)


def _gelu_and_grad(x):
    x2 = x * x
    th = jnp.tanh(_GELU_C * (x + 0.044715 * x * x2))
    gel = 0.5 * x * (1.0 + th)
    dgel = 0.5 * (1.0 + th) + 0.5 * x * (1.0 - th * th) * (_GELU_C * (1.0 + 3.0 * 0.044715 * x2))
    return gel, dgel


def _conv_taps(up, h6, h7):
    r1 = pltpu.roll(up, 1, 0)
    r2 = pltpu.roll(up, 2, 0)
    rows = lax.broadcasted_iota(jnp.int32, (8, up.shape[1]), 0)
    xm1 = jnp.concatenate([jnp.where(rows == 0, h7, r1[0:8]), r1[8:]], axis=0)
    xm2 = jnp.concatenate([jnp.where(rows == 0, h6, jnp.where(rows == 1, h7, r2[0:8])), r2[8:]], axis=0)
    return xm1, xm2


def _conv_taps_next(du, n0, n1):
    tm = du.shape[0]
    r1 = pltpu.roll(du, tm - 1, 0)
    r2 = pltpu.roll(du, tm - 2, 0)
    rows = lax.broadcasted_iota(jnp.int32, (8, du.shape[1]), 0)
    xp1 = jnp.concatenate([r1[:tm - 8], jnp.where(rows == 7, n0, r1[tm - 8:])], axis=0)
    xp2 = jnp.concatenate([r2[:tm - 8], jnp.where(rows == 6, n0, jnp.where(rows == 7, n1, r2[tm - 8:]))], axis=0)
    return xp1, xp2


def _row(tm, n):
    return pl.BlockSpec((tm, n), lambda i: (i, 0))


def _full(shape):
    nd = len(shape)
    return pl.BlockSpec(tuple(shape), lambda i: (0,) * nd)


def _resident(shape):
    nd = len(shape)
    return pl.BlockSpec(tuple(shape), lambda i: (0,) * nd, pipeline_mode=pl.Buffered(1))


def _heads(tm, h):
    return pl.BlockSpec((h, tm, LANES), lambda i: (0, i, 0))


def _rows_call(name, body, t_rows, tm, ins, outs, scratch=()):
    return pl.pallas_call(
        body, name=name, grid=(t_rows // tm,),
        in_specs=[s for _, s in ins],
        out_specs=[s for _, s in outs],
        out_shape=[s for s, _ in outs],
        scratch_shapes=list(scratch),
        compiler_params=pltpu.CompilerParams(dimension_semantics=("arbitrary",), vmem_limit_bytes=VMEM_LIMIT),
    )(*[a for a, _ in ins])


def _sds(shape, dtype):
    return jax.ShapeDtypeStruct(tuple(shape), dtype)


def _rope_consts():
    c = np.zeros((16, LANES), np.float32)
    lane = np.arange(LANES)
    inv_a = (ROPE_THETA ** (-(np.arange(0, A_HEAD_DIM, 2, dtype=np.float32) / A_HEAD_DIM))).astype(np.float32)
    in_a = lane < A_HEAD_DIM
    c[0, in_a] = inv_a[lane[in_a] % (A_HEAD_DIM // 2)]
    c[1, in_a] = 1.0
    c[2, lane < A_HEAD_DIM // 2] = -1.0
    c[3, (lane >= A_HEAD_DIM // 2) & in_a] = 1.0
    inv_b = (ROPE_THETA ** (-(np.arange(0, ROPE_DIM, 2, dtype=np.float32) / ROPE_DIM))).astype(np.float32)
    pe = (lane >= NOPE_DIM) & (lane < NOPE_DIM + ROPE_DIM)
    c[5, pe] = inv_b[(lane[pe] - NOPE_DIM) % (ROPE_DIM // 2)]
    c[6, pe] = 1.0
    c[7, (lane >= NOPE_DIM) & (lane < NOPE_DIM + ROPE_DIM // 2)] = -1.0
    c[8, (lane >= NOPE_DIM + ROPE_DIM // 2) & (lane < NOPE_DIM + ROPE_DIM)] = 1.0
    c[9, lane < NOPE_DIM] = 1.0
    c[10, pe] = 1.0
    return jnp.asarray(c)


def _rope_tables(pos_f, consts, tm):
    t_rows = pos_f.shape[0]

    def body(pos_ref, c_ref, ca, sa1, sa2, cb, sb1, sb2):
        ang = pos_ref[...] * (c_ref[0:1, :] + c_ref[5:6, :])
        cs, sn = jnp.cos(ang), jnp.sin(ang)
        ca[...] = cs * c_ref[1:2, :]
        sa1[...] = sn * c_ref[2:3, :]
        sa2[...] = sn * c_ref[3:4, :]
        cb[...] = cs * c_ref[6:7, :] + c_ref[9:10, :]
        sb1[...] = sn * c_ref[7:8, :]
        sb2[...] = sn * c_ref[8:9, :]

    tab = (_sds((t_rows, LANES), F32), _row(tm, LANES))
    return _rows_call("rope_tables", body, t_rows, tm,
                      [(pos_f, _row(tm, 1)), (consts, _full(consts.shape))], [tab] * 6)


def _fwd_in(x, g1, win, bg, gq, gkv, wuq, wk, wv, tabs, tm):
    t_rows = x.shape[0]

    def body(x_ref, g1_ref, win_ref, bg_ref, gq_ref, gkv_ref, wuq_ref, wk_ref, wv_ref,
             ca, sa1, sa2, cb, sb1, sb2,
             h1_ref, qs_ref, ks_ref, vs_ref, cq_ref, cqn_ref, ckv_ref, ckvn_ref, qm_ref, km_ref, vm_ref, gate_ref):
        xn, _ = _rms_stats(x_ref[...])
        hb = (xn * g1_ref[...]).astype(BF16)
        h1_ref[...] = hb
        ta = (ca[...], sa1[...], sa2[...])
        tb = (cb[...], sb1[...], sb2[...])
        qs_ref[...] = (_rope(_dot(hb, win_ref[:, Z_QA:Z_KA]), *ta, A_HEAD_DIM // 2) * SCALE_A).astype(BF16)
        ks_ref[...] = _rope(_dot(hb, win_ref[:, Z_KA:Z_VA]), *ta, A_HEAD_DIM // 2).astype(BF16)
        vs_ref[...] = _dot(hb, win_ref[:, Z_VA:Z_CQ]).astype(BF16)
        cq = _dot(hb, win_ref[:, Z_CQ:Z_CKV])
        cq_ref[...] = cq
        cqn, _ = _rms_stats(cq)
        cqb = (cqn * gq_ref[...]).astype(BF16)
        cqn_ref[...] = cqb
        qm_ref[...] = (_rope(_dot(cqb, wuq_ref[...]), *tb, ROPE_DIM // 2) * SCALE_B).astype(BF16)
        ckv = _dot(hb, win_ref[:, Z_CKV:Z_KR])
        ckv_ref[...] = ckv
        ckvn, _ = _rms_stats(ckv)
        ckvb = (ckvn * gkv_ref[...]).astype(BF16)
        ckvn_ref[...] = ckvb
        kpe = _rope(_dot(hb, win_ref[:, Z_KR:Z_GATE]), *tb, ROPE_DIM // 2)
        km_ref[...] = (_dot(ckvb, wk_ref[...]) + _tile_lanes(kpe, HEADS)).astype(BF16)
        vm_ref[...] = _dot(ckvb, wv_ref[...]).astype(BF16)
        gate_ref[...] = _sigmoid(_dot(hb, win_ref[:, Z_GATE:ZW]) + bg_ref[...])

    def o(n, dt):
        return (_sds((t_rows, n), dt), _row(tm, n))

    ins = [(x, _row(tm, D_MODEL)), (g1, _full(g1.shape)), (win, _resident(win.shape)), (bg, _full(bg.shape)),
           (gq, _full(gq.shape)), (gkv, _full(gkv.shape)), (wuq, _full(wuq.shape)), (wk, _full(wk.shape)),
           (wv, _full(wv.shape))] + [(t, _row(tm, LANES)) for t in tabs]
    outs = [o(1024, BF16), o(1024, BF16), o(256, BF16), o(256, BF16), o(256, F32), o(256, BF16), o(128, F32),
            o(128, BF16), o(1024, BF16), o(1024, BF16), o(1024, BF16), o(2048, F32)]
    return _rows_call("fwd_in", body, t_rows, tm, ins, outs)


def _attn_tile(t_rows):
    return min(512, t_rows)


MLA_HEADS_PER_STEP = 2


def _causal_pairs(nq, by_kv):
    if by_kv:
        pairs = [(i, j) for j in range(nq) for i in range(j, nq)]
    else:
        pairs = [(i, j) for i in range(nq) for j in range(i + 1)]
    return (jnp.asarray([p[0] for p in pairs], jnp.int32), jnp.asarray([p[1] for p in pairs], jnp.int32))


def _mla_fwd(q, k, v):
    t_rows = q.shape[0]
    t = _attn_tile(t_rows)
    hp = MLA_HEADS_PER_STEP
    w = hp * LANES
    ii, jj = _causal_pairs(t_rows // t, by_kv=False)

    def body(i_ref, j_ref, q_ref, k_ref, v_ref, o_ref, lse_ref, m_s, l_s, acc_s):
        i = i_ref[pl.program_id(1)]
        j = j_ref[pl.program_id(1)]

        @pl.when(j == 0)
        def _():
            m_s[...] = jnp.full(m_s.shape, NEG, F32)
            l_s[...] = jnp.zeros(l_s.shape, F32)
            acc_s[...] = jnp.zeros(acc_s.shape, F32)

        def step(diagonal):
            for hh in range(hp):
                sl = slice(hh * LANES, (hh + 1) * LANES)
                s = _dot_nt(k_ref[:, sl], q_ref[:, sl])
                if diagonal:
                    valid = (lax.broadcasted_iota(jnp.int32, (t, t), 0) <= lax.broadcasted_iota(jnp.int32, (t, t), 1))
                    s = jnp.where(valid, s, NEG)
                m_prev = m_s[hh]
                m_new = jnp.maximum(m_prev, jnp.max(s, axis=0, keepdims=True))
                p = jnp.exp(s - m_new)
                alpha = jnp.exp(m_prev - m_new)
                l_new = alpha * l_s[hh] + jnp.sum(p, axis=0, keepdims=True)
                acc = alpha * acc_s[hh] + _dot_tn(v_ref[:, sl], p.astype(BF16))
                if diagonal:
                    o_ref[:, sl] = (acc / l_new).T.astype(o_ref.dtype)
                    lse_ref[hh] = m_new + jnp.log(l_new)
                else:
                    m_s[hh] = m_new
                    l_s[hh] = l_new
                    acc_s[hh] = acc

        pl.when(j < i)(lambda: step(False))
        pl.when(j == i)(lambda: step(True))

    grid_spec = pltpu.PrefetchScalarGridSpec(
        num_scalar_prefetch=2, grid=(HEADS // hp, ii.shape[0]),
        in_specs=[pl.BlockSpec((t, w), lambda hb, s, ir, jr: (ir[s], hb)),
                  pl.BlockSpec((t, w), lambda hb, s, ir, jr: (jr[s], hb)),
                  pl.BlockSpec((t, w), lambda hb, s, ir, jr: (jr[s], hb))],
        out_specs=[pl.BlockSpec((t, w), lambda hb, s, ir, jr: (ir[s], hb)),
                   pl.BlockSpec((hp, 1, t), lambda hb, s, ir, jr: (hb, 0, ir[s]))],
        scratch_shapes=[pltpu.VMEM((hp, 1, t), F32), pltpu.VMEM((hp, 1, t), F32), pltpu.VMEM((hp, LANES, t), F32)])
    return pl.pallas_call(
        body, name="mla_fwd", grid_spec=grid_spec,
        out_shape=[_sds((t_rows, HEADS * LANES), BF16), _sds((HEADS, 1, t_rows), F32)],
        compiler_params=pltpu.CompilerParams(dimension_semantics=("arbitrary",) * 2, vmem_limit_bytes=VMEM_LIMIT),
    )(ii, jj, q, k, v)


def _mla_bwd(q, k, v, do, lse, delta):
    t_rows = q.shape[0]
    t = _attn_tile(t_rows)
    hp = MLA_HEADS_PER_STEP
    w = hp * LANES
    ii, jj = _causal_pairs(t_rows // t, by_kv=True)

    def body(i_ref, j_ref, q_ref, k_ref, v_ref, do_ref, lse_ref, dl_ref, dq_ref, dk_ref, dv_ref):
        i = i_ref[pl.program_id(1)]
        j = j_ref[pl.program_id(1)]

        @pl.when(pl.program_id(1) == 0)
        def _():
            dq_ref[...] = jnp.zeros(dq_ref.shape, F32)

        def step(diagonal):
            r0 = pl.multiple_of(i * t, t)
            for hh in range(hp):
                sl = slice(hh * LANES, (hh + 1) * LANES)
                qv = q_ref[:, sl]
                kv = k_ref[:, sl]
                dov = do_ref[:, sl]
                s = _dot_nt(kv, qv)
                if diagonal:
                    valid = (lax.broadcasted_iota(jnp.int32, (t, t), 0) <= lax.broadcasted_iota(jnp.int32, (t, t), 1))
                    s = jnp.where(valid, s, NEG)
                p = jnp.exp(s - lse_ref[hh])
                dv = _dot(p.astype(BF16), dov)
                dp = _dot_nt(v_ref[:, sl], dov)
                ds = (p * (dp - dl_ref[hh])).astype(BF16)
                dk = _dot(ds, qv)
                if diagonal:
                    dv_ref[:, sl] = dv
                    dk_ref[:, sl] = dk
                else:
                    dv_ref[:, sl] += dv
                    dk_ref[:, sl] += dk
                dq_ref[hh, pl.ds(r0, t), :] += _dot_tn(ds, kv)

        pl.when(i > j)(lambda: step(False))
        pl.when(i == j)(lambda: step(True))

    def qmap(hb, s, ir, jr):
        return (ir[s], hb)

    def kvmap(hb, s, ir, jr):
        return (jr[s], hb)

    def rowmap(hb, s, ir, jr):
        return (hb, 0, ir[s])

    grid_spec = pltpu.PrefetchScalarGridSpec(
        num_scalar_prefetch=2, grid=(HEADS // hp, ii.shape[0]),
        in_specs=[pl.BlockSpec((t, w), qmap), pl.BlockSpec((t, w), kvmap), pl.BlockSpec((t, w), kvmap),
                  pl.BlockSpec((t, w), qmap), pl.BlockSpec((hp, 1, t), rowmap), pl.BlockSpec((hp, 1, t), rowmap)],
        out_specs=[pl.BlockSpec((hp, t_rows, LANES), lambda hb, s, ir, jr: (hb, 0, 0)),
                   pl.BlockSpec((t, w), kvmap), pl.BlockSpec((t, w), kvmap)])
    return pl.pallas_call(
        body, name="mla_bwd", grid_spec=grid_spec,
        out_shape=[_sds((HEADS, t_rows, LANES), F32), _sds((t_rows, HEADS * LANES), F32),
                   _sds((t_rows, HEADS * LANES), F32)],
        compiler_params=pltpu.CompilerParams(dimension_semantics=("arbitrary",) * 2, vmem_limit_bytes=VMEM_LIMIT),
    )(ii, jj, q, k, v, do, lse, delta)


SWA_TILE = 2 * SWA_WINDOW
SWA_GROUP = HEADS // A_KV_HEADS


def _swa_bias(tq):
    koff = lax.broadcasted_iota(jnp.int32, (tq + SWA_WINDOW, SWA_GROUP * tq), 0) - SWA_WINDOW
    qoff = (lax.broadcasted_iota(jnp.int32, (tq + SWA_WINDOW, SWA_GROUP * tq), 1) % tq)
    band = (koff <= qoff) & (qoff - koff < SWA_WINDOW)
    return jnp.stack([jnp.where(band & (koff >= 0), 0.0, NEG), jnp.where(band, 0.0, NEG)]).astype(F32)


def _swa_specs(tq, nq):
    wb = tq // SWA_WINDOW

    def qi(i):
        return jnp.minimum(i, nq - 1)

    q = pl.BlockSpec((tq, SWA_GROUP * LANES), lambda h, i: (qi(i), h))
    cur = pl.BlockSpec((tq, LANES), lambda h, i: (qi(i), h))
    prev = pl.BlockSpec((SWA_WINDOW, LANES), lambda h, i: (jnp.maximum(qi(i) * wb - 1, 0), h))
    bias = pl.BlockSpec((1, tq + SWA_WINDOW, SWA_GROUP * tq), lambda h, i: (jnp.minimum(i, 1), 0, 0))
    rows = pl.BlockSpec((1, 1, 1, SWA_GROUP * tq), lambda h, i: (h, qi(i), 0, 0))
    sink = pl.BlockSpec((1, 1, SWA_GROUP * tq), lambda h, i: (h, 0, 0))
    return q, cur, prev, bias, rows, sink


def _stack_heads(ref):
    return jnp.concatenate([ref[:, g * LANES:(g + 1) * LANES] for g in range(SWA_GROUP)], axis=0)


def _swa_fwd(q, k, v, bias, sink_rows):
    t_rows = q.shape[0]
    tq = min(SWA_TILE, t_rows)
    nq = t_rows // tq
    qs_, cur, prev, bs, rows, sk = _swa_specs(tq, nq)

    def body(q_ref, kc_ref, kp_ref, vc_ref, vp_ref, b_ref, sink_ref, o_ref, lse_ref):
        qs = _stack_heads(q_ref)
        kk = jnp.concatenate([kp_ref[...], kc_ref[...]], axis=0)
        vv = jnp.concatenate([vp_ref[...], vc_ref[...]], axis=0)
        s = _dot_nt(kk, qs) + b_ref[0]
        sink = sink_ref[0]
        m = jnp.maximum(jnp.max(s, axis=0, keepdims=True), sink)
        p = jnp.exp(s - m)
        l = jnp.sum(p, axis=0, keepdims=True) + jnp.exp(sink - m)
        o = (_dot_tn(vv, p.astype(BF16)) / l).T
        for g in range(SWA_GROUP):
            o_ref[:, g * LANES:(g + 1) * LANES] = o[g * tq:(g + 1) * tq].astype(o_ref.dtype)
        lse_ref[0, 0] = m + jnp.log(l)

    return pl.pallas_call(
        body, name="swa_fwd", grid=(A_KV_HEADS, nq),
        in_specs=[qs_, cur, prev, cur, prev, bs, sk],
        out_specs=[qs_, rows],
        out_shape=[_sds((t_rows, HEADS * LANES), BF16), _sds((A_KV_HEADS, nq, 1, SWA_GROUP * tq), F32)],
        compiler_params=pltpu.CompilerParams(dimension_semantics=("arbitrary",) * 2, vmem_limit_bytes=VMEM_LIMIT),
    )(q, k, k, v, v, bias, sink_rows)


def _swa_bwd(q, k, v, o, do, lse, bias, sink_rows):
    t_rows = q.shape[0]
    tq = min(SWA_TILE, t_rows)
    nq = t_rows // tq
    qs_, cur, prev, bs, rows, sk = _swa_specs(tq, nq)
    hw = SWA_WINDOW

    def body(q_ref, kc_ref, kp_ref, vc_ref, vp_ref, o_ref, do_ref, lse_ref, b_ref, sink_ref,
             dq_ref, dk_ref, dv_ref, dsink_ref, ck, cv, dsa):
        i = pl.program_id(1)

        @pl.when(i == 0)
        def _():
            dsa[...] = jnp.zeros(dsa.shape, F32)

        @pl.when(i < nq)
        def _():
            qs = _stack_heads(q_ref)
            dos = _stack_heads(do_ref)
            kk = jnp.concatenate([kp_ref[...], kc_ref[...]], axis=0)
            vv = jnp.concatenate([vp_ref[...], vc_ref[...]], axis=0)
            lse = lse_ref[0, 0]
            p = jnp.exp(_dot_nt(kk, qs) + b_ref[0] - lse)
            delta = jnp.sum((_stack_heads(o_ref).astype(F32) * dos.astype(F32)).T, axis=0, keepdims=True)
            dsa[...] += -jnp.exp(sink_ref[0] - lse) * delta
            dv = _dot(p.astype(BF16), dos)
            ds = (p * (_dot_nt(vv, dos) - delta)).astype(BF16)
            dk = _dot(ds, qs)
            dq = _dot_tn(ds, kk)
            for g in range(SWA_GROUP):
                dq_ref[:, g * LANES:(g + 1) * LANES] = dq[g * tq:(g + 1) * tq]

            @pl.when(i > 0)
            def _():
                dk_ref[0:tq - hw, :] = ck[0:tq - hw, :]
                dk_ref[tq - hw:tq, :] = ck[tq - hw:tq, :] + dk[0:hw]
                dv_ref[0:tq - hw, :] = cv[0:tq - hw, :]
                dv_ref[tq - hw:tq, :] = cv[tq - hw:tq, :] + dv[0:hw]

            ck[...] = dk[hw:hw + tq]
            cv[...] = dv[hw:hw + tq]

        @pl.when(i == nq)
        def _():
            dk_ref[...] = ck[...]
            dv_ref[...] = cv[...]
            dsink_ref[...] = jnp.zeros(dsink_ref.shape, F32)
            for g in range(SWA_GROUP):
                tot = jnp.sum(dsa[:, g * tq:(g + 1) * tq], axis=1, keepdims=True)
                dsink_ref[0, g:g + 1, :] = jnp.zeros((1, LANES), F32) + tot

    kv_out = pl.BlockSpec((tq, LANES), lambda h, i: (jnp.maximum(i - 1, 0), h))
    return pl.pallas_call(
        body, name="swa_bwd", grid=(A_KV_HEADS, nq + 1),
        in_specs=[qs_, cur, prev, cur, prev, qs_, qs_, rows, bs, sk],
        out_specs=[qs_, kv_out, kv_out, pl.BlockSpec((1, 8, LANES), lambda h, i: (h, 0, 0))],
        out_shape=[_sds((t_rows, HEADS * LANES), F32), _sds((t_rows, A_KV_HEADS * LANES), F32),
                   _sds((t_rows, A_KV_HEADS * LANES), F32), _sds((A_KV_HEADS, 8, LANES), F32)],
        scratch_shapes=[pltpu.VMEM((tq, LANES), F32), pltpu.VMEM((tq, LANES), F32),
                        pltpu.VMEM((1, SWA_GROUP * tq), F32)],
        compiler_params=pltpu.CompilerParams(dimension_semantics=("arbitrary",) * 2, vmem_limit_bytes=VMEM_LIMIT),
    )(q, k, k, v, v, o, do, lse, bias, sink_rows)


def _fwd_mix(x, ya, yb, gate, wba, wbb, wout, g2, g3, tm):
    t_rows = x.shape[0]

    def body(x_ref, ya_ref, yb_ref, gate_ref, wba_ref, wbb_ref, wout_ref, g2_ref, g3_ref,
             pa_ref, pb_ref, mixed_ref, o_ref, x1_ref, h2_ref):
        pa = _dot(ya_ref[...], wba_ref[...])
        pb = _dot(yb_ref[...], wbb_ref[...])
        pa_ref[...] = pa
        pb_ref[...] = pb
        mixed = (gate_ref[:, 0:D_MODEL] * pa + gate_ref[:, D_MODEL:2 * D_MODEL] * pb).astype(BF16)
        mixed_ref[...] = mixed
        o = _dot(mixed, wout_ref[...])
        o_ref[...] = o
        on, _ = _rms_stats(o)
        x1 = x_ref[...] + on * g2_ref[...]
        x1_ref[...] = x1
        x1n, _ = _rms_stats(x1)
        h2_ref[...] = (x1n * g3_ref[...]).astype(BF16)

    def o_(dt):
        return (_sds((t_rows, D_MODEL), dt), _row(tm, D_MODEL))

    ins = [(x, _row(tm, D_MODEL)), (ya, _row(tm, 1024)), (yb, _row(tm, 1024)), (gate, _row(tm, 2048)),
           (wba, _resident(wba.shape)), (wbb, _resident(wbb.shape)), (wout, _resident(wout.shape)),
           (g2, _full(g2.shape)), (g3, _full(g3.shape))]
    return _rows_call("fwd_mix", body, t_rows, tm, ins, [o_(F32), o_(F32), o_(BF16), o_(F32), o_(F32), o_(BF16)])


CONV_CHUNK = 1408


def _fwd_up(h2, wup, convw8, convb, tm):
    t_rows = h2.shape[0]
    cdim = 2 * D_FF

    def body(h2_ref, wup_ref, cw_ref, cb_ref, up_ref, a_ref, carry):
        i = pl.program_id(0)

        @pl.when(i == 0)
        def _():
            carry[...] = jnp.zeros(carry.shape, F32)

        hb = h2_ref[...]

        def conv(c0):
            sl = slice(c0, c0 + CONV_CHUNK)
            up = _dot(hb, wup_ref[c0 // CONV_CHUNK])
            up_ref[:, sl] = up
            xm1, xm2 = _conv_taps(up, carry[6:7, sl], carry[7:8, sl])
            u = cw_ref[0:1, sl] * xm2 + cw_ref[1:2, sl] * xm1 + cw_ref[2:3, sl] * up + cb_ref[:, sl]
            carry[:, sl] = up[tm - 8:tm, :]
            return u

        for c0 in range(0, D_FF, CONV_CHUNK):
            ug = conv(c0)
            uv = conv(D_FF + c0)
            gel, _ = _gelu_and_grad(ug)
            a_ref[:, c0:c0 + CONV_CHUNK] = (gel * uv).astype(BF16)

    ins = [(h2, _row(tm, D_MODEL)), (wup, _resident(wup.shape)), (convw8, _full(convw8.shape)), (convb, _full(convb.shape))]
    outs = [(_sds((t_rows, cdim), F32), _row(tm, cdim)), (_sds((t_rows, D_FF), BF16), _row(tm, D_FF))]
    return _rows_call("fwd_up", body, t_rows, tm, ins, outs, scratch=[pltpu.VMEM((8, cdim), F32)])


def _fwd_out(a, wdown, x1, g4, p, wple, g5, wpg, tgt, tm):
    t_rows = a.shape[0]

    def body(a_ref, wdown_ref, x1_ref, g4_ref, p_ref, wple_ref, g5_ref, wpg_ref, tgt_ref,
             ff_ref, x2_ref, e_ref, n5_ref, sg_ref, dx3_ref, loss_ref):
        i = pl.program_id(0)
        ff = _dot(a_ref[...], wdown_ref[...])
        ff_ref[...] = ff
        ffn, _ = _rms_stats(ff)
        x2 = x1_ref[...] + ffn * g4_ref[...]
        x2_ref[...] = x2
        e = _dot(p_ref[...].astype(BF16), wple_ref[...])
        e_ref[...] = e
        x2n, _ = _rms_stats(x2)
        n5 = (x2n * g5_ref[...]).astype(BF16)
        n5_ref[...] = n5
        sg = _sigmoid(_dot(n5, wpg_ref[...]))
        sg_ref[...] = sg
        d = x2 + sg * e - tgt_ref[...]
        dx3_ref[...] = d * (1.0 / D_MODEL)

        @pl.when(i == 0)
        def _():
            loss_ref[...] = jnp.zeros((1, 1), F32)

        loss_ref[...] += 0.5 * jnp.sum(jnp.sum(d * d, axis=1, keepdims=True), axis=0, keepdims=True) * (1.0 / D_MODEL)

    def o_(dt):
        return (_sds((t_rows, D_MODEL), dt), _row(tm, D_MODEL))

    ins = [(a, _row(tm, D_FF)), (wdown, _resident(wdown.shape)), (x1, _row(tm, D_MODEL)), (g4, _full(g4.shape)),
           (p, _row(tm, PLE_DIM)), (wple, _full(wple.shape)), (g5, _full(g5.shape)), (wpg, _resident(wpg.shape)),
           (tgt, _row(tm, D_MODEL))]
    outs = [o_(F32), o_(F32), o_(F32), o_(BF16), o_(F32), o_(F32), (_sds((1, 1), F32), _full((1, 1)))]
    return _rows_call("fwd_out", body, t_rows, tm, ins, outs)


def _bwd_out(dx3, e, sg, x2, ff, g5, g4, wpg, wdown, up, convw8, convb, tm):
    t_rows = dx3.shape[0]
    cdim = 2 * D_FF
    hb = tm // 8

    def body(dx3_ref, e_ref, sg_ref, x2_ref, ff_ref, g5_ref, g4_ref, wpg_ref, wdown_ref, up_ref, halo_ref, cw_ref,
             cb_ref, dpre_ref, de_ref, dx2_ref, dff_ref, du_ref, dg5_ref, dg4_ref, dcb_ref, dcw_ref):
        i = pl.program_id(0)

        @pl.when(i == 0)
        def _():
            dg5_ref[...] = jnp.zeros(dg5_ref.shape, F32)
            dg4_ref[...] = jnp.zeros(dg4_ref.shape, F32)
            dcb_ref[...] = jnp.zeros(dcb_ref.shape, F32)
            dcw_ref[...] = jnp.zeros(dcw_ref.shape, F32)

        dx3 = dx3_ref[...]
        sg = sg_ref[...]
        dpre = (dx3 * e_ref[...] * sg * (1.0 - sg)).astype(BF16)
        dpre_ref[...] = dpre
        de_ref[...] = (dx3 * sg).astype(BF16)
        dn5 = _dot_nt(dpre, wpg_ref[...])
        x2n, r5 = _rms_stats(x2_ref[...])
        d2, dg5 = _rms_bwd(dn5, x2n, r5, g5_ref[...])
        dx2 = dx3 + d2
        dx2_ref[...] = dx2
        dg5_ref[...] += dg5
        ffn, r4 = _rms_stats(ff_ref[...])
        dff, dg4 = _rms_bwd(dx2, ffn, r4, g4_ref[...])
        dg4_ref[...] += dg4
        dffb = dff.astype(BF16)
        dff_ref[...] = dffb
        keep = jnp.where(i > 0, 1.0, 0.0)

        def conv(c0):
            sl = slice(c0, c0 + CONV_CHUNK)
            up = up_ref[:, sl]
            xm1, xm2 = _conv_taps(up, halo_ref[6:7, sl] * keep, halo_ref[7:8, sl] * keep)
            u = cw_ref[0:1, sl] * xm2 + cw_ref[1:2, sl] * xm1 + cw_ref[2:3, sl] * up + cb_ref[:, sl]
            return u, up, xm1, xm2

        def grads(c0, du, up, xm1, xm2):
            sl = slice(c0, c0 + CONV_CHUNK)
            du_ref[:, sl] = du.astype(BF16)
            dcb_ref[:, sl] += jnp.sum(du, axis=0, keepdims=True)
            dcw_ref[0:1, sl] += jnp.sum(du * xm2, axis=0, keepdims=True)
            dcw_ref[1:2, sl] += jnp.sum(du * xm1, axis=0, keepdims=True)
            dcw_ref[2:3, sl] += jnp.sum(du * up, axis=0, keepdims=True)

        for c0 in range(0, D_FF, CONV_CHUNK):
            da = _dot_nt(dffb, wdown_ref[c0:c0 + CONV_CHUNK, :])
            ug, *rg = conv(c0)
            uv, *rv = conv(D_FF + c0)
            gel, dgel = _gelu_and_grad(ug)
            grads(c0, da * uv * dgel, *rg)
            grads(D_FF + c0, da * gel, *rv)

    def o_(n, dt):
        return (_sds((t_rows, n), dt), _row(tm, n))

    def acc(r, n):
        return (_sds((r, n), F32), _full((r, n)))

    halo = pl.BlockSpec((8, cdim), lambda i: (jnp.maximum(i * hb - 1, 0), 0))
    ins = [(dx3, _row(tm, D_MODEL)), (e, _row(tm, D_MODEL)), (sg, _row(tm, D_MODEL)), (x2, _row(tm, D_MODEL)),
           (ff, _row(tm, D_MODEL)), (g5, _full(g5.shape)), (g4, _full(g4.shape)), (wpg, _resident(wpg.shape)),
           (wdown, _resident(wdown.shape)), (up, _row(tm, cdim)), (up, halo), (convw8, _full(convw8.shape)),
           (convb, _full(convb.shape))]
    outs = [o_(D_MODEL, BF16), o_(D_MODEL, BF16), o_(D_MODEL, F32), o_(D_MODEL, BF16), o_(cdim, BF16),
            acc(1, D_MODEL), acc(1, D_MODEL), acc(1, cdim), acc(8, cdim)]
    return _rows_call("bwd_out", body, t_rows, tm, ins, outs)


def _bwd_mid(du, convw8, wup, dx2, x1, g3, o, g2, wout, gate, pa, pb, wba, wbb, yb, tm):
    t_rows = du.shape[0]
    cdim = 2 * D_FF
    halo_rows = 16
    hb = tm // halo_rows
    last_blk = t_rows // halo_rows - 1
    n_tiles = t_rows // tm

    def body(du_ref, halo_ref, cw_ref, wup_ref, dx2_ref, x1_ref, g3_ref, o_ref, g2_ref, wout_ref, gate_ref, pa_ref,
             pb_ref, wba_ref, wbb_ref, yb_ref,
             dup_ref, dx1_ref, do_ref, dpa_ref, dpb_ref, dgt_ref, dya_ref, dyb_ref, dl_ref, dg3_ref, dg2_ref, dbg_ref):
        i = pl.program_id(0)

        @pl.when(i == 0)
        def _():
            dg3_ref[...] = jnp.zeros(dg3_ref.shape, F32)
            dg2_ref[...] = jnp.zeros(dg2_ref.shape, F32)
            dbg_ref[...] = jnp.zeros(dbg_ref.shape, F32)

        keep = jnp.where(i < n_tiles - 1, 1.0, 0.0)
        dh2 = jnp.zeros((tm, D_MODEL), F32)
        for c0 in range(0, cdim, CONV_CHUNK):
            sl = slice(c0, c0 + CONV_CHUNK)
            du = du_ref[:, sl].astype(F32)
            nxt = halo_ref[:, sl].astype(F32)
            xp1, xp2 = _conv_taps_next(du, nxt[0:1] * keep, nxt[1:2] * keep)
            dup = (cw_ref[2:3, sl] * du + cw_ref[1:2, sl] * xp1 + cw_ref[0:1, sl] * xp2).astype(BF16)
            dup_ref[:, sl] = dup
            dh2 = dh2 + _dot_nt(dup, wup_ref[c0 // CONV_CHUNK])
        x1n, r3 = _rms_stats(x1_ref[...])
        d1, dg3 = _rms_bwd(dh2, x1n, r3, g3_ref[...])
        dx1 = dx2_ref[...] + d1
        dx1_ref[...] = dx1
        dg3_ref[...] += dg3
        on, r2 = _rms_stats(o_ref[...])
        do, dg2 = _rms_bwd(dx1, on, r2, g2_ref[...])
        dg2_ref[...] += dg2
        dob = do.astype(BF16)
        do_ref[...] = dob
        dmixed = _dot_nt(dob, wout_ref[...])
        ga = gate_ref[:, 0:D_MODEL]
        gb = gate_ref[:, D_MODEL:2 * D_MODEL]
        dpa = (dmixed * ga).astype(BF16)
        dpb = (dmixed * gb).astype(BF16)
        dpa_ref[...] = dpa
        dpb_ref[...] = dpb
        dga = dmixed * pa_ref[...] * ga * (1.0 - ga)
        dgb = dmixed * pb_ref[...] * gb * (1.0 - gb)
        dgt_ref[:, 0:D_MODEL] = dga.astype(BF16)
        dgt_ref[:, D_MODEL:2 * D_MODEL] = dgb.astype(BF16)
        dbg_ref[:, 0:D_MODEL] += jnp.sum(dga, axis=0, keepdims=True)
        dbg_ref[:, D_MODEL:2 * D_MODEL] += jnp.sum(dgb, axis=0, keepdims=True)
        dya_ref[...] = _dot_nt(dpa, wba_ref[...]).astype(BF16)
        dyb = _dot_nt(dpb, wbb_ref[...]).astype(BF16)
        dyb_ref[...] = dyb
        prod = yb_ref[...].astype(F32) * dyb.astype(F32)
        lane_head = lax.broadcasted_iota(jnp.int32, (HEADS, HEADS * LANES), 1) // LANES
        sel = (lane_head == lax.broadcasted_iota(jnp.int32, (HEADS, HEADS * LANES), 0)).astype(BF16)
        hi = prod.astype(BF16)
        lo = (prod - hi.astype(F32)).astype(BF16)
        dl_ref[...] = _dot_nt(sel, hi) + _dot_nt(sel, lo)

    def o_(n, dt):
        return (_sds((t_rows, n), dt), _row(tm, n))

    def acc(r, n):
        return (_sds((r, n), F32), _full((r, n)))

    halo = pl.BlockSpec((halo_rows, cdim), lambda i: (jnp.minimum((i + 1) * hb, last_blk), 0))
    ins = [(du, _row(tm, cdim)), (du, halo), (convw8, _full(convw8.shape)), (wup, _resident(wup.shape)),
           (dx2, _row(tm, D_MODEL)), (x1, _row(tm, D_MODEL)), (g3, _full(g3.shape)), (o, _row(tm, D_MODEL)),
           (g2, _full(g2.shape)), (wout, _resident(wout.shape)), (gate, _row(tm, 2048)), (pa, _row(tm, D_MODEL)),
           (pb, _row(tm, D_MODEL)), (wba, _resident(wba.shape)), (wbb, _resident(wbb.shape)), (yb, _row(tm, 1024))]
    outs = [o_(cdim, BF16), o_(D_MODEL, F32), o_(D_MODEL, BF16), o_(D_MODEL, BF16), o_(D_MODEL, BF16),
            o_(2048, BF16), o_(1024, BF16), o_(1024, BF16),
            (_sds((HEADS, t_rows), F32), pl.BlockSpec((HEADS, tm), lambda i: (0, i))),
            acc(1, D_MODEL), acc(1, D_MODEL), acc(1, 2048)]
    return _rows_call("bwd_mid", body, t_rows, tm, ins, outs)


def _bwd_in(dqs, dks, dvs, dqm, dkm, dvm, tabs, consts, cq, ckv, gq, gkv, wuq, wk, wv, dgates, win, x, g1, dx1, tm):
    t_rows = x.shape[0]

    def body(dqs_ref, dks_ref, dvs_ref, dqm_ref, dkm_ref, dvm_ref, ca, sa1, sa2, cb, sb1, sb2, c_ref, cq_ref,
             ckv_ref, gq_ref, gkv_ref, wuq_ref, wk_ref, wv_ref, dgt_ref, win_ref, x_ref, g1_ref, dx1_ref,
             dz_ref, dqb_ref, dx_ref, dgq_ref, dgkv_ref, dg1_ref):
        i = pl.program_id(0)

        @pl.when(i == 0)
        def _():
            dgq_ref[...] = jnp.zeros(dgq_ref.shape, F32)
            dgkv_ref[...] = jnp.zeros(dgkv_ref.shape, F32)
            dg1_ref[...] = jnp.zeros(dg1_ref.shape, F32)

        ta = (ca[...], sa1[...], sa2[...])
        tb = (cb[...], sb1[...], sb2[...])
        dz_ref[:, Z_QA:Z_KA] = _rope_t(dqs_ref[...] * SCALE_A, *ta, A_HEAD_DIM // 2).astype(BF16)
        dz_ref[:, Z_KA:Z_VA] = _rope_t(dks_ref[...], *ta, A_HEAD_DIM // 2).astype(BF16)
        dz_ref[:, Z_VA:Z_CQ] = dvs_ref[...].astype(BF16)
        dqm = jnp.concatenate([dqm_ref[h] for h in range(HEADS)], axis=1)
        dqb = _rope_t(dqm * SCALE_B, *tb, ROPE_DIM // 2).astype(BF16)
        dqb_ref[...] = dqb
        dcqn = _dot_nt(dqb, wuq_ref[...])
        cqn, rq = _rms_stats(cq_ref[...])
        dcq, dgq = _rms_bwd(dcqn, cqn, rq, gq_ref[...])
        dgq_ref[...] += dgq
        dz_ref[:, Z_CQ:Z_CKV] = dcq.astype(BF16)
        dkm = dkm_ref[...]
        dslot = dkm[:, 0:LANES]
        for h in range(1, HEADS):
            dslot = dslot + dkm[:, h * LANES:(h + 1) * LANES]
        dz_ref[:, Z_KR:Z_GATE] = _rope_t(dslot * c_ref[10:11, :], *tb, ROPE_DIM // 2).astype(BF16)
        dckvn = _dot_nt(dkm.astype(BF16), wk_ref[...]) + _dot_nt(dvm_ref[...].astype(BF16), wv_ref[...])
        ckvn, rkv = _rms_stats(ckv_ref[...])
        dckv, dgkv = _rms_bwd(dckvn, ckvn, rkv, gkv_ref[...])
        dgkv_ref[...] += dgkv
        dz_ref[:, Z_CKV:Z_KR] = dckv.astype(BF16)
        dz_ref[:, Z_GATE:ZW] = dgt_ref[...]
        dh1 = _dot_nt(dz_ref[...], win_ref[...])
        xn, r1 = _rms_stats(x_ref[...])
        d0, dg1 = _rms_bwd(dh1, xn, r1, g1_ref[...])
        dg1_ref[...] += dg1
        dx_ref[...] = dx1_ref[...] + d0

    def acc(n):
        return (_sds((1, n), F32), _full((1, n)))

    ins = [(dqs, _row(tm, 1024)), (dks, _row(tm, 256)), (dvs, _row(tm, 256)), (dqm, _heads(tm, HEADS)),
           (dkm, _row(tm, 1024)), (dvm, _row(tm, 1024))] + [(t, _row(tm, LANES)) for t in tabs] + [
           (consts, _full(consts.shape)), (cq, _row(tm, 256)), (ckv, _row(tm, 128)), (gq, _full(gq.shape)),
           (gkv, _full(gkv.shape)), (wuq, _full(wuq.shape)), (wk, _full(wk.shape)), (wv, _full(wv.shape)),
           (dgates, _row(tm, 2048)), (win, _resident(win.shape)), (x, _row(tm, D_MODEL)), (g1, _full(g1.shape)),
           (dx1, _row(tm, D_MODEL))]
    outs = [(_sds((t_rows, ZW), BF16), _row(tm, ZW)), (_sds((t_rows, 1024), BF16), _row(tm, 1024)),
            (_sds((t_rows, D_MODEL), F32), _row(tm, D_MODEL)), acc(256), acc(128), acc(D_MODEL)]
    return _rows_call("bwd_in", body, t_rows, tm, ins, outs)


def _pick_cols(n):
    best = LANES
    for d in range(LANES, min(n, 1408) + 1, LANES):
        if n % d == 0:
            best = d
    return best


def _mm_tn(name, a, b, column_shards=1):
    t_rows, m = a.shape
    n = b.shape[1]
    bk = min(1024, t_rows)
    bm, bn = _pick_cols(m), _pick_cols(n // column_shards)
    per_shard = n // column_shards // bn

    def body(a_ref, b_ref, o_ref):
        @pl.when(pl.program_id(2) == 0)
        def _():
            o_ref[...] = jnp.zeros((bm, bn), F32)

        o_ref[...] += _dot_tn(a_ref[...].astype(BF16), b_ref[...].astype(BF16))

    return pl.pallas_call(
        body, name=name, grid=(m // bm, n // bn, t_rows // bk),
        in_specs=[pl.BlockSpec((bk, bm), lambda i, j, k: (k, i)), pl.BlockSpec((bk, bn), lambda i, j, k: (k, j))],
        out_specs=(pl.BlockSpec((bm, bn), lambda i, j, k: (i, j)) if column_shards == 1 else
                   pl.BlockSpec((None, bm, bn), lambda i, j, k: (j // per_shard, i, j % per_shard))),
        out_shape=_sds((m, n) if column_shards == 1 else (column_shards, m, n // column_shards), F32),
        compiler_params=pltpu.CompilerParams(dimension_semantics=("arbitrary",) * 3, vmem_limit_bytes=VMEM_LIMIT),
    )(a, b)


PACK_ROWS = 512


ADD_TILE_ELEMS = 1 << 17


def _add_rows(rows, cols):
    best = 16
    for d in range(16, rows + 1, 16):
        if rows % d == 0 and d * cols <= ADD_TILE_ELEMS:
            best = d
    assert rows % best == 0
    return best


def _add_pair(name, g, recv, half):
    _, _, rows, cols = g.shape
    t = _add_rows(rows, cols)

    def body(h_ref, g_ref, r_ref, o_ref):
        o_ref[...] = (g_ref[:, 0] + r_ref[...]).astype(BF16)

    spec = pl.BlockSpec((4, t, cols), lambda i, h: (0, i, 0))
    grid_spec = pltpu.PrefetchScalarGridSpec(
        num_scalar_prefetch=1, grid=(rows // t,),
        in_specs=[pl.BlockSpec((4, 1, t, cols), lambda i, h: (0, h[0], i, 0)), spec], out_specs=spec)
    return pl.pallas_call(body, name=name, grid_spec=grid_spec,
                          out_shape=_sds(recv.shape, BF16))(jnp.reshape(half, (1,)).astype(jnp.int32), g, recv)


def _add_chips(name, parts):
    _, rows, cols = parts.shape
    t = _add_rows(rows, cols)

    def body(p_ref, o_ref):
        acc = p_ref[0].astype(F32)
        for j in range(1, 4):
            acc = acc + p_ref[j].astype(F32)
        o_ref[...] = acc

    return pl.pallas_call(body, name=name, grid=(rows // t,),
                          in_specs=[pl.BlockSpec((4, t, cols), lambda i: (0, i, 0))],
                          out_specs=pl.BlockSpec((t, cols), lambda i: (i, 0)),
                          out_shape=_sds((rows, cols), F32))(parts)


def _add_devices(parts):
    n, rows, _ = parts.shape

    def body(p_ref, o_ref):
        acc = p_ref[0]
        for j in range(1, n):
            acc = acc + p_ref[j]
        o_ref[...] = acc

    return pl.pallas_call(body, name="small_add", grid=(1,),
                          in_specs=[pl.BlockSpec((n, rows, LANES), lambda i: (0, 0, 0))],
                          out_specs=pl.BlockSpec((rows, LANES), lambda i: (0, 0)),
                          out_shape=_sds((rows, LANES), F32))(parts)


def _adam_rows(k, n):
    target = max(8, (1 << 20) // (4 * n))
    if k <= target:
        return k
    best = None
    for d in range(8, target + 1, 8):
        if k % d == 0:
            best = d
    return best if best is not None else k


def _adam_update(w, g, m, v):
    m_ = ADAM_B1 * m + (1.0 - ADAM_B1) * g
    v_ = ADAM_B2 * v + (1.0 - ADAM_B2) * (g * g)
    delta = -ADAM_LR * ((m_ / (1.0 - ADAM_B1 ** ADAM_STEP)) / (jnp.sqrt(v_ / (1.0 - ADAM_B2 ** ADAM_STEP)) + ADAM_EPS)
                        + ADAM_WD * w)
    return delta, m_, v_


def _adamw(name, w, g, m, v):
    k, n = w.shape
    bk = _adam_rows(k, n)

    def body(w_ref, g_ref, m_ref, v_ref, d_ref, mo_ref, vo_ref):
        d_ref[...], mo_ref[...], vo_ref[...] = _adam_update(w_ref[...], g_ref[...], m_ref[...], v_ref[...])

    spec = pl.BlockSpec((bk, n), lambda i: (i, 0))
    out = pl.pallas_call(body, name=name, grid=(k // bk,), in_specs=[spec] * 4, out_specs=[spec] * 3,
                         out_shape=[_sds((k, n), F32)] * 3,
                         compiler_params=pltpu.CompilerParams(vmem_limit_bytes=VMEM_LIMIT))(w, g, m, v)
    return (g, *out)


def _adamw_halves(name, w, mine, theirs, m, v, half):
    k, n = w.shape
    bk = _adam_rows(k // 2, n)
    nb = k // 2 // bk

    def body(h_ref, w_ref, mine_ref, theirs_ref, m_ref, v_ref, g_ref, d_ref, mo_ref, vo_ref):
        g = jnp.where(pl.program_id(0) == h_ref[0], mine_ref[...], theirs_ref[...])
        g_ref[...] = g
        d_ref[...], mo_ref[...], vo_ref[...] = _adam_update(w_ref[...], g, m_ref[...], v_ref[...])

    full = pl.BlockSpec((bk, n), lambda h, i, c: (h * nb + i, 0))
    part = pl.BlockSpec((bk, n), lambda h, i, c: (i, 0))
    grid_spec = pltpu.PrefetchScalarGridSpec(num_scalar_prefetch=1, grid=(2, nb),
                                             in_specs=[full, part, part, full, full], out_specs=[full] * 4)
    return tuple(pl.pallas_call(
        body, name=name, grid_spec=grid_spec, out_shape=[_sds((k, n), F32)] * 4,
        compiler_params=pltpu.CompilerParams(vmem_limit_bytes=VMEM_LIMIT),
    )(jnp.reshape(half, (1,)).astype(jnp.int32), w, mine, theirs, m, v))


_HBM = pl.BlockSpec(memory_space=pltpu.HBM)


def _me():
    return lax.axis_index("x"), lax.axis_index("y"), lax.axis_index("c")


def _other_chips(x, y):
    return [(1 - x, y), (x, 1 - y), (1 - x, 1 - y)]


def _gather_weights(shards):
    n = len(shards)

    def body(*refs):
        x_refs, out_refs = refs[:n], refs[n:2 * n]
        send_sems, recv_sems = refs[2 * n:]
        x, y, c = _me()
        sibling = (x, y, 1 - c)
        chips = _other_chips(x, y)

        def copy(k, src, dst, to):
            return pltpu.make_async_remote_copy(src_ref=src, dst_ref=dst, send_sem=send_sems.at[k],
                                                recv_sem=recv_sems.at[k], device_id=to, device_id_type=MESH)

        first, passed = [], []
        for a, (x_ref, out_ref) in enumerate(zip(x_refs, out_refs)):
            for j, (cx, cy) in enumerate(chips):
                first.append(copy(6 * a + j, x_ref.at[c], out_ref.at[2 * x + y, c], (cx, cy, c)))
        for cp in first:
            cp.start()
        for a, (x_ref, out_ref) in enumerate(zip(x_refs, out_refs)):
            for j, (cx, cy) in enumerate(chips):
                landed = out_ref.at[2 * cx + cy, c]
                copy(6 * a + j, x_ref.at[c], landed, (cx, cy, c)).wait_recv()
                passed.append(copy(6 * a + 3 + j, landed, landed, sibling))
                passed[-1].start()
        for a, (x_ref, out_ref) in enumerate(zip(x_refs, out_refs)):
            for j, (cx, cy) in enumerate(chips):
                theirs = out_ref.at[2 * cx + cy, 1 - c]
                copy(6 * a + 3 + j, theirs, theirs, sibling).wait_recv()
        for cp in first + passed:
            cp.wait_send()

    return pl.pallas_call(
        body, name="gather_weights", out_shape=[_sds((4,) + s.shape, s.dtype) for s in shards],
        in_specs=[_HBM] * n, out_specs=[_HBM] * n,
        scratch_shapes=[pltpu.SemaphoreType.DMA((6 * n,)), pltpu.SemaphoreType.DMA((6 * n,))],
    )(*shards)


def _swap_sibling(name, vs, other_half=False):
    n = len(vs)

    def body(*refs):
        v_refs, out_refs = refs[:n], refs[n:2 * n]
        send_sems, recv_sems = refs[2 * n:]
        x, y, c = _me()
        cps = [pltpu.make_async_remote_copy(src_ref=v_ref.at[:, 1 - c] if other_half else v_ref, dst_ref=out_ref,
                                            send_sem=send_sems.at[a], recv_sem=recv_sems.at[a],
                                            device_id=(x, y, 1 - c), device_id_type=MESH)
               for a, (v_ref, out_ref) in enumerate(zip(v_refs, out_refs))]
        for cp in cps:
            cp.start()
        for cp in cps:
            cp.wait()

    def landing(v):
        return _sds((v.shape[0],) + v.shape[2:] if other_half else v.shape, v.dtype)

    return pl.pallas_call(
        body, name=name, out_shape=[landing(v) for v in vs], in_specs=[_HBM] * n, out_specs=[_HBM] * n,
        scratch_shapes=[pltpu.SemaphoreType.DMA((n,)), pltpu.SemaphoreType.DMA((n,))],
    )(*vs)


_SEM = pl.BlockSpec(memory_space=pltpu.SEMAPHORE)
_EFFECT = pltpu.SideEffectType.DATAFLOW_SIDE_EFFECTING
WHOLE = "whole"
PIECE = "piece"
SIBLING_HALF = "sibling"


def _landing_shape(v, mode):
    return {WHOLE: (4,) + v.shape, PIECE: v.shape, SIBLING_HALF: (v.shape[0],) + v.shape[2:]}[mode]


def _chip_copies(v_ref, land_ref, send_sems, recv_sems, mode, sem0=0):
    x, y, c = _me()
    if mode == SIBLING_HALF:
        cp = pltpu.make_async_remote_copy(src_ref=v_ref.at[:, 1 - c], dst_ref=land_ref, send_sem=send_sems.at[sem0],
                                          recv_sem=recv_sems.at[sem0], device_id=(x, y, 1 - c), device_id_type=MESH)
        return [(cp, cp)]
    k = 2 * x + y
    out = []
    for j, (cx, cy) in enumerate(_other_chips(x, y)):
        src = v_ref.at[2 * cx + cy] if mode == PIECE else v_ref
        sems = dict(send_sem=send_sems.at[sem0 + j], recv_sem=recv_sems.at[sem0 + j], device_id=(cx, cy, c),
                    device_id_type=MESH)
        send = pltpu.make_async_remote_copy(src_ref=src, dst_ref=land_ref.at[k], **sems)
        recv = pltpu.make_async_remote_copy(src_ref=src, dst_ref=land_ref.at[2 * cx + cy], **sems)
        out.append((send, recv))
    return out


def _chips_start(name, vs, mode, after=None):
    n = len(vs)
    lands = [_landing_shape(v, mode) for v in vs]

    def body(*refs):
        v_refs, land_refs = refs[:n], refs[n:2 * n]
        send_sems, recv_sems = refs[-2 * n - 3], refs[-2 * n - 2]
        token = refs[-1]
        for a in range(n):
            for send, _ in _chip_copies(v_refs[a], land_refs[a], send_sems, recv_sems, mode, 3 * a):
                send.start()
        token[...] = jnp.zeros_like(token)

    extra = () if after is None else (after,)
    hbm = [pltpu.with_memory_space_constraint(v, pltpu.HBM) for v in vs]
    zones = [pltpu.with_memory_space_constraint(lax.empty(s, v.dtype), pltpu.HBM) for s, v in zip(lands, vs)]
    out = pl.pallas_call(
        body, name=name,
        out_shape=(pltpu.SemaphoreType.DMA((3 * n,)), pltpu.SemaphoreType.DMA((3 * n,)),
                   *[pltpu.HBM(v.shape, v.dtype) for v in vs], *[pltpu.HBM(s, v.dtype) for s, v in zip(lands, vs)],
                   _sds((8, LANES), F32)),
        in_specs=(_HBM,) * (2 * n) + (pl.BlockSpec(memory_space=pl.ANY),) * len(extra),
        out_specs=(_SEM, _SEM) + (_HBM,) * (2 * n) + (pl.BlockSpec(memory_space=pltpu.VMEM),),
        input_output_aliases={i: 2 + i for i in range(2 * n)},
        compiler_params=pltpu.CompilerParams(has_side_effects=_EFFECT),
    )(*hbm, *zones, *extra)
    return out[0], out[1], list(out[2:2 + n]), list(out[2 + n:2 + 2 * n]), out[-1]


def _chips_wait(name, send_sems, recv_sems, v_thru, land_thru, mode, after):
    n = len(v_thru)

    def body(*refs):
        v_refs, land_refs = refs[:n], refs[n:2 * n]
        send_sems, recv_sems = refs[2 * n], refs[2 * n + 1]
        for a in range(n):
            for send, recv in _chip_copies(v_refs[a], land_refs[a], send_sems, recv_sems, mode, 3 * a):
                send.wait_send()
                recv.wait_recv()

    out = pl.pallas_call(
        body, name=name,
        out_shape=tuple(pltpu.HBM(a.shape, a.dtype) for a in list(v_thru) + list(land_thru)),
        in_specs=(_HBM,) * (2 * n) + (_SEM, _SEM, pl.BlockSpec(memory_space=pl.ANY)), out_specs=(_HBM,) * (2 * n),
        input_output_aliases={i: i for i in range(2 * n)},
        compiler_params=pltpu.CompilerParams(has_side_effects=_EFFECT),
    )(*v_thru, *land_thru, send_sems, recv_sems, after)
    return list(out[:n]), list(out[n:])


def _gather_small(name, v):
    def body(v_ref, out_ref, send_sems, recv_sems, local_sem):
        x, y, c = _me()
        me = 4 * x + 2 * y + c
        mine = pltpu.make_async_copy(v_ref, out_ref.at[me], local_sem)
        mine.start()
        peers = []
        for f in range(1, 8):
            fx, fy, fc = (f >> 2) & 1, (f >> 1) & 1, f & 1
            px = 1 - x if fx else x
            py = 1 - y if fy else y
            pc = 1 - c if fc else c
            peers.append((f - 1, (px, py, pc)))
        sends = [pltpu.make_async_remote_copy(src_ref=v_ref, dst_ref=out_ref.at[me], send_sem=send_sems.at[k],
                                              recv_sem=recv_sems.at[k], device_id=peer, device_id_type=MESH)
                 for k, peer in peers]
        for cp in sends:
            cp.start()
        for k, (px, py, pc) in peers:
            pltpu.make_async_remote_copy(src_ref=v_ref, dst_ref=out_ref.at[4 * px + 2 * py + pc],
                                         send_sem=send_sems.at[k], recv_sem=recv_sems.at[k],
                                         device_id=(px, py, pc), device_id_type=MESH).wait_recv()
        for cp in sends:
            cp.wait_send()
        mine.wait()

    return pl.pallas_call(
        body, name=name, out_shape=_sds((8,) + v.shape, v.dtype), in_specs=[_HBM], out_specs=_HBM,
        scratch_shapes=[pltpu.SemaphoreType.DMA((7,)), pltpu.SemaphoreType.DMA((7,)), pltpu.SemaphoreType.DMA],
    )(v)


_BIG = (("w_in", (1024, 3232), 1), ("w_uq", (256, 768), 1), ("w_ukv", (128, 1024), 1), ("w_branch_a", (512, 1024), 1),
        ("w_branch_b", (512, 1024), 1), ("w_out", (1024, 1024), 0), ("w_up", (1024, 5632), 1),
        ("w_down", (2816, 1024), 0), ("w_ple_gate", (1024, 1024), 0), ("w_ple", (256, 1024), 1))


def _shard_shape(shape, axis):
    return (shape[0] // 4, shape[1]) if axis == 0 else (shape[0], shape[1] // 4)


def _half_rows(shape, axis):
    k, n = _shard_shape(shape, axis)
    return k * n // (2 * LANES)


_EARLY = ("w_in", "w_uq", "w_ukv")
_LATE = ("w_branch_a", "w_branch_b", "w_out", "w_up", "w_down", "w_ple_gate", "w_ple")
_NATURAL = ("w_in", "w_up", "w_down", "w_out", "w_ple_gate")
_EARLY_PACKED = tuple(b for b in _BIG if b[0] in _EARLY and b[0] not in _NATURAL)
_LATE_PACKED = tuple(b for b in _BIG if b[0] in _LATE and b[0] not in _NATURAL)
_SHARD = {name: _shard_shape(shape, axis) for name, shape, axis in _BIG}


def _halves(a):
    return a.reshape(a.shape[:-2] + (2, a.shape[-2] // 2, a.shape[-1]))


def _rows_joined(a):
    return a.reshape(a.shape[:-3] + (a.shape[-3] * a.shape[-2], a.shape[-1]))


def _pack_pad(group):
    return -sum(_half_rows(shape, axis) for _, shape, axis in group) % PACK_ROWS


def _pack_shards(shards, dtype, group):
    parts = [shards[name].astype(dtype).reshape(2, _half_rows(shape, axis), LANES) for name, shape, axis in group]
    return jnp.concatenate(parts + [jnp.zeros((2, _pack_pad(group), LANES), dtype)], axis=1)


def _unpack_gathered(g, group):
    out, off = {}, 0
    for name, shape, axis in group:
        r = _half_rows(shape, axis)
        k, n = _shard_shape(shape, axis)
        w = g[:, :, off:off + r, :].reshape(4, k, n)
        out[name] = w.reshape(shape) if axis == 0 else w.transpose(1, 0, 2).reshape(shape)
        off += r
    return out


def _pack_grads(grads, group):
    parts = []
    for name, shape, axis in group:
        k, n = _shard_shape(shape, axis)
        g = grads[name]
        g4 = g.reshape(4, k, n) if axis == 0 else g.reshape(k, 4, n).transpose(1, 0, 2)
        parts.append(g4.reshape(4, 2, _half_rows(shape, axis), LANES))
    return jnp.concatenate(parts + [jnp.zeros((4, 2, _pack_pad(group), LANES), F32)], axis=2)


def _unpack_shard_grads(f, group):
    out, off = {}, 0
    for name, shape, axis in group:
        r = _half_rows(shape, axis)
        out[name] = f[:, off:off + r, :].reshape(_shard_shape(shape, axis))
        off += r
    return out


def _pad_slots(w, heads, dim, axis):
    if axis == 1:
        k = w.shape[0]
        return jnp.pad(w.reshape(k, heads, dim), ((0, 0), (0, 0), (0, LANES - dim))).reshape(k, heads * LANES)
    n = w.shape[1]
    return jnp.pad(w.reshape(heads, dim, n), ((0, 0), (0, LANES - dim), (0, 0))).reshape(heads * LANES, n)


def _unpad_slots(w, heads, dim, axis):
    if axis == 1:
        k = w.shape[0]
        return w.reshape(k, heads, LANES)[:, :, :dim].reshape(k, heads * dim)
    n = w.shape[1]
    return w.reshape(heads, LANES, n)[:, :dim, :].reshape(heads * dim, n)


def _pad_w_in(w):
    kr = jnp.pad(w[:, 1152:1184], ((0, 0), (NOPE_DIM, LANES - NOPE_DIM - ROPE_DIM)))
    return jnp.concatenate([_pad_slots(w[:, 0:512], HEADS, A_HEAD_DIM, 1),
                            _pad_slots(w[:, 512:640], A_KV_HEADS, A_HEAD_DIM, 1),
                            _pad_slots(w[:, 640:768], A_KV_HEADS, A_HEAD_DIM, 1),
                            w[:, 768:1024], w[:, 1024:1152], kr, w[:, 1184:3232]], axis=1)


def _unpad_w_in(w):
    return jnp.concatenate([_unpad_slots(w[:, Z_QA:Z_KA], HEADS, A_HEAD_DIM, 1),
                            _unpad_slots(w[:, Z_KA:Z_VA], A_KV_HEADS, A_HEAD_DIM, 1),
                            _unpad_slots(w[:, Z_VA:Z_CQ], A_KV_HEADS, A_HEAD_DIM, 1),
                            w[:, Z_CQ:Z_CKV], w[:, Z_CKV:Z_KR],
                            w[:, Z_KR + NOPE_DIM:Z_KR + NOPE_DIM + ROPE_DIM], w[:, Z_GATE:ZW]], axis=1)


_SMALL = (("attn_pre_norm", 1024), ("attn_post_norm", 1024), ("b_gate", 2048), ("sinks", 8), ("q_a_norm", 256),
          ("kv_a_norm", 128), ("mlp_pre_norm", 1024), ("mlp_post_norm", 1024), ("conv_b", 5632), ("ple_norm", 1024),
          ("conv_w", 3 * 5632), ("loss", 1))


def _small_rows(n):
    return 8 * -(-n // (8 * LANES))


def _pack_small(vals):
    parts = []
    for name, n in _SMALL:
        r = _small_rows(n)
        parts.append(jnp.pad(vals[name].reshape(-1), (0, r * LANES - n)).reshape(r, LANES))
    return jnp.concatenate(parts, axis=0)


def _unpack_small(buf):
    out, off = {}, 0
    for name, n in _SMALL:
        r = _small_rows(n)
        out[name] = buf[off:off + r].reshape(-1)[:n]
        off += r
    return out


def kernel(x, p, positions, attn_pre_norm, attn_post_norm, w_in, b_gate, sinks, q_a_norm, w_uq, kv_a_norm, w_ukv, w_branch_a, w_branch_b, w_out, mlp_pre_norm, mlp_post_norm, w_up, conv_w, conv_b, w_down, ple_norm, w_ple_gate, w_ple, loss_target, m_attn_pre_norm, m_attn_post_norm, m_w_in, m_b_gate, m_sinks, m_q_a_norm, m_w_uq, m_kv_a_norm, m_w_ukv, m_w_branch_a, m_w_branch_b, m_w_out, m_mlp_pre_norm, m_mlp_post_norm, m_w_up, m_conv_w, m_conv_b, m_w_down, m_ple_norm, m_w_ple_gate, m_w_ple, v_attn_pre_norm, v_attn_post_norm, v_w_in, v_b_gate, v_sinks, v_q_a_norm, v_w_uq, v_kv_a_norm, v_w_ukv, v_w_branch_a, v_w_branch_b, v_w_out, v_mlp_pre_norm, v_mlp_post_norm, v_w_up, v_conv_w, v_conv_b, v_w_down, v_ple_norm, v_w_ple_gate, v_w_ple):
    names = ["attn_pre_norm", "attn_post_norm", "w_in", "b_gate", "sinks", "q_a_norm", "w_uq", "kv_a_norm", "w_ukv",
             "w_branch_a", "w_branch_b", "w_out", "mlp_pre_norm", "mlp_post_norm", "w_up", "conv_w", "conv_b",
             "w_down", "ple_norm", "w_ple_gate", "w_ple"]
    wts = dict(zip(names, [attn_pre_norm, attn_post_norm, w_in, b_gate, sinks, q_a_norm, w_uq, kv_a_norm, w_ukv,
                           w_branch_a, w_branch_b, w_out, mlp_pre_norm, mlp_post_norm, w_up, conv_w, conv_b, w_down,
                           ple_norm, w_ple_gate, w_ple]))
    moms = dict(zip(names, [m_attn_pre_norm, m_attn_post_norm, m_w_in, m_b_gate, m_sinks, m_q_a_norm, m_w_uq,
                            m_kv_a_norm, m_w_ukv, m_w_branch_a, m_w_branch_b, m_w_out, m_mlp_pre_norm,
                            m_mlp_post_norm, m_w_up, m_conv_w, m_conv_b, m_w_down, m_ple_norm, m_w_ple_gate, m_w_ple]))
    vars_ = dict(zip(names, [v_attn_pre_norm, v_attn_post_norm, v_w_in, v_b_gate, v_sinks, v_q_a_norm, v_w_uq,
                             v_kv_a_norm, v_w_ukv, v_w_branch_a, v_w_branch_b, v_w_out, v_mlp_pre_norm,
                             v_mlp_post_norm, v_w_up, v_conv_w, v_conv_b, v_w_down, v_ple_norm, v_w_ple_gate, v_w_ple]))
    w2 = {n: a.reshape(a.shape[-2:]) for n, a in wts.items()}
    m2 = {n: a.reshape(a.shape[-2:]) for n, a in moms.items()}
    v2 = {n: a.reshape(a.shape[-2:]) for n, a in vars_.items()}

    t_rows = x.shape[-2]
    tm = min(256, t_rows)
    tm_wide = min(512, t_rows)
    xc, yc, cc = lax.axis_index("x"), lax.axis_index("y"), lax.axis_index("c")
    chip = 2 * xc + yc

    x2d = x.reshape(t_rows, D_MODEL)
    p2d = p.reshape(t_rows, PLE_DIM)
    tgt = loss_target.reshape(t_rows, D_MODEL)
    pos_f = positions.reshape(t_rows, 1).astype(F32)

    def own_slot_filled(gathered, mine):
        return [lax.dynamic_update_slice(g, m[None], (chip, 0, 0, 0)) for g, m in zip(gathered, mine)]

    def shard_lists(group, packed_group):
        return ([_halves(w2[n].astype(BF16)) for n in group if n in _NATURAL]
                + [_pack_shards(w2, BF16, packed_group)])

    cw_rows = 3 * 1408 // LANES
    conv_mine = jnp.pad(w2["conv_w"].reshape(cw_rows, LANES), ((0, 48 - cw_rows), (0, 0))).reshape(2, 24, LANES)
    early_mine = shard_lists(_EARLY, _EARLY_PACKED) + [conv_mine]
    late_mine = shard_lists(_LATE, _LATE_PACKED)
    early = own_slot_filled(_gather_weights(early_mine), early_mine)
    late_sems = _chips_start("gather_late_start", late_mine, WHOLE, after=early[0])
    late_token = late_sems[4][0:1, 0:1]
    full = _unpack_gathered(early[1], _EARLY_PACKED)
    full["w_in"] = _rows_joined(early[0]).transpose(1, 0, 2).reshape(D_MODEL, 3232)
    conv_full = early[2].reshape(4, 48, LANES)[:, :cw_rows].reshape(4, 3, 1408).transpose(1, 0, 2).reshape(3, 2 * D_FF)
    convw8 = jnp.pad(conv_full, ((0, 5), (0, 0)))

    win = _pad_w_in(full["w_in"])
    wuq = _pad_slots(full["w_uq"], HEADS, NOPE_DIM + ROPE_DIM, 1)
    ukv = full["w_ukv"].reshape(KV_LORA, HEADS, NOPE_DIM + V_DIM)
    wk = _pad_slots(ukv[:, :, :NOPE_DIM].reshape(KV_LORA, HEADS * NOPE_DIM), HEADS, NOPE_DIM, 1)
    wv = _pad_slots(ukv[:, :, NOPE_DIM:].reshape(KV_LORA, HEADS * V_DIM), HEADS, V_DIM, 1)
    g1, g2, g3, g4, g5 = (w2["attn_pre_norm"], w2["attn_post_norm"], w2["mlp_pre_norm"], w2["mlp_post_norm"],
                          w2["ple_norm"])
    gq, gkv, bg, convb = w2["q_a_norm"], w2["kv_a_norm"], w2["b_gate"], w2["conv_b"]
    swa_tile = min(SWA_TILE, t_rows)
    sink_rows = jnp.repeat(w2["sinks"].reshape(A_KV_HEADS, SWA_GROUP, 1), swa_tile, axis=2).reshape(
        A_KV_HEADS, 1, SWA_GROUP * swa_tile)
    swa_bias = _swa_bias(swa_tile)

    consts = _rope_consts()
    tabs = _rope_tables(pos_f, consts, tm)
    h1, qs, ks, vs, cq, cqn, ckv, ckvn, qm, km, vm, gate = _fwd_in(x2d, g1, win, bg + late_token, gq, gkv, wuq, wk, wv,
                                                                   tabs, tm_wide)
    ya, lse_a = _swa_fwd(qs, ks, vs, swa_bias, sink_rows)
    yb, lse_b = _mla_fwd(qm, km, vm)
    late_sent, late_landed = _chips_wait("gather_late_wait", *late_sems[:4], WHOLE, after=yb)
    late = own_slot_filled(late_landed, late_sent)
    full = _unpack_gathered(late[-1], _LATE_PACKED)
    wba = _pad_slots(full["w_branch_a"], HEADS, A_HEAD_DIM, 0)
    wbb = _pad_slots(full["w_branch_b"], HEADS, V_DIM, 0)
    wple = full["w_ple"]
    natural = dict(zip([n for n in _LATE if n in _NATURAL], late))
    wup = _rows_joined(natural["w_up"])
    wout, wdown, wpg = (_rows_joined(natural[n]).reshape(-1, D_MODEL) for n in ("w_out", "w_down", "w_ple_gate"))
    pa, pb, mixed, o, x1, h2 = _fwd_mix(x2d, ya, yb, gate, wba, wbb, wout, g2, g3, tm_wide)
    up, a = _fwd_up(h2, wup, convw8, convb, tm)
    ff, x2, e, n5, sg, dx3, loss_part = _fwd_out(a, wdown, x1, g4, p2d, wple, g5, wpg, tgt, tm_wide)

    dpre, de, dx2, dff, du, dg5, dg4, dconvb, dconvw8 = _bwd_out(dx3, e, sg, x2, ff, g5, g4, wpg, wdown, up, convw8,
                                                                 convb, tm)
    dup, dx1, do, dpa, dpb, dgates, dya, dyb, delta_b, dg3, dg2, dbg = _bwd_mid(
        du, convw8, wup, dx2, x1, g3, o, g2, wout, gate, pa, pb, wba, wbb, yb, tm)
    late_grads = {
        "w_branch_a": _unpad_slots(_mm_tn("dw_branch_a", ya, dpa), HEADS, A_HEAD_DIM, 0),
        "w_branch_b": _unpad_slots(_mm_tn("dw_branch_b", yb, dpb), HEADS, V_DIM, 0),
        "w_out": _mm_tn("dw_out", mixed, do).reshape(4, D_MODEL // 4, D_MODEL),
        "w_up": _mm_tn("dw_up", h2, dup, column_shards=4),
        "w_down": _mm_tn("dw_down", a, dff).reshape(4, D_FF // 4, D_MODEL),
        "w_ple_gate": _mm_tn("dw_ple_gate", n5, dpre).reshape(4, D_MODEL // 4, D_MODEL),
        "w_ple": _mm_tn("dw_ple", p2d, de),
    }

    def grad_views(grads, group, packed_group):
        return [_halves(grads[n]) for n in group if n in _NATURAL] + [_pack_grads(grads, packed_group)]

    def pair_sums(tag, views, theirs):
        return [_add_pair("rs_%s_add_pair_%d" % (tag, i), g, r, cc) for i, (g, r) in enumerate(zip(views, theirs))]

    swap_sems = _chips_start("swap_late_start", grad_views(late_grads, _LATE, _LATE_PACKED), SIBLING_HALF)
    dqs, dks, dvs, dsink_rows = _swa_bwd(qs, ks, vs, ya, dya, lse_a, swa_bias, sink_rows + swap_sems[4][0:1, 0:1])
    dsink = dsink_rows[:, 0:SWA_GROUP, 0]
    late_views, late_theirs = _chips_wait("swap_late_wait", *swap_sems[:4], SIBLING_HALF, after=dqs)
    rs_sems = _chips_start("scatter_late_start", pair_sums("late", late_views, late_theirs), PIECE)
    dqm, dkm, dvm = _mla_bwd(qm, km, vm, dyb, lse_b, delta_b.reshape(HEADS, 1, t_rows) + rs_sems[4][0:1, 0:1])
    dz, dqb, dx, dgq, dgkv, dg1 = _bwd_in(dqs, dks, dvs, dqm, dkm, dvm, tabs, consts, cq, ckv, gq, gkv, wuq, wk, wv,
                                           dgates, win, x2d, g1, dx1, tm)

    dwk = _unpad_slots(_mm_tn("dw_k", ckvn, dkm), HEADS, NOPE_DIM, 1).reshape(KV_LORA, HEADS, NOPE_DIM)
    dwv = _unpad_slots(_mm_tn("dw_v", ckvn, dvm), HEADS, V_DIM, 1).reshape(KV_LORA, HEADS, V_DIM)
    early_grads = {
        "w_in": _unpad_w_in(_mm_tn("dw_in", h1, dz)).reshape(D_MODEL, 4, 808).transpose(1, 0, 2),
        "w_uq": _unpad_slots(_mm_tn("dw_uq", cqn, dqb), HEADS, NOPE_DIM + ROPE_DIM, 1),
        "w_ukv": jnp.concatenate([dwk, dwv], axis=2).reshape(KV_LORA, HEADS * (NOPE_DIM + V_DIM)),
    }

    def finish(tag, pairs, landed, group, packed_group):
        reduced = []
        for i, (pair, land) in enumerate(zip(pairs, landed)):
            own = lax.dynamic_index_in_dim(pair, chip, 0, keepdims=True)
            reduced.append(_add_chips("rs_%s_add_chips_%d" % (tag, i),
                                      lax.dynamic_update_slice(land, own, (chip, 0, 0))))
        others = _swap_sibling("swap_%s_reduced_halves" % tag, reduced)
        r, o = reduced[-1], others[-1]
        packed = jnp.where(cc == 0, jnp.stack([r, o]), jnp.stack([o, r]))
        for n, g in _unpack_shard_grads(packed, packed_group).items():
            updates[n] = _adamw("adamw_" + n, w2[n], g, m2[n], v2[n])
        for n, r, o in zip([n for n in group if n in _NATURAL], reduced, others):
            updates[n] = _adamw_halves("adamw_" + n, w2[n], r, o, m2[n], v2[n], cc)

    small = {"attn_pre_norm": dg1, "attn_post_norm": dg2, "b_gate": dbg, "sinks": dsink, "q_a_norm": dgq,
             "kv_a_norm": dgkv, "mlp_pre_norm": dg3, "mlp_post_norm": dg4, "conv_b": dconvb, "ple_norm": dg5,
             "conv_w": dconvw8[0:3], "loss": loss_part}
    small_all = _gather_small("gather_small_grads", _pack_small(small))

    updates = {}

    def adamw(n, g):
        updates[n] = _adamw("adamw_" + n, w2[n], g, m2[n], v2[n])

    early_views = grad_views(early_grads, _EARLY, _EARLY_PACKED)
    early_theirs = _swap_sibling("swap_early_grad_halves", early_views, other_half=True)
    early_sems = _chips_start("scatter_early_start", pair_sums("early", early_views, early_theirs), PIECE,
                              after=small_all)
    late_pairs, late_landed = _chips_wait("scatter_late_wait", *rs_sems[:4], PIECE, after=early_sems[4])
    finish("late", late_pairs, late_landed, _LATE, _LATE_PACKED)
    early_pairs, early_landed = _chips_wait("scatter_early_wait", *early_sems[:4], PIECE,
                                            after=updates[_LATE[-1]][1])
    finish("early", early_pairs, early_landed, _EARLY, _EARLY_PACKED)

    small_sum = _unpack_small(_add_devices(small_all))
    for n in names:
        if n == "conv_w":
            adamw(n, lax.dynamic_index_in_dim(small_sum[n].reshape(3, 4, 1408), chip, 1, keepdims=False))
        elif n in small_sum:
            adamw(n, small_sum[n].reshape(w2[n].shape))
    loss = small_sum["loss"][0]

    outs = [[updates[n][i].reshape(wts[n].shape) for n in names] for i in range(4)]
    return (loss, dx.reshape(x.shape), *outs[0], *outs[1], *outs[2], *outs[3])
```

```python
import functools
import math

import numpy as np
import jax
import jax.numpy as jnp
from jax import lax
from jax.experimental import pallas as pl
from jax.experimental.pallas import tpu as pltpu

F32 = jnp.float32
BF16 = jnp.bfloat16

D_MODEL = 1024
D_FF = 2816
PLE_DIM = 256
ROPE_THETA = 10000.0
RMS_EPS = 1e-6
SWA_WINDOW = 128
HEADS = 8
A_KV_HEADS = 2
A_HEAD_DIM = 64
Q_LORA = 256
KV_LORA = 128
NOPE_DIM = 64
ROPE_DIM = 32
V_DIM = 64
LANES = 128
ZW = 4096
NEG = -1e30
SCALE_A = A_HEAD_DIM ** -0.5
SCALE_B = (NOPE_DIM + ROPE_DIM) ** -0.5

ADAM_LR = 0.001
ADAM_B1 = 0.9
ADAM_B2 = 0.999
ADAM_EPS = 1e-08
ADAM_WD = 0.01
ADAM_STEP = 10

VMEM_LIMIT = 60 * 1024 * 1024
MESH_AXES = ("x", "y", "c")
MESH = pl.DeviceIdType.MESH

Z_QA, Z_KA, Z_VA, Z_CQ, Z_CKV, Z_KR, Z_GATE = 0, 1024, 1280, 1536, 1792, 1920, 2048


def _dot(a, b):
    return jnp.dot(a, b, preferred_element_type=F32)


def _dot_nt(a, b):
    return lax.dot_general(a, b, (((1,), (1,)), ((), ())), preferred_element_type=F32)


def _dot_tn(a, b):
    return lax.dot_general(a, b, (((0,), (0,)), ((), ())), preferred_element_type=F32)


def _rms_stats(x):
    r = lax.rsqrt(jnp.mean(x * x, axis=-1, keepdims=True) + RMS_EPS)
    return x * r, r


def _rms_bwd(dy, xn, r, g):
    dxn = dy * g
    dx = r * (dxn - xn * jnp.mean(dxn * xn, axis=-1, keepdims=True))
    dg = jnp.sum(dy * xn, axis=0, keepdims=True)
    return dx, dg


def _tile_lanes(t, n):
    return t if n == 1 else jnp.concatenate([t] * n, axis=1)


def _rope(x, c, s1, s2, half):
    w = x.shape[1]
    n = w // LANES
    return (x * _tile_lanes(c, n) + pltpu.roll(x, w - half, 1) * _tile_lanes(s1, n)
            + pltpu.roll(x, half, 1) * _tile_lanes(s2, n))


def _rope_t(dy, c, s1, s2, half):
    w = dy.shape[1]
    n = w // LANES
    return (dy * _tile_lanes(c, n) + pltpu.roll(dy * _tile_lanes(s1, n), half, 1)
            + pltpu.roll(dy * _tile_lanes(s2, n), w - half, 1))


def _sigmoid(x):
    return 1.0 / (1.0 + jnp.exp(-x))


_GELU_C = math.sqrt(2.0 / math.pi)


def _gelu_and_grad(x):
    a = _GELU_C + (_GELU_C * 0.044715) * (x * x)
    th = jnp.tanh(x * a)
    hx = 0.5 * x
    p1 = 1.0 + th
    gel = hx * p1
    dgel = 0.5 * p1 + (hx * (1.0 - th * th)) * (3.0 * a - 2.0 * _GELU_C)
    return gel, dgel


def _conv_taps(up, h6, h7):
    r1 = pltpu.roll(up, 1, 0)
    r2 = pltpu.roll(up, 2, 0)
    rows = lax.broadcasted_iota(jnp.int32, (8, up.shape[1]), 0)
    xm1 = jnp.concatenate([jnp.where(rows == 0, h7, r1[0:8]), r1[8:]], axis=0)
    xm2 = jnp.concatenate([jnp.where(rows == 0, h6, jnp.where(rows == 1, h7, r2[0:8])), r2[8:]], axis=0)
    return xm1, xm2


def _conv_taps_next(du, n0, n1):
    tm = du.shape[0]
    r1 = pltpu.roll(du, tm - 1, 0)
    r2 = pltpu.roll(du, tm - 2, 0)
    rows = lax.broadcasted_iota(jnp.int32, (8, du.shape[1]), 0)
    xp1 = jnp.concatenate([r1[:tm - 8], jnp.where(rows == 7, n0, r1[tm - 8:])], axis=0)
    xp2 = jnp.concatenate([r2[:tm - 8], jnp.where(rows == 6, n0, jnp.where(rows == 7, n1, r2[tm - 8:]))], axis=0)
    return xp1, xp2


def _row(tm, n):
    return pl.BlockSpec((tm, n), lambda i: (i, 0))


def _full(shape):
    nd = len(shape)
    return pl.BlockSpec(tuple(shape), lambda i: (0,) * nd)


def _resident(shape):
    nd = len(shape)
    return pl.BlockSpec(tuple(shape), lambda i: (0,) * nd, pipeline_mode=pl.Buffered(1))


def _heads(tm, h):
    return pl.BlockSpec((h, tm, LANES), lambda i: (0, i, 0))


def _rows_call(name, body, t_rows, tm, ins, outs, scratch=()):
    return pl.pallas_call(
        body, name=name, grid=(t_rows // tm,),
        in_specs=[s for _, s in ins],
        out_specs=[s for _, s in outs],
        out_shape=[s for s, _ in outs],
        scratch_shapes=list(scratch),
        compiler_params=pltpu.CompilerParams(dimension_semantics=("arbitrary",), vmem_limit_bytes=VMEM_LIMIT),
    )(*[a for a, _ in ins])


def _sds(shape, dtype):
    return jax.ShapeDtypeStruct(tuple(shape), dtype)


def _rope_consts():
    c = np.zeros((16, LANES), np.float32)
    lane = np.arange(LANES)
    inv_a = (ROPE_THETA ** (-(np.arange(0, A_HEAD_DIM, 2, dtype=np.float32) / A_HEAD_DIM))).astype(np.float32)
    in_a = lane < A_HEAD_DIM
    c[0, in_a] = inv_a[lane[in_a] % (A_HEAD_DIM // 2)]
    c[1, in_a] = 1.0
    c[2, lane < A_HEAD_DIM // 2] = -1.0
    c[3, (lane >= A_HEAD_DIM // 2) & in_a] = 1.0
    inv_b = (ROPE_THETA ** (-(np.arange(0, ROPE_DIM, 2, dtype=np.float32) / ROPE_DIM))).astype(np.float32)
    pe = (lane >= NOPE_DIM) & (lane < NOPE_DIM + ROPE_DIM)
    c[5, pe] = inv_b[(lane[pe] - NOPE_DIM) % (ROPE_DIM // 2)]
    c[6, pe] = 1.0
    c[7, (lane >= NOPE_DIM) & (lane < NOPE_DIM + ROPE_DIM // 2)] = -1.0
    c[8, (lane >= NOPE_DIM + ROPE_DIM // 2) & (lane < NOPE_DIM + ROPE_DIM)] = 1.0
    c[9, lane < NOPE_DIM] = 1.0
    c[10, pe] = 1.0
    return jnp.asarray(c)


def _rope_tables(pos_f, consts, tm):
    t_rows = pos_f.shape[0]

    def body(pos_ref, c_ref, ca, sa1, sa2, cb, sb1, sb2):
        ang = pos_ref[...] * (c_ref[0:1, :] + c_ref[5:6, :])
        cs, sn = jnp.cos(ang), jnp.sin(ang)
        ca[...] = cs * c_ref[1:2, :]
        sa1[...] = sn * c_ref[2:3, :]
        sa2[...] = sn * c_ref[3:4, :]
        cb[...] = cs * c_ref[6:7, :] + c_ref[9:10, :]
        sb1[...] = sn * c_ref[7:8, :]
        sb2[...] = sn * c_ref[8:9, :]

    tab = (_sds((t_rows, LANES), F32), _row(tm, LANES))
    return _rows_call("rope_tables", body, t_rows, tm,
                      [(pos_f, _row(tm, 1)), (consts, _full(consts.shape))], [tab] * 6)


def _fwd_in(x, g1, win, bg, gq, gkv, wuq, wk, wv, tabs, tm):
    t_rows = x.shape[0]

    def body(x_ref, g1_ref, win_ref, bg_ref, gq_ref, gkv_ref, wuq_ref, wk_ref, wv_ref,
             ca, sa1, sa2, cb, sb1, sb2,
             h1_ref, qs_ref, ks_ref, vs_ref, cq_ref, cqn_ref, ckv_ref, ckvn_ref, qm_ref, km_ref, vm_ref, gate_ref):
        xn, _ = _rms_stats(x_ref[...])
        hb = (xn * g1_ref[...]).astype(BF16)
        h1_ref[...] = hb
        ta = (ca[...], sa1[...], sa2[...])
        tb = (cb[...], sb1[...], sb2[...])
        qs_ref[...] = (_rope(_dot(hb, win_ref[:, Z_QA:Z_KA]), *ta, A_HEAD_DIM // 2) * SCALE_A).astype(BF16)
        ks_ref[...] = _rope(_dot(hb, win_ref[:, Z_KA:Z_VA]), *ta, A_HEAD_DIM // 2).astype(BF16)
        vs_ref[...] = _dot(hb, win_ref[:, Z_VA:Z_CQ]).astype(BF16)
        cq = _dot(hb, win_ref[:, Z_CQ:Z_CKV])
        cq_ref[...] = cq
        cqn, _ = _rms_stats(cq)
        cqb = (cqn * gq_ref[...]).astype(BF16)
        cqn_ref[...] = cqb
        qm_ref[...] = (_rope(_dot(cqb, wuq_ref[...]), *tb, ROPE_DIM // 2) * SCALE_B).astype(BF16)
        ckv = _dot(hb, win_ref[:, Z_CKV:Z_KR])
        ckv_ref[...] = ckv
        ckvn, _ = _rms_stats(ckv)
        ckvb = (ckvn * gkv_ref[...]).astype(BF16)
        ckvn_ref[...] = ckvb
        kpe = _rope(_dot(hb, win_ref[:, Z_KR:Z_GATE]), *tb, ROPE_DIM // 2)
        km_ref[...] = (_dot(ckvb, wk_ref[...]) + _tile_lanes(kpe, HEADS)).astype(BF16)
        vm_ref[...] = _dot(ckvb, wv_ref[...]).astype(BF16)
        gate_ref[...] = _sigmoid(_dot(hb, win_ref[:, Z_GATE:ZW]) + bg_ref[...])

    def o(n, dt):
        return (_sds((t_rows, n), dt), _row(tm, n))

    ins = [(x, _row(tm, D_MODEL)), (g1, _full(g1.shape)), (win, _resident(win.shape)), (bg, _full(bg.shape)),
           (gq, _full(gq.shape)), (gkv, _full(gkv.shape)), (wuq, _full(wuq.shape)), (wk, _full(wk.shape)),
           (wv, _full(wv.shape))] + [(t, _row(tm, LANES)) for t in tabs]
    outs = [o(1024, BF16), o(1024, BF16), o(256, BF16), o(256, BF16), o(256, F32), o(256, BF16), o(128, F32),
            o(128, BF16), o(1024, BF16), o(1024, BF16), o(1024, BF16), o(2048, F32)]
    return _rows_call("fwd_in", body, t_rows, tm, ins, outs)


def _attn_tile(t_rows):
    return min(512, t_rows)


MLA_HEADS_PER_STEP = 2


def _causal_pairs(nq, by_kv):
    if by_kv:
        pairs = [(i, j) for j in range(nq) for i in range(j, nq)]
    else:
        pairs = [(i, j) for i in range(nq) for j in range(i + 1)]
    return (jnp.asarray([p[0] for p in pairs], jnp.int32), jnp.asarray([p[1] for p in pairs], jnp.int32))


def _mla_fwd(q, k, v):
    t_rows = q.shape[0]
    t = _attn_tile(t_rows)
    hp = MLA_HEADS_PER_STEP
    w = hp * LANES
    ii, jj = _causal_pairs(t_rows // t, by_kv=False)

    def body(i_ref, j_ref, q_ref, k_ref, v_ref, o_ref, lse_ref, m_s, l_s, acc_s):
        i = i_ref[pl.program_id(1)]
        j = j_ref[pl.program_id(1)]

        @pl.when(j == 0)
        def _():
            m_s[...] = jnp.full(m_s.shape, NEG, F32)
            l_s[...] = jnp.zeros(l_s.shape, F32)
            acc_s[...] = jnp.zeros(acc_s.shape, F32)

        def step(diagonal):
            for hh in range(hp):
                sl = slice(hh * LANES, (hh + 1) * LANES)
                s = _dot_nt(k_ref[:, sl], q_ref[:, sl])
                if diagonal:
                    valid = (lax.broadcasted_iota(jnp.int32, (t, t), 0) <= lax.broadcasted_iota(jnp.int32, (t, t), 1))
                    s = jnp.where(valid, s, NEG)
                m_prev = m_s[hh]
                m_new = jnp.maximum(m_prev, jnp.max(s, axis=0, keepdims=True))
                p = jnp.exp(s - m_new)
                alpha = jnp.exp(m_prev - m_new)
                l_new = alpha * l_s[hh] + jnp.sum(p, axis=0, keepdims=True)
                acc = alpha * acc_s[hh] + _dot_tn(v_ref[:, sl], p.astype(BF16))
                if diagonal:
                    o_ref[:, sl] = (acc / l_new).T.astype(o_ref.dtype)
                    lse_ref[hh] = m_new + jnp.log(l_new)
                else:
                    m_s[hh] = m_new
                    l_s[hh] = l_new
                    acc_s[hh] = acc

        pl.when(j < i)(lambda: step(False))
        pl.when(j == i)(lambda: step(True))

    grid_spec = pltpu.PrefetchScalarGridSpec(
        num_scalar_prefetch=2, grid=(HEADS // hp, ii.shape[0]),
        in_specs=[pl.BlockSpec((t, w), lambda hb, s, ir, jr: (ir[s], hb)),
                  pl.BlockSpec((t, w), lambda hb, s, ir, jr: (jr[s], hb)),
                  pl.BlockSpec((t, w), lambda hb, s, ir, jr: (jr[s], hb))],
        out_specs=[pl.BlockSpec((t, w), lambda hb, s, ir, jr: (ir[s], hb)),
                   pl.BlockSpec((hp, 1, t), lambda hb, s, ir, jr: (hb, 0, ir[s]))],
        scratch_shapes=[pltpu.VMEM((hp, 1, t), F32), pltpu.VMEM((hp, 1, t), F32), pltpu.VMEM((hp, LANES, t), F32)])
    return pl.pallas_call(
        body, name="mla_fwd", grid_spec=grid_spec,
        out_shape=[_sds((t_rows, HEADS * LANES), BF16), _sds((HEADS, 1, t_rows), F32)],
        compiler_params=pltpu.CompilerParams(dimension_semantics=("arbitrary",) * 2, vmem_limit_bytes=VMEM_LIMIT),
    )(ii, jj, q, k, v)


def _mla_bwd(q, k, v, do, lse, delta):
    t_rows = q.shape[0]
    t = _attn_tile(t_rows)
    hp = MLA_HEADS_PER_STEP
    w = hp * LANES
    ii, jj = _causal_pairs(t_rows // t, by_kv=True)

    def body(i_ref, j_ref, q_ref, k_ref, v_ref, do_ref, lse_ref, dl_ref, dq_ref, dk_ref, dv_ref):
        i = i_ref[pl.program_id(1)]
        j = j_ref[pl.program_id(1)]

        @pl.when(pl.program_id(1) == 0)
        def _():
            dq_ref[...] = jnp.zeros(dq_ref.shape, F32)

        def step(diagonal):
            r0 = pl.multiple_of(i * t, t)
            for hh in range(hp):
                sl = slice(hh * LANES, (hh + 1) * LANES)
                qv = q_ref[:, sl]
                kv = k_ref[:, sl]
                dov = do_ref[:, sl]
                s = _dot_nt(kv, qv)
                if diagonal:
                    valid = (lax.broadcasted_iota(jnp.int32, (t, t), 0) <= lax.broadcasted_iota(jnp.int32, (t, t), 1))
                    s = jnp.where(valid, s, NEG)
                p = jnp.exp(s - lse_ref[hh])
                dv = _dot(p.astype(BF16), dov)
                dp = _dot_nt(v_ref[:, sl], dov)
                ds = (p * (dp - dl_ref[hh])).astype(BF16)
                dk = _dot(ds, qv)
                if diagonal:
                    dv_ref[:, sl] = dv
                    dk_ref[:, sl] = dk
                else:
                    dv_ref[:, sl] += dv
                    dk_ref[:, sl] += dk
                dq_ref[hh, pl.ds(r0, t), :] += _dot_tn(ds, kv)

        pl.when(i > j)(lambda: step(False))
        pl.when(i == j)(lambda: step(True))

    def qmap(hb, s, ir, jr):
        return (ir[s], hb)

    def kvmap(hb, s, ir, jr):
        return (jr[s], hb)

    def rowmap(hb, s, ir, jr):
        return (hb, 0, ir[s])

    grid_spec = pltpu.PrefetchScalarGridSpec(
        num_scalar_prefetch=2, grid=(HEADS // hp, ii.shape[0]),
        in_specs=[pl.BlockSpec((t, w), qmap), pl.BlockSpec((t, w), kvmap), pl.BlockSpec((t, w), kvmap),
                  pl.BlockSpec((t, w), qmap), pl.BlockSpec((hp, 1, t), rowmap), pl.BlockSpec((hp, 1, t), rowmap)],
        out_specs=[pl.BlockSpec((hp, t_rows, LANES), lambda hb, s, ir, jr: (hb, 0, 0)),
                   pl.BlockSpec((t, w), kvmap), pl.BlockSpec((t, w), kvmap)])
    return pl.pallas_call(
        body, name="mla_bwd", grid_spec=grid_spec,
        out_shape=[_sds((HEADS, t_rows, LANES), F32), _sds((t_rows, HEADS * LANES), F32),
                   _sds((t_rows, HEADS * LANES), F32)],
        compiler_params=pltpu.CompilerParams(dimension_semantics=("arbitrary",) * 2, vmem_limit_bytes=VMEM_LIMIT),
    )(ii, jj, q, k, v, do, lse, delta)


SWA_TILE = 2 * SWA_WINDOW
SWA_GROUP = HEADS // A_KV_HEADS


def _swa_bias(tq):
    koff = lax.broadcasted_iota(jnp.int32, (tq + SWA_WINDOW, SWA_GROUP * tq), 0) - SWA_WINDOW
    qoff = (lax.broadcasted_iota(jnp.int32, (tq + SWA_WINDOW, SWA_GROUP * tq), 1) % tq)
    band = (koff <= qoff) & (qoff - koff < SWA_WINDOW)
    return jnp.stack([jnp.where(band & (koff >= 0), 0.0, NEG), jnp.where(band, 0.0, NEG)]).astype(F32)


def _swa_specs(tq, nq):
    wb = tq // SWA_WINDOW

    def qi(i):
        return jnp.minimum(i, nq - 1)

    q = pl.BlockSpec((tq, SWA_GROUP * LANES), lambda h, i: (qi(i), h))
    cur = pl.BlockSpec((tq, LANES), lambda h, i: (qi(i), h))
    prev = pl.BlockSpec((SWA_WINDOW, LANES), lambda h, i: (jnp.maximum(qi(i) * wb - 1, 0), h))
    bias = pl.BlockSpec((1, tq + SWA_WINDOW, SWA_GROUP * tq), lambda h, i: (jnp.minimum(i, 1), 0, 0))
    rows = pl.BlockSpec((1, 1, 1, SWA_GROUP * tq), lambda h, i: (h, qi(i), 0, 0))
    sink = pl.BlockSpec((1, 1, SWA_GROUP * tq), lambda h, i: (h, 0, 0))
    return q, cur, prev, bias, rows, sink


def _stack_heads(ref):
    return jnp.concatenate([ref[:, g * LANES:(g + 1) * LANES] for g in range(SWA_GROUP)], axis=0)


def _swa_fwd(q, k, v, bias, sink_rows):
    t_rows = q.shape[0]
    tq = min(SWA_TILE, t_rows)
    nq = t_rows // tq
    qs_, cur, prev, bs, rows, sk = _swa_specs(tq, nq)

    def body(q_ref, kc_ref, kp_ref, vc_ref, vp_ref, b_ref, sink_ref, o_ref, lse_ref):
        qs = _stack_heads(q_ref)
        kk = jnp.concatenate([kp_ref[...], kc_ref[...]], axis=0)
        vv = jnp.concatenate([vp_ref[...], vc_ref[...]], axis=0)
        s = _dot_nt(kk, qs) + b_ref[0]
        sink = sink_ref[0]
        m = jnp.maximum(jnp.max(s, axis=0, keepdims=True), sink)
        p = jnp.exp(s - m)
        l = jnp.sum(p, axis=0, keepdims=True) + jnp.exp(sink - m)
        o = (_dot_tn(vv, p.astype(BF16)) / l).T
        for g in range(SWA_GROUP):
            o_ref[:, g * LANES:(g + 1) * LANES] = o[g * tq:(g + 1) * tq].astype(o_ref.dtype)
        lse_ref[0, 0] = m + jnp.log(l)

    return pl.pallas_call(
        body, name="swa_fwd", grid=(A_KV_HEADS, nq),
        in_specs=[qs_, cur, prev, cur, prev, bs, sk],
        out_specs=[qs_, rows],
        out_shape=[_sds((t_rows, HEADS * LANES), BF16), _sds((A_KV_HEADS, nq, 1, SWA_GROUP * tq), F32)],
        compiler_params=pltpu.CompilerParams(dimension_semantics=("arbitrary",) * 2, vmem_limit_bytes=VMEM_LIMIT),
    )(q, k, k, v, v, bias, sink_rows)


def _swa_bwd(q, k, v, o, do, lse, bias, sink_rows):
    t_rows = q.shape[0]
    tq = min(SWA_TILE, t_rows)
    nq = t_rows // tq
    qs_, cur, prev, bs, rows, sk = _swa_specs(tq, nq)
    hw = SWA_WINDOW

    def body(q_ref, kc_ref, kp_ref, vc_ref, vp_ref, o_ref, do_ref, lse_ref, b_ref, sink_ref,
             dq_ref, dk_ref, dv_ref, dsink_ref, ck, cv, dsa):
        i = pl.program_id(1)

        @pl.when(i == 0)
        def _():
            dsa[...] = jnp.zeros(dsa.shape, F32)

        @pl.when(i < nq)
        def _():
            qs = _stack_heads(q_ref)
            dos = _stack_heads(do_ref)
            kk = jnp.concatenate([kp_ref[...], kc_ref[...]], axis=0)
            vv = jnp.concatenate([vp_ref[...], vc_ref[...]], axis=0)
            lse = lse_ref[0, 0]
            p = jnp.exp(_dot_nt(kk, qs) + b_ref[0] - lse)
            delta = jnp.sum((_stack_heads(o_ref).astype(F32) * dos.astype(F32)).T, axis=0, keepdims=True)
            dsa[...] += -jnp.exp(sink_ref[0] - lse) * delta
            dv = _dot(p.astype(BF16), dos)
            ds = (p * (_dot_nt(vv, dos) - delta)).astype(BF16)
            dk = _dot(ds, qs)
            dq = _dot_tn(ds, kk)
            for g in range(SWA_GROUP):
                dq_ref[:, g * LANES:(g + 1) * LANES] = dq[g * tq:(g + 1) * tq]

            @pl.when(i > 0)
            def _():
                dk_ref[0:tq - hw, :] = ck[0:tq - hw, :]
                dk_ref[tq - hw:tq, :] = ck[tq - hw:tq, :] + dk[0:hw]
                dv_ref[0:tq - hw, :] = cv[0:tq - hw, :]
                dv_ref[tq - hw:tq, :] = cv[tq - hw:tq, :] + dv[0:hw]

            ck[...] = dk[hw:hw + tq]
            cv[...] = dv[hw:hw + tq]

        @pl.when(i == nq)
        def _():
            dk_ref[...] = ck[...]
            dv_ref[...] = cv[...]
            dsink_ref[...] = jnp.zeros(dsink_ref.shape, F32)
            for g in range(SWA_GROUP):
                tot = jnp.sum(dsa[:, g * tq:(g + 1) * tq], axis=1, keepdims=True)
                dsink_ref[0, g:g + 1, :] = jnp.zeros((1, LANES), F32) + tot

    kv_out = pl.BlockSpec((tq, LANES), lambda h, i: (jnp.maximum(i - 1, 0), h))
    return pl.pallas_call(
        body, name="swa_bwd", grid=(A_KV_HEADS, nq + 1),
        in_specs=[qs_, cur, prev, cur, prev, qs_, qs_, rows, bs, sk],
        out_specs=[qs_, kv_out, kv_out, pl.BlockSpec((1, 8, LANES), lambda h, i: (h, 0, 0))],
        out_shape=[_sds((t_rows, HEADS * LANES), F32), _sds((t_rows, A_KV_HEADS * LANES), F32),
                   _sds((t_rows, A_KV_HEADS * LANES), F32), _sds((A_KV_HEADS, 8, LANES), F32)],
        scratch_shapes=[pltpu.VMEM((tq, LANES), F32), pltpu.VMEM((tq, LANES), F32),
                        pltpu.VMEM((1, SWA_GROUP * tq), F32)],
        compiler_params=pltpu.CompilerParams(dimension_semantics=("arbitrary",) * 2, vmem_limit_bytes=VMEM_LIMIT),
    )(q, k, k, v, v, o, do, lse, bias, sink_rows)


def _fwd_mix(x, ya, yb, gate, wba, wbb, wout, g2, g3, tm):
    t_rows = x.shape[0]

    def body(x_ref, ya_ref, yb_ref, gate_ref, wba_ref, wbb_ref, wout_ref, g2_ref, g3_ref,
             pa_ref, pb_ref, mixed_ref, o_ref, x1_ref, h2_ref):
        pa = _dot(ya_ref[...], wba_ref[...])
        pb = _dot(yb_ref[...], wbb_ref[...])
        pa_ref[...] = pa
        pb_ref[...] = pb
        mixed = (gate_ref[:, 0:D_MODEL] * pa + gate_ref[:, D_MODEL:2 * D_MODEL] * pb).astype(BF16)
        mixed_ref[...] = mixed
        o = _dot(mixed, wout_ref[...])
        o_ref[...] = o
        on, _ = _rms_stats(o)
        x1 = x_ref[...] + on * g2_ref[...]
        x1_ref[...] = x1
        x1n, _ = _rms_stats(x1)
        h2_ref[...] = (x1n * g3_ref[...]).astype(BF16)

    def o_(dt):
        return (_sds((t_rows, D_MODEL), dt), _row(tm, D_MODEL))

    ins = [(x, _row(tm, D_MODEL)), (ya, _row(tm, 1024)), (yb, _row(tm, 1024)), (gate, _row(tm, 2048)),
           (wba, _resident(wba.shape)), (wbb, _resident(wbb.shape)), (wout, _resident(wout.shape)),
           (g2, _full(g2.shape)), (g3, _full(g3.shape))]
    return _rows_call("fwd_mix", body, t_rows, tm, ins, [o_(F32), o_(F32), o_(BF16), o_(F32), o_(F32), o_(BF16)])


CONV_CHUNK = 1408


def _fwd_up(h2, wup, convw8, convb, tm):
    t_rows = h2.shape[0]
    cdim = 2 * D_FF

    def body(h2_ref, wup_ref, cw_ref, cb_ref, up_ref, a_ref, carry):
        i = pl.program_id(0)

        @pl.when(i == 0)
        def _():
            carry[...] = jnp.zeros(carry.shape, F32)

        hb = h2_ref[...]

        def conv(c0):
            sl = slice(c0, c0 + CONV_CHUNK)
            up = _dot(hb, wup_ref[c0 // CONV_CHUNK])
            up_ref[:, sl] = up
            xm1, xm2 = _conv_taps(up, carry[6:7, sl], carry[7:8, sl])
            u = cw_ref[0:1, sl] * xm2 + cw_ref[1:2, sl] * xm1 + cw_ref[2:3, sl] * up + cb_ref[:, sl]
            carry[:, sl] = up[tm - 8:tm, :]
            return u

        for c0 in range(0, D_FF, CONV_CHUNK):
            ug = conv(c0)
            uv = conv(D_FF + c0)
            gel, _ = _gelu_and_grad(ug)
            a_ref[:, c0:c0 + CONV_CHUNK] = (gel * uv).astype(BF16)

    ins = [(h2, _row(tm, D_MODEL)), (wup, _resident(wup.shape)), (convw8, _full(convw8.shape)), (convb, _full(convb.shape))]
    outs = [(_sds((t_rows, cdim), F32), _row(tm, cdim)), (_sds((t_rows, D_FF), BF16), _row(tm, D_FF))]
    return _rows_call("fwd_up", body, t_rows, tm, ins, outs, scratch=[pltpu.VMEM((8, cdim), F32)])


def _fwd_out(a, wdown, x1, g4, p, wple, g5, wpg, tgt, tm):
    t_rows = a.shape[0]

    def body(a_ref, wdown_ref, x1_ref, g4_ref, p_ref, wple_ref, g5_ref, wpg_ref, tgt_ref,
             ff_ref, x2_ref, e_ref, n5_ref, sg_ref, dx3_ref, loss_ref):
        i = pl.program_id(0)
        ff = _dot(a_ref[...], wdown_ref[...])
        ff_ref[...] = ff
        ffn, _ = _rms_stats(ff)
        x2 = x1_ref[...] + ffn * g4_ref[...]
        x2_ref[...] = x2
        e = _dot(p_ref[...].astype(BF16), wple_ref[...])
        e_ref[...] = e
        x2n, _ = _rms_stats(x2)
        n5 = (x2n * g5_ref[...]).astype(BF16)
        n5_ref[...] = n5
        sg = _sigmoid(_dot(n5, wpg_ref[...]))
        sg_ref[...] = sg
        d = x2 + sg * e - tgt_ref[...]
        dx3_ref[...] = d * (1.0 / D_MODEL)

        @pl.when(i == 0)
        def _():
            loss_ref[...] = jnp.zeros((1, 1), F32)

        loss_ref[...] += 0.5 * jnp.sum(jnp.sum(d * d, axis=1, keepdims=True), axis=0, keepdims=True) * (1.0 / D_MODEL)

    def o_(dt):
        return (_sds((t_rows, D_MODEL), dt), _row(tm, D_MODEL))

    ins = [(a, _row(tm, D_FF)), (wdown, _resident(wdown.shape)), (x1, _row(tm, D_MODEL)), (g4, _full(g4.shape)),
           (p, _row(tm, PLE_DIM)), (wple, _full(wple.shape)), (g5, _full(g5.shape)), (wpg, _resident(wpg.shape)),
           (tgt, _row(tm, D_MODEL))]
    outs = [o_(F32), o_(F32), o_(F32), o_(BF16), o_(F32), o_(F32), (_sds((1, 1), F32), _full((1, 1)))]
    return _rows_call("fwd_out", body, t_rows, tm, ins, outs)


def _bwd_out(dx3, e, sg, x2, ff, g5, g4, wpg, wdown, up, convw8, convb, tm):
    t_rows = dx3.shape[0]
    cdim = 2 * D_FF
    hb = tm // 8

    def body(dx3_ref, e_ref, sg_ref, x2_ref, ff_ref, g5_ref, g4_ref, wpg_ref, wdown_ref, up_ref, halo_ref, cw_ref,
             cb_ref, dpre_ref, de_ref, dx2_ref, dff_ref, du_ref, dg5_ref, dg4_ref, dcb_ref, dcw_ref):
        i = pl.program_id(0)

        @pl.when(i == 0)
        def _():
            dg5_ref[...] = jnp.zeros(dg5_ref.shape, F32)
            dg4_ref[...] = jnp.zeros(dg4_ref.shape, F32)
            dcb_ref[...] = jnp.zeros(dcb_ref.shape, F32)
            dcw_ref[...] = jnp.zeros(dcw_ref.shape, F32)

        dx3 = dx3_ref[...]
        sg = sg_ref[...]
        dpre = (dx3 * e_ref[...] * sg * (1.0 - sg)).astype(BF16)
        dpre_ref[...] = dpre
        de_ref[...] = (dx3 * sg).astype(BF16)
        dn5 = _dot_nt(dpre, wpg_ref[...])
        x2n, r5 = _rms_stats(x2_ref[...])
        d2, dg5 = _rms_bwd(dn5, x2n, r5, g5_ref[...])
        dx2 = dx3 + d2
        dx2_ref[...] = dx2
        dg5_ref[...] += dg5
        ffn, r4 = _rms_stats(ff_ref[...])
        dff, dg4 = _rms_bwd(dx2, ffn, r4, g4_ref[...])
        dg4_ref[...] += dg4
        dffb = dff.astype(BF16)
        dff_ref[...] = dffb
        keep = jnp.where(i > 0, 1.0, 0.0)

        def conv(c0):
            sl = slice(c0, c0 + CONV_CHUNK)
            up = up_ref[:, sl]
            xm1, xm2 = _conv_taps(up, halo_ref[6:7, sl] * keep, halo_ref[7:8, sl] * keep)
            u = cw_ref[0:1, sl] * xm2 + cw_ref[1:2, sl] * xm1 + cw_ref[2:3, sl] * up + cb_ref[:, sl]
            return u, up, xm1, xm2

        def grads(c0, du, up, xm1, xm2):
            sl = slice(c0, c0 + CONV_CHUNK)
            du_ref[:, sl] = du.astype(BF16)
            dcb_ref[:, sl] += jnp.sum(du, axis=0, keepdims=True)
            dcw_ref[0:1, sl] += jnp.sum(du * xm2, axis=0, keepdims=True)
            dcw_ref[1:2, sl] += jnp.sum(du * xm1, axis=0, keepdims=True)
            dcw_ref[2:3, sl] += jnp.sum(du * up, axis=0, keepdims=True)

        for c0 in range(0, D_FF, CONV_CHUNK):
            da = _dot_nt(dffb, wdown_ref[c0:c0 + CONV_CHUNK, :])
            ug, *rg = conv(c0)
            uv, *rv = conv(D_FF + c0)
            gel, dgel = _gelu_and_grad(ug)
            grads(c0, da * uv * dgel, *rg)
            grads(D_FF + c0, da * gel, *rv)

    def o_(n, dt):
        return (_sds((t_rows, n), dt), _row(tm, n))

    def acc(r, n):
        return (_sds((r, n), F32), _full((r, n)))

    halo = pl.BlockSpec((8, cdim), lambda i: (jnp.maximum(i * hb - 1, 0), 0))
    ins = [(dx3, _row(tm, D_MODEL)), (e, _row(tm, D_MODEL)), (sg, _row(tm, D_MODEL)), (x2, _row(tm, D_MODEL)),
           (ff, _row(tm, D_MODEL)), (g5, _full(g5.shape)), (g4, _full(g4.shape)), (wpg, _resident(wpg.shape)),
           (wdown, _resident(wdown.shape)), (up, _row(tm, cdim)), (up, halo), (convw8, _full(convw8.shape)),
           (convb, _full(convb.shape))]
    outs = [o_(D_MODEL, BF16), o_(D_MODEL, BF16), o_(D_MODEL, F32), o_(D_MODEL, BF16), o_(cdim, BF16),
            acc(1, D_MODEL), acc(1, D_MODEL), acc(1, cdim), acc(8, cdim)]
    return _rows_call("bwd_out", body, t_rows, tm, ins, outs)


def _bwd_mid(du, convw8, wup, dx2, x1, g3, o, g2, wout, gate, pa, pb, wba, wbb, yb, tm):
    t_rows = du.shape[0]
    cdim = 2 * D_FF
    halo_rows = 16
    hb = tm // halo_rows
    last_blk = t_rows // halo_rows - 1
    n_tiles = t_rows // tm

    def body(du_ref, halo_ref, cw_ref, wup_ref, dx2_ref, x1_ref, g3_ref, o_ref, g2_ref, wout_ref, gate_ref, pa_ref,
             pb_ref, wba_ref, wbb_ref, yb_ref,
             dup_ref, dx1_ref, do_ref, dpa_ref, dpb_ref, dgt_ref, dya_ref, dyb_ref, dl_ref, dg3_ref, dg2_ref, dbg_ref):
        i = pl.program_id(0)

        @pl.when(i == 0)
        def _():
            dg3_ref[...] = jnp.zeros(dg3_ref.shape, F32)
            dg2_ref[...] = jnp.zeros(dg2_ref.shape, F32)
            dbg_ref[...] = jnp.zeros(dbg_ref.shape, F32)

        keep = jnp.where(i < n_tiles - 1, 1.0, 0.0)
        dh2 = jnp.zeros((tm, D_MODEL), F32)
        for c0 in range(0, cdim, CONV_CHUNK):
            sl = slice(c0, c0 + CONV_CHUNK)
            du = du_ref[:, sl].astype(F32)
            nxt = halo_ref[:, sl].astype(F32)
            xp1, xp2 = _conv_taps_next(du, nxt[0:1] * keep, nxt[1:2] * keep)
            dup = (cw_ref[2:3, sl] * du + cw_ref[1:2, sl] * xp1 + cw_ref[0:1, sl] * xp2).astype(BF16)
            dup_ref[:, sl] = dup
            dh2 = dh2 + _dot_nt(dup, wup_ref[c0 // CONV_CHUNK])
        x1n, r3 = _rms_stats(x1_ref[...])
        d1, dg3 = _rms_bwd(dh2, x1n, r3, g3_ref[...])
        dx1 = dx2_ref[...] + d1
        dx1_ref[...] = dx1
        dg3_ref[...] += dg3
        on, r2 = _rms_stats(o_ref[...])
        do, dg2 = _rms_bwd(dx1, on, r2, g2_ref[...])
        dg2_ref[...] += dg2
        dob = do.astype(BF16)
        do_ref[...] = dob
        dmixed = _dot_nt(dob, wout_ref[...])
        ga = gate_ref[:, 0:D_MODEL]
        gb = gate_ref[:, D_MODEL:2 * D_MODEL]
        dpa = (dmixed * ga).astype(BF16)
        dpb = (dmixed * gb).astype(BF16)
        dpa_ref[...] = dpa
        dpb_ref[...] = dpb
        dga = dmixed * pa_ref[...] * ga * (1.0 - ga)
        dgb = dmixed * pb_ref[...] * gb * (1.0 - gb)
        dgt_ref[:, 0:D_MODEL] = dga.astype(BF16)
        dgt_ref[:, D_MODEL:2 * D_MODEL] = dgb.astype(BF16)
        dbg_ref[:, 0:D_MODEL] += jnp.sum(dga, axis=0, keepdims=True)
        dbg_ref[:, D_MODEL:2 * D_MODEL] += jnp.sum(dgb, axis=0, keepdims=True)
        dya_ref[...] = _dot_nt(dpa, wba_ref[...]).astype(BF16)
        dyb = _dot_nt(dpb, wbb_ref[...]).astype(BF16)
        dyb_ref[...] = dyb
        prod = yb_ref[...].astype(F32) * dyb.astype(F32)
        lane_head = lax.broadcasted_iota(jnp.int32, (HEADS, HEADS * LANES), 1) // LANES
        sel = (lane_head == lax.broadcasted_iota(jnp.int32, (HEADS, HEADS * LANES), 0)).astype(BF16)
        hi = prod.astype(BF16)
        lo = (prod - hi.astype(F32)).astype(BF16)
        dl_ref[...] = _dot_nt(sel, hi) + _dot_nt(sel, lo)

    def o_(n, dt):
        return (_sds((t_rows, n), dt), _row(tm, n))

    def acc(r, n):
        return (_sds((r, n), F32), _full((r, n)))

    halo = pl.BlockSpec((halo_rows, cdim), lambda i: (jnp.minimum((i + 1) * hb, last_blk), 0))
    ins = [(du, _row(tm, cdim)), (du, halo), (convw8, _full(convw8.shape)), (wup, _resident(wup.shape)),
           (dx2, _row(tm, D_MODEL)), (x1, _row(tm, D_MODEL)), (g3, _full(g3.shape)), (o, _row(tm, D_MODEL)),
           (g2, _full(g2.shape)), (wout, _resident(wout.shape)), (gate, _row(tm, 2048)), (pa, _row(tm, D_MODEL)),
           (pb, _row(tm, D_MODEL)), (wba, _resident(wba.shape)), (wbb, _resident(wbb.shape)), (yb, _row(tm, 1024))]
    outs = [o_(cdim, BF16), o_(D_MODEL, F32), o_(D_MODEL, BF16), o_(D_MODEL, BF16), o_(D_MODEL, BF16),
            o_(2048, BF16), o_(1024, BF16), o_(1024, BF16),
            (_sds((HEADS, t_rows), F32), pl.BlockSpec((HEADS, tm), lambda i: (0, i))),
            acc(1, D_MODEL), acc(1, D_MODEL), acc(1, 2048)]
    return _rows_call("bwd_mid", body, t_rows, tm, ins, outs)


def _bwd_in(dqs, dks, dvs, dqm, dkm, dvm, tabs, consts, cq, ckv, gq, gkv, wuq, wk, wv, dgates, win, x, g1, dx1, tm):
    t_rows = x.shape[0]

    def body(dqs_ref, dks_ref, dvs_ref, dqm_ref, dkm_ref, dvm_ref, ca, sa1, sa2, cb, sb1, sb2, c_ref, cq_ref,
             ckv_ref, gq_ref, gkv_ref, wuq_ref, wk_ref, wv_ref, dgt_ref, win_ref, x_ref, g1_ref, dx1_ref,
             dz_ref, dqb_ref, dx_ref, dgq_ref, dgkv_ref, dg1_ref):
        i = pl.program_id(0)

        @pl.when(i == 0)
        def _():
            dgq_ref[...] = jnp.zeros(dgq_ref.shape, F32)
            dgkv_ref[...] = jnp.zeros(dgkv_ref.shape, F32)
            dg1_ref[...] = jnp.zeros(dg1_ref.shape, F32)

        ta = (ca[...], sa1[...], sa2[...])
        tb = (cb[...], sb1[...], sb2[...])
        dz_ref[:, Z_QA:Z_KA] = _rope_t(dqs_ref[...] * SCALE_A, *ta, A_HEAD_DIM // 2).astype(BF16)
        dz_ref[:, Z_KA:Z_VA] = _rope_t(dks_ref[...], *ta, A_HEAD_DIM // 2).astype(BF16)
        dz_ref[:, Z_VA:Z_CQ] = dvs_ref[...].astype(BF16)
        dqm = jnp.concatenate([dqm_ref[h] for h in range(HEADS)], axis=1)
        dqb = _rope_t(dqm * SCALE_B, *tb, ROPE_DIM // 2).astype(BF16)
        dqb_ref[...] = dqb
        dcqn = _dot_nt(dqb, wuq_ref[...])
        cqn, rq = _rms_stats(cq_ref[...])
        dcq, dgq = _rms_bwd(dcqn, cqn, rq, gq_ref[...])
        dgq_ref[...] += dgq
        dz_ref[:, Z_CQ:Z_CKV] = dcq.astype(BF16)
        dkm = dkm_ref[...]
        dslot = dkm[:, 0:LANES]
        for h in range(1, HEADS):
            dslot = dslot + dkm[:, h * LANES:(h + 1) * LANES]
        dz_ref[:, Z_KR:Z_GATE] = _rope_t(dslot * c_ref[10:11, :], *tb, ROPE_DIM // 2).astype(BF16)
        dckvn = _dot_nt(dkm.astype(BF16), wk_ref[...]) + _dot_nt(dvm_ref[...].astype(BF16), wv_ref[...])
        ckvn, rkv = _rms_stats(ckv_ref[...])
        dckv, dgkv = _rms_bwd(dckvn, ckvn, rkv, gkv_ref[...])
        dgkv_ref[...] += dgkv
        dz_ref[:, Z_CKV:Z_KR] = dckv.astype(BF16)
        dz_ref[:, Z_GATE:ZW] = dgt_ref[...]
        dh1 = _dot_nt(dz_ref[...], win_ref[...])
        xn, r1 = _rms_stats(x_ref[...])
        d0, dg1 = _rms_bwd(dh1, xn, r1, g1_ref[...])
        dg1_ref[...] += dg1
        dx_ref[...] = dx1_ref[...] + d0

    def acc(n):
        return (_sds((1, n), F32), _full((1, n)))

    ins = [(dqs, _row(tm, 1024)), (dks, _row(tm, 256)), (dvs, _row(tm, 256)), (dqm, _heads(tm, HEADS)),
           (dkm, _row(tm, 1024)), (dvm, _row(tm, 1024))] + [(t, _row(tm, LANES)) for t in tabs] + [
           (consts, _full(consts.shape)), (cq, _row(tm, 256)), (ckv, _row(tm, 128)), (gq, _full(gq.shape)),
           (gkv, _full(gkv.shape)), (wuq, _full(wuq.shape)), (wk, _full(wk.shape)), (wv, _full(wv.shape)),
           (dgates, _row(tm, 2048)), (win, _resident(win.shape)), (x, _row(tm, D_MODEL)), (g1, _full(g1.shape)),
           (dx1, _row(tm, D_MODEL))]
    outs = [(_sds((t_rows, ZW), BF16), _row(tm, ZW)), (_sds((t_rows, 1024), BF16), _row(tm, 1024)),
            (_sds((t_rows, D_MODEL), F32), _row(tm, D_MODEL)), acc(256), acc(128), acc(D_MODEL)]
    return _rows_call("bwd_in", body, t_rows, tm, ins, outs)


def _pick_cols(n):
    best = LANES
    for d in range(LANES, min(n, 1408) + 1, LANES):
        if n % d == 0:
            best = d
    return best


def _mm_tn(name, a, b, column_shards=1):
    t_rows, m = a.shape
    n = b.shape[1]
    bk = min(1024, t_rows)
    bm, bn = _pick_cols(m), _pick_cols(n // column_shards)
    per_shard = n // column_shards // bn

    def body(a_ref, b_ref, o_ref):
        @pl.when(pl.program_id(2) == 0)
        def _():
            o_ref[...] = jnp.zeros((bm, bn), F32)

        o_ref[...] += _dot_tn(a_ref[...].astype(BF16), b_ref[...].astype(BF16))

    return pl.pallas_call(
        body, name=name, grid=(m // bm, n // bn, t_rows // bk),
        in_specs=[pl.BlockSpec((bk, bm), lambda i, j, k: (k, i)), pl.BlockSpec((bk, bn), lambda i, j, k: (k, j))],
        out_specs=(pl.BlockSpec((bm, bn), lambda i, j, k: (i, j)) if column_shards == 1 else
                   pl.BlockSpec((None, bm, bn), lambda i, j, k: (j // per_shard, i, j % per_shard))),
        out_shape=_sds((m, n) if column_shards == 1 else (column_shards, m, n // column_shards), F32),
        compiler_params=pltpu.CompilerParams(dimension_semantics=("arbitrary",) * 3, vmem_limit_bytes=VMEM_LIMIT),
    )(a, b)


PACK_ROWS = 512


ADD_TILE_ELEMS = 1 << 17


def _add_rows(rows, cols):
    best = 16
    for d in range(16, rows + 1, 16):
        if rows % d == 0 and d * cols <= ADD_TILE_ELEMS:
            best = d
    assert rows % best == 0
    return best


def _add_pair(name, g, recv, half):
    _, _, rows, cols = g.shape
    t = _add_rows(rows, cols)

    def body(h_ref, g_ref, r_ref, o_ref):
        o_ref[...] = (g_ref[:, 0] + r_ref[...]).astype(BF16)

    spec = pl.BlockSpec((4, t, cols), lambda i, h: (0, i, 0))
    grid_spec = pltpu.PrefetchScalarGridSpec(
        num_scalar_prefetch=1, grid=(rows // t,),
        in_specs=[pl.BlockSpec((4, 1, t, cols), lambda i, h: (0, h[0], i, 0)), spec], out_specs=spec)
    return pl.pallas_call(body, name=name, grid_spec=grid_spec,
                          out_shape=_sds(recv.shape, BF16))(jnp.reshape(half, (1,)).astype(jnp.int32), g, recv)


def _add_chips(name, parts):
    _, rows, cols = parts.shape
    t = _add_rows(rows, cols)

    def body(p_ref, o_ref):
        acc = p_ref[0].astype(F32)
        for j in range(1, 4):
            acc = acc + p_ref[j].astype(F32)
        o_ref[...] = acc

    return pl.pallas_call(body, name=name, grid=(rows // t,),
                          in_specs=[pl.BlockSpec((4, t, cols), lambda i: (0, i, 0))],
                          out_specs=pl.BlockSpec((t, cols), lambda i: (i, 0)),
                          out_shape=_sds((rows, cols), F32))(parts)


def _add_devices(parts):
    n, rows, _ = parts.shape

    def body(p_ref, o_ref):
        acc = p_ref[0]
        for j in range(1, n):
            acc = acc + p_ref[j]
        o_ref[...] = acc

    return pl.pallas_call(body, name="small_add", grid=(1,),
                          in_specs=[pl.BlockSpec((n, rows, LANES), lambda i: (0, 0, 0))],
                          out_specs=pl.BlockSpec((rows, LANES), lambda i: (0, 0)),
                          out_shape=_sds((rows, LANES), F32))(parts)


def _adam_rows(k, n):
    target = max(8, (1 << 20) // (4 * n))
    if k <= target:
        return k
    best = None
    for d in range(8, target + 1, 8):
        if k % d == 0:
            best = d
    return best if best is not None else k


def _adam_update(w, g, m, v):
    m_ = ADAM_B1 * m + (1.0 - ADAM_B1) * g
    v_ = ADAM_B2 * v + (1.0 - ADAM_B2) * (g * g)
    delta = -ADAM_LR * ((m_ / (1.0 - ADAM_B1 ** ADAM_STEP)) / (jnp.sqrt(v_ / (1.0 - ADAM_B2 ** ADAM_STEP)) + ADAM_EPS)
                        + ADAM_WD * w)
    return delta, m_, v_


def _adamw(name, w, g, m, v):
    k, n = w.shape
    bk = _adam_rows(k, n)

    def body(w_ref, g_ref, m_ref, v_ref, d_ref, mo_ref, vo_ref):
        d_ref[...], mo_ref[...], vo_ref[...] = _adam_update(w_ref[...], g_ref[...], m_ref[...], v_ref[...])

    spec = pl.BlockSpec((bk, n), lambda i: (i, 0))
    out = pl.pallas_call(body, name=name, grid=(k // bk,), in_specs=[spec] * 4, out_specs=[spec] * 3,
                         out_shape=[_sds((k, n), F32)] * 3,
                         compiler_params=pltpu.CompilerParams(vmem_limit_bytes=VMEM_LIMIT))(w, g, m, v)
    return (g, *out)


def _adamw_halves(name, w, mine, theirs, m, v, half):
    k, n = w.shape
    bk = _adam_rows(k // 2, n)
    nb = k // 2 // bk

    def body(h_ref, w_ref, mine_ref, theirs_ref, m_ref, v_ref, g_ref, d_ref, mo_ref, vo_ref):
        g = jnp.where(pl.program_id(0) == h_ref[0], mine_ref[...], theirs_ref[...])
        g_ref[...] = g
        d_ref[...], mo_ref[...], vo_ref[...] = _adam_update(w_ref[...], g, m_ref[...], v_ref[...])

    full = pl.BlockSpec((bk, n), lambda h, i, c: (h * nb + i, 0))
    part = pl.BlockSpec((bk, n), lambda h, i, c: (i, 0))
    grid_spec = pltpu.PrefetchScalarGridSpec(num_scalar_prefetch=1, grid=(2, nb),
                                             in_specs=[full, part, part, full, full], out_specs=[full] * 4)
    return tuple(pl.pallas_call(
        body, name=name, grid_spec=grid_spec, out_shape=[_sds((k, n), F32)] * 4,
        compiler_params=pltpu.CompilerParams(vmem_limit_bytes=VMEM_LIMIT),
    )(jnp.reshape(half, (1,)).astype(jnp.int32), w, mine, theirs, m, v))


_HBM = pl.BlockSpec(memory_space=pltpu.HBM)


def _me():
    return lax.axis_index("x"), lax.axis_index("y"), lax.axis_index("c")


def _other_chips(x, y):
    return [(1 - x, y), (x, 1 - y), (1 - x, 1 - y)]


def _pass_to_sibling(zones):
    n = len(zones)

    def body(*refs):
        in_refs, out_refs = refs[:n], refs[n:2 * n]
        send_sems, recv_sems = refs[2 * n:]
        x, y, c = _me()
        sent = []
        for a, (in_ref, out_ref) in enumerate(zip(in_refs, out_refs)):
            for j, (cx, cy) in enumerate(_other_chips(x, y)):
                mine, theirs = (2 * cx + cy, c), (2 * cx + cy, 1 - c)
                sems = dict(send_sem=send_sems.at[3 * a + j], recv_sem=recv_sems.at[3 * a + j],
                            device_id=(x, y, 1 - c), device_id_type=MESH)
                sent.append((pltpu.make_async_remote_copy(src_ref=in_ref.at[mine], dst_ref=out_ref.at[mine], **sems),
                             pltpu.make_async_remote_copy(src_ref=in_ref.at[theirs], dst_ref=out_ref.at[theirs], **sems)))
        for send, _ in sent:
            send.start()
        for _, recv in sent:
            recv.wait_recv()
        for send, _ in sent:
            send.wait_send()

    return pl.pallas_call(
        body, name="pass_to_sibling", out_shape=[_sds(z.shape, z.dtype) for z in zones],
        in_specs=[_HBM] * n, out_specs=[_HBM] * n, input_output_aliases={i: i for i in range(n)},
        scratch_shapes=[pltpu.SemaphoreType.DMA((3 * n,)), pltpu.SemaphoreType.DMA((3 * n,))],
    )(*zones)


def _swap_sibling(name, vs, other_half=False):
    n = len(vs)

    def body(*refs):
        v_refs, out_refs = refs[:n], refs[n:2 * n]
        send_sems, recv_sems = refs[2 * n:]
        x, y, c = _me()
        cps = [pltpu.make_async_remote_copy(src_ref=v_ref.at[:, 1 - c] if other_half else v_ref, dst_ref=out_ref,
                                            send_sem=send_sems.at[a], recv_sem=recv_sems.at[a],
                                            device_id=(x, y, 1 - c), device_id_type=MESH)
               for a, (v_ref, out_ref) in enumerate(zip(v_refs, out_refs))]
        for cp in cps:
            cp.start()
        for cp in cps:
            cp.wait()

    def landing(v):
        return _sds((v.shape[0],) + v.shape[2:] if other_half else v.shape, v.dtype)

    return pl.pallas_call(
        body, name=name, out_shape=[landing(v) for v in vs], in_specs=[_HBM] * n, out_specs=[_HBM] * n,
        scratch_shapes=[pltpu.SemaphoreType.DMA((n,)), pltpu.SemaphoreType.DMA((n,))],
    )(*vs)


_SEM = pl.BlockSpec(memory_space=pltpu.SEMAPHORE)
_EFFECT = pltpu.SideEffectType.DATAFLOW_SIDE_EFFECTING
WHOLE = "whole"
PIECE = "piece"
SIBLING_HALF = "sibling"
MY_HALF = "half"


def _landing_shape(v, mode):
    return {WHOLE: (4,) + v.shape, MY_HALF: (4,) + v.shape, PIECE: v.shape,
            SIBLING_HALF: (v.shape[0],) + v.shape[2:]}[mode]


def _chip_copies(v_ref, land_ref, send_sems, recv_sems, mode, sem0=0):
    x, y, c = _me()
    if mode == SIBLING_HALF:
        cp = pltpu.make_async_remote_copy(src_ref=v_ref.at[:, 1 - c], dst_ref=land_ref, send_sem=send_sems.at[sem0],
                                          recv_sem=recv_sems.at[sem0], device_id=(x, y, 1 - c), device_id_type=MESH)
        return [(cp, cp)]
    k = 2 * x + y
    out = []
    for j, (cx, cy) in enumerate(_other_chips(x, y)):
        if mode == MY_HALF:
            src, mine, theirs = v_ref.at[c], land_ref.at[k, c], land_ref.at[2 * cx + cy, c]
        else:
            src = v_ref.at[2 * cx + cy] if mode == PIECE else v_ref
            mine, theirs = land_ref.at[k], land_ref.at[2 * cx + cy]
        sems = dict(send_sem=send_sems.at[sem0 + j], recv_sem=recv_sems.at[sem0 + j], device_id=(cx, cy, c),
                    device_id_type=MESH)
        send = pltpu.make_async_remote_copy(src_ref=src, dst_ref=mine, **sems)
        recv = pltpu.make_async_remote_copy(src_ref=src, dst_ref=theirs, **sems)
        out.append((send, recv))
    return out


def _chips_start(name, vs, mode, after=None):
    n = len(vs)
    lands = [_landing_shape(v, mode) for v in vs]

    def body(*refs):
        v_refs, land_refs = refs[:n], refs[n:2 * n]
        send_sems, recv_sems = refs[-2 * n - 3], refs[-2 * n - 2]
        token = refs[-1]
        for a in range(n):
            for send, _ in _chip_copies(v_refs[a], land_refs[a], send_sems, recv_sems, mode, 3 * a):
                send.start()
        token[...] = jnp.zeros_like(token)

    extra = () if after is None else (after,)
    hbm = [pltpu.with_memory_space_constraint(v, pltpu.HBM) for v in vs]
    zones = [pltpu.with_memory_space_constraint(lax.empty(s, v.dtype), pltpu.HBM) for s, v in zip(lands, vs)]
    out = pl.pallas_call(
        body, name=name,
        out_shape=(pltpu.SemaphoreType.DMA((3 * n,)), pltpu.SemaphoreType.DMA((3 * n,)),
                   *[pltpu.HBM(v.shape, v.dtype) for v in vs], *[pltpu.HBM(s, v.dtype) for s, v in zip(lands, vs)],
                   _sds((8, LANES), F32)),
        in_specs=(_HBM,) * (2 * n) + (pl.BlockSpec(memory_space=pl.ANY),) * len(extra),
        out_specs=(_SEM, _SEM) + (_HBM,) * (2 * n) + (pl.BlockSpec(memory_space=pltpu.VMEM),),
        input_output_aliases={i: 2 + i for i in range(2 * n)},
        compiler_params=pltpu.CompilerParams(has_side_effects=_EFFECT),
    )(*hbm, *zones, *extra)
    return out[0], out[1], list(out[2:2 + n]), list(out[2 + n:2 + 2 * n]), out[-1]


def _chips_wait(name, send_sems, recv_sems, v_thru, land_thru, mode, after):
    n = len(v_thru)

    def body(*refs):
        v_refs, land_refs = refs[:n], refs[n:2 * n]
        send_sems, recv_sems = refs[2 * n], refs[2 * n + 1]
        for a in range(n):
            for send, recv in _chip_copies(v_refs[a], land_refs[a], send_sems, recv_sems, mode, 3 * a):
                send.wait_send()
                recv.wait_recv()

    out = pl.pallas_call(
        body, name=name,
        out_shape=tuple(pltpu.HBM(a.shape, a.dtype) for a in list(v_thru) + list(land_thru)),
        in_specs=(_HBM,) * (2 * n) + (_SEM, _SEM, pl.BlockSpec(memory_space=pl.ANY)), out_specs=(_HBM,) * (2 * n),
        input_output_aliases={i: i for i in range(2 * n)},
        compiler_params=pltpu.CompilerParams(has_side_effects=_EFFECT),
    )(*v_thru, *land_thru, send_sems, recv_sems, after)
    return list(out[:n]), list(out[n:])


def _gather_small(name, v):
    def body(v_ref, out_ref, send_sems, recv_sems, local_sem):
        x, y, c = _me()
        me = 4 * x + 2 * y + c
        mine = pltpu.make_async_copy(v_ref, out_ref.at[me], local_sem)
        mine.start()
        peers = []
        for f in range(1, 8):
            fx, fy, fc = (f >> 2) & 1, (f >> 1) & 1, f & 1
            px = 1 - x if fx else x
            py = 1 - y if fy else y
            pc = 1 - c if fc else c
            peers.append((f - 1, (px, py, pc)))
        sends = [pltpu.make_async_remote_copy(src_ref=v_ref, dst_ref=out_ref.at[me], send_sem=send_sems.at[k],
                                              recv_sem=recv_sems.at[k], device_id=peer, device_id_type=MESH)
                 for k, peer in peers]
        for cp in sends:
            cp.start()
        for k, (px, py, pc) in peers:
            pltpu.make_async_remote_copy(src_ref=v_ref, dst_ref=out_ref.at[4 * px + 2 * py + pc],
                                         send_sem=send_sems.at[k], recv_sem=recv_sems.at[k],
                                         device_id=(px, py, pc), device_id_type=MESH).wait_recv()
        for cp in sends:
            cp.wait_send()
        mine.wait()

    return pl.pallas_call(
        body, name=name, out_shape=_sds((8,) + v.shape, v.dtype), in_specs=[_HBM], out_specs=_HBM,
        scratch_shapes=[pltpu.SemaphoreType.DMA((7,)), pltpu.SemaphoreType.DMA((7,)), pltpu.SemaphoreType.DMA],
    )(v)


_BIG = (("w_in", (1024, 3232), 1), ("w_uq", (256, 768), 1), ("w_ukv", (128, 1024), 1), ("w_branch_a", (512, 1024), 1),
        ("w_branch_b", (512, 1024), 1), ("w_out", (1024, 1024), 0), ("w_up", (1024, 5632), 1),
        ("w_down", (2816, 1024), 0), ("w_ple_gate", (1024, 1024), 0), ("w_ple", (256, 1024), 1))


def _shard_shape(shape, axis):
    return (shape[0] // 4, shape[1]) if axis == 0 else (shape[0], shape[1] // 4)


def _half_rows(shape, axis):
    k, n = _shard_shape(shape, axis)
    return k * n // (2 * LANES)


_EARLY = ("w_in", "w_uq", "w_ukv")
_LATE = ("w_branch_a", "w_branch_b", "w_out", "w_up", "w_down", "w_ple_gate", "w_ple")
_NATURAL = ("w_in", "w_up", "w_down", "w_out", "w_ple_gate")
_EARLY_PACKED = tuple(b for b in _BIG if b[0] in _EARLY and b[0] not in _NATURAL)
_LATE_PACKED = tuple(b for b in _BIG if b[0] in _LATE and b[0] not in _NATURAL)
_SHARD = {name: _shard_shape(shape, axis) for name, shape, axis in _BIG}


def _halves(a):
    return a.reshape(a.shape[:-2] + (2, a.shape[-2] // 2, a.shape[-1]))


def _rows_joined(a):
    return a.reshape(a.shape[:-3] + (a.shape[-3] * a.shape[-2], a.shape[-1]))


def _pack_pad(group):
    return -sum(_half_rows(shape, axis) for _, shape, axis in group) % PACK_ROWS


def _pack_shards(shards, dtype, group):
    parts = [shards[name].astype(dtype).reshape(2, _half_rows(shape, axis), LANES) for name, shape, axis in group]
    return jnp.concatenate(parts + [jnp.zeros((2, _pack_pad(group), LANES), dtype)], axis=1)


def _unpack_gathered(g, group):
    out, off = {}, 0
    for name, shape, axis in group:
        r = _half_rows(shape, axis)
        k, n = _shard_shape(shape, axis)
        w = g[:, :, off:off + r, :].reshape(4, k, n)
        out[name] = w.reshape(shape) if axis == 0 else w.transpose(1, 0, 2).reshape(shape)
        off += r
    return out


def _pack_grads(grads, group):
    parts = []
    for name, shape, axis in group:
        k, n = _shard_shape(shape, axis)
        g = grads[name]
        g4 = g.reshape(4, k, n) if axis == 0 else g.reshape(k, 4, n).transpose(1, 0, 2)
        parts.append(g4.reshape(4, 2, _half_rows(shape, axis), LANES))
    return jnp.concatenate(parts + [jnp.zeros((4, 2, _pack_pad(group), LANES), F32)], axis=2)


def _unpack_shard_grads(f, group):
    out, off = {}, 0
    for name, shape, axis in group:
        r = _half_rows(shape, axis)
        out[name] = f[:, off:off + r, :].reshape(_shard_shape(shape, axis))
        off += r
    return out


def _pad_slots(w, heads, dim, axis):
    if axis == 1:
        k = w.shape[0]
        return jnp.pad(w.reshape(k, heads, dim), ((0, 0), (0, 0), (0, LANES - dim))).reshape(k, heads * LANES)
    n = w.shape[1]
    return jnp.pad(w.reshape(heads, dim, n), ((0, 0), (0, LANES - dim), (0, 0))).reshape(heads * LANES, n)


def _unpad_slots(w, heads, dim, axis):
    if axis == 1:
        k = w.shape[0]
        return w.reshape(k, heads, LANES)[:, :, :dim].reshape(k, heads * dim)
    n = w.shape[1]
    return w.reshape(heads, LANES, n)[:, :dim, :].reshape(heads * dim, n)


def _pad_w_in(w):
    kr = jnp.pad(w[:, 1152:1184], ((0, 0), (NOPE_DIM, LANES - NOPE_DIM - ROPE_DIM)))
    return jnp.concatenate([_pad_slots(w[:, 0:512], HEADS, A_HEAD_DIM, 1),
                            _pad_slots(w[:, 512:640], A_KV_HEADS, A_HEAD_DIM, 1),
                            _pad_slots(w[:, 640:768], A_KV_HEADS, A_HEAD_DIM, 1),
                            w[:, 768:1024], w[:, 1024:1152], kr, w[:, 1184:3232]], axis=1)


def _unpad_w_in(w):
    return jnp.concatenate([_unpad_slots(w[:, Z_QA:Z_KA], HEADS, A_HEAD_DIM, 1),
                            _unpad_slots(w[:, Z_KA:Z_VA], A_KV_HEADS, A_HEAD_DIM, 1),
                            _unpad_slots(w[:, Z_VA:Z_CQ], A_KV_HEADS, A_HEAD_DIM, 1),
                            w[:, Z_CQ:Z_CKV], w[:, Z_CKV:Z_KR],
                            w[:, Z_KR + NOPE_DIM:Z_KR + NOPE_DIM + ROPE_DIM], w[:, Z_GATE:ZW]], axis=1)


_SMALL = (("attn_pre_norm", 1024), ("attn_post_norm", 1024), ("b_gate", 2048), ("sinks", 8), ("q_a_norm", 256),
          ("kv_a_norm", 128), ("mlp_pre_norm", 1024), ("mlp_post_norm", 1024), ("conv_b", 5632), ("ple_norm", 1024),
          ("conv_w", 3 * 5632), ("loss", 1))


def _small_rows(n):
    return 8 * -(-n // (8 * LANES))


def _pack_small(vals):
    parts = []
    for name, n in _SMALL:
        r = _small_rows(n)
        parts.append(jnp.pad(vals[name].reshape(-1), (0, r * LANES - n)).reshape(r, LANES))
    return jnp.concatenate(parts, axis=0)


def _unpack_small(buf):
    out, off = {}, 0
    for name, n in _SMALL:
        r = _small_rows(n)
        out[name] = buf[off:off + r].reshape(-1)[:n]
        off += r
    return out


def kernel(x, p, positions, attn_pre_norm, attn_post_norm, w_in, b_gate, sinks, q_a_norm, w_uq, kv_a_norm, w_ukv, w_branch_a, w_branch_b, w_out, mlp_pre_norm, mlp_post_norm, w_up, conv_w, conv_b, w_down, ple_norm, w_ple_gate, w_ple, loss_target, m_attn_pre_norm, m_attn_post_norm, m_w_in, m_b_gate, m_sinks, m_q_a_norm, m_w_uq, m_kv_a_norm, m_w_ukv, m_w_branch_a, m_w_branch_b, m_w_out, m_mlp_pre_norm, m_mlp_post_norm, m_w_up, m_conv_w, m_conv_b, m_w_down, m_ple_norm, m_w_ple_gate, m_w_ple, v_attn_pre_norm, v_attn_post_norm, v_w_in, v_b_gate, v_sinks, v_q_a_norm, v_w_uq, v_kv_a_norm, v_w_ukv, v_w_branch_a, v_w_branch_b, v_w_out, v_mlp_pre_norm, v_mlp_post_norm, v_w_up, v_conv_w, v_conv_b, v_w_down, v_ple_norm, v_w_ple_gate, v_w_ple):
    names = ["attn_pre_norm", "attn_post_norm", "w_in", "b_gate", "sinks", "q_a_norm", "w_uq", "kv_a_norm", "w_ukv",
             "w_branch_a", "w_branch_b", "w_out", "mlp_pre_norm", "mlp_post_norm", "w_up", "conv_w", "conv_b",
             "w_down", "ple_norm", "w_ple_gate", "w_ple"]
    wts = dict(zip(names, [attn_pre_norm, attn_post_norm, w_in, b_gate, sinks, q_a_norm, w_uq, kv_a_norm, w_ukv,
                           w_branch_a, w_branch_b, w_out, mlp_pre_norm, mlp_post_norm, w_up, conv_w, conv_b, w_down,
                           ple_norm, w_ple_gate, w_ple]))
    moms = dict(zip(names, [m_attn_pre_norm, m_attn_post_norm, m_w_in, m_b_gate, m_sinks, m_q_a_norm, m_w_uq,
                            m_kv_a_norm, m_w_ukv, m_w_branch_a, m_w_branch_b, m_w_out, m_mlp_pre_norm,
                            m_mlp_post_norm, m_w_up, m_conv_w, m_conv_b, m_w_down, m_ple_norm, m_w_ple_gate, m_w_ple]))
    vars_ = dict(zip(names, [v_attn_pre_norm, v_attn_post_norm, v_w_in, v_b_gate, v_sinks, v_q_a_norm, v_w_uq,
                             v_kv_a_norm, v_w_ukv, v_w_branch_a, v_w_branch_b, v_w_out, v_mlp_pre_norm,
                             v_mlp_post_norm, v_w_up, v_conv_w, v_conv_b, v_w_down, v_ple_norm, v_w_ple_gate, v_w_ple]))
    w2 = {n: a.reshape(a.shape[-2:]) for n, a in wts.items()}
    m2 = {n: a.reshape(a.shape[-2:]) for n, a in moms.items()}
    v2 = {n: a.reshape(a.shape[-2:]) for n, a in vars_.items()}

    t_rows = x.shape[-2]
    tm = min(256, t_rows)
    tm_wide = min(512, t_rows)
    xc, yc, cc = lax.axis_index("x"), lax.axis_index("y"), lax.axis_index("c")
    chip = 2 * xc + yc

    x2d = x.reshape(t_rows, D_MODEL)
    p2d = p.reshape(t_rows, PLE_DIM)
    tgt = loss_target.reshape(t_rows, D_MODEL)
    pos_f = positions.reshape(t_rows, 1).astype(F32)

    def own_slot_filled(gathered, mine):
        return [lax.dynamic_update_slice(g, m[None], (chip, 0, 0, 0)) for g, m in zip(gathered, mine)]

    def shard_lists(group, packed_group):
        return ([_halves(w2[n].astype(BF16)) for n in group if n in _NATURAL]
                + [_pack_shards(w2, BF16, packed_group)])

    cw_rows = 3 * 1408 // LANES
    conv_mine = jnp.pad(w2["conv_w"].reshape(cw_rows, LANES), ((0, 48 - cw_rows), (0, 0))).reshape(2, 24, LANES)
    early_mine = shard_lists(_EARLY, _EARLY_PACKED) + [conv_mine]
    late_mine = shard_lists(_LATE, _LATE_PACKED)
    early_sems = _chips_start("gather_early_start", early_mine, MY_HALF)
    consts = _rope_consts()
    tabs = _rope_tables(pos_f + early_sems[4][0:1, 0:1], consts, tm)
    early_sent, early_landed = _chips_wait("gather_early_wait", *early_sems[:4], MY_HALF, after=tabs[0])
    early = own_slot_filled(_pass_to_sibling(early_landed), early_sent)
    late_sems = _chips_start("gather_late_start", late_mine, WHOLE, after=early[0])
    late_token = late_sems[4][0:1, 0:1]
    full = _unpack_gathered(early[1], _EARLY_PACKED)
    full["w_in"] = _rows_joined(early[0]).transpose(1, 0, 2).reshape(D_MODEL, 3232)
    conv_full = early[2].reshape(4, 48, LANES)[:, :cw_rows].reshape(4, 3, 1408).transpose(1, 0, 2).reshape(3, 2 * D_FF)
    convw8 = jnp.pad(conv_full, ((0, 5), (0, 0)))

    win = _pad_w_in(full["w_in"])
    wuq = _pad_slots(full["w_uq"], HEADS, NOPE_DIM + ROPE_DIM, 1)
    ukv = full["w_ukv"].reshape(KV_LORA, HEADS, NOPE_DIM + V_DIM)
    wk = _pad_slots(ukv[:, :, :NOPE_DIM].reshape(KV_LORA, HEADS * NOPE_DIM), HEADS, NOPE_DIM, 1)
    wv = _pad_slots(ukv[:, :, NOPE_DIM:].reshape(KV_LORA, HEADS * V_DIM), HEADS, V_DIM, 1)
    g1, g2, g3, g4, g5 = (w2["attn_pre_norm"], w2["attn_post_norm"], w2["mlp_pre_norm"], w2["mlp_post_norm"],
                          w2["ple_norm"])
    gq, gkv, bg, convb = w2["q_a_norm"], w2["kv_a_norm"], w2["b_gate"], w2["conv_b"]
    swa_tile = min(SWA_TILE, t_rows)
    sink_rows = jnp.repeat(w2["sinks"].reshape(A_KV_HEADS, SWA_GROUP, 1), swa_tile, axis=2).reshape(
        A_KV_HEADS, 1, SWA_GROUP * swa_tile)
    swa_bias = _swa_bias(swa_tile)

    h1, qs, ks, vs, cq, cqn, ckv, ckvn, qm, km, vm, gate = _fwd_in(x2d, g1, win, bg + late_token, gq, gkv, wuq, wk, wv,
                                                                   tabs, tm_wide)
    ya, lse_a = _swa_fwd(qs, ks, vs, swa_bias, sink_rows)
    yb, lse_b = _mla_fwd(qm, km, vm)
    late_sent, late_landed = _chips_wait("gather_late_wait", *late_sems[:4], WHOLE, after=yb)
    late = own_slot_filled(late_landed, late_sent)
    full = _unpack_gathered(late[-1], _LATE_PACKED)
    wba = _pad_slots(full["w_branch_a"], HEADS, A_HEAD_DIM, 0)
    wbb = _pad_slots(full["w_branch_b"], HEADS, V_DIM, 0)
    wple = full["w_ple"]
    natural = dict(zip([n for n in _LATE if n in _NATURAL], late))
    wup = _rows_joined(natural["w_up"])
    wout, wdown, wpg = (_rows_joined(natural[n]).reshape(-1, D_MODEL) for n in ("w_out", "w_down", "w_ple_gate"))
    pa, pb, mixed, o, x1, h2 = _fwd_mix(x2d, ya, yb, gate, wba, wbb, wout, g2, g3, tm_wide)
    up, a = _fwd_up(h2, wup, convw8, convb, tm)
    ff, x2, e, n5, sg, dx3, loss_part = _fwd_out(a, wdown, x1, g4, p2d, wple, g5, wpg, tgt, tm_wide)

    dpre, de, dx2, dff, du, dg5, dg4, dconvb, dconvw8 = _bwd_out(dx3, e, sg, x2, ff, g5, g4, wpg, wdown, up, convw8,
                                                                 convb, tm)
    dup, dx1, do, dpa, dpb, dgates, dya, dyb, delta_b, dg3, dg2, dbg = _bwd_mid(
        du, convw8, wup, dx2, x1, g3, o, g2, wout, gate, pa, pb, wba, wbb, yb, tm)
    late_grads = {
        "w_branch_a": _unpad_slots(_mm_tn("dw_branch_a", ya, dpa), HEADS, A_HEAD_DIM, 0),
        "w_branch_b": _unpad_slots(_mm_tn("dw_branch_b", yb, dpb), HEADS, V_DIM, 0),
        "w_out": _mm_tn("dw_out", mixed, do).reshape(4, D_MODEL // 4, D_MODEL),
        "w_up": _mm_tn("dw_up", h2, dup, column_shards=4),
        "w_down": _mm_tn("dw_down", a, dff).reshape(4, D_FF // 4, D_MODEL),
        "w_ple_gate": _mm_tn("dw_ple_gate", n5, dpre).reshape(4, D_MODEL // 4, D_MODEL),
        "w_ple": _mm_tn("dw_ple", p2d, de),
    }

    def grad_views(grads, group, packed_group):
        return [_halves(grads[n]) for n in group if n in _NATURAL] + [_pack_grads(grads, packed_group)]

    def pair_sums(tag, views, theirs):
        return [_add_pair("rs_%s_add_pair_%d" % (tag, i), g, r, cc) for i, (g, r) in enumerate(zip(views, theirs))]

    swap_sems = _chips_start("swap_late_start", grad_views(late_grads, _LATE, _LATE_PACKED), SIBLING_HALF)
    dqs, dks, dvs, dsink_rows = _swa_bwd(qs, ks, vs, ya, dya, lse_a, swa_bias, sink_rows + swap_sems[4][0:1, 0:1])
    dsink = dsink_rows[:, 0:SWA_GROUP, 0]
    late_views, late_theirs = _chips_wait("swap_late_wait", *swap_sems[:4], SIBLING_HALF, after=dqs)
    rs_sems = _chips_start("scatter_late_start", pair_sums("late", late_views, late_theirs), PIECE)
    dqm, dkm, dvm = _mla_bwd(qm, km, vm, dyb, lse_b, delta_b.reshape(HEADS, 1, t_rows) + rs_sems[4][0:1, 0:1])
    dz, dqb, dx, dgq, dgkv, dg1 = _bwd_in(dqs, dks, dvs, dqm, dkm, dvm, tabs, consts, cq, ckv, gq, gkv, wuq, wk, wv,
                                           dgates, win, x2d, g1, dx1, tm)

    dwk = _unpad_slots(_mm_tn("dw_k", ckvn, dkm), HEADS, NOPE_DIM, 1).reshape(KV_LORA, HEADS, NOPE_DIM)
    dwv = _unpad_slots(_mm_tn("dw_v", ckvn, dvm), HEADS, V_DIM, 1).reshape(KV_LORA, HEADS, V_DIM)
    early_grads = {
        "w_in": _unpad_w_in(_mm_tn("dw_in", h1, dz)).reshape(D_MODEL, 4, 808).transpose(1, 0, 2),
        "w_uq": _unpad_slots(_mm_tn("dw_uq", cqn, dqb), HEADS, NOPE_DIM + ROPE_DIM, 1),
        "w_ukv": jnp.concatenate([dwk, dwv], axis=2).reshape(KV_LORA, HEADS * (NOPE_DIM + V_DIM)),
    }

    def finish(tag, pairs, landed, group, packed_group):
        reduced = []
        for i, (pair, land) in enumerate(zip(pairs, landed)):
            own = lax.dynamic_index_in_dim(pair, chip, 0, keepdims=True)
            reduced.append(_add_chips("rs_%s_add_chips_%d" % (tag, i),
                                      lax.dynamic_update_slice(land, own, (chip, 0, 0))))
        others = _swap_sibling("swap_%s_reduced_halves" % tag, reduced)
        r, o = reduced[-1], others[-1]
        packed = jnp.where(cc == 0, jnp.stack([r, o]), jnp.stack([o, r]))
        for n, g in _unpack_shard_grads(packed, packed_group).items():
            updates[n] = _adamw("adamw_" + n, w2[n], g, m2[n], v2[n])
        for n, r, o in zip([n for n in group if n in _NATURAL], reduced, others):
            updates[n] = _adamw_halves("adamw_" + n, w2[n], r, o, m2[n], v2[n], cc)

    small = {"attn_pre_norm": dg1, "attn_post_norm": dg2, "b_gate": dbg, "sinks": dsink, "q_a_norm": dgq,
             "kv_a_norm": dgkv, "mlp_pre_norm": dg3, "mlp_post_norm": dg4, "conv_b": dconvb, "ple_norm": dg5,
             "conv_w": dconvw8[0:3], "loss": loss_part}
    small_all = _gather_small("gather_small_grads", _pack_small(small))

    updates = {}

    def adamw(n, g):
        updates[n] = _adamw("adamw_" + n, w2[n], g, m2[n], v2[n])

    early_views = grad_views(early_grads, _EARLY, _EARLY_PACKED)
    early_theirs = _swap_sibling("swap_early_grad_halves", early_views, other_half=True)
    early_sems = _chips_start("scatter_early_start", pair_sums("early", early_views, early_theirs), PIECE,
                              after=small_all)
    late_pairs, late_landed = _chips_wait("scatter_late_wait", *rs_sems[:4], PIECE, after=early_sems[4])
    finish("late", late_pairs, late_landed, _LATE, _LATE_PACKED)
    early_pairs, early_landed = _chips_wait("scatter_early_wait", *early_sems[:4], PIECE,
                                            after=updates[_LATE[-1]][1])
    finish("early", early_pairs, early_landed, _EARLY, _EARLY_PACKED)

    small_sum = _unpack_small(_add_devices(small_all))
    for n in names:
        if n == "conv_w":
            adamw(n, lax.dynamic_index_in_dim(small_sum[n].reshape(3, 4, 1408), chip, 1, keepdims=False))
        elif n in small_sum:
            adamw(n, small_sum[n].reshape(w2[n].shape))
    loss = small_sum["loss"][0]

    outs = [[updates[n][i].reshape(wts[n].shape) for n in names] for i in range(4)]
    return (loss, dx.reshape(x.shape), *outs[0], *outs[1], *outs[2], *outs[3])
```

```python
import functools
import math

import numpy as np
import jax
import jax.numpy as jnp
from jax import lax
from jax.experimental import pallas as pl
from jax.experimental.pallas import tpu as pltpu

F32 = jnp.float32
BF16 = jnp.bfloat16

D_MODEL = 1024
D_FF = 2816
PLE_DIM = 256
ROPE_THETA = 10000.0
RMS_EPS = 1e-6
SWA_WINDOW = 128
HEADS = 8
A_KV_HEADS = 2
A_HEAD_DIM = 64
Q_LORA = 256
KV_LORA = 128
NOPE_DIM = 64
ROPE_DIM = 32
V_DIM = 64
LANES = 128
ZW = 4096
NEG = -1e30
SCALE_A = A_HEAD_DIM ** -0.5
SCALE_B = (NOPE_DIM + ROPE_DIM) ** -0.5

ADAM_LR = 0.001
ADAM_B1 = 0.9
ADAM_B2 = 0.999
ADAM_EPS = 1e-08
ADAM_WD = 0.01
ADAM_STEP = 10

VMEM_LIMIT = 60 * 1024 * 1024
MESH_AXES = ("x", "y", "c")
MESH = pl.DeviceIdType.MESH

Z_QA, Z_KA, Z_VA, Z_CQ, Z_CKV, Z_KR, Z_GATE = 0, 1024, 1280, 1536, 1792, 1920, 2048


def _dot(a, b):
    return jnp.dot(a, b, preferred_element_type=F32)


def _dot_nt(a, b):
    return lax.dot_general(a, b, (((1,), (1,)), ((), ())), preferred_element_type=F32)


def _dot_tn(a, b):
    return lax.dot_general(a, b, (((0,), (0,)), ((), ())), preferred_element_type=F32)


def _rms_stats(x):
    r = lax.rsqrt(jnp.mean(x * x, axis=-1, keepdims=True) + RMS_EPS)
    return x * r, r


def _rms_bwd(dy, xn, r, g):
    dxn = dy * g
    dx = r * (dxn - xn * jnp.mean(dxn * xn, axis=-1, keepdims=True))
    dg = jnp.sum(dy * xn, axis=0, keepdims=True)
    return dx, dg


def _tile_lanes(t, n):
    return t if n == 1 else jnp.concatenate([t] * n, axis=1)


def _rope(x, c, s1, s2, half):
    w = x.shape[1]
    n = w // LANES
    return (x * _tile_lanes(c, n) + pltpu.roll(x, w - half, 1) * _tile_lanes(s1, n)
            + pltpu.roll(x, half, 1) * _tile_lanes(s2, n))


def _rope_t(dy, c, s1, s2, half):
    w = dy.shape[1]
    n = w // LANES
    return (dy * _tile_lanes(c, n) + pltpu.roll(dy * _tile_lanes(s1, n), half, 1)
            + pltpu.roll(dy * _tile_lanes(s2, n), w - half, 1))


def _sigmoid(x):
    return 1.0 / (1.0 + jnp.exp(-x))


_GELU_C = math.sqrt(2.0 / math.pi)


def _gelu_and_grad(x):
    a = _GELU_C + (_GELU_C * 0.044715) * (x * x)
    th = jnp.tanh(x * a)
    hx = 0.5 * x
    p1 = 1.0 + th
    gel = hx * p1
    dgel = 0.5 * p1 + (hx * (1.0 - th * th)) * (3.0 * a - 2.0 * _GELU_C)
    return gel, dgel


def _conv_taps(up, h6, h7):
    r1 = pltpu.roll(up, 1, 0)
    r2 = pltpu.roll(up, 2, 0)
    rows = lax.broadcasted_iota(jnp.int32, (8, up.shape[1]), 0)
    xm1 = jnp.concatenate([jnp.where(rows == 0, h7, r1[0:8]), r1[8:]], axis=0)
    xm2 = jnp.concatenate([jnp.where(rows == 0, h6, jnp.where(rows == 1, h7, r2[0:8])), r2[8:]], axis=0)
    return xm1, xm2


def _conv_taps_next(du, n0, n1):
    tm = du.shape[0]
    r1 = pltpu.roll(du, tm - 1, 0)
    r2 = pltpu.roll(du, tm - 2, 0)
    rows = lax.broadcasted_iota(jnp.int32, (8, du.shape[1]), 0)
    xp1 = jnp.concatenate([r1[:tm - 8], jnp.where(rows == 7, n0, r1[tm - 8:])], axis=0)
    xp2 = jnp.concatenate([r2[:tm - 8], jnp.where(rows == 6, n0, jnp.where(rows == 7, n1, r2[tm - 8:]))], axis=0)
    return xp1, xp2


def _row(tm, n):
    return pl.BlockSpec((tm, n), lambda i: (i, 0))


def _full(shape):
    nd = len(shape)
    return pl.BlockSpec(tuple(shape), lambda i: (0,) * nd)


def _resident(shape):
    nd = len(shape)
    return pl.BlockSpec(tuple(shape), lambda i: (0,) * nd, pipeline_mode=pl.Buffered(1))


def _heads(tm, h):
    return pl.BlockSpec((h, tm, LANES), lambda i: (0, i, 0))


def _rows_call(name, body, t_rows, tm, ins, outs, scratch=()):
    return pl.pallas_call(
        body, name=name, grid=(t_rows // tm,),
        in_specs=[s for _, s in ins],
        out_specs=[s for _, s in outs],
        out_shape=[s for s, _ in outs],
        scratch_shapes=list(scratch),
        compiler_params=pltpu.CompilerParams(dimension_semantics=("arbitrary",), vmem_limit_bytes=VMEM_LIMIT),
    )(*[a for a, _ in ins])


def _sds(shape, dtype):
    return jax.ShapeDtypeStruct(tuple(shape), dtype)


def _rope_consts():
    c = np.zeros((16, LANES), np.float32)
    lane = np.arange(LANES)
    inv_a = (ROPE_THETA ** (-(np.arange(0, A_HEAD_DIM, 2, dtype=np.float32) / A_HEAD_DIM))).astype(np.float32)
    in_a = lane < A_HEAD_DIM
    c[0, in_a] = inv_a[lane[in_a] % (A_HEAD_DIM // 2)]
    c[1, in_a] = 1.0
    c[2, lane < A_HEAD_DIM // 2] = -1.0
    c[3, (lane >= A_HEAD_DIM // 2) & in_a] = 1.0
    inv_b = (ROPE_THETA ** (-(np.arange(0, ROPE_DIM, 2, dtype=np.float32) / ROPE_DIM))).astype(np.float32)
    pe = (lane >= NOPE_DIM) & (lane < NOPE_DIM + ROPE_DIM)
    c[5, pe] = inv_b[(lane[pe] - NOPE_DIM) % (ROPE_DIM // 2)]
    c[6, pe] = 1.0
    c[7, (lane >= NOPE_DIM) & (lane < NOPE_DIM + ROPE_DIM // 2)] = -1.0
    c[8, (lane >= NOPE_DIM + ROPE_DIM // 2) & (lane < NOPE_DIM + ROPE_DIM)] = 1.0
    c[9, lane < NOPE_DIM] = 1.0
    c[10, pe] = 1.0
    return jnp.asarray(c)


def _rope_tables(pos_f, consts, tm):
    t_rows = pos_f.shape[0]

    def body(pos_ref, c_ref, ca, sa1, sa2, cb, sb1, sb2):
        ang = pos_ref[...] * (c_ref[0:1, :] + c_ref[5:6, :])
        cs, sn = jnp.cos(ang), jnp.sin(ang)
        ca[...] = cs * c_ref[1:2, :]
        sa1[...] = sn * c_ref[2:3, :]
        sa2[...] = sn * c_ref[3:4, :]
        cb[...] = cs * c_ref[6:7, :] + c_ref[9:10, :]
        sb1[...] = sn * c_ref[7:8, :]
        sb2[...] = sn * c_ref[8:9, :]

    tab = (_sds((t_rows, LANES), F32), _row(tm, LANES))
    return _rows_call("rope_tables", body, t_rows, tm,
                      [(pos_f, _row(tm, 1)), (consts, _full(consts.shape))], [tab] * 6)


def _fwd_in(x, g1, win, bg, gq, gkv, wuq, wk, wv, tabs, tm):
    t_rows = x.shape[0]

    def body(x_ref, g1_ref, win_ref, bg_ref, gq_ref, gkv_ref, wuq_ref, wk_ref, wv_ref,
             ca, sa1, sa2, cb, sb1, sb2,
             h1_ref, qs_ref, ks_ref, vs_ref, cq_ref, cqn_ref, ckv_ref, ckvn_ref, qm_ref, km_ref, vm_ref, gate_ref):
        xn, _ = _rms_stats(x_ref[...])
        hb = (xn * g1_ref[...]).astype(BF16)
        h1_ref[...] = hb
        ta = (ca[...], sa1[...], sa2[...])
        tb = (cb[...], sb1[...], sb2[...])
        qs_ref[...] = (_rope(_dot(hb, win_ref[:, Z_QA:Z_KA]), *ta, A_HEAD_DIM // 2) * SCALE_A).astype(BF16)
        ks_ref[...] = _rope(_dot(hb, win_ref[:, Z_KA:Z_VA]), *ta, A_HEAD_DIM // 2).astype(BF16)
        vs_ref[...] = _dot(hb, win_ref[:, Z_VA:Z_CQ]).astype(BF16)
        cq = _dot(hb, win_ref[:, Z_CQ:Z_CKV])
        cq_ref[...] = cq
        cqn, _ = _rms_stats(cq)
        cqb = (cqn * gq_ref[...]).astype(BF16)
        cqn_ref[...] = cqb
        qm_ref[...] = (_rope(_dot(cqb, wuq_ref[...]), *tb, ROPE_DIM // 2) * SCALE_B).astype(BF16)
        ckv = _dot(hb, win_ref[:, Z_CKV:Z_KR])
        ckv_ref[...] = ckv
        ckvn, _ = _rms_stats(ckv)
        ckvb = (ckvn * gkv_ref[...]).astype(BF16)
        ckvn_ref[...] = ckvb
        kpe = _rope(_dot(hb, win_ref[:, Z_KR:Z_GATE]), *tb, ROPE_DIM // 2)
        km_ref[...] = (_dot(ckvb, wk_ref[...]) + _tile_lanes(kpe, HEADS)).astype(BF16)
        vm_ref[...] = _dot(ckvb, wv_ref[...]).astype(BF16)
        gate_ref[...] = _sigmoid(_dot(hb, win_ref[:, Z_GATE:ZW]) + bg_ref[...])

    def o(n, dt):
        return (_sds((t_rows, n), dt), _row(tm, n))

    ins = [(x, _row(tm, D_MODEL)), (g1, _full(g1.shape)), (win, _resident(win.shape)), (bg, _full(bg.shape)),
           (gq, _full(gq.shape)), (gkv, _full(gkv.shape)), (wuq, _full(wuq.shape)), (wk, _full(wk.shape)),
           (wv, _full(wv.shape))] + [(t, _row(tm, LANES)) for t in tabs]
    outs = [o(1024, BF16), o(1024, BF16), o(256, BF16), o(256, BF16), o(256, F32), o(256, BF16), o(128, F32),
            o(128, BF16), o(1024, BF16), o(1024, BF16), o(1024, BF16), o(2048, F32)]
    return _rows_call("fwd_in", body, t_rows, tm, ins, outs)


def _attn_tile(t_rows):
    return min(512, t_rows)


MLA_HEADS_PER_STEP = 2


def _causal_pairs(nq, by_kv):
    if by_kv:
        pairs = [(i, j) for j in range(nq) for i in range(j, nq)]
    else:
        pairs = [(i, j) for i in range(nq) for j in range(i + 1)]
    return (jnp.asarray([p[0] for p in pairs], jnp.int32), jnp.asarray([p[1] for p in pairs], jnp.int32))


def _mla_fwd(q, k, v):
    t_rows = q.shape[0]
    t = _attn_tile(t_rows)
    hp = MLA_HEADS_PER_STEP
    w = hp * LANES
    ii, jj = _causal_pairs(t_rows // t, by_kv=False)

    def body(i_ref, j_ref, q_ref, k_ref, v_ref, o_ref, lse_ref, m_s, l_s, acc_s):
        i = i_ref[pl.program_id(1)]
        j = j_ref[pl.program_id(1)]

        @pl.when(j == 0)
        def _():
            m_s[...] = jnp.full(m_s.shape, NEG, F32)
            l_s[...] = jnp.zeros(l_s.shape, F32)
            acc_s[...] = jnp.zeros(acc_s.shape, F32)

        def step(diagonal):
            for hh in range(hp):
                sl = slice(hh * LANES, (hh + 1) * LANES)
                s = _dot_nt(k_ref[:, sl], q_ref[:, sl])
                if diagonal:
                    valid = (lax.broadcasted_iota(jnp.int32, (t, t), 0) <= lax.broadcasted_iota(jnp.int32, (t, t), 1))
                    s = jnp.where(valid, s, NEG)
                m_prev = m_s[hh]
                m_new = jnp.maximum(m_prev, jnp.max(s, axis=0, keepdims=True))
                p = jnp.exp(s - m_new)
                alpha = jnp.exp(m_prev - m_new)
                l_new = alpha * l_s[hh] + jnp.sum(p, axis=0, keepdims=True)
                acc = alpha * acc_s[hh] + _dot_tn(v_ref[:, sl], p.astype(BF16))
                if diagonal:
                    o_ref[:, sl] = (acc / l_new).T.astype(o_ref.dtype)
                    lse_ref[hh] = m_new + jnp.log(l_new)
                else:
                    m_s[hh] = m_new
                    l_s[hh] = l_new
                    acc_s[hh] = acc

        pl.when(j < i)(lambda: step(False))
        pl.when(j == i)(lambda: step(True))

    grid_spec = pltpu.PrefetchScalarGridSpec(
        num_scalar_prefetch=2, grid=(HEADS // hp, ii.shape[0]),
        in_specs=[pl.BlockSpec((t, w), lambda hb, s, ir, jr: (ir[s], hb)),
                  pl.BlockSpec((t, w), lambda hb, s, ir, jr: (jr[s], hb)),
                  pl.BlockSpec((t, w), lambda hb, s, ir, jr: (jr[s], hb))],
        out_specs=[pl.BlockSpec((t, w), lambda hb, s, ir, jr: (ir[s], hb)),
                   pl.BlockSpec((hp, 1, t), lambda hb, s, ir, jr: (hb, 0, ir[s]))],
        scratch_shapes=[pltpu.VMEM((hp, 1, t), F32), pltpu.VMEM((hp, 1, t), F32), pltpu.VMEM((hp, LANES, t), F32)])
    return pl.pallas_call(
        body, name="mla_fwd", grid_spec=grid_spec,
        out_shape=[_sds((t_rows, HEADS * LANES), BF16), _sds((HEADS, 1, t_rows), F32)],
        compiler_params=pltpu.CompilerParams(dimension_semantics=("arbitrary",) * 2, vmem_limit_bytes=VMEM_LIMIT),
    )(ii, jj, q, k, v)


def _mla_bwd(q, k, v, do, lse, delta):
    t_rows = q.shape[0]
    t = _attn_tile(t_rows)
    hp = MLA_HEADS_PER_STEP
    w = hp * LANES
    ii, jj = _causal_pairs(t_rows // t, by_kv=True)

    def body(i_ref, j_ref, q_ref, k_ref, v_ref, do_ref, lse_ref, dl_ref, dq_ref, dk_ref, dv_ref):
        i = i_ref[pl.program_id(1)]
        j = j_ref[pl.program_id(1)]

        @pl.when(pl.program_id(1) == 0)
        def _():
            dq_ref[...] = jnp.zeros(dq_ref.shape, F32)

        def step(diagonal):
            r0 = pl.multiple_of(i * t, t)
            for hh in range(hp):
                sl = slice(hh * LANES, (hh + 1) * LANES)
                qv = q_ref[:, sl]
                kv = k_ref[:, sl]
                dov = do_ref[:, sl]
                s = _dot_nt(kv, qv)
                if diagonal:
                    valid = (lax.broadcasted_iota(jnp.int32, (t, t), 0) <= lax.broadcasted_iota(jnp.int32, (t, t), 1))
                    s = jnp.where(valid, s, NEG)
                p = jnp.exp(s - lse_ref[hh])
                dv = _dot(p.astype(BF16), dov)
                dp = _dot_nt(v_ref[:, sl], dov)
                ds = (p * (dp - dl_ref[hh])).astype(BF16)
                dk = _dot(ds, qv)
                if diagonal:
                    dv_ref[:, sl] = dv
                    dk_ref[:, sl] = dk
                else:
                    dv_ref[:, sl] += dv
                    dk_ref[:, sl] += dk
                dq_ref[hh, pl.ds(r0, t), :] += _dot_tn(ds, kv)

        pl.when(i > j)(lambda: step(False))
        pl.when(i == j)(lambda: step(True))

    def qmap(hb, s, ir, jr):
        return (ir[s], hb)

    def kvmap(hb, s, ir, jr):
        return (jr[s], hb)

    def rowmap(hb, s, ir, jr):
        return (hb, 0, ir[s])

    grid_spec = pltpu.PrefetchScalarGridSpec(
        num_scalar_prefetch=2, grid=(HEADS // hp, ii.shape[0]),
        in_specs=[pl.BlockSpec((t, w), qmap), pl.BlockSpec((t, w), kvmap), pl.BlockSpec((t, w), kvmap),
                  pl.BlockSpec((t, w), qmap), pl.BlockSpec((hp, 1, t), rowmap), pl.BlockSpec((hp, 1, t), rowmap)],
        out_specs=[pl.BlockSpec((hp, t_rows, LANES), lambda hb, s, ir, jr: (hb, 0, 0)),
                   pl.BlockSpec((t, w), kvmap), pl.BlockSpec((t, w), kvmap)])
    return pl.pallas_call(
        body, name="mla_bwd", grid_spec=grid_spec,
        out_shape=[_sds((HEADS, t_rows, LANES), F32), _sds((t_rows, HEADS * LANES), F32),
                   _sds((t_rows, HEADS * LANES), F32)],
        compiler_params=pltpu.CompilerParams(dimension_semantics=("arbitrary",) * 2, vmem_limit_bytes=VMEM_LIMIT),
    )(ii, jj, q, k, v, do, lse, delta)


SWA_TILE = 2 * SWA_WINDOW
SWA_GROUP = HEADS // A_KV_HEADS


def _swa_bias(tq):
    koff = lax.broadcasted_iota(jnp.int32, (tq + SWA_WINDOW, SWA_GROUP * tq), 0) - SWA_WINDOW
    qoff = (lax.broadcasted_iota(jnp.int32, (tq + SWA_WINDOW, SWA_GROUP * tq), 1) % tq)
    band = (koff <= qoff) & (qoff - koff < SWA_WINDOW)
    return jnp.stack([jnp.where(band & (koff >= 0), 0.0, NEG), jnp.where(band, 0.0, NEG)]).astype(F32)


def _swa_specs(tq, nq):
    wb = tq // SWA_WINDOW

    def qi(i):
        return jnp.minimum(i, nq - 1)

    q = pl.BlockSpec((tq, SWA_GROUP * LANES), lambda h, i: (qi(i), h))
    cur = pl.BlockSpec((tq, LANES), lambda h, i: (qi(i), h))
    prev = pl.BlockSpec((SWA_WINDOW, LANES), lambda h, i: (jnp.maximum(qi(i) * wb - 1, 0), h))
    bias = pl.BlockSpec((1, tq + SWA_WINDOW, SWA_GROUP * tq), lambda h, i: (jnp.minimum(i, 1), 0, 0))
    rows = pl.BlockSpec((1, 1, 1, SWA_GROUP * tq), lambda h, i: (h, qi(i), 0, 0))
    sink = pl.BlockSpec((1, 1, SWA_GROUP * tq), lambda h, i: (h, 0, 0))
    return q, cur, prev, bias, rows, sink


def _stack_heads(ref):
    return jnp.concatenate([ref[:, g * LANES:(g + 1) * LANES] for g in range(SWA_GROUP)], axis=0)


def _swa_fwd(q, k, v, bias, sink_rows):
    t_rows = q.shape[0]
    tq = min(SWA_TILE, t_rows)
    nq = t_rows // tq
    qs_, cur, prev, bs, rows, sk = _swa_specs(tq, nq)

    def body(q_ref, kc_ref, kp_ref, vc_ref, vp_ref, b_ref, sink_ref, o_ref, lse_ref):
        qs = _stack_heads(q_ref)
        kk = jnp.concatenate([kp_ref[...], kc_ref[...]], axis=0)
        vv = jnp.concatenate([vp_ref[...], vc_ref[...]], axis=0)
        s = _dot_nt(kk, qs) + b_ref[0]
        sink = sink_ref[0]
        m = jnp.maximum(jnp.max(s, axis=0, keepdims=True), sink)
        p = jnp.exp(s - m)
        l = jnp.sum(p, axis=0, keepdims=True) + jnp.exp(sink - m)
        o = (_dot_tn(vv, p.astype(BF16)) / l).T
        for g in range(SWA_GROUP):
            o_ref[:, g * LANES:(g + 1) * LANES] = o[g * tq:(g + 1) * tq].astype(o_ref.dtype)
        lse_ref[0, 0] = m + jnp.log(l)

    return pl.pallas_call(
        body, name="swa_fwd", grid=(A_KV_HEADS, nq),
        in_specs=[qs_, cur, prev, cur, prev, bs, sk],
        out_specs=[qs_, rows],
        out_shape=[_sds((t_rows, HEADS * LANES), BF16), _sds((A_KV_HEADS, nq, 1, SWA_GROUP * tq), F32)],
        compiler_params=pltpu.CompilerParams(dimension_semantics=("arbitrary",) * 2, vmem_limit_bytes=VMEM_LIMIT),
    )(q, k, k, v, v, bias, sink_rows)


def _swa_bwd(q, k, v, o, do, lse, bias, sink_rows):
    t_rows = q.shape[0]
    tq = min(SWA_TILE, t_rows)
    nq = t_rows // tq
    qs_, cur, prev, bs, rows, sk = _swa_specs(tq, nq)
    hw = SWA_WINDOW

    def body(q_ref, kc_ref, kp_ref, vc_ref, vp_ref, o_ref, do_ref, lse_ref, b_ref, sink_ref,
             dq_ref, dk_ref, dv_ref, dsink_ref, ck, cv, dsa):
        i = pl.program_id(1)

        @pl.when(i == 0)
        def _():
            dsa[...] = jnp.zeros(dsa.shape, F32)

        @pl.when(i < nq)
        def _():
            qs = _stack_heads(q_ref)
            dos = _stack_heads(do_ref)
            kk = jnp.concatenate([kp_ref[...], kc_ref[...]], axis=0)
            vv = jnp.concatenate([vp_ref[...], vc_ref[...]], axis=0)
            lse = lse_ref[0, 0]
            p = jnp.exp(_dot_nt(kk, qs) + b_ref[0] - lse)
            delta = jnp.sum((_stack_heads(o_ref).astype(F32) * dos.astype(F32)).T, axis=0, keepdims=True)
            dsa[...] += -jnp.exp(sink_ref[0] - lse) * delta
            dv = _dot(p.astype(BF16), dos)
            ds = (p * (_dot_nt(vv, dos) - delta)).astype(BF16)
            dk = _dot(ds, qs)
            dq = _dot_tn(ds, kk)
            for g in range(SWA_GROUP):
                dq_ref[:, g * LANES:(g + 1) * LANES] = dq[g * tq:(g + 1) * tq]

            @pl.when(i > 0)
            def _():
                dk_ref[0:tq - hw, :] = ck[0:tq - hw, :]
                dk_ref[tq - hw:tq, :] = ck[tq - hw:tq, :] + dk[0:hw]
                dv_ref[0:tq - hw, :] = cv[0:tq - hw, :]
                dv_ref[tq - hw:tq, :] = cv[tq - hw:tq, :] + dv[0:hw]

            ck[...] = dk[hw:hw + tq]
            cv[...] = dv[hw:hw + tq]

        @pl.when(i == nq)
        def _():
            dk_ref[...] = ck[...]
            dv_ref[...] = cv[...]
            dsink_ref[...] = jnp.zeros(dsink_ref.shape, F32)
            for g in range(SWA_GROUP):
                tot = jnp.sum(dsa[:, g * tq:(g + 1) * tq], axis=1, keepdims=True)
                dsink_ref[0, g:g + 1, :] = jnp.zeros((1, LANES), F32) + tot

    kv_out = pl.BlockSpec((tq, LANES), lambda h, i: (jnp.maximum(i - 1, 0), h))
    return pl.pallas_call(
        body, name="swa_bwd", grid=(A_KV_HEADS, nq + 1),
        in_specs=[qs_, cur, prev, cur, prev, qs_, qs_, rows, bs, sk],
        out_specs=[qs_, kv_out, kv_out, pl.BlockSpec((1, 8, LANES), lambda h, i: (h, 0, 0))],
        out_shape=[_sds((t_rows, HEADS * LANES), F32), _sds((t_rows, A_KV_HEADS * LANES), F32),
                   _sds((t_rows, A_KV_HEADS * LANES), F32), _sds((A_KV_HEADS, 8, LANES), F32)],
        scratch_shapes=[pltpu.VMEM((tq, LANES), F32), pltpu.VMEM((tq, LANES), F32),
                        pltpu.VMEM((1, SWA_GROUP * tq), F32)],
        compiler_params=pltpu.CompilerParams(dimension_semantics=("arbitrary",) * 2, vmem_limit_bytes=VMEM_LIMIT),
    )(q, k, k, v, v, o, do, lse, bias, sink_rows)


def _fwd_mix(x, ya, yb, gate, wba, wbb, wout, g2, g3, tm):
    t_rows = x.shape[0]

    def body(x_ref, ya_ref, yb_ref, gate_ref, wba_ref, wbb_ref, wout_ref, g2_ref, g3_ref,
             pa_ref, pb_ref, mixed_ref, o_ref, x1_ref, h2_ref):
        pa = _dot(ya_ref[...], wba_ref[...])
        pb = _dot(yb_ref[...], wbb_ref[...])
        pa_ref[...] = pa
        pb_ref[...] = pb
        mixed = (gate_ref[:, 0:D_MODEL] * pa + gate_ref[:, D_MODEL:2 * D_MODEL] * pb).astype(BF16)
        mixed_ref[...] = mixed
        o = _dot(mixed, wout_ref[...])
        o_ref[...] = o
        on, _ = _rms_stats(o)
        x1 = x_ref[...] + on * g2_ref[...]
        x1_ref[...] = x1
        x1n, _ = _rms_stats(x1)
        h2_ref[...] = (x1n * g3_ref[...]).astype(BF16)

    def o_(dt):
        return (_sds((t_rows, D_MODEL), dt), _row(tm, D_MODEL))

    ins = [(x, _row(tm, D_MODEL)), (ya, _row(tm, 1024)), (yb, _row(tm, 1024)), (gate, _row(tm, 2048)),
           (wba, _resident(wba.shape)), (wbb, _resident(wbb.shape)), (wout, _resident(wout.shape)),
           (g2, _full(g2.shape)), (g3, _full(g3.shape))]
    return _rows_call("fwd_mix", body, t_rows, tm, ins, [o_(F32), o_(F32), o_(BF16), o_(F32), o_(F32), o_(BF16)])


CONV_CHUNK = 1408


def _fwd_up(h2, wup, convw8, convb, tm):
    t_rows = h2.shape[0]
    cdim = 2 * D_FF

    def body(h2_ref, wup_ref, cw_ref, cb_ref, up_ref, a_ref, carry):
        i = pl.program_id(0)

        @pl.when(i == 0)
        def _():
            carry[...] = jnp.zeros(carry.shape, F32)

        hb = h2_ref[...]

        def conv(c0):
            sl = slice(c0, c0 + CONV_CHUNK)
            up = _dot(hb, wup_ref[c0 // CONV_CHUNK])
            up_ref[:, sl] = up
            xm1, xm2 = _conv_taps(up, carry[6:7, sl], carry[7:8, sl])
            u = cw_ref[0:1, sl] * xm2 + cw_ref[1:2, sl] * xm1 + cw_ref[2:3, sl] * up + cb_ref[:, sl]
            carry[:, sl] = up[tm - 8:tm, :]
            return u

        for c0 in range(0, D_FF, CONV_CHUNK):
            ug = conv(c0)
            uv = conv(D_FF + c0)
            gel, _ = _gelu_and_grad(ug)
            a_ref[:, c0:c0 + CONV_CHUNK] = (gel * uv).astype(BF16)

    ins = [(h2, _row(tm, D_MODEL)), (wup, _resident(wup.shape)), (convw8, _full(convw8.shape)), (convb, _full(convb.shape))]
    outs = [(_sds((t_rows, cdim), F32), _row(tm, cdim)), (_sds((t_rows, D_FF), BF16), _row(tm, D_FF))]
    return _rows_call("fwd_up", body, t_rows, tm, ins, outs, scratch=[pltpu.VMEM((8, cdim), F32)])


def _fwd_out(a, wdown, x1, g4, p, wple, g5, wpg, tgt, tm):
    t_rows = a.shape[0]

    def body(a_ref, wdown_ref, x1_ref, g4_ref, p_ref, wple_ref, g5_ref, wpg_ref, tgt_ref,
             ff_ref, x2_ref, e_ref, n5_ref, sg_ref, dx3_ref, loss_ref):
        i = pl.program_id(0)
        ff = _dot(a_ref[...], wdown_ref[...])
        ff_ref[...] = ff
        ffn, _ = _rms_stats(ff)
        x2 = x1_ref[...] + ffn * g4_ref[...]
        x2_ref[...] = x2
        e = _dot(p_ref[...].astype(BF16), wple_ref[...])
        e_ref[...] = e
        x2n, _ = _rms_stats(x2)
        n5 = (x2n * g5_ref[...]).astype(BF16)
        n5_ref[...] = n5
        sg = _sigmoid(_dot(n5, wpg_ref[...]))
        sg_ref[...] = sg
        d = x2 + sg * e - tgt_ref[...]
        dx3_ref[...] = d * (1.0 / D_MODEL)

        @pl.when(i == 0)
        def _():
            loss_ref[...] = jnp.zeros((1, 1), F32)

        loss_ref[...] += 0.5 * jnp.sum(jnp.sum(d * d, axis=1, keepdims=True), axis=0, keepdims=True) * (1.0 / D_MODEL)

    def o_(dt):
        return (_sds((t_rows, D_MODEL), dt), _row(tm, D_MODEL))

    ins = [(a, _row(tm, D_FF)), (wdown, _resident(wdown.shape)), (x1, _row(tm, D_MODEL)), (g4, _full(g4.shape)),
           (p, _row(tm, PLE_DIM)), (wple, _full(wple.shape)), (g5, _full(g5.shape)), (wpg, _resident(wpg.shape)),
           (tgt, _row(tm, D_MODEL))]
    outs = [o_(F32), o_(F32), o_(F32), o_(BF16), o_(F32), o_(F32), (_sds((1, 1), F32), _full((1, 1)))]
    return _rows_call("fwd_out", body, t_rows, tm, ins, outs)


def _bwd_out(dx3, e, sg, x2, ff, g5, g4, wpg, wdown, up, convw8, convb, tm):
    t_rows = dx3.shape[0]
    cdim = 2 * D_FF
    hb = tm // 8

    def body(dx3_ref, e_ref, sg_ref, x2_ref, ff_ref, g5_ref, g4_ref, wpg_ref, wdown_ref, up_ref, halo_ref, cw_ref,
             cb_ref, dpre_ref, de_ref, dx2_ref, dff_ref, du_ref, dg5_ref, dg4_ref, dcb_ref, dcw_ref):
        i = pl.program_id(0)

        @pl.when(i == 0)
        def _():
            dg5_ref[...] = jnp.zeros(dg5_ref.shape, F32)
            dg4_ref[...] = jnp.zeros(dg4_ref.shape, F32)
            dcb_ref[...] = jnp.zeros(dcb_ref.shape, F32)
            dcw_ref[...] = jnp.zeros(dcw_ref.shape, F32)

        dx3 = dx3_ref[...]
        sg = sg_ref[...]
        dpre = (dx3 * e_ref[...] * sg * (1.0 - sg)).astype(BF16)
        dpre_ref[...] = dpre
        de_ref[...] = (dx3 * sg).astype(BF16)
        dn5 = _dot_nt(dpre, wpg_ref[...])
        x2n, r5 = _rms_stats(x2_ref[...])
        d2, dg5 = _rms_bwd(dn5, x2n, r5, g5_ref[...])
        dx2 = dx3 + d2
        dx2_ref[...] = dx2
        dg5_ref[...] += dg5
        ffn, r4 = _rms_stats(ff_ref[...])
        dff, dg4 = _rms_bwd(dx2, ffn, r4, g4_ref[...])
        dg4_ref[...] += dg4
        dffb = dff.astype(BF16)
        dff_ref[...] = dffb
        keep = jnp.where(i > 0, 1.0, 0.0)

        def conv(c0):
            sl = slice(c0, c0 + CONV_CHUNK)
            up = up_ref[:, sl]
            xm1, xm2 = _conv_taps(up, halo_ref[6:7, sl] * keep, halo_ref[7:8, sl] * keep)
            u = cw_ref[0:1, sl] * xm2 + cw_ref[1:2, sl] * xm1 + cw_ref[2:3, sl] * up + cb_ref[:, sl]
            return u, up, xm1, xm2

        def grads(c0, du, up, xm1, xm2):
            sl = slice(c0, c0 + CONV_CHUNK)
            du_ref[:, sl] = du.astype(BF16)
            dcb_ref[:, sl] += jnp.sum(du, axis=0, keepdims=True)
            dcw_ref[0:1, sl] += jnp.sum(du * xm2, axis=0, keepdims=True)
            dcw_ref[1:2, sl] += jnp.sum(du * xm1, axis=0, keepdims=True)
            dcw_ref[2:3, sl] += jnp.sum(du * up, axis=0, keepdims=True)

        for c0 in range(0, D_FF, CONV_CHUNK):
            da = _dot_nt(dffb, wdown_ref[c0:c0 + CONV_CHUNK, :])
            ug, *rg = conv(c0)
            uv, *rv = conv(D_FF + c0)
            gel, dgel = _gelu_and_grad(ug)
            grads(c0, da * uv * dgel, *rg)
            grads(D_FF + c0, da * gel, *rv)

    def o_(n, dt):
        return (_sds((t_rows, n), dt), _row(tm, n))

    def acc(r, n):
        return (_sds((r, n), F32), _full((r, n)))

    halo = pl.BlockSpec((8, cdim), lambda i: (jnp.maximum(i * hb - 1, 0), 0))
    ins = [(dx3, _row(tm, D_MODEL)), (e, _row(tm, D_MODEL)), (sg, _row(tm, D_MODEL)), (x2, _row(tm, D_MODEL)),
           (ff, _row(tm, D_MODEL)), (g5, _full(g5.shape)), (g4, _full(g4.shape)), (wpg, _resident(wpg.shape)),
           (wdown, _resident(wdown.shape)), (up, _row(tm, cdim)), (up, halo), (convw8, _full(convw8.shape)),
           (convb, _full(convb.shape))]
    outs = [o_(D_MODEL, BF16), o_(D_MODEL, BF16), o_(D_MODEL, F32), o_(D_MODEL, BF16), o_(cdim, BF16),
            acc(1, D_MODEL), acc(1, D_MODEL), acc(1, cdim), acc(8, cdim)]
    return _rows_call("bwd_out", body, t_rows, tm, ins, outs)


def _bwd_mid(du, convw8, wup, dx2, x1, g3, o, g2, wout, gate, pa, pb, wba, wbb, yb, tm):
    t_rows = du.shape[0]
    cdim = 2 * D_FF
    halo_rows = 16
    hb = tm // halo_rows
    last_blk = t_rows // halo_rows - 1
    n_tiles = t_rows // tm

    def body(du_ref, halo_ref, cw_ref, wup_ref, dx2_ref, x1_ref, g3_ref, o_ref, g2_ref, wout_ref, gate_ref, pa_ref,
             pb_ref, wba_ref, wbb_ref, yb_ref,
             dup_ref, dx1_ref, do_ref, dpa_ref, dpb_ref, dgt_ref, dya_ref, dyb_ref, dl_ref, dg3_ref, dg2_ref, dbg_ref):
        i = pl.program_id(0)

        @pl.when(i == 0)
        def _():
            dg3_ref[...] = jnp.zeros(dg3_ref.shape, F32)
            dg2_ref[...] = jnp.zeros(dg2_ref.shape, F32)
            dbg_ref[...] = jnp.zeros(dbg_ref.shape, F32)

        keep = jnp.where(i < n_tiles - 1, 1.0, 0.0)
        dh2 = jnp.zeros((tm, D_MODEL), F32)
        for c0 in range(0, cdim, CONV_CHUNK):
            sl = slice(c0, c0 + CONV_CHUNK)
            du = du_ref[:, sl].astype(F32)
            nxt = halo_ref[:, sl].astype(F32)
            xp1, xp2 = _conv_taps_next(du, nxt[0:1] * keep, nxt[1:2] * keep)
            dup = (cw_ref[2:3, sl] * du + cw_ref[1:2, sl] * xp1 + cw_ref[0:1, sl] * xp2).astype(BF16)
            dup_ref[:, sl] = dup
            dh2 = dh2 + _dot_nt(dup, wup_ref[c0 // CONV_CHUNK])
        x1n, r3 = _rms_stats(x1_ref[...])
        d1, dg3 = _rms_bwd(dh2, x1n, r3, g3_ref[...])
        dx1 = dx2_ref[...] + d1
        dx1_ref[...] = dx1
        dg3_ref[...] += dg3
        on, r2 = _rms_stats(o_ref[...])
        do, dg2 = _rms_bwd(dx1, on, r2, g2_ref[...])
        dg2_ref[...] += dg2
        dob = do.astype(BF16)
        do_ref[...] = dob
        dmixed = _dot_nt(dob, wout_ref[...])
        ga = gate_ref[:, 0:D_MODEL]
        gb = gate_ref[:, D_MODEL:2 * D_MODEL]
        dpa = (dmixed * ga).astype(BF16)
        dpb = (dmixed * gb).astype(BF16)
        dpa_ref[...] = dpa
        dpb_ref[...] = dpb
        dga = dmixed * pa_ref[...] * ga * (1.0 - ga)
        dgb = dmixed * pb_ref[...] * gb * (1.0 - gb)
        dgt_ref[:, 0:D_MODEL] = dga.astype(BF16)
        dgt_ref[:, D_MODEL:2 * D_MODEL] = dgb.astype(BF16)
        dbg_ref[:, 0:D_MODEL] += jnp.sum(dga, axis=0, keepdims=True)
        dbg_ref[:, D_MODEL:2 * D_MODEL] += jnp.sum(dgb, axis=0, keepdims=True)
        dya_ref[...] = _dot_nt(dpa, wba_ref[...]).astype(BF16)
        dyb = _dot_nt(dpb, wbb_ref[...]).astype(BF16)
        dyb_ref[...] = dyb
        prod = yb_ref[...].astype(F32) * dyb.astype(F32)
        lane_head = lax.broadcasted_iota(jnp.int32, (HEADS, HEADS * LANES), 1) // LANES
        sel = (lane_head == lax.broadcasted_iota(jnp.int32, (HEADS, HEADS * LANES), 0)).astype(BF16)
        hi = prod.astype(BF16)
        lo = (prod - hi.astype(F32)).astype(BF16)
        dl_ref[...] = _dot_nt(sel, hi) + _dot_nt(sel, lo)

    def o_(n, dt):
        return (_sds((t_rows, n), dt), _row(tm, n))

    def acc(r, n):
        return (_sds((r, n), F32), _full((r, n)))

    halo = pl.BlockSpec((halo_rows, cdim), lambda i: (jnp.minimum((i + 1) * hb, last_blk), 0))
    ins = [(du, _row(tm, cdim)), (du, halo), (convw8, _full(convw8.shape)), (wup, _resident(wup.shape)),
           (dx2, _row(tm, D_MODEL)), (x1, _row(tm, D_MODEL)), (g3, _full(g3.shape)), (o, _row(tm, D_MODEL)),
           (g2, _full(g2.shape)), (wout, _resident(wout.shape)), (gate, _row(tm, 2048)), (pa, _row(tm, D_MODEL)),
           (pb, _row(tm, D_MODEL)), (wba, _resident(wba.shape)), (wbb, _resident(wbb.shape)), (yb, _row(tm, 1024))]
    outs = [o_(cdim, BF16), o_(D_MODEL, F32), o_(D_MODEL, BF16), o_(D_MODEL, BF16), o_(D_MODEL, BF16),
            o_(2048, BF16), o_(1024, BF16), o_(1024, BF16),
            (_sds((HEADS, t_rows), F32), pl.BlockSpec((HEADS, tm), lambda i: (0, i))),
            acc(1, D_MODEL), acc(1, D_MODEL), acc(1, 2048)]
    return _rows_call("bwd_mid", body, t_rows, tm, ins, outs)


def _bwd_in(dqs, dks, dvs, dqm, dkm, dvm, tabs, consts, cq, ckv, gq, gkv, wuq, wk, wv, dgates, win, x, g1, dx1, tm):
    t_rows = x.shape[0]

    def body(dqs_ref, dks_ref, dvs_ref, dqm_ref, dkm_ref, dvm_ref, ca, sa1, sa2, cb, sb1, sb2, c_ref, cq_ref,
             ckv_ref, gq_ref, gkv_ref, wuq_ref, wk_ref, wv_ref, dgt_ref, win_ref, x_ref, g1_ref, dx1_ref,
             dz_ref, dqb_ref, dx_ref, dgq_ref, dgkv_ref, dg1_ref):
        i = pl.program_id(0)

        @pl.when(i == 0)
        def _():
            dgq_ref[...] = jnp.zeros(dgq_ref.shape, F32)
            dgkv_ref[...] = jnp.zeros(dgkv_ref.shape, F32)
            dg1_ref[...] = jnp.zeros(dg1_ref.shape, F32)

        ta = (ca[...], sa1[...], sa2[...])
        tb = (cb[...], sb1[...], sb2[...])
        dz_ref[:, Z_QA:Z_KA] = _rope_t(dqs_ref[...] * SCALE_A, *ta, A_HEAD_DIM // 2).astype(BF16)
        dz_ref[:, Z_KA:Z_VA] = _rope_t(dks_ref[...], *ta, A_HEAD_DIM // 2).astype(BF16)
        dz_ref[:, Z_VA:Z_CQ] = dvs_ref[...].astype(BF16)
        dqm = jnp.concatenate([dqm_ref[h] for h in range(HEADS)], axis=1)
        dqb = _rope_t(dqm * SCALE_B, *tb, ROPE_DIM // 2).astype(BF16)
        dqb_ref[...] = dqb
        dcqn = _dot_nt(dqb, wuq_ref[...])
        cqn, rq = _rms_stats(cq_ref[...])
        dcq, dgq = _rms_bwd(dcqn, cqn, rq, gq_ref[...])
        dgq_ref[...] += dgq
        dz_ref[:, Z_CQ:Z_CKV] = dcq.astype(BF16)
        dkm = dkm_ref[...]
        dslot = dkm[:, 0:LANES]
        for h in range(1, HEADS):
            dslot = dslot + dkm[:, h * LANES:(h + 1) * LANES]
        dz_ref[:, Z_KR:Z_GATE] = _rope_t(dslot * c_ref[10:11, :], *tb, ROPE_DIM // 2).astype(BF16)
        dckvn = _dot_nt(dkm.astype(BF16), wk_ref[...]) + _dot_nt(dvm_ref[...].astype(BF16), wv_ref[...])
        ckvn, rkv = _rms_stats(ckv_ref[...])
        dckv, dgkv = _rms_bwd(dckvn, ckvn, rkv, gkv_ref[...])
        dgkv_ref[...] += dgkv
        dz_ref[:, Z_CKV:Z_KR] = dckv.astype(BF16)
        dz_ref[:, Z_GATE:ZW] = dgt_ref[...]
        dh1 = _dot_nt(dz_ref[...], win_ref[...])
        xn, r1 = _rms_stats(x_ref[...])
        d0, dg1 = _rms_bwd(dh1, xn, r1, g1_ref[...])
        dg1_ref[...] += dg1
        dx_ref[...] = dx1_ref[...] + d0

    def acc(n):
        return (_sds((1, n), F32), _full((1, n)))

    ins = [(dqs, _row(tm, 1024)), (dks, _row(tm, 256)), (dvs, _row(tm, 256)), (dqm, _heads(tm, HEADS)),
           (dkm, _row(tm, 1024)), (dvm, _row(tm, 1024))] + [(t, _row(tm, LANES)) for t in tabs] + [
           (consts, _full(consts.shape)), (cq, _row(tm, 256)), (ckv, _row(tm, 128)), (gq, _full(gq.shape)),
           (gkv, _full(gkv.shape)), (wuq, _full(wuq.shape)), (wk, _full(wk.shape)), (wv, _full(wv.shape)),
           (dgates, _row(tm, 2048)), (win, _resident(win.shape)), (x, _row(tm, D_MODEL)), (g1, _full(g1.shape)),
           (dx1, _row(tm, D_MODEL))]
    outs = [(_sds((t_rows, ZW), BF16), _row(tm, ZW)), (_sds((t_rows, 1024), BF16), _row(tm, 1024)),
            (_sds((t_rows, D_MODEL), F32), _row(tm, D_MODEL)), acc(256), acc(128), acc(D_MODEL)]
    return _rows_call("bwd_in", body, t_rows, tm, ins, outs)


def _pick_cols(n):
    best = LANES
    for d in range(LANES, min(n, 1408) + 1, LANES):
        if n % d == 0:
            best = d
    return best


def _mm_tn(name, a, b, column_shards=1, after=None):
    t_rows, m = a.shape
    n = b.shape[1]
    bk = min(1024, t_rows)
    bm, bn = _pick_cols(m), _pick_cols(n // column_shards)
    per_shard = n // column_shards // bn
    extra = () if after is None else (after,)

    def body(a_ref, b_ref, *rest):
        o_ref = rest[-1]

        @pl.when(pl.program_id(2) == 0)
        def _():
            o_ref[...] = jnp.zeros((bm, bn), F32)

        o_ref[...] += _dot_tn(a_ref[...].astype(BF16), b_ref[...].astype(BF16))

    return pl.pallas_call(
        body, name=name, grid=(m // bm, n // bn, t_rows // bk),
        in_specs=[pl.BlockSpec((bk, bm), lambda i, j, k: (k, i)), pl.BlockSpec((bk, bn), lambda i, j, k: (k, j))]
        + [pl.BlockSpec((8, LANES), lambda i, j, k: (0, 0))] * len(extra),
        out_specs=(pl.BlockSpec((bm, bn), lambda i, j, k: (i, j)) if column_shards == 1 else
                   pl.BlockSpec((None, bm, bn), lambda i, j, k: (j // per_shard, i, j % per_shard))),
        out_shape=_sds((m, n) if column_shards == 1 else (column_shards, m, n // column_shards), F32),
        compiler_params=pltpu.CompilerParams(dimension_semantics=("arbitrary",) * 3, vmem_limit_bytes=VMEM_LIMIT),
    )(a, b, *extra)


PACK_ROWS = 512


ADD_TILE_ELEMS = 1 << 17


def _add_rows(rows, cols):
    best = 16
    for d in range(16, rows + 1, 16):
        if rows % d == 0 and d * cols <= ADD_TILE_ELEMS:
            best = d
    assert rows % best == 0
    return best


def _add_pair(name, g, recv, half):
    _, _, rows, cols = g.shape
    t = _add_rows(rows, cols)

    def body(h_ref, g_ref, r_ref, o_ref):
        o_ref[...] = (g_ref[:, 0] + r_ref[...]).astype(BF16)

    spec = pl.BlockSpec((4, t, cols), lambda i, h: (0, i, 0))
    grid_spec = pltpu.PrefetchScalarGridSpec(
        num_scalar_prefetch=1, grid=(rows // t,),
        in_specs=[pl.BlockSpec((4, 1, t, cols), lambda i, h: (0, h[0], i, 0)), spec], out_specs=spec)
    return pl.pallas_call(body, name=name, grid_spec=grid_spec,
                          out_shape=_sds(recv.shape, BF16))(jnp.reshape(half, (1,)).astype(jnp.int32), g, recv)


def _add_chips(name, parts):
    _, rows, cols = parts.shape
    t = _add_rows(rows, cols)

    def body(p_ref, o_ref):
        acc = p_ref[0].astype(F32)
        for j in range(1, 4):
            acc = acc + p_ref[j].astype(F32)
        o_ref[...] = acc

    return pl.pallas_call(body, name=name, grid=(rows // t,),
                          in_specs=[pl.BlockSpec((4, t, cols), lambda i: (0, i, 0))],
                          out_specs=pl.BlockSpec((t, cols), lambda i: (i, 0)),
                          out_shape=_sds((rows, cols), F32))(parts)


def _add_devices(parts):
    n, rows, _ = parts.shape

    def body(p_ref, o_ref):
        acc = p_ref[0]
        for j in range(1, n):
            acc = acc + p_ref[j]
        o_ref[...] = acc

    return pl.pallas_call(body, name="small_add", grid=(1,),
                          in_specs=[pl.BlockSpec((n, rows, LANES), lambda i: (0, 0, 0))],
                          out_specs=pl.BlockSpec((rows, LANES), lambda i: (0, 0)),
                          out_shape=_sds((rows, LANES), F32))(parts)


def _adam_rows(k, n):
    target = max(8, (1 << 20) // (4 * n))
    if k <= target:
        return k
    best = None
    for d in range(8, target + 1, 8):
        if k % d == 0:
            best = d
    return best if best is not None else k


def _adam_update(w, g, m, v):
    m_ = ADAM_B1 * m + (1.0 - ADAM_B1) * g
    v_ = ADAM_B2 * v + (1.0 - ADAM_B2) * (g * g)
    delta = -ADAM_LR * ((m_ / (1.0 - ADAM_B1 ** ADAM_STEP)) / (jnp.sqrt(v_ / (1.0 - ADAM_B2 ** ADAM_STEP)) + ADAM_EPS)
                        + ADAM_WD * w)
    return delta, m_, v_


def _adamw(name, w, g, m, v):
    k, n = w.shape
    bk = _adam_rows(k, n)

    def body(w_ref, g_ref, m_ref, v_ref, d_ref, mo_ref, vo_ref):
        d_ref[...], mo_ref[...], vo_ref[...] = _adam_update(w_ref[...], g_ref[...], m_ref[...], v_ref[...])

    spec = pl.BlockSpec((bk, n), lambda i: (i, 0))
    out = pl.pallas_call(body, name=name, grid=(k // bk,), in_specs=[spec] * 4, out_specs=[spec] * 3,
                         out_shape=[_sds((k, n), F32)] * 3,
                         compiler_params=pltpu.CompilerParams(vmem_limit_bytes=VMEM_LIMIT))(w, g, m, v)
    return (g, *out)


def _adamw_halves(name, w, mine, theirs, m, v, half):
    k, n = w.shape
    bk = _adam_rows(k // 2, n)
    nb = k // 2 // bk

    def body(h_ref, w_ref, mine_ref, theirs_ref, m_ref, v_ref, g_ref, d_ref, mo_ref, vo_ref):
        g = jnp.where(pl.program_id(0) == h_ref[0], mine_ref[...], theirs_ref[...])
        g_ref[...] = g
        d_ref[...], mo_ref[...], vo_ref[...] = _adam_update(w_ref[...], g, m_ref[...], v_ref[...])

    full = pl.BlockSpec((bk, n), lambda h, i, c: (h * nb + i, 0))
    part = pl.BlockSpec((bk, n), lambda h, i, c: (i, 0))
    grid_spec = pltpu.PrefetchScalarGridSpec(num_scalar_prefetch=1, grid=(2, nb),
                                             in_specs=[full, part, part, full, full], out_specs=[full] * 4)
    return tuple(pl.pallas_call(
        body, name=name, grid_spec=grid_spec, out_shape=[_sds((k, n), F32)] * 4,
        compiler_params=pltpu.CompilerParams(vmem_limit_bytes=VMEM_LIMIT),
    )(jnp.reshape(half, (1,)).astype(jnp.int32), w, mine, theirs, m, v))


_HBM = pl.BlockSpec(memory_space=pltpu.HBM)


def _me():
    return lax.axis_index("x"), lax.axis_index("y"), lax.axis_index("c")


def _other_chips(x, y):
    return [(1 - x, y), (x, 1 - y), (1 - x, 1 - y)]


def _pass_to_sibling(zones):
    n = len(zones)

    def body(*refs):
        in_refs, out_refs = refs[:n], refs[n:2 * n]
        send_sems, recv_sems = refs[2 * n:]
        x, y, c = _me()
        sent = []
        for a, (in_ref, out_ref) in enumerate(zip(in_refs, out_refs)):
            for j, (cx, cy) in enumerate(_other_chips(x, y)):
                mine, theirs = (2 * cx + cy, c), (2 * cx + cy, 1 - c)
                sems = dict(send_sem=send_sems.at[3 * a + j], recv_sem=recv_sems.at[3 * a + j],
                            device_id=(x, y, 1 - c), device_id_type=MESH)
                sent.append((pltpu.make_async_remote_copy(src_ref=in_ref.at[mine], dst_ref=out_ref.at[mine], **sems),
                             pltpu.make_async_remote_copy(src_ref=in_ref.at[theirs], dst_ref=out_ref.at[theirs], **sems)))
        for send, _ in sent:
            send.start()
        for _, recv in sent:
            recv.wait_recv()
        for send, _ in sent:
            send.wait_send()

    return pl.pallas_call(
        body, name="pass_to_sibling", out_shape=[_sds(z.shape, z.dtype) for z in zones],
        in_specs=[_HBM] * n, out_specs=[_HBM] * n, input_output_aliases={i: i for i in range(n)},
        scratch_shapes=[pltpu.SemaphoreType.DMA((3 * n,)), pltpu.SemaphoreType.DMA((3 * n,))],
    )(*zones)


def _swap_sibling(name, vs, other_half=False):
    n = len(vs)

    def body(*refs):
        v_refs, out_refs = refs[:n], refs[n:2 * n]
        send_sems, recv_sems = refs[2 * n:]
        x, y, c = _me()
        cps = [pltpu.make_async_remote_copy(src_ref=v_ref.at[:, 1 - c] if other_half else v_ref, dst_ref=out_ref,
                                            send_sem=send_sems.at[a], recv_sem=recv_sems.at[a],
                                            device_id=(x, y, 1 - c), device_id_type=MESH)
               for a, (v_ref, out_ref) in enumerate(zip(v_refs, out_refs))]
        for cp in cps:
            cp.start()
        for cp in cps:
            cp.wait()

    def landing(v):
        return _sds((v.shape[0],) + v.shape[2:] if other_half else v.shape, v.dtype)

    return pl.pallas_call(
        body, name=name, out_shape=[landing(v) for v in vs], in_specs=[_HBM] * n, out_specs=[_HBM] * n,
        scratch_shapes=[pltpu.SemaphoreType.DMA((n,)), pltpu.SemaphoreType.DMA((n,))],
    )(*vs)


_SEM = pl.BlockSpec(memory_space=pltpu.SEMAPHORE)
_EFFECT = pltpu.SideEffectType.DATAFLOW_SIDE_EFFECTING
WHOLE = "whole"
PIECE = "piece"
SIBLING_HALF = "sibling"
MY_HALF = "half"
EVERYONE = "everyone"
_COPIES = {WHOLE: 3, PIECE: 3, MY_HALF: 3, SIBLING_HALF: 1, EVERYONE: 7}


def _landing_shape(v, mode):
    return {WHOLE: (4,) + v.shape, MY_HALF: (4,) + v.shape, PIECE: v.shape, EVERYONE: (8,) + v.shape,
            SIBLING_HALF: (v.shape[0],) + v.shape[2:]}[mode]


def _chip_copies(v_ref, land_ref, send_sems, recv_sems, mode, sem0=0):
    x, y, c = _me()
    if mode == SIBLING_HALF:
        cp = pltpu.make_async_remote_copy(src_ref=v_ref.at[:, 1 - c], dst_ref=land_ref, send_sem=send_sems.at[sem0],
                                          recv_sem=recv_sems.at[sem0], device_id=(x, y, 1 - c), device_id_type=MESH)
        return [(cp, cp)]
    if mode == EVERYONE:
        out = []
        for f in range(1, 8):
            px, py, pc = (1 - x if f & 4 else x), (1 - y if f & 2 else y), (1 - c if f & 1 else c)
            sems = dict(send_sem=send_sems.at[sem0 + f - 1], recv_sem=recv_sems.at[sem0 + f - 1],
                        device_id=(px, py, pc), device_id_type=MESH)
            out.append((pltpu.make_async_remote_copy(src_ref=v_ref, dst_ref=land_ref.at[4 * x + 2 * y + c], **sems),
                        pltpu.make_async_remote_copy(src_ref=v_ref, dst_ref=land_ref.at[4 * px + 2 * py + pc], **sems)))
        return out
    k = 2 * x + y
    out = []
    for j, (cx, cy) in enumerate(_other_chips(x, y)):
        if mode == MY_HALF:
            src, mine, theirs = v_ref.at[c], land_ref.at[k, c], land_ref.at[2 * cx + cy, c]
        else:
            src = v_ref.at[2 * cx + cy] if mode == PIECE else v_ref
            mine, theirs = land_ref.at[k], land_ref.at[2 * cx + cy]
        sems = dict(send_sem=send_sems.at[sem0 + j], recv_sem=recv_sems.at[sem0 + j], device_id=(cx, cy, c),
                    device_id_type=MESH)
        send = pltpu.make_async_remote_copy(src_ref=src, dst_ref=mine, **sems)
        recv = pltpu.make_async_remote_copy(src_ref=src, dst_ref=theirs, **sems)
        out.append((send, recv))
    return out


def _chips_start(name, vs, mode, after=None):
    n = len(vs)
    lands = [_landing_shape(v, mode) for v in vs]

    def body(*refs):
        v_refs, land_refs = refs[:n], refs[n:2 * n]
        send_sems, recv_sems = refs[-2 * n - 3], refs[-2 * n - 2]
        token = refs[-1]
        for a in range(n):
            for send, _ in _chip_copies(v_refs[a], land_refs[a], send_sems, recv_sems, mode, _COPIES[mode] * a):
                send.start()
        token[...] = jnp.zeros_like(token)

    extra = () if after is None else (after,)
    hbm = [pltpu.with_memory_space_constraint(v, pltpu.HBM) for v in vs]
    zones = [pltpu.with_memory_space_constraint(lax.empty(s, v.dtype), pltpu.HBM) for s, v in zip(lands, vs)]
    out = pl.pallas_call(
        body, name=name,
        out_shape=(pltpu.SemaphoreType.DMA((_COPIES[mode] * n,)), pltpu.SemaphoreType.DMA((_COPIES[mode] * n,)),
                   *[pltpu.HBM(v.shape, v.dtype) for v in vs], *[pltpu.HBM(s, v.dtype) for s, v in zip(lands, vs)],
                   _sds((8, LANES), F32)),
        in_specs=(_HBM,) * (2 * n) + (pl.BlockSpec(memory_space=pl.ANY),) * len(extra),
        out_specs=(_SEM, _SEM) + (_HBM,) * (2 * n) + (pl.BlockSpec(memory_space=pltpu.VMEM),),
        input_output_aliases={i: 2 + i for i in range(2 * n)},
        compiler_params=pltpu.CompilerParams(has_side_effects=_EFFECT),
    )(*hbm, *zones, *extra)
    return out[0], out[1], list(out[2:2 + n]), list(out[2 + n:2 + 2 * n]), out[-1]


def _chips_wait(name, send_sems, recv_sems, v_thru, land_thru, mode, after):
    n = len(v_thru)

    def body(*refs):
        v_refs, land_refs = refs[:n], refs[n:2 * n]
        send_sems, recv_sems = refs[2 * n], refs[2 * n + 1]
        for a in range(n):
            for send, recv in _chip_copies(v_refs[a], land_refs[a], send_sems, recv_sems, mode, _COPIES[mode] * a):
                send.wait_send()
                recv.wait_recv()

    out = pl.pallas_call(
        body, name=name,
        out_shape=tuple(pltpu.HBM(a.shape, a.dtype) for a in list(v_thru) + list(land_thru)),
        in_specs=(_HBM,) * (2 * n) + (_SEM, _SEM, pl.BlockSpec(memory_space=pl.ANY)), out_specs=(_HBM,) * (2 * n),
        input_output_aliases={i: i for i in range(2 * n)},
        compiler_params=pltpu.CompilerParams(has_side_effects=_EFFECT),
    )(*v_thru, *land_thru, send_sems, recv_sems, after)
    return list(out[:n]), list(out[n:])


_BIG = (("w_in", (1024, 3232), 1), ("w_uq", (256, 768), 1), ("w_ukv", (128, 1024), 1), ("w_branch_a", (512, 1024), 1),
        ("w_branch_b", (512, 1024), 1), ("w_out", (1024, 1024), 0), ("w_up", (1024, 5632), 1),
        ("w_down", (2816, 1024), 0), ("w_ple_gate", (1024, 1024), 0), ("w_ple", (256, 1024), 1))


def _shard_shape(shape, axis):
    return (shape[0] // 4, shape[1]) if axis == 0 else (shape[0], shape[1] // 4)


def _half_rows(shape, axis):
    k, n = _shard_shape(shape, axis)
    return k * n // (2 * LANES)


_EARLY = ("w_in", "w_uq", "w_ukv")
_LATE = ("w_branch_a", "w_branch_b", "w_out", "w_up", "w_down", "w_ple_gate", "w_ple")
_NATURAL = ("w_in", "w_up", "w_down", "w_out", "w_ple_gate")
_EARLY_PACKED = tuple(b for b in _BIG if b[0] in _EARLY and b[0] not in _NATURAL)
_LATE_PACKED = tuple(b for b in _BIG if b[0] in _LATE and b[0] not in _NATURAL)
_SHARD = {name: _shard_shape(shape, axis) for name, shape, axis in _BIG}


def _halves(a):
    return a.reshape(a.shape[:-2] + (2, a.shape[-2] // 2, a.shape[-1]))


def _rows_joined(a):
    return a.reshape(a.shape[:-3] + (a.shape[-3] * a.shape[-2], a.shape[-1]))


def _pack_pad(group):
    return -sum(_half_rows(shape, axis) for _, shape, axis in group) % PACK_ROWS


def _pack_shards(shards, dtype, group):
    parts = [shards[name].astype(dtype).reshape(2, _half_rows(shape, axis), LANES) for name, shape, axis in group]
    return jnp.concatenate(parts + [jnp.zeros((2, _pack_pad(group), LANES), dtype)], axis=1)


def _unpack_gathered(g, group):
    out, off = {}, 0
    for name, shape, axis in group:
        r = _half_rows(shape, axis)
        k, n = _shard_shape(shape, axis)
        w = g[:, :, off:off + r, :].reshape(4, k, n)
        out[name] = w.reshape(shape) if axis == 0 else w.transpose(1, 0, 2).reshape(shape)
        off += r
    return out


def _pack_grads(grads, group):
    parts = []
    for name, shape, axis in group:
        k, n = _shard_shape(shape, axis)
        g = grads[name]
        g4 = g.reshape(4, k, n) if axis == 0 else g.reshape(k, 4, n).transpose(1, 0, 2)
        parts.append(g4.reshape(4, 2, _half_rows(shape, axis), LANES))
    return jnp.concatenate(parts + [jnp.zeros((4, 2, _pack_pad(group), LANES), F32)], axis=2)


def _unpack_shard_grads(f, group):
    out, off = {}, 0
    for name, shape, axis in group:
        r = _half_rows(shape, axis)
        out[name] = f[:, off:off + r, :].reshape(_shard_shape(shape, axis))
        off += r
    return out


def _pad_slots(w, heads, dim, axis):
    if axis == 1:
        k = w.shape[0]
        return jnp.pad(w.reshape(k, heads, dim), ((0, 0), (0, 0), (0, LANES - dim))).reshape(k, heads * LANES)
    n = w.shape[1]
    return jnp.pad(w.reshape(heads, dim, n), ((0, 0), (0, LANES - dim), (0, 0))).reshape(heads * LANES, n)


def _unpad_slots(w, heads, dim, axis):
    if axis == 1:
        k = w.shape[0]
        return w.reshape(k, heads, LANES)[:, :, :dim].reshape(k, heads * dim)
    n = w.shape[1]
    return w.reshape(heads, LANES, n)[:, :dim, :].reshape(heads * dim, n)


def _pad_w_in(w):
    kr = jnp.pad(w[:, 1152:1184], ((0, 0), (NOPE_DIM, LANES - NOPE_DIM - ROPE_DIM)))
    return jnp.concatenate([_pad_slots(w[:, 0:512], HEADS, A_HEAD_DIM, 1),
                            _pad_slots(w[:, 512:640], A_KV_HEADS, A_HEAD_DIM, 1),
                            _pad_slots(w[:, 640:768], A_KV_HEADS, A_HEAD_DIM, 1),
                            w[:, 768:1024], w[:, 1024:1152], kr, w[:, 1184:3232]], axis=1)


def _unpad_w_in(w):
    return jnp.concatenate([_unpad_slots(w[:, Z_QA:Z_KA], HEADS, A_HEAD_DIM, 1),
                            _unpad_slots(w[:, Z_KA:Z_VA], A_KV_HEADS, A_HEAD_DIM, 1),
                            _unpad_slots(w[:, Z_VA:Z_CQ], A_KV_HEADS, A_HEAD_DIM, 1),
                            w[:, Z_CQ:Z_CKV], w[:, Z_CKV:Z_KR],
                            w[:, Z_KR + NOPE_DIM:Z_KR + NOPE_DIM + ROPE_DIM], w[:, Z_GATE:ZW]], axis=1)


_SMALL = (("attn_pre_norm", 1024), ("attn_post_norm", 1024), ("b_gate", 2048), ("sinks", 8), ("q_a_norm", 256),
          ("kv_a_norm", 128), ("mlp_pre_norm", 1024), ("mlp_post_norm", 1024), ("conv_b", 5632), ("ple_norm", 1024),
          ("conv_w", 3 * 5632), ("loss", 1))


def _small_rows(n):
    return 8 * -(-n // (8 * LANES))


def _pack_small(vals):
    parts = []
    for name, n in _SMALL:
        r = _small_rows(n)
        parts.append(jnp.pad(vals[name].reshape(-1), (0, r * LANES - n)).reshape(r, LANES))
    return jnp.concatenate(parts, axis=0)


def _unpack_small(buf):
    out, off = {}, 0
    for name, n in _SMALL:
        r = _small_rows(n)
        out[name] = buf[off:off + r].reshape(-1)[:n]
        off += r
    return out


def kernel(x, p, positions, attn_pre_norm, attn_post_norm, w_in, b_gate, sinks, q_a_norm, w_uq, kv_a_norm, w_ukv, w_branch_a, w_branch_b, w_out, mlp_pre_norm, mlp_post_norm, w_up, conv_w, conv_b, w_down, ple_norm, w_ple_gate, w_ple, loss_target, m_attn_pre_norm, m_attn_post_norm, m_w_in, m_b_gate, m_sinks, m_q_a_norm, m_w_uq, m_kv_a_norm, m_w_ukv, m_w_branch_a, m_w_branch_b, m_w_out, m_mlp_pre_norm, m_mlp_post_norm, m_w_up, m_conv_w, m_conv_b, m_w_down, m_ple_norm, m_w_ple_gate, m_w_ple, v_attn_pre_norm, v_attn_post_norm, v_w_in, v_b_gate, v_sinks, v_q_a_norm, v_w_uq, v_kv_a_norm, v_w_ukv, v_w_branch_a, v_w_branch_b, v_w_out, v_mlp_pre_norm, v_mlp_post_norm, v_w_up, v_conv_w, v_conv_b, v_w_down, v_ple_norm, v_w_ple_gate, v_w_ple):
    names = ["attn_pre_norm", "attn_post_norm", "w_in", "b_gate", "sinks", "q_a_norm", "w_uq", "kv_a_norm", "w_ukv",
             "w_branch_a", "w_branch_b", "w_out", "mlp_pre_norm", "mlp_post_norm", "w_up", "conv_w", "conv_b",
             "w_down", "ple_norm", "w_ple_gate", "w_ple"]
    wts = dict(zip(names, [attn_pre_norm, attn_post_norm, w_in, b_gate, sinks, q_a_norm, w_uq, kv_a_norm, w_ukv,
                           w_branch_a, w_branch_b, w_out, mlp_pre_norm, mlp_post_norm, w_up, conv_w, conv_b, w_down,
                           ple_norm, w_ple_gate, w_ple]))
    moms = dict(zip(names, [m_attn_pre_norm, m_attn_post_norm, m_w_in, m_b_gate, m_sinks, m_q_a_norm, m_w_uq,
                            m_kv_a_norm, m_w_ukv, m_w_branch_a, m_w_branch_b, m_w_out, m_mlp_pre_norm,
                            m_mlp_post_norm, m_w_up, m_conv_w, m_conv_b, m_w_down, m_ple_norm, m_w_ple_gate, m_w_ple]))
    vars_ = dict(zip(names, [v_attn_pre_norm, v_attn_post_norm, v_w_in, v_b_gate, v_sinks, v_q_a_norm, v_w_uq,
                             v_kv_a_norm, v_w_ukv, v_w_branch_a, v_w_branch_b, v_w_out, v_mlp_pre_norm,
                             v_mlp_post_norm, v_w_up, v_conv_w, v_conv_b, v_w_down, v_ple_norm, v_w_ple_gate, v_w_ple]))
    w2 = {n: a.reshape(a.shape[-2:]) for n, a in wts.items()}
    m2 = {n: a.reshape(a.shape[-2:]) for n, a in moms.items()}
    v2 = {n: a.reshape(a.shape[-2:]) for n, a in vars_.items()}

    t_rows = x.shape[-2]
    tm = min(256, t_rows)
    tm_wide = min(512, t_rows)
    xc, yc, cc = lax.axis_index("x"), lax.axis_index("y"), lax.axis_index("c")
    chip = 2 * xc + yc

    x2d = x.reshape(t_rows, D_MODEL)
    p2d = p.reshape(t_rows, PLE_DIM)
    tgt = loss_target.reshape(t_rows, D_MODEL)
    pos_f = positions.reshape(t_rows, 1).astype(F32)

    def own_slot_filled(gathered, mine):
        return [lax.dynamic_update_slice(g, m[None], (chip, 0, 0, 0)) for g, m in zip(gathered, mine)]

    def shard_lists(group, packed_group, token=0.0):
        ws = {n: w2[n] + token for n in group}
        return [_halves(ws[n].astype(BF16)) for n in group if n in _NATURAL] + [_pack_shards(ws, BF16, packed_group)]

    cw_rows = 3 * 1408 // LANES
    conv_mine = jnp.pad(w2["conv_w"].reshape(cw_rows, LANES), ((0, 48 - cw_rows), (0, 0))).reshape(2, 24, LANES)
    early_mine = shard_lists(_EARLY, _EARLY_PACKED) + [conv_mine]
    early_sems = _chips_start("gather_early_start", early_mine, MY_HALF)
    early_token = early_sems[4][0:1, 0:1]
    consts = _rope_consts()
    tabs = _rope_tables(pos_f + early_token, consts, tm)
    late_mine = shard_lists(_LATE, _LATE_PACKED, early_token)
    both_done = tabs[0][0:1, 0:1] + sum(m[0, 0:1, 0:1].astype(F32) for m in late_mine)
    early_sent, early_landed = _chips_wait("gather_early_wait", *early_sems[:4], MY_HALF, after=both_done)
    early = own_slot_filled(_pass_to_sibling(early_landed), early_sent)
    late_sems = _chips_start("gather_late_start", late_mine, WHOLE, after=early[0])
    late_token = late_sems[4][0:1, 0:1]
    full = _unpack_gathered(early[1], _EARLY_PACKED)
    full["w_in"] = _rows_joined(early[0]).transpose(1, 0, 2).reshape(D_MODEL, 3232)
    conv_full = early[2].reshape(4, 48, LANES)[:, :cw_rows].reshape(4, 3, 1408).transpose(1, 0, 2).reshape(3, 2 * D_FF)
    convw8 = jnp.pad(conv_full, ((0, 5), (0, 0)))

    win = _pad_w_in(full["w_in"])
    wuq = _pad_slots(full["w_uq"], HEADS, NOPE_DIM + ROPE_DIM, 1)
    ukv = full["w_ukv"].reshape(KV_LORA, HEADS, NOPE_DIM + V_DIM)
    wk = _pad_slots(ukv[:, :, :NOPE_DIM].reshape(KV_LORA, HEADS * NOPE_DIM), HEADS, NOPE_DIM, 1)
    wv = _pad_slots(ukv[:, :, NOPE_DIM:].reshape(KV_LORA, HEADS * V_DIM), HEADS, V_DIM, 1)
    g1, g2, g3, g4, g5 = (w2["attn_pre_norm"], w2["attn_post_norm"], w2["mlp_pre_norm"], w2["mlp_post_norm"],
                          w2["ple_norm"])
    gq, gkv, bg, convb = w2["q_a_norm"], w2["kv_a_norm"], w2["b_gate"], w2["conv_b"]
    swa_tile = min(SWA_TILE, t_rows)
    sink_rows = jnp.repeat(w2["sinks"].reshape(A_KV_HEADS, SWA_GROUP, 1), swa_tile, axis=2).reshape(
        A_KV_HEADS, 1, SWA_GROUP * swa_tile)
    swa_bias = _swa_bias(swa_tile)

    h1, qs, ks, vs, cq, cqn, ckv, ckvn, qm, km, vm, gate = _fwd_in(x2d, g1, win, bg + late_token, gq, gkv, wuq, wk, wv,
                                                                   tabs, tm_wide)
    ya, lse_a = _swa_fwd(qs, ks, vs, swa_bias, sink_rows)
    yb, lse_b = _mla_fwd(qm, km, vm)
    late_sent, late_landed = _chips_wait("gather_late_wait", *late_sems[:4], WHOLE, after=yb)
    late = own_slot_filled(late_landed, late_sent)
    full = _unpack_gathered(late[-1], _LATE_PACKED)
    wba = _pad_slots(full["w_branch_a"], HEADS, A_HEAD_DIM, 0)
    wbb = _pad_slots(full["w_branch_b"], HEADS, V_DIM, 0)
    wple = full["w_ple"]
    natural = dict(zip([n for n in _LATE if n in _NATURAL], late))
    wup = _rows_joined(natural["w_up"])
    wout, wdown, wpg = (_rows_joined(natural[n]).reshape(-1, D_MODEL) for n in ("w_out", "w_down", "w_ple_gate"))
    pa, pb, mixed, o, x1, h2 = _fwd_mix(x2d, ya, yb, gate, wba, wbb, wout, g2, g3, tm_wide)
    up, a = _fwd_up(h2, wup, convw8, convb, tm)
    ff, x2, e, n5, sg, dx3, loss_part = _fwd_out(a, wdown, x1, g4, p2d, wple, g5, wpg, tgt, tm_wide)

    dpre, de, dx2, dff, du, dg5, dg4, dconvb, dconvw8 = _bwd_out(dx3, e, sg, x2, ff, g5, g4, wpg, wdown, up, convw8,
                                                                 convb, tm)
    dup, dx1, do, dpa, dpb, dgates, dya, dyb, delta_b, dg3, dg2, dbg = _bwd_mid(
        du, convw8, wup, dx2, x1, g3, o, g2, wout, gate, pa, pb, wba, wbb, yb, tm)
    late_grads = {
        "w_branch_a": _unpad_slots(_mm_tn("dw_branch_a", ya, dpa), HEADS, A_HEAD_DIM, 0),
        "w_branch_b": _unpad_slots(_mm_tn("dw_branch_b", yb, dpb), HEADS, V_DIM, 0),
        "w_out": _mm_tn("dw_out", mixed, do).reshape(4, D_MODEL // 4, D_MODEL),
        "w_up": _mm_tn("dw_up", h2, dup, column_shards=4),
        "w_down": _mm_tn("dw_down", a, dff).reshape(4, D_FF // 4, D_MODEL),
        "w_ple_gate": _mm_tn("dw_ple_gate", n5, dpre).reshape(4, D_MODEL // 4, D_MODEL),
        "w_ple": _mm_tn("dw_ple", p2d, de),
    }

    def grad_views(grads, group, packed_group):
        return [_halves(grads[n]) for n in group if n in _NATURAL] + [_pack_grads(grads, packed_group)]

    def pair_sums(tag, views, theirs):
        return [_add_pair("rs_%s_add_pair_%d" % (tag, i), g, r, cc) for i, (g, r) in enumerate(zip(views, theirs))]

    swap_sems = _chips_start("swap_late_start", grad_views(late_grads, _LATE, _LATE_PACKED), SIBLING_HALF)
    dqs, dks, dvs, dsink_rows = _swa_bwd(qs, ks, vs, ya, dya, lse_a, swa_bias, sink_rows + swap_sems[4][0:1, 0:1])
    dsink = dsink_rows[:, 0:SWA_GROUP, 0]
    late_views, late_theirs = _chips_wait("swap_late_wait", *swap_sems[:4], SIBLING_HALF, after=dqs)
    rs_sems = _chips_start("scatter_late_start", pair_sums("late", late_views, late_theirs), PIECE)
    dqm, dkm, dvm = _mla_bwd(qm, km, vm, dyb, lse_b, delta_b.reshape(HEADS, 1, t_rows) + rs_sems[4][0:1, 0:1])
    dz, dqb, dx, dgq, dgkv, dg1 = _bwd_in(dqs, dks, dvs, dqm, dkm, dvm, tabs, consts, cq, ckv, gq, gkv, wuq, wk, wv,
                                           dgates, win, x2d, g1, dx1, tm)

    small = {"attn_pre_norm": dg1, "attn_post_norm": dg2, "b_gate": dbg, "sinks": dsink, "q_a_norm": dgq,
             "kv_a_norm": dgkv, "mlp_pre_norm": dg3, "mlp_post_norm": dg4, "conv_b": dconvb, "ple_norm": dg5,
             "conv_w": dconvw8[0:3], "loss": loss_part}
    small_sems = _chips_start("gather_small_start", [_pack_small(small)], EVERYONE)
    small_token = small_sems[4]

    dwk = _unpad_slots(_mm_tn("dw_k", ckvn, dkm, after=small_token), HEADS, NOPE_DIM, 1).reshape(
        KV_LORA, HEADS, NOPE_DIM)
    dwv = _unpad_slots(_mm_tn("dw_v", ckvn, dvm, after=small_token), HEADS, V_DIM, 1).reshape(KV_LORA, HEADS, V_DIM)
    early_grads = {
        "w_in": _unpad_w_in(_mm_tn("dw_in", h1, dz, after=small_token)).reshape(D_MODEL, 4, 808).transpose(1, 0, 2),
        "w_uq": _unpad_slots(_mm_tn("dw_uq", cqn, dqb, after=small_token), HEADS, NOPE_DIM + ROPE_DIM, 1),
        "w_ukv": jnp.concatenate([dwk, dwv], axis=2).reshape(KV_LORA, HEADS * (NOPE_DIM + V_DIM)),
    }

    def finish(tag, pairs, landed, group, packed_group):
        reduced = []
        for i, (pair, land) in enumerate(zip(pairs, landed)):
            own = lax.dynamic_index_in_dim(pair, chip, 0, keepdims=True)
            reduced.append(_add_chips("rs_%s_add_chips_%d" % (tag, i),
                                      lax.dynamic_update_slice(land, own, (chip, 0, 0))))
        others = _swap_sibling("swap_%s_reduced_halves" % tag, reduced)
        r, o = reduced[-1], others[-1]
        packed = jnp.where(cc == 0, jnp.stack([r, o]), jnp.stack([o, r]))
        for n, g in _unpack_shard_grads(packed, packed_group).items():
            updates[n] = _adamw("adamw_" + n, w2[n], g, m2[n], v2[n])
        for n, r, o in zip([n for n in group if n in _NATURAL], reduced, others):
            updates[n] = _adamw_halves("adamw_" + n, w2[n], r, o, m2[n], v2[n], cc)

    updates = {}

    def adamw(n, g):
        updates[n] = _adamw("adamw_" + n, w2[n], g, m2[n], v2[n])

    early_views = grad_views(early_grads, _EARLY, _EARLY_PACKED)
    early_theirs = _swap_sibling("swap_early_grad_halves", early_views, other_half=True)
    small_sent, small_landed = _chips_wait("gather_small_wait", *small_sems[:4], EVERYONE, after=early_theirs[0])
    small_all = lax.dynamic_update_slice(small_landed[0], small_sent[0][None], (4 * xc + 2 * yc + cc, 0, 0))
    early_sems = _chips_start("scatter_early_start", pair_sums("early", early_views, early_theirs), PIECE,
                              after=small_all)
    late_pairs, late_landed = _chips_wait("scatter_late_wait", *rs_sems[:4], PIECE, after=early_sems[4])
    finish("late", late_pairs, late_landed, _LATE, _LATE_PACKED)
    early_pairs, early_landed = _chips_wait("scatter_early_wait", *early_sems[:4], PIECE,
                                            after=updates[_LATE[-1]][1])
    finish("early", early_pairs, early_landed, _EARLY, _EARLY_PACKED)

    small_sum = _unpack_small(_add_devices(small_all))
    for n in names:
        if n == "conv_w":
            adamw(n, lax.dynamic_index_in_dim(small_sum[n].reshape(3, 4, 1408), chip, 1, keepdims=False))
        elif n in small_sum:
            adamw(n, small_sum[n].reshape(w2[n].shape))
    loss = small_sum["loss"][0]

    outs = [[updates[n][i].reshape(wts[n].shape) for n in names] for i in range(4)]
    return (loss, dx.reshape(x.shape), *outs[0], *outs[1], *outs[2], *outs[3])
```

```python
import functools
import math

import numpy as np
import jax
import jax.numpy as jnp
from jax import lax
from jax.experimental import pallas as pl
from jax.experimental.pallas import tpu as pltpu

F32 = jnp.float32
BF16 = jnp.bfloat16

D_MODEL = 1024
D_FF = 2816
PLE_DIM = 256
ROPE_THETA = 10000.0
RMS_EPS = 1e-6
SWA_WINDOW = 128
HEADS = 8
A_KV_HEADS = 2
A_HEAD_DIM = 64
Q_LORA = 256
KV_LORA = 128
NOPE_DIM = 64
ROPE_DIM = 32
V_DIM = 64
LANES = 128
ZW = 4096
NEG = -1e30
SCALE_A = A_HEAD_DIM ** -0.5
SCALE_B = (NOPE_DIM + ROPE_DIM) ** -0.5

ADAM_LR = 0.001
ADAM_B1 = 0.9
ADAM_B2 = 0.999
ADAM_EPS = 1e-08
ADAM_WD = 0.01
ADAM_STEP = 10

VMEM_LIMIT = 60 * 1024 * 1024
MESH_AXES = ("x", "y", "c")
MESH = pl.DeviceIdType.MESH

Z_QA, Z_KA, Z_VA, Z_CQ, Z_CKV, Z_KR, Z_GATE = 0, 1024, 1280, 1536, 1792, 1920, 2048


def _dot(a, b):
    return jnp.dot(a, b, preferred_element_type=F32)


def _dot_nt(a, b):
    return lax.dot_general(a, b, (((1,), (1,)), ((), ())), preferred_element_type=F32)


def _dot_tn(a, b):
    return lax.dot_general(a, b, (((0,), (0,)), ((), ())), preferred_element_type=F32)


def _rms_stats(x):
    r = lax.rsqrt(jnp.mean(x * x, axis=-1, keepdims=True) + RMS_EPS)
    return x * r, r


def _rms_bwd(dy, xn, r, g):
    dxn = dy * g
    dx = r * (dxn - xn * jnp.mean(dxn * xn, axis=-1, keepdims=True))
    dg = jnp.sum(dy * xn, axis=0, keepdims=True)
    return dx, dg


def _tile_lanes(t, n):
    return t if n == 1 else jnp.concatenate([t] * n, axis=1)


def _rope(x, c, s1, s2, half):
    w = x.shape[1]
    n = w // LANES
    return (x * _tile_lanes(c, n) + pltpu.roll(x, w - half, 1) * _tile_lanes(s1, n)
            + pltpu.roll(x, half, 1) * _tile_lanes(s2, n))


def _rope_t(dy, c, s1, s2, half):
    w = dy.shape[1]
    n = w // LANES
    return (dy * _tile_lanes(c, n) + pltpu.roll(dy * _tile_lanes(s1, n), half, 1)
            + pltpu.roll(dy * _tile_lanes(s2, n), w - half, 1))


def _sigmoid(x):
    return 1.0 / (1.0 + jnp.exp(-x))


_GELU_C = math.sqrt(2.0 / math.pi)


def _gelu_and_grad(x):
    a = _GELU_C + (_GELU_C * 0.044715) * (x * x)
    th = jnp.tanh(x * a)
    hx = 0.5 * x
    p1 = 1.0 + th
    gel = hx * p1
    dgel = 0.5 * p1 + (hx * (1.0 - th * th)) * (3.0 * a - 2.0 * _GELU_C)
    return gel, dgel


def _conv_taps(up, h6, h7):
    r1 = pltpu.roll(up, 1, 0)
    r2 = pltpu.roll(up, 2, 0)
    rows = lax.broadcasted_iota(jnp.int32, (8, up.shape[1]), 0)
    xm1 = jnp.concatenate([jnp.where(rows == 0, h7, r1[0:8]), r1[8:]], axis=0)
    xm2 = jnp.concatenate([jnp.where(rows == 0, h6, jnp.where(rows == 1, h7, r2[0:8])), r2[8:]], axis=0)
    return xm1, xm2


def _conv_taps_next(du, n0, n1):
    tm = du.shape[0]
    r1 = pltpu.roll(du, tm - 1, 0)
    r2 = pltpu.roll(du, tm - 2, 0)
    rows = lax.broadcasted_iota(jnp.int32, (8, du.shape[1]), 0)
    xp1 = jnp.concatenate([r1[:tm - 8], jnp.where(rows == 7, n0, r1[tm - 8:])], axis=0)
    xp2 = jnp.concatenate([r2[:tm - 8], jnp.where(rows == 6, n0, jnp.where(rows == 7, n1, r2[tm - 8:]))], axis=0)
    return xp1, xp2


def _row(tm, n):
    return pl.BlockSpec((tm, n), lambda i: (i, 0))


def _full(shape):
    nd = len(shape)
    return pl.BlockSpec(tuple(shape), lambda i: (0,) * nd)


def _resident(shape):
    nd = len(shape)
    return pl.BlockSpec(tuple(shape), lambda i: (0,) * nd, pipeline_mode=pl.Buffered(1))


def _heads(tm, h):
    return pl.BlockSpec((h, tm, LANES), lambda i: (0, i, 0))


def _rows_call(name, body, t_rows, tm, ins, outs, scratch=()):
    return pl.pallas_call(
        body, name=name, grid=(t_rows // tm,),
        in_specs=[s for _, s in ins],
        out_specs=[s for _, s in outs],
        out_shape=[s for s, _ in outs],
        scratch_shapes=list(scratch),
        compiler_params=pltpu.CompilerParams(dimension_semantics=("arbitrary",), vmem_limit_bytes=VMEM_LIMIT),
    )(*[a for a, _ in ins])


def _sds(shape, dtype):
    return jax.ShapeDtypeStruct(tuple(shape), dtype)


def _rope_consts():
    c = np.zeros((16, LANES), np.float32)
    lane = np.arange(LANES)
    inv_a = (ROPE_THETA ** (-(np.arange(0, A_HEAD_DIM, 2, dtype=np.float32) / A_HEAD_DIM))).astype(np.float32)
    in_a = lane < A_HEAD_DIM
    c[0, in_a] = inv_a[lane[in_a] % (A_HEAD_DIM // 2)]
    c[1, in_a] = 1.0
    c[2, lane < A_HEAD_DIM // 2] = -1.0
    c[3, (lane >= A_HEAD_DIM // 2) & in_a] = 1.0
    inv_b = (ROPE_THETA ** (-(np.arange(0, ROPE_DIM, 2, dtype=np.float32) / ROPE_DIM))).astype(np.float32)
    pe = (lane >= NOPE_DIM) & (lane < NOPE_DIM + ROPE_DIM)
    c[5, pe] = inv_b[(lane[pe] - NOPE_DIM) % (ROPE_DIM // 2)]
    c[6, pe] = 1.0
    c[7, (lane >= NOPE_DIM) & (lane < NOPE_DIM + ROPE_DIM // 2)] = -1.0
    c[8, (lane >= NOPE_DIM + ROPE_DIM // 2) & (lane < NOPE_DIM + ROPE_DIM)] = 1.0
    c[9, lane < NOPE_DIM] = 1.0
    c[10, pe] = 1.0
    return jnp.asarray(c)


def _rope_tables(pos_f, consts, tm):
    t_rows = pos_f.shape[0]

    def body(pos_ref, c_ref, ca, sa1, sa2, cb, sb1, sb2):
        ang = pos_ref[...] * (c_ref[0:1, :] + c_ref[5:6, :])
        cs, sn = jnp.cos(ang), jnp.sin(ang)
        ca[...] = cs * c_ref[1:2, :]
        sa1[...] = sn * c_ref[2:3, :]
        sa2[...] = sn * c_ref[3:4, :]
        cb[...] = cs * c_ref[6:7, :] + c_ref[9:10, :]
        sb1[...] = sn * c_ref[7:8, :]
        sb2[...] = sn * c_ref[8:9, :]

    tab = (_sds((t_rows, LANES), F32), _row(tm, LANES))
    return _rows_call("rope_tables", body, t_rows, tm,
                      [(pos_f, _row(tm, 1)), (consts, _full(consts.shape))], [tab] * 6)


def _fwd_in(x, g1, win, bg, gq, gkv, wuq, wk, wv, tabs, tm):
    t_rows = x.shape[0]

    def body(x_ref, g1_ref, win_ref, bg_ref, gq_ref, gkv_ref, wuq_ref, wk_ref, wv_ref,
             ca, sa1, sa2, cb, sb1, sb2,
             h1_ref, qs_ref, ks_ref, vs_ref, cq_ref, cqn_ref, ckv_ref, ckvn_ref, qm_ref, km_ref, vm_ref, gate_ref):
        xn, _ = _rms_stats(x_ref[...])
        hb = (xn * g1_ref[...]).astype(BF16)
        h1_ref[...] = hb
        ta = (ca[...], sa1[...], sa2[...])
        tb = (cb[...], sb1[...], sb2[...])
        qs_ref[...] = (_rope(_dot(hb, win_ref[:, Z_QA:Z_KA]), *ta, A_HEAD_DIM // 2) * SCALE_A).astype(BF16)
        ks_ref[...] = _rope(_dot(hb, win_ref[:, Z_KA:Z_VA]), *ta, A_HEAD_DIM // 2).astype(BF16)
        vs_ref[...] = _dot(hb, win_ref[:, Z_VA:Z_CQ]).astype(BF16)
        cq = _dot(hb, win_ref[:, Z_CQ:Z_CKV])
        cq_ref[...] = cq
        cqn, _ = _rms_stats(cq)
        cqb = (cqn * gq_ref[...]).astype(BF16)
        cqn_ref[...] = cqb
        qm_ref[...] = (_rope(_dot(cqb, wuq_ref[...]), *tb, ROPE_DIM // 2) * SCALE_B).astype(BF16)
        ckv = _dot(hb, win_ref[:, Z_CKV:Z_KR])
        ckv_ref[...] = ckv
        ckvn, _ = _rms_stats(ckv)
        ckvb = (ckvn * gkv_ref[...]).astype(BF16)
        ckvn_ref[...] = ckvb
        kpe = _rope(_dot(hb, win_ref[:, Z_KR:Z_GATE]), *tb, ROPE_DIM // 2)
        km_ref[...] = (_dot(ckvb, wk_ref[...]) + _tile_lanes(kpe, HEADS)).astype(BF16)
        vm_ref[...] = _dot(ckvb, wv_ref[...]).astype(BF16)
        gate_ref[...] = _sigmoid(_dot(hb, win_ref[:, Z_GATE:ZW]) + bg_ref[...])

    def o(n, dt):
        return (_sds((t_rows, n), dt), _row(tm, n))

    ins = [(x, _row(tm, D_MODEL)), (g1, _full(g1.shape)), (win, _resident(win.shape)), (bg, _full(bg.shape)),
           (gq, _full(gq.shape)), (gkv, _full(gkv.shape)), (wuq, _full(wuq.shape)), (wk, _full(wk.shape)),
           (wv, _full(wv.shape))] + [(t, _row(tm, LANES)) for t in tabs]
    outs = [o(1024, BF16), o(1024, BF16), o(256, BF16), o(256, BF16), o(256, F32), o(256, BF16), o(128, F32),
            o(128, BF16), o(1024, BF16), o(1024, BF16), o(1024, BF16), o(2048, F32)]
    return _rows_call("fwd_in", body, t_rows, tm, ins, outs)


def _attn_tile(t_rows):
    return min(512, t_rows)


MLA_HEADS_PER_STEP = 4


def _causal_pairs(nq, by_kv):
    if by_kv:
        pairs = [(i, j) for j in range(nq) for i in range(j, nq)]
    else:
        pairs = [(i, j) for i in range(nq) for j in range(i + 1)]
    return (jnp.asarray([p[0] for p in pairs], jnp.int32), jnp.asarray([p[1] for p in pairs], jnp.int32))


def _mla_fwd(q, k, v):
    t_rows = q.shape[0]
    t = _attn_tile(t_rows)
    hp = MLA_HEADS_PER_STEP
    w = hp * LANES
    ii, jj = _causal_pairs(t_rows // t, by_kv=False)

    def body(i_ref, j_ref, q_ref, k_ref, v_ref, o_ref, lse_ref, m_s, l_s, acc_s):
        i = i_ref[pl.program_id(1)]
        j = j_ref[pl.program_id(1)]

        @pl.when(j == 0)
        def _():
            m_s[...] = jnp.full(m_s.shape, NEG, F32)
            l_s[...] = jnp.zeros(l_s.shape, F32)
            acc_s[...] = jnp.zeros(acc_s.shape, F32)

        def step(diagonal):
            sls = [slice(hh * LANES, (hh + 1) * LANES) for hh in range(hp)]
            scores = [_dot_nt(k_ref[:, sl], q_ref[:, sl]) for sl in sls]
            if diagonal:
                valid = (lax.broadcasted_iota(jnp.int32, (t, t), 0) <= lax.broadcasted_iota(jnp.int32, (t, t), 1))
                scores = [jnp.where(valid, s, NEG) for s in scores]
            stats = []
            for hh, s in enumerate(scores):
                m_prev = m_s[hh]
                m_new = jnp.maximum(m_prev, jnp.max(s, axis=0, keepdims=True))
                p = jnp.exp(s - m_new)
                alpha = jnp.exp(m_prev - m_new)
                stats.append((m_new, alpha, alpha * l_s[hh] + jnp.sum(p, axis=0, keepdims=True), p.astype(BF16)))
            for hh, (m_new, alpha, l_new, p) in enumerate(stats):
                sl = sls[hh]
                acc = alpha * acc_s[hh] + _dot_tn(v_ref[:, sl], p)
                if diagonal:
                    o_ref[:, sl] = (acc / l_new).T.astype(o_ref.dtype)
                    lse_ref[hh] = m_new + jnp.log(l_new)
                else:
                    m_s[hh] = m_new
                    l_s[hh] = l_new
                    acc_s[hh] = acc

        pl.when(j < i)(lambda: step(False))
        pl.when(j == i)(lambda: step(True))

    grid_spec = pltpu.PrefetchScalarGridSpec(
        num_scalar_prefetch=2, grid=(HEADS // hp, ii.shape[0]),
        in_specs=[pl.BlockSpec((t, w), lambda hb, s, ir, jr: (ir[s], hb)),
                  pl.BlockSpec((t, w), lambda hb, s, ir, jr: (jr[s], hb)),
                  pl.BlockSpec((t, w), lambda hb, s, ir, jr: (jr[s], hb))],
        out_specs=[pl.BlockSpec((t, w), lambda hb, s, ir, jr: (ir[s], hb)),
                   pl.BlockSpec((hp, 1, t), lambda hb, s, ir, jr: (hb, 0, ir[s]))],
        scratch_shapes=[pltpu.VMEM((hp, 1, t), F32), pltpu.VMEM((hp, 1, t), F32), pltpu.VMEM((hp, LANES, t), F32)])
    return pl.pallas_call(
        body, name="mla_fwd", grid_spec=grid_spec,
        out_shape=[_sds((t_rows, HEADS * LANES), BF16), _sds((HEADS, 1, t_rows), F32)],
        compiler_params=pltpu.CompilerParams(dimension_semantics=("arbitrary",) * 2, vmem_limit_bytes=VMEM_LIMIT),
    )(ii, jj, q, k, v)


def _mla_bwd(q, k, v, do, lse, delta):
    t_rows = q.shape[0]
    t = _attn_tile(t_rows)
    hp = MLA_HEADS_PER_STEP
    w = hp * LANES
    ii, jj = _causal_pairs(t_rows // t, by_kv=True)

    def body(i_ref, j_ref, q_ref, k_ref, v_ref, do_ref, lse_ref, dl_ref, dq_ref, dk_ref, dv_ref):
        i = i_ref[pl.program_id(1)]
        j = j_ref[pl.program_id(1)]

        @pl.when(pl.program_id(1) == 0)
        def _():
            dq_ref[...] = jnp.zeros(dq_ref.shape, F32)

        def step(diagonal):
            r0 = pl.multiple_of(i * t, t)
            sls = [slice(hh * LANES, (hh + 1) * LANES) for hh in range(hp)]
            scores = [_dot_nt(k_ref[:, sl], q_ref[:, sl]) for sl in sls]
            if diagonal:
                valid = (lax.broadcasted_iota(jnp.int32, (t, t), 0) <= lax.broadcasted_iota(jnp.int32, (t, t), 1))
                scores = [jnp.where(valid, s, NEG) for s in scores]
            dps = [_dot_nt(v_ref[:, sl], do_ref[:, sl]) for sl in sls]
            ps = [jnp.exp(s - lse_ref[hh]) for hh, s in enumerate(scores)]
            dss = [(p * (dp - dl_ref[hh])).astype(BF16) for hh, (p, dp) in enumerate(zip(ps, dps))]
            for hh, sl in enumerate(sls):
                dv = _dot(ps[hh].astype(BF16), do_ref[:, sl])
                dk = _dot(dss[hh], q_ref[:, sl])
                if diagonal:
                    dv_ref[:, sl] = dv
                    dk_ref[:, sl] = dk
                else:
                    dv_ref[:, sl] += dv
                    dk_ref[:, sl] += dk
                dq_ref[hh, pl.ds(r0, t), :] += _dot_tn(dss[hh], k_ref[:, sl])

        pl.when(i > j)(lambda: step(False))
        pl.when(i == j)(lambda: step(True))

    def qmap(hb, s, ir, jr):
        return (ir[s], hb)

    def kvmap(hb, s, ir, jr):
        return (jr[s], hb)

    def rowmap(hb, s, ir, jr):
        return (hb, 0, ir[s])

    grid_spec = pltpu.PrefetchScalarGridSpec(
        num_scalar_prefetch=2, grid=(HEADS // hp, ii.shape[0]),
        in_specs=[pl.BlockSpec((t, w), qmap), pl.BlockSpec((t, w), kvmap), pl.BlockSpec((t, w), kvmap),
                  pl.BlockSpec((t, w), qmap), pl.BlockSpec((hp, 1, t), rowmap), pl.BlockSpec((hp, 1, t), rowmap)],
        out_specs=[pl.BlockSpec((hp, t_rows, LANES), lambda hb, s, ir, jr: (hb, 0, 0)),
                   pl.BlockSpec((t, w), kvmap), pl.BlockSpec((t, w), kvmap)])
    return pl.pallas_call(
        body, name="mla_bwd", grid_spec=grid_spec,
        out_shape=[_sds((HEADS, t_rows, LANES), F32), _sds((t_rows, HEADS * LANES), F32),
                   _sds((t_rows, HEADS * LANES), F32)],
        compiler_params=pltpu.CompilerParams(dimension_semantics=("arbitrary",) * 2, vmem_limit_bytes=VMEM_LIMIT),
    )(ii, jj, q, k, v, do, lse, delta)


SWA_TILE = 2 * SWA_WINDOW
SWA_GROUP = HEADS // A_KV_HEADS


def _swa_bias(tq):
    koff = lax.broadcasted_iota(jnp.int32, (tq + SWA_WINDOW, SWA_GROUP * tq), 0) - SWA_WINDOW
    qoff = (lax.broadcasted_iota(jnp.int32, (tq + SWA_WINDOW, SWA_GROUP * tq), 1) % tq)
    band = (koff <= qoff) & (qoff - koff < SWA_WINDOW)
    return jnp.stack([jnp.where(band & (koff >= 0), 0.0, NEG), jnp.where(band, 0.0, NEG)]).astype(F32)


def _swa_specs(tq, nq):
    wb = tq // SWA_WINDOW

    def qi(i):
        return jnp.minimum(i, nq - 1)

    q = pl.BlockSpec((tq, SWA_GROUP * LANES), lambda h, i: (qi(i), h))
    cur = pl.BlockSpec((tq, LANES), lambda h, i: (qi(i), h))
    prev = pl.BlockSpec((SWA_WINDOW, LANES), lambda h, i: (jnp.maximum(qi(i) * wb - 1, 0), h))
    bias = pl.BlockSpec((1, tq + SWA_WINDOW, SWA_GROUP * tq), lambda h, i: (jnp.minimum(i, 1), 0, 0))
    rows = pl.BlockSpec((1, 1, 1, SWA_GROUP * tq), lambda h, i: (h, qi(i), 0, 0))
    sink = pl.BlockSpec((1, 1, SWA_GROUP * tq), lambda h, i: (h, 0, 0))
    return q, cur, prev, bias, rows, sink


def _stack_heads(ref):
    return jnp.concatenate([ref[:, g * LANES:(g + 1) * LANES] for g in range(SWA_GROUP)], axis=0)


def _swa_fwd(q, k, v, bias, sink_rows):
    t_rows = q.shape[0]
    tq = min(SWA_TILE, t_rows)
    nq = t_rows // tq
    qs_, cur, prev, bs, rows, sk = _swa_specs(tq, nq)

    def body(q_ref, kc_ref, kp_ref, vc_ref, vp_ref, b_ref, sink_ref, o_ref, lse_ref):
        qs = _stack_heads(q_ref)
        kk = jnp.concatenate([kp_ref[...], kc_ref[...]], axis=0)
        vv = jnp.concatenate([vp_ref[...], vc_ref[...]], axis=0)
        s = _dot_nt(kk, qs) + b_ref[0]
        sink = sink_ref[0]
        m = jnp.maximum(jnp.max(s, axis=0, keepdims=True), sink)
        p = jnp.exp(s - m)
        l = jnp.sum(p, axis=0, keepdims=True) + jnp.exp(sink - m)
        o = (_dot_tn(vv, p.astype(BF16)) / l).T
        for g in range(SWA_GROUP):
            o_ref[:, g * LANES:(g + 1) * LANES] = o[g * tq:(g + 1) * tq].astype(o_ref.dtype)
        lse_ref[0, 0] = m + jnp.log(l)

    return pl.pallas_call(
        body, name="swa_fwd", grid=(A_KV_HEADS, nq),
        in_specs=[qs_, cur, prev, cur, prev, bs, sk],
        out_specs=[qs_, rows],
        out_shape=[_sds((t_rows, HEADS * LANES), BF16), _sds((A_KV_HEADS, nq, 1, SWA_GROUP * tq), F32)],
        compiler_params=pltpu.CompilerParams(dimension_semantics=("arbitrary",) * 2, vmem_limit_bytes=VMEM_LIMIT),
    )(q, k, k, v, v, bias, sink_rows)


def _swa_bwd(q, k, v, o, do, lse, bias, sink_rows):
    t_rows = q.shape[0]
    tq = min(SWA_TILE, t_rows)
    nq = t_rows // tq
    qs_, cur, prev, bs, rows, sk = _swa_specs(tq, nq)
    hw = SWA_WINDOW

    def body(q_ref, kc_ref, kp_ref, vc_ref, vp_ref, o_ref, do_ref, lse_ref, b_ref, sink_ref,
             dq_ref, dk_ref, dv_ref, dsink_ref, ck, cv, dsa):
        i = pl.program_id(1)

        @pl.when(i == 0)
        def _():
            dsa[...] = jnp.zeros(dsa.shape, F32)

        @pl.when(i < nq)
        def _():
            qs = _stack_heads(q_ref)
            dos = _stack_heads(do_ref)
            kk = jnp.concatenate([kp_ref[...], kc_ref[...]], axis=0)
            vv = jnp.concatenate([vp_ref[...], vc_ref[...]], axis=0)
            lse = lse_ref[0, 0]
            p = jnp.exp(_dot_nt(kk, qs) + b_ref[0] - lse)
            delta = jnp.sum((_stack_heads(o_ref).astype(F32) * dos.astype(F32)).T, axis=0, keepdims=True)
            dsa[...] += -jnp.exp(sink_ref[0] - lse) * delta
            dv = _dot(p.astype(BF16), dos)
            ds = (p * (_dot_nt(vv, dos) - delta)).astype(BF16)
            dk = _dot(ds, qs)
            dq = _dot_tn(ds, kk)
            for g in range(SWA_GROUP):
                dq_ref[:, g * LANES:(g + 1) * LANES] = dq[g * tq:(g + 1) * tq]

            @pl.when(i > 0)
            def _():
                dk_ref[0:tq - hw, :] = ck[0:tq - hw, :]
                dk_ref[tq - hw:tq, :] = ck[tq - hw:tq, :] + dk[0:hw]
                dv_ref[0:tq - hw, :] = cv[0:tq - hw, :]
                dv_ref[tq - hw:tq, :] = cv[tq - hw:tq, :] + dv[0:hw]

            ck[...] = dk[hw:hw + tq]
            cv[...] = dv[hw:hw + tq]

        @pl.when(i == nq)
        def _():
            dk_ref[...] = ck[...]
            dv_ref[...] = cv[...]
            dsink_ref[...] = jnp.zeros(dsink_ref.shape, F32)
            for g in range(SWA_GROUP):
                tot = jnp.sum(dsa[:, g * tq:(g + 1) * tq], axis=1, keepdims=True)
                dsink_ref[0, g:g + 1, :] = jnp.zeros((1, LANES), F32) + tot

    kv_out = pl.BlockSpec((tq, LANES), lambda h, i: (jnp.maximum(i - 1, 0), h))
    return pl.pallas_call(
        body, name="swa_bwd", grid=(A_KV_HEADS, nq + 1),
        in_specs=[qs_, cur, prev, cur, prev, qs_, qs_, rows, bs, sk],
        out_specs=[qs_, kv_out, kv_out, pl.BlockSpec((1, 8, LANES), lambda h, i: (h, 0, 0))],
        out_shape=[_sds((t_rows, HEADS * LANES), F32), _sds((t_rows, A_KV_HEADS * LANES), F32),
                   _sds((t_rows, A_KV_HEADS * LANES), F32), _sds((A_KV_HEADS, 8, LANES), F32)],
        scratch_shapes=[pltpu.VMEM((tq, LANES), F32), pltpu.VMEM((tq, LANES), F32),
                        pltpu.VMEM((1, SWA_GROUP * tq), F32)],
        compiler_params=pltpu.CompilerParams(dimension_semantics=("arbitrary",) * 2, vmem_limit_bytes=VMEM_LIMIT),
    )(q, k, k, v, v, o, do, lse, bias, sink_rows)


def _fwd_mix(x, ya, yb, gate, wba, wbb, wout, g2, g3, tm):
    t_rows = x.shape[0]

    def body(x_ref, ya_ref, yb_ref, gate_ref, wba_ref, wbb_ref, wout_ref, g2_ref, g3_ref,
             pa_ref, pb_ref, mixed_ref, o_ref, x1_ref, h2_ref):
        pa = _dot(ya_ref[...], wba_ref[...])
        pb = _dot(yb_ref[...], wbb_ref[...])
        pa_ref[...] = pa
        pb_ref[...] = pb
        mixed = (gate_ref[:, 0:D_MODEL] * pa + gate_ref[:, D_MODEL:2 * D_MODEL] * pb).astype(BF16)
        mixed_ref[...] = mixed
        o = _dot(mixed, wout_ref[...])
        o_ref[...] = o
        on, _ = _rms_stats(o)
        x1 = x_ref[...] + on * g2_ref[...]
        x1_ref[...] = x1
        x1n, _ = _rms_stats(x1)
        h2_ref[...] = (x1n * g3_ref[...]).astype(BF16)

    def o_(dt):
        return (_sds((t_rows, D_MODEL), dt), _row(tm, D_MODEL))

    ins = [(x, _row(tm, D_MODEL)), (ya, _row(tm, 1024)), (yb, _row(tm, 1024)), (gate, _row(tm, 2048)),
           (wba, _resident(wba.shape)), (wbb, _resident(wbb.shape)), (wout, _resident(wout.shape)),
           (g2, _full(g2.shape)), (g3, _full(g3.shape))]
    return _rows_call("fwd_mix", body, t_rows, tm, ins, [o_(F32), o_(F32), o_(BF16), o_(F32), o_(F32), o_(BF16)])


CONV_CHUNK = 1408


def _fwd_up(h2, wup, convw8, convb, tm):
    t_rows = h2.shape[0]
    cdim = 2 * D_FF

    def body(h2_ref, wup_ref, cw_ref, cb_ref, up_ref, a_ref, carry):
        i = pl.program_id(0)

        @pl.when(i == 0)
        def _():
            carry[...] = jnp.zeros(carry.shape, F32)

        hb = h2_ref[...]

        def conv(c0):
            sl = slice(c0, c0 + CONV_CHUNK)
            up = _dot(hb, wup_ref[c0 // CONV_CHUNK])
            up_ref[:, sl] = up
            xm1, xm2 = _conv_taps(up, carry[6:7, sl], carry[7:8, sl])
            u = cw_ref[0:1, sl] * xm2 + cw_ref[1:2, sl] * xm1 + cw_ref[2:3, sl] * up + cb_ref[:, sl]
            carry[:, sl] = up[tm - 8:tm, :]
            return u

        for c0 in range(0, D_FF, CONV_CHUNK):
            ug = conv(c0)
            uv = conv(D_FF + c0)
            gel, _ = _gelu_and_grad(ug)
            a_ref[:, c0:c0 + CONV_CHUNK] = (gel * uv).astype(BF16)

    ins = [(h2, _row(tm, D_MODEL)), (wup, _resident(wup.shape)), (convw8, _full(convw8.shape)), (convb, _full(convb.shape))]
    outs = [(_sds((t_rows, cdim), F32), _row(tm, cdim)), (_sds((t_rows, D_FF), BF16), _row(tm, D_FF))]
    return _rows_call("fwd_up", body, t_rows, tm, ins, outs, scratch=[pltpu.VMEM((8, cdim), F32)])


def _fwd_out(a, wdown, x1, g4, p, wple, g5, wpg, tgt, tm):
    t_rows = a.shape[0]

    def body(a_ref, wdown_ref, x1_ref, g4_ref, p_ref, wple_ref, g5_ref, wpg_ref, tgt_ref,
             ff_ref, x2_ref, e_ref, n5_ref, sg_ref, dx3_ref, loss_ref):
        i = pl.program_id(0)
        ff = _dot(a_ref[...], wdown_ref[...])
        ff_ref[...] = ff
        ffn, _ = _rms_stats(ff)
        x2 = x1_ref[...] + ffn * g4_ref[...]
        x2_ref[...] = x2
        e = _dot(p_ref[...].astype(BF16), wple_ref[...])
        e_ref[...] = e
        x2n, _ = _rms_stats(x2)
        n5 = (x2n * g5_ref[...]).astype(BF16)
        n5_ref[...] = n5
        sg = _sigmoid(_dot(n5, wpg_ref[...]))
        sg_ref[...] = sg
        d = x2 + sg * e - tgt_ref[...]
        dx3_ref[...] = d * (1.0 / D_MODEL)

        @pl.when(i == 0)
        def _():
            loss_ref[...] = jnp.zeros((1, 1), F32)

        loss_ref[...] += 0.5 * jnp.sum(jnp.sum(d * d, axis=1, keepdims=True), axis=0, keepdims=True) * (1.0 / D_MODEL)

    def o_(dt):
        return (_sds((t_rows, D_MODEL), dt), _row(tm, D_MODEL))

    ins = [(a, _row(tm, D_FF)), (wdown, _resident(wdown.shape)), (x1, _row(tm, D_MODEL)), (g4, _full(g4.shape)),
           (p, _row(tm, PLE_DIM)), (wple, _full(wple.shape)), (g5, _full(g5.shape)), (wpg, _resident(wpg.shape)),
           (tgt, _row(tm, D_MODEL))]
    outs = [o_(F32), o_(F32), o_(F32), o_(BF16), o_(F32), o_(F32), (_sds((1, 1), F32), _full((1, 1)))]
    return _rows_call("fwd_out", body, t_rows, tm, ins, outs)


def _bwd_out(dx3, e, sg, x2, ff, g5, g4, wpg, wdown, up, convw8, convb, tm):
    t_rows = dx3.shape[0]
    cdim = 2 * D_FF
    hb = tm // 8

    def body(dx3_ref, e_ref, sg_ref, x2_ref, ff_ref, g5_ref, g4_ref, wpg_ref, wdown_ref, up_ref, halo_ref, cw_ref,
             cb_ref, dpre_ref, de_ref, dx2_ref, dff_ref, du_ref, dg5_ref, dg4_ref, dcb_ref, dcw_ref):
        i = pl.program_id(0)

        @pl.when(i == 0)
        def _():
            dg5_ref[...] = jnp.zeros(dg5_ref.shape, F32)
            dg4_ref[...] = jnp.zeros(dg4_ref.shape, F32)
            dcb_ref[...] = jnp.zeros(dcb_ref.shape, F32)
            dcw_ref[...] = jnp.zeros(dcw_ref.shape, F32)

        dx3 = dx3_ref[...]
        sg = sg_ref[...]
        dpre = (dx3 * e_ref[...] * sg * (1.0 - sg)).astype(BF16)
        dpre_ref[...] = dpre
        de_ref[...] = (dx3 * sg).astype(BF16)
        dn5 = _dot_nt(dpre, wpg_ref[...])
        x2n, r5 = _rms_stats(x2_ref[...])
        d2, dg5 = _rms_bwd(dn5, x2n, r5, g5_ref[...])
        dx2 = dx3 + d2
        dx2_ref[...] = dx2
        dg5_ref[...] += dg5
        ffn, r4 = _rms_stats(ff_ref[...])
        dff, dg4 = _rms_bwd(dx2, ffn, r4, g4_ref[...])
        dg4_ref[...] += dg4
        dffb = dff.astype(BF16)
        dff_ref[...] = dffb
        keep = jnp.where(i > 0, 1.0, 0.0)

        def conv(c0):
            sl = slice(c0, c0 + CONV_CHUNK)
            up = up_ref[:, sl]
            xm1, xm2 = _conv_taps(up, halo_ref[6:7, sl] * keep, halo_ref[7:8, sl] * keep)
            u = cw_ref[0:1, sl] * xm2 + cw_ref[1:2, sl] * xm1 + cw_ref[2:3, sl] * up + cb_ref[:, sl]
            return u, up, xm1, xm2

        def grads(c0, du, up, xm1, xm2):
            sl = slice(c0, c0 + CONV_CHUNK)
            du_ref[:, sl] = du.astype(BF16)
            dcb_ref[:, sl] += jnp.sum(du, axis=0, keepdims=True)
            dcw_ref[0:1, sl] += jnp.sum(du * xm2, axis=0, keepdims=True)
            dcw_ref[1:2, sl] += jnp.sum(du * xm1, axis=0, keepdims=True)
            dcw_ref[2:3, sl] += jnp.sum(du * up, axis=0, keepdims=True)

        for c0 in range(0, D_FF, CONV_CHUNK):
            da = _dot_nt(dffb, wdown_ref[c0:c0 + CONV_CHUNK, :])
            ug, *rg = conv(c0)
            uv, *rv = conv(D_FF + c0)
            gel, dgel = _gelu_and_grad(ug)
            grads(c0, da * uv * dgel, *rg)
            grads(D_FF + c0, da * gel, *rv)

    def o_(n, dt):
        return (_sds((t_rows, n), dt), _row(tm, n))

    def acc(r, n):
        return (_sds((r, n), F32), _full((r, n)))

    halo = pl.BlockSpec((8, cdim), lambda i: (jnp.maximum(i * hb - 1, 0), 0))
    ins = [(dx3, _row(tm, D_MODEL)), (e, _row(tm, D_MODEL)), (sg, _row(tm, D_MODEL)), (x2, _row(tm, D_MODEL)),
           (ff, _row(tm, D_MODEL)), (g5, _full(g5.shape)), (g4, _full(g4.shape)), (wpg, _resident(wpg.shape)),
           (wdown, _resident(wdown.shape)), (up, _row(tm, cdim)), (up, halo), (convw8, _full(convw8.shape)),
           (convb, _full(convb.shape))]
    outs = [o_(D_MODEL, BF16), o_(D_MODEL, BF16), o_(D_MODEL, F32), o_(D_MODEL, BF16), o_(cdim, BF16),
            acc(1, D_MODEL), acc(1, D_MODEL), acc(1, cdim), acc(8, cdim)]
    return _rows_call("bwd_out", body, t_rows, tm, ins, outs)


def _bwd_mid(du, convw8, wup, dx2, x1, g3, o, g2, wout, gate, pa, pb, wba, wbb, yb, tm):
    t_rows = du.shape[0]
    cdim = 2 * D_FF
    halo_rows = 16
    hb = tm // halo_rows
    last_blk = t_rows // halo_rows - 1
    n_tiles = t_rows // tm

    def body(du_ref, halo_ref, cw_ref, wup_ref, dx2_ref, x1_ref, g3_ref, o_ref, g2_ref, wout_ref, gate_ref, pa_ref,
             pb_ref, wba_ref, wbb_ref, yb_ref,
             dup_ref, dx1_ref, do_ref, dpa_ref, dpb_ref, dgt_ref, dya_ref, dyb_ref, dl_ref, dg3_ref, dg2_ref, dbg_ref):
        i = pl.program_id(0)

        @pl.when(i == 0)
        def _():
            dg3_ref[...] = jnp.zeros(dg3_ref.shape, F32)
            dg2_ref[...] = jnp.zeros(dg2_ref.shape, F32)
            dbg_ref[...] = jnp.zeros(dbg_ref.shape, F32)

        keep = jnp.where(i < n_tiles - 1, 1.0, 0.0)
        dh2 = jnp.zeros((tm, D_MODEL), F32)
        for c0 in range(0, cdim, CONV_CHUNK):
            sl = slice(c0, c0 + CONV_CHUNK)
            du = du_ref[:, sl].astype(F32)
            nxt = halo_ref[:, sl].astype(F32)
            xp1, xp2 = _conv_taps_next(du, nxt[0:1] * keep, nxt[1:2] * keep)
            dup = (cw_ref[2:3, sl] * du + cw_ref[1:2, sl] * xp1 + cw_ref[0:1, sl] * xp2).astype(BF16)
            dup_ref[:, sl] = dup
            dh2 = dh2 + _dot_nt(dup, wup_ref[c0 // CONV_CHUNK])
        x1n, r3 = _rms_stats(x1_ref[...])
        d1, dg3 = _rms_bwd(dh2, x1n, r3, g3_ref[...])
        dx1 = dx2_ref[...] + d1
        dx1_ref[...] = dx1
        dg3_ref[...] += dg3
        on, r2 = _rms_stats(o_ref[...])
        do, dg2 = _rms_bwd(dx1, on, r2, g2_ref[...])
        dg2_ref[...] += dg2
        dob = do.astype(BF16)
        do_ref[...] = dob
        dmixed = _dot_nt(dob, wout_ref[...])
        ga = gate_ref[:, 0:D_MODEL]
        gb = gate_ref[:, D_MODEL:2 * D_MODEL]
        dpa = (dmixed * ga).astype(BF16)
        dpb = (dmixed * gb).astype(BF16)
        dpa_ref[...] = dpa
        dpb_ref[...] = dpb
        dga = dmixed * pa_ref[...] * ga * (1.0 - ga)
        dgb = dmixed * pb_ref[...] * gb * (1.0 - gb)
        dgt_ref[:, 0:D_MODEL] = dga.astype(BF16)
        dgt_ref[:, D_MODEL:2 * D_MODEL] = dgb.astype(BF16)
        dbg_ref[:, 0:D_MODEL] += jnp.sum(dga, axis=0, keepdims=True)
        dbg_ref[:, D_MODEL:2 * D_MODEL] += jnp.sum(dgb, axis=0, keepdims=True)
        dya_ref[...] = _dot_nt(dpa, wba_ref[...]).astype(BF16)
        dyb = _dot_nt(dpb, wbb_ref[...]).astype(BF16)
        dyb_ref[...] = dyb
        prod = yb_ref[...].astype(F32) * dyb.astype(F32)
        lane_head = lax.broadcasted_iota(jnp.int32, (HEADS, HEADS * LANES), 1) // LANES
        sel = (lane_head == lax.broadcasted_iota(jnp.int32, (HEADS, HEADS * LANES), 0)).astype(BF16)
        hi = prod.astype(BF16)
        lo = (prod - hi.astype(F32)).astype(BF16)
        dl_ref[...] = _dot_nt(sel, hi) + _dot_nt(sel, lo)

    def o_(n, dt):
        return (_sds((t_rows, n), dt), _row(tm, n))

    def acc(r, n):
        return (_sds((r, n), F32), _full((r, n)))

    halo = pl.BlockSpec((halo_rows, cdim), lambda i: (jnp.minimum((i + 1) * hb, last_blk), 0))
    ins = [(du, _row(tm, cdim)), (du, halo), (convw8, _full(convw8.shape)), (wup, _resident(wup.shape)),
           (dx2, _row(tm, D_MODEL)), (x1, _row(tm, D_MODEL)), (g3, _full(g3.shape)), (o, _row(tm, D_MODEL)),
           (g2, _full(g2.shape)), (wout, _resident(wout.shape)), (gate, _row(tm, 2048)), (pa, _row(tm, D_MODEL)),
           (pb, _row(tm, D_MODEL)), (wba, _resident(wba.shape)), (wbb, _resident(wbb.shape)), (yb, _row(tm, 1024))]
    outs = [o_(cdim, BF16), o_(D_MODEL, F32), o_(D_MODEL, BF16), o_(D_MODEL, BF16), o_(D_MODEL, BF16),
            o_(2048, BF16), o_(1024, BF16), o_(1024, BF16),
            (_sds((HEADS, t_rows), F32), pl.BlockSpec((HEADS, tm), lambda i: (0, i))),
            acc(1, D_MODEL), acc(1, D_MODEL), acc(1, 2048)]
    return _rows_call("bwd_mid", body, t_rows, tm, ins, outs)


def _bwd_in(dqs, dks, dvs, dqm, dkm, dvm, tabs, consts, cq, ckv, gq, gkv, wuq, wk, wv, dgates, win, x, g1, dx1, tm):
    t_rows = x.shape[0]

    def body(dqs_ref, dks_ref, dvs_ref, dqm_ref, dkm_ref, dvm_ref, ca, sa1, sa2, cb, sb1, sb2, c_ref, cq_ref,
             ckv_ref, gq_ref, gkv_ref, wuq_ref, wk_ref, wv_ref, dgt_ref, win_ref, x_ref, g1_ref, dx1_ref,
             dz_ref, dqb_ref, dx_ref, dgq_ref, dgkv_ref, dg1_ref):
        i = pl.program_id(0)

        @pl.when(i == 0)
        def _():
            dgq_ref[...] = jnp.zeros(dgq_ref.shape, F32)
            dgkv_ref[...] = jnp.zeros(dgkv_ref.shape, F32)
            dg1_ref[...] = jnp.zeros(dg1_ref.shape, F32)

        ta = (ca[...], sa1[...], sa2[...])
        tb = (cb[...], sb1[...], sb2[...])
        dz_ref[:, Z_QA:Z_KA] = _rope_t(dqs_ref[...] * SCALE_A, *ta, A_HEAD_DIM // 2).astype(BF16)
        dz_ref[:, Z_KA:Z_VA] = _rope_t(dks_ref[...], *ta, A_HEAD_DIM // 2).astype(BF16)
        dz_ref[:, Z_VA:Z_CQ] = dvs_ref[...].astype(BF16)
        dqm = jnp.concatenate([dqm_ref[h] for h in range(HEADS)], axis=1)
        dqb = _rope_t(dqm * SCALE_B, *tb, ROPE_DIM // 2).astype(BF16)
        dqb_ref[...] = dqb
        dcqn = _dot_nt(dqb, wuq_ref[...])
        cqn, rq = _rms_stats(cq_ref[...])
        dcq, dgq = _rms_bwd(dcqn, cqn, rq, gq_ref[...])
        dgq_ref[...] += dgq
        dz_ref[:, Z_CQ:Z_CKV] = dcq.astype(BF16)
        dkm = dkm_ref[...]
        dslot = dkm[:, 0:LANES]
        for h in range(1, HEADS):
            dslot = dslot + dkm[:, h * LANES:(h + 1) * LANES]
        dz_ref[:, Z_KR:Z_GATE] = _rope_t(dslot * c_ref[10:11, :], *tb, ROPE_DIM // 2).astype(BF16)
        dckvn = _dot_nt(dkm.astype(BF16), wk_ref[...]) + _dot_nt(dvm_ref[...].astype(BF16), wv_ref[...])
        ckvn, rkv = _rms_stats(ckv_ref[...])
        dckv, dgkv = _rms_bwd(dckvn, ckvn, rkv, gkv_ref[...])
        dgkv_ref[...] += dgkv
        dz_ref[:, Z_CKV:Z_KR] = dckv.astype(BF16)
        dz_ref[:, Z_GATE:ZW] = dgt_ref[...]
        dh1 = _dot_nt(dz_ref[...], win_ref[...])
        xn, r1 = _rms_stats(x_ref[...])
        d0, dg1 = _rms_bwd(dh1, xn, r1, g1_ref[...])
        dg1_ref[...] += dg1
        dx_ref[...] = dx1_ref[...] + d0

    def acc(n):
        return (_sds((1, n), F32), _full((1, n)))

    ins = [(dqs, _row(tm, 1024)), (dks, _row(tm, 256)), (dvs, _row(tm, 256)), (dqm, _heads(tm, HEADS)),
           (dkm, _row(tm, 1024)), (dvm, _row(tm, 1024))] + [(t, _row(tm, LANES)) for t in tabs] + [
           (consts, _full(consts.shape)), (cq, _row(tm, 256)), (ckv, _row(tm, 128)), (gq, _full(gq.shape)),
           (gkv, _full(gkv.shape)), (wuq, _full(wuq.shape)), (wk, _full(wk.shape)), (wv, _full(wv.shape)),
           (dgates, _row(tm, 2048)), (win, _resident(win.shape)), (x, _row(tm, D_MODEL)), (g1, _full(g1.shape)),
           (dx1, _row(tm, D_MODEL))]
    outs = [(_sds((t_rows, ZW), BF16), _row(tm, ZW)), (_sds((t_rows, 1024), BF16), _row(tm, 1024)),
            (_sds((t_rows, D_MODEL), F32), _row(tm, D_MODEL)), acc(256), acc(128), acc(D_MODEL)]
    return _rows_call("bwd_in", body, t_rows, tm, ins, outs)


def _pick_cols(n):
    best = LANES
    for d in range(LANES, min(n, 1408) + 1, LANES):
        if n % d == 0:
            best = d
    return best


def _mm_tn(name, a, b, column_shards=1, after=None):
    t_rows, m = a.shape
    n = b.shape[1]
    bk = min(1024, t_rows)
    bm, bn = _pick_cols(m), _pick_cols(n // column_shards)
    per_shard = n // column_shards // bn
    extra = () if after is None else (after,)

    def body(a_ref, b_ref, *rest):
        o_ref = rest[-1]

        @pl.when(pl.program_id(2) == 0)
        def _():
            o_ref[...] = jnp.zeros((bm, bn), F32)

        o_ref[...] += _dot_tn(a_ref[...].astype(BF16), b_ref[...].astype(BF16))

    return pl.pallas_call(
        body, name=name, grid=(m // bm, n // bn, t_rows // bk),
        in_specs=[pl.BlockSpec((bk, bm), lambda i, j, k: (k, i)), pl.BlockSpec((bk, bn), lambda i, j, k: (k, j))]
        + [pl.BlockSpec((8, LANES), lambda i, j, k: (0, 0))] * len(extra),
        out_specs=(pl.BlockSpec((bm, bn), lambda i, j, k: (i, j)) if column_shards == 1 else
                   pl.BlockSpec((None, bm, bn), lambda i, j, k: (j // per_shard, i, j % per_shard))),
        out_shape=_sds((m, n) if column_shards == 1 else (column_shards, m, n // column_shards), F32),
        compiler_params=pltpu.CompilerParams(dimension_semantics=("arbitrary",) * 3, vmem_limit_bytes=VMEM_LIMIT),
    )(a, b, *extra)


PACK_ROWS = 512


ADD_TILE_ELEMS = 1 << 17


def _add_rows(rows, cols):
    best = 16
    for d in range(16, rows + 1, 16):
        if rows % d == 0 and d * cols <= ADD_TILE_ELEMS:
            best = d
    assert rows % best == 0
    return best


def _add_pair(name, g, recv, half):
    _, _, rows, cols = g.shape
    t = _add_rows(rows, cols)

    def body(h_ref, g_ref, r_ref, o_ref):
        o_ref[...] = (g_ref[:, 0] + r_ref[...]).astype(BF16)

    spec = pl.BlockSpec((4, t, cols), lambda i, h: (0, i, 0))
    grid_spec = pltpu.PrefetchScalarGridSpec(
        num_scalar_prefetch=1, grid=(rows // t,),
        in_specs=[pl.BlockSpec((4, 1, t, cols), lambda i, h: (0, h[0], i, 0)), spec], out_specs=spec)
    return pl.pallas_call(body, name=name, grid_spec=grid_spec,
                          out_shape=_sds(recv.shape, BF16))(jnp.reshape(half, (1,)).astype(jnp.int32), g, recv)


def _add_chips(name, parts):
    _, rows, cols = parts.shape
    t = _add_rows(rows, cols)

    def body(p_ref, o_ref):
        acc = p_ref[0].astype(F32)
        for j in range(1, 4):
            acc = acc + p_ref[j].astype(F32)
        o_ref[...] = acc

    return pl.pallas_call(body, name=name, grid=(rows // t,),
                          in_specs=[pl.BlockSpec((4, t, cols), lambda i: (0, i, 0))],
                          out_specs=pl.BlockSpec((t, cols), lambda i: (i, 0)),
                          out_shape=_sds((rows, cols), F32))(parts)


def _add_devices(parts):
    n, rows, _ = parts.shape

    def body(p_ref, o_ref):
        acc = p_ref[0]
        for j in range(1, n):
            acc = acc + p_ref[j]
        o_ref[...] = acc

    return pl.pallas_call(body, name="small_add", grid=(1,),
                          in_specs=[pl.BlockSpec((n, rows, LANES), lambda i: (0, 0, 0))],
                          out_specs=pl.BlockSpec((rows, LANES), lambda i: (0, 0)),
                          out_shape=_sds((rows, LANES), F32))(parts)


def _adam_rows(k, n):
    target = max(8, (1 << 20) // (4 * n))
    if k <= target:
        return k
    best = None
    for d in range(8, target + 1, 8):
        if k % d == 0:
            best = d
    return best if best is not None else k


def _adam_update(w, g, m, v):
    m_ = ADAM_B1 * m + (1.0 - ADAM_B1) * g
    v_ = ADAM_B2 * v + (1.0 - ADAM_B2) * (g * g)
    delta = -ADAM_LR * ((m_ / (1.0 - ADAM_B1 ** ADAM_STEP)) / (jnp.sqrt(v_ / (1.0 - ADAM_B2 ** ADAM_STEP)) + ADAM_EPS)
                        + ADAM_WD * w)
    return delta, m_, v_


def _adamw(name, w, g, m, v):
    k, n = w.shape
    bk = _adam_rows(k, n)

    def body(w_ref, g_ref, m_ref, v_ref, d_ref, mo_ref, vo_ref):
        d_ref[...], mo_ref[...], vo_ref[...] = _adam_update(w_ref[...], g_ref[...], m_ref[...], v_ref[...])

    spec = pl.BlockSpec((bk, n), lambda i: (i, 0))
    out = pl.pallas_call(body, name=name, grid=(k // bk,), in_specs=[spec] * 4, out_specs=[spec] * 3,
                         out_shape=[_sds((k, n), F32)] * 3,
                         compiler_params=pltpu.CompilerParams(vmem_limit_bytes=VMEM_LIMIT))(w, g, m, v)
    return (g, *out)


def _adamw_halves(name, w, mine, theirs, m, v, half):
    k, n = w.shape
    bk = _adam_rows(k // 2, n)
    nb = k // 2 // bk

    def body(h_ref, w_ref, mine_ref, theirs_ref, m_ref, v_ref, g_ref, d_ref, mo_ref, vo_ref):
        g = jnp.where(pl.program_id(0) == h_ref[0], mine_ref[...], theirs_ref[...])
        g_ref[...] = g
        d_ref[...], mo_ref[...], vo_ref[...] = _adam_update(w_ref[...], g, m_ref[...], v_ref[...])

    full = pl.BlockSpec((bk, n), lambda h, i, c: (h * nb + i, 0))
    part = pl.BlockSpec((bk, n), lambda h, i, c: (i, 0))
    grid_spec = pltpu.PrefetchScalarGridSpec(num_scalar_prefetch=1, grid=(2, nb),
                                             in_specs=[full, part, part, full, full], out_specs=[full] * 4)
    return tuple(pl.pallas_call(
        body, name=name, grid_spec=grid_spec, out_shape=[_sds((k, n), F32)] * 4,
        compiler_params=pltpu.CompilerParams(vmem_limit_bytes=VMEM_LIMIT),
    )(jnp.reshape(half, (1,)).astype(jnp.int32), w, mine, theirs, m, v))


_HBM = pl.BlockSpec(memory_space=pltpu.HBM)


def _me():
    return lax.axis_index("x"), lax.axis_index("y"), lax.axis_index("c")


def _other_chips(x, y):
    return [(1 - x, y), (x, 1 - y), (1 - x, 1 - y)]


def _pass_to_sibling(zones):
    n = len(zones)

    def body(*refs):
        in_refs, out_refs = refs[:n], refs[n:2 * n]
        send_sems, recv_sems = refs[2 * n:]
        x, y, c = _me()
        sent = []
        for a, (in_ref, out_ref) in enumerate(zip(in_refs, out_refs)):
            for j, (cx, cy) in enumerate(_other_chips(x, y)):
                mine, theirs = (2 * cx + cy, c), (2 * cx + cy, 1 - c)
                sems = dict(send_sem=send_sems.at[3 * a + j], recv_sem=recv_sems.at[3 * a + j],
                            device_id=(x, y, 1 - c), device_id_type=MESH)
                sent.append((pltpu.make_async_remote_copy(src_ref=in_ref.at[mine], dst_ref=out_ref.at[mine], **sems),
                             pltpu.make_async_remote_copy(src_ref=in_ref.at[theirs], dst_ref=out_ref.at[theirs], **sems)))
        for send, _ in sent:
            send.start()
        for _, recv in sent:
            recv.wait_recv()
        for send, _ in sent:
            send.wait_send()

    return pl.pallas_call(
        body, name="pass_to_sibling", out_shape=[_sds(z.shape, z.dtype) for z in zones],
        in_specs=[_HBM] * n, out_specs=[_HBM] * n, input_output_aliases={i: i for i in range(n)},
        scratch_shapes=[pltpu.SemaphoreType.DMA((3 * n,)), pltpu.SemaphoreType.DMA((3 * n,))],
    )(*zones)


def _swap_sibling(name, vs, other_half=False):
    n = len(vs)

    def body(*refs):
        v_refs, out_refs = refs[:n], refs[n:2 * n]
        send_sems, recv_sems = refs[2 * n:]
        x, y, c = _me()
        cps = [pltpu.make_async_remote_copy(src_ref=v_ref.at[:, 1 - c] if other_half else v_ref, dst_ref=out_ref,
                                            send_sem=send_sems.at[a], recv_sem=recv_sems.at[a],
                                            device_id=(x, y, 1 - c), device_id_type=MESH)
               for a, (v_ref, out_ref) in enumerate(zip(v_refs, out_refs))]
        for cp in cps:
            cp.start()
        for cp in cps:
            cp.wait()

    def landing(v):
        return _sds((v.shape[0],) + v.shape[2:] if other_half else v.shape, v.dtype)

    return pl.pallas_call(
        body, name=name, out_shape=[landing(v) for v in vs], in_specs=[_HBM] * n, out_specs=[_HBM] * n,
        scratch_shapes=[pltpu.SemaphoreType.DMA((n,)), pltpu.SemaphoreType.DMA((n,))],
    )(*vs)


_SEM = pl.BlockSpec(memory_space=pltpu.SEMAPHORE)
_EFFECT = pltpu.SideEffectType.DATAFLOW_SIDE_EFFECTING
WHOLE = "whole"
PIECE = "piece"
SIBLING_HALF = "sibling"
MY_HALF = "half"
EVERYONE = "everyone"
_COPIES = {WHOLE: 3, PIECE: 3, MY_HALF: 3, SIBLING_HALF: 1, EVERYONE: 7}


def _landing_shape(v, mode):
    return {WHOLE: (4,) + v.shape, MY_HALF: (4,) + v.shape, PIECE: v.shape, EVERYONE: (8,) + v.shape,
            SIBLING_HALF: (v.shape[0],) + v.shape[2:]}[mode]


def _chip_copies(v_ref, land_ref, send_sems, recv_sems, mode, sem0=0):
    x, y, c = _me()
    if mode == SIBLING_HALF:
        cp = pltpu.make_async_remote_copy(src_ref=v_ref.at[:, 1 - c], dst_ref=land_ref, send_sem=send_sems.at[sem0],
                                          recv_sem=recv_sems.at[sem0], device_id=(x, y, 1 - c), device_id_type=MESH)
        return [(cp, cp)]
    if mode == EVERYONE:
        out = []
        for f in range(1, 8):
            px, py, pc = (1 - x if f & 4 else x), (1 - y if f & 2 else y), (1 - c if f & 1 else c)
            sems = dict(send_sem=send_sems.at[sem0 + f - 1], recv_sem=recv_sems.at[sem0 + f - 1],
                        device_id=(px, py, pc), device_id_type=MESH)
            out.append((pltpu.make_async_remote_copy(src_ref=v_ref, dst_ref=land_ref.at[4 * x + 2 * y + c], **sems),
                        pltpu.make_async_remote_copy(src_ref=v_ref, dst_ref=land_ref.at[4 * px + 2 * py + pc], **sems)))
        return out
    k = 2 * x + y
    out = []
    for j, (cx, cy) in enumerate(_other_chips(x, y)):
        if mode == MY_HALF:
            src, mine, theirs = v_ref.at[c], land_ref.at[k, c], land_ref.at[2 * cx + cy, c]
        else:
            src = v_ref.at[2 * cx + cy] if mode == PIECE else v_ref
            mine, theirs = land_ref.at[k], land_ref.at[2 * cx + cy]
        sems = dict(send_sem=send_sems.at[sem0 + j], recv_sem=recv_sems.at[sem0 + j], device_id=(cx, cy, c),
                    device_id_type=MESH)
        send = pltpu.make_async_remote_copy(src_ref=src, dst_ref=mine, **sems)
        recv = pltpu.make_async_remote_copy(src_ref=src, dst_ref=theirs, **sems)
        out.append((send, recv))
    return out


def _chips_start(name, vs, mode, after=None):
    n = len(vs)
    lands = [_landing_shape(v, mode) for v in vs]

    def body(*refs):
        v_refs, land_refs = refs[:n], refs[n:2 * n]
        send_sems, recv_sems = refs[-2 * n - 3], refs[-2 * n - 2]
        token = refs[-1]
        for a in range(n):
            for send, _ in _chip_copies(v_refs[a], land_refs[a], send_sems, recv_sems, mode, _COPIES[mode] * a):
                send.start()
        token[...] = jnp.zeros_like(token)

    extra = () if after is None else (after,)
    hbm = [pltpu.with_memory_space_constraint(v, pltpu.HBM) for v in vs]
    zones = [pltpu.with_memory_space_constraint(lax.empty(s, v.dtype), pltpu.HBM) for s, v in zip(lands, vs)]
    out = pl.pallas_call(
        body, name=name,
        out_shape=(pltpu.SemaphoreType.DMA((_COPIES[mode] * n,)), pltpu.SemaphoreType.DMA((_COPIES[mode] * n,)),
                   *[pltpu.HBM(v.shape, v.dtype) for v in vs], *[pltpu.HBM(s, v.dtype) for s, v in zip(lands, vs)],
                   _sds((8, LANES), F32)),
        in_specs=(_HBM,) * (2 * n) + (pl.BlockSpec(memory_space=pl.ANY),) * len(extra),
        out_specs=(_SEM, _SEM) + (_HBM,) * (2 * n) + (pl.BlockSpec(memory_space=pltpu.VMEM),),
        input_output_aliases={i: 2 + i for i in range(2 * n)},
        compiler_params=pltpu.CompilerParams(has_side_effects=_EFFECT),
    )(*hbm, *zones, *extra)
    return out[0], out[1], list(out[2:2 + n]), list(out[2 + n:2 + 2 * n]), out[-1]


def _chips_wait(name, send_sems, recv_sems, v_thru, land_thru, mode, after):
    n = len(v_thru)

    def body(*refs):
        v_refs, land_refs = refs[:n], refs[n:2 * n]
        send_sems, recv_sems = refs[2 * n], refs[2 * n + 1]
        for a in range(n):
            for send, recv in _chip_copies(v_refs[a], land_refs[a], send_sems, recv_sems, mode, _COPIES[mode] * a):
                send.wait_send()
                recv.wait_recv()

    out = pl.pallas_call(
        body, name=name,
        out_shape=tuple(pltpu.HBM(a.shape, a.dtype) for a in list(v_thru) + list(land_thru)),
        in_specs=(_HBM,) * (2 * n) + (_SEM, _SEM, pl.BlockSpec(memory_space=pl.ANY)), out_specs=(_HBM,) * (2 * n),
        input_output_aliases={i: i for i in range(2 * n)},
        compiler_params=pltpu.CompilerParams(has_side_effects=_EFFECT),
    )(*v_thru, *land_thru, send_sems, recv_sems, after)
    return list(out[:n]), list(out[n:])


_BIG = (("w_in", (1024, 3232), 1), ("w_uq", (256, 768), 1), ("w_ukv", (128, 1024), 1), ("w_branch_a", (512, 1024), 1),
        ("w_branch_b", (512, 1024), 1), ("w_out", (1024, 1024), 0), ("w_up", (1024, 5632), 1),
        ("w_down", (2816, 1024), 0), ("w_ple_gate", (1024, 1024), 0), ("w_ple", (256, 1024), 1))


def _shard_shape(shape, axis):
    return (shape[0] // 4, shape[1]) if axis == 0 else (shape[0], shape[1] // 4)


def _half_rows(shape, axis):
    k, n = _shard_shape(shape, axis)
    return k * n // (2 * LANES)


_EARLY = ("w_in", "w_uq", "w_ukv")
_LATE = ("w_branch_a", "w_branch_b", "w_out", "w_up", "w_down", "w_ple_gate", "w_ple")
_NATURAL = ("w_in", "w_up", "w_down", "w_out", "w_ple_gate")
_EARLY_PACKED = tuple(b for b in _BIG if b[0] in _EARLY and b[0] not in _NATURAL)
_LATE_PACKED = tuple(b for b in _BIG if b[0] in _LATE and b[0] not in _NATURAL)
_SHARD = {name: _shard_shape(shape, axis) for name, shape, axis in _BIG}


def _halves(a):
    return a.reshape(a.shape[:-2] + (2, a.shape[-2] // 2, a.shape[-1]))


def _rows_joined(a):
    return a.reshape(a.shape[:-3] + (a.shape[-3] * a.shape[-2], a.shape[-1]))


def _pack_pad(group):
    return -sum(_half_rows(shape, axis) for _, shape, axis in group) % PACK_ROWS


def _pack_shards(shards, dtype, group):
    parts = [shards[name].astype(dtype).reshape(2, _half_rows(shape, axis), LANES) for name, shape, axis in group]
    return jnp.concatenate(parts + [jnp.zeros((2, _pack_pad(group), LANES), dtype)], axis=1)


def _unpack_gathered(g, group):
    out, off = {}, 0
    for name, shape, axis in group:
        r = _half_rows(shape, axis)
        k, n = _shard_shape(shape, axis)
        w = g[:, :, off:off + r, :].reshape(4, k, n)
        out[name] = w.reshape(shape) if axis == 0 else w.transpose(1, 0, 2).reshape(shape)
        off += r
    return out


def _pack_grads(grads, group):
    parts = []
    for name, shape, axis in group:
        k, n = _shard_shape(shape, axis)
        g = grads[name]
        g4 = g.reshape(4, k, n) if axis == 0 else g.reshape(k, 4, n).transpose(1, 0, 2)
        parts.append(g4.reshape(4, 2, _half_rows(shape, axis), LANES))
    return jnp.concatenate(parts + [jnp.zeros((4, 2, _pack_pad(group), LANES), F32)], axis=2)


def _unpack_shard_grads(f, group):
    out, off = {}, 0
    for name, shape, axis in group:
        r = _half_rows(shape, axis)
        out[name] = f[:, off:off + r, :].reshape(_shard_shape(shape, axis))
        off += r
    return out


def _pad_slots(w, heads, dim, axis):
    if axis == 1:
        k = w.shape[0]
        return jnp.pad(w.reshape(k, heads, dim), ((0, 0), (0, 0), (0, LANES - dim))).reshape(k, heads * LANES)
    n = w.shape[1]
    return jnp.pad(w.reshape(heads, dim, n), ((0, 0), (0, LANES - dim), (0, 0))).reshape(heads * LANES, n)


def _unpad_slots(w, heads, dim, axis):
    if axis == 1:
        k = w.shape[0]
        return w.reshape(k, heads, LANES)[:, :, :dim].reshape(k, heads * dim)
    n = w.shape[1]
    return w.reshape(heads, LANES, n)[:, :dim, :].reshape(heads * dim, n)


def _pad_w_in(w):
    kr = jnp.pad(w[:, 1152:1184], ((0, 0), (NOPE_DIM, LANES - NOPE_DIM - ROPE_DIM)))
    return jnp.concatenate([_pad_slots(w[:, 0:512], HEADS, A_HEAD_DIM, 1),
                            _pad_slots(w[:, 512:640], A_KV_HEADS, A_HEAD_DIM, 1),
                            _pad_slots(w[:, 640:768], A_KV_HEADS, A_HEAD_DIM, 1),
                            w[:, 768:1024], w[:, 1024:1152], kr, w[:, 1184:3232]], axis=1)


def _unpad_w_in(w):
    return jnp.concatenate([_unpad_slots(w[:, Z_QA:Z_KA], HEADS, A_HEAD_DIM, 1),
                            _unpad_slots(w[:, Z_KA:Z_VA], A_KV_HEADS, A_HEAD_DIM, 1),
                            _unpad_slots(w[:, Z_VA:Z_CQ], A_KV_HEADS, A_HEAD_DIM, 1),
                            w[:, Z_CQ:Z_CKV], w[:, Z_CKV:Z_KR],
                            w[:, Z_KR + NOPE_DIM:Z_KR + NOPE_DIM + ROPE_DIM], w[:, Z_GATE:ZW]], axis=1)


_SMALL = (("attn_pre_norm", 1024), ("attn_post_norm", 1024), ("b_gate", 2048), ("sinks", 8), ("q_a_norm", 256),
          ("kv_a_norm", 128), ("mlp_pre_norm", 1024), ("mlp_post_norm", 1024), ("conv_b", 5632), ("ple_norm", 1024),
          ("conv_w", 3 * 5632), ("loss", 1))


def _small_rows(n):
    return 8 * -(-n // (8 * LANES))


def _pack_small(vals):
    parts = []
    for name, n in _SMALL:
        r = _small_rows(n)
        parts.append(jnp.pad(vals[name].reshape(-1), (0, r * LANES - n)).reshape(r, LANES))
    return jnp.concatenate(parts, axis=0)


def _unpack_small(buf):
    out, off = {}, 0
    for name, n in _SMALL:
        r = _small_rows(n)
        out[name] = buf[off:off + r].reshape(-1)[:n]
        off += r
    return out


def kernel(x, p, positions, attn_pre_norm, attn_post_norm, w_in, b_gate, sinks, q_a_norm, w_uq, kv_a_norm, w_ukv, w_branch_a, w_branch_b, w_out, mlp_pre_norm, mlp_post_norm, w_up, conv_w, conv_b, w_down, ple_norm, w_ple_gate, w_ple, loss_target, m_attn_pre_norm, m_attn_post_norm, m_w_in, m_b_gate, m_sinks, m_q_a_norm, m_w_uq, m_kv_a_norm, m_w_ukv, m_w_branch_a, m_w_branch_b, m_w_out, m_mlp_pre_norm, m_mlp_post_norm, m_w_up, m_conv_w, m_conv_b, m_w_down, m_ple_norm, m_w_ple_gate, m_w_ple, v_attn_pre_norm, v_attn_post_norm, v_w_in, v_b_gate, v_sinks, v_q_a_norm, v_w_uq, v_kv_a_norm, v_w_ukv, v_w_branch_a, v_w_branch_b, v_w_out, v_mlp_pre_norm, v_mlp_post_norm, v_w_up, v_conv_w, v_conv_b, v_w_down, v_ple_norm, v_w_ple_gate, v_w_ple):
    names = ["attn_pre_norm", "attn_post_norm", "w_in", "b_gate", "sinks", "q_a_norm", "w_uq", "kv_a_norm", "w_ukv",
             "w_branch_a", "w_branch_b", "w_out", "mlp_pre_norm", "mlp_post_norm", "w_up", "conv_w", "conv_b",
             "w_down", "ple_norm", "w_ple_gate", "w_ple"]
    wts = dict(zip(names, [attn_pre_norm, attn_post_norm, w_in, b_gate, sinks, q_a_norm, w_uq, kv_a_norm, w_ukv,
                           w_branch_a, w_branch_b, w_out, mlp_pre_norm, mlp_post_norm, w_up, conv_w, conv_b, w_down,
                           ple_norm, w_ple_gate, w_ple]))
    moms = dict(zip(names, [m_attn_pre_norm, m_attn_post_norm, m_w_in, m_b_gate, m_sinks, m_q_a_norm, m_w_uq,
                            m_kv_a_norm, m_w_ukv, m_w_branch_a, m_w_branch_b, m_w_out, m_mlp_pre_norm,
                            m_mlp_post_norm, m_w_up, m_conv_w, m_conv_b, m_w_down, m_ple_norm, m_w_ple_gate, m_w_ple]))
    vars_ = dict(zip(names, [v_attn_pre_norm, v_attn_post_norm, v_w_in, v_b_gate, v_sinks, v_q_a_norm, v_w_uq,
                             v_kv_a_norm, v_w_ukv, v_w_branch_a, v_w_branch_b, v_w_out, v_mlp_pre_norm,
                             v_mlp_post_norm, v_w_up, v_conv_w, v_conv_b, v_w_down, v_ple_norm, v_w_ple_gate, v_w_ple]))
    w2 = {n: a.reshape(a.shape[-2:]) for n, a in wts.items()}
    m2 = {n: a.reshape(a.shape[-2:]) for n, a in moms.items()}
    v2 = {n: a.reshape(a.shape[-2:]) for n, a in vars_.items()}

    t_rows = x.shape[-2]
    tm = min(256, t_rows)
    tm_wide = min(512, t_rows)
    xc, yc, cc = lax.axis_index("x"), lax.axis_index("y"), lax.axis_index("c")
    chip = 2 * xc + yc

    x2d = x.reshape(t_rows, D_MODEL)
    p2d = p.reshape(t_rows, PLE_DIM)
    tgt = loss_target.reshape(t_rows, D_MODEL)
    pos_f = positions.reshape(t_rows, 1).astype(F32)

    def own_slot_filled(gathered, mine):
        return [lax.dynamic_update_slice(g, m[None], (chip, 0, 0, 0)) for g, m in zip(gathered, mine)]

    def shard_lists(group, packed_group, token=0.0):
        ws = {n: w2[n] + token for n in group}
        return [_halves(ws[n].astype(BF16)) for n in group if n in _NATURAL] + [_pack_shards(ws, BF16, packed_group)]

    cw_rows = 3 * 1408 // LANES
    conv_mine = jnp.pad(w2["conv_w"].reshape(cw_rows, LANES), ((0, 48 - cw_rows), (0, 0))).reshape(2, 24, LANES)
    early_mine = shard_lists(_EARLY, _EARLY_PACKED) + [conv_mine]
    early_sems = _chips_start("gather_early_start", early_mine, MY_HALF)
    early_token = early_sems[4][0:1, 0:1]
    consts = _rope_consts()
    tabs = _rope_tables(pos_f + early_token, consts, tm)
    late_mine = shard_lists(_LATE, _LATE_PACKED, early_token)
    both_done = tabs[0][0:1, 0:1] + sum(m[0, 0:1, 0:1].astype(F32) for m in late_mine)
    early_sent, early_landed = _chips_wait("gather_early_wait", *early_sems[:4], MY_HALF, after=both_done)
    early = own_slot_filled(_pass_to_sibling(early_landed), early_sent)
    late_sems = _chips_start("gather_late_start", late_mine, WHOLE, after=early[0])
    late_token = late_sems[4][0:1, 0:1]
    full = _unpack_gathered(early[1], _EARLY_PACKED)
    full["w_in"] = _rows_joined(early[0]).transpose(1, 0, 2).reshape(D_MODEL, 3232)
    conv_full = early[2].reshape(4, 48, LANES)[:, :cw_rows].reshape(4, 3, 1408).transpose(1, 0, 2).reshape(3, 2 * D_FF)
    convw8 = jnp.pad(conv_full, ((0, 5), (0, 0)))

    win = _pad_w_in(full["w_in"])
    wuq = _pad_slots(full["w_uq"], HEADS, NOPE_DIM + ROPE_DIM, 1)
    ukv = full["w_ukv"].reshape(KV_LORA, HEADS, NOPE_DIM + V_DIM)
    wk = _pad_slots(ukv[:, :, :NOPE_DIM].reshape(KV_LORA, HEADS * NOPE_DIM), HEADS, NOPE_DIM, 1)
    wv = _pad_slots(ukv[:, :, NOPE_DIM:].reshape(KV_LORA, HEADS * V_DIM), HEADS, V_DIM, 1)
    g1, g2, g3, g4, g5 = (w2["attn_pre_norm"], w2["attn_post_norm"], w2["mlp_pre_norm"], w2["mlp_post_norm"],
                          w2["ple_norm"])
    gq, gkv, bg, convb = w2["q_a_norm"], w2["kv_a_norm"], w2["b_gate"], w2["conv_b"]
    swa_tile = min(SWA_TILE, t_rows)
    sink_rows = jnp.repeat(w2["sinks"].reshape(A_KV_HEADS, SWA_GROUP, 1), swa_tile, axis=2).reshape(
        A_KV_HEADS, 1, SWA_GROUP * swa_tile)
    swa_bias = _swa_bias(swa_tile)

    h1, qs, ks, vs, cq, cqn, ckv, ckvn, qm, km, vm, gate = _fwd_in(x2d, g1, win, bg + late_token, gq, gkv, wuq, wk, wv,
                                                                   tabs, tm_wide)
    ya, lse_a = _swa_fwd(qs, ks, vs, swa_bias, sink_rows)
    yb, lse_b = _mla_fwd(qm, km, vm)
    late_sent, late_landed = _chips_wait("gather_late_wait", *late_sems[:4], WHOLE, after=yb)
    late = own_slot_filled(late_landed, late_sent)
    full = _unpack_gathered(late[-1], _LATE_PACKED)
    wba = _pad_slots(full["w_branch_a"], HEADS, A_HEAD_DIM, 0)
    wbb = _pad_slots(full["w_branch_b"], HEADS, V_DIM, 0)
    wple = full["w_ple"]
    natural = dict(zip([n for n in _LATE if n in _NATURAL], late))
    wup = _rows_joined(natural["w_up"])
    wout, wdown, wpg = (_rows_joined(natural[n]).reshape(-1, D_MODEL) for n in ("w_out", "w_down", "w_ple_gate"))
    pa, pb, mixed, o, x1, h2 = _fwd_mix(x2d, ya, yb, gate, wba, wbb, wout, g2, g3, tm_wide)
    up, a = _fwd_up(h2, wup, convw8, convb, tm)
    ff, x2, e, n5, sg, dx3, loss_part = _fwd_out(a, wdown, x1, g4, p2d, wple, g5, wpg, tgt, tm_wide)

    dpre, de, dx2, dff, du, dg5, dg4, dconvb, dconvw8 = _bwd_out(dx3, e, sg, x2, ff, g5, g4, wpg, wdown, up, convw8,
                                                                 convb, tm)
    dup, dx1, do, dpa, dpb, dgates, dya, dyb, delta_b, dg3, dg2, dbg = _bwd_mid(
        du, convw8, wup, dx2, x1, g3, o, g2, wout, gate, pa, pb, wba, wbb, yb, tm)
    late_grads = {
        "w_branch_a": _unpad_slots(_mm_tn("dw_branch_a", ya, dpa), HEADS, A_HEAD_DIM, 0),
        "w_branch_b": _unpad_slots(_mm_tn("dw_branch_b", yb, dpb), HEADS, V_DIM, 0),
        "w_out": _mm_tn("dw_out", mixed, do).reshape(4, D_MODEL // 4, D_MODEL),
        "w_up": _mm_tn("dw_up", h2, dup, column_shards=4),
        "w_down": _mm_tn("dw_down", a, dff).reshape(4, D_FF // 4, D_MODEL),
        "w_ple_gate": _mm_tn("dw_ple_gate", n5, dpre).reshape(4, D_MODEL // 4, D_MODEL),
        "w_ple": _mm_tn("dw_ple", p2d, de),
    }

    def grad_views(grads, group, packed_group):
        return [_halves(grads[n]) for n in group if n in _NATURAL] + [_pack_grads(grads, packed_group)]

    def pair_sums(tag, views, theirs):
        return [_add_pair("rs_%s_add_pair_%d" % (tag, i), g, r, cc) for i, (g, r) in enumerate(zip(views, theirs))]

    swap_sems = _chips_start("swap_late_start", grad_views(late_grads, _LATE, _LATE_PACKED), SIBLING_HALF)
    dqs, dks, dvs, dsink_rows = _swa_bwd(qs, ks, vs, ya, dya, lse_a, swa_bias, sink_rows + swap_sems[4][0:1, 0:1])
    dsink = dsink_rows[:, 0:SWA_GROUP, 0]
    late_views, late_theirs = _chips_wait("swap_late_wait", *swap_sems[:4], SIBLING_HALF, after=dqs)
    rs_sems = _chips_start("scatter_late_start", pair_sums("late", late_views, late_theirs), PIECE)
    dqm, dkm, dvm = _mla_bwd(qm, km, vm, dyb, lse_b, delta_b.reshape(HEADS, 1, t_rows) + rs_sems[4][0:1, 0:1])
    dz, dqb, dx, dgq, dgkv, dg1 = _bwd_in(dqs, dks, dvs, dqm, dkm, dvm, tabs, consts, cq, ckv, gq, gkv, wuq, wk, wv,
                                           dgates, win, x2d, g1, dx1, tm)

    small = {"attn_pre_norm": dg1, "attn_post_norm": dg2, "b_gate": dbg, "sinks": dsink, "q_a_norm": dgq,
             "kv_a_norm": dgkv, "mlp_pre_norm": dg3, "mlp_post_norm": dg4, "conv_b": dconvb, "ple_norm": dg5,
             "conv_w": dconvw8[0:3], "loss": loss_part}
    small_sems = _chips_start("gather_small_start", [_pack_small(small)], EVERYONE)
    small_token = small_sems[4]

    dwk = _unpad_slots(_mm_tn("dw_k", ckvn, dkm, after=small_token), HEADS, NOPE_DIM, 1).reshape(
        KV_LORA, HEADS, NOPE_DIM)
    dwv = _unpad_slots(_mm_tn("dw_v", ckvn, dvm, after=small_token), HEADS, V_DIM, 1).reshape(KV_LORA, HEADS, V_DIM)
    early_grads = {
        "w_in": _unpad_w_in(_mm_tn("dw_in", h1, dz, after=small_token)).reshape(D_MODEL, 4, 808).transpose(1, 0, 2),
        "w_uq": _unpad_slots(_mm_tn("dw_uq", cqn, dqb, after=small_token), HEADS, NOPE_DIM + ROPE_DIM, 1),
        "w_ukv": jnp.concatenate([dwk, dwv], axis=2).reshape(KV_LORA, HEADS * (NOPE_DIM + V_DIM)),
    }

    def finish(tag, pairs, landed, group, packed_group):
        reduced = []
        for i, (pair, land) in enumerate(zip(pairs, landed)):
            own = lax.dynamic_index_in_dim(pair, chip, 0, keepdims=True)
            reduced.append(_add_chips("rs_%s_add_chips_%d" % (tag, i),
                                      lax.dynamic_update_slice(land, own, (chip, 0, 0))))
        others = _swap_sibling("swap_%s_reduced_halves" % tag, reduced)
        r, o = reduced[-1], others[-1]
        packed = jnp.where(cc == 0, jnp.stack([r, o]), jnp.stack([o, r]))
        for n, g in _unpack_shard_grads(packed, packed_group).items():
            updates[n] = _adamw("adamw_" + n, w2[n], g, m2[n], v2[n])
        for n, r, o in zip([n for n in group if n in _NATURAL], reduced, others):
            updates[n] = _adamw_halves("adamw_" + n, w2[n], r, o, m2[n], v2[n], cc)

    updates = {}

    def adamw(n, g):
        updates[n] = _adamw("adamw_" + n, w2[n], g, m2[n], v2[n])

    early_views = grad_views(early_grads, _EARLY, _EARLY_PACKED)
    early_theirs = _swap_sibling("swap_early_grad_halves", early_views, other_half=True)
    small_sent, small_landed = _chips_wait("gather_small_wait", *small_sems[:4], EVERYONE, after=early_theirs[0])
    small_all = lax.dynamic_update_slice(small_landed[0], small_sent[0][None], (4 * xc + 2 * yc + cc, 0, 0))
    early_sems = _chips_start("scatter_early_start", pair_sums("early", early_views, early_theirs), PIECE,
                              after=small_all)
    late_pairs, late_landed = _chips_wait("scatter_late_wait", *rs_sems[:4], PIECE, after=early_sems[4])
    finish("late", late_pairs, late_landed, _LATE, _LATE_PACKED)
    early_pairs, early_landed = _chips_wait("scatter_early_wait", *early_sems[:4], PIECE,
                                            after=updates[_LATE[-1]][1])
    finish("early", early_pairs, early_landed, _EARLY, _EARLY_PACKED)

    small_sum = _unpack_small(_add_devices(small_all))
    for n in names:
        if n == "conv_w":
            adamw(n, lax.dynamic_index_in_dim(small_sum[n].reshape(3, 4, 1408), chip, 1, keepdims=False))
        elif n in small_sum:
            adamw(n, small_sum[n].reshape(w2[n].shape))
    loss = small_sum["loss"][0]

    outs = [[updates[n][i].reshape(wts[n].shape) for n in names] for i in range(4)]
    return (loss, dx.reshape(x.shape), *outs[0], *outs[1], *outs[2], *outs[3])
```

```python
import functools
import math

import numpy as np
import jax
import jax.numpy as jnp
from jax import lax
from jax.experimental import pallas as pl
from jax.experimental.pallas import tpu as pltpu

F32 = jnp.float32
BF16 = jnp.bfloat16

D_MODEL = 1024
D_FF = 2816
PLE_DIM = 256
ROPE_THETA = 10000.0
RMS_EPS = 1e-6
SWA_WINDOW = 128
HEADS = 8
A_KV_HEADS = 2
A_HEAD_DIM = 64
Q_LORA = 256
KV_LORA = 128
NOPE_DIM = 64
ROPE_DIM = 32
V_DIM = 64
LANES = 128
ZW = 4096
NEG = -1e30
SCALE_A = A_HEAD_DIM ** -0.5
SCALE_B = (NOPE_DIM + ROPE_DIM) ** -0.5

ADAM_LR = 0.001
ADAM_B1 = 0.9
ADAM_B2 = 0.999
ADAM_EPS = 1e-08
ADAM_WD = 0.01
ADAM_STEP = 10

VMEM_LIMIT = 60 * 1024 * 1024
MESH_AXES = ("x", "y", "c")
MESH = pl.DeviceIdType.MESH

Z_QA, Z_KA, Z_VA, Z_CQ, Z_CKV, Z_KR, Z_GATE = 0, 1024, 1280, 1536, 1792, 1920, 2048


def _dot(a, b):
    return jnp.dot(a, b, preferred_element_type=F32)


def _dot_nt(a, b):
    return lax.dot_general(a, b, (((1,), (1,)), ((), ())), preferred_element_type=F32)


def _dot_tn(a, b):
    return lax.dot_general(a, b, (((0,), (0,)), ((), ())), preferred_element_type=F32)


def _rms_stats(x):
    r = lax.rsqrt(jnp.mean(x * x, axis=-1, keepdims=True) + RMS_EPS)
    return x * r, r


def _rms_bwd(dy, xn, r, g):
    dxn = dy * g
    dx = r * (dxn - xn * jnp.mean(dxn * xn, axis=-1, keepdims=True))
    dg = jnp.sum(dy * xn, axis=0, keepdims=True)
    return dx, dg


def _tile_lanes(t, n):
    return t if n == 1 else jnp.concatenate([t] * n, axis=1)


def _rope(x, c, s1, s2, half):
    w = x.shape[1]
    n = w // LANES
    return (x * _tile_lanes(c, n) + pltpu.roll(x, w - half, 1) * _tile_lanes(s1, n)
            + pltpu.roll(x, half, 1) * _tile_lanes(s2, n))


def _rope_t(dy, c, s1, s2, half):
    w = dy.shape[1]
    n = w // LANES
    return (dy * _tile_lanes(c, n) + pltpu.roll(dy * _tile_lanes(s1, n), half, 1)
            + pltpu.roll(dy * _tile_lanes(s2, n), w - half, 1))


def _sigmoid(x):
    return 1.0 / (1.0 + jnp.exp(-x))


_GELU_C = math.sqrt(2.0 / math.pi)


def _gelu_and_grad(x):
    a = _GELU_C + (_GELU_C * 0.044715) * (x * x)
    th = jnp.tanh(x * a)
    hx = 0.5 * x
    p1 = 1.0 + th
    gel = hx * p1
    dgel = 0.5 * p1 + (hx * (1.0 - th * th)) * (3.0 * a - 2.0 * _GELU_C)
    return gel, dgel


def _conv_taps(up, h6, h7):
    r1 = pltpu.roll(up, 1, 0)
    r2 = pltpu.roll(up, 2, 0)
    rows = lax.broadcasted_iota(jnp.int32, (8, up.shape[1]), 0)
    xm1 = jnp.concatenate([jnp.where(rows == 0, h7, r1[0:8]), r1[8:]], axis=0)
    xm2 = jnp.concatenate([jnp.where(rows == 0, h6, jnp.where(rows == 1, h7, r2[0:8])), r2[8:]], axis=0)
    return xm1, xm2


def _conv_taps_next(du, n0, n1):
    tm = du.shape[0]
    r1 = pltpu.roll(du, tm - 1, 0)
    r2 = pltpu.roll(du, tm - 2, 0)
    rows = lax.broadcasted_iota(jnp.int32, (8, du.shape[1]), 0)
    xp1 = jnp.concatenate([r1[:tm - 8], jnp.where(rows == 7, n0, r1[tm - 8:])], axis=0)
    xp2 = jnp.concatenate([r2[:tm - 8], jnp.where(rows == 6, n0, jnp.where(rows == 7, n1, r2[tm - 8:]))], axis=0)
    return xp1, xp2


def _row(tm, n):
    return pl.BlockSpec((tm, n), lambda i: (i, 0))


def _full(shape):
    nd = len(shape)
    return pl.BlockSpec(tuple(shape), lambda i: (0,) * nd)


def _resident(shape):
    nd = len(shape)
    return pl.BlockSpec(tuple(shape), lambda i: (0,) * nd, pipeline_mode=pl.Buffered(1))


def _heads(tm, h):
    return pl.BlockSpec((h, tm, LANES), lambda i: (0, i, 0))


def _rows_call(name, body, t_rows, tm, ins, outs, scratch=()):
    return pl.pallas_call(
        body, name=name, grid=(t_rows // tm,),
        in_specs=[s for _, s in ins],
        out_specs=[s for _, s in outs],
        out_shape=[s for s, _ in outs],
        scratch_shapes=list(scratch),
        compiler_params=pltpu.CompilerParams(dimension_semantics=("arbitrary",), vmem_limit_bytes=VMEM_LIMIT),
    )(*[a for a, _ in ins])


def _sds(shape, dtype):
    return jax.ShapeDtypeStruct(tuple(shape), dtype)


def _rope_consts():
    c = np.zeros((16, LANES), np.float32)
    lane = np.arange(LANES)
    inv_a = (ROPE_THETA ** (-(np.arange(0, A_HEAD_DIM, 2, dtype=np.float32) / A_HEAD_DIM))).astype(np.float32)
    in_a = lane < A_HEAD_DIM
    c[0, in_a] = inv_a[lane[in_a] % (A_HEAD_DIM // 2)]
    c[1, in_a] = 1.0
    c[2, lane < A_HEAD_DIM // 2] = -1.0
    c[3, (lane >= A_HEAD_DIM // 2) & in_a] = 1.0
    inv_b = (ROPE_THETA ** (-(np.arange(0, ROPE_DIM, 2, dtype=np.float32) / ROPE_DIM))).astype(np.float32)
    pe = (lane >= NOPE_DIM) & (lane < NOPE_DIM + ROPE_DIM)
    c[5, pe] = inv_b[(lane[pe] - NOPE_DIM) % (ROPE_DIM // 2)]
    c[6, pe] = 1.0
    c[7, (lane >= NOPE_DIM) & (lane < NOPE_DIM + ROPE_DIM // 2)] = -1.0
    c[8, (lane >= NOPE_DIM + ROPE_DIM // 2) & (lane < NOPE_DIM + ROPE_DIM)] = 1.0
    c[9, lane < NOPE_DIM] = 1.0
    c[10, pe] = 1.0
    return jnp.asarray(c)


def _rope_tables(pos_f, consts, tm):
    t_rows = pos_f.shape[0]

    def body(pos_ref, c_ref, ca, sa1, sa2, cb, sb1, sb2):
        ang = pos_ref[...] * (c_ref[0:1, :] + c_ref[5:6, :])
        cs, sn = jnp.cos(ang), jnp.sin(ang)
        ca[...] = cs * c_ref[1:2, :]
        sa1[...] = sn * c_ref[2:3, :]
        sa2[...] = sn * c_ref[3:4, :]
        cb[...] = cs * c_ref[6:7, :] + c_ref[9:10, :]
        sb1[...] = sn * c_ref[7:8, :]
        sb2[...] = sn * c_ref[8:9, :]

    tab = (_sds((t_rows, LANES), F32), _row(tm, LANES))
    return _rows_call("rope_tables", body, t_rows, tm,
                      [(pos_f, _row(tm, 1)), (consts, _full(consts.shape))], [tab] * 6)


def _fwd_in(x, g1, win, bg, gq, gkv, wuq, wk, wv, tabs, tm):
    t_rows = x.shape[0]

    def body(x_ref, g1_ref, win_ref, bg_ref, gq_ref, gkv_ref, wuq_ref, wk_ref, wv_ref,
             ca, sa1, sa2, cb, sb1, sb2,
             h1_ref, qs_ref, ks_ref, vs_ref, cq_ref, cqn_ref, ckv_ref, ckvn_ref, qm_ref, km_ref, vm_ref, gate_ref):
        xn, _ = _rms_stats(x_ref[...])
        hb = (xn * g1_ref[...]).astype(BF16)
        h1_ref[...] = hb
        ta = (ca[...], sa1[...], sa2[...])
        tb = (cb[...], sb1[...], sb2[...])
        qs_ref[...] = (_rope(_dot(hb, win_ref[:, Z_QA:Z_KA]), *ta, A_HEAD_DIM // 2) * SCALE_A).astype(BF16)
        ks_ref[...] = _rope(_dot(hb, win_ref[:, Z_KA:Z_VA]), *ta, A_HEAD_DIM // 2).astype(BF16)
        vs_ref[...] = _dot(hb, win_ref[:, Z_VA:Z_CQ]).astype(BF16)
        cq = _dot(hb, win_ref[:, Z_CQ:Z_CKV])
        cq_ref[...] = cq
        cqn, _ = _rms_stats(cq)
        cqb = (cqn * gq_ref[...]).astype(BF16)
        cqn_ref[...] = cqb
        qm_ref[...] = (_rope(_dot(cqb, wuq_ref[...]), *tb, ROPE_DIM // 2) * SCALE_B).astype(BF16)
        ckv = _dot(hb, win_ref[:, Z_CKV:Z_KR])
        ckv_ref[...] = ckv
        ckvn, _ = _rms_stats(ckv)
        ckvb = (ckvn * gkv_ref[...]).astype(BF16)
        ckvn_ref[...] = ckvb
        kpe = _rope(_dot(hb, win_ref[:, Z_KR:Z_GATE]), *tb, ROPE_DIM // 2)
        km_ref[...] = (_dot(ckvb, wk_ref[...]) + _tile_lanes(kpe, HEADS)).astype(BF16)
        vm_ref[...] = _dot(ckvb, wv_ref[...]).astype(BF16)
        gate_ref[...] = _sigmoid(_dot(hb, win_ref[:, Z_GATE:ZW]) + bg_ref[...])

    def o(n, dt):
        return (_sds((t_rows, n), dt), _row(tm, n))

    ins = [(x, _row(tm, D_MODEL)), (g1, _full(g1.shape)), (win, _resident(win.shape)), (bg, _full(bg.shape)),
           (gq, _full(gq.shape)), (gkv, _full(gkv.shape)), (wuq, _full(wuq.shape)), (wk, _full(wk.shape)),
           (wv, _full(wv.shape))] + [(t, _row(tm, LANES)) for t in tabs]
    outs = [o(1024, BF16), o(1024, BF16), o(256, BF16), o(256, BF16), o(256, F32), o(256, BF16), o(128, F32),
            o(128, BF16), o(1024, BF16), o(1024, BF16), o(1024, BF16), o(2048, F32)]
    return _rows_call("fwd_in", body, t_rows, tm, ins, outs)


def _attn_tile(t_rows):
    return min(512, t_rows)


MLA_HEADS_PER_STEP = 4
MLA_FWD_HEADS_PER_STEP = 8


def _causal_pairs(nq, by_kv):
    if by_kv:
        pairs = [(i, j) for j in range(nq) for i in range(j, nq)]
    else:
        pairs = [(i, j) for i in range(nq) for j in range(i + 1)]
    return (jnp.asarray([p[0] for p in pairs], jnp.int32), jnp.asarray([p[1] for p in pairs], jnp.int32))


def _mla_fwd(q, k, v):
    t_rows = q.shape[0]
    t = _attn_tile(t_rows)
    hp = MLA_FWD_HEADS_PER_STEP
    w = hp * LANES
    ii, jj = _causal_pairs(t_rows // t, by_kv=False)

    def body(i_ref, j_ref, q_ref, k_ref, v_ref, o_ref, lse_ref, m_s, l_s, acc_s):
        i = i_ref[pl.program_id(1)]
        j = j_ref[pl.program_id(1)]

        @pl.when(j == 0)
        def _():
            m_s[...] = jnp.full(m_s.shape, NEG, F32)
            l_s[...] = jnp.zeros(l_s.shape, F32)
            acc_s[...] = jnp.zeros(acc_s.shape, F32)

        def step(diagonal):
            sls = [slice(hh * LANES, (hh + 1) * LANES) for hh in range(hp)]
            scores = [_dot_nt(k_ref[:, sl], q_ref[:, sl]) for sl in sls]
            if diagonal:
                valid = (lax.broadcasted_iota(jnp.int32, (t, t), 0) <= lax.broadcasted_iota(jnp.int32, (t, t), 1))
                scores = [jnp.where(valid, s, NEG) for s in scores]
            stats = []
            for hh, s in enumerate(scores):
                m_prev = m_s[hh]
                m_new = jnp.maximum(m_prev, jnp.max(s, axis=0, keepdims=True))
                p = jnp.exp(s - m_new)
                alpha = jnp.exp(m_prev - m_new)
                stats.append((m_new, alpha, alpha * l_s[hh] + jnp.sum(p, axis=0, keepdims=True), p.astype(BF16)))
            for hh, (m_new, alpha, l_new, p) in enumerate(stats):
                sl = sls[hh]
                acc = alpha * acc_s[hh] + _dot_tn(v_ref[:, sl], p)
                if diagonal:
                    o_ref[:, sl] = (acc / l_new).T.astype(o_ref.dtype)
                    lse_ref[hh] = m_new + jnp.log(l_new)
                else:
                    m_s[hh] = m_new
                    l_s[hh] = l_new
                    acc_s[hh] = acc

        pl.when(j < i)(lambda: step(False))
        pl.when(j == i)(lambda: step(True))

    grid_spec = pltpu.PrefetchScalarGridSpec(
        num_scalar_prefetch=2, grid=(HEADS // hp, ii.shape[0]),
        in_specs=[pl.BlockSpec((t, w), lambda hb, s, ir, jr: (ir[s], hb)),
                  pl.BlockSpec((t, w), lambda hb, s, ir, jr: (jr[s], hb)),
                  pl.BlockSpec((t, w), lambda hb, s, ir, jr: (jr[s], hb))],
        out_specs=[pl.BlockSpec((t, w), lambda hb, s, ir, jr: (ir[s], hb)),
                   pl.BlockSpec((hp, 1, t), lambda hb, s, ir, jr: (hb, 0, ir[s]))],
        scratch_shapes=[pltpu.VMEM((hp, 1, t), F32), pltpu.VMEM((hp, 1, t), F32), pltpu.VMEM((hp, LANES, t), F32)])
    return pl.pallas_call(
        body, name="mla_fwd", grid_spec=grid_spec,
        out_shape=[_sds((t_rows, HEADS * LANES), BF16), _sds((HEADS, 1, t_rows), F32)],
        compiler_params=pltpu.CompilerParams(dimension_semantics=("arbitrary",) * 2, vmem_limit_bytes=VMEM_LIMIT),
    )(ii, jj, q, k, v)


def _mla_bwd(q, k, v, do, lse, delta):
    t_rows = q.shape[0]
    t = _attn_tile(t_rows)
    hp = MLA_HEADS_PER_STEP
    w = hp * LANES
    ii, jj = _causal_pairs(t_rows // t, by_kv=True)

    def body(i_ref, j_ref, q_ref, k_ref, v_ref, do_ref, lse_ref, dl_ref, dq_ref, dk_ref, dv_ref):
        i = i_ref[pl.program_id(1)]
        j = j_ref[pl.program_id(1)]

        @pl.when(pl.program_id(1) == 0)
        def _():
            dq_ref[...] = jnp.zeros(dq_ref.shape, F32)

        def step(diagonal):
            r0 = pl.multiple_of(i * t, t)
            sls = [slice(hh * LANES, (hh + 1) * LANES) for hh in range(hp)]
            scores = [_dot_nt(k_ref[:, sl], q_ref[:, sl]) for sl in sls]
            if diagonal:
                valid = (lax.broadcasted_iota(jnp.int32, (t, t), 0) <= lax.broadcasted_iota(jnp.int32, (t, t), 1))
                scores = [jnp.where(valid, s, NEG) for s in scores]
            dps = [_dot_nt(v_ref[:, sl], do_ref[:, sl]) for sl in sls]
            ps = [jnp.exp(s - lse_ref[hh]) for hh, s in enumerate(scores)]
            dss = [(p * (dp - dl_ref[hh])).astype(BF16) for hh, (p, dp) in enumerate(zip(ps, dps))]
            for hh, sl in enumerate(sls):
                dv = _dot(ps[hh].astype(BF16), do_ref[:, sl])
                dk = _dot(dss[hh], q_ref[:, sl])
                if diagonal:
                    dv_ref[:, sl] = dv
                    dk_ref[:, sl] = dk
                else:
                    dv_ref[:, sl] += dv
                    dk_ref[:, sl] += dk
                dq_ref[hh, pl.ds(r0, t), :] += _dot_tn(dss[hh], k_ref[:, sl])

        pl.when(i > j)(lambda: step(False))
        pl.when(i == j)(lambda: step(True))

    def qmap(hb, s, ir, jr):
        return (ir[s], hb)

    def kvmap(hb, s, ir, jr):
        return (jr[s], hb)

    def rowmap(hb, s, ir, jr):
        return (hb, 0, ir[s])

    grid_spec = pltpu.PrefetchScalarGridSpec(
        num_scalar_prefetch=2, grid=(HEADS // hp, ii.shape[0]),
        in_specs=[pl.BlockSpec((t, w), qmap), pl.BlockSpec((t, w), kvmap), pl.BlockSpec((t, w), kvmap),
                  pl.BlockSpec((t, w), qmap), pl.BlockSpec((hp, 1, t), rowmap), pl.BlockSpec((hp, 1, t), rowmap)],
        out_specs=[pl.BlockSpec((hp, t_rows, LANES), lambda hb, s, ir, jr: (hb, 0, 0)),
                   pl.BlockSpec((t, w), kvmap), pl.BlockSpec((t, w), kvmap)])
    return pl.pallas_call(
        body, name="mla_bwd", grid_spec=grid_spec,
        out_shape=[_sds((HEADS, t_rows, LANES), F32), _sds((t_rows, HEADS * LANES), F32),
                   _sds((t_rows, HEADS * LANES), F32)],
        compiler_params=pltpu.CompilerParams(dimension_semantics=("arbitrary",) * 2, vmem_limit_bytes=VMEM_LIMIT),
    )(ii, jj, q, k, v, do, lse, delta)


SWA_TILE = 2 * SWA_WINDOW
SWA_GROUP = HEADS // A_KV_HEADS


def _swa_bias(tq):
    koff = lax.broadcasted_iota(jnp.int32, (tq + SWA_WINDOW, SWA_GROUP * tq), 0) - SWA_WINDOW
    qoff = (lax.broadcasted_iota(jnp.int32, (tq + SWA_WINDOW, SWA_GROUP * tq), 1) % tq)
    band = (koff <= qoff) & (qoff - koff < SWA_WINDOW)
    return jnp.stack([jnp.where(band & (koff >= 0), 0.0, NEG), jnp.where(band, 0.0, NEG)]).astype(F32)


def _swa_specs(tq, nq):
    wb = tq // SWA_WINDOW
    kvw = A_KV_HEADS * LANES

    def qi(i):
        return jnp.minimum(i, nq - 1)

    q = pl.BlockSpec((tq, HEADS * LANES), lambda i: (qi(i), 0))
    cur = pl.BlockSpec((tq, kvw), lambda i: (qi(i), 0))
    prev = pl.BlockSpec((SWA_WINDOW, kvw), lambda i: (jnp.maximum(qi(i) * wb - 1, 0), 0))
    bias = pl.BlockSpec((1, tq + SWA_WINDOW, SWA_GROUP * tq), lambda i: (jnp.minimum(i, 1), 0, 0))
    rows = pl.BlockSpec((A_KV_HEADS, 1, 1, SWA_GROUP * tq), lambda i: (0, qi(i), 0, 0))
    sink = pl.BlockSpec((A_KV_HEADS, 1, SWA_GROUP * tq), lambda i: (0, 0, 0))
    return q, cur, prev, bias, rows, sink


def _stack_heads(ref, kvh):
    base = kvh * SWA_GROUP
    return jnp.concatenate([ref[:, (base + g) * LANES:(base + g + 1) * LANES] for g in range(SWA_GROUP)], axis=0)


def _unstack_heads(ref, kvh, val, tq):
    base = kvh * SWA_GROUP
    for g in range(SWA_GROUP):
        ref[:, (base + g) * LANES:(base + g + 1) * LANES] = val[g * tq:(g + 1) * tq].astype(ref.dtype)


def _kv_window(prev_ref, cur_ref, kvh):
    sl = slice(kvh * LANES, (kvh + 1) * LANES)
    return jnp.concatenate([prev_ref[:, sl], cur_ref[:, sl]], axis=0)


def _swa_fwd(q, k, v, bias, sink_rows):
    t_rows = q.shape[0]
    tq = min(SWA_TILE, t_rows)
    nq = t_rows // tq
    qs_, cur, prev, bs, rows, sk = _swa_specs(tq, nq)
    kvhs = range(A_KV_HEADS)

    def body(q_ref, kc_ref, kp_ref, vc_ref, vp_ref, b_ref, sink_ref, o_ref, lse_ref):
        scores = [_dot_nt(_kv_window(kp_ref, kc_ref, h), _stack_heads(q_ref, h)) + b_ref[0] for h in kvhs]
        stats = []
        for h, s in zip(kvhs, scores):
            sink = sink_ref[h]
            m = jnp.maximum(jnp.max(s, axis=0, keepdims=True), sink)
            p = jnp.exp(s - m)
            l = jnp.sum(p, axis=0, keepdims=True) + jnp.exp(sink - m)
            lse_ref[h, 0] = m + jnp.log(l)
            stats.append((p.astype(BF16), l))
        for h, (p, l) in zip(kvhs, stats):
            _unstack_heads(o_ref, h, (_dot_tn(_kv_window(vp_ref, vc_ref, h), p) / l).T, tq)

    return pl.pallas_call(
        body, name="swa_fwd", grid=(nq,),
        in_specs=[qs_, cur, prev, cur, prev, bs, sk],
        out_specs=[qs_, rows],
        out_shape=[_sds((t_rows, HEADS * LANES), BF16), _sds((A_KV_HEADS, nq, 1, SWA_GROUP * tq), F32)],
        compiler_params=pltpu.CompilerParams(dimension_semantics=("arbitrary",), vmem_limit_bytes=VMEM_LIMIT),
    )(q, k, k, v, v, bias, sink_rows)


def _swa_bwd(q, k, v, o, do, lse, bias, sink_rows):
    t_rows = q.shape[0]
    tq = min(SWA_TILE, t_rows)
    nq = t_rows // tq
    qs_, cur, prev, bs, rows, sk = _swa_specs(tq, nq)
    hw = SWA_WINDOW
    kvhs = range(A_KV_HEADS)
    kvw = A_KV_HEADS * LANES

    def body(q_ref, kc_ref, kp_ref, vc_ref, vp_ref, o_ref, do_ref, lse_ref, b_ref, sink_ref,
             dq_ref, dk_ref, dv_ref, dsink_ref, ck, cv, dsa):
        i = pl.program_id(0)

        @pl.when(i == 0)
        def _():
            dsa[...] = jnp.zeros(dsa.shape, F32)

        @pl.when(i < nq)
        def _():
            qs = [_stack_heads(q_ref, h) for h in kvhs]
            dos = [_stack_heads(do_ref, h) for h in kvhs]
            kks = [_kv_window(kp_ref, kc_ref, h) for h in kvhs]
            scores = [_dot_nt(kks[h], qs[h]) for h in kvhs]
            dps = [_dot_nt(_kv_window(vp_ref, vc_ref, h), dos[h]) for h in kvhs]
            ps, dss = [], []
            for h in kvhs:
                lse = lse_ref[h, 0]
                p = jnp.exp(scores[h] + b_ref[0] - lse)
                delta = jnp.sum((_stack_heads(o_ref, h).astype(F32) * dos[h].astype(F32)).T, axis=0, keepdims=True)
                dsa[h] += -jnp.exp(sink_ref[h] - lse) * delta
                ps.append(p.astype(BF16))
                dss.append((p * (dps[h] - delta)).astype(BF16))
            for h in kvhs:
                sl = slice(h * LANES, (h + 1) * LANES)
                dv = _dot(ps[h], dos[h])
                dk = _dot(dss[h], qs[h])
                _unstack_heads(dq_ref, h, _dot_tn(dss[h], kks[h]), tq)

                @pl.when(i > 0)
                def _():
                    dk_ref[0:tq - hw, sl] = ck[0:tq - hw, sl]
                    dk_ref[tq - hw:tq, sl] = ck[tq - hw:tq, sl] + dk[0:hw]
                    dv_ref[0:tq - hw, sl] = cv[0:tq - hw, sl]
                    dv_ref[tq - hw:tq, sl] = cv[tq - hw:tq, sl] + dv[0:hw]

                ck[:, sl] = dk[hw:hw + tq]
                cv[:, sl] = dv[hw:hw + tq]

        @pl.when(i == nq)
        def _():
            dk_ref[...] = ck[...]
            dv_ref[...] = cv[...]
            dsink_ref[...] = jnp.zeros(dsink_ref.shape, F32)
            for h in kvhs:
                for g in range(SWA_GROUP):
                    tot = jnp.sum(dsa[h, :, g * tq:(g + 1) * tq], axis=1, keepdims=True)
                    dsink_ref[h, g:g + 1, :] = jnp.zeros((1, LANES), F32) + tot

    kv_out = pl.BlockSpec((tq, kvw), lambda i: (jnp.maximum(i - 1, 0), 0))
    return pl.pallas_call(
        body, name="swa_bwd", grid=(nq + 1,),
        in_specs=[qs_, cur, prev, cur, prev, qs_, qs_, rows, bs, sk],
        out_specs=[qs_, kv_out, kv_out, pl.BlockSpec((A_KV_HEADS, 8, LANES), lambda i: (0, 0, 0))],
        out_shape=[_sds((t_rows, HEADS * LANES), F32), _sds((t_rows, kvw), F32), _sds((t_rows, kvw), F32),
                   _sds((A_KV_HEADS, 8, LANES), F32)],
        scratch_shapes=[pltpu.VMEM((tq, kvw), F32), pltpu.VMEM((tq, kvw), F32),
                        pltpu.VMEM((A_KV_HEADS, 1, SWA_GROUP * tq), F32)],
        compiler_params=pltpu.CompilerParams(dimension_semantics=("arbitrary",), vmem_limit_bytes=VMEM_LIMIT),
    )(q, k, k, v, v, o, do, lse, bias, sink_rows)


def _fwd_mix(x, ya, yb, gate, wba, wbb, wout, g2, g3, tm):
    t_rows = x.shape[0]

    def body(x_ref, ya_ref, yb_ref, gate_ref, wba_ref, wbb_ref, wout_ref, g2_ref, g3_ref,
             pa_ref, pb_ref, mixed_ref, o_ref, x1_ref, h2_ref):
        pa = _dot(ya_ref[...], wba_ref[...])
        pb = _dot(yb_ref[...], wbb_ref[...])
        pa_ref[...] = pa
        pb_ref[...] = pb
        mixed = (gate_ref[:, 0:D_MODEL] * pa + gate_ref[:, D_MODEL:2 * D_MODEL] * pb).astype(BF16)
        mixed_ref[...] = mixed
        o = _dot(mixed, wout_ref[...])
        o_ref[...] = o
        on, _ = _rms_stats(o)
        x1 = x_ref[...] + on * g2_ref[...]
        x1_ref[...] = x1
        x1n, _ = _rms_stats(x1)
        h2_ref[...] = (x1n * g3_ref[...]).astype(BF16)

    def o_(dt):
        return (_sds((t_rows, D_MODEL), dt), _row(tm, D_MODEL))

    ins = [(x, _row(tm, D_MODEL)), (ya, _row(tm, 1024)), (yb, _row(tm, 1024)), (gate, _row(tm, 2048)),
           (wba, _resident(wba.shape)), (wbb, _resident(wbb.shape)), (wout, _resident(wout.shape)),
           (g2, _full(g2.shape)), (g3, _full(g3.shape))]
    return _rows_call("fwd_mix", body, t_rows, tm, ins, [o_(F32), o_(F32), o_(BF16), o_(F32), o_(F32), o_(BF16)])


CONV_CHUNK = 1408


def _fwd_up(h2, wup, convw8, convb, tm):
    t_rows = h2.shape[0]
    cdim = 2 * D_FF

    def body(h2_ref, wup_ref, cw_ref, cb_ref, up_ref, a_ref, carry):
        i = pl.program_id(0)

        @pl.when(i == 0)
        def _():
            carry[...] = jnp.zeros(carry.shape, F32)

        hb = h2_ref[...]

        def conv(c0):
            sl = slice(c0, c0 + CONV_CHUNK)
            up = _dot(hb, wup_ref[c0 // CONV_CHUNK])
            up_ref[:, sl] = up
            xm1, xm2 = _conv_taps(up, carry[6:7, sl], carry[7:8, sl])
            u = cw_ref[0:1, sl] * xm2 + cw_ref[1:2, sl] * xm1 + cw_ref[2:3, sl] * up + cb_ref[:, sl]
            carry[:, sl] = up[tm - 8:tm, :]
            return u

        for c0 in range(0, D_FF, CONV_CHUNK):
            ug = conv(c0)
            uv = conv(D_FF + c0)
            gel, _ = _gelu_and_grad(ug)
            a_ref[:, c0:c0 + CONV_CHUNK] = (gel * uv).astype(BF16)

    ins = [(h2, _row(tm, D_MODEL)), (wup, _resident(wup.shape)), (convw8, _full(convw8.shape)), (convb, _full(convb.shape))]
    outs = [(_sds((t_rows, cdim), F32), _row(tm, cdim)), (_sds((t_rows, D_FF), BF16), _row(tm, D_FF))]
    return _rows_call("fwd_up", body, t_rows, tm, ins, outs, scratch=[pltpu.VMEM((8, cdim), F32)])


def _fwd_out(a, wdown, x1, g4, p, wple, g5, wpg, tgt, tm):
    t_rows = a.shape[0]

    def body(a_ref, wdown_ref, x1_ref, g4_ref, p_ref, wple_ref, g5_ref, wpg_ref, tgt_ref,
             ff_ref, x2_ref, e_ref, n5_ref, sg_ref, dx3_ref, loss_ref):
        i = pl.program_id(0)
        ff = _dot(a_ref[...], wdown_ref[...])
        ff_ref[...] = ff
        ffn, _ = _rms_stats(ff)
        x2 = x1_ref[...] + ffn * g4_ref[...]
        x2_ref[...] = x2
        e = _dot(p_ref[...].astype(BF16), wple_ref[...])
        e_ref[...] = e
        x2n, _ = _rms_stats(x2)
        n5 = (x2n * g5_ref[...]).astype(BF16)
        n5_ref[...] = n5
        sg = _sigmoid(_dot(n5, wpg_ref[...]))
        sg_ref[...] = sg
        d = x2 + sg * e - tgt_ref[...]
        dx3_ref[...] = d * (1.0 / D_MODEL)

        @pl.when(i == 0)
        def _():
            loss_ref[...] = jnp.zeros((1, 1), F32)

        loss_ref[...] += 0.5 * jnp.sum(jnp.sum(d * d, axis=1, keepdims=True), axis=0, keepdims=True) * (1.0 / D_MODEL)

    def o_(dt):
        return (_sds((t_rows, D_MODEL), dt), _row(tm, D_MODEL))

    ins = [(a, _row(tm, D_FF)), (wdown, _resident(wdown.shape)), (x1, _row(tm, D_MODEL)), (g4, _full(g4.shape)),
           (p, _row(tm, PLE_DIM)), (wple, _full(wple.shape)), (g5, _full(g5.shape)), (wpg, _resident(wpg.shape)),
           (tgt, _row(tm, D_MODEL))]
    outs = [o_(F32), o_(F32), o_(F32), o_(BF16), o_(F32), o_(F32), (_sds((1, 1), F32), _full((1, 1)))]
    return _rows_call("fwd_out", body, t_rows, tm, ins, outs)


def _bwd_out(dx3, e, sg, x2, ff, g5, g4, wpg, wdown, up, convw8, convb, tm):
    t_rows = dx3.shape[0]
    cdim = 2 * D_FF
    hb = tm // 8

    def body(dx3_ref, e_ref, sg_ref, x2_ref, ff_ref, g5_ref, g4_ref, wpg_ref, wdown_ref, up_ref, halo_ref, cw_ref,
             cb_ref, dpre_ref, de_ref, dx2_ref, dff_ref, du_ref, dg5_ref, dg4_ref, dcb_ref, dcw_ref):
        i = pl.program_id(0)

        @pl.when(i == 0)
        def _():
            dg5_ref[...] = jnp.zeros(dg5_ref.shape, F32)
            dg4_ref[...] = jnp.zeros(dg4_ref.shape, F32)
            dcb_ref[...] = jnp.zeros(dcb_ref.shape, F32)
            dcw_ref[...] = jnp.zeros(dcw_ref.shape, F32)

        dx3 = dx3_ref[...]
        sg = sg_ref[...]
        dpre = (dx3 * e_ref[...] * sg * (1.0 - sg)).astype(BF16)
        dpre_ref[...] = dpre
        de_ref[...] = (dx3 * sg).astype(BF16)
        dn5 = _dot_nt(dpre, wpg_ref[...])
        x2n, r5 = _rms_stats(x2_ref[...])
        d2, dg5 = _rms_bwd(dn5, x2n, r5, g5_ref[...])
        dx2 = dx3 + d2
        dx2_ref[...] = dx2
        dg5_ref[...] += dg5
        ffn, r4 = _rms_stats(ff_ref[...])
        dff, dg4 = _rms_bwd(dx2, ffn, r4, g4_ref[...])
        dg4_ref[...] += dg4
        dffb = dff.astype(BF16)
        dff_ref[...] = dffb
        keep = jnp.where(i > 0, 1.0, 0.0)

        def conv(c0):
            sl = slice(c0, c0 + CONV_CHUNK)
            up = up_ref[:, sl]
            xm1, xm2 = _conv_taps(up, halo_ref[6:7, sl] * keep, halo_ref[7:8, sl] * keep)
            u = cw_ref[0:1, sl] * xm2 + cw_ref[1:2, sl] * xm1 + cw_ref[2:3, sl] * up + cb_ref[:, sl]
            return u, up, xm1, xm2

        def grads(c0, du, up, xm1, xm2):
            sl = slice(c0, c0 + CONV_CHUNK)
            du_ref[:, sl] = du.astype(BF16)
            dcb_ref[:, sl] += jnp.sum(du, axis=0, keepdims=True)
            dcw_ref[0:1, sl] += jnp.sum(du * xm2, axis=0, keepdims=True)
            dcw_ref[1:2, sl] += jnp.sum(du * xm1, axis=0, keepdims=True)
            dcw_ref[2:3, sl] += jnp.sum(du * up, axis=0, keepdims=True)

        for c0 in range(0, D_FF, CONV_CHUNK):
            da = _dot_nt(dffb, wdown_ref[c0:c0 + CONV_CHUNK, :])
            ug, *rg = conv(c0)
            uv, *rv = conv(D_FF + c0)
            gel, dgel = _gelu_and_grad(ug)
            grads(c0, da * uv * dgel, *rg)
            grads(D_FF + c0, da * gel, *rv)

    def o_(n, dt):
        return (_sds((t_rows, n), dt), _row(tm, n))

    def acc(r, n):
        return (_sds((r, n), F32), _full((r, n)))

    halo = pl.BlockSpec((8, cdim), lambda i: (jnp.maximum(i * hb - 1, 0), 0))
    ins = [(dx3, _row(tm, D_MODEL)), (e, _row(tm, D_MODEL)), (sg, _row(tm, D_MODEL)), (x2, _row(tm, D_MODEL)),
           (ff, _row(tm, D_MODEL)), (g5, _full(g5.shape)), (g4, _full(g4.shape)), (wpg, _resident(wpg.shape)),
           (wdown, _resident(wdown.shape)), (up, _row(tm, cdim)), (up, halo), (convw8, _full(convw8.shape)),
           (convb, _full(convb.shape))]
    outs = [o_(D_MODEL, BF16), o_(D_MODEL, BF16), o_(D_MODEL, F32), o_(D_MODEL, BF16), o_(cdim, BF16),
            acc(1, D_MODEL), acc(1, D_MODEL), acc(1, cdim), acc(8, cdim)]
    return _rows_call("bwd_out", body, t_rows, tm, ins, outs)


def _bwd_mid(du, convw8, wup, dx2, x1, g3, o, g2, wout, gate, pa, pb, wba, wbb, yb, tm):
    t_rows = du.shape[0]
    cdim = 2 * D_FF
    halo_rows = 16
    hb = tm // halo_rows
    last_blk = t_rows // halo_rows - 1
    n_tiles = t_rows // tm

    def body(du_ref, halo_ref, cw_ref, wup_ref, dx2_ref, x1_ref, g3_ref, o_ref, g2_ref, wout_ref, gate_ref, pa_ref,
             pb_ref, wba_ref, wbb_ref, yb_ref,
             dup_ref, dx1_ref, do_ref, dpa_ref, dpb_ref, dgt_ref, dya_ref, dyb_ref, dl_ref, dg3_ref, dg2_ref, dbg_ref):
        i = pl.program_id(0)

        @pl.when(i == 0)
        def _():
            dg3_ref[...] = jnp.zeros(dg3_ref.shape, F32)
            dg2_ref[...] = jnp.zeros(dg2_ref.shape, F32)
            dbg_ref[...] = jnp.zeros(dbg_ref.shape, F32)

        keep = jnp.where(i < n_tiles - 1, 1.0, 0.0)
        dh2 = jnp.zeros((tm, D_MODEL), F32)
        for c0 in range(0, cdim, CONV_CHUNK):
            sl = slice(c0, c0 + CONV_CHUNK)
            du = du_ref[:, sl].astype(F32)
            nxt = halo_ref[:, sl].astype(F32)
            xp1, xp2 = _conv_taps_next(du, nxt[0:1] * keep, nxt[1:2] * keep)
            dup = (cw_ref[2:3, sl] * du + cw_ref[1:2, sl] * xp1 + cw_ref[0:1, sl] * xp2).astype(BF16)
            dup_ref[:, sl] = dup
            dh2 = dh2 + _dot_nt(dup, wup_ref[c0 // CONV_CHUNK])
        x1n, r3 = _rms_stats(x1_ref[...])
        d1, dg3 = _rms_bwd(dh2, x1n, r3, g3_ref[...])
        dx1 = dx2_ref[...] + d1
        dx1_ref[...] = dx1
        dg3_ref[...] += dg3
        on, r2 = _rms_stats(o_ref[...])
        do, dg2 = _rms_bwd(dx1, on, r2, g2_ref[...])
        dg2_ref[...] += dg2
        dob = do.astype(BF16)
        do_ref[...] = dob
        dmixed = _dot_nt(dob, wout_ref[...])
        ga = gate_ref[:, 0:D_MODEL]
        gb = gate_ref[:, D_MODEL:2 * D_MODEL]
        dpa = (dmixed * ga).astype(BF16)
        dpb = (dmixed * gb).astype(BF16)
        dpa_ref[...] = dpa
        dpb_ref[...] = dpb
        dga = dmixed * pa_ref[...] * ga * (1.0 - ga)
        dgb = dmixed * pb_ref[...] * gb * (1.0 - gb)
        dgt_ref[:, 0:D_MODEL] = dga.astype(BF16)
        dgt_ref[:, D_MODEL:2 * D_MODEL] = dgb.astype(BF16)
        dbg_ref[:, 0:D_MODEL] += jnp.sum(dga, axis=0, keepdims=True)
        dbg_ref[:, D_MODEL:2 * D_MODEL] += jnp.sum(dgb, axis=0, keepdims=True)
        dya_ref[...] = _dot_nt(dpa, wba_ref[...]).astype(BF16)
        dyb = _dot_nt(dpb, wbb_ref[...]).astype(BF16)
        dyb_ref[...] = dyb
        prod = yb_ref[...].astype(F32) * dyb.astype(F32)
        lane_head = lax.broadcasted_iota(jnp.int32, (HEADS, HEADS * LANES), 1) // LANES
        sel = (lane_head == lax.broadcasted_iota(jnp.int32, (HEADS, HEADS * LANES), 0)).astype(BF16)
        hi = prod.astype(BF16)
        lo = (prod - hi.astype(F32)).astype(BF16)
        dl_ref[...] = _dot_nt(sel, hi) + _dot_nt(sel, lo)

    def o_(n, dt):
        return (_sds((t_rows, n), dt), _row(tm, n))

    def acc(r, n):
        return (_sds((r, n), F32), _full((r, n)))

    halo = pl.BlockSpec((halo_rows, cdim), lambda i: (jnp.minimum((i + 1) * hb, last_blk), 0))
    ins = [(du, _row(tm, cdim)), (du, halo), (convw8, _full(convw8.shape)), (wup, _resident(wup.shape)),
           (dx2, _row(tm, D_MODEL)), (x1, _row(tm, D_MODEL)), (g3, _full(g3.shape)), (o, _row(tm, D_MODEL)),
           (g2, _full(g2.shape)), (wout, _resident(wout.shape)), (gate, _row(tm, 2048)), (pa, _row(tm, D_MODEL)),
           (pb, _row(tm, D_MODEL)), (wba, _resident(wba.shape)), (wbb, _resident(wbb.shape)), (yb, _row(tm, 1024))]
    outs = [o_(cdim, BF16), o_(D_MODEL, F32), o_(D_MODEL, BF16), o_(D_MODEL, BF16), o_(D_MODEL, BF16),
            o_(2048, BF16), o_(1024, BF16), o_(1024, BF16),
            (_sds((HEADS, t_rows), F32), pl.BlockSpec((HEADS, tm), lambda i: (0, i))),
            acc(1, D_MODEL), acc(1, D_MODEL), acc(1, 2048)]
    return _rows_call("bwd_mid", body, t_rows, tm, ins, outs)


def _bwd_in(dqs, dks, dvs, dqm, dkm, dvm, tabs, consts, cq, ckv, gq, gkv, wuq, wk, wv, dgates, win, x, g1, dx1, tm):
    t_rows = x.shape[0]

    def body(dqs_ref, dks_ref, dvs_ref, dqm_ref, dkm_ref, dvm_ref, ca, sa1, sa2, cb, sb1, sb2, c_ref, cq_ref,
             ckv_ref, gq_ref, gkv_ref, wuq_ref, wk_ref, wv_ref, dgt_ref, win_ref, x_ref, g1_ref, dx1_ref,
             dz_ref, dqb_ref, dx_ref, dgq_ref, dgkv_ref, dg1_ref):
        i = pl.program_id(0)

        @pl.when(i == 0)
        def _():
            dgq_ref[...] = jnp.zeros(dgq_ref.shape, F32)
            dgkv_ref[...] = jnp.zeros(dgkv_ref.shape, F32)
            dg1_ref[...] = jnp.zeros(dg1_ref.shape, F32)

        ta = (ca[...], sa1[...], sa2[...])
        tb = (cb[...], sb1[...], sb2[...])
        dz_ref[:, Z_QA:Z_KA] = _rope_t(dqs_ref[...] * SCALE_A, *ta, A_HEAD_DIM // 2).astype(BF16)
        dz_ref[:, Z_KA:Z_VA] = _rope_t(dks_ref[...], *ta, A_HEAD_DIM // 2).astype(BF16)
        dz_ref[:, Z_VA:Z_CQ] = dvs_ref[...].astype(BF16)
        dqm = jnp.concatenate([dqm_ref[h] for h in range(HEADS)], axis=1)
        dqb = _rope_t(dqm * SCALE_B, *tb, ROPE_DIM // 2).astype(BF16)
        dqb_ref[...] = dqb
        dcqn = _dot_nt(dqb, wuq_ref[...])
        cqn, rq = _rms_stats(cq_ref[...])
        dcq, dgq = _rms_bwd(dcqn, cqn, rq, gq_ref[...])
        dgq_ref[...] += dgq
        dz_ref[:, Z_CQ:Z_CKV] = dcq.astype(BF16)
        dkm = dkm_ref[...]
        dslot = dkm[:, 0:LANES]
        for h in range(1, HEADS):
            dslot = dslot + dkm[:, h * LANES:(h + 1) * LANES]
        dz_ref[:, Z_KR:Z_GATE] = _rope_t(dslot * c_ref[10:11, :], *tb, ROPE_DIM // 2).astype(BF16)
        dckvn = _dot_nt(dkm.astype(BF16), wk_ref[...]) + _dot_nt(dvm_ref[...].astype(BF16), wv_ref[...])
        ckvn, rkv = _rms_stats(ckv_ref[...])
        dckv, dgkv = _rms_bwd(dckvn, ckvn, rkv, gkv_ref[...])
        dgkv_ref[...] += dgkv
        dz_ref[:, Z_CKV:Z_KR] = dckv.astype(BF16)
        dz_ref[:, Z_GATE:ZW] = dgt_ref[...]
        dh1 = _dot_nt(dz_ref[...], win_ref[...])
        xn, r1 = _rms_stats(x_ref[...])
        d0, dg1 = _rms_bwd(dh1, xn, r1, g1_ref[...])
        dg1_ref[...] += dg1
        dx_ref[...] = dx1_ref[...] + d0

    def acc(n):
        return (_sds((1, n), F32), _full((1, n)))

    ins = [(dqs, _row(tm, 1024)), (dks, _row(tm, 256)), (dvs, _row(tm, 256)), (dqm, _heads(tm, HEADS)),
           (dkm, _row(tm, 1024)), (dvm, _row(tm, 1024))] + [(t, _row(tm, LANES)) for t in tabs] + [
           (consts, _full(consts.shape)), (cq, _row(tm, 256)), (ckv, _row(tm, 128)), (gq, _full(gq.shape)),
           (gkv, _full(gkv.shape)), (wuq, _full(wuq.shape)), (wk, _full(wk.shape)), (wv, _full(wv.shape)),
           (dgates, _row(tm, 2048)), (win, _resident(win.shape)), (x, _row(tm, D_MODEL)), (g1, _full(g1.shape)),
           (dx1, _row(tm, D_MODEL))]
    outs = [(_sds((t_rows, ZW), BF16), _row(tm, ZW)), (_sds((t_rows, 1024), BF16), _row(tm, 1024)),
            (_sds((t_rows, D_MODEL), F32), _row(tm, D_MODEL)), acc(256), acc(128), acc(D_MODEL)]
    return _rows_call("bwd_in", body, t_rows, tm, ins, outs)


def _pick_cols(n):
    best = LANES
    for d in range(LANES, min(n, 1408) + 1, LANES):
        if n % d == 0:
            best = d
    return best


def _mm_tn(name, a, b, column_shards=1, after=None):
    t_rows, m = a.shape
    n = b.shape[1]
    bk = min(1024, t_rows)
    bm, bn = _pick_cols(m), _pick_cols(n // column_shards)
    per_shard = n // column_shards // bn
    extra = () if after is None else (after,)

    def body(a_ref, b_ref, *rest):
        o_ref = rest[-1]

        @pl.when(pl.program_id(2) == 0)
        def _():
            o_ref[...] = jnp.zeros((bm, bn), F32)

        o_ref[...] += _dot_tn(a_ref[...].astype(BF16), b_ref[...].astype(BF16))

    return pl.pallas_call(
        body, name=name, grid=(m // bm, n // bn, t_rows // bk),
        in_specs=[pl.BlockSpec((bk, bm), lambda i, j, k: (k, i)), pl.BlockSpec((bk, bn), lambda i, j, k: (k, j))]
        + [pl.BlockSpec((8, LANES), lambda i, j, k: (0, 0))] * len(extra),
        out_specs=(pl.BlockSpec((bm, bn), lambda i, j, k: (i, j)) if column_shards == 1 else
                   pl.BlockSpec((None, bm, bn), lambda i, j, k: (j // per_shard, i, j % per_shard))),
        out_shape=_sds((m, n) if column_shards == 1 else (column_shards, m, n // column_shards), F32),
        compiler_params=pltpu.CompilerParams(dimension_semantics=("arbitrary",) * 3, vmem_limit_bytes=VMEM_LIMIT),
    )(a, b, *extra)


PACK_ROWS = 512


ADD_TILE_ELEMS = 1 << 17


def _add_rows(rows, cols):
    best = 16
    for d in range(16, rows + 1, 16):
        if rows % d == 0 and d * cols <= ADD_TILE_ELEMS:
            best = d
    assert rows % best == 0
    return best


def _add_pair(name, g, recv, half):
    _, _, rows, cols = g.shape
    t = _add_rows(rows, cols)

    def body(h_ref, g_ref, r_ref, o_ref):
        o_ref[...] = (g_ref[:, 0] + r_ref[...]).astype(BF16)

    spec = pl.BlockSpec((4, t, cols), lambda i, h: (0, i, 0))
    grid_spec = pltpu.PrefetchScalarGridSpec(
        num_scalar_prefetch=1, grid=(rows // t,),
        in_specs=[pl.BlockSpec((4, 1, t, cols), lambda i, h: (0, h[0], i, 0)), spec], out_specs=spec)
    return pl.pallas_call(body, name=name, grid_spec=grid_spec,
                          out_shape=_sds(recv.shape, BF16))(jnp.reshape(half, (1,)).astype(jnp.int32), g, recv)


def _add_chips(name, parts):
    _, rows, cols = parts.shape
    t = _add_rows(rows, cols)

    def body(p_ref, o_ref):
        acc = p_ref[0].astype(F32)
        for j in range(1, 4):
            acc = acc + p_ref[j].astype(F32)
        o_ref[...] = acc

    return pl.pallas_call(body, name=name, grid=(rows // t,),
                          in_specs=[pl.BlockSpec((4, t, cols), lambda i: (0, i, 0))],
                          out_specs=pl.BlockSpec((t, cols), lambda i: (i, 0)),
                          out_shape=_sds((rows, cols), F32))(parts)


def _add_devices(parts):
    n, rows, _ = parts.shape

    def body(p_ref, o_ref):
        acc = p_ref[0]
        for j in range(1, n):
            acc = acc + p_ref[j]
        o_ref[...] = acc

    return pl.pallas_call(body, name="small_add", grid=(1,),
                          in_specs=[pl.BlockSpec((n, rows, LANES), lambda i: (0, 0, 0))],
                          out_specs=pl.BlockSpec((rows, LANES), lambda i: (0, 0)),
                          out_shape=_sds((rows, LANES), F32))(parts)


def _adam_rows(k, n):
    target = max(8, (1 << 20) // (4 * n))
    if k <= target:
        return k
    best = None
    for d in range(8, target + 1, 8):
        if k % d == 0:
            best = d
    return best if best is not None else k


def _adam_update(w, g, m, v):
    m_ = ADAM_B1 * m + (1.0 - ADAM_B1) * g
    v_ = ADAM_B2 * v + (1.0 - ADAM_B2) * (g * g)
    delta = -ADAM_LR * ((m_ / (1.0 - ADAM_B1 ** ADAM_STEP)) / (jnp.sqrt(v_ / (1.0 - ADAM_B2 ** ADAM_STEP)) + ADAM_EPS)
                        + ADAM_WD * w)
    return delta, m_, v_


def _adamw(name, w, g, m, v):
    k, n = w.shape
    bk = _adam_rows(k, n)

    def body(w_ref, g_ref, m_ref, v_ref, d_ref, mo_ref, vo_ref):
        d_ref[...], mo_ref[...], vo_ref[...] = _adam_update(w_ref[...], g_ref[...], m_ref[...], v_ref[...])

    spec = pl.BlockSpec((bk, n), lambda i: (i, 0))
    out = pl.pallas_call(body, name=name, grid=(k // bk,), in_specs=[spec] * 4, out_specs=[spec] * 3,
                         out_shape=[_sds((k, n), F32)] * 3,
                         compiler_params=pltpu.CompilerParams(vmem_limit_bytes=VMEM_LIMIT))(w, g, m, v)
    return (g, *out)


def _adamw_halves(name, w, mine, theirs, m, v, half):
    k, n = w.shape
    bk = _adam_rows(k // 2, n)
    nb = k // 2 // bk

    def body(h_ref, w_ref, mine_ref, theirs_ref, m_ref, v_ref, g_ref, d_ref, mo_ref, vo_ref):
        g = jnp.where(pl.program_id(0) == h_ref[0], mine_ref[...], theirs_ref[...])
        g_ref[...] = g
        d_ref[...], mo_ref[...], vo_ref[...] = _adam_update(w_ref[...], g, m_ref[...], v_ref[...])

    full = pl.BlockSpec((bk, n), lambda h, i, c: (h * nb + i, 0))
    part = pl.BlockSpec((bk, n), lambda h, i, c: (i, 0))
    grid_spec = pltpu.PrefetchScalarGridSpec(num_scalar_prefetch=1, grid=(2, nb),
                                             in_specs=[full, part, part, full, full], out_specs=[full] * 4)
    return tuple(pl.pallas_call(
        body, name=name, grid_spec=grid_spec, out_shape=[_sds((k, n), F32)] * 4,
        compiler_params=pltpu.CompilerParams(vmem_limit_bytes=VMEM_LIMIT),
    )(jnp.reshape(half, (1,)).astype(jnp.int32), w, mine, theirs, m, v))


_HBM = pl.BlockSpec(memory_space=pltpu.HBM)


def _me():
    return lax.axis_index("x"), lax.axis_index("y"), lax.axis_index("c")


def _other_chips(x, y):
    return [(1 - x, y), (x, 1 - y), (1 - x, 1 - y)]


def _pass_to_sibling(zones):
    n = len(zones)

    def body(*refs):
        in_refs, out_refs = refs[:n], refs[n:2 * n]
        send_sems, recv_sems = refs[2 * n:]
        x, y, c = _me()
        sent = []
        for a, (in_ref, out_ref) in enumerate(zip(in_refs, out_refs)):
            for j, (cx, cy) in enumerate(_other_chips(x, y)):
                mine, theirs = (2 * cx + cy, c), (2 * cx + cy, 1 - c)
                sems = dict(send_sem=send_sems.at[3 * a + j], recv_sem=recv_sems.at[3 * a + j],
                            device_id=(x, y, 1 - c), device_id_type=MESH)
                sent.append((pltpu.make_async_remote_copy(src_ref=in_ref.at[mine], dst_ref=out_ref.at[mine], **sems),
                             pltpu.make_async_remote_copy(src_ref=in_ref.at[theirs], dst_ref=out_ref.at[theirs], **sems)))
        for send, _ in sent:
            send.start()
        for _, recv in sent:
            recv.wait_recv()
        for send, _ in sent:
            send.wait_send()

    return pl.pallas_call(
        body, name="pass_to_sibling", out_shape=[_sds(z.shape, z.dtype) for z in zones],
        in_specs=[_HBM] * n, out_specs=[_HBM] * n, input_output_aliases={i: i for i in range(n)},
        scratch_shapes=[pltpu.SemaphoreType.DMA((3 * n,)), pltpu.SemaphoreType.DMA((3 * n,))],
    )(*zones)


def _swap_sibling(name, vs, other_half=False):
    n = len(vs)

    def body(*refs):
        v_refs, out_refs = refs[:n], refs[n:2 * n]
        send_sems, recv_sems = refs[2 * n:]
        x, y, c = _me()
        cps = [pltpu.make_async_remote_copy(src_ref=v_ref.at[:, 1 - c] if other_half else v_ref, dst_ref=out_ref,
                                            send_sem=send_sems.at[a], recv_sem=recv_sems.at[a],
                                            device_id=(x, y, 1 - c), device_id_type=MESH)
               for a, (v_ref, out_ref) in enumerate(zip(v_refs, out_refs))]
        for cp in cps:
            cp.start()
        for cp in cps:
            cp.wait()

    def landing(v):
        return _sds((v.shape[0],) + v.shape[2:] if other_half else v.shape, v.dtype)

    return pl.pallas_call(
        body, name=name, out_shape=[landing(v) for v in vs], in_specs=[_HBM] * n, out_specs=[_HBM] * n,
        scratch_shapes=[pltpu.SemaphoreType.DMA((n,)), pltpu.SemaphoreType.DMA((n,))],
    )(*vs)


_SEM = pl.BlockSpec(memory_space=pltpu.SEMAPHORE)
_EFFECT = pltpu.SideEffectType.DATAFLOW_SIDE_EFFECTING
WHOLE = "whole"
PIECE = "piece"
SIBLING_HALF = "sibling"
MY_HALF = "half"
EVERYONE = "everyone"
_COPIES = {WHOLE: 3, PIECE: 3, MY_HALF: 3, SIBLING_HALF: 1, EVERYONE: 7}


def _landing_shape(v, mode):
    return {WHOLE: (4,) + v.shape, MY_HALF: (4,) + v.shape, PIECE: v.shape, EVERYONE: (8,) + v.shape,
            SIBLING_HALF: (v.shape[0],) + v.shape[2:]}[mode]


def _chip_copies(v_ref, land_ref, send_sems, recv_sems, mode, sem0=0):
    x, y, c = _me()
    if mode == SIBLING_HALF:
        cp = pltpu.make_async_remote_copy(src_ref=v_ref.at[:, 1 - c], dst_ref=land_ref, send_sem=send_sems.at[sem0],
                                          recv_sem=recv_sems.at[sem0], device_id=(x, y, 1 - c), device_id_type=MESH)
        return [(cp, cp)]
    if mode == EVERYONE:
        out = []
        for f in range(1, 8):
            px, py, pc = (1 - x if f & 4 else x), (1 - y if f & 2 else y), (1 - c if f & 1 else c)
            sems = dict(send_sem=send_sems.at[sem0 + f - 1], recv_sem=recv_sems.at[sem0 + f - 1],
                        device_id=(px, py, pc), device_id_type=MESH)
            out.append((pltpu.make_async_remote_copy(src_ref=v_ref, dst_ref=land_ref.at[4 * x + 2 * y + c], **sems),
                        pltpu.make_async_remote_copy(src_ref=v_ref, dst_ref=land_ref.at[4 * px + 2 * py + pc], **sems)))
        return out
    k = 2 * x + y
    out = []
    for j, (cx, cy) in enumerate(_other_chips(x, y)):
        if mode == MY_HALF:
            src, mine, theirs = v_ref.at[c], land_ref.at[k, c], land_ref.at[2 * cx + cy, c]
        else:
            src = v_ref.at[2 * cx + cy] if mode == PIECE else v_ref
            mine, theirs = land_ref.at[k], land_ref.at[2 * cx + cy]
        sems = dict(send_sem=send_sems.at[sem0 + j], recv_sem=recv_sems.at[sem0 + j], device_id=(cx, cy, c),
                    device_id_type=MESH)
        send = pltpu.make_async_remote_copy(src_ref=src, dst_ref=mine, **sems)
        recv = pltpu.make_async_remote_copy(src_ref=src, dst_ref=theirs, **sems)
        out.append((send, recv))
    return out


def _chips_start(name, vs, mode, after=None):
    n = len(vs)
    lands = [_landing_shape(v, mode) for v in vs]

    def body(*refs):
        v_refs, land_refs = refs[:n], refs[n:2 * n]
        send_sems, recv_sems = refs[-2 * n - 3], refs[-2 * n - 2]
        token = refs[-1]
        for a in range(n):
            for send, _ in _chip_copies(v_refs[a], land_refs[a], send_sems, recv_sems, mode, _COPIES[mode] * a):
                send.start()
        token[...] = jnp.zeros_like(token)

    extra = () if after is None else (after,)
    hbm = [pltpu.with_memory_space_constraint(v, pltpu.HBM) for v in vs]
    zones = [pltpu.with_memory_space_constraint(lax.empty(s, v.dtype), pltpu.HBM) for s, v in zip(lands, vs)]
    out = pl.pallas_call(
        body, name=name,
        out_shape=(pltpu.SemaphoreType.DMA((_COPIES[mode] * n,)), pltpu.SemaphoreType.DMA((_COPIES[mode] * n,)),
                   *[pltpu.HBM(v.shape, v.dtype) for v in vs], *[pltpu.HBM(s, v.dtype) for s, v in zip(lands, vs)],
                   _sds((8, LANES), F32)),
        in_specs=(_HBM,) * (2 * n) + (pl.BlockSpec(memory_space=pl.ANY),) * len(extra),
        out_specs=(_SEM, _SEM) + (_HBM,) * (2 * n) + (pl.BlockSpec(memory_space=pltpu.VMEM),),
        input_output_aliases={i: 2 + i for i in range(2 * n)},
        compiler_params=pltpu.CompilerParams(has_side_effects=_EFFECT),
    )(*hbm, *zones, *extra)
    return out[0], out[1], list(out[2:2 + n]), list(out[2 + n:2 + 2 * n]), out[-1]


def _chips_wait(name, send_sems, recv_sems, v_thru, land_thru, mode, after):
    n = len(v_thru)

    def body(*refs):
        v_refs, land_refs = refs[:n], refs[n:2 * n]
        send_sems, recv_sems = refs[2 * n], refs[2 * n + 1]
        for a in range(n):
            for send, recv in _chip_copies(v_refs[a], land_refs[a], send_sems, recv_sems, mode, _COPIES[mode] * a):
                send.wait_send()
                recv.wait_recv()

    out = pl.pallas_call(
        body, name=name,
        out_shape=tuple(pltpu.HBM(a.shape, a.dtype) for a in list(v_thru) + list(land_thru)),
        in_specs=(_HBM,) * (2 * n) + (_SEM, _SEM, pl.BlockSpec(memory_space=pl.ANY)), out_specs=(_HBM,) * (2 * n),
        input_output_aliases={i: i for i in range(2 * n)},
        compiler_params=pltpu.CompilerParams(has_side_effects=_EFFECT),
    )(*v_thru, *land_thru, send_sems, recv_sems, after)
    return list(out[:n]), list(out[n:])


_BIG = (("w_in", (1024, 3232), 1), ("w_uq", (256, 768), 1), ("w_ukv", (128, 1024), 1), ("w_branch_a", (512, 1024), 1),
        ("w_branch_b", (512, 1024), 1), ("w_out", (1024, 1024), 0), ("w_up", (1024, 5632), 1),
        ("w_down", (2816, 1024), 0), ("w_ple_gate", (1024, 1024), 0), ("w_ple", (256, 1024), 1))


def _shard_shape(shape, axis):
    return (shape[0] // 4, shape[1]) if axis == 0 else (shape[0], shape[1] // 4)


def _half_rows(shape, axis):
    k, n = _shard_shape(shape, axis)
    return k * n // (2 * LANES)


_EARLY = ("w_in", "w_uq", "w_ukv")
_LATE = ("w_branch_a", "w_branch_b", "w_out", "w_up", "w_down", "w_ple_gate", "w_ple")
_NATURAL = ("w_in", "w_up", "w_down", "w_out", "w_ple_gate")
_EARLY_PACKED = tuple(b for b in _BIG if b[0] in _EARLY and b[0] not in _NATURAL)
_LATE_PACKED = tuple(b for b in _BIG if b[0] in _LATE and b[0] not in _NATURAL)
_SHARD = {name: _shard_shape(shape, axis) for name, shape, axis in _BIG}


def _halves(a):
    return a.reshape(a.shape[:-2] + (2, a.shape[-2] // 2, a.shape[-1]))


def _rows_joined(a):
    return a.reshape(a.shape[:-3] + (a.shape[-3] * a.shape[-2], a.shape[-1]))


def _pack_pad(group):
    return -sum(_half_rows(shape, axis) for _, shape, axis in group) % PACK_ROWS


def _pack_shards(shards, dtype, group):
    parts = [shards[name].astype(dtype).reshape(2, _half_rows(shape, axis), LANES) for name, shape, axis in group]
    return jnp.concatenate(parts + [jnp.zeros((2, _pack_pad(group), LANES), dtype)], axis=1)


def _unpack_gathered(g, group):
    out, off = {}, 0
    for name, shape, axis in group:
        r = _half_rows(shape, axis)
        k, n = _shard_shape(shape, axis)
        w = g[:, :, off:off + r, :].reshape(4, k, n)
        out[name] = w.reshape(shape) if axis == 0 else w.transpose(1, 0, 2).reshape(shape)
        off += r
    return out


def _pack_grads(grads, group):
    parts = []
    for name, shape, axis in group:
        k, n = _shard_shape(shape, axis)
        g = grads[name]
        g4 = g.reshape(4, k, n) if axis == 0 else g.reshape(k, 4, n).transpose(1, 0, 2)
        parts.append(g4.reshape(4, 2, _half_rows(shape, axis), LANES))
    return jnp.concatenate(parts + [jnp.zeros((4, 2, _pack_pad(group), LANES), F32)], axis=2)


def _unpack_shard_grads(f, group):
    out, off = {}, 0
    for name, shape, axis in group:
        r = _half_rows(shape, axis)
        out[name] = f[:, off:off + r, :].reshape(_shard_shape(shape, axis))
        off += r
    return out


def _pad_slots(w, heads, dim, axis):
    if axis == 1:
        k = w.shape[0]
        return jnp.pad(w.reshape(k, heads, dim), ((0, 0), (0, 0), (0, LANES - dim))).reshape(k, heads * LANES)
    n = w.shape[1]
    return jnp.pad(w.reshape(heads, dim, n), ((0, 0), (0, LANES - dim), (0, 0))).reshape(heads * LANES, n)


def _unpad_slots(w, heads, dim, axis):
    if axis == 1:
        k = w.shape[0]
        return w.reshape(k, heads, LANES)[:, :, :dim].reshape(k, heads * dim)
    n = w.shape[1]
    return w.reshape(heads, LANES, n)[:, :dim, :].reshape(heads * dim, n)


def _pad_w_in(w):
    kr = jnp.pad(w[:, 1152:1184], ((0, 0), (NOPE_DIM, LANES - NOPE_DIM - ROPE_DIM)))
    return jnp.concatenate([_pad_slots(w[:, 0:512], HEADS, A_HEAD_DIM, 1),
                            _pad_slots(w[:, 512:640], A_KV_HEADS, A_HEAD_DIM, 1),
                            _pad_slots(w[:, 640:768], A_KV_HEADS, A_HEAD_DIM, 1),
                            w[:, 768:1024], w[:, 1024:1152], kr, w[:, 1184:3232]], axis=1)


def _unpad_w_in(w):
    return jnp.concatenate([_unpad_slots(w[:, Z_QA:Z_KA], HEADS, A_HEAD_DIM, 1),
                            _unpad_slots(w[:, Z_KA:Z_VA], A_KV_HEADS, A_HEAD_DIM, 1),
                            _unpad_slots(w[:, Z_VA:Z_CQ], A_KV_HEADS, A_HEAD_DIM, 1),
                            w[:, Z_CQ:Z_CKV], w[:, Z_CKV:Z_KR],
                            w[:, Z_KR + NOPE_DIM:Z_KR + NOPE_DIM + ROPE_DIM], w[:, Z_GATE:ZW]], axis=1)


_SMALL = (("attn_pre_norm", 1024), ("attn_post_norm", 1024), ("b_gate", 2048), ("sinks", 8), ("q_a_norm", 256),
          ("kv_a_norm", 128), ("mlp_pre_norm", 1024), ("mlp_post_norm", 1024), ("conv_b", 5632), ("ple_norm", 1024),
          ("conv_w", 3 * 5632), ("loss", 1))


def _small_rows(n):
    return 8 * -(-n // (8 * LANES))


def _pack_small(vals):
    parts = []
    for name, n in _SMALL:
        r = _small_rows(n)
        parts.append(jnp.pad(vals[name].reshape(-1), (0, r * LANES - n)).reshape(r, LANES))
    return jnp.concatenate(parts, axis=0)


def _unpack_small(buf):
    out, off = {}, 0
    for name, n in _SMALL:
        r = _small_rows(n)
        out[name] = buf[off:off + r].reshape(-1)[:n]
        off += r
    return out


def kernel(x, p, positions, attn_pre_norm, attn_post_norm, w_in, b_gate, sinks, q_a_norm, w_uq, kv_a_norm, w_ukv, w_branch_a, w_branch_b, w_out, mlp_pre_norm, mlp_post_norm, w_up, conv_w, conv_b, w_down, ple_norm, w_ple_gate, w_ple, loss_target, m_attn_pre_norm, m_attn_post_norm, m_w_in, m_b_gate, m_sinks, m_q_a_norm, m_w_uq, m_kv_a_norm, m_w_ukv, m_w_branch_a, m_w_branch_b, m_w_out, m_mlp_pre_norm, m_mlp_post_norm, m_w_up, m_conv_w, m_conv_b, m_w_down, m_ple_norm, m_w_ple_gate, m_w_ple, v_attn_pre_norm, v_attn_post_norm, v_w_in, v_b_gate, v_sinks, v_q_a_norm, v_w_uq, v_kv_a_norm, v_w_ukv, v_w_branch_a, v_w_branch_b, v_w_out, v_mlp_pre_norm, v_mlp_post_norm, v_w_up, v_conv_w, v_conv_b, v_w_down, v_ple_norm, v_w_ple_gate, v_w_ple):
    names = ["attn_pre_norm", "attn_post_norm", "w_in", "b_gate", "sinks", "q_a_norm", "w_uq", "kv_a_norm", "w_ukv",
             "w_branch_a", "w_branch_b", "w_out", "mlp_pre_norm", "mlp_post_norm", "w_up", "conv_w", "conv_b",
             "w_down", "ple_norm", "w_ple_gate", "w_ple"]
    wts = dict(zip(names, [attn_pre_norm, attn_post_norm, w_in, b_gate, sinks, q_a_norm, w_uq, kv_a_norm, w_ukv,
                           w_branch_a, w_branch_b, w_out, mlp_pre_norm, mlp_post_norm, w_up, conv_w, conv_b, w_down,
                           ple_norm, w_ple_gate, w_ple]))
    moms = dict(zip(names, [m_attn_pre_norm, m_attn_post_norm, m_w_in, m_b_gate, m_sinks, m_q_a_norm, m_w_uq,
                            m_kv_a_norm, m_w_ukv, m_w_branch_a, m_w_branch_b, m_w_out, m_mlp_pre_norm,
                            m_mlp_post_norm, m_w_up, m_conv_w, m_conv_b, m_w_down, m_ple_norm, m_w_ple_gate, m_w_ple]))
    vars_ = dict(zip(names, [v_attn_pre_norm, v_attn_post_norm, v_w_in, v_b_gate, v_sinks, v_q_a_norm, v_w_uq,
                             v_kv_a_norm, v_w_ukv, v_w_branch_a, v_w_branch_b, v_w_out, v_mlp_pre_norm,
                             v_mlp_post_norm, v_w_up, v_conv_w, v_conv_b, v_w_down, v_ple_norm, v_w_ple_gate, v_w_ple]))
    w2 = {n: a.reshape(a.shape[-2:]) for n, a in wts.items()}
    m2 = {n: a.reshape(a.shape[-2:]) for n, a in moms.items()}
    v2 = {n: a.reshape(a.shape[-2:]) for n, a in vars_.items()}

    t_rows = x.shape[-2]
    tm = min(256, t_rows)
    tm_wide = min(512, t_rows)
    xc, yc, cc = lax.axis_index("x"), lax.axis_index("y"), lax.axis_index("c")
    chip = 2 * xc + yc

    x2d = x.reshape(t_rows, D_MODEL)
    p2d = p.reshape(t_rows, PLE_DIM)
    tgt = loss_target.reshape(t_rows, D_MODEL)
    pos_f = positions.reshape(t_rows, 1).astype(F32)

    def own_slot_filled(gathered, mine):
        return [lax.dynamic_update_slice(g, m[None], (chip, 0, 0, 0)) for g, m in zip(gathered, mine)]

    def shard_lists(group, packed_group, token=0.0):
        ws = {n: w2[n] + token for n in group}
        return [_halves(ws[n].astype(BF16)) for n in group if n in _NATURAL] + [_pack_shards(ws, BF16, packed_group)]

    cw_rows = 3 * 1408 // LANES
    conv_mine = jnp.pad(w2["conv_w"].reshape(cw_rows, LANES), ((0, 48 - cw_rows), (0, 0))).reshape(2, 24, LANES)
    early_mine = shard_lists(_EARLY, _EARLY_PACKED) + [conv_mine]
    early_sems = _chips_start("gather_early_start", early_mine, MY_HALF)
    early_token = early_sems[4][0:1, 0:1]
    consts = _rope_consts()
    tabs = _rope_tables(pos_f + early_token, consts, tm)
    late_mine = shard_lists(_LATE, _LATE_PACKED, early_token)
    both_done = tabs[0][0:1, 0:1] + sum(m[0, 0:1, 0:1].astype(F32) for m in late_mine)
    early_sent, early_landed = _chips_wait("gather_early_wait", *early_sems[:4], MY_HALF, after=both_done)
    early = own_slot_filled(_pass_to_sibling(early_landed), early_sent)
    late_sems = _chips_start("gather_late_start", late_mine, WHOLE, after=early[0])
    late_token = late_sems[4][0:1, 0:1]
    full = _unpack_gathered(early[1], _EARLY_PACKED)
    full["w_in"] = _rows_joined(early[0]).transpose(1, 0, 2).reshape(D_MODEL, 3232)
    conv_full = early[2].reshape(4, 48, LANES)[:, :cw_rows].reshape(4, 3, 1408).transpose(1, 0, 2).reshape(3, 2 * D_FF)
    convw8 = jnp.pad(conv_full, ((0, 5), (0, 0)))

    win = _pad_w_in(full["w_in"])
    wuq = _pad_slots(full["w_uq"], HEADS, NOPE_DIM + ROPE_DIM, 1)
    ukv = full["w_ukv"].reshape(KV_LORA, HEADS, NOPE_DIM + V_DIM)
    wk = _pad_slots(ukv[:, :, :NOPE_DIM].reshape(KV_LORA, HEADS * NOPE_DIM), HEADS, NOPE_DIM, 1)
    wv = _pad_slots(ukv[:, :, NOPE_DIM:].reshape(KV_LORA, HEADS * V_DIM), HEADS, V_DIM, 1)
    g1, g2, g3, g4, g5 = (w2["attn_pre_norm"], w2["attn_post_norm"], w2["mlp_pre_norm"], w2["mlp_post_norm"],
                          w2["ple_norm"])
    gq, gkv, bg, convb = w2["q_a_norm"], w2["kv_a_norm"], w2["b_gate"], w2["conv_b"]
    swa_tile = min(SWA_TILE, t_rows)
    sink_rows = jnp.repeat(w2["sinks"].reshape(A_KV_HEADS, SWA_GROUP, 1), swa_tile, axis=2).reshape(
        A_KV_HEADS, 1, SWA_GROUP * swa_tile)
    swa_bias = _swa_bias(swa_tile)

    h1, qs, ks, vs, cq, cqn, ckv, ckvn, qm, km, vm, gate = _fwd_in(x2d, g1, win, bg + late_token, gq, gkv, wuq, wk, wv,
                                                                   tabs, tm_wide)
    ya, lse_a = _swa_fwd(qs, ks, vs, swa_bias, sink_rows)
    yb, lse_b = _mla_fwd(qm, km, vm)
    late_sent, late_landed = _chips_wait("gather_late_wait", *late_sems[:4], WHOLE, after=yb)
    late = own_slot_filled(late_landed, late_sent)
    full = _unpack_gathered(late[-1], _LATE_PACKED)
    wba = _pad_slots(full["w_branch_a"], HEADS, A_HEAD_DIM, 0)
    wbb = _pad_slots(full["w_branch_b"], HEADS, V_DIM, 0)
    wple = full["w_ple"]
    natural = dict(zip([n for n in _LATE if n in _NATURAL], late))
    wup = _rows_joined(natural["w_up"])
    wout, wdown, wpg = (_rows_joined(natural[n]).reshape(-1, D_MODEL) for n in ("w_out", "w_down", "w_ple_gate"))
    pa, pb, mixed, o, x1, h2 = _fwd_mix(x2d, ya, yb, gate, wba, wbb, wout, g2, g3, tm_wide)
    up, a = _fwd_up(h2, wup, convw8, convb, tm)
    ff, x2, e, n5, sg, dx3, loss_part = _fwd_out(a, wdown, x1, g4, p2d, wple, g5, wpg, tgt, tm_wide)

    dpre, de, dx2, dff, du, dg5, dg4, dconvb, dconvw8 = _bwd_out(dx3, e, sg, x2, ff, g5, g4, wpg, wdown, up, convw8,
                                                                 convb, tm)
    dup, dx1, do, dpa, dpb, dgates, dya, dyb, delta_b, dg3, dg2, dbg = _bwd_mid(
        du, convw8, wup, dx2, x1, g3, o, g2, wout, gate, pa, pb, wba, wbb, yb, tm)
    late_grads = {
        "w_branch_a": _unpad_slots(_mm_tn("dw_branch_a", ya, dpa), HEADS, A_HEAD_DIM, 0),
        "w_branch_b": _unpad_slots(_mm_tn("dw_branch_b", yb, dpb), HEADS, V_DIM, 0),
        "w_out": _mm_tn("dw_out", mixed, do).reshape(4, D_MODEL // 4, D_MODEL),
        "w_up": _mm_tn("dw_up", h2, dup, column_shards=4),
        "w_down": _mm_tn("dw_down", a, dff).reshape(4, D_FF // 4, D_MODEL),
        "w_ple_gate": _mm_tn("dw_ple_gate", n5, dpre).reshape(4, D_MODEL // 4, D_MODEL),
        "w_ple": _mm_tn("dw_ple", p2d, de),
    }

    def grad_views(grads, group, packed_group):
        return [_halves(grads[n]) for n in group if n in _NATURAL] + [_pack_grads(grads, packed_group)]

    def pair_sums(tag, views, theirs):
        return [_add_pair("rs_%s_add_pair_%d" % (tag, i), g, r, cc) for i, (g, r) in enumerate(zip(views, theirs))]

    swap_sems = _chips_start("swap_late_start", grad_views(late_grads, _LATE, _LATE_PACKED), SIBLING_HALF)
    dqs, dks, dvs, dsink_rows = _swa_bwd(qs, ks, vs, ya, dya, lse_a, swa_bias, sink_rows + swap_sems[4][0:1, 0:1])
    dsink = dsink_rows[:, 0:SWA_GROUP, 0]
    late_views, late_theirs = _chips_wait("swap_late_wait", *swap_sems[:4], SIBLING_HALF, after=dqs)
    rs_sems = _chips_start("scatter_late_start", pair_sums("late", late_views, late_theirs), PIECE)
    dqm, dkm, dvm = _mla_bwd(qm, km, vm, dyb, lse_b, delta_b.reshape(HEADS, 1, t_rows) + rs_sems[4][0:1, 0:1])
    dz, dqb, dx, dgq, dgkv, dg1 = _bwd_in(dqs, dks, dvs, dqm, dkm, dvm, tabs, consts, cq, ckv, gq, gkv, wuq, wk, wv,
                                           dgates, win, x2d, g1, dx1, tm)

    small = {"attn_pre_norm": dg1, "attn_post_norm": dg2, "b_gate": dbg, "sinks": dsink, "q_a_norm": dgq,
             "kv_a_norm": dgkv, "mlp_pre_norm": dg3, "mlp_post_norm": dg4, "conv_b": dconvb, "ple_norm": dg5,
             "conv_w": dconvw8[0:3], "loss": loss_part}
    small_sems = _chips_start("gather_small_start", [_pack_small(small)], EVERYONE)
    small_token = small_sems[4]

    dwk = _unpad_slots(_mm_tn("dw_k", ckvn, dkm, after=small_token), HEADS, NOPE_DIM, 1).reshape(
        KV_LORA, HEADS, NOPE_DIM)
    dwv = _unpad_slots(_mm_tn("dw_v", ckvn, dvm, after=small_token), HEADS, V_DIM, 1).reshape(KV_LORA, HEADS, V_DIM)
    early_grads = {
        "w_in": _unpad_w_in(_mm_tn("dw_in", h1, dz, after=small_token)).reshape(D_MODEL, 4, 808).transpose(1, 0, 2),
        "w_uq": _unpad_slots(_mm_tn("dw_uq", cqn, dqb, after=small_token), HEADS, NOPE_DIM + ROPE_DIM, 1),
        "w_ukv": jnp.concatenate([dwk, dwv], axis=2).reshape(KV_LORA, HEADS * (NOPE_DIM + V_DIM)),
    }

    def finish(tag, pairs, landed, group, packed_group):
        reduced = []
        for i, (pair, land) in enumerate(zip(pairs, landed)):
            own = lax.dynamic_index_in_dim(pair, chip, 0, keepdims=True)
            reduced.append(_add_chips("rs_%s_add_chips_%d" % (tag, i),
                                      lax.dynamic_update_slice(land, own, (chip, 0, 0))))
        others = _swap_sibling("swap_%s_reduced_halves" % tag, reduced)
        r, o = reduced[-1], others[-1]
        packed = jnp.where(cc == 0, jnp.stack([r, o]), jnp.stack([o, r]))
        for n, g in _unpack_shard_grads(packed, packed_group).items():
            updates[n] = _adamw("adamw_" + n, w2[n], g, m2[n], v2[n])
        for n, r, o in zip([n for n in group if n in _NATURAL], reduced, others):
            updates[n] = _adamw_halves("adamw_" + n, w2[n], r, o, m2[n], v2[n], cc)

    updates = {}

    def adamw(n, g):
        updates[n] = _adamw("adamw_" + n, w2[n], g, m2[n], v2[n])

    early_views = grad_views(early_grads, _EARLY, _EARLY_PACKED)
    early_theirs = _swap_sibling("swap_early_grad_halves", early_views, other_half=True)
    small_sent, small_landed = _chips_wait("gather_small_wait", *small_sems[:4], EVERYONE, after=early_theirs[0])
    small_all = lax.dynamic_update_slice(small_landed[0], small_sent[0][None], (4 * xc + 2 * yc + cc, 0, 0))
    early_sems = _chips_start("scatter_early_start", pair_sums("early", early_views, early_theirs), PIECE,
                              after=small_all)
    late_pairs, late_landed = _chips_wait("scatter_late_wait", *rs_sems[:4], PIECE, after=early_sems[4])
    finish("late", late_pairs, late_landed, _LATE, _LATE_PACKED)
    early_pairs, early_landed = _chips_wait("scatter_early_wait", *early_sems[:4], PIECE,
                                            after=updates[_LATE[-1]][1])
    finish("early", early_pairs, early_landed, _EARLY, _EARLY_PACKED)

    small_sum = _unpack_small(_add_devices(small_all))
    for n in names:
        if n == "conv_w":
            adamw(n, lax.dynamic_index_in_dim(small_sum[n].reshape(3, 4, 1408), chip, 1, keepdims=False))
        elif n in small_sum:
            adamw(n, small_sum[n].reshape(w2[n].shape))
    loss = small_sum["loss"][0]

    outs = [[updates[n][i].reshape(wts[n].shape) for n in names] for i in range(4)]
    return (loss, dx.reshape(x.shape), *outs[0], *outs[1], *outs[2], *outs[3])
```

```python
import functools
import math

import numpy as np
import jax
import jax.numpy as jnp
from jax import lax
from jax.experimental import pallas as pl
from jax.experimental.pallas import tpu as pltpu

F32 = jnp.float32
BF16 = jnp.bfloat16

D_MODEL = 1024
D_FF = 2816
PLE_DIM = 256
ROPE_THETA = 10000.0
RMS_EPS = 1e-6
SWA_WINDOW = 128
HEADS = 8
A_KV_HEADS = 2
A_HEAD_DIM = 64
Q_LORA = 256
KV_LORA = 128
NOPE_DIM = 64
ROPE_DIM = 32
V_DIM = 64
LANES = 128
ZW = 4096
NEG = -1e30
SCALE_A = A_HEAD_DIM ** -0.5
SCALE_B = (NOPE_DIM + ROPE_DIM) ** -0.5

ADAM_LR = 0.001
ADAM_B1 = 0.9
ADAM_B2 = 0.999
ADAM_EPS = 1e-08
ADAM_WD = 0.01
ADAM_STEP = 10

VMEM_LIMIT = 60 * 1024 * 1024
MESH_AXES = ("x", "y", "c")
MESH = pl.DeviceIdType.MESH

Z_QA, Z_KA, Z_VA, Z_CQ, Z_CKV, Z_KR, Z_GATE = 0, 1024, 1280, 1536, 1792, 1920, 2048


def _dot(a, b):
    return jnp.dot(a, b, preferred_element_type=F32)


def _dot_nt(a, b):
    return lax.dot_general(a, b, (((1,), (1,)), ((), ())), preferred_element_type=F32)


def _dot_tn(a, b):
    return lax.dot_general(a, b, (((0,), (0,)), ((), ())), preferred_element_type=F32)


def _rms_stats(x):
    r = lax.rsqrt(jnp.mean(x * x, axis=-1, keepdims=True) + RMS_EPS)
    return x * r, r


def _rms_bwd(dy, xn, r, g):
    dxn = dy * g
    dx = r * (dxn - xn * jnp.mean(dxn * xn, axis=-1, keepdims=True))
    dg = jnp.sum(dy * xn, axis=0, keepdims=True)
    return dx, dg


def _tile_lanes(t, n):
    return t if n == 1 else jnp.concatenate([t] * n, axis=1)


def _rope(x, c, s1, s2, half):
    w = x.shape[1]
    n = w // LANES
    return (x * _tile_lanes(c, n) + pltpu.roll(x, w - half, 1) * _tile_lanes(s1, n)
            + pltpu.roll(x, half, 1) * _tile_lanes(s2, n))


def _rope_t(dy, c, s1, s2, half):
    w = dy.shape[1]
    n = w // LANES
    return (dy * _tile_lanes(c, n) + pltpu.roll(dy * _tile_lanes(s1, n), half, 1)
            + pltpu.roll(dy * _tile_lanes(s2, n), w - half, 1))


def _sigmoid(x):
    return 1.0 / (1.0 + jnp.exp(-x))


_GELU_C = math.sqrt(2.0 / math.pi)


def _gelu_and_grad(x):
    a = _GELU_C + (_GELU_C * 0.044715) * (x * x)
    th = jnp.tanh(x * a)
    hx = 0.5 * x
    p1 = 1.0 + th
    gel = hx * p1
    dgel = 0.5 * p1 + (hx * (1.0 - th * th)) * (3.0 * a - 2.0 * _GELU_C)
    return gel, dgel


def _conv_taps(up, h6, h7):
    r1 = pltpu.roll(up, 1, 0)
    r2 = pltpu.roll(up, 2, 0)
    rows = lax.broadcasted_iota(jnp.int32, (8, up.shape[1]), 0)
    xm1 = jnp.concatenate([jnp.where(rows == 0, h7, r1[0:8]), r1[8:]], axis=0)
    xm2 = jnp.concatenate([jnp.where(rows == 0, h6, jnp.where(rows == 1, h7, r2[0:8])), r2[8:]], axis=0)
    return xm1, xm2


def _conv_taps_next(du, n0, n1):
    tm = du.shape[0]
    r1 = pltpu.roll(du, tm - 1, 0)
    r2 = pltpu.roll(du, tm - 2, 0)
    rows = lax.broadcasted_iota(jnp.int32, (8, du.shape[1]), 0)
    xp1 = jnp.concatenate([r1[:tm - 8], jnp.where(rows == 7, n0, r1[tm - 8:])], axis=0)
    xp2 = jnp.concatenate([r2[:tm - 8], jnp.where(rows == 6, n0, jnp.where(rows == 7, n1, r2[tm - 8:]))], axis=0)
    return xp1, xp2


def _row(tm, n):
    return pl.BlockSpec((tm, n), lambda i: (i, 0))


def _full(shape):
    nd = len(shape)
    return pl.BlockSpec(tuple(shape), lambda i: (0,) * nd)


def _resident(shape):
    nd = len(shape)
    return pl.BlockSpec(tuple(shape), lambda i: (0,) * nd, pipeline_mode=pl.Buffered(1))


def _heads(tm, h):
    return pl.BlockSpec((h, tm, LANES), lambda i: (0, i, 0))


def _rows_call(name, body, t_rows, tm, ins, outs, scratch=()):
    return pl.pallas_call(
        body, name=name, grid=(t_rows // tm,),
        in_specs=[s for _, s in ins],
        out_specs=[s for _, s in outs],
        out_shape=[s for s, _ in outs],
        scratch_shapes=list(scratch),
        compiler_params=pltpu.CompilerParams(dimension_semantics=("arbitrary",), vmem_limit_bytes=VMEM_LIMIT),
    )(*[a for a, _ in ins])


def _sds(shape, dtype):
    return jax.ShapeDtypeStruct(tuple(shape), dtype)


def _rope_consts():
    c = np.zeros((16, LANES), np.float32)
    lane = np.arange(LANES)
    inv_a = (ROPE_THETA ** (-(np.arange(0, A_HEAD_DIM, 2, dtype=np.float32) / A_HEAD_DIM))).astype(np.float32)
    in_a = lane < A_HEAD_DIM
    c[0, in_a] = inv_a[lane[in_a] % (A_HEAD_DIM // 2)]
    c[1, in_a] = 1.0
    c[2, lane < A_HEAD_DIM // 2] = -1.0
    c[3, (lane >= A_HEAD_DIM // 2) & in_a] = 1.0
    inv_b = (ROPE_THETA ** (-(np.arange(0, ROPE_DIM, 2, dtype=np.float32) / ROPE_DIM))).astype(np.float32)
    pe = (lane >= NOPE_DIM) & (lane < NOPE_DIM + ROPE_DIM)
    c[5, pe] = inv_b[(lane[pe] - NOPE_DIM) % (ROPE_DIM // 2)]
    c[6, pe] = 1.0
    c[7, (lane >= NOPE_DIM) & (lane < NOPE_DIM + ROPE_DIM // 2)] = -1.0
    c[8, (lane >= NOPE_DIM + ROPE_DIM // 2) & (lane < NOPE_DIM + ROPE_DIM)] = 1.0
    c[9, lane < NOPE_DIM] = 1.0
    c[10, pe] = 1.0
    return jnp.asarray(c)


def _rope_tables(pos_f, consts, tm):
    t_rows = pos_f.shape[0]

    def body(pos_ref, c_ref, ca, sa1, sa2, cb, sb1, sb2):
        ang = pos_ref[...] * (c_ref[0:1, :] + c_ref[5:6, :])
        cs, sn = jnp.cos(ang), jnp.sin(ang)
        ca[...] = cs * c_ref[1:2, :]
        sa1[...] = sn * c_ref[2:3, :]
        sa2[...] = sn * c_ref[3:4, :]
        cb[...] = cs * c_ref[6:7, :] + c_ref[9:10, :]
        sb1[...] = sn * c_ref[7:8, :]
        sb2[...] = sn * c_ref[8:9, :]

    tab = (_sds((t_rows, LANES), F32), _row(tm, LANES))
    return _rows_call("rope_tables", body, t_rows, tm,
                      [(pos_f, _row(tm, 1)), (consts, _full(consts.shape))], [tab] * 6)


def _fwd_in(x, g1, win, bg, gq, gkv, wuq, wk, wv, tabs, tm):
    t_rows = x.shape[0]

    def body(x_ref, g1_ref, win_ref, bg_ref, gq_ref, gkv_ref, wuq_ref, wk_ref, wv_ref,
             ca, sa1, sa2, cb, sb1, sb2,
             h1_ref, qs_ref, ks_ref, vs_ref, cq_ref, cqn_ref, ckv_ref, ckvn_ref, qm_ref, km_ref, vm_ref, gate_ref):
        xn, _ = _rms_stats(x_ref[...])
        hb = (xn * g1_ref[...]).astype(BF16)
        h1_ref[...] = hb
        ta = (ca[...], sa1[...], sa2[...])
        tb = (cb[...], sb1[...], sb2[...])
        cq = _dot(hb, win_ref[:, Z_CQ:Z_CKV])
        ckv = _dot(hb, win_ref[:, Z_CKV:Z_KR])
        z_qa = _dot(hb, win_ref[:, Z_QA:Z_KA])
        z_ka = _dot(hb, win_ref[:, Z_KA:Z_VA])
        z_va = _dot(hb, win_ref[:, Z_VA:Z_CQ])
        z_kr = _dot(hb, win_ref[:, Z_KR:Z_GATE])
        cq_ref[...] = cq
        cqn, _ = _rms_stats(cq)
        cqb = (cqn * gq_ref[...]).astype(BF16)
        cqn_ref[...] = cqb
        ckv_ref[...] = ckv
        ckvn, _ = _rms_stats(ckv)
        ckvb = (ckvn * gkv_ref[...]).astype(BF16)
        ckvn_ref[...] = ckvb
        z_qm = _dot(cqb, wuq_ref[...])
        z_km = _dot(ckvb, wk_ref[...])
        z_vm = _dot(ckvb, wv_ref[...])
        z_gate = _dot(hb, win_ref[:, Z_GATE:ZW])
        qs_ref[...] = (_rope(z_qa, *ta, A_HEAD_DIM // 2) * SCALE_A).astype(BF16)
        ks_ref[...] = _rope(z_ka, *ta, A_HEAD_DIM // 2).astype(BF16)
        vs_ref[...] = z_va.astype(BF16)
        qm_ref[...] = (_rope(z_qm, *tb, ROPE_DIM // 2) * SCALE_B).astype(BF16)
        km_ref[...] = (z_km + _tile_lanes(_rope(z_kr, *tb, ROPE_DIM // 2), HEADS)).astype(BF16)
        vm_ref[...] = z_vm.astype(BF16)
        gate_ref[...] = _sigmoid(z_gate + bg_ref[...])

    def o(n, dt):
        return (_sds((t_rows, n), dt), _row(tm, n))

    ins = [(x, _row(tm, D_MODEL)), (g1, _full(g1.shape)), (win, _resident(win.shape)), (bg, _full(bg.shape)),
           (gq, _full(gq.shape)), (gkv, _full(gkv.shape)), (wuq, _full(wuq.shape)), (wk, _full(wk.shape)),
           (wv, _full(wv.shape))] + [(t, _row(tm, LANES)) for t in tabs]
    outs = [o(1024, BF16), o(1024, BF16), o(256, BF16), o(256, BF16), o(256, F32), o(256, BF16), o(128, F32),
            o(128, BF16), o(1024, BF16), o(1024, BF16), o(1024, BF16), o(2048, F32)]
    return _rows_call("fwd_in", body, t_rows, tm, ins, outs)


def _attn_tile(t_rows):
    return min(512, t_rows)


MLA_HEADS_PER_STEP = 4
MLA_FWD_HEADS_PER_STEP = 8


def _causal_pairs(nq, by_kv):
    if by_kv:
        pairs = [(i, j) for j in range(nq) for i in range(j, nq)]
    else:
        pairs = [(i, j) for i in range(nq) for j in range(i + 1)]
    return (jnp.asarray([p[0] for p in pairs], jnp.int32), jnp.asarray([p[1] for p in pairs], jnp.int32))


def _mla_fwd(q, k, v):
    t_rows = q.shape[0]
    t = _attn_tile(t_rows)
    hp = MLA_FWD_HEADS_PER_STEP
    w = hp * LANES
    ii, jj = _causal_pairs(t_rows // t, by_kv=False)

    def body(i_ref, j_ref, q_ref, k_ref, v_ref, o_ref, lse_ref, m_s, l_s, acc_s):
        i = i_ref[pl.program_id(1)]
        j = j_ref[pl.program_id(1)]

        @pl.when(j == 0)
        def _():
            m_s[...] = jnp.full(m_s.shape, NEG, F32)
            l_s[...] = jnp.zeros(l_s.shape, F32)
            acc_s[...] = jnp.zeros(acc_s.shape, F32)

        def step(diagonal):
            sls = [slice(hh * LANES, (hh + 1) * LANES) for hh in range(hp)]
            scores = [_dot_nt(k_ref[:, sl], q_ref[:, sl]) for sl in sls]
            if diagonal:
                valid = (lax.broadcasted_iota(jnp.int32, (t, t), 0) <= lax.broadcasted_iota(jnp.int32, (t, t), 1))
                scores = [jnp.where(valid, s, NEG) for s in scores]
            stats = []
            for hh, s in enumerate(scores):
                m_prev = m_s[hh]
                m_new = jnp.maximum(m_prev, jnp.max(s, axis=0, keepdims=True))
                p = jnp.exp(s - m_new)
                alpha = jnp.exp(m_prev - m_new)
                stats.append((m_new, alpha, alpha * l_s[hh] + jnp.sum(p, axis=0, keepdims=True), p.astype(BF16)))
            for hh, (m_new, alpha, l_new, p) in enumerate(stats):
                sl = sls[hh]
                acc = alpha * acc_s[hh] + _dot_tn(v_ref[:, sl], p)
                if diagonal:
                    o_ref[:, sl] = (acc / l_new).T.astype(o_ref.dtype)
                    lse_ref[hh] = m_new + jnp.log(l_new)
                else:
                    m_s[hh] = m_new
                    l_s[hh] = l_new
                    acc_s[hh] = acc

        pl.when(j < i)(lambda: step(False))
        pl.when(j == i)(lambda: step(True))

    grid_spec = pltpu.PrefetchScalarGridSpec(
        num_scalar_prefetch=2, grid=(HEADS // hp, ii.shape[0]),
        in_specs=[pl.BlockSpec((t, w), lambda hb, s, ir, jr: (ir[s], hb)),
                  pl.BlockSpec((t, w), lambda hb, s, ir, jr: (jr[s], hb)),
                  pl.BlockSpec((t, w), lambda hb, s, ir, jr: (jr[s], hb))],
        out_specs=[pl.BlockSpec((t, w), lambda hb, s, ir, jr: (ir[s], hb)),
                   pl.BlockSpec((hp, 1, t), lambda hb, s, ir, jr: (hb, 0, ir[s]))],
        scratch_shapes=[pltpu.VMEM((hp, 1, t), F32), pltpu.VMEM((hp, 1, t), F32), pltpu.VMEM((hp, LANES, t), F32)])
    return pl.pallas_call(
        body, name="mla_fwd", grid_spec=grid_spec,
        out_shape=[_sds((t_rows, HEADS * LANES), BF16), _sds((HEADS, 1, t_rows), F32)],
        compiler_params=pltpu.CompilerParams(dimension_semantics=("arbitrary",) * 2, vmem_limit_bytes=VMEM_LIMIT),
    )(ii, jj, q, k, v)


def _mla_bwd(q, k, v, do, lse, delta):
    t_rows = q.shape[0]
    t = _attn_tile(t_rows)
    hp = MLA_HEADS_PER_STEP
    w = hp * LANES
    ii, jj = _causal_pairs(t_rows // t, by_kv=True)

    def body(i_ref, j_ref, q_ref, k_ref, v_ref, do_ref, lse_ref, dl_ref, dq_ref, dk_ref, dv_ref):
        i = i_ref[pl.program_id(1)]
        j = j_ref[pl.program_id(1)]

        @pl.when(pl.program_id(1) == 0)
        def _():
            dq_ref[...] = jnp.zeros(dq_ref.shape, F32)

        def step(diagonal):
            r0 = pl.multiple_of(i * t, t)
            sls = [slice(hh * LANES, (hh + 1) * LANES) for hh in range(hp)]
            scores = [_dot_nt(k_ref[:, sl], q_ref[:, sl]) for sl in sls]
            if diagonal:
                valid = (lax.broadcasted_iota(jnp.int32, (t, t), 0) <= lax.broadcasted_iota(jnp.int32, (t, t), 1))
                scores = [jnp.where(valid, s, NEG) for s in scores]
            dps = [_dot_nt(v_ref[:, sl], do_ref[:, sl]) for sl in sls]
            ps = [jnp.exp(s - lse_ref[hh]) for hh, s in enumerate(scores)]
            dss = [(p * (dp - dl_ref[hh])).astype(BF16) for hh, (p, dp) in enumerate(zip(ps, dps))]
            for hh, sl in enumerate(sls):
                dv = _dot(ps[hh].astype(BF16), do_ref[:, sl])
                dk = _dot(dss[hh], q_ref[:, sl])
                if diagonal:
                    dv_ref[:, sl] = dv
                    dk_ref[:, sl] = dk
                else:
                    dv_ref[:, sl] += dv
                    dk_ref[:, sl] += dk
                dq_ref[hh, pl.ds(r0, t), :] += _dot_tn(dss[hh], k_ref[:, sl])

        pl.when(i > j)(lambda: step(False))
        pl.when(i == j)(lambda: step(True))

    def qmap(hb, s, ir, jr):
        return (ir[s], hb)

    def kvmap(hb, s, ir, jr):
        return (jr[s], hb)

    def rowmap(hb, s, ir, jr):
        return (hb, 0, ir[s])

    grid_spec = pltpu.PrefetchScalarGridSpec(
        num_scalar_prefetch=2, grid=(HEADS // hp, ii.shape[0]),
        in_specs=[pl.BlockSpec((t, w), qmap), pl.BlockSpec((t, w), kvmap), pl.BlockSpec((t, w), kvmap),
                  pl.BlockSpec((t, w), qmap), pl.BlockSpec((hp, 1, t), rowmap), pl.BlockSpec((hp, 1, t), rowmap)],
        out_specs=[pl.BlockSpec((hp, t_rows, LANES), lambda hb, s, ir, jr: (hb, 0, 0)),
                   pl.BlockSpec((t, w), kvmap), pl.BlockSpec((t, w), kvmap)])
    return pl.pallas_call(
        body, name="mla_bwd", grid_spec=grid_spec,
        out_shape=[_sds((HEADS, t_rows, LANES), F32), _sds((t_rows, HEADS * LANES), F32),
                   _sds((t_rows, HEADS * LANES), F32)],
        compiler_params=pltpu.CompilerParams(dimension_semantics=("arbitrary",) * 2, vmem_limit_bytes=VMEM_LIMIT),
    )(ii, jj, q, k, v, do, lse, delta)


SWA_TILE = 2 * SWA_WINDOW
SWA_GROUP = HEADS // A_KV_HEADS


def _swa_bias(tq):
    koff = lax.broadcasted_iota(jnp.int32, (tq + SWA_WINDOW, SWA_GROUP * tq), 0) - SWA_WINDOW
    qoff = (lax.broadcasted_iota(jnp.int32, (tq + SWA_WINDOW, SWA_GROUP * tq), 1) % tq)
    band = (koff <= qoff) & (qoff - koff < SWA_WINDOW)
    return jnp.stack([jnp.where(band & (koff >= 0), 0.0, NEG), jnp.where(band, 0.0, NEG)]).astype(F32)


def _swa_specs(tq, nq):
    wb = tq // SWA_WINDOW
    kvw = A_KV_HEADS * LANES

    def qi(i):
        return jnp.minimum(i, nq - 1)

    q = pl.BlockSpec((tq, HEADS * LANES), lambda i: (qi(i), 0))
    cur = pl.BlockSpec((tq, kvw), lambda i: (qi(i), 0))
    prev = pl.BlockSpec((SWA_WINDOW, kvw), lambda i: (jnp.maximum(qi(i) * wb - 1, 0), 0))
    bias = pl.BlockSpec((1, tq + SWA_WINDOW, SWA_GROUP * tq), lambda i: (jnp.minimum(i, 1), 0, 0))
    rows = pl.BlockSpec((A_KV_HEADS, 1, 1, SWA_GROUP * tq), lambda i: (0, qi(i), 0, 0))
    sink = pl.BlockSpec((A_KV_HEADS, 1, SWA_GROUP * tq), lambda i: (0, 0, 0))
    return q, cur, prev, bias, rows, sink


def _stack_heads(ref, kvh):
    base = kvh * SWA_GROUP
    return jnp.concatenate([ref[:, (base + g) * LANES:(base + g + 1) * LANES] for g in range(SWA_GROUP)], axis=0)


def _unstack_heads(ref, kvh, val, tq):
    base = kvh * SWA_GROUP
    for g in range(SWA_GROUP):
        ref[:, (base + g) * LANES:(base + g + 1) * LANES] = val[g * tq:(g + 1) * tq].astype(ref.dtype)


def _kv_window(prev_ref, cur_ref, kvh):
    sl = slice(kvh * LANES, (kvh + 1) * LANES)
    return jnp.concatenate([prev_ref[:, sl], cur_ref[:, sl]], axis=0)


def _swa_fwd(q, k, v, bias, sink_rows):
    t_rows = q.shape[0]
    tq = min(SWA_TILE, t_rows)
    nq = t_rows // tq
    qs_, cur, prev, bs, rows, sk = _swa_specs(tq, nq)
    kvhs = range(A_KV_HEADS)

    def body(q_ref, kc_ref, kp_ref, vc_ref, vp_ref, b_ref, sink_ref, o_ref, lse_ref):
        scores = [_dot_nt(_kv_window(kp_ref, kc_ref, h), _stack_heads(q_ref, h)) + b_ref[0] for h in kvhs]
        stats = []
        for h, s in zip(kvhs, scores):
            sink = sink_ref[h]
            m = jnp.maximum(jnp.max(s, axis=0, keepdims=True), sink)
            p = jnp.exp(s - m)
            l = jnp.sum(p, axis=0, keepdims=True) + jnp.exp(sink - m)
            lse_ref[h, 0] = m + jnp.log(l)
            stats.append((p.astype(BF16), l))
        for h, (p, l) in zip(kvhs, stats):
            _unstack_heads(o_ref, h, (_dot_tn(_kv_window(vp_ref, vc_ref, h), p) / l).T, tq)

    return pl.pallas_call(
        body, name="swa_fwd", grid=(nq,),
        in_specs=[qs_, cur, prev, cur, prev, bs, sk],
        out_specs=[qs_, rows],
        out_shape=[_sds((t_rows, HEADS * LANES), BF16), _sds((A_KV_HEADS, nq, 1, SWA_GROUP * tq), F32)],
        compiler_params=pltpu.CompilerParams(dimension_semantics=("arbitrary",), vmem_limit_bytes=VMEM_LIMIT),
    )(q, k, k, v, v, bias, sink_rows)


def _swa_bwd(q, k, v, o, do, lse, bias, sink_rows):
    t_rows = q.shape[0]
    tq = min(SWA_TILE, t_rows)
    nq = t_rows // tq
    qs_, cur, prev, bs, rows, sk = _swa_specs(tq, nq)
    hw = SWA_WINDOW
    kvhs = range(A_KV_HEADS)
    kvw = A_KV_HEADS * LANES

    def body(q_ref, kc_ref, kp_ref, vc_ref, vp_ref, o_ref, do_ref, lse_ref, b_ref, sink_ref,
             dq_ref, dk_ref, dv_ref, dsink_ref, ck, cv, dsa):
        i = pl.program_id(0)

        @pl.when(i == 0)
        def _():
            dsa[...] = jnp.zeros(dsa.shape, F32)

        @pl.when(i < nq)
        def _():
            qs = [_stack_heads(q_ref, h) for h in kvhs]
            dos = [_stack_heads(do_ref, h) for h in kvhs]
            kks = [_kv_window(kp_ref, kc_ref, h) for h in kvhs]
            scores = [_dot_nt(kks[h], qs[h]) for h in kvhs]
            dps = [_dot_nt(_kv_window(vp_ref, vc_ref, h), dos[h]) for h in kvhs]
            ps, dss = [], []
            for h in kvhs:
                lse = lse_ref[h, 0]
                p = jnp.exp(scores[h] + b_ref[0] - lse)
                delta = jnp.sum((_stack_heads(o_ref, h).astype(F32) * dos[h].astype(F32)).T, axis=0, keepdims=True)
                dsa[h] += -jnp.exp(sink_ref[h] - lse) * delta
                ps.append(p.astype(BF16))
                dss.append((p * (dps[h] - delta)).astype(BF16))
            for h in kvhs:
                sl = slice(h * LANES, (h + 1) * LANES)
                dv = _dot(ps[h], dos[h])
                dk = _dot(dss[h], qs[h])
                _unstack_heads(dq_ref, h, _dot_tn(dss[h], kks[h]), tq)

                @pl.when(i > 0)
                def _():
                    dk_ref[0:tq - hw, sl] = ck[0:tq - hw, sl]
                    dk_ref[tq - hw:tq, sl] = ck[tq - hw:tq, sl] + dk[0:hw]
                    dv_ref[0:tq - hw, sl] = cv[0:tq - hw, sl]
                    dv_ref[tq - hw:tq, sl] = cv[tq - hw:tq, sl] + dv[0:hw]

                ck[:, sl] = dk[hw:hw + tq]
                cv[:, sl] = dv[hw:hw + tq]

        @pl.when(i == nq)
        def _():
            dk_ref[...] = ck[...]
            dv_ref[...] = cv[...]
            dsink_ref[...] = jnp.zeros(dsink_ref.shape, F32)
            for h in kvhs:
                for g in range(SWA_GROUP):
                    tot = jnp.sum(dsa[h, :, g * tq:(g + 1) * tq], axis=1, keepdims=True)
                    dsink_ref[h, g:g + 1, :] = jnp.zeros((1, LANES), F32) + tot

    kv_out = pl.BlockSpec((tq, kvw), lambda i: (jnp.maximum(i - 1, 0), 0))
    return pl.pallas_call(
        body, name="swa_bwd", grid=(nq + 1,),
        in_specs=[qs_, cur, prev, cur, prev, qs_, qs_, rows, bs, sk],
        out_specs=[qs_, kv_out, kv_out, pl.BlockSpec((A_KV_HEADS, 8, LANES), lambda i: (0, 0, 0))],
        out_shape=[_sds((t_rows, HEADS * LANES), F32), _sds((t_rows, kvw), F32), _sds((t_rows, kvw), F32),
                   _sds((A_KV_HEADS, 8, LANES), F32)],
        scratch_shapes=[pltpu.VMEM((tq, kvw), F32), pltpu.VMEM((tq, kvw), F32),
                        pltpu.VMEM((A_KV_HEADS, 1, SWA_GROUP * tq), F32)],
        compiler_params=pltpu.CompilerParams(dimension_semantics=("arbitrary",), vmem_limit_bytes=VMEM_LIMIT),
    )(q, k, k, v, v, o, do, lse, bias, sink_rows)


def _fwd_mix(x, ya, yb, gate, wba, wbb, wout, g2, g3, tm):
    t_rows = x.shape[0]

    def body(x_ref, ya_ref, yb_ref, gate_ref, wba_ref, wbb_ref, wout_ref, g2_ref, g3_ref,
             pa_ref, pb_ref, mixed_ref, o_ref, x1_ref, h2_ref):
        pa = _dot(ya_ref[...], wba_ref[...])
        pb = _dot(yb_ref[...], wbb_ref[...])
        pa_ref[...] = pa
        pb_ref[...] = pb
        mixed = (gate_ref[:, 0:D_MODEL] * pa + gate_ref[:, D_MODEL:2 * D_MODEL] * pb).astype(BF16)
        mixed_ref[...] = mixed
        o = _dot(mixed, wout_ref[...])
        o_ref[...] = o
        on, _ = _rms_stats(o)
        x1 = x_ref[...] + on * g2_ref[...]
        x1_ref[...] = x1
        x1n, _ = _rms_stats(x1)
        h2_ref[...] = (x1n * g3_ref[...]).astype(BF16)

    def o_(dt):
        return (_sds((t_rows, D_MODEL), dt), _row(tm, D_MODEL))

    ins = [(x, _row(tm, D_MODEL)), (ya, _row(tm, 1024)), (yb, _row(tm, 1024)), (gate, _row(tm, 2048)),
           (wba, _resident(wba.shape)), (wbb, _resident(wbb.shape)), (wout, _resident(wout.shape)),
           (g2, _full(g2.shape)), (g3, _full(g3.shape))]
    return _rows_call("fwd_mix", body, t_rows, tm, ins, [o_(F32), o_(F32), o_(BF16), o_(F32), o_(F32), o_(BF16)])


CONV_CHUNK = 1408


def _fwd_up(h2, wup, convw8, convb, tm):
    t_rows = h2.shape[0]
    cdim = 2 * D_FF

    def body(h2_ref, wup_ref, cw_ref, cb_ref, up_ref, a_ref, carry):
        i = pl.program_id(0)

        @pl.when(i == 0)
        def _():
            carry[...] = jnp.zeros(carry.shape, F32)

        hb = h2_ref[...]
        ups = [_dot(hb, wup_ref[s]) for s in range(cdim // CONV_CHUNK)]

        def conv(c0):
            sl = slice(c0, c0 + CONV_CHUNK)
            up = ups[c0 // CONV_CHUNK]
            up_ref[:, sl] = up
            xm1, xm2 = _conv_taps(up, carry[6:7, sl], carry[7:8, sl])
            u = cw_ref[0:1, sl] * xm2 + cw_ref[1:2, sl] * xm1 + cw_ref[2:3, sl] * up + cb_ref[:, sl]
            carry[:, sl] = up[tm - 8:tm, :]
            return u

        for c0 in range(0, D_FF, CONV_CHUNK):
            ug = conv(c0)
            uv = conv(D_FF + c0)
            gel, _ = _gelu_and_grad(ug)
            a_ref[:, c0:c0 + CONV_CHUNK] = (gel * uv).astype(BF16)

    ins = [(h2, _row(tm, D_MODEL)), (wup, _resident(wup.shape)), (convw8, _full(convw8.shape)), (convb, _full(convb.shape))]
    outs = [(_sds((t_rows, cdim), F32), _row(tm, cdim)), (_sds((t_rows, D_FF), BF16), _row(tm, D_FF))]
    return _rows_call("fwd_up", body, t_rows, tm, ins, outs, scratch=[pltpu.VMEM((8, cdim), F32)])


def _fwd_out(a, wdown, x1, g4, p, wple, g5, wpg, tgt, tm):
    t_rows = a.shape[0]

    def body(a_ref, wdown_ref, x1_ref, g4_ref, p_ref, wple_ref, g5_ref, wpg_ref, tgt_ref,
             ff_ref, x2_ref, e_ref, n5_ref, sg_ref, dx3_ref, loss_ref):
        i = pl.program_id(0)
        ff = _dot(a_ref[...], wdown_ref[...])
        e = _dot(p_ref[...].astype(BF16), wple_ref[...])
        ff_ref[...] = ff
        ffn, _ = _rms_stats(ff)
        x2 = x1_ref[...] + ffn * g4_ref[...]
        x2_ref[...] = x2
        e_ref[...] = e
        x2n, _ = _rms_stats(x2)
        n5 = (x2n * g5_ref[...]).astype(BF16)
        n5_ref[...] = n5
        sg = _sigmoid(_dot(n5, wpg_ref[...]))
        sg_ref[...] = sg
        d = x2 + sg * e - tgt_ref[...]
        dx3_ref[...] = d * (1.0 / D_MODEL)

        @pl.when(i == 0)
        def _():
            loss_ref[...] = jnp.zeros((1, 1), F32)

        loss_ref[...] += 0.5 * jnp.sum(jnp.sum(d * d, axis=1, keepdims=True), axis=0, keepdims=True) * (1.0 / D_MODEL)

    def o_(dt):
        return (_sds((t_rows, D_MODEL), dt), _row(tm, D_MODEL))

    ins = [(a, _row(tm, D_FF)), (wdown, _resident(wdown.shape)), (x1, _row(tm, D_MODEL)), (g4, _full(g4.shape)),
           (p, _row(tm, PLE_DIM)), (wple, _full(wple.shape)), (g5, _full(g5.shape)), (wpg, _resident(wpg.shape)),
           (tgt, _row(tm, D_MODEL))]
    outs = [o_(F32), o_(F32), o_(F32), o_(BF16), o_(F32), o_(F32), (_sds((1, 1), F32), _full((1, 1)))]
    return _rows_call("fwd_out", body, t_rows, tm, ins, outs)


def _bwd_out(dx3, e, sg, x2, ff, g5, g4, wpg, wdown, up, convw8, convb, tm):
    t_rows = dx3.shape[0]
    cdim = 2 * D_FF
    hb = tm // 8

    def body(dx3_ref, e_ref, sg_ref, x2_ref, ff_ref, g5_ref, g4_ref, wpg_ref, wdown_ref, up_ref, halo_ref, cw_ref,
             cb_ref, dpre_ref, de_ref, dx2_ref, dff_ref, du_ref, dg5_ref, dg4_ref, dcb_ref, dcw_ref):
        i = pl.program_id(0)

        @pl.when(i == 0)
        def _():
            dg5_ref[...] = jnp.zeros(dg5_ref.shape, F32)
            dg4_ref[...] = jnp.zeros(dg4_ref.shape, F32)
            dcb_ref[...] = jnp.zeros(dcb_ref.shape, F32)
            dcw_ref[...] = jnp.zeros(dcw_ref.shape, F32)

        dx3 = dx3_ref[...]
        sg = sg_ref[...]
        dpre = (dx3 * e_ref[...] * sg * (1.0 - sg)).astype(BF16)
        dpre_ref[...] = dpre
        de_ref[...] = (dx3 * sg).astype(BF16)
        dn5 = _dot_nt(dpre, wpg_ref[...])
        x2n, r5 = _rms_stats(x2_ref[...])
        d2, dg5 = _rms_bwd(dn5, x2n, r5, g5_ref[...])
        dx2 = dx3 + d2
        dx2_ref[...] = dx2
        dg5_ref[...] += dg5
        ffn, r4 = _rms_stats(ff_ref[...])
        dff, dg4 = _rms_bwd(dx2, ffn, r4, g4_ref[...])
        dg4_ref[...] += dg4
        dffb = dff.astype(BF16)
        dff_ref[...] = dffb
        keep = jnp.where(i > 0, 1.0, 0.0)

        def conv(c0):
            sl = slice(c0, c0 + CONV_CHUNK)
            up = up_ref[:, sl]
            xm1, xm2 = _conv_taps(up, halo_ref[6:7, sl] * keep, halo_ref[7:8, sl] * keep)
            u = cw_ref[0:1, sl] * xm2 + cw_ref[1:2, sl] * xm1 + cw_ref[2:3, sl] * up + cb_ref[:, sl]
            return u, up, xm1, xm2

        def grads(c0, du, up, xm1, xm2):
            sl = slice(c0, c0 + CONV_CHUNK)
            du_ref[:, sl] = du.astype(BF16)
            dcb_ref[:, sl] += jnp.sum(du, axis=0, keepdims=True)
            dcw_ref[0:1, sl] += jnp.sum(du * xm2, axis=0, keepdims=True)
            dcw_ref[1:2, sl] += jnp.sum(du * xm1, axis=0, keepdims=True)
            dcw_ref[2:3, sl] += jnp.sum(du * up, axis=0, keepdims=True)

        for c0 in range(0, D_FF, CONV_CHUNK):
            da = _dot_nt(dffb, wdown_ref[c0:c0 + CONV_CHUNK, :])
            ug, *rg = conv(c0)
            uv, *rv = conv(D_FF + c0)
            gel, dgel = _gelu_and_grad(ug)
            grads(c0, da * uv * dgel, *rg)
            grads(D_FF + c0, da * gel, *rv)

    def o_(n, dt):
        return (_sds((t_rows, n), dt), _row(tm, n))

    def acc(r, n):
        return (_sds((r, n), F32), _full((r, n)))

    halo = pl.BlockSpec((8, cdim), lambda i: (jnp.maximum(i * hb - 1, 0), 0))
    ins = [(dx3, _row(tm, D_MODEL)), (e, _row(tm, D_MODEL)), (sg, _row(tm, D_MODEL)), (x2, _row(tm, D_MODEL)),
           (ff, _row(tm, D_MODEL)), (g5, _full(g5.shape)), (g4, _full(g4.shape)), (wpg, _resident(wpg.shape)),
           (wdown, _resident(wdown.shape)), (up, _row(tm, cdim)), (up, halo), (convw8, _full(convw8.shape)),
           (convb, _full(convb.shape))]
    outs = [o_(D_MODEL, BF16), o_(D_MODEL, BF16), o_(D_MODEL, F32), o_(D_MODEL, BF16), o_(cdim, BF16),
            acc(1, D_MODEL), acc(1, D_MODEL), acc(1, cdim), acc(8, cdim)]
    return _rows_call("bwd_out", body, t_rows, tm, ins, outs)


def _bwd_mid(du, convw8, wup, dx2, x1, g3, o, g2, wout, gate, pa, pb, wba, wbb, yb, tm):
    t_rows = du.shape[0]
    cdim = 2 * D_FF
    halo_rows = 16
    hb = tm // halo_rows
    last_blk = t_rows // halo_rows - 1
    n_tiles = t_rows // tm

    def body(du_ref, halo_ref, cw_ref, wup_ref, dx2_ref, x1_ref, g3_ref, o_ref, g2_ref, wout_ref, gate_ref, pa_ref,
             pb_ref, wba_ref, wbb_ref, yb_ref,
             dup_ref, dx1_ref, do_ref, dpa_ref, dpb_ref, dgt_ref, dya_ref, dyb_ref, dl_ref, dg3_ref, dg2_ref, dbg_ref):
        i = pl.program_id(0)

        @pl.when(i == 0)
        def _():
            dg3_ref[...] = jnp.zeros(dg3_ref.shape, F32)
            dg2_ref[...] = jnp.zeros(dg2_ref.shape, F32)
            dbg_ref[...] = jnp.zeros(dbg_ref.shape, F32)

        keep = jnp.where(i < n_tiles - 1, 1.0, 0.0)
        dh2 = jnp.zeros((tm, D_MODEL), F32)
        dups = []
        for c0 in range(0, cdim, CONV_CHUNK):
            sl = slice(c0, c0 + CONV_CHUNK)
            du = du_ref[:, sl].astype(F32)
            nxt = halo_ref[:, sl].astype(F32)
            xp1, xp2 = _conv_taps_next(du, nxt[0:1] * keep, nxt[1:2] * keep)
            dups.append((cw_ref[2:3, sl] * du + cw_ref[1:2, sl] * xp1 + cw_ref[0:1, sl] * xp2).astype(BF16))
            dup_ref[:, sl] = dups[-1]
            if len(dups) > 1:
                dh2 = dh2 + _dot_nt(dups[-2], wup_ref[len(dups) - 2])
        dh2 = dh2 + _dot_nt(dups[-1], wup_ref[len(dups) - 1])
        x1n, r3 = _rms_stats(x1_ref[...])
        d1, dg3 = _rms_bwd(dh2, x1n, r3, g3_ref[...])
        dx1 = dx2_ref[...] + d1
        dx1_ref[...] = dx1
        dg3_ref[...] += dg3
        on, r2 = _rms_stats(o_ref[...])
        do, dg2 = _rms_bwd(dx1, on, r2, g2_ref[...])
        dg2_ref[...] += dg2
        dob = do.astype(BF16)
        do_ref[...] = dob
        dmixed = _dot_nt(dob, wout_ref[...])
        ga = gate_ref[:, 0:D_MODEL]
        gb = gate_ref[:, D_MODEL:2 * D_MODEL]
        dpa = (dmixed * ga).astype(BF16)
        dpb = (dmixed * gb).astype(BF16)
        dpa_ref[...] = dpa
        dpb_ref[...] = dpb
        dga = dmixed * pa_ref[...] * ga * (1.0 - ga)
        dgb = dmixed * pb_ref[...] * gb * (1.0 - gb)
        dgt_ref[:, 0:D_MODEL] = dga.astype(BF16)
        dgt_ref[:, D_MODEL:2 * D_MODEL] = dgb.astype(BF16)
        dbg_ref[:, 0:D_MODEL] += jnp.sum(dga, axis=0, keepdims=True)
        dbg_ref[:, D_MODEL:2 * D_MODEL] += jnp.sum(dgb, axis=0, keepdims=True)
        dya_ref[...] = _dot_nt(dpa, wba_ref[...]).astype(BF16)
        dyb = _dot_nt(dpb, wbb_ref[...]).astype(BF16)
        dyb_ref[...] = dyb
        prod = yb_ref[...].astype(F32) * dyb.astype(F32)
        lane_head = lax.broadcasted_iota(jnp.int32, (HEADS, HEADS * LANES), 1) // LANES
        sel = (lane_head == lax.broadcasted_iota(jnp.int32, (HEADS, HEADS * LANES), 0)).astype(BF16)
        hi = prod.astype(BF16)
        lo = (prod - hi.astype(F32)).astype(BF16)
        dl_ref[...] = _dot_nt(sel, hi) + _dot_nt(sel, lo)

    def o_(n, dt):
        return (_sds((t_rows, n), dt), _row(tm, n))

    def acc(r, n):
        return (_sds((r, n), F32), _full((r, n)))

    halo = pl.BlockSpec((halo_rows, cdim), lambda i: (jnp.minimum((i + 1) * hb, last_blk), 0))
    ins = [(du, _row(tm, cdim)), (du, halo), (convw8, _full(convw8.shape)), (wup, _resident(wup.shape)),
           (dx2, _row(tm, D_MODEL)), (x1, _row(tm, D_MODEL)), (g3, _full(g3.shape)), (o, _row(tm, D_MODEL)),
           (g2, _full(g2.shape)), (wout, _resident(wout.shape)), (gate, _row(tm, 2048)), (pa, _row(tm, D_MODEL)),
           (pb, _row(tm, D_MODEL)), (wba, _resident(wba.shape)), (wbb, _resident(wbb.shape)), (yb, _row(tm, 1024))]
    outs = [o_(cdim, BF16), o_(D_MODEL, F32), o_(D_MODEL, BF16), o_(D_MODEL, BF16), o_(D_MODEL, BF16),
            o_(2048, BF16), o_(1024, BF16), o_(1024, BF16),
            (_sds((HEADS, t_rows), F32), pl.BlockSpec((HEADS, tm), lambda i: (0, i))),
            acc(1, D_MODEL), acc(1, D_MODEL), acc(1, 2048)]
    return _rows_call("bwd_mid", body, t_rows, tm, ins, outs)


def _bwd_in(dqs, dks, dvs, dqm, dkm, dvm, tabs, consts, cq, ckv, gq, gkv, wuq, wk, wv, dgates, win, x, g1, dx1, tm):
    t_rows = x.shape[0]

    def body(dqs_ref, dks_ref, dvs_ref, dqm_ref, dkm_ref, dvm_ref, ca, sa1, sa2, cb, sb1, sb2, c_ref, cq_ref,
             ckv_ref, gq_ref, gkv_ref, wuq_ref, wk_ref, wv_ref, dgt_ref, win_ref, x_ref, g1_ref, dx1_ref,
             dz_ref, dqb_ref, dx_ref, dgq_ref, dgkv_ref, dg1_ref):
        i = pl.program_id(0)

        @pl.when(i == 0)
        def _():
            dgq_ref[...] = jnp.zeros(dgq_ref.shape, F32)
            dgkv_ref[...] = jnp.zeros(dgkv_ref.shape, F32)
            dg1_ref[...] = jnp.zeros(dg1_ref.shape, F32)

        ta = (ca[...], sa1[...], sa2[...])
        tb = (cb[...], sb1[...], sb2[...])

        def piece(lo, hi, val):
            dz_ref[:, lo:hi] = val
            return _dot_nt(val, win_ref[:, lo:hi])

        dh1 = piece(Z_GATE, ZW, dgt_ref[...])
        dkm = dkm_ref[...]
        dckvn = _dot_nt(dkm.astype(BF16), wk_ref[...]) + _dot_nt(dvm_ref[...].astype(BF16), wv_ref[...])
        dh1 = dh1 + piece(Z_VA, Z_CQ, dvs_ref[...].astype(BF16))
        dqm = jnp.concatenate([dqm_ref[h] for h in range(HEADS)], axis=1)
        dqb = _rope_t(dqm * SCALE_B, *tb, ROPE_DIM // 2).astype(BF16)
        dqb_ref[...] = dqb
        dcqn = _dot_nt(dqb, wuq_ref[...])
        dh1 = dh1 + piece(Z_QA, Z_KA, _rope_t(dqs_ref[...] * SCALE_A, *ta, A_HEAD_DIM // 2).astype(BF16))
        dh1 = dh1 + piece(Z_KA, Z_VA, _rope_t(dks_ref[...], *ta, A_HEAD_DIM // 2).astype(BF16))
        ckvn, rkv = _rms_stats(ckv_ref[...])
        dckv, dgkv = _rms_bwd(dckvn, ckvn, rkv, gkv_ref[...])
        dgkv_ref[...] += dgkv
        dh1 = dh1 + piece(Z_CKV, Z_KR, dckv.astype(BF16))
        dslot = dkm[:, 0:LANES]
        for h in range(1, HEADS):
            dslot = dslot + dkm[:, h * LANES:(h + 1) * LANES]
        dh1 = dh1 + piece(Z_KR, Z_GATE, _rope_t(dslot * c_ref[10:11, :], *tb, ROPE_DIM // 2).astype(BF16))
        cqn, rq = _rms_stats(cq_ref[...])
        dcq, dgq = _rms_bwd(dcqn, cqn, rq, gq_ref[...])
        dgq_ref[...] += dgq
        dh1 = dh1 + piece(Z_CQ, Z_CKV, dcq.astype(BF16))
        xn, r1 = _rms_stats(x_ref[...])
        d0, dg1 = _rms_bwd(dh1, xn, r1, g1_ref[...])
        dg1_ref[...] += dg1
        dx_ref[...] = dx1_ref[...] + d0

    def acc(n):
        return (_sds((1, n), F32), _full((1, n)))

    ins = [(dqs, _row(tm, 1024)), (dks, _row(tm, 256)), (dvs, _row(tm, 256)), (dqm, _heads(tm, HEADS)),
           (dkm, _row(tm, 1024)), (dvm, _row(tm, 1024))] + [(t, _row(tm, LANES)) for t in tabs] + [
           (consts, _full(consts.shape)), (cq, _row(tm, 256)), (ckv, _row(tm, 128)), (gq, _full(gq.shape)),
           (gkv, _full(gkv.shape)), (wuq, _full(wuq.shape)), (wk, _full(wk.shape)), (wv, _full(wv.shape)),
           (dgates, _row(tm, 2048)), (win, _resident(win.shape)), (x, _row(tm, D_MODEL)), (g1, _full(g1.shape)),
           (dx1, _row(tm, D_MODEL))]
    outs = [(_sds((t_rows, ZW), BF16), _row(tm, ZW)), (_sds((t_rows, 1024), BF16), _row(tm, 1024)),
            (_sds((t_rows, D_MODEL), F32), _row(tm, D_MODEL)), acc(256), acc(128), acc(D_MODEL)]
    return _rows_call("bwd_in", body, t_rows, tm, ins, outs)


def _pick_cols(n):
    best = LANES
    for d in range(LANES, min(n, 1408) + 1, LANES):
        if n % d == 0:
            best = d
    return best


def _mm_tn(name, a, b, column_shards=1, after=None):
    t_rows, m = a.shape
    n = b.shape[1]
    bk = min(1024, t_rows)
    bm, bn = _pick_cols(m), _pick_cols(n // column_shards)
    per_shard = n // column_shards // bn
    extra = () if after is None else (after,)

    def body(a_ref, b_ref, *rest):
        o_ref = rest[-1]

        @pl.when(pl.program_id(2) == 0)
        def _():
            o_ref[...] = jnp.zeros((bm, bn), F32)

        o_ref[...] += _dot_tn(a_ref[...].astype(BF16), b_ref[...].astype(BF16))

    return pl.pallas_call(
        body, name=name, grid=(m // bm, n // bn, t_rows // bk),
        in_specs=[pl.BlockSpec((bk, bm), lambda i, j, k: (k, i)), pl.BlockSpec((bk, bn), lambda i, j, k: (k, j))]
        + [pl.BlockSpec((8, LANES), lambda i, j, k: (0, 0))] * len(extra),
        out_specs=(pl.BlockSpec((bm, bn), lambda i, j, k: (i, j)) if column_shards == 1 else
                   pl.BlockSpec((None, bm, bn), lambda i, j, k: (j // per_shard, i, j % per_shard))),
        out_shape=_sds((m, n) if column_shards == 1 else (column_shards, m, n // column_shards), F32),
        compiler_params=pltpu.CompilerParams(dimension_semantics=("arbitrary",) * 3, vmem_limit_bytes=VMEM_LIMIT),
    )(a, b, *extra)


PACK_ROWS = 512


ADD_TILE_ELEMS = 1 << 17


def _add_rows(rows, cols):
    best = 16
    for d in range(16, rows + 1, 16):
        if rows % d == 0 and d * cols <= ADD_TILE_ELEMS:
            best = d
    assert rows % best == 0
    return best


def _add_pair(name, g, recv, half):
    _, _, rows, cols = g.shape
    t = _add_rows(rows, cols)

    def body(h_ref, g_ref, r_ref, o_ref):
        o_ref[...] = (g_ref[:, 0] + r_ref[...]).astype(BF16)

    spec = pl.BlockSpec((4, t, cols), lambda i, h: (0, i, 0))
    grid_spec = pltpu.PrefetchScalarGridSpec(
        num_scalar_prefetch=1, grid=(rows // t,),
        in_specs=[pl.BlockSpec((4, 1, t, cols), lambda i, h: (0, h[0], i, 0)), spec], out_specs=spec)
    return pl.pallas_call(body, name=name, grid_spec=grid_spec,
                          out_shape=_sds(recv.shape, BF16))(jnp.reshape(half, (1,)).astype(jnp.int32), g, recv)


def _add_chips(name, parts):
    _, rows, cols = parts.shape
    t = _add_rows(rows, cols)

    def body(p_ref, o_ref):
        acc = p_ref[0].astype(F32)
        for j in range(1, 4):
            acc = acc + p_ref[j].astype(F32)
        o_ref[...] = acc

    return pl.pallas_call(body, name=name, grid=(rows // t,),
                          in_specs=[pl.BlockSpec((4, t, cols), lambda i: (0, i, 0))],
                          out_specs=pl.BlockSpec((t, cols), lambda i: (i, 0)),
                          out_shape=_sds((rows, cols), F32))(parts)


def _add_devices(parts):
    n, rows, _ = parts.shape

    def body(p_ref, o_ref):
        acc = p_ref[0]
        for j in range(1, n):
            acc = acc + p_ref[j]
        o_ref[...] = acc

    return pl.pallas_call(body, name="small_add", grid=(1,),
                          in_specs=[pl.BlockSpec((n, rows, LANES), lambda i: (0, 0, 0))],
                          out_specs=pl.BlockSpec((rows, LANES), lambda i: (0, 0)),
                          out_shape=_sds((rows, LANES), F32))(parts)


def _adam_rows(k, n):
    target = max(8, (1 << 20) // (4 * n))
    if k <= target:
        return k
    best = None
    for d in range(8, target + 1, 8):
        if k % d == 0:
            best = d
    return best if best is not None else k


def _adam_update(w, g, m, v):
    m_ = ADAM_B1 * m + (1.0 - ADAM_B1) * g
    v_ = ADAM_B2 * v + (1.0 - ADAM_B2) * (g * g)
    delta = -ADAM_LR * ((m_ / (1.0 - ADAM_B1 ** ADAM_STEP)) / (jnp.sqrt(v_ / (1.0 - ADAM_B2 ** ADAM_STEP)) + ADAM_EPS)
                        + ADAM_WD * w)
    return delta, m_, v_


def _adamw(name, w, g, m, v):
    k, n = w.shape
    bk = _adam_rows(k, n)

    def body(w_ref, g_ref, m_ref, v_ref, d_ref, mo_ref, vo_ref):
        d_ref[...], mo_ref[...], vo_ref[...] = _adam_update(w_ref[...], g_ref[...], m_ref[...], v_ref[...])

    spec = pl.BlockSpec((bk, n), lambda i: (i, 0))
    out = pl.pallas_call(body, name=name, grid=(k // bk,), in_specs=[spec] * 4, out_specs=[spec] * 3,
                         out_shape=[_sds((k, n), F32)] * 3,
                         compiler_params=pltpu.CompilerParams(vmem_limit_bytes=VMEM_LIMIT))(w, g, m, v)
    return (g, *out)


def _adamw_halves(name, w, mine, theirs, m, v, half):
    k, n = w.shape
    bk = _adam_rows(k // 2, n)
    nb = k // 2 // bk

    def body(h_ref, w_ref, mine_ref, theirs_ref, m_ref, v_ref, g_ref, d_ref, mo_ref, vo_ref):
        g = jnp.where(pl.program_id(0) == h_ref[0], mine_ref[...], theirs_ref[...])
        g_ref[...] = g
        d_ref[...], mo_ref[...], vo_ref[...] = _adam_update(w_ref[...], g, m_ref[...], v_ref[...])

    full = pl.BlockSpec((bk, n), lambda h, i, c: (h * nb + i, 0))
    part = pl.BlockSpec((bk, n), lambda h, i, c: (i, 0))
    grid_spec = pltpu.PrefetchScalarGridSpec(num_scalar_prefetch=1, grid=(2, nb),
                                             in_specs=[full, part, part, full, full], out_specs=[full] * 4)
    return tuple(pl.pallas_call(
        body, name=name, grid_spec=grid_spec, out_shape=[_sds((k, n), F32)] * 4,
        compiler_params=pltpu.CompilerParams(vmem_limit_bytes=VMEM_LIMIT),
    )(jnp.reshape(half, (1,)).astype(jnp.int32), w, mine, theirs, m, v))


_HBM = pl.BlockSpec(memory_space=pltpu.HBM)


def _me():
    return lax.axis_index("x"), lax.axis_index("y"), lax.axis_index("c")


def _other_chips(x, y):
    return [(1 - x, y), (x, 1 - y), (1 - x, 1 - y)]


def _pass_to_sibling(zones):
    n = len(zones)

    def body(*refs):
        in_refs, out_refs = refs[:n], refs[n:2 * n]
        send_sems, recv_sems = refs[2 * n:]
        x, y, c = _me()
        sent = []
        for a, (in_ref, out_ref) in enumerate(zip(in_refs, out_refs)):
            for j, (cx, cy) in enumerate(_other_chips(x, y)):
                mine, theirs = (2 * cx + cy, c), (2 * cx + cy, 1 - c)
                sems = dict(send_sem=send_sems.at[3 * a + j], recv_sem=recv_sems.at[3 * a + j],
                            device_id=(x, y, 1 - c), device_id_type=MESH)
                sent.append((pltpu.make_async_remote_copy(src_ref=in_ref.at[mine], dst_ref=out_ref.at[mine], **sems),
                             pltpu.make_async_remote_copy(src_ref=in_ref.at[theirs], dst_ref=out_ref.at[theirs], **sems)))
        for send, _ in sent:
            send.start()
        for _, recv in sent:
            recv.wait_recv()
        for send, _ in sent:
            send.wait_send()

    return pl.pallas_call(
        body, name="pass_to_sibling", out_shape=[_sds(z.shape, z.dtype) for z in zones],
        in_specs=[_HBM] * n, out_specs=[_HBM] * n, input_output_aliases={i: i for i in range(n)},
        scratch_shapes=[pltpu.SemaphoreType.DMA((3 * n,)), pltpu.SemaphoreType.DMA((3 * n,))],
    )(*zones)


def _swap_sibling(name, vs, other_half=False):
    n = len(vs)

    def body(*refs):
        v_refs, out_refs = refs[:n], refs[n:2 * n]
        send_sems, recv_sems = refs[2 * n:]
        x, y, c = _me()
        cps = [pltpu.make_async_remote_copy(src_ref=v_ref.at[:, 1 - c] if other_half else v_ref, dst_ref=out_ref,
                                            send_sem=send_sems.at[a], recv_sem=recv_sems.at[a],
                                            device_id=(x, y, 1 - c), device_id_type=MESH)
               for a, (v_ref, out_ref) in enumerate(zip(v_refs, out_refs))]
        for cp in cps:
            cp.start()
        for cp in cps:
            cp.wait()

    def landing(v):
        return _sds((v.shape[0],) + v.shape[2:] if other_half else v.shape, v.dtype)

    return pl.pallas_call(
        body, name=name, out_shape=[landing(v) for v in vs], in_specs=[_HBM] * n, out_specs=[_HBM] * n,
        scratch_shapes=[pltpu.SemaphoreType.DMA((n,)), pltpu.SemaphoreType.DMA((n,))],
    )(*vs)


_SEM = pl.BlockSpec(memory_space=pltpu.SEMAPHORE)
_EFFECT = pltpu.SideEffectType.DATAFLOW_SIDE_EFFECTING
WHOLE = "whole"
PIECE = "piece"
SIBLING_HALF = "sibling"
MY_HALF = "half"
EVERYONE = "everyone"
_COPIES = {WHOLE: 3, PIECE: 3, MY_HALF: 3, SIBLING_HALF: 1, EVERYONE: 7}


def _landing_shape(v, mode):
    return {WHOLE: (4,) + v.shape, MY_HALF: (4,) + v.shape, PIECE: v.shape, EVERYONE: (8,) + v.shape,
            SIBLING_HALF: (v.shape[0],) + v.shape[2:]}[mode]


def _chip_copies(v_ref, land_ref, send_sems, recv_sems, mode, sem0=0):
    x, y, c = _me()
    if mode == SIBLING_HALF:
        cp = pltpu.make_async_remote_copy(src_ref=v_ref.at[:, 1 - c], dst_ref=land_ref, send_sem=send_sems.at[sem0],
                                          recv_sem=recv_sems.at[sem0], device_id=(x, y, 1 - c), device_id_type=MESH)
        return [(cp, cp)]
    if mode == EVERYONE:
        out = []
        for f in range(1, 8):
            px, py, pc = (1 - x if f & 4 else x), (1 - y if f & 2 else y), (1 - c if f & 1 else c)
            sems = dict(send_sem=send_sems.at[sem0 + f - 1], recv_sem=recv_sems.at[sem0 + f - 1],
                        device_id=(px, py, pc), device_id_type=MESH)
            out.append((pltpu.make_async_remote_copy(src_ref=v_ref, dst_ref=land_ref.at[4 * x + 2 * y + c], **sems),
                        pltpu.make_async_remote_copy(src_ref=v_ref, dst_ref=land_ref.at[4 * px + 2 * py + pc], **sems)))
        return out
    k = 2 * x + y
    out = []
    for j, (cx, cy) in enumerate(_other_chips(x, y)):
        if mode == MY_HALF:
            src, mine, theirs = v_ref.at[c], land_ref.at[k, c], land_ref.at[2 * cx + cy, c]
        else:
            src = v_ref.at[2 * cx + cy] if mode == PIECE else v_ref
            mine, theirs = land_ref.at[k], land_ref.at[2 * cx + cy]
        sems = dict(send_sem=send_sems.at[sem0 + j], recv_sem=recv_sems.at[sem0 + j], device_id=(cx, cy, c),
                    device_id_type=MESH)
        send = pltpu.make_async_remote_copy(src_ref=src, dst_ref=mine, **sems)
        recv = pltpu.make_async_remote_copy(src_ref=src, dst_ref=theirs, **sems)
        out.append((send, recv))
    return out


def _chips_start(name, vs, mode, after=None):
    n = len(vs)
    lands = [_landing_shape(v, mode) for v in vs]

    def body(*refs):
        v_refs, land_refs = refs[:n], refs[n:2 * n]
        send_sems, recv_sems = refs[-2 * n - 3], refs[-2 * n - 2]
        token = refs[-1]
        for a in range(n):
            for send, _ in _chip_copies(v_refs[a], land_refs[a], send_sems, recv_sems, mode, _COPIES[mode] * a):
                send.start()
        token[...] = jnp.zeros_like(token)

    extra = () if after is None else (after,)
    hbm = [pltpu.with_memory_space_constraint(v, pltpu.HBM) for v in vs]
    zones = [pltpu.with_memory_space_constraint(lax.empty(s, v.dtype), pltpu.HBM) for s, v in zip(lands, vs)]
    out = pl.pallas_call(
        body, name=name,
        out_shape=(pltpu.SemaphoreType.DMA((_COPIES[mode] * n,)), pltpu.SemaphoreType.DMA((_COPIES[mode] * n,)),
                   *[pltpu.HBM(v.shape, v.dtype) for v in vs], *[pltpu.HBM(s, v.dtype) for s, v in zip(lands, vs)],
                   _sds((8, LANES), F32)),
        in_specs=(_HBM,) * (2 * n) + (pl.BlockSpec(memory_space=pl.ANY),) * len(extra),
        out_specs=(_SEM, _SEM) + (_HBM,) * (2 * n) + (pl.BlockSpec(memory_space=pltpu.VMEM),),
        input_output_aliases={i: 2 + i for i in range(2 * n)},
        compiler_params=pltpu.CompilerParams(has_side_effects=_EFFECT),
    )(*hbm, *zones, *extra)
    return out[0], out[1], list(out[2:2 + n]), list(out[2 + n:2 + 2 * n]), out[-1]


def _chips_wait(name, send_sems, recv_sems, v_thru, land_thru, mode, after):
    n = len(v_thru)

    def body(*refs):
        v_refs, land_refs = refs[:n], refs[n:2 * n]
        send_sems, recv_sems = refs[2 * n], refs[2 * n + 1]
        for a in range(n):
            for send, recv in _chip_copies(v_refs[a], land_refs[a], send_sems, recv_sems, mode, _COPIES[mode] * a):
                send.wait_send()
                recv.wait_recv()

    out = pl.pallas_call(
        body, name=name,
        out_shape=tuple(pltpu.HBM(a.shape, a.dtype) for a in list(v_thru) + list(land_thru)),
        in_specs=(_HBM,) * (2 * n) + (_SEM, _SEM, pl.BlockSpec(memory_space=pl.ANY)), out_specs=(_HBM,) * (2 * n),
        input_output_aliases={i: i for i in range(2 * n)},
        compiler_params=pltpu.CompilerParams(has_side_effects=_EFFECT),
    )(*v_thru, *land_thru, send_sems, recv_sems, after)
    return list(out[:n]), list(out[n:])


_BIG = (("w_in", (1024, 3232), 1), ("w_uq", (256, 768), 1), ("w_ukv", (128, 1024), 1), ("w_branch_a", (512, 1024), 1),
        ("w_branch_b", (512, 1024), 1), ("w_out", (1024, 1024), 0), ("w_up", (1024, 5632), 1),
        ("w_down", (2816, 1024), 0), ("w_ple_gate", (1024, 1024), 0), ("w_ple", (256, 1024), 1))


def _shard_shape(shape, axis):
    return (shape[0] // 4, shape[1]) if axis == 0 else (shape[0], shape[1] // 4)


def _half_rows(shape, axis):
    k, n = _shard_shape(shape, axis)
    return k * n // (2 * LANES)


_EARLY = ("w_in", "w_uq", "w_ukv")
_LATE = ("w_branch_a", "w_branch_b", "w_out", "w_up", "w_down", "w_ple_gate", "w_ple")
_NATURAL = ("w_in", "w_up", "w_down", "w_out", "w_ple_gate")
_EARLY_PACKED = tuple(b for b in _BIG if b[0] in _EARLY and b[0] not in _NATURAL)
_LATE_PACKED = tuple(b for b in _BIG if b[0] in _LATE and b[0] not in _NATURAL)
_SHARD = {name: _shard_shape(shape, axis) for name, shape, axis in _BIG}


def _halves(a):
    return a.reshape(a.shape[:-2] + (2, a.shape[-2] // 2, a.shape[-1]))


def _rows_joined(a):
    return a.reshape(a.shape[:-3] + (a.shape[-3] * a.shape[-2], a.shape[-1]))


def _pack_pad(group):
    return -sum(_half_rows(shape, axis) for _, shape, axis in group) % PACK_ROWS


def _pack_shards(shards, dtype, group):
    parts = [shards[name].astype(dtype).reshape(2, _half_rows(shape, axis), LANES) for name, shape, axis in group]
    return jnp.concatenate(parts + [jnp.zeros((2, _pack_pad(group), LANES), dtype)], axis=1)


def _unpack_gathered(g, group):
    out, off = {}, 0
    for name, shape, axis in group:
        r = _half_rows(shape, axis)
        k, n = _shard_shape(shape, axis)
        w = g[:, :, off:off + r, :].reshape(4, k, n)
        out[name] = w.reshape(shape) if axis == 0 else w.transpose(1, 0, 2).reshape(shape)
        off += r
    return out


def _pack_grads(grads, group):
    parts = []
    for name, shape, axis in group:
        k, n = _shard_shape(shape, axis)
        g = grads[name]
        g4 = g.reshape(4, k, n) if axis == 0 else g.reshape(k, 4, n).transpose(1, 0, 2)
        parts.append(g4.reshape(4, 2, _half_rows(shape, axis), LANES))
    return jnp.concatenate(parts + [jnp.zeros((4, 2, _pack_pad(group), LANES), F32)], axis=2)


def _unpack_shard_grads(f, group):
    out, off = {}, 0
    for name, shape, axis in group:
        r = _half_rows(shape, axis)
        out[name] = f[:, off:off + r, :].reshape(_shard_shape(shape, axis))
        off += r
    return out


def _pad_slots(w, heads, dim, axis):
    if axis == 1:
        k = w.shape[0]
        return jnp.pad(w.reshape(k, heads, dim), ((0, 0), (0, 0), (0, LANES - dim))).reshape(k, heads * LANES)
    n = w.shape[1]
    return jnp.pad(w.reshape(heads, dim, n), ((0, 0), (0, LANES - dim), (0, 0))).reshape(heads * LANES, n)


def _unpad_slots(w, heads, dim, axis):
    if axis == 1:
        k = w.shape[0]
        return w.reshape(k, heads, LANES)[:, :, :dim].reshape(k, heads * dim)
    n = w.shape[1]
    return w.reshape(heads, LANES, n)[:, :dim, :].reshape(heads * dim, n)


def _pad_w_in(w):
    kr = jnp.pad(w[:, 1152:1184], ((0, 0), (NOPE_DIM, LANES - NOPE_DIM - ROPE_DIM)))
    return jnp.concatenate([_pad_slots(w[:, 0:512], HEADS, A_HEAD_DIM, 1),
                            _pad_slots(w[:, 512:640], A_KV_HEADS, A_HEAD_DIM, 1),
                            _pad_slots(w[:, 640:768], A_KV_HEADS, A_HEAD_DIM, 1),
                            w[:, 768:1024], w[:, 1024:1152], kr, w[:, 1184:3232]], axis=1)


def _unpad_w_in(w):
    return jnp.concatenate([_unpad_slots(w[:, Z_QA:Z_KA], HEADS, A_HEAD_DIM, 1),
                            _unpad_slots(w[:, Z_KA:Z_VA], A_KV_HEADS, A_HEAD_DIM, 1),
                            _unpad_slots(w[:, Z_VA:Z_CQ], A_KV_HEADS, A_HEAD_DIM, 1),
                            w[:, Z_CQ:Z_CKV], w[:, Z_CKV:Z_KR],
                            w[:, Z_KR + NOPE_DIM:Z_KR + NOPE_DIM + ROPE_DIM], w[:, Z_GATE:ZW]], axis=1)


_SMALL = (("attn_pre_norm", 1024), ("attn_post_norm", 1024), ("b_gate", 2048), ("sinks", 8), ("q_a_norm", 256),
          ("kv_a_norm", 128), ("mlp_pre_norm", 1024), ("mlp_post_norm", 1024), ("conv_b", 5632), ("ple_norm", 1024),
          ("conv_w", 3 * 5632), ("loss", 1))


def _small_rows(n):
    return 8 * -(-n // (8 * LANES))


def _pack_small(vals):
    parts = []
    for name, n in _SMALL:
        r = _small_rows(n)
        parts.append(jnp.pad(vals[name].reshape(-1), (0, r * LANES - n)).reshape(r, LANES))
    return jnp.concatenate(parts, axis=0)


def _unpack_small(buf):
    out, off = {}, 0
    for name, n in _SMALL:
        r = _small_rows(n)
        out[name] = buf[off:off + r].reshape(-1)[:n]
        off += r
    return out


def kernel(x, p, positions, attn_pre_norm, attn_post_norm, w_in, b_gate, sinks, q_a_norm, w_uq, kv_a_norm, w_ukv, w_branch_a, w_branch_b, w_out, mlp_pre_norm, mlp_post_norm, w_up, conv_w, conv_b, w_down, ple_norm, w_ple_gate, w_ple, loss_target, m_attn_pre_norm, m_attn_post_norm, m_w_in, m_b_gate, m_sinks, m_q_a_norm, m_w_uq, m_kv_a_norm, m_w_ukv, m_w_branch_a, m_w_branch_b, m_w_out, m_mlp_pre_norm, m_mlp_post_norm, m_w_up, m_conv_w, m_conv_b, m_w_down, m_ple_norm, m_w_ple_gate, m_w_ple, v_attn_pre_norm, v_attn_post_norm, v_w_in, v_b_gate, v_sinks, v_q_a_norm, v_w_uq, v_kv_a_norm, v_w_ukv, v_w_branch_a, v_w_branch_b, v_w_out, v_mlp_pre_norm, v_mlp_post_norm, v_w_up, v_conv_w, v_conv_b, v_w_down, v_ple_norm, v_w_ple_gate, v_w_ple):
    names = ["attn_pre_norm", "attn_post_norm", "w_in", "b_gate", "sinks", "q_a_norm", "w_uq", "kv_a_norm", "w_ukv",
             "w_branch_a", "w_branch_b", "w_out", "mlp_pre_norm", "mlp_post_norm", "w_up", "conv_w", "conv_b",
             "w_down", "ple_norm", "w_ple_gate", "w_ple"]
    wts = dict(zip(names, [attn_pre_norm, attn_post_norm, w_in, b_gate, sinks, q_a_norm, w_uq, kv_a_norm, w_ukv,
                           w_branch_a, w_branch_b, w_out, mlp_pre_norm, mlp_post_norm, w_up, conv_w, conv_b, w_down,
                           ple_norm, w_ple_gate, w_ple]))
    moms = dict(zip(names, [m_attn_pre_norm, m_attn_post_norm, m_w_in, m_b_gate, m_sinks, m_q_a_norm, m_w_uq,
                            m_kv_a_norm, m_w_ukv, m_w_branch_a, m_w_branch_b, m_w_out, m_mlp_pre_norm,
                            m_mlp_post_norm, m_w_up, m_conv_w, m_conv_b, m_w_down, m_ple_norm, m_w_ple_gate, m_w_ple]))
    vars_ = dict(zip(names, [v_attn_pre_norm, v_attn_post_norm, v_w_in, v_b_gate, v_sinks, v_q_a_norm, v_w_uq,
                             v_kv_a_norm, v_w_ukv, v_w_branch_a, v_w_branch_b, v_w_out, v_mlp_pre_norm,
                             v_mlp_post_norm, v_w_up, v_conv_w, v_conv_b, v_w_down, v_ple_norm, v_w_ple_gate, v_w_ple]))
    w2 = {n: a.reshape(a.shape[-2:]) for n, a in wts.items()}
    m2 = {n: a.reshape(a.shape[-2:]) for n, a in moms.items()}
    v2 = {n: a.reshape(a.shape[-2:]) for n, a in vars_.items()}

    t_rows = x.shape[-2]
    tm = min(256, t_rows)
    tm_wide = min(512, t_rows)
    xc, yc, cc = lax.axis_index("x"), lax.axis_index("y"), lax.axis_index("c")
    chip = 2 * xc + yc

    x2d = x.reshape(t_rows, D_MODEL)
    p2d = p.reshape(t_rows, PLE_DIM)
    tgt = loss_target.reshape(t_rows, D_MODEL)
    pos_f = positions.reshape(t_rows, 1).astype(F32)

    def own_slot_filled(gathered, mine):
        return [lax.dynamic_update_slice(g, m[None], (chip, 0, 0, 0)) for g, m in zip(gathered, mine)]

    def shard_lists(group, packed_group, token=0.0):
        ws = {n: w2[n] + token for n in group}
        return [_halves(ws[n].astype(BF16)) for n in group if n in _NATURAL] + [_pack_shards(ws, BF16, packed_group)]

    cw_rows = 3 * 1408 // LANES
    conv_mine = jnp.pad(w2["conv_w"].reshape(cw_rows, LANES), ((0, 48 - cw_rows), (0, 0))).reshape(2, 24, LANES)
    early_mine = shard_lists(_EARLY, _EARLY_PACKED) + [conv_mine]
    early_sems = _chips_start("gather_early_start", early_mine, MY_HALF)
    early_token = early_sems[4][0:1, 0:1]
    consts = _rope_consts()
    tabs = _rope_tables(pos_f + early_token, consts, tm)
    late_mine = shard_lists(_LATE, _LATE_PACKED, early_token)
    both_done = tabs[0][0:1, 0:1] + sum(m[0, 0:1, 0:1].astype(F32) for m in late_mine)
    early_sent, early_landed = _chips_wait("gather_early_wait", *early_sems[:4], MY_HALF, after=both_done)
    early = own_slot_filled(_pass_to_sibling(early_landed), early_sent)
    late_sems = _chips_start("gather_late_start", late_mine, WHOLE, after=early[0])
    late_token = late_sems[4][0:1, 0:1]
    full = _unpack_gathered(early[1], _EARLY_PACKED)
    full["w_in"] = _rows_joined(early[0]).transpose(1, 0, 2).reshape(D_MODEL, 3232)
    conv_full = early[2].reshape(4, 48, LANES)[:, :cw_rows].reshape(4, 3, 1408).transpose(1, 0, 2).reshape(3, 2 * D_FF)
    convw8 = jnp.pad(conv_full, ((0, 5), (0, 0)))

    win = _pad_w_in(full["w_in"])
    wuq = _pad_slots(full["w_uq"], HEADS, NOPE_DIM + ROPE_DIM, 1)
    ukv = full["w_ukv"].reshape(KV_LORA, HEADS, NOPE_DIM + V_DIM)
    wk = _pad_slots(ukv[:, :, :NOPE_DIM].reshape(KV_LORA, HEADS * NOPE_DIM), HEADS, NOPE_DIM, 1)
    wv = _pad_slots(ukv[:, :, NOPE_DIM:].reshape(KV_LORA, HEADS * V_DIM), HEADS, V_DIM, 1)
    g1, g2, g3, g4, g5 = (w2["attn_pre_norm"], w2["attn_post_norm"], w2["mlp_pre_norm"], w2["mlp_post_norm"],
                          w2["ple_norm"])
    gq, gkv, bg, convb = w2["q_a_norm"], w2["kv_a_norm"], w2["b_gate"], w2["conv_b"]
    swa_tile = min(SWA_TILE, t_rows)
    sink_rows = jnp.repeat(w2["sinks"].reshape(A_KV_HEADS, SWA_GROUP, 1), swa_tile, axis=2).reshape(
        A_KV_HEADS, 1, SWA_GROUP * swa_tile)
    swa_bias = _swa_bias(swa_tile)

    h1, qs, ks, vs, cq, cqn, ckv, ckvn, qm, km, vm, gate = _fwd_in(x2d, g1, win, bg + late_token, gq, gkv, wuq, wk, wv,
                                                                   tabs, tm_wide)
    ya, lse_a = _swa_fwd(qs, ks, vs, swa_bias, sink_rows)
    yb, lse_b = _mla_fwd(qm, km, vm)
    late_sent, late_landed = _chips_wait("gather_late_wait", *late_sems[:4], WHOLE, after=yb)
    late = own_slot_filled(late_landed, late_sent)
    full = _unpack_gathered(late[-1], _LATE_PACKED)
    wba = _pad_slots(full["w_branch_a"], HEADS, A_HEAD_DIM, 0)
    wbb = _pad_slots(full["w_branch_b"], HEADS, V_DIM, 0)
    wple = full["w_ple"]
    natural = dict(zip([n for n in _LATE if n in _NATURAL], late))
    wup = _rows_joined(natural["w_up"])
    wout, wdown, wpg = (_rows_joined(natural[n]).reshape(-1, D_MODEL) for n in ("w_out", "w_down", "w_ple_gate"))
    pa, pb, mixed, o, x1, h2 = _fwd_mix(x2d, ya, yb, gate, wba, wbb, wout, g2, g3, tm_wide)
    up, a = _fwd_up(h2, wup, convw8, convb, tm)
    ff, x2, e, n5, sg, dx3, loss_part = _fwd_out(a, wdown, x1, g4, p2d, wple, g5, wpg, tgt, tm_wide)

    dpre, de, dx2, dff, du, dg5, dg4, dconvb, dconvw8 = _bwd_out(dx3, e, sg, x2, ff, g5, g4, wpg, wdown, up, convw8,
                                                                 convb, tm)
    dup, dx1, do, dpa, dpb, dgates, dya, dyb, delta_b, dg3, dg2, dbg = _bwd_mid(
        du, convw8, wup, dx2, x1, g3, o, g2, wout, gate, pa, pb, wba, wbb, yb, tm)
    late_grads = {
        "w_branch_a": _unpad_slots(_mm_tn("dw_branch_a", ya, dpa), HEADS, A_HEAD_DIM, 0),
        "w_branch_b": _unpad_slots(_mm_tn("dw_branch_b", yb, dpb), HEADS, V_DIM, 0),
        "w_out": _mm_tn("dw_out", mixed, do).reshape(4, D_MODEL // 4, D_MODEL),
        "w_up": _mm_tn("dw_up", h2, dup, column_shards=4),
        "w_down": _mm_tn("dw_down", a, dff).reshape(4, D_FF // 4, D_MODEL),
        "w_ple_gate": _mm_tn("dw_ple_gate", n5, dpre).reshape(4, D_MODEL // 4, D_MODEL),
        "w_ple": _mm_tn("dw_ple", p2d, de),
    }

    def grad_views(grads, group, packed_group):
        return [_halves(grads[n]) for n in group if n in _NATURAL] + [_pack_grads(grads, packed_group)]

    def pair_sums(tag, views, theirs):
        return [_add_pair("rs_%s_add_pair_%d" % (tag, i), g, r, cc) for i, (g, r) in enumerate(zip(views, theirs))]

    swap_sems = _chips_start("swap_late_start", grad_views(late_grads, _LATE, _LATE_PACKED), SIBLING_HALF)
    dqs, dks, dvs, dsink_rows = _swa_bwd(qs, ks, vs, ya, dya, lse_a, swa_bias, sink_rows + swap_sems[4][0:1, 0:1])
    dsink = dsink_rows[:, 0:SWA_GROUP, 0]
    late_views, late_theirs = _chips_wait("swap_late_wait", *swap_sems[:4], SIBLING_HALF, after=dqs)
    rs_sems = _chips_start("scatter_late_start", pair_sums("late", late_views, late_theirs), PIECE)
    dqm, dkm, dvm = _mla_bwd(qm, km, vm, dyb, lse_b, delta_b.reshape(HEADS, 1, t_rows) + rs_sems[4][0:1, 0:1])
    dz, dqb, dx, dgq, dgkv, dg1 = _bwd_in(dqs, dks, dvs, dqm, dkm, dvm, tabs, consts, cq, ckv, gq, gkv, wuq, wk, wv,
                                           dgates, win, x2d, g1, dx1, tm)

    small = {"attn_pre_norm": dg1, "attn_post_norm": dg2, "b_gate": dbg, "sinks": dsink, "q_a_norm": dgq,
             "kv_a_norm": dgkv, "mlp_pre_norm": dg3, "mlp_post_norm": dg4, "conv_b": dconvb, "ple_norm": dg5,
             "conv_w": dconvw8[0:3], "loss": loss_part}
    small_sems = _chips_start("gather_small_start", [_pack_small(small)], EVERYONE)
    small_token = small_sems[4]

    dwk = _unpad_slots(_mm_tn("dw_k", ckvn, dkm, after=small_token), HEADS, NOPE_DIM, 1).reshape(
        KV_LORA, HEADS, NOPE_DIM)
    dwv = _unpad_slots(_mm_tn("dw_v", ckvn, dvm, after=small_token), HEADS, V_DIM, 1).reshape(KV_LORA, HEADS, V_DIM)
    early_grads = {
        "w_in": _unpad_w_in(_mm_tn("dw_in", h1, dz, after=small_token)).reshape(D_MODEL, 4, 808).transpose(1, 0, 2),
        "w_uq": _unpad_slots(_mm_tn("dw_uq", cqn, dqb, after=small_token), HEADS, NOPE_DIM + ROPE_DIM, 1),
        "w_ukv": jnp.concatenate([dwk, dwv], axis=2).reshape(KV_LORA, HEADS * (NOPE_DIM + V_DIM)),
    }

    def finish(tag, pairs, landed, group, packed_group):
        reduced = []
        for i, (pair, land) in enumerate(zip(pairs, landed)):
            own = lax.dynamic_index_in_dim(pair, chip, 0, keepdims=True)
            reduced.append(_add_chips("rs_%s_add_chips_%d" % (tag, i),
                                      lax.dynamic_update_slice(land, own, (chip, 0, 0))))
        others = _swap_sibling("swap_%s_reduced_halves" % tag, reduced)
        r, o = reduced[-1], others[-1]
        packed = jnp.where(cc == 0, jnp.stack([r, o]), jnp.stack([o, r]))
        for n, g in _unpack_shard_grads(packed, packed_group).items():
            updates[n] = _adamw("adamw_" + n, w2[n], g, m2[n], v2[n])
        for n, r, o in zip([n for n in group if n in _NATURAL], reduced, others):
            updates[n] = _adamw_halves("adamw_" + n, w2[n], r, o, m2[n], v2[n], cc)

    updates = {}

    def adamw(n, g):
        updates[n] = _adamw("adamw_" + n, w2[n], g, m2[n], v2[n])

    early_views = grad_views(early_grads, _EARLY, _EARLY_PACKED)
    early_theirs = _swap_sibling("swap_early_grad_halves", early_views, other_half=True)
    small_sent, small_landed = _chips_wait("gather_small_wait", *small_sems[:4], EVERYONE, after=early_theirs[0])
    small_all = lax.dynamic_update_slice(small_landed[0], small_sent[0][None], (4 * xc + 2 * yc + cc, 0, 0))
    early_sems = _chips_start("scatter_early_start", pair_sums("early", early_views, early_theirs), PIECE,
                              after=small_all)
    late_pairs, late_landed = _chips_wait("scatter_late_wait", *rs_sems[:4], PIECE, after=early_sems[4])
    finish("late", late_pairs, late_landed, _LATE, _LATE_PACKED)
    early_pairs, early_landed = _chips_wait("scatter_early_wait", *early_sems[:4], PIECE,
                                            after=updates[_LATE[-1]][1])
    finish("early", early_pairs, early_landed, _EARLY, _EARLY_PACKED)

    small_sum = _unpack_small(_add_devices(small_all))
    for n in names:
        if n == "conv_w":
            adamw(n, lax.dynamic_index_in_dim(small_sum[n].reshape(3, 4, 1408), chip, 1, keepdims=False))
        elif n in small_sum:
            adamw(n, small_sum[n].reshape(w2[n].shape))
    loss = small_sum["loss"][0]

    outs = [[updates[n][i].reshape(wts[n].shape) for n in names] for i in range(4)]
    return (loss, dx.reshape(x.shape), *outs[0], *outs[1], *outs[2], *outs[3])
```

```python
import functools
import math

import numpy as np
import jax
import jax.numpy as jnp
from jax import lax
from jax.experimental import pallas as pl
from jax.experimental.pallas import tpu as pltpu

F32 = jnp.float32
BF16 = jnp.bfloat16

D_MODEL = 1024
D_FF = 2816
PLE_DIM = 256
ROPE_THETA = 10000.0
RMS_EPS = 1e-6
SWA_WINDOW = 128
HEADS = 8
A_KV_HEADS = 2
A_HEAD_DIM = 64
Q_LORA = 256
KV_LORA = 128
NOPE_DIM = 64
ROPE_DIM = 32
V_DIM = 64
LANES = 128
ZW = 4096
NEG = -1e30
SCALE_A = A_HEAD_DIM ** -0.5
SCALE_B = (NOPE_DIM + ROPE_DIM) ** -0.5

ADAM_LR = 0.001
ADAM_B1 = 0.9
ADAM_B2 = 0.999
ADAM_EPS = 1e-08
ADAM_WD = 0.01
ADAM_STEP = 10

VMEM_LIMIT = 60 * 1024 * 1024
MESH_AXES = ("x", "y", "c")
MESH = pl.DeviceIdType.MESH

Z_QA, Z_KA, Z_VA, Z_CQ, Z_CKV, Z_KR, Z_GATE = 0, 1024, 1280, 1536, 1792, 1920, 2048


def _dot(a, b):
    return jnp.dot(a, b, preferred_element_type=F32)


def _dot_nt(a, b):
    return lax.dot_general(a, b, (((1,), (1,)), ((), ())), preferred_element_type=F32)


def _dot_tn(a, b):
    return lax.dot_general(a, b, (((0,), (0,)), ((), ())), preferred_element_type=F32)


def _rms_stats(x):
    r = lax.rsqrt(jnp.mean(x * x, axis=-1, keepdims=True) + RMS_EPS)
    return x * r, r


def _rms_bwd(dy, xn, r, g):
    dxn = dy * g
    dx = r * (dxn - xn * jnp.mean(dxn * xn, axis=-1, keepdims=True))
    dg = jnp.sum(dy * xn, axis=0, keepdims=True)
    return dx, dg


def _tile_lanes(t, n):
    return t if n == 1 else jnp.concatenate([t] * n, axis=1)


def _rope(x, c, s1, s2, half):
    w = x.shape[1]
    n = w // LANES
    return (x * _tile_lanes(c, n) + pltpu.roll(x, w - half, 1) * _tile_lanes(s1, n)
            + pltpu.roll(x, half, 1) * _tile_lanes(s2, n))


def _rope_t(dy, c, s1, s2, half):
    w = dy.shape[1]
    n = w // LANES
    return (dy * _tile_lanes(c, n) + pltpu.roll(dy * _tile_lanes(s1, n), half, 1)
            + pltpu.roll(dy * _tile_lanes(s2, n), w - half, 1))


def _sigmoid(x):
    return 1.0 / (1.0 + jnp.exp(-x))


_GELU_C = math.sqrt(2.0 / math.pi)


def _gelu_and_grad(x):
    a = _GELU_C + (_GELU_C * 0.044715) * (x * x)
    th = jnp.tanh(x * a)
    hx = 0.5 * x
    p1 = 1.0 + th
    gel = hx * p1
    dgel = 0.5 * p1 + (hx * (1.0 - th * th)) * (3.0 * a - 2.0 * _GELU_C)
    return gel, dgel


def _conv_taps(up, h6, h7):
    r1 = pltpu.roll(up, 1, 0)
    r2 = pltpu.roll(up, 2, 0)
    rows = lax.broadcasted_iota(jnp.int32, (8, up.shape[1]), 0)
    xm1 = jnp.concatenate([jnp.where(rows == 0, h7, r1[0:8]), r1[8:]], axis=0)
    xm2 = jnp.concatenate([jnp.where(rows == 0, h6, jnp.where(rows == 1, h7, r2[0:8])), r2[8:]], axis=0)
    return xm1, xm2


def _conv_taps_next(du, n0, n1):
    tm = du.shape[0]
    r1 = pltpu.roll(du, tm - 1, 0)
    r2 = pltpu.roll(du, tm - 2, 0)
    rows = lax.broadcasted_iota(jnp.int32, (8, du.shape[1]), 0)
    xp1 = jnp.concatenate([r1[:tm - 8], jnp.where(rows == 7, n0, r1[tm - 8:])], axis=0)
    xp2 = jnp.concatenate([r2[:tm - 8], jnp.where(rows == 6, n0, jnp.where(rows == 7, n1, r2[tm - 8:]))], axis=0)
    return xp1, xp2


def _row(tm, n):
    return pl.BlockSpec((tm, n), lambda i: (i, 0))


def _full(shape):
    nd = len(shape)
    return pl.BlockSpec(tuple(shape), lambda i: (0,) * nd)


def _resident(shape):
    nd = len(shape)
    return pl.BlockSpec(tuple(shape), lambda i: (0,) * nd, pipeline_mode=pl.Buffered(1))


def _heads(tm, h):
    return pl.BlockSpec((h, tm, LANES), lambda i: (0, i, 0))


def _rows_call(name, body, t_rows, tm, ins, outs, scratch=()):
    return pl.pallas_call(
        body, name=name, grid=(t_rows // tm,),
        in_specs=[s for _, s in ins],
        out_specs=[s for _, s in outs],
        out_shape=[s for s, _ in outs],
        scratch_shapes=list(scratch),
        compiler_params=pltpu.CompilerParams(dimension_semantics=("arbitrary",), vmem_limit_bytes=VMEM_LIMIT),
    )(*[a for a, _ in ins])


def _sds(shape, dtype):
    return jax.ShapeDtypeStruct(tuple(shape), dtype)


def _rope_consts():
    c = np.zeros((16, LANES), np.float32)
    lane = np.arange(LANES)
    inv_a = (ROPE_THETA ** (-(np.arange(0, A_HEAD_DIM, 2, dtype=np.float32) / A_HEAD_DIM))).astype(np.float32)
    in_a = lane < A_HEAD_DIM
    c[0, in_a] = inv_a[lane[in_a] % (A_HEAD_DIM // 2)]
    c[1, in_a] = 1.0
    c[2, lane < A_HEAD_DIM // 2] = -1.0
    c[3, (lane >= A_HEAD_DIM // 2) & in_a] = 1.0
    inv_b = (ROPE_THETA ** (-(np.arange(0, ROPE_DIM, 2, dtype=np.float32) / ROPE_DIM))).astype(np.float32)
    pe = (lane >= NOPE_DIM) & (lane < NOPE_DIM + ROPE_DIM)
    c[5, pe] = inv_b[(lane[pe] - NOPE_DIM) % (ROPE_DIM // 2)]
    c[6, pe] = 1.0
    c[7, (lane >= NOPE_DIM) & (lane < NOPE_DIM + ROPE_DIM // 2)] = -1.0
    c[8, (lane >= NOPE_DIM + ROPE_DIM // 2) & (lane < NOPE_DIM + ROPE_DIM)] = 1.0
    c[9, lane < NOPE_DIM] = 1.0
    c[10, pe] = 1.0
    return jnp.asarray(c)


def _rope_tables(pos_f, consts, tm):
    t_rows = pos_f.shape[0]

    def body(pos_ref, c_ref, ca, sa1, sa2, cb, sb1, sb2):
        ang = pos_ref[...] * (c_ref[0:1, :] + c_ref[5:6, :])
        cs, sn = jnp.cos(ang), jnp.sin(ang)
        ca[...] = cs * c_ref[1:2, :]
        sa1[...] = sn * c_ref[2:3, :]
        sa2[...] = sn * c_ref[3:4, :]
        cb[...] = cs * c_ref[6:7, :] + c_ref[9:10, :]
        sb1[...] = sn * c_ref[7:8, :]
        sb2[...] = sn * c_ref[8:9, :]

    tab = (_sds((t_rows, LANES), F32), _row(tm, LANES))
    return _rows_call("rope_tables", body, t_rows, tm,
                      [(pos_f, _row(tm, 1)), (consts, _full(consts.shape))], [tab] * 6)


def _fwd_in(x, g1, win, bg, gq, gkv, wuq, wk, wv, tabs, tm):
    t_rows = x.shape[0]

    def body(x_ref, g1_ref, win_ref, bg_ref, gq_ref, gkv_ref, wuq_ref, wk_ref, wv_ref,
             ca, sa1, sa2, cb, sb1, sb2,
             h1_ref, qs_ref, ks_ref, vs_ref, cq_ref, cqn_ref, ckv_ref, ckvn_ref, qm_ref, km_ref, vm_ref, gate_ref):
        xn, _ = _rms_stats(x_ref[...])
        hb = (xn * g1_ref[...]).astype(BF16)
        h1_ref[...] = hb
        ta = (ca[...], sa1[...], sa2[...])
        tb = (cb[...], sb1[...], sb2[...])
        cq = _dot(hb, win_ref[:, Z_CQ:Z_CKV])
        ckv = _dot(hb, win_ref[:, Z_CKV:Z_KR])
        z_qa = _dot(hb, win_ref[:, Z_QA:Z_KA])
        z_ka = _dot(hb, win_ref[:, Z_KA:Z_VA])
        z_va = _dot(hb, win_ref[:, Z_VA:Z_CQ])
        z_kr = _dot(hb, win_ref[:, Z_KR:Z_GATE])
        cq_ref[...] = cq
        cqn, _ = _rms_stats(cq)
        cqb = (cqn * gq_ref[...]).astype(BF16)
        cqn_ref[...] = cqb
        ckv_ref[...] = ckv
        ckvn, _ = _rms_stats(ckv)
        ckvb = (ckvn * gkv_ref[...]).astype(BF16)
        ckvn_ref[...] = ckvb
        z_qm = _dot(cqb, wuq_ref[...])
        z_km = _dot(ckvb, wk_ref[...])
        z_vm = _dot(ckvb, wv_ref[...])
        z_gate = _dot(hb, win_ref[:, Z_GATE:ZW])
        qs_ref[...] = (_rope(z_qa, *ta, A_HEAD_DIM // 2) * SCALE_A).astype(BF16)
        ks_ref[...] = _rope(z_ka, *ta, A_HEAD_DIM // 2).astype(BF16)
        vs_ref[...] = z_va.astype(BF16)
        qm_ref[...] = (_rope(z_qm, *tb, ROPE_DIM // 2) * SCALE_B).astype(BF16)
        km_ref[...] = (z_km + _tile_lanes(_rope(z_kr, *tb, ROPE_DIM // 2), HEADS)).astype(BF16)
        vm_ref[...] = z_vm.astype(BF16)
        gate_ref[...] = _sigmoid(z_gate + bg_ref[...])

    def o(n, dt):
        return (_sds((t_rows, n), dt), _row(tm, n))

    ins = [(x, _row(tm, D_MODEL)), (g1, _full(g1.shape)), (win, _resident(win.shape)), (bg, _full(bg.shape)),
           (gq, _full(gq.shape)), (gkv, _full(gkv.shape)), (wuq, _full(wuq.shape)), (wk, _full(wk.shape)),
           (wv, _full(wv.shape))] + [(t, _row(tm, LANES)) for t in tabs]
    outs = [o(1024, BF16), o(1024, BF16), o(256, BF16), o(256, BF16), o(256, F32), o(256, BF16), o(128, F32),
            o(128, BF16), o(1024, BF16), o(1024, BF16), o(1024, BF16), o(2048, F32)]
    return _rows_call("fwd_in", body, t_rows, tm, ins, outs)


def _attn_tile(t_rows):
    return min(512, t_rows)


MLA_HEADS_PER_STEP = 4
MLA_FWD_HEADS_PER_STEP = 8


def _causal_pairs(nq, by_kv):
    if by_kv:
        pairs = [(i, j) for j in range(nq) for i in range(j, nq)]
    else:
        pairs = [(i, j) for i in range(nq) for j in range(i + 1)]
    return (jnp.asarray([p[0] for p in pairs], jnp.int32), jnp.asarray([p[1] for p in pairs], jnp.int32))


def _mla_fwd(q, k, v):
    t_rows = q.shape[0]
    t = _attn_tile(t_rows)
    hp = MLA_FWD_HEADS_PER_STEP
    w = hp * LANES
    ii, jj = _causal_pairs(t_rows // t, by_kv=False)

    def body(i_ref, j_ref, q_ref, k_ref, v_ref, o_ref, lse_ref, m_s, l_s, acc_s):
        i = i_ref[pl.program_id(1)]
        j = j_ref[pl.program_id(1)]

        @pl.when(j == 0)
        def _():
            m_s[...] = jnp.full(m_s.shape, NEG, F32)
            l_s[...] = jnp.zeros(l_s.shape, F32)
            acc_s[...] = jnp.zeros(acc_s.shape, F32)

        def step(diagonal):
            sls = [slice(hh * LANES, (hh + 1) * LANES) for hh in range(hp)]
            scores = [_dot_nt(k_ref[:, sl], q_ref[:, sl]) for sl in sls]
            if diagonal:
                valid = (lax.broadcasted_iota(jnp.int32, (t, t), 0) <= lax.broadcasted_iota(jnp.int32, (t, t), 1))
                scores = [jnp.where(valid, s, NEG) for s in scores]
            stats = []
            for hh, s in enumerate(scores):
                m_prev = m_s[hh]
                m_new = jnp.maximum(m_prev, jnp.max(s, axis=0, keepdims=True))
                p = jnp.exp(s - m_new)
                alpha = jnp.exp(m_prev - m_new)
                stats.append((m_new, alpha, alpha * l_s[hh] + jnp.sum(p, axis=0, keepdims=True), p.astype(BF16)))
            for hh, (m_new, alpha, l_new, p) in enumerate(stats):
                sl = sls[hh]
                acc = alpha * acc_s[hh] + _dot_tn(v_ref[:, sl], p)
                if diagonal:
                    o_ref[:, sl] = (acc / l_new).T.astype(o_ref.dtype)
                    lse_ref[hh] = m_new + jnp.log(l_new)
                else:
                    m_s[hh] = m_new
                    l_s[hh] = l_new
                    acc_s[hh] = acc

        pl.when(j < i)(lambda: step(False))
        pl.when(j == i)(lambda: step(True))

    grid_spec = pltpu.PrefetchScalarGridSpec(
        num_scalar_prefetch=2, grid=(HEADS // hp, ii.shape[0]),
        in_specs=[pl.BlockSpec((t, w), lambda hb, s, ir, jr: (ir[s], hb)),
                  pl.BlockSpec((t, w), lambda hb, s, ir, jr: (jr[s], hb)),
                  pl.BlockSpec((t, w), lambda hb, s, ir, jr: (jr[s], hb))],
        out_specs=[pl.BlockSpec((t, w), lambda hb, s, ir, jr: (ir[s], hb)),
                   pl.BlockSpec((hp, 1, t), lambda hb, s, ir, jr: (hb, 0, ir[s]))],
        scratch_shapes=[pltpu.VMEM((hp, 1, t), F32), pltpu.VMEM((hp, 1, t), F32), pltpu.VMEM((hp, LANES, t), F32)])
    return pl.pallas_call(
        body, name="mla_fwd", grid_spec=grid_spec,
        out_shape=[_sds((t_rows, HEADS * LANES), BF16), _sds((HEADS, 1, t_rows), F32)],
        compiler_params=pltpu.CompilerParams(dimension_semantics=("arbitrary",) * 2, vmem_limit_bytes=VMEM_LIMIT),
    )(ii, jj, q, k, v)


def _mla_bwd(q, k, v, do, lse, delta):
    t_rows = q.shape[0]
    t = _attn_tile(t_rows)
    hp = MLA_HEADS_PER_STEP
    w = hp * LANES
    ii, jj = _causal_pairs(t_rows // t, by_kv=True)

    def body(i_ref, j_ref, q_ref, k_ref, v_ref, do_ref, lse_ref, dl_ref, dq_ref, dk_ref, dv_ref):
        i = i_ref[pl.program_id(1)]
        j = j_ref[pl.program_id(1)]

        @pl.when(pl.program_id(1) == 0)
        def _():
            dq_ref[...] = jnp.zeros(dq_ref.shape, F32)

        def step(diagonal):
            r0 = pl.multiple_of(i * t, t)
            sls = [slice(hh * LANES, (hh + 1) * LANES) for hh in range(hp)]
            scores = [_dot_nt(k_ref[:, sl], q_ref[:, sl]) for sl in sls]
            if diagonal:
                valid = (lax.broadcasted_iota(jnp.int32, (t, t), 0) <= lax.broadcasted_iota(jnp.int32, (t, t), 1))
                scores = [jnp.where(valid, s, NEG) for s in scores]
            dps = [_dot_nt(v_ref[:, sl], do_ref[:, sl]) for sl in sls]
            ps = [jnp.exp(s - lse_ref[hh]) for hh, s in enumerate(scores)]
            dss = [(p * (dp - dl_ref[hh])).astype(BF16) for hh, (p, dp) in enumerate(zip(ps, dps))]
            for hh, sl in enumerate(sls):
                dv = _dot(ps[hh].astype(BF16), do_ref[:, sl])
                dk = _dot(dss[hh], q_ref[:, sl])
                if diagonal:
                    dv_ref[:, sl] = dv
                    dk_ref[:, sl] = dk
                else:
                    dv_ref[:, sl] += dv
                    dk_ref[:, sl] += dk
                dq_ref[hh, pl.ds(r0, t), :] += _dot_tn(dss[hh], k_ref[:, sl])

        pl.when(i > j)(lambda: step(False))
        pl.when(i == j)(lambda: step(True))

    def qmap(hb, s, ir, jr):
        return (ir[s], hb)

    def kvmap(hb, s, ir, jr):
        return (jr[s], hb)

    def rowmap(hb, s, ir, jr):
        return (hb, 0, ir[s])

    grid_spec = pltpu.PrefetchScalarGridSpec(
        num_scalar_prefetch=2, grid=(HEADS // hp, ii.shape[0]),
        in_specs=[pl.BlockSpec((t, w), qmap), pl.BlockSpec((t, w), kvmap), pl.BlockSpec((t, w), kvmap),
                  pl.BlockSpec((t, w), qmap), pl.BlockSpec((hp, 1, t), rowmap), pl.BlockSpec((hp, 1, t), rowmap)],
        out_specs=[pl.BlockSpec((hp, t_rows, LANES), lambda hb, s, ir, jr: (hb, 0, 0)),
                   pl.BlockSpec((t, w), kvmap), pl.BlockSpec((t, w), kvmap)])
    return pl.pallas_call(
        body, name="mla_bwd", grid_spec=grid_spec,
        out_shape=[_sds((HEADS, t_rows, LANES), F32), _sds((t_rows, HEADS * LANES), F32),
                   _sds((t_rows, HEADS * LANES), F32)],
        compiler_params=pltpu.CompilerParams(dimension_semantics=("arbitrary",) * 2, vmem_limit_bytes=VMEM_LIMIT),
    )(ii, jj, q, k, v, do, lse, delta)


SWA_TILE = 2 * SWA_WINDOW
SWA_GROUP = HEADS // A_KV_HEADS


def _swa_bias(tq):
    koff = lax.broadcasted_iota(jnp.int32, (tq + SWA_WINDOW, SWA_GROUP * tq), 0) - SWA_WINDOW
    qoff = (lax.broadcasted_iota(jnp.int32, (tq + SWA_WINDOW, SWA_GROUP * tq), 1) % tq)
    band = (koff <= qoff) & (qoff - koff < SWA_WINDOW)
    return jnp.stack([jnp.where(band & (koff >= 0), 0.0, NEG), jnp.where(band, 0.0, NEG)]).astype(F32)


def _swa_specs(tq, nq):
    wb = tq // SWA_WINDOW
    kvw = A_KV_HEADS * LANES

    def qi(i):
        return jnp.minimum(i, nq - 1)

    q = pl.BlockSpec((tq, HEADS * LANES), lambda i: (qi(i), 0))
    cur = pl.BlockSpec((tq, kvw), lambda i: (qi(i), 0))
    prev = pl.BlockSpec((SWA_WINDOW, kvw), lambda i: (jnp.maximum(qi(i) * wb - 1, 0), 0))
    bias = pl.BlockSpec((1, tq + SWA_WINDOW, SWA_GROUP * tq), lambda i: (jnp.minimum(i, 1), 0, 0))
    rows = pl.BlockSpec((A_KV_HEADS, 1, 1, SWA_GROUP * tq), lambda i: (0, qi(i), 0, 0))
    sink = pl.BlockSpec((A_KV_HEADS, 1, SWA_GROUP * tq), lambda i: (0, 0, 0))
    return q, cur, prev, bias, rows, sink


def _stack_heads(ref, kvh):
    base = kvh * SWA_GROUP
    return jnp.concatenate([ref[:, (base + g) * LANES:(base + g + 1) * LANES] for g in range(SWA_GROUP)], axis=0)


def _unstack_heads(ref, kvh, val, tq):
    base = kvh * SWA_GROUP
    for g in range(SWA_GROUP):
        ref[:, (base + g) * LANES:(base + g + 1) * LANES] = val[g * tq:(g + 1) * tq].astype(ref.dtype)


def _kv_window(prev_ref, cur_ref, kvh):
    sl = slice(kvh * LANES, (kvh + 1) * LANES)
    return jnp.concatenate([prev_ref[:, sl], cur_ref[:, sl]], axis=0)


def _swa_fwd(q, k, v, bias, sink_rows):
    t_rows = q.shape[0]
    tq = min(SWA_TILE, t_rows)
    nq = t_rows // tq
    qs_, cur, prev, bs, rows, sk = _swa_specs(tq, nq)
    kvhs = range(A_KV_HEADS)

    def body(q_ref, kc_ref, kp_ref, vc_ref, vp_ref, b_ref, sink_ref, o_ref, lse_ref):
        scores = [_dot_nt(_kv_window(kp_ref, kc_ref, h), _stack_heads(q_ref, h)) + b_ref[0] for h in kvhs]
        stats = []
        for h, s in zip(kvhs, scores):
            sink = sink_ref[h]
            m = jnp.maximum(jnp.max(s, axis=0, keepdims=True), sink)
            p = jnp.exp(s - m)
            l = jnp.sum(p, axis=0, keepdims=True) + jnp.exp(sink - m)
            lse_ref[h, 0] = m + jnp.log(l)
            stats.append((p.astype(BF16), l))
        for h, (p, l) in zip(kvhs, stats):
            _unstack_heads(o_ref, h, (_dot_tn(_kv_window(vp_ref, vc_ref, h), p) / l).T, tq)

    return pl.pallas_call(
        body, name="swa_fwd", grid=(nq,),
        in_specs=[qs_, cur, prev, cur, prev, bs, sk],
        out_specs=[qs_, rows],
        out_shape=[_sds((t_rows, HEADS * LANES), BF16), _sds((A_KV_HEADS, nq, 1, SWA_GROUP * tq), F32)],
        compiler_params=pltpu.CompilerParams(dimension_semantics=("arbitrary",), vmem_limit_bytes=VMEM_LIMIT),
    )(q, k, k, v, v, bias, sink_rows)


def _swa_bwd(q, k, v, o, do, lse, bias, sink_rows):
    t_rows = q.shape[0]
    tq = min(SWA_TILE, t_rows)
    nq = t_rows // tq
    qs_, cur, prev, bs, rows, sk = _swa_specs(tq, nq)
    hw = SWA_WINDOW
    kvhs = range(A_KV_HEADS)
    kvw = A_KV_HEADS * LANES

    def body(q_ref, kc_ref, kp_ref, vc_ref, vp_ref, o_ref, do_ref, lse_ref, b_ref, sink_ref,
             dq_ref, dk_ref, dv_ref, dsink_ref, ck, cv, dsa):
        i = pl.program_id(0)

        @pl.when(i == 0)
        def _():
            dsa[...] = jnp.zeros(dsa.shape, F32)

        @pl.when(i < nq)
        def _():
            qs = [_stack_heads(q_ref, h) for h in kvhs]
            dos = [_stack_heads(do_ref, h) for h in kvhs]
            kks = [_kv_window(kp_ref, kc_ref, h) for h in kvhs]
            scores = [_dot_nt(kks[h], qs[h]) for h in kvhs]
            dps = [_dot_nt(_kv_window(vp_ref, vc_ref, h), dos[h]) for h in kvhs]
            ps, dss = [], []
            for h in kvhs:
                lse = lse_ref[h, 0]
                p = jnp.exp(scores[h] + b_ref[0] - lse)
                delta = jnp.sum((_stack_heads(o_ref, h).astype(F32) * dos[h].astype(F32)).T, axis=0, keepdims=True)
                dsa[h] += -jnp.exp(sink_ref[h] - lse) * delta
                ps.append(p.astype(BF16))
                dss.append((p * (dps[h] - delta)).astype(BF16))
            for h in kvhs:
                sl = slice(h * LANES, (h + 1) * LANES)
                dv = _dot(ps[h], dos[h])
                dk = _dot(dss[h], qs[h])
                _unstack_heads(dq_ref, h, _dot_tn(dss[h], kks[h]), tq)

                @pl.when(i > 0)
                def _():
                    dk_ref[0:tq - hw, sl] = ck[0:tq - hw, sl]
                    dk_ref[tq - hw:tq, sl] = ck[tq - hw:tq, sl] + dk[0:hw]
                    dv_ref[0:tq - hw, sl] = cv[0:tq - hw, sl]
                    dv_ref[tq - hw:tq, sl] = cv[tq - hw:tq, sl] + dv[0:hw]

                ck[:, sl] = dk[hw:hw + tq]
                cv[:, sl] = dv[hw:hw + tq]

        @pl.when(i == nq)
        def _():
            dk_ref[...] = ck[...]
            dv_ref[...] = cv[...]
            dsink_ref[...] = jnp.zeros(dsink_ref.shape, F32)
            for h in kvhs:
                for g in range(SWA_GROUP):
                    tot = jnp.sum(dsa[h, :, g * tq:(g + 1) * tq], axis=1, keepdims=True)
                    dsink_ref[h, g:g + 1, :] = jnp.zeros((1, LANES), F32) + tot

    kv_out = pl.BlockSpec((tq, kvw), lambda i: (jnp.maximum(i - 1, 0), 0))
    return pl.pallas_call(
        body, name="swa_bwd", grid=(nq + 1,),
        in_specs=[qs_, cur, prev, cur, prev, qs_, qs_, rows, bs, sk],
        out_specs=[qs_, kv_out, kv_out, pl.BlockSpec((A_KV_HEADS, 8, LANES), lambda i: (0, 0, 0))],
        out_shape=[_sds((t_rows, HEADS * LANES), F32), _sds((t_rows, kvw), F32), _sds((t_rows, kvw), F32),
                   _sds((A_KV_HEADS, 8, LANES), F32)],
        scratch_shapes=[pltpu.VMEM((tq, kvw), F32), pltpu.VMEM((tq, kvw), F32),
                        pltpu.VMEM((A_KV_HEADS, 1, SWA_GROUP * tq), F32)],
        compiler_params=pltpu.CompilerParams(dimension_semantics=("arbitrary",), vmem_limit_bytes=VMEM_LIMIT),
    )(q, k, k, v, v, o, do, lse, bias, sink_rows)


def _fwd_mix(x, ya, yb, gate, wba, wbb, wout, g2, g3, tm):
    t_rows = x.shape[0]

    def body(x_ref, ya_ref, yb_ref, gate_ref, wba_ref, wbb_ref, wout_ref, g2_ref, g3_ref,
             pa_ref, pb_ref, mixed_ref, o_ref, x1_ref, h2_ref):
        pa = _dot(ya_ref[...], wba_ref[...])
        pb = _dot(yb_ref[...], wbb_ref[...])
        pa_ref[...] = pa
        pb_ref[...] = pb
        mixed = (gate_ref[:, 0:D_MODEL] * pa + gate_ref[:, D_MODEL:2 * D_MODEL] * pb).astype(BF16)
        mixed_ref[...] = mixed
        o = _dot(mixed, wout_ref[...])
        o_ref[...] = o
        on, _ = _rms_stats(o)
        x1 = x_ref[...] + on * g2_ref[...]
        x1_ref[...] = x1
        x1n, _ = _rms_stats(x1)
        h2_ref[...] = (x1n * g3_ref[...]).astype(BF16)

    def o_(dt):
        return (_sds((t_rows, D_MODEL), dt), _row(tm, D_MODEL))

    ins = [(x, _row(tm, D_MODEL)), (ya, _row(tm, 1024)), (yb, _row(tm, 1024)), (gate, _row(tm, 2048)),
           (wba, _resident(wba.shape)), (wbb, _resident(wbb.shape)), (wout, _resident(wout.shape)),
           (g2, _full(g2.shape)), (g3, _full(g3.shape))]
    return _rows_call("fwd_mix", body, t_rows, tm, ins, [o_(F32), o_(F32), o_(BF16), o_(F32), o_(F32), o_(BF16)])


CONV_CHUNK = 1408


def _fwd_up(h2, wup, convw8, convb, tm):
    t_rows = h2.shape[0]
    cdim = 2 * D_FF

    def body(h2_ref, wup_ref, cw_ref, cb_ref, up_ref, a_ref, carry):
        i = pl.program_id(0)

        @pl.when(i == 0)
        def _():
            carry[...] = jnp.zeros(carry.shape, F32)

        hb = h2_ref[...]
        ups = [_dot(hb, wup_ref[s]) for s in range(cdim // CONV_CHUNK)]

        def conv(c0):
            sl = slice(c0, c0 + CONV_CHUNK)
            up = ups[c0 // CONV_CHUNK]
            up_ref[:, sl] = up
            xm1, xm2 = _conv_taps(up, carry[6:7, sl], carry[7:8, sl])
            u = cw_ref[0:1, sl] * xm2 + cw_ref[1:2, sl] * xm1 + cw_ref[2:3, sl] * up + cb_ref[:, sl]
            carry[:, sl] = up[tm - 8:tm, :]
            return u

        for c0 in range(0, D_FF, CONV_CHUNK):
            ug = conv(c0)
            uv = conv(D_FF + c0)
            gel, _ = _gelu_and_grad(ug)
            a_ref[:, c0:c0 + CONV_CHUNK] = (gel * uv).astype(BF16)

    ins = [(h2, _row(tm, D_MODEL)), (wup, _resident(wup.shape)), (convw8, _full(convw8.shape)), (convb, _full(convb.shape))]
    outs = [(_sds((t_rows, cdim), F32), _row(tm, cdim)), (_sds((t_rows, D_FF), BF16), _row(tm, D_FF))]
    return _rows_call("fwd_up", body, t_rows, tm, ins, outs, scratch=[pltpu.VMEM((8, cdim), F32)])


def _fwd_out(a, wdown, x1, g4, p, wple, g5, wpg, tgt, tm):
    t_rows = a.shape[0]

    def body(a_ref, wdown_ref, x1_ref, g4_ref, p_ref, wple_ref, g5_ref, wpg_ref, tgt_ref,
             ff_ref, x2_ref, e_ref, n5_ref, sg_ref, dx3_ref, loss_ref):
        i = pl.program_id(0)
        ff = _dot(a_ref[...], wdown_ref[...])
        e = _dot(p_ref[...].astype(BF16), wple_ref[...])
        ff_ref[...] = ff
        ffn, _ = _rms_stats(ff)
        x2 = x1_ref[...] + ffn * g4_ref[...]
        x2_ref[...] = x2
        e_ref[...] = e
        x2n, _ = _rms_stats(x2)
        n5 = (x2n * g5_ref[...]).astype(BF16)
        n5_ref[...] = n5
        sg = _sigmoid(_dot(n5, wpg_ref[...]))
        sg_ref[...] = sg
        d = x2 + sg * e - tgt_ref[...]
        dx3_ref[...] = d * (1.0 / D_MODEL)

        @pl.when(i == 0)
        def _():
            loss_ref[...] = jnp.zeros((1, 1), F32)

        loss_ref[...] += 0.5 * jnp.sum(jnp.sum(d * d, axis=1, keepdims=True), axis=0, keepdims=True) * (1.0 / D_MODEL)

    def o_(dt):
        return (_sds((t_rows, D_MODEL), dt), _row(tm, D_MODEL))

    ins = [(a, _row(tm, D_FF)), (wdown, _resident(wdown.shape)), (x1, _row(tm, D_MODEL)), (g4, _full(g4.shape)),
           (p, _row(tm, PLE_DIM)), (wple, _full(wple.shape)), (g5, _full(g5.shape)), (wpg, _resident(wpg.shape)),
           (tgt, _row(tm, D_MODEL))]
    outs = [o_(F32), o_(F32), o_(F32), o_(BF16), o_(F32), o_(F32), (_sds((1, 1), F32), _full((1, 1)))]
    return _rows_call("fwd_out", body, t_rows, tm, ins, outs)


def _bwd_out(dx3, e, sg, x2, ff, g5, g4, wpg, wdown, up, convw8, convb, tm):
    t_rows = dx3.shape[0]
    cdim = 2 * D_FF
    hb = tm // 8

    def body(dx3_ref, e_ref, sg_ref, x2_ref, ff_ref, g5_ref, g4_ref, wpg_ref, wdown_ref, up_ref, halo_ref, cw_ref,
             cb_ref, dpre_ref, de_ref, dx2_ref, dff_ref, du_ref, dg5_ref, dg4_ref, dcb_ref, dcw_ref):
        i = pl.program_id(0)

        @pl.when(i == 0)
        def _():
            dg5_ref[...] = jnp.zeros(dg5_ref.shape, F32)
            dg4_ref[...] = jnp.zeros(dg4_ref.shape, F32)
            dcb_ref[...] = jnp.zeros(dcb_ref.shape, F32)
            dcw_ref[...] = jnp.zeros(dcw_ref.shape, F32)

        dx3 = dx3_ref[...]
        sg = sg_ref[...]
        dpre = (dx3 * e_ref[...] * sg * (1.0 - sg)).astype(BF16)
        dpre_ref[...] = dpre
        de_ref[...] = (dx3 * sg).astype(BF16)
        dn5 = _dot_nt(dpre, wpg_ref[...])
        x2n, r5 = _rms_stats(x2_ref[...])
        d2, dg5 = _rms_bwd(dn5, x2n, r5, g5_ref[...])
        dx2 = dx3 + d2
        dx2_ref[...] = dx2
        dg5_ref[...] += dg5
        ffn, r4 = _rms_stats(ff_ref[...])
        dff, dg4 = _rms_bwd(dx2, ffn, r4, g4_ref[...])
        dg4_ref[...] += dg4
        dffb = dff.astype(BF16)
        dff_ref[...] = dffb
        keep = jnp.where(i > 0, 1.0, 0.0)

        def conv(c0):
            sl = slice(c0, c0 + CONV_CHUNK)
            up = up_ref[:, sl]
            xm1, xm2 = _conv_taps(up, halo_ref[6:7, sl] * keep, halo_ref[7:8, sl] * keep)
            u = cw_ref[0:1, sl] * xm2 + cw_ref[1:2, sl] * xm1 + cw_ref[2:3, sl] * up + cb_ref[:, sl]
            return u, up, xm1, xm2

        def grads(c0, du, up, xm1, xm2):
            sl = slice(c0, c0 + CONV_CHUNK)
            du_ref[:, sl] = du.astype(BF16)
            dcb_ref[:, sl] += jnp.sum(du, axis=0, keepdims=True)
            dcw_ref[0:1, sl] += jnp.sum(du * xm2, axis=0, keepdims=True)
            dcw_ref[1:2, sl] += jnp.sum(du * xm1, axis=0, keepdims=True)
            dcw_ref[2:3, sl] += jnp.sum(du * up, axis=0, keepdims=True)

        for c0 in range(0, D_FF, CONV_CHUNK):
            da = _dot_nt(dffb, wdown_ref[c0:c0 + CONV_CHUNK, :])
            ug, *rg = conv(c0)
            uv, *rv = conv(D_FF + c0)
            gel, dgel = _gelu_and_grad(ug)
            grads(c0, da * uv * dgel, *rg)
            grads(D_FF + c0, da * gel, *rv)

    def o_(n, dt):
        return (_sds((t_rows, n), dt), _row(tm, n))

    def acc(r, n):
        return (_sds((r, n), F32), _full((r, n)))

    halo = pl.BlockSpec((8, cdim), lambda i: (jnp.maximum(i * hb - 1, 0), 0))
    ins = [(dx3, _row(tm, D_MODEL)), (e, _row(tm, D_MODEL)), (sg, _row(tm, D_MODEL)), (x2, _row(tm, D_MODEL)),
           (ff, _row(tm, D_MODEL)), (g5, _full(g5.shape)), (g4, _full(g4.shape)), (wpg, _resident(wpg.shape)),
           (wdown, _resident(wdown.shape)), (up, _row(tm, cdim)), (up, halo), (convw8, _full(convw8.shape)),
           (convb, _full(convb.shape))]
    outs = [o_(D_MODEL, BF16), o_(D_MODEL, BF16), o_(D_MODEL, F32), o_(D_MODEL, BF16), o_(cdim, BF16),
            acc(1, D_MODEL), acc(1, D_MODEL), acc(1, cdim), acc(8, cdim)]
    return _rows_call("bwd_out", body, t_rows, tm, ins, outs)


def _bwd_mid(du, convw8, wup, dx2, x1, g3, o, g2, wout, gate, pa, pb, wba, wbb, yb, tm):
    t_rows = du.shape[0]
    cdim = 2 * D_FF
    halo_rows = 16
    hb = tm // halo_rows
    last_blk = t_rows // halo_rows - 1
    n_tiles = t_rows // tm

    def body(du_ref, halo_ref, cw_ref, wup_ref, dx2_ref, x1_ref, g3_ref, o_ref, g2_ref, wout_ref, gate_ref, pa_ref,
             pb_ref, wba_ref, wbb_ref, yb_ref,
             dup_ref, dx1_ref, do_ref, dpa_ref, dpb_ref, dgt_ref, dya_ref, dyb_ref, dl_ref, dg3_ref, dg2_ref, dbg_ref):
        i = pl.program_id(0)

        @pl.when(i == 0)
        def _():
            dg3_ref[...] = jnp.zeros(dg3_ref.shape, F32)
            dg2_ref[...] = jnp.zeros(dg2_ref.shape, F32)
            dbg_ref[...] = jnp.zeros(dbg_ref.shape, F32)

        keep = jnp.where(i < n_tiles - 1, 1.0, 0.0)
        dh2 = jnp.zeros((tm, D_MODEL), F32)
        dups = []
        for c0 in range(0, cdim, CONV_CHUNK):
            sl = slice(c0, c0 + CONV_CHUNK)
            du = du_ref[:, sl].astype(F32)
            nxt = halo_ref[:, sl].astype(F32)
            xp1, xp2 = _conv_taps_next(du, nxt[0:1] * keep, nxt[1:2] * keep)
            dups.append((cw_ref[2:3, sl] * du + cw_ref[1:2, sl] * xp1 + cw_ref[0:1, sl] * xp2).astype(BF16))
            dup_ref[:, sl] = dups[-1]
            if len(dups) > 1:
                dh2 = dh2 + _dot_nt(dups[-2], wup_ref[len(dups) - 2])
        dh2 = dh2 + _dot_nt(dups[-1], wup_ref[len(dups) - 1])
        x1n, r3 = _rms_stats(x1_ref[...])
        d1, dg3 = _rms_bwd(dh2, x1n, r3, g3_ref[...])
        dx1 = dx2_ref[...] + d1
        dx1_ref[...] = dx1
        dg3_ref[...] += dg3
        on, r2 = _rms_stats(o_ref[...])
        do, dg2 = _rms_bwd(dx1, on, r2, g2_ref[...])
        dg2_ref[...] += dg2
        dob = do.astype(BF16)
        do_ref[...] = dob
        dmixed = _dot_nt(dob, wout_ref[...])
        ga = gate_ref[:, 0:D_MODEL]
        gb = gate_ref[:, D_MODEL:2 * D_MODEL]
        dpa = (dmixed * ga).astype(BF16)
        dpb = (dmixed * gb).astype(BF16)
        dpa_ref[...] = dpa
        dpb_ref[...] = dpb
        dga = dmixed * pa_ref[...] * ga * (1.0 - ga)
        dgb = dmixed * pb_ref[...] * gb * (1.0 - gb)
        dgt_ref[:, 0:D_MODEL] = dga.astype(BF16)
        dgt_ref[:, D_MODEL:2 * D_MODEL] = dgb.astype(BF16)
        dbg_ref[:, 0:D_MODEL] += jnp.sum(dga, axis=0, keepdims=True)
        dbg_ref[:, D_MODEL:2 * D_MODEL] += jnp.sum(dgb, axis=0, keepdims=True)
        dya_ref[...] = _dot_nt(dpa, wba_ref[...]).astype(BF16)
        dyb = _dot_nt(dpb, wbb_ref[...]).astype(BF16)
        dyb_ref[...] = dyb
        prod = yb_ref[...].astype(F32) * dyb.astype(F32)
        lane_head = lax.broadcasted_iota(jnp.int32, (HEADS, HEADS * LANES), 1) // LANES
        sel = (lane_head == lax.broadcasted_iota(jnp.int32, (HEADS, HEADS * LANES), 0)).astype(BF16)
        hi = prod.astype(BF16)
        lo = (prod - hi.astype(F32)).astype(BF16)
        dl_ref[...] = _dot_nt(sel, hi) + _dot_nt(sel, lo)

    def o_(n, dt):
        return (_sds((t_rows, n), dt), _row(tm, n))

    def acc(r, n):
        return (_sds((r, n), F32), _full((r, n)))

    halo = pl.BlockSpec((halo_rows, cdim), lambda i: (jnp.minimum((i + 1) * hb, last_blk), 0))
    ins = [(du, _row(tm, cdim)), (du, halo), (convw8, _full(convw8.shape)), (wup, _resident(wup.shape)),
           (dx2, _row(tm, D_MODEL)), (x1, _row(tm, D_MODEL)), (g3, _full(g3.shape)), (o, _row(tm, D_MODEL)),
           (g2, _full(g2.shape)), (wout, _resident(wout.shape)), (gate, _row(tm, 2048)), (pa, _row(tm, D_MODEL)),
           (pb, _row(tm, D_MODEL)), (wba, _resident(wba.shape)), (wbb, _resident(wbb.shape)), (yb, _row(tm, 1024))]
    outs = [o_(cdim, BF16), o_(D_MODEL, F32), o_(D_MODEL, BF16), o_(D_MODEL, BF16), o_(D_MODEL, BF16),
            o_(2048, BF16), o_(1024, BF16), o_(1024, BF16),
            (_sds((HEADS, t_rows), F32), pl.BlockSpec((HEADS, tm), lambda i: (0, i))),
            acc(1, D_MODEL), acc(1, D_MODEL), acc(1, 2048)]
    return _rows_call("bwd_mid", body, t_rows, tm, ins, outs)


def _bwd_in(dqs, dks, dvs, dqm, dkm, dvm, tabs, consts, cq, ckv, gq, gkv, wuq, wk, wv, dgates, win, x, g1, dx1, tm):
    t_rows = x.shape[0]

    def body(dqs_ref, dks_ref, dvs_ref, dqm_ref, dkm_ref, dvm_ref, ca, sa1, sa2, cb, sb1, sb2, c_ref, cq_ref,
             ckv_ref, gq_ref, gkv_ref, wuq_ref, wk_ref, wv_ref, dgt_ref, win_ref, x_ref, g1_ref, dx1_ref,
             dz_ref, dqb_ref, dx_ref, dgq_ref, dgkv_ref, dg1_ref):
        i = pl.program_id(0)

        @pl.when(i == 0)
        def _():
            dgq_ref[...] = jnp.zeros(dgq_ref.shape, F32)
            dgkv_ref[...] = jnp.zeros(dgkv_ref.shape, F32)
            dg1_ref[...] = jnp.zeros(dg1_ref.shape, F32)

        ta = (ca[...], sa1[...], sa2[...])
        tb = (cb[...], sb1[...], sb2[...])

        def piece(lo, hi, val):
            dz_ref[:, lo:hi] = val
            return _dot_nt(val, win_ref[:, lo:hi])

        dh1 = piece(Z_GATE, ZW, dgt_ref[...])
        dkm = dkm_ref[...]
        dckvn = _dot_nt(dkm.astype(BF16), wk_ref[...]) + _dot_nt(dvm_ref[...].astype(BF16), wv_ref[...])
        dh1 = dh1 + piece(Z_VA, Z_CQ, dvs_ref[...].astype(BF16))
        dqm = jnp.concatenate([dqm_ref[h] for h in range(HEADS)], axis=1)
        dqb = _rope_t(dqm * SCALE_B, *tb, ROPE_DIM // 2).astype(BF16)
        dqb_ref[...] = dqb
        dcqn = _dot_nt(dqb, wuq_ref[...])
        dh1 = dh1 + piece(Z_QA, Z_KA, _rope_t(dqs_ref[...] * SCALE_A, *ta, A_HEAD_DIM // 2).astype(BF16))
        dh1 = dh1 + piece(Z_KA, Z_VA, _rope_t(dks_ref[...], *ta, A_HEAD_DIM // 2).astype(BF16))
        ckvn, rkv = _rms_stats(ckv_ref[...])
        dckv, dgkv = _rms_bwd(dckvn, ckvn, rkv, gkv_ref[...])
        dgkv_ref[...] += dgkv
        dh1 = dh1 + piece(Z_CKV, Z_KR, dckv.astype(BF16))
        dslot = dkm[:, 0:LANES]
        for h in range(1, HEADS):
            dslot = dslot + dkm[:, h * LANES:(h + 1) * LANES]
        dh1 = dh1 + piece(Z_KR, Z_GATE, _rope_t(dslot * c_ref[10:11, :], *tb, ROPE_DIM // 2).astype(BF16))
        cqn, rq = _rms_stats(cq_ref[...])
        dcq, dgq = _rms_bwd(dcqn, cqn, rq, gq_ref[...])
        dgq_ref[...] += dgq
        dh1 = dh1 + piece(Z_CQ, Z_CKV, dcq.astype(BF16))
        xn, r1 = _rms_stats(x_ref[...])
        d0, dg1 = _rms_bwd(dh1, xn, r1, g1_ref[...])
        dg1_ref[...] += dg1
        dx_ref[...] = dx1_ref[...] + d0

    def acc(n):
        return (_sds((1, n), F32), _full((1, n)))

    ins = [(dqs, _row(tm, 1024)), (dks, _row(tm, 256)), (dvs, _row(tm, 256)), (dqm, _heads(tm, HEADS)),
           (dkm, _row(tm, 1024)), (dvm, _row(tm, 1024))] + [(t, _row(tm, LANES)) for t in tabs] + [
           (consts, _full(consts.shape)), (cq, _row(tm, 256)), (ckv, _row(tm, 128)), (gq, _full(gq.shape)),
           (gkv, _full(gkv.shape)), (wuq, _full(wuq.shape)), (wk, _full(wk.shape)), (wv, _full(wv.shape)),
           (dgates, _row(tm, 2048)), (win, _resident(win.shape)), (x, _row(tm, D_MODEL)), (g1, _full(g1.shape)),
           (dx1, _row(tm, D_MODEL))]
    outs = [(_sds((t_rows, ZW), BF16), _row(tm, ZW)), (_sds((t_rows, 1024), BF16), _row(tm, 1024)),
            (_sds((t_rows, D_MODEL), F32), _row(tm, D_MODEL)), acc(256), acc(128), acc(D_MODEL)]
    return _rows_call("bwd_in", body, t_rows, tm, ins, outs)


def _pick_cols(n):
    best = LANES
    for d in range(LANES, min(n, 1408) + 1, LANES):
        if n % d == 0:
            best = d
    return best


def _mm_tn(name, a, b, column_shards=1, after=None):
    t_rows, m = a.shape
    n = b.shape[1]
    bk = min(1024, t_rows)
    bm, bn = _pick_cols(m), _pick_cols(n // column_shards)
    per_shard = n // column_shards // bn
    extra = () if after is None else (after,)

    def body(a_ref, b_ref, *rest):
        o_ref = rest[-1]

        @pl.when(pl.program_id(2) == 0)
        def _():
            o_ref[...] = jnp.zeros((bm, bn), F32)

        o_ref[...] += _dot_tn(a_ref[...].astype(BF16), b_ref[...].astype(BF16))

    return pl.pallas_call(
        body, name=name, grid=(m // bm, n // bn, t_rows // bk),
        in_specs=[pl.BlockSpec((bk, bm), lambda i, j, k: (k, i)), pl.BlockSpec((bk, bn), lambda i, j, k: (k, j))]
        + [pl.BlockSpec((8, LANES), lambda i, j, k: (0, 0))] * len(extra),
        out_specs=(pl.BlockSpec((bm, bn), lambda i, j, k: (i, j)) if column_shards == 1 else
                   pl.BlockSpec((None, bm, bn), lambda i, j, k: (j // per_shard, i, j % per_shard))),
        out_shape=_sds((m, n) if column_shards == 1 else (column_shards, m, n // column_shards), F32),
        compiler_params=pltpu.CompilerParams(dimension_semantics=("arbitrary",) * 3, vmem_limit_bytes=VMEM_LIMIT),
    )(a, b, *extra)


PACK_ROWS = 512


ADD_TILE_ELEMS = 1 << 17


def _add_rows(rows, cols):
    best = 16
    for d in range(16, rows + 1, 16):
        if rows % d == 0 and d * cols <= ADD_TILE_ELEMS:
            best = d
    assert rows % best == 0
    return best


def _add_pair(name, g, recv, half):
    _, _, rows, cols = g.shape
    t = _add_rows(rows, cols)

    def body(h_ref, g_ref, r_ref, o_ref):
        o_ref[...] = (g_ref[:, 0] + r_ref[...]).astype(BF16)

    spec = pl.BlockSpec((4, t, cols), lambda i, h: (0, i, 0))
    grid_spec = pltpu.PrefetchScalarGridSpec(
        num_scalar_prefetch=1, grid=(rows // t,),
        in_specs=[pl.BlockSpec((4, 1, t, cols), lambda i, h: (0, h[0], i, 0)), spec], out_specs=spec)
    return pl.pallas_call(body, name=name, grid_spec=grid_spec,
                          out_shape=_sds(recv.shape, BF16))(jnp.reshape(half, (1,)).astype(jnp.int32), g, recv)


def _add_chips(name, parts):
    _, rows, cols = parts.shape
    t = _add_rows(rows, cols)

    def body(p_ref, o_ref):
        acc = p_ref[0].astype(F32)
        for j in range(1, 4):
            acc = acc + p_ref[j].astype(F32)
        o_ref[...] = acc

    return pl.pallas_call(body, name=name, grid=(rows // t,),
                          in_specs=[pl.BlockSpec((4, t, cols), lambda i: (0, i, 0))],
                          out_specs=pl.BlockSpec((t, cols), lambda i: (i, 0)),
                          out_shape=_sds((rows, cols), F32))(parts)


def _add_devices(parts):
    n, rows, _ = parts.shape

    def body(p_ref, o_ref):
        acc = p_ref[0]
        for j in range(1, n):
            acc = acc + p_ref[j]
        o_ref[...] = acc

    return pl.pallas_call(body, name="small_add", grid=(1,),
                          in_specs=[pl.BlockSpec((n, rows, LANES), lambda i: (0, 0, 0))],
                          out_specs=pl.BlockSpec((rows, LANES), lambda i: (0, 0)),
                          out_shape=_sds((rows, LANES), F32))(parts)


def _adam_rows(k, n):
    target = max(8, (1 << 20) // (4 * n))
    if k <= target:
        return k
    best = None
    for d in range(8, target + 1, 8):
        if k % d == 0:
            best = d
    return best if best is not None else k


def _adam_update(w, g, m, v):
    m_ = ADAM_B1 * m + (1.0 - ADAM_B1) * g
    v_ = ADAM_B2 * v + (1.0 - ADAM_B2) * (g * g)
    delta = -ADAM_LR * ((m_ / (1.0 - ADAM_B1 ** ADAM_STEP)) / (jnp.sqrt(v_ / (1.0 - ADAM_B2 ** ADAM_STEP)) + ADAM_EPS)
                        + ADAM_WD * w)
    return delta, m_, v_


def _adamw(name, w, g, m, v):
    k, n = w.shape
    bk = _adam_rows(k, n)

    def body(w_ref, g_ref, m_ref, v_ref, d_ref, mo_ref, vo_ref):
        d_ref[...], mo_ref[...], vo_ref[...] = _adam_update(w_ref[...], g_ref[...], m_ref[...], v_ref[...])

    spec = pl.BlockSpec((bk, n), lambda i: (i, 0))
    out = pl.pallas_call(body, name=name, grid=(k // bk,), in_specs=[spec] * 4, out_specs=[spec] * 3,
                         out_shape=[_sds((k, n), F32)] * 3,
                         compiler_params=pltpu.CompilerParams(vmem_limit_bytes=VMEM_LIMIT))(w, g, m, v)
    return (g, *out)


def _adamw_halves(name, w, mine, theirs, m, v, half):
    k, n = w.shape
    bk = _adam_rows(k // 2, n)
    nb = k // 2 // bk

    def body(h_ref, w_ref, mine_ref, theirs_ref, m_ref, v_ref, g_ref, d_ref, mo_ref, vo_ref):
        g = jnp.where(pl.program_id(0) == h_ref[0], mine_ref[...], theirs_ref[...])
        g_ref[...] = g
        d_ref[...], mo_ref[...], vo_ref[...] = _adam_update(w_ref[...], g, m_ref[...], v_ref[...])

    full = pl.BlockSpec((bk, n), lambda h, i, c: (h * nb + i, 0))
    part = pl.BlockSpec((bk, n), lambda h, i, c: (i, 0))
    grid_spec = pltpu.PrefetchScalarGridSpec(num_scalar_prefetch=1, grid=(2, nb),
                                             in_specs=[full, part, part, full, full], out_specs=[full] * 4)
    return tuple(pl.pallas_call(
        body, name=name, grid_spec=grid_spec, out_shape=[_sds((k, n), F32)] * 4,
        compiler_params=pltpu.CompilerParams(vmem_limit_bytes=VMEM_LIMIT),
    )(jnp.reshape(half, (1,)).astype(jnp.int32), w, mine, theirs, m, v))


_HBM = pl.BlockSpec(memory_space=pltpu.HBM)


def _me():
    return lax.axis_index("x"), lax.axis_index("y"), lax.axis_index("c")


def _other_chips(x, y):
    return [(1 - x, y), (x, 1 - y), (1 - x, 1 - y)]


def _pass_to_sibling(zones):
    n = len(zones)

    def body(*refs):
        in_refs, out_refs = refs[:n], refs[n:2 * n]
        send_sems, recv_sems = refs[2 * n:]
        x, y, c = _me()
        sent = []
        for a, (in_ref, out_ref) in enumerate(zip(in_refs, out_refs)):
            for j, (cx, cy) in enumerate(_other_chips(x, y)):
                mine, theirs = (2 * cx + cy, c), (2 * cx + cy, 1 - c)
                sems = dict(send_sem=send_sems.at[3 * a + j], recv_sem=recv_sems.at[3 * a + j],
                            device_id=(x, y, 1 - c), device_id_type=MESH)
                sent.append((pltpu.make_async_remote_copy(src_ref=in_ref.at[mine], dst_ref=out_ref.at[mine], **sems),
                             pltpu.make_async_remote_copy(src_ref=in_ref.at[theirs], dst_ref=out_ref.at[theirs], **sems)))
        for send, _ in sent:
            send.start()
        for _, recv in sent:
            recv.wait_recv()
        for send, _ in sent:
            send.wait_send()

    return pl.pallas_call(
        body, name="pass_to_sibling", out_shape=[_sds(z.shape, z.dtype) for z in zones],
        in_specs=[_HBM] * n, out_specs=[_HBM] * n, input_output_aliases={i: i for i in range(n)},
        scratch_shapes=[pltpu.SemaphoreType.DMA((3 * n,)), pltpu.SemaphoreType.DMA((3 * n,))],
    )(*zones)


def _swap_sibling(name, vs, other_half=False):
    n = len(vs)

    def body(*refs):
        v_refs, out_refs = refs[:n], refs[n:2 * n]
        send_sems, recv_sems = refs[2 * n:]
        x, y, c = _me()
        cps = [pltpu.make_async_remote_copy(src_ref=v_ref.at[:, 1 - c] if other_half else v_ref, dst_ref=out_ref,
                                            send_sem=send_sems.at[a], recv_sem=recv_sems.at[a],
                                            device_id=(x, y, 1 - c), device_id_type=MESH)
               for a, (v_ref, out_ref) in enumerate(zip(v_refs, out_refs))]
        for cp in cps:
            cp.start()
        for cp in cps:
            cp.wait()

    def landing(v):
        return _sds((v.shape[0],) + v.shape[2:] if other_half else v.shape, v.dtype)

    return pl.pallas_call(
        body, name=name, out_shape=[landing(v) for v in vs], in_specs=[_HBM] * n, out_specs=[_HBM] * n,
        scratch_shapes=[pltpu.SemaphoreType.DMA((n,)), pltpu.SemaphoreType.DMA((n,))],
    )(*vs)


_SEM = pl.BlockSpec(memory_space=pltpu.SEMAPHORE)
_EFFECT = pltpu.SideEffectType.DATAFLOW_SIDE_EFFECTING
WHOLE = "whole"
PIECE = "piece"
SIBLING_HALF = "sibling"
MY_HALF = "half"
EVERYONE = "everyone"
_COPIES = {WHOLE: 3, PIECE: 3, MY_HALF: 3, SIBLING_HALF: 1, EVERYONE: 7}


def _landing_shape(v, mode):
    return {WHOLE: (4,) + v.shape, MY_HALF: (4,) + v.shape, PIECE: v.shape, EVERYONE: (8,) + v.shape,
            SIBLING_HALF: (v.shape[0],) + v.shape[2:]}[mode]


def _chip_copies(v_ref, land_ref, send_sems, recv_sems, mode, sem0=0):
    x, y, c = _me()
    if mode == SIBLING_HALF:
        cp = pltpu.make_async_remote_copy(src_ref=v_ref.at[:, 1 - c], dst_ref=land_ref, send_sem=send_sems.at[sem0],
                                          recv_sem=recv_sems.at[sem0], device_id=(x, y, 1 - c), device_id_type=MESH)
        return [(cp, cp)]
    if mode == EVERYONE:
        out = []
        for f in range(1, 8):
            px, py, pc = (1 - x if f & 4 else x), (1 - y if f & 2 else y), (1 - c if f & 1 else c)
            sems = dict(send_sem=send_sems.at[sem0 + f - 1], recv_sem=recv_sems.at[sem0 + f - 1],
                        device_id=(px, py, pc), device_id_type=MESH)
            out.append((pltpu.make_async_remote_copy(src_ref=v_ref, dst_ref=land_ref.at[4 * x + 2 * y + c], **sems),
                        pltpu.make_async_remote_copy(src_ref=v_ref, dst_ref=land_ref.at[4 * px + 2 * py + pc], **sems)))
        return out
    k = 2 * x + y
    out = []
    for j, (cx, cy) in enumerate(_other_chips(x, y)):
        if mode == MY_HALF:
            src, mine, theirs = v_ref.at[c], land_ref.at[k, c], land_ref.at[2 * cx + cy, c]
        else:
            src = v_ref.at[2 * cx + cy] if mode == PIECE else v_ref
            mine, theirs = land_ref.at[k], land_ref.at[2 * cx + cy]
        sems = dict(send_sem=send_sems.at[sem0 + j], recv_sem=recv_sems.at[sem0 + j], device_id=(cx, cy, c),
                    device_id_type=MESH)
        send = pltpu.make_async_remote_copy(src_ref=src, dst_ref=mine, **sems)
        recv = pltpu.make_async_remote_copy(src_ref=src, dst_ref=theirs, **sems)
        out.append((send, recv))
    return out


def _chips_start(name, vs, mode, after=None):
    n = len(vs)
    lands = [_landing_shape(v, mode) for v in vs]

    def body(*refs):
        v_refs, land_refs = refs[:n], refs[n:2 * n]
        send_sems, recv_sems = refs[-2 * n - 3], refs[-2 * n - 2]
        token = refs[-1]
        for a in range(n):
            for send, _ in _chip_copies(v_refs[a], land_refs[a], send_sems, recv_sems, mode, _COPIES[mode] * a):
                send.start()
        token[...] = jnp.zeros_like(token)

    extra = () if after is None else (after,)
    hbm = [pltpu.with_memory_space_constraint(v, pltpu.HBM) for v in vs]
    zones = [pltpu.with_memory_space_constraint(lax.empty(s, v.dtype), pltpu.HBM) for s, v in zip(lands, vs)]
    out = pl.pallas_call(
        body, name=name,
        out_shape=(pltpu.SemaphoreType.DMA((_COPIES[mode] * n,)), pltpu.SemaphoreType.DMA((_COPIES[mode] * n,)),
                   *[pltpu.HBM(v.shape, v.dtype) for v in vs], *[pltpu.HBM(s, v.dtype) for s, v in zip(lands, vs)],
                   _sds((8, LANES), F32)),
        in_specs=(_HBM,) * (2 * n) + (pl.BlockSpec(memory_space=pl.ANY),) * len(extra),
        out_specs=(_SEM, _SEM) + (_HBM,) * (2 * n) + (pl.BlockSpec(memory_space=pltpu.VMEM),),
        input_output_aliases={i: 2 + i for i in range(2 * n)},
        compiler_params=pltpu.CompilerParams(has_side_effects=_EFFECT),
    )(*hbm, *zones, *extra)
    return out[0], out[1], list(out[2:2 + n]), list(out[2 + n:2 + 2 * n]), out[-1]


def _chips_wait(name, send_sems, recv_sems, v_thru, land_thru, mode, after):
    n = len(v_thru)

    def body(*refs):
        v_refs, land_refs = refs[:n], refs[n:2 * n]
        send_sems, recv_sems = refs[2 * n], refs[2 * n + 1]
        for a in range(n):
            for send, recv in _chip_copies(v_refs[a], land_refs[a], send_sems, recv_sems, mode, _COPIES[mode] * a):
                send.wait_send()
                recv.wait_recv()

    out = pl.pallas_call(
        body, name=name,
        out_shape=tuple(pltpu.HBM(a.shape, a.dtype) for a in list(v_thru) + list(land_thru)),
        in_specs=(_HBM,) * (2 * n) + (_SEM, _SEM, pl.BlockSpec(memory_space=pl.ANY)), out_specs=(_HBM,) * (2 * n),
        input_output_aliases={i: i for i in range(2 * n)},
        compiler_params=pltpu.CompilerParams(has_side_effects=_EFFECT),
    )(*v_thru, *land_thru, send_sems, recv_sems, after)
    return list(out[:n]), list(out[n:])


_BIG = (("w_in", (1024, 3232), 1), ("w_uq", (256, 768), 1), ("w_ukv", (128, 1024), 1), ("w_branch_a", (512, 1024), 1),
        ("w_branch_b", (512, 1024), 1), ("w_out", (1024, 1024), 0), ("w_up", (1024, 5632), 1),
        ("w_down", (2816, 1024), 0), ("w_ple_gate", (1024, 1024), 0), ("w_ple", (256, 1024), 1))


def _shard_shape(shape, axis):
    return (shape[0] // 4, shape[1]) if axis == 0 else (shape[0], shape[1] // 4)


def _half_rows(shape, axis):
    k, n = _shard_shape(shape, axis)
    return k * n // (2 * LANES)


_EARLY = ("w_in", "w_uq", "w_ukv")
_LATE = ("w_branch_a", "w_branch_b", "w_out", "w_up", "w_down", "w_ple_gate", "w_ple")
_NATURAL = ("w_in", "w_up", "w_down", "w_out", "w_ple_gate")
_EARLY_PACKED = tuple(b for b in _BIG if b[0] in _EARLY and b[0] not in _NATURAL)
_LATE_PACKED = tuple(b for b in _BIG if b[0] in _LATE and b[0] not in _NATURAL)
_SHARD = {name: _shard_shape(shape, axis) for name, shape, axis in _BIG}


def _halves(a):
    return a.reshape(a.shape[:-2] + (2, a.shape[-2] // 2, a.shape[-1]))


def _rows_joined(a):
    return a.reshape(a.shape[:-3] + (a.shape[-3] * a.shape[-2], a.shape[-1]))


def _pack_pad(group):
    return -sum(_half_rows(shape, axis) for _, shape, axis in group) % PACK_ROWS


def _pack_shards(shards, dtype, group):
    parts = [shards[name].astype(dtype).reshape(2, _half_rows(shape, axis), LANES) for name, shape, axis in group]
    return jnp.concatenate(parts + [jnp.zeros((2, _pack_pad(group), LANES), dtype)], axis=1)


def _unpack_gathered(g, group):
    out, off = {}, 0
    for name, shape, axis in group:
        r = _half_rows(shape, axis)
        k, n = _shard_shape(shape, axis)
        w = g[:, :, off:off + r, :].reshape(4, k, n)
        out[name] = w.reshape(shape) if axis == 0 else w.transpose(1, 0, 2).reshape(shape)
        off += r
    return out


def _pack_grads(grads, group):
    parts = []
    for name, shape, axis in group:
        k, n = _shard_shape(shape, axis)
        g = grads[name]
        g4 = g.reshape(4, k, n) if axis == 0 else g.reshape(k, 4, n).transpose(1, 0, 2)
        parts.append(g4.reshape(4, 2, _half_rows(shape, axis), LANES))
    return jnp.concatenate(parts + [jnp.zeros((4, 2, _pack_pad(group), LANES), F32)], axis=2)


def _unpack_shard_grads(f, group):
    out, off = {}, 0
    for name, shape, axis in group:
        r = _half_rows(shape, axis)
        out[name] = f[:, off:off + r, :].reshape(_shard_shape(shape, axis))
        off += r
    return out


def _pad_slots(w, heads, dim, axis):
    if axis == 1:
        k = w.shape[0]
        return jnp.pad(w.reshape(k, heads, dim), ((0, 0), (0, 0), (0, LANES - dim))).reshape(k, heads * LANES)
    n = w.shape[1]
    return jnp.pad(w.reshape(heads, dim, n), ((0, 0), (0, LANES - dim), (0, 0))).reshape(heads * LANES, n)


def _unpad_slots(w, heads, dim, axis):
    if axis == 1:
        k = w.shape[0]
        return w.reshape(k, heads, LANES)[:, :, :dim].reshape(k, heads * dim)
    n = w.shape[1]
    return w.reshape(heads, LANES, n)[:, :dim, :].reshape(heads * dim, n)


def _pad_w_in(w):
    kr = jnp.pad(w[:, 1152:1184], ((0, 0), (NOPE_DIM, LANES - NOPE_DIM - ROPE_DIM)))
    return jnp.concatenate([_pad_slots(w[:, 0:512], HEADS, A_HEAD_DIM, 1),
                            _pad_slots(w[:, 512:640], A_KV_HEADS, A_HEAD_DIM, 1),
                            _pad_slots(w[:, 640:768], A_KV_HEADS, A_HEAD_DIM, 1),
                            w[:, 768:1024], w[:, 1024:1152], kr, w[:, 1184:3232]], axis=1)


def _unpad_w_in(w):
    return jnp.concatenate([_unpad_slots(w[:, Z_QA:Z_KA], HEADS, A_HEAD_DIM, 1),
                            _unpad_slots(w[:, Z_KA:Z_VA], A_KV_HEADS, A_HEAD_DIM, 1),
                            _unpad_slots(w[:, Z_VA:Z_CQ], A_KV_HEADS, A_HEAD_DIM, 1),
                            w[:, Z_CQ:Z_CKV], w[:, Z_CKV:Z_KR],
                            w[:, Z_KR + NOPE_DIM:Z_KR + NOPE_DIM + ROPE_DIM], w[:, Z_GATE:ZW]], axis=1)


_SMALL = (("attn_pre_norm", 1024), ("attn_post_norm", 1024), ("b_gate", 2048), ("sinks", 8), ("q_a_norm", 256),
          ("kv_a_norm", 128), ("mlp_pre_norm", 1024), ("mlp_post_norm", 1024), ("conv_b", 5632), ("ple_norm", 1024),
          ("conv_w", 3 * 5632), ("loss", 1))


def _small_rows(n):
    return 8 * -(-n // (8 * LANES))


def _pack_small(vals):
    parts = []
    for name, n in _SMALL:
        r = _small_rows(n)
        parts.append(jnp.pad(vals[name].reshape(-1), (0, r * LANES - n)).reshape(r, LANES))
    return jnp.concatenate(parts, axis=0)


def _unpack_small(buf):
    out, off = {}, 0
    for name, n in _SMALL:
        r = _small_rows(n)
        out[name] = buf[off:off + r].reshape(-1)[:n]
        off += r
    return out


def kernel(x, p, positions, attn_pre_norm, attn_post_norm, w_in, b_gate, sinks, q_a_norm, w_uq, kv_a_norm, w_ukv, w_branch_a, w_branch_b, w_out, mlp_pre_norm, mlp_post_norm, w_up, conv_w, conv_b, w_down, ple_norm, w_ple_gate, w_ple, loss_target, m_attn_pre_norm, m_attn_post_norm, m_w_in, m_b_gate, m_sinks, m_q_a_norm, m_w_uq, m_kv_a_norm, m_w_ukv, m_w_branch_a, m_w_branch_b, m_w_out, m_mlp_pre_norm, m_mlp_post_norm, m_w_up, m_conv_w, m_conv_b, m_w_down, m_ple_norm, m_w_ple_gate, m_w_ple, v_attn_pre_norm, v_attn_post_norm, v_w_in, v_b_gate, v_sinks, v_q_a_norm, v_w_uq, v_kv_a_norm, v_w_ukv, v_w_branch_a, v_w_branch_b, v_w_out, v_mlp_pre_norm, v_mlp_post_norm, v_w_up, v_conv_w, v_conv_b, v_w_down, v_ple_norm, v_w_ple_gate, v_w_ple):
    names = ["attn_pre_norm", "attn_post_norm", "w_in", "b_gate", "sinks", "q_a_norm", "w_uq", "kv_a_norm", "w_ukv",
             "w_branch_a", "w_branch_b", "w_out", "mlp_pre_norm", "mlp_post_norm", "w_up", "conv_w", "conv_b",
             "w_down", "ple_norm", "w_ple_gate", "w_ple"]
    wts = dict(zip(names, [attn_pre_norm, attn_post_norm, w_in, b_gate, sinks, q_a_norm, w_uq, kv_a_norm, w_ukv,
                           w_branch_a, w_branch_b, w_out, mlp_pre_norm, mlp_post_norm, w_up, conv_w, conv_b, w_down,
                           ple_norm, w_ple_gate, w_ple]))
    moms = dict(zip(names, [m_attn_pre_norm, m_attn_post_norm, m_w_in, m_b_gate, m_sinks, m_q_a_norm, m_w_uq,
                            m_kv_a_norm, m_w_ukv, m_w_branch_a, m_w_branch_b, m_w_out, m_mlp_pre_norm,
                            m_mlp_post_norm, m_w_up, m_conv_w, m_conv_b, m_w_down, m_ple_norm, m_w_ple_gate, m_w_ple]))
    vars_ = dict(zip(names, [v_attn_pre_norm, v_attn_post_norm, v_w_in, v_b_gate, v_sinks, v_q_a_norm, v_w_uq,
                             v_kv_a_norm, v_w_ukv, v_w_branch_a, v_w_branch_b, v_w_out, v_mlp_pre_norm,
                             v_mlp_post_norm, v_w_up, v_conv_w, v_conv_b, v_w_down, v_ple_norm, v_w_ple_gate, v_w_ple]))
    w2 = {n: a.reshape(a.shape[-2:]) for n, a in wts.items()}
    m2 = {n: a.reshape(a.shape[-2:]) for n, a in moms.items()}
    v2 = {n: a.reshape(a.shape[-2:]) for n, a in vars_.items()}

    t_rows = x.shape[-2]
    tm = min(256, t_rows)
    tm_wide = min(512, t_rows)
    xc, yc, cc = lax.axis_index("x"), lax.axis_index("y"), lax.axis_index("c")
    chip = 2 * xc + yc

    x2d = x.reshape(t_rows, D_MODEL)
    p2d = p.reshape(t_rows, PLE_DIM)
    tgt = loss_target.reshape(t_rows, D_MODEL)
    pos_f = positions.reshape(t_rows, 1).astype(F32)

    def own_slot_filled(gathered, mine):
        return [lax.dynamic_update_slice(g, m[None], (chip, 0, 0, 0)) for g, m in zip(gathered, mine)]

    def shard_lists(group, packed_group, token=0.0):
        ws = {n: w2[n] + token for n in group}
        return [_halves(ws[n].astype(BF16)) for n in group if n in _NATURAL] + [_pack_shards(ws, BF16, packed_group)]

    cw_rows = 3 * 1408 // LANES
    conv_mine = jnp.pad(w2["conv_w"].reshape(cw_rows, LANES), ((0, 48 - cw_rows), (0, 0))).reshape(2, 24, LANES)
    early_mine = shard_lists(_EARLY, _EARLY_PACKED) + [conv_mine]
    early_sems = _chips_start("gather_early_start", early_mine, MY_HALF)
    early_token = early_sems[4][0:1, 0:1]
    consts = _rope_consts()
    tabs = _rope_tables(pos_f + early_token, consts, tm)
    late_mine = shard_lists(_LATE, _LATE_PACKED, early_token)
    both_done = tabs[0][0:1, 0:1] + sum(m[0, 0:1, 0:1].astype(F32) for m in late_mine)
    early_sent, early_landed = _chips_wait("gather_early_wait", *early_sems[:4], MY_HALF, after=both_done)
    early = own_slot_filled(_pass_to_sibling(early_landed), early_sent)
    late_names = [n for n in _LATE if n in _NATURAL]
    first = [late_names.index("w_out"), len(late_names)]
    late_a = [late_mine[i] for i in first]
    late_b = [m for i, m in enumerate(late_mine) if i not in first]
    late_a_sems = _chips_start("gather_late_a_start", late_a, WHOLE, after=early[0])
    late_b_sems = _chips_start("gather_late_b_start", late_b, WHOLE, after=late_a_sems[4])
    late_token = late_b_sems[4][0:1, 0:1]
    full = _unpack_gathered(early[1], _EARLY_PACKED)
    full["w_in"] = _rows_joined(early[0]).transpose(1, 0, 2).reshape(D_MODEL, 3232)
    conv_full = early[2].reshape(4, 48, LANES)[:, :cw_rows].reshape(4, 3, 1408).transpose(1, 0, 2).reshape(3, 2 * D_FF)
    convw8 = jnp.pad(conv_full, ((0, 5), (0, 0)))

    win = _pad_w_in(full["w_in"])
    wuq = _pad_slots(full["w_uq"], HEADS, NOPE_DIM + ROPE_DIM, 1)
    ukv = full["w_ukv"].reshape(KV_LORA, HEADS, NOPE_DIM + V_DIM)
    wk = _pad_slots(ukv[:, :, :NOPE_DIM].reshape(KV_LORA, HEADS * NOPE_DIM), HEADS, NOPE_DIM, 1)
    wv = _pad_slots(ukv[:, :, NOPE_DIM:].reshape(KV_LORA, HEADS * V_DIM), HEADS, V_DIM, 1)
    g1, g2, g3, g4, g5 = (w2["attn_pre_norm"], w2["attn_post_norm"], w2["mlp_pre_norm"], w2["mlp_post_norm"],
                          w2["ple_norm"])
    gq, gkv, bg, convb = w2["q_a_norm"], w2["kv_a_norm"], w2["b_gate"], w2["conv_b"]
    swa_tile = min(SWA_TILE, t_rows)
    sink_rows = jnp.repeat(w2["sinks"].reshape(A_KV_HEADS, SWA_GROUP, 1), swa_tile, axis=2).reshape(
        A_KV_HEADS, 1, SWA_GROUP * swa_tile)
    swa_bias = _swa_bias(swa_tile)

    h1, qs, ks, vs, cq, cqn, ckv, ckvn, qm, km, vm, gate = _fwd_in(x2d, g1, win, bg + late_token, gq, gkv, wuq, wk, wv,
                                                                   tabs, tm_wide)
    ya, lse_a = _swa_fwd(qs, ks, vs, swa_bias, sink_rows)
    yb, lse_b = _mla_fwd(qm, km, vm)
    late_sent, late_landed = _chips_wait("gather_late_a_wait", *late_a_sems[:4], WHOLE, after=yb)
    wout_g, packed_g = own_slot_filled(late_landed, late_sent)
    full = _unpack_gathered(packed_g, _LATE_PACKED)
    wba = _pad_slots(full["w_branch_a"], HEADS, A_HEAD_DIM, 0)
    wbb = _pad_slots(full["w_branch_b"], HEADS, V_DIM, 0)
    wple = full["w_ple"]
    wout = _rows_joined(wout_g).reshape(-1, D_MODEL)
    pa, pb, mixed, o, x1, h2 = _fwd_mix(x2d, ya, yb, gate, wba, wbb, wout, g2, g3, tm_wide)
    late_sent, late_landed = _chips_wait("gather_late_b_wait", *late_b_sems[:4], WHOLE, after=pa)
    natural = dict(zip([n for n in late_names if n != "w_out"], own_slot_filled(late_landed, late_sent)))
    wup = _rows_joined(natural["w_up"])
    wdown, wpg = (_rows_joined(natural[n]).reshape(-1, D_MODEL) for n in ("w_down", "w_ple_gate"))
    up, a = _fwd_up(h2, wup, convw8, convb, tm)
    ff, x2, e, n5, sg, dx3, loss_part = _fwd_out(a, wdown, x1, g4, p2d, wple, g5, wpg, tgt, tm_wide)

    dpre, de, dx2, dff, du, dg5, dg4, dconvb, dconvw8 = _bwd_out(dx3, e, sg, x2, ff, g5, g4, wpg, wdown, up, convw8,
                                                                 convb, tm)
    dup, dx1, do, dpa, dpb, dgates, dya, dyb, delta_b, dg3, dg2, dbg = _bwd_mid(
        du, convw8, wup, dx2, x1, g3, o, g2, wout, gate, pa, pb, wba, wbb, yb, tm)
    late_grads = {
        "w_branch_a": _unpad_slots(_mm_tn("dw_branch_a", ya, dpa), HEADS, A_HEAD_DIM, 0),
        "w_branch_b": _unpad_slots(_mm_tn("dw_branch_b", yb, dpb), HEADS, V_DIM, 0),
        "w_out": _mm_tn("dw_out", mixed, do).reshape(4, D_MODEL // 4, D_MODEL),
        "w_up": _mm_tn("dw_up", h2, dup, column_shards=4),
        "w_down": _mm_tn("dw_down", a, dff).reshape(4, D_FF // 4, D_MODEL),
        "w_ple_gate": _mm_tn("dw_ple_gate", n5, dpre).reshape(4, D_MODEL // 4, D_MODEL),
        "w_ple": _mm_tn("dw_ple", p2d, de),
    }

    def grad_views(grads, group, packed_group):
        return [_halves(grads[n]) for n in group if n in _NATURAL] + [_pack_grads(grads, packed_group)]

    def pair_sums(tag, views, theirs):
        return [_add_pair("rs_%s_add_pair_%d" % (tag, i), g, r, cc) for i, (g, r) in enumerate(zip(views, theirs))]

    swap_sems = _chips_start("swap_late_start", grad_views(late_grads, _LATE, _LATE_PACKED), SIBLING_HALF)
    dqs, dks, dvs, dsink_rows = _swa_bwd(qs, ks, vs, ya, dya, lse_a, swa_bias, sink_rows + swap_sems[4][0:1, 0:1])
    dsink = dsink_rows[:, 0:SWA_GROUP, 0]
    late_views, late_theirs = _chips_wait("swap_late_wait", *swap_sems[:4], SIBLING_HALF, after=dqs)
    rs_sems = _chips_start("scatter_late_start", pair_sums("late", late_views, late_theirs), PIECE)
    dqm, dkm, dvm = _mla_bwd(qm, km, vm, dyb, lse_b, delta_b.reshape(HEADS, 1, t_rows) + rs_sems[4][0:1, 0:1])
    dz, dqb, dx, dgq, dgkv, dg1 = _bwd_in(dqs, dks, dvs, dqm, dkm, dvm, tabs, consts, cq, ckv, gq, gkv, wuq, wk, wv,
                                           dgates, win, x2d, g1, dx1, tm)

    small = {"attn_pre_norm": dg1, "attn_post_norm": dg2, "b_gate": dbg, "sinks": dsink, "q_a_norm": dgq,
             "kv_a_norm": dgkv, "mlp_pre_norm": dg3, "mlp_post_norm": dg4, "conv_b": dconvb, "ple_norm": dg5,
             "conv_w": dconvw8[0:3], "loss": loss_part}
    small_sems = _chips_start("gather_small_start", [_pack_small(small)], EVERYONE)
    small_token = small_sems[4]

    dwk = _unpad_slots(_mm_tn("dw_k", ckvn, dkm, after=small_token), HEADS, NOPE_DIM, 1).reshape(
        KV_LORA, HEADS, NOPE_DIM)
    dwv = _unpad_slots(_mm_tn("dw_v", ckvn, dvm, after=small_token), HEADS, V_DIM, 1).reshape(KV_LORA, HEADS, V_DIM)
    early_grads = {
        "w_in": _unpad_w_in(_mm_tn("dw_in", h1, dz, after=small_token)).reshape(D_MODEL, 4, 808).transpose(1, 0, 2),
        "w_uq": _unpad_slots(_mm_tn("dw_uq", cqn, dqb, after=small_token), HEADS, NOPE_DIM + ROPE_DIM, 1),
        "w_ukv": jnp.concatenate([dwk, dwv], axis=2).reshape(KV_LORA, HEADS * (NOPE_DIM + V_DIM)),
    }

    def finish(tag, pairs, landed, group, packed_group):
        reduced = []
        for i, (pair, land) in enumerate(zip(pairs, landed)):
            own = lax.dynamic_index_in_dim(pair, chip, 0, keepdims=True)
            reduced.append(_add_chips("rs_%s_add_chips_%d" % (tag, i),
                                      lax.dynamic_update_slice(land, own, (chip, 0, 0))))
        others = _swap_sibling("swap_%s_reduced_halves" % tag, reduced)
        r, o = reduced[-1], others[-1]
        packed = jnp.where(cc == 0, jnp.stack([r, o]), jnp.stack([o, r]))
        for n, g in _unpack_shard_grads(packed, packed_group).items():
            updates[n] = _adamw("adamw_" + n, w2[n], g, m2[n], v2[n])
        for n, r, o in zip([n for n in group if n in _NATURAL], reduced, others):
            updates[n] = _adamw_halves("adamw_" + n, w2[n], r, o, m2[n], v2[n], cc)

    updates = {}

    def adamw(n, g):
        updates[n] = _adamw("adamw_" + n, w2[n], g, m2[n], v2[n])

    early_views = grad_views(early_grads, _EARLY, _EARLY_PACKED)
    early_theirs = _swap_sibling("swap_early_grad_halves", early_views, other_half=True)
    small_sent, small_landed = _chips_wait("gather_small_wait", *small_sems[:4], EVERYONE, after=early_theirs[0])
    small_all = lax.dynamic_update_slice(small_landed[0], small_sent[0][None], (4 * xc + 2 * yc + cc, 0, 0))
    early_sems = _chips_start("scatter_early_start", pair_sums("early", early_views, early_theirs), PIECE,
                              after=small_all)
    late_pairs, late_landed = _chips_wait("scatter_late_wait", *rs_sems[:4], PIECE, after=early_sems[4])
    finish("late", late_pairs, late_landed, _LATE, _LATE_PACKED)
    early_pairs, early_landed = _chips_wait("scatter_early_wait", *early_sems[:4], PIECE,
                                            after=updates[_LATE[-1]][1])
    finish("early", early_pairs, early_landed, _EARLY, _EARLY_PACKED)

    small_sum = _unpack_small(_add_devices(small_all))
    for n in names:
        if n == "conv_w":
            adamw(n, lax.dynamic_index_in_dim(small_sum[n].reshape(3, 4, 1408), chip, 1, keepdims=False))
        elif n in small_sum:
            adamw(n, small_sum[n].reshape(w2[n].shape))
    loss = small_sum["loss"][0]

    outs = [[updates[n][i].reshape(wts[n].shape) for n in names] for i in range(4)]
    return (loss, dx.reshape(x.shape), *outs[0], *outs[1], *outs[2], *outs[3])
```

```python
import functools
import math

import numpy as np
import jax
import jax.numpy as jnp
from jax import lax
from jax.experimental import pallas as pl
from jax.experimental.pallas import tpu as pltpu

F32 = jnp.float32
BF16 = jnp.bfloat16

D_MODEL = 1024
D_FF = 2816
PLE_DIM = 256
ROPE_THETA = 10000.0
RMS_EPS = 1e-6
SWA_WINDOW = 128
HEADS = 8
A_KV_HEADS = 2
A_HEAD_DIM = 64
Q_LORA = 256
KV_LORA = 128
NOPE_DIM = 64
ROPE_DIM = 32
V_DIM = 64
LANES = 128
ZW = 3328
NEG = -1e30
SCALE_A = A_HEAD_DIM ** -0.5
SCALE_B = (NOPE_DIM + ROPE_DIM) ** -0.5

ADAM_LR = 0.001
ADAM_B1 = 0.9
ADAM_B2 = 0.999
ADAM_EPS = 1e-08
ADAM_WD = 0.01
ADAM_STEP = 10

VMEM_LIMIT = 60 * 1024 * 1024
MESH_AXES = ("x", "y", "c")
MESH = pl.DeviceIdType.MESH

Z_QA, Z_KA, Z_VA, Z_CQ, Z_CKV, Z_KR, Z_GATE = 0, 512, 640, 768, 1024, 1152, 1280


def _dot(a, b):
    return jnp.dot(a, b, preferred_element_type=F32)


def _dot_nt(a, b):
    return lax.dot_general(a, b, (((1,), (1,)), ((), ())), preferred_element_type=F32)


def _dot_tn(a, b):
    return lax.dot_general(a, b, (((0,), (0,)), ((), ())), preferred_element_type=F32)


def _rms_stats(x):
    r = lax.rsqrt(jnp.mean(x * x, axis=-1, keepdims=True) + RMS_EPS)
    return x * r, r


def _rms_bwd(dy, xn, r, g):
    dxn = dy * g
    dx = r * (dxn - xn * jnp.mean(dxn * xn, axis=-1, keepdims=True))
    dg = jnp.sum(dy * xn, axis=0, keepdims=True)
    return dx, dg


def _tile_lanes(t, n):
    return t if n == 1 else jnp.concatenate([t] * n, axis=1)


def _rope(x, c, s1, s2, half):
    w = x.shape[1]
    n = w // LANES
    return (x * _tile_lanes(c, n) + pltpu.roll(x, w - half, 1) * _tile_lanes(s1, n)
            + pltpu.roll(x, half, 1) * _tile_lanes(s2, n))


def _rope_t(dy, c, s1, s2, half):
    w = dy.shape[1]
    n = w // LANES
    return (dy * _tile_lanes(c, n) + pltpu.roll(dy * _tile_lanes(s1, n), half, 1)
            + pltpu.roll(dy * _tile_lanes(s2, n), w - half, 1))


def _sigmoid(x):
    return 1.0 / (1.0 + jnp.exp(-x))


_GELU_C = math.sqrt(2.0 / math.pi)


def _gelu_and_grad(x):
    a = _GELU_C + (_GELU_C * 0.044715) * (x * x)
    th = jnp.tanh(x * a)
    hx = 0.5 * x
    p1 = 1.0 + th
    gel = hx * p1
    dgel = 0.5 * p1 + (hx * (1.0 - th * th)) * (3.0 * a - 2.0 * _GELU_C)
    return gel, dgel


def _conv_taps(up, h6, h7):
    r1 = pltpu.roll(up, 1, 0)
    r2 = pltpu.roll(up, 2, 0)
    rows = lax.broadcasted_iota(jnp.int32, (8, up.shape[1]), 0)
    xm1 = jnp.concatenate([jnp.where(rows == 0, h7, r1[0:8]), r1[8:]], axis=0)
    xm2 = jnp.concatenate([jnp.where(rows == 0, h6, jnp.where(rows == 1, h7, r2[0:8])), r2[8:]], axis=0)
    return xm1, xm2


def _conv_taps_next(du, n0, n1):
    tm = du.shape[0]
    r1 = pltpu.roll(du, tm - 1, 0)
    r2 = pltpu.roll(du, tm - 2, 0)
    rows = lax.broadcasted_iota(jnp.int32, (8, du.shape[1]), 0)
    xp1 = jnp.concatenate([r1[:tm - 8], jnp.where(rows == 7, n0, r1[tm - 8:])], axis=0)
    xp2 = jnp.concatenate([r2[:tm - 8], jnp.where(rows == 6, n0, jnp.where(rows == 7, n1, r2[tm - 8:]))], axis=0)
    return xp1, xp2


def _row(tm, n):
    return pl.BlockSpec((tm, n), lambda i: (i, 0))


def _full(shape):
    nd = len(shape)
    return pl.BlockSpec(tuple(shape), lambda i: (0,) * nd)


def _resident(shape):
    nd = len(shape)
    return pl.BlockSpec(tuple(shape), lambda i: (0,) * nd, pipeline_mode=pl.Buffered(1))


def _heads(tm, h):
    return pl.BlockSpec((h, tm, LANES), lambda i: (0, i, 0))


def _rows_call(name, body, t_rows, tm, ins, outs, scratch=()):
    return pl.pallas_call(
        body, name=name, grid=(t_rows // tm,),
        in_specs=[s for _, s in ins],
        out_specs=[s for _, s in outs],
        out_shape=[s for s, _ in outs],
        scratch_shapes=list(scratch),
        compiler_params=pltpu.CompilerParams(dimension_semantics=("arbitrary",), vmem_limit_bytes=VMEM_LIMIT),
    )(*[a for a, _ in ins])


def _sds(shape, dtype):
    return jax.ShapeDtypeStruct(tuple(shape), dtype)


def _rope_consts():
    c = np.zeros((16, LANES), np.float32)
    lane = np.arange(LANES)
    inv_a = (ROPE_THETA ** (-(np.arange(0, A_HEAD_DIM, 2, dtype=np.float32) / A_HEAD_DIM))).astype(np.float32)
    in_a = lane < A_HEAD_DIM
    c[0, in_a] = inv_a[lane[in_a] % (A_HEAD_DIM // 2)]
    c[1, in_a] = 1.0
    c[2, lane < A_HEAD_DIM // 2] = -1.0
    c[3, (lane >= A_HEAD_DIM // 2) & in_a] = 1.0
    inv_b = (ROPE_THETA ** (-(np.arange(0, ROPE_DIM, 2, dtype=np.float32) / ROPE_DIM))).astype(np.float32)
    pe = (lane >= NOPE_DIM) & (lane < NOPE_DIM + ROPE_DIM)
    c[5, pe] = inv_b[(lane[pe] - NOPE_DIM) % (ROPE_DIM // 2)]
    c[6, pe] = 1.0
    c[7, (lane >= NOPE_DIM) & (lane < NOPE_DIM + ROPE_DIM // 2)] = -1.0
    c[8, (lane >= NOPE_DIM + ROPE_DIM // 2) & (lane < NOPE_DIM + ROPE_DIM)] = 1.0
    c[9, lane < NOPE_DIM] = 1.0
    c[10, pe] = 1.0
    return jnp.asarray(c)


def _rope_tables(pos_f, consts, tm):
    t_rows = pos_f.shape[0]

    def body(pos_ref, c_ref, ca, sa1, sa2, cb, sb1, sb2):
        ang = pos_ref[...] * (c_ref[0:1, :] + c_ref[5:6, :])
        cs, sn = jnp.cos(ang), jnp.sin(ang)
        for ref, row in ((ca, 1), (sa1, 2), (sa2, 3)):
            half = (cs if row == 1 else sn) * c_ref[row:row + 1, :]
            ref[...] = half + pltpu.roll(half, A_HEAD_DIM, 1)
        cb[...] = cs * c_ref[6:7, :] + c_ref[9:10, :]
        sb1[...] = sn * c_ref[7:8, :]
        sb2[...] = sn * c_ref[8:9, :]

    tab = (_sds((t_rows, LANES), F32), _row(tm, LANES))
    return _rows_call("rope_tables", body, t_rows, tm,
                      [(pos_f, _row(tm, 1)), (consts, _full(consts.shape))], [tab] * 6)


def _fwd_in(x, g1, win, bg, gq, gkv, wuq, wk, wv, eq, ek, tabs, tm):
    t_rows = x.shape[0]

    def body(x_ref, g1_ref, win_ref, bg_ref, gq_ref, gkv_ref, wuq_ref, wk_ref, wv_ref, eq_ref, ek_ref,
             ca, sa1, sa2, cb, sb1, sb2,
             h1_ref, qs_ref, ks_ref, vs_ref, cq_ref, cqn_ref, ckv_ref, ckvn_ref, qm_ref, km_ref, vm_ref, gate_ref):
        xn, _ = _rms_stats(x_ref[...])
        hb = (xn * g1_ref[...]).astype(BF16)
        h1_ref[...] = hb
        ta = (ca[...], sa1[...], sa2[...])
        tb = (cb[...], sb1[...], sb2[...])
        cq = _dot(hb, win_ref[:, Z_CQ:Z_CKV])
        ckv = _dot(hb, win_ref[:, Z_CKV:Z_KR])
        z_qa = _dot(hb, win_ref[:, Z_QA:Z_KA])
        z_ka = _dot(hb, win_ref[:, Z_KA:Z_VA])
        z_va = _dot(hb, win_ref[:, Z_VA:Z_CQ])
        z_kr = _dot(hb, win_ref[:, Z_KR:Z_GATE])
        cq_ref[...] = cq
        cqn, _ = _rms_stats(cq)
        cqb = (cqn * gq_ref[...]).astype(BF16)
        cqn_ref[...] = cqb
        ckv_ref[...] = ckv
        ckvn, _ = _rms_stats(ckv)
        ckvb = (ckvn * gkv_ref[...]).astype(BF16)
        ckvn_ref[...] = ckvb
        z_qm = _dot(cqb, wuq_ref[...])
        z_km = _dot(ckvb, wk_ref[...])
        z_vm = _dot(ckvb, wv_ref[...])
        z_gate = _dot(hb, win_ref[:, Z_GATE:ZW])
        qs_ref[...] = _dot((_rope(z_qa, *ta, A_HEAD_DIM // 2) * SCALE_A).astype(BF16), eq_ref[...]).astype(BF16)
        ks_ref[...] = _dot(_rope(z_ka, *ta, A_HEAD_DIM // 2).astype(BF16), ek_ref[...]).astype(BF16)
        vs_ref[...] = _dot(z_va.astype(BF16), ek_ref[...]).astype(BF16)
        qm_ref[...] = (_rope(z_qm, *tb, ROPE_DIM // 2) * SCALE_B).astype(BF16)
        km_ref[...] = (z_km + _tile_lanes(_rope(z_kr, *tb, ROPE_DIM // 2), HEADS)).astype(BF16)
        vm_ref[...] = z_vm.astype(BF16)
        gate_ref[...] = _sigmoid(z_gate + bg_ref[...])

    def o(n, dt):
        return (_sds((t_rows, n), dt), _row(tm, n))

    ins = [(x, _row(tm, D_MODEL)), (g1, _full(g1.shape)), (win, _resident(win.shape)), (bg, _full(bg.shape)),
           (gq, _full(gq.shape)), (gkv, _full(gkv.shape)), (wuq, _full(wuq.shape)), (wk, _full(wk.shape)),
           (wv, _full(wv.shape)), (eq, _full(eq.shape)), (ek, _full(ek.shape))] + [(t, _row(tm, LANES)) for t in tabs]
    outs = [o(1024, BF16), o(1024, BF16), o(256, BF16), o(256, BF16), o(256, F32), o(256, BF16), o(128, F32),
            o(128, BF16), o(1024, BF16), o(1024, BF16), o(1024, BF16), o(2048, F32)]
    return _rows_call("fwd_in", body, t_rows, tm, ins, outs)


def _attn_tile(t_rows):
    return min(512, t_rows)


MLA_HEADS_PER_STEP = 4
MLA_FWD_HEADS_PER_STEP = 8


def _causal_pairs(nq, by_kv):
    if by_kv:
        pairs = [(i, j) for j in range(nq) for i in range(j, nq)]
    else:
        pairs = [(i, j) for i in range(nq) for j in range(i + 1)]
    return (jnp.asarray([p[0] for p in pairs], jnp.int32), jnp.asarray([p[1] for p in pairs], jnp.int32))


def _mla_fwd(q, k, v):
    t_rows = q.shape[0]
    t = _attn_tile(t_rows)
    hp = MLA_FWD_HEADS_PER_STEP
    w = hp * LANES
    ii, jj = _causal_pairs(t_rows // t, by_kv=False)

    def body(i_ref, j_ref, q_ref, k_ref, v_ref, o_ref, lse_ref, m_s, l_s, acc_s):
        i = i_ref[pl.program_id(1)]
        j = j_ref[pl.program_id(1)]

        @pl.when(j == 0)
        def _():
            m_s[...] = jnp.full(m_s.shape, NEG, F32)
            l_s[...] = jnp.zeros(l_s.shape, F32)
            acc_s[...] = jnp.zeros(acc_s.shape, F32)

        def step(diagonal):
            sls = [slice(hh * LANES, (hh + 1) * LANES) for hh in range(hp)]
            scores = [_dot_nt(k_ref[:, sl], q_ref[:, sl]) for sl in sls]
            if diagonal:
                valid = (lax.broadcasted_iota(jnp.int32, (t, t), 0) <= lax.broadcasted_iota(jnp.int32, (t, t), 1))
                scores = [jnp.where(valid, s, NEG) for s in scores]
            stats = []
            for hh, s in enumerate(scores):
                m_prev = m_s[hh]
                m_new = jnp.maximum(m_prev, jnp.max(s, axis=0, keepdims=True))
                p = jnp.exp(s - m_new)
                alpha = jnp.exp(m_prev - m_new)
                stats.append((m_new, alpha, alpha * l_s[hh] + jnp.sum(p, axis=0, keepdims=True), p.astype(BF16)))
            for hh, (m_new, alpha, l_new, p) in enumerate(stats):
                sl = sls[hh]
                acc = alpha * acc_s[hh] + _dot_tn(v_ref[:, sl], p)
                if diagonal:
                    o_ref[:, sl] = (acc / l_new).T.astype(o_ref.dtype)
                    lse_ref[hh] = m_new + jnp.log(l_new)
                else:
                    m_s[hh] = m_new
                    l_s[hh] = l_new
                    acc_s[hh] = acc

        pl.when(j < i)(lambda: step(False))
        pl.when(j == i)(lambda: step(True))

    grid_spec = pltpu.PrefetchScalarGridSpec(
        num_scalar_prefetch=2, grid=(HEADS // hp, ii.shape[0]),
        in_specs=[pl.BlockSpec((t, w), lambda hb, s, ir, jr: (ir[s], hb)),
                  pl.BlockSpec((t, w), lambda hb, s, ir, jr: (jr[s], hb)),
                  pl.BlockSpec((t, w), lambda hb, s, ir, jr: (jr[s], hb))],
        out_specs=[pl.BlockSpec((t, w), lambda hb, s, ir, jr: (ir[s], hb)),
                   pl.BlockSpec((hp, 1, t), lambda hb, s, ir, jr: (hb, 0, ir[s]))],
        scratch_shapes=[pltpu.VMEM((hp, 1, t), F32), pltpu.VMEM((hp, 1, t), F32), pltpu.VMEM((hp, LANES, t), F32)])
    return pl.pallas_call(
        body, name="mla_fwd", grid_spec=grid_spec,
        out_shape=[_sds((t_rows, HEADS * LANES), BF16), _sds((HEADS, 1, t_rows), F32)],
        compiler_params=pltpu.CompilerParams(dimension_semantics=("arbitrary",) * 2, vmem_limit_bytes=VMEM_LIMIT),
    )(ii, jj, q, k, v)


def _mla_bwd(q, k, v, do, lse, delta):
    t_rows = q.shape[0]
    t = _attn_tile(t_rows)
    hp = MLA_HEADS_PER_STEP
    w = hp * LANES
    ii, jj = _causal_pairs(t_rows // t, by_kv=True)

    def body(i_ref, j_ref, q_ref, k_ref, v_ref, do_ref, lse_ref, dl_ref, dq_ref, dk_ref, dv_ref):
        i = i_ref[pl.program_id(1)]
        j = j_ref[pl.program_id(1)]

        @pl.when(pl.program_id(1) == 0)
        def _():
            dq_ref[...] = jnp.zeros(dq_ref.shape, F32)

        def step(diagonal):
            r0 = pl.multiple_of(i * t, t)
            sls = [slice(hh * LANES, (hh + 1) * LANES) for hh in range(hp)]
            scores = [_dot_nt(k_ref[:, sl], q_ref[:, sl]) for sl in sls]
            if diagonal:
                valid = (lax.broadcasted_iota(jnp.int32, (t, t), 0) <= lax.broadcasted_iota(jnp.int32, (t, t), 1))
                scores = [jnp.where(valid, s, NEG) for s in scores]
            dps = [_dot_nt(v_ref[:, sl], do_ref[:, sl]) for sl in sls]
            ps = [jnp.exp(s - lse_ref[hh]) for hh, s in enumerate(scores)]
            dss = [(p * (dp - dl_ref[hh])).astype(BF16) for hh, (p, dp) in enumerate(zip(ps, dps))]
            for hh, sl in enumerate(sls):
                dv = _dot(ps[hh].astype(BF16), do_ref[:, sl])
                dk = _dot(dss[hh], q_ref[:, sl])
                if diagonal:
                    dv_ref[:, sl] = dv
                    dk_ref[:, sl] = dk
                else:
                    dv_ref[:, sl] += dv
                    dk_ref[:, sl] += dk
                dq_ref[hh, pl.ds(r0, t), :] += _dot_tn(dss[hh], k_ref[:, sl])

        pl.when(i > j)(lambda: step(False))
        pl.when(i == j)(lambda: step(True))

    def qmap(hb, s, ir, jr):
        return (ir[s], hb)

    def kvmap(hb, s, ir, jr):
        return (jr[s], hb)

    def rowmap(hb, s, ir, jr):
        return (hb, 0, ir[s])

    grid_spec = pltpu.PrefetchScalarGridSpec(
        num_scalar_prefetch=2, grid=(HEADS // hp, ii.shape[0]),
        in_specs=[pl.BlockSpec((t, w), qmap), pl.BlockSpec((t, w), kvmap), pl.BlockSpec((t, w), kvmap),
                  pl.BlockSpec((t, w), qmap), pl.BlockSpec((hp, 1, t), rowmap), pl.BlockSpec((hp, 1, t), rowmap)],
        out_specs=[pl.BlockSpec((hp, t_rows, LANES), lambda hb, s, ir, jr: (hb, 0, 0)),
                   pl.BlockSpec((t, w), kvmap), pl.BlockSpec((t, w), kvmap)])
    return pl.pallas_call(
        body, name="mla_bwd", grid_spec=grid_spec,
        out_shape=[_sds((HEADS, t_rows, LANES), F32), _sds((t_rows, HEADS * LANES), F32),
                   _sds((t_rows, HEADS * LANES), F32)],
        compiler_params=pltpu.CompilerParams(dimension_semantics=("arbitrary",) * 2, vmem_limit_bytes=VMEM_LIMIT),
    )(ii, jj, q, k, v, do, lse, delta)


SWA_TILE = 2 * SWA_WINDOW
SWA_GROUP = HEADS // A_KV_HEADS


def _swa_bias(tq):
    koff = lax.broadcasted_iota(jnp.int32, (tq + SWA_WINDOW, SWA_GROUP * tq), 0) - SWA_WINDOW
    qoff = (lax.broadcasted_iota(jnp.int32, (tq + SWA_WINDOW, SWA_GROUP * tq), 1) % tq)
    band = (koff <= qoff) & (qoff - koff < SWA_WINDOW)
    return jnp.stack([jnp.where(band & (koff >= 0), 0.0, NEG), jnp.where(band, 0.0, NEG)]).astype(F32)


def _swa_specs(tq, nq):
    wb = tq // SWA_WINDOW
    kvw = A_KV_HEADS * LANES

    def qi(i):
        return jnp.minimum(i, nq - 1)

    q = pl.BlockSpec((tq, HEADS * LANES), lambda i: (qi(i), 0))
    cur = pl.BlockSpec((tq, kvw), lambda i: (qi(i), 0))
    prev = pl.BlockSpec((SWA_WINDOW, kvw), lambda i: (jnp.maximum(qi(i) * wb - 1, 0), 0))
    bias = pl.BlockSpec((1, tq + SWA_WINDOW, SWA_GROUP * tq), lambda i: (jnp.minimum(i, 1), 0, 0))
    rows = pl.BlockSpec((A_KV_HEADS, 1, 1, SWA_GROUP * tq), lambda i: (0, qi(i), 0, 0))
    sink = pl.BlockSpec((A_KV_HEADS, 1, SWA_GROUP * tq), lambda i: (0, 0, 0))
    return q, cur, prev, bias, rows, sink


def _stack_heads(ref, kvh):
    base = kvh * SWA_GROUP
    return jnp.concatenate([ref[:, (base + g) * LANES:(base + g + 1) * LANES] for g in range(SWA_GROUP)], axis=0)


def _unstack_heads(ref, kvh, val, tq):
    base = kvh * SWA_GROUP
    for g in range(SWA_GROUP):
        ref[:, (base + g) * LANES:(base + g + 1) * LANES] = val[g * tq:(g + 1) * tq].astype(ref.dtype)


def _kv_window(prev_ref, cur_ref, kvh):
    sl = slice(kvh * LANES, (kvh + 1) * LANES)
    return jnp.concatenate([prev_ref[:, sl], cur_ref[:, sl]], axis=0)


def _swa_fwd(q, k, v, bias, sink_rows):
    t_rows = q.shape[0]
    tq = min(SWA_TILE, t_rows)
    nq = t_rows // tq
    qs_, cur, prev, bs, rows, sk = _swa_specs(tq, nq)
    kvhs = range(A_KV_HEADS)

    def body(q_ref, kc_ref, kp_ref, vc_ref, vp_ref, b_ref, sink_ref, o_ref, lse_ref):
        scores = [_dot_nt(_kv_window(kp_ref, kc_ref, h), _stack_heads(q_ref, h)) + b_ref[0] for h in kvhs]
        stats = []
        for h, s in zip(kvhs, scores):
            sink = sink_ref[h]
            m = jnp.maximum(jnp.max(s, axis=0, keepdims=True), sink)
            p = jnp.exp(s - m)
            l = jnp.sum(p, axis=0, keepdims=True) + jnp.exp(sink - m)
            lse_ref[h, 0] = m + jnp.log(l)
            stats.append((p.astype(BF16), l))
        for h, (p, l) in zip(kvhs, stats):
            _unstack_heads(o_ref, h, (_dot_tn(_kv_window(vp_ref, vc_ref, h), p) / l).T, tq)

    return pl.pallas_call(
        body, name="swa_fwd", grid=(nq,),
        in_specs=[qs_, cur, prev, cur, prev, bs, sk],
        out_specs=[qs_, rows],
        out_shape=[_sds((t_rows, HEADS * LANES), BF16), _sds((A_KV_HEADS, nq, 1, SWA_GROUP * tq), F32)],
        compiler_params=pltpu.CompilerParams(dimension_semantics=("arbitrary",), vmem_limit_bytes=VMEM_LIMIT),
    )(q, k, k, v, v, bias, sink_rows)


def _swa_bwd(q, k, v, o, do, lse, bias, sink_rows):
    t_rows = q.shape[0]
    tq = min(SWA_TILE, t_rows)
    nq = t_rows // tq
    qs_, cur, prev, bs, rows, sk = _swa_specs(tq, nq)
    hw = SWA_WINDOW
    kvhs = range(A_KV_HEADS)
    kvw = A_KV_HEADS * LANES

    def body(q_ref, kc_ref, kp_ref, vc_ref, vp_ref, o_ref, do_ref, lse_ref, b_ref, sink_ref,
             dq_ref, dk_ref, dv_ref, dsink_ref, ck, cv, dsa):
        i = pl.program_id(0)

        @pl.when(i == 0)
        def _():
            dsa[...] = jnp.zeros(dsa.shape, F32)

        @pl.when(i < nq)
        def _():
            qs = [_stack_heads(q_ref, h) for h in kvhs]
            dos = [_stack_heads(do_ref, h) for h in kvhs]
            kks = [_kv_window(kp_ref, kc_ref, h) for h in kvhs]
            scores = [_dot_nt(kks[h], qs[h]) for h in kvhs]
            dps = [_dot_nt(_kv_window(vp_ref, vc_ref, h), dos[h]) for h in kvhs]
            ps, dss = [], []
            for h in kvhs:
                lse = lse_ref[h, 0]
                p = jnp.exp(scores[h] + b_ref[0] - lse)
                delta = jnp.sum((_stack_heads(o_ref, h).astype(F32) * dos[h].astype(F32)).T, axis=0, keepdims=True)
                dsa[h] += -jnp.exp(sink_ref[h] - lse) * delta
                ps.append(p.astype(BF16))
                dss.append((p * (dps[h] - delta)).astype(BF16))
            for h in kvhs:
                sl = slice(h * LANES, (h + 1) * LANES)
                dv = _dot(ps[h], dos[h])
                dk = _dot(dss[h], qs[h])
                _unstack_heads(dq_ref, h, _dot_tn(dss[h], kks[h]), tq)

                @pl.when(i > 0)
                def _():
                    dk_ref[0:tq - hw, sl] = ck[0:tq - hw, sl]
                    dk_ref[tq - hw:tq, sl] = ck[tq - hw:tq, sl] + dk[0:hw]
                    dv_ref[0:tq - hw, sl] = cv[0:tq - hw, sl]
                    dv_ref[tq - hw:tq, sl] = cv[tq - hw:tq, sl] + dv[0:hw]

                ck[:, sl] = dk[hw:hw + tq]
                cv[:, sl] = dv[hw:hw + tq]

        @pl.when(i == nq)
        def _():
            dk_ref[...] = ck[...]
            dv_ref[...] = cv[...]
            dsink_ref[...] = jnp.zeros(dsink_ref.shape, F32)
            for h in kvhs:
                for g in range(SWA_GROUP):
                    tot = jnp.sum(dsa[h, :, g * tq:(g + 1) * tq], axis=1, keepdims=True)
                    dsink_ref[h, g:g + 1, :] = jnp.zeros((1, LANES), F32) + tot

    kv_out = pl.BlockSpec((tq, kvw), lambda i: (jnp.maximum(i - 1, 0), 0))
    return pl.pallas_call(
        body, name="swa_bwd", grid=(nq + 1,),
        in_specs=[qs_, cur, prev, cur, prev, qs_, qs_, rows, bs, sk],
        out_specs=[qs_, kv_out, kv_out, pl.BlockSpec((A_KV_HEADS, 8, LANES), lambda i: (0, 0, 0))],
        out_shape=[_sds((t_rows, HEADS * LANES), F32), _sds((t_rows, kvw), F32), _sds((t_rows, kvw), F32),
                   _sds((A_KV_HEADS, 8, LANES), F32)],
        scratch_shapes=[pltpu.VMEM((tq, kvw), F32), pltpu.VMEM((tq, kvw), F32),
                        pltpu.VMEM((A_KV_HEADS, 1, SWA_GROUP * tq), F32)],
        compiler_params=pltpu.CompilerParams(dimension_semantics=("arbitrary",), vmem_limit_bytes=VMEM_LIMIT),
    )(q, k, k, v, v, o, do, lse, bias, sink_rows)


def _fwd_mix(x, ya, yb, gate, wba, wbb, wout, g2, g3, tm):
    t_rows = x.shape[0]

    def body(x_ref, ya_ref, yb_ref, gate_ref, wba_ref, wbb_ref, wout_ref, g2_ref, g3_ref,
             pa_ref, pb_ref, mixed_ref, o_ref, x1_ref, h2_ref):
        pa = _dot(ya_ref[...], wba_ref[...])
        pb = _dot(yb_ref[...], wbb_ref[...])
        pa_ref[...] = pa
        pb_ref[...] = pb
        mixed = (gate_ref[:, 0:D_MODEL] * pa + gate_ref[:, D_MODEL:2 * D_MODEL] * pb).astype(BF16)
        mixed_ref[...] = mixed
        o = _dot(mixed, wout_ref[...])
        o_ref[...] = o
        on, _ = _rms_stats(o)
        x1 = x_ref[...] + on * g2_ref[...]
        x1_ref[...] = x1
        x1n, _ = _rms_stats(x1)
        h2_ref[...] = (x1n * g3_ref[...]).astype(BF16)

    def o_(dt):
        return (_sds((t_rows, D_MODEL), dt), _row(tm, D_MODEL))

    ins = [(x, _row(tm, D_MODEL)), (ya, _row(tm, 1024)), (yb, _row(tm, 1024)), (gate, _row(tm, 2048)),
           (wba, _resident(wba.shape)), (wbb, _resident(wbb.shape)), (wout, _resident(wout.shape)),
           (g2, _full(g2.shape)), (g3, _full(g3.shape))]
    return _rows_call("fwd_mix", body, t_rows, tm, ins, [o_(F32), o_(F32), o_(BF16), o_(F32), o_(F32), o_(BF16)])


CONV_CHUNK = 1408


def _fwd_up(h2, wup, convw8, convb, tm):
    t_rows = h2.shape[0]
    cdim = 2 * D_FF

    def body(h2_ref, wup_ref, cw_ref, cb_ref, up_ref, a_ref, carry):
        i = pl.program_id(0)

        @pl.when(i == 0)
        def _():
            carry[...] = jnp.zeros(carry.shape, F32)

        hb = h2_ref[...]
        ups = [_dot(hb, wup_ref[s]) for s in range(cdim // CONV_CHUNK)]

        def conv(c0):
            sl = slice(c0, c0 + CONV_CHUNK)
            up = ups[c0 // CONV_CHUNK]
            up_ref[:, sl] = up
            xm1, xm2 = _conv_taps(up, carry[6:7, sl], carry[7:8, sl])
            u = cw_ref[0:1, sl] * xm2 + cw_ref[1:2, sl] * xm1 + cw_ref[2:3, sl] * up + cb_ref[:, sl]
            carry[:, sl] = up[tm - 8:tm, :]
            return u

        for c0 in range(0, D_FF, CONV_CHUNK):
            ug = conv(c0)
            uv = conv(D_FF + c0)
            gel, _ = _gelu_and_grad(ug)
            a_ref[:, c0:c0 + CONV_CHUNK] = (gel * uv).astype(BF16)

    ins = [(h2, _row(tm, D_MODEL)), (wup, _resident(wup.shape)), (convw8, _full(convw8.shape)), (convb, _full(convb.shape))]
    outs = [(_sds((t_rows, cdim), F32), _row(tm, cdim)), (_sds((t_rows, D_FF), BF16), _row(tm, D_FF))]
    return _rows_call("fwd_up", body, t_rows, tm, ins, outs, scratch=[pltpu.VMEM((8, cdim), F32)])


def _fwd_out(a, wdown, x1, g4, p, wple, g5, wpg, tgt, tm):
    t_rows = a.shape[0]

    def body(a_ref, wdown_ref, x1_ref, g4_ref, p_ref, wple_ref, g5_ref, wpg_ref, tgt_ref,
             ff_ref, x2_ref, e_ref, n5_ref, sg_ref, dx3_ref, loss_ref):
        i = pl.program_id(0)
        ff = _dot(a_ref[...], wdown_ref[...])
        e = _dot(p_ref[...].astype(BF16), wple_ref[...])
        ff_ref[...] = ff
        ffn, _ = _rms_stats(ff)
        x2 = x1_ref[...] + ffn * g4_ref[...]
        x2_ref[...] = x2
        e_ref[...] = e
        x2n, _ = _rms_stats(x2)
        n5 = (x2n * g5_ref[...]).astype(BF16)
        n5_ref[...] = n5
        sg = _sigmoid(_dot(n5, wpg_ref[...]))
        sg_ref[...] = sg
        d = x2 + sg * e - tgt_ref[...]
        dx3_ref[...] = d * (1.0 / D_MODEL)

        @pl.when(i == 0)
        def _():
            loss_ref[...] = jnp.zeros((1, 1), F32)

        loss_ref[...] += 0.5 * jnp.sum(jnp.sum(d * d, axis=1, keepdims=True), axis=0, keepdims=True) * (1.0 / D_MODEL)

    def o_(dt):
        return (_sds((t_rows, D_MODEL), dt), _row(tm, D_MODEL))

    ins = [(a, _row(tm, D_FF)), (wdown, _resident(wdown.shape)), (x1, _row(tm, D_MODEL)), (g4, _full(g4.shape)),
           (p, _row(tm, PLE_DIM)), (wple, _full(wple.shape)), (g5, _full(g5.shape)), (wpg, _resident(wpg.shape)),
           (tgt, _row(tm, D_MODEL))]
    outs = [o_(F32), o_(F32), o_(F32), o_(BF16), o_(F32), o_(F32), (_sds((1, 1), F32), _full((1, 1)))]
    return _rows_call("fwd_out", body, t_rows, tm, ins, outs)


def _bwd_out(dx3, e, sg, x2, ff, g5, g4, wpg, wdown, up, convw8, convb, tm):
    t_rows = dx3.shape[0]
    cdim = 2 * D_FF
    hb = tm // 8

    def body(dx3_ref, e_ref, sg_ref, x2_ref, ff_ref, g5_ref, g4_ref, wpg_ref, wdown_ref, up_ref, halo_ref, cw_ref,
             cb_ref, dpre_ref, de_ref, dx2_ref, dff_ref, du_ref, dg5_ref, dg4_ref, dcb_ref, dcw_ref):
        i = pl.program_id(0)

        @pl.when(i == 0)
        def _():
            dg5_ref[...] = jnp.zeros(dg5_ref.shape, F32)
            dg4_ref[...] = jnp.zeros(dg4_ref.shape, F32)
            dcb_ref[...] = jnp.zeros(dcb_ref.shape, F32)
            dcw_ref[...] = jnp.zeros(dcw_ref.shape, F32)

        dx3 = dx3_ref[...]
        sg = sg_ref[...]
        dpre = (dx3 * e_ref[...] * sg * (1.0 - sg)).astype(BF16)
        dpre_ref[...] = dpre
        de_ref[...] = (dx3 * sg).astype(BF16)
        dn5 = _dot_nt(dpre, wpg_ref[...])
        x2n, r5 = _rms_stats(x2_ref[...])
        d2, dg5 = _rms_bwd(dn5, x2n, r5, g5_ref[...])
        dx2 = dx3 + d2
        dx2_ref[...] = dx2
        dg5_ref[...] += dg5
        ffn, r4 = _rms_stats(ff_ref[...])
        dff, dg4 = _rms_bwd(dx2, ffn, r4, g4_ref[...])
        dg4_ref[...] += dg4
        dffb = dff.astype(BF16)
        dff_ref[...] = dffb
        keep = jnp.where(i > 0, 1.0, 0.0)

        def conv(c0):
            sl = slice(c0, c0 + CONV_CHUNK)
            up = up_ref[:, sl]
            xm1, xm2 = _conv_taps(up, halo_ref[6:7, sl] * keep, halo_ref[7:8, sl] * keep)
            u = cw_ref[0:1, sl] * xm2 + cw_ref[1:2, sl] * xm1 + cw_ref[2:3, sl] * up + cb_ref[:, sl]
            return u, up, xm1, xm2

        def grads(c0, du, up, xm1, xm2):
            sl = slice(c0, c0 + CONV_CHUNK)
            du_ref[:, sl] = du.astype(BF16)
            dcb_ref[:, sl] += jnp.sum(du, axis=0, keepdims=True)
            dcw_ref[0:1, sl] += jnp.sum(du * xm2, axis=0, keepdims=True)
            dcw_ref[1:2, sl] += jnp.sum(du * xm1, axis=0, keepdims=True)
            dcw_ref[2:3, sl] += jnp.sum(du * up, axis=0, keepdims=True)

        for c0 in range(0, D_FF, CONV_CHUNK):
            da = _dot_nt(dffb, wdown_ref[c0:c0 + CONV_CHUNK, :])
            ug, *rg = conv(c0)
            uv, *rv = conv(D_FF + c0)
            gel, dgel = _gelu_and_grad(ug)
            grads(c0, da * uv * dgel, *rg)
            grads(D_FF + c0, da * gel, *rv)

    def o_(n, dt):
        return (_sds((t_rows, n), dt), _row(tm, n))

    def acc(r, n):
        return (_sds((r, n), F32), _full((r, n)))

    halo = pl.BlockSpec((8, cdim), lambda i: (jnp.maximum(i * hb - 1, 0), 0))
    ins = [(dx3, _row(tm, D_MODEL)), (e, _row(tm, D_MODEL)), (sg, _row(tm, D_MODEL)), (x2, _row(tm, D_MODEL)),
           (ff, _row(tm, D_MODEL)), (g5, _full(g5.shape)), (g4, _full(g4.shape)), (wpg, _resident(wpg.shape)),
           (wdown, _resident(wdown.shape)), (up, _row(tm, cdim)), (up, halo), (convw8, _full(convw8.shape)),
           (convb, _full(convb.shape))]
    outs = [o_(D_MODEL, BF16), o_(D_MODEL, BF16), o_(D_MODEL, F32), o_(D_MODEL, BF16), o_(cdim, BF16),
            acc(1, D_MODEL), acc(1, D_MODEL), acc(1, cdim), acc(8, cdim)]
    return _rows_call("bwd_out", body, t_rows, tm, ins, outs)


def _bwd_mid(du, convw8, wup, dx2, x1, g3, o, g2, wout, gate, pa, pb, wba, wbb, yb, tm):
    t_rows = du.shape[0]
    cdim = 2 * D_FF
    halo_rows = 16
    hb = tm // halo_rows
    last_blk = t_rows // halo_rows - 1
    n_tiles = t_rows // tm

    def body(du_ref, halo_ref, cw_ref, wup_ref, dx2_ref, x1_ref, g3_ref, o_ref, g2_ref, wout_ref, gate_ref, pa_ref,
             pb_ref, wba_ref, wbb_ref, yb_ref,
             dup_ref, dx1_ref, do_ref, dpa_ref, dpb_ref, dgt_ref, dya_ref, dyb_ref, dl_ref, dg3_ref, dg2_ref, dbg_ref):
        i = pl.program_id(0)

        @pl.when(i == 0)
        def _():
            dg3_ref[...] = jnp.zeros(dg3_ref.shape, F32)
            dg2_ref[...] = jnp.zeros(dg2_ref.shape, F32)
            dbg_ref[...] = jnp.zeros(dbg_ref.shape, F32)

        keep = jnp.where(i < n_tiles - 1, 1.0, 0.0)
        dh2 = jnp.zeros((tm, D_MODEL), F32)
        dups = []
        for c0 in range(0, cdim, CONV_CHUNK):
            sl = slice(c0, c0 + CONV_CHUNK)
            du = du_ref[:, sl].astype(F32)
            nxt = halo_ref[:, sl].astype(F32)
            xp1, xp2 = _conv_taps_next(du, nxt[0:1] * keep, nxt[1:2] * keep)
            dups.append((cw_ref[2:3, sl] * du + cw_ref[1:2, sl] * xp1 + cw_ref[0:1, sl] * xp2).astype(BF16))
            dup_ref[:, sl] = dups[-1]
            if len(dups) > 1:
                dh2 = dh2 + _dot_nt(dups[-2], wup_ref[len(dups) - 2])
        dh2 = dh2 + _dot_nt(dups[-1], wup_ref[len(dups) - 1])
        x1n, r3 = _rms_stats(x1_ref[...])
        d1, dg3 = _rms_bwd(dh2, x1n, r3, g3_ref[...])
        dx1 = dx2_ref[...] + d1
        dx1_ref[...] = dx1
        dg3_ref[...] += dg3
        on, r2 = _rms_stats(o_ref[...])
        do, dg2 = _rms_bwd(dx1, on, r2, g2_ref[...])
        dg2_ref[...] += dg2
        dob = do.astype(BF16)
        do_ref[...] = dob
        dmixed = _dot_nt(dob, wout_ref[...])
        ga = gate_ref[:, 0:D_MODEL]
        gb = gate_ref[:, D_MODEL:2 * D_MODEL]
        dpa = (dmixed * ga).astype(BF16)
        dpb = (dmixed * gb).astype(BF16)
        dpa_ref[...] = dpa
        dpb_ref[...] = dpb
        dga = dmixed * pa_ref[...] * ga * (1.0 - ga)
        dgb = dmixed * pb_ref[...] * gb * (1.0 - gb)
        dgt_ref[:, 0:D_MODEL] = dga.astype(BF16)
        dgt_ref[:, D_MODEL:2 * D_MODEL] = dgb.astype(BF16)
        dbg_ref[:, 0:D_MODEL] += jnp.sum(dga, axis=0, keepdims=True)
        dbg_ref[:, D_MODEL:2 * D_MODEL] += jnp.sum(dgb, axis=0, keepdims=True)
        dya_ref[...] = _dot_nt(dpa, wba_ref[...]).astype(BF16)
        dyb = _dot_nt(dpb, wbb_ref[...]).astype(BF16)
        dyb_ref[...] = dyb
        prod = yb_ref[...].astype(F32) * dyb.astype(F32)
        lane_head = lax.broadcasted_iota(jnp.int32, (HEADS, HEADS * LANES), 1) // LANES
        sel = (lane_head == lax.broadcasted_iota(jnp.int32, (HEADS, HEADS * LANES), 0)).astype(BF16)
        hi = prod.astype(BF16)
        lo = (prod - hi.astype(F32)).astype(BF16)
        dl_ref[...] = _dot_nt(sel, hi) + _dot_nt(sel, lo)

    def o_(n, dt):
        return (_sds((t_rows, n), dt), _row(tm, n))

    def acc(r, n):
        return (_sds((r, n), F32), _full((r, n)))

    halo = pl.BlockSpec((halo_rows, cdim), lambda i: (jnp.minimum((i + 1) * hb, last_blk), 0))
    ins = [(du, _row(tm, cdim)), (du, halo), (convw8, _full(convw8.shape)), (wup, _resident(wup.shape)),
           (dx2, _row(tm, D_MODEL)), (x1, _row(tm, D_MODEL)), (g3, _full(g3.shape)), (o, _row(tm, D_MODEL)),
           (g2, _full(g2.shape)), (wout, _resident(wout.shape)), (gate, _row(tm, 2048)), (pa, _row(tm, D_MODEL)),
           (pb, _row(tm, D_MODEL)), (wba, _resident(wba.shape)), (wbb, _resident(wbb.shape)), (yb, _row(tm, 1024))]
    outs = [o_(cdim, BF16), o_(D_MODEL, F32), o_(D_MODEL, BF16), o_(D_MODEL, BF16), o_(D_MODEL, BF16),
            o_(2048, BF16), o_(1024, BF16), o_(1024, BF16),
            (_sds((HEADS, t_rows), F32), pl.BlockSpec((HEADS, tm), lambda i: (0, i))),
            acc(1, D_MODEL), acc(1, D_MODEL), acc(1, 2048)]
    return _rows_call("bwd_mid", body, t_rows, tm, ins, outs)


def _bwd_in(dqs, dks, dvs, dqm, dkm, dvm, tabs, consts, cq, ckv, gq, gkv, wuq, wk, wv, dgates, win, x, g1, dx1, tm):
    t_rows = x.shape[0]

    def body(dqs_ref, dks_ref, dvs_ref, dqm_ref, dkm_ref, dvm_ref, ca, sa1, sa2, cb, sb1, sb2, c_ref, cq_ref,
             ckv_ref, gq_ref, gkv_ref, wuq_ref, wk_ref, wv_ref, dgt_ref, win_ref, x_ref, g1_ref, dx1_ref,
             dz_ref, dqb_ref, dx_ref, dgq_ref, dgkv_ref, dg1_ref):
        i = pl.program_id(0)

        @pl.when(i == 0)
        def _():
            dgq_ref[...] = jnp.zeros(dgq_ref.shape, F32)
            dgkv_ref[...] = jnp.zeros(dgkv_ref.shape, F32)
            dg1_ref[...] = jnp.zeros(dg1_ref.shape, F32)

        ta = (ca[...], sa1[...], sa2[...])
        tb = (cb[...], sb1[...], sb2[...])

        def piece(lo, hi, val):
            dz_ref[:, lo:hi] = val
            return _dot_nt(val, win_ref[:, lo:hi])

        dh1 = piece(Z_GATE, ZW, dgt_ref[...])
        dkm = dkm_ref[...]
        dckvn = _dot_nt(dkm.astype(BF16), wk_ref[...]) + _dot_nt(dvm_ref[...].astype(BF16), wv_ref[...])
        def compact(d):
            tiles = []
            for j in range(d.shape[1] // (2 * LANES)):
                even = d[:, 2 * j * LANES:(2 * j + 1) * LANES]
                odd = d[:, (2 * j + 1) * LANES:(2 * j + 2) * LANES]
                tiles.append(even + pltpu.roll(odd, A_HEAD_DIM, 1))
            return tiles[0] if len(tiles) == 1 else jnp.concatenate(tiles, axis=1)

        dh1 = dh1 + piece(Z_VA, Z_CQ, compact(dvs_ref[...]).astype(BF16))
        dqm = jnp.concatenate([dqm_ref[h] for h in range(HEADS)], axis=1)
        dqb = _rope_t(dqm * SCALE_B, *tb, ROPE_DIM // 2).astype(BF16)
        dqb_ref[...] = dqb
        dcqn = _dot_nt(dqb, wuq_ref[...])
        dqa = _rope_t(compact(dqs_ref[...]) * SCALE_A, *ta, A_HEAD_DIM // 2)
        dh1 = dh1 + piece(Z_QA, Z_KA, dqa.astype(BF16))
        dh1 = dh1 + piece(Z_KA, Z_VA, _rope_t(compact(dks_ref[...]), *ta, A_HEAD_DIM // 2).astype(BF16))
        ckvn, rkv = _rms_stats(ckv_ref[...])
        dckv, dgkv = _rms_bwd(dckvn, ckvn, rkv, gkv_ref[...])
        dgkv_ref[...] += dgkv
        dh1 = dh1 + piece(Z_CKV, Z_KR, dckv.astype(BF16))
        dslot = dkm[:, 0:LANES]
        for h in range(1, HEADS):
            dslot = dslot + dkm[:, h * LANES:(h + 1) * LANES]
        dh1 = dh1 + piece(Z_KR, Z_GATE, _rope_t(dslot * c_ref[10:11, :], *tb, ROPE_DIM // 2).astype(BF16))
        cqn, rq = _rms_stats(cq_ref[...])
        dcq, dgq = _rms_bwd(dcqn, cqn, rq, gq_ref[...])
        dgq_ref[...] += dgq
        dh1 = dh1 + piece(Z_CQ, Z_CKV, dcq.astype(BF16))
        xn, r1 = _rms_stats(x_ref[...])
        d0, dg1 = _rms_bwd(dh1, xn, r1, g1_ref[...])
        dg1_ref[...] += dg1
        dx_ref[...] = dx1_ref[...] + d0

    def acc(n):
        return (_sds((1, n), F32), _full((1, n)))

    ins = [(dqs, _row(tm, 1024)), (dks, _row(tm, 256)), (dvs, _row(tm, 256)), (dqm, _heads(tm, HEADS)),
           (dkm, _row(tm, 1024)), (dvm, _row(tm, 1024))] + [(t, _row(tm, LANES)) for t in tabs] + [
           (consts, _full(consts.shape)), (cq, _row(tm, 256)), (ckv, _row(tm, 128)), (gq, _full(gq.shape)),
           (gkv, _full(gkv.shape)), (wuq, _full(wuq.shape)), (wk, _full(wk.shape)), (wv, _full(wv.shape)),
           (dgates, _row(tm, 2048)), (win, _resident(win.shape)), (x, _row(tm, D_MODEL)), (g1, _full(g1.shape)),
           (dx1, _row(tm, D_MODEL))]
    outs = [(_sds((t_rows, ZW), BF16), _row(tm, ZW)), (_sds((t_rows, 1024), BF16), _row(tm, 1024)),
            (_sds((t_rows, D_MODEL), F32), _row(tm, D_MODEL)), acc(256), acc(128), acc(D_MODEL)]
    return _rows_call("bwd_in", body, t_rows, tm, ins, outs)


def _pick_cols(n):
    best = LANES
    for d in range(LANES, min(n, 1664) + 1, LANES):
        if n % d == 0:
            best = d
    return best


def _mm_tn(name, a, b, column_shards=1, after=None):
    t_rows, m = a.shape
    n = b.shape[1]
    bk = min(1024, t_rows)
    bm, bn = _pick_cols(m), _pick_cols(n // column_shards)
    per_shard = n // column_shards // bn
    extra = () if after is None else (after,)

    def body(a_ref, b_ref, *rest):
        o_ref = rest[-1]

        @pl.when(pl.program_id(2) == 0)
        def _():
            o_ref[...] = jnp.zeros((bm, bn), F32)

        o_ref[...] += _dot_tn(a_ref[...].astype(BF16), b_ref[...].astype(BF16))

    return pl.pallas_call(
        body, name=name, grid=(m // bm, n // bn, t_rows // bk),
        in_specs=[pl.BlockSpec((bk, bm), lambda i, j, k: (k, i)), pl.BlockSpec((bk, bn), lambda i, j, k: (k, j))]
        + [pl.BlockSpec((8, LANES), lambda i, j, k: (0, 0))] * len(extra),
        out_specs=(pl.BlockSpec((bm, bn), lambda i, j, k: (i, j)) if column_shards == 1 else
                   pl.BlockSpec((None, bm, bn), lambda i, j, k: (j // per_shard, i, j % per_shard))),
        out_shape=_sds((m, n) if column_shards == 1 else (column_shards, m, n // column_shards), F32),
        compiler_params=pltpu.CompilerParams(dimension_semantics=("arbitrary",) * 3, vmem_limit_bytes=VMEM_LIMIT),
    )(a, b, *extra)


PACK_ROWS = 512


ADD_TILE_ELEMS = 1 << 17


def _add_rows(rows, cols):
    best = 16
    for d in range(16, rows + 1, 16):
        if rows % d == 0 and d * cols <= ADD_TILE_ELEMS:
            best = d
    assert rows % best == 0
    return best


def _add_pair(name, g, recv, half):
    _, _, rows, cols = g.shape
    t = _add_rows(rows, cols)

    def body(h_ref, g_ref, r_ref, o_ref):
        o_ref[...] = (g_ref[:, 0] + r_ref[...]).astype(BF16)

    spec = pl.BlockSpec((4, t, cols), lambda i, h: (0, i, 0))
    grid_spec = pltpu.PrefetchScalarGridSpec(
        num_scalar_prefetch=1, grid=(rows // t,),
        in_specs=[pl.BlockSpec((4, 1, t, cols), lambda i, h: (0, h[0], i, 0)), spec], out_specs=spec)
    return pl.pallas_call(body, name=name, grid_spec=grid_spec,
                          out_shape=_sds(recv.shape, BF16))(jnp.reshape(half, (1,)).astype(jnp.int32), g, recv)


def _add_chips(name, parts):
    _, rows, cols = parts.shape
    t = _add_rows(rows, cols)

    def body(p_ref, o_ref):
        acc = p_ref[0].astype(F32)
        for j in range(1, 4):
            acc = acc + p_ref[j].astype(F32)
        o_ref[...] = acc

    return pl.pallas_call(body, name=name, grid=(rows // t,),
                          in_specs=[pl.BlockSpec((4, t, cols), lambda i: (0, i, 0))],
                          out_specs=pl.BlockSpec((t, cols), lambda i: (i, 0)),
                          out_shape=_sds((rows, cols), F32))(parts)


def _add_devices(parts):
    n, rows, _ = parts.shape

    def body(p_ref, o_ref):
        acc = p_ref[0]
        for j in range(1, n):
            acc = acc + p_ref[j]
        o_ref[...] = acc

    return pl.pallas_call(body, name="small_add", grid=(1,),
                          in_specs=[pl.BlockSpec((n, rows, LANES), lambda i: (0, 0, 0))],
                          out_specs=pl.BlockSpec((rows, LANES), lambda i: (0, 0)),
                          out_shape=_sds((rows, LANES), F32))(parts)


def _adam_rows(k, n):
    target = max(8, (1 << 20) // (4 * n))
    if k <= target:
        return k
    best = None
    for d in range(8, target + 1, 8):
        if k % d == 0:
            best = d
    return best if best is not None else k


def _adam_update(w, g, m, v):
    m_ = ADAM_B1 * m + (1.0 - ADAM_B1) * g
    v_ = ADAM_B2 * v + (1.0 - ADAM_B2) * (g * g)
    delta = -ADAM_LR * ((m_ / (1.0 - ADAM_B1 ** ADAM_STEP)) / (jnp.sqrt(v_ / (1.0 - ADAM_B2 ** ADAM_STEP)) + ADAM_EPS)
                        + ADAM_WD * w)
    return delta, m_, v_


def _adamw(name, w, g, m, v):
    k, n = w.shape
    bk = _adam_rows(k, n)

    def body(w_ref, g_ref, m_ref, v_ref, d_ref, mo_ref, vo_ref):
        d_ref[...], mo_ref[...], vo_ref[...] = _adam_update(w_ref[...], g_ref[...], m_ref[...], v_ref[...])

    spec = pl.BlockSpec((bk, n), lambda i: (i, 0))
    out = pl.pallas_call(body, name=name, grid=(k // bk,), in_specs=[spec] * 4, out_specs=[spec] * 3,
                         out_shape=[_sds((k, n), F32)] * 3,
                         compiler_params=pltpu.CompilerParams(vmem_limit_bytes=VMEM_LIMIT))(w, g, m, v)
    return (g, *out)


def _adamw_halves(name, w, mine, theirs, m, v, half):
    k, n = w.shape
    bk = _adam_rows(k // 2, n)
    nb = k // 2 // bk

    def body(h_ref, w_ref, mine_ref, theirs_ref, m_ref, v_ref, g_ref, d_ref, mo_ref, vo_ref):
        g = jnp.where(pl.program_id(0) == h_ref[0], mine_ref[...], theirs_ref[...])
        g_ref[...] = g
        d_ref[...], mo_ref[...], vo_ref[...] = _adam_update(w_ref[...], g, m_ref[...], v_ref[...])

    full = pl.BlockSpec((bk, n), lambda h, i, c: (h * nb + i, 0))
    part = pl.BlockSpec((bk, n), lambda h, i, c: (i, 0))
    grid_spec = pltpu.PrefetchScalarGridSpec(num_scalar_prefetch=1, grid=(2, nb),
                                             in_specs=[full, part, part, full, full], out_specs=[full] * 4)
    return tuple(pl.pallas_call(
        body, name=name, grid_spec=grid_spec, out_shape=[_sds((k, n), F32)] * 4,
        compiler_params=pltpu.CompilerParams(vmem_limit_bytes=VMEM_LIMIT),
    )(jnp.reshape(half, (1,)).astype(jnp.int32), w, mine, theirs, m, v))


_HBM = pl.BlockSpec(memory_space=pltpu.HBM)


def _me():
    return lax.axis_index("x"), lax.axis_index("y"), lax.axis_index("c")


def _other_chips(x, y):
    return [(1 - x, y), (x, 1 - y), (1 - x, 1 - y)]


def _pass_to_sibling(zones):
    n = len(zones)

    def body(*refs):
        in_refs, out_refs = refs[:n], refs[n:2 * n]
        send_sems, recv_sems = refs[2 * n:]
        x, y, c = _me()
        sent = []
        for a, (in_ref, out_ref) in enumerate(zip(in_refs, out_refs)):
            for j, (cx, cy) in enumerate(_other_chips(x, y)):
                mine, theirs = (2 * cx + cy, c), (2 * cx + cy, 1 - c)
                sems = dict(send_sem=send_sems.at[3 * a + j], recv_sem=recv_sems.at[3 * a + j],
                            device_id=(x, y, 1 - c), device_id_type=MESH)
                sent.append((pltpu.make_async_remote_copy(src_ref=in_ref.at[mine], dst_ref=out_ref.at[mine], **sems),
                             pltpu.make_async_remote_copy(src_ref=in_ref.at[theirs], dst_ref=out_ref.at[theirs], **sems)))
        for send, _ in sent:
            send.start()
        for _, recv in sent:
            recv.wait_recv()
        for send, _ in sent:
            send.wait_send()

    return pl.pallas_call(
        body, name="pass_to_sibling", out_shape=[_sds(z.shape, z.dtype) for z in zones],
        in_specs=[_HBM] * n, out_specs=[_HBM] * n, input_output_aliases={i: i for i in range(n)},
        scratch_shapes=[pltpu.SemaphoreType.DMA((3 * n,)), pltpu.SemaphoreType.DMA((3 * n,))],
    )(*zones)


def _swap_sibling(name, vs, other_half=False):
    n = len(vs)

    def body(*refs):
        v_refs, out_refs = refs[:n], refs[n:2 * n]
        send_sems, recv_sems = refs[2 * n:]
        x, y, c = _me()
        cps = [pltpu.make_async_remote_copy(src_ref=v_ref.at[:, 1 - c] if other_half else v_ref, dst_ref=out_ref,
                                            send_sem=send_sems.at[a], recv_sem=recv_sems.at[a],
                                            device_id=(x, y, 1 - c), device_id_type=MESH)
               for a, (v_ref, out_ref) in enumerate(zip(v_refs, out_refs))]
        for cp in cps:
            cp.start()
        for cp in cps:
            cp.wait()

    def landing(v):
        return _sds((v.shape[0],) + v.shape[2:] if other_half else v.shape, v.dtype)

    return pl.pallas_call(
        body, name=name, out_shape=[landing(v) for v in vs], in_specs=[_HBM] * n, out_specs=[_HBM] * n,
        scratch_shapes=[pltpu.SemaphoreType.DMA((n,)), pltpu.SemaphoreType.DMA((n,))],
    )(*vs)


_SEM = pl.BlockSpec(memory_space=pltpu.SEMAPHORE)
_EFFECT = pltpu.SideEffectType.DATAFLOW_SIDE_EFFECTING
WHOLE = "whole"
PIECE = "piece"
SIBLING_HALF = "sibling"
MY_HALF = "half"
EVERYONE = "everyone"
_COPIES = {WHOLE: 3, PIECE: 3, MY_HALF: 3, SIBLING_HALF: 1, EVERYONE: 7}


def _landing_shape(v, mode):
    return {WHOLE: (4,) + v.shape, MY_HALF: (4,) + v.shape, PIECE: v.shape, EVERYONE: (8,) + v.shape,
            SIBLING_HALF: (v.shape[0],) + v.shape[2:]}[mode]


def _chip_copies(v_ref, land_ref, send_sems, recv_sems, mode, sem0=0):
    x, y, c = _me()
    if mode == SIBLING_HALF:
        cp = pltpu.make_async_remote_copy(src_ref=v_ref.at[:, 1 - c], dst_ref=land_ref, send_sem=send_sems.at[sem0],
                                          recv_sem=recv_sems.at[sem0], device_id=(x, y, 1 - c), device_id_type=MESH)
        return [(cp, cp)]
    if mode == EVERYONE:
        out = []
        for f in range(1, 8):
            px, py, pc = (1 - x if f & 4 else x), (1 - y if f & 2 else y), (1 - c if f & 1 else c)
            sems = dict(send_sem=send_sems.at[sem0 + f - 1], recv_sem=recv_sems.at[sem0 + f - 1],
                        device_id=(px, py, pc), device_id_type=MESH)
            out.append((pltpu.make_async_remote_copy(src_ref=v_ref, dst_ref=land_ref.at[4 * x + 2 * y + c], **sems),
                        pltpu.make_async_remote_copy(src_ref=v_ref, dst_ref=land_ref.at[4 * px + 2 * py + pc], **sems)))
        return out
    k = 2 * x + y
    out = []
    for j, (cx, cy) in enumerate(_other_chips(x, y)):
        if mode == MY_HALF:
            src, mine, theirs = v_ref.at[c], land_ref.at[k, c], land_ref.at[2 * cx + cy, c]
        else:
            src = v_ref.at[2 * cx + cy] if mode == PIECE else v_ref
            mine, theirs = land_ref.at[k], land_ref.at[2 * cx + cy]
        sems = dict(send_sem=send_sems.at[sem0 + j], recv_sem=recv_sems.at[sem0 + j], device_id=(cx, cy, c),
                    device_id_type=MESH)
        send = pltpu.make_async_remote_copy(src_ref=src, dst_ref=mine, **sems)
        recv = pltpu.make_async_remote_copy(src_ref=src, dst_ref=theirs, **sems)
        out.append((send, recv))
    return out


def _chips_start(name, vs, mode, after=None):
    n = len(vs)
    lands = [_landing_shape(v, mode) for v in vs]

    def body(*refs):
        v_refs, land_refs = refs[:n], refs[n:2 * n]
        send_sems, recv_sems = refs[-2 * n - 3], refs[-2 * n - 2]
        token = refs[-1]
        for a in range(n):
            for send, _ in _chip_copies(v_refs[a], land_refs[a], send_sems, recv_sems, mode, _COPIES[mode] * a):
                send.start()
        token[...] = jnp.zeros_like(token)

    extra = () if after is None else (after,)
    hbm = [pltpu.with_memory_space_constraint(v, pltpu.HBM) for v in vs]
    zones = [pltpu.with_memory_space_constraint(lax.empty(s, v.dtype), pltpu.HBM) for s, v in zip(lands, vs)]
    out = pl.pallas_call(
        body, name=name,
        out_shape=(pltpu.SemaphoreType.DMA((_COPIES[mode] * n,)), pltpu.SemaphoreType.DMA((_COPIES[mode] * n,)),
                   *[pltpu.HBM(v.shape, v.dtype) for v in vs], *[pltpu.HBM(s, v.dtype) for s, v in zip(lands, vs)],
                   _sds((8, LANES), F32)),
        in_specs=(_HBM,) * (2 * n) + (pl.BlockSpec(memory_space=pl.ANY),) * len(extra),
        out_specs=(_SEM, _SEM) + (_HBM,) * (2 * n) + (pl.BlockSpec(memory_space=pltpu.VMEM),),
        input_output_aliases={i: 2 + i for i in range(2 * n)},
        compiler_params=pltpu.CompilerParams(has_side_effects=_EFFECT),
    )(*hbm, *zones, *extra)
    return out[0], out[1], list(out[2:2 + n]), list(out[2 + n:2 + 2 * n]), out[-1]


def _chips_wait(name, send_sems, recv_sems, v_thru, land_thru, mode, after):
    n = len(v_thru)

    def body(*refs):
        v_refs, land_refs = refs[:n], refs[n:2 * n]
        send_sems, recv_sems = refs[2 * n], refs[2 * n + 1]
        for a in range(n):
            for send, recv in _chip_copies(v_refs[a], land_refs[a], send_sems, recv_sems, mode, _COPIES[mode] * a):
                send.wait_send()
                recv.wait_recv()

    out = pl.pallas_call(
        body, name=name,
        out_shape=tuple(pltpu.HBM(a.shape, a.dtype) for a in list(v_thru) + list(land_thru)),
        in_specs=(_HBM,) * (2 * n) + (_SEM, _SEM, pl.BlockSpec(memory_space=pl.ANY)), out_specs=(_HBM,) * (2 * n),
        input_output_aliases={i: i for i in range(2 * n)},
        compiler_params=pltpu.CompilerParams(has_side_effects=_EFFECT),
    )(*v_thru, *land_thru, send_sems, recv_sems, after)
    return list(out[:n]), list(out[n:])


_BIG = (("w_in", (1024, 3232), 1), ("w_uq", (256, 768), 1), ("w_ukv", (128, 1024), 1), ("w_branch_a", (512, 1024), 1),
        ("w_branch_b", (512, 1024), 1), ("w_out", (1024, 1024), 0), ("w_up", (1024, 5632), 1),
        ("w_down", (2816, 1024), 0), ("w_ple_gate", (1024, 1024), 0), ("w_ple", (256, 1024), 1))


def _shard_shape(shape, axis):
    return (shape[0] // 4, shape[1]) if axis == 0 else (shape[0], shape[1] // 4)


def _half_rows(shape, axis):
    k, n = _shard_shape(shape, axis)
    return k * n // (2 * LANES)


_EARLY = ("w_in", "w_uq", "w_ukv")
_LATE = ("w_branch_a", "w_branch_b", "w_out", "w_up", "w_down", "w_ple_gate", "w_ple")
_NATURAL = ("w_in", "w_up", "w_down", "w_out", "w_ple_gate")
_EARLY_PACKED = tuple(b for b in _BIG if b[0] in _EARLY and b[0] not in _NATURAL)
_LATE_PACKED = tuple(b for b in _BIG if b[0] in _LATE and b[0] not in _NATURAL)
_SHARD = {name: _shard_shape(shape, axis) for name, shape, axis in _BIG}


def _halves(a):
    return a.reshape(a.shape[:-2] + (2, a.shape[-2] // 2, a.shape[-1]))


def _rows_joined(a):
    return a.reshape(a.shape[:-3] + (a.shape[-3] * a.shape[-2], a.shape[-1]))


def _pack_pad(group):
    return -sum(_half_rows(shape, axis) for _, shape, axis in group) % PACK_ROWS


def _pack_shards(shards, dtype, group):
    parts = [shards[name].astype(dtype).reshape(2, _half_rows(shape, axis), LANES) for name, shape, axis in group]
    return jnp.concatenate(parts + [jnp.zeros((2, _pack_pad(group), LANES), dtype)], axis=1)


def _unpack_gathered(g, group):
    out, off = {}, 0
    for name, shape, axis in group:
        r = _half_rows(shape, axis)
        k, n = _shard_shape(shape, axis)
        w = g[:, :, off:off + r, :].reshape(4, k, n)
        out[name] = w.reshape(shape) if axis == 0 else w.transpose(1, 0, 2).reshape(shape)
        off += r
    return out


def _pack_grads(grads, group):
    parts = []
    for name, shape, axis in group:
        k, n = _shard_shape(shape, axis)
        g = grads[name]
        g4 = g.reshape(4, k, n) if axis == 0 else g.reshape(k, 4, n).transpose(1, 0, 2)
        parts.append(g4.reshape(4, 2, _half_rows(shape, axis), LANES))
    return jnp.concatenate(parts + [jnp.zeros((4, 2, _pack_pad(group), LANES), F32)], axis=2)


def _unpack_shard_grads(f, group):
    out, off = {}, 0
    for name, shape, axis in group:
        r = _half_rows(shape, axis)
        out[name] = f[:, off:off + r, :].reshape(_shard_shape(shape, axis))
        off += r
    return out


def _pad_slots(w, heads, dim, axis):
    if axis == 1:
        k = w.shape[0]
        return jnp.pad(w.reshape(k, heads, dim), ((0, 0), (0, 0), (0, LANES - dim))).reshape(k, heads * LANES)
    n = w.shape[1]
    return jnp.pad(w.reshape(heads, dim, n), ((0, 0), (0, LANES - dim), (0, 0))).reshape(heads * LANES, n)


def _unpad_slots(w, heads, dim, axis):
    if axis == 1:
        k = w.shape[0]
        return w.reshape(k, heads, LANES)[:, :, :dim].reshape(k, heads * dim)
    n = w.shape[1]
    return w.reshape(heads, LANES, n)[:, :dim, :].reshape(heads * dim, n)


def _pad_w_in(w):
    kr = jnp.pad(w[:, Z_KR:Z_KR + ROPE_DIM], ((0, 0), (NOPE_DIM, LANES - NOPE_DIM - ROPE_DIM)))
    return jnp.concatenate([w[:, :Z_KR], kr, w[:, Z_KR + ROPE_DIM:]], axis=1)


def _unpad_w_in(w):
    return jnp.concatenate([w[:, :Z_KR], w[:, Z_KR + NOPE_DIM:Z_KR + NOPE_DIM + ROPE_DIM], w[:, Z_GATE:ZW]], axis=1)


def _spread_matrix(heads, dim):
    row = lax.broadcasted_iota(jnp.int32, (heads * dim, heads * LANES), 0)
    col = lax.broadcasted_iota(jnp.int32, (heads * dim, heads * LANES), 1)
    return (col == (row // dim) * LANES + row % dim).astype(BF16)


_SMALL = (("attn_pre_norm", 1024), ("attn_post_norm", 1024), ("b_gate", 2048), ("sinks", 8), ("q_a_norm", 256),
          ("kv_a_norm", 128), ("mlp_pre_norm", 1024), ("mlp_post_norm", 1024), ("conv_b", 5632), ("ple_norm", 1024),
          ("conv_w", 3 * 5632), ("loss", 1))


def _small_rows(n):
    return 8 * -(-n // (8 * LANES))


def _pack_small(vals):
    parts = []
    for name, n in _SMALL:
        r = _small_rows(n)
        parts.append(jnp.pad(vals[name].reshape(-1), (0, r * LANES - n)).reshape(r, LANES))
    return jnp.concatenate(parts, axis=0)


def _unpack_small(buf):
    out, off = {}, 0
    for name, n in _SMALL:
        r = _small_rows(n)
        out[name] = buf[off:off + r].reshape(-1)[:n]
        off += r
    return out


def kernel(x, p, positions, attn_pre_norm, attn_post_norm, w_in, b_gate, sinks, q_a_norm, w_uq, kv_a_norm, w_ukv, w_branch_a, w_branch_b, w_out, mlp_pre_norm, mlp_post_norm, w_up, conv_w, conv_b, w_down, ple_norm, w_ple_gate, w_ple, loss_target, m_attn_pre_norm, m_attn_post_norm, m_w_in, m_b_gate, m_sinks, m_q_a_norm, m_w_uq, m_kv_a_norm, m_w_ukv, m_w_branch_a, m_w_branch_b, m_w_out, m_mlp_pre_norm, m_mlp_post_norm, m_w_up, m_conv_w, m_conv_b, m_w_down, m_ple_norm, m_w_ple_gate, m_w_ple, v_attn_pre_norm, v_attn_post_norm, v_w_in, v_b_gate, v_sinks, v_q_a_norm, v_w_uq, v_kv_a_norm, v_w_ukv, v_w_branch_a, v_w_branch_b, v_w_out, v_mlp_pre_norm, v_mlp_post_norm, v_w_up, v_conv_w, v_conv_b, v_w_down, v_ple_norm, v_w_ple_gate, v_w_ple):
    names = ["attn_pre_norm", "attn_post_norm", "w_in", "b_gate", "sinks", "q_a_norm", "w_uq", "kv_a_norm", "w_ukv",
             "w_branch_a", "w_branch_b", "w_out", "mlp_pre_norm", "mlp_post_norm", "w_up", "conv_w", "conv_b",
             "w_down", "ple_norm", "w_ple_gate", "w_ple"]
    wts = dict(zip(names, [attn_pre_norm, attn_post_norm, w_in, b_gate, sinks, q_a_norm, w_uq, kv_a_norm, w_ukv,
                           w_branch_a, w_branch_b, w_out, mlp_pre_norm, mlp_post_norm, w_up, conv_w, conv_b, w_down,
                           ple_norm, w_ple_gate, w_ple]))
    moms = dict(zip(names, [m_attn_pre_norm, m_attn_post_norm, m_w_in, m_b_gate, m_sinks, m_q_a_norm, m_w_uq,
                            m_kv_a_norm, m_w_ukv, m_w_branch_a, m_w_branch_b, m_w_out, m_mlp_pre_norm,
                            m_mlp_post_norm, m_w_up, m_conv_w, m_conv_b, m_w_down, m_ple_norm, m_w_ple_gate, m_w_ple]))
    vars_ = dict(zip(names, [v_attn_pre_norm, v_attn_post_norm, v_w_in, v_b_gate, v_sinks, v_q_a_norm, v_w_uq,
                             v_kv_a_norm, v_w_ukv, v_w_branch_a, v_w_branch_b, v_w_out, v_mlp_pre_norm,
                             v_mlp_post_norm, v_w_up, v_conv_w, v_conv_b, v_w_down, v_ple_norm, v_w_ple_gate, v_w_ple]))
    w2 = {n: a.reshape(a.shape[-2:]) for n, a in wts.items()}
    m2 = {n: a.reshape(a.shape[-2:]) for n, a in moms.items()}
    v2 = {n: a.reshape(a.shape[-2:]) for n, a in vars_.items()}

    t_rows = x.shape[-2]
    tm = min(256, t_rows)
    tm_wide = min(512, t_rows)
    xc, yc, cc = lax.axis_index("x"), lax.axis_index("y"), lax.axis_index("c")
    chip = 2 * xc + yc

    x2d = x.reshape(t_rows, D_MODEL)
    p2d = p.reshape(t_rows, PLE_DIM)
    tgt = loss_target.reshape(t_rows, D_MODEL)
    pos_f = positions.reshape(t_rows, 1).astype(F32)

    def own_slot_filled(gathered, mine):
        return [lax.dynamic_update_slice(g, m[None], (chip, 0, 0, 0)) for g, m in zip(gathered, mine)]

    def shard_lists(group, packed_group, token=0.0):
        ws = {n: w2[n] + token for n in group}
        return [_halves(ws[n].astype(BF16)) for n in group if n in _NATURAL] + [_pack_shards(ws, BF16, packed_group)]

    cw_rows = 3 * 1408 // LANES
    conv_mine = jnp.pad(w2["conv_w"].reshape(cw_rows, LANES), ((0, 48 - cw_rows), (0, 0))).reshape(2, 24, LANES)
    early_mine = shard_lists(_EARLY, _EARLY_PACKED) + [conv_mine]
    early_sems = _chips_start("gather_early_start", early_mine, MY_HALF)
    early_token = early_sems[4][0:1, 0:1]
    consts = _rope_consts()
    tabs = _rope_tables(pos_f + early_token, consts, tm)
    late_mine = shard_lists(_LATE, _LATE_PACKED, early_token)
    both_done = tabs[0][0:1, 0:1] + sum(m[0, 0:1, 0:1].astype(F32) for m in late_mine)
    early_sent, early_landed = _chips_wait("gather_early_wait", *early_sems[:4], MY_HALF, after=both_done)
    early = own_slot_filled(_pass_to_sibling(early_landed), early_sent)
    late_names = [n for n in _LATE if n in _NATURAL]
    first = [late_names.index("w_out"), len(late_names)]
    late_a = [late_mine[i] for i in first]
    late_b = [m for i, m in enumerate(late_mine) if i not in first]
    late_a_sems = _chips_start("gather_late_a_start", late_a, WHOLE, after=early[0])
    late_b_sems = _chips_start("gather_late_b_start", late_b, WHOLE, after=late_a_sems[4])
    late_token = late_b_sems[4][0:1, 0:1]
    full = _unpack_gathered(early[1], _EARLY_PACKED)
    full["w_in"] = _rows_joined(early[0]).transpose(1, 0, 2).reshape(D_MODEL, 3232)
    conv_full = early[2].reshape(4, 48, LANES)[:, :cw_rows].reshape(4, 3, 1408).transpose(1, 0, 2).reshape(3, 2 * D_FF)
    convw8 = jnp.pad(conv_full, ((0, 5), (0, 0)))

    win = _pad_w_in(full["w_in"])
    wuq = _pad_slots(full["w_uq"], HEADS, NOPE_DIM + ROPE_DIM, 1)
    ukv = full["w_ukv"].reshape(KV_LORA, HEADS, NOPE_DIM + V_DIM)
    wk = _pad_slots(ukv[:, :, :NOPE_DIM].reshape(KV_LORA, HEADS * NOPE_DIM), HEADS, NOPE_DIM, 1)
    wv = _pad_slots(ukv[:, :, NOPE_DIM:].reshape(KV_LORA, HEADS * V_DIM), HEADS, V_DIM, 1)
    g1, g2, g3, g4, g5 = (w2["attn_pre_norm"], w2["attn_post_norm"], w2["mlp_pre_norm"], w2["mlp_post_norm"],
                          w2["ple_norm"])
    gq, gkv, bg, convb = w2["q_a_norm"], w2["kv_a_norm"], w2["b_gate"], w2["conv_b"]
    swa_tile = min(SWA_TILE, t_rows)
    sink_rows = jnp.repeat(w2["sinks"].reshape(A_KV_HEADS, SWA_GROUP, 1), swa_tile, axis=2).reshape(
        A_KV_HEADS, 1, SWA_GROUP * swa_tile)
    swa_bias = _swa_bias(swa_tile)
    spread_q = _spread_matrix(HEADS, A_HEAD_DIM)
    spread_kv = _spread_matrix(A_KV_HEADS, A_HEAD_DIM)

    h1, qs, ks, vs, cq, cqn, ckv, ckvn, qm, km, vm, gate = _fwd_in(x2d, g1, win, bg + late_token, gq, gkv, wuq, wk, wv,
                                                                   spread_q, spread_kv, tabs, tm_wide)
    ya, lse_a = _swa_fwd(qs, ks, vs, swa_bias, sink_rows)
    yb, lse_b = _mla_fwd(qm, km, vm)
    late_sent, late_landed = _chips_wait("gather_late_a_wait", *late_a_sems[:4], WHOLE, after=yb)
    wout_g, packed_g = own_slot_filled(late_landed, late_sent)
    full = _unpack_gathered(packed_g, _LATE_PACKED)
    wba = _pad_slots(full["w_branch_a"], HEADS, A_HEAD_DIM, 0)
    wbb = _pad_slots(full["w_branch_b"], HEADS, V_DIM, 0)
    wple = full["w_ple"]
    wout = _rows_joined(wout_g).reshape(-1, D_MODEL)
    pa, pb, mixed, o, x1, h2 = _fwd_mix(x2d, ya, yb, gate, wba, wbb, wout, g2, g3, tm_wide)
    late_sent, late_landed = _chips_wait("gather_late_b_wait", *late_b_sems[:4], WHOLE, after=pa)
    natural = dict(zip([n for n in late_names if n != "w_out"], own_slot_filled(late_landed, late_sent)))
    wup = _rows_joined(natural["w_up"])
    wdown, wpg = (_rows_joined(natural[n]).reshape(-1, D_MODEL) for n in ("w_down", "w_ple_gate"))
    up, a = _fwd_up(h2, wup, convw8, convb, tm)
    ff, x2, e, n5, sg, dx3, loss_part = _fwd_out(a, wdown, x1, g4, p2d, wple, g5, wpg, tgt, tm_wide)

    dpre, de, dx2, dff, du, dg5, dg4, dconvb, dconvw8 = _bwd_out(dx3, e, sg, x2, ff, g5, g4, wpg, wdown, up, convw8,
                                                                 convb, tm)
    dup, dx1, do, dpa, dpb, dgates, dya, dyb, delta_b, dg3, dg2, dbg = _bwd_mid(
        du, convw8, wup, dx2, x1, g3, o, g2, wout, gate, pa, pb, wba, wbb, yb, tm)
    late_grads = {
        "w_branch_a": _unpad_slots(_mm_tn("dw_branch_a", ya, dpa), HEADS, A_HEAD_DIM, 0),
        "w_branch_b": _unpad_slots(_mm_tn("dw_branch_b", yb, dpb), HEADS, V_DIM, 0),
        "w_out": _mm_tn("dw_out", mixed, do).reshape(4, D_MODEL // 4, D_MODEL),
        "w_up": _mm_tn("dw_up", h2, dup, column_shards=4),
        "w_down": _mm_tn("dw_down", a, dff).reshape(4, D_FF // 4, D_MODEL),
        "w_ple_gate": _mm_tn("dw_ple_gate", n5, dpre).reshape(4, D_MODEL // 4, D_MODEL),
        "w_ple": _mm_tn("dw_ple", p2d, de),
    }

    def grad_views(grads, group, packed_group):
        return [_halves(grads[n]) for n in group if n in _NATURAL] + [_pack_grads(grads, packed_group)]

    def pair_sums(tag, views, theirs):
        return [_add_pair("rs_%s_add_pair_%d" % (tag, i), g, r, cc) for i, (g, r) in enumerate(zip(views, theirs))]

    swap_sems = _chips_start("swap_late_start", grad_views(late_grads, _LATE, _LATE_PACKED), SIBLING_HALF)
    dqs, dks, dvs, dsink_rows = _swa_bwd(qs, ks, vs, ya, dya, lse_a, swa_bias, sink_rows + swap_sems[4][0:1, 0:1])
    dsink = dsink_rows[:, 0:SWA_GROUP, 0]
    late_views, late_theirs = _chips_wait("swap_late_wait", *swap_sems[:4], SIBLING_HALF, after=dqs)
    rs_sems = _chips_start("scatter_late_start", pair_sums("late", late_views, late_theirs), PIECE)
    dqm, dkm, dvm = _mla_bwd(qm, km, vm, dyb, lse_b, delta_b.reshape(HEADS, 1, t_rows) + rs_sems[4][0:1, 0:1])
    dz, dqb, dx, dgq, dgkv, dg1 = _bwd_in(dqs, dks, dvs, dqm, dkm, dvm, tabs, consts, cq, ckv, gq, gkv, wuq, wk, wv,
                                           dgates, win, x2d, g1, dx1, tm)

    small = {"attn_pre_norm": dg1, "attn_post_norm": dg2, "b_gate": dbg, "sinks": dsink, "q_a_norm": dgq,
             "kv_a_norm": dgkv, "mlp_pre_norm": dg3, "mlp_post_norm": dg4, "conv_b": dconvb, "ple_norm": dg5,
             "conv_w": dconvw8[0:3], "loss": loss_part}
    small_sems = _chips_start("gather_small_start", [_pack_small(small)], EVERYONE)
    small_token = small_sems[4]

    dwk = _unpad_slots(_mm_tn("dw_k", ckvn, dkm, after=small_token), HEADS, NOPE_DIM, 1).reshape(
        KV_LORA, HEADS, NOPE_DIM)
    dwv = _unpad_slots(_mm_tn("dw_v", ckvn, dvm, after=small_token), HEADS, V_DIM, 1).reshape(KV_LORA, HEADS, V_DIM)
    early_grads = {
        "w_in": _unpad_w_in(_mm_tn("dw_in", h1, dz, after=small_token)).reshape(D_MODEL, 4, 808).transpose(1, 0, 2),
        "w_uq": _unpad_slots(_mm_tn("dw_uq", cqn, dqb, after=small_token), HEADS, NOPE_DIM + ROPE_DIM, 1),
        "w_ukv": jnp.concatenate([dwk, dwv], axis=2).reshape(KV_LORA, HEADS * (NOPE_DIM + V_DIM)),
    }

    def finish(tag, pairs, landed, group, packed_group):
        reduced = []
        for i, (pair, land) in enumerate(zip(pairs, landed)):
            own = lax.dynamic_index_in_dim(pair, chip, 0, keepdims=True)
            reduced.append(_add_chips("rs_%s_add_chips_%d" % (tag, i),
                                      lax.dynamic_update_slice(land, own, (chip, 0, 0))))
        others = _swap_sibling("swap_%s_reduced_halves" % tag, reduced)
        r, o = reduced[-1], others[-1]
        packed = jnp.where(cc == 0, jnp.stack([r, o]), jnp.stack([o, r]))
        for n, g in _unpack_shard_grads(packed, packed_group).items():
            updates[n] = _adamw("adamw_" + n, w2[n], g, m2[n], v2[n])
        for n, r, o in zip([n for n in group if n in _NATURAL], reduced, others):
            updates[n] = _adamw_halves("adamw_" + n, w2[n], r, o, m2[n], v2[n], cc)

    updates = {}

    def adamw(n, g):
        updates[n] = _adamw("adamw_" + n, w2[n], g, m2[n], v2[n])

    early_views = grad_views(early_grads, _EARLY, _EARLY_PACKED)
    early_theirs = _swap_sibling("swap_early_grad_halves", early_views, other_half=True)
    small_sent, small_landed = _chips_wait("gather_small_wait", *small_sems[:4], EVERYONE, after=early_theirs[0])
    small_all = lax.dynamic_update_slice(small_landed[0], small_sent[0][None], (4 * xc + 2 * yc + cc, 0, 0))
    early_sems = _chips_start("scatter_early_start", pair_sums("early", early_views, early_theirs), PIECE,
                              after=small_all)
    late_pairs, late_landed = _chips_wait("scatter_late_wait", *rs_sems[:4], PIECE, after=early_sems[4])
    finish("late", late_pairs, late_landed, _LATE, _LATE_PACKED)
    early_pairs, early_landed = _chips_wait("scatter_early_wait", *early_sems[:4], PIECE,
                                            after=updates[_LATE[-1]][1])
    finish("early", early_pairs, early_landed, _EARLY, _EARLY_PACKED)

    small_sum = _unpack_small(_add_devices(small_all))
    for n in names:
        if n == "conv_w":
            adamw(n, lax.dynamic_index_in_dim(small_sum[n].reshape(3, 4, 1408), chip, 1, keepdims=False))
        elif n in small_sum:
            adamw(n, small_sum[n].reshape(w2[n].shape))
    loss = small_sum["loss"][0]

    outs = [[updates[n][i].reshape(wts[n].shape) for n in names] for i in range(4)]
    return (loss, dx.reshape(x.shape), *outs[0], *outs[1], *outs[2], *outs[3])
```

```python
import functools
import math

import numpy as np
import jax
import jax.numpy as jnp
from jax import lax
from jax.experimental import pallas as pl
from jax.experimental.pallas import tpu as pltpu

F32 = jnp.float32
BF16 = jnp.bfloat16

D_MODEL = 1024
D_FF = 2816
PLE_DIM = 256
ROPE_THETA = 10000.0
RMS_EPS = 1e-6
SWA_WINDOW = 128
HEADS = 8
A_KV_HEADS = 2
A_HEAD_DIM = 64
Q_LORA = 256
KV_LORA = 128
NOPE_DIM = 64
ROPE_DIM = 32
V_DIM = 64
LANES = 128
ZW = 3328
NEG = -1e30
SCALE_A = A_HEAD_DIM ** -0.5
SCALE_B = (NOPE_DIM + ROPE_DIM) ** -0.5

ADAM_LR = 0.001
ADAM_B1 = 0.9
ADAM_B2 = 0.999
ADAM_EPS = 1e-08
ADAM_WD = 0.01
ADAM_STEP = 10

VMEM_LIMIT = 60 * 1024 * 1024
MESH_AXES = ("x", "y", "c")
MESH = pl.DeviceIdType.MESH

Z_QA, Z_KA, Z_VA, Z_CQ, Z_CKV, Z_KR, Z_GATE = 0, 512, 640, 768, 1024, 1152, 1280


def _dot(a, b):
    return jnp.dot(a, b, preferred_element_type=F32)


def _dot_nt(a, b):
    return lax.dot_general(a, b, (((1,), (1,)), ((), ())), preferred_element_type=F32)


def _dot_tn(a, b):
    return lax.dot_general(a, b, (((0,), (0,)), ((), ())), preferred_element_type=F32)


def _rms_stats(x):
    r = lax.rsqrt(jnp.mean(x * x, axis=-1, keepdims=True) + RMS_EPS)
    return x * r, r


def _rms_bwd(dy, xn, r, g):
    dxn = dy * g
    dx = r * (dxn - xn * jnp.mean(dxn * xn, axis=-1, keepdims=True))
    dg = jnp.sum(dy * xn, axis=0, keepdims=True)
    return dx, dg


def _tile_lanes(t, n):
    return t if n == 1 else jnp.concatenate([t] * n, axis=1)


def _rope(x, c, s1, s2, half):
    w = x.shape[1]
    n = w // LANES
    return (x * _tile_lanes(c, n) + pltpu.roll(x, w - half, 1) * _tile_lanes(s1, n)
            + pltpu.roll(x, half, 1) * _tile_lanes(s2, n))


def _rope_t(dy, c, s1, s2, half):
    w = dy.shape[1]
    n = w // LANES
    return (dy * _tile_lanes(c, n) + pltpu.roll(dy * _tile_lanes(s1, n), half, 1)
            + pltpu.roll(dy * _tile_lanes(s2, n), w - half, 1))


def _fold_slots(d):
    tiles = []
    for j in range(d.shape[1] // (2 * LANES)):
        even = d[:, 2 * j * LANES:(2 * j + 1) * LANES]
        odd = d[:, (2 * j + 1) * LANES:(2 * j + 2) * LANES]
        tiles.append(even + pltpu.roll(odd, A_HEAD_DIM, 1))
    return tiles[0] if len(tiles) == 1 else jnp.concatenate(tiles, axis=1)


def _spread_slots(c):
    low = lax.broadcasted_iota(jnp.int32, (c.shape[0], LANES), 1) < A_HEAD_DIM
    slots = []
    for j in range(c.shape[1] // LANES):
        tile = c[:, j * LANES:(j + 1) * LANES]
        slots += [jnp.where(low, tile, 0.0), jnp.where(low, pltpu.roll(tile, A_HEAD_DIM, 1), 0.0)]
    return jnp.concatenate(slots, axis=1)


def _sigmoid(x):
    return 1.0 / (1.0 + jnp.exp(-x))


_GELU_C = math.sqrt(2.0 / math.pi)


def _gelu_and_grad(x):
    a = _GELU_C + (_GELU_C * 0.044715) * (x * x)
    th = jnp.tanh(x * a)
    hx = 0.5 * x
    p1 = 1.0 + th
    gel = hx * p1
    dgel = 0.5 * p1 + (hx * (1.0 - th * th)) * (3.0 * a - 2.0 * _GELU_C)
    return gel, dgel


def _conv_taps(up, h6, h7):
    r1 = pltpu.roll(up, 1, 0)
    r2 = pltpu.roll(up, 2, 0)
    rows = lax.broadcasted_iota(jnp.int32, (8, up.shape[1]), 0)
    xm1 = jnp.concatenate([jnp.where(rows == 0, h7, r1[0:8]), r1[8:]], axis=0)
    xm2 = jnp.concatenate([jnp.where(rows == 0, h6, jnp.where(rows == 1, h7, r2[0:8])), r2[8:]], axis=0)
    return xm1, xm2


def _conv_taps_next(du, n0, n1):
    tm = du.shape[0]
    r1 = pltpu.roll(du, tm - 1, 0)
    r2 = pltpu.roll(du, tm - 2, 0)
    rows = lax.broadcasted_iota(jnp.int32, (8, du.shape[1]), 0)
    xp1 = jnp.concatenate([r1[:tm - 8], jnp.where(rows == 7, n0, r1[tm - 8:])], axis=0)
    xp2 = jnp.concatenate([r2[:tm - 8], jnp.where(rows == 6, n0, jnp.where(rows == 7, n1, r2[tm - 8:]))], axis=0)
    return xp1, xp2


def _row(tm, n):
    return pl.BlockSpec((tm, n), lambda i: (i, 0))


def _full(shape):
    nd = len(shape)
    return pl.BlockSpec(tuple(shape), lambda i: (0,) * nd)


def _resident(shape):
    nd = len(shape)
    return pl.BlockSpec(tuple(shape), lambda i: (0,) * nd, pipeline_mode=pl.Buffered(1))


def _heads(tm, h):
    return pl.BlockSpec((h, tm, LANES), lambda i: (0, i, 0))


def _rows_call(name, body, t_rows, tm, ins, outs, scratch=()):
    return pl.pallas_call(
        body, name=name, grid=(t_rows // tm,),
        in_specs=[s for _, s in ins],
        out_specs=[s for _, s in outs],
        out_shape=[s for s, _ in outs],
        scratch_shapes=list(scratch),
        compiler_params=pltpu.CompilerParams(dimension_semantics=("arbitrary",), vmem_limit_bytes=VMEM_LIMIT),
    )(*[a for a, _ in ins])


def _sds(shape, dtype):
    return jax.ShapeDtypeStruct(tuple(shape), dtype)


def _rope_consts():
    c = np.zeros((16, LANES), np.float32)
    lane = np.arange(LANES)
    inv_a = (ROPE_THETA ** (-(np.arange(0, A_HEAD_DIM, 2, dtype=np.float32) / A_HEAD_DIM))).astype(np.float32)
    in_a = lane < A_HEAD_DIM
    c[0, in_a] = inv_a[lane[in_a] % (A_HEAD_DIM // 2)]
    c[1, in_a] = 1.0
    c[2, lane < A_HEAD_DIM // 2] = -1.0
    c[3, (lane >= A_HEAD_DIM // 2) & in_a] = 1.0
    inv_b = (ROPE_THETA ** (-(np.arange(0, ROPE_DIM, 2, dtype=np.float32) / ROPE_DIM))).astype(np.float32)
    pe = (lane >= NOPE_DIM) & (lane < NOPE_DIM + ROPE_DIM)
    c[5, pe] = inv_b[(lane[pe] - NOPE_DIM) % (ROPE_DIM // 2)]
    c[6, pe] = 1.0
    c[7, (lane >= NOPE_DIM) & (lane < NOPE_DIM + ROPE_DIM // 2)] = -1.0
    c[8, (lane >= NOPE_DIM + ROPE_DIM // 2) & (lane < NOPE_DIM + ROPE_DIM)] = 1.0
    c[9, lane < NOPE_DIM] = 1.0
    c[10, pe] = 1.0
    return jnp.asarray(c)


def _rope_tables(pos_f, consts, tm):
    t_rows = pos_f.shape[0]

    def body(pos_ref, c_ref, ca, sa1, sa2, cb, sb1, sb2):
        ang = pos_ref[...] * (c_ref[0:1, :] + c_ref[5:6, :])
        cs, sn = jnp.cos(ang), jnp.sin(ang)
        for ref, row in ((ca, 1), (sa1, 2), (sa2, 3)):
            half = (cs if row == 1 else sn) * c_ref[row:row + 1, :]
            ref[...] = half + pltpu.roll(half, A_HEAD_DIM, 1)
        cb[...] = cs * c_ref[6:7, :] + c_ref[9:10, :]
        sb1[...] = sn * c_ref[7:8, :]
        sb2[...] = sn * c_ref[8:9, :]

    tab = (_sds((t_rows, LANES), F32), _row(tm, LANES))
    return _rows_call("rope_tables", body, t_rows, tm,
                      [(pos_f, _row(tm, 1)), (consts, _full(consts.shape))], [tab] * 6)


def _fwd_in(x, g1, win, bg, gq, gkv, wuq, wk, wv, eq, ek, tabs, tm):
    t_rows = x.shape[0]

    def body(x_ref, g1_ref, win_ref, bg_ref, gq_ref, gkv_ref, wuq_ref, wk_ref, wv_ref, eq_ref, ek_ref,
             ca, sa1, sa2, cb, sb1, sb2,
             h1_ref, qs_ref, ks_ref, vs_ref, cq_ref, cqn_ref, ckv_ref, ckvn_ref, qm_ref, km_ref, vm_ref, gate_ref):
        xn, _ = _rms_stats(x_ref[...])
        hb = (xn * g1_ref[...]).astype(BF16)
        h1_ref[...] = hb
        ta = (ca[...], sa1[...], sa2[...])
        tb = (cb[...], sb1[...], sb2[...])
        cq = _dot(hb, win_ref[:, Z_CQ:Z_CKV])
        ckv = _dot(hb, win_ref[:, Z_CKV:Z_KR])
        z_qa = _dot(hb, win_ref[:, Z_QA:Z_KA])
        z_ka = _dot(hb, win_ref[:, Z_KA:Z_VA])
        z_va = _dot(hb, win_ref[:, Z_VA:Z_CQ])
        z_kr = _dot(hb, win_ref[:, Z_KR:Z_GATE])
        cq_ref[...] = cq
        cqn, _ = _rms_stats(cq)
        cqb = (cqn * gq_ref[...]).astype(BF16)
        cqn_ref[...] = cqb
        ckv_ref[...] = ckv
        ckvn, _ = _rms_stats(ckv)
        ckvb = (ckvn * gkv_ref[...]).astype(BF16)
        ckvn_ref[...] = ckvb
        z_qm = _dot(cqb, wuq_ref[...])
        z_km = _dot(ckvb, wk_ref[...])
        z_vm = _dot(ckvb, wv_ref[...])
        z_gate = _dot(hb, win_ref[:, Z_GATE:ZW])
        qs_ref[...] = _dot((_rope(z_qa, *ta, A_HEAD_DIM // 2) * SCALE_A).astype(BF16), eq_ref[...]).astype(BF16)
        ks_ref[...] = _dot(_rope(z_ka, *ta, A_HEAD_DIM // 2).astype(BF16), ek_ref[...]).astype(BF16)
        vs_ref[...] = _dot(z_va.astype(BF16), ek_ref[...]).astype(BF16)
        qm_ref[...] = (_rope(z_qm, *tb, ROPE_DIM // 2) * SCALE_B).astype(BF16)
        km_ref[...] = (z_km + _tile_lanes(_rope(z_kr, *tb, ROPE_DIM // 2), HEADS)).astype(BF16)
        vm_ref[...] = z_vm.astype(BF16)
        gate_ref[...] = _sigmoid(z_gate + bg_ref[...])

    def o(n, dt):
        return (_sds((t_rows, n), dt), _row(tm, n))

    ins = [(x, _row(tm, D_MODEL)), (g1, _full(g1.shape)), (win, _resident(win.shape)), (bg, _full(bg.shape)),
           (gq, _full(gq.shape)), (gkv, _full(gkv.shape)), (wuq, _full(wuq.shape)), (wk, _full(wk.shape)),
           (wv, _full(wv.shape)), (eq, _full(eq.shape)), (ek, _full(ek.shape))] + [(t, _row(tm, LANES)) for t in tabs]
    outs = [o(1024, BF16), o(1024, BF16), o(256, BF16), o(256, BF16), o(256, F32), o(256, BF16), o(128, F32),
            o(128, BF16), o(1024, BF16), o(1024, BF16), o(1024, BF16), o(2048, F32)]
    return _rows_call("fwd_in", body, t_rows, tm, ins, outs)


def _attn_tile(t_rows):
    return min(512, t_rows)


MLA_HEADS_PER_STEP = 4
MLA_FWD_HEADS_PER_STEP = 8


def _causal_pairs(nq, by_kv):
    if by_kv:
        pairs = [(i, j) for j in range(nq) for i in range(j, nq)]
    else:
        pairs = [(i, j) for i in range(nq) for j in range(i + 1)]
    return (jnp.asarray([p[0] for p in pairs], jnp.int32), jnp.asarray([p[1] for p in pairs], jnp.int32))


def _mla_fwd(q, k, v):
    t_rows = q.shape[0]
    t = _attn_tile(t_rows)
    hp = MLA_FWD_HEADS_PER_STEP
    w = hp * LANES
    ii, jj = _causal_pairs(t_rows // t, by_kv=False)

    def body(i_ref, j_ref, q_ref, k_ref, v_ref, o_ref, lse_ref, m_s, l_s, acc_s):
        i = i_ref[pl.program_id(1)]
        j = j_ref[pl.program_id(1)]

        @pl.when(j == 0)
        def _():
            m_s[...] = jnp.full(m_s.shape, NEG, F32)
            l_s[...] = jnp.zeros(l_s.shape, F32)
            acc_s[...] = jnp.zeros(acc_s.shape, F32)

        def step(diagonal):
            sls = [slice(hh * LANES, (hh + 1) * LANES) for hh in range(hp)]
            scores = [_dot_nt(k_ref[:, sl], q_ref[:, sl]) for sl in sls]
            if diagonal:
                valid = (lax.broadcasted_iota(jnp.int32, (t, t), 0) <= lax.broadcasted_iota(jnp.int32, (t, t), 1))
                scores = [jnp.where(valid, s, NEG) for s in scores]
            stats = []
            for hh, s in enumerate(scores):
                m_prev = m_s[hh]
                m_new = jnp.maximum(m_prev, jnp.max(s, axis=0, keepdims=True))
                p = jnp.exp(s - m_new)
                alpha = jnp.exp(m_prev - m_new)
                stats.append((m_new, alpha, alpha * l_s[hh] + jnp.sum(p, axis=0, keepdims=True), p.astype(BF16)))
            for hh, (m_new, alpha, l_new, p) in enumerate(stats):
                sl = sls[hh]
                acc = alpha * acc_s[hh] + _dot_tn(v_ref[:, sl], p)
                if diagonal:
                    o_ref[:, sl] = (acc / l_new).T.astype(o_ref.dtype)
                    lse_ref[hh] = m_new + jnp.log(l_new)
                else:
                    m_s[hh] = m_new
                    l_s[hh] = l_new
                    acc_s[hh] = acc

        pl.when(j < i)(lambda: step(False))
        pl.when(j == i)(lambda: step(True))

    grid_spec = pltpu.PrefetchScalarGridSpec(
        num_scalar_prefetch=2, grid=(HEADS // hp, ii.shape[0]),
        in_specs=[pl.BlockSpec((t, w), lambda hb, s, ir, jr: (ir[s], hb)),
                  pl.BlockSpec((t, w), lambda hb, s, ir, jr: (jr[s], hb)),
                  pl.BlockSpec((t, w), lambda hb, s, ir, jr: (jr[s], hb))],
        out_specs=[pl.BlockSpec((t, w), lambda hb, s, ir, jr: (ir[s], hb)),
                   pl.BlockSpec((hp, 1, t), lambda hb, s, ir, jr: (hb, 0, ir[s]))],
        scratch_shapes=[pltpu.VMEM((hp, 1, t), F32), pltpu.VMEM((hp, 1, t), F32), pltpu.VMEM((hp, LANES, t), F32)])
    return pl.pallas_call(
        body, name="mla_fwd", grid_spec=grid_spec,
        out_shape=[_sds((t_rows, HEADS * LANES), BF16), _sds((HEADS, 1, t_rows), F32)],
        compiler_params=pltpu.CompilerParams(dimension_semantics=("arbitrary",) * 2, vmem_limit_bytes=VMEM_LIMIT),
    )(ii, jj, q, k, v)


def _mla_bwd(q, k, v, do, lse, delta):
    t_rows = q.shape[0]
    t = _attn_tile(t_rows)
    hp = MLA_HEADS_PER_STEP
    w = hp * LANES
    ii, jj = _causal_pairs(t_rows // t, by_kv=True)

    def body(i_ref, j_ref, q_ref, k_ref, v_ref, do_ref, lse_ref, dl_ref, dq_ref, dk_ref, dv_ref):
        i = i_ref[pl.program_id(1)]
        j = j_ref[pl.program_id(1)]

        @pl.when(pl.program_id(1) == 0)
        def _():
            dq_ref[...] = jnp.zeros(dq_ref.shape, F32)

        def step(diagonal):
            r0 = pl.multiple_of(i * t, t)
            sls = [slice(hh * LANES, (hh + 1) * LANES) for hh in range(hp)]
            scores = [_dot_nt(k_ref[:, sl], q_ref[:, sl]) for sl in sls]
            if diagonal:
                valid = (lax.broadcasted_iota(jnp.int32, (t, t), 0) <= lax.broadcasted_iota(jnp.int32, (t, t), 1))
                scores = [jnp.where(valid, s, NEG) for s in scores]
            dps = [_dot_nt(v_ref[:, sl], do_ref[:, sl]) for sl in sls]
            ps = [jnp.exp(s - lse_ref[hh]) for hh, s in enumerate(scores)]
            dss = [(p * (dp - dl_ref[hh])).astype(BF16) for hh, (p, dp) in enumerate(zip(ps, dps))]
            for hh, sl in enumerate(sls):
                dv = _dot(ps[hh].astype(BF16), do_ref[:, sl])
                dk = _dot(dss[hh], q_ref[:, sl])
                if diagonal:
                    dv_ref[:, sl] = dv
                    dk_ref[:, sl] = dk
                else:
                    dv_ref[:, sl] += dv
                    dk_ref[:, sl] += dk
                dq_ref[hh, pl.ds(r0, t), :] += _dot_tn(dss[hh], k_ref[:, sl])

        pl.when(i > j)(lambda: step(False))
        pl.when(i == j)(lambda: step(True))

    def qmap(hb, s, ir, jr):
        return (ir[s], hb)

    def kvmap(hb, s, ir, jr):
        return (jr[s], hb)

    def rowmap(hb, s, ir, jr):
        return (hb, 0, ir[s])

    grid_spec = pltpu.PrefetchScalarGridSpec(
        num_scalar_prefetch=2, grid=(HEADS // hp, ii.shape[0]),
        in_specs=[pl.BlockSpec((t, w), qmap), pl.BlockSpec((t, w), kvmap), pl.BlockSpec((t, w), kvmap),
                  pl.BlockSpec((t, w), qmap), pl.BlockSpec((hp, 1, t), rowmap), pl.BlockSpec((hp, 1, t), rowmap)],
        out_specs=[pl.BlockSpec((hp, t_rows, LANES), lambda hb, s, ir, jr: (hb, 0, 0)),
                   pl.BlockSpec((t, w), kvmap), pl.BlockSpec((t, w), kvmap)])
    return pl.pallas_call(
        body, name="mla_bwd", grid_spec=grid_spec,
        out_shape=[_sds((HEADS, t_rows, LANES), F32), _sds((t_rows, HEADS * LANES), F32),
                   _sds((t_rows, HEADS * LANES), F32)],
        compiler_params=pltpu.CompilerParams(dimension_semantics=("arbitrary",) * 2, vmem_limit_bytes=VMEM_LIMIT),
    )(ii, jj, q, k, v, do, lse, delta)


SWA_TILE = 2 * SWA_WINDOW
SWA_GROUP = HEADS // A_KV_HEADS


def _swa_bias(tq):
    koff = lax.broadcasted_iota(jnp.int32, (tq + SWA_WINDOW, SWA_GROUP * tq), 0) - SWA_WINDOW
    qoff = (lax.broadcasted_iota(jnp.int32, (tq + SWA_WINDOW, SWA_GROUP * tq), 1) % tq)
    band = (koff <= qoff) & (qoff - koff < SWA_WINDOW)
    return jnp.stack([jnp.where(band & (koff >= 0), 0.0, NEG), jnp.where(band, 0.0, NEG)]).astype(F32)


def _swa_specs(tq, nq):
    wb = tq // SWA_WINDOW
    kvw = A_KV_HEADS * LANES

    def qi(i):
        return jnp.minimum(i, nq - 1)

    q = pl.BlockSpec((tq, HEADS * LANES), lambda i: (qi(i), 0))
    cur = pl.BlockSpec((tq, kvw), lambda i: (qi(i), 0))
    prev = pl.BlockSpec((SWA_WINDOW, kvw), lambda i: (jnp.maximum(qi(i) * wb - 1, 0), 0))
    bias = pl.BlockSpec((1, tq + SWA_WINDOW, SWA_GROUP * tq), lambda i: (jnp.minimum(i, 1), 0, 0))
    rows = pl.BlockSpec((A_KV_HEADS, 1, 1, SWA_GROUP * tq), lambda i: (0, qi(i), 0, 0))
    sink = pl.BlockSpec((A_KV_HEADS, 1, SWA_GROUP * tq), lambda i: (0, 0, 0))
    return q, cur, prev, bias, rows, sink


def _stack_heads(ref, kvh):
    base = kvh * SWA_GROUP
    return jnp.concatenate([ref[:, (base + g) * LANES:(base + g + 1) * LANES] for g in range(SWA_GROUP)], axis=0)


def _unstack_heads(ref, kvh, val, tq):
    base = kvh * SWA_GROUP
    for g in range(SWA_GROUP):
        ref[:, (base + g) * LANES:(base + g + 1) * LANES] = val[g * tq:(g + 1) * tq].astype(ref.dtype)


def _kv_window(prev_ref, cur_ref, kvh):
    sl = slice(kvh * LANES, (kvh + 1) * LANES)
    return jnp.concatenate([prev_ref[:, sl], cur_ref[:, sl]], axis=0)


def _swa_fwd(q, k, v, bias, sink_rows):
    t_rows = q.shape[0]
    tq = min(SWA_TILE, t_rows)
    nq = t_rows // tq
    qs_, cur, prev, bs, rows, sk = _swa_specs(tq, nq)
    kvhs = range(A_KV_HEADS)

    def body(q_ref, kc_ref, kp_ref, vc_ref, vp_ref, b_ref, sink_ref, o_ref, lse_ref):
        scores = [_dot_nt(_kv_window(kp_ref, kc_ref, h), _stack_heads(q_ref, h)) + b_ref[0] for h in kvhs]
        stats = []
        for h, s in zip(kvhs, scores):
            sink = sink_ref[h]
            m = jnp.maximum(jnp.max(s, axis=0, keepdims=True), sink)
            p = jnp.exp(s - m)
            l = jnp.sum(p, axis=0, keepdims=True) + jnp.exp(sink - m)
            lse_ref[h, 0] = m + jnp.log(l)
            stats.append((p.astype(BF16), l))
        for h, (p, l) in zip(kvhs, stats):
            _unstack_heads(o_ref, h, (_dot_tn(_kv_window(vp_ref, vc_ref, h), p) / l).T, tq)

    return pl.pallas_call(
        body, name="swa_fwd", grid=(nq,),
        in_specs=[qs_, cur, prev, cur, prev, bs, sk],
        out_specs=[qs_, rows],
        out_shape=[_sds((t_rows, HEADS * LANES), BF16), _sds((A_KV_HEADS, nq, 1, SWA_GROUP * tq), F32)],
        compiler_params=pltpu.CompilerParams(dimension_semantics=("arbitrary",), vmem_limit_bytes=VMEM_LIMIT),
    )(q, k, k, v, v, bias, sink_rows)


def _swa_bwd(q, k, v, o, do, lse, bias, sink_rows):
    t_rows = q.shape[0]
    tq = min(SWA_TILE, t_rows)
    nq = t_rows // tq
    qs_, cur, prev, bs, rows, sk = _swa_specs(tq, nq)
    hw = SWA_WINDOW
    kvhs = range(A_KV_HEADS)
    kvw = A_KV_HEADS * LANES

    def body(q_ref, kc_ref, kp_ref, vc_ref, vp_ref, o_ref, do_ref, lse_ref, b_ref, sink_ref,
             dq_ref, dk_ref, dv_ref, dsink_ref, ck, cv, dsa):
        i = pl.program_id(0)

        @pl.when(i == 0)
        def _():
            dsa[...] = jnp.zeros(dsa.shape, F32)

        @pl.when(i < nq)
        def _():
            qs = [_stack_heads(q_ref, h) for h in kvhs]
            dos = [_stack_heads(do_ref, h) for h in kvhs]
            kks = [_kv_window(kp_ref, kc_ref, h) for h in kvhs]
            scores = [_dot_nt(kks[h], qs[h]) for h in kvhs]
            dps = [_dot_nt(_kv_window(vp_ref, vc_ref, h), dos[h]) for h in kvhs]
            ps, dss = [], []
            for h in kvhs:
                lse = lse_ref[h, 0]
                p = jnp.exp(scores[h] + b_ref[0] - lse)
                delta = jnp.sum((_stack_heads(o_ref, h).astype(F32) * dos[h].astype(F32)).T, axis=0, keepdims=True)
                dsa[h] += -jnp.exp(sink_ref[h] - lse) * delta
                ps.append(p.astype(BF16))
                dss.append((p * (dps[h] - delta)).astype(BF16))
            for h in kvhs:
                sl = slice(h * LANES, (h + 1) * LANES)
                dv = _dot(ps[h], dos[h])
                dk = _dot(dss[h], qs[h])
                _unstack_heads(dq_ref, h, _dot_tn(dss[h], kks[h]), tq)

                @pl.when(i > 0)
                def _():
                    dk_ref[0:tq - hw, sl] = ck[0:tq - hw, sl]
                    dk_ref[tq - hw:tq, sl] = ck[tq - hw:tq, sl] + dk[0:hw]
                    dv_ref[0:tq - hw, sl] = cv[0:tq - hw, sl]
                    dv_ref[tq - hw:tq, sl] = cv[tq - hw:tq, sl] + dv[0:hw]

                ck[:, sl] = dk[hw:hw + tq]
                cv[:, sl] = dv[hw:hw + tq]

        @pl.when(i == nq)
        def _():
            dk_ref[...] = ck[...]
            dv_ref[...] = cv[...]
            dsink_ref[...] = jnp.zeros(dsink_ref.shape, F32)
            for h in kvhs:
                for g in range(SWA_GROUP):
                    tot = jnp.sum(dsa[h, :, g * tq:(g + 1) * tq], axis=1, keepdims=True)
                    dsink_ref[h, g:g + 1, :] = jnp.zeros((1, LANES), F32) + tot

    kv_out = pl.BlockSpec((tq, kvw), lambda i: (jnp.maximum(i - 1, 0), 0))
    return pl.pallas_call(
        body, name="swa_bwd", grid=(nq + 1,),
        in_specs=[qs_, cur, prev, cur, prev, qs_, qs_, rows, bs, sk],
        out_specs=[qs_, kv_out, kv_out, pl.BlockSpec((A_KV_HEADS, 8, LANES), lambda i: (0, 0, 0))],
        out_shape=[_sds((t_rows, HEADS * LANES), F32), _sds((t_rows, kvw), F32), _sds((t_rows, kvw), F32),
                   _sds((A_KV_HEADS, 8, LANES), F32)],
        scratch_shapes=[pltpu.VMEM((tq, kvw), F32), pltpu.VMEM((tq, kvw), F32),
                        pltpu.VMEM((A_KV_HEADS, 1, SWA_GROUP * tq), F32)],
        compiler_params=pltpu.CompilerParams(dimension_semantics=("arbitrary",), vmem_limit_bytes=VMEM_LIMIT),
    )(q, k, k, v, v, o, do, lse, bias, sink_rows)


def _fwd_mix(x, ya, yb, gate, wba, wbb, wout, g2, g3, tm):
    t_rows = x.shape[0]

    def body(x_ref, ya_ref, yb_ref, gate_ref, wba_ref, wbb_ref, wout_ref, g2_ref, g3_ref,
             pa_ref, pb_ref, mixed_ref, o_ref, x1_ref, h2_ref, yac_ref, ybc_ref):
        yac = _fold_slots(ya_ref[...].astype(F32)).astype(BF16)
        ybc = _fold_slots(yb_ref[...].astype(F32)).astype(BF16)
        yac_ref[...] = yac
        ybc_ref[...] = ybc
        pa = _dot(yac, wba_ref[...])
        pb = _dot(ybc, wbb_ref[...])
        pa_ref[...] = pa
        pb_ref[...] = pb
        mixed = (gate_ref[:, 0:D_MODEL] * pa + gate_ref[:, D_MODEL:2 * D_MODEL] * pb).astype(BF16)
        mixed_ref[...] = mixed
        o = _dot(mixed, wout_ref[...])
        o_ref[...] = o
        on, _ = _rms_stats(o)
        x1 = x_ref[...] + on * g2_ref[...]
        x1_ref[...] = x1
        x1n, _ = _rms_stats(x1)
        h2_ref[...] = (x1n * g3_ref[...]).astype(BF16)

    def o_(dt):
        return (_sds((t_rows, D_MODEL), dt), _row(tm, D_MODEL))

    ins = [(x, _row(tm, D_MODEL)), (ya, _row(tm, 1024)), (yb, _row(tm, 1024)), (gate, _row(tm, 2048)),
           (wba, _resident(wba.shape)), (wbb, _resident(wbb.shape)), (wout, _resident(wout.shape)),
           (g2, _full(g2.shape)), (g3, _full(g3.shape))]
    half = (_sds((t_rows, D_MODEL // 2), BF16), _row(tm, D_MODEL // 2))
    return _rows_call("fwd_mix", body, t_rows, tm, ins,
                      [o_(F32), o_(F32), o_(BF16), o_(F32), o_(F32), o_(BF16), half, half])


CONV_CHUNK = 1408


def _fwd_up(h2, wup, convw8, convb, tm):
    t_rows = h2.shape[0]
    cdim = 2 * D_FF

    def body(h2_ref, wup_ref, cw_ref, cb_ref, up_ref, a_ref, carry):
        i = pl.program_id(0)

        @pl.when(i == 0)
        def _():
            carry[...] = jnp.zeros(carry.shape, F32)

        hb = h2_ref[...]
        ups = [_dot(hb, wup_ref[s]) for s in range(cdim // CONV_CHUNK)]

        def conv(c0):
            sl = slice(c0, c0 + CONV_CHUNK)
            up = ups[c0 // CONV_CHUNK]
            up_ref[:, sl] = up
            xm1, xm2 = _conv_taps(up, carry[6:7, sl], carry[7:8, sl])
            u = cw_ref[0:1, sl] * xm2 + cw_ref[1:2, sl] * xm1 + cw_ref[2:3, sl] * up + cb_ref[:, sl]
            carry[:, sl] = up[tm - 8:tm, :]
            return u

        for c0 in range(0, D_FF, CONV_CHUNK):
            ug = conv(c0)
            uv = conv(D_FF + c0)
            gel, _ = _gelu_and_grad(ug)
            a_ref[:, c0:c0 + CONV_CHUNK] = (gel * uv).astype(BF16)

    ins = [(h2, _row(tm, D_MODEL)), (wup, _resident(wup.shape)), (convw8, _full(convw8.shape)), (convb, _full(convb.shape))]
    outs = [(_sds((t_rows, cdim), F32), _row(tm, cdim)), (_sds((t_rows, D_FF), BF16), _row(tm, D_FF))]
    return _rows_call("fwd_up", body, t_rows, tm, ins, outs, scratch=[pltpu.VMEM((8, cdim), F32)])


def _fwd_out(a, wdown, x1, g4, p, wple, g5, wpg, tgt, tm):
    t_rows = a.shape[0]

    def body(a_ref, wdown_ref, x1_ref, g4_ref, p_ref, wple_ref, g5_ref, wpg_ref, tgt_ref,
             ff_ref, x2_ref, e_ref, n5_ref, sg_ref, dx3_ref, loss_ref):
        i = pl.program_id(0)
        ff = _dot(a_ref[...], wdown_ref[...])
        e = _dot(p_ref[...].astype(BF16), wple_ref[...])
        ff_ref[...] = ff
        ffn, _ = _rms_stats(ff)
        x2 = x1_ref[...] + ffn * g4_ref[...]
        x2_ref[...] = x2
        e_ref[...] = e
        x2n, _ = _rms_stats(x2)
        n5 = (x2n * g5_ref[...]).astype(BF16)
        n5_ref[...] = n5
        sg = _sigmoid(_dot(n5, wpg_ref[...]))
        sg_ref[...] = sg
        d = x2 + sg * e - tgt_ref[...]
        dx3_ref[...] = d * (1.0 / D_MODEL)

        @pl.when(i == 0)
        def _():
            loss_ref[...] = jnp.zeros((1, 1), F32)

        loss_ref[...] += 0.5 * jnp.sum(jnp.sum(d * d, axis=1, keepdims=True), axis=0, keepdims=True) * (1.0 / D_MODEL)

    def o_(dt):
        return (_sds((t_rows, D_MODEL), dt), _row(tm, D_MODEL))

    ins = [(a, _row(tm, D_FF)), (wdown, _resident(wdown.shape)), (x1, _row(tm, D_MODEL)), (g4, _full(g4.shape)),
           (p, _row(tm, PLE_DIM)), (wple, _full(wple.shape)), (g5, _full(g5.shape)), (wpg, _resident(wpg.shape)),
           (tgt, _row(tm, D_MODEL))]
    outs = [o_(F32), o_(F32), o_(F32), o_(BF16), o_(F32), o_(F32), (_sds((1, 1), F32), _full((1, 1)))]
    return _rows_call("fwd_out", body, t_rows, tm, ins, outs)


def _bwd_out(dx3, e, sg, x2, ff, g5, g4, wpg, wdown, up, convw8, convb, tm):
    t_rows = dx3.shape[0]
    cdim = 2 * D_FF
    hb = tm // 8

    def body(dx3_ref, e_ref, sg_ref, x2_ref, ff_ref, g5_ref, g4_ref, wpg_ref, wdown_ref, up_ref, halo_ref, cw_ref,
             cb_ref, dpre_ref, de_ref, dx2_ref, dff_ref, du_ref, dg5_ref, dg4_ref, dcb_ref, dcw_ref):
        i = pl.program_id(0)

        @pl.when(i == 0)
        def _():
            dg5_ref[...] = jnp.zeros(dg5_ref.shape, F32)
            dg4_ref[...] = jnp.zeros(dg4_ref.shape, F32)
            dcb_ref[...] = jnp.zeros(dcb_ref.shape, F32)
            dcw_ref[...] = jnp.zeros(dcw_ref.shape, F32)

        dx3 = dx3_ref[...]
        sg = sg_ref[...]
        dpre = (dx3 * e_ref[...] * sg * (1.0 - sg)).astype(BF16)
        dpre_ref[...] = dpre
        de_ref[...] = (dx3 * sg).astype(BF16)
        dn5 = _dot_nt(dpre, wpg_ref[...])
        x2n, r5 = _rms_stats(x2_ref[...])
        d2, dg5 = _rms_bwd(dn5, x2n, r5, g5_ref[...])
        dx2 = dx3 + d2
        dx2_ref[...] = dx2
        dg5_ref[...] += dg5
        ffn, r4 = _rms_stats(ff_ref[...])
        dff, dg4 = _rms_bwd(dx2, ffn, r4, g4_ref[...])
        dg4_ref[...] += dg4
        dffb = dff.astype(BF16)
        dff_ref[...] = dffb
        keep = jnp.where(i > 0, 1.0, 0.0)

        def conv(c0):
            sl = slice(c0, c0 + CONV_CHUNK)
            up = up_ref[:, sl]
            xm1, xm2 = _conv_taps(up, halo_ref[6:7, sl] * keep, halo_ref[7:8, sl] * keep)
            u = cw_ref[0:1, sl] * xm2 + cw_ref[1:2, sl] * xm1 + cw_ref[2:3, sl] * up + cb_ref[:, sl]
            return u, up, xm1, xm2

        def grads(c0, du, up, xm1, xm2):
            sl = slice(c0, c0 + CONV_CHUNK)
            du_ref[:, sl] = du.astype(BF16)
            dcb_ref[:, sl] += jnp.sum(du, axis=0, keepdims=True)
            dcw_ref[0:1, sl] += jnp.sum(du * xm2, axis=0, keepdims=True)
            dcw_ref[1:2, sl] += jnp.sum(du * xm1, axis=0, keepdims=True)
            dcw_ref[2:3, sl] += jnp.sum(du * up, axis=0, keepdims=True)

        for c0 in range(0, D_FF, CONV_CHUNK):
            da = _dot_nt(dffb, wdown_ref[c0:c0 + CONV_CHUNK, :])
            ug, *rg = conv(c0)
            uv, *rv = conv(D_FF + c0)
            gel, dgel = _gelu_and_grad(ug)
            grads(c0, da * uv * dgel, *rg)
            grads(D_FF + c0, da * gel, *rv)

    def o_(n, dt):
        return (_sds((t_rows, n), dt), _row(tm, n))

    def acc(r, n):
        return (_sds((r, n), F32), _full((r, n)))

    halo = pl.BlockSpec((8, cdim), lambda i: (jnp.maximum(i * hb - 1, 0), 0))
    ins = [(dx3, _row(tm, D_MODEL)), (e, _row(tm, D_MODEL)), (sg, _row(tm, D_MODEL)), (x2, _row(tm, D_MODEL)),
           (ff, _row(tm, D_MODEL)), (g5, _full(g5.shape)), (g4, _full(g4.shape)), (wpg, _resident(wpg.shape)),
           (wdown, _resident(wdown.shape)), (up, _row(tm, cdim)), (up, halo), (convw8, _full(convw8.shape)),
           (convb, _full(convb.shape))]
    outs = [o_(D_MODEL, BF16), o_(D_MODEL, BF16), o_(D_MODEL, F32), o_(D_MODEL, BF16), o_(cdim, BF16),
            acc(1, D_MODEL), acc(1, D_MODEL), acc(1, cdim), acc(8, cdim)]
    return _rows_call("bwd_out", body, t_rows, tm, ins, outs)


def _bwd_mid(du, convw8, wup, dx2, x1, g3, o, g2, wout, gate, pa, pb, wba, wbb, yb, tm):
    t_rows = du.shape[0]
    cdim = 2 * D_FF
    halo_rows = 16
    hb = tm // halo_rows
    last_blk = t_rows // halo_rows - 1
    n_tiles = t_rows // tm

    def body(du_ref, halo_ref, cw_ref, wup_ref, dx2_ref, x1_ref, g3_ref, o_ref, g2_ref, wout_ref, gate_ref, pa_ref,
             pb_ref, wba_ref, wbb_ref, yb_ref,
             dup_ref, dx1_ref, do_ref, dpa_ref, dpb_ref, dgt_ref, dya_ref, dyb_ref, dl_ref, dg3_ref, dg2_ref, dbg_ref):
        i = pl.program_id(0)

        @pl.when(i == 0)
        def _():
            dg3_ref[...] = jnp.zeros(dg3_ref.shape, F32)
            dg2_ref[...] = jnp.zeros(dg2_ref.shape, F32)
            dbg_ref[...] = jnp.zeros(dbg_ref.shape, F32)

        keep = jnp.where(i < n_tiles - 1, 1.0, 0.0)
        dh2 = jnp.zeros((tm, D_MODEL), F32)
        dups = []
        for c0 in range(0, cdim, CONV_CHUNK):
            sl = slice(c0, c0 + CONV_CHUNK)
            du = du_ref[:, sl].astype(F32)
            nxt = halo_ref[:, sl].astype(F32)
            xp1, xp2 = _conv_taps_next(du, nxt[0:1] * keep, nxt[1:2] * keep)
            dups.append((cw_ref[2:3, sl] * du + cw_ref[1:2, sl] * xp1 + cw_ref[0:1, sl] * xp2).astype(BF16))
            dup_ref[:, sl] = dups[-1]
            if len(dups) > 1:
                dh2 = dh2 + _dot_nt(dups[-2], wup_ref[len(dups) - 2])
        dh2 = dh2 + _dot_nt(dups[-1], wup_ref[len(dups) - 1])
        x1n, r3 = _rms_stats(x1_ref[...])
        d1, dg3 = _rms_bwd(dh2, x1n, r3, g3_ref[...])
        dx1 = dx2_ref[...] + d1
        dx1_ref[...] = dx1
        dg3_ref[...] += dg3
        on, r2 = _rms_stats(o_ref[...])
        do, dg2 = _rms_bwd(dx1, on, r2, g2_ref[...])
        dg2_ref[...] += dg2
        dob = do.astype(BF16)
        do_ref[...] = dob
        dmixed = _dot_nt(dob, wout_ref[...])
        ga = gate_ref[:, 0:D_MODEL]
        gb = gate_ref[:, D_MODEL:2 * D_MODEL]
        dpa = (dmixed * ga).astype(BF16)
        dpb = (dmixed * gb).astype(BF16)
        dpa_ref[...] = dpa
        dpb_ref[...] = dpb
        dga = dmixed * pa_ref[...] * ga * (1.0 - ga)
        dgb = dmixed * pb_ref[...] * gb * (1.0 - gb)
        dgt_ref[:, 0:D_MODEL] = dga.astype(BF16)
        dgt_ref[:, D_MODEL:2 * D_MODEL] = dgb.astype(BF16)
        dbg_ref[:, 0:D_MODEL] += jnp.sum(dga, axis=0, keepdims=True)
        dbg_ref[:, D_MODEL:2 * D_MODEL] += jnp.sum(dgb, axis=0, keepdims=True)
        dya_ref[...] = _spread_slots(_dot_nt(dpa, wba_ref[...])).astype(BF16)
        dyb = _dot_nt(dpb, wbb_ref[...]).astype(BF16)
        dyb_ref[...] = _spread_slots(dyb.astype(F32)).astype(BF16)
        prod = yb_ref[...].astype(F32) * dyb.astype(F32)
        width = HEADS * V_DIM
        lane_head = lax.broadcasted_iota(jnp.int32, (HEADS, width), 1) // V_DIM
        sel = (lane_head == lax.broadcasted_iota(jnp.int32, (HEADS, width), 0)).astype(BF16)
        hi = prod.astype(BF16)
        lo = (prod - hi.astype(F32)).astype(BF16)
        dl_ref[...] = _dot_nt(sel, hi) + _dot_nt(sel, lo)

    def o_(n, dt):
        return (_sds((t_rows, n), dt), _row(tm, n))

    def acc(r, n):
        return (_sds((r, n), F32), _full((r, n)))

    halo = pl.BlockSpec((halo_rows, cdim), lambda i: (jnp.minimum((i + 1) * hb, last_blk), 0))
    ins = [(du, _row(tm, cdim)), (du, halo), (convw8, _full(convw8.shape)), (wup, _resident(wup.shape)),
           (dx2, _row(tm, D_MODEL)), (x1, _row(tm, D_MODEL)), (g3, _full(g3.shape)), (o, _row(tm, D_MODEL)),
           (g2, _full(g2.shape)), (wout, _resident(wout.shape)), (gate, _row(tm, 2048)), (pa, _row(tm, D_MODEL)),
           (pb, _row(tm, D_MODEL)), (wba, _resident(wba.shape)), (wbb, _resident(wbb.shape)), (yb, _row(tm, D_MODEL // 2))]
    outs = [o_(cdim, BF16), o_(D_MODEL, F32), o_(D_MODEL, BF16), o_(D_MODEL, BF16), o_(D_MODEL, BF16),
            o_(2048, BF16), o_(1024, BF16), o_(1024, BF16),
            (_sds((HEADS, t_rows), F32), pl.BlockSpec((HEADS, tm), lambda i: (0, i))),
            acc(1, D_MODEL), acc(1, D_MODEL), acc(1, 2048)]
    return _rows_call("bwd_mid", body, t_rows, tm, ins, outs)


def _bwd_in(dqs, dks, dvs, dqm, dkm, dvm, tabs, consts, cq, ckv, gq, gkv, wuq, wk, wv, dgates, win, x, g1, dx1, tm):
    t_rows = x.shape[0]

    def body(dqs_ref, dks_ref, dvs_ref, dqm_ref, dkm_ref, dvm_ref, ca, sa1, sa2, cb, sb1, sb2, c_ref, cq_ref,
             ckv_ref, gq_ref, gkv_ref, wuq_ref, wk_ref, wv_ref, dgt_ref, win_ref, x_ref, g1_ref, dx1_ref,
             dz_ref, dqb_ref, dx_ref, dgq_ref, dgkv_ref, dg1_ref):
        i = pl.program_id(0)

        @pl.when(i == 0)
        def _():
            dgq_ref[...] = jnp.zeros(dgq_ref.shape, F32)
            dgkv_ref[...] = jnp.zeros(dgkv_ref.shape, F32)
            dg1_ref[...] = jnp.zeros(dg1_ref.shape, F32)

        ta = (ca[...], sa1[...], sa2[...])
        tb = (cb[...], sb1[...], sb2[...])

        def piece(lo, hi, val):
            dz_ref[:, lo:hi] = val
            return _dot_nt(val, win_ref[:, lo:hi])

        dh1 = piece(Z_GATE, ZW, dgt_ref[...])
        dkm = dkm_ref[...]
        dckvn = _dot_nt(dkm.astype(BF16), wk_ref[...]) + _dot_nt(dvm_ref[...].astype(BF16), wv_ref[...])
        dh1 = dh1 + piece(Z_VA, Z_CQ, _fold_slots(dvs_ref[...]).astype(BF16))
        dqm = jnp.concatenate([dqm_ref[h] for h in range(HEADS)], axis=1)
        dqb = _rope_t(dqm * SCALE_B, *tb, ROPE_DIM // 2).astype(BF16)
        dqb_ref[...] = dqb
        dcqn = _dot_nt(dqb, wuq_ref[...])
        dqa = _rope_t(_fold_slots(dqs_ref[...]) * SCALE_A, *ta, A_HEAD_DIM // 2)
        dh1 = dh1 + piece(Z_QA, Z_KA, dqa.astype(BF16))
        dh1 = dh1 + piece(Z_KA, Z_VA, _rope_t(_fold_slots(dks_ref[...]), *ta, A_HEAD_DIM // 2).astype(BF16))
        ckvn, rkv = _rms_stats(ckv_ref[...])
        dckv, dgkv = _rms_bwd(dckvn, ckvn, rkv, gkv_ref[...])
        dgkv_ref[...] += dgkv
        dh1 = dh1 + piece(Z_CKV, Z_KR, dckv.astype(BF16))
        dslot = dkm[:, 0:LANES]
        for h in range(1, HEADS):
            dslot = dslot + dkm[:, h * LANES:(h + 1) * LANES]
        dh1 = dh1 + piece(Z_KR, Z_GATE, _rope_t(dslot * c_ref[10:11, :], *tb, ROPE_DIM // 2).astype(BF16))
        cqn, rq = _rms_stats(cq_ref[...])
        dcq, dgq = _rms_bwd(dcqn, cqn, rq, gq_ref[...])
        dgq_ref[...] += dgq
        dh1 = dh1 + piece(Z_CQ, Z_CKV, dcq.astype(BF16))
        xn, r1 = _rms_stats(x_ref[...])
        d0, dg1 = _rms_bwd(dh1, xn, r1, g1_ref[...])
        dg1_ref[...] += dg1
        dx_ref[...] = dx1_ref[...] + d0

    def acc(n):
        return (_sds((1, n), F32), _full((1, n)))

    ins = [(dqs, _row(tm, 1024)), (dks, _row(tm, 256)), (dvs, _row(tm, 256)), (dqm, _heads(tm, HEADS)),
           (dkm, _row(tm, 1024)), (dvm, _row(tm, 1024))] + [(t, _row(tm, LANES)) for t in tabs] + [
           (consts, _full(consts.shape)), (cq, _row(tm, 256)), (ckv, _row(tm, 128)), (gq, _full(gq.shape)),
           (gkv, _full(gkv.shape)), (wuq, _full(wuq.shape)), (wk, _full(wk.shape)), (wv, _full(wv.shape)),
           (dgates, _row(tm, 2048)), (win, _resident(win.shape)), (x, _row(tm, D_MODEL)), (g1, _full(g1.shape)),
           (dx1, _row(tm, D_MODEL))]
    outs = [(_sds((t_rows, ZW), BF16), _row(tm, ZW)), (_sds((t_rows, 1024), BF16), _row(tm, 1024)),
            (_sds((t_rows, D_MODEL), F32), _row(tm, D_MODEL)), acc(256), acc(128), acc(D_MODEL)]
    return _rows_call("bwd_in", body, t_rows, tm, ins, outs)


def _pick_cols(n):
    best = LANES
    for d in range(LANES, min(n, 1664) + 1, LANES):
        if n % d == 0:
            best = d
    return best


def _mm_tn(name, a, b, column_shards=1, after=None):
    t_rows, m = a.shape
    n = b.shape[1]
    bk = min(1024, t_rows)
    bm, bn = _pick_cols(m), _pick_cols(n // column_shards)
    per_shard = n // column_shards // bn
    extra = () if after is None else (after,)

    def body(a_ref, b_ref, *rest):
        o_ref = rest[-1]

        @pl.when(pl.program_id(2) == 0)
        def _():
            o_ref[...] = jnp.zeros((bm, bn), F32)

        o_ref[...] += _dot_tn(a_ref[...].astype(BF16), b_ref[...].astype(BF16))

    return pl.pallas_call(
        body, name=name, grid=(m // bm, n // bn, t_rows // bk),
        in_specs=[pl.BlockSpec((bk, bm), lambda i, j, k: (k, i)), pl.BlockSpec((bk, bn), lambda i, j, k: (k, j))]
        + [pl.BlockSpec((8, LANES), lambda i, j, k: (0, 0))] * len(extra),
        out_specs=(pl.BlockSpec((bm, bn), lambda i, j, k: (i, j)) if column_shards == 1 else
                   pl.BlockSpec((None, bm, bn), lambda i, j, k: (j // per_shard, i, j % per_shard))),
        out_shape=_sds((m, n) if column_shards == 1 else (column_shards, m, n // column_shards), F32),
        compiler_params=pltpu.CompilerParams(dimension_semantics=("arbitrary",) * 3, vmem_limit_bytes=VMEM_LIMIT),
    )(a, b, *extra)


PACK_ROWS = 512


ADD_TILE_ELEMS = 1 << 17


def _add_rows(rows, cols):
    best = 16
    for d in range(16, rows + 1, 16):
        if rows % d == 0 and d * cols <= ADD_TILE_ELEMS:
            best = d
    assert rows % best == 0
    return best


def _add_pair(name, g, recv, half):
    _, _, rows, cols = g.shape
    t = _add_rows(rows, cols)

    def body(h_ref, g_ref, r_ref, o_ref):
        o_ref[...] = (g_ref[:, 0] + r_ref[...]).astype(BF16)

    spec = pl.BlockSpec((4, t, cols), lambda i, h: (0, i, 0))
    grid_spec = pltpu.PrefetchScalarGridSpec(
        num_scalar_prefetch=1, grid=(rows // t,),
        in_specs=[pl.BlockSpec((4, 1, t, cols), lambda i, h: (0, h[0], i, 0)), spec], out_specs=spec)
    return pl.pallas_call(body, name=name, grid_spec=grid_spec,
                          out_shape=_sds(recv.shape, BF16))(jnp.reshape(half, (1,)).astype(jnp.int32), g, recv)


def _add_chips(name, parts):
    _, rows, cols = parts.shape
    t = _add_rows(rows, cols)

    def body(p_ref, o_ref):
        acc = p_ref[0].astype(F32)
        for j in range(1, 4):
            acc = acc + p_ref[j].astype(F32)
        o_ref[...] = acc

    return pl.pallas_call(body, name=name, grid=(rows // t,),
                          in_specs=[pl.BlockSpec((4, t, cols), lambda i: (0, i, 0))],
                          out_specs=pl.BlockSpec((t, cols), lambda i: (i, 0)),
                          out_shape=_sds((rows, cols), F32))(parts)


def _add_devices(parts):
    n, rows, _ = parts.shape

    def body(p_ref, o_ref):
        acc = p_ref[0]
        for j in range(1, n):
            acc = acc + p_ref[j]
        o_ref[...] = acc

    return pl.pallas_call(body, name="small_add", grid=(1,),
                          in_specs=[pl.BlockSpec((n, rows, LANES), lambda i: (0, 0, 0))],
                          out_specs=pl.BlockSpec((rows, LANES), lambda i: (0, 0)),
                          out_shape=_sds((rows, LANES), F32))(parts)


def _adam_rows(k, n):
    target = max(8, (1 << 20) // (4 * n))
    if k <= target:
        return k
    best = None
    for d in range(8, target + 1, 8):
        if k % d == 0:
            best = d
    return best if best is not None else k


def _adam_update(w, g, m, v):
    m_ = ADAM_B1 * m + (1.0 - ADAM_B1) * g
    v_ = ADAM_B2 * v + (1.0 - ADAM_B2) * (g * g)
    delta = -ADAM_LR * ((m_ / (1.0 - ADAM_B1 ** ADAM_STEP)) / (jnp.sqrt(v_ / (1.0 - ADAM_B2 ** ADAM_STEP)) + ADAM_EPS)
                        + ADAM_WD * w)
    return delta, m_, v_


def _adamw(name, w, g, m, v):
    k, n = w.shape
    bk = _adam_rows(k, n)

    def body(w_ref, g_ref, m_ref, v_ref, d_ref, mo_ref, vo_ref):
        d_ref[...], mo_ref[...], vo_ref[...] = _adam_update(w_ref[...], g_ref[...], m_ref[...], v_ref[...])

    spec = pl.BlockSpec((bk, n), lambda i: (i, 0))
    out = pl.pallas_call(body, name=name, grid=(k // bk,), in_specs=[spec] * 4, out_specs=[spec] * 3,
                         out_shape=[_sds((k, n), F32)] * 3,
                         compiler_params=pltpu.CompilerParams(vmem_limit_bytes=VMEM_LIMIT))(w, g, m, v)
    return (g, *out)


def _adamw_halves(name, w, mine, theirs, m, v, half):
    k, n = w.shape
    bk = _adam_rows(k // 2, n)
    nb = k // 2 // bk

    def body(h_ref, w_ref, mine_ref, theirs_ref, m_ref, v_ref, g_ref, d_ref, mo_ref, vo_ref):
        g = jnp.where(pl.program_id(0) == h_ref[0], mine_ref[...], theirs_ref[...])
        g_ref[...] = g
        d_ref[...], mo_ref[...], vo_ref[...] = _adam_update(w_ref[...], g, m_ref[...], v_ref[...])

    full = pl.BlockSpec((bk, n), lambda h, i, c: (h * nb + i, 0))
    part = pl.BlockSpec((bk, n), lambda h, i, c: (i, 0))
    grid_spec = pltpu.PrefetchScalarGridSpec(num_scalar_prefetch=1, grid=(2, nb),
                                             in_specs=[full, part, part, full, full], out_specs=[full] * 4)
    return tuple(pl.pallas_call(
        body, name=name, grid_spec=grid_spec, out_shape=[_sds((k, n), F32)] * 4,
        compiler_params=pltpu.CompilerParams(vmem_limit_bytes=VMEM_LIMIT),
    )(jnp.reshape(half, (1,)).astype(jnp.int32), w, mine, theirs, m, v))


_HBM = pl.BlockSpec(memory_space=pltpu.HBM)


def _me():
    return lax.axis_index("x"), lax.axis_index("y"), lax.axis_index("c")


def _other_chips(x, y):
    return [(1 - x, y), (x, 1 - y), (1 - x, 1 - y)]


def _pass_to_sibling(zones):
    n = len(zones)

    def body(*refs):
        in_refs, out_refs = refs[:n], refs[n:2 * n]
        send_sems, recv_sems = refs[2 * n:]
        x, y, c = _me()
        sent = []
        for a, (in_ref, out_ref) in enumerate(zip(in_refs, out_refs)):
            for j, (cx, cy) in enumerate(_other_chips(x, y)):
                mine, theirs = (2 * cx + cy, c), (2 * cx + cy, 1 - c)
                sems = dict(send_sem=send_sems.at[3 * a + j], recv_sem=recv_sems.at[3 * a + j],
                            device_id=(x, y, 1 - c), device_id_type=MESH)
                sent.append((pltpu.make_async_remote_copy(src_ref=in_ref.at[mine], dst_ref=out_ref.at[mine], **sems),
                             pltpu.make_async_remote_copy(src_ref=in_ref.at[theirs], dst_ref=out_ref.at[theirs], **sems)))
        for send, _ in sent:
            send.start()
        for _, recv in sent:
            recv.wait_recv()
        for send, _ in sent:
            send.wait_send()

    return pl.pallas_call(
        body, name="pass_to_sibling", out_shape=[_sds(z.shape, z.dtype) for z in zones],
        in_specs=[_HBM] * n, out_specs=[_HBM] * n, input_output_aliases={i: i for i in range(n)},
        scratch_shapes=[pltpu.SemaphoreType.DMA((3 * n,)), pltpu.SemaphoreType.DMA((3 * n,))],
    )(*zones)


def _swap_sibling(name, vs, other_half=False):
    n = len(vs)

    def body(*refs):
        v_refs, out_refs = refs[:n], refs[n:2 * n]
        send_sems, recv_sems = refs[2 * n:]
        x, y, c = _me()
        cps = [pltpu.make_async_remote_copy(src_ref=v_ref.at[:, 1 - c] if other_half else v_ref, dst_ref=out_ref,
                                            send_sem=send_sems.at[a], recv_sem=recv_sems.at[a],
                                            device_id=(x, y, 1 - c), device_id_type=MESH)
               for a, (v_ref, out_ref) in enumerate(zip(v_refs, out_refs))]
        for cp in cps:
            cp.start()
        for cp in cps:
            cp.wait()

    def landing(v):
        return _sds((v.shape[0],) + v.shape[2:] if other_half else v.shape, v.dtype)

    return pl.pallas_call(
        body, name=name, out_shape=[landing(v) for v in vs], in_specs=[_HBM] * n, out_specs=[_HBM] * n,
        scratch_shapes=[pltpu.SemaphoreType.DMA((n,)), pltpu.SemaphoreType.DMA((n,))],
    )(*vs)


_SEM = pl.BlockSpec(memory_space=pltpu.SEMAPHORE)
_EFFECT = pltpu.SideEffectType.DATAFLOW_SIDE_EFFECTING
WHOLE = "whole"
PIECE = "piece"
SIBLING_HALF = "sibling"
MY_HALF = "half"
EVERYONE = "everyone"
_COPIES = {WHOLE: 3, PIECE: 3, MY_HALF: 3, SIBLING_HALF: 1, EVERYONE: 7}


def _landing_shape(v, mode):
    return {WHOLE: (4,) + v.shape, MY_HALF: (4,) + v.shape, PIECE: v.shape, EVERYONE: (8,) + v.shape,
            SIBLING_HALF: (v.shape[0],) + v.shape[2:]}[mode]


def _chip_copies(v_ref, land_ref, send_sems, recv_sems, mode, sem0=0):
    x, y, c = _me()
    if mode == SIBLING_HALF:
        cp = pltpu.make_async_remote_copy(src_ref=v_ref.at[:, 1 - c], dst_ref=land_ref, send_sem=send_sems.at[sem0],
                                          recv_sem=recv_sems.at[sem0], device_id=(x, y, 1 - c), device_id_type=MESH)
        return [(cp, cp)]
    if mode == EVERYONE:
        out = []
        for f in range(1, 8):
            px, py, pc = (1 - x if f & 4 else x), (1 - y if f & 2 else y), (1 - c if f & 1 else c)
            sems = dict(send_sem=send_sems.at[sem0 + f - 1], recv_sem=recv_sems.at[sem0 + f - 1],
                        device_id=(px, py, pc), device_id_type=MESH)
            out.append((pltpu.make_async_remote_copy(src_ref=v_ref, dst_ref=land_ref.at[4 * x + 2 * y + c], **sems),
                        pltpu.make_async_remote_copy(src_ref=v_ref, dst_ref=land_ref.at[4 * px + 2 * py + pc], **sems)))
        return out
    k = 2 * x + y
    out = []
    for j, (cx, cy) in enumerate(_other_chips(x, y)):
        if mode == MY_HALF:
            src, mine, theirs = v_ref.at[c], land_ref.at[k, c], land_ref.at[2 * cx + cy, c]
        else:
            src = v_ref.at[2 * cx + cy] if mode == PIECE else v_ref
            mine, theirs = land_ref.at[k], land_ref.at[2 * cx + cy]
        sems = dict(send_sem=send_sems.at[sem0 + j], recv_sem=recv_sems.at[sem0 + j], device_id=(cx, cy, c),
                    device_id_type=MESH)
        send = pltpu.make_async_remote_copy(src_ref=src, dst_ref=mine, **sems)
        recv = pltpu.make_async_remote_copy(src_ref=src, dst_ref=theirs, **sems)
        out.append((send, recv))
    return out


def _chips_start(name, vs, mode, after=None):
    n = len(vs)
    lands = [_landing_shape(v, mode) for v in vs]

    def body(*refs):
        v_refs, land_refs = refs[:n], refs[n:2 * n]
        send_sems, recv_sems = refs[-2 * n - 3], refs[-2 * n - 2]
        token = refs[-1]
        for a in range(n):
            for send, _ in _chip_copies(v_refs[a], land_refs[a], send_sems, recv_sems, mode, _COPIES[mode] * a):
                send.start()
        token[...] = jnp.zeros_like(token)

    extra = () if after is None else (after,)
    hbm = [pltpu.with_memory_space_constraint(v, pltpu.HBM) for v in vs]
    zones = [pltpu.with_memory_space_constraint(lax.empty(s, v.dtype), pltpu.HBM) for s, v in zip(lands, vs)]
    out = pl.pallas_call(
        body, name=name,
        out_shape=(pltpu.SemaphoreType.DMA((_COPIES[mode] * n,)), pltpu.SemaphoreType.DMA((_COPIES[mode] * n,)),
                   *[pltpu.HBM(v.shape, v.dtype) for v in vs], *[pltpu.HBM(s, v.dtype) for s, v in zip(lands, vs)],
                   _sds((8, LANES), F32)),
        in_specs=(_HBM,) * (2 * n) + (pl.BlockSpec(memory_space=pl.ANY),) * len(extra),
        out_specs=(_SEM, _SEM) + (_HBM,) * (2 * n) + (pl.BlockSpec(memory_space=pltpu.VMEM),),
        input_output_aliases={i: 2 + i for i in range(2 * n)},
        compiler_params=pltpu.CompilerParams(has_side_effects=_EFFECT),
    )(*hbm, *zones, *extra)
    return out[0], out[1], list(out[2:2 + n]), list(out[2 + n:2 + 2 * n]), out[-1]


def _chips_wait(name, send_sems, recv_sems, v_thru, land_thru, mode, after):
    n = len(v_thru)

    def body(*refs):
        v_refs, land_refs = refs[:n], refs[n:2 * n]
        send_sems, recv_sems = refs[2 * n], refs[2 * n + 1]
        for a in range(n):
            for send, recv in _chip_copies(v_refs[a], land_refs[a], send_sems, recv_sems, mode, _COPIES[mode] * a):
                send.wait_send()
                recv.wait_recv()

    out = pl.pallas_call(
        body, name=name,
        out_shape=tuple(pltpu.HBM(a.shape, a.dtype) for a in list(v_thru) + list(land_thru)),
        in_specs=(_HBM,) * (2 * n) + (_SEM, _SEM, pl.BlockSpec(memory_space=pl.ANY)), out_specs=(_HBM,) * (2 * n),
        input_output_aliases={i: i for i in range(2 * n)},
        compiler_params=pltpu.CompilerParams(has_side_effects=_EFFECT),
    )(*v_thru, *land_thru, send_sems, recv_sems, after)
    return list(out[:n]), list(out[n:])


_BIG = (("w_in", (1024, 3232), 1), ("w_uq", (256, 768), 1), ("w_ukv", (128, 1024), 1), ("w_branch_a", (512, 1024), 1),
        ("w_branch_b", (512, 1024), 1), ("w_out", (1024, 1024), 0), ("w_up", (1024, 5632), 1),
        ("w_down", (2816, 1024), 0), ("w_ple_gate", (1024, 1024), 0), ("w_ple", (256, 1024), 1))


def _shard_shape(shape, axis):
    return (shape[0] // 4, shape[1]) if axis == 0 else (shape[0], shape[1] // 4)


def _half_rows(shape, axis):
    k, n = _shard_shape(shape, axis)
    return k * n // (2 * LANES)


_EARLY = ("w_in", "w_uq", "w_ukv")
_LATE = ("w_branch_a", "w_branch_b", "w_out", "w_up", "w_down", "w_ple_gate", "w_ple")
_NATURAL = ("w_in", "w_up", "w_down", "w_out", "w_ple_gate")
_EARLY_PACKED = tuple(b for b in _BIG if b[0] in _EARLY and b[0] not in _NATURAL)
_LATE_PACKED = tuple(b for b in _BIG if b[0] in _LATE and b[0] not in _NATURAL)
_SHARD = {name: _shard_shape(shape, axis) for name, shape, axis in _BIG}


def _halves(a):
    return a.reshape(a.shape[:-2] + (2, a.shape[-2] // 2, a.shape[-1]))


def _rows_joined(a):
    return a.reshape(a.shape[:-3] + (a.shape[-3] * a.shape[-2], a.shape[-1]))


def _pack_pad(group):
    return -sum(_half_rows(shape, axis) for _, shape, axis in group) % PACK_ROWS


def _pack_shards(shards, dtype, group):
    parts = [shards[name].astype(dtype).reshape(2, _half_rows(shape, axis), LANES) for name, shape, axis in group]
    return jnp.concatenate(parts + [jnp.zeros((2, _pack_pad(group), LANES), dtype)], axis=1)


def _unpack_gathered(g, group):
    out, off = {}, 0
    for name, shape, axis in group:
        r = _half_rows(shape, axis)
        k, n = _shard_shape(shape, axis)
        w = g[:, :, off:off + r, :].reshape(4, k, n)
        out[name] = w.reshape(shape) if axis == 0 else w.transpose(1, 0, 2).reshape(shape)
        off += r
    return out


def _pack_grads(grads, group):
    parts = []
    for name, shape, axis in group:
        k, n = _shard_shape(shape, axis)
        g = grads[name]
        g4 = g.reshape(4, k, n) if axis == 0 else g.reshape(k, 4, n).transpose(1, 0, 2)
        parts.append(g4.reshape(4, 2, _half_rows(shape, axis), LANES))
    return jnp.concatenate(parts + [jnp.zeros((4, 2, _pack_pad(group), LANES), F32)], axis=2)


def _unpack_shard_grads(f, group):
    out, off = {}, 0
    for name, shape, axis in group:
        r = _half_rows(shape, axis)
        out[name] = f[:, off:off + r, :].reshape(_shard_shape(shape, axis))
        off += r
    return out


def _pad_slots(w, heads, dim, axis):
    if axis == 1:
        k = w.shape[0]
        return jnp.pad(w.reshape(k, heads, dim), ((0, 0), (0, 0), (0, LANES - dim))).reshape(k, heads * LANES)
    n = w.shape[1]
    return jnp.pad(w.reshape(heads, dim, n), ((0, 0), (0, LANES - dim), (0, 0))).reshape(heads * LANES, n)


def _unpad_slots(w, heads, dim, axis):
    if axis == 1:
        k = w.shape[0]
        return w.reshape(k, heads, LANES)[:, :, :dim].reshape(k, heads * dim)
    n = w.shape[1]
    return w.reshape(heads, LANES, n)[:, :dim, :].reshape(heads * dim, n)


def _pad_w_in(w):
    kr = jnp.pad(w[:, Z_KR:Z_KR + ROPE_DIM], ((0, 0), (NOPE_DIM, LANES - NOPE_DIM - ROPE_DIM)))
    return jnp.concatenate([w[:, :Z_KR], kr, w[:, Z_KR + ROPE_DIM:]], axis=1)


def _unpad_w_in(w):
    return jnp.concatenate([w[:, :Z_KR], w[:, Z_KR + NOPE_DIM:Z_KR + NOPE_DIM + ROPE_DIM], w[:, Z_GATE:ZW]], axis=1)


def _spread_matrix(heads, dim):
    row = lax.broadcasted_iota(jnp.int32, (heads * dim, heads * LANES), 0)
    col = lax.broadcasted_iota(jnp.int32, (heads * dim, heads * LANES), 1)
    return (col == (row // dim) * LANES + row % dim).astype(BF16)


_SMALL = (("attn_pre_norm", 1024), ("attn_post_norm", 1024), ("b_gate", 2048), ("sinks", 8), ("q_a_norm", 256),
          ("kv_a_norm", 128), ("mlp_pre_norm", 1024), ("mlp_post_norm", 1024), ("conv_b", 5632), ("ple_norm", 1024),
          ("conv_w", 3 * 5632), ("loss", 1))


def _small_rows(n):
    return 8 * -(-n // (8 * LANES))


def _pack_small(vals):
    parts = []
    for name, n in _SMALL:
        r = _small_rows(n)
        parts.append(jnp.pad(vals[name].reshape(-1), (0, r * LANES - n)).reshape(r, LANES))
    return jnp.concatenate(parts, axis=0)


def _unpack_small(buf):
    out, off = {}, 0
    for name, n in _SMALL:
        r = _small_rows(n)
        out[name] = buf[off:off + r].reshape(-1)[:n]
        off += r
    return out


def kernel(x, p, positions, attn_pre_norm, attn_post_norm, w_in, b_gate, sinks, q_a_norm, w_uq, kv_a_norm, w_ukv, w_branch_a, w_branch_b, w_out, mlp_pre_norm, mlp_post_norm, w_up, conv_w, conv_b, w_down, ple_norm, w_ple_gate, w_ple, loss_target, m_attn_pre_norm, m_attn_post_norm, m_w_in, m_b_gate, m_sinks, m_q_a_norm, m_w_uq, m_kv_a_norm, m_w_ukv, m_w_branch_a, m_w_branch_b, m_w_out, m_mlp_pre_norm, m_mlp_post_norm, m_w_up, m_conv_w, m_conv_b, m_w_down, m_ple_norm, m_w_ple_gate, m_w_ple, v_attn_pre_norm, v_attn_post_norm, v_w_in, v_b_gate, v_sinks, v_q_a_norm, v_w_uq, v_kv_a_norm, v_w_ukv, v_w_branch_a, v_w_branch_b, v_w_out, v_mlp_pre_norm, v_mlp_post_norm, v_w_up, v_conv_w, v_conv_b, v_w_down, v_ple_norm, v_w_ple_gate, v_w_ple):
    names = ["attn_pre_norm", "attn_post_norm", "w_in", "b_gate", "sinks", "q_a_norm", "w_uq", "kv_a_norm", "w_ukv",
             "w_branch_a", "w_branch_b", "w_out", "mlp_pre_norm", "mlp_post_norm", "w_up", "conv_w", "conv_b",
             "w_down", "ple_norm", "w_ple_gate", "w_ple"]
    wts = dict(zip(names, [attn_pre_norm, attn_post_norm, w_in, b_gate, sinks, q_a_norm, w_uq, kv_a_norm, w_ukv,
                           w_branch_a, w_branch_b, w_out, mlp_pre_norm, mlp_post_norm, w_up, conv_w, conv_b, w_down,
                           ple_norm, w_ple_gate, w_ple]))
    moms = dict(zip(names, [m_attn_pre_norm, m_attn_post_norm, m_w_in, m_b_gate, m_sinks, m_q_a_norm, m_w_uq,
                            m_kv_a_norm, m_w_ukv, m_w_branch_a, m_w_branch_b, m_w_out, m_mlp_pre_norm,
                            m_mlp_post_norm, m_w_up, m_conv_w, m_conv_b, m_w_down, m_ple_norm, m_w_ple_gate, m_w_ple]))
    vars_ = dict(zip(names, [v_attn_pre_norm, v_attn_post_norm, v_w_in, v_b_gate, v_sinks, v_q_a_norm, v_w_uq,
                             v_kv_a_norm, v_w_ukv, v_w_branch_a, v_w_branch_b, v_w_out, v_mlp_pre_norm,
                             v_mlp_post_norm, v_w_up, v_conv_w, v_conv_b, v_w_down, v_ple_norm, v_w_ple_gate, v_w_ple]))
    w2 = {n: a.reshape(a.shape[-2:]) for n, a in wts.items()}
    m2 = {n: a.reshape(a.shape[-2:]) for n, a in moms.items()}
    v2 = {n: a.reshape(a.shape[-2:]) for n, a in vars_.items()}

    t_rows = x.shape[-2]
    tm = min(256, t_rows)
    tm_wide = min(512, t_rows)
    xc, yc, cc = lax.axis_index("x"), lax.axis_index("y"), lax.axis_index("c")
    chip = 2 * xc + yc

    x2d = x.reshape(t_rows, D_MODEL)
    p2d = p.reshape(t_rows, PLE_DIM)
    tgt = loss_target.reshape(t_rows, D_MODEL)
    pos_f = positions.reshape(t_rows, 1).astype(F32)

    def own_slot_filled(gathered, mine):
        return [lax.dynamic_update_slice(g, m[None], (chip, 0, 0, 0)) for g, m in zip(gathered, mine)]

    def shard_lists(group, packed_group, token=0.0):
        ws = {n: w2[n] + token for n in group}
        return [_halves(ws[n].astype(BF16)) for n in group if n in _NATURAL] + [_pack_shards(ws, BF16, packed_group)]

    cw_rows = 3 * 1408 // LANES
    conv_mine = jnp.pad(w2["conv_w"].reshape(cw_rows, LANES), ((0, 48 - cw_rows), (0, 0))).reshape(2, 24, LANES)
    early_mine = shard_lists(_EARLY, _EARLY_PACKED) + [conv_mine]
    early_sems = _chips_start("gather_early_start", early_mine, MY_HALF)
    early_token = early_sems[4][0:1, 0:1]
    consts = _rope_consts()
    tabs = _rope_tables(pos_f + early_token, consts, tm)
    late_mine = shard_lists(_LATE, _LATE_PACKED, early_token)
    both_done = tabs[0][0:1, 0:1] + sum(m[0, 0:1, 0:1].astype(F32) for m in late_mine)
    early_sent, early_landed = _chips_wait("gather_early_wait", *early_sems[:4], MY_HALF, after=both_done)
    early = own_slot_filled(_pass_to_sibling(early_landed), early_sent)
    late_names = [n for n in _LATE if n in _NATURAL]
    first = [late_names.index("w_out"), len(late_names)]
    late_a = [late_mine[i] for i in first]
    late_b = [m for i, m in enumerate(late_mine) if i not in first]
    late_a_sems = _chips_start("gather_late_a_start", late_a, WHOLE, after=early[0])
    late_b_sems = _chips_start("gather_late_b_start", late_b, WHOLE, after=late_a_sems[4])
    late_token = late_b_sems[4][0:1, 0:1]
    full = _unpack_gathered(early[1], _EARLY_PACKED)
    full["w_in"] = _rows_joined(early[0]).transpose(1, 0, 2).reshape(D_MODEL, 3232)
    conv_full = early[2].reshape(4, 48, LANES)[:, :cw_rows].reshape(4, 3, 1408).transpose(1, 0, 2).reshape(3, 2 * D_FF)
    convw8 = jnp.pad(conv_full, ((0, 5), (0, 0)))

    win = _pad_w_in(full["w_in"])
    wuq = _pad_slots(full["w_uq"], HEADS, NOPE_DIM + ROPE_DIM, 1)
    ukv = full["w_ukv"].reshape(KV_LORA, HEADS, NOPE_DIM + V_DIM)
    wk = _pad_slots(ukv[:, :, :NOPE_DIM].reshape(KV_LORA, HEADS * NOPE_DIM), HEADS, NOPE_DIM, 1)
    wv = _pad_slots(ukv[:, :, NOPE_DIM:].reshape(KV_LORA, HEADS * V_DIM), HEADS, V_DIM, 1)
    g1, g2, g3, g4, g5 = (w2["attn_pre_norm"], w2["attn_post_norm"], w2["mlp_pre_norm"], w2["mlp_post_norm"],
                          w2["ple_norm"])
    gq, gkv, bg, convb = w2["q_a_norm"], w2["kv_a_norm"], w2["b_gate"], w2["conv_b"]
    swa_tile = min(SWA_TILE, t_rows)
    sink_rows = jnp.repeat(w2["sinks"].reshape(A_KV_HEADS, SWA_GROUP, 1), swa_tile, axis=2).reshape(
        A_KV_HEADS, 1, SWA_GROUP * swa_tile)
    swa_bias = _swa_bias(swa_tile)
    spread_q = _spread_matrix(HEADS, A_HEAD_DIM)
    spread_kv = _spread_matrix(A_KV_HEADS, A_HEAD_DIM)

    h1, qs, ks, vs, cq, cqn, ckv, ckvn, qm, km, vm, gate = _fwd_in(x2d, g1, win, bg + late_token, gq, gkv, wuq, wk, wv,
                                                                   spread_q, spread_kv, tabs, tm_wide)
    ya, lse_a = _swa_fwd(qs, ks, vs, swa_bias, sink_rows)
    yb, lse_b = _mla_fwd(qm, km, vm)
    late_sent, late_landed = _chips_wait("gather_late_a_wait", *late_a_sems[:4], WHOLE, after=yb)
    wout_g, packed_g = own_slot_filled(late_landed, late_sent)
    full = _unpack_gathered(packed_g, _LATE_PACKED)
    wba, wbb = full["w_branch_a"], full["w_branch_b"]
    wple = full["w_ple"]
    wout = _rows_joined(wout_g).reshape(-1, D_MODEL)
    pa, pb, mixed, o, x1, h2, ya_c, yb_c = _fwd_mix(x2d, ya, yb, gate, wba, wbb, wout, g2, g3, tm_wide)
    late_sent, late_landed = _chips_wait("gather_late_b_wait", *late_b_sems[:4], WHOLE, after=pa)
    natural = dict(zip([n for n in late_names if n != "w_out"], own_slot_filled(late_landed, late_sent)))
    wup = _rows_joined(natural["w_up"])
    wdown, wpg = (_rows_joined(natural[n]).reshape(-1, D_MODEL) for n in ("w_down", "w_ple_gate"))
    up, a = _fwd_up(h2, wup, convw8, convb, tm)
    ff, x2, e, n5, sg, dx3, loss_part = _fwd_out(a, wdown, x1, g4, p2d, wple, g5, wpg, tgt, tm_wide)

    dpre, de, dx2, dff, du, dg5, dg4, dconvb, dconvw8 = _bwd_out(dx3, e, sg, x2, ff, g5, g4, wpg, wdown, up, convw8,
                                                                 convb, tm)
    dup, dx1, do, dpa, dpb, dgates, dya, dyb, delta_b, dg3, dg2, dbg = _bwd_mid(
        du, convw8, wup, dx2, x1, g3, o, g2, wout, gate, pa, pb, wba, wbb, yb_c, tm)
    late_grads = {
        "w_branch_a": _mm_tn("dw_branch_a", ya_c, dpa),
        "w_branch_b": _mm_tn("dw_branch_b", yb_c, dpb),
        "w_out": _mm_tn("dw_out", mixed, do).reshape(4, D_MODEL // 4, D_MODEL),
        "w_up": _mm_tn("dw_up", h2, dup, column_shards=4),
        "w_down": _mm_tn("dw_down", a, dff).reshape(4, D_FF // 4, D_MODEL),
        "w_ple_gate": _mm_tn("dw_ple_gate", n5, dpre).reshape(4, D_MODEL // 4, D_MODEL),
        "w_ple": _mm_tn("dw_ple", p2d, de),
    }

    def grad_views(grads, group, packed_group):
        return [_halves(grads[n]) for n in group if n in _NATURAL] + [_pack_grads(grads, packed_group)]

    def pair_sums(tag, views, theirs):
        return [_add_pair("rs_%s_add_pair_%d" % (tag, i), g, r, cc) for i, (g, r) in enumerate(zip(views, theirs))]

    swap_sems = _chips_start("swap_late_start", grad_views(late_grads, _LATE, _LATE_PACKED), SIBLING_HALF)
    dqs, dks, dvs, dsink_rows = _swa_bwd(qs, ks, vs, ya, dya, lse_a, swa_bias, sink_rows + swap_sems[4][0:1, 0:1])
    dsink = dsink_rows[:, 0:SWA_GROUP, 0]
    late_views, late_theirs = _chips_wait("swap_late_wait", *swap_sems[:4], SIBLING_HALF, after=dqs)
    rs_sems = _chips_start("scatter_late_start", pair_sums("late", late_views, late_theirs), PIECE)
    dqm, dkm, dvm = _mla_bwd(qm, km, vm, dyb, lse_b, delta_b.reshape(HEADS, 1, t_rows) + rs_sems[4][0:1, 0:1])
    dz, dqb, dx, dgq, dgkv, dg1 = _bwd_in(dqs, dks, dvs, dqm, dkm, dvm, tabs, consts, cq, ckv, gq, gkv, wuq, wk, wv,
                                           dgates, win, x2d, g1, dx1, tm)

    small = {"attn_pre_norm": dg1, "attn_post_norm": dg2, "b_gate": dbg, "sinks": dsink, "q_a_norm": dgq,
             "kv_a_norm": dgkv, "mlp_pre_norm": dg3, "mlp_post_norm": dg4, "conv_b": dconvb, "ple_norm": dg5,
             "conv_w": dconvw8[0:3], "loss": loss_part}
    small_sems = _chips_start("gather_small_start", [_pack_small(small)], EVERYONE)
    small_token = small_sems[4]

    dwk = _unpad_slots(_mm_tn("dw_k", ckvn, dkm, after=small_token), HEADS, NOPE_DIM, 1).reshape(
        KV_LORA, HEADS, NOPE_DIM)
    dwv = _unpad_slots(_mm_tn("dw_v", ckvn, dvm, after=small_token), HEADS, V_DIM, 1).reshape(KV_LORA, HEADS, V_DIM)
    early_grads = {
        "w_in": _unpad_w_in(_mm_tn("dw_in", h1, dz, after=small_token)).reshape(D_MODEL, 4, 808).transpose(1, 0, 2),
        "w_uq": _unpad_slots(_mm_tn("dw_uq", cqn, dqb, after=small_token), HEADS, NOPE_DIM + ROPE_DIM, 1),
        "w_ukv": jnp.concatenate([dwk, dwv], axis=2).reshape(KV_LORA, HEADS * (NOPE_DIM + V_DIM)),
    }

    def finish(tag, pairs, landed, group, packed_group):
        reduced = []
        for i, (pair, land) in enumerate(zip(pairs, landed)):
            own = lax.dynamic_index_in_dim(pair, chip, 0, keepdims=True)
            reduced.append(_add_chips("rs_%s_add_chips_%d" % (tag, i),
                                      lax.dynamic_update_slice(land, own, (chip, 0, 0))))
        others = _swap_sibling("swap_%s_reduced_halves" % tag, reduced)
        r, o = reduced[-1], others[-1]
        packed = jnp.where(cc == 0, jnp.stack([r, o]), jnp.stack([o, r]))
        for n, g in _unpack_shard_grads(packed, packed_group).items():
            updates[n] = _adamw("adamw_" + n, w2[n], g, m2[n], v2[n])
        for n, r, o in zip([n for n in group if n in _NATURAL], reduced, others):
            updates[n] = _adamw_halves("adamw_" + n, w2[n], r, o, m2[n], v2[n], cc)

    updates = {}

    def adamw(n, g):
        updates[n] = _adamw("adamw_" + n, w2[n], g, m2[n], v2[n])

    early_views = grad_views(early_grads, _EARLY, _EARLY_PACKED)
    early_theirs = _swap_sibling("swap_early_grad_halves", early_views, other_half=True)
    small_sent, small_landed = _chips_wait("gather_small_wait", *small_sems[:4], EVERYONE, after=early_theirs[0])
    small_all = lax.dynamic_update_slice(small_landed[0], small_sent[0][None], (4 * xc + 2 * yc + cc, 0, 0))
    early_sems = _chips_start("scatter_early_start", pair_sums("early", early_views, early_theirs), PIECE,
                              after=small_all)
    late_pairs, late_landed = _chips_wait("scatter_late_wait", *rs_sems[:4], PIECE, after=early_sems[4])
    finish("late", late_pairs, late_landed, _LATE, _LATE_PACKED)
    early_pairs, early_landed = _chips_wait("scatter_early_wait", *early_sems[:4], PIECE,
                                            after=updates[_LATE[-1]][1])
    finish("early", early_pairs, early_landed, _EARLY, _EARLY_PACKED)

    small_sum = _unpack_small(_add_devices(small_all))
    for n in names:
        if n == "conv_w":
            adamw(n, lax.dynamic_index_in_dim(small_sum[n].reshape(3, 4, 1408), chip, 1, keepdims=False))
        elif n in small_sum:
            adamw(n, small_sum[n].reshape(w2[n].shape))
    loss = small_sum["loss"][0]

    outs = [[updates[n][i].reshape(wts[n].shape) for n in names] for i in range(4)]
    return (loss, dx.reshape(x.shape), *outs[0], *outs[1], *outs[2], *outs[3])
```

```python
import functools
import math

import numpy as np
import jax
import jax.numpy as jnp
from jax import lax
from jax.experimental import pallas as pl
from jax.experimental.pallas import tpu as pltpu

F32 = jnp.float32
BF16 = jnp.bfloat16

D_MODEL = 1024
D_FF = 2816
PLE_DIM = 256
ROPE_THETA = 10000.0
RMS_EPS = 1e-6
SWA_WINDOW = 128
HEADS = 8
A_KV_HEADS = 2
A_HEAD_DIM = 64
Q_LORA = 256
KV_LORA = 128
NOPE_DIM = 64
ROPE_DIM = 32
V_DIM = 64
LANES = 128
ZW = 3328
NEG = -1e30
SCALE_A = A_HEAD_DIM ** -0.5
SCALE_B = (NOPE_DIM + ROPE_DIM) ** -0.5

ADAM_LR = 0.001
ADAM_B1 = 0.9
ADAM_B2 = 0.999
ADAM_EPS = 1e-08
ADAM_WD = 0.01
ADAM_STEP = 10

VMEM_LIMIT = 60 * 1024 * 1024
MESH_AXES = ("x", "y", "c")
MESH = pl.DeviceIdType.MESH

Z_QA, Z_KA, Z_VA, Z_CQ, Z_CKV, Z_KR, Z_GATE = 0, 512, 640, 768, 1024, 1152, 1280


def _dot(a, b):
    return jnp.dot(a, b, preferred_element_type=F32)


def _dot_nt(a, b):
    return lax.dot_general(a, b, (((1,), (1,)), ((), ())), preferred_element_type=F32)


def _dot_tn(a, b):
    return lax.dot_general(a, b, (((0,), (0,)), ((), ())), preferred_element_type=F32)


def _rms_stats(x):
    r = lax.rsqrt(jnp.mean(x * x, axis=-1, keepdims=True) + RMS_EPS)
    return x * r, r


def _rms_bwd(dy, xn, r, g):
    dxn = dy * g
    dx = r * (dxn - xn * jnp.mean(dxn * xn, axis=-1, keepdims=True))
    dg = jnp.sum(dy * xn, axis=0, keepdims=True)
    return dx, dg


def _tile_lanes(t, n):
    return t if n == 1 else jnp.concatenate([t] * n, axis=1)


def _rope(x, c, s1, s2, half):
    w = x.shape[1]
    n = w // LANES
    return (x * _tile_lanes(c, n) + pltpu.roll(x, w - half, 1) * _tile_lanes(s1, n)
            + pltpu.roll(x, half, 1) * _tile_lanes(s2, n))


def _rope_t(dy, c, s1, s2, half):
    w = dy.shape[1]
    n = w // LANES
    return (dy * _tile_lanes(c, n) + pltpu.roll(dy * _tile_lanes(s1, n), half, 1)
            + pltpu.roll(dy * _tile_lanes(s2, n), w - half, 1))


def _fold_slots(d):
    tiles = []
    for j in range(d.shape[1] // (2 * LANES)):
        even = d[:, 2 * j * LANES:(2 * j + 1) * LANES]
        odd = d[:, (2 * j + 1) * LANES:(2 * j + 2) * LANES]
        tiles.append(even + pltpu.roll(odd, A_HEAD_DIM, 1))
    return tiles[0] if len(tiles) == 1 else jnp.concatenate(tiles, axis=1)


def _spread_slots(c):
    low = lax.broadcasted_iota(jnp.int32, (c.shape[0], LANES), 1) < A_HEAD_DIM
    slots = []
    for j in range(c.shape[1] // LANES):
        tile = c[:, j * LANES:(j + 1) * LANES]
        slots += [jnp.where(low, tile, 0.0), jnp.where(low, pltpu.roll(tile, A_HEAD_DIM, 1), 0.0)]
    return jnp.concatenate(slots, axis=1)


def _sigmoid(x):
    return 1.0 / (1.0 + jnp.exp(-x))


_GELU_C = math.sqrt(2.0 / math.pi)


def _gelu_and_grad(x):
    a = _GELU_C + (_GELU_C * 0.044715) * (x * x)
    th = jnp.tanh(x * a)
    hx = 0.5 * x
    p1 = 1.0 + th
    gel = hx * p1
    dgel = 0.5 * p1 + (hx * (1.0 - th * th)) * (3.0 * a - 2.0 * _GELU_C)
    return gel, dgel


def _conv_taps(up, h6, h7):
    r1 = pltpu.roll(up, 1, 0)
    r2 = pltpu.roll(up, 2, 0)
    rows = lax.broadcasted_iota(jnp.int32, (8, up.shape[1]), 0)
    xm1 = jnp.concatenate([jnp.where(rows == 0, h7, r1[0:8]), r1[8:]], axis=0)
    xm2 = jnp.concatenate([jnp.where(rows == 0, h6, jnp.where(rows == 1, h7, r2[0:8])), r2[8:]], axis=0)
    return xm1, xm2


def _conv_taps_next(du, n0, n1):
    tm = du.shape[0]
    r1 = pltpu.roll(du, tm - 1, 0)
    r2 = pltpu.roll(du, tm - 2, 0)
    rows = lax.broadcasted_iota(jnp.int32, (8, du.shape[1]), 0)
    xp1 = jnp.concatenate([r1[:tm - 8], jnp.where(rows == 7, n0, r1[tm - 8:])], axis=0)
    xp2 = jnp.concatenate([r2[:tm - 8], jnp.where(rows == 6, n0, jnp.where(rows == 7, n1, r2[tm - 8:]))], axis=0)
    return xp1, xp2


def _row(tm, n):
    return pl.BlockSpec((tm, n), lambda i: (i, 0))


def _full(shape):
    nd = len(shape)
    return pl.BlockSpec(tuple(shape), lambda i: (0,) * nd)


def _resident(shape):
    nd = len(shape)
    return pl.BlockSpec(tuple(shape), lambda i: (0,) * nd, pipeline_mode=pl.Buffered(1))


def _heads(tm, h):
    return pl.BlockSpec((h, tm, LANES), lambda i: (0, i, 0))


def _rows_call(name, body, t_rows, tm, ins, outs, scratch=()):
    return pl.pallas_call(
        body, name=name, grid=(t_rows // tm,),
        in_specs=[s for _, s in ins],
        out_specs=[s for _, s in outs],
        out_shape=[s for s, _ in outs],
        scratch_shapes=list(scratch),
        compiler_params=pltpu.CompilerParams(dimension_semantics=("arbitrary",), vmem_limit_bytes=VMEM_LIMIT),
    )(*[a for a, _ in ins])


def _sds(shape, dtype):
    return jax.ShapeDtypeStruct(tuple(shape), dtype)


def _rope_consts():
    c = np.zeros((16, LANES), np.float32)
    lane = np.arange(LANES)
    inv_a = (ROPE_THETA ** (-(np.arange(0, A_HEAD_DIM, 2, dtype=np.float32) / A_HEAD_DIM))).astype(np.float32)
    in_a = lane < A_HEAD_DIM
    c[0, in_a] = inv_a[lane[in_a] % (A_HEAD_DIM // 2)]
    c[1, in_a] = 1.0
    c[2, lane < A_HEAD_DIM // 2] = -1.0
    c[3, (lane >= A_HEAD_DIM // 2) & in_a] = 1.0
    inv_b = (ROPE_THETA ** (-(np.arange(0, ROPE_DIM, 2, dtype=np.float32) / ROPE_DIM))).astype(np.float32)
    pe = (lane >= NOPE_DIM) & (lane < NOPE_DIM + ROPE_DIM)
    c[5, pe] = inv_b[(lane[pe] - NOPE_DIM) % (ROPE_DIM // 2)]
    c[6, pe] = 1.0
    c[7, (lane >= NOPE_DIM) & (lane < NOPE_DIM + ROPE_DIM // 2)] = -1.0
    c[8, (lane >= NOPE_DIM + ROPE_DIM // 2) & (lane < NOPE_DIM + ROPE_DIM)] = 1.0
    c[9, lane < NOPE_DIM] = 1.0
    c[10, pe] = 1.0
    return jnp.asarray(c)


def _rope_tables(pos_f, consts, tm):
    t_rows = pos_f.shape[0]

    def body(pos_ref, c_ref, ca, sa1, sa2, cb, sb1, sb2):
        ang = pos_ref[...] * (c_ref[0:1, :] + c_ref[5:6, :])
        cs, sn = jnp.cos(ang), jnp.sin(ang)
        for ref, row in ((ca, 1), (sa1, 2), (sa2, 3)):
            half = (cs if row == 1 else sn) * c_ref[row:row + 1, :]
            ref[...] = half + pltpu.roll(half, A_HEAD_DIM, 1)
        cb[...] = cs * c_ref[6:7, :] + c_ref[9:10, :]
        sb1[...] = sn * c_ref[7:8, :]
        sb2[...] = sn * c_ref[8:9, :]

    tab = (_sds((t_rows, LANES), F32), _row(tm, LANES))
    return _rows_call("rope_tables", body, t_rows, tm,
                      [(pos_f, _row(tm, 1)), (consts, _full(consts.shape))], [tab] * 6)


def _fwd_in(x, g1, win, bg, gq, gkv, wuq, wk, wv, eq, ek, tabs, tm):
    t_rows = x.shape[0]

    def body(x_ref, g1_ref, win_ref, bg_ref, gq_ref, gkv_ref, wuq_ref, wk_ref, wv_ref, eq_ref, ek_ref,
             ca, sa1, sa2, cb, sb1, sb2,
             h1_ref, qs_ref, ks_ref, vs_ref, cq_ref, cqn_ref, ckv_ref, ckvn_ref, qm_ref, km_ref, vm_ref, gate_ref):
        xn, _ = _rms_stats(x_ref[...])
        hb = (xn * g1_ref[...]).astype(BF16)
        h1_ref[...] = hb
        ta = (ca[...], sa1[...], sa2[...])
        tb = (cb[...], sb1[...], sb2[...])
        cq = _dot(hb, win_ref[:, Z_CQ:Z_CKV])
        ckv = _dot(hb, win_ref[:, Z_CKV:Z_KR])
        z_qa = _dot(hb, win_ref[:, Z_QA:Z_KA])
        z_ka = _dot(hb, win_ref[:, Z_KA:Z_VA])
        z_va = _dot(hb, win_ref[:, Z_VA:Z_CQ])
        z_kr = _dot(hb, win_ref[:, Z_KR:Z_GATE])
        cq_ref[...] = cq
        cqn, _ = _rms_stats(cq)
        cqb = (cqn * gq_ref[...]).astype(BF16)
        cqn_ref[...] = cqb
        ckv_ref[...] = ckv
        ckvn, _ = _rms_stats(ckv)
        ckvb = (ckvn * gkv_ref[...]).astype(BF16)
        ckvn_ref[...] = ckvb
        z_qm = _dot(cqb, wuq_ref[...])
        z_km = _dot(ckvb, wk_ref[...])
        z_vm = _dot(ckvb, wv_ref[...])
        z_gate = _dot(hb, win_ref[:, Z_GATE:ZW])
        qs_ref[...] = _dot((_rope(z_qa, *ta, A_HEAD_DIM // 2) * SCALE_A).astype(BF16), eq_ref[...]).astype(BF16)
        ks_ref[...] = _dot(_rope(z_ka, *ta, A_HEAD_DIM // 2).astype(BF16), ek_ref[...]).astype(BF16)
        vs_ref[...] = _dot(z_va.astype(BF16), ek_ref[...]).astype(BF16)
        qm_ref[...] = (_rope(z_qm, *tb, ROPE_DIM // 2) * SCALE_B).astype(BF16)
        km_ref[...] = (z_km + _tile_lanes(_rope(z_kr, *tb, ROPE_DIM // 2), HEADS)).astype(BF16)
        vm_ref[...] = z_vm.astype(BF16)
        gate_ref[...] = _sigmoid(z_gate + bg_ref[...]).astype(BF16)

    def o(n, dt):
        return (_sds((t_rows, n), dt), _row(tm, n))

    ins = [(x, _row(tm, D_MODEL)), (g1, _full(g1.shape)), (win, _resident(win.shape)), (bg, _full(bg.shape)),
           (gq, _full(gq.shape)), (gkv, _full(gkv.shape)), (wuq, _full(wuq.shape)), (wk, _full(wk.shape)),
           (wv, _full(wv.shape)), (eq, _full(eq.shape)), (ek, _full(ek.shape))] + [(t, _row(tm, LANES)) for t in tabs]
    outs = [o(1024, BF16), o(1024, BF16), o(256, BF16), o(256, BF16), o(256, F32), o(256, BF16), o(128, F32),
            o(128, BF16), o(1024, BF16), o(1024, BF16), o(1024, BF16), o(2048, BF16)]
    return _rows_call("fwd_in", body, t_rows, tm, ins, outs)


def _attn_tile(t_rows):
    return min(512, t_rows)


MLA_HEADS_PER_STEP = 4
MLA_FWD_HEADS_PER_STEP = 8


def _causal_pairs(nq, by_kv):
    if by_kv:
        pairs = [(i, j) for j in range(nq) for i in range(j, nq)]
    else:
        pairs = [(i, j) for i in range(nq) for j in range(i + 1)]
    return (jnp.asarray([p[0] for p in pairs], jnp.int32), jnp.asarray([p[1] for p in pairs], jnp.int32))


def _mla_fwd(q, k, v):
    t_rows = q.shape[0]
    t = _attn_tile(t_rows)
    hp = MLA_FWD_HEADS_PER_STEP
    w = hp * LANES
    ii, jj = _causal_pairs(t_rows // t, by_kv=False)

    def body(i_ref, j_ref, q_ref, k_ref, v_ref, o_ref, lse_ref, m_s, l_s, acc_s):
        i = i_ref[pl.program_id(1)]
        j = j_ref[pl.program_id(1)]

        @pl.when(j == 0)
        def _():
            m_s[...] = jnp.full(m_s.shape, NEG, F32)
            l_s[...] = jnp.zeros(l_s.shape, F32)
            acc_s[...] = jnp.zeros(acc_s.shape, F32)

        def step(diagonal):
            sls = [slice(hh * LANES, (hh + 1) * LANES) for hh in range(hp)]
            scores = [_dot_nt(k_ref[:, sl], q_ref[:, sl]) for sl in sls]
            if diagonal:
                valid = (lax.broadcasted_iota(jnp.int32, (t, t), 0) <= lax.broadcasted_iota(jnp.int32, (t, t), 1))
                scores = [jnp.where(valid, s, NEG) for s in scores]
            stats = []
            for hh, s in enumerate(scores):
                m_prev = m_s[hh]
                m_new = jnp.maximum(m_prev, jnp.max(s, axis=0, keepdims=True))
                p = jnp.exp(s - m_new)
                alpha = jnp.exp(m_prev - m_new)
                stats.append((m_new, alpha, alpha * l_s[hh] + jnp.sum(p, axis=0, keepdims=True), p.astype(BF16)))
            for hh, (m_new, alpha, l_new, p) in enumerate(stats):
                sl = sls[hh]
                acc = alpha * acc_s[hh] + _dot_tn(v_ref[:, sl], p)
                if diagonal:
                    o_ref[:, sl] = (acc / l_new).T.astype(o_ref.dtype)
                    lse_ref[hh] = m_new + jnp.log(l_new)
                else:
                    m_s[hh] = m_new
                    l_s[hh] = l_new
                    acc_s[hh] = acc

        pl.when(j < i)(lambda: step(False))
        pl.when(j == i)(lambda: step(True))

    grid_spec = pltpu.PrefetchScalarGridSpec(
        num_scalar_prefetch=2, grid=(HEADS // hp, ii.shape[0]),
        in_specs=[pl.BlockSpec((t, w), lambda hb, s, ir, jr: (ir[s], hb)),
                  pl.BlockSpec((t, w), lambda hb, s, ir, jr: (jr[s], hb)),
                  pl.BlockSpec((t, w), lambda hb, s, ir, jr: (jr[s], hb))],
        out_specs=[pl.BlockSpec((t, w), lambda hb, s, ir, jr: (ir[s], hb)),
                   pl.BlockSpec((hp, 1, t), lambda hb, s, ir, jr: (hb, 0, ir[s]))],
        scratch_shapes=[pltpu.VMEM((hp, 1, t), F32), pltpu.VMEM((hp, 1, t), F32), pltpu.VMEM((hp, LANES, t), F32)])
    return pl.pallas_call(
        body, name="mla_fwd", grid_spec=grid_spec,
        out_shape=[_sds((t_rows, HEADS * LANES), BF16), _sds((HEADS, 1, t_rows), F32)],
        compiler_params=pltpu.CompilerParams(dimension_semantics=("arbitrary",) * 2, vmem_limit_bytes=VMEM_LIMIT),
    )(ii, jj, q, k, v)


def _mla_bwd(q, k, v, do, lse, delta):
    t_rows = q.shape[0]
    t = _attn_tile(t_rows)
    hp = MLA_HEADS_PER_STEP
    w = hp * LANES
    ii, jj = _causal_pairs(t_rows // t, by_kv=True)

    def body(i_ref, j_ref, q_ref, k_ref, v_ref, do_ref, lse_ref, dl_ref, dq_ref, dk_ref, dv_ref):
        i = i_ref[pl.program_id(1)]
        j = j_ref[pl.program_id(1)]

        @pl.when(pl.program_id(1) == 0)
        def _():
            dq_ref[...] = jnp.zeros(dq_ref.shape, F32)

        def step(diagonal):
            r0 = pl.multiple_of(i * t, t)
            sls = [slice(hh * LANES, (hh + 1) * LANES) for hh in range(hp)]
            scores = [_dot_nt(k_ref[:, sl], q_ref[:, sl]) for sl in sls]
            if diagonal:
                valid = (lax.broadcasted_iota(jnp.int32, (t, t), 0) <= lax.broadcasted_iota(jnp.int32, (t, t), 1))
                scores = [jnp.where(valid, s, NEG) for s in scores]
            dps = [_dot_nt(v_ref[:, sl], do_ref[:, sl]) for sl in sls]
            ps = [jnp.exp(s - lse_ref[hh]) for hh, s in enumerate(scores)]
            dss = [(p * (dp - dl_ref[hh])).astype(BF16) for hh, (p, dp) in enumerate(zip(ps, dps))]
            for hh, sl in enumerate(sls):
                dv = _dot(ps[hh].astype(BF16), do_ref[:, sl])
                dk = _dot(dss[hh], q_ref[:, sl])
                if diagonal:
                    dv_ref[:, sl] = dv
                    dk_ref[:, sl] = dk
                else:
                    dv_ref[:, sl] += dv
                    dk_ref[:, sl] += dk
                dq_ref[hh, pl.ds(r0, t), :] += _dot_tn(dss[hh], k_ref[:, sl])

        pl.when(i > j)(lambda: step(False))
        pl.when(i == j)(lambda: step(True))

    def qmap(hb, s, ir, jr):
        return (ir[s], hb)

    def kvmap(hb, s, ir, jr):
        return (jr[s], hb)

    def rowmap(hb, s, ir, jr):
        return (hb, 0, ir[s])

    grid_spec = pltpu.PrefetchScalarGridSpec(
        num_scalar_prefetch=2, grid=(HEADS // hp, ii.shape[0]),
        in_specs=[pl.BlockSpec((t, w), qmap), pl.BlockSpec((t, w), kvmap), pl.BlockSpec((t, w), kvmap),
                  pl.BlockSpec((t, w), qmap), pl.BlockSpec((hp, 1, t), rowmap), pl.BlockSpec((hp, 1, t), rowmap)],
        out_specs=[pl.BlockSpec((hp, t_rows, LANES), lambda hb, s, ir, jr: (hb, 0, 0)),
                   pl.BlockSpec((t, w), kvmap), pl.BlockSpec((t, w), kvmap)])
    return pl.pallas_call(
        body, name="mla_bwd", grid_spec=grid_spec,
        out_shape=[_sds((HEADS, t_rows, LANES), F32), _sds((t_rows, HEADS * LANES), F32),
                   _sds((t_rows, HEADS * LANES), F32)],
        compiler_params=pltpu.CompilerParams(dimension_semantics=("arbitrary",) * 2, vmem_limit_bytes=VMEM_LIMIT),
    )(ii, jj, q, k, v, do, lse, delta)


SWA_TILE = 2 * SWA_WINDOW
SWA_GROUP = HEADS // A_KV_HEADS


def _swa_bias(tq):
    koff = lax.broadcasted_iota(jnp.int32, (tq + SWA_WINDOW, SWA_GROUP * tq), 0) - SWA_WINDOW
    qoff = (lax.broadcasted_iota(jnp.int32, (tq + SWA_WINDOW, SWA_GROUP * tq), 1) % tq)
    band = (koff <= qoff) & (qoff - koff < SWA_WINDOW)
    return jnp.stack([jnp.where(band & (koff >= 0), 0.0, NEG), jnp.where(band, 0.0, NEG)]).astype(F32)


def _swa_specs(tq, nq):
    wb = tq // SWA_WINDOW
    kvw = A_KV_HEADS * LANES

    def qi(i):
        return jnp.minimum(i, nq - 1)

    q = pl.BlockSpec((tq, HEADS * LANES), lambda i: (qi(i), 0))
    cur = pl.BlockSpec((tq, kvw), lambda i: (qi(i), 0))
    prev = pl.BlockSpec((SWA_WINDOW, kvw), lambda i: (jnp.maximum(qi(i) * wb - 1, 0), 0))
    bias = pl.BlockSpec((1, tq + SWA_WINDOW, SWA_GROUP * tq), lambda i: (jnp.minimum(i, 1), 0, 0))
    rows = pl.BlockSpec((A_KV_HEADS, 1, 1, SWA_GROUP * tq), lambda i: (0, qi(i), 0, 0))
    sink = pl.BlockSpec((A_KV_HEADS, 1, SWA_GROUP * tq), lambda i: (0, 0, 0))
    return q, cur, prev, bias, rows, sink


def _stack_heads(ref, kvh):
    base = kvh * SWA_GROUP
    return jnp.concatenate([ref[:, (base + g) * LANES:(base + g + 1) * LANES] for g in range(SWA_GROUP)], axis=0)


def _unstack_heads(ref, kvh, val, tq):
    base = kvh * SWA_GROUP
    for g in range(SWA_GROUP):
        ref[:, (base + g) * LANES:(base + g + 1) * LANES] = val[g * tq:(g + 1) * tq].astype(ref.dtype)


def _kv_window(prev_ref, cur_ref, kvh):
    sl = slice(kvh * LANES, (kvh + 1) * LANES)
    return jnp.concatenate([prev_ref[:, sl], cur_ref[:, sl]], axis=0)


def _swa_fwd(q, k, v, bias, sink_rows):
    t_rows = q.shape[0]
    tq = min(SWA_TILE, t_rows)
    nq = t_rows // tq
    qs_, cur, prev, bs, rows, sk = _swa_specs(tq, nq)
    kvhs = range(A_KV_HEADS)

    def body(q_ref, kc_ref, kp_ref, vc_ref, vp_ref, b_ref, sink_ref, o_ref, lse_ref):
        scores = [_dot_nt(_kv_window(kp_ref, kc_ref, h), _stack_heads(q_ref, h)) + b_ref[0] for h in kvhs]
        stats = []
        for h, s in zip(kvhs, scores):
            sink = sink_ref[h]
            m = jnp.maximum(jnp.max(s, axis=0, keepdims=True), sink)
            p = jnp.exp(s - m)
            l = jnp.sum(p, axis=0, keepdims=True) + jnp.exp(sink - m)
            lse_ref[h, 0] = m + jnp.log(l)
            stats.append((p.astype(BF16), l))
        for h, (p, l) in zip(kvhs, stats):
            _unstack_heads(o_ref, h, (_dot_tn(_kv_window(vp_ref, vc_ref, h), p) / l).T, tq)

    return pl.pallas_call(
        body, name="swa_fwd", grid=(nq,),
        in_specs=[qs_, cur, prev, cur, prev, bs, sk],
        out_specs=[qs_, rows],
        out_shape=[_sds((t_rows, HEADS * LANES), BF16), _sds((A_KV_HEADS, nq, 1, SWA_GROUP * tq), F32)],
        compiler_params=pltpu.CompilerParams(dimension_semantics=("arbitrary",), vmem_limit_bytes=VMEM_LIMIT),
    )(q, k, k, v, v, bias, sink_rows)


def _swa_bwd(q, k, v, o, do, lse, bias, sink_rows):
    t_rows = q.shape[0]
    tq = min(SWA_TILE, t_rows)
    nq = t_rows // tq
    qs_, cur, prev, bs, rows, sk = _swa_specs(tq, nq)
    hw = SWA_WINDOW
    kvhs = range(A_KV_HEADS)
    kvw = A_KV_HEADS * LANES

    def body(q_ref, kc_ref, kp_ref, vc_ref, vp_ref, o_ref, do_ref, lse_ref, b_ref, sink_ref,
             dq_ref, dk_ref, dv_ref, dsink_ref, ck, cv, dsa):
        i = pl.program_id(0)

        @pl.when(i == 0)
        def _():
            dsa[...] = jnp.zeros(dsa.shape, F32)

        @pl.when(i < nq)
        def _():
            qs = [_stack_heads(q_ref, h) for h in kvhs]
            dos = [_stack_heads(do_ref, h) for h in kvhs]
            kks = [_kv_window(kp_ref, kc_ref, h) for h in kvhs]
            scores = [_dot_nt(kks[h], qs[h]) for h in kvhs]
            dps = [_dot_nt(_kv_window(vp_ref, vc_ref, h), dos[h]) for h in kvhs]
            ps, dss = [], []
            for h in kvhs:
                lse = lse_ref[h, 0]
                p = jnp.exp(scores[h] + b_ref[0] - lse)
                delta = jnp.sum((_stack_heads(o_ref, h).astype(F32) * dos[h].astype(F32)).T, axis=0, keepdims=True)
                dsa[h] += -jnp.exp(sink_ref[h] - lse) * delta
                ps.append(p.astype(BF16))
                dss.append((p * (dps[h] - delta)).astype(BF16))
            for h in kvhs:
                sl = slice(h * LANES, (h + 1) * LANES)
                dv = _dot(ps[h], dos[h])
                dk = _dot(dss[h], qs[h])
                _unstack_heads(dq_ref, h, _dot_tn(dss[h], kks[h]), tq)

                @pl.when(i > 0)
                def _():
                    dk_ref[0:tq - hw, sl] = ck[0:tq - hw, sl]
                    dk_ref[tq - hw:tq, sl] = ck[tq - hw:tq, sl] + dk[0:hw]
                    dv_ref[0:tq - hw, sl] = cv[0:tq - hw, sl]
                    dv_ref[tq - hw:tq, sl] = cv[tq - hw:tq, sl] + dv[0:hw]

                ck[:, sl] = dk[hw:hw + tq]
                cv[:, sl] = dv[hw:hw + tq]

        @pl.when(i == nq)
        def _():
            dk_ref[...] = ck[...]
            dv_ref[...] = cv[...]
            dsink_ref[...] = jnp.zeros(dsink_ref.shape, F32)
            for h in kvhs:
                for g in range(SWA_GROUP):
                    tot = jnp.sum(dsa[h, :, g * tq:(g + 1) * tq], axis=1, keepdims=True)
                    dsink_ref[h, g:g + 1, :] = jnp.zeros((1, LANES), F32) + tot

    kv_out = pl.BlockSpec((tq, kvw), lambda i: (jnp.maximum(i - 1, 0), 0))
    return pl.pallas_call(
        body, name="swa_bwd", grid=(nq + 1,),
        in_specs=[qs_, cur, prev, cur, prev, qs_, qs_, rows, bs, sk],
        out_specs=[qs_, kv_out, kv_out, pl.BlockSpec((A_KV_HEADS, 8, LANES), lambda i: (0, 0, 0))],
        out_shape=[_sds((t_rows, HEADS * LANES), F32), _sds((t_rows, kvw), F32), _sds((t_rows, kvw), F32),
                   _sds((A_KV_HEADS, 8, LANES), F32)],
        scratch_shapes=[pltpu.VMEM((tq, kvw), F32), pltpu.VMEM((tq, kvw), F32),
                        pltpu.VMEM((A_KV_HEADS, 1, SWA_GROUP * tq), F32)],
        compiler_params=pltpu.CompilerParams(dimension_semantics=("arbitrary",), vmem_limit_bytes=VMEM_LIMIT),
    )(q, k, k, v, v, o, do, lse, bias, sink_rows)


def _fwd_mix(x, ya, yb, gate, wba, wbb, wout, g2, g3, tm):
    t_rows = x.shape[0]

    def body(x_ref, ya_ref, yb_ref, gate_ref, wba_ref, wbb_ref, wout_ref, g2_ref, g3_ref,
             pa_ref, pb_ref, mixed_ref, o_ref, x1_ref, h2_ref, yac_ref, ybc_ref):
        yac = _fold_slots(ya_ref[...].astype(F32)).astype(BF16)
        ybc = _fold_slots(yb_ref[...].astype(F32)).astype(BF16)
        yac_ref[...] = yac
        ybc_ref[...] = ybc
        pa = _dot(yac, wba_ref[...])
        pb = _dot(ybc, wbb_ref[...])
        pa_ref[...] = pa.astype(BF16)
        pb_ref[...] = pb.astype(BF16)
        mixed = (gate_ref[:, 0:D_MODEL].astype(F32) * pa
                 + gate_ref[:, D_MODEL:2 * D_MODEL].astype(F32) * pb).astype(BF16)
        mixed_ref[...] = mixed
        o = _dot(mixed, wout_ref[...])
        o_ref[...] = o
        on, _ = _rms_stats(o)
        x1 = x_ref[...] + on * g2_ref[...]
        x1_ref[...] = x1
        x1n, _ = _rms_stats(x1)
        h2_ref[...] = (x1n * g3_ref[...]).astype(BF16)

    def o_(dt):
        return (_sds((t_rows, D_MODEL), dt), _row(tm, D_MODEL))

    ins = [(x, _row(tm, D_MODEL)), (ya, _row(tm, 1024)), (yb, _row(tm, 1024)), (gate, _row(tm, 2048)),
           (wba, _resident(wba.shape)), (wbb, _resident(wbb.shape)), (wout, _resident(wout.shape)),
           (g2, _full(g2.shape)), (g3, _full(g3.shape))]
    half = (_sds((t_rows, D_MODEL // 2), BF16), _row(tm, D_MODEL // 2))
    return _rows_call("fwd_mix", body, t_rows, tm, ins,
                      [o_(BF16), o_(BF16), o_(BF16), o_(F32), o_(F32), o_(BF16), half, half])


CONV_CHUNK = 1408


def _fwd_up(h2, wup, convw8, convb, tm):
    t_rows = h2.shape[0]
    cdim = 2 * D_FF

    def body(h2_ref, wup_ref, cw_ref, cb_ref, up_ref, a_ref, carry):
        i = pl.program_id(0)

        @pl.when(i == 0)
        def _():
            carry[...] = jnp.zeros(carry.shape, F32)

        hb = h2_ref[...]
        ups = [_dot(hb, wup_ref[s]) for s in range(cdim // CONV_CHUNK)]

        def conv(c0):
            sl = slice(c0, c0 + CONV_CHUNK)
            up = ups[c0 // CONV_CHUNK]
            up_ref[:, sl] = up
            xm1, xm2 = _conv_taps(up, carry[6:7, sl], carry[7:8, sl])
            u = cw_ref[0:1, sl] * xm2 + cw_ref[1:2, sl] * xm1 + cw_ref[2:3, sl] * up + cb_ref[:, sl]
            carry[:, sl] = up[tm - 8:tm, :]
            return u

        for c0 in range(0, D_FF, CONV_CHUNK):
            ug = conv(c0)
            uv = conv(D_FF + c0)
            gel, _ = _gelu_and_grad(ug)
            a_ref[:, c0:c0 + CONV_CHUNK] = (gel * uv).astype(BF16)

    ins = [(h2, _row(tm, D_MODEL)), (wup, _resident(wup.shape)), (convw8, _full(convw8.shape)), (convb, _full(convb.shape))]
    outs = [(_sds((t_rows, cdim), F32), _row(tm, cdim)), (_sds((t_rows, D_FF), BF16), _row(tm, D_FF))]
    return _rows_call("fwd_up", body, t_rows, tm, ins, outs, scratch=[pltpu.VMEM((8, cdim), F32)])


def _fwd_out(a, wdown, x1, g4, p, wple, g5, wpg, tgt, tm):
    t_rows = a.shape[0]

    def body(a_ref, wdown_ref, x1_ref, g4_ref, p_ref, wple_ref, g5_ref, wpg_ref, tgt_ref,
             ff_ref, x2_ref, e_ref, n5_ref, sg_ref, dx3_ref, loss_ref):
        i = pl.program_id(0)
        ff = _dot(a_ref[...], wdown_ref[...])
        e = _dot(p_ref[...].astype(BF16), wple_ref[...])
        ff_ref[...] = ff
        ffn, _ = _rms_stats(ff)
        x2 = x1_ref[...] + ffn * g4_ref[...]
        x2_ref[...] = x2
        e_ref[...] = e.astype(BF16)
        x2n, _ = _rms_stats(x2)
        n5 = (x2n * g5_ref[...]).astype(BF16)
        n5_ref[...] = n5
        sg = _sigmoid(_dot(n5, wpg_ref[...]))
        sg_ref[...] = sg.astype(BF16)
        d = x2 + sg * e - tgt_ref[...]
        dx3_ref[...] = d * (1.0 / D_MODEL)

        @pl.when(i == 0)
        def _():
            loss_ref[...] = jnp.zeros((1, 1), F32)

        loss_ref[...] += 0.5 * jnp.sum(jnp.sum(d * d, axis=1, keepdims=True), axis=0, keepdims=True) * (1.0 / D_MODEL)

    def o_(dt):
        return (_sds((t_rows, D_MODEL), dt), _row(tm, D_MODEL))

    ins = [(a, _row(tm, D_FF)), (wdown, _resident(wdown.shape)), (x1, _row(tm, D_MODEL)), (g4, _full(g4.shape)),
           (p, _row(tm, PLE_DIM)), (wple, _full(wple.shape)), (g5, _full(g5.shape)), (wpg, _resident(wpg.shape)),
           (tgt, _row(tm, D_MODEL))]
    outs = [o_(F32), o_(F32), o_(BF16), o_(BF16), o_(BF16), o_(F32), (_sds((1, 1), F32), _full((1, 1)))]
    return _rows_call("fwd_out", body, t_rows, tm, ins, outs)


def _bwd_out(dx3, e, sg, x2, ff, g5, g4, wpg, wdown, up, convw8, convb, tm):
    t_rows = dx3.shape[0]
    cdim = 2 * D_FF
    hb = tm // 8

    def body(dx3_ref, e_ref, sg_ref, x2_ref, ff_ref, g5_ref, g4_ref, wpg_ref, wdown_ref, up_ref, halo_ref, cw_ref,
             cb_ref, dpre_ref, de_ref, dx2_ref, dff_ref, du_ref, dg5_ref, dg4_ref, dcb_ref, dcw_ref):
        i = pl.program_id(0)

        @pl.when(i == 0)
        def _():
            dg5_ref[...] = jnp.zeros(dg5_ref.shape, F32)
            dg4_ref[...] = jnp.zeros(dg4_ref.shape, F32)
            dcb_ref[...] = jnp.zeros(dcb_ref.shape, F32)
            dcw_ref[...] = jnp.zeros(dcw_ref.shape, F32)

        dx3 = dx3_ref[...]
        sg = sg_ref[...].astype(F32)
        dpre = (dx3 * e_ref[...].astype(F32) * sg * (1.0 - sg)).astype(BF16)
        dpre_ref[...] = dpre
        de_ref[...] = (dx3 * sg).astype(BF16)
        dn5 = _dot_nt(dpre, wpg_ref[...])
        x2n, r5 = _rms_stats(x2_ref[...])
        d2, dg5 = _rms_bwd(dn5, x2n, r5, g5_ref[...])
        dx2 = dx3 + d2
        dx2_ref[...] = dx2
        dg5_ref[...] += dg5
        ffn, r4 = _rms_stats(ff_ref[...])
        dff, dg4 = _rms_bwd(dx2, ffn, r4, g4_ref[...])
        dg4_ref[...] += dg4
        dffb = dff.astype(BF16)
        dff_ref[...] = dffb
        keep = jnp.where(i > 0, 1.0, 0.0)

        def conv(c0):
            sl = slice(c0, c0 + CONV_CHUNK)
            up = up_ref[:, sl]
            xm1, xm2 = _conv_taps(up, halo_ref[6:7, sl] * keep, halo_ref[7:8, sl] * keep)
            u = cw_ref[0:1, sl] * xm2 + cw_ref[1:2, sl] * xm1 + cw_ref[2:3, sl] * up + cb_ref[:, sl]
            return u, up, xm1, xm2

        def grads(c0, du, up, xm1, xm2):
            sl = slice(c0, c0 + CONV_CHUNK)
            du_ref[:, sl] = du.astype(BF16)
            dcb_ref[:, sl] += jnp.sum(du, axis=0, keepdims=True)
            dcw_ref[0:1, sl] += jnp.sum(du * xm2, axis=0, keepdims=True)
            dcw_ref[1:2, sl] += jnp.sum(du * xm1, axis=0, keepdims=True)
            dcw_ref[2:3, sl] += jnp.sum(du * up, axis=0, keepdims=True)

        for c0 in range(0, D_FF, CONV_CHUNK):
            da = _dot_nt(dffb, wdown_ref[c0:c0 + CONV_CHUNK, :])
            ug, *rg = conv(c0)
            uv, *rv = conv(D_FF + c0)
            gel, dgel = _gelu_and_grad(ug)
            grads(c0, da * uv * dgel, *rg)
            grads(D_FF + c0, da * gel, *rv)

    def o_(n, dt):
        return (_sds((t_rows, n), dt), _row(tm, n))

    def acc(r, n):
        return (_sds((r, n), F32), _full((r, n)))

    halo = pl.BlockSpec((8, cdim), lambda i: (jnp.maximum(i * hb - 1, 0), 0))
    ins = [(dx3, _row(tm, D_MODEL)), (e, _row(tm, D_MODEL)), (sg, _row(tm, D_MODEL)), (x2, _row(tm, D_MODEL)),
           (ff, _row(tm, D_MODEL)), (g5, _full(g5.shape)), (g4, _full(g4.shape)), (wpg, _resident(wpg.shape)),
           (wdown, _resident(wdown.shape)), (up, _row(tm, cdim)), (up, halo), (convw8, _full(convw8.shape)),
           (convb, _full(convb.shape))]
    outs = [o_(D_MODEL, BF16), o_(D_MODEL, BF16), o_(D_MODEL, F32), o_(D_MODEL, BF16), o_(cdim, BF16),
            acc(1, D_MODEL), acc(1, D_MODEL), acc(1, cdim), acc(8, cdim)]
    return _rows_call("bwd_out", body, t_rows, tm, ins, outs)


def _bwd_mid(du, convw8, wup, dx2, x1, g3, o, g2, wout, gate, pa, pb, wba, wbb, yb, tm):
    t_rows = du.shape[0]
    cdim = 2 * D_FF
    halo_rows = 16
    hb = tm // halo_rows
    last_blk = t_rows // halo_rows - 1
    n_tiles = t_rows // tm

    def body(du_ref, halo_ref, cw_ref, wup_ref, dx2_ref, x1_ref, g3_ref, o_ref, g2_ref, wout_ref, gate_ref, pa_ref,
             pb_ref, wba_ref, wbb_ref, yb_ref,
             dup_ref, dx1_ref, do_ref, dpa_ref, dpb_ref, dgt_ref, dya_ref, dyb_ref, dl_ref, dg3_ref, dg2_ref, dbg_ref):
        i = pl.program_id(0)

        @pl.when(i == 0)
        def _():
            dg3_ref[...] = jnp.zeros(dg3_ref.shape, F32)
            dg2_ref[...] = jnp.zeros(dg2_ref.shape, F32)
            dbg_ref[...] = jnp.zeros(dbg_ref.shape, F32)

        keep = jnp.where(i < n_tiles - 1, 1.0, 0.0)
        dh2 = jnp.zeros((tm, D_MODEL), F32)
        dups = []
        for c0 in range(0, cdim, CONV_CHUNK):
            sl = slice(c0, c0 + CONV_CHUNK)
            du = du_ref[:, sl].astype(F32)
            nxt = halo_ref[:, sl].astype(F32)
            xp1, xp2 = _conv_taps_next(du, nxt[0:1] * keep, nxt[1:2] * keep)
            dups.append((cw_ref[2:3, sl] * du + cw_ref[1:2, sl] * xp1 + cw_ref[0:1, sl] * xp2).astype(BF16))
            dup_ref[:, sl] = dups[-1]
            if len(dups) > 1:
                dh2 = dh2 + _dot_nt(dups[-2], wup_ref[len(dups) - 2])
        dh2 = dh2 + _dot_nt(dups[-1], wup_ref[len(dups) - 1])
        x1n, r3 = _rms_stats(x1_ref[...])
        d1, dg3 = _rms_bwd(dh2, x1n, r3, g3_ref[...])
        dx1 = dx2_ref[...] + d1
        dx1_ref[...] = dx1
        dg3_ref[...] += dg3
        on, r2 = _rms_stats(o_ref[...])
        do, dg2 = _rms_bwd(dx1, on, r2, g2_ref[...])
        dg2_ref[...] += dg2
        dob = do.astype(BF16)
        do_ref[...] = dob
        dmixed = _dot_nt(dob, wout_ref[...])
        ga = gate_ref[:, 0:D_MODEL].astype(F32)
        gb = gate_ref[:, D_MODEL:2 * D_MODEL].astype(F32)
        dpa = (dmixed * ga).astype(BF16)
        dpb = (dmixed * gb).astype(BF16)
        dpa_ref[...] = dpa
        dpb_ref[...] = dpb
        dga = dmixed * pa_ref[...].astype(F32) * ga * (1.0 - ga)
        dgb = dmixed * pb_ref[...].astype(F32) * gb * (1.0 - gb)
        dgt_ref[:, 0:D_MODEL] = dga.astype(BF16)
        dgt_ref[:, D_MODEL:2 * D_MODEL] = dgb.astype(BF16)
        dbg_ref[:, 0:D_MODEL] += jnp.sum(dga, axis=0, keepdims=True)
        dbg_ref[:, D_MODEL:2 * D_MODEL] += jnp.sum(dgb, axis=0, keepdims=True)
        dya_ref[...] = _spread_slots(_dot_nt(dpa, wba_ref[...])).astype(BF16)
        dyb = _dot_nt(dpb, wbb_ref[...]).astype(BF16)
        dyb_ref[...] = _spread_slots(dyb.astype(F32)).astype(BF16)
        prod = yb_ref[...].astype(F32) * dyb.astype(F32)
        width = HEADS * V_DIM
        lane_head = lax.broadcasted_iota(jnp.int32, (HEADS, width), 1) // V_DIM
        sel = (lane_head == lax.broadcasted_iota(jnp.int32, (HEADS, width), 0)).astype(BF16)
        hi = prod.astype(BF16)
        lo = (prod - hi.astype(F32)).astype(BF16)
        dl_ref[...] = _dot_nt(sel, hi) + _dot_nt(sel, lo)

    def o_(n, dt):
        return (_sds((t_rows, n), dt), _row(tm, n))

    def acc(r, n):
        return (_sds((r, n), F32), _full((r, n)))

    halo = pl.BlockSpec((halo_rows, cdim), lambda i: (jnp.minimum((i + 1) * hb, last_blk), 0))
    ins = [(du, _row(tm, cdim)), (du, halo), (convw8, _full(convw8.shape)), (wup, _resident(wup.shape)),
           (dx2, _row(tm, D_MODEL)), (x1, _row(tm, D_MODEL)), (g3, _full(g3.shape)), (o, _row(tm, D_MODEL)),
           (g2, _full(g2.shape)), (wout, _resident(wout.shape)), (gate, _row(tm, 2048)), (pa, _row(tm, D_MODEL)),
           (pb, _row(tm, D_MODEL)), (wba, _resident(wba.shape)), (wbb, _resident(wbb.shape)), (yb, _row(tm, D_MODEL // 2))]
    outs = [o_(cdim, BF16), o_(D_MODEL, F32), o_(D_MODEL, BF16), o_(D_MODEL, BF16), o_(D_MODEL, BF16),
            o_(2048, BF16), o_(1024, BF16), o_(1024, BF16),
            (_sds((HEADS, t_rows), F32), pl.BlockSpec((HEADS, tm), lambda i: (0, i))),
            acc(1, D_MODEL), acc(1, D_MODEL), acc(1, 2048)]
    return _rows_call("bwd_mid", body, t_rows, tm, ins, outs)


def _bwd_in(dqs, dks, dvs, dqm, dkm, dvm, tabs, consts, cq, ckv, gq, gkv, wuq, wk, wv, dgates, win, x, g1, dx1, tm):
    t_rows = x.shape[0]

    def body(dqs_ref, dks_ref, dvs_ref, dqm_ref, dkm_ref, dvm_ref, ca, sa1, sa2, cb, sb1, sb2, c_ref, cq_ref,
             ckv_ref, gq_ref, gkv_ref, wuq_ref, wk_ref, wv_ref, dgt_ref, win_ref, x_ref, g1_ref, dx1_ref,
             dz_ref, dqb_ref, dx_ref, dgq_ref, dgkv_ref, dg1_ref):
        i = pl.program_id(0)

        @pl.when(i == 0)
        def _():
            dgq_ref[...] = jnp.zeros(dgq_ref.shape, F32)
            dgkv_ref[...] = jnp.zeros(dgkv_ref.shape, F32)
            dg1_ref[...] = jnp.zeros(dg1_ref.shape, F32)

        ta = (ca[...], sa1[...], sa2[...])
        tb = (cb[...], sb1[...], sb2[...])

        def piece(lo, hi, val):
            dz_ref[:, lo:hi] = val
            return _dot_nt(val, win_ref[:, lo:hi])

        dh1 = piece(Z_GATE, ZW, dgt_ref[...])
        dkm = dkm_ref[...]
        dckvn = _dot_nt(dkm.astype(BF16), wk_ref[...]) + _dot_nt(dvm_ref[...].astype(BF16), wv_ref[...])
        dh1 = dh1 + piece(Z_VA, Z_CQ, _fold_slots(dvs_ref[...]).astype(BF16))
        dqm = jnp.concatenate([dqm_ref[h] for h in range(HEADS)], axis=1)
        dqb = _rope_t(dqm * SCALE_B, *tb, ROPE_DIM // 2).astype(BF16)
        dqb_ref[...] = dqb
        dcqn = _dot_nt(dqb, wuq_ref[...])
        dqa = _rope_t(_fold_slots(dqs_ref[...]) * SCALE_A, *ta, A_HEAD_DIM // 2)
        dh1 = dh1 + piece(Z_QA, Z_KA, dqa.astype(BF16))
        dh1 = dh1 + piece(Z_KA, Z_VA, _rope_t(_fold_slots(dks_ref[...]), *ta, A_HEAD_DIM // 2).astype(BF16))
        ckvn, rkv = _rms_stats(ckv_ref[...])
        dckv, dgkv = _rms_bwd(dckvn, ckvn, rkv, gkv_ref[...])
        dgkv_ref[...] += dgkv
        dh1 = dh1 + piece(Z_CKV, Z_KR, dckv.astype(BF16))
        dslot = dkm[:, 0:LANES]
        for h in range(1, HEADS):
            dslot = dslot + dkm[:, h * LANES:(h + 1) * LANES]
        dh1 = dh1 + piece(Z_KR, Z_GATE, _rope_t(dslot * c_ref[10:11, :], *tb, ROPE_DIM // 2).astype(BF16))
        cqn, rq = _rms_stats(cq_ref[...])
        dcq, dgq = _rms_bwd(dcqn, cqn, rq, gq_ref[...])
        dgq_ref[...] += dgq
        dh1 = dh1 + piece(Z_CQ, Z_CKV, dcq.astype(BF16))
        xn, r1 = _rms_stats(x_ref[...])
        d0, dg1 = _rms_bwd(dh1, xn, r1, g1_ref[...])
        dg1_ref[...] += dg1
        dx_ref[...] = dx1_ref[...] + d0

    def acc(n):
        return (_sds((1, n), F32), _full((1, n)))

    ins = [(dqs, _row(tm, 1024)), (dks, _row(tm, 256)), (dvs, _row(tm, 256)), (dqm, _heads(tm, HEADS)),
           (dkm, _row(tm, 1024)), (dvm, _row(tm, 1024))] + [(t, _row(tm, LANES)) for t in tabs] + [
           (consts, _full(consts.shape)), (cq, _row(tm, 256)), (ckv, _row(tm, 128)), (gq, _full(gq.shape)),
           (gkv, _full(gkv.shape)), (wuq, _full(wuq.shape)), (wk, _full(wk.shape)), (wv, _full(wv.shape)),
           (dgates, _row(tm, 2048)), (win, _resident(win.shape)), (x, _row(tm, D_MODEL)), (g1, _full(g1.shape)),
           (dx1, _row(tm, D_MODEL))]
    outs = [(_sds((t_rows, ZW), BF16), _row(tm, ZW)), (_sds((t_rows, 1024), BF16), _row(tm, 1024)),
            (_sds((t_rows, D_MODEL), F32), _row(tm, D_MODEL)), acc(256), acc(128), acc(D_MODEL)]
    return _rows_call("bwd_in", body, t_rows, tm, ins, outs)


def _pick_cols(n):
    best = LANES
    for d in range(LANES, min(n, 1664) + 1, LANES):
        if n % d == 0:
            best = d
    return best


def _mm_tn(name, a, b, column_shards=1, after=None):
    t_rows, m = a.shape
    n = b.shape[1]
    bk = min(1024, t_rows)
    bm, bn = _pick_cols(m), _pick_cols(n // column_shards)
    per_shard = n // column_shards // bn
    extra = () if after is None else (after,)

    def body(a_ref, b_ref, *rest):
        o_ref = rest[-1]

        @pl.when(pl.program_id(2) == 0)
        def _():
            o_ref[...] = jnp.zeros((bm, bn), F32)

        o_ref[...] += _dot_tn(a_ref[...].astype(BF16), b_ref[...].astype(BF16))

    return pl.pallas_call(
        body, name=name, grid=(m // bm, n // bn, t_rows // bk),
        in_specs=[pl.BlockSpec((bk, bm), lambda i, j, k: (k, i)), pl.BlockSpec((bk, bn), lambda i, j, k: (k, j))]
        + [pl.BlockSpec((8, LANES), lambda i, j, k: (0, 0))] * len(extra),
        out_specs=(pl.BlockSpec((bm, bn), lambda i, j, k: (i, j)) if column_shards == 1 else
                   pl.BlockSpec((None, bm, bn), lambda i, j, k: (j // per_shard, i, j % per_shard))),
        out_shape=_sds((m, n) if column_shards == 1 else (column_shards, m, n // column_shards), F32),
        compiler_params=pltpu.CompilerParams(dimension_semantics=("arbitrary",) * 3, vmem_limit_bytes=VMEM_LIMIT),
    )(a, b, *extra)


PACK_ROWS = 512


ADD_TILE_ELEMS = 1 << 17


def _add_rows(rows, cols):
    best = 16
    for d in range(16, rows + 1, 16):
        if rows % d == 0 and d * cols <= ADD_TILE_ELEMS:
            best = d
    assert rows % best == 0
    return best


def _add_pair(name, g, recv, half):
    _, _, rows, cols = g.shape
    t = _add_rows(rows, cols)

    def body(h_ref, g_ref, r_ref, o_ref):
        o_ref[...] = (g_ref[:, 0] + r_ref[...]).astype(BF16)

    spec = pl.BlockSpec((4, t, cols), lambda i, h: (0, i, 0))
    grid_spec = pltpu.PrefetchScalarGridSpec(
        num_scalar_prefetch=1, grid=(rows // t,),
        in_specs=[pl.BlockSpec((4, 1, t, cols), lambda i, h: (0, h[0], i, 0)), spec], out_specs=spec)
    return pl.pallas_call(body, name=name, grid_spec=grid_spec,
                          out_shape=_sds(recv.shape, BF16))(jnp.reshape(half, (1,)).astype(jnp.int32), g, recv)


def _add_chips(name, parts):
    _, rows, cols = parts.shape
    t = _add_rows(rows, cols)

    def body(p_ref, o_ref):
        acc = p_ref[0].astype(F32)
        for j in range(1, 4):
            acc = acc + p_ref[j].astype(F32)
        o_ref[...] = acc

    return pl.pallas_call(body, name=name, grid=(rows // t,),
                          in_specs=[pl.BlockSpec((4, t, cols), lambda i: (0, i, 0))],
                          out_specs=pl.BlockSpec((t, cols), lambda i: (i, 0)),
                          out_shape=_sds((rows, cols), F32))(parts)


def _add_devices(parts):
    n, rows, _ = parts.shape

    def body(p_ref, o_ref):
        acc = p_ref[0]
        for j in range(1, n):
            acc = acc + p_ref[j]
        o_ref[...] = acc

    return pl.pallas_call(body, name="small_add", grid=(1,),
                          in_specs=[pl.BlockSpec((n, rows, LANES), lambda i: (0, 0, 0))],
                          out_specs=pl.BlockSpec((rows, LANES), lambda i: (0, 0)),
                          out_shape=_sds((rows, LANES), F32))(parts)


def _adam_rows(k, n):
    target = max(8, (1 << 20) // (4 * n))
    if k <= target:
        return k
    best = None
    for d in range(8, target + 1, 8):
        if k % d == 0:
            best = d
    return best if best is not None else k


def _adam_update(w, g, m, v):
    m_ = ADAM_B1 * m + (1.0 - ADAM_B1) * g
    v_ = ADAM_B2 * v + (1.0 - ADAM_B2) * (g * g)
    delta = -ADAM_LR * ((m_ / (1.0 - ADAM_B1 ** ADAM_STEP)) / (jnp.sqrt(v_ / (1.0 - ADAM_B2 ** ADAM_STEP)) + ADAM_EPS)
                        + ADAM_WD * w)
    return delta, m_, v_


def _adamw(name, w, g, m, v):
    k, n = w.shape
    bk = _adam_rows(k, n)

    def body(w_ref, g_ref, m_ref, v_ref, d_ref, mo_ref, vo_ref):
        d_ref[...], mo_ref[...], vo_ref[...] = _adam_update(w_ref[...], g_ref[...], m_ref[...], v_ref[...])

    spec = pl.BlockSpec((bk, n), lambda i: (i, 0))
    out = pl.pallas_call(body, name=name, grid=(k // bk,), in_specs=[spec] * 4, out_specs=[spec] * 3,
                         out_shape=[_sds((k, n), F32)] * 3,
                         compiler_params=pltpu.CompilerParams(vmem_limit_bytes=VMEM_LIMIT))(w, g, m, v)
    return (g, *out)


def _adamw_halves(name, w, mine, theirs, m, v, half):
    k, n = w.shape
    bk = _adam_rows(k // 2, n)
    nb = k // 2 // bk

    def body(h_ref, w_ref, mine_ref, theirs_ref, m_ref, v_ref, g_ref, d_ref, mo_ref, vo_ref):
        g = jnp.where(pl.program_id(0) == h_ref[0], mine_ref[...], theirs_ref[...])
        g_ref[...] = g
        d_ref[...], mo_ref[...], vo_ref[...] = _adam_update(w_ref[...], g, m_ref[...], v_ref[...])

    full = pl.BlockSpec((bk, n), lambda h, i, c: (h * nb + i, 0))
    part = pl.BlockSpec((bk, n), lambda h, i, c: (i, 0))
    grid_spec = pltpu.PrefetchScalarGridSpec(num_scalar_prefetch=1, grid=(2, nb),
                                             in_specs=[full, part, part, full, full], out_specs=[full] * 4)
    return tuple(pl.pallas_call(
        body, name=name, grid_spec=grid_spec, out_shape=[_sds((k, n), F32)] * 4,
        compiler_params=pltpu.CompilerParams(vmem_limit_bytes=VMEM_LIMIT),
    )(jnp.reshape(half, (1,)).astype(jnp.int32), w, mine, theirs, m, v))


_HBM = pl.BlockSpec(memory_space=pltpu.HBM)


def _me():
    return lax.axis_index("x"), lax.axis_index("y"), lax.axis_index("c")


def _other_chips(x, y):
    return [(1 - x, y), (x, 1 - y), (1 - x, 1 - y)]


def _pass_to_sibling(zones):
    n = len(zones)

    def body(*refs):
        in_refs, out_refs = refs[:n], refs[n:2 * n]
        send_sems, recv_sems = refs[2 * n:]
        x, y, c = _me()
        sent = []
        for a, (in_ref, out_ref) in enumerate(zip(in_refs, out_refs)):
            for j, (cx, cy) in enumerate(_other_chips(x, y)):
                mine, theirs = (2 * cx + cy, c), (2 * cx + cy, 1 - c)
                sems = dict(send_sem=send_sems.at[3 * a + j], recv_sem=recv_sems.at[3 * a + j],
                            device_id=(x, y, 1 - c), device_id_type=MESH)
                sent.append((pltpu.make_async_remote_copy(src_ref=in_ref.at[mine], dst_ref=out_ref.at[mine], **sems),
                             pltpu.make_async_remote_copy(src_ref=in_ref.at[theirs], dst_ref=out_ref.at[theirs], **sems)))
        for send, _ in sent:
            send.start()
        for _, recv in sent:
            recv.wait_recv()
        for send, _ in sent:
            send.wait_send()

    return pl.pallas_call(
        body, name="pass_to_sibling", out_shape=[_sds(z.shape, z.dtype) for z in zones],
        in_specs=[_HBM] * n, out_specs=[_HBM] * n, input_output_aliases={i: i for i in range(n)},
        scratch_shapes=[pltpu.SemaphoreType.DMA((3 * n,)), pltpu.SemaphoreType.DMA((3 * n,))],
    )(*zones)


def _swap_sibling(name, vs, other_half=False):
    n = len(vs)

    def body(*refs):
        v_refs, out_refs = refs[:n], refs[n:2 * n]
        send_sems, recv_sems = refs[2 * n:]
        x, y, c = _me()
        cps = [pltpu.make_async_remote_copy(src_ref=v_ref.at[:, 1 - c] if other_half else v_ref, dst_ref=out_ref,
                                            send_sem=send_sems.at[a], recv_sem=recv_sems.at[a],
                                            device_id=(x, y, 1 - c), device_id_type=MESH)
               for a, (v_ref, out_ref) in enumerate(zip(v_refs, out_refs))]
        for cp in cps:
            cp.start()
        for cp in cps:
            cp.wait()

    def landing(v):
        return _sds((v.shape[0],) + v.shape[2:] if other_half else v.shape, v.dtype)

    return pl.pallas_call(
        body, name=name, out_shape=[landing(v) for v in vs], in_specs=[_HBM] * n, out_specs=[_HBM] * n,
        scratch_shapes=[pltpu.SemaphoreType.DMA((n,)), pltpu.SemaphoreType.DMA((n,))],
    )(*vs)


_SEM = pl.BlockSpec(memory_space=pltpu.SEMAPHORE)
_EFFECT = pltpu.SideEffectType.DATAFLOW_SIDE_EFFECTING
WHOLE = "whole"
PIECE = "piece"
SIBLING_HALF = "sibling"
MY_HALF = "half"
EVERYONE = "everyone"
_COPIES = {WHOLE: 3, PIECE: 3, MY_HALF: 3, SIBLING_HALF: 1, EVERYONE: 7}


def _landing_shape(v, mode):
    return {WHOLE: (4,) + v.shape, MY_HALF: (4,) + v.shape, PIECE: v.shape, EVERYONE: (8,) + v.shape,
            SIBLING_HALF: (v.shape[0],) + v.shape[2:]}[mode]


def _chip_copies(v_ref, land_ref, send_sems, recv_sems, mode, sem0=0):
    x, y, c = _me()
    if mode == SIBLING_HALF:
        cp = pltpu.make_async_remote_copy(src_ref=v_ref.at[:, 1 - c], dst_ref=land_ref, send_sem=send_sems.at[sem0],
                                          recv_sem=recv_sems.at[sem0], device_id=(x, y, 1 - c), device_id_type=MESH)
        return [(cp, cp)]
    if mode == EVERYONE:
        out = []
        for f in range(1, 8):
            px, py, pc = (1 - x if f & 4 else x), (1 - y if f & 2 else y), (1 - c if f & 1 else c)
            sems = dict(send_sem=send_sems.at[sem0 + f - 1], recv_sem=recv_sems.at[sem0 + f - 1],
                        device_id=(px, py, pc), device_id_type=MESH)
            out.append((pltpu.make_async_remote_copy(src_ref=v_ref, dst_ref=land_ref.at[4 * x + 2 * y + c], **sems),
                        pltpu.make_async_remote_copy(src_ref=v_ref, dst_ref=land_ref.at[4 * px + 2 * py + pc], **sems)))
        return out
    k = 2 * x + y
    out = []
    for j, (cx, cy) in enumerate(_other_chips(x, y)):
        if mode == MY_HALF:
            src, mine, theirs = v_ref.at[c], land_ref.at[k, c], land_ref.at[2 * cx + cy, c]
        else:
            src = v_ref.at[2 * cx + cy] if mode == PIECE else v_ref
            mine, theirs = land_ref.at[k], land_ref.at[2 * cx + cy]
        sems = dict(send_sem=send_sems.at[sem0 + j], recv_sem=recv_sems.at[sem0 + j], device_id=(cx, cy, c),
                    device_id_type=MESH)
        send = pltpu.make_async_remote_copy(src_ref=src, dst_ref=mine, **sems)
        recv = pltpu.make_async_remote_copy(src_ref=src, dst_ref=theirs, **sems)
        out.append((send, recv))
    return out


def _chips_start(name, vs, mode, after=None):
    n = len(vs)
    lands = [_landing_shape(v, mode) for v in vs]

    def body(*refs):
        v_refs, land_refs = refs[:n], refs[n:2 * n]
        send_sems, recv_sems = refs[-2 * n - 3], refs[-2 * n - 2]
        token = refs[-1]
        for a in range(n):
            for send, _ in _chip_copies(v_refs[a], land_refs[a], send_sems, recv_sems, mode, _COPIES[mode] * a):
                send.start()
        token[...] = jnp.zeros_like(token)

    extra = () if after is None else (after,)
    hbm = [pltpu.with_memory_space_constraint(v, pltpu.HBM) for v in vs]
    zones = [pltpu.with_memory_space_constraint(lax.empty(s, v.dtype), pltpu.HBM) for s, v in zip(lands, vs)]
    out = pl.pallas_call(
        body, name=name,
        out_shape=(pltpu.SemaphoreType.DMA((_COPIES[mode] * n,)), pltpu.SemaphoreType.DMA((_COPIES[mode] * n,)),
                   *[pltpu.HBM(v.shape, v.dtype) for v in vs], *[pltpu.HBM(s, v.dtype) for s, v in zip(lands, vs)],
                   _sds((8, LANES), F32)),
        in_specs=(_HBM,) * (2 * n) + (pl.BlockSpec(memory_space=pl.ANY),) * len(extra),
        out_specs=(_SEM, _SEM) + (_HBM,) * (2 * n) + (pl.BlockSpec(memory_space=pltpu.VMEM),),
        input_output_aliases={i: 2 + i for i in range(2 * n)},
        compiler_params=pltpu.CompilerParams(has_side_effects=_EFFECT),
    )(*hbm, *zones, *extra)
    return out[0], out[1], list(out[2:2 + n]), list(out[2 + n:2 + 2 * n]), out[-1]


def _chips_wait(name, send_sems, recv_sems, v_thru, land_thru, mode, after):
    n = len(v_thru)

    def body(*refs):
        v_refs, land_refs = refs[:n], refs[n:2 * n]
        send_sems, recv_sems = refs[2 * n], refs[2 * n + 1]
        for a in range(n):
            for send, recv in _chip_copies(v_refs[a], land_refs[a], send_sems, recv_sems, mode, _COPIES[mode] * a):
                send.wait_send()
                recv.wait_recv()

    out = pl.pallas_call(
        body, name=name,
        out_shape=tuple(pltpu.HBM(a.shape, a.dtype) for a in list(v_thru) + list(land_thru)),
        in_specs=(_HBM,) * (2 * n) + (_SEM, _SEM, pl.BlockSpec(memory_space=pl.ANY)), out_specs=(_HBM,) * (2 * n),
        input_output_aliases={i: i for i in range(2 * n)},
        compiler_params=pltpu.CompilerParams(has_side_effects=_EFFECT),
    )(*v_thru, *land_thru, send_sems, recv_sems, after)
    return list(out[:n]), list(out[n:])


_BIG = (("w_in", (1024, 3232), 1), ("w_uq", (256, 768), 1), ("w_ukv", (128, 1024), 1), ("w_branch_a", (512, 1024), 1),
        ("w_branch_b", (512, 1024), 1), ("w_out", (1024, 1024), 0), ("w_up", (1024, 5632), 1),
        ("w_down", (2816, 1024), 0), ("w_ple_gate", (1024, 1024), 0), ("w_ple", (256, 1024), 1))


def _shard_shape(shape, axis):
    return (shape[0] // 4, shape[1]) if axis == 0 else (shape[0], shape[1] // 4)


def _half_rows(shape, axis):
    k, n = _shard_shape(shape, axis)
    return k * n // (2 * LANES)


_EARLY = ("w_in", "w_uq", "w_ukv")
_LATE = ("w_branch_a", "w_branch_b", "w_out", "w_up", "w_down", "w_ple_gate", "w_ple")
_NATURAL = ("w_in", "w_up", "w_down", "w_out", "w_ple_gate")
_EARLY_PACKED = tuple(b for b in _BIG if b[0] in _EARLY and b[0] not in _NATURAL)
_LATE_PACKED = tuple(b for b in _BIG if b[0] in _LATE and b[0] not in _NATURAL)
_SHARD = {name: _shard_shape(shape, axis) for name, shape, axis in _BIG}


def _halves(a):
    return a.reshape(a.shape[:-2] + (2, a.shape[-2] // 2, a.shape[-1]))


def _rows_joined(a):
    return a.reshape(a.shape[:-3] + (a.shape[-3] * a.shape[-2], a.shape[-1]))


def _pack_pad(group):
    return -sum(_half_rows(shape, axis) for _, shape, axis in group) % PACK_ROWS


def _pack_shards(shards, dtype, group):
    parts = [shards[name].astype(dtype).reshape(2, _half_rows(shape, axis), LANES) for name, shape, axis in group]
    return jnp.concatenate(parts + [jnp.zeros((2, _pack_pad(group), LANES), dtype)], axis=1)


def _unpack_gathered(g, group):
    out, off = {}, 0
    for name, shape, axis in group:
        r = _half_rows(shape, axis)
        k, n = _shard_shape(shape, axis)
        w = g[:, :, off:off + r, :].reshape(4, k, n)
        out[name] = w.reshape(shape) if axis == 0 else w.transpose(1, 0, 2).reshape(shape)
        off += r
    return out


def _pack_grads(grads, group):
    parts = []
    for name, shape, axis in group:
        k, n = _shard_shape(shape, axis)
        g = grads[name]
        g4 = g.reshape(4, k, n) if axis == 0 else g.reshape(k, 4, n).transpose(1, 0, 2)
        parts.append(g4.reshape(4, 2, _half_rows(shape, axis), LANES))
    return jnp.concatenate(parts + [jnp.zeros((4, 2, _pack_pad(group), LANES), F32)], axis=2)


def _unpack_shard_grads(f, group):
    out, off = {}, 0
    for name, shape, axis in group:
        r = _half_rows(shape, axis)
        out[name] = f[:, off:off + r, :].reshape(_shard_shape(shape, axis))
        off += r
    return out


def _pad_slots(w, heads, dim, axis):
    if axis == 1:
        k = w.shape[0]
        return jnp.pad(w.reshape(k, heads, dim), ((0, 0), (0, 0), (0, LANES - dim))).reshape(k, heads * LANES)
    n = w.shape[1]
    return jnp.pad(w.reshape(heads, dim, n), ((0, 0), (0, LANES - dim), (0, 0))).reshape(heads * LANES, n)


def _unpad_slots(w, heads, dim, axis):
    if axis == 1:
        k = w.shape[0]
        return w.reshape(k, heads, LANES)[:, :, :dim].reshape(k, heads * dim)
    n = w.shape[1]
    return w.reshape(heads, LANES, n)[:, :dim, :].reshape(heads * dim, n)


def _pad_w_in(w):
    kr = jnp.pad(w[:, Z_KR:Z_KR + ROPE_DIM], ((0, 0), (NOPE_DIM, LANES - NOPE_DIM - ROPE_DIM)))
    return jnp.concatenate([w[:, :Z_KR], kr, w[:, Z_KR + ROPE_DIM:]], axis=1)


def _unpad_w_in(w):
    return jnp.concatenate([w[:, :Z_KR], w[:, Z_KR + NOPE_DIM:Z_KR + NOPE_DIM + ROPE_DIM], w[:, Z_GATE:ZW]], axis=1)


def _spread_matrix(heads, dim):
    row = lax.broadcasted_iota(jnp.int32, (heads * dim, heads * LANES), 0)
    col = lax.broadcasted_iota(jnp.int32, (heads * dim, heads * LANES), 1)
    return (col == (row // dim) * LANES + row % dim).astype(BF16)


_SMALL = (("attn_pre_norm", 1024), ("attn_post_norm", 1024), ("b_gate", 2048), ("sinks", 8), ("q_a_norm", 256),
          ("kv_a_norm", 128), ("mlp_pre_norm", 1024), ("mlp_post_norm", 1024), ("conv_b", 5632), ("ple_norm", 1024),
          ("conv_w", 3 * 5632), ("loss", 1))


def _small_rows(n):
    return 8 * -(-n // (8 * LANES))


def _pack_small(vals):
    parts = []
    for name, n in _SMALL:
        r = _small_rows(n)
        parts.append(jnp.pad(vals[name].reshape(-1), (0, r * LANES - n)).reshape(r, LANES))
    return jnp.concatenate(parts, axis=0)


def _unpack_small(buf):
    out, off = {}, 0
    for name, n in _SMALL:
        r = _small_rows(n)
        out[name] = buf[off:off + r].reshape(-1)[:n]
        off += r
    return out


def kernel(x, p, positions, attn_pre_norm, attn_post_norm, w_in, b_gate, sinks, q_a_norm, w_uq, kv_a_norm, w_ukv, w_branch_a, w_branch_b, w_out, mlp_pre_norm, mlp_post_norm, w_up, conv_w, conv_b, w_down, ple_norm, w_ple_gate, w_ple, loss_target, m_attn_pre_norm, m_attn_post_norm, m_w_in, m_b_gate, m_sinks, m_q_a_norm, m_w_uq, m_kv_a_norm, m_w_ukv, m_w_branch_a, m_w_branch_b, m_w_out, m_mlp_pre_norm, m_mlp_post_norm, m_w_up, m_conv_w, m_conv_b, m_w_down, m_ple_norm, m_w_ple_gate, m_w_ple, v_attn_pre_norm, v_attn_post_norm, v_w_in, v_b_gate, v_sinks, v_q_a_norm, v_w_uq, v_kv_a_norm, v_w_ukv, v_w_branch_a, v_w_branch_b, v_w_out, v_mlp_pre_norm, v_mlp_post_norm, v_w_up, v_conv_w, v_conv_b, v_w_down, v_ple_norm, v_w_ple_gate, v_w_ple):
    names = ["attn_pre_norm", "attn_post_norm", "w_in", "b_gate", "sinks", "q_a_norm", "w_uq", "kv_a_norm", "w_ukv",
             "w_branch_a", "w_branch_b", "w_out", "mlp_pre_norm", "mlp_post_norm", "w_up", "conv_w", "conv_b",
             "w_down", "ple_norm", "w_ple_gate", "w_ple"]
    wts = dict(zip(names, [attn_pre_norm, attn_post_norm, w_in, b_gate, sinks, q_a_norm, w_uq, kv_a_norm, w_ukv,
                           w_branch_a, w_branch_b, w_out, mlp_pre_norm, mlp_post_norm, w_up, conv_w, conv_b, w_down,
                           ple_norm, w_ple_gate, w_ple]))
    moms = dict(zip(names, [m_attn_pre_norm, m_attn_post_norm, m_w_in, m_b_gate, m_sinks, m_q_a_norm, m_w_uq,
                            m_kv_a_norm, m_w_ukv, m_w_branch_a, m_w_branch_b, m_w_out, m_mlp_pre_norm,
                            m_mlp_post_norm, m_w_up, m_conv_w, m_conv_b, m_w_down, m_ple_norm, m_w_ple_gate, m_w_ple]))
    vars_ = dict(zip(names, [v_attn_pre_norm, v_attn_post_norm, v_w_in, v_b_gate, v_sinks, v_q_a_norm, v_w_uq,
                             v_kv_a_norm, v_w_ukv, v_w_branch_a, v_w_branch_b, v_w_out, v_mlp_pre_norm,
                             v_mlp_post_norm, v_w_up, v_conv_w, v_conv_b, v_w_down, v_ple_norm, v_w_ple_gate, v_w_ple]))
    w2 = {n: a.reshape(a.shape[-2:]) for n, a in wts.items()}
    m2 = {n: a.reshape(a.shape[-2:]) for n, a in moms.items()}
    v2 = {n: a.reshape(a.shape[-2:]) for n, a in vars_.items()}

    t_rows = x.shape[-2]
    tm = min(256, t_rows)
    tm_wide = min(512, t_rows)
    xc, yc, cc = lax.axis_index("x"), lax.axis_index("y"), lax.axis_index("c")
    chip = 2 * xc + yc

    x2d = x.reshape(t_rows, D_MODEL)
    p2d = p.reshape(t_rows, PLE_DIM)
    tgt = loss_target.reshape(t_rows, D_MODEL)
    pos_f = positions.reshape(t_rows, 1).astype(F32)

    def own_slot_filled(gathered, mine):
        return [lax.dynamic_update_slice(g, m[None], (chip, 0, 0, 0)) for g, m in zip(gathered, mine)]

    def shard_lists(group, packed_group, token=0.0):
        ws = {n: w2[n] + token for n in group}
        return [_halves(ws[n].astype(BF16)) for n in group if n in _NATURAL] + [_pack_shards(ws, BF16, packed_group)]

    cw_rows = 3 * 1408 // LANES
    conv_mine = jnp.pad(w2["conv_w"].reshape(cw_rows, LANES), ((0, 48 - cw_rows), (0, 0))).reshape(2, 24, LANES)
    early_mine = shard_lists(_EARLY, _EARLY_PACKED) + [conv_mine]
    early_sems = _chips_start("gather_early_start", early_mine, MY_HALF)
    early_token = early_sems[4][0:1, 0:1]
    consts = _rope_consts()
    tabs = _rope_tables(pos_f + early_token, consts, tm)
    late_mine = shard_lists(_LATE, _LATE_PACKED, early_token)
    both_done = tabs[0][0:1, 0:1] + sum(m[0, 0:1, 0:1].astype(F32) for m in late_mine)
    early_sent, early_landed = _chips_wait("gather_early_wait", *early_sems[:4], MY_HALF, after=both_done)
    early = own_slot_filled(_pass_to_sibling(early_landed), early_sent)
    late_names = [n for n in _LATE if n in _NATURAL]
    first = [late_names.index("w_out"), len(late_names)]
    late_a = [late_mine[i] for i in first]
    late_b = [m for i, m in enumerate(late_mine) if i not in first]
    late_a_sems = _chips_start("gather_late_a_start", late_a, WHOLE, after=early[0])
    late_b_sems = _chips_start("gather_late_b_start", late_b, WHOLE, after=late_a_sems[4])
    late_token = late_b_sems[4][0:1, 0:1]
    full = _unpack_gathered(early[1], _EARLY_PACKED)
    full["w_in"] = _rows_joined(early[0]).transpose(1, 0, 2).reshape(D_MODEL, 3232)
    conv_full = early[2].reshape(4, 48, LANES)[:, :cw_rows].reshape(4, 3, 1408).transpose(1, 0, 2).reshape(3, 2 * D_FF)
    convw8 = jnp.pad(conv_full, ((0, 5), (0, 0)))

    win = _pad_w_in(full["w_in"])
    wuq = _pad_slots(full["w_uq"], HEADS, NOPE_DIM + ROPE_DIM, 1)
    ukv = full["w_ukv"].reshape(KV_LORA, HEADS, NOPE_DIM + V_DIM)
    wk = _pad_slots(ukv[:, :, :NOPE_DIM].reshape(KV_LORA, HEADS * NOPE_DIM), HEADS, NOPE_DIM, 1)
    wv = _pad_slots(ukv[:, :, NOPE_DIM:].reshape(KV_LORA, HEADS * V_DIM), HEADS, V_DIM, 1)
    g1, g2, g3, g4, g5 = (w2["attn_pre_norm"], w2["attn_post_norm"], w2["mlp_pre_norm"], w2["mlp_post_norm"],
                          w2["ple_norm"])
    gq, gkv, bg, convb = w2["q_a_norm"], w2["kv_a_norm"], w2["b_gate"], w2["conv_b"]
    swa_tile = min(SWA_TILE, t_rows)
    sink_rows = jnp.repeat(w2["sinks"].reshape(A_KV_HEADS, SWA_GROUP, 1), swa_tile, axis=2).reshape(
        A_KV_HEADS, 1, SWA_GROUP * swa_tile)
    swa_bias = _swa_bias(swa_tile)
    spread_q = _spread_matrix(HEADS, A_HEAD_DIM)
    spread_kv = _spread_matrix(A_KV_HEADS, A_HEAD_DIM)

    h1, qs, ks, vs, cq, cqn, ckv, ckvn, qm, km, vm, gate = _fwd_in(x2d, g1, win, bg + late_token, gq, gkv, wuq, wk, wv,
                                                                   spread_q, spread_kv, tabs, tm_wide)
    ya, lse_a = _swa_fwd(qs, ks, vs, swa_bias, sink_rows)
    yb, lse_b = _mla_fwd(qm, km, vm)
    late_sent, late_landed = _chips_wait("gather_late_a_wait", *late_a_sems[:4], WHOLE, after=yb)
    wout_g, packed_g = own_slot_filled(late_landed, late_sent)
    full = _unpack_gathered(packed_g, _LATE_PACKED)
    wba, wbb = full["w_branch_a"], full["w_branch_b"]
    wple = full["w_ple"]
    wout = _rows_joined(wout_g).reshape(-1, D_MODEL)
    pa, pb, mixed, o, x1, h2, ya_c, yb_c = _fwd_mix(x2d, ya, yb, gate, wba, wbb, wout, g2, g3, tm_wide)
    late_sent, late_landed = _chips_wait("gather_late_b_wait", *late_b_sems[:4], WHOLE, after=pa)
    natural = dict(zip([n for n in late_names if n != "w_out"], own_slot_filled(late_landed, late_sent)))
    wup = _rows_joined(natural["w_up"])
    wdown, wpg = (_rows_joined(natural[n]).reshape(-1, D_MODEL) for n in ("w_down", "w_ple_gate"))
    up, a = _fwd_up(h2, wup, convw8, convb, tm)
    ff, x2, e, n5, sg, dx3, loss_part = _fwd_out(a, wdown, x1, g4, p2d, wple, g5, wpg, tgt, tm_wide)

    dpre, de, dx2, dff, du, dg5, dg4, dconvb, dconvw8 = _bwd_out(dx3, e, sg, x2, ff, g5, g4, wpg, wdown, up, convw8,
                                                                 convb, tm)
    dup, dx1, do, dpa, dpb, dgates, dya, dyb, delta_b, dg3, dg2, dbg = _bwd_mid(
        du, convw8, wup, dx2, x1, g3, o, g2, wout, gate, pa, pb, wba, wbb, yb_c, tm)
    late_grads = {
        "w_branch_a": _mm_tn("dw_branch_a", ya_c, dpa),
        "w_branch_b": _mm_tn("dw_branch_b", yb_c, dpb),
        "w_out": _mm_tn("dw_out", mixed, do).reshape(4, D_MODEL // 4, D_MODEL),
        "w_up": _mm_tn("dw_up", h2, dup, column_shards=4),
        "w_down": _mm_tn("dw_down", a, dff).reshape(4, D_FF // 4, D_MODEL),
        "w_ple_gate": _mm_tn("dw_ple_gate", n5, dpre).reshape(4, D_MODEL // 4, D_MODEL),
        "w_ple": _mm_tn("dw_ple", p2d, de),
    }

    def grad_views(grads, group, packed_group):
        return [_halves(grads[n]) for n in group if n in _NATURAL] + [_pack_grads(grads, packed_group)]

    def pair_sums(tag, views, theirs):
        return [_add_pair("rs_%s_add_pair_%d" % (tag, i), g, r, cc) for i, (g, r) in enumerate(zip(views, theirs))]

    swap_sems = _chips_start("swap_late_start", grad_views(late_grads, _LATE, _LATE_PACKED), SIBLING_HALF)
    dqs, dks, dvs, dsink_rows = _swa_bwd(qs, ks, vs, ya, dya, lse_a, swa_bias, sink_rows + swap_sems[4][0:1, 0:1])
    dsink = dsink_rows[:, 0:SWA_GROUP, 0]
    late_views, late_theirs = _chips_wait("swap_late_wait", *swap_sems[:4], SIBLING_HALF, after=dqs)
    rs_sems = _chips_start("scatter_late_start", pair_sums("late", late_views, late_theirs), PIECE)
    dqm, dkm, dvm = _mla_bwd(qm, km, vm, dyb, lse_b, delta_b.reshape(HEADS, 1, t_rows) + rs_sems[4][0:1, 0:1])
    dz, dqb, dx, dgq, dgkv, dg1 = _bwd_in(dqs, dks, dvs, dqm, dkm, dvm, tabs, consts, cq, ckv, gq, gkv, wuq, wk, wv,
                                           dgates, win, x2d, g1, dx1, tm)

    small = {"attn_pre_norm": dg1, "attn_post_norm": dg2, "b_gate": dbg, "sinks": dsink, "q_a_norm": dgq,
             "kv_a_norm": dgkv, "mlp_pre_norm": dg3, "mlp_post_norm": dg4, "conv_b": dconvb, "ple_norm": dg5,
             "conv_w": dconvw8[0:3], "loss": loss_part}
    small_sems = _chips_start("gather_small_start", [_pack_small(small)], EVERYONE)
    small_token = small_sems[4]

    dwk = _unpad_slots(_mm_tn("dw_k", ckvn, dkm, after=small_token), HEADS, NOPE_DIM, 1).reshape(
        KV_LORA, HEADS, NOPE_DIM)
    dwv = _unpad_slots(_mm_tn("dw_v", ckvn, dvm, after=small_token), HEADS, V_DIM, 1).reshape(KV_LORA, HEADS, V_DIM)
    early_grads = {
        "w_in": _unpad_w_in(_mm_tn("dw_in", h1, dz, after=small_token)).reshape(D_MODEL, 4, 808).transpose(1, 0, 2),
        "w_uq": _unpad_slots(_mm_tn("dw_uq", cqn, dqb, after=small_token), HEADS, NOPE_DIM + ROPE_DIM, 1),
        "w_ukv": jnp.concatenate([dwk, dwv], axis=2).reshape(KV_LORA, HEADS * (NOPE_DIM + V_DIM)),
    }

    def finish(tag, pairs, landed, group, packed_group):
        reduced = []
        for i, (pair, land) in enumerate(zip(pairs, landed)):
            own = lax.dynamic_index_in_dim(pair, chip, 0, keepdims=True)
            reduced.append(_add_chips("rs_%s_add_chips_%d" % (tag, i),
                                      lax.dynamic_update_slice(land, own, (chip, 0, 0))))
        others = _swap_sibling("swap_%s_reduced_halves" % tag, reduced)
        r, o = reduced[-1], others[-1]
        packed = jnp.where(cc == 0, jnp.stack([r, o]), jnp.stack([o, r]))
        for n, g in _unpack_shard_grads(packed, packed_group).items():
            updates[n] = _adamw("adamw_" + n, w2[n], g, m2[n], v2[n])
        for n, r, o in zip([n for n in group if n in _NATURAL], reduced, others):
            updates[n] = _adamw_halves("adamw_" + n, w2[n], r, o, m2[n], v2[n], cc)

    updates = {}

    def adamw(n, g):
        updates[n] = _adamw("adamw_" + n, w2[n], g, m2[n], v2[n])

    early_views = grad_views(early_grads, _EARLY, _EARLY_PACKED)
    early_theirs = _swap_sibling("swap_early_grad_halves", early_views, other_half=True)
    small_sent, small_landed = _chips_wait("gather_small_wait", *small_sems[:4], EVERYONE, after=early_theirs[0])
    small_all = lax.dynamic_update_slice(small_landed[0], small_sent[0][None], (4 * xc + 2 * yc + cc, 0, 0))
    early_sems = _chips_start("scatter_early_start", pair_sums("early", early_views, early_theirs), PIECE,
                              after=small_all)
    late_pairs, late_landed = _chips_wait("scatter_late_wait", *rs_sems[:4], PIECE, after=early_sems[4])
    finish("late", late_pairs, late_landed, _LATE, _LATE_PACKED)
    early_pairs, early_landed = _chips_wait("scatter_early_wait", *early_sems[:4], PIECE,
                                            after=updates[_LATE[-1]][1])
    finish("early", early_pairs, early_landed, _EARLY, _EARLY_PACKED)

    small_sum = _unpack_small(_add_devices(small_all))
    for n in names:
        if n == "conv_w":
            adamw(n, lax.dynamic_index_in_dim(small_sum[n].reshape(3, 4, 1408), chip, 1, keepdims=False))
        elif n in small_sum:
            adamw(n, small_sum[n].reshape(w2[n].shape))
    loss = small_sum["loss"][0]

    outs = [[updates[n][i].reshape(wts[n].shape) for n in names] for i in range(4)]
    return (loss, dx.reshape(x.shape), *outs[0], *outs[1], *outs[2], *outs[3])
```

```python
import math

import numpy as np
import jax
import jax.numpy as jnp
from jax import lax
from jax.experimental import pallas as pl
from jax.experimental.pallas import tpu as pltpu

F32 = jnp.float32
BF16 = jnp.bfloat16

D_MODEL = 1024
D_FF = 2816
PLE_DIM = 256
ROPE_THETA = 10000.0
RMS_EPS = 1e-6
SWA_WINDOW = 128
HEADS = 8
A_KV_HEADS = 2
A_HEAD_DIM = 64
Q_LORA = 256
KV_LORA = 128
NOPE_DIM = 64
ROPE_DIM = 32
V_DIM = 64
LANES = 128
ZW = 3328
NEG = -1e30
SCALE_A = A_HEAD_DIM ** -0.5
SCALE_B = (NOPE_DIM + ROPE_DIM) ** -0.5

ADAM_LR = 0.001
ADAM_B1 = 0.9
ADAM_B2 = 0.999
ADAM_EPS = 1e-08
ADAM_WD = 0.01
ADAM_STEP = 10

VMEM_LIMIT = 60 * 1024 * 1024
MESH = pl.DeviceIdType.MESH

Z_QA, Z_KA, Z_VA, Z_CQ, Z_CKV, Z_KR, Z_GATE = 0, 512, 640, 768, 1024, 1152, 1280


def _dot(a, b):
    return jnp.dot(a, b, preferred_element_type=F32)


def _dot_nt(a, b):
    return lax.dot_general(a, b, (((1,), (1,)), ((), ())), preferred_element_type=F32)


def _dot_tn(a, b):
    return lax.dot_general(a, b, (((0,), (0,)), ((), ())), preferred_element_type=F32)


def _rms_stats(x):
    r = lax.rsqrt(jnp.mean(x * x, axis=-1, keepdims=True) + RMS_EPS)
    return x * r, r


def _rms_bwd(dy, xn, r, g):
    dxn = dy * g
    dx = r * (dxn - xn * jnp.mean(dxn * xn, axis=-1, keepdims=True))
    dg = jnp.sum(dy * xn, axis=0, keepdims=True)
    return dx, dg


def _tile_lanes(t, n):
    return t if n == 1 else jnp.concatenate([t] * n, axis=1)


def _rope(x, c, s1, s2, half):
    w = x.shape[1]
    n = w // LANES
    return (x * _tile_lanes(c, n) + pltpu.roll(x, w - half, 1) * _tile_lanes(s1, n)
            + pltpu.roll(x, half, 1) * _tile_lanes(s2, n))


def _rope_t(dy, c, s1, s2, half):
    w = dy.shape[1]
    n = w // LANES
    return (dy * _tile_lanes(c, n) + pltpu.roll(dy * _tile_lanes(s1, n), half, 1)
            + pltpu.roll(dy * _tile_lanes(s2, n), w - half, 1))


def _fold_slots(d):
    tiles = []
    for j in range(d.shape[1] // (2 * LANES)):
        even = d[:, 2 * j * LANES:(2 * j + 1) * LANES]
        odd = d[:, (2 * j + 1) * LANES:(2 * j + 2) * LANES]
        tiles.append(even + pltpu.roll(odd, A_HEAD_DIM, 1))
    return tiles[0] if len(tiles) == 1 else jnp.concatenate(tiles, axis=1)


def _spread_slots(c):
    low = lax.broadcasted_iota(jnp.int32, (c.shape[0], LANES), 1) < A_HEAD_DIM
    slots = []
    for j in range(c.shape[1] // LANES):
        tile = c[:, j * LANES:(j + 1) * LANES]
        slots += [jnp.where(low, tile, 0.0), jnp.where(low, pltpu.roll(tile, A_HEAD_DIM, 1), 0.0)]
    return jnp.concatenate(slots, axis=1)


def _sigmoid(x):
    return 1.0 / (1.0 + jnp.exp(-x))


_GELU_C = math.sqrt(2.0 / math.pi)


def _gelu_and_grad(x):
    a = _GELU_C + (_GELU_C * 0.044715) * (x * x)
    th = jnp.tanh(x * a)
    hx = 0.5 * x
    p1 = 1.0 + th
    gel = hx * p1
    dgel = 0.5 * p1 + (hx * (1.0 - th * th)) * (3.0 * a - 2.0 * _GELU_C)
    return gel, dgel


def _conv_taps(up, h6, h7):
    r1 = pltpu.roll(up, 1, 0)
    r2 = pltpu.roll(up, 2, 0)
    rows = lax.broadcasted_iota(jnp.int32, (8, up.shape[1]), 0)
    xm1 = jnp.concatenate([jnp.where(rows == 0, h7, r1[0:8]), r1[8:]], axis=0)
    xm2 = jnp.concatenate([jnp.where(rows == 0, h6, jnp.where(rows == 1, h7, r2[0:8])), r2[8:]], axis=0)
    return xm1, xm2


def _conv_taps_next(du, n0, n1):
    tm = du.shape[0]
    r1 = pltpu.roll(du, tm - 1, 0)
    r2 = pltpu.roll(du, tm - 2, 0)
    rows = lax.broadcasted_iota(jnp.int32, (8, du.shape[1]), 0)
    xp1 = jnp.concatenate([r1[:tm - 8], jnp.where(rows == 7, n0, r1[tm - 8:])], axis=0)
    xp2 = jnp.concatenate([r2[:tm - 8], jnp.where(rows == 6, n0, jnp.where(rows == 7, n1, r2[tm - 8:]))], axis=0)
    return xp1, xp2


def _row(tm, n):
    return pl.BlockSpec((tm, n), lambda i: (i, 0))


def _full(shape):
    nd = len(shape)
    return pl.BlockSpec(tuple(shape), lambda i: (0,) * nd)


def _resident(shape):
    nd = len(shape)
    return pl.BlockSpec(tuple(shape), lambda i: (0,) * nd, pipeline_mode=pl.Buffered(1))


def _heads(tm, h):
    return pl.BlockSpec((h, tm, LANES), lambda i: (0, i, 0))


def _rows_call(name, body, t_rows, tm, ins, outs, scratch=()):
    return pl.pallas_call(
        body, name=name, grid=(t_rows // tm,),
        in_specs=[s for _, s in ins],
        out_specs=[s for _, s in outs],
        out_shape=[s for s, _ in outs],
        scratch_shapes=list(scratch),
        compiler_params=pltpu.CompilerParams(dimension_semantics=("arbitrary",), vmem_limit_bytes=VMEM_LIMIT),
    )(*[a for a, _ in ins])


def _sds(shape, dtype):
    return jax.ShapeDtypeStruct(tuple(shape), dtype)


def _rope_consts():
    c = np.zeros((16, LANES), np.float32)
    lane = np.arange(LANES)
    inv_a = (ROPE_THETA ** (-(np.arange(0, A_HEAD_DIM, 2, dtype=np.float32) / A_HEAD_DIM))).astype(np.float32)
    in_a = lane < A_HEAD_DIM
    c[0, in_a] = inv_a[lane[in_a] % (A_HEAD_DIM // 2)]
    c[1, in_a] = 1.0
    c[2, lane < A_HEAD_DIM // 2] = -1.0
    c[3, (lane >= A_HEAD_DIM // 2) & in_a] = 1.0
    inv_b = (ROPE_THETA ** (-(np.arange(0, ROPE_DIM, 2, dtype=np.float32) / ROPE_DIM))).astype(np.float32)
    pe = (lane >= NOPE_DIM) & (lane < NOPE_DIM + ROPE_DIM)
    c[5, pe] = inv_b[(lane[pe] - NOPE_DIM) % (ROPE_DIM // 2)]
    c[6, pe] = 1.0
    c[7, (lane >= NOPE_DIM) & (lane < NOPE_DIM + ROPE_DIM // 2)] = -1.0
    c[8, (lane >= NOPE_DIM + ROPE_DIM // 2) & (lane < NOPE_DIM + ROPE_DIM)] = 1.0
    c[9, lane < NOPE_DIM] = 1.0
    c[10, pe] = 1.0
    return jnp.asarray(c)


def _rope_tables(pos_f, consts, tm):
    t_rows = pos_f.shape[0]

    def body(pos_ref, c_ref, ca, sa1, sa2, cb, sb1, sb2):
        ang = pos_ref[...] * (c_ref[0:1, :] + c_ref[5:6, :])
        cs, sn = jnp.cos(ang), jnp.sin(ang)
        for ref, row in ((ca, 1), (sa1, 2), (sa2, 3)):
            half = (cs if row == 1 else sn) * c_ref[row:row + 1, :]
            ref[...] = half + pltpu.roll(half, A_HEAD_DIM, 1)
        cb[...] = cs * c_ref[6:7, :] + c_ref[9:10, :]
        sb1[...] = sn * c_ref[7:8, :]
        sb2[...] = sn * c_ref[8:9, :]

    tab = (_sds((t_rows, LANES), F32), _row(tm, LANES))
    return _rows_call("rope_tables", body, t_rows, tm,
                      [(pos_f, _row(tm, 1)), (consts, _full(consts.shape))], [tab] * 6)


def _fwd_in(x, g1, win, bg, gq, gkv, wuq, wk, wv, eq, ek, tabs, tm):
    t_rows = x.shape[0]

    def body(x_ref, g1_ref, win_ref, bg_ref, gq_ref, gkv_ref, wuq_ref, wk_ref, wv_ref, eq_ref, ek_ref,
             ca, sa1, sa2, cb, sb1, sb2,
             h1_ref, qs_ref, ks_ref, vs_ref, cq_ref, cqn_ref, ckv_ref, ckvn_ref, qm_ref, km_ref, vm_ref, gate_ref):
        xn, _ = _rms_stats(x_ref[...])
        hb = (xn * g1_ref[...]).astype(BF16)
        h1_ref[...] = hb
        ta = (ca[...], sa1[...], sa2[...])
        tb = (cb[...], sb1[...], sb2[...])
        cq = _dot(hb, win_ref[:, Z_CQ:Z_CKV])
        ckv = _dot(hb, win_ref[:, Z_CKV:Z_KR])
        z_qa = _dot(hb, win_ref[:, Z_QA:Z_KA])
        z_ka = _dot(hb, win_ref[:, Z_KA:Z_VA])
        z_va = _dot(hb, win_ref[:, Z_VA:Z_CQ])
        z_kr = _dot(hb, win_ref[:, Z_KR:Z_GATE])
        cq_ref[...] = cq
        cqn, _ = _rms_stats(cq)
        cqb = (cqn * gq_ref[...]).astype(BF16)
        cqn_ref[...] = cqb
        ckv_ref[...] = ckv
        ckvn, _ = _rms_stats(ckv)
        ckvb = (ckvn * gkv_ref[...]).astype(BF16)
        ckvn_ref[...] = ckvb
        z_qm = _dot(cqb, wuq_ref[...])
        z_km = _dot(ckvb, wk_ref[...])
        z_vm = _dot(ckvb, wv_ref[...])
        z_gate = _dot(hb, win_ref[:, Z_GATE:ZW])
        qs_ref[...] = _dot((_rope(z_qa, *ta, A_HEAD_DIM // 2) * SCALE_A).astype(BF16), eq_ref[...]).astype(BF16)
        ks_ref[...] = _dot(_rope(z_ka, *ta, A_HEAD_DIM // 2).astype(BF16), ek_ref[...]).astype(BF16)
        vs_ref[...] = _dot(z_va.astype(BF16), ek_ref[...]).astype(BF16)
        qm_ref[...] = (_rope(z_qm, *tb, ROPE_DIM // 2) * SCALE_B).astype(BF16)
        km_ref[...] = (z_km + _tile_lanes(_rope(z_kr, *tb, ROPE_DIM // 2), HEADS)).astype(BF16)
        vm_ref[...] = z_vm.astype(BF16)
        gate_ref[...] = _sigmoid(z_gate + bg_ref[...]).astype(BF16)

    def o(n, dt):
        return (_sds((t_rows, n), dt), _row(tm, n))

    ins = [(x, _row(tm, D_MODEL)), (g1, _full(g1.shape)), (win, _resident(win.shape)), (bg, _full(bg.shape)),
           (gq, _full(gq.shape)), (gkv, _full(gkv.shape)), (wuq, _full(wuq.shape)), (wk, _full(wk.shape)),
           (wv, _full(wv.shape)), (eq, _full(eq.shape)), (ek, _full(ek.shape))] + [(t, _row(tm, LANES)) for t in tabs]
    outs = [o(1024, BF16), o(1024, BF16), o(256, BF16), o(256, BF16), o(256, F32), o(256, BF16), o(128, F32),
            o(128, BF16), o(1024, BF16), o(1024, BF16), o(1024, BF16), o(2048, BF16)]
    return _rows_call("fwd_in", body, t_rows, tm, ins, outs)


def _attn_tile(t_rows):
    return min(512, t_rows)


MLA_HEADS_PER_STEP = 4
MLA_FWD_HEADS_PER_STEP = 8


def _causal_pairs(nq, by_kv):
    if by_kv:
        pairs = [(i, j) for j in range(nq) for i in range(j, nq)]
    else:
        pairs = [(i, j) for i in range(nq) for j in range(i + 1)]
    return (jnp.asarray([p[0] for p in pairs], jnp.int32), jnp.asarray([p[1] for p in pairs], jnp.int32))


def _mla_fwd(q, k, v):
    t_rows = q.shape[0]
    t = _attn_tile(t_rows)
    hp = MLA_FWD_HEADS_PER_STEP
    w = hp * LANES
    ii, jj = _causal_pairs(t_rows // t, by_kv=False)

    def body(i_ref, j_ref, q_ref, k_ref, v_ref, o_ref, lse_ref, m_s, l_s, acc_s):
        i = i_ref[pl.program_id(1)]
        j = j_ref[pl.program_id(1)]

        @pl.when(j == 0)
        def _():
            m_s[...] = jnp.full(m_s.shape, NEG, F32)
            l_s[...] = jnp.zeros(l_s.shape, F32)
            acc_s[...] = jnp.zeros(acc_s.shape, F32)

        def step(diagonal):
            sls = [slice(hh * LANES, (hh + 1) * LANES) for hh in range(hp)]
            scores = [_dot_nt(k_ref[:, sl], q_ref[:, sl]) for sl in sls]
            if diagonal:
                valid = (lax.broadcasted_iota(jnp.int32, (t, t), 0) <= lax.broadcasted_iota(jnp.int32, (t, t), 1))
                scores = [jnp.where(valid, s, NEG) for s in scores]
            stats = []
            for hh, s in enumerate(scores):
                m_prev = m_s[hh]
                m_new = jnp.maximum(m_prev, jnp.max(s, axis=0, keepdims=True))
                p = jnp.exp(s - m_new)
                alpha = jnp.exp(m_prev - m_new)
                stats.append((m_new, alpha, alpha * l_s[hh] + jnp.sum(p, axis=0, keepdims=True), p.astype(BF16)))
            for hh, (m_new, alpha, l_new, p) in enumerate(stats):
                sl = sls[hh]
                acc = alpha * acc_s[hh] + _dot_tn(v_ref[:, sl], p)
                if diagonal:
                    o_ref[:, sl] = (acc / l_new).T.astype(o_ref.dtype)
                    lse_ref[hh] = m_new + jnp.log(l_new)
                else:
                    m_s[hh] = m_new
                    l_s[hh] = l_new
                    acc_s[hh] = acc

        pl.when(j < i)(lambda: step(False))
        pl.when(j == i)(lambda: step(True))

    grid_spec = pltpu.PrefetchScalarGridSpec(
        num_scalar_prefetch=2, grid=(HEADS // hp, ii.shape[0]),
        in_specs=[pl.BlockSpec((t, w), lambda hb, s, ir, jr: (ir[s], hb)),
                  pl.BlockSpec((t, w), lambda hb, s, ir, jr: (jr[s], hb)),
                  pl.BlockSpec((t, w), lambda hb, s, ir, jr: (jr[s], hb))],
        out_specs=[pl.BlockSpec((t, w), lambda hb, s, ir, jr: (ir[s], hb)),
                   pl.BlockSpec((hp, 1, t), lambda hb, s, ir, jr: (hb, 0, ir[s]))],
        scratch_shapes=[pltpu.VMEM((hp, 1, t), F32), pltpu.VMEM((hp, 1, t), F32), pltpu.VMEM((hp, LANES, t), F32)])
    return pl.pallas_call(
        body, name="mla_fwd", grid_spec=grid_spec,
        out_shape=[_sds((t_rows, HEADS * LANES), BF16), _sds((HEADS, 1, t_rows), F32)],
        compiler_params=pltpu.CompilerParams(dimension_semantics=("arbitrary",) * 2, vmem_limit_bytes=VMEM_LIMIT),
    )(ii, jj, q, k, v)


def _mla_bwd(q, k, v, do, lse, delta):
    t_rows = q.shape[0]
    t = _attn_tile(t_rows)
    hp = MLA_HEADS_PER_STEP
    w = hp * LANES
    ii, jj = _causal_pairs(t_rows // t, by_kv=True)

    def body(i_ref, j_ref, q_ref, k_ref, v_ref, do_ref, lse_ref, dl_ref, dq_ref, dk_ref, dv_ref):
        i = i_ref[pl.program_id(1)]
        j = j_ref[pl.program_id(1)]

        @pl.when(pl.program_id(1) == 0)
        def _():
            dq_ref[...] = jnp.zeros(dq_ref.shape, F32)

        def step(diagonal):
            r0 = pl.multiple_of(i * t, t)
            sls = [slice(hh * LANES, (hh + 1) * LANES) for hh in range(hp)]
            scores = [_dot_nt(k_ref[:, sl], q_ref[:, sl]) for sl in sls]
            if diagonal:
                valid = (lax.broadcasted_iota(jnp.int32, (t, t), 0) <= lax.broadcasted_iota(jnp.int32, (t, t), 1))
                scores = [jnp.where(valid, s, NEG) for s in scores]
            dps = [_dot_nt(v_ref[:, sl], do_ref[:, sl]) for sl in sls]
            ps = [jnp.exp(s - lse_ref[hh]) for hh, s in enumerate(scores)]
            dss = [(p * (dp - dl_ref[hh])).astype(BF16) for hh, (p, dp) in enumerate(zip(ps, dps))]
            for hh, sl in enumerate(sls):
                dv = _dot(ps[hh].astype(BF16), do_ref[:, sl])
                dk = _dot(dss[hh], q_ref[:, sl])
                if diagonal:
                    dv_ref[:, sl] = dv
                    dk_ref[:, sl] = dk
                else:
                    dv_ref[:, sl] += dv
                    dk_ref[:, sl] += dk
                dq_ref[hh, pl.ds(r0, t), :] += _dot_tn(dss[hh], k_ref[:, sl])

        pl.when(i > j)(lambda: step(False))
        pl.when(i == j)(lambda: step(True))

    def qmap(hb, s, ir, jr):
        return (ir[s], hb)

    def kvmap(hb, s, ir, jr):
        return (jr[s], hb)

    def rowmap(hb, s, ir, jr):
        return (hb, 0, ir[s])

    grid_spec = pltpu.PrefetchScalarGridSpec(
        num_scalar_prefetch=2, grid=(HEADS // hp, ii.shape[0]),
        in_specs=[pl.BlockSpec((t, w), qmap), pl.BlockSpec((t, w), kvmap), pl.BlockSpec((t, w), kvmap),
                  pl.BlockSpec((t, w), qmap), pl.BlockSpec((hp, 1, t), rowmap), pl.BlockSpec((hp, 1, t), rowmap)],
        out_specs=[pl.BlockSpec((hp, t_rows, LANES), lambda hb, s, ir, jr: (hb, 0, 0)),
                   pl.BlockSpec((t, w), kvmap), pl.BlockSpec((t, w), kvmap)])
    return pl.pallas_call(
        body, name="mla_bwd", grid_spec=grid_spec,
        out_shape=[_sds((HEADS, t_rows, LANES), F32), _sds((t_rows, HEADS * LANES), F32),
                   _sds((t_rows, HEADS * LANES), F32)],
        compiler_params=pltpu.CompilerParams(dimension_semantics=("arbitrary",) * 2, vmem_limit_bytes=VMEM_LIMIT),
    )(ii, jj, q, k, v, do, lse, delta)


SWA_TILE = 2 * SWA_WINDOW
SWA_GROUP = HEADS // A_KV_HEADS


def _swa_bias(tq):
    koff = lax.broadcasted_iota(jnp.int32, (tq + SWA_WINDOW, SWA_GROUP * tq), 0) - SWA_WINDOW
    qoff = (lax.broadcasted_iota(jnp.int32, (tq + SWA_WINDOW, SWA_GROUP * tq), 1) % tq)
    band = (koff <= qoff) & (qoff - koff < SWA_WINDOW)
    return jnp.stack([jnp.where(band & (koff >= 0), 0.0, NEG), jnp.where(band, 0.0, NEG)]).astype(F32)


def _swa_specs(tq, nq):
    wb = tq // SWA_WINDOW
    kvw = A_KV_HEADS * LANES

    def qi(i):
        return jnp.minimum(i, nq - 1)

    q = pl.BlockSpec((tq, HEADS * LANES), lambda i: (qi(i), 0))
    cur = pl.BlockSpec((tq, kvw), lambda i: (qi(i), 0))
    prev = pl.BlockSpec((SWA_WINDOW, kvw), lambda i: (jnp.maximum(qi(i) * wb - 1, 0), 0))
    bias = pl.BlockSpec((1, tq + SWA_WINDOW, SWA_GROUP * tq), lambda i: (jnp.minimum(i, 1), 0, 0))
    rows = pl.BlockSpec((A_KV_HEADS, 1, 1, SWA_GROUP * tq), lambda i: (0, qi(i), 0, 0))
    sink = pl.BlockSpec((A_KV_HEADS, 1, SWA_GROUP * tq), lambda i: (0, 0, 0))
    return q, cur, prev, bias, rows, sink


def _stack_heads(ref, kvh):
    base = kvh * SWA_GROUP
    return jnp.concatenate([ref[:, (base + g) * LANES:(base + g + 1) * LANES] for g in range(SWA_GROUP)], axis=0)


def _unstack_heads(ref, kvh, val, tq):
    base = kvh * SWA_GROUP
    for g in range(SWA_GROUP):
        ref[:, (base + g) * LANES:(base + g + 1) * LANES] = val[g * tq:(g + 1) * tq].astype(ref.dtype)


def _kv_window(prev_ref, cur_ref, kvh):
    sl = slice(kvh * LANES, (kvh + 1) * LANES)
    return jnp.concatenate([prev_ref[:, sl], cur_ref[:, sl]], axis=0)


def _swa_fwd(q, k, v, bias, sink_rows):
    t_rows = q.shape[0]
    tq = min(SWA_TILE, t_rows)
    nq = t_rows // tq
    qs_, cur, prev, bs, rows, sk = _swa_specs(tq, nq)
    kvhs = range(A_KV_HEADS)

    def body(q_ref, kc_ref, kp_ref, vc_ref, vp_ref, b_ref, sink_ref, o_ref, lse_ref):
        scores = [_dot_nt(_kv_window(kp_ref, kc_ref, h), _stack_heads(q_ref, h)) + b_ref[0] for h in kvhs]
        stats = []
        for h, s in zip(kvhs, scores):
            sink = sink_ref[h]
            m = jnp.maximum(jnp.max(s, axis=0, keepdims=True), sink)
            p = jnp.exp(s - m)
            l = jnp.sum(p, axis=0, keepdims=True) + jnp.exp(sink - m)
            lse_ref[h, 0] = m + jnp.log(l)
            stats.append((p.astype(BF16), l))
        for h, (p, l) in zip(kvhs, stats):
            _unstack_heads(o_ref, h, (_dot_tn(_kv_window(vp_ref, vc_ref, h), p) / l).T, tq)

    return pl.pallas_call(
        body, name="swa_fwd", grid=(nq,),
        in_specs=[qs_, cur, prev, cur, prev, bs, sk],
        out_specs=[qs_, rows],
        out_shape=[_sds((t_rows, HEADS * LANES), BF16), _sds((A_KV_HEADS, nq, 1, SWA_GROUP * tq), F32)],
        compiler_params=pltpu.CompilerParams(dimension_semantics=("arbitrary",), vmem_limit_bytes=VMEM_LIMIT),
    )(q, k, k, v, v, bias, sink_rows)


def _swa_bwd(q, k, v, o, do, lse, bias, sink_rows):
    t_rows = q.shape[0]
    tq = min(SWA_TILE, t_rows)
    nq = t_rows // tq
    qs_, cur, prev, bs, rows, sk = _swa_specs(tq, nq)
    hw = SWA_WINDOW
    kvhs = range(A_KV_HEADS)
    kvw = A_KV_HEADS * LANES

    def body(q_ref, kc_ref, kp_ref, vc_ref, vp_ref, o_ref, do_ref, lse_ref, b_ref, sink_ref,
             dq_ref, dk_ref, dv_ref, dsink_ref, ck, cv, dsa):
        i = pl.program_id(0)

        @pl.when(i == 0)
        def _():
            dsa[...] = jnp.zeros(dsa.shape, F32)

        @pl.when(i < nq)
        def _():
            qs = [_stack_heads(q_ref, h) for h in kvhs]
            dos = [_stack_heads(do_ref, h) for h in kvhs]
            kks = [_kv_window(kp_ref, kc_ref, h) for h in kvhs]
            scores = [_dot_nt(kks[h], qs[h]) for h in kvhs]
            dps = [_dot_nt(_kv_window(vp_ref, vc_ref, h), dos[h]) for h in kvhs]
            ps, dss = [], []
            for h in kvhs:
                lse = lse_ref[h, 0]
                p = jnp.exp(scores[h] + b_ref[0] - lse)
                delta = jnp.sum((_stack_heads(o_ref, h).astype(F32) * dos[h].astype(F32)).T, axis=0, keepdims=True)
                dsa[h] += -jnp.exp(sink_ref[h] - lse) * delta
                ps.append(p.astype(BF16))
                dss.append((p * (dps[h] - delta)).astype(BF16))
            for h in kvhs:
                sl = slice(h * LANES, (h + 1) * LANES)
                dv = _dot(ps[h], dos[h])
                dk = _dot(dss[h], qs[h])
                _unstack_heads(dq_ref, h, _dot_tn(dss[h], kks[h]), tq)

                @pl.when(i > 0)
                def _():
                    dk_ref[0:tq - hw, sl] = ck[0:tq - hw, sl]
                    dk_ref[tq - hw:tq, sl] = ck[tq - hw:tq, sl] + dk[0:hw]
                    dv_ref[0:tq - hw, sl] = cv[0:tq - hw, sl]
                    dv_ref[tq - hw:tq, sl] = cv[tq - hw:tq, sl] + dv[0:hw]

                ck[:, sl] = dk[hw:hw + tq]
                cv[:, sl] = dv[hw:hw + tq]

        @pl.when(i == nq)
        def _():
            dk_ref[...] = ck[...]
            dv_ref[...] = cv[...]
            dsink_ref[...] = jnp.zeros(dsink_ref.shape, F32)
            for h in kvhs:
                for g in range(SWA_GROUP):
                    tot = jnp.sum(dsa[h, :, g * tq:(g + 1) * tq], axis=1, keepdims=True)
                    dsink_ref[h, g:g + 1, :] = jnp.zeros((1, LANES), F32) + tot

    kv_out = pl.BlockSpec((tq, kvw), lambda i: (jnp.maximum(i - 1, 0), 0))
    return pl.pallas_call(
        body, name="swa_bwd", grid=(nq + 1,),
        in_specs=[qs_, cur, prev, cur, prev, qs_, qs_, rows, bs, sk],
        out_specs=[qs_, kv_out, kv_out, pl.BlockSpec((A_KV_HEADS, 8, LANES), lambda i: (0, 0, 0))],
        out_shape=[_sds((t_rows, HEADS * LANES), F32), _sds((t_rows, kvw), F32), _sds((t_rows, kvw), F32),
                   _sds((A_KV_HEADS, 8, LANES), F32)],
        scratch_shapes=[pltpu.VMEM((tq, kvw), F32), pltpu.VMEM((tq, kvw), F32),
                        pltpu.VMEM((A_KV_HEADS, 1, SWA_GROUP * tq), F32)],
        compiler_params=pltpu.CompilerParams(dimension_semantics=("arbitrary",), vmem_limit_bytes=VMEM_LIMIT),
    )(q, k, k, v, v, o, do, lse, bias, sink_rows)


def _fwd_mix(x, ya, yb, gate, wba, wbb, wout, g2, g3, tm):
    t_rows = x.shape[0]

    def body(x_ref, ya_ref, yb_ref, gate_ref, wba_ref, wbb_ref, wout_ref, g2_ref, g3_ref,
             pa_ref, pb_ref, mixed_ref, o_ref, x1_ref, h2_ref, yac_ref, ybc_ref):
        yac = _fold_slots(ya_ref[...].astype(F32)).astype(BF16)
        ybc = _fold_slots(yb_ref[...].astype(F32)).astype(BF16)
        yac_ref[...] = yac
        ybc_ref[...] = ybc
        pa = _dot(yac, wba_ref[...])
        pb = _dot(ybc, wbb_ref[...])
        pa_ref[...] = pa.astype(BF16)
        pb_ref[...] = pb.astype(BF16)
        mixed = (gate_ref[:, 0:D_MODEL].astype(F32) * pa
                 + gate_ref[:, D_MODEL:2 * D_MODEL].astype(F32) * pb).astype(BF16)
        mixed_ref[...] = mixed
        o = _dot(mixed, wout_ref[...])
        o_ref[...] = o
        on, _ = _rms_stats(o)
        x1 = x_ref[...] + on * g2_ref[...]
        x1_ref[...] = x1
        x1n, _ = _rms_stats(x1)
        h2_ref[...] = (x1n * g3_ref[...]).astype(BF16)

    def o_(dt):
        return (_sds((t_rows, D_MODEL), dt), _row(tm, D_MODEL))

    ins = [(x, _row(tm, D_MODEL)), (ya, _row(tm, 1024)), (yb, _row(tm, 1024)), (gate, _row(tm, 2048)),
           (wba, _resident(wba.shape)), (wbb, _resident(wbb.shape)), (wout, _resident(wout.shape)),
           (g2, _full(g2.shape)), (g3, _full(g3.shape))]
    half = (_sds((t_rows, D_MODEL // 2), BF16), _row(tm, D_MODEL // 2))
    return _rows_call("fwd_mix", body, t_rows, tm, ins,
                      [o_(BF16), o_(BF16), o_(BF16), o_(F32), o_(F32), o_(BF16), half, half])


CONV_CHUNK = 1408


def _fwd_up(h2, wup, convw8, convb, tm):
    t_rows = h2.shape[0]
    cdim = 2 * D_FF

    def body(h2_ref, wup_ref, cw_ref, cb_ref, up_ref, a_ref, carry):
        i = pl.program_id(0)

        @pl.when(i == 0)
        def _():
            carry[...] = jnp.zeros(carry.shape, F32)

        hb = h2_ref[...]
        ups = [_dot(hb, wup_ref[s]) for s in range(cdim // CONV_CHUNK)]

        def conv(c0):
            sl = slice(c0, c0 + CONV_CHUNK)
            up = ups[c0 // CONV_CHUNK]
            up_ref[:, sl] = up
            xm1, xm2 = _conv_taps(up, carry[6:7, sl], carry[7:8, sl])
            u = cw_ref[0:1, sl] * xm2 + cw_ref[1:2, sl] * xm1 + cw_ref[2:3, sl] * up + cb_ref[:, sl]
            carry[:, sl] = up[tm - 8:tm, :]
            return u

        for c0 in range(0, D_FF, CONV_CHUNK):
            ug = conv(c0)
            uv = conv(D_FF + c0)
            gel, _ = _gelu_and_grad(ug)
            a_ref[:, c0:c0 + CONV_CHUNK] = (gel * uv).astype(BF16)

    ins = [(h2, _row(tm, D_MODEL)), (wup, _resident(wup.shape)), (convw8, _full(convw8.shape)), (convb, _full(convb.shape))]
    outs = [(_sds((t_rows, cdim), F32), _row(tm, cdim)), (_sds((t_rows, D_FF), BF16), _row(tm, D_FF))]
    return _rows_call("fwd_up", body, t_rows, tm, ins, outs, scratch=[pltpu.VMEM((8, cdim), F32)])


def _fwd_out(a, wdown, x1, g4, p, wple, g5, wpg, tgt, tm):
    t_rows = a.shape[0]

    def body(a_ref, wdown_ref, x1_ref, g4_ref, p_ref, wple_ref, g5_ref, wpg_ref, tgt_ref,
             ff_ref, x2_ref, e_ref, n5_ref, sg_ref, dx3_ref, loss_ref):
        i = pl.program_id(0)
        ff = _dot(a_ref[...], wdown_ref[...])
        e = _dot(p_ref[...].astype(BF16), wple_ref[...])
        ff_ref[...] = ff
        ffn, _ = _rms_stats(ff)
        x2 = x1_ref[...] + ffn * g4_ref[...]
        x2_ref[...] = x2
        e_ref[...] = e.astype(BF16)
        x2n, _ = _rms_stats(x2)
        n5 = (x2n * g5_ref[...]).astype(BF16)
        n5_ref[...] = n5
        sg = _sigmoid(_dot(n5, wpg_ref[...]))
        sg_ref[...] = sg.astype(BF16)
        d = x2 + sg * e - tgt_ref[...]
        dx3_ref[...] = d * (1.0 / D_MODEL)

        @pl.when(i == 0)
        def _():
            loss_ref[...] = jnp.zeros((1, 1), F32)

        loss_ref[...] += 0.5 * jnp.sum(jnp.sum(d * d, axis=1, keepdims=True), axis=0, keepdims=True) * (1.0 / D_MODEL)

    def o_(dt):
        return (_sds((t_rows, D_MODEL), dt), _row(tm, D_MODEL))

    ins = [(a, _row(tm, D_FF)), (wdown, _resident(wdown.shape)), (x1, _row(tm, D_MODEL)), (g4, _full(g4.shape)),
           (p, _row(tm, PLE_DIM)), (wple, _full(wple.shape)), (g5, _full(g5.shape)), (wpg, _resident(wpg.shape)),
           (tgt, _row(tm, D_MODEL))]
    outs = [o_(F32), o_(F32), o_(BF16), o_(BF16), o_(BF16), o_(F32), (_sds((1, 1), F32), _full((1, 1)))]
    return _rows_call("fwd_out", body, t_rows, tm, ins, outs)


def _bwd_out(dx3, e, sg, x2, ff, g5, g4, wpg, wdown, up, convw8, convb, tm):
    t_rows = dx3.shape[0]
    cdim = 2 * D_FF
    hb = tm // 8

    def body(dx3_ref, e_ref, sg_ref, x2_ref, ff_ref, g5_ref, g4_ref, wpg_ref, wdown_ref, up_ref, halo_ref, cw_ref,
             cb_ref, dpre_ref, de_ref, dx2_ref, dff_ref, du_ref, dg5_ref, dg4_ref, dcb_ref, dcw_ref):
        i = pl.program_id(0)

        @pl.when(i == 0)
        def _():
            dg5_ref[...] = jnp.zeros(dg5_ref.shape, F32)
            dg4_ref[...] = jnp.zeros(dg4_ref.shape, F32)
            dcb_ref[...] = jnp.zeros(dcb_ref.shape, F32)
            dcw_ref[...] = jnp.zeros(dcw_ref.shape, F32)

        dx3 = dx3_ref[...]
        sg = sg_ref[...].astype(F32)
        dpre = (dx3 * e_ref[...].astype(F32) * sg * (1.0 - sg)).astype(BF16)
        dpre_ref[...] = dpre
        de_ref[...] = (dx3 * sg).astype(BF16)
        dn5 = _dot_nt(dpre, wpg_ref[...])
        x2n, r5 = _rms_stats(x2_ref[...])
        d2, dg5 = _rms_bwd(dn5, x2n, r5, g5_ref[...])
        dx2 = dx3 + d2
        dx2_ref[...] = dx2
        dg5_ref[...] += dg5
        ffn, r4 = _rms_stats(ff_ref[...])
        dff, dg4 = _rms_bwd(dx2, ffn, r4, g4_ref[...])
        dg4_ref[...] += dg4
        dffb = dff.astype(BF16)
        dff_ref[...] = dffb
        keep = jnp.where(i > 0, 1.0, 0.0)

        def conv(c0):
            sl = slice(c0, c0 + CONV_CHUNK)
            up = up_ref[:, sl]
            xm1, xm2 = _conv_taps(up, halo_ref[6:7, sl] * keep, halo_ref[7:8, sl] * keep)
            u = cw_ref[0:1, sl] * xm2 + cw_ref[1:2, sl] * xm1 + cw_ref[2:3, sl] * up + cb_ref[:, sl]
            return u, up, xm1, xm2

        def grads(c0, du, up, xm1, xm2):
            sl = slice(c0, c0 + CONV_CHUNK)
            du_ref[:, sl] = du.astype(BF16)
            dcb_ref[:, sl] += jnp.sum(du, axis=0, keepdims=True)
            dcw_ref[0:1, sl] += jnp.sum(du * xm2, axis=0, keepdims=True)
            dcw_ref[1:2, sl] += jnp.sum(du * xm1, axis=0, keepdims=True)
            dcw_ref[2:3, sl] += jnp.sum(du * up, axis=0, keepdims=True)

        for c0 in range(0, D_FF, CONV_CHUNK):
            da = _dot_nt(dffb, wdown_ref[c0:c0 + CONV_CHUNK, :])
            ug, *rg = conv(c0)
            uv, *rv = conv(D_FF + c0)
            gel, dgel = _gelu_and_grad(ug)
            grads(c0, da * uv * dgel, *rg)
            grads(D_FF + c0, da * gel, *rv)

    def o_(n, dt):
        return (_sds((t_rows, n), dt), _row(tm, n))

    def acc(r, n):
        return (_sds((r, n), F32), _full((r, n)))

    halo = pl.BlockSpec((8, cdim), lambda i: (jnp.maximum(i * hb - 1, 0), 0))
    ins = [(dx3, _row(tm, D_MODEL)), (e, _row(tm, D_MODEL)), (sg, _row(tm, D_MODEL)), (x2, _row(tm, D_MODEL)),
           (ff, _row(tm, D_MODEL)), (g5, _full(g5.shape)), (g4, _full(g4.shape)), (wpg, _resident(wpg.shape)),
           (wdown, _resident(wdown.shape)), (up, _row(tm, cdim)), (up, halo), (convw8, _full(convw8.shape)),
           (convb, _full(convb.shape))]
    outs = [o_(D_MODEL, BF16), o_(D_MODEL, BF16), o_(D_MODEL, F32), o_(D_MODEL, BF16), o_(cdim, BF16),
            acc(1, D_MODEL), acc(1, D_MODEL), acc(1, cdim), acc(8, cdim)]
    return _rows_call("bwd_out", body, t_rows, tm, ins, outs)


def _bwd_mid(du, convw8, wup, dx2, x1, g3, o, g2, wout, gate, pa, pb, wba, wbb, yb, tm):
    t_rows = du.shape[0]
    cdim = 2 * D_FF
    halo_rows = 16
    hb = tm // halo_rows
    last_blk = t_rows // halo_rows - 1
    n_tiles = t_rows // tm

    def body(du_ref, halo_ref, cw_ref, wup_ref, dx2_ref, x1_ref, g3_ref, o_ref, g2_ref, wout_ref, gate_ref, pa_ref,
             pb_ref, wba_ref, wbb_ref, yb_ref,
             dup_ref, dx1_ref, do_ref, dpa_ref, dpb_ref, dgt_ref, dya_ref, dyb_ref, dl_ref, dg3_ref, dg2_ref, dbg_ref):
        i = pl.program_id(0)

        @pl.when(i == 0)
        def _():
            dg3_ref[...] = jnp.zeros(dg3_ref.shape, F32)
            dg2_ref[...] = jnp.zeros(dg2_ref.shape, F32)
            dbg_ref[...] = jnp.zeros(dbg_ref.shape, F32)

        keep = jnp.where(i < n_tiles - 1, 1.0, 0.0)
        dh2 = jnp.zeros((tm, D_MODEL), F32)
        dups = []
        for c0 in range(0, cdim, CONV_CHUNK):
            sl = slice(c0, c0 + CONV_CHUNK)
            du = du_ref[:, sl].astype(F32)
            nxt = halo_ref[:, sl].astype(F32)
            xp1, xp2 = _conv_taps_next(du, nxt[0:1] * keep, nxt[1:2] * keep)
            dups.append((cw_ref[2:3, sl] * du + cw_ref[1:2, sl] * xp1 + cw_ref[0:1, sl] * xp2).astype(BF16))
            dup_ref[:, sl] = dups[-1]
            if len(dups) > 1:
                dh2 = dh2 + _dot_nt(dups[-2], wup_ref[len(dups) - 2])
        dh2 = dh2 + _dot_nt(dups[-1], wup_ref[len(dups) - 1])
        x1n, r3 = _rms_stats(x1_ref[...])
        d1, dg3 = _rms_bwd(dh2, x1n, r3, g3_ref[...])
        dx1 = dx2_ref[...] + d1
        dx1_ref[...] = dx1
        dg3_ref[...] += dg3
        on, r2 = _rms_stats(o_ref[...])
        do, dg2 = _rms_bwd(dx1, on, r2, g2_ref[...])
        dg2_ref[...] += dg2
        dob = do.astype(BF16)
        do_ref[...] = dob
        dmixed = _dot_nt(dob, wout_ref[...])
        ga = gate_ref[:, 0:D_MODEL].astype(F32)
        gb = gate_ref[:, D_MODEL:2 * D_MODEL].astype(F32)
        dpa = (dmixed * ga).astype(BF16)
        dpb = (dmixed * gb).astype(BF16)
        dpa_ref[...] = dpa
        dpb_ref[...] = dpb
        dga = dmixed * pa_ref[...].astype(F32) * ga * (1.0 - ga)
        dgb = dmixed * pb_ref[...].astype(F32) * gb * (1.0 - gb)
        dgt_ref[:, 0:D_MODEL] = dga.astype(BF16)
        dgt_ref[:, D_MODEL:2 * D_MODEL] = dgb.astype(BF16)
        dbg_ref[:, 0:D_MODEL] += jnp.sum(dga, axis=0, keepdims=True)
        dbg_ref[:, D_MODEL:2 * D_MODEL] += jnp.sum(dgb, axis=0, keepdims=True)
        dya_ref[...] = _spread_slots(_dot_nt(dpa, wba_ref[...])).astype(BF16)
        dyb = _dot_nt(dpb, wbb_ref[...]).astype(BF16)
        dyb_ref[...] = _spread_slots(dyb.astype(F32)).astype(BF16)
        prod = yb_ref[...].astype(F32) * dyb.astype(F32)
        width = HEADS * V_DIM
        lane_head = lax.broadcasted_iota(jnp.int32, (HEADS, width), 1) // V_DIM
        sel = (lane_head == lax.broadcasted_iota(jnp.int32, (HEADS, width), 0)).astype(BF16)
        hi = prod.astype(BF16)
        lo = (prod - hi.astype(F32)).astype(BF16)
        dl_ref[...] = _dot_nt(sel, hi) + _dot_nt(sel, lo)

    def o_(n, dt):
        return (_sds((t_rows, n), dt), _row(tm, n))

    def acc(r, n):
        return (_sds((r, n), F32), _full((r, n)))

    halo = pl.BlockSpec((halo_rows, cdim), lambda i: (jnp.minimum((i + 1) * hb, last_blk), 0))
    ins = [(du, _row(tm, cdim)), (du, halo), (convw8, _full(convw8.shape)), (wup, _resident(wup.shape)),
           (dx2, _row(tm, D_MODEL)), (x1, _row(tm, D_MODEL)), (g3, _full(g3.shape)), (o, _row(tm, D_MODEL)),
           (g2, _full(g2.shape)), (wout, _resident(wout.shape)), (gate, _row(tm, 2048)), (pa, _row(tm, D_MODEL)),
           (pb, _row(tm, D_MODEL)), (wba, _resident(wba.shape)), (wbb, _resident(wbb.shape)), (yb, _row(tm, D_MODEL // 2))]
    outs = [o_(cdim, BF16), o_(D_MODEL, F32), o_(D_MODEL, BF16), o_(D_MODEL, BF16), o_(D_MODEL, BF16),
            o_(2048, BF16), o_(1024, BF16), o_(1024, BF16),
            (_sds((HEADS, t_rows), F32), pl.BlockSpec((HEADS, tm), lambda i: (0, i))),
            acc(1, D_MODEL), acc(1, D_MODEL), acc(1, 2048)]
    return _rows_call("bwd_mid", body, t_rows, tm, ins, outs)


def _bwd_in(dqs, dks, dvs, dqm, dkm, dvm, tabs, consts, cq, ckv, gq, gkv, wuq, wk, wv, dgates, win, x, g1, dx1, tm):
    t_rows = x.shape[0]

    def body(dqs_ref, dks_ref, dvs_ref, dqm_ref, dkm_ref, dvm_ref, ca, sa1, sa2, cb, sb1, sb2, c_ref, cq_ref,
             ckv_ref, gq_ref, gkv_ref, wuq_ref, wk_ref, wv_ref, dgt_ref, win_ref, x_ref, g1_ref, dx1_ref,
             dz_ref, dqb_ref, dx_ref, dgq_ref, dgkv_ref, dg1_ref):
        i = pl.program_id(0)

        @pl.when(i == 0)
        def _():
            dgq_ref[...] = jnp.zeros(dgq_ref.shape, F32)
            dgkv_ref[...] = jnp.zeros(dgkv_ref.shape, F32)
            dg1_ref[...] = jnp.zeros(dg1_ref.shape, F32)

        ta = (ca[...], sa1[...], sa2[...])
        tb = (cb[...], sb1[...], sb2[...])

        def piece(lo, hi, val):
            dz_ref[:, lo:hi] = val
            return _dot_nt(val, win_ref[:, lo:hi])

        dh1 = piece(Z_GATE, ZW, dgt_ref[...])
        dkm = dkm_ref[...]
        dckvn = _dot_nt(dkm.astype(BF16), wk_ref[...]) + _dot_nt(dvm_ref[...].astype(BF16), wv_ref[...])
        dh1 = dh1 + piece(Z_VA, Z_CQ, _fold_slots(dvs_ref[...]).astype(BF16))
        dqm = jnp.concatenate([dqm_ref[h] for h in range(HEADS)], axis=1)
        dqb = _rope_t(dqm * SCALE_B, *tb, ROPE_DIM // 2).astype(BF16)
        dqb_ref[...] = dqb
        dcqn = _dot_nt(dqb, wuq_ref[...])
        dqa = _rope_t(_fold_slots(dqs_ref[...]) * SCALE_A, *ta, A_HEAD_DIM // 2)
        dh1 = dh1 + piece(Z_QA, Z_KA, dqa.astype(BF16))
        dh1 = dh1 + piece(Z_KA, Z_VA, _rope_t(_fold_slots(dks_ref[...]), *ta, A_HEAD_DIM // 2).astype(BF16))
        ckvn, rkv = _rms_stats(ckv_ref[...])
        dckv, dgkv = _rms_bwd(dckvn, ckvn, rkv, gkv_ref[...])
        dgkv_ref[...] += dgkv
        dh1 = dh1 + piece(Z_CKV, Z_KR, dckv.astype(BF16))
        dslot = dkm[:, 0:LANES]
        for h in range(1, HEADS):
            dslot = dslot + dkm[:, h * LANES:(h + 1) * LANES]
        dh1 = dh1 + piece(Z_KR, Z_GATE, _rope_t(dslot * c_ref[10:11, :], *tb, ROPE_DIM // 2).astype(BF16))
        cqn, rq = _rms_stats(cq_ref[...])
        dcq, dgq = _rms_bwd(dcqn, cqn, rq, gq_ref[...])
        dgq_ref[...] += dgq
        dh1 = dh1 + piece(Z_CQ, Z_CKV, dcq.astype(BF16))
        xn, r1 = _rms_stats(x_ref[...])
        d0, dg1 = _rms_bwd(dh1, xn, r1, g1_ref[...])
        dg1_ref[...] += dg1
        dx_ref[...] = dx1_ref[...] + d0

    def acc(n):
        return (_sds((1, n), F32), _full((1, n)))

    ins = [(dqs, _row(tm, 1024)), (dks, _row(tm, 256)), (dvs, _row(tm, 256)), (dqm, _heads(tm, HEADS)),
           (dkm, _row(tm, 1024)), (dvm, _row(tm, 1024))] + [(t, _row(tm, LANES)) for t in tabs] + [
           (consts, _full(consts.shape)), (cq, _row(tm, 256)), (ckv, _row(tm, 128)), (gq, _full(gq.shape)),
           (gkv, _full(gkv.shape)), (wuq, _full(wuq.shape)), (wk, _full(wk.shape)), (wv, _full(wv.shape)),
           (dgates, _row(tm, 2048)), (win, _resident(win.shape)), (x, _row(tm, D_MODEL)), (g1, _full(g1.shape)),
           (dx1, _row(tm, D_MODEL))]
    outs = [(_sds((t_rows, ZW), BF16), _row(tm, ZW)), (_sds((t_rows, 1024), BF16), _row(tm, 1024)),
            (_sds((t_rows, D_MODEL), F32), _row(tm, D_MODEL)), acc(256), acc(128), acc(D_MODEL)]
    return _rows_call("bwd_in", body, t_rows, tm, ins, outs)


def _pick_cols(n):
    best = LANES
    for d in range(LANES, min(n, 1664) + 1, LANES):
        if n % d == 0:
            best = d
    return best


def _mm_tn(name, a, b, column_shards=1, after=None):
    t_rows, m = a.shape
    n = b.shape[1]
    bk = min(2048, t_rows)
    bm, bn = _pick_cols(m), _pick_cols(n // column_shards)
    per_shard = n // column_shards // bn
    extra = () if after is None else (after,)

    def body(a_ref, b_ref, *rest):
        o_ref = rest[-1]

        @pl.when(pl.program_id(2) == 0)
        def _():
            o_ref[...] = jnp.zeros((bm, bn), F32)

        o_ref[...] += _dot_tn(a_ref[...].astype(BF16), b_ref[...].astype(BF16))

    return pl.pallas_call(
        body, name=name, grid=(m // bm, n // bn, t_rows // bk),
        in_specs=[pl.BlockSpec((bk, bm), lambda i, j, k: (k, i)), pl.BlockSpec((bk, bn), lambda i, j, k: (k, j))]
        + [pl.BlockSpec((8, LANES), lambda i, j, k: (0, 0))] * len(extra),
        out_specs=(pl.BlockSpec((bm, bn), lambda i, j, k: (i, j)) if column_shards == 1 else
                   pl.BlockSpec((None, bm, bn), lambda i, j, k: (j // per_shard, i, j % per_shard))),
        out_shape=_sds((m, n) if column_shards == 1 else (column_shards, m, n // column_shards), F32),
        compiler_params=pltpu.CompilerParams(dimension_semantics=("arbitrary",) * 3, vmem_limit_bytes=VMEM_LIMIT),
    )(a, b, *extra)


PACK_ROWS = 512


ADD_TILE_ELEMS = 1 << 17


def _add_rows(rows, cols):
    best = 16
    for d in range(16, rows + 1, 16):
        if rows % d == 0 and d * cols <= ADD_TILE_ELEMS:
            best = d
    assert rows % best == 0
    return best


def _add_pair(name, g, recv, half):
    _, _, rows, cols = g.shape
    t = _add_rows(rows, cols)

    def body(h_ref, g_ref, r_ref, o_ref):
        o_ref[...] = (g_ref[:, 0] + r_ref[...]).astype(BF16)

    spec = pl.BlockSpec((4, t, cols), lambda i, h: (0, i, 0))
    grid_spec = pltpu.PrefetchScalarGridSpec(
        num_scalar_prefetch=1, grid=(rows // t,),
        in_specs=[pl.BlockSpec((4, 1, t, cols), lambda i, h: (0, h[0], i, 0)), spec], out_specs=spec)
    return pl.pallas_call(body, name=name, grid_spec=grid_spec,
                          out_shape=_sds(recv.shape, BF16))(jnp.reshape(half, (1,)).astype(jnp.int32), g, recv)


def _add_chips(name, parts):
    _, rows, cols = parts.shape
    t = _add_rows(rows, cols)

    def body(p_ref, o_ref):
        acc = p_ref[0].astype(F32)
        for j in range(1, 4):
            acc = acc + p_ref[j].astype(F32)
        o_ref[...] = acc

    return pl.pallas_call(body, name=name, grid=(rows // t,),
                          in_specs=[pl.BlockSpec((4, t, cols), lambda i: (0, i, 0))],
                          out_specs=pl.BlockSpec((t, cols), lambda i: (i, 0)),
                          out_shape=_sds((rows, cols), F32))(parts)


def _add_devices(parts):
    n, rows, _ = parts.shape

    def body(p_ref, o_ref):
        acc = p_ref[0]
        for j in range(1, n):
            acc = acc + p_ref[j]
        o_ref[...] = acc

    return pl.pallas_call(body, name="small_add", grid=(1,),
                          in_specs=[pl.BlockSpec((n, rows, LANES), lambda i: (0, 0, 0))],
                          out_specs=pl.BlockSpec((rows, LANES), lambda i: (0, 0)),
                          out_shape=_sds((rows, LANES), F32))(parts)


def _adam_rows(k, n):
    target = max(8, (1 << 20) // (4 * n))
    if k <= target:
        return k
    best = None
    for d in range(8, target + 1, 8):
        if k % d == 0:
            best = d
    return best if best is not None else k


def _adam_update(w, g, m, v):
    m_ = ADAM_B1 * m + (1.0 - ADAM_B1) * g
    v_ = ADAM_B2 * v + (1.0 - ADAM_B2) * (g * g)
    delta = -ADAM_LR * ((m_ / (1.0 - ADAM_B1 ** ADAM_STEP)) / (jnp.sqrt(v_ / (1.0 - ADAM_B2 ** ADAM_STEP)) + ADAM_EPS)
                        + ADAM_WD * w)
    return delta, m_, v_


def _adamw(name, w, g, m, v):
    k, n = w.shape
    bk = _adam_rows(k, n)

    def body(w_ref, g_ref, m_ref, v_ref, d_ref, mo_ref, vo_ref):
        d_ref[...], mo_ref[...], vo_ref[...] = _adam_update(w_ref[...], g_ref[...], m_ref[...], v_ref[...])

    spec = pl.BlockSpec((bk, n), lambda i: (i, 0))
    out = pl.pallas_call(body, name=name, grid=(k // bk,), in_specs=[spec] * 4, out_specs=[spec] * 3,
                         out_shape=[_sds((k, n), F32)] * 3,
                         compiler_params=pltpu.CompilerParams(vmem_limit_bytes=VMEM_LIMIT))(w, g, m, v)
    return (g, *out)


def _adamw_halves(name, w, mine, theirs, m, v, half):
    k, n = w.shape
    bk = _adam_rows(k // 2, n)
    nb = k // 2 // bk

    def body(h_ref, w_ref, mine_ref, theirs_ref, m_ref, v_ref, g_ref, d_ref, mo_ref, vo_ref):
        g = jnp.where(pl.program_id(0) == h_ref[0], mine_ref[...], theirs_ref[...])
        g_ref[...] = g
        d_ref[...], mo_ref[...], vo_ref[...] = _adam_update(w_ref[...], g, m_ref[...], v_ref[...])

    full = pl.BlockSpec((bk, n), lambda h, i, c: (h * nb + i, 0))
    part = pl.BlockSpec((bk, n), lambda h, i, c: (i, 0))
    grid_spec = pltpu.PrefetchScalarGridSpec(num_scalar_prefetch=1, grid=(2, nb),
                                             in_specs=[full, part, part, full, full], out_specs=[full] * 4)
    return tuple(pl.pallas_call(
        body, name=name, grid_spec=grid_spec, out_shape=[_sds((k, n), F32)] * 4,
        compiler_params=pltpu.CompilerParams(vmem_limit_bytes=VMEM_LIMIT),
    )(jnp.reshape(half, (1,)).astype(jnp.int32), w, mine, theirs, m, v))


_HBM = pl.BlockSpec(memory_space=pltpu.HBM)


def _me():
    return lax.axis_index("x"), lax.axis_index("y"), lax.axis_index("c")


def _other_chips(x, y):
    return [(1 - x, y), (x, 1 - y), (1 - x, 1 - y)]


def _pass_to_sibling(zones):
    n = len(zones)

    def body(*refs):
        in_refs, out_refs = refs[:n], refs[n:2 * n]
        send_sems, recv_sems = refs[2 * n:]
        x, y, c = _me()
        sent = []
        for a, (in_ref, out_ref) in enumerate(zip(in_refs, out_refs)):
            for j, (cx, cy) in enumerate(_other_chips(x, y)):
                mine, theirs = (2 * cx + cy, c), (2 * cx + cy, 1 - c)
                sems = dict(send_sem=send_sems.at[3 * a + j], recv_sem=recv_sems.at[3 * a + j],
                            device_id=(x, y, 1 - c), device_id_type=MESH)
                sent.append((pltpu.make_async_remote_copy(src_ref=in_ref.at[mine], dst_ref=out_ref.at[mine], **sems),
                             pltpu.make_async_remote_copy(src_ref=in_ref.at[theirs], dst_ref=out_ref.at[theirs], **sems)))
        for send, _ in sent:
            send.start()
        for _, recv in sent:
            recv.wait_recv()
        for send, _ in sent:
            send.wait_send()

    return pl.pallas_call(
        body, name="pass_to_sibling", out_shape=[_sds(z.shape, z.dtype) for z in zones],
        in_specs=[_HBM] * n, out_specs=[_HBM] * n, input_output_aliases={i: i for i in range(n)},
        scratch_shapes=[pltpu.SemaphoreType.DMA((3 * n,)), pltpu.SemaphoreType.DMA((3 * n,))],
    )(*zones)


def _swap_sibling(name, vs, other_half=False):
    n = len(vs)

    def body(*refs):
        v_refs, out_refs = refs[:n], refs[n:2 * n]
        send_sems, recv_sems = refs[2 * n:]
        x, y, c = _me()
        cps = [pltpu.make_async_remote_copy(src_ref=v_ref.at[:, 1 - c] if other_half else v_ref, dst_ref=out_ref,
                                            send_sem=send_sems.at[a], recv_sem=recv_sems.at[a],
                                            device_id=(x, y, 1 - c), device_id_type=MESH)
               for a, (v_ref, out_ref) in enumerate(zip(v_refs, out_refs))]
        for cp in cps:
            cp.start()
        for cp in cps:
            cp.wait()

    def landing(v):
        return _sds((v.shape[0],) + v.shape[2:] if other_half else v.shape, v.dtype)

    return pl.pallas_call(
        body, name=name, out_shape=[landing(v) for v in vs], in_specs=[_HBM] * n, out_specs=[_HBM] * n,
        scratch_shapes=[pltpu.SemaphoreType.DMA((n,)), pltpu.SemaphoreType.DMA((n,))],
    )(*vs)


_SEM = pl.BlockSpec(memory_space=pltpu.SEMAPHORE)
_EFFECT = pltpu.SideEffectType.DATAFLOW_SIDE_EFFECTING
WHOLE = "whole"
PIECE = "piece"
SIBLING_HALF = "sibling"
MY_HALF = "half"
EVERYONE = "everyone"
_COPIES = {WHOLE: 3, PIECE: 3, MY_HALF: 3, SIBLING_HALF: 1, EVERYONE: 7}


def _landing_shape(v, mode):
    return {WHOLE: (4,) + v.shape, MY_HALF: (4,) + v.shape, PIECE: v.shape, EVERYONE: (8,) + v.shape,
            SIBLING_HALF: (v.shape[0],) + v.shape[2:]}[mode]


def _chip_copies(v_ref, land_ref, send_sems, recv_sems, mode, sem0=0):
    x, y, c = _me()
    if mode == SIBLING_HALF:
        cp = pltpu.make_async_remote_copy(src_ref=v_ref.at[:, 1 - c], dst_ref=land_ref, send_sem=send_sems.at[sem0],
                                          recv_sem=recv_sems.at[sem0], device_id=(x, y, 1 - c), device_id_type=MESH)
        return [(cp, cp)]
    if mode == EVERYONE:
        out = []
        for f in range(1, 8):
            px, py, pc = (1 - x if f & 4 else x), (1 - y if f & 2 else y), (1 - c if f & 1 else c)
            sems = dict(send_sem=send_sems.at[sem0 + f - 1], recv_sem=recv_sems.at[sem0 + f - 1],
                        device_id=(px, py, pc), device_id_type=MESH)
            out.append((pltpu.make_async_remote_copy(src_ref=v_ref, dst_ref=land_ref.at[4 * x + 2 * y + c], **sems),
                        pltpu.make_async_remote_copy(src_ref=v_ref, dst_ref=land_ref.at[4 * px + 2 * py + pc], **sems)))
        return out
    k = 2 * x + y
    out = []
    for j, (cx, cy) in enumerate(_other_chips(x, y)):
        if mode == MY_HALF:
            src, mine, theirs = v_ref.at[c], land_ref.at[k, c], land_ref.at[2 * cx + cy, c]
        else:
            src = v_ref.at[2 * cx + cy] if mode == PIECE else v_ref
            mine, theirs = land_ref.at[k], land_ref.at[2 * cx + cy]
        sems = dict(send_sem=send_sems.at[sem0 + j], recv_sem=recv_sems.at[sem0 + j], device_id=(cx, cy, c),
                    device_id_type=MESH)
        send = pltpu.make_async_remote_copy(src_ref=src, dst_ref=mine, **sems)
        recv = pltpu.make_async_remote_copy(src_ref=src, dst_ref=theirs, **sems)
        out.append((send, recv))
    return out


def _chips_start(name, vs, mode, after=None):
    n = len(vs)
    lands = [_landing_shape(v, mode) for v in vs]

    def body(*refs):
        v_refs, land_refs = refs[:n], refs[n:2 * n]
        send_sems, recv_sems = refs[-2 * n - 3], refs[-2 * n - 2]
        token = refs[-1]
        for a in range(n):
            for send, _ in _chip_copies(v_refs[a], land_refs[a], send_sems, recv_sems, mode, _COPIES[mode] * a):
                send.start()
        token[...] = jnp.zeros_like(token)

    extra = () if after is None else (after,)
    hbm = [pltpu.with_memory_space_constraint(v, pltpu.HBM) for v in vs]
    zones = [pltpu.with_memory_space_constraint(lax.empty(s, v.dtype), pltpu.HBM) for s, v in zip(lands, vs)]
    out = pl.pallas_call(
        body, name=name,
        out_shape=(pltpu.SemaphoreType.DMA((_COPIES[mode] * n,)), pltpu.SemaphoreType.DMA((_COPIES[mode] * n,)),
                   *[pltpu.HBM(v.shape, v.dtype) for v in vs], *[pltpu.HBM(s, v.dtype) for s, v in zip(lands, vs)],
                   _sds((8, LANES), F32)),
        in_specs=(_HBM,) * (2 * n) + (pl.BlockSpec(memory_space=pl.ANY),) * len(extra),
        out_specs=(_SEM, _SEM) + (_HBM,) * (2 * n) + (pl.BlockSpec(memory_space=pltpu.VMEM),),
        input_output_aliases={i: 2 + i for i in range(2 * n)},
        compiler_params=pltpu.CompilerParams(has_side_effects=_EFFECT),
    )(*hbm, *zones, *extra)
    return out[0], out[1], list(out[2:2 + n]), list(out[2 + n:2 + 2 * n]), out[-1]


def _chips_wait(name, send_sems, recv_sems, v_thru, land_thru, mode, after):
    n = len(v_thru)

    def body(*refs):
        v_refs, land_refs = refs[:n], refs[n:2 * n]
        send_sems, recv_sems = refs[2 * n], refs[2 * n + 1]
        for a in range(n):
            for send, recv in _chip_copies(v_refs[a], land_refs[a], send_sems, recv_sems, mode, _COPIES[mode] * a):
                send.wait_send()
                recv.wait_recv()

    out = pl.pallas_call(
        body, name=name,
        out_shape=tuple(pltpu.HBM(a.shape, a.dtype) for a in list(v_thru) + list(land_thru)),
        in_specs=(_HBM,) * (2 * n) + (_SEM, _SEM, pl.BlockSpec(memory_space=pl.ANY)), out_specs=(_HBM,) * (2 * n),
        input_output_aliases={i: i for i in range(2 * n)},
        compiler_params=pltpu.CompilerParams(has_side_effects=_EFFECT),
    )(*v_thru, *land_thru, send_sems, recv_sems, after)
    return list(out[:n]), list(out[n:])


_BIG = (("w_in", (1024, 3232), 1), ("w_uq", (256, 768), 1), ("w_ukv", (128, 1024), 1), ("w_branch_a", (512, 1024), 1),
        ("w_branch_b", (512, 1024), 1), ("w_out", (1024, 1024), 0), ("w_up", (1024, 5632), 1),
        ("w_down", (2816, 1024), 0), ("w_ple_gate", (1024, 1024), 0), ("w_ple", (256, 1024), 1))


def _shard_shape(shape, axis):
    return (shape[0] // 4, shape[1]) if axis == 0 else (shape[0], shape[1] // 4)


def _half_rows(shape, axis):
    k, n = _shard_shape(shape, axis)
    return k * n // (2 * LANES)


_EARLY = ("w_in", "w_uq", "w_ukv")
_LATE = ("w_branch_a", "w_branch_b", "w_out", "w_up", "w_down", "w_ple_gate", "w_ple")
_NATURAL = ("w_in", "w_up", "w_down", "w_out", "w_ple_gate")
_EARLY_PACKED = tuple(b for b in _BIG if b[0] in _EARLY and b[0] not in _NATURAL)
_LATE_PACKED = tuple(b for b in _BIG if b[0] in _LATE and b[0] not in _NATURAL)


def _halves(a):
    return a.reshape(a.shape[:-2] + (2, a.shape[-2] // 2, a.shape[-1]))


def _rows_joined(a):
    return a.reshape(a.shape[:-3] + (a.shape[-3] * a.shape[-2], a.shape[-1]))


def _pack_pad(group):
    return -sum(_half_rows(shape, axis) for _, shape, axis in group) % PACK_ROWS


def _pack_shards(shards, dtype, group):
    parts = [shards[name].astype(dtype).reshape(2, _half_rows(shape, axis), LANES) for name, shape, axis in group]
    return jnp.concatenate(parts + [jnp.zeros((2, _pack_pad(group), LANES), dtype)], axis=1)


def _unpack_gathered(g, group):
    out, off = {}, 0
    for name, shape, axis in group:
        r = _half_rows(shape, axis)
        k, n = _shard_shape(shape, axis)
        w = g[:, :, off:off + r, :].reshape(4, k, n)
        out[name] = w.reshape(shape) if axis == 0 else w.transpose(1, 0, 2).reshape(shape)
        off += r
    return out


def _pack_grads(grads, group):
    parts = []
    for name, shape, axis in group:
        k, n = _shard_shape(shape, axis)
        g = grads[name]
        g4 = g.reshape(4, k, n) if axis == 0 else g.reshape(k, 4, n).transpose(1, 0, 2)
        parts.append(g4.reshape(4, 2, _half_rows(shape, axis), LANES))
    return jnp.concatenate(parts + [jnp.zeros((4, 2, _pack_pad(group), LANES), F32)], axis=2)


def _unpack_shard_grads(f, group):
    out, off = {}, 0
    for name, shape, axis in group:
        r = _half_rows(shape, axis)
        out[name] = f[:, off:off + r, :].reshape(_shard_shape(shape, axis))
        off += r
    return out


def _pad_slots(w, heads, dim):
    k = w.shape[0]
    return jnp.pad(w.reshape(k, heads, dim), ((0, 0), (0, 0), (0, LANES - dim))).reshape(k, heads * LANES)


def _unpad_slots(w, heads, dim):
    k = w.shape[0]
    return w.reshape(k, heads, LANES)[:, :, :dim].reshape(k, heads * dim)


def _pad_w_in(w):
    kr = jnp.pad(w[:, Z_KR:Z_KR + ROPE_DIM], ((0, 0), (NOPE_DIM, LANES - NOPE_DIM - ROPE_DIM)))
    return jnp.concatenate([w[:, :Z_KR], kr, w[:, Z_KR + ROPE_DIM:]], axis=1)


def _unpad_w_in(w):
    return jnp.concatenate([w[:, :Z_KR], w[:, Z_KR + NOPE_DIM:Z_KR + NOPE_DIM + ROPE_DIM], w[:, Z_GATE:ZW]], axis=1)


def _spread_matrix(heads, dim):
    row = lax.broadcasted_iota(jnp.int32, (heads * dim, heads * LANES), 0)
    col = lax.broadcasted_iota(jnp.int32, (heads * dim, heads * LANES), 1)
    return (col == (row // dim) * LANES + row % dim).astype(BF16)


_SMALL = (("attn_pre_norm", 1024), ("attn_post_norm", 1024), ("b_gate", 2048), ("sinks", 8), ("q_a_norm", 256),
          ("kv_a_norm", 128), ("mlp_pre_norm", 1024), ("mlp_post_norm", 1024), ("conv_b", 5632), ("ple_norm", 1024),
          ("conv_w", 3 * 5632), ("loss", 1))


def _small_rows(n):
    return 8 * -(-n // (8 * LANES))


def _pack_small(vals):
    parts = []
    for name, n in _SMALL:
        r = _small_rows(n)
        parts.append(jnp.pad(vals[name].reshape(-1), (0, r * LANES - n)).reshape(r, LANES))
    return jnp.concatenate(parts, axis=0)


def _unpack_small(buf):
    out, off = {}, 0
    for name, n in _SMALL:
        r = _small_rows(n)
        out[name] = buf[off:off + r].reshape(-1)[:n]
        off += r
    return out


def kernel(x, p, positions, attn_pre_norm, attn_post_norm, w_in, b_gate, sinks, q_a_norm, w_uq, kv_a_norm, w_ukv, w_branch_a, w_branch_b, w_out, mlp_pre_norm, mlp_post_norm, w_up, conv_w, conv_b, w_down, ple_norm, w_ple_gate, w_ple, loss_target, m_attn_pre_norm, m_attn_post_norm, m_w_in, m_b_gate, m_sinks, m_q_a_norm, m_w_uq, m_kv_a_norm, m_w_ukv, m_w_branch_a, m_w_branch_b, m_w_out, m_mlp_pre_norm, m_mlp_post_norm, m_w_up, m_conv_w, m_conv_b, m_w_down, m_ple_norm, m_w_ple_gate, m_w_ple, v_attn_pre_norm, v_attn_post_norm, v_w_in, v_b_gate, v_sinks, v_q_a_norm, v_w_uq, v_kv_a_norm, v_w_ukv, v_w_branch_a, v_w_branch_b, v_w_out, v_mlp_pre_norm, v_mlp_post_norm, v_w_up, v_conv_w, v_conv_b, v_w_down, v_ple_norm, v_w_ple_gate, v_w_ple):
    names = ["attn_pre_norm", "attn_post_norm", "w_in", "b_gate", "sinks", "q_a_norm", "w_uq", "kv_a_norm", "w_ukv",
             "w_branch_a", "w_branch_b", "w_out", "mlp_pre_norm", "mlp_post_norm", "w_up", "conv_w", "conv_b",
             "w_down", "ple_norm", "w_ple_gate", "w_ple"]
    wts = dict(zip(names, [attn_pre_norm, attn_post_norm, w_in, b_gate, sinks, q_a_norm, w_uq, kv_a_norm, w_ukv,
                           w_branch_a, w_branch_b, w_out, mlp_pre_norm, mlp_post_norm, w_up, conv_w, conv_b, w_down,
                           ple_norm, w_ple_gate, w_ple]))
    moms = dict(zip(names, [m_attn_pre_norm, m_attn_post_norm, m_w_in, m_b_gate, m_sinks, m_q_a_norm, m_w_uq,
                            m_kv_a_norm, m_w_ukv, m_w_branch_a, m_w_branch_b, m_w_out, m_mlp_pre_norm,
                            m_mlp_post_norm, m_w_up, m_conv_w, m_conv_b, m_w_down, m_ple_norm, m_w_ple_gate, m_w_ple]))
    vars_ = dict(zip(names, [v_attn_pre_norm, v_attn_post_norm, v_w_in, v_b_gate, v_sinks, v_q_a_norm, v_w_uq,
                             v_kv_a_norm, v_w_ukv, v_w_branch_a, v_w_branch_b, v_w_out, v_mlp_pre_norm,
                             v_mlp_post_norm, v_w_up, v_conv_w, v_conv_b, v_w_down, v_ple_norm, v_w_ple_gate, v_w_ple]))
    w2 = {n: a.reshape(a.shape[-2:]) for n, a in wts.items()}
    m2 = {n: a.reshape(a.shape[-2:]) for n, a in moms.items()}
    v2 = {n: a.reshape(a.shape[-2:]) for n, a in vars_.items()}

    t_rows = x.shape[-2]
    tm = min(256, t_rows)
    tm_wide = min(512, t_rows)
    xc, yc, cc = lax.axis_index("x"), lax.axis_index("y"), lax.axis_index("c")
    chip = 2 * xc + yc

    x2d = x.reshape(t_rows, D_MODEL)
    p2d = p.reshape(t_rows, PLE_DIM)
    tgt = loss_target.reshape(t_rows, D_MODEL)
    pos_f = positions.reshape(t_rows, 1).astype(F32)

    def own_slot_filled(gathered, mine):
        return [lax.dynamic_update_slice(g, m[None], (chip, 0, 0, 0)) for g, m in zip(gathered, mine)]

    def shard_lists(group, packed_group, token=0.0):
        ws = {n: w2[n] + token for n in group}
        return [_halves(ws[n].astype(BF16)) for n in group if n in _NATURAL] + [_pack_shards(ws, BF16, packed_group)]

    cw_rows = 3 * 1408 // LANES
    conv_mine = jnp.pad(w2["conv_w"].reshape(cw_rows, LANES), ((0, 48 - cw_rows), (0, 0))).reshape(2, 24, LANES)
    early_mine = shard_lists(_EARLY, _EARLY_PACKED) + [conv_mine]
    early_sems = _chips_start("gather_early_start", early_mine, MY_HALF)
    early_token = early_sems[4][0:1, 0:1]
    consts = _rope_consts()
    tabs = _rope_tables(pos_f + early_token, consts, tm)
    late_mine = shard_lists(_LATE, _LATE_PACKED, early_token)
    both_done = tabs[0][0:1, 0:1] + sum(m[0, 0:1, 0:1].astype(F32) for m in late_mine)
    early_sent, early_landed = _chips_wait("gather_early_wait", *early_sems[:4], MY_HALF, after=both_done)
    early = own_slot_filled(_pass_to_sibling(early_landed), early_sent)
    late_names = [n for n in _LATE if n in _NATURAL]
    first = [late_names.index("w_out"), len(late_names)]
    late_a = [late_mine[i] for i in first]
    late_b = [m for i, m in enumerate(late_mine) if i not in first]
    late_a_sems = _chips_start("gather_late_a_start", late_a, WHOLE, after=early[0])
    late_b_sems = _chips_start("gather_late_b_start", late_b, WHOLE, after=late_a_sems[4])
    late_token = late_b_sems[4][0:1, 0:1]
    full = _unpack_gathered(early[1], _EARLY_PACKED)
    full["w_in"] = _rows_joined(early[0]).transpose(1, 0, 2).reshape(D_MODEL, 3232)
    conv_full = early[2].reshape(4, 48, LANES)[:, :cw_rows].reshape(4, 3, 1408).transpose(1, 0, 2).reshape(3, 2 * D_FF)
    convw8 = jnp.pad(conv_full, ((0, 5), (0, 0)))

    win = _pad_w_in(full["w_in"])
    wuq = _pad_slots(full["w_uq"], HEADS, NOPE_DIM + ROPE_DIM)
    ukv = full["w_ukv"].reshape(KV_LORA, HEADS, NOPE_DIM + V_DIM)
    wk = _pad_slots(ukv[:, :, :NOPE_DIM].reshape(KV_LORA, HEADS * NOPE_DIM), HEADS, NOPE_DIM)
    wv = _pad_slots(ukv[:, :, NOPE_DIM:].reshape(KV_LORA, HEADS * V_DIM), HEADS, V_DIM)
    g1, g2, g3, g4, g5 = (w2["attn_pre_norm"], w2["attn_post_norm"], w2["mlp_pre_norm"], w2["mlp_post_norm"],
                          w2["ple_norm"])
    gq, gkv, bg, convb = w2["q_a_norm"], w2["kv_a_norm"], w2["b_gate"], w2["conv_b"]
    swa_tile = min(SWA_TILE, t_rows)
    sink_rows = jnp.repeat(w2["sinks"].reshape(A_KV_HEADS, SWA_GROUP, 1), swa_tile, axis=2).reshape(
        A_KV_HEADS, 1, SWA_GROUP * swa_tile)
    swa_bias = _swa_bias(swa_tile)
    spread_q = _spread_matrix(HEADS, A_HEAD_DIM)
    spread_kv = _spread_matrix(A_KV_HEADS, A_HEAD_DIM)

    h1, qs, ks, vs, cq, cqn, ckv, ckvn, qm, km, vm, gate = _fwd_in(x2d, g1, win, bg + late_token, gq, gkv, wuq, wk, wv,
                                                                   spread_q, spread_kv, tabs, tm_wide)
    ya, lse_a = _swa_fwd(qs, ks, vs, swa_bias, sink_rows)
    yb, lse_b = _mla_fwd(qm, km, vm)
    late_sent, late_landed = _chips_wait("gather_late_a_wait", *late_a_sems[:4], WHOLE, after=yb)
    wout_g, packed_g = own_slot_filled(late_landed, late_sent)
    full = _unpack_gathered(packed_g, _LATE_PACKED)
    wba, wbb = full["w_branch_a"], full["w_branch_b"]
    wple = full["w_ple"]
    wout = _rows_joined(wout_g).reshape(-1, D_MODEL)
    pa, pb, mixed, o, x1, h2, ya_c, yb_c = _fwd_mix(x2d, ya, yb, gate, wba, wbb, wout, g2, g3, tm_wide)
    late_sent, late_landed = _chips_wait("gather_late_b_wait", *late_b_sems[:4], WHOLE, after=pa)
    natural = dict(zip([n for n in late_names if n != "w_out"], own_slot_filled(late_landed, late_sent)))
    wup = _rows_joined(natural["w_up"])
    wdown, wpg = (_rows_joined(natural[n]).reshape(-1, D_MODEL) for n in ("w_down", "w_ple_gate"))
    up, a = _fwd_up(h2, wup, convw8, convb, tm)
    ff, x2, e, n5, sg, dx3, loss_part = _fwd_out(a, wdown, x1, g4, p2d, wple, g5, wpg, tgt, tm_wide)

    dpre, de, dx2, dff, du, dg5, dg4, dconvb, dconvw8 = _bwd_out(dx3, e, sg, x2, ff, g5, g4, wpg, wdown, up, convw8,
                                                                 convb, tm)
    dup, dx1, do, dpa, dpb, dgates, dya, dyb, delta_b, dg3, dg2, dbg = _bwd_mid(
        du, convw8, wup, dx2, x1, g3, o, g2, wout, gate, pa, pb, wba, wbb, yb_c, tm)
    late_grads = {
        "w_branch_a": _mm_tn("dw_branch_a", ya_c, dpa),
        "w_branch_b": _mm_tn("dw_branch_b", yb_c, dpb),
        "w_out": _mm_tn("dw_out", mixed, do).reshape(4, D_MODEL // 4, D_MODEL),
        "w_up": _mm_tn("dw_up", h2, dup, column_shards=4),
        "w_down": _mm_tn("dw_down", a, dff).reshape(4, D_FF // 4, D_MODEL),
        "w_ple_gate": _mm_tn("dw_ple_gate", n5, dpre).reshape(4, D_MODEL // 4, D_MODEL),
        "w_ple": _mm_tn("dw_ple", p2d, de),
    }

    def grad_views(grads, group, packed_group):
        return [_halves(grads[n]) for n in group if n in _NATURAL] + [_pack_grads(grads, packed_group)]

    def pair_sums(tag, views, theirs):
        return [_add_pair("rs_%s_add_pair_%d" % (tag, i), g, r, cc) for i, (g, r) in enumerate(zip(views, theirs))]

    swap_sems = _chips_start("swap_late_start", grad_views(late_grads, _LATE, _LATE_PACKED), SIBLING_HALF)
    dqs, dks, dvs, dsink_rows = _swa_bwd(qs, ks, vs, ya, dya, lse_a, swa_bias, sink_rows + swap_sems[4][0:1, 0:1])
    dsink = dsink_rows[:, 0:SWA_GROUP, 0]
    late_views, late_theirs = _chips_wait("swap_late_wait", *swap_sems[:4], SIBLING_HALF, after=dqs)
    rs_sems = _chips_start("scatter_late_start", pair_sums("late", late_views, late_theirs), PIECE)
    dqm, dkm, dvm = _mla_bwd(qm, km, vm, dyb, lse_b, delta_b.reshape(HEADS, 1, t_rows) + rs_sems[4][0:1, 0:1])
    dz, dqb, dx, dgq, dgkv, dg1 = _bwd_in(dqs, dks, dvs, dqm, dkm, dvm, tabs, consts, cq, ckv, gq, gkv, wuq, wk, wv,
                                           dgates, win, x2d, g1, dx1, tm)

    small = {"attn_pre_norm": dg1, "attn_post_norm": dg2, "b_gate": dbg, "sinks": dsink, "q_a_norm": dgq,
             "kv_a_norm": dgkv, "mlp_pre_norm": dg3, "mlp_post_norm": dg4, "conv_b": dconvb, "ple_norm": dg5,
             "conv_w": dconvw8[0:3], "loss": loss_part}
    small_sems = _chips_start("gather_small_start", [_pack_small(small)], EVERYONE)
    small_token = small_sems[4]

    dwk = _unpad_slots(_mm_tn("dw_k", ckvn, dkm, after=small_token), HEADS, NOPE_DIM).reshape(
        KV_LORA, HEADS, NOPE_DIM)
    dwv = _unpad_slots(_mm_tn("dw_v", ckvn, dvm, after=small_token), HEADS, V_DIM).reshape(KV_LORA, HEADS, V_DIM)
    early_grads = {
        "w_in": _unpad_w_in(_mm_tn("dw_in", h1, dz, after=small_token)).reshape(D_MODEL, 4, 808).transpose(1, 0, 2),
        "w_uq": _unpad_slots(_mm_tn("dw_uq", cqn, dqb, after=small_token), HEADS, NOPE_DIM + ROPE_DIM),
        "w_ukv": jnp.concatenate([dwk, dwv], axis=2).reshape(KV_LORA, HEADS * (NOPE_DIM + V_DIM)),
    }

    def finish(tag, pairs, landed, group, packed_group):
        reduced = []
        for i, (pair, land) in enumerate(zip(pairs, landed)):
            own = lax.dynamic_index_in_dim(pair, chip, 0, keepdims=True)
            reduced.append(_add_chips("rs_%s_add_chips_%d" % (tag, i),
                                      lax.dynamic_update_slice(land, own, (chip, 0, 0))))
        others = _swap_sibling("swap_%s_reduced_halves" % tag, reduced)
        r, o = reduced[-1], others[-1]
        packed = jnp.where(cc == 0, jnp.stack([r, o]), jnp.stack([o, r]))
        for n, g in _unpack_shard_grads(packed, packed_group).items():
            updates[n] = _adamw("adamw_" + n, w2[n], g, m2[n], v2[n])
        for n, r, o in zip([n for n in group if n in _NATURAL], reduced, others):
            updates[n] = _adamw_halves("adamw_" + n, w2[n], r, o, m2[n], v2[n], cc)

    updates = {}

    def adamw(n, g):
        updates[n] = _adamw("adamw_" + n, w2[n], g, m2[n], v2[n])

    early_views = grad_views(early_grads, _EARLY, _EARLY_PACKED)
    early_theirs = _swap_sibling("swap_early_grad_halves", early_views, other_half=True)
    small_sent, small_landed = _chips_wait("gather_small_wait", *small_sems[:4], EVERYONE, after=early_theirs[0])
    small_all = lax.dynamic_update_slice(small_landed[0], small_sent[0][None], (4 * xc + 2 * yc + cc, 0, 0))
    early_sems = _chips_start("scatter_early_start", pair_sums("early", early_views, early_theirs), PIECE,
                              after=small_all)
    late_pairs, late_landed = _chips_wait("scatter_late_wait", *rs_sems[:4], PIECE, after=early_sems[4])
    finish("late", late_pairs, late_landed, _LATE, _LATE_PACKED)
    early_pairs, early_landed = _chips_wait("scatter_early_wait", *early_sems[:4], PIECE,
                                            after=updates[_LATE[-1]][1])
    finish("early", early_pairs, early_landed, _EARLY, _EARLY_PACKED)

    small_sum = _unpack_small(_add_devices(small_all))
    for n in names:
        if n == "conv_w":
            adamw(n, lax.dynamic_index_in_dim(small_sum[n].reshape(3, 4, 1408), chip, 1, keepdims=False))
        elif n in small_sum:
            adamw(n, small_sum[n].reshape(w2[n].shape))
    loss = small_sum["loss"][0]

    outs = [[updates[n][i].reshape(wts[n].shape) for n in names] for i in range(4)]
    return (loss, dx.reshape(x.shape), *outs[0], *outs[1], *outs[2], *outs[3])
```

```python
import math

import numpy as np
import jax
import jax.numpy as jnp
from jax import lax
from jax.experimental import pallas as pl
from jax.experimental.pallas import tpu as pltpu

F32 = jnp.float32
BF16 = jnp.bfloat16

D_MODEL = 1024
D_FF = 2816
PLE_DIM = 256
ROPE_THETA = 10000.0
RMS_EPS = 1e-6
SWA_WINDOW = 128
HEADS = 8
A_KV_HEADS = 2
A_HEAD_DIM = 64
Q_LORA = 256
KV_LORA = 128
NOPE_DIM = 64
ROPE_DIM = 32
V_DIM = 64
LANES = 128
ZW = 3328
NEG = -1e30
SCALE_A = A_HEAD_DIM ** -0.5
SCALE_B = (NOPE_DIM + ROPE_DIM) ** -0.5

ADAM_LR = 0.001
ADAM_B1 = 0.9
ADAM_B2 = 0.999
ADAM_EPS = 1e-08
ADAM_WD = 0.01
ADAM_STEP = 10

VMEM_LIMIT = 60 * 1024 * 1024
MESH = pl.DeviceIdType.MESH

Z_QA, Z_KA, Z_VA, Z_CQ, Z_CKV, Z_KR, Z_GATE = 0, 512, 640, 768, 1024, 1152, 1280


def _dot(a, b):
    return jnp.dot(a, b, preferred_element_type=F32)


def _dot_nt(a, b):
    return lax.dot_general(a, b, (((1,), (1,)), ((), ())), preferred_element_type=F32)


def _dot_tn(a, b):
    return lax.dot_general(a, b, (((0,), (0,)), ((), ())), preferred_element_type=F32)


def _rms_stats(x):
    r = lax.rsqrt(jnp.mean(x * x, axis=-1, keepdims=True) + RMS_EPS)
    return x * r, r


def _rms_bwd(dy, xn, r, g):
    dxn = dy * g
    dx = r * (dxn - xn * jnp.mean(dxn * xn, axis=-1, keepdims=True))
    dg = jnp.sum(dy * xn, axis=0, keepdims=True)
    return dx, dg


def _tile_lanes(t, n):
    return t if n == 1 else jnp.concatenate([t] * n, axis=1)


def _rope(x, c, s1, s2, half):
    w = x.shape[1]
    n = w // LANES
    return (x * _tile_lanes(c, n) + pltpu.roll(x, w - half, 1) * _tile_lanes(s1, n)
            + pltpu.roll(x, half, 1) * _tile_lanes(s2, n))


def _rope_t(dy, c, s1, s2, half):
    w = dy.shape[1]
    n = w // LANES
    return (dy * _tile_lanes(c, n) + pltpu.roll(dy * _tile_lanes(s1, n), half, 1)
            + pltpu.roll(dy * _tile_lanes(s2, n), w - half, 1))


def _fold_slots(d):
    tiles = []
    for j in range(d.shape[1] // (2 * LANES)):
        even = d[:, 2 * j * LANES:(2 * j + 1) * LANES]
        odd = d[:, (2 * j + 1) * LANES:(2 * j + 2) * LANES]
        tiles.append(even + pltpu.roll(odd, A_HEAD_DIM, 1))
    return tiles[0] if len(tiles) == 1 else jnp.concatenate(tiles, axis=1)


def _spread_slots(c):
    low = lax.broadcasted_iota(jnp.int32, (c.shape[0], LANES), 1) < A_HEAD_DIM
    slots = []
    for j in range(c.shape[1] // LANES):
        tile = c[:, j * LANES:(j + 1) * LANES]
        slots += [jnp.where(low, tile, 0.0), jnp.where(low, pltpu.roll(tile, A_HEAD_DIM, 1), 0.0)]
    return jnp.concatenate(slots, axis=1)


def _sigmoid(x):
    return 1.0 / (1.0 + jnp.exp(-x))


_GELU_C = math.sqrt(2.0 / math.pi)


def _gelu_and_grad(x):
    a = _GELU_C + (_GELU_C * 0.044715) * (x * x)
    th = jnp.tanh(x * a)
    hx = 0.5 * x
    p1 = 1.0 + th
    gel = hx * p1
    dgel = 0.5 * p1 + (hx * (1.0 - th * th)) * (3.0 * a - 2.0 * _GELU_C)
    return gel, dgel


def _conv_taps(up, h6, h7):
    r1 = pltpu.roll(up, 1, 0)
    r2 = pltpu.roll(up, 2, 0)
    rows = lax.broadcasted_iota(jnp.int32, (8, up.shape[1]), 0)
    xm1 = jnp.concatenate([jnp.where(rows == 0, h7, r1[0:8]), r1[8:]], axis=0)
    xm2 = jnp.concatenate([jnp.where(rows == 0, h6, jnp.where(rows == 1, h7, r2[0:8])), r2[8:]], axis=0)
    return xm1, xm2


def _conv_taps_next(du, n0, n1):
    tm = du.shape[0]
    r1 = pltpu.roll(du, tm - 1, 0)
    r2 = pltpu.roll(du, tm - 2, 0)
    rows = lax.broadcasted_iota(jnp.int32, (8, du.shape[1]), 0)
    xp1 = jnp.concatenate([r1[:tm - 8], jnp.where(rows == 7, n0, r1[tm - 8:])], axis=0)
    xp2 = jnp.concatenate([r2[:tm - 8], jnp.where(rows == 6, n0, jnp.where(rows == 7, n1, r2[tm - 8:]))], axis=0)
    return xp1, xp2


def _row(tm, n):
    return pl.BlockSpec((tm, n), lambda i: (i, 0))


def _full(shape):
    nd = len(shape)
    return pl.BlockSpec(tuple(shape), lambda i: (0,) * nd)


def _resident(shape):
    nd = len(shape)
    return pl.BlockSpec(tuple(shape), lambda i: (0,) * nd, pipeline_mode=pl.Buffered(1))


def _heads(tm, h):
    return pl.BlockSpec((h, tm, LANES), lambda i: (0, i, 0))


def _rows_call(name, body, t_rows, tm, ins, outs, scratch=()):
    return pl.pallas_call(
        body, name=name, grid=(t_rows // tm,),
        in_specs=[s for _, s in ins],
        out_specs=[s for _, s in outs],
        out_shape=[s for s, _ in outs],
        scratch_shapes=list(scratch),
        compiler_params=pltpu.CompilerParams(dimension_semantics=("arbitrary",), vmem_limit_bytes=VMEM_LIMIT),
    )(*[a for a, _ in ins])


def _sds(shape, dtype):
    return jax.ShapeDtypeStruct(tuple(shape), dtype)


def _rope_consts():
    c = np.zeros((16, LANES), np.float32)
    lane = np.arange(LANES)
    inv_a = (ROPE_THETA ** (-(np.arange(0, A_HEAD_DIM, 2, dtype=np.float32) / A_HEAD_DIM))).astype(np.float32)
    in_a = lane < A_HEAD_DIM
    c[0, in_a] = inv_a[lane[in_a] % (A_HEAD_DIM // 2)]
    c[1, in_a] = 1.0
    c[2, lane < A_HEAD_DIM // 2] = -1.0
    c[3, (lane >= A_HEAD_DIM // 2) & in_a] = 1.0
    inv_b = (ROPE_THETA ** (-(np.arange(0, ROPE_DIM, 2, dtype=np.float32) / ROPE_DIM))).astype(np.float32)
    pe = (lane >= NOPE_DIM) & (lane < NOPE_DIM + ROPE_DIM)
    c[5, pe] = inv_b[(lane[pe] - NOPE_DIM) % (ROPE_DIM // 2)]
    c[6, pe] = 1.0
    c[7, (lane >= NOPE_DIM) & (lane < NOPE_DIM + ROPE_DIM // 2)] = -1.0
    c[8, (lane >= NOPE_DIM + ROPE_DIM // 2) & (lane < NOPE_DIM + ROPE_DIM)] = 1.0
    c[9, lane < NOPE_DIM] = 1.0
    c[10, pe] = 1.0
    return jnp.asarray(c)


def _rope_tables(pos_f, consts, tm):
    t_rows = pos_f.shape[0]

    def body(pos_ref, c_ref, ca, sa1, sa2, cb, sb1, sb2):
        ang = pos_ref[...] * (c_ref[0:1, :] + c_ref[5:6, :])
        cs, sn = jnp.cos(ang), jnp.sin(ang)
        for ref, row in ((ca, 1), (sa1, 2), (sa2, 3)):
            half = (cs if row == 1 else sn) * c_ref[row:row + 1, :]
            ref[...] = half + pltpu.roll(half, A_HEAD_DIM, 1)
        cb[...] = cs * c_ref[6:7, :] + c_ref[9:10, :]
        sb1[...] = sn * c_ref[7:8, :]
        sb2[...] = sn * c_ref[8:9, :]

    tab = (_sds((t_rows, LANES), F32), _row(tm, LANES))
    return _rows_call("rope_tables", body, t_rows, tm,
                      [(pos_f, _row(tm, 1)), (consts, _full(consts.shape))], [tab] * 6)


def _fwd_in(x, g1, win, bg, gq, gkv, wuq, wk, wv, eq, ek, tabs, tm):
    t_rows = x.shape[0]

    def body(x_ref, g1_ref, win_ref, bg_ref, gq_ref, gkv_ref, wuq_ref, wk_ref, wv_ref, eq_ref, ek_ref,
             ca, sa1, sa2, cb, sb1, sb2,
             h1_ref, qs_ref, ks_ref, vs_ref, cq_ref, cqn_ref, ckv_ref, ckvn_ref, qm_ref, km_ref, vm_ref, gate_ref):
        xn, _ = _rms_stats(x_ref[...])
        hb = (xn * g1_ref[...]).astype(BF16)
        h1_ref[...] = hb
        ta = (ca[...], sa1[...], sa2[...])
        tb = (cb[...], sb1[...], sb2[...])
        cq = _dot(hb, win_ref[:, Z_CQ:Z_CKV])
        ckv = _dot(hb, win_ref[:, Z_CKV:Z_KR])
        z_qa = _dot(hb, win_ref[:, Z_QA:Z_KA])
        z_ka = _dot(hb, win_ref[:, Z_KA:Z_VA])
        z_va = _dot(hb, win_ref[:, Z_VA:Z_CQ])
        z_kr = _dot(hb, win_ref[:, Z_KR:Z_GATE])
        cq_ref[...] = cq
        cqn, _ = _rms_stats(cq)
        cqb = (cqn * gq_ref[...]).astype(BF16)
        cqn_ref[...] = cqb
        ckv_ref[...] = ckv
        ckvn, _ = _rms_stats(ckv)
        ckvb = (ckvn * gkv_ref[...]).astype(BF16)
        ckvn_ref[...] = ckvb
        z_qm = _dot(cqb, wuq_ref[...])
        z_km = _dot(ckvb, wk_ref[...])
        z_vm = _dot(ckvb, wv_ref[...])
        z_gate = _dot(hb, win_ref[:, Z_GATE:ZW])
        qs_ref[...] = _dot((_rope(z_qa, *ta, A_HEAD_DIM // 2) * SCALE_A).astype(BF16), eq_ref[...]).astype(BF16)
        ks_ref[...] = _dot(_rope(z_ka, *ta, A_HEAD_DIM // 2).astype(BF16), ek_ref[...]).astype(BF16)
        vs_ref[...] = _dot(z_va.astype(BF16), ek_ref[...]).astype(BF16)
        qm_ref[...] = (_rope(z_qm, *tb, ROPE_DIM // 2) * SCALE_B).astype(BF16)
        km_ref[...] = (z_km + _tile_lanes(_rope(z_kr, *tb, ROPE_DIM // 2), HEADS)).astype(BF16)
        vm_ref[...] = z_vm.astype(BF16)
        gate_ref[...] = _sigmoid(z_gate + bg_ref[...]).astype(BF16)

    def o(n, dt):
        return (_sds((t_rows, n), dt), _row(tm, n))

    ins = [(x, _row(tm, D_MODEL)), (g1, _full(g1.shape)), (win, _resident(win.shape)), (bg, _full(bg.shape)),
           (gq, _full(gq.shape)), (gkv, _full(gkv.shape)), (wuq, _full(wuq.shape)), (wk, _full(wk.shape)),
           (wv, _full(wv.shape)), (eq, _full(eq.shape)), (ek, _full(ek.shape))] + [(t, _row(tm, LANES)) for t in tabs]
    outs = [o(1024, BF16), o(1024, BF16), o(256, BF16), o(256, BF16), o(256, F32), o(256, BF16), o(128, F32),
            o(128, BF16), o(1024, BF16), o(1024, BF16), o(1024, BF16), o(2048, BF16)]
    return _rows_call("fwd_in", body, t_rows, tm, ins, outs)


def _attn_tile(t_rows):
    return min(512, t_rows)


MLA_HEADS_PER_STEP = 4
MLA_FWD_HEADS_PER_STEP = 8


def _causal_pairs(nq, by_kv):
    if by_kv:
        pairs = [(i, j) for j in range(nq) for i in range(j, nq)]
    else:
        pairs = [(i, j) for i in range(nq) for j in range(i + 1)]
    return (jnp.asarray([p[0] for p in pairs], jnp.int32), jnp.asarray([p[1] for p in pairs], jnp.int32))


def _mla_fwd(q, k, v):
    t_rows = q.shape[0]
    t = _attn_tile(t_rows)
    hp = MLA_FWD_HEADS_PER_STEP
    w = hp * LANES
    ii, jj = _causal_pairs(t_rows // t, by_kv=False)

    def body(i_ref, j_ref, q_ref, k_ref, v_ref, o_ref, lse_ref, m_s, l_s, acc_s):
        i = i_ref[pl.program_id(1)]
        j = j_ref[pl.program_id(1)]

        @pl.when(j == 0)
        def _():
            m_s[...] = jnp.full(m_s.shape, NEG, F32)
            l_s[...] = jnp.zeros(l_s.shape, F32)
            acc_s[...] = jnp.zeros(acc_s.shape, F32)

        def step(diagonal):
            sls = [slice(hh * LANES, (hh + 1) * LANES) for hh in range(hp)]
            scores = [_dot_nt(k_ref[:, sl], q_ref[:, sl]) for sl in sls]
            if diagonal:
                valid = (lax.broadcasted_iota(jnp.int32, (t, t), 0) <= lax.broadcasted_iota(jnp.int32, (t, t), 1))
                scores = [jnp.where(valid, s, NEG) for s in scores]
            stats = []
            for hh, s in enumerate(scores):
                m_prev = m_s[hh]
                m_new = jnp.maximum(m_prev, jnp.max(s, axis=0, keepdims=True))
                p = jnp.exp(s - m_new)
                alpha = jnp.exp(m_prev - m_new)
                stats.append((m_new, alpha, alpha * l_s[hh] + jnp.sum(p, axis=0, keepdims=True), p.astype(BF16)))
            for hh, (m_new, alpha, l_new, p) in enumerate(stats):
                sl = sls[hh]
                acc = alpha * acc_s[hh] + _dot_tn(v_ref[:, sl], p)
                if diagonal:
                    o_ref[:, sl] = (acc / l_new).T.astype(o_ref.dtype)
                    lse_ref[hh] = m_new + jnp.log(l_new)
                else:
                    m_s[hh] = m_new
                    l_s[hh] = l_new
                    acc_s[hh] = acc

        pl.when(j < i)(lambda: step(False))
        pl.when(j == i)(lambda: step(True))

    grid_spec = pltpu.PrefetchScalarGridSpec(
        num_scalar_prefetch=2, grid=(HEADS // hp, ii.shape[0]),
        in_specs=[pl.BlockSpec((t, w), lambda hb, s, ir, jr: (ir[s], hb)),
                  pl.BlockSpec((t, w), lambda hb, s, ir, jr: (jr[s], hb)),
                  pl.BlockSpec((t, w), lambda hb, s, ir, jr: (jr[s], hb))],
        out_specs=[pl.BlockSpec((t, w), lambda hb, s, ir, jr: (ir[s], hb)),
                   pl.BlockSpec((hp, 1, t), lambda hb, s, ir, jr: (hb, 0, ir[s]))],
        scratch_shapes=[pltpu.VMEM((hp, 1, t), F32), pltpu.VMEM((hp, 1, t), F32), pltpu.VMEM((hp, LANES, t), F32)])
    return pl.pallas_call(
        body, name="mla_fwd", grid_spec=grid_spec,
        out_shape=[_sds((t_rows, HEADS * LANES), BF16), _sds((HEADS, 1, t_rows), F32)],
        compiler_params=pltpu.CompilerParams(dimension_semantics=("arbitrary",) * 2, vmem_limit_bytes=VMEM_LIMIT),
    )(ii, jj, q, k, v)


def _mla_bwd(q, k, v, do, lse, delta):
    t_rows = q.shape[0]
    t = _attn_tile(t_rows)
    hp = MLA_HEADS_PER_STEP
    w = hp * LANES
    ii, jj = _causal_pairs(t_rows // t, by_kv=True)

    def body(i_ref, j_ref, q_ref, k_ref, v_ref, do_ref, lse_ref, dl_ref, dq_ref, dk_ref, dv_ref):
        i = i_ref[pl.program_id(1)]
        j = j_ref[pl.program_id(1)]

        @pl.when(pl.program_id(1) == 0)
        def _():
            dq_ref[...] = jnp.zeros(dq_ref.shape, F32)

        def step(diagonal):
            r0 = pl.multiple_of(i * t, t)
            sls = [slice(hh * LANES, (hh + 1) * LANES) for hh in range(hp)]
            scores = [_dot_nt(k_ref[:, sl], q_ref[:, sl]) for sl in sls]
            if diagonal:
                valid = (lax.broadcasted_iota(jnp.int32, (t, t), 0) <= lax.broadcasted_iota(jnp.int32, (t, t), 1))
                scores = [jnp.where(valid, s, NEG) for s in scores]
            dps = [_dot_nt(v_ref[:, sl], do_ref[:, sl]) for sl in sls]
            ps = [jnp.exp(s - lse_ref[hh]) for hh, s in enumerate(scores)]
            dss = [(p * (dp - dl_ref[hh])).astype(BF16) for hh, (p, dp) in enumerate(zip(ps, dps))]
            for hh, sl in enumerate(sls):
                dv = _dot(ps[hh].astype(BF16), do_ref[:, sl])
                dk = _dot(dss[hh], q_ref[:, sl])
                if diagonal:
                    dv_ref[:, sl] = dv
                    dk_ref[:, sl] = dk
                else:
                    dv_ref[:, sl] += dv
                    dk_ref[:, sl] += dk
                dq_ref[hh, pl.ds(r0, t), :] += _dot_tn(dss[hh], k_ref[:, sl])

        pl.when(i > j)(lambda: step(False))
        pl.when(i == j)(lambda: step(True))

    def qmap(hb, s, ir, jr):
        return (ir[s], hb)

    def kvmap(hb, s, ir, jr):
        return (jr[s], hb)

    def rowmap(hb, s, ir, jr):
        return (hb, 0, ir[s])

    grid_spec = pltpu.PrefetchScalarGridSpec(
        num_scalar_prefetch=2, grid=(HEADS // hp, ii.shape[0]),
        in_specs=[pl.BlockSpec((t, w), qmap), pl.BlockSpec((t, w), kvmap), pl.BlockSpec((t, w), kvmap),
                  pl.BlockSpec((t, w), qmap), pl.BlockSpec((hp, 1, t), rowmap), pl.BlockSpec((hp, 1, t), rowmap)],
        out_specs=[pl.BlockSpec((hp, t_rows, LANES), lambda hb, s, ir, jr: (hb, 0, 0)),
                   pl.BlockSpec((t, w), kvmap), pl.BlockSpec((t, w), kvmap)])
    return pl.pallas_call(
        body, name="mla_bwd", grid_spec=grid_spec,
        out_shape=[_sds((HEADS, t_rows, LANES), F32), _sds((t_rows, HEADS * LANES), F32),
                   _sds((t_rows, HEADS * LANES), F32)],
        compiler_params=pltpu.CompilerParams(dimension_semantics=("arbitrary",) * 2, vmem_limit_bytes=VMEM_LIMIT),
    )(ii, jj, q, k, v, do, lse, delta)


SWA_TILE = 2 * SWA_WINDOW
SWA_GROUP = HEADS // A_KV_HEADS


def _swa_bias(tq):
    koff = lax.broadcasted_iota(jnp.int32, (tq + SWA_WINDOW, SWA_GROUP * tq), 0) - SWA_WINDOW
    qoff = (lax.broadcasted_iota(jnp.int32, (tq + SWA_WINDOW, SWA_GROUP * tq), 1) % tq)
    band = (koff <= qoff) & (qoff - koff < SWA_WINDOW)
    return jnp.stack([jnp.where(band & (koff >= 0), 0.0, NEG), jnp.where(band, 0.0, NEG)]).astype(F32)


def _swa_specs(tq, nq):
    wb = tq // SWA_WINDOW
    kvw = A_KV_HEADS * LANES

    def qi(i):
        return jnp.minimum(i, nq - 1)

    q = pl.BlockSpec((tq, HEADS * LANES), lambda i: (qi(i), 0))
    cur = pl.BlockSpec((tq, kvw), lambda i: (qi(i), 0))
    prev = pl.BlockSpec((SWA_WINDOW, kvw), lambda i: (jnp.maximum(qi(i) * wb - 1, 0), 0))
    bias = pl.BlockSpec((1, tq + SWA_WINDOW, SWA_GROUP * tq), lambda i: (jnp.minimum(i, 1), 0, 0))
    rows = pl.BlockSpec((A_KV_HEADS, 1, 1, SWA_GROUP * tq), lambda i: (0, qi(i), 0, 0))
    sink = pl.BlockSpec((A_KV_HEADS, 1, SWA_GROUP * tq), lambda i: (0, 0, 0))
    return q, cur, prev, bias, rows, sink


def _stack_heads(ref, kvh):
    base = kvh * SWA_GROUP
    return jnp.concatenate([ref[:, (base + g) * LANES:(base + g + 1) * LANES] for g in range(SWA_GROUP)], axis=0)


def _unstack_heads(ref, kvh, val, tq):
    base = kvh * SWA_GROUP
    for g in range(SWA_GROUP):
        ref[:, (base + g) * LANES:(base + g + 1) * LANES] = val[g * tq:(g + 1) * tq].astype(ref.dtype)


def _kv_window(prev_ref, cur_ref, kvh):
    sl = slice(kvh * LANES, (kvh + 1) * LANES)
    return jnp.concatenate([prev_ref[:, sl], cur_ref[:, sl]], axis=0)


def _swa_fwd(q, k, v, bias, sink_rows):
    t_rows = q.shape[0]
    tq = min(SWA_TILE, t_rows)
    nq = t_rows // tq
    qs_, cur, prev, bs, rows, sk = _swa_specs(tq, nq)
    kvhs = range(A_KV_HEADS)

    def body(q_ref, kc_ref, kp_ref, vc_ref, vp_ref, b_ref, sink_ref, o_ref, lse_ref):
        scores = [_dot_nt(_kv_window(kp_ref, kc_ref, h), _stack_heads(q_ref, h)) + b_ref[0] for h in kvhs]
        stats = []
        for h, s in zip(kvhs, scores):
            sink = sink_ref[h]
            m = jnp.maximum(jnp.max(s, axis=0, keepdims=True), sink)
            p = jnp.exp(s - m)
            l = jnp.sum(p, axis=0, keepdims=True) + jnp.exp(sink - m)
            lse_ref[h, 0] = m + jnp.log(l)
            stats.append((p.astype(BF16), l))
        for h, (p, l) in zip(kvhs, stats):
            _unstack_heads(o_ref, h, (_dot_tn(_kv_window(vp_ref, vc_ref, h), p) / l).T, tq)

    return pl.pallas_call(
        body, name="swa_fwd", grid=(nq,),
        in_specs=[qs_, cur, prev, cur, prev, bs, sk],
        out_specs=[qs_, rows],
        out_shape=[_sds((t_rows, HEADS * LANES), BF16), _sds((A_KV_HEADS, nq, 1, SWA_GROUP * tq), F32)],
        compiler_params=pltpu.CompilerParams(dimension_semantics=("arbitrary",), vmem_limit_bytes=VMEM_LIMIT),
    )(q, k, k, v, v, bias, sink_rows)


def _swa_bwd(q, k, v, o, do, lse, bias, sink_rows):
    t_rows = q.shape[0]
    tq = min(SWA_TILE, t_rows)
    nq = t_rows // tq
    qs_, cur, prev, bs, rows, sk = _swa_specs(tq, nq)
    hw = SWA_WINDOW
    kvhs = range(A_KV_HEADS)
    kvw = A_KV_HEADS * LANES

    def body(q_ref, kc_ref, kp_ref, vc_ref, vp_ref, o_ref, do_ref, lse_ref, b_ref, sink_ref,
             dq_ref, dk_ref, dv_ref, dsink_ref, ck, cv, dsa):
        i = pl.program_id(0)

        @pl.when(i == 0)
        def _():
            dsa[...] = jnp.zeros(dsa.shape, F32)

        @pl.when(i < nq)
        def _():
            qs = [_stack_heads(q_ref, h) for h in kvhs]
            dos = [_stack_heads(do_ref, h) for h in kvhs]
            kks = [_kv_window(kp_ref, kc_ref, h) for h in kvhs]
            scores = [_dot_nt(kks[h], qs[h]) for h in kvhs]
            dps = [_dot_nt(_kv_window(vp_ref, vc_ref, h), dos[h]) for h in kvhs]
            ps, dss = [], []
            for h in kvhs:
                lse = lse_ref[h, 0]
                p = jnp.exp(scores[h] + b_ref[0] - lse)
                delta = jnp.sum((_stack_heads(o_ref, h).astype(F32) * dos[h].astype(F32)).T, axis=0, keepdims=True)
                dsa[h] += -jnp.exp(sink_ref[h] - lse) * delta
                ps.append(p.astype(BF16))
                dss.append((p * (dps[h] - delta)).astype(BF16))
            for h in kvhs:
                sl = slice(h * LANES, (h + 1) * LANES)
                dv = _dot(ps[h], dos[h])
                dk = _dot(dss[h], qs[h])
                _unstack_heads(dq_ref, h, _dot_tn(dss[h], kks[h]), tq)

                @pl.when(i > 0)
                def _():
                    dk_ref[0:tq - hw, sl] = ck[0:tq - hw, sl]
                    dk_ref[tq - hw:tq, sl] = ck[tq - hw:tq, sl] + dk[0:hw]
                    dv_ref[0:tq - hw, sl] = cv[0:tq - hw, sl]
                    dv_ref[tq - hw:tq, sl] = cv[tq - hw:tq, sl] + dv[0:hw]

                ck[:, sl] = dk[hw:hw + tq]
                cv[:, sl] = dv[hw:hw + tq]

        @pl.when(i == nq)
        def _():
            dk_ref[...] = ck[...]
            dv_ref[...] = cv[...]
            dsink_ref[...] = jnp.zeros(dsink_ref.shape, F32)
            for h in kvhs:
                for g in range(SWA_GROUP):
                    tot = jnp.sum(dsa[h, :, g * tq:(g + 1) * tq], axis=1, keepdims=True)
                    dsink_ref[h, g:g + 1, :] = jnp.zeros((1, LANES), F32) + tot

    kv_out = pl.BlockSpec((tq, kvw), lambda i: (jnp.maximum(i - 1, 0), 0))
    return pl.pallas_call(
        body, name="swa_bwd", grid=(nq + 1,),
        in_specs=[qs_, cur, prev, cur, prev, qs_, qs_, rows, bs, sk],
        out_specs=[qs_, kv_out, kv_out, pl.BlockSpec((A_KV_HEADS, 8, LANES), lambda i: (0, 0, 0))],
        out_shape=[_sds((t_rows, HEADS * LANES), F32), _sds((t_rows, kvw), F32), _sds((t_rows, kvw), F32),
                   _sds((A_KV_HEADS, 8, LANES), F32)],
        scratch_shapes=[pltpu.VMEM((tq, kvw), F32), pltpu.VMEM((tq, kvw), F32),
                        pltpu.VMEM((A_KV_HEADS, 1, SWA_GROUP * tq), F32)],
        compiler_params=pltpu.CompilerParams(dimension_semantics=("arbitrary",), vmem_limit_bytes=VMEM_LIMIT),
    )(q, k, k, v, v, o, do, lse, bias, sink_rows)


def _fwd_mix(x, ya, yb, gate, wba, wbb, wout, g2, g3, tm):
    t_rows = x.shape[0]

    def body(x_ref, ya_ref, yb_ref, gate_ref, wba_ref, wbb_ref, wout_ref, g2_ref, g3_ref,
             pa_ref, pb_ref, mixed_ref, o_ref, x1_ref, h2_ref, yac_ref, ybc_ref):
        yac = _fold_slots(ya_ref[...].astype(F32)).astype(BF16)
        ybc = _fold_slots(yb_ref[...].astype(F32)).astype(BF16)
        yac_ref[...] = yac
        ybc_ref[...] = ybc
        pa = _dot(yac, wba_ref[...])
        pb = _dot(ybc, wbb_ref[...])
        pa_ref[...] = pa.astype(BF16)
        pb_ref[...] = pb.astype(BF16)
        mixed = (gate_ref[:, 0:D_MODEL].astype(F32) * pa
                 + gate_ref[:, D_MODEL:2 * D_MODEL].astype(F32) * pb).astype(BF16)
        mixed_ref[...] = mixed
        o = _dot(mixed, wout_ref[...])
        o_ref[...] = o
        on, _ = _rms_stats(o)
        x1 = x_ref[...] + on * g2_ref[...]
        x1_ref[...] = x1
        x1n, _ = _rms_stats(x1)
        h2_ref[...] = (x1n * g3_ref[...]).astype(BF16)

    def o_(dt):
        return (_sds((t_rows, D_MODEL), dt), _row(tm, D_MODEL))

    ins = [(x, _row(tm, D_MODEL)), (ya, _row(tm, 1024)), (yb, _row(tm, 1024)), (gate, _row(tm, 2048)),
           (wba, _resident(wba.shape)), (wbb, _resident(wbb.shape)), (wout, _resident(wout.shape)),
           (g2, _full(g2.shape)), (g3, _full(g3.shape))]
    half = (_sds((t_rows, D_MODEL // 2), BF16), _row(tm, D_MODEL // 2))
    return _rows_call("fwd_mix", body, t_rows, tm, ins,
                      [o_(BF16), o_(BF16), o_(BF16), o_(F32), o_(F32), o_(BF16), half, half])


CONV_CHUNK = 1408


def _fwd_up(h2, wup, convw8, convb, tm):
    t_rows = h2.shape[0]
    cdim = 2 * D_FF

    def body(h2_ref, wup_ref, cw_ref, cb_ref, up_ref, a_ref, carry):
        i = pl.program_id(0)

        @pl.when(i == 0)
        def _():
            carry[...] = jnp.zeros(carry.shape, F32)

        hb = h2_ref[...]
        ups = [_dot(hb, wup_ref[s]) for s in range(cdim // CONV_CHUNK)]

        def conv(c0):
            sl = slice(c0, c0 + CONV_CHUNK)
            up = ups[c0 // CONV_CHUNK]
            up_ref[:, sl] = up
            xm1, xm2 = _conv_taps(up, carry[6:7, sl], carry[7:8, sl])
            u = cw_ref[0:1, sl] * xm2 + cw_ref[1:2, sl] * xm1 + cw_ref[2:3, sl] * up + cb_ref[:, sl]
            carry[:, sl] = up[tm - 8:tm, :]
            return u

        for c0 in range(0, D_FF, CONV_CHUNK):
            ug = conv(c0)
            uv = conv(D_FF + c0)
            gel, _ = _gelu_and_grad(ug)
            a_ref[:, c0:c0 + CONV_CHUNK] = (gel * uv).astype(BF16)

    ins = [(h2, _row(tm, D_MODEL)), (wup, _resident(wup.shape)), (convw8, _full(convw8.shape)), (convb, _full(convb.shape))]
    outs = [(_sds((t_rows, cdim), F32), _row(tm, cdim)), (_sds((t_rows, D_FF), BF16), _row(tm, D_FF))]
    return _rows_call("fwd_up", body, t_rows, tm, ins, outs, scratch=[pltpu.VMEM((8, cdim), F32)])


def _fwd_out(a, wdown, x1, g4, p, wple, g5, wpg, tgt, tm):
    t_rows = a.shape[0]

    def body(a_ref, wdown_ref, x1_ref, g4_ref, p_ref, wple_ref, g5_ref, wpg_ref, tgt_ref,
             ff_ref, x2_ref, e_ref, n5_ref, sg_ref, dx3_ref, loss_ref):
        i = pl.program_id(0)
        ff = _dot(a_ref[...], wdown_ref[...])
        e = _dot(p_ref[...].astype(BF16), wple_ref[...])
        ff_ref[...] = ff
        ffn, _ = _rms_stats(ff)
        x2 = x1_ref[...] + ffn * g4_ref[...]
        x2_ref[...] = x2
        e_ref[...] = e.astype(BF16)
        x2n, _ = _rms_stats(x2)
        n5 = (x2n * g5_ref[...]).astype(BF16)
        n5_ref[...] = n5
        sg = _sigmoid(_dot(n5, wpg_ref[...]))
        sg_ref[...] = sg.astype(BF16)
        d = x2 + sg * e - tgt_ref[...]
        dx3_ref[...] = d * (1.0 / D_MODEL)

        @pl.when(i == 0)
        def _():
            loss_ref[...] = jnp.zeros((1, 1), F32)

        loss_ref[...] += 0.5 * jnp.sum(jnp.sum(d * d, axis=1, keepdims=True), axis=0, keepdims=True) * (1.0 / D_MODEL)

    def o_(dt):
        return (_sds((t_rows, D_MODEL), dt), _row(tm, D_MODEL))

    ins = [(a, _row(tm, D_FF)), (wdown, _resident(wdown.shape)), (x1, _row(tm, D_MODEL)), (g4, _full(g4.shape)),
           (p, _row(tm, PLE_DIM)), (wple, _full(wple.shape)), (g5, _full(g5.shape)), (wpg, _resident(wpg.shape)),
           (tgt, _row(tm, D_MODEL))]
    outs = [o_(F32), o_(F32), o_(BF16), o_(BF16), o_(BF16), o_(F32), (_sds((1, 1), F32), _full((1, 1)))]
    return _rows_call("fwd_out", body, t_rows, tm, ins, outs)


def _bwd_out(dx3, e, sg, x2, ff, g5, g4, wpg, wdown, up, convw8, convb, tm):
    t_rows = dx3.shape[0]
    cdim = 2 * D_FF
    hb = tm // 8

    def body(dx3_ref, e_ref, sg_ref, x2_ref, ff_ref, g5_ref, g4_ref, wpg_ref, wdown_ref, up_ref, halo_ref, cw_ref,
             cb_ref, dpre_ref, de_ref, dx2_ref, dff_ref, du_ref, dg5_ref, dg4_ref, dcb_ref, dcw_ref):
        i = pl.program_id(0)

        @pl.when(i == 0)
        def _():
            dg5_ref[...] = jnp.zeros(dg5_ref.shape, F32)
            dg4_ref[...] = jnp.zeros(dg4_ref.shape, F32)
            dcb_ref[...] = jnp.zeros(dcb_ref.shape, F32)
            dcw_ref[...] = jnp.zeros(dcw_ref.shape, F32)

        dx3 = dx3_ref[...]
        sg = sg_ref[...].astype(F32)
        dpre = (dx3 * e_ref[...].astype(F32) * sg * (1.0 - sg)).astype(BF16)
        dpre_ref[...] = dpre
        de_ref[...] = (dx3 * sg).astype(BF16)
        dn5 = _dot_nt(dpre, wpg_ref[...])
        x2n, r5 = _rms_stats(x2_ref[...])
        d2, dg5 = _rms_bwd(dn5, x2n, r5, g5_ref[...])
        dx2 = dx3 + d2
        dx2_ref[...] = dx2
        dg5_ref[...] += dg5
        ffn, r4 = _rms_stats(ff_ref[...])
        dff, dg4 = _rms_bwd(dx2, ffn, r4, g4_ref[...])
        dg4_ref[...] += dg4
        dffb = dff.astype(BF16)
        dff_ref[...] = dffb
        keep = jnp.where(i > 0, 1.0, 0.0)

        def conv(c0):
            sl = slice(c0, c0 + CONV_CHUNK)
            up = up_ref[:, sl]
            xm1, xm2 = _conv_taps(up, halo_ref[6:7, sl] * keep, halo_ref[7:8, sl] * keep)
            u = cw_ref[0:1, sl] * xm2 + cw_ref[1:2, sl] * xm1 + cw_ref[2:3, sl] * up + cb_ref[:, sl]
            return u, up, xm1, xm2

        def grads(c0, du, up, xm1, xm2):
            sl = slice(c0, c0 + CONV_CHUNK)
            du_ref[:, sl] = du.astype(BF16)
            dcb_ref[:, sl] += jnp.sum(du, axis=0, keepdims=True)
            dcw_ref[0:1, sl] += jnp.sum(du * xm2, axis=0, keepdims=True)
            dcw_ref[1:2, sl] += jnp.sum(du * xm1, axis=0, keepdims=True)
            dcw_ref[2:3, sl] += jnp.sum(du * up, axis=0, keepdims=True)

        for c0 in range(0, D_FF, CONV_CHUNK):
            da = _dot_nt(dffb, wdown_ref[c0:c0 + CONV_CHUNK, :])
            ug, *rg = conv(c0)
            uv, *rv = conv(D_FF + c0)
            gel, dgel = _gelu_and_grad(ug)
            grads(c0, da * uv * dgel, *rg)
            grads(D_FF + c0, da * gel, *rv)

    def o_(n, dt):
        return (_sds((t_rows, n), dt), _row(tm, n))

    def acc(r, n):
        return (_sds((r, n), F32), _full((r, n)))

    halo = pl.BlockSpec((8, cdim), lambda i: (jnp.maximum(i * hb - 1, 0), 0))
    ins = [(dx3, _row(tm, D_MODEL)), (e, _row(tm, D_MODEL)), (sg, _row(tm, D_MODEL)), (x2, _row(tm, D_MODEL)),
           (ff, _row(tm, D_MODEL)), (g5, _full(g5.shape)), (g4, _full(g4.shape)), (wpg, _resident(wpg.shape)),
           (wdown, _resident(wdown.shape)), (up, _row(tm, cdim)), (up, halo), (convw8, _full(convw8.shape)),
           (convb, _full(convb.shape))]
    outs = [o_(D_MODEL, BF16), o_(D_MODEL, BF16), o_(D_MODEL, F32), o_(D_MODEL, BF16), o_(cdim, BF16),
            acc(1, D_MODEL), acc(1, D_MODEL), acc(1, cdim), acc(8, cdim)]
    return _rows_call("bwd_out", body, t_rows, tm, ins, outs)


def _bwd_mid(du, convw8, wup, dx2, x1, g3, o, g2, wout, gate, pa, pb, wba, wbb, yb, tm):
    t_rows = du.shape[0]
    cdim = 2 * D_FF
    halo_rows = 16
    hb = tm // halo_rows
    last_blk = t_rows // halo_rows - 1
    n_tiles = t_rows // tm

    def body(du_ref, halo_ref, cw_ref, wup_ref, dx2_ref, x1_ref, g3_ref, o_ref, g2_ref, wout_ref, gate_ref, pa_ref,
             pb_ref, wba_ref, wbb_ref, yb_ref,
             dup_ref, dx1_ref, do_ref, dpa_ref, dpb_ref, dgt_ref, dya_ref, dyb_ref, dl_ref, dg3_ref, dg2_ref, dbg_ref):
        i = pl.program_id(0)

        @pl.when(i == 0)
        def _():
            dg3_ref[...] = jnp.zeros(dg3_ref.shape, F32)
            dg2_ref[...] = jnp.zeros(dg2_ref.shape, F32)
            dbg_ref[...] = jnp.zeros(dbg_ref.shape, F32)

        keep = jnp.where(i < n_tiles - 1, 1.0, 0.0)
        dh2 = jnp.zeros((tm, D_MODEL), F32)
        dups = []
        for c0 in range(0, cdim, CONV_CHUNK):
            sl = slice(c0, c0 + CONV_CHUNK)
            du = du_ref[:, sl].astype(F32)
            nxt = halo_ref[:, sl].astype(F32)
            xp1, xp2 = _conv_taps_next(du, nxt[0:1] * keep, nxt[1:2] * keep)
            dups.append((cw_ref[2:3, sl] * du + cw_ref[1:2, sl] * xp1 + cw_ref[0:1, sl] * xp2).astype(BF16))
            dup_ref[:, sl] = dups[-1]
            if len(dups) > 1:
                dh2 = dh2 + _dot_nt(dups[-2], wup_ref[len(dups) - 2])
        dh2 = dh2 + _dot_nt(dups[-1], wup_ref[len(dups) - 1])
        x1n, r3 = _rms_stats(x1_ref[...])
        d1, dg3 = _rms_bwd(dh2, x1n, r3, g3_ref[...])
        dx1 = dx2_ref[...] + d1
        dx1_ref[...] = dx1
        dg3_ref[...] += dg3
        on, r2 = _rms_stats(o_ref[...])
        do, dg2 = _rms_bwd(dx1, on, r2, g2_ref[...])
        dg2_ref[...] += dg2
        dob = do.astype(BF16)
        do_ref[...] = dob
        dmixed = _dot_nt(dob, wout_ref[...])
        ga = gate_ref[:, 0:D_MODEL].astype(F32)
        gb = gate_ref[:, D_MODEL:2 * D_MODEL].astype(F32)
        dpa = (dmixed * ga).astype(BF16)
        dpb = (dmixed * gb).astype(BF16)
        dpa_ref[...] = dpa
        dpb_ref[...] = dpb
        dga = dmixed * pa_ref[...].astype(F32) * ga * (1.0 - ga)
        dgb = dmixed * pb_ref[...].astype(F32) * gb * (1.0 - gb)
        dgt_ref[:, 0:D_MODEL] = dga.astype(BF16)
        dgt_ref[:, D_MODEL:2 * D_MODEL] = dgb.astype(BF16)
        dbg_ref[:, 0:D_MODEL] += jnp.sum(dga, axis=0, keepdims=True)
        dbg_ref[:, D_MODEL:2 * D_MODEL] += jnp.sum(dgb, axis=0, keepdims=True)
        dya_ref[...] = _spread_slots(_dot_nt(dpa, wba_ref[...])).astype(BF16)
        dyb = _dot_nt(dpb, wbb_ref[...]).astype(BF16)
        dyb_ref[...] = _spread_slots(dyb.astype(F32)).astype(BF16)
        prod = yb_ref[...].astype(F32) * dyb.astype(F32)
        width = HEADS * V_DIM
        lane_head = lax.broadcasted_iota(jnp.int32, (HEADS, width), 1) // V_DIM
        sel = (lane_head == lax.broadcasted_iota(jnp.int32, (HEADS, width), 0)).astype(BF16)
        hi = prod.astype(BF16)
        lo = (prod - hi.astype(F32)).astype(BF16)
        dl_ref[...] = _dot_nt(sel, hi) + _dot_nt(sel, lo)

    def o_(n, dt):
        return (_sds((t_rows, n), dt), _row(tm, n))

    def acc(r, n):
        return (_sds((r, n), F32), _full((r, n)))

    halo = pl.BlockSpec((halo_rows, cdim), lambda i: (jnp.minimum((i + 1) * hb, last_blk), 0))
    ins = [(du, _row(tm, cdim)), (du, halo), (convw8, _full(convw8.shape)), (wup, _resident(wup.shape)),
           (dx2, _row(tm, D_MODEL)), (x1, _row(tm, D_MODEL)), (g3, _full(g3.shape)), (o, _row(tm, D_MODEL)),
           (g2, _full(g2.shape)), (wout, _resident(wout.shape)), (gate, _row(tm, 2048)), (pa, _row(tm, D_MODEL)),
           (pb, _row(tm, D_MODEL)), (wba, _resident(wba.shape)), (wbb, _resident(wbb.shape)), (yb, _row(tm, D_MODEL // 2))]
    outs = [o_(cdim, BF16), o_(D_MODEL, F32), o_(D_MODEL, BF16), o_(D_MODEL, BF16), o_(D_MODEL, BF16),
            o_(2048, BF16), o_(1024, BF16), o_(1024, BF16),
            (_sds((HEADS, t_rows), F32), pl.BlockSpec((HEADS, tm), lambda i: (0, i))),
            acc(1, D_MODEL), acc(1, D_MODEL), acc(1, 2048)]
    return _rows_call("bwd_mid", body, t_rows, tm, ins, outs)


def _bwd_in(dqs, dks, dvs, dqm, dkm, dvm, tabs, consts, cq, ckv, gq, gkv, wuq, wk, wv, dgates, win, x, g1, dx1, tm):
    t_rows = x.shape[0]

    def body(dqs_ref, dks_ref, dvs_ref, dqm_ref, dkm_ref, dvm_ref, ca, sa1, sa2, cb, sb1, sb2, c_ref, cq_ref,
             ckv_ref, gq_ref, gkv_ref, wuq_ref, wk_ref, wv_ref, dgt_ref, win_ref, x_ref, g1_ref, dx1_ref,
             dz_ref, dqb_ref, dx_ref, dgq_ref, dgkv_ref, dg1_ref):
        i = pl.program_id(0)

        @pl.when(i == 0)
        def _():
            dgq_ref[...] = jnp.zeros(dgq_ref.shape, F32)
            dgkv_ref[...] = jnp.zeros(dgkv_ref.shape, F32)
            dg1_ref[...] = jnp.zeros(dg1_ref.shape, F32)

        ta = (ca[...], sa1[...], sa2[...])
        tb = (cb[...], sb1[...], sb2[...])

        def piece(lo, hi, val):
            dz_ref[:, lo:hi] = val
            return _dot_nt(val, win_ref[:, lo:hi])

        dh1 = piece(Z_GATE, ZW, dgt_ref[...])
        dkm = dkm_ref[...]
        dckvn = _dot_nt(dkm.astype(BF16), wk_ref[...]) + _dot_nt(dvm_ref[...].astype(BF16), wv_ref[...])
        dh1 = dh1 + piece(Z_VA, Z_CQ, _fold_slots(dvs_ref[...]).astype(BF16))
        dqm = jnp.concatenate([dqm_ref[h] for h in range(HEADS)], axis=1)
        dqb = _rope_t(dqm * SCALE_B, *tb, ROPE_DIM // 2).astype(BF16)
        dqb_ref[...] = dqb
        dcqn = _dot_nt(dqb, wuq_ref[...])
        dqa = _rope_t(_fold_slots(dqs_ref[...]) * SCALE_A, *ta, A_HEAD_DIM // 2)
        dh1 = dh1 + piece(Z_QA, Z_KA, dqa.astype(BF16))
        dh1 = dh1 + piece(Z_KA, Z_VA, _rope_t(_fold_slots(dks_ref[...]), *ta, A_HEAD_DIM // 2).astype(BF16))
        ckvn, rkv = _rms_stats(ckv_ref[...])
        dckv, dgkv = _rms_bwd(dckvn, ckvn, rkv, gkv_ref[...])
        dgkv_ref[...] += dgkv
        dh1 = dh1 + piece(Z_CKV, Z_KR, dckv.astype(BF16))
        dslot = dkm[:, 0:LANES]
        for h in range(1, HEADS):
            dslot = dslot + dkm[:, h * LANES:(h + 1) * LANES]
        dh1 = dh1 + piece(Z_KR, Z_GATE, _rope_t(dslot * c_ref[10:11, :], *tb, ROPE_DIM // 2).astype(BF16))
        cqn, rq = _rms_stats(cq_ref[...])
        dcq, dgq = _rms_bwd(dcqn, cqn, rq, gq_ref[...])
        dgq_ref[...] += dgq
        dh1 = dh1 + piece(Z_CQ, Z_CKV, dcq.astype(BF16))
        xn, r1 = _rms_stats(x_ref[...])
        d0, dg1 = _rms_bwd(dh1, xn, r1, g1_ref[...])
        dg1_ref[...] += dg1
        dx_ref[...] = dx1_ref[...] + d0

    def acc(n):
        return (_sds((1, n), F32), _full((1, n)))

    ins = [(dqs, _row(tm, 1024)), (dks, _row(tm, 256)), (dvs, _row(tm, 256)), (dqm, _heads(tm, HEADS)),
           (dkm, _row(tm, 1024)), (dvm, _row(tm, 1024))] + [(t, _row(tm, LANES)) for t in tabs] + [
           (consts, _full(consts.shape)), (cq, _row(tm, 256)), (ckv, _row(tm, 128)), (gq, _full(gq.shape)),
           (gkv, _full(gkv.shape)), (wuq, _full(wuq.shape)), (wk, _full(wk.shape)), (wv, _full(wv.shape)),
           (dgates, _row(tm, 2048)), (win, _resident(win.shape)), (x, _row(tm, D_MODEL)), (g1, _full(g1.shape)),
           (dx1, _row(tm, D_MODEL))]
    outs = [(_sds((t_rows, ZW), BF16), _row(tm, ZW)), (_sds((t_rows, 1024), BF16), _row(tm, 1024)),
            (_sds((t_rows, D_MODEL), F32), _row(tm, D_MODEL)), acc(256), acc(128), acc(D_MODEL)]
    return _rows_call("bwd_in", body, t_rows, tm, ins, outs)


def _pick_cols(n):
    best = LANES
    for d in range(LANES, min(n, 1664) + 1, LANES):
        if n % d == 0:
            best = d
    return best


def _mm_tn(name, a, b, column_shards=1, after=None):
    t_rows, m = a.shape
    n = b.shape[1]
    bk = min(2048, t_rows)
    bm, bn = _pick_cols(m), _pick_cols(n // column_shards)
    per_shard = n // column_shards // bn
    extra = () if after is None else (after,)

    def body(a_ref, b_ref, *rest):
        o_ref = rest[-1]

        @pl.when(pl.program_id(2) == 0)
        def _():
            o_ref[...] = jnp.zeros((bm, bn), F32)

        o_ref[...] += _dot_tn(a_ref[...].astype(BF16), b_ref[...].astype(BF16))

    return pl.pallas_call(
        body, name=name, grid=(m // bm, n // bn, t_rows // bk),
        in_specs=[pl.BlockSpec((bk, bm), lambda i, j, k: (k, i)), pl.BlockSpec((bk, bn), lambda i, j, k: (k, j))]
        + [pl.BlockSpec((8, LANES), lambda i, j, k: (0, 0))] * len(extra),
        out_specs=(pl.BlockSpec((bm, bn), lambda i, j, k: (i, j)) if column_shards == 1 else
                   pl.BlockSpec((None, bm, bn), lambda i, j, k: (j // per_shard, i, j % per_shard))),
        out_shape=_sds((m, n) if column_shards == 1 else (column_shards, m, n // column_shards), F32),
        compiler_params=pltpu.CompilerParams(dimension_semantics=("arbitrary",) * 3, vmem_limit_bytes=VMEM_LIMIT),
    )(a, b, *extra)


PACK_ROWS = 512


ADD_TILE_ELEMS = 1 << 17


def _add_rows(rows, cols):
    best = 16
    for d in range(16, rows + 1, 16):
        if rows % d == 0 and d * cols <= ADD_TILE_ELEMS:
            best = d
    assert rows % best == 0
    return best


def _add_pair(name, g, recv, half):
    _, _, rows, cols = g.shape
    t = _add_rows(rows, cols)

    def body(h_ref, g_ref, r_ref, o_ref):
        o_ref[...] = (g_ref[:, 0] + r_ref[...]).astype(BF16)

    spec = pl.BlockSpec((4, t, cols), lambda i, h: (0, i, 0))
    grid_spec = pltpu.PrefetchScalarGridSpec(
        num_scalar_prefetch=1, grid=(rows // t,),
        in_specs=[pl.BlockSpec((4, 1, t, cols), lambda i, h: (0, h[0], i, 0)), spec], out_specs=spec)
    return pl.pallas_call(body, name=name, grid_spec=grid_spec,
                          out_shape=_sds(recv.shape, BF16))(jnp.reshape(half, (1,)).astype(jnp.int32), g, recv)


def _add_chips(name, parts):
    _, rows, cols = parts.shape
    t = _add_rows(rows, cols)

    def body(p_ref, o_ref):
        acc = p_ref[0].astype(F32)
        for j in range(1, 4):
            acc = acc + p_ref[j].astype(F32)
        o_ref[...] = acc

    return pl.pallas_call(body, name=name, grid=(rows // t,),
                          in_specs=[pl.BlockSpec((4, t, cols), lambda i: (0, i, 0))],
                          out_specs=pl.BlockSpec((t, cols), lambda i: (i, 0)),
                          out_shape=_sds((rows, cols), F32))(parts)


def _add_devices(parts):
    n, rows, _ = parts.shape

    def body(p_ref, o_ref):
        acc = p_ref[0]
        for j in range(1, n):
            acc = acc + p_ref[j]
        o_ref[...] = acc

    return pl.pallas_call(body, name="small_add", grid=(1,),
                          in_specs=[pl.BlockSpec((n, rows, LANES), lambda i: (0, 0, 0))],
                          out_specs=pl.BlockSpec((rows, LANES), lambda i: (0, 0)),
                          out_shape=_sds((rows, LANES), F32))(parts)


def _adam_rows(k, n):
    target = max(8, (1 << 20) // (4 * n))
    if k <= target:
        return k
    best = None
    for d in range(8, target + 1, 8):
        if k % d == 0:
            best = d
    return best if best is not None else k


def _adam_update(w, g, m, v):
    m_ = ADAM_B1 * m + (1.0 - ADAM_B1) * g
    v_ = ADAM_B2 * v + (1.0 - ADAM_B2) * (g * g)
    delta = -ADAM_LR * ((m_ / (1.0 - ADAM_B1 ** ADAM_STEP)) / (jnp.sqrt(v_ / (1.0 - ADAM_B2 ** ADAM_STEP)) + ADAM_EPS)
                        + ADAM_WD * w)
    return delta, m_, v_


def _adamw(name, w, g, m, v):
    k, n = w.shape
    bk = _adam_rows(k, n)

    def body(w_ref, g_ref, m_ref, v_ref, d_ref, mo_ref, vo_ref):
        d_ref[...], mo_ref[...], vo_ref[...] = _adam_update(w_ref[...], g_ref[...], m_ref[...], v_ref[...])

    spec = pl.BlockSpec((bk, n), lambda i: (i, 0))
    out = pl.pallas_call(body, name=name, grid=(k // bk,), in_specs=[spec] * 4, out_specs=[spec] * 3,
                         out_shape=[_sds((k, n), F32)] * 3,
                         compiler_params=pltpu.CompilerParams(vmem_limit_bytes=VMEM_LIMIT))(w, g, m, v)
    return (g, *out)


def _adamw_many(name, ws, gs, ms, vs):
    n = len(ws)

    def body(*refs):
        for i in range(n):
            w_ref, g_ref, m_ref, v_ref = (refs[k * n + i] for k in range(4))
            d_ref, mo_ref, vo_ref = (refs[(4 + k) * n + i] for k in range(3))
            d_ref[...], mo_ref[...], vo_ref[...] = _adam_update(w_ref[...], g_ref[...], m_ref[...], v_ref[...])

    specs = [pl.BlockSpec(w.shape, lambda i: (0, 0)) for w in ws]
    out = pl.pallas_call(body, name=name, grid=(1,), in_specs=specs * 4, out_specs=specs * 3,
                         out_shape=[_sds(w.shape, F32) for w in ws] * 3)(*ws, *gs, *ms, *vs)
    return [(gs[i], out[i], out[n + i], out[2 * n + i]) for i in range(n)]


def _adamw_halves(name, w, mine, theirs, m, v, half):
    k, n = w.shape
    bk = _adam_rows(k // 2, n)
    nb = k // 2 // bk

    def body(h_ref, w_ref, mine_ref, theirs_ref, m_ref, v_ref, g_ref, d_ref, mo_ref, vo_ref):
        g = jnp.where(pl.program_id(0) == h_ref[0], mine_ref[...], theirs_ref[...])
        g_ref[...] = g
        d_ref[...], mo_ref[...], vo_ref[...] = _adam_update(w_ref[...], g, m_ref[...], v_ref[...])

    full = pl.BlockSpec((bk, n), lambda h, i, c: (h * nb + i, 0))
    part = pl.BlockSpec((bk, n), lambda h, i, c: (i, 0))
    grid_spec = pltpu.PrefetchScalarGridSpec(num_scalar_prefetch=1, grid=(2, nb),
                                             in_specs=[full, part, part, full, full], out_specs=[full] * 4)
    return tuple(pl.pallas_call(
        body, name=name, grid_spec=grid_spec, out_shape=[_sds((k, n), F32)] * 4,
        compiler_params=pltpu.CompilerParams(vmem_limit_bytes=VMEM_LIMIT),
    )(jnp.reshape(half, (1,)).astype(jnp.int32), w, mine, theirs, m, v))


_HBM = pl.BlockSpec(memory_space=pltpu.HBM)


def _me():
    return lax.axis_index("x"), lax.axis_index("y"), lax.axis_index("c")


def _other_chips(x, y):
    return [(1 - x, y), (x, 1 - y), (1 - x, 1 - y)]


def _pass_to_sibling(zones):
    n = len(zones)

    def body(*refs):
        in_refs, out_refs = refs[:n], refs[n:2 * n]
        send_sems, recv_sems = refs[2 * n:]
        x, y, c = _me()
        sent = []
        for a, (in_ref, out_ref) in enumerate(zip(in_refs, out_refs)):
            for j, (cx, cy) in enumerate(_other_chips(x, y)):
                mine, theirs = (2 * cx + cy, c), (2 * cx + cy, 1 - c)
                sems = dict(send_sem=send_sems.at[3 * a + j], recv_sem=recv_sems.at[3 * a + j],
                            device_id=(x, y, 1 - c), device_id_type=MESH)
                sent.append((pltpu.make_async_remote_copy(src_ref=in_ref.at[mine], dst_ref=out_ref.at[mine], **sems),
                             pltpu.make_async_remote_copy(src_ref=in_ref.at[theirs], dst_ref=out_ref.at[theirs], **sems)))
        for send, _ in sent:
            send.start()
        for _, recv in sent:
            recv.wait_recv()
        for send, _ in sent:
            send.wait_send()

    return pl.pallas_call(
        body, name="pass_to_sibling", out_shape=[_sds(z.shape, z.dtype) for z in zones],
        in_specs=[_HBM] * n, out_specs=[_HBM] * n, input_output_aliases={i: i for i in range(n)},
        scratch_shapes=[pltpu.SemaphoreType.DMA((3 * n,)), pltpu.SemaphoreType.DMA((3 * n,))],
    )(*zones)


def _swap_sibling(name, vs, other_half=False):
    n = len(vs)

    def body(*refs):
        v_refs, out_refs = refs[:n], refs[n:2 * n]
        send_sems, recv_sems = refs[2 * n:]
        x, y, c = _me()
        cps = [pltpu.make_async_remote_copy(src_ref=v_ref.at[:, 1 - c] if other_half else v_ref, dst_ref=out_ref,
                                            send_sem=send_sems.at[a], recv_sem=recv_sems.at[a],
                                            device_id=(x, y, 1 - c), device_id_type=MESH)
               for a, (v_ref, out_ref) in enumerate(zip(v_refs, out_refs))]
        for cp in cps:
            cp.start()
        for cp in cps:
            cp.wait()

    def landing(v):
        return _sds((v.shape[0],) + v.shape[2:] if other_half else v.shape, v.dtype)

    return pl.pallas_call(
        body, name=name, out_shape=[landing(v) for v in vs], in_specs=[_HBM] * n, out_specs=[_HBM] * n,
        scratch_shapes=[pltpu.SemaphoreType.DMA((n,)), pltpu.SemaphoreType.DMA((n,))],
    )(*vs)


_SEM = pl.BlockSpec(memory_space=pltpu.SEMAPHORE)
_EFFECT = pltpu.SideEffectType.DATAFLOW_SIDE_EFFECTING
WHOLE = "whole"
PIECE = "piece"
SIBLING_HALF = "sibling"
MY_HALF = "half"
EVERYONE = "everyone"
_COPIES = {WHOLE: 3, PIECE: 3, MY_HALF: 3, SIBLING_HALF: 1, EVERYONE: 7}


def _landing_shape(v, mode):
    return {WHOLE: (4,) + v.shape, MY_HALF: (4,) + v.shape, PIECE: v.shape, EVERYONE: (8,) + v.shape,
            SIBLING_HALF: (v.shape[0],) + v.shape[2:]}[mode]


def _chip_copies(v_ref, land_ref, send_sems, recv_sems, mode, sem0=0):
    x, y, c = _me()
    if mode == SIBLING_HALF:
        cp = pltpu.make_async_remote_copy(src_ref=v_ref.at[:, 1 - c], dst_ref=land_ref, send_sem=send_sems.at[sem0],
                                          recv_sem=recv_sems.at[sem0], device_id=(x, y, 1 - c), device_id_type=MESH)
        return [(cp, cp)]
    if mode == EVERYONE:
        out = []
        for f in range(1, 8):
            px, py, pc = (1 - x if f & 4 else x), (1 - y if f & 2 else y), (1 - c if f & 1 else c)
            sems = dict(send_sem=send_sems.at[sem0 + f - 1], recv_sem=recv_sems.at[sem0 + f - 1],
                        device_id=(px, py, pc), device_id_type=MESH)
            out.append((pltpu.make_async_remote_copy(src_ref=v_ref, dst_ref=land_ref.at[4 * x + 2 * y + c], **sems),
                        pltpu.make_async_remote_copy(src_ref=v_ref, dst_ref=land_ref.at[4 * px + 2 * py + pc], **sems)))
        return out
    k = 2 * x + y
    out = []
    for j, (cx, cy) in enumerate(_other_chips(x, y)):
        if mode == MY_HALF:
            src, mine, theirs = v_ref.at[c], land_ref.at[k, c], land_ref.at[2 * cx + cy, c]
        else:
            src = v_ref.at[2 * cx + cy] if mode == PIECE else v_ref
            mine, theirs = land_ref.at[k], land_ref.at[2 * cx + cy]
        sems = dict(send_sem=send_sems.at[sem0 + j], recv_sem=recv_sems.at[sem0 + j], device_id=(cx, cy, c),
                    device_id_type=MESH)
        send = pltpu.make_async_remote_copy(src_ref=src, dst_ref=mine, **sems)
        recv = pltpu.make_async_remote_copy(src_ref=src, dst_ref=theirs, **sems)
        out.append((send, recv))
    return out


def _chips_start(name, vs, mode, after=None):
    n = len(vs)
    lands = [_landing_shape(v, mode) for v in vs]

    def body(*refs):
        v_refs, land_refs = refs[:n], refs[n:2 * n]
        send_sems, recv_sems = refs[-2 * n - 3], refs[-2 * n - 2]
        token = refs[-1]
        for a in range(n):
            for send, _ in _chip_copies(v_refs[a], land_refs[a], send_sems, recv_sems, mode, _COPIES[mode] * a):
                send.start()
        token[...] = jnp.zeros_like(token)

    extra = () if after is None else (after,)
    hbm = [pltpu.with_memory_space_constraint(v, pltpu.HBM) for v in vs]
    zones = [pltpu.with_memory_space_constraint(lax.empty(s, v.dtype), pltpu.HBM) for s, v in zip(lands, vs)]
    out = pl.pallas_call(
        body, name=name,
        out_shape=(pltpu.SemaphoreType.DMA((_COPIES[mode] * n,)), pltpu.SemaphoreType.DMA((_COPIES[mode] * n,)),
                   *[pltpu.HBM(v.shape, v.dtype) for v in vs], *[pltpu.HBM(s, v.dtype) for s, v in zip(lands, vs)],
                   _sds((8, LANES), F32)),
        in_specs=(_HBM,) * (2 * n) + (pl.BlockSpec(memory_space=pl.ANY),) * len(extra),
        out_specs=(_SEM, _SEM) + (_HBM,) * (2 * n) + (pl.BlockSpec(memory_space=pltpu.VMEM),),
        input_output_aliases={i: 2 + i for i in range(2 * n)},
        compiler_params=pltpu.CompilerParams(has_side_effects=_EFFECT),
    )(*hbm, *zones, *extra)
    return out[0], out[1], list(out[2:2 + n]), list(out[2 + n:2 + 2 * n]), out[-1]


def _chips_wait(name, send_sems, recv_sems, v_thru, land_thru, mode, after):
    n = len(v_thru)

    def body(*refs):
        v_refs, land_refs = refs[:n], refs[n:2 * n]
        send_sems, recv_sems = refs[2 * n], refs[2 * n + 1]
        for a in range(n):
            for send, recv in _chip_copies(v_refs[a], land_refs[a], send_sems, recv_sems, mode, _COPIES[mode] * a):
                send.wait_send()
                recv.wait_recv()

    out = pl.pallas_call(
        body, name=name,
        out_shape=tuple(pltpu.HBM(a.shape, a.dtype) for a in list(v_thru) + list(land_thru)),
        in_specs=(_HBM,) * (2 * n) + (_SEM, _SEM, pl.BlockSpec(memory_space=pl.ANY)), out_specs=(_HBM,) * (2 * n),
        input_output_aliases={i: i for i in range(2 * n)},
        compiler_params=pltpu.CompilerParams(has_side_effects=_EFFECT),
    )(*v_thru, *land_thru, send_sems, recv_sems, after)
    return list(out[:n]), list(out[n:])


_BIG = (("w_in", (1024, 3232), 1), ("w_uq", (256, 768), 1), ("w_ukv", (128, 1024), 1), ("w_branch_a", (512, 1024), 1),
        ("w_branch_b", (512, 1024), 1), ("w_out", (1024, 1024), 0), ("w_up", (1024, 5632), 1),
        ("w_down", (2816, 1024), 0), ("w_ple_gate", (1024, 1024), 0), ("w_ple", (256, 1024), 1))


def _shard_shape(shape, axis):
    return (shape[0] // 4, shape[1]) if axis == 0 else (shape[0], shape[1] // 4)


def _half_rows(shape, axis):
    k, n = _shard_shape(shape, axis)
    return k * n // (2 * LANES)


_EARLY = ("w_in", "w_uq", "w_ukv")
_LATE = ("w_branch_a", "w_branch_b", "w_out", "w_up", "w_down", "w_ple_gate", "w_ple")
_NATURAL = ("w_in", "w_up", "w_down", "w_out", "w_ple_gate")
_EARLY_PACKED = tuple(b for b in _BIG if b[0] in _EARLY and b[0] not in _NATURAL)
_LATE_PACKED = tuple(b for b in _BIG if b[0] in _LATE and b[0] not in _NATURAL)


def _halves(a):
    return a.reshape(a.shape[:-2] + (2, a.shape[-2] // 2, a.shape[-1]))


def _rows_joined(a):
    return a.reshape(a.shape[:-3] + (a.shape[-3] * a.shape[-2], a.shape[-1]))


def _pack_pad(group):
    return -sum(_half_rows(shape, axis) for _, shape, axis in group) % PACK_ROWS


def _pack_shards(shards, dtype, group):
    parts = [shards[name].astype(dtype).reshape(2, _half_rows(shape, axis), LANES) for name, shape, axis in group]
    return jnp.concatenate(parts + [jnp.zeros((2, _pack_pad(group), LANES), dtype)], axis=1)


def _unpack_gathered(g, group):
    out, off = {}, 0
    for name, shape, axis in group:
        r = _half_rows(shape, axis)
        k, n = _shard_shape(shape, axis)
        w = g[:, :, off:off + r, :].reshape(4, k, n)
        out[name] = w.reshape(shape) if axis == 0 else w.transpose(1, 0, 2).reshape(shape)
        off += r
    return out


def _pack_grads(grads, group):
    parts = []
    for name, shape, axis in group:
        k, n = _shard_shape(shape, axis)
        g = grads[name]
        g4 = g.reshape(4, k, n) if axis == 0 else g.reshape(k, 4, n).transpose(1, 0, 2)
        parts.append(g4.reshape(4, 2, _half_rows(shape, axis), LANES))
    return jnp.concatenate(parts + [jnp.zeros((4, 2, _pack_pad(group), LANES), F32)], axis=2)


def _unpack_shard_grads(f, group):
    out, off = {}, 0
    for name, shape, axis in group:
        r = _half_rows(shape, axis)
        out[name] = f[:, off:off + r, :].reshape(_shard_shape(shape, axis))
        off += r
    return out


def _pad_slots(w, heads, dim):
    k = w.shape[0]
    return jnp.pad(w.reshape(k, heads, dim), ((0, 0), (0, 0), (0, LANES - dim))).reshape(k, heads * LANES)


def _unpad_slots(w, heads, dim):
    k = w.shape[0]
    return w.reshape(k, heads, LANES)[:, :, :dim].reshape(k, heads * dim)


def _pad_w_in(w):
    kr = jnp.pad(w[:, Z_KR:Z_KR + ROPE_DIM], ((0, 0), (NOPE_DIM, LANES - NOPE_DIM - ROPE_DIM)))
    return jnp.concatenate([w[:, :Z_KR], kr, w[:, Z_KR + ROPE_DIM:]], axis=1)


def _unpad_w_in(w):
    return jnp.concatenate([w[:, :Z_KR], w[:, Z_KR + NOPE_DIM:Z_KR + NOPE_DIM + ROPE_DIM], w[:, Z_GATE:ZW]], axis=1)


def _spread_matrix(heads, dim):
    row = lax.broadcasted_iota(jnp.int32, (heads * dim, heads * LANES), 0)
    col = lax.broadcasted_iota(jnp.int32, (heads * dim, heads * LANES), 1)
    return (col == (row // dim) * LANES + row % dim).astype(BF16)


_SMALL = (("attn_pre_norm", 1024), ("attn_post_norm", 1024), ("b_gate", 2048), ("sinks", 8), ("q_a_norm", 256),
          ("kv_a_norm", 128), ("mlp_pre_norm", 1024), ("mlp_post_norm", 1024), ("conv_b", 5632), ("ple_norm", 1024),
          ("conv_w", 3 * 5632), ("loss", 1))


def _small_rows(n):
    return 8 * -(-n // (8 * LANES))


def _pack_small(vals):
    parts = []
    for name, n in _SMALL:
        r = _small_rows(n)
        parts.append(jnp.pad(vals[name].reshape(-1), (0, r * LANES - n)).reshape(r, LANES))
    return jnp.concatenate(parts, axis=0)


def _unpack_small(buf):
    out, off = {}, 0
    for name, n in _SMALL:
        r = _small_rows(n)
        out[name] = buf[off:off + r].reshape(-1)[:n]
        off += r
    return out


def kernel(x, p, positions, attn_pre_norm, attn_post_norm, w_in, b_gate, sinks, q_a_norm, w_uq, kv_a_norm, w_ukv, w_branch_a, w_branch_b, w_out, mlp_pre_norm, mlp_post_norm, w_up, conv_w, conv_b, w_down, ple_norm, w_ple_gate, w_ple, loss_target, m_attn_pre_norm, m_attn_post_norm, m_w_in, m_b_gate, m_sinks, m_q_a_norm, m_w_uq, m_kv_a_norm, m_w_ukv, m_w_branch_a, m_w_branch_b, m_w_out, m_mlp_pre_norm, m_mlp_post_norm, m_w_up, m_conv_w, m_conv_b, m_w_down, m_ple_norm, m_w_ple_gate, m_w_ple, v_attn_pre_norm, v_attn_post_norm, v_w_in, v_b_gate, v_sinks, v_q_a_norm, v_w_uq, v_kv_a_norm, v_w_ukv, v_w_branch_a, v_w_branch_b, v_w_out, v_mlp_pre_norm, v_mlp_post_norm, v_w_up, v_conv_w, v_conv_b, v_w_down, v_ple_norm, v_w_ple_gate, v_w_ple):
    names = ["attn_pre_norm", "attn_post_norm", "w_in", "b_gate", "sinks", "q_a_norm", "w_uq", "kv_a_norm", "w_ukv",
             "w_branch_a", "w_branch_b", "w_out", "mlp_pre_norm", "mlp_post_norm", "w_up", "conv_w", "conv_b",
             "w_down", "ple_norm", "w_ple_gate", "w_ple"]
    wts = dict(zip(names, [attn_pre_norm, attn_post_norm, w_in, b_gate, sinks, q_a_norm, w_uq, kv_a_norm, w_ukv,
                           w_branch_a, w_branch_b, w_out, mlp_pre_norm, mlp_post_norm, w_up, conv_w, conv_b, w_down,
                           ple_norm, w_ple_gate, w_ple]))
    moms = dict(zip(names, [m_attn_pre_norm, m_attn_post_norm, m_w_in, m_b_gate, m_sinks, m_q_a_norm, m_w_uq,
                            m_kv_a_norm, m_w_ukv, m_w_branch_a, m_w_branch_b, m_w_out, m_mlp_pre_norm,
                            m_mlp_post_norm, m_w_up, m_conv_w, m_conv_b, m_w_down, m_ple_norm, m_w_ple_gate, m_w_ple]))
    vars_ = dict(zip(names, [v_attn_pre_norm, v_attn_post_norm, v_w_in, v_b_gate, v_sinks, v_q_a_norm, v_w_uq,
                             v_kv_a_norm, v_w_ukv, v_w_branch_a, v_w_branch_b, v_w_out, v_mlp_pre_norm,
                             v_mlp_post_norm, v_w_up, v_conv_w, v_conv_b, v_w_down, v_ple_norm, v_w_ple_gate, v_w_ple]))
    w2 = {n: a.reshape(a.shape[-2:]) for n, a in wts.items()}
    m2 = {n: a.reshape(a.shape[-2:]) for n, a in moms.items()}
    v2 = {n: a.reshape(a.shape[-2:]) for n, a in vars_.items()}

    t_rows = x.shape[-2]
    tm = min(256, t_rows)
    tm_wide = min(512, t_rows)
    xc, yc, cc = lax.axis_index("x"), lax.axis_index("y"), lax.axis_index("c")
    chip = 2 * xc + yc

    x2d = x.reshape(t_rows, D_MODEL)
    p2d = p.reshape(t_rows, PLE_DIM)
    tgt = loss_target.reshape(t_rows, D_MODEL)
    pos_f = positions.reshape(t_rows, 1).astype(F32)

    def own_slot_filled(gathered, mine):
        return [lax.dynamic_update_slice(g, m[None], (chip, 0, 0, 0)) for g, m in zip(gathered, mine)]

    def shard_lists(group, packed_group, token=0.0):
        ws = {n: w2[n] + token for n in group}
        return [_halves(ws[n].astype(BF16)) for n in group if n in _NATURAL] + [_pack_shards(ws, BF16, packed_group)]

    cw_rows = 3 * 1408 // LANES
    conv_mine = jnp.pad(w2["conv_w"].reshape(cw_rows, LANES), ((0, 48 - cw_rows), (0, 0))).reshape(2, 24, LANES)
    early_mine = shard_lists(_EARLY, _EARLY_PACKED) + [conv_mine]
    early_sems = _chips_start("gather_early_start", early_mine, MY_HALF)
    early_token = early_sems[4][0:1, 0:1]
    consts = _rope_consts()
    tabs = _rope_tables(pos_f + early_token, consts, tm)
    late_mine = shard_lists(_LATE, _LATE_PACKED, early_token)
    both_done = tabs[0][0:1, 0:1] + sum(m[0, 0:1, 0:1].astype(F32) for m in late_mine)
    early_sent, early_landed = _chips_wait("gather_early_wait", *early_sems[:4], MY_HALF, after=both_done)
    early = own_slot_filled(_pass_to_sibling(early_landed), early_sent)
    late_names = [n for n in _LATE if n in _NATURAL]
    first = [late_names.index("w_out"), len(late_names)]
    late_a = [late_mine[i] for i in first]
    late_b = [m for i, m in enumerate(late_mine) if i not in first]
    late_a_sems = _chips_start("gather_late_a_start", late_a, WHOLE, after=early[0])
    late_b_sems = _chips_start("gather_late_b_start", late_b, WHOLE, after=late_a_sems[4])
    late_token = late_b_sems[4][0:1, 0:1]
    full = _unpack_gathered(early[1], _EARLY_PACKED)
    full["w_in"] = _rows_joined(early[0]).transpose(1, 0, 2).reshape(D_MODEL, 3232)
    conv_full = early[2].reshape(4, 48, LANES)[:, :cw_rows].reshape(4, 3, 1408).transpose(1, 0, 2).reshape(3, 2 * D_FF)
    convw8 = jnp.pad(conv_full, ((0, 5), (0, 0)))

    win = _pad_w_in(full["w_in"])
    wuq = _pad_slots(full["w_uq"], HEADS, NOPE_DIM + ROPE_DIM)
    ukv = full["w_ukv"].reshape(KV_LORA, HEADS, NOPE_DIM + V_DIM)
    wk = _pad_slots(ukv[:, :, :NOPE_DIM].reshape(KV_LORA, HEADS * NOPE_DIM), HEADS, NOPE_DIM)
    wv = _pad_slots(ukv[:, :, NOPE_DIM:].reshape(KV_LORA, HEADS * V_DIM), HEADS, V_DIM)
    g1, g2, g3, g4, g5 = (w2["attn_pre_norm"], w2["attn_post_norm"], w2["mlp_pre_norm"], w2["mlp_post_norm"],
                          w2["ple_norm"])
    gq, gkv, bg, convb = w2["q_a_norm"], w2["kv_a_norm"], w2["b_gate"], w2["conv_b"]
    swa_tile = min(SWA_TILE, t_rows)
    sink_rows = jnp.repeat(w2["sinks"].reshape(A_KV_HEADS, SWA_GROUP, 1), swa_tile, axis=2).reshape(
        A_KV_HEADS, 1, SWA_GROUP * swa_tile)
    swa_bias = _swa_bias(swa_tile)
    spread_q = _spread_matrix(HEADS, A_HEAD_DIM)
    spread_kv = _spread_matrix(A_KV_HEADS, A_HEAD_DIM)

    h1, qs, ks, vs, cq, cqn, ckv, ckvn, qm, km, vm, gate = _fwd_in(x2d, g1, win, bg + late_token, gq, gkv, wuq, wk, wv,
                                                                   spread_q, spread_kv, tabs, tm_wide)
    ya, lse_a = _swa_fwd(qs, ks, vs, swa_bias, sink_rows)
    yb, lse_b = _mla_fwd(qm, km, vm)
    late_sent, late_landed = _chips_wait("gather_late_a_wait", *late_a_sems[:4], WHOLE, after=yb)
    wout_g, packed_g = own_slot_filled(late_landed, late_sent)
    full = _unpack_gathered(packed_g, _LATE_PACKED)
    wba, wbb = full["w_branch_a"], full["w_branch_b"]
    wple = full["w_ple"]
    wout = _rows_joined(wout_g).reshape(-1, D_MODEL)
    pa, pb, mixed, o, x1, h2, ya_c, yb_c = _fwd_mix(x2d, ya, yb, gate, wba, wbb, wout, g2, g3, tm_wide)
    late_sent, late_landed = _chips_wait("gather_late_b_wait", *late_b_sems[:4], WHOLE, after=pa)
    natural = dict(zip([n for n in late_names if n != "w_out"], own_slot_filled(late_landed, late_sent)))
    wup = _rows_joined(natural["w_up"])
    wdown, wpg = (_rows_joined(natural[n]).reshape(-1, D_MODEL) for n in ("w_down", "w_ple_gate"))
    up, a = _fwd_up(h2, wup, convw8, convb, tm)
    ff, x2, e, n5, sg, dx3, loss_part = _fwd_out(a, wdown, x1, g4, p2d, wple, g5, wpg, tgt, tm_wide)

    dpre, de, dx2, dff, du, dg5, dg4, dconvb, dconvw8 = _bwd_out(dx3, e, sg, x2, ff, g5, g4, wpg, wdown, up, convw8,
                                                                 convb, tm)
    dup, dx1, do, dpa, dpb, dgates, dya, dyb, delta_b, dg3, dg2, dbg = _bwd_mid(
        du, convw8, wup, dx2, x1, g3, o, g2, wout, gate, pa, pb, wba, wbb, yb_c, tm)
    late_grads = {
        "w_branch_a": _mm_tn("dw_branch_a", ya_c, dpa),
        "w_branch_b": _mm_tn("dw_branch_b", yb_c, dpb),
        "w_out": _mm_tn("dw_out", mixed, do).reshape(4, D_MODEL // 4, D_MODEL),
        "w_up": _mm_tn("dw_up", h2, dup, column_shards=4),
        "w_down": _mm_tn("dw_down", a, dff).reshape(4, D_FF // 4, D_MODEL),
        "w_ple_gate": _mm_tn("dw_ple_gate", n5, dpre).reshape(4, D_MODEL // 4, D_MODEL),
        "w_ple": _mm_tn("dw_ple", p2d, de),
    }

    def grad_views(grads, group, packed_group):
        return [_halves(grads[n]) for n in group if n in _NATURAL] + [_pack_grads(grads, packed_group)]

    def pair_sums(tag, views, theirs):
        return [_add_pair("rs_%s_add_pair_%d" % (tag, i), g, r, cc) for i, (g, r) in enumerate(zip(views, theirs))]

    swap_sems = _chips_start("swap_late_start", grad_views(late_grads, _LATE, _LATE_PACKED), SIBLING_HALF)
    dqs, dks, dvs, dsink_rows = _swa_bwd(qs, ks, vs, ya, dya, lse_a, swa_bias, sink_rows + swap_sems[4][0:1, 0:1])
    dsink = dsink_rows[:, 0:SWA_GROUP, 0]
    late_views, late_theirs = _chips_wait("swap_late_wait", *swap_sems[:4], SIBLING_HALF, after=dqs)
    rs_sems = _chips_start("scatter_late_start", pair_sums("late", late_views, late_theirs), PIECE)
    dqm, dkm, dvm = _mla_bwd(qm, km, vm, dyb, lse_b, delta_b.reshape(HEADS, 1, t_rows) + rs_sems[4][0:1, 0:1])
    dz, dqb, dx, dgq, dgkv, dg1 = _bwd_in(dqs, dks, dvs, dqm, dkm, dvm, tabs, consts, cq, ckv, gq, gkv, wuq, wk, wv,
                                           dgates, win, x2d, g1, dx1, tm)

    small = {"attn_pre_norm": dg1, "attn_post_norm": dg2, "b_gate": dbg, "sinks": dsink, "q_a_norm": dgq,
             "kv_a_norm": dgkv, "mlp_pre_norm": dg3, "mlp_post_norm": dg4, "conv_b": dconvb, "ple_norm": dg5,
             "conv_w": dconvw8[0:3], "loss": loss_part}
    small_sems = _chips_start("gather_small_start", [_pack_small(small)], EVERYONE)
    small_token = small_sems[4]

    dwk = _unpad_slots(_mm_tn("dw_k", ckvn, dkm, after=small_token), HEADS, NOPE_DIM).reshape(
        KV_LORA, HEADS, NOPE_DIM)
    dwv = _unpad_slots(_mm_tn("dw_v", ckvn, dvm, after=small_token), HEADS, V_DIM).reshape(KV_LORA, HEADS, V_DIM)
    early_grads = {
        "w_in": _unpad_w_in(_mm_tn("dw_in", h1, dz, after=small_token)).reshape(D_MODEL, 4, 808).transpose(1, 0, 2),
        "w_uq": _unpad_slots(_mm_tn("dw_uq", cqn, dqb, after=small_token), HEADS, NOPE_DIM + ROPE_DIM),
        "w_ukv": jnp.concatenate([dwk, dwv], axis=2).reshape(KV_LORA, HEADS * (NOPE_DIM + V_DIM)),
    }

    def finish(tag, pairs, landed, group, packed_group):
        reduced = []
        for i, (pair, land) in enumerate(zip(pairs, landed)):
            own = lax.dynamic_index_in_dim(pair, chip, 0, keepdims=True)
            reduced.append(_add_chips("rs_%s_add_chips_%d" % (tag, i),
                                      lax.dynamic_update_slice(land, own, (chip, 0, 0))))
        others = _swap_sibling("swap_%s_reduced_halves" % tag, reduced)
        r, o = reduced[-1], others[-1]
        packed = jnp.where(cc == 0, jnp.stack([r, o]), jnp.stack([o, r]))
        for n, g in _unpack_shard_grads(packed, packed_group).items():
            updates[n] = _adamw("adamw_" + n, w2[n], g, m2[n], v2[n])
        for n, r, o in zip([n for n in group if n in _NATURAL], reduced, others):
            updates[n] = _adamw_halves("adamw_" + n, w2[n], r, o, m2[n], v2[n], cc)

    updates = {}

    early_views = grad_views(early_grads, _EARLY, _EARLY_PACKED)
    early_theirs = _swap_sibling("swap_early_grad_halves", early_views, other_half=True)
    small_sent, small_landed = _chips_wait("gather_small_wait", *small_sems[:4], EVERYONE, after=early_theirs[0])
    small_all = lax.dynamic_update_slice(small_landed[0], small_sent[0][None], (4 * xc + 2 * yc + cc, 0, 0))
    early_sems = _chips_start("scatter_early_start", pair_sums("early", early_views, early_theirs), PIECE,
                              after=small_all)
    late_pairs, late_landed = _chips_wait("scatter_late_wait", *rs_sems[:4], PIECE, after=early_sems[4])
    finish("late", late_pairs, late_landed, _LATE, _LATE_PACKED)
    early_pairs, early_landed = _chips_wait("scatter_early_wait", *early_sems[:4], PIECE,
                                            after=updates[_LATE[-1]][1])
    finish("early", early_pairs, early_landed, _EARLY, _EARLY_PACKED)

    small_sum = _unpack_small(_add_devices(small_all))
    small_names = [n for n in names if n in small_sum]
    small_grads = [lax.dynamic_index_in_dim(small_sum[n].reshape(3, 4, 1408), chip, 1, keepdims=False)
                   if n == "conv_w" else small_sum[n].reshape(w2[n].shape) for n in small_names]
    updates.update(zip(small_names, _adamw_many("adamw_small", [w2[n] for n in small_names], small_grads,
                                                [m2[n] for n in small_names], [v2[n] for n in small_names])))
    loss = small_sum["loss"][0]

    outs = [[updates[n][i].reshape(wts[n].shape) for n in names] for i in range(4)]
    return (loss, dx.reshape(x.shape), *outs[0], *outs[1], *outs[2], *outs[3])
```

```python
import math

import numpy as np
import jax
import jax.numpy as jnp
from jax import lax
from jax.experimental import pallas as pl
from jax.experimental.pallas import tpu as pltpu

F32 = jnp.float32
BF16 = jnp.bfloat16

D_MODEL = 1024
D_FF = 2816
PLE_DIM = 256
ROPE_THETA = 10000.0
RMS_EPS = 1e-6
SWA_WINDOW = 128
HEADS = 8
A_KV_HEADS = 2
A_HEAD_DIM = 64
KV_LORA = 128
NOPE_DIM = 64
ROPE_DIM = 32
V_DIM = 64
LANES = 128
ZW = 3328
NEG = -1e30
SCALE_A = A_HEAD_DIM ** -0.5
SCALE_B = (NOPE_DIM + ROPE_DIM) ** -0.5

ADAM_LR = 0.001
ADAM_B1 = 0.9
ADAM_B2 = 0.999
ADAM_EPS = 1e-08
ADAM_WD = 0.01
ADAM_STEP = 10

VMEM_LIMIT = 60 * 1024 * 1024
MESH = pl.DeviceIdType.MESH

Z_QA, Z_KA, Z_VA, Z_CQ, Z_CKV, Z_KR, Z_GATE = 0, 512, 640, 768, 1024, 1152, 1280


def _dot(a, b):
    return jnp.dot(a, b, preferred_element_type=F32)


def _dot_nt(a, b):
    return lax.dot_general(a, b, (((1,), (1,)), ((), ())), preferred_element_type=F32)


def _dot_tn(a, b):
    return lax.dot_general(a, b, (((0,), (0,)), ((), ())), preferred_element_type=F32)


def _rms_stats(x):
    r = lax.rsqrt(jnp.mean(x * x, axis=-1, keepdims=True) + RMS_EPS)
    return x * r, r


def _rms_bwd(dy, xn, r, g):
    dxn = dy * g
    dx = r * (dxn - xn * jnp.mean(dxn * xn, axis=-1, keepdims=True))
    dg = jnp.sum(dy * xn, axis=0, keepdims=True)
    return dx, dg


def _tile_lanes(t, n):
    return t if n == 1 else jnp.concatenate([t] * n, axis=1)


def _rope(x, c, s1, s2, half):
    w = x.shape[1]
    n = w // LANES
    return (x * _tile_lanes(c, n) + pltpu.roll(x, w - half, 1) * _tile_lanes(s1, n)
            + pltpu.roll(x, half, 1) * _tile_lanes(s2, n))


def _rope_t(dy, c, s1, s2, half):
    w = dy.shape[1]
    n = w // LANES
    return (dy * _tile_lanes(c, n) + pltpu.roll(dy * _tile_lanes(s1, n), half, 1)
            + pltpu.roll(dy * _tile_lanes(s2, n), w - half, 1))


def _fold_slots(d):
    tiles = []
    for j in range(d.shape[1] // (2 * LANES)):
        even = d[:, 2 * j * LANES:(2 * j + 1) * LANES]
        odd = d[:, (2 * j + 1) * LANES:(2 * j + 2) * LANES]
        tiles.append(even + pltpu.roll(odd, A_HEAD_DIM, 1))
    return tiles[0] if len(tiles) == 1 else jnp.concatenate(tiles, axis=1)


def _spread_slots(c):
    low = lax.broadcasted_iota(jnp.int32, (c.shape[0], LANES), 1) < A_HEAD_DIM
    slots = []
    for j in range(c.shape[1] // LANES):
        tile = c[:, j * LANES:(j + 1) * LANES]
        slots += [jnp.where(low, tile, 0.0), jnp.where(low, pltpu.roll(tile, A_HEAD_DIM, 1), 0.0)]
    return jnp.concatenate(slots, axis=1)


def _sigmoid(x):
    return 1.0 / (1.0 + jnp.exp(-x))


_GELU_C = math.sqrt(2.0 / math.pi)


def _gelu_and_grad(x):
    a = _GELU_C + (_GELU_C * 0.044715) * (x * x)
    th = jnp.tanh(x * a)
    hx = 0.5 * x
    p1 = 1.0 + th
    gel = hx * p1
    dgel = 0.5 * p1 + (hx * (1.0 - th * th)) * (3.0 * a - 2.0 * _GELU_C)
    return gel, dgel


def _conv_taps(up, h6, h7):
    r1 = pltpu.roll(up, 1, 0)
    r2 = pltpu.roll(up, 2, 0)
    rows = lax.broadcasted_iota(jnp.int32, (8, up.shape[1]), 0)
    xm1 = jnp.concatenate([jnp.where(rows == 0, h7, r1[0:8]), r1[8:]], axis=0)
    xm2 = jnp.concatenate([jnp.where(rows == 0, h6, jnp.where(rows == 1, h7, r2[0:8])), r2[8:]], axis=0)
    return xm1, xm2


def _conv_taps_next(du, n0, n1):
    tm = du.shape[0]
    r1 = pltpu.roll(du, tm - 1, 0)
    r2 = pltpu.roll(du, tm - 2, 0)
    rows = lax.broadcasted_iota(jnp.int32, (8, du.shape[1]), 0)
    xp1 = jnp.concatenate([r1[:tm - 8], jnp.where(rows == 7, n0, r1[tm - 8:])], axis=0)
    xp2 = jnp.concatenate([r2[:tm - 8], jnp.where(rows == 6, n0, jnp.where(rows == 7, n1, r2[tm - 8:]))], axis=0)
    return xp1, xp2


def _row(tm, n):
    return pl.BlockSpec((tm, n), lambda i: (i, 0))


def _full(shape):
    nd = len(shape)
    return pl.BlockSpec(tuple(shape), lambda i: (0,) * nd)


def _resident(shape):
    nd = len(shape)
    return pl.BlockSpec(tuple(shape), lambda i: (0,) * nd, pipeline_mode=pl.Buffered(1))


def _heads(tm, h):
    return pl.BlockSpec((h, tm, LANES), lambda i: (0, i, 0))


def _rows_call(name, body, t_rows, tm, ins, outs, scratch=()):
    return pl.pallas_call(
        body, name=name, grid=(t_rows // tm,),
        in_specs=[s for _, s in ins],
        out_specs=[s for _, s in outs],
        out_shape=[s for s, _ in outs],
        scratch_shapes=list(scratch),
        compiler_params=pltpu.CompilerParams(dimension_semantics=("arbitrary",), vmem_limit_bytes=VMEM_LIMIT),
    )(*[a for a, _ in ins])


def _sds(shape, dtype):
    return jax.ShapeDtypeStruct(tuple(shape), dtype)


def _rope_consts():
    c = np.zeros((16, LANES), np.float32)
    lane = np.arange(LANES)
    inv_a = (ROPE_THETA ** (-(np.arange(0, A_HEAD_DIM, 2, dtype=np.float32) / A_HEAD_DIM))).astype(np.float32)
    in_a = lane < A_HEAD_DIM
    c[0, in_a] = inv_a[lane[in_a] % (A_HEAD_DIM // 2)]
    c[1, in_a] = 1.0
    c[2, lane < A_HEAD_DIM // 2] = -1.0
    c[3, (lane >= A_HEAD_DIM // 2) & in_a] = 1.0
    inv_b = (ROPE_THETA ** (-(np.arange(0, ROPE_DIM, 2, dtype=np.float32) / ROPE_DIM))).astype(np.float32)
    pe = (lane >= NOPE_DIM) & (lane < NOPE_DIM + ROPE_DIM)
    c[5, pe] = inv_b[(lane[pe] - NOPE_DIM) % (ROPE_DIM // 2)]
    c[6, pe] = 1.0
    c[7, (lane >= NOPE_DIM) & (lane < NOPE_DIM + ROPE_DIM // 2)] = -1.0
    c[8, (lane >= NOPE_DIM + ROPE_DIM // 2) & (lane < NOPE_DIM + ROPE_DIM)] = 1.0
    c[9, lane < NOPE_DIM] = 1.0
    c[10, pe] = 1.0
    return jnp.asarray(c)


def _rope_tables(pos_f, consts, tm):
    t_rows = pos_f.shape[0]

    def body(pos_ref, c_ref, ca, sa1, sa2, cb, sb1, sb2):
        ang = pos_ref[...] * (c_ref[0:1, :] + c_ref[5:6, :])
        cs, sn = jnp.cos(ang), jnp.sin(ang)
        for ref, row in ((ca, 1), (sa1, 2), (sa2, 3)):
            half = (cs if row == 1 else sn) * c_ref[row:row + 1, :]
            ref[...] = half + pltpu.roll(half, A_HEAD_DIM, 1)
        cb[...] = cs * c_ref[6:7, :] + c_ref[9:10, :]
        sb1[...] = sn * c_ref[7:8, :]
        sb2[...] = sn * c_ref[8:9, :]

    tab = (_sds((t_rows, LANES), F32), _row(tm, LANES))
    return _rows_call("rope_tables", body, t_rows, tm,
                      [(pos_f, _row(tm, 1)), (consts, _full(consts.shape))], [tab] * 6)


def _fwd_in(x, g1, win, bg, gq, gkv, wuq, wk, wv, eq, ek, tabs, tm):
    t_rows = x.shape[0]

    def body(x_ref, g1_ref, win_ref, bg_ref, gq_ref, gkv_ref, wuq_ref, wk_ref, wv_ref, eq_ref, ek_ref,
             ca, sa1, sa2, cb, sb1, sb2,
             h1_ref, qs_ref, ks_ref, vs_ref, cq_ref, cqn_ref, ckv_ref, ckvn_ref, qm_ref, km_ref, vm_ref, gate_ref):
        xn, _ = _rms_stats(x_ref[...])
        hb = (xn * g1_ref[...]).astype(BF16)
        h1_ref[...] = hb
        ta = (ca[...], sa1[...], sa2[...])
        tb = (cb[...], sb1[...], sb2[...])
        cq = _dot(hb, win_ref[:, Z_CQ:Z_CKV])
        ckv = _dot(hb, win_ref[:, Z_CKV:Z_KR])
        z_qa = _dot(hb, win_ref[:, Z_QA:Z_KA])
        z_ka = _dot(hb, win_ref[:, Z_KA:Z_VA])
        z_va = _dot(hb, win_ref[:, Z_VA:Z_CQ])
        z_kr = _dot(hb, win_ref[:, Z_KR:Z_GATE])
        cq_ref[...] = cq
        cqn, _ = _rms_stats(cq)
        cqb = (cqn * gq_ref[...]).astype(BF16)
        cqn_ref[...] = cqb
        ckv_ref[...] = ckv
        ckvn, _ = _rms_stats(ckv)
        ckvb = (ckvn * gkv_ref[...]).astype(BF16)
        ckvn_ref[...] = ckvb
        z_qm = _dot(cqb, wuq_ref[...])
        z_km = _dot(ckvb, wk_ref[...])
        z_vm = _dot(ckvb, wv_ref[...])
        z_gate = _dot(hb, win_ref[:, Z_GATE:ZW])
        qs_ref[...] = _dot((_rope(z_qa, *ta, A_HEAD_DIM // 2) * SCALE_A).astype(BF16), eq_ref[...]).astype(BF16)
        ks_ref[...] = _dot(_rope(z_ka, *ta, A_HEAD_DIM // 2).astype(BF16), ek_ref[...]).astype(BF16)
        vs_ref[...] = _dot(z_va.astype(BF16), ek_ref[...]).astype(BF16)
        qm_ref[...] = (_rope(z_qm, *tb, ROPE_DIM // 2) * SCALE_B).astype(BF16)
        km_ref[...] = (z_km + _tile_lanes(_rope(z_kr, *tb, ROPE_DIM // 2), HEADS)).astype(BF16)
        vm_ref[...] = z_vm.astype(BF16)
        gate_ref[...] = _sigmoid(z_gate + bg_ref[...]).astype(BF16)

    def o(n, dt):
        return (_sds((t_rows, n), dt), _row(tm, n))

    ins = [(x, _row(tm, D_MODEL)), (g1, _full(g1.shape)), (win, _resident(win.shape)), (bg, _full(bg.shape)),
           (gq, _full(gq.shape)), (gkv, _full(gkv.shape)), (wuq, _full(wuq.shape)), (wk, _full(wk.shape)),
           (wv, _full(wv.shape)), (eq, _full(eq.shape)), (ek, _full(ek.shape))] + [(t, _row(tm, LANES)) for t in tabs]
    outs = [o(1024, BF16), o(1024, BF16), o(256, BF16), o(256, BF16), o(256, F32), o(256, BF16), o(128, F32),
            o(128, BF16), o(1024, BF16), o(1024, BF16), o(1024, BF16), o(2048, BF16)]
    return _rows_call("fwd_in", body, t_rows, tm, ins, outs)


def _attn_tile(t_rows):
    return min(512, t_rows)


MLA_HEADS_PER_STEP = 4
MLA_FWD_HEADS_PER_STEP = 8


def _causal_pairs(nq, by_kv):
    if by_kv:
        pairs = [(i, j) for j in range(nq) for i in range(j, nq)]
    else:
        pairs = [(i, j) for i in range(nq) for j in range(i + 1)]
    return (jnp.asarray([p[0] for p in pairs], jnp.int32), jnp.asarray([p[1] for p in pairs], jnp.int32))


def _mla_fwd(q, k, v):
    t_rows = q.shape[0]
    t = _attn_tile(t_rows)
    hp = MLA_FWD_HEADS_PER_STEP
    w = hp * LANES
    ii, jj = _causal_pairs(t_rows // t, by_kv=False)

    def body(i_ref, j_ref, q_ref, k_ref, v_ref, o_ref, lse_ref, m_s, l_s, acc_s):
        i = i_ref[pl.program_id(1)]
        j = j_ref[pl.program_id(1)]

        @pl.when(j == 0)
        def _():
            m_s[...] = jnp.full(m_s.shape, NEG, F32)
            l_s[...] = jnp.zeros(l_s.shape, F32)
            acc_s[...] = jnp.zeros(acc_s.shape, F32)

        def step(diagonal):
            sls = [slice(hh * LANES, (hh + 1) * LANES) for hh in range(hp)]
            scores = [_dot_nt(k_ref[:, sl], q_ref[:, sl]) for sl in sls]
            if diagonal:
                valid = (lax.broadcasted_iota(jnp.int32, (t, t), 0) <= lax.broadcasted_iota(jnp.int32, (t, t), 1))
                scores = [jnp.where(valid, s, NEG) for s in scores]
            stats = []
            for hh, s in enumerate(scores):
                m_prev = m_s[hh]
                m_new = jnp.maximum(m_prev, jnp.max(s, axis=0, keepdims=True))
                p = jnp.exp(s - m_new)
                alpha = jnp.exp(m_prev - m_new)
                stats.append((m_new, alpha, alpha * l_s[hh] + jnp.sum(p, axis=0, keepdims=True), p.astype(BF16)))
            for hh, (m_new, alpha, l_new, p) in enumerate(stats):
                sl = sls[hh]
                acc = alpha * acc_s[hh] + _dot_tn(v_ref[:, sl], p)
                if diagonal:
                    o_ref[:, sl] = (acc / l_new).T.astype(o_ref.dtype)
                    lse_ref[hh] = m_new + jnp.log(l_new)
                else:
                    m_s[hh] = m_new
                    l_s[hh] = l_new
                    acc_s[hh] = acc

        pl.when(j < i)(lambda: step(False))
        pl.when(j == i)(lambda: step(True))

    grid_spec = pltpu.PrefetchScalarGridSpec(
        num_scalar_prefetch=2, grid=(HEADS // hp, ii.shape[0]),
        in_specs=[pl.BlockSpec((t, w), lambda hb, s, ir, jr: (ir[s], hb)),
                  pl.BlockSpec((t, w), lambda hb, s, ir, jr: (jr[s], hb)),
                  pl.BlockSpec((t, w), lambda hb, s, ir, jr: (jr[s], hb))],
        out_specs=[pl.BlockSpec((t, w), lambda hb, s, ir, jr: (ir[s], hb)),
                   pl.BlockSpec((hp, 1, t), lambda hb, s, ir, jr: (hb, 0, ir[s]))],
        scratch_shapes=[pltpu.VMEM((hp, 1, t), F32), pltpu.VMEM((hp, 1, t), F32), pltpu.VMEM((hp, LANES, t), F32)])
    return pl.pallas_call(
        body, name="mla_fwd", grid_spec=grid_spec,
        out_shape=[_sds((t_rows, HEADS * LANES), BF16), _sds((HEADS, 1, t_rows), F32)],
        compiler_params=pltpu.CompilerParams(dimension_semantics=("arbitrary",) * 2, vmem_limit_bytes=VMEM_LIMIT),
    )(ii, jj, q, k, v)


def _mla_bwd(q, k, v, do, lse, delta):
    t_rows = q.shape[0]
    t = _attn_tile(t_rows)
    hp = MLA_HEADS_PER_STEP
    w = hp * LANES
    ii, jj = _causal_pairs(t_rows // t, by_kv=True)

    def body(i_ref, j_ref, q_ref, k_ref, v_ref, do_ref, lse_ref, dl_ref, dq_ref, dk_ref, dv_ref):
        i = i_ref[pl.program_id(1)]
        j = j_ref[pl.program_id(1)]

        @pl.when(pl.program_id(1) == 0)
        def _():
            dq_ref[...] = jnp.zeros(dq_ref.shape, F32)

        def step(diagonal):
            r0 = pl.multiple_of(i * t, t)
            sls = [slice(hh * LANES, (hh + 1) * LANES) for hh in range(hp)]
            scores = [_dot_nt(k_ref[:, sl], q_ref[:, sl]) for sl in sls]
            if diagonal:
                valid = (lax.broadcasted_iota(jnp.int32, (t, t), 0) <= lax.broadcasted_iota(jnp.int32, (t, t), 1))
                scores = [jnp.where(valid, s, NEG) for s in scores]
            dps = [_dot_nt(v_ref[:, sl], do_ref[:, sl]) for sl in sls]
            ps = [jnp.exp(s - lse_ref[hh]) for hh, s in enumerate(scores)]
            dss = [(p * (dp - dl_ref[hh])).astype(BF16) for hh, (p, dp) in enumerate(zip(ps, dps))]
            for hh, sl in enumerate(sls):
                dv = _dot(ps[hh].astype(BF16), do_ref[:, sl])
                dk = _dot(dss[hh], q_ref[:, sl])
                if diagonal:
                    dv_ref[:, sl] = dv
                    dk_ref[:, sl] = dk
                else:
                    dv_ref[:, sl] += dv
                    dk_ref[:, sl] += dk
                dq_ref[hh, pl.ds(r0, t), :] += _dot_tn(dss[hh], k_ref[:, sl])

        pl.when(i > j)(lambda: step(False))
        pl.when(i == j)(lambda: step(True))

    def qmap(hb, s, ir, jr):
        return (ir[s], hb)

    def kvmap(hb, s, ir, jr):
        return (jr[s], hb)

    def rowmap(hb, s, ir, jr):
        return (hb, 0, ir[s])

    grid_spec = pltpu.PrefetchScalarGridSpec(
        num_scalar_prefetch=2, grid=(HEADS // hp, ii.shape[0]),
        in_specs=[pl.BlockSpec((t, w), qmap), pl.BlockSpec((t, w), kvmap), pl.BlockSpec((t, w), kvmap),
                  pl.BlockSpec((t, w), qmap), pl.BlockSpec((hp, 1, t), rowmap), pl.BlockSpec((hp, 1, t), rowmap)],
        out_specs=[pl.BlockSpec((hp, t_rows, LANES), lambda hb, s, ir, jr: (hb, 0, 0)),
                   pl.BlockSpec((t, w), kvmap), pl.BlockSpec((t, w), kvmap)])
    return pl.pallas_call(
        body, name="mla_bwd", grid_spec=grid_spec,
        out_shape=[_sds((HEADS, t_rows, LANES), F32), _sds((t_rows, HEADS * LANES), F32),
                   _sds((t_rows, HEADS * LANES), F32)],
        compiler_params=pltpu.CompilerParams(dimension_semantics=("arbitrary",) * 2, vmem_limit_bytes=VMEM_LIMIT),
    )(ii, jj, q, k, v, do, lse, delta)


SWA_TILE = 2 * SWA_WINDOW
SWA_GROUP = HEADS // A_KV_HEADS


def _swa_bias(tq):
    koff = lax.broadcasted_iota(jnp.int32, (tq + SWA_WINDOW, SWA_GROUP * tq), 0) - SWA_WINDOW
    qoff = (lax.broadcasted_iota(jnp.int32, (tq + SWA_WINDOW, SWA_GROUP * tq), 1) % tq)
    band = (koff <= qoff) & (qoff - koff < SWA_WINDOW)
    return jnp.stack([jnp.where(band & (koff >= 0), 0.0, NEG), jnp.where(band, 0.0, NEG)]).astype(F32)


def _swa_specs(tq, nq):
    wb = tq // SWA_WINDOW
    kvw = A_KV_HEADS * LANES

    def qi(i):
        return jnp.minimum(i, nq - 1)

    q = pl.BlockSpec((tq, HEADS * LANES), lambda i: (qi(i), 0))
    cur = pl.BlockSpec((tq, kvw), lambda i: (qi(i), 0))
    prev = pl.BlockSpec((SWA_WINDOW, kvw), lambda i: (jnp.maximum(qi(i) * wb - 1, 0), 0))
    bias = pl.BlockSpec((1, tq + SWA_WINDOW, SWA_GROUP * tq), lambda i: (jnp.minimum(i, 1), 0, 0))
    rows = pl.BlockSpec((A_KV_HEADS, 1, 1, SWA_GROUP * tq), lambda i: (0, qi(i), 0, 0))
    sink = pl.BlockSpec((A_KV_HEADS, 1, SWA_GROUP * tq), lambda i: (0, 0, 0))
    return q, cur, prev, bias, rows, sink


def _stack_heads(ref, kvh):
    base = kvh * SWA_GROUP
    return jnp.concatenate([ref[:, (base + g) * LANES:(base + g + 1) * LANES] for g in range(SWA_GROUP)], axis=0)


def _unstack_heads(ref, kvh, val, tq):
    base = kvh * SWA_GROUP
    for g in range(SWA_GROUP):
        ref[:, (base + g) * LANES:(base + g + 1) * LANES] = val[g * tq:(g + 1) * tq].astype(ref.dtype)


def _kv_window(prev_ref, cur_ref, kvh):
    sl = slice(kvh * LANES, (kvh + 1) * LANES)
    return jnp.concatenate([prev_ref[:, sl], cur_ref[:, sl]], axis=0)


def _swa_fwd(q, k, v, bias, sink_rows):
    t_rows = q.shape[0]
    tq = min(SWA_TILE, t_rows)
    nq = t_rows // tq
    qs_, cur, prev, bs, rows, sk = _swa_specs(tq, nq)
    kvhs = range(A_KV_HEADS)

    def body(q_ref, kc_ref, kp_ref, vc_ref, vp_ref, b_ref, sink_ref, o_ref, lse_ref):
        scores = [_dot_nt(_kv_window(kp_ref, kc_ref, h), _stack_heads(q_ref, h)) + b_ref[0] for h in kvhs]
        stats = []
        for h, s in zip(kvhs, scores):
            sink = sink_ref[h]
            m = jnp.maximum(jnp.max(s, axis=0, keepdims=True), sink)
            p = jnp.exp(s - m)
            l = jnp.sum(p, axis=0, keepdims=True) + jnp.exp(sink - m)
            lse_ref[h, 0] = m + jnp.log(l)
            stats.append((p.astype(BF16), l))
        for h, (p, l) in zip(kvhs, stats):
            _unstack_heads(o_ref, h, (_dot_tn(_kv_window(vp_ref, vc_ref, h), p) / l).T, tq)

    return pl.pallas_call(
        body, name="swa_fwd", grid=(nq,),
        in_specs=[qs_, cur, prev, cur, prev, bs, sk],
        out_specs=[qs_, rows],
        out_shape=[_sds((t_rows, HEADS * LANES), BF16), _sds((A_KV_HEADS, nq, 1, SWA_GROUP * tq), F32)],
        compiler_params=pltpu.CompilerParams(dimension_semantics=("arbitrary",), vmem_limit_bytes=VMEM_LIMIT),
    )(q, k, k, v, v, bias, sink_rows)


def _swa_bwd(q, k, v, o, do, lse, bias, sink_rows):
    t_rows = q.shape[0]
    tq = min(SWA_TILE, t_rows)
    nq = t_rows // tq
    qs_, cur, prev, bs, rows, sk = _swa_specs(tq, nq)
    hw = SWA_WINDOW
    kvhs = range(A_KV_HEADS)
    kvw = A_KV_HEADS * LANES

    def body(q_ref, kc_ref, kp_ref, vc_ref, vp_ref, o_ref, do_ref, lse_ref, b_ref, sink_ref,
             dq_ref, dk_ref, dv_ref, dsink_ref, ck, cv, dsa):
        i = pl.program_id(0)

        @pl.when(i == 0)
        def _():
            dsa[...] = jnp.zeros(dsa.shape, F32)

        @pl.when(i < nq)
        def _():
            qs = [_stack_heads(q_ref, h) for h in kvhs]
            dos = [_stack_heads(do_ref, h) for h in kvhs]
            kks = [_kv_window(kp_ref, kc_ref, h) for h in kvhs]
            scores = [_dot_nt(kks[h], qs[h]) for h in kvhs]
            dps = [_dot_nt(_kv_window(vp_ref, vc_ref, h), dos[h]) for h in kvhs]
            ps, dss = [], []
            for h in kvhs:
                lse = lse_ref[h, 0]
                p = jnp.exp(scores[h] + b_ref[0] - lse)
                delta = jnp.sum((_stack_heads(o_ref, h).astype(F32) * dos[h].astype(F32)).T, axis=0, keepdims=True)
                dsa[h] += -jnp.exp(sink_ref[h] - lse) * delta
                ps.append(p.astype(BF16))
                dss.append((p * (dps[h] - delta)).astype(BF16))
            for h in kvhs:
                sl = slice(h * LANES, (h + 1) * LANES)
                dv = _dot(ps[h], dos[h])
                dk = _dot(dss[h], qs[h])
                _unstack_heads(dq_ref, h, _dot_tn(dss[h], kks[h]), tq)

                @pl.when(i > 0)
                def _():
                    dk_ref[0:tq - hw, sl] = ck[0:tq - hw, sl]
                    dk_ref[tq - hw:tq, sl] = ck[tq - hw:tq, sl] + dk[0:hw]
                    dv_ref[0:tq - hw, sl] = cv[0:tq - hw, sl]
                    dv_ref[tq - hw:tq, sl] = cv[tq - hw:tq, sl] + dv[0:hw]

                ck[:, sl] = dk[hw:hw + tq]
                cv[:, sl] = dv[hw:hw + tq]

        @pl.when(i == nq)
        def _():
            dk_ref[...] = ck[...]
            dv_ref[...] = cv[...]
            dsink_ref[...] = jnp.zeros(dsink_ref.shape, F32)
            for h in kvhs:
                for g in range(SWA_GROUP):
                    tot = jnp.sum(dsa[h, :, g * tq:(g + 1) * tq], axis=1, keepdims=True)
                    dsink_ref[h, g:g + 1, :] = jnp.zeros((1, LANES), F32) + tot

    kv_out = pl.BlockSpec((tq, kvw), lambda i: (jnp.maximum(i - 1, 0), 0))
    return pl.pallas_call(
        body, name="swa_bwd", grid=(nq + 1,),
        in_specs=[qs_, cur, prev, cur, prev, qs_, qs_, rows, bs, sk],
        out_specs=[qs_, kv_out, kv_out, pl.BlockSpec((A_KV_HEADS, 8, LANES), lambda i: (0, 0, 0))],
        out_shape=[_sds((t_rows, HEADS * LANES), F32), _sds((t_rows, kvw), F32), _sds((t_rows, kvw), F32),
                   _sds((A_KV_HEADS, 8, LANES), F32)],
        scratch_shapes=[pltpu.VMEM((tq, kvw), F32), pltpu.VMEM((tq, kvw), F32),
                        pltpu.VMEM((A_KV_HEADS, 1, SWA_GROUP * tq), F32)],
        compiler_params=pltpu.CompilerParams(dimension_semantics=("arbitrary",), vmem_limit_bytes=VMEM_LIMIT),
    )(q, k, k, v, v, o, do, lse, bias, sink_rows)


def _fwd_mix(x, ya, yb, gate, wba, wbb, wout, g2, g3, tm):
    t_rows = x.shape[0]

    def body(x_ref, ya_ref, yb_ref, gate_ref, wba_ref, wbb_ref, wout_ref, g2_ref, g3_ref,
             pa_ref, pb_ref, mixed_ref, o_ref, x1_ref, h2_ref, yac_ref, ybc_ref):
        yac = _fold_slots(ya_ref[...].astype(F32)).astype(BF16)
        ybc = _fold_slots(yb_ref[...].astype(F32)).astype(BF16)
        yac_ref[...] = yac
        ybc_ref[...] = ybc
        pa = _dot(yac, wba_ref[...])
        pb = _dot(ybc, wbb_ref[...])
        pa_ref[...] = pa.astype(BF16)
        pb_ref[...] = pb.astype(BF16)
        mixed = (gate_ref[:, 0:D_MODEL].astype(F32) * pa
                 + gate_ref[:, D_MODEL:2 * D_MODEL].astype(F32) * pb).astype(BF16)
        mixed_ref[...] = mixed
        o = _dot(mixed, wout_ref[...])
        o_ref[...] = o
        on, _ = _rms_stats(o)
        x1 = x_ref[...] + on * g2_ref[...]
        x1_ref[...] = x1
        x1n, _ = _rms_stats(x1)
        h2_ref[...] = (x1n * g3_ref[...]).astype(BF16)

    def o_(dt):
        return (_sds((t_rows, D_MODEL), dt), _row(tm, D_MODEL))

    ins = [(x, _row(tm, D_MODEL)), (ya, _row(tm, 1024)), (yb, _row(tm, 1024)), (gate, _row(tm, 2048)),
           (wba, _resident(wba.shape)), (wbb, _resident(wbb.shape)), (wout, _resident(wout.shape)),
           (g2, _full(g2.shape)), (g3, _full(g3.shape))]
    half = (_sds((t_rows, D_MODEL // 2), BF16), _row(tm, D_MODEL // 2))
    return _rows_call("fwd_mix", body, t_rows, tm, ins,
                      [o_(BF16), o_(BF16), o_(BF16), o_(F32), o_(F32), o_(BF16), half, half])


CONV_CHUNK = 1408


def _fwd_up(h2, wup, convw8, convb, tm):
    t_rows = h2.shape[0]
    cdim = 2 * D_FF

    def body(h2_ref, wup_ref, cw_ref, cb_ref, up_ref, a_ref, carry):
        i = pl.program_id(0)

        @pl.when(i == 0)
        def _():
            carry[...] = jnp.zeros(carry.shape, F32)

        hb = h2_ref[...]
        ups = [_dot(hb, wup_ref[s]) for s in range(cdim // CONV_CHUNK)]

        def conv(c0):
            sl = slice(c0, c0 + CONV_CHUNK)
            up = ups[c0 // CONV_CHUNK]
            up_ref[:, sl] = up
            xm1, xm2 = _conv_taps(up, carry[6:7, sl], carry[7:8, sl])
            u = cw_ref[0:1, sl] * xm2 + cw_ref[1:2, sl] * xm1 + cw_ref[2:3, sl] * up + cb_ref[:, sl]
            carry[:, sl] = up[tm - 8:tm, :]
            return u

        for c0 in range(0, D_FF, CONV_CHUNK):
            ug = conv(c0)
            uv = conv(D_FF + c0)
            gel, _ = _gelu_and_grad(ug)
            a_ref[:, c0:c0 + CONV_CHUNK] = (gel * uv).astype(BF16)

    ins = [(h2, _row(tm, D_MODEL)), (wup, _resident(wup.shape)), (convw8, _full(convw8.shape)),
           (convb, _full(convb.shape))]
    outs = [(_sds((t_rows, cdim), F32), _row(tm, cdim)), (_sds((t_rows, D_FF), BF16), _row(tm, D_FF))]
    return _rows_call("fwd_up", body, t_rows, tm, ins, outs, scratch=[pltpu.VMEM((8, cdim), F32)])


def _fwd_out(a, wdown, x1, g4, p, wple, g5, wpg, tgt, tm):
    t_rows = a.shape[0]

    def body(a_ref, wdown_ref, x1_ref, g4_ref, p_ref, wple_ref, g5_ref, wpg_ref, tgt_ref,
             ff_ref, x2_ref, e_ref, n5_ref, sg_ref, dx3_ref, loss_ref):
        i = pl.program_id(0)
        ff = _dot(a_ref[...], wdown_ref[...])
        e = _dot(p_ref[...].astype(BF16), wple_ref[...])
        ff_ref[...] = ff
        ffn, _ = _rms_stats(ff)
        x2 = x1_ref[...] + ffn * g4_ref[...]
        x2_ref[...] = x2
        e_ref[...] = e.astype(BF16)
        x2n, _ = _rms_stats(x2)
        n5 = (x2n * g5_ref[...]).astype(BF16)
        n5_ref[...] = n5
        sg = _sigmoid(_dot(n5, wpg_ref[...]))
        sg_ref[...] = sg.astype(BF16)
        d = x2 + sg * e - tgt_ref[...]
        dx3_ref[...] = d * (1.0 / D_MODEL)

        @pl.when(i == 0)
        def _():
            loss_ref[...] = jnp.zeros((1, 1), F32)

        loss_ref[...] += 0.5 * jnp.sum(jnp.sum(d * d, axis=1, keepdims=True), axis=0, keepdims=True) * (1.0 / D_MODEL)

    def o_(dt):
        return (_sds((t_rows, D_MODEL), dt), _row(tm, D_MODEL))

    ins = [(a, _row(tm, D_FF)), (wdown, _resident(wdown.shape)), (x1, _row(tm, D_MODEL)), (g4, _full(g4.shape)),
           (p, _row(tm, PLE_DIM)), (wple, _full(wple.shape)), (g5, _full(g5.shape)), (wpg, _resident(wpg.shape)),
           (tgt, _row(tm, D_MODEL))]
    outs = [o_(F32), o_(F32), o_(BF16), o_(BF16), o_(BF16), o_(F32), (_sds((1, 1), F32), _full((1, 1)))]
    return _rows_call("fwd_out", body, t_rows, tm, ins, outs)


def _bwd_out(dx3, e, sg, x2, ff, g5, g4, wpg, wdown, up, convw8, convb, tm):
    t_rows = dx3.shape[0]
    cdim = 2 * D_FF
    hb = tm // 8

    def body(dx3_ref, e_ref, sg_ref, x2_ref, ff_ref, g5_ref, g4_ref, wpg_ref, wdown_ref, up_ref, halo_ref, cw_ref,
             cb_ref, dpre_ref, de_ref, dx2_ref, dff_ref, du_ref, dg5_ref, dg4_ref, dcb_ref, dcw_ref):
        i = pl.program_id(0)

        @pl.when(i == 0)
        def _():
            dg5_ref[...] = jnp.zeros(dg5_ref.shape, F32)
            dg4_ref[...] = jnp.zeros(dg4_ref.shape, F32)
            dcb_ref[...] = jnp.zeros(dcb_ref.shape, F32)
            dcw_ref[...] = jnp.zeros(dcw_ref.shape, F32)

        dx3 = dx3_ref[...]
        sg = sg_ref[...].astype(F32)
        dpre = (dx3 * e_ref[...].astype(F32) * sg * (1.0 - sg)).astype(BF16)
        dpre_ref[...] = dpre
        de_ref[...] = (dx3 * sg).astype(BF16)
        dn5 = _dot_nt(dpre, wpg_ref[...])
        x2n, r5 = _rms_stats(x2_ref[...])
        d2, dg5 = _rms_bwd(dn5, x2n, r5, g5_ref[...])
        dx2 = dx3 + d2
        dx2_ref[...] = dx2
        dg5_ref[...] += dg5
        ffn, r4 = _rms_stats(ff_ref[...])
        dff, dg4 = _rms_bwd(dx2, ffn, r4, g4_ref[...])
        dg4_ref[...] += dg4
        dffb = dff.astype(BF16)
        dff_ref[...] = dffb
        keep = jnp.where(i > 0, 1.0, 0.0)

        def conv(c0):
            sl = slice(c0, c0 + CONV_CHUNK)
            up = up_ref[:, sl]
            xm1, xm2 = _conv_taps(up, halo_ref[6:7, sl] * keep, halo_ref[7:8, sl] * keep)
            u = cw_ref[0:1, sl] * xm2 + cw_ref[1:2, sl] * xm1 + cw_ref[2:3, sl] * up + cb_ref[:, sl]
            return u, up, xm1, xm2

        def grads(c0, du, up, xm1, xm2):
            sl = slice(c0, c0 + CONV_CHUNK)
            du_ref[:, sl] = du.astype(BF16)
            dcb_ref[:, sl] += jnp.sum(du, axis=0, keepdims=True)
            dcw_ref[0:1, sl] += jnp.sum(du * xm2, axis=0, keepdims=True)
            dcw_ref[1:2, sl] += jnp.sum(du * xm1, axis=0, keepdims=True)
            dcw_ref[2:3, sl] += jnp.sum(du * up, axis=0, keepdims=True)

        for c0 in range(0, D_FF, CONV_CHUNK):
            da = _dot_nt(dffb, wdown_ref[c0:c0 + CONV_CHUNK, :])
            ug, *rg = conv(c0)
            uv, *rv = conv(D_FF + c0)
            gel, dgel = _gelu_and_grad(ug)
            grads(c0, da * uv * dgel, *rg)
            grads(D_FF + c0, da * gel, *rv)

    def o_(n, dt):
        return (_sds((t_rows, n), dt), _row(tm, n))

    def acc(r, n):
        return (_sds((r, n), F32), _full((r, n)))

    halo = pl.BlockSpec((8, cdim), lambda i: (jnp.maximum(i * hb - 1, 0), 0))
    ins = [(dx3, _row(tm, D_MODEL)), (e, _row(tm, D_MODEL)), (sg, _row(tm, D_MODEL)), (x2, _row(tm, D_MODEL)),
           (ff, _row(tm, D_MODEL)), (g5, _full(g5.shape)), (g4, _full(g4.shape)), (wpg, _resident(wpg.shape)),
           (wdown, _resident(wdown.shape)), (up, _row(tm, cdim)), (up, halo), (convw8, _full(convw8.shape)),
           (convb, _full(convb.shape))]
    outs = [o_(D_MODEL, BF16), o_(D_MODEL, BF16), o_(D_MODEL, F32), o_(D_MODEL, BF16), o_(cdim, BF16),
            acc(1, D_MODEL), acc(1, D_MODEL), acc(1, cdim), acc(8, cdim)]
    return _rows_call("bwd_out", body, t_rows, tm, ins, outs)


def _bwd_mid(du, convw8, wup, dx2, x1, g3, o, g2, wout, gate, pa, pb, wba, wbb, yb, tm):
    t_rows = du.shape[0]
    cdim = 2 * D_FF
    halo_rows = 16
    hb = tm // halo_rows
    last_blk = t_rows // halo_rows - 1
    n_tiles = t_rows // tm

    def body(du_ref, halo_ref, cw_ref, wup_ref, dx2_ref, x1_ref, g3_ref, o_ref, g2_ref, wout_ref, gate_ref, pa_ref,
             pb_ref, wba_ref, wbb_ref, yb_ref,
             dup_ref, dx1_ref, do_ref, dpa_ref, dpb_ref, dgt_ref, dya_ref, dyb_ref, dl_ref, dg3_ref, dg2_ref, dbg_ref):
        i = pl.program_id(0)

        @pl.when(i == 0)
        def _():
            dg3_ref[...] = jnp.zeros(dg3_ref.shape, F32)
            dg2_ref[...] = jnp.zeros(dg2_ref.shape, F32)
            dbg_ref[...] = jnp.zeros(dbg_ref.shape, F32)

        keep = jnp.where(i < n_tiles - 1, 1.0, 0.0)
        dh2 = jnp.zeros((tm, D_MODEL), F32)
        dups = []
        for c0 in range(0, cdim, CONV_CHUNK):
            sl = slice(c0, c0 + CONV_CHUNK)
            du = du_ref[:, sl].astype(F32)
            nxt = halo_ref[:, sl].astype(F32)
            xp1, xp2 = _conv_taps_next(du, nxt[0:1] * keep, nxt[1:2] * keep)
            dups.append((cw_ref[2:3, sl] * du + cw_ref[1:2, sl] * xp1 + cw_ref[0:1, sl] * xp2).astype(BF16))
            dup_ref[:, sl] = dups[-1]
            if len(dups) > 1:
                dh2 = dh2 + _dot_nt(dups[-2], wup_ref[len(dups) - 2])
        dh2 = dh2 + _dot_nt(dups[-1], wup_ref[len(dups) - 1])
        x1n, r3 = _rms_stats(x1_ref[...])
        d1, dg3 = _rms_bwd(dh2, x1n, r3, g3_ref[...])
        dx1 = dx2_ref[...] + d1
        dx1_ref[...] = dx1
        dg3_ref[...] += dg3
        on, r2 = _rms_stats(o_ref[...])
        do, dg2 = _rms_bwd(dx1, on, r2, g2_ref[...])
        dg2_ref[...] += dg2
        dob = do.astype(BF16)
        do_ref[...] = dob
        dmixed = _dot_nt(dob, wout_ref[...])
        ga = gate_ref[:, 0:D_MODEL].astype(F32)
        gb = gate_ref[:, D_MODEL:2 * D_MODEL].astype(F32)
        dpa = (dmixed * ga).astype(BF16)
        dpb = (dmixed * gb).astype(BF16)
        dpa_ref[...] = dpa
        dpb_ref[...] = dpb
        dga = dmixed * pa_ref[...].astype(F32) * ga * (1.0 - ga)
        dgb = dmixed * pb_ref[...].astype(F32) * gb * (1.0 - gb)
        dgt_ref[:, 0:D_MODEL] = dga.astype(BF16)
        dgt_ref[:, D_MODEL:2 * D_MODEL] = dgb.astype(BF16)
        dbg_ref[:, 0:D_MODEL] += jnp.sum(dga, axis=0, keepdims=True)
        dbg_ref[:, D_MODEL:2 * D_MODEL] += jnp.sum(dgb, axis=0, keepdims=True)
        dya_ref[...] = _spread_slots(_dot_nt(dpa, wba_ref[...])).astype(BF16)
        dyb = _dot_nt(dpb, wbb_ref[...]).astype(BF16)
        dyb_ref[...] = _spread_slots(dyb.astype(F32)).astype(BF16)
        prod = yb_ref[...].astype(F32) * dyb.astype(F32)
        width = HEADS * V_DIM
        lane_head = lax.broadcasted_iota(jnp.int32, (HEADS, width), 1) // V_DIM
        sel = (lane_head == lax.broadcasted_iota(jnp.int32, (HEADS, width), 0)).astype(BF16)
        hi = prod.astype(BF16)
        lo = (prod - hi.astype(F32)).astype(BF16)
        dl_ref[...] = _dot_nt(sel, hi) + _dot_nt(sel, lo)

    def o_(n, dt):
        return (_sds((t_rows, n), dt), _row(tm, n))

    def acc(r, n):
        return (_sds((r, n), F32), _full((r, n)))

    halo = pl.BlockSpec((halo_rows, cdim), lambda i: (jnp.minimum((i + 1) * hb, last_blk), 0))
    ins = [(du, _row(tm, cdim)), (du, halo), (convw8, _full(convw8.shape)), (wup, _resident(wup.shape)),
           (dx2, _row(tm, D_MODEL)), (x1, _row(tm, D_MODEL)), (g3, _full(g3.shape)), (o, _row(tm, D_MODEL)),
           (g2, _full(g2.shape)), (wout, _resident(wout.shape)), (gate, _row(tm, 2048)), (pa, _row(tm, D_MODEL)),
           (pb, _row(tm, D_MODEL)), (wba, _resident(wba.shape)), (wbb, _resident(wbb.shape)),
           (yb, _row(tm, D_MODEL // 2))]
    outs = [o_(cdim, BF16), o_(D_MODEL, F32), o_(D_MODEL, BF16), o_(D_MODEL, BF16), o_(D_MODEL, BF16),
            o_(2048, BF16), o_(1024, BF16), o_(1024, BF16),
            (_sds((HEADS, t_rows), F32), pl.BlockSpec((HEADS, tm), lambda i: (0, i))),
            acc(1, D_MODEL), acc(1, D_MODEL), acc(1, 2048)]
    return _rows_call("bwd_mid", body, t_rows, tm, ins, outs)


def _bwd_in(dqs, dks, dvs, dqm, dkm, dvm, tabs, consts, cq, ckv, gq, gkv, wuq, wk, wv, dgates, win, x, g1, dx1, tm):
    t_rows = x.shape[0]

    def body(dqs_ref, dks_ref, dvs_ref, dqm_ref, dkm_ref, dvm_ref, ca, sa1, sa2, cb, sb1, sb2, c_ref, cq_ref,
             ckv_ref, gq_ref, gkv_ref, wuq_ref, wk_ref, wv_ref, dgt_ref, win_ref, x_ref, g1_ref, dx1_ref,
             dz_ref, dqb_ref, dx_ref, dgq_ref, dgkv_ref, dg1_ref):
        i = pl.program_id(0)

        @pl.when(i == 0)
        def _():
            dgq_ref[...] = jnp.zeros(dgq_ref.shape, F32)
            dgkv_ref[...] = jnp.zeros(dgkv_ref.shape, F32)
            dg1_ref[...] = jnp.zeros(dg1_ref.shape, F32)

        ta = (ca[...], sa1[...], sa2[...])
        tb = (cb[...], sb1[...], sb2[...])

        def piece(lo, hi, val):
            dz_ref[:, lo:hi] = val
            return _dot_nt(val, win_ref[:, lo:hi])

        dh1 = piece(Z_GATE, ZW, dgt_ref[...])
        dkm = dkm_ref[...]
        dckvn = _dot_nt(dkm.astype(BF16), wk_ref[...]) + _dot_nt(dvm_ref[...].astype(BF16), wv_ref[...])
        dh1 = dh1 + piece(Z_VA, Z_CQ, _fold_slots(dvs_ref[...]).astype(BF16))
        dqm = jnp.concatenate([dqm_ref[h] for h in range(HEADS)], axis=1)
        dqb = _rope_t(dqm * SCALE_B, *tb, ROPE_DIM // 2).astype(BF16)
        dqb_ref[...] = dqb
        dcqn = _dot_nt(dqb, wuq_ref[...])
        dqa = _rope_t(_fold_slots(dqs_ref[...]) * SCALE_A, *ta, A_HEAD_DIM // 2)
        dh1 = dh1 + piece(Z_QA, Z_KA, dqa.astype(BF16))
        dh1 = dh1 + piece(Z_KA, Z_VA, _rope_t(_fold_slots(dks_ref[...]), *ta, A_HEAD_DIM // 2).astype(BF16))
        ckvn, rkv = _rms_stats(ckv_ref[...])
        dckv, dgkv = _rms_bwd(dckvn, ckvn, rkv, gkv_ref[...])
        dgkv_ref[...] += dgkv
        dh1 = dh1 + piece(Z_CKV, Z_KR, dckv.astype(BF16))
        dslot = dkm[:, 0:LANES]
        for h in range(1, HEADS):
            dslot = dslot + dkm[:, h * LANES:(h + 1) * LANES]
        dh1 = dh1 + piece(Z_KR, Z_GATE, _rope_t(dslot * c_ref[10:11, :], *tb, ROPE_DIM // 2).astype(BF16))
        cqn, rq = _rms_stats(cq_ref[...])
        dcq, dgq = _rms_bwd(dcqn, cqn, rq, gq_ref[...])
        dgq_ref[...] += dgq
        dh1 = dh1 + piece(Z_CQ, Z_CKV, dcq.astype(BF16))
        xn, r1 = _rms_stats(x_ref[...])
        d0, dg1 = _rms_bwd(dh1, xn, r1, g1_ref[...])
        dg1_ref[...] += dg1
        dx_ref[...] = dx1_ref[...] + d0

    def acc(n):
        return (_sds((1, n), F32), _full((1, n)))

    ins = [(dqs, _row(tm, 1024)), (dks, _row(tm, 256)), (dvs, _row(tm, 256)), (dqm, _heads(tm, HEADS)),
           (dkm, _row(tm, 1024)), (dvm, _row(tm, 1024))] + [(t, _row(tm, LANES)) for t in tabs] + [
           (consts, _full(consts.shape)), (cq, _row(tm, 256)), (ckv, _row(tm, 128)), (gq, _full(gq.shape)),
           (gkv, _full(gkv.shape)), (wuq, _full(wuq.shape)), (wk, _full(wk.shape)), (wv, _full(wv.shape)),
           (dgates, _row(tm, 2048)), (win, _resident(win.shape)), (x, _row(tm, D_MODEL)), (g1, _full(g1.shape)),
           (dx1, _row(tm, D_MODEL))]
    outs = [(_sds((t_rows, ZW), BF16), _row(tm, ZW)), (_sds((t_rows, 1024), BF16), _row(tm, 1024)),
            (_sds((t_rows, D_MODEL), F32), _row(tm, D_MODEL)), acc(256), acc(128), acc(D_MODEL)]
    return _rows_call("bwd_in", body, t_rows, tm, ins, outs)


def _pick_cols(n):
    best = LANES
    for d in range(LANES, min(n, 1664) + 1, LANES):
        if n % d == 0:
            best = d
    return best


def _mm_tn(name, a, b, column_shards=1, after=None):
    t_rows, m = a.shape
    n = b.shape[1]
    bk = min(2048, t_rows)
    bm, bn = _pick_cols(m), _pick_cols(n // column_shards)
    per_shard = n // column_shards // bn
    extra = () if after is None else (after,)

    def body(a_ref, b_ref, *rest):
        o_ref = rest[-1]

        @pl.when(pl.program_id(2) == 0)
        def _():
            o_ref[...] = jnp.zeros((bm, bn), F32)

        o_ref[...] += _dot_tn(a_ref[...].astype(BF16), b_ref[...].astype(BF16))

    return pl.pallas_call(
        body, name=name, grid=(m // bm, n // bn, t_rows // bk),
        in_specs=[pl.BlockSpec((bk, bm), lambda i, j, k: (k, i)), pl.BlockSpec((bk, bn), lambda i, j, k: (k, j))]
        + [pl.BlockSpec((8, LANES), lambda i, j, k: (0, 0))] * len(extra),
        out_specs=(pl.BlockSpec((bm, bn), lambda i, j, k: (i, j)) if column_shards == 1 else
                   pl.BlockSpec((None, bm, bn), lambda i, j, k: (j // per_shard, i, j % per_shard))),
        out_shape=_sds((m, n) if column_shards == 1 else (column_shards, m, n // column_shards), F32),
        compiler_params=pltpu.CompilerParams(dimension_semantics=("arbitrary",) * 3, vmem_limit_bytes=VMEM_LIMIT),
    )(a, b, *extra)


PACK_ROWS = 512


ADD_TILE_ELEMS = 1 << 17


def _add_rows(rows, cols):
    best = 16
    for d in range(16, rows + 1, 16):
        if rows % d == 0 and d * cols <= ADD_TILE_ELEMS:
            best = d
    assert rows % best == 0
    return best


def _add_pair(name, g, recv, half):
    _, _, rows, cols = g.shape
    t = _add_rows(rows, cols)

    def body(h_ref, g_ref, r_ref, o_ref):
        o_ref[...] = (g_ref[:, 0] + r_ref[...]).astype(BF16)

    spec = pl.BlockSpec((4, t, cols), lambda i, h: (0, i, 0))
    grid_spec = pltpu.PrefetchScalarGridSpec(
        num_scalar_prefetch=1, grid=(rows // t,),
        in_specs=[pl.BlockSpec((4, 1, t, cols), lambda i, h: (0, h[0], i, 0)), spec], out_specs=spec)
    return pl.pallas_call(body, name=name, grid_spec=grid_spec,
                          out_shape=_sds(recv.shape, BF16))(jnp.reshape(half, (1,)).astype(jnp.int32), g, recv)


def _add_chips(name, parts):
    _, rows, cols = parts.shape
    t = _add_rows(rows, cols)

    def body(p_ref, o_ref):
        acc = p_ref[0].astype(F32)
        for j in range(1, 4):
            acc = acc + p_ref[j].astype(F32)
        o_ref[...] = acc

    return pl.pallas_call(body, name=name, grid=(rows // t,),
                          in_specs=[pl.BlockSpec((4, t, cols), lambda i: (0, i, 0))],
                          out_specs=pl.BlockSpec((t, cols), lambda i: (i, 0)),
                          out_shape=_sds((rows, cols), F32))(parts)


def _add_devices(parts):
    n, rows, _ = parts.shape

    def body(p_ref, o_ref):
        acc = p_ref[0]
        for j in range(1, n):
            acc = acc + p_ref[j]
        o_ref[...] = acc

    return pl.pallas_call(body, name="small_add", grid=(1,),
                          in_specs=[pl.BlockSpec((n, rows, LANES), lambda i: (0, 0, 0))],
                          out_specs=pl.BlockSpec((rows, LANES), lambda i: (0, 0)),
                          out_shape=_sds((rows, LANES), F32))(parts)


def _adam_rows(k, n):
    target = max(8, (1 << 20) // (4 * n))
    if k <= target:
        return k
    best = None
    for d in range(8, target + 1, 8):
        if k % d == 0:
            best = d
    return best if best is not None else k


def _adam_update(w, g, m, v):
    m_ = ADAM_B1 * m + (1.0 - ADAM_B1) * g
    v_ = ADAM_B2 * v + (1.0 - ADAM_B2) * (g * g)
    delta = -ADAM_LR * ((m_ / (1.0 - ADAM_B1 ** ADAM_STEP)) / (jnp.sqrt(v_ / (1.0 - ADAM_B2 ** ADAM_STEP)) + ADAM_EPS)
                        + ADAM_WD * w)
    return delta, m_, v_


def _adamw_many(name, ws, gs, ms, vs):
    n = len(ws)

    def body(*refs):
        for i in range(n):
            w_ref, g_ref, m_ref, v_ref = (refs[k * n + i] for k in range(4))
            d_ref, mo_ref, vo_ref = (refs[(4 + k) * n + i] for k in range(3))
            d_ref[...], mo_ref[...], vo_ref[...] = _adam_update(w_ref[...], g_ref[...], m_ref[...], v_ref[...])

    specs = [pl.BlockSpec(w.shape, lambda i: (0, 0)) for w in ws]
    out = pl.pallas_call(body, name=name, grid=(1,), in_specs=specs * 4, out_specs=specs * 3,
                         out_shape=[_sds(w.shape, F32) for w in ws] * 3)(*ws, *gs, *ms, *vs)
    return [(gs[i], out[i], out[n + i], out[2 * n + i]) for i in range(n)]


def _adamw_halves(name, w, mine, theirs, m, v, half):
    k, n = w.shape
    bk = _adam_rows(k // 2, n)
    nb = k // 2 // bk

    def body(h_ref, w_ref, mine_ref, theirs_ref, m_ref, v_ref, g_ref, d_ref, mo_ref, vo_ref):
        g = jnp.where(pl.program_id(0) == h_ref[0], mine_ref[...], theirs_ref[...])
        g_ref[...] = g
        d_ref[...], mo_ref[...], vo_ref[...] = _adam_update(w_ref[...], g, m_ref[...], v_ref[...])

    full = pl.BlockSpec((bk, n), lambda h, i, c: (h * nb + i, 0))
    part = pl.BlockSpec((bk, n), lambda h, i, c: (i, 0))
    grid_spec = pltpu.PrefetchScalarGridSpec(num_scalar_prefetch=1, grid=(2, nb),
                                             in_specs=[full, part, part, full, full], out_specs=[full] * 4)
    return tuple(pl.pallas_call(
        body, name=name, grid_spec=grid_spec, out_shape=[_sds((k, n), F32)] * 4,
        compiler_params=pltpu.CompilerParams(vmem_limit_bytes=VMEM_LIMIT),
    )(jnp.reshape(half, (1,)).astype(jnp.int32), w, mine, theirs, m, v))


_HBM = pl.BlockSpec(memory_space=pltpu.HBM)


def _me():
    return lax.axis_index("x"), lax.axis_index("y"), lax.axis_index("c")


def _other_chips(x, y):
    return [(1 - x, y), (x, 1 - y), (1 - x, 1 - y)]


def _pass_to_sibling(zones):
    n = len(zones)

    def body(*refs):
        in_refs, out_refs = refs[:n], refs[n:2 * n]
        send_sems, recv_sems = refs[2 * n:]
        x, y, c = _me()
        sent = []
        for a, (in_ref, out_ref) in enumerate(zip(in_refs, out_refs)):
            for j, (cx, cy) in enumerate(_other_chips(x, y)):
                mine, theirs = (2 * cx + cy, c), (2 * cx + cy, 1 - c)
                sems = dict(send_sem=send_sems.at[3 * a + j], recv_sem=recv_sems.at[3 * a + j],
                            device_id=(x, y, 1 - c), device_id_type=MESH)
                sent.append(tuple(pltpu.make_async_remote_copy(src_ref=in_ref.at[part], dst_ref=out_ref.at[part], **sems)
                                  for part in (mine, theirs)))
        for send, _ in sent:
            send.start()
        for _, recv in sent:
            recv.wait_recv()
        for send, _ in sent:
            send.wait_send()

    return pl.pallas_call(
        body, name="pass_to_sibling", out_shape=[_sds(z.shape, z.dtype) for z in zones],
        in_specs=[_HBM] * n, out_specs=[_HBM] * n, input_output_aliases={i: i for i in range(n)},
        scratch_shapes=[pltpu.SemaphoreType.DMA((3 * n,)), pltpu.SemaphoreType.DMA((3 * n,))],
    )(*zones)


def _swap_sibling(name, vs, other_half=False):
    n = len(vs)

    def body(*refs):
        v_refs, out_refs = refs[:n], refs[n:2 * n]
        send_sems, recv_sems = refs[2 * n:]
        x, y, c = _me()
        cps = [pltpu.make_async_remote_copy(src_ref=v_ref.at[:, 1 - c] if other_half else v_ref, dst_ref=out_ref,
                                            send_sem=send_sems.at[a], recv_sem=recv_sems.at[a],
                                            device_id=(x, y, 1 - c), device_id_type=MESH)
               for a, (v_ref, out_ref) in enumerate(zip(v_refs, out_refs))]
        for cp in cps:
            cp.start()
        for cp in cps:
            cp.wait()

    def landing(v):
        return _sds((v.shape[0],) + v.shape[2:] if other_half else v.shape, v.dtype)

    return pl.pallas_call(
        body, name=name, out_shape=[landing(v) for v in vs], in_specs=[_HBM] * n, out_specs=[_HBM] * n,
        scratch_shapes=[pltpu.SemaphoreType.DMA((n,)), pltpu.SemaphoreType.DMA((n,))],
    )(*vs)


_SEM = pl.BlockSpec(memory_space=pltpu.SEMAPHORE)
_EFFECT = pltpu.SideEffectType.DATAFLOW_SIDE_EFFECTING
WHOLE = "whole"
PIECE = "piece"
SIBLING_HALF = "sibling"
MY_HALF = "half"
EVERYONE = "everyone"
_COPIES = {WHOLE: 3, PIECE: 3, MY_HALF: 3, SIBLING_HALF: 1, EVERYONE: 7}


def _landing_shape(v, mode):
    return {WHOLE: (4,) + v.shape, MY_HALF: (4,) + v.shape, PIECE: v.shape, EVERYONE: (8,) + v.shape,
            SIBLING_HALF: (v.shape[0],) + v.shape[2:]}[mode]


def _chip_copies(v_ref, land_ref, send_sems, recv_sems, mode, sem0=0):
    x, y, c = _me()
    if mode == SIBLING_HALF:
        cp = pltpu.make_async_remote_copy(src_ref=v_ref.at[:, 1 - c], dst_ref=land_ref, send_sem=send_sems.at[sem0],
                                          recv_sem=recv_sems.at[sem0], device_id=(x, y, 1 - c), device_id_type=MESH)
        return [(cp, cp)]
    if mode == EVERYONE:
        out = []
        for f in range(1, 8):
            px, py, pc = (1 - x if f & 4 else x), (1 - y if f & 2 else y), (1 - c if f & 1 else c)
            sems = dict(send_sem=send_sems.at[sem0 + f - 1], recv_sem=recv_sems.at[sem0 + f - 1],
                        device_id=(px, py, pc), device_id_type=MESH)
            out.append((pltpu.make_async_remote_copy(src_ref=v_ref, dst_ref=land_ref.at[4 * x + 2 * y + c], **sems),
                        pltpu.make_async_remote_copy(src_ref=v_ref, dst_ref=land_ref.at[4 * px + 2 * py + pc], **sems)))
        return out
    k = 2 * x + y
    out = []
    for j, (cx, cy) in enumerate(_other_chips(x, y)):
        if mode == MY_HALF:
            src, mine, theirs = v_ref.at[c], land_ref.at[k, c], land_ref.at[2 * cx + cy, c]
        else:
            src = v_ref.at[2 * cx + cy] if mode == PIECE else v_ref
            mine, theirs = land_ref.at[k], land_ref.at[2 * cx + cy]
        sems = dict(send_sem=send_sems.at[sem0 + j], recv_sem=recv_sems.at[sem0 + j], device_id=(cx, cy, c),
                    device_id_type=MESH)
        send = pltpu.make_async_remote_copy(src_ref=src, dst_ref=mine, **sems)
        recv = pltpu.make_async_remote_copy(src_ref=src, dst_ref=theirs, **sems)
        out.append((send, recv))
    return out


def _chips_start(name, vs, mode, after=None):
    n = len(vs)
    lands = [_landing_shape(v, mode) for v in vs]

    def body(*refs):
        v_refs, land_refs = refs[:n], refs[n:2 * n]
        send_sems, recv_sems = refs[-2 * n - 3], refs[-2 * n - 2]
        token = refs[-1]
        for a in range(n):
            for send, _ in _chip_copies(v_refs[a], land_refs[a], send_sems, recv_sems, mode, _COPIES[mode] * a):
                send.start()
        token[...] = jnp.zeros_like(token)

    extra = () if after is None else (after,)
    hbm = [pltpu.with_memory_space_constraint(v, pltpu.HBM) for v in vs]
    zones = [pltpu.with_memory_space_constraint(lax.empty(s, v.dtype), pltpu.HBM) for s, v in zip(lands, vs)]
    out = pl.pallas_call(
        body, name=name,
        out_shape=(pltpu.SemaphoreType.DMA((_COPIES[mode] * n,)), pltpu.SemaphoreType.DMA((_COPIES[mode] * n,)),
                   *[pltpu.HBM(v.shape, v.dtype) for v in vs], *[pltpu.HBM(s, v.dtype) for s, v in zip(lands, vs)],
                   _sds((8, LANES), F32)),
        in_specs=(_HBM,) * (2 * n) + (pl.BlockSpec(memory_space=pl.ANY),) * len(extra),
        out_specs=(_SEM, _SEM) + (_HBM,) * (2 * n) + (pl.BlockSpec(memory_space=pltpu.VMEM),),
        input_output_aliases={i: 2 + i for i in range(2 * n)},
        compiler_params=pltpu.CompilerParams(has_side_effects=_EFFECT),
    )(*hbm, *zones, *extra)
    return out[0], out[1], list(out[2:2 + n]), list(out[2 + n:2 + 2 * n]), out[-1]


def _chips_wait(name, send_sems, recv_sems, v_thru, land_thru, mode, after):
    n = len(v_thru)

    def body(*refs):
        v_refs, land_refs = refs[:n], refs[n:2 * n]
        send_sems, recv_sems = refs[2 * n], refs[2 * n + 1]
        for a in range(n):
            for send, recv in _chip_copies(v_refs[a], land_refs[a], send_sems, recv_sems, mode, _COPIES[mode] * a):
                send.wait_send()
                recv.wait_recv()

    out = pl.pallas_call(
        body, name=name,
        out_shape=tuple(pltpu.HBM(a.shape, a.dtype) for a in list(v_thru) + list(land_thru)),
        in_specs=(_HBM,) * (2 * n) + (_SEM, _SEM, pl.BlockSpec(memory_space=pl.ANY)), out_specs=(_HBM,) * (2 * n),
        input_output_aliases={i: i for i in range(2 * n)},
        compiler_params=pltpu.CompilerParams(has_side_effects=_EFFECT),
    )(*v_thru, *land_thru, send_sems, recv_sems, after)
    return list(out[:n]), list(out[n:])


_BIG = (("w_in", (1024, 3232), 1), ("w_uq", (256, 768), 1), ("w_ukv", (128, 1024), 1), ("w_branch_a", (512, 1024), 1),
        ("w_branch_b", (512, 1024), 1), ("w_out", (1024, 1024), 0), ("w_up", (1024, 5632), 1),
        ("w_down", (2816, 1024), 0), ("w_ple_gate", (1024, 1024), 0), ("w_ple", (256, 1024), 1))


def _shard_shape(shape, axis):
    return (shape[0] // 4, shape[1]) if axis == 0 else (shape[0], shape[1] // 4)


def _half_rows(shape, axis):
    k, n = _shard_shape(shape, axis)
    return k * n // (2 * LANES)


_EARLY = ("w_in", "w_uq", "w_ukv")
_LATE = ("w_branch_a", "w_branch_b", "w_out", "w_up", "w_down", "w_ple_gate", "w_ple")
_NATURAL = ("w_in", "w_up", "w_down", "w_out", "w_ple_gate")
_EARLY_PACKED = tuple(b for b in _BIG if b[0] in _EARLY and b[0] not in _NATURAL)
_LATE_PACKED = tuple(b for b in _BIG if b[0] in _LATE and b[0] not in _NATURAL)


def _halves(a):
    return a.reshape(a.shape[:-2] + (2, a.shape[-2] // 2, a.shape[-1]))


def _rows_joined(a):
    return a.reshape(a.shape[:-3] + (a.shape[-3] * a.shape[-2], a.shape[-1]))


def _pack_pad(group):
    return -sum(_half_rows(shape, axis) for _, shape, axis in group) % PACK_ROWS


def _pack_shards(shards, dtype, group):
    parts = [shards[name].astype(dtype).reshape(2, _half_rows(shape, axis), LANES) for name, shape, axis in group]
    return jnp.concatenate(parts + [jnp.zeros((2, _pack_pad(group), LANES), dtype)], axis=1)


def _unpack_gathered(g, group):
    out, off = {}, 0
    for name, shape, axis in group:
        r = _half_rows(shape, axis)
        k, n = _shard_shape(shape, axis)
        w = g[:, :, off:off + r, :].reshape(4, k, n)
        out[name] = w.reshape(shape) if axis == 0 else w.transpose(1, 0, 2).reshape(shape)
        off += r
    return out


def _pack_grads(grads, group):
    parts = []
    for name, shape, axis in group:
        k, n = _shard_shape(shape, axis)
        g = grads[name]
        g4 = g.reshape(4, k, n) if axis == 0 else g.reshape(k, 4, n).transpose(1, 0, 2)
        parts.append(g4.reshape(4, 2, _half_rows(shape, axis), LANES))
    return jnp.concatenate(parts + [jnp.zeros((4, 2, _pack_pad(group), LANES), F32)], axis=2)


def _unpack_shard_grads(f, group):
    out, off = {}, 0
    for name, shape, axis in group:
        r = _half_rows(shape, axis)
        out[name] = f[:, off:off + r, :].reshape(_shard_shape(shape, axis))
        off += r
    return out


def _pad_slots(w, heads, dim):
    k = w.shape[0]
    return jnp.pad(w.reshape(k, heads, dim), ((0, 0), (0, 0), (0, LANES - dim))).reshape(k, heads * LANES)


def _unpad_slots(w, heads, dim):
    k = w.shape[0]
    return w.reshape(k, heads, LANES)[:, :, :dim].reshape(k, heads * dim)


def _pad_w_in(w):
    kr = jnp.pad(w[:, Z_KR:Z_KR + ROPE_DIM], ((0, 0), (NOPE_DIM, LANES - NOPE_DIM - ROPE_DIM)))
    return jnp.concatenate([w[:, :Z_KR], kr, w[:, Z_KR + ROPE_DIM:]], axis=1)


def _unpad_w_in(w):
    return jnp.concatenate([w[:, :Z_KR], w[:, Z_KR + NOPE_DIM:Z_KR + NOPE_DIM + ROPE_DIM], w[:, Z_GATE:ZW]], axis=1)


def _spread_matrix(heads, dim):
    row = lax.broadcasted_iota(jnp.int32, (heads * dim, heads * LANES), 0)
    col = lax.broadcasted_iota(jnp.int32, (heads * dim, heads * LANES), 1)
    return (col == (row // dim) * LANES + row % dim).astype(BF16)


_SMALL = (("attn_pre_norm", 1024), ("attn_post_norm", 1024), ("b_gate", 2048), ("sinks", 8), ("q_a_norm", 256),
          ("kv_a_norm", 128), ("mlp_pre_norm", 1024), ("mlp_post_norm", 1024), ("conv_b", 5632), ("ple_norm", 1024),
          ("conv_w", 3 * 5632), ("loss", 1))


def _small_rows(n):
    return 8 * -(-n // (8 * LANES))


def _pack_small(vals):
    parts = []
    for name, n in _SMALL:
        r = _small_rows(n)
        parts.append(jnp.pad(vals[name].reshape(-1), (0, r * LANES - n)).reshape(r, LANES))
    return jnp.concatenate(parts, axis=0)


def _unpack_small(buf):
    out, off = {}, 0
    for name, n in _SMALL:
        r = _small_rows(n)
        out[name] = buf[off:off + r].reshape(-1)[:n]
        off += r
    return out


def kernel(x, p, positions, attn_pre_norm, attn_post_norm, w_in, b_gate, sinks, q_a_norm, w_uq, kv_a_norm, w_ukv, w_branch_a, w_branch_b, w_out, mlp_pre_norm, mlp_post_norm, w_up, conv_w, conv_b, w_down, ple_norm, w_ple_gate, w_ple, loss_target, m_attn_pre_norm, m_attn_post_norm, m_w_in, m_b_gate, m_sinks, m_q_a_norm, m_w_uq, m_kv_a_norm, m_w_ukv, m_w_branch_a, m_w_branch_b, m_w_out, m_mlp_pre_norm, m_mlp_post_norm, m_w_up, m_conv_w, m_conv_b, m_w_down, m_ple_norm, m_w_ple_gate, m_w_ple, v_attn_pre_norm, v_attn_post_norm, v_w_in, v_b_gate, v_sinks, v_q_a_norm, v_w_uq, v_kv_a_norm, v_w_ukv, v_w_branch_a, v_w_branch_b, v_w_out, v_mlp_pre_norm, v_mlp_post_norm, v_w_up, v_conv_w, v_conv_b, v_w_down, v_ple_norm, v_w_ple_gate, v_w_ple):
    names = ["attn_pre_norm", "attn_post_norm", "w_in", "b_gate", "sinks", "q_a_norm", "w_uq", "kv_a_norm", "w_ukv",
             "w_branch_a", "w_branch_b", "w_out", "mlp_pre_norm", "mlp_post_norm", "w_up", "conv_w", "conv_b",
             "w_down", "ple_norm", "w_ple_gate", "w_ple"]
    wts = dict(zip(names, [attn_pre_norm, attn_post_norm, w_in, b_gate, sinks, q_a_norm, w_uq, kv_a_norm, w_ukv,
                           w_branch_a, w_branch_b, w_out, mlp_pre_norm, mlp_post_norm, w_up, conv_w, conv_b, w_down,
                           ple_norm, w_ple_gate, w_ple]))
    moms = dict(zip(names, [m_attn_pre_norm, m_attn_post_norm, m_w_in, m_b_gate, m_sinks, m_q_a_norm, m_w_uq,
                            m_kv_a_norm, m_w_ukv, m_w_branch_a, m_w_branch_b, m_w_out, m_mlp_pre_norm,
                            m_mlp_post_norm, m_w_up, m_conv_w, m_conv_b, m_w_down, m_ple_norm, m_w_ple_gate, m_w_ple]))
    vars_ = dict(zip(names, [v_attn_pre_norm, v_attn_post_norm, v_w_in, v_b_gate, v_sinks, v_q_a_norm, v_w_uq,
                             v_kv_a_norm, v_w_ukv, v_w_branch_a, v_w_branch_b, v_w_out, v_mlp_pre_norm,
                             v_mlp_post_norm, v_w_up, v_conv_w, v_conv_b, v_w_down, v_ple_norm, v_w_ple_gate, v_w_ple]))
    w2 = {n: a.reshape(a.shape[-2:]) for n, a in wts.items()}
    m2 = {n: a.reshape(a.shape[-2:]) for n, a in moms.items()}
    v2 = {n: a.reshape(a.shape[-2:]) for n, a in vars_.items()}

    t_rows = x.shape[-2]
    tm = min(256, t_rows)
    tm_wide = min(512, t_rows)
    xc, yc, cc = lax.axis_index("x"), lax.axis_index("y"), lax.axis_index("c")
    chip = 2 * xc + yc

    x2d = x.reshape(t_rows, D_MODEL)
    p2d = p.reshape(t_rows, PLE_DIM)
    tgt = loss_target.reshape(t_rows, D_MODEL)
    pos_f = positions.reshape(t_rows, 1).astype(F32)

    def own_slot_filled(gathered, mine):
        return [lax.dynamic_update_slice(g, m[None], (chip, 0, 0, 0)) for g, m in zip(gathered, mine)]

    def shard_lists(group, packed_group, token=0.0):
        ws = {n: w2[n] + token for n in group}
        return [_halves(ws[n].astype(BF16)) for n in group if n in _NATURAL] + [_pack_shards(ws, BF16, packed_group)]

    cw_rows = 3 * 1408 // LANES
    conv_mine = jnp.pad(w2["conv_w"].reshape(cw_rows, LANES), ((0, 48 - cw_rows), (0, 0))).reshape(2, 24, LANES)
    early_mine = shard_lists(_EARLY, _EARLY_PACKED) + [conv_mine]
    early_sems = _chips_start("gather_early_start", early_mine, MY_HALF)
    early_token = early_sems[4][0:1, 0:1]
    consts = _rope_consts()
    tabs = _rope_tables(pos_f + early_token, consts, tm)
    late_mine = shard_lists(_LATE, _LATE_PACKED, early_token)
    both_done = tabs[0][0:1, 0:1] + sum(m[0, 0:1, 0:1].astype(F32) for m in late_mine)
    early_sent, early_landed = _chips_wait("gather_early_wait", *early_sems[:4], MY_HALF, after=both_done)
    early = own_slot_filled(_pass_to_sibling(early_landed), early_sent)
    late_names = [n for n in _LATE if n in _NATURAL]
    first = [late_names.index("w_out"), len(late_names)]
    late_a = [late_mine[i] for i in first]
    late_b = [m for i, m in enumerate(late_mine) if i not in first]
    late_a_sems = _chips_start("gather_late_a_start", late_a, WHOLE, after=early[0])
    late_b_sems = _chips_start("gather_late_b_start", late_b, WHOLE, after=late_a_sems[4])
    late_token = late_b_sems[4][0:1, 0:1]
    full = _unpack_gathered(early[1], _EARLY_PACKED)
    full["w_in"] = _rows_joined(early[0]).transpose(1, 0, 2).reshape(D_MODEL, 3232)
    conv_full = early[2].reshape(4, 48, LANES)[:, :cw_rows].reshape(4, 3, 1408).transpose(1, 0, 2).reshape(3, 2 * D_FF)
    convw8 = jnp.pad(conv_full, ((0, 5), (0, 0)))

    win = _pad_w_in(full["w_in"])
    wuq = _pad_slots(full["w_uq"], HEADS, NOPE_DIM + ROPE_DIM)
    ukv = full["w_ukv"].reshape(KV_LORA, HEADS, NOPE_DIM + V_DIM)
    wk = _pad_slots(ukv[:, :, :NOPE_DIM].reshape(KV_LORA, HEADS * NOPE_DIM), HEADS, NOPE_DIM)
    wv = _pad_slots(ukv[:, :, NOPE_DIM:].reshape(KV_LORA, HEADS * V_DIM), HEADS, V_DIM)
    g1, g2, g3, g4, g5 = (w2["attn_pre_norm"], w2["attn_post_norm"], w2["mlp_pre_norm"], w2["mlp_post_norm"],
                          w2["ple_norm"])
    gq, gkv, bg, convb = w2["q_a_norm"], w2["kv_a_norm"], w2["b_gate"], w2["conv_b"]
    swa_tile = min(SWA_TILE, t_rows)
    sink_rows = jnp.repeat(w2["sinks"].reshape(A_KV_HEADS, SWA_GROUP, 1), swa_tile, axis=2).reshape(
        A_KV_HEADS, 1, SWA_GROUP * swa_tile)
    swa_bias = _swa_bias(swa_tile)
    spread_q = _spread_matrix(HEADS, A_HEAD_DIM)
    spread_kv = _spread_matrix(A_KV_HEADS, A_HEAD_DIM)

    h1, qs, ks, vs, cq, cqn, ckv, ckvn, qm, km, vm, gate = _fwd_in(x2d, g1, win, bg + late_token, gq, gkv, wuq, wk, wv,
                                                                   spread_q, spread_kv, tabs, tm_wide)
    ya, lse_a = _swa_fwd(qs, ks, vs, swa_bias, sink_rows)
    yb, lse_b = _mla_fwd(qm, km, vm)
    late_sent, late_landed = _chips_wait("gather_late_a_wait", *late_a_sems[:4], WHOLE, after=yb)
    wout_g, packed_g = own_slot_filled(late_landed, late_sent)
    full = _unpack_gathered(packed_g, _LATE_PACKED)
    wba, wbb = full["w_branch_a"], full["w_branch_b"]
    wple = full["w_ple"]
    wout = _rows_joined(wout_g).reshape(-1, D_MODEL)
    pa, pb, mixed, o, x1, h2, ya_c, yb_c = _fwd_mix(x2d, ya, yb, gate, wba, wbb, wout, g2, g3, tm_wide)
    late_sent, late_landed = _chips_wait("gather_late_b_wait", *late_b_sems[:4], WHOLE, after=pa)
    natural = dict(zip([n for n in late_names if n != "w_out"], own_slot_filled(late_landed, late_sent)))
    wup = _rows_joined(natural["w_up"])
    wdown, wpg = (_rows_joined(natural[n]).reshape(-1, D_MODEL) for n in ("w_down", "w_ple_gate"))
    up, a = _fwd_up(h2, wup, convw8, convb, tm)
    ff, x2, e, n5, sg, dx3, loss_part = _fwd_out(a, wdown, x1, g4, p2d, wple, g5, wpg, tgt, tm_wide)

    dpre, de, dx2, dff, du, dg5, dg4, dconvb, dconvw8 = _bwd_out(dx3, e, sg, x2, ff, g5, g4, wpg, wdown, up, convw8,
                                                                 convb, tm)
    dup, dx1, do, dpa, dpb, dgates, dya, dyb, delta_b, dg3, dg2, dbg = _bwd_mid(
        du, convw8, wup, dx2, x1, g3, o, g2, wout, gate, pa, pb, wba, wbb, yb_c, tm)
    late_grads = {
        "w_branch_a": _mm_tn("dw_branch_a", ya_c, dpa),
        "w_branch_b": _mm_tn("dw_branch_b", yb_c, dpb),
        "w_out": _mm_tn("dw_out", mixed, do).reshape(4, D_MODEL // 4, D_MODEL),
        "w_up": _mm_tn("dw_up", h2, dup, column_shards=4),
        "w_down": _mm_tn("dw_down", a, dff).reshape(4, D_FF // 4, D_MODEL),
        "w_ple_gate": _mm_tn("dw_ple_gate", n5, dpre).reshape(4, D_MODEL // 4, D_MODEL),
        "w_ple": _mm_tn("dw_ple", p2d, de),
    }

    def grad_views(grads, group, packed_group):
        return [_halves(grads[n]) for n in group if n in _NATURAL] + [_pack_grads(grads, packed_group)]

    def pair_sums(tag, views, theirs):
        return [_add_pair("rs_%s_add_pair_%d" % (tag, i), g, r, cc) for i, (g, r) in enumerate(zip(views, theirs))]

    swap_sems = _chips_start("swap_late_start", grad_views(late_grads, _LATE, _LATE_PACKED), SIBLING_HALF)
    dqs, dks, dvs, dsink_rows = _swa_bwd(qs, ks, vs, ya, dya, lse_a, swa_bias, sink_rows + swap_sems[4][0:1, 0:1])
    dsink = dsink_rows[:, 0:SWA_GROUP, 0]
    late_views, late_theirs = _chips_wait("swap_late_wait", *swap_sems[:4], SIBLING_HALF, after=dqs)
    rs_sems = _chips_start("scatter_late_start", pair_sums("late", late_views, late_theirs), PIECE)
    dqm, dkm, dvm = _mla_bwd(qm, km, vm, dyb, lse_b, delta_b.reshape(HEADS, 1, t_rows) + rs_sems[4][0:1, 0:1])
    dz, dqb, dx, dgq, dgkv, dg1 = _bwd_in(dqs, dks, dvs, dqm, dkm, dvm, tabs, consts, cq, ckv, gq, gkv, wuq, wk, wv,
                                           dgates, win, x2d, g1, dx1, tm)

    small = {"attn_pre_norm": dg1, "attn_post_norm": dg2, "b_gate": dbg, "sinks": dsink, "q_a_norm": dgq,
             "kv_a_norm": dgkv, "mlp_pre_norm": dg3, "mlp_post_norm": dg4, "conv_b": dconvb, "ple_norm": dg5,
             "conv_w": dconvw8[0:3], "loss": loss_part}
    small_sems = _chips_start("gather_small_start", [_pack_small(small)], EVERYONE)
    small_token = small_sems[4]

    dwk = _unpad_slots(_mm_tn("dw_k", ckvn, dkm, after=small_token), HEADS, NOPE_DIM).reshape(
        KV_LORA, HEADS, NOPE_DIM)
    dwv = _unpad_slots(_mm_tn("dw_v", ckvn, dvm, after=small_token), HEADS, V_DIM).reshape(KV_LORA, HEADS, V_DIM)
    early_grads = {
        "w_in": _unpad_w_in(_mm_tn("dw_in", h1, dz, after=small_token)).reshape(D_MODEL, 4, 808).transpose(1, 0, 2),
        "w_uq": _unpad_slots(_mm_tn("dw_uq", cqn, dqb, after=small_token), HEADS, NOPE_DIM + ROPE_DIM),
        "w_ukv": jnp.concatenate([dwk, dwv], axis=2).reshape(KV_LORA, HEADS * (NOPE_DIM + V_DIM)),
    }

    def finish(tag, pairs, landed, group, packed_group):
        reduced = []
        for i, (pair, land) in enumerate(zip(pairs, landed)):
            own = lax.dynamic_index_in_dim(pair, chip, 0, keepdims=True)
            reduced.append(_add_chips("rs_%s_add_chips_%d" % (tag, i),
                                      lax.dynamic_update_slice(land, own, (chip, 0, 0))))
        others = _swap_sibling("swap_%s_reduced_halves" % tag, reduced)
        r, o = reduced[-1], others[-1]
        packed = jnp.where(cc == 0, jnp.stack([r, o]), jnp.stack([o, r]))
        shards = _unpack_shard_grads(packed, packed_group)
        updates.update(zip(shards, _adamw_many("adamw_%s_packed" % tag, [w2[n] for n in shards], list(shards.values()),
                                               [m2[n] for n in shards], [v2[n] for n in shards])))
        for n, r, o in zip([n for n in group if n in _NATURAL], reduced, others):
            updates[n] = _adamw_halves("adamw_" + n, w2[n], r, o, m2[n], v2[n], cc)

    updates = {}

    early_views = grad_views(early_grads, _EARLY, _EARLY_PACKED)
    early_theirs = _swap_sibling("swap_early_grad_halves", early_views, other_half=True)
    small_sent, small_landed = _chips_wait("gather_small_wait", *small_sems[:4], EVERYONE, after=early_theirs[0])
    small_all = lax.dynamic_update_slice(small_landed[0], small_sent[0][None], (4 * xc + 2 * yc + cc, 0, 0))
    early_sems = _chips_start("scatter_early_start", pair_sums("early", early_views, early_theirs), PIECE,
                              after=small_all)
    late_pairs, late_landed = _chips_wait("scatter_late_wait", *rs_sems[:4], PIECE, after=early_sems[4])
    finish("late", late_pairs, late_landed, _LATE, _LATE_PACKED)
    early_pairs, early_landed = _chips_wait("scatter_early_wait", *early_sems[:4], PIECE,
                                            after=updates[_LATE[-1]][1])
    finish("early", early_pairs, early_landed, _EARLY, _EARLY_PACKED)

    small_sum = _unpack_small(_add_devices(small_all))
    small_names = [n for n in names if n in small_sum]
    small_grads = [lax.dynamic_index_in_dim(small_sum[n].reshape(3, 4, 1408), chip, 1, keepdims=False)
                   if n == "conv_w" else small_sum[n].reshape(w2[n].shape) for n in small_names]
    updates.update(zip(small_names, _adamw_many("adamw_small", [w2[n] for n in small_names], small_grads,
                                                [m2[n] for n in small_names], [v2[n] for n in small_names])))
    loss = small_sum["loss"][0]

    outs = [[updates[n][i].reshape(wts[n].shape) for n in names] for i in range(4)]
    return (loss, dx.reshape(x.shape), *outs[0], *outs[1], *outs[2], *outs[3])
```

```python
import math

import numpy as np
import jax
import jax.numpy as jnp
from jax import lax
from jax.experimental import pallas as pl
from jax.experimental.pallas import tpu as pltpu

F32 = jnp.float32
BF16 = jnp.bfloat16

D_MODEL = 1024
D_FF = 2816
PLE_DIM = 256
ROPE_THETA = 10000.0
RMS_EPS = 1e-6
SWA_WINDOW = 128
HEADS = 8
A_KV_HEADS = 2
A_HEAD_DIM = 64
KV_LORA = 128
NOPE_DIM = 64
ROPE_DIM = 32
V_DIM = 64
LANES = 128
ZW = 3328
NEG = -1e30
SCALE_A = A_HEAD_DIM ** -0.5
SCALE_B = (NOPE_DIM + ROPE_DIM) ** -0.5

ADAM_LR = 0.001
ADAM_B1 = 0.9
ADAM_B2 = 0.999
ADAM_EPS = 1e-08
ADAM_WD = 0.01
ADAM_STEP = 10

VMEM_LIMIT = 60 * 1024 * 1024
MESH = pl.DeviceIdType.MESH

Z_QA, Z_KA, Z_VA, Z_CQ, Z_CKV, Z_KR, Z_GATE = 0, 512, 640, 768, 1024, 1152, 1280


def _dot(a, b):
    return jnp.dot(a, b, preferred_element_type=F32)


def _dot_nt(a, b):
    return lax.dot_general(a, b, (((1,), (1,)), ((), ())), preferred_element_type=F32)


def _dot_tn(a, b):
    return lax.dot_general(a, b, (((0,), (0,)), ((), ())), preferred_element_type=F32)


def _rms_stats(x):
    r = lax.rsqrt(jnp.mean(x * x, axis=-1, keepdims=True) + RMS_EPS)
    return x * r, r


def _rms_bwd(dy, xn, r, g):
    dxn = dy * g
    dx = r * (dxn - xn * jnp.mean(dxn * xn, axis=-1, keepdims=True))
    dg = jnp.sum(dy * xn, axis=0, keepdims=True)
    return dx, dg


def _tile_lanes(t, n):
    return t if n == 1 else jnp.concatenate([t] * n, axis=1)


def _rope(x, c, s1, s2, half):
    w = x.shape[1]
    n = w // LANES
    return (x * _tile_lanes(c, n) + pltpu.roll(x, w - half, 1) * _tile_lanes(s1, n)
            + pltpu.roll(x, half, 1) * _tile_lanes(s2, n))


def _rope_t(dy, c, s1, s2, half):
    w = dy.shape[1]
    n = w // LANES
    return (dy * _tile_lanes(c, n) + pltpu.roll(dy * _tile_lanes(s1, n), half, 1)
            + pltpu.roll(dy * _tile_lanes(s2, n), w - half, 1))


def _fold_slots(d):
    tiles = []
    for j in range(d.shape[1] // (2 * LANES)):
        even = d[:, 2 * j * LANES:(2 * j + 1) * LANES]
        odd = d[:, (2 * j + 1) * LANES:(2 * j + 2) * LANES]
        tiles.append(even + pltpu.roll(odd, A_HEAD_DIM, 1))
    return tiles[0] if len(tiles) == 1 else jnp.concatenate(tiles, axis=1)


def _spread_slots(c):
    low = lax.broadcasted_iota(jnp.int32, (c.shape[0], LANES), 1) < A_HEAD_DIM
    slots = []
    for j in range(c.shape[1] // LANES):
        tile = c[:, j * LANES:(j + 1) * LANES]
        slots += [jnp.where(low, tile, 0.0), jnp.where(low, pltpu.roll(tile, A_HEAD_DIM, 1), 0.0)]
    return jnp.concatenate(slots, axis=1)


def _sigmoid(x):
    return 1.0 / (1.0 + jnp.exp(-x))


_GELU_C = math.sqrt(2.0 / math.pi)


def _gelu_and_grad(x):
    a = _GELU_C + (_GELU_C * 0.044715) * (x * x)
    th = jnp.tanh(x * a)
    hx = 0.5 * x
    p1 = 1.0 + th
    gel = hx * p1
    dgel = 0.5 * p1 + (hx * (1.0 - th * th)) * (3.0 * a - 2.0 * _GELU_C)
    return gel, dgel


def _conv_taps(up, h6, h7):
    r1 = pltpu.roll(up, 1, 0)
    r2 = pltpu.roll(up, 2, 0)
    rows = lax.broadcasted_iota(jnp.int32, (8, up.shape[1]), 0)
    xm1 = jnp.concatenate([jnp.where(rows == 0, h7, r1[0:8]), r1[8:]], axis=0)
    xm2 = jnp.concatenate([jnp.where(rows == 0, h6, jnp.where(rows == 1, h7, r2[0:8])), r2[8:]], axis=0)
    return xm1, xm2


def _conv_taps_next(du, n0, n1):
    tm = du.shape[0]
    r1 = pltpu.roll(du, tm - 1, 0)
    r2 = pltpu.roll(du, tm - 2, 0)
    rows = lax.broadcasted_iota(jnp.int32, (8, du.shape[1]), 0)
    xp1 = jnp.concatenate([r1[:tm - 8], jnp.where(rows == 7, n0, r1[tm - 8:])], axis=0)
    xp2 = jnp.concatenate([r2[:tm - 8], jnp.where(rows == 6, n0, jnp.where(rows == 7, n1, r2[tm - 8:]))], axis=0)
    return xp1, xp2


def _row(tm, n):
    return pl.BlockSpec((tm, n), lambda i: (i, 0))


def _full(shape):
    nd = len(shape)
    return pl.BlockSpec(tuple(shape), lambda i: (0,) * nd)


def _resident(shape):
    nd = len(shape)
    return pl.BlockSpec(tuple(shape), lambda i: (0,) * nd, pipeline_mode=pl.Buffered(1))


def _heads(tm, h):
    return pl.BlockSpec((h, tm, LANES), lambda i: (0, i, 0))


def _rows_call(name, body, t_rows, tm, ins, outs, scratch=()):
    return pl.pallas_call(
        body, name=name, grid=(t_rows // tm,),
        in_specs=[s for _, s in ins],
        out_specs=[s for _, s in outs],
        out_shape=[s for s, _ in outs],
        scratch_shapes=list(scratch),
        compiler_params=pltpu.CompilerParams(dimension_semantics=("arbitrary",), vmem_limit_bytes=VMEM_LIMIT),
    )(*[a for a, _ in ins])


def _sds(shape, dtype):
    return jax.ShapeDtypeStruct(tuple(shape), dtype)


def _rope_consts():
    c = np.zeros((16, LANES), np.float32)
    lane = np.arange(LANES)
    inv_a = (ROPE_THETA ** (-(np.arange(0, A_HEAD_DIM, 2, dtype=np.float32) / A_HEAD_DIM))).astype(np.float32)
    in_a = lane < A_HEAD_DIM
    c[0, in_a] = inv_a[lane[in_a] % (A_HEAD_DIM // 2)]
    c[1, in_a] = 1.0
    c[2, lane < A_HEAD_DIM // 2] = -1.0
    c[3, (lane >= A_HEAD_DIM // 2) & in_a] = 1.0
    inv_b = (ROPE_THETA ** (-(np.arange(0, ROPE_DIM, 2, dtype=np.float32) / ROPE_DIM))).astype(np.float32)
    pe = (lane >= NOPE_DIM) & (lane < NOPE_DIM + ROPE_DIM)
    c[5, pe] = inv_b[(lane[pe] - NOPE_DIM) % (ROPE_DIM // 2)]
    c[6, pe] = 1.0
    c[7, (lane >= NOPE_DIM) & (lane < NOPE_DIM + ROPE_DIM // 2)] = -1.0
    c[8, (lane >= NOPE_DIM + ROPE_DIM // 2) & (lane < NOPE_DIM + ROPE_DIM)] = 1.0
    c[9, lane < NOPE_DIM] = 1.0
    c[10, pe] = 1.0
    return jnp.asarray(c)


def _rope_tables(pos_f, consts, tm):
    t_rows = pos_f.shape[0]

    def body(pos_ref, c_ref, ca, sa1, sa2, cb, sb1, sb2):
        ang = pos_ref[...] * (c_ref[0:1, :] + c_ref[5:6, :])
        cs, sn = jnp.cos(ang), jnp.sin(ang)
        for ref, row in ((ca, 1), (sa1, 2), (sa2, 3)):
            half = (cs if row == 1 else sn) * c_ref[row:row + 1, :]
            ref[...] = half + pltpu.roll(half, A_HEAD_DIM, 1)
        cb[...] = cs * c_ref[6:7, :] + c_ref[9:10, :]
        sb1[...] = sn * c_ref[7:8, :]
        sb2[...] = sn * c_ref[8:9, :]

    tab = (_sds((t_rows, LANES), F32), _row(tm, LANES))
    return _rows_call("rope_tables", body, t_rows, tm,
                      [(pos_f, _row(tm, 1)), (consts, _full(consts.shape))], [tab] * 6)


def _fwd_in(x, g1, win, bg, gq, gkv, wuq, wk, wv, eq, ek, tabs, tm):
    t_rows = x.shape[0]

    def body(x_ref, g1_ref, win_ref, bg_ref, gq_ref, gkv_ref, wuq_ref, wk_ref, wv_ref, eq_ref, ek_ref,
             ca, sa1, sa2, cb, sb1, sb2,
             h1_ref, qs_ref, ks_ref, vs_ref, cq_ref, cqn_ref, ckv_ref, ckvn_ref, qm_ref, km_ref, vm_ref, gate_ref):
        xn, _ = _rms_stats(x_ref[...])
        hb = (xn * g1_ref[...]).astype(BF16)
        h1_ref[...] = hb
        ta = (ca[...], sa1[...], sa2[...])
        tb = (cb[...], sb1[...], sb2[...])
        cq = _dot(hb, win_ref[:, Z_CQ:Z_CKV])
        ckv = _dot(hb, win_ref[:, Z_CKV:Z_KR])
        z_qa = _dot(hb, win_ref[:, Z_QA:Z_KA])
        z_ka = _dot(hb, win_ref[:, Z_KA:Z_VA])
        z_va = _dot(hb, win_ref[:, Z_VA:Z_CQ])
        z_kr = _dot(hb, win_ref[:, Z_KR:Z_GATE])
        cq_ref[...] = cq
        cqn, _ = _rms_stats(cq)
        cqb = (cqn * gq_ref[...]).astype(BF16)
        cqn_ref[...] = cqb
        ckv_ref[...] = ckv
        ckvn, _ = _rms_stats(ckv)
        ckvb = (ckvn * gkv_ref[...]).astype(BF16)
        ckvn_ref[...] = ckvb
        z_qm = _dot(cqb, wuq_ref[...])
        z_km = _dot(ckvb, wk_ref[...])
        z_vm = _dot(ckvb, wv_ref[...])
        z_gate = _dot(hb, win_ref[:, Z_GATE:ZW])
        qs_ref[...] = _dot((_rope(z_qa, *ta, A_HEAD_DIM // 2) * SCALE_A).astype(BF16), eq_ref[...]).astype(BF16)
        ks_ref[...] = _dot(_rope(z_ka, *ta, A_HEAD_DIM // 2).astype(BF16), ek_ref[...]).astype(BF16)
        vs_ref[...] = _dot(z_va.astype(BF16), ek_ref[...]).astype(BF16)
        qm_ref[...] = (_rope(z_qm, *tb, ROPE_DIM // 2) * SCALE_B).astype(BF16)
        km_ref[...] = (z_km + _tile_lanes(_rope(z_kr, *tb, ROPE_DIM // 2), HEADS)).astype(BF16)
        vm_ref[...] = z_vm.astype(BF16)
        gate_ref[...] = _sigmoid(z_gate + bg_ref[...]).astype(BF16)

    def o(n, dt):
        return (_sds((t_rows, n), dt), _row(tm, n))

    ins = [(x, _row(tm, D_MODEL)), (g1, _full(g1.shape)), (win, _resident(win.shape)), (bg, _full(bg.shape)),
           (gq, _full(gq.shape)), (gkv, _full(gkv.shape)), (wuq, _full(wuq.shape)), (wk, _full(wk.shape)),
           (wv, _full(wv.shape)), (eq, _full(eq.shape)), (ek, _full(ek.shape))] + [(t, _row(tm, LANES)) for t in tabs]
    outs = [o(1024, BF16), o(1024, BF16), o(256, BF16), o(256, BF16), o(256, F32), o(256, BF16), o(128, F32),
            o(128, BF16), o(1024, BF16), o(1024, BF16), o(1024, BF16), o(2048, BF16)]
    return _rows_call("fwd_in", body, t_rows, tm, ins, outs)


def _attn_tile(t_rows):
    return min(512, t_rows)


MLA_HEADS_PER_STEP = 4
MLA_FWD_HEADS_PER_STEP = 8


def _causal_pairs(nq, by_kv):
    if by_kv:
        pairs = [(i, j) for j in range(nq) for i in range(j, nq)]
    else:
        pairs = [(i, j) for i in range(nq) for j in range(i + 1)]
    return (jnp.asarray([p[0] for p in pairs], jnp.int32), jnp.asarray([p[1] for p in pairs], jnp.int32))


def _mla_fwd(q, k, v):
    t_rows = q.shape[0]
    t = _attn_tile(t_rows)
    hp = MLA_FWD_HEADS_PER_STEP
    w = hp * LANES
    ii, jj = _causal_pairs(t_rows // t, by_kv=False)

    def body(i_ref, j_ref, q_ref, k_ref, v_ref, o_ref, lse_ref, m_s, l_s, acc_s):
        i = i_ref[pl.program_id(1)]
        j = j_ref[pl.program_id(1)]

        @pl.when(j == 0)
        def _():
            m_s[...] = jnp.full(m_s.shape, NEG, F32)
            l_s[...] = jnp.zeros(l_s.shape, F32)
            acc_s[...] = jnp.zeros(acc_s.shape, F32)

        def step(diagonal):
            sls = [slice(hh * LANES, (hh + 1) * LANES) for hh in range(hp)]
            scores = [_dot_nt(k_ref[:, sl], q_ref[:, sl]) for sl in sls]
            if diagonal:
                valid = (lax.broadcasted_iota(jnp.int32, (t, t), 0) <= lax.broadcasted_iota(jnp.int32, (t, t), 1))
                scores = [jnp.where(valid, s, NEG) for s in scores]
            stats = []
            for hh, s in enumerate(scores):
                m_prev = m_s[hh]
                m_new = jnp.maximum(m_prev, jnp.max(s, axis=0, keepdims=True))
                p = jnp.exp(s - m_new)
                alpha = jnp.exp(m_prev - m_new)
                stats.append((m_new, alpha, alpha * l_s[hh] + jnp.sum(p, axis=0, keepdims=True), p.astype(BF16)))
            for hh, (m_new, alpha, l_new, p) in enumerate(stats):
                sl = sls[hh]
                acc = alpha * acc_s[hh] + _dot_tn(v_ref[:, sl], p)
                if diagonal:
                    o_ref[:, sl] = (acc / l_new).T.astype(o_ref.dtype)
                    lse_ref[hh] = m_new + jnp.log(l_new)
                else:
                    m_s[hh] = m_new
                    l_s[hh] = l_new
                    acc_s[hh] = acc

        pl.when(j < i)(lambda: step(False))
        pl.when(j == i)(lambda: step(True))

    grid_spec = pltpu.PrefetchScalarGridSpec(
        num_scalar_prefetch=2, grid=(HEADS // hp, ii.shape[0]),
        in_specs=[pl.BlockSpec((t, w), lambda hb, s, ir, jr: (ir[s], hb)),
                  pl.BlockSpec((t, w), lambda hb, s, ir, jr: (jr[s], hb)),
                  pl.BlockSpec((t, w), lambda hb, s, ir, jr: (jr[s], hb))],
        out_specs=[pl.BlockSpec((t, w), lambda hb, s, ir, jr: (ir[s], hb)),
                   pl.BlockSpec((hp, 1, t), lambda hb, s, ir, jr: (hb, 0, ir[s]))],
        scratch_shapes=[pltpu.VMEM((hp, 1, t), F32), pltpu.VMEM((hp, 1, t), F32), pltpu.VMEM((hp, LANES, t), F32)])
    return pl.pallas_call(
        body, name="mla_fwd", grid_spec=grid_spec,
        out_shape=[_sds((t_rows, HEADS * LANES), BF16), _sds((HEADS, 1, t_rows), F32)],
        compiler_params=pltpu.CompilerParams(dimension_semantics=("arbitrary",) * 2, vmem_limit_bytes=VMEM_LIMIT),
    )(ii, jj, q, k, v)


def _mla_bwd(q, k, v, do, lse, delta):
    t_rows = q.shape[0]
    t = _attn_tile(t_rows)
    hp = MLA_HEADS_PER_STEP
    w = hp * LANES
    ii, jj = _causal_pairs(t_rows // t, by_kv=True)

    def body(i_ref, j_ref, q_ref, k_ref, v_ref, do_ref, lse_ref, dl_ref, dq_ref, dk_ref, dv_ref):
        i = i_ref[pl.program_id(1)]
        j = j_ref[pl.program_id(1)]

        @pl.when(pl.program_id(1) == 0)
        def _():
            dq_ref[...] = jnp.zeros(dq_ref.shape, F32)

        def step(diagonal):
            r0 = pl.multiple_of(i * t, t)
            sls = [slice(hh * LANES, (hh + 1) * LANES) for hh in range(hp)]
            scores = [_dot_nt(k_ref[:, sl], q_ref[:, sl]) for sl in sls]
            if diagonal:
                valid = (lax.broadcasted_iota(jnp.int32, (t, t), 0) <= lax.broadcasted_iota(jnp.int32, (t, t), 1))
                scores = [jnp.where(valid, s, NEG) for s in scores]
            dps = [_dot_nt(v_ref[:, sl], do_ref[:, sl]) for sl in sls]
            ps = [jnp.exp(s - lse_ref[hh]) for hh, s in enumerate(scores)]
            dss = [(p * (dp - dl_ref[hh])).astype(BF16) for hh, (p, dp) in enumerate(zip(ps, dps))]
            for hh, sl in enumerate(sls):
                dv = _dot(ps[hh].astype(BF16), do_ref[:, sl])
                dk = _dot(dss[hh], q_ref[:, sl])
                if diagonal:
                    dv_ref[:, sl] = dv
                    dk_ref[:, sl] = dk
                else:
                    dv_ref[:, sl] += dv
                    dk_ref[:, sl] += dk
                dq_ref[hh, pl.ds(r0, t), :] += _dot_tn(dss[hh], k_ref[:, sl])

        pl.when(i > j)(lambda: step(False))
        pl.when(i == j)(lambda: step(True))

    def qmap(hb, s, ir, jr):
        return (ir[s], hb)

    def kvmap(hb, s, ir, jr):
        return (jr[s], hb)

    def rowmap(hb, s, ir, jr):
        return (hb, 0, ir[s])

    grid_spec = pltpu.PrefetchScalarGridSpec(
        num_scalar_prefetch=2, grid=(HEADS // hp, ii.shape[0]),
        in_specs=[pl.BlockSpec((t, w), qmap), pl.BlockSpec((t, w), kvmap), pl.BlockSpec((t, w), kvmap),
                  pl.BlockSpec((t, w), qmap), pl.BlockSpec((hp, 1, t), rowmap), pl.BlockSpec((hp, 1, t), rowmap)],
        out_specs=[pl.BlockSpec((hp, t_rows, LANES), lambda hb, s, ir, jr: (hb, 0, 0)),
                   pl.BlockSpec((t, w), kvmap), pl.BlockSpec((t, w), kvmap)])
    return pl.pallas_call(
        body, name="mla_bwd", grid_spec=grid_spec,
        out_shape=[_sds((HEADS, t_rows, LANES), F32), _sds((t_rows, HEADS * LANES), F32),
                   _sds((t_rows, HEADS * LANES), F32)],
        compiler_params=pltpu.CompilerParams(dimension_semantics=("arbitrary",) * 2, vmem_limit_bytes=VMEM_LIMIT),
    )(ii, jj, q, k, v, do, lse, delta)


SWA_TILE = 2 * SWA_WINDOW
SWA_GROUP = HEADS // A_KV_HEADS


def _swa_bias(tq):
    koff = lax.broadcasted_iota(jnp.int32, (tq + SWA_WINDOW, SWA_GROUP * tq), 0) - SWA_WINDOW
    qoff = (lax.broadcasted_iota(jnp.int32, (tq + SWA_WINDOW, SWA_GROUP * tq), 1) % tq)
    band = (koff <= qoff) & (qoff - koff < SWA_WINDOW)
    return jnp.stack([jnp.where(band & (koff >= 0), 0.0, NEG), jnp.where(band, 0.0, NEG)]).astype(F32)


def _swa_specs(tq, nq):
    wb = tq // SWA_WINDOW
    kvw = A_KV_HEADS * LANES

    def qi(i):
        return jnp.minimum(i, nq - 1)

    q = pl.BlockSpec((tq, HEADS * LANES), lambda i: (qi(i), 0))
    cur = pl.BlockSpec((tq, kvw), lambda i: (qi(i), 0))
    prev = pl.BlockSpec((SWA_WINDOW, kvw), lambda i: (jnp.maximum(qi(i) * wb - 1, 0), 0))
    bias = pl.BlockSpec((1, tq + SWA_WINDOW, SWA_GROUP * tq), lambda i: (jnp.minimum(i, 1), 0, 0))
    rows = pl.BlockSpec((A_KV_HEADS, 1, 1, SWA_GROUP * tq), lambda i: (0, qi(i), 0, 0))
    sink = pl.BlockSpec((A_KV_HEADS, 1, SWA_GROUP * tq), lambda i: (0, 0, 0))
    return q, cur, prev, bias, rows, sink


def _stack_heads(ref, kvh):
    base = kvh * SWA_GROUP
    return jnp.concatenate([ref[:, (base + g) * LANES:(base + g + 1) * LANES] for g in range(SWA_GROUP)], axis=0)


def _unstack_heads(ref, kvh, val, tq):
    base = kvh * SWA_GROUP
    for g in range(SWA_GROUP):
        ref[:, (base + g) * LANES:(base + g + 1) * LANES] = val[g * tq:(g + 1) * tq].astype(ref.dtype)


def _kv_window(prev_ref, cur_ref, kvh):
    sl = slice(kvh * LANES, (kvh + 1) * LANES)
    return jnp.concatenate([prev_ref[:, sl], cur_ref[:, sl]], axis=0)


def _swa_fwd(q, k, v, bias, sink_rows):
    t_rows = q.shape[0]
    tq = min(SWA_TILE, t_rows)
    nq = t_rows // tq
    qs_, cur, prev, bs, rows, sk = _swa_specs(tq, nq)
    kvhs = range(A_KV_HEADS)

    def body(q_ref, kc_ref, kp_ref, vc_ref, vp_ref, b_ref, sink_ref, o_ref, lse_ref):
        scores = [_dot_nt(_kv_window(kp_ref, kc_ref, h), _stack_heads(q_ref, h)) + b_ref[0] for h in kvhs]
        stats = []
        for h, s in zip(kvhs, scores):
            sink = sink_ref[h]
            m = jnp.maximum(jnp.max(s, axis=0, keepdims=True), sink)
            p = jnp.exp(s - m)
            l = jnp.sum(p, axis=0, keepdims=True) + jnp.exp(sink - m)
            lse_ref[h, 0] = m + jnp.log(l)
            stats.append((p.astype(BF16), l))
        for h, (p, l) in zip(kvhs, stats):
            _unstack_heads(o_ref, h, (_dot_tn(_kv_window(vp_ref, vc_ref, h), p) / l).T, tq)

    return pl.pallas_call(
        body, name="swa_fwd", grid=(nq,),
        in_specs=[qs_, cur, prev, cur, prev, bs, sk],
        out_specs=[qs_, rows],
        out_shape=[_sds((t_rows, HEADS * LANES), BF16), _sds((A_KV_HEADS, nq, 1, SWA_GROUP * tq), F32)],
        compiler_params=pltpu.CompilerParams(dimension_semantics=("arbitrary",), vmem_limit_bytes=VMEM_LIMIT),
    )(q, k, k, v, v, bias, sink_rows)


def _swa_bwd(q, k, v, do, lse, delta, bias, sink_rows):
    t_rows = q.shape[0]
    tq = min(SWA_TILE, t_rows)
    nq = t_rows // tq
    qs_, cur, prev, bs, rows, sk = _swa_specs(tq, nq)
    hw = SWA_WINDOW
    kvhs = range(A_KV_HEADS)
    kvw = A_KV_HEADS * LANES

    def body(q_ref, kc_ref, kp_ref, vc_ref, vp_ref, do_ref, lse_ref, dl_ref, b_ref, sink_ref,
             dq_ref, dk_ref, dv_ref, dsink_ref, ck, cv, dsa):
        i = pl.program_id(0)

        @pl.when(i == 0)
        def _():
            dsa[...] = jnp.zeros(dsa.shape, F32)

        @pl.when(i < nq)
        def _():
            qs = [_stack_heads(q_ref, h) for h in kvhs]
            dos = [_stack_heads(do_ref, h) for h in kvhs]
            kks = [_kv_window(kp_ref, kc_ref, h) for h in kvhs]
            scores = [_dot_nt(kks[h], qs[h]) for h in kvhs]
            dps = [_dot_nt(_kv_window(vp_ref, vc_ref, h), dos[h]) for h in kvhs]
            ps, dss = [], []
            for h in kvhs:
                lse = lse_ref[h, 0]
                p = jnp.exp(scores[h] + b_ref[0] - lse)
                delta = dl_ref[h, 0]
                dsa[h] += -jnp.exp(sink_ref[h] - lse) * delta
                ps.append(p.astype(BF16))
                dss.append((p * (dps[h] - delta)).astype(BF16))
            for h in kvhs:
                sl = slice(h * LANES, (h + 1) * LANES)
                dv = _dot(ps[h], dos[h])
                dk = _dot(dss[h], qs[h])
                _unstack_heads(dq_ref, h, _dot_tn(dss[h], kks[h]), tq)

                @pl.when(i > 0)
                def _():
                    dk_ref[0:tq - hw, sl] = ck[0:tq - hw, sl]
                    dk_ref[tq - hw:tq, sl] = ck[tq - hw:tq, sl] + dk[0:hw]
                    dv_ref[0:tq - hw, sl] = cv[0:tq - hw, sl]
                    dv_ref[tq - hw:tq, sl] = cv[tq - hw:tq, sl] + dv[0:hw]

                ck[:, sl] = dk[hw:hw + tq]
                cv[:, sl] = dv[hw:hw + tq]

        @pl.when(i == nq)
        def _():
            dk_ref[...] = ck[...]
            dv_ref[...] = cv[...]
            dsink_ref[...] = jnp.zeros(dsink_ref.shape, F32)
            for h in kvhs:
                for g in range(SWA_GROUP):
                    tot = jnp.sum(dsa[h, :, g * tq:(g + 1) * tq], axis=1, keepdims=True)
                    dsink_ref[h, g:g + 1, :] = jnp.zeros((1, LANES), F32) + tot

    kv_out = pl.BlockSpec((tq, kvw), lambda i: (jnp.maximum(i - 1, 0), 0))
    return pl.pallas_call(
        body, name="swa_bwd", grid=(nq + 1,),
        in_specs=[qs_, cur, prev, cur, prev, qs_, rows, rows, bs, sk],
        out_specs=[qs_, kv_out, kv_out, pl.BlockSpec((A_KV_HEADS, 8, LANES), lambda i: (0, 0, 0))],
        out_shape=[_sds((t_rows, HEADS * LANES), F32), _sds((t_rows, kvw), F32), _sds((t_rows, kvw), F32),
                   _sds((A_KV_HEADS, 8, LANES), F32)],
        scratch_shapes=[pltpu.VMEM((tq, kvw), F32), pltpu.VMEM((tq, kvw), F32),
                        pltpu.VMEM((A_KV_HEADS, 1, SWA_GROUP * tq), F32)],
        compiler_params=pltpu.CompilerParams(dimension_semantics=("arbitrary",), vmem_limit_bytes=VMEM_LIMIT),
    )(q, k, k, v, v, do, lse, delta, bias, sink_rows)


def _fwd_mix(x, ya, yb, gate, wba, wbb, wout, g2, g3, tm):
    t_rows = x.shape[0]

    def body(x_ref, ya_ref, yb_ref, gate_ref, wba_ref, wbb_ref, wout_ref, g2_ref, g3_ref,
             pa_ref, pb_ref, mixed_ref, o_ref, x1_ref, h2_ref, yac_ref, ybc_ref):
        yac = _fold_slots(ya_ref[...].astype(F32)).astype(BF16)
        ybc = _fold_slots(yb_ref[...].astype(F32)).astype(BF16)
        yac_ref[...] = yac
        ybc_ref[...] = ybc
        pa = _dot(yac, wba_ref[...])
        pb = _dot(ybc, wbb_ref[...])
        pa_ref[...] = pa.astype(BF16)
        pb_ref[...] = pb.astype(BF16)
        mixed = (gate_ref[:, 0:D_MODEL].astype(F32) * pa
                 + gate_ref[:, D_MODEL:2 * D_MODEL].astype(F32) * pb).astype(BF16)
        mixed_ref[...] = mixed
        o = _dot(mixed, wout_ref[...])
        o_ref[...] = o
        on, _ = _rms_stats(o)
        x1 = x_ref[...] + on * g2_ref[...]
        x1_ref[...] = x1
        x1n, _ = _rms_stats(x1)
        h2_ref[...] = (x1n * g3_ref[...]).astype(BF16)

    def o_(dt):
        return (_sds((t_rows, D_MODEL), dt), _row(tm, D_MODEL))

    ins = [(x, _row(tm, D_MODEL)), (ya, _row(tm, 1024)), (yb, _row(tm, 1024)), (gate, _row(tm, 2048)),
           (wba, _resident(wba.shape)), (wbb, _resident(wbb.shape)), (wout, _resident(wout.shape)),
           (g2, _full(g2.shape)), (g3, _full(g3.shape))]
    half = (_sds((t_rows, D_MODEL // 2), BF16), _row(tm, D_MODEL // 2))
    return _rows_call("fwd_mix", body, t_rows, tm, ins,
                      [o_(BF16), o_(BF16), o_(BF16), o_(F32), o_(F32), o_(BF16), half, half])


CONV_CHUNK = 1408


def _fwd_up(h2, wup, convw8, convb, tm):
    t_rows = h2.shape[0]
    cdim = 2 * D_FF

    def body(h2_ref, wup_ref, cw_ref, cb_ref, up_ref, a_ref, carry):
        i = pl.program_id(0)

        @pl.when(i == 0)
        def _():
            carry[...] = jnp.zeros(carry.shape, F32)

        hb = h2_ref[...]
        ups = [_dot(hb, wup_ref[s]) for s in range(cdim // CONV_CHUNK)]

        def conv(c0):
            sl = slice(c0, c0 + CONV_CHUNK)
            up = ups[c0 // CONV_CHUNK]
            up_ref[:, sl] = up
            xm1, xm2 = _conv_taps(up, carry[6:7, sl], carry[7:8, sl])
            u = cw_ref[0:1, sl] * xm2 + cw_ref[1:2, sl] * xm1 + cw_ref[2:3, sl] * up + cb_ref[:, sl]
            carry[:, sl] = up[tm - 8:tm, :]
            return u

        for c0 in range(0, D_FF, CONV_CHUNK):
            ug = conv(c0)
            uv = conv(D_FF + c0)
            gel, _ = _gelu_and_grad(ug)
            a_ref[:, c0:c0 + CONV_CHUNK] = (gel * uv).astype(BF16)

    ins = [(h2, _row(tm, D_MODEL)), (wup, _resident(wup.shape)), (convw8, _full(convw8.shape)),
           (convb, _full(convb.shape))]
    outs = [(_sds((t_rows, cdim), F32), _row(tm, cdim)), (_sds((t_rows, D_FF), BF16), _row(tm, D_FF))]
    return _rows_call("fwd_up", body, t_rows, tm, ins, outs, scratch=[pltpu.VMEM((8, cdim), F32)])


def _fwd_out(a, wdown, x1, g4, p, wple, g5, wpg, tgt, tm):
    t_rows = a.shape[0]

    def body(a_ref, wdown_ref, x1_ref, g4_ref, p_ref, wple_ref, g5_ref, wpg_ref, tgt_ref,
             ff_ref, x2_ref, e_ref, n5_ref, sg_ref, dx3_ref, loss_ref):
        i = pl.program_id(0)
        ff = _dot(a_ref[...], wdown_ref[...])
        e = _dot(p_ref[...].astype(BF16), wple_ref[...])
        ff_ref[...] = ff
        ffn, _ = _rms_stats(ff)
        x2 = x1_ref[...] + ffn * g4_ref[...]
        x2_ref[...] = x2
        e_ref[...] = e.astype(BF16)
        x2n, _ = _rms_stats(x2)
        n5 = (x2n * g5_ref[...]).astype(BF16)
        n5_ref[...] = n5
        sg = _sigmoid(_dot(n5, wpg_ref[...]))
        sg_ref[...] = sg.astype(BF16)
        d = x2 + sg * e - tgt_ref[...]
        dx3_ref[...] = d * (1.0 / D_MODEL)

        @pl.when(i == 0)
        def _():
            loss_ref[...] = jnp.zeros((1, 1), F32)

        loss_ref[...] += 0.5 * jnp.sum(jnp.sum(d * d, axis=1, keepdims=True), axis=0, keepdims=True) * (1.0 / D_MODEL)

    def o_(dt):
        return (_sds((t_rows, D_MODEL), dt), _row(tm, D_MODEL))

    ins = [(a, _row(tm, D_FF)), (wdown, _resident(wdown.shape)), (x1, _row(tm, D_MODEL)), (g4, _full(g4.shape)),
           (p, _row(tm, PLE_DIM)), (wple, _full(wple.shape)), (g5, _full(g5.shape)), (wpg, _resident(wpg.shape)),
           (tgt, _row(tm, D_MODEL))]
    outs = [o_(F32), o_(F32), o_(BF16), o_(BF16), o_(BF16), o_(F32), (_sds((1, 1), F32), _full((1, 1)))]
    return _rows_call("fwd_out", body, t_rows, tm, ins, outs)


def _bwd_out(dx3, e, sg, x2, ff, g5, g4, wpg, wdown, up, convw8, convb, tm):
    t_rows = dx3.shape[0]
    cdim = 2 * D_FF
    hb = tm // 8

    def body(dx3_ref, e_ref, sg_ref, x2_ref, ff_ref, g5_ref, g4_ref, wpg_ref, wdown_ref, up_ref, halo_ref, cw_ref,
             cb_ref, dpre_ref, de_ref, dx2_ref, dff_ref, du_ref, dg5_ref, dg4_ref, dcb_ref, dcw_ref):
        i = pl.program_id(0)

        @pl.when(i == 0)
        def _():
            dg5_ref[...] = jnp.zeros(dg5_ref.shape, F32)
            dg4_ref[...] = jnp.zeros(dg4_ref.shape, F32)
            dcb_ref[...] = jnp.zeros(dcb_ref.shape, F32)
            dcw_ref[...] = jnp.zeros(dcw_ref.shape, F32)

        dx3 = dx3_ref[...]
        sg = sg_ref[...].astype(F32)
        dpre = (dx3 * e_ref[...].astype(F32) * sg * (1.0 - sg)).astype(BF16)
        dpre_ref[...] = dpre
        de_ref[...] = (dx3 * sg).astype(BF16)
        dn5 = _dot_nt(dpre, wpg_ref[...])
        x2n, r5 = _rms_stats(x2_ref[...])
        d2, dg5 = _rms_bwd(dn5, x2n, r5, g5_ref[...])
        dx2 = dx3 + d2
        dx2_ref[...] = dx2
        dg5_ref[...] += dg5
        ffn, r4 = _rms_stats(ff_ref[...])
        dff, dg4 = _rms_bwd(dx2, ffn, r4, g4_ref[...])
        dg4_ref[...] += dg4
        dffb = dff.astype(BF16)
        dff_ref[...] = dffb
        keep = jnp.where(i > 0, 1.0, 0.0)

        def conv(c0):
            sl = slice(c0, c0 + CONV_CHUNK)
            up = up_ref[:, sl]
            xm1, xm2 = _conv_taps(up, halo_ref[6:7, sl] * keep, halo_ref[7:8, sl] * keep)
            u = cw_ref[0:1, sl] * xm2 + cw_ref[1:2, sl] * xm1 + cw_ref[2:3, sl] * up + cb_ref[:, sl]
            return u, up, xm1, xm2

        def grads(c0, du, up, xm1, xm2):
            sl = slice(c0, c0 + CONV_CHUNK)
            du_ref[:, sl] = du.astype(BF16)
            dcb_ref[:, sl] += jnp.sum(du, axis=0, keepdims=True)
            dcw_ref[0:1, sl] += jnp.sum(du * xm2, axis=0, keepdims=True)
            dcw_ref[1:2, sl] += jnp.sum(du * xm1, axis=0, keepdims=True)
            dcw_ref[2:3, sl] += jnp.sum(du * up, axis=0, keepdims=True)

        for c0 in range(0, D_FF, CONV_CHUNK):
            da = _dot_nt(dffb, wdown_ref[c0:c0 + CONV_CHUNK, :])
            ug, *rg = conv(c0)
            uv, *rv = conv(D_FF + c0)
            gel, dgel = _gelu_and_grad(ug)
            grads(c0, da * uv * dgel, *rg)
            grads(D_FF + c0, da * gel, *rv)

    def o_(n, dt):
        return (_sds((t_rows, n), dt), _row(tm, n))

    def acc(r, n):
        return (_sds((r, n), F32), _full((r, n)))

    halo = pl.BlockSpec((8, cdim), lambda i: (jnp.maximum(i * hb - 1, 0), 0))
    ins = [(dx3, _row(tm, D_MODEL)), (e, _row(tm, D_MODEL)), (sg, _row(tm, D_MODEL)), (x2, _row(tm, D_MODEL)),
           (ff, _row(tm, D_MODEL)), (g5, _full(g5.shape)), (g4, _full(g4.shape)), (wpg, _resident(wpg.shape)),
           (wdown, _resident(wdown.shape)), (up, _row(tm, cdim)), (up, halo), (convw8, _full(convw8.shape)),
           (convb, _full(convb.shape))]
    outs = [o_(D_MODEL, BF16), o_(D_MODEL, BF16), o_(D_MODEL, F32), o_(D_MODEL, BF16), o_(cdim, BF16),
            acc(1, D_MODEL), acc(1, D_MODEL), acc(1, cdim), acc(8, cdim)]
    return _rows_call("bwd_out", body, t_rows, tm, ins, outs)


def _bwd_mid(du, convw8, wup, dx2, x1, g3, o, g2, wout, gate, pa, pb, wba, wbb, ya, yb, tm):
    t_rows = du.shape[0]
    cdim = 2 * D_FF
    halo_rows = 16
    hb = tm // halo_rows
    last_blk = t_rows // halo_rows - 1
    n_tiles = t_rows // tm

    def body(du_ref, halo_ref, cw_ref, wup_ref, dx2_ref, x1_ref, g3_ref, o_ref, g2_ref, wout_ref, gate_ref, pa_ref,
             pb_ref, wba_ref, wbb_ref, ya_ref, yb_ref,
             dup_ref, dx1_ref, do_ref, dpa_ref, dpb_ref, dgt_ref, dya_ref, dyb_ref, dla_ref, dl_ref, dg3_ref, dg2_ref,
             dbg_ref):
        i = pl.program_id(0)

        @pl.when(i == 0)
        def _():
            dg3_ref[...] = jnp.zeros(dg3_ref.shape, F32)
            dg2_ref[...] = jnp.zeros(dg2_ref.shape, F32)
            dbg_ref[...] = jnp.zeros(dbg_ref.shape, F32)

        keep = jnp.where(i < n_tiles - 1, 1.0, 0.0)
        dh2 = jnp.zeros((tm, D_MODEL), F32)
        dups = []
        for c0 in range(0, cdim, CONV_CHUNK):
            sl = slice(c0, c0 + CONV_CHUNK)
            du = du_ref[:, sl].astype(F32)
            nxt = halo_ref[:, sl].astype(F32)
            xp1, xp2 = _conv_taps_next(du, nxt[0:1] * keep, nxt[1:2] * keep)
            dups.append((cw_ref[2:3, sl] * du + cw_ref[1:2, sl] * xp1 + cw_ref[0:1, sl] * xp2).astype(BF16))
            dup_ref[:, sl] = dups[-1]
            if len(dups) > 1:
                dh2 = dh2 + _dot_nt(dups[-2], wup_ref[len(dups) - 2])
        dh2 = dh2 + _dot_nt(dups[-1], wup_ref[len(dups) - 1])
        x1n, r3 = _rms_stats(x1_ref[...])
        d1, dg3 = _rms_bwd(dh2, x1n, r3, g3_ref[...])
        dx1 = dx2_ref[...] + d1
        dx1_ref[...] = dx1
        dg3_ref[...] += dg3
        on, r2 = _rms_stats(o_ref[...])
        do, dg2 = _rms_bwd(dx1, on, r2, g2_ref[...])
        dg2_ref[...] += dg2
        dob = do.astype(BF16)
        do_ref[...] = dob
        dmixed = _dot_nt(dob, wout_ref[...])
        ga = gate_ref[:, 0:D_MODEL].astype(F32)
        gb = gate_ref[:, D_MODEL:2 * D_MODEL].astype(F32)
        dpa = (dmixed * ga).astype(BF16)
        dpb = (dmixed * gb).astype(BF16)
        dpa_ref[...] = dpa
        dpb_ref[...] = dpb
        dga = dmixed * pa_ref[...].astype(F32) * ga * (1.0 - ga)
        dgb = dmixed * pb_ref[...].astype(F32) * gb * (1.0 - gb)
        dgt_ref[:, 0:D_MODEL] = dga.astype(BF16)
        dgt_ref[:, D_MODEL:2 * D_MODEL] = dgb.astype(BF16)
        dbg_ref[:, 0:D_MODEL] += jnp.sum(dga, axis=0, keepdims=True)
        dbg_ref[:, D_MODEL:2 * D_MODEL] += jnp.sum(dgb, axis=0, keepdims=True)
        dya = _dot_nt(dpa, wba_ref[...])
        dya_ref[...] = _spread_slots(dya).astype(BF16)
        dyb = _dot_nt(dpb, wbb_ref[...]).astype(BF16)
        dyb_ref[...] = _spread_slots(dyb.astype(F32)).astype(BF16)
        prod = yb_ref[...].astype(F32) * dyb.astype(F32)
        width = HEADS * V_DIM
        lane_head = lax.broadcasted_iota(jnp.int32, (HEADS, width), 1) // V_DIM
        sel = (lane_head == lax.broadcasted_iota(jnp.int32, (HEADS, width), 0)).astype(BF16)
        def head_sums(prod):
            hi = prod.astype(BF16)
            lo = (prod - hi.astype(F32)).astype(BF16)
            return _dot_nt(sel, hi) + _dot_nt(sel, lo)

        dl_ref[...] = head_sums(prod)
        dla_ref[...] = head_sums(ya_ref[...].astype(F32) * dya)

    def o_(n, dt):
        return (_sds((t_rows, n), dt), _row(tm, n))

    def acc(r, n):
        return (_sds((r, n), F32), _full((r, n)))

    halo = pl.BlockSpec((halo_rows, cdim), lambda i: (jnp.minimum((i + 1) * hb, last_blk), 0))
    ins = [(du, _row(tm, cdim)), (du, halo), (convw8, _full(convw8.shape)), (wup, _resident(wup.shape)),
           (dx2, _row(tm, D_MODEL)), (x1, _row(tm, D_MODEL)), (g3, _full(g3.shape)), (o, _row(tm, D_MODEL)),
           (g2, _full(g2.shape)), (wout, _resident(wout.shape)), (gate, _row(tm, 2048)), (pa, _row(tm, D_MODEL)),
           (pb, _row(tm, D_MODEL)), (wba, _resident(wba.shape)), (wbb, _resident(wbb.shape)),
           (ya, _row(tm, D_MODEL // 2)), (yb, _row(tm, D_MODEL // 2))]
    outs = [o_(cdim, BF16), o_(D_MODEL, F32), o_(D_MODEL, BF16), o_(D_MODEL, BF16), o_(D_MODEL, BF16),
            o_(2048, BF16), o_(1024, BF16), o_(1024, BF16),
            (_sds((HEADS, t_rows), F32), pl.BlockSpec((HEADS, tm), lambda i: (0, i))),
            (_sds((HEADS, t_rows), F32), pl.BlockSpec((HEADS, tm), lambda i: (0, i))),
            acc(1, D_MODEL), acc(1, D_MODEL), acc(1, 2048)]
    return _rows_call("bwd_mid", body, t_rows, tm, ins, outs)


def _bwd_in(dqs, dks, dvs, dqm, dkm, dvm, tabs, consts, cq, ckv, gq, gkv, wuq, wk, wv, dgates, win, x, g1, dx1, tm):
    t_rows = x.shape[0]

    def body(dqs_ref, dks_ref, dvs_ref, dqm_ref, dkm_ref, dvm_ref, ca, sa1, sa2, cb, sb1, sb2, c_ref, cq_ref,
             ckv_ref, gq_ref, gkv_ref, wuq_ref, wk_ref, wv_ref, dgt_ref, win_ref, x_ref, g1_ref, dx1_ref,
             dz_ref, dqb_ref, dx_ref, dgq_ref, dgkv_ref, dg1_ref):
        i = pl.program_id(0)

        @pl.when(i == 0)
        def _():
            dgq_ref[...] = jnp.zeros(dgq_ref.shape, F32)
            dgkv_ref[...] = jnp.zeros(dgkv_ref.shape, F32)
            dg1_ref[...] = jnp.zeros(dg1_ref.shape, F32)

        ta = (ca[...], sa1[...], sa2[...])
        tb = (cb[...], sb1[...], sb2[...])

        def piece(lo, hi, val):
            dz_ref[:, lo:hi] = val
            return _dot_nt(val, win_ref[:, lo:hi])

        dh1 = piece(Z_GATE, ZW, dgt_ref[...])
        dkm = dkm_ref[...]
        dckvn = _dot_nt(dkm.astype(BF16), wk_ref[...]) + _dot_nt(dvm_ref[...].astype(BF16), wv_ref[...])
        dh1 = dh1 + piece(Z_VA, Z_CQ, _fold_slots(dvs_ref[...]).astype(BF16))
        dqm = jnp.concatenate([dqm_ref[h] for h in range(HEADS)], axis=1)
        dqb = _rope_t(dqm * SCALE_B, *tb, ROPE_DIM // 2).astype(BF16)
        dqb_ref[...] = dqb
        dcqn = _dot_nt(dqb, wuq_ref[...])
        dqa = _rope_t(_fold_slots(dqs_ref[...]) * SCALE_A, *ta, A_HEAD_DIM // 2)
        dh1 = dh1 + piece(Z_QA, Z_KA, dqa.astype(BF16))
        dh1 = dh1 + piece(Z_KA, Z_VA, _rope_t(_fold_slots(dks_ref[...]), *ta, A_HEAD_DIM // 2).astype(BF16))
        ckvn, rkv = _rms_stats(ckv_ref[...])
        dckv, dgkv = _rms_bwd(dckvn, ckvn, rkv, gkv_ref[...])
        dgkv_ref[...] += dgkv
        dh1 = dh1 + piece(Z_CKV, Z_KR, dckv.astype(BF16))
        dslot = dkm[:, 0:LANES]
        for h in range(1, HEADS):
            dslot = dslot + dkm[:, h * LANES:(h + 1) * LANES]
        dh1 = dh1 + piece(Z_KR, Z_GATE, _rope_t(dslot * c_ref[10:11, :], *tb, ROPE_DIM // 2).astype(BF16))
        cqn, rq = _rms_stats(cq_ref[...])
        dcq, dgq = _rms_bwd(dcqn, cqn, rq, gq_ref[...])
        dgq_ref[...] += dgq
        dh1 = dh1 + piece(Z_CQ, Z_CKV, dcq.astype(BF16))
        xn, r1 = _rms_stats(x_ref[...])
        d0, dg1 = _rms_bwd(dh1, xn, r1, g1_ref[...])
        dg1_ref[...] += dg1
        dx_ref[...] = dx1_ref[...] + d0

    def acc(n):
        return (_sds((1, n), F32), _full((1, n)))

    ins = [(dqs, _row(tm, 1024)), (dks, _row(tm, 256)), (dvs, _row(tm, 256)), (dqm, _heads(tm, HEADS)),
           (dkm, _row(tm, 1024)), (dvm, _row(tm, 1024))] + [(t, _row(tm, LANES)) for t in tabs] + [
           (consts, _full(consts.shape)), (cq, _row(tm, 256)), (ckv, _row(tm, 128)), (gq, _full(gq.shape)),
           (gkv, _full(gkv.shape)), (wuq, _full(wuq.shape)), (wk, _full(wk.shape)), (wv, _full(wv.shape)),
           (dgates, _row(tm, 2048)), (win, _resident(win.shape)), (x, _row(tm, D_MODEL)), (g1, _full(g1.shape)),
           (dx1, _row(tm, D_MODEL))]
    outs = [(_sds((t_rows, ZW), BF16), _row(tm, ZW)), (_sds((t_rows, 1024), BF16), _row(tm, 1024)),
            (_sds((t_rows, D_MODEL), F32), _row(tm, D_MODEL)), acc(256), acc(128), acc(D_MODEL)]
    return _rows_call("bwd_in", body, t_rows, tm, ins, outs)


def _pick_cols(n):
    best = LANES
    for d in range(LANES, min(n, 1664) + 1, LANES):
        if n % d == 0:
            best = d
    return best


def _mm_tn(name, a, b, column_shards=1, after=None):
    t_rows, m = a.shape
    n = b.shape[1]
    bk = min(2048, t_rows)
    bm, bn = _pick_cols(m), _pick_cols(n // column_shards)
    per_shard = n // column_shards // bn
    extra = () if after is None else (after,)

    def body(a_ref, b_ref, *rest):
        o_ref = rest[-1]

        @pl.when(pl.program_id(2) == 0)
        def _():
            o_ref[...] = jnp.zeros((bm, bn), F32)

        o_ref[...] += _dot_tn(a_ref[...].astype(BF16), b_ref[...].astype(BF16))

    return pl.pallas_call(
        body, name=name, grid=(m // bm, n // bn, t_rows // bk),
        in_specs=[pl.BlockSpec((bk, bm), lambda i, j, k: (k, i)), pl.BlockSpec((bk, bn), lambda i, j, k: (k, j))]
        + [pl.BlockSpec((8, LANES), lambda i, j, k: (0, 0))] * len(extra),
        out_specs=(pl.BlockSpec((bm, bn), lambda i, j, k: (i, j)) if column_shards == 1 else
                   pl.BlockSpec((None, bm, bn), lambda i, j, k: (j // per_shard, i, j % per_shard))),
        out_shape=_sds((m, n) if column_shards == 1 else (column_shards, m, n // column_shards), F32),
        compiler_params=pltpu.CompilerParams(dimension_semantics=("arbitrary",) * 3, vmem_limit_bytes=VMEM_LIMIT),
    )(a, b, *extra)


PACK_ROWS = 512


ADD_TILE_ELEMS = 1 << 17


def _add_rows(rows, cols):
    best = 16
    for d in range(16, rows + 1, 16):
        if rows % d == 0 and d * cols <= ADD_TILE_ELEMS:
            best = d
    assert rows % best == 0
    return best


def _add_pair(name, g, recv, half):
    _, _, rows, cols = g.shape
    t = _add_rows(rows, cols)

    def body(h_ref, g_ref, r_ref, o_ref):
        o_ref[...] = (g_ref[:, 0] + r_ref[...]).astype(BF16)

    spec = pl.BlockSpec((4, t, cols), lambda i, h: (0, i, 0))
    grid_spec = pltpu.PrefetchScalarGridSpec(
        num_scalar_prefetch=1, grid=(rows // t,),
        in_specs=[pl.BlockSpec((4, 1, t, cols), lambda i, h: (0, h[0], i, 0)), spec], out_specs=spec)
    return pl.pallas_call(body, name=name, grid_spec=grid_spec,
                          out_shape=_sds(recv.shape, BF16))(jnp.reshape(half, (1,)).astype(jnp.int32), g, recv)


def _add_chips(name, parts):
    _, rows, cols = parts.shape
    t = _add_rows(rows, cols)

    def body(p_ref, o_ref):
        acc = p_ref[0].astype(F32)
        for j in range(1, 4):
            acc = acc + p_ref[j].astype(F32)
        o_ref[...] = acc

    return pl.pallas_call(body, name=name, grid=(rows // t,),
                          in_specs=[pl.BlockSpec((4, t, cols), lambda i: (0, i, 0))],
                          out_specs=pl.BlockSpec((t, cols), lambda i: (i, 0)),
                          out_shape=_sds((rows, cols), F32))(parts)


def _add_devices(parts):
    n, rows, _ = parts.shape

    def body(p_ref, o_ref):
        acc = p_ref[0]
        for j in range(1, n):
            acc = acc + p_ref[j]
        o_ref[...] = acc

    return pl.pallas_call(body, name="small_add", grid=(1,),
                          in_specs=[pl.BlockSpec((n, rows, LANES), lambda i: (0, 0, 0))],
                          out_specs=pl.BlockSpec((rows, LANES), lambda i: (0, 0)),
                          out_shape=_sds((rows, LANES), F32))(parts)


def _adam_rows(k, n):
    target = max(8, (1 << 20) // (4 * n))
    if k <= target:
        return k
    best = None
    for d in range(8, target + 1, 8):
        if k % d == 0:
            best = d
    return best if best is not None else k


def _adam_update(w, g, m, v):
    m_ = ADAM_B1 * m + (1.0 - ADAM_B1) * g
    v_ = ADAM_B2 * v + (1.0 - ADAM_B2) * (g * g)
    delta = -ADAM_LR * ((m_ / (1.0 - ADAM_B1 ** ADAM_STEP)) / (jnp.sqrt(v_ / (1.0 - ADAM_B2 ** ADAM_STEP)) + ADAM_EPS)
                        + ADAM_WD * w)
    return delta, m_, v_


def _adamw_many(name, ws, gs, ms, vs):
    n = len(ws)

    def body(*refs):
        for i in range(n):
            w_ref, g_ref, m_ref, v_ref = (refs[k * n + i] for k in range(4))
            d_ref, mo_ref, vo_ref = (refs[(4 + k) * n + i] for k in range(3))
            d_ref[...], mo_ref[...], vo_ref[...] = _adam_update(w_ref[...], g_ref[...], m_ref[...], v_ref[...])

    specs = [pl.BlockSpec(w.shape, lambda i: (0, 0)) for w in ws]
    out = pl.pallas_call(body, name=name, grid=(1,), in_specs=specs * 4, out_specs=specs * 3,
                         out_shape=[_sds(w.shape, F32) for w in ws] * 3)(*ws, *gs, *ms, *vs)
    return [(gs[i], out[i], out[n + i], out[2 * n + i]) for i in range(n)]


def _adamw_halves(name, w, mine, theirs, m, v, half):
    k, n = w.shape
    bk = _adam_rows(k // 2, n)
    nb = k // 2 // bk

    def body(h_ref, w_ref, mine_ref, theirs_ref, m_ref, v_ref, g_ref, d_ref, mo_ref, vo_ref):
        g = jnp.where(pl.program_id(0) == h_ref[0], mine_ref[...], theirs_ref[...])
        g_ref[...] = g
        d_ref[...], mo_ref[...], vo_ref[...] = _adam_update(w_ref[...], g, m_ref[...], v_ref[...])

    full = pl.BlockSpec((bk, n), lambda h, i, c: (h * nb + i, 0))
    part = pl.BlockSpec((bk, n), lambda h, i, c: (i, 0))
    grid_spec = pltpu.PrefetchScalarGridSpec(num_scalar_prefetch=1, grid=(2, nb),
                                             in_specs=[full, part, part, full, full], out_specs=[full] * 4)
    return tuple(pl.pallas_call(
        body, name=name, grid_spec=grid_spec, out_shape=[_sds((k, n), F32)] * 4,
        compiler_params=pltpu.CompilerParams(vmem_limit_bytes=VMEM_LIMIT),
    )(jnp.reshape(half, (1,)).astype(jnp.int32), w, mine, theirs, m, v))


_HBM = pl.BlockSpec(memory_space=pltpu.HBM)


def _me():
    return lax.axis_index("x"), lax.axis_index("y"), lax.axis_index("c")


def _other_chips(x, y):
    return [(1 - x, y), (x, 1 - y), (1 - x, 1 - y)]


def _pass_to_sibling(zones):
    n = len(zones)

    def body(*refs):
        in_refs, out_refs = refs[:n], refs[n:2 * n]
        send_sems, recv_sems = refs[2 * n:]
        x, y, c = _me()
        sent = []
        for a, (in_ref, out_ref) in enumerate(zip(in_refs, out_refs)):
            for j, (cx, cy) in enumerate(_other_chips(x, y)):
                mine, theirs = (2 * cx + cy, c), (2 * cx + cy, 1 - c)
                sems = dict(send_sem=send_sems.at[3 * a + j], recv_sem=recv_sems.at[3 * a + j],
                            device_id=(x, y, 1 - c), device_id_type=MESH)
                sent.append(tuple(pltpu.make_async_remote_copy(src_ref=in_ref.at[part], dst_ref=out_ref.at[part], **sems)
                                  for part in (mine, theirs)))
        for send, _ in sent:
            send.start()
        for _, recv in sent:
            recv.wait_recv()
        for send, _ in sent:
            send.wait_send()

    return pl.pallas_call(
        body, name="pass_to_sibling", out_shape=[_sds(z.shape, z.dtype) for z in zones],
        in_specs=[_HBM] * n, out_specs=[_HBM] * n, input_output_aliases={i: i for i in range(n)},
        scratch_shapes=[pltpu.SemaphoreType.DMA((3 * n,)), pltpu.SemaphoreType.DMA((3 * n,))],
    )(*zones)


def _swap_sibling(name, vs, other_half=False):
    n = len(vs)

    def body(*refs):
        v_refs, out_refs = refs[:n], refs[n:2 * n]
        send_sems, recv_sems = refs[2 * n:]
        x, y, c = _me()
        cps = [pltpu.make_async_remote_copy(src_ref=v_ref.at[:, 1 - c] if other_half else v_ref, dst_ref=out_ref,
                                            send_sem=send_sems.at[a], recv_sem=recv_sems.at[a],
                                            device_id=(x, y, 1 - c), device_id_type=MESH)
               for a, (v_ref, out_ref) in enumerate(zip(v_refs, out_refs))]
        for cp in cps:
            cp.start()
        for cp in cps:
            cp.wait()

    def landing(v):
        return _sds((v.shape[0],) + v.shape[2:] if other_half else v.shape, v.dtype)

    return pl.pallas_call(
        body, name=name, out_shape=[landing(v) for v in vs], in_specs=[_HBM] * n, out_specs=[_HBM] * n,
        scratch_shapes=[pltpu.SemaphoreType.DMA((n,)), pltpu.SemaphoreType.DMA((n,))],
    )(*vs)


_SEM = pl.BlockSpec(memory_space=pltpu.SEMAPHORE)
_EFFECT = pltpu.SideEffectType.DATAFLOW_SIDE_EFFECTING
WHOLE = "whole"
PIECE = "piece"
SIBLING_HALF = "sibling"
MY_HALF = "half"
EVERYONE = "everyone"
_COPIES = {WHOLE: 3, PIECE: 3, MY_HALF: 3, SIBLING_HALF: 1, EVERYONE: 7}


def _landing_shape(v, mode):
    return {WHOLE: (4,) + v.shape, MY_HALF: (4,) + v.shape, PIECE: v.shape, EVERYONE: (8,) + v.shape,
            SIBLING_HALF: (v.shape[0],) + v.shape[2:]}[mode]


def _chip_copies(v_ref, land_ref, send_sems, recv_sems, mode, sem0=0):
    x, y, c = _me()
    if mode == SIBLING_HALF:
        cp = pltpu.make_async_remote_copy(src_ref=v_ref.at[:, 1 - c], dst_ref=land_ref, send_sem=send_sems.at[sem0],
                                          recv_sem=recv_sems.at[sem0], device_id=(x, y, 1 - c), device_id_type=MESH)
        return [(cp, cp)]
    if mode == EVERYONE:
        out = []
        for f in range(1, 8):
            px, py, pc = (1 - x if f & 4 else x), (1 - y if f & 2 else y), (1 - c if f & 1 else c)
            sems = dict(send_sem=send_sems.at[sem0 + f - 1], recv_sem=recv_sems.at[sem0 + f - 1],
                        device_id=(px, py, pc), device_id_type=MESH)
            out.append((pltpu.make_async_remote_copy(src_ref=v_ref, dst_ref=land_ref.at[4 * x + 2 * y + c], **sems),
                        pltpu.make_async_remote_copy(src_ref=v_ref, dst_ref=land_ref.at[4 * px + 2 * py + pc], **sems)))
        return out
    k = 2 * x + y
    out = []
    for j, (cx, cy) in enumerate(_other_chips(x, y)):
        if mode == MY_HALF:
            src, mine, theirs = v_ref.at[c], land_ref.at[k, c], land_ref.at[2 * cx + cy, c]
        else:
            src = v_ref.at[2 * cx + cy] if mode == PIECE else v_ref
            mine, theirs = land_ref.at[k], land_ref.at[2 * cx + cy]
        sems = dict(send_sem=send_sems.at[sem0 + j], recv_sem=recv_sems.at[sem0 + j], device_id=(cx, cy, c),
                    device_id_type=MESH)
        send = pltpu.make_async_remote_copy(src_ref=src, dst_ref=mine, **sems)
        recv = pltpu.make_async_remote_copy(src_ref=src, dst_ref=theirs, **sems)
        out.append((send, recv))
    return out


def _chips_start(name, vs, mode, after=None):
    n = len(vs)
    lands = [_landing_shape(v, mode) for v in vs]

    def body(*refs):
        v_refs, land_refs = refs[:n], refs[n:2 * n]
        send_sems, recv_sems = refs[-2 * n - 3], refs[-2 * n - 2]
        token = refs[-1]
        for a in range(n):
            for send, _ in _chip_copies(v_refs[a], land_refs[a], send_sems, recv_sems, mode, _COPIES[mode] * a):
                send.start()
        token[...] = jnp.zeros_like(token)

    extra = () if after is None else (after,)
    hbm = [pltpu.with_memory_space_constraint(v, pltpu.HBM) for v in vs]
    zones = [pltpu.with_memory_space_constraint(lax.empty(s, v.dtype), pltpu.HBM) for s, v in zip(lands, vs)]
    out = pl.pallas_call(
        body, name=name,
        out_shape=(pltpu.SemaphoreType.DMA((_COPIES[mode] * n,)), pltpu.SemaphoreType.DMA((_COPIES[mode] * n,)),
                   *[pltpu.HBM(v.shape, v.dtype) for v in vs], *[pltpu.HBM(s, v.dtype) for s, v in zip(lands, vs)],
                   _sds((8, LANES), F32)),
        in_specs=(_HBM,) * (2 * n) + (pl.BlockSpec(memory_space=pl.ANY),) * len(extra),
        out_specs=(_SEM, _SEM) + (_HBM,) * (2 * n) + (pl.BlockSpec(memory_space=pltpu.VMEM),),
        input_output_aliases={i: 2 + i for i in range(2 * n)},
        compiler_params=pltpu.CompilerParams(has_side_effects=_EFFECT),
    )(*hbm, *zones, *extra)
    return out[0], out[1], list(out[2:2 + n]), list(out[2 + n:2 + 2 * n]), out[-1]


def _chips_wait(name, send_sems, recv_sems, v_thru, land_thru, mode, after):
    n = len(v_thru)

    def body(*refs):
        v_refs, land_refs = refs[:n], refs[n:2 * n]
        send_sems, recv_sems = refs[2 * n], refs[2 * n + 1]
        for a in range(n):
            for send, recv in _chip_copies(v_refs[a], land_refs[a], send_sems, recv_sems, mode, _COPIES[mode] * a):
                send.wait_send()
                recv.wait_recv()

    out = pl.pallas_call(
        body, name=name,
        out_shape=tuple(pltpu.HBM(a.shape, a.dtype) for a in list(v_thru) + list(land_thru)),
        in_specs=(_HBM,) * (2 * n) + (_SEM, _SEM, pl.BlockSpec(memory_space=pl.ANY)), out_specs=(_HBM,) * (2 * n),
        input_output_aliases={i: i for i in range(2 * n)},
        compiler_params=pltpu.CompilerParams(has_side_effects=_EFFECT),
    )(*v_thru, *land_thru, send_sems, recv_sems, after)
    return list(out[:n]), list(out[n:])


_BIG = (("w_in", (1024, 3232), 1), ("w_uq", (256, 768), 1), ("w_ukv", (128, 1024), 1), ("w_branch_a", (512, 1024), 1),
        ("w_branch_b", (512, 1024), 1), ("w_out", (1024, 1024), 0), ("w_up", (1024, 5632), 1),
        ("w_down", (2816, 1024), 0), ("w_ple_gate", (1024, 1024), 0), ("w_ple", (256, 1024), 1))


def _shard_shape(shape, axis):
    return (shape[0] // 4, shape[1]) if axis == 0 else (shape[0], shape[1] // 4)


def _half_rows(shape, axis):
    k, n = _shard_shape(shape, axis)
    return k * n // (2 * LANES)


_EARLY = ("w_in", "w_uq", "w_ukv")
_LATE = ("w_branch_a", "w_branch_b", "w_out", "w_up", "w_down", "w_ple_gate", "w_ple")
_NATURAL = ("w_in", "w_up", "w_down", "w_out", "w_ple_gate")
_EARLY_PACKED = tuple(b for b in _BIG if b[0] in _EARLY and b[0] not in _NATURAL)
_LATE_PACKED = tuple(b for b in _BIG if b[0] in _LATE and b[0] not in _NATURAL)


def _halves(a):
    return a.reshape(a.shape[:-2] + (2, a.shape[-2] // 2, a.shape[-1]))


def _rows_joined(a):
    return a.reshape(a.shape[:-3] + (a.shape[-3] * a.shape[-2], a.shape[-1]))


def _pack_pad(group):
    return -sum(_half_rows(shape, axis) for _, shape, axis in group) % PACK_ROWS


def _pack_shards(shards, dtype, group):
    parts = [shards[name].astype(dtype).reshape(2, _half_rows(shape, axis), LANES) for name, shape, axis in group]
    return jnp.concatenate(parts + [jnp.zeros((2, _pack_pad(group), LANES), dtype)], axis=1)


def _unpack_gathered(g, group):
    out, off = {}, 0
    for name, shape, axis in group:
        r = _half_rows(shape, axis)
        k, n = _shard_shape(shape, axis)
        w = g[:, :, off:off + r, :].reshape(4, k, n)
        out[name] = w.reshape(shape) if axis == 0 else w.transpose(1, 0, 2).reshape(shape)
        off += r
    return out


def _pack_grads(grads, group):
    parts = []
    for name, shape, axis in group:
        k, n = _shard_shape(shape, axis)
        g = grads[name]
        g4 = g.reshape(4, k, n) if axis == 0 else g.reshape(k, 4, n).transpose(1, 0, 2)
        parts.append(g4.reshape(4, 2, _half_rows(shape, axis), LANES))
    return jnp.concatenate(parts + [jnp.zeros((4, 2, _pack_pad(group), LANES), F32)], axis=2)


def _unpack_shard_grads(f, group):
    out, off = {}, 0
    for name, shape, axis in group:
        r = _half_rows(shape, axis)
        out[name] = f[:, off:off + r, :].reshape(_shard_shape(shape, axis))
        off += r
    return out


def _pad_slots(w, heads, dim):
    k = w.shape[0]
    return jnp.pad(w.reshape(k, heads, dim), ((0, 0), (0, 0), (0, LANES - dim))).reshape(k, heads * LANES)


def _unpad_slots(w, heads, dim):
    k = w.shape[0]
    return w.reshape(k, heads, LANES)[:, :, :dim].reshape(k, heads * dim)


def _pad_w_in(w):
    kr = jnp.pad(w[:, Z_KR:Z_KR + ROPE_DIM], ((0, 0), (NOPE_DIM, LANES - NOPE_DIM - ROPE_DIM)))
    return jnp.concatenate([w[:, :Z_KR], kr, w[:, Z_KR + ROPE_DIM:]], axis=1)


def _unpad_w_in(w):
    return jnp.concatenate([w[:, :Z_KR], w[:, Z_KR + NOPE_DIM:Z_KR + NOPE_DIM + ROPE_DIM], w[:, Z_GATE:ZW]], axis=1)


def _spread_matrix(heads, dim):
    row = lax.broadcasted_iota(jnp.int32, (heads * dim, heads * LANES), 0)
    col = lax.broadcasted_iota(jnp.int32, (heads * dim, heads * LANES), 1)
    return (col == (row // dim) * LANES + row % dim).astype(BF16)


_SMALL = (("attn_pre_norm", 1024), ("attn_post_norm", 1024), ("b_gate", 2048), ("sinks", 8), ("q_a_norm", 256),
          ("kv_a_norm", 128), ("mlp_pre_norm", 1024), ("mlp_post_norm", 1024), ("conv_b", 5632), ("ple_norm", 1024),
          ("conv_w", 3 * 5632), ("loss", 1))


def _small_rows(n):
    return 8 * -(-n // (8 * LANES))


def _pack_small(vals):
    parts = []
    for name, n in _SMALL:
        r = _small_rows(n)
        parts.append(jnp.pad(vals[name].reshape(-1), (0, r * LANES - n)).reshape(r, LANES))
    return jnp.concatenate(parts, axis=0)


def _unpack_small(buf):
    out, off = {}, 0
    for name, n in _SMALL:
        r = _small_rows(n)
        out[name] = buf[off:off + r].reshape(-1)[:n]
        off += r
    return out


def kernel(x, p, positions, attn_pre_norm, attn_post_norm, w_in, b_gate, sinks, q_a_norm, w_uq, kv_a_norm, w_ukv, w_branch_a, w_branch_b, w_out, mlp_pre_norm, mlp_post_norm, w_up, conv_w, conv_b, w_down, ple_norm, w_ple_gate, w_ple, loss_target, m_attn_pre_norm, m_attn_post_norm, m_w_in, m_b_gate, m_sinks, m_q_a_norm, m_w_uq, m_kv_a_norm, m_w_ukv, m_w_branch_a, m_w_branch_b, m_w_out, m_mlp_pre_norm, m_mlp_post_norm, m_w_up, m_conv_w, m_conv_b, m_w_down, m_ple_norm, m_w_ple_gate, m_w_ple, v_attn_pre_norm, v_attn_post_norm, v_w_in, v_b_gate, v_sinks, v_q_a_norm, v_w_uq, v_kv_a_norm, v_w_ukv, v_w_branch_a, v_w_branch_b, v_w_out, v_mlp_pre_norm, v_mlp_post_norm, v_w_up, v_conv_w, v_conv_b, v_w_down, v_ple_norm, v_w_ple_gate, v_w_ple):
    names = ["attn_pre_norm", "attn_post_norm", "w_in", "b_gate", "sinks", "q_a_norm", "w_uq", "kv_a_norm", "w_ukv",
             "w_branch_a", "w_branch_b", "w_out", "mlp_pre_norm", "mlp_post_norm", "w_up", "conv_w", "conv_b",
             "w_down", "ple_norm", "w_ple_gate", "w_ple"]
    wts = dict(zip(names, [attn_pre_norm, attn_post_norm, w_in, b_gate, sinks, q_a_norm, w_uq, kv_a_norm, w_ukv,
                           w_branch_a, w_branch_b, w_out, mlp_pre_norm, mlp_post_norm, w_up, conv_w, conv_b, w_down,
                           ple_norm, w_ple_gate, w_ple]))
    moms = dict(zip(names, [m_attn_pre_norm, m_attn_post_norm, m_w_in, m_b_gate, m_sinks, m_q_a_norm, m_w_uq,
                            m_kv_a_norm, m_w_ukv, m_w_branch_a, m_w_branch_b, m_w_out, m_mlp_pre_norm,
                            m_mlp_post_norm, m_w_up, m_conv_w, m_conv_b, m_w_down, m_ple_norm, m_w_ple_gate, m_w_ple]))
    vars_ = dict(zip(names, [v_attn_pre_norm, v_attn_post_norm, v_w_in, v_b_gate, v_sinks, v_q_a_norm, v_w_uq,
                             v_kv_a_norm, v_w_ukv, v_w_branch_a, v_w_branch_b, v_w_out, v_mlp_pre_norm,
                             v_mlp_post_norm, v_w_up, v_conv_w, v_conv_b, v_w_down, v_ple_norm, v_w_ple_gate, v_w_ple]))
    w2 = {n: a.reshape(a.shape[-2:]) for n, a in wts.items()}
    m2 = {n: a.reshape(a.shape[-2:]) for n, a in moms.items()}
    v2 = {n: a.reshape(a.shape[-2:]) for n, a in vars_.items()}

    t_rows = x.shape[-2]
    tm = min(256, t_rows)
    tm_wide = min(512, t_rows)
    xc, yc, cc = lax.axis_index("x"), lax.axis_index("y"), lax.axis_index("c")
    chip = 2 * xc + yc

    x2d = x.reshape(t_rows, D_MODEL)
    p2d = p.reshape(t_rows, PLE_DIM)
    tgt = loss_target.reshape(t_rows, D_MODEL)
    pos_f = positions.reshape(t_rows, 1).astype(F32)

    def own_slot_filled(gathered, mine):
        return [lax.dynamic_update_slice(g, m[None], (chip, 0, 0, 0)) for g, m in zip(gathered, mine)]

    def shard_lists(group, packed_group, token=0.0):
        ws = {n: w2[n] + token for n in group}
        return [_halves(ws[n].astype(BF16)) for n in group if n in _NATURAL] + [_pack_shards(ws, BF16, packed_group)]

    cw_rows = 3 * 1408 // LANES
    conv_mine = jnp.pad(w2["conv_w"].reshape(cw_rows, LANES), ((0, 48 - cw_rows), (0, 0))).reshape(2, 24, LANES)
    early_mine = shard_lists(_EARLY, _EARLY_PACKED) + [conv_mine]
    early_sems = _chips_start("gather_early_start", early_mine, MY_HALF)
    early_token = early_sems[4][0:1, 0:1]
    consts = _rope_consts()
    tabs = _rope_tables(pos_f + early_token, consts, tm)
    late_mine = shard_lists(_LATE, _LATE_PACKED, early_token)
    both_done = tabs[0][0:1, 0:1] + sum(m[0, 0:1, 0:1].astype(F32) for m in late_mine)
    early_sent, early_landed = _chips_wait("gather_early_wait", *early_sems[:4], MY_HALF, after=both_done)
    early = own_slot_filled(_pass_to_sibling(early_landed), early_sent)
    late_names = [n for n in _LATE if n in _NATURAL]
    first = [late_names.index("w_out"), len(late_names)]
    late_a = [late_mine[i] for i in first]
    late_b = [m for i, m in enumerate(late_mine) if i not in first]
    late_a_sems = _chips_start("gather_late_a_start", late_a, WHOLE, after=early[0])
    late_b_sems = _chips_start("gather_late_b_start", late_b, WHOLE, after=late_a_sems[4])
    late_token = late_b_sems[4][0:1, 0:1]
    full = _unpack_gathered(early[1], _EARLY_PACKED)
    full["w_in"] = _rows_joined(early[0]).transpose(1, 0, 2).reshape(D_MODEL, 3232)
    conv_full = early[2].reshape(4, 48, LANES)[:, :cw_rows].reshape(4, 3, 1408).transpose(1, 0, 2).reshape(3, 2 * D_FF)
    convw8 = jnp.pad(conv_full, ((0, 5), (0, 0)))

    win = _pad_w_in(full["w_in"])
    wuq = _pad_slots(full["w_uq"], HEADS, NOPE_DIM + ROPE_DIM)
    ukv = full["w_ukv"].reshape(KV_LORA, HEADS, NOPE_DIM + V_DIM)
    wk = _pad_slots(ukv[:, :, :NOPE_DIM].reshape(KV_LORA, HEADS * NOPE_DIM), HEADS, NOPE_DIM)
    wv = _pad_slots(ukv[:, :, NOPE_DIM:].reshape(KV_LORA, HEADS * V_DIM), HEADS, V_DIM)
    g1, g2, g3, g4, g5 = (w2["attn_pre_norm"], w2["attn_post_norm"], w2["mlp_pre_norm"], w2["mlp_post_norm"],
                          w2["ple_norm"])
    gq, gkv, bg, convb = w2["q_a_norm"], w2["kv_a_norm"], w2["b_gate"], w2["conv_b"]
    swa_tile = min(SWA_TILE, t_rows)
    sink_rows = jnp.repeat(w2["sinks"].reshape(A_KV_HEADS, SWA_GROUP, 1), swa_tile, axis=2).reshape(
        A_KV_HEADS, 1, SWA_GROUP * swa_tile)
    swa_bias = _swa_bias(swa_tile)
    spread_q = _spread_matrix(HEADS, A_HEAD_DIM)
    spread_kv = _spread_matrix(A_KV_HEADS, A_HEAD_DIM)

    h1, qs, ks, vs, cq, cqn, ckv, ckvn, qm, km, vm, gate = _fwd_in(x2d, g1, win, bg + late_token, gq, gkv, wuq, wk, wv,
                                                                   spread_q, spread_kv, tabs, tm_wide)
    ya, lse_a = _swa_fwd(qs, ks, vs, swa_bias, sink_rows)
    yb, lse_b = _mla_fwd(qm, km, vm)
    late_sent, late_landed = _chips_wait("gather_late_a_wait", *late_a_sems[:4], WHOLE, after=yb)
    wout_g, packed_g = own_slot_filled(late_landed, late_sent)
    full = _unpack_gathered(packed_g, _LATE_PACKED)
    wba, wbb = full["w_branch_a"], full["w_branch_b"]
    wple = full["w_ple"]
    wout = _rows_joined(wout_g).reshape(-1, D_MODEL)
    pa, pb, mixed, o, x1, h2, ya_c, yb_c = _fwd_mix(x2d, ya, yb, gate, wba, wbb, wout, g2, g3, tm_wide)
    late_sent, late_landed = _chips_wait("gather_late_b_wait", *late_b_sems[:4], WHOLE, after=pa)
    natural = dict(zip([n for n in late_names if n != "w_out"], own_slot_filled(late_landed, late_sent)))
    wup = _rows_joined(natural["w_up"])
    wdown, wpg = (_rows_joined(natural[n]).reshape(-1, D_MODEL) for n in ("w_down", "w_ple_gate"))
    up, a = _fwd_up(h2, wup, convw8, convb, tm)
    ff, x2, e, n5, sg, dx3, loss_part = _fwd_out(a, wdown, x1, g4, p2d, wple, g5, wpg, tgt, tm_wide)

    dpre, de, dx2, dff, du, dg5, dg4, dconvb, dconvw8 = _bwd_out(dx3, e, sg, x2, ff, g5, g4, wpg, wdown, up, convw8,
                                                                 convb, tm)
    dup, dx1, do, dpa, dpb, dgates, dya, dyb, delta_a, delta_b, dg3, dg2, dbg = _bwd_mid(
        du, convw8, wup, dx2, x1, g3, o, g2, wout, gate, pa, pb, wba, wbb, ya_c, yb_c, tm)
    late_grads = {
        "w_branch_a": _mm_tn("dw_branch_a", ya_c, dpa),
        "w_branch_b": _mm_tn("dw_branch_b", yb_c, dpb),
        "w_out": _mm_tn("dw_out", mixed, do).reshape(4, D_MODEL // 4, D_MODEL),
        "w_up": _mm_tn("dw_up", h2, dup, column_shards=4),
        "w_down": _mm_tn("dw_down", a, dff).reshape(4, D_FF // 4, D_MODEL),
        "w_ple_gate": _mm_tn("dw_ple_gate", n5, dpre).reshape(4, D_MODEL // 4, D_MODEL),
        "w_ple": _mm_tn("dw_ple", p2d, de),
    }

    def grad_views(grads, group, packed_group):
        return [_halves(grads[n]) for n in group if n in _NATURAL] + [_pack_grads(grads, packed_group)]

    def pair_sums(tag, views, theirs):
        return [_add_pair("rs_%s_add_pair_%d" % (tag, i), g, r, cc) for i, (g, r) in enumerate(zip(views, theirs))]

    swap_sems = _chips_start("swap_late_start", grad_views(late_grads, _LATE, _LATE_PACKED), SIBLING_HALF)
    delta_a = delta_a.reshape(A_KV_HEADS, SWA_GROUP, t_rows // swa_tile, swa_tile).transpose(0, 2, 1, 3).reshape(
        A_KV_HEADS, t_rows // swa_tile, 1, SWA_GROUP * swa_tile)
    dqs, dks, dvs, dsink_rows = _swa_bwd(qs, ks, vs, dya, lse_a, delta_a, swa_bias,
                                          sink_rows + swap_sems[4][0:1, 0:1])
    dsink = dsink_rows[:, 0:SWA_GROUP, 0]
    late_views, late_theirs = _chips_wait("swap_late_wait", *swap_sems[:4], SIBLING_HALF, after=dqs)
    rs_sems = _chips_start("scatter_late_start", pair_sums("late", late_views, late_theirs), PIECE)
    dqm, dkm, dvm = _mla_bwd(qm, km, vm, dyb, lse_b, delta_b.reshape(HEADS, 1, t_rows) + rs_sems[4][0:1, 0:1])
    dz, dqb, dx, dgq, dgkv, dg1 = _bwd_in(dqs, dks, dvs, dqm, dkm, dvm, tabs, consts, cq, ckv, gq, gkv, wuq, wk, wv,
                                           dgates, win, x2d, g1, dx1, tm)

    small = {"attn_pre_norm": dg1, "attn_post_norm": dg2, "b_gate": dbg, "sinks": dsink, "q_a_norm": dgq,
             "kv_a_norm": dgkv, "mlp_pre_norm": dg3, "mlp_post_norm": dg4, "conv_b": dconvb, "ple_norm": dg5,
             "conv_w": dconvw8[0:3], "loss": loss_part}
    small_sems = _chips_start("gather_small_start", [_pack_small(small)], EVERYONE)
    small_token = small_sems[4]

    dwk = _unpad_slots(_mm_tn("dw_k", ckvn, dkm, after=small_token), HEADS, NOPE_DIM).reshape(
        KV_LORA, HEADS, NOPE_DIM)
    dwv = _unpad_slots(_mm_tn("dw_v", ckvn, dvm, after=small_token), HEADS, V_DIM).reshape(KV_LORA, HEADS, V_DIM)
    early_grads = {
        "w_in": _unpad_w_in(_mm_tn("dw_in", h1, dz, after=small_token)).reshape(D_MODEL, 4, 808).transpose(1, 0, 2),
        "w_uq": _unpad_slots(_mm_tn("dw_uq", cqn, dqb, after=small_token), HEADS, NOPE_DIM + ROPE_DIM),
        "w_ukv": jnp.concatenate([dwk, dwv], axis=2).reshape(KV_LORA, HEADS * (NOPE_DIM + V_DIM)),
    }

    def finish(tag, pairs, landed, group, packed_group):
        reduced = []
        for i, (pair, land) in enumerate(zip(pairs, landed)):
            own = lax.dynamic_index_in_dim(pair, chip, 0, keepdims=True)
            reduced.append(_add_chips("rs_%s_add_chips_%d" % (tag, i),
                                      lax.dynamic_update_slice(land, own, (chip, 0, 0))))
        others = _swap_sibling("swap_%s_reduced_halves" % tag, reduced)
        r, o = reduced[-1], others[-1]
        packed = jnp.where(cc == 0, jnp.stack([r, o]), jnp.stack([o, r]))
        shards = _unpack_shard_grads(packed, packed_group)
        updates.update(zip(shards, _adamw_many("adamw_%s_packed" % tag, [w2[n] for n in shards], list(shards.values()),
                                               [m2[n] for n in shards], [v2[n] for n in shards])))
        for n, r, o in zip([n for n in group if n in _NATURAL], reduced, others):
            updates[n] = _adamw_halves("adamw_" + n, w2[n], r, o, m2[n], v2[n], cc)

    updates = {}

    early_views = grad_views(early_grads, _EARLY, _EARLY_PACKED)
    early_theirs = _swap_sibling("swap_early_grad_halves", early_views, other_half=True)
    small_sent, small_landed = _chips_wait("gather_small_wait", *small_sems[:4], EVERYONE, after=early_theirs[0])
    small_all = lax.dynamic_update_slice(small_landed[0], small_sent[0][None], (4 * xc + 2 * yc + cc, 0, 0))
    early_sems = _chips_start("scatter_early_start", pair_sums("early", early_views, early_theirs), PIECE,
                              after=small_all)
    late_pairs, late_landed = _chips_wait("scatter_late_wait", *rs_sems[:4], PIECE, after=early_sems[4])
    finish("late", late_pairs, late_landed, _LATE, _LATE_PACKED)
    early_pairs, early_landed = _chips_wait("scatter_early_wait", *early_sems[:4], PIECE,
                                            after=updates[_LATE[-1]][1])
    finish("early", early_pairs, early_landed, _EARLY, _EARLY_PACKED)

    small_sum = _unpack_small(_add_devices(small_all))
    small_names = [n for n in names if n in small_sum]
    small_grads = [lax.dynamic_index_in_dim(small_sum[n].reshape(3, 4, 1408), chip, 1, keepdims=False)
                   if n == "conv_w" else small_sum[n].reshape(w2[n].shape) for n in small_names]
    updates.update(zip(small_names, _adamw_many("adamw_small", [w2[n] for n in small_names], small_grads,
                                                [m2[n] for n in small_names], [v2[n] for n in small_names])))
    loss = small_sum["loss"][0]

    outs = [[updates[n][i].reshape(wts[n].shape) for n in names] for i in range(4)]
    return (loss, dx.reshape(x.shape), *outs[0], *outs[1], *outs[2], *outs[3])
```

```python
import math

import numpy as np
import jax
import jax.numpy as jnp
from jax import lax
from jax.experimental import pallas as pl
from jax.experimental.pallas import tpu as pltpu

F32 = jnp.float32
BF16 = jnp.bfloat16

D_MODEL = 1024
D_FF = 2816
PLE_DIM = 256
ROPE_THETA = 10000.0
RMS_EPS = 1e-6
SWA_WINDOW = 128
HEADS = 8
A_KV_HEADS = 2
A_HEAD_DIM = 64
KV_LORA = 128
NOPE_DIM = 64
ROPE_DIM = 32
V_DIM = 64
LANES = 128
ZW = 3328
NEG = -1e30
SCALE_A = A_HEAD_DIM ** -0.5
SCALE_B = (NOPE_DIM + ROPE_DIM) ** -0.5

ADAM_LR = 0.001
ADAM_B1 = 0.9
ADAM_B2 = 0.999
ADAM_EPS = 1e-08
ADAM_WD = 0.01
ADAM_STEP = 10

VMEM_LIMIT = 60 * 1024 * 1024
MESH = pl.DeviceIdType.MESH

Z_QA, Z_KA, Z_VA, Z_CQ, Z_CKV, Z_KR, Z_GATE = 0, 512, 640, 768, 1024, 1152, 1280


def _dot(a, b):
    return jnp.dot(a, b, preferred_element_type=F32)


def _dot_nt(a, b):
    return lax.dot_general(a, b, (((1,), (1,)), ((), ())), preferred_element_type=F32)


def _dot_tn(a, b):
    return lax.dot_general(a, b, (((0,), (0,)), ((), ())), preferred_element_type=F32)


def _rms_stats(x):
    r = lax.rsqrt(jnp.mean(x * x, axis=-1, keepdims=True) + RMS_EPS)
    return x * r, r


def _rms_bwd(dy, xn, r, g):
    dxn = dy * g
    dx = r * (dxn - xn * jnp.mean(dxn * xn, axis=-1, keepdims=True))
    dg = jnp.sum(dy * xn, axis=0, keepdims=True)
    return dx, dg


def _tile_lanes(t, n):
    return t if n == 1 else jnp.concatenate([t] * n, axis=1)


def _rope(x, c, s1, s2, half):
    w = x.shape[1]
    n = w // LANES
    return (x * _tile_lanes(c, n) + pltpu.roll(x, w - half, 1) * _tile_lanes(s1, n)
            + pltpu.roll(x, half, 1) * _tile_lanes(s2, n))


def _rope_t(dy, c, s1, s2, half):
    w = dy.shape[1]
    n = w // LANES
    return (dy * _tile_lanes(c, n) + pltpu.roll(dy * _tile_lanes(s1, n), half, 1)
            + pltpu.roll(dy * _tile_lanes(s2, n), w - half, 1))


def _fold_slots(d):
    tiles = []
    for j in range(d.shape[1] // (2 * LANES)):
        even = d[:, 2 * j * LANES:(2 * j + 1) * LANES]
        odd = d[:, (2 * j + 1) * LANES:(2 * j + 2) * LANES]
        tiles.append(even + pltpu.roll(odd, A_HEAD_DIM, 1))
    return tiles[0] if len(tiles) == 1 else jnp.concatenate(tiles, axis=1)


def _spread_slots(c):
    low = lax.broadcasted_iota(jnp.int32, (c.shape[0], LANES), 1) < A_HEAD_DIM
    slots = []
    for j in range(c.shape[1] // LANES):
        tile = c[:, j * LANES:(j + 1) * LANES]
        slots += [jnp.where(low, tile, 0.0), jnp.where(low, pltpu.roll(tile, A_HEAD_DIM, 1), 0.0)]
    return jnp.concatenate(slots, axis=1)


def _sigmoid(x):
    return 1.0 / (1.0 + jnp.exp(-x))


_GELU_C = math.sqrt(2.0 / math.pi)


def _gelu_and_grad(x):
    a = _GELU_C + (_GELU_C * 0.044715) * (x * x)
    th = jnp.tanh(x * a)
    hx = 0.5 * x
    p1 = 1.0 + th
    gel = hx * p1
    dgel = 0.5 * p1 + (hx * (1.0 - th * th)) * (3.0 * a - 2.0 * _GELU_C)
    return gel, dgel


def _conv_taps(up, h6, h7):
    r1 = pltpu.roll(up, 1, 0)
    r2 = pltpu.roll(up, 2, 0)
    rows = lax.broadcasted_iota(jnp.int32, (8, up.shape[1]), 0)
    xm1 = jnp.concatenate([jnp.where(rows == 0, h7, r1[0:8]), r1[8:]], axis=0)
    xm2 = jnp.concatenate([jnp.where(rows == 0, h6, jnp.where(rows == 1, h7, r2[0:8])), r2[8:]], axis=0)
    return xm1, xm2


def _conv_taps_next(du, n0, n1):
    tm = du.shape[0]
    r1 = pltpu.roll(du, tm - 1, 0)
    r2 = pltpu.roll(du, tm - 2, 0)
    rows = lax.broadcasted_iota(jnp.int32, (8, du.shape[1]), 0)
    xp1 = jnp.concatenate([r1[:tm - 8], jnp.where(rows == 7, n0, r1[tm - 8:])], axis=0)
    xp2 = jnp.concatenate([r2[:tm - 8], jnp.where(rows == 6, n0, jnp.where(rows == 7, n1, r2[tm - 8:]))], axis=0)
    return xp1, xp2


def _row(tm, n):
    return pl.BlockSpec((tm, n), lambda i: (i, 0))


def _full(shape):
    nd = len(shape)
    return pl.BlockSpec(tuple(shape), lambda i: (0,) * nd)


def _resident(shape):
    nd = len(shape)
    return pl.BlockSpec(tuple(shape), lambda i: (0,) * nd, pipeline_mode=pl.Buffered(1))


def _heads(tm, h):
    return pl.BlockSpec((h, tm, LANES), lambda i: (0, i, 0))


def _rows_call(name, body, t_rows, tm, ins, outs, scratch=()):
    return pl.pallas_call(
        body, name=name, grid=(t_rows // tm,),
        in_specs=[s for _, s in ins],
        out_specs=[s for _, s in outs],
        out_shape=[s for s, _ in outs],
        scratch_shapes=list(scratch),
        compiler_params=pltpu.CompilerParams(dimension_semantics=("arbitrary",), vmem_limit_bytes=VMEM_LIMIT),
    )(*[a for a, _ in ins])


def _sds(shape, dtype):
    return jax.ShapeDtypeStruct(tuple(shape), dtype)


def _rope_consts():
    c = np.zeros((16, LANES), np.float32)
    lane = np.arange(LANES)
    inv_a = (ROPE_THETA ** (-(np.arange(0, A_HEAD_DIM, 2, dtype=np.float32) / A_HEAD_DIM))).astype(np.float32)
    in_a = lane < A_HEAD_DIM
    c[0, in_a] = inv_a[lane[in_a] % (A_HEAD_DIM // 2)]
    c[1, in_a] = 1.0
    c[2, lane < A_HEAD_DIM // 2] = -1.0
    c[3, (lane >= A_HEAD_DIM // 2) & in_a] = 1.0
    inv_b = (ROPE_THETA ** (-(np.arange(0, ROPE_DIM, 2, dtype=np.float32) / ROPE_DIM))).astype(np.float32)
    pe = (lane >= NOPE_DIM) & (lane < NOPE_DIM + ROPE_DIM)
    c[5, pe] = inv_b[(lane[pe] - NOPE_DIM) % (ROPE_DIM // 2)]
    c[6, pe] = 1.0
    c[7, (lane >= NOPE_DIM) & (lane < NOPE_DIM + ROPE_DIM // 2)] = -1.0
    c[8, (lane >= NOPE_DIM + ROPE_DIM // 2) & (lane < NOPE_DIM + ROPE_DIM)] = 1.0
    c[9, lane < NOPE_DIM] = 1.0
    c[10, pe] = 1.0
    return jnp.asarray(c)


def _rope_tables(pos_f, consts, tm):
    t_rows = pos_f.shape[0]

    def body(pos_ref, c_ref, ca, sa1, sa2, cb, sb1, sb2):
        ang = pos_ref[...] * (c_ref[0:1, :] + c_ref[5:6, :])
        cs, sn = jnp.cos(ang), jnp.sin(ang)
        for ref, row in ((ca, 1), (sa1, 2), (sa2, 3)):
            half = (cs if row == 1 else sn) * c_ref[row:row + 1, :]
            ref[...] = half + pltpu.roll(half, A_HEAD_DIM, 1)
        cb[...] = cs * c_ref[6:7, :] + c_ref[9:10, :]
        sb1[...] = sn * c_ref[7:8, :]
        sb2[...] = sn * c_ref[8:9, :]

    tab = (_sds((t_rows, LANES), F32), _row(tm, LANES))
    return _rows_call("rope_tables", body, t_rows, tm,
                      [(pos_f, _row(tm, 1)), (consts, _full(consts.shape))], [tab] * 6)


def _fwd_in(x, g1, win, bg, gq, gkv, wuq, wk, wv, eq, ek, tabs, tm):
    t_rows = x.shape[0]

    def body(x_ref, g1_ref, win_ref, bg_ref, gq_ref, gkv_ref, wuq_ref, wk_ref, wv_ref, eq_ref, ek_ref,
             ca, sa1, sa2, cb, sb1, sb2,
             h1_ref, qs_ref, ks_ref, vs_ref, cq_ref, cqn_ref, ckv_ref, ckvn_ref, qm_ref, km_ref, vm_ref, gate_ref):
        xn, _ = _rms_stats(x_ref[...])
        hb = (xn * g1_ref[...]).astype(BF16)
        h1_ref[...] = hb
        ta = (ca[...], sa1[...], sa2[...])
        tb = (cb[...], sb1[...], sb2[...])
        cq = _dot(hb, win_ref[:, Z_CQ:Z_CKV])
        ckv = _dot(hb, win_ref[:, Z_CKV:Z_KR])
        z_qa = _dot(hb, win_ref[:, Z_QA:Z_KA])
        z_ka = _dot(hb, win_ref[:, Z_KA:Z_VA])
        z_va = _dot(hb, win_ref[:, Z_VA:Z_CQ])
        z_kr = _dot(hb, win_ref[:, Z_KR:Z_GATE])
        cq_ref[...] = cq
        cqn, _ = _rms_stats(cq)
        cqb = (cqn * gq_ref[...]).astype(BF16)
        cqn_ref[...] = cqb
        ckv_ref[...] = ckv
        ckvn, _ = _rms_stats(ckv)
        ckvb = (ckvn * gkv_ref[...]).astype(BF16)
        ckvn_ref[...] = ckvb
        z_qm = _dot(cqb, wuq_ref[...])
        z_km = _dot(ckvb, wk_ref[...])
        z_vm = _dot(ckvb, wv_ref[...])
        z_gate = _dot(hb, win_ref[:, Z_GATE:ZW])
        qs_ref[...] = _dot((_rope(z_qa, *ta, A_HEAD_DIM // 2) * SCALE_A).astype(BF16), eq_ref[...]).astype(BF16)
        ks_ref[...] = _dot(_rope(z_ka, *ta, A_HEAD_DIM // 2).astype(BF16), ek_ref[...]).astype(BF16)
        vs_ref[...] = _dot(z_va.astype(BF16), ek_ref[...]).astype(BF16)
        qm_ref[...] = (_rope(z_qm, *tb, ROPE_DIM // 2) * SCALE_B).astype(BF16)
        km_ref[...] = (z_km + _tile_lanes(_rope(z_kr, *tb, ROPE_DIM // 2), HEADS)).astype(BF16)
        vm_ref[...] = z_vm.astype(BF16)
        gate_ref[...] = _sigmoid(z_gate + bg_ref[...]).astype(BF16)

    def o(n, dt):
        return (_sds((t_rows, n), dt), _row(tm, n))

    ins = [(x, _row(tm, D_MODEL)), (g1, _full(g1.shape)), (win, _resident(win.shape)), (bg, _full(bg.shape)),
           (gq, _full(gq.shape)), (gkv, _full(gkv.shape)), (wuq, _full(wuq.shape)), (wk, _full(wk.shape)),
           (wv, _full(wv.shape)), (eq, _full(eq.shape)), (ek, _full(ek.shape))] + [(t, _row(tm, LANES)) for t in tabs]
    outs = [o(1024, BF16), o(1024, BF16), o(256, BF16), o(256, BF16), o(256, F32), o(256, BF16), o(128, F32),
            o(128, BF16), o(1024, BF16), o(1024, BF16), o(1024, BF16), o(2048, BF16)]
    return _rows_call("fwd_in", body, t_rows, tm, ins, outs)


def _attn_tile(t_rows):
    return min(512, t_rows)


MLA_HEADS_PER_STEP = 4
MLA_FWD_HEADS_PER_STEP = 8


def _causal_pairs(nq, by_kv):
    if by_kv:
        pairs = [(i, j) for j in range(nq) for i in range(j, nq)]
    else:
        pairs = [(i, j) for i in range(nq) for j in range(i + 1)]
    return (jnp.asarray([p[0] for p in pairs], jnp.int32), jnp.asarray([p[1] for p in pairs], jnp.int32))


def _mla_fwd(q, k, v):
    t_rows = q.shape[0]
    t = _attn_tile(t_rows)
    hp = MLA_FWD_HEADS_PER_STEP
    w = hp * LANES
    ii, jj = _causal_pairs(t_rows // t, by_kv=False)

    def body(i_ref, j_ref, q_ref, k_ref, v_ref, o_ref, lse_ref, m_s, l_s, acc_s):
        i = i_ref[pl.program_id(1)]
        j = j_ref[pl.program_id(1)]

        @pl.when(j == 0)
        def _():
            m_s[...] = jnp.full(m_s.shape, NEG, F32)
            l_s[...] = jnp.zeros(l_s.shape, F32)
            acc_s[...] = jnp.zeros(acc_s.shape, F32)

        def step(diagonal):
            sls = [slice(hh * LANES, (hh + 1) * LANES) for hh in range(hp)]
            scores = [_dot_nt(k_ref[:, sl], q_ref[:, sl]) for sl in sls]
            if diagonal:
                valid = (lax.broadcasted_iota(jnp.int32, (t, t), 0) <= lax.broadcasted_iota(jnp.int32, (t, t), 1))
                scores = [jnp.where(valid, s, NEG) for s in scores]
            stats = []
            for hh, s in enumerate(scores):
                m_prev = m_s[hh]
                m_new = jnp.maximum(m_prev, jnp.max(s, axis=0, keepdims=True))
                p = jnp.exp(s - m_new)
                alpha = jnp.exp(m_prev - m_new)
                stats.append((m_new, alpha, alpha * l_s[hh] + jnp.sum(p, axis=0, keepdims=True), p.astype(BF16)))
            for hh, (m_new, alpha, l_new, p) in enumerate(stats):
                sl = sls[hh]
                acc = alpha * acc_s[hh] + _dot_tn(v_ref[:, sl], p)
                if diagonal:
                    o_ref[:, sl] = (acc / l_new).T.astype(o_ref.dtype)
                    lse_ref[hh] = m_new + jnp.log(l_new)
                else:
                    m_s[hh] = m_new
                    l_s[hh] = l_new
                    acc_s[hh] = acc

        pl.when(j < i)(lambda: step(False))
        pl.when(j == i)(lambda: step(True))

    grid_spec = pltpu.PrefetchScalarGridSpec(
        num_scalar_prefetch=2, grid=(HEADS // hp, ii.shape[0]),
        in_specs=[pl.BlockSpec((t, w), lambda hb, s, ir, jr: (ir[s], hb)),
                  pl.BlockSpec((t, w), lambda hb, s, ir, jr: (jr[s], hb)),
                  pl.BlockSpec((t, w), lambda hb, s, ir, jr: (jr[s], hb))],
        out_specs=[pl.BlockSpec((t, w), lambda hb, s, ir, jr: (ir[s], hb)),
                   pl.BlockSpec((hp, 1, t), lambda hb, s, ir, jr: (hb, 0, ir[s]))],
        scratch_shapes=[pltpu.VMEM((hp, 1, t), F32), pltpu.VMEM((hp, 1, t), F32), pltpu.VMEM((hp, LANES, t), F32)])
    return pl.pallas_call(
        body, name="mla_fwd", grid_spec=grid_spec,
        out_shape=[_sds((t_rows, HEADS * LANES), BF16), _sds((HEADS, 1, t_rows), F32)],
        compiler_params=pltpu.CompilerParams(dimension_semantics=("arbitrary",) * 2, vmem_limit_bytes=VMEM_LIMIT),
    )(ii, jj, q, k, v)


def _mla_bwd(q, k, v, do, lse, delta):
    t_rows = q.shape[0]
    t = _attn_tile(t_rows)
    hp = MLA_HEADS_PER_STEP
    w = hp * LANES
    ii, jj = _causal_pairs(t_rows // t, by_kv=True)

    def body(i_ref, j_ref, q_ref, k_ref, v_ref, do_ref, lse_ref, dl_ref, dq_ref, dk_ref, dv_ref):
        i = i_ref[pl.program_id(1)]
        j = j_ref[pl.program_id(1)]

        @pl.when(pl.program_id(1) == 0)
        def _():
            dq_ref[...] = jnp.zeros(dq_ref.shape, F32)

        def step(diagonal):
            r0 = pl.multiple_of(i * t, t)
            sls = [slice(hh * LANES, (hh + 1) * LANES) for hh in range(hp)]
            scores = [_dot_nt(k_ref[:, sl], q_ref[:, sl]) for sl in sls]
            if diagonal:
                valid = (lax.broadcasted_iota(jnp.int32, (t, t), 0) <= lax.broadcasted_iota(jnp.int32, (t, t), 1))
                scores = [jnp.where(valid, s, NEG) for s in scores]
            dps = [_dot_nt(v_ref[:, sl], do_ref[:, sl]) for sl in sls]
            ps = [jnp.exp(s - lse_ref[hh]) for hh, s in enumerate(scores)]
            dss = [(p * (dp - dl_ref[hh])).astype(BF16) for hh, (p, dp) in enumerate(zip(ps, dps))]
            for hh, sl in enumerate(sls):
                dv = _dot(ps[hh].astype(BF16), do_ref[:, sl])
                dk = _dot(dss[hh], q_ref[:, sl])
                if diagonal:
                    dv_ref[:, sl] = dv
                    dk_ref[:, sl] = dk
                else:
                    dv_ref[:, sl] += dv
                    dk_ref[:, sl] += dk
                dq_ref[hh, pl.ds(r0, t), :] += _dot_tn(dss[hh], k_ref[:, sl])

        pl.when(i > j)(lambda: step(False))
        pl.when(i == j)(lambda: step(True))

    def qmap(hb, s, ir, jr):
        return (ir[s], hb)

    def kvmap(hb, s, ir, jr):
        return (jr[s], hb)

    def rowmap(hb, s, ir, jr):
        return (hb, 0, ir[s])

    grid_spec = pltpu.PrefetchScalarGridSpec(
        num_scalar_prefetch=2, grid=(HEADS // hp, ii.shape[0]),
        in_specs=[pl.BlockSpec((t, w), qmap), pl.BlockSpec((t, w), kvmap), pl.BlockSpec((t, w), kvmap),
                  pl.BlockSpec((t, w), qmap), pl.BlockSpec((hp, 1, t), rowmap), pl.BlockSpec((hp, 1, t), rowmap)],
        out_specs=[pl.BlockSpec((hp, t_rows, LANES), lambda hb, s, ir, jr: (hb, 0, 0)),
                   pl.BlockSpec((t, w), kvmap), pl.BlockSpec((t, w), kvmap)])
    return pl.pallas_call(
        body, name="mla_bwd", grid_spec=grid_spec,
        out_shape=[_sds((HEADS, t_rows, LANES), F32), _sds((t_rows, HEADS * LANES), F32),
                   _sds((t_rows, HEADS * LANES), F32)],
        compiler_params=pltpu.CompilerParams(dimension_semantics=("arbitrary",) * 2, vmem_limit_bytes=VMEM_LIMIT),
    )(ii, jj, q, k, v, do, lse, delta)


SWA_TILE = 2 * SWA_WINDOW
SWA_GROUP = HEADS // A_KV_HEADS


def _swa_bias(tq):
    koff = lax.broadcasted_iota(jnp.int32, (tq + SWA_WINDOW, SWA_GROUP * tq), 0) - SWA_WINDOW
    qoff = (lax.broadcasted_iota(jnp.int32, (tq + SWA_WINDOW, SWA_GROUP * tq), 1) % tq)
    band = (koff <= qoff) & (qoff - koff < SWA_WINDOW)
    return jnp.stack([jnp.where(band & (koff >= 0), 0.0, NEG), jnp.where(band, 0.0, NEG)]).astype(F32)


def _swa_specs(tq, nq):
    wb = tq // SWA_WINDOW
    kvw = A_KV_HEADS * LANES

    def qi(i):
        return jnp.minimum(i, nq - 1)

    q = pl.BlockSpec((tq, HEADS * LANES), lambda i: (qi(i), 0))
    cur = pl.BlockSpec((tq, kvw), lambda i: (qi(i), 0))
    prev = pl.BlockSpec((SWA_WINDOW, kvw), lambda i: (jnp.maximum(qi(i) * wb - 1, 0), 0))
    bias = pl.BlockSpec((1, tq + SWA_WINDOW, SWA_GROUP * tq), lambda i: (jnp.minimum(i, 1), 0, 0))
    rows = pl.BlockSpec((A_KV_HEADS, 1, 1, SWA_GROUP * tq), lambda i: (0, qi(i), 0, 0))
    sink = pl.BlockSpec((A_KV_HEADS, 1, SWA_GROUP * tq), lambda i: (0, 0, 0))
    return q, cur, prev, bias, rows, sink


def _stack_heads(ref, kvh):
    base = kvh * SWA_GROUP
    return jnp.concatenate([ref[:, (base + g) * LANES:(base + g + 1) * LANES] for g in range(SWA_GROUP)], axis=0)


def _unstack_heads(ref, kvh, val, tq):
    base = kvh * SWA_GROUP
    for g in range(SWA_GROUP):
        ref[:, (base + g) * LANES:(base + g + 1) * LANES] = val[g * tq:(g + 1) * tq].astype(ref.dtype)


def _kv_window(prev_ref, cur_ref, kvh):
    sl = slice(kvh * LANES, (kvh + 1) * LANES)
    return jnp.concatenate([prev_ref[:, sl], cur_ref[:, sl]], axis=0)


def _swa_fwd(q, k, v, bias, sink_rows):
    t_rows = q.shape[0]
    tq = min(SWA_TILE, t_rows)
    nq = t_rows // tq
    qs_, cur, prev, bs, rows, sk = _swa_specs(tq, nq)
    kvhs = range(A_KV_HEADS)

    def body(q_ref, kc_ref, kp_ref, vc_ref, vp_ref, b_ref, sink_ref, o_ref, lse_ref):
        scores = [_dot_nt(_kv_window(kp_ref, kc_ref, h), _stack_heads(q_ref, h)) + b_ref[0] for h in kvhs]
        stats = []
        for h, s in zip(kvhs, scores):
            sink = sink_ref[h]
            m = jnp.maximum(jnp.max(s, axis=0, keepdims=True), sink)
            p = jnp.exp(s - m)
            l = jnp.sum(p, axis=0, keepdims=True) + jnp.exp(sink - m)
            lse_ref[h, 0] = m + jnp.log(l)
            stats.append((p.astype(BF16), l))
        for h, (p, l) in zip(kvhs, stats):
            _unstack_heads(o_ref, h, (_dot_tn(_kv_window(vp_ref, vc_ref, h), p) / l).T, tq)

    return pl.pallas_call(
        body, name="swa_fwd", grid=(nq,),
        in_specs=[qs_, cur, prev, cur, prev, bs, sk],
        out_specs=[qs_, rows],
        out_shape=[_sds((t_rows, HEADS * LANES), BF16), _sds((A_KV_HEADS, nq, 1, SWA_GROUP * tq), F32)],
        compiler_params=pltpu.CompilerParams(dimension_semantics=("arbitrary",), vmem_limit_bytes=VMEM_LIMIT),
    )(q, k, k, v, v, bias, sink_rows)


def _swa_bwd(q, k, v, o, do, lse, bias, sink_rows):
    t_rows = q.shape[0]
    tq = min(SWA_TILE, t_rows)
    nq = t_rows // tq
    qs_, cur, prev, bs, rows, sk = _swa_specs(tq, nq)
    hw = SWA_WINDOW
    kvhs = range(A_KV_HEADS)
    kvw = A_KV_HEADS * LANES

    def body(q_ref, kc_ref, kp_ref, vc_ref, vp_ref, o_ref, do_ref, lse_ref, b_ref, sink_ref,
             dq_ref, dk_ref, dv_ref, dsink_ref, ck, cv, dsa):
        i = pl.program_id(0)

        @pl.when(i == 0)
        def _():
            dsa[...] = jnp.zeros(dsa.shape, F32)

        @pl.when(i < nq)
        def _():
            qs = [_stack_heads(q_ref, h) for h in kvhs]
            dos = [_stack_heads(do_ref, h) for h in kvhs]
            kks = [_kv_window(kp_ref, kc_ref, h) for h in kvhs]
            scores = [_dot_nt(kks[h], qs[h]) for h in kvhs]
            dps = [_dot_nt(_kv_window(vp_ref, vc_ref, h), dos[h]) for h in kvhs]
            ps, dss = [], []
            for h in kvhs:
                lse = lse_ref[h, 0]
                p = jnp.exp(scores[h] + b_ref[0] - lse)
                delta = jnp.sum((_stack_heads(o_ref, h).astype(F32) * dos[h].astype(F32)).T, axis=0, keepdims=True)
                dsa[h] += -jnp.exp(sink_ref[h] - lse) * delta
                ps.append(p.astype(BF16))
                dss.append((p * (dps[h] - delta)).astype(BF16))
            for h in kvhs:
                sl = slice(h * LANES, (h + 1) * LANES)
                dv = _dot(ps[h], dos[h])
                dk = _dot(dss[h], qs[h])
                _unstack_heads(dq_ref, h, _dot_tn(dss[h], kks[h]), tq)

                @pl.when(i > 0)
                def _():
                    dk_ref[0:tq - hw, sl] = ck[0:tq - hw, sl]
                    dk_ref[tq - hw:tq, sl] = ck[tq - hw:tq, sl] + dk[0:hw]
                    dv_ref[0:tq - hw, sl] = cv[0:tq - hw, sl]
                    dv_ref[tq - hw:tq, sl] = cv[tq - hw:tq, sl] + dv[0:hw]

                ck[:, sl] = dk[hw:hw + tq]
                cv[:, sl] = dv[hw:hw + tq]

        @pl.when(i == nq)
        def _():
            dk_ref[...] = ck[...]
            dv_ref[...] = cv[...]
            dsink_ref[...] = jnp.zeros(dsink_ref.shape, F32)
            for h in kvhs:
                for g in range(SWA_GROUP):
                    tot = jnp.sum(dsa[h, :, g * tq:(g + 1) * tq], axis=1, keepdims=True)
                    dsink_ref[h, g:g + 1, :] = jnp.zeros((1, LANES), F32) + tot

    kv_out = pl.BlockSpec((tq, kvw), lambda i: (jnp.maximum(i - 1, 0), 0))
    return pl.pallas_call(
        body, name="swa_bwd", grid=(nq + 1,),
        in_specs=[qs_, cur, prev, cur, prev, qs_, qs_, rows, bs, sk],
        out_specs=[qs_, kv_out, kv_out, pl.BlockSpec((A_KV_HEADS, 8, LANES), lambda i: (0, 0, 0))],
        out_shape=[_sds((t_rows, HEADS * LANES), F32), _sds((t_rows, kvw), F32), _sds((t_rows, kvw), F32),
                   _sds((A_KV_HEADS, 8, LANES), F32)],
        scratch_shapes=[pltpu.VMEM((tq, kvw), F32), pltpu.VMEM((tq, kvw), F32),
                        pltpu.VMEM((A_KV_HEADS, 1, SWA_GROUP * tq), F32)],
        compiler_params=pltpu.CompilerParams(dimension_semantics=("arbitrary",), vmem_limit_bytes=VMEM_LIMIT),
    )(q, k, k, v, v, o, do, lse, bias, sink_rows)


def _fwd_mix(x, ya, yb, gate, wba, wbb, wout, g2, g3, tm):
    t_rows = x.shape[0]

    def body(x_ref, ya_ref, yb_ref, gate_ref, wba_ref, wbb_ref, wout_ref, g2_ref, g3_ref,
             pa_ref, pb_ref, mixed_ref, o_ref, x1_ref, h2_ref, yac_ref, ybc_ref):
        yac = _fold_slots(ya_ref[...].astype(F32)).astype(BF16)
        ybc = _fold_slots(yb_ref[...].astype(F32)).astype(BF16)
        yac_ref[...] = yac
        ybc_ref[...] = ybc
        pa = _dot(yac, wba_ref[...])
        pb = _dot(ybc, wbb_ref[...])
        pa_ref[...] = pa.astype(BF16)
        pb_ref[...] = pb.astype(BF16)
        mixed = (gate_ref[:, 0:D_MODEL].astype(F32) * pa
                 + gate_ref[:, D_MODEL:2 * D_MODEL].astype(F32) * pb).astype(BF16)
        mixed_ref[...] = mixed
        o = _dot(mixed, wout_ref[...])
        o_ref[...] = o
        on, _ = _rms_stats(o)
        x1 = x_ref[...] + on * g2_ref[...]
        x1_ref[...] = x1
        x1n, _ = _rms_stats(x1)
        h2_ref[...] = (x1n * g3_ref[...]).astype(BF16)

    def o_(dt):
        return (_sds((t_rows, D_MODEL), dt), _row(tm, D_MODEL))

    ins = [(x, _row(tm, D_MODEL)), (ya, _row(tm, 1024)), (yb, _row(tm, 1024)), (gate, _row(tm, 2048)),
           (wba, _resident(wba.shape)), (wbb, _resident(wbb.shape)), (wout, _resident(wout.shape)),
           (g2, _full(g2.shape)), (g3, _full(g3.shape))]
    half = (_sds((t_rows, D_MODEL // 2), BF16), _row(tm, D_MODEL // 2))
    return _rows_call("fwd_mix", body, t_rows, tm, ins,
                      [o_(BF16), o_(BF16), o_(BF16), o_(F32), o_(F32), o_(BF16), half, half])


CONV_CHUNK = 1408


def _fwd_up(h2, wup, convw8, convb, tm):
    t_rows = h2.shape[0]
    cdim = 2 * D_FF

    def body(h2_ref, wup_ref, cw_ref, cb_ref, up_ref, a_ref, u_ref, carry):
        i = pl.program_id(0)

        @pl.when(i == 0)
        def _():
            carry[...] = jnp.zeros(carry.shape, F32)

        hb = h2_ref[...]
        ups = [_dot(hb, wup_ref[s]) for s in range(cdim // CONV_CHUNK)]

        def conv(c0):
            sl = slice(c0, c0 + CONV_CHUNK)
            up = ups[c0 // CONV_CHUNK]
            up_ref[:, sl] = up
            xm1, xm2 = _conv_taps(up, carry[6:7, sl], carry[7:8, sl])
            u = cw_ref[0:1, sl] * xm2 + cw_ref[1:2, sl] * xm1 + cw_ref[2:3, sl] * up + cb_ref[:, sl]
            u_ref[:, sl] = u.astype(BF16)
            carry[:, sl] = up[tm - 8:tm, :]
            return u

        for c0 in range(0, D_FF, CONV_CHUNK):
            ug = conv(c0)
            uv = conv(D_FF + c0)
            gel, _ = _gelu_and_grad(ug)
            a_ref[:, c0:c0 + CONV_CHUNK] = (gel * uv).astype(BF16)

    ins = [(h2, _row(tm, D_MODEL)), (wup, _resident(wup.shape)), (convw8, _full(convw8.shape)),
           (convb, _full(convb.shape))]
    outs = [(_sds((t_rows, cdim), F32), _row(tm, cdim)), (_sds((t_rows, D_FF), BF16), _row(tm, D_FF)),
            (_sds((t_rows, cdim), BF16), _row(tm, cdim))]
    return _rows_call("fwd_up", body, t_rows, tm, ins, outs, scratch=[pltpu.VMEM((8, cdim), F32)])


def _fwd_out(a, wdown, x1, g4, p, wple, g5, wpg, tgt, tm):
    t_rows = a.shape[0]

    def body(a_ref, wdown_ref, x1_ref, g4_ref, p_ref, wple_ref, g5_ref, wpg_ref, tgt_ref,
             ff_ref, x2_ref, e_ref, n5_ref, sg_ref, dx3_ref, loss_ref):
        i = pl.program_id(0)
        ff = _dot(a_ref[...], wdown_ref[...])
        e = _dot(p_ref[...].astype(BF16), wple_ref[...])
        ff_ref[...] = ff
        ffn, _ = _rms_stats(ff)
        x2 = x1_ref[...] + ffn * g4_ref[...]
        x2_ref[...] = x2
        e_ref[...] = e.astype(BF16)
        x2n, _ = _rms_stats(x2)
        n5 = (x2n * g5_ref[...]).astype(BF16)
        n5_ref[...] = n5
        sg = _sigmoid(_dot(n5, wpg_ref[...]))
        sg_ref[...] = sg.astype(BF16)
        d = x2 + sg * e - tgt_ref[...]
        dx3_ref[...] = d * (1.0 / D_MODEL)

        @pl.when(i == 0)
        def _():
            loss_ref[...] = jnp.zeros((1, 1), F32)

        loss_ref[...] += 0.5 * jnp.sum(jnp.sum(d * d, axis=1, keepdims=True), axis=0, keepdims=True) * (1.0 / D_MODEL)

    def o_(dt):
        return (_sds((t_rows, D_MODEL), dt), _row(tm, D_MODEL))

    ins = [(a, _row(tm, D_FF)), (wdown, _resident(wdown.shape)), (x1, _row(tm, D_MODEL)), (g4, _full(g4.shape)),
           (p, _row(tm, PLE_DIM)), (wple, _full(wple.shape)), (g5, _full(g5.shape)), (wpg, _resident(wpg.shape)),
           (tgt, _row(tm, D_MODEL))]
    outs = [o_(F32), o_(F32), o_(BF16), o_(BF16), o_(BF16), o_(F32), (_sds((1, 1), F32), _full((1, 1)))]
    return _rows_call("fwd_out", body, t_rows, tm, ins, outs)


def _bwd_out(dx3, e, sg, x2, ff, g5, g4, wpg, wdown, up, u, tm):
    t_rows = dx3.shape[0]
    cdim = 2 * D_FF
    hb = tm // 8

    def body(dx3_ref, e_ref, sg_ref, x2_ref, ff_ref, g5_ref, g4_ref, wpg_ref, wdown_ref, up_ref, halo_ref, u_ref,
             dpre_ref, de_ref, dx2_ref, dff_ref, du_ref, dg5_ref, dg4_ref, dcb_ref, dcw_ref):
        i = pl.program_id(0)

        @pl.when(i == 0)
        def _():
            dg5_ref[...] = jnp.zeros(dg5_ref.shape, F32)
            dg4_ref[...] = jnp.zeros(dg4_ref.shape, F32)
            dcb_ref[...] = jnp.zeros(dcb_ref.shape, F32)
            dcw_ref[...] = jnp.zeros(dcw_ref.shape, F32)

        dx3 = dx3_ref[...]
        sg = sg_ref[...].astype(F32)
        dpre = (dx3 * e_ref[...].astype(F32) * sg * (1.0 - sg)).astype(BF16)
        dpre_ref[...] = dpre
        de_ref[...] = (dx3 * sg).astype(BF16)
        dn5 = _dot_nt(dpre, wpg_ref[...])
        x2n, r5 = _rms_stats(x2_ref[...])
        d2, dg5 = _rms_bwd(dn5, x2n, r5, g5_ref[...])
        dx2 = dx3 + d2
        dx2_ref[...] = dx2
        dg5_ref[...] += dg5
        ffn, r4 = _rms_stats(ff_ref[...])
        dff, dg4 = _rms_bwd(dx2, ffn, r4, g4_ref[...])
        dg4_ref[...] += dg4
        dffb = dff.astype(BF16)
        dff_ref[...] = dffb
        keep = jnp.where(i > 0, 1.0, 0.0)

        def conv(c0):
            sl = slice(c0, c0 + CONV_CHUNK)
            up = up_ref[:, sl]
            xm1, xm2 = _conv_taps(up, halo_ref[6:7, sl] * keep, halo_ref[7:8, sl] * keep)
            return u_ref[:, sl].astype(F32), up, xm1, xm2

        def grads(c0, du, up, xm1, xm2):
            sl = slice(c0, c0 + CONV_CHUNK)
            du_ref[:, sl] = du.astype(BF16)
            dcb_ref[:, sl] += jnp.sum(du, axis=0, keepdims=True)
            dcw_ref[0:1, sl] += jnp.sum(du * xm2, axis=0, keepdims=True)
            dcw_ref[1:2, sl] += jnp.sum(du * xm1, axis=0, keepdims=True)
            dcw_ref[2:3, sl] += jnp.sum(du * up, axis=0, keepdims=True)

        for c0 in range(0, D_FF, CONV_CHUNK):
            da = _dot_nt(dffb, wdown_ref[c0:c0 + CONV_CHUNK, :])
            ug, *rg = conv(c0)
            uv, *rv = conv(D_FF + c0)
            gel, dgel = _gelu_and_grad(ug)
            grads(c0, da * uv * dgel, *rg)
            grads(D_FF + c0, da * gel, *rv)

    def o_(n, dt):
        return (_sds((t_rows, n), dt), _row(tm, n))

    def acc(r, n):
        return (_sds((r, n), F32), _full((r, n)))

    halo = pl.BlockSpec((8, cdim), lambda i: (jnp.maximum(i * hb - 1, 0), 0))
    ins = [(dx3, _row(tm, D_MODEL)), (e, _row(tm, D_MODEL)), (sg, _row(tm, D_MODEL)), (x2, _row(tm, D_MODEL)),
           (ff, _row(tm, D_MODEL)), (g5, _full(g5.shape)), (g4, _full(g4.shape)), (wpg, _resident(wpg.shape)),
           (wdown, _resident(wdown.shape)), (up, _row(tm, cdim)), (up, halo), (u, _row(tm, cdim))]
    outs = [o_(D_MODEL, BF16), o_(D_MODEL, BF16), o_(D_MODEL, F32), o_(D_MODEL, BF16), o_(cdim, BF16),
            acc(1, D_MODEL), acc(1, D_MODEL), acc(1, cdim), acc(8, cdim)]
    return _rows_call("bwd_out", body, t_rows, tm, ins, outs)


def _bwd_mid(du, convw8, wup, dx2, x1, g3, o, g2, wout, gate, pa, pb, wba, wbb, yb, tm):
    t_rows = du.shape[0]
    cdim = 2 * D_FF
    halo_rows = 16
    hb = tm // halo_rows
    last_blk = t_rows // halo_rows - 1
    n_tiles = t_rows // tm

    def body(du_ref, halo_ref, cw_ref, wup_ref, dx2_ref, x1_ref, g3_ref, o_ref, g2_ref, wout_ref, gate_ref, pa_ref,
             pb_ref, wba_ref, wbb_ref, yb_ref,
             dup_ref, dx1_ref, do_ref, dpa_ref, dpb_ref, dgt_ref, dya_ref, dyb_ref, dl_ref, dg3_ref, dg2_ref, dbg_ref):
        i = pl.program_id(0)

        @pl.when(i == 0)
        def _():
            dg3_ref[...] = jnp.zeros(dg3_ref.shape, F32)
            dg2_ref[...] = jnp.zeros(dg2_ref.shape, F32)
            dbg_ref[...] = jnp.zeros(dbg_ref.shape, F32)

        keep = jnp.where(i < n_tiles - 1, 1.0, 0.0)
        dh2 = jnp.zeros((tm, D_MODEL), F32)
        dups = []
        for c0 in range(0, cdim, CONV_CHUNK):
            sl = slice(c0, c0 + CONV_CHUNK)
            du = du_ref[:, sl].astype(F32)
            nxt = halo_ref[:, sl].astype(F32)
            xp1, xp2 = _conv_taps_next(du, nxt[0:1] * keep, nxt[1:2] * keep)
            dups.append((cw_ref[2:3, sl] * du + cw_ref[1:2, sl] * xp1 + cw_ref[0:1, sl] * xp2).astype(BF16))
            dup_ref[:, sl] = dups[-1]
            if len(dups) > 1:
                dh2 = dh2 + _dot_nt(dups[-2], wup_ref[len(dups) - 2])
        dh2 = dh2 + _dot_nt(dups[-1], wup_ref[len(dups) - 1])
        x1n, r3 = _rms_stats(x1_ref[...])
        d1, dg3 = _rms_bwd(dh2, x1n, r3, g3_ref[...])
        dx1 = dx2_ref[...] + d1
        dx1_ref[...] = dx1
        dg3_ref[...] += dg3
        on, r2 = _rms_stats(o_ref[...])
        do, dg2 = _rms_bwd(dx1, on, r2, g2_ref[...])
        dg2_ref[...] += dg2
        dob = do.astype(BF16)
        do_ref[...] = dob
        dmixed = _dot_nt(dob, wout_ref[...])
        ga = gate_ref[:, 0:D_MODEL].astype(F32)
        gb = gate_ref[:, D_MODEL:2 * D_MODEL].astype(F32)
        dpa = (dmixed * ga).astype(BF16)
        dpb = (dmixed * gb).astype(BF16)
        dpa_ref[...] = dpa
        dpb_ref[...] = dpb
        dga = dmixed * pa_ref[...].astype(F32) * ga * (1.0 - ga)
        dgb = dmixed * pb_ref[...].astype(F32) * gb * (1.0 - gb)
        dgt_ref[:, 0:D_MODEL] = dga.astype(BF16)
        dgt_ref[:, D_MODEL:2 * D_MODEL] = dgb.astype(BF16)
        dbg_ref[:, 0:D_MODEL] += jnp.sum(dga, axis=0, keepdims=True)
        dbg_ref[:, D_MODEL:2 * D_MODEL] += jnp.sum(dgb, axis=0, keepdims=True)
        dya_ref[...] = _spread_slots(_dot_nt(dpa, wba_ref[...])).astype(BF16)
        dyb = _dot_nt(dpb, wbb_ref[...]).astype(BF16)
        dyb_ref[...] = _spread_slots(dyb.astype(F32)).astype(BF16)
        prod = yb_ref[...].astype(F32) * dyb.astype(F32)
        width = HEADS * V_DIM
        lane_head = lax.broadcasted_iota(jnp.int32, (HEADS, width), 1) // V_DIM
        sel = (lane_head == lax.broadcasted_iota(jnp.int32, (HEADS, width), 0)).astype(BF16)
        hi = prod.astype(BF16)
        lo = (prod - hi.astype(F32)).astype(BF16)
        dl_ref[...] = _dot_nt(sel, hi) + _dot_nt(sel, lo)

    def o_(n, dt):
        return (_sds((t_rows, n), dt), _row(tm, n))

    def acc(r, n):
        return (_sds((r, n), F32), _full((r, n)))

    halo = pl.BlockSpec((halo_rows, cdim), lambda i: (jnp.minimum((i + 1) * hb, last_blk), 0))
    ins = [(du, _row(tm, cdim)), (du, halo), (convw8, _full(convw8.shape)), (wup, _resident(wup.shape)),
           (dx2, _row(tm, D_MODEL)), (x1, _row(tm, D_MODEL)), (g3, _full(g3.shape)), (o, _row(tm, D_MODEL)),
           (g2, _full(g2.shape)), (wout, _resident(wout.shape)), (gate, _row(tm, 2048)), (pa, _row(tm, D_MODEL)),
           (pb, _row(tm, D_MODEL)), (wba, _resident(wba.shape)), (wbb, _resident(wbb.shape)),
           (yb, _row(tm, D_MODEL // 2))]
    outs = [o_(cdim, BF16), o_(D_MODEL, F32), o_(D_MODEL, BF16), o_(D_MODEL, BF16), o_(D_MODEL, BF16),
            o_(2048, BF16), o_(1024, BF16), o_(1024, BF16),
            (_sds((HEADS, t_rows), F32), pl.BlockSpec((HEADS, tm), lambda i: (0, i))),
            acc(1, D_MODEL), acc(1, D_MODEL), acc(1, 2048)]
    return _rows_call("bwd_mid", body, t_rows, tm, ins, outs)


def _bwd_in(dqs, dks, dvs, dqm, dkm, dvm, tabs, consts, cq, ckv, gq, gkv, wuq, wk, wv, dgates, win, x, g1, dx1, tm):
    t_rows = x.shape[0]

    def body(dqs_ref, dks_ref, dvs_ref, dqm_ref, dkm_ref, dvm_ref, ca, sa1, sa2, cb, sb1, sb2, c_ref, cq_ref,
             ckv_ref, gq_ref, gkv_ref, wuq_ref, wk_ref, wv_ref, dgt_ref, win_ref, x_ref, g1_ref, dx1_ref,
             dz_ref, dqb_ref, dx_ref, dgq_ref, dgkv_ref, dg1_ref):
        i = pl.program_id(0)

        @pl.when(i == 0)
        def _():
            dgq_ref[...] = jnp.zeros(dgq_ref.shape, F32)
            dgkv_ref[...] = jnp.zeros(dgkv_ref.shape, F32)
            dg1_ref[...] = jnp.zeros(dg1_ref.shape, F32)

        ta = (ca[...], sa1[...], sa2[...])
        tb = (cb[...], sb1[...], sb2[...])

        def piece(lo, hi, val):
            dz_ref[:, lo:hi] = val
            return _dot_nt(val, win_ref[:, lo:hi])

        dh1 = piece(Z_GATE, ZW, dgt_ref[...])
        dkm = dkm_ref[...]
        dckvn = _dot_nt(dkm.astype(BF16), wk_ref[...]) + _dot_nt(dvm_ref[...].astype(BF16), wv_ref[...])
        dh1 = dh1 + piece(Z_VA, Z_CQ, _fold_slots(dvs_ref[...]).astype(BF16))
        dqm = jnp.concatenate([dqm_ref[h] for h in range(HEADS)], axis=1)
        dqb = _rope_t(dqm * SCALE_B, *tb, ROPE_DIM // 2).astype(BF16)
        dqb_ref[...] = dqb
        dcqn = _dot_nt(dqb, wuq_ref[...])
        dqa = _rope_t(_fold_slots(dqs_ref[...]) * SCALE_A, *ta, A_HEAD_DIM // 2)
        dh1 = dh1 + piece(Z_QA, Z_KA, dqa.astype(BF16))
        dh1 = dh1 + piece(Z_KA, Z_VA, _rope_t(_fold_slots(dks_ref[...]), *ta, A_HEAD_DIM // 2).astype(BF16))
        ckvn, rkv = _rms_stats(ckv_ref[...])
        dckv, dgkv = _rms_bwd(dckvn, ckvn, rkv, gkv_ref[...])
        dgkv_ref[...] += dgkv
        dh1 = dh1 + piece(Z_CKV, Z_KR, dckv.astype(BF16))
        dslot = dkm[:, 0:LANES]
        for h in range(1, HEADS):
            dslot = dslot + dkm[:, h * LANES:(h + 1) * LANES]
        dh1 = dh1 + piece(Z_KR, Z_GATE, _rope_t(dslot * c_ref[10:11, :], *tb, ROPE_DIM // 2).astype(BF16))
        cqn, rq = _rms_stats(cq_ref[...])
        dcq, dgq = _rms_bwd(dcqn, cqn, rq, gq_ref[...])
        dgq_ref[...] += dgq
        dh1 = dh1 + piece(Z_CQ, Z_CKV, dcq.astype(BF16))
        xn, r1 = _rms_stats(x_ref[...])
        d0, dg1 = _rms_bwd(dh1, xn, r1, g1_ref[...])
        dg1_ref[...] += dg1
        dx_ref[...] = dx1_ref[...] + d0

    def acc(n):
        return (_sds((1, n), F32), _full((1, n)))

    ins = [(dqs, _row(tm, 1024)), (dks, _row(tm, 256)), (dvs, _row(tm, 256)), (dqm, _heads(tm, HEADS)),
           (dkm, _row(tm, 1024)), (dvm, _row(tm, 1024))] + [(t, _row(tm, LANES)) for t in tabs] + [
           (consts, _full(consts.shape)), (cq, _row(tm, 256)), (ckv, _row(tm, 128)), (gq, _full(gq.shape)),
           (gkv, _full(gkv.shape)), (wuq, _full(wuq.shape)), (wk, _full(wk.shape)), (wv, _full(wv.shape)),
           (dgates, _row(tm, 2048)), (win, _resident(win.shape)), (x, _row(tm, D_MODEL)), (g1, _full(g1.shape)),
           (dx1, _row(tm, D_MODEL))]
    outs = [(_sds((t_rows, ZW), BF16), _row(tm, ZW)), (_sds((t_rows, 1024), BF16), _row(tm, 1024)),
            (_sds((t_rows, D_MODEL), F32), _row(tm, D_MODEL)), acc(256), acc(128), acc(D_MODEL)]
    return _rows_call("bwd_in", body, t_rows, tm, ins, outs)


def _pick_cols(n):
    best = LANES
    for d in range(LANES, min(n, 1664) + 1, LANES):
        if n % d == 0:
            best = d
    return best


def _mm_tn(name, a, b, column_shards=1, after=None):
    t_rows, m = a.shape
    n = b.shape[1]
    bk = min(2048, t_rows)
    bm, bn = _pick_cols(m), _pick_cols(n // column_shards)
    per_shard = n // column_shards // bn
    extra = () if after is None else (after,)

    def body(a_ref, b_ref, *rest):
        o_ref = rest[-1]

        @pl.when(pl.program_id(2) == 0)
        def _():
            o_ref[...] = jnp.zeros((bm, bn), F32)

        o_ref[...] += _dot_tn(a_ref[...].astype(BF16), b_ref[...].astype(BF16))

    return pl.pallas_call(
        body, name=name, grid=(m // bm, n // bn, t_rows // bk),
        in_specs=[pl.BlockSpec((bk, bm), lambda i, j, k: (k, i)), pl.BlockSpec((bk, bn), lambda i, j, k: (k, j))]
        + [pl.BlockSpec((8, LANES), lambda i, j, k: (0, 0))] * len(extra),
        out_specs=(pl.BlockSpec((bm, bn), lambda i, j, k: (i, j)) if column_shards == 1 else
                   pl.BlockSpec((None, bm, bn), lambda i, j, k: (j // per_shard, i, j % per_shard))),
        out_shape=_sds((m, n) if column_shards == 1 else (column_shards, m, n // column_shards), F32),
        compiler_params=pltpu.CompilerParams(dimension_semantics=("arbitrary",) * 3, vmem_limit_bytes=VMEM_LIMIT),
    )(a, b, *extra)


PACK_ROWS = 512


ADD_TILE_ELEMS = 1 << 17


def _add_rows(rows, cols):
    best = 16
    for d in range(16, rows + 1, 16):
        if rows % d == 0 and d * cols <= ADD_TILE_ELEMS:
            best = d
    assert rows % best == 0
    return best


def _add_pair(name, g, recv, half):
    _, _, rows, cols = g.shape
    t = _add_rows(rows, cols)

    def body(h_ref, g_ref, r_ref, o_ref):
        o_ref[...] = (g_ref[:, 0] + r_ref[...]).astype(BF16)

    spec = pl.BlockSpec((4, t, cols), lambda i, h: (0, i, 0))
    grid_spec = pltpu.PrefetchScalarGridSpec(
        num_scalar_prefetch=1, grid=(rows // t,),
        in_specs=[pl.BlockSpec((4, 1, t, cols), lambda i, h: (0, h[0], i, 0)), spec], out_specs=spec)
    return pl.pallas_call(body, name=name, grid_spec=grid_spec,
                          out_shape=_sds(recv.shape, BF16))(jnp.reshape(half, (1,)).astype(jnp.int32), g, recv)


def _add_chips(name, parts):
    _, rows, cols = parts.shape
    t = _add_rows(rows, cols)

    def body(p_ref, o_ref):
        acc = p_ref[0].astype(F32)
        for j in range(1, 4):
            acc = acc + p_ref[j].astype(F32)
        o_ref[...] = acc

    return pl.pallas_call(body, name=name, grid=(rows // t,),
                          in_specs=[pl.BlockSpec((4, t, cols), lambda i: (0, i, 0))],
                          out_specs=pl.BlockSpec((t, cols), lambda i: (i, 0)),
                          out_shape=_sds((rows, cols), F32))(parts)


def _add_devices(parts):
    n, rows, _ = parts.shape

    def body(p_ref, o_ref):
        acc = p_ref[0]
        for j in range(1, n):
            acc = acc + p_ref[j]
        o_ref[...] = acc

    return pl.pallas_call(body, name="small_add", grid=(1,),
                          in_specs=[pl.BlockSpec((n, rows, LANES), lambda i: (0, 0, 0))],
                          out_specs=pl.BlockSpec((rows, LANES), lambda i: (0, 0)),
                          out_shape=_sds((rows, LANES), F32))(parts)


def _adam_rows(k, n):
    target = max(8, (1 << 20) // (4 * n))
    if k <= target:
        return k
    best = None
    for d in range(8, target + 1, 8):
        if k % d == 0:
            best = d
    return best if best is not None else k


def _adam_update(w, g, m, v):
    m_ = ADAM_B1 * m + (1.0 - ADAM_B1) * g
    v_ = ADAM_B2 * v + (1.0 - ADAM_B2) * (g * g)
    delta = -ADAM_LR * ((m_ / (1.0 - ADAM_B1 ** ADAM_STEP)) / (jnp.sqrt(v_ / (1.0 - ADAM_B2 ** ADAM_STEP)) + ADAM_EPS)
                        + ADAM_WD * w)
    return delta, m_, v_


def _adamw_many(name, ws, gs, ms, vs):
    n = len(ws)

    def body(*refs):
        for i in range(n):
            w_ref, g_ref, m_ref, v_ref = (refs[k * n + i] for k in range(4))
            d_ref, mo_ref, vo_ref = (refs[(4 + k) * n + i] for k in range(3))
            d_ref[...], mo_ref[...], vo_ref[...] = _adam_update(w_ref[...], g_ref[...], m_ref[...], v_ref[...])

    specs = [pl.BlockSpec(w.shape, lambda i: (0, 0)) for w in ws]
    out = pl.pallas_call(body, name=name, grid=(1,), in_specs=specs * 4, out_specs=specs * 3,
                         out_shape=[_sds(w.shape, F32) for w in ws] * 3)(*ws, *gs, *ms, *vs)
    return [(gs[i], out[i], out[n + i], out[2 * n + i]) for i in range(n)]


def _adamw_halves(name, w, mine, theirs, m, v, half):
    k, n = w.shape
    bk = _adam_rows(k // 2, n)
    nb = k // 2 // bk

    def body(h_ref, w_ref, mine_ref, theirs_ref, m_ref, v_ref, g_ref, d_ref, mo_ref, vo_ref):
        g = jnp.where(pl.program_id(0) == h_ref[0], mine_ref[...], theirs_ref[...])
        g_ref[...] = g
        d_ref[...], mo_ref[...], vo_ref[...] = _adam_update(w_ref[...], g, m_ref[...], v_ref[...])

    full = pl.BlockSpec((bk, n), lambda h, i, c: (h * nb + i, 0))
    part = pl.BlockSpec((bk, n), lambda h, i, c: (i, 0))
    grid_spec = pltpu.PrefetchScalarGridSpec(num_scalar_prefetch=1, grid=(2, nb),
                                             in_specs=[full, part, part, full, full], out_specs=[full] * 4)
    return tuple(pl.pallas_call(
        body, name=name, grid_spec=grid_spec, out_shape=[_sds((k, n), F32)] * 4,
        compiler_params=pltpu.CompilerParams(vmem_limit_bytes=VMEM_LIMIT),
    )(jnp.reshape(half, (1,)).astype(jnp.int32), w, mine, theirs, m, v))


_HBM = pl.BlockSpec(memory_space=pltpu.HBM)


def _me():
    return lax.axis_index("x"), lax.axis_index("y"), lax.axis_index("c")


def _other_chips(x, y):
    return [(1 - x, y), (x, 1 - y), (1 - x, 1 - y)]


def _pass_to_sibling(zones):
    n = len(zones)

    def body(*refs):
        in_refs, out_refs = refs[:n], refs[n:2 * n]
        send_sems, recv_sems = refs[2 * n:]
        x, y, c = _me()
        sent = []
        for a, (in_ref, out_ref) in enumerate(zip(in_refs, out_refs)):
            for j, (cx, cy) in enumerate(_other_chips(x, y)):
                mine, theirs = (2 * cx + cy, c), (2 * cx + cy, 1 - c)
                sems = dict(send_sem=send_sems.at[3 * a + j], recv_sem=recv_sems.at[3 * a + j],
                            device_id=(x, y, 1 - c), device_id_type=MESH)
                sent.append(tuple(pltpu.make_async_remote_copy(src_ref=in_ref.at[part], dst_ref=out_ref.at[part], **sems)
                                  for part in (mine, theirs)))
        for send, _ in sent:
            send.start()
        for _, recv in sent:
            recv.wait_recv()
        for send, _ in sent:
            send.wait_send()

    return pl.pallas_call(
        body, name="pass_to_sibling", out_shape=[_sds(z.shape, z.dtype) for z in zones],
        in_specs=[_HBM] * n, out_specs=[_HBM] * n, input_output_aliases={i: i for i in range(n)},
        scratch_shapes=[pltpu.SemaphoreType.DMA((3 * n,)), pltpu.SemaphoreType.DMA((3 * n,))],
    )(*zones)


def _swap_sibling(name, vs, other_half=False):
    n = len(vs)

    def body(*refs):
        v_refs, out_refs = refs[:n], refs[n:2 * n]
        send_sems, recv_sems = refs[2 * n:]
        x, y, c = _me()
        cps = [pltpu.make_async_remote_copy(src_ref=v_ref.at[:, 1 - c] if other_half else v_ref, dst_ref=out_ref,
                                            send_sem=send_sems.at[a], recv_sem=recv_sems.at[a],
                                            device_id=(x, y, 1 - c), device_id_type=MESH)
               for a, (v_ref, out_ref) in enumerate(zip(v_refs, out_refs))]
        for cp in cps:
            cp.start()
        for cp in cps:
            cp.wait()

    def landing(v):
        return _sds((v.shape[0],) + v.shape[2:] if other_half else v.shape, v.dtype)

    return pl.pallas_call(
        body, name=name, out_shape=[landing(v) for v in vs], in_specs=[_HBM] * n, out_specs=[_HBM] * n,
        scratch_shapes=[pltpu.SemaphoreType.DMA((n,)), pltpu.SemaphoreType.DMA((n,))],
    )(*vs)


_SEM = pl.BlockSpec(memory_space=pltpu.SEMAPHORE)
_EFFECT = pltpu.SideEffectType.DATAFLOW_SIDE_EFFECTING
WHOLE = "whole"
PIECE = "piece"
SIBLING_HALF = "sibling"
MY_HALF = "half"
EVERYONE = "everyone"
_COPIES = {WHOLE: 3, PIECE: 3, MY_HALF: 3, SIBLING_HALF: 1, EVERYONE: 7}


def _landing_shape(v, mode):
    return {WHOLE: (4,) + v.shape, MY_HALF: (4,) + v.shape, PIECE: v.shape, EVERYONE: (8,) + v.shape,
            SIBLING_HALF: (v.shape[0],) + v.shape[2:]}[mode]


def _chip_copies(v_ref, land_ref, send_sems, recv_sems, mode, sem0=0):
    x, y, c = _me()
    if mode == SIBLING_HALF:
        cp = pltpu.make_async_remote_copy(src_ref=v_ref.at[:, 1 - c], dst_ref=land_ref, send_sem=send_sems.at[sem0],
                                          recv_sem=recv_sems.at[sem0], device_id=(x, y, 1 - c), device_id_type=MESH)
        return [(cp, cp)]
    if mode == EVERYONE:
        out = []
        for f in range(1, 8):
            px, py, pc = (1 - x if f & 4 else x), (1 - y if f & 2 else y), (1 - c if f & 1 else c)
            sems = dict(send_sem=send_sems.at[sem0 + f - 1], recv_sem=recv_sems.at[sem0 + f - 1],
                        device_id=(px, py, pc), device_id_type=MESH)
            out.append((pltpu.make_async_remote_copy(src_ref=v_ref, dst_ref=land_ref.at[4 * x + 2 * y + c], **sems),
                        pltpu.make_async_remote_copy(src_ref=v_ref, dst_ref=land_ref.at[4 * px + 2 * py + pc], **sems)))
        return out
    k = 2 * x + y
    out = []
    for j, (cx, cy) in enumerate(_other_chips(x, y)):
        if mode == MY_HALF:
            src, mine, theirs = v_ref.at[c], land_ref.at[k, c], land_ref.at[2 * cx + cy, c]
        else:
            src = v_ref.at[2 * cx + cy] if mode == PIECE else v_ref
            mine, theirs = land_ref.at[k], land_ref.at[2 * cx + cy]
        sems = dict(send_sem=send_sems.at[sem0 + j], recv_sem=recv_sems.at[sem0 + j], device_id=(cx, cy, c),
                    device_id_type=MESH)
        send = pltpu.make_async_remote_copy(src_ref=src, dst_ref=mine, **sems)
        recv = pltpu.make_async_remote_copy(src_ref=src, dst_ref=theirs, **sems)
        out.append((send, recv))
    return out


def _chips_start(name, vs, mode, after=None):
    n = len(vs)
    lands = [_landing_shape(v, mode) for v in vs]

    def body(*refs):
        v_refs, land_refs = refs[:n], refs[n:2 * n]
        send_sems, recv_sems = refs[-2 * n - 3], refs[-2 * n - 2]
        token = refs[-1]
        for a in range(n):
            for send, _ in _chip_copies(v_refs[a], land_refs[a], send_sems, recv_sems, mode, _COPIES[mode] * a):
                send.start()
        token[...] = jnp.zeros_like(token)

    extra = () if after is None else (after,)
    hbm = [pltpu.with_memory_space_constraint(v, pltpu.HBM) for v in vs]
    zones = [pltpu.with_memory_space_constraint(lax.empty(s, v.dtype), pltpu.HBM) for s, v in zip(lands, vs)]
    out = pl.pallas_call(
        body, name=name,
        out_shape=(pltpu.SemaphoreType.DMA((_COPIES[mode] * n,)), pltpu.SemaphoreType.DMA((_COPIES[mode] * n,)),
                   *[pltpu.HBM(v.shape, v.dtype) for v in vs], *[pltpu.HBM(s, v.dtype) for s, v in zip(lands, vs)],
                   _sds((8, LANES), F32)),
        in_specs=(_HBM,) * (2 * n) + (pl.BlockSpec(memory_space=pl.ANY),) * len(extra),
        out_specs=(_SEM, _SEM) + (_HBM,) * (2 * n) + (pl.BlockSpec(memory_space=pltpu.VMEM),),
        input_output_aliases={i: 2 + i for i in range(2 * n)},
        compiler_params=pltpu.CompilerParams(has_side_effects=_EFFECT),
    )(*hbm, *zones, *extra)
    return out[0], out[1], list(out[2:2 + n]), list(out[2 + n:2 + 2 * n]), out[-1]


def _chips_wait(name, send_sems, recv_sems, v_thru, land_thru, mode, after):
    n = len(v_thru)

    def body(*refs):
        v_refs, land_refs = refs[:n], refs[n:2 * n]
        send_sems, recv_sems = refs[2 * n], refs[2 * n + 1]
        for a in range(n):
            for send, recv in _chip_copies(v_refs[a], land_refs[a], send_sems, recv_sems, mode, _COPIES[mode] * a):
                send.wait_send()
                recv.wait_recv()

    out = pl.pallas_call(
        body, name=name,
        out_shape=tuple(pltpu.HBM(a.shape, a.dtype) for a in list(v_thru) + list(land_thru)),
        in_specs=(_HBM,) * (2 * n) + (_SEM, _SEM, pl.BlockSpec(memory_space=pl.ANY)), out_specs=(_HBM,) * (2 * n),
        input_output_aliases={i: i for i in range(2 * n)},
        compiler_params=pltpu.CompilerParams(has_side_effects=_EFFECT),
    )(*v_thru, *land_thru, send_sems, recv_sems, after)
    return list(out[:n]), list(out[n:])


_BIG = (("w_in", (1024, 3232), 1), ("w_uq", (256, 768), 1), ("w_ukv", (128, 1024), 1), ("w_branch_a", (512, 1024), 1),
        ("w_branch_b", (512, 1024), 1), ("w_out", (1024, 1024), 0), ("w_up", (1024, 5632), 1),
        ("w_down", (2816, 1024), 0), ("w_ple_gate", (1024, 1024), 0), ("w_ple", (256, 1024), 1))


def _shard_shape(shape, axis):
    return (shape[0] // 4, shape[1]) if axis == 0 else (shape[0], shape[1] // 4)


def _half_rows(shape, axis):
    k, n = _shard_shape(shape, axis)
    return k * n // (2 * LANES)


_EARLY = ("w_in", "w_uq", "w_ukv")
_LATE = ("w_branch_a", "w_branch_b", "w_out", "w_up", "w_down", "w_ple_gate", "w_ple")
_NATURAL = ("w_in", "w_up", "w_down", "w_out", "w_ple_gate")
_EARLY_PACKED = tuple(b for b in _BIG if b[0] in _EARLY and b[0] not in _NATURAL)
_LATE_PACKED = tuple(b for b in _BIG if b[0] in _LATE and b[0] not in _NATURAL)


def _halves(a):
    return a.reshape(a.shape[:-2] + (2, a.shape[-2] // 2, a.shape[-1]))


def _rows_joined(a):
    return a.reshape(a.shape[:-3] + (a.shape[-3] * a.shape[-2], a.shape[-1]))


def _pack_pad(group):
    return -sum(_half_rows(shape, axis) for _, shape, axis in group) % PACK_ROWS


def _pack_shards(shards, dtype, group):
    parts = [shards[name].astype(dtype).reshape(2, _half_rows(shape, axis), LANES) for name, shape, axis in group]
    return jnp.concatenate(parts + [jnp.zeros((2, _pack_pad(group), LANES), dtype)], axis=1)


def _unpack_gathered(g, group):
    out, off = {}, 0
    for name, shape, axis in group:
        r = _half_rows(shape, axis)
        k, n = _shard_shape(shape, axis)
        w = g[:, :, off:off + r, :].reshape(4, k, n)
        out[name] = w.reshape(shape) if axis == 0 else w.transpose(1, 0, 2).reshape(shape)
        off += r
    return out


def _pack_grads(grads, group):
    parts = []
    for name, shape, axis in group:
        k, n = _shard_shape(shape, axis)
        g = grads[name]
        g4 = g.reshape(4, k, n) if axis == 0 else g.reshape(k, 4, n).transpose(1, 0, 2)
        parts.append(g4.reshape(4, 2, _half_rows(shape, axis), LANES))
    return jnp.concatenate(parts + [jnp.zeros((4, 2, _pack_pad(group), LANES), F32)], axis=2)


def _unpack_shard_grads(f, group):
    out, off = {}, 0
    for name, shape, axis in group:
        r = _half_rows(shape, axis)
        out[name] = f[:, off:off + r, :].reshape(_shard_shape(shape, axis))
        off += r
    return out


def _pad_slots(w, heads, dim):
    k = w.shape[0]
    return jnp.pad(w.reshape(k, heads, dim), ((0, 0), (0, 0), (0, LANES - dim))).reshape(k, heads * LANES)


def _unpad_slots(w, heads, dim):
    k = w.shape[0]
    return w.reshape(k, heads, LANES)[:, :, :dim].reshape(k, heads * dim)


def _pad_w_in(w):
    kr = jnp.pad(w[:, Z_KR:Z_KR + ROPE_DIM], ((0, 0), (NOPE_DIM, LANES - NOPE_DIM - ROPE_DIM)))
    return jnp.concatenate([w[:, :Z_KR], kr, w[:, Z_KR + ROPE_DIM:]], axis=1)


def _unpad_w_in(w):
    return jnp.concatenate([w[:, :Z_KR], w[:, Z_KR + NOPE_DIM:Z_KR + NOPE_DIM + ROPE_DIM], w[:, Z_GATE:ZW]], axis=1)


def _spread_matrix(heads, dim):
    row = lax.broadcasted_iota(jnp.int32, (heads * dim, heads * LANES), 0)
    col = lax.broadcasted_iota(jnp.int32, (heads * dim, heads * LANES), 1)
    return (col == (row // dim) * LANES + row % dim).astype(BF16)


_SMALL = (("attn_pre_norm", 1024), ("attn_post_norm", 1024), ("b_gate", 2048), ("sinks", 8), ("q_a_norm", 256),
          ("kv_a_norm", 128), ("mlp_pre_norm", 1024), ("mlp_post_norm", 1024), ("conv_b", 5632), ("ple_norm", 1024),
          ("conv_w", 3 * 5632), ("loss", 1))


def _small_rows(n):
    return 8 * -(-n // (8 * LANES))


def _pack_small(vals):
    parts = []
    for name, n in _SMALL:
        r = _small_rows(n)
        parts.append(jnp.pad(vals[name].reshape(-1), (0, r * LANES - n)).reshape(r, LANES))
    return jnp.concatenate(parts, axis=0)


def _unpack_small(buf):
    out, off = {}, 0
    for name, n in _SMALL:
        r = _small_rows(n)
        out[name] = buf[off:off + r].reshape(-1)[:n]
        off += r
    return out


def kernel(x, p, positions, attn_pre_norm, attn_post_norm, w_in, b_gate, sinks, q_a_norm, w_uq, kv_a_norm, w_ukv, w_branch_a, w_branch_b, w_out, mlp_pre_norm, mlp_post_norm, w_up, conv_w, conv_b, w_down, ple_norm, w_ple_gate, w_ple, loss_target, m_attn_pre_norm, m_attn_post_norm, m_w_in, m_b_gate, m_sinks, m_q_a_norm, m_w_uq, m_kv_a_norm, m_w_ukv, m_w_branch_a, m_w_branch_b, m_w_out, m_mlp_pre_norm, m_mlp_post_norm, m_w_up, m_conv_w, m_conv_b, m_w_down, m_ple_norm, m_w_ple_gate, m_w_ple, v_attn_pre_norm, v_attn_post_norm, v_w_in, v_b_gate, v_sinks, v_q_a_norm, v_w_uq, v_kv_a_norm, v_w_ukv, v_w_branch_a, v_w_branch_b, v_w_out, v_mlp_pre_norm, v_mlp_post_norm, v_w_up, v_conv_w, v_conv_b, v_w_down, v_ple_norm, v_w_ple_gate, v_w_ple):
    names = ["attn_pre_norm", "attn_post_norm", "w_in", "b_gate", "sinks", "q_a_norm", "w_uq", "kv_a_norm", "w_ukv",
             "w_branch_a", "w_branch_b", "w_out", "mlp_pre_norm", "mlp_post_norm", "w_up", "conv_w", "conv_b",
             "w_down", "ple_norm", "w_ple_gate", "w_ple"]
    wts = dict(zip(names, [attn_pre_norm, attn_post_norm, w_in, b_gate, sinks, q_a_norm, w_uq, kv_a_norm, w_ukv,
                           w_branch_a, w_branch_b, w_out, mlp_pre_norm, mlp_post_norm, w_up, conv_w, conv_b, w_down,
                           ple_norm, w_ple_gate, w_ple]))
    moms = dict(zip(names, [m_attn_pre_norm, m_attn_post_norm, m_w_in, m_b_gate, m_sinks, m_q_a_norm, m_w_uq,
                            m_kv_a_norm, m_w_ukv, m_w_branch_a, m_w_branch_b, m_w_out, m_mlp_pre_norm,
                            m_mlp_post_norm, m_w_up, m_conv_w, m_conv_b, m_w_down, m_ple_norm, m_w_ple_gate, m_w_ple]))
    vars_ = dict(zip(names, [v_attn_pre_norm, v_attn_post_norm, v_w_in, v_b_gate, v_sinks, v_q_a_norm, v_w_uq,
                             v_kv_a_norm, v_w_ukv, v_w_branch_a, v_w_branch_b, v_w_out, v_mlp_pre_norm,
                             v_mlp_post_norm, v_w_up, v_conv_w, v_conv_b, v_w_down, v_ple_norm, v_w_ple_gate, v_w_ple]))
    w2 = {n: a.reshape(a.shape[-2:]) for n, a in wts.items()}
    m2 = {n: a.reshape(a.shape[-2:]) for n, a in moms.items()}
    v2 = {n: a.reshape(a.shape[-2:]) for n, a in vars_.items()}

    t_rows = x.shape[-2]
    tm = min(256, t_rows)
    tm_wide = min(512, t_rows)
    xc, yc, cc = lax.axis_index("x"), lax.axis_index("y"), lax.axis_index("c")
    chip = 2 * xc + yc

    x2d = x.reshape(t_rows, D_MODEL)
    p2d = p.reshape(t_rows, PLE_DIM)
    tgt = loss_target.reshape(t_rows, D_MODEL)
    pos_f = positions.reshape(t_rows, 1).astype(F32)

    def own_slot_filled(gathered, mine):
        return [lax.dynamic_update_slice(g, m[None], (chip, 0, 0, 0)) for g, m in zip(gathered, mine)]

    def shard_lists(group, packed_group, token=0.0):
        ws = {n: w2[n] + token for n in group}
        return [_halves(ws[n].astype(BF16)) for n in group if n in _NATURAL] + [_pack_shards(ws, BF16, packed_group)]

    cw_rows = 3 * 1408 // LANES
    conv_mine = jnp.pad(w2["conv_w"].reshape(cw_rows, LANES), ((0, 48 - cw_rows), (0, 0))).reshape(2, 24, LANES)
    early_mine = shard_lists(_EARLY, _EARLY_PACKED) + [conv_mine]
    early_sems = _chips_start("gather_early_start", early_mine, MY_HALF)
    early_token = early_sems[4][0:1, 0:1]
    consts = _rope_consts()
    tabs = _rope_tables(pos_f + early_token, consts, tm)
    late_mine = shard_lists(_LATE, _LATE_PACKED, early_token)
    both_done = tabs[0][0:1, 0:1] + sum(m[0, 0:1, 0:1].astype(F32) for m in late_mine)
    early_sent, early_landed = _chips_wait("gather_early_wait", *early_sems[:4], MY_HALF, after=both_done)
    early = own_slot_filled(_pass_to_sibling(early_landed), early_sent)
    late_names = [n for n in _LATE if n in _NATURAL]
    first = [late_names.index("w_out"), len(late_names)]
    late_a = [late_mine[i] for i in first]
    late_b = [m for i, m in enumerate(late_mine) if i not in first]
    late_a_sems = _chips_start("gather_late_a_start", late_a, WHOLE, after=early[0])
    late_b_sems = _chips_start("gather_late_b_start", late_b, WHOLE, after=late_a_sems[4])
    late_token = late_b_sems[4][0:1, 0:1]
    full = _unpack_gathered(early[1], _EARLY_PACKED)
    full["w_in"] = _rows_joined(early[0]).transpose(1, 0, 2).reshape(D_MODEL, 3232)
    conv_full = early[2].reshape(4, 48, LANES)[:, :cw_rows].reshape(4, 3, 1408).transpose(1, 0, 2).reshape(3, 2 * D_FF)
    convw8 = jnp.pad(conv_full, ((0, 5), (0, 0)))

    win = _pad_w_in(full["w_in"])
    wuq = _pad_slots(full["w_uq"], HEADS, NOPE_DIM + ROPE_DIM)
    ukv = full["w_ukv"].reshape(KV_LORA, HEADS, NOPE_DIM + V_DIM)
    wk = _pad_slots(ukv[:, :, :NOPE_DIM].reshape(KV_LORA, HEADS * NOPE_DIM), HEADS, NOPE_DIM)
    wv = _pad_slots(ukv[:, :, NOPE_DIM:].reshape(KV_LORA, HEADS * V_DIM), HEADS, V_DIM)
    g1, g2, g3, g4, g5 = (w2["attn_pre_norm"], w2["attn_post_norm"], w2["mlp_pre_norm"], w2["mlp_post_norm"],
                          w2["ple_norm"])
    gq, gkv, bg, convb = w2["q_a_norm"], w2["kv_a_norm"], w2["b_gate"], w2["conv_b"]
    swa_tile = min(SWA_TILE, t_rows)
    sink_rows = jnp.repeat(w2["sinks"].reshape(A_KV_HEADS, SWA_GROUP, 1), swa_tile, axis=2).reshape(
        A_KV_HEADS, 1, SWA_GROUP * swa_tile)
    swa_bias = _swa_bias(swa_tile)
    spread_q = _spread_matrix(HEADS, A_HEAD_DIM)
    spread_kv = _spread_matrix(A_KV_HEADS, A_HEAD_DIM)

    h1, qs, ks, vs, cq, cqn, ckv, ckvn, qm, km, vm, gate = _fwd_in(x2d, g1, win, bg + late_token, gq, gkv, wuq, wk, wv,
                                                                   spread_q, spread_kv, tabs, tm_wide)
    ya, lse_a = _swa_fwd(qs, ks, vs, swa_bias, sink_rows)
    yb, lse_b = _mla_fwd(qm, km, vm)
    late_sent, late_landed = _chips_wait("gather_late_a_wait", *late_a_sems[:4], WHOLE, after=yb)
    wout_g, packed_g = own_slot_filled(late_landed, late_sent)
    full = _unpack_gathered(packed_g, _LATE_PACKED)
    wba, wbb = full["w_branch_a"], full["w_branch_b"]
    wple = full["w_ple"]
    wout = _rows_joined(wout_g).reshape(-1, D_MODEL)
    pa, pb, mixed, o, x1, h2, ya_c, yb_c = _fwd_mix(x2d, ya, yb, gate, wba, wbb, wout, g2, g3, tm_wide)
    late_sent, late_landed = _chips_wait("gather_late_b_wait", *late_b_sems[:4], WHOLE, after=pa)
    natural = dict(zip([n for n in late_names if n != "w_out"], own_slot_filled(late_landed, late_sent)))
    wup = _rows_joined(natural["w_up"])
    wdown, wpg = (_rows_joined(natural[n]).reshape(-1, D_MODEL) for n in ("w_down", "w_ple_gate"))
    up, a, u = _fwd_up(h2, wup, convw8, convb, tm)
    ff, x2, e, n5, sg, dx3, loss_part = _fwd_out(a, wdown, x1, g4, p2d, wple, g5, wpg, tgt, tm_wide)

    dpre, de, dx2, dff, du, dg5, dg4, dconvb, dconvw8 = _bwd_out(dx3, e, sg, x2, ff, g5, g4, wpg, wdown, up, u, tm)
    dup, dx1, do, dpa, dpb, dgates, dya, dyb, delta_b, dg3, dg2, dbg = _bwd_mid(
        du, convw8, wup, dx2, x1, g3, o, g2, wout, gate, pa, pb, wba, wbb, yb_c, tm)
    late_grads = {
        "w_branch_a": _mm_tn("dw_branch_a", ya_c, dpa),
        "w_branch_b": _mm_tn("dw_branch_b", yb_c, dpb),
        "w_out": _mm_tn("dw_out", mixed, do).reshape(4, D_MODEL // 4, D_MODEL),
        "w_up": _mm_tn("dw_up", h2, dup, column_shards=4),
        "w_down": _mm_tn("dw_down", a, dff).reshape(4, D_FF // 4, D_MODEL),
        "w_ple_gate": _mm_tn("dw_ple_gate", n5, dpre).reshape(4, D_MODEL // 4, D_MODEL),
        "w_ple": _mm_tn("dw_ple", p2d, de),
    }

    def grad_views(grads, group, packed_group):
        return [_halves(grads[n]) for n in group if n in _NATURAL] + [_pack_grads(grads, packed_group)]

    def pair_sums(tag, views, theirs):
        return [_add_pair("rs_%s_add_pair_%d" % (tag, i), g, r, cc) for i, (g, r) in enumerate(zip(views, theirs))]

    swap_sems = _chips_start("swap_late_start", grad_views(late_grads, _LATE, _LATE_PACKED), SIBLING_HALF)
    dqs, dks, dvs, dsink_rows = _swa_bwd(qs, ks, vs, ya, dya, lse_a, swa_bias, sink_rows + swap_sems[4][0:1, 0:1])
    dsink = dsink_rows[:, 0:SWA_GROUP, 0]
    late_views, late_theirs = _chips_wait("swap_late_wait", *swap_sems[:4], SIBLING_HALF, after=dqs)
    rs_sems = _chips_start("scatter_late_start", pair_sums("late", late_views, late_theirs), PIECE)
    dqm, dkm, dvm = _mla_bwd(qm, km, vm, dyb, lse_b, delta_b.reshape(HEADS, 1, t_rows) + rs_sems[4][0:1, 0:1])
    dz, dqb, dx, dgq, dgkv, dg1 = _bwd_in(dqs, dks, dvs, dqm, dkm, dvm, tabs, consts, cq, ckv, gq, gkv, wuq, wk, wv,
                                           dgates, win, x2d, g1, dx1, tm)

    small = {"attn_pre_norm": dg1, "attn_post_norm": dg2, "b_gate": dbg, "sinks": dsink, "q_a_norm": dgq,
             "kv_a_norm": dgkv, "mlp_pre_norm": dg3, "mlp_post_norm": dg4, "conv_b": dconvb, "ple_norm": dg5,
             "conv_w": dconvw8[0:3], "loss": loss_part}
    small_sems = _chips_start("gather_small_start", [_pack_small(small)], EVERYONE)
    small_token = small_sems[4]

    dwk = _unpad_slots(_mm_tn("dw_k", ckvn, dkm, after=small_token), HEADS, NOPE_DIM).reshape(
        KV_LORA, HEADS, NOPE_DIM)
    dwv = _unpad_slots(_mm_tn("dw_v", ckvn, dvm, after=small_token), HEADS, V_DIM).reshape(KV_LORA, HEADS, V_DIM)
    early_grads = {
        "w_in": _unpad_w_in(_mm_tn("dw_in", h1, dz, after=small_token)).reshape(D_MODEL, 4, 808).transpose(1, 0, 2),
        "w_uq": _unpad_slots(_mm_tn("dw_uq", cqn, dqb, after=small_token), HEADS, NOPE_DIM + ROPE_DIM),
        "w_ukv": jnp.concatenate([dwk, dwv], axis=2).reshape(KV_LORA, HEADS * (NOPE_DIM + V_DIM)),
    }

    def finish(tag, pairs, landed, group, packed_group):
        reduced = []
        for i, (pair, land) in enumerate(zip(pairs, landed)):
            own = lax.dynamic_index_in_dim(pair, chip, 0, keepdims=True)
            reduced.append(_add_chips("rs_%s_add_chips_%d" % (tag, i),
                                      lax.dynamic_update_slice(land, own, (chip, 0, 0))))
        others = _swap_sibling("swap_%s_reduced_halves" % tag, reduced)
        r, o = reduced[-1], others[-1]
        packed = jnp.where(cc == 0, jnp.stack([r, o]), jnp.stack([o, r]))
        shards = _unpack_shard_grads(packed, packed_group)
        updates.update(zip(shards, _adamw_many("adamw_%s_packed" % tag, [w2[n] for n in shards], list(shards.values()),
                                               [m2[n] for n in shards], [v2[n] for n in shards])))
        for n, r, o in zip([n for n in group if n in _NATURAL], reduced, others):
            updates[n] = _adamw_halves("adamw_" + n, w2[n], r, o, m2[n], v2[n], cc)

    updates = {}

    early_views = grad_views(early_grads, _EARLY, _EARLY_PACKED)
    early_theirs = _swap_sibling("swap_early_grad_halves", early_views, other_half=True)
    small_sent, small_landed = _chips_wait("gather_small_wait", *small_sems[:4], EVERYONE, after=early_theirs[0])
    small_all = lax.dynamic_update_slice(small_landed[0], small_sent[0][None], (4 * xc + 2 * yc + cc, 0, 0))
    early_sems = _chips_start("scatter_early_start", pair_sums("early", early_views, early_theirs), PIECE,
                              after=small_all)
    late_pairs, late_landed = _chips_wait("scatter_late_wait", *rs_sems[:4], PIECE, after=early_sems[4])
    finish("late", late_pairs, late_landed, _LATE, _LATE_PACKED)
    early_pairs, early_landed = _chips_wait("scatter_early_wait", *early_sems[:4], PIECE,
                                            after=updates[_LATE[-1]][1])
    finish("early", early_pairs, early_landed, _EARLY, _EARLY_PACKED)

    small_sum = _unpack_small(_add_devices(small_all))
    small_names = [n for n in names if n in small_sum]
    small_grads = [lax.dynamic_index_in_dim(small_sum[n].reshape(3, 4, 1408), chip, 1, keepdims=False)
                   if n == "conv_w" else small_sum[n].reshape(w2[n].shape) for n in small_names]
    updates.update(zip(small_names, _adamw_many("adamw_small", [w2[n] for n in small_names], small_grads,
                                                [m2[n] for n in small_names], [v2[n] for n in small_names])))
    loss = small_sum["loss"][0]

    outs = [[updates[n][i].reshape(wts[n].shape) for n in names] for i in range(4)]
    return (loss, dx.reshape(x.shape), *outs[0], *outs[1], *outs[2], *outs[3])
```

```python
import math

import numpy as np
import jax
import jax.numpy as jnp
from jax import lax
from jax.experimental import pallas as pl
from jax.experimental.pallas import tpu as pltpu

F32 = jnp.float32
BF16 = jnp.bfloat16

D_MODEL = 1024
D_FF = 2816
PLE_DIM = 256
ROPE_THETA = 10000.0
RMS_EPS = 1e-6
SWA_WINDOW = 128
HEADS = 8
A_KV_HEADS = 2
A_HEAD_DIM = 64
KV_LORA = 128
NOPE_DIM = 64
ROPE_DIM = 32
V_DIM = 64
LANES = 128
ZW = 3328
NEG = -1e30
SCALE_A = A_HEAD_DIM ** -0.5
SCALE_B = (NOPE_DIM + ROPE_DIM) ** -0.5
LOG2E = math.log2(math.e)

ADAM_LR = 0.001
ADAM_B1 = 0.9
ADAM_B2 = 0.999
ADAM_EPS = 1e-08
ADAM_WD = 0.01
ADAM_STEP = 10

VMEM_LIMIT = 60 * 1024 * 1024
MESH = pl.DeviceIdType.MESH

Z_QA, Z_KA, Z_VA, Z_CQ, Z_CKV, Z_KR, Z_GATE = 0, 512, 640, 768, 1024, 1152, 1280


def _dot(a, b):
    return jnp.dot(a, b, preferred_element_type=F32)


def _dot_nt(a, b):
    return lax.dot_general(a, b, (((1,), (1,)), ((), ())), preferred_element_type=F32)


def _dot_tn(a, b):
    return lax.dot_general(a, b, (((0,), (0,)), ((), ())), preferred_element_type=F32)


def _rms_stats(x):
    r = lax.rsqrt(jnp.mean(x * x, axis=-1, keepdims=True) + RMS_EPS)
    return x * r, r


def _rms_bwd(dy, xn, r, g):
    dxn = dy * g
    dx = r * (dxn - xn * jnp.mean(dxn * xn, axis=-1, keepdims=True))
    dg = jnp.sum(dy * xn, axis=0, keepdims=True)
    return dx, dg


def _tile_lanes(t, n):
    return t if n == 1 else jnp.concatenate([t] * n, axis=1)


def _rope(x, c, s1, s2, half):
    w = x.shape[1]
    n = w // LANES
    return (x * _tile_lanes(c, n) + pltpu.roll(x, w - half, 1) * _tile_lanes(s1, n)
            + pltpu.roll(x, half, 1) * _tile_lanes(s2, n))


def _rope_t(dy, c, s1, s2, half):
    w = dy.shape[1]
    n = w // LANES
    return (dy * _tile_lanes(c, n) + pltpu.roll(dy * _tile_lanes(s1, n), half, 1)
            + pltpu.roll(dy * _tile_lanes(s2, n), w - half, 1))


def _fold_slots(d):
    tiles = []
    for j in range(d.shape[1] // (2 * LANES)):
        even = d[:, 2 * j * LANES:(2 * j + 1) * LANES]
        odd = d[:, (2 * j + 1) * LANES:(2 * j + 2) * LANES]
        tiles.append(even + pltpu.roll(odd, A_HEAD_DIM, 1))
    return tiles[0] if len(tiles) == 1 else jnp.concatenate(tiles, axis=1)


def _spread_slots(c):
    low = lax.broadcasted_iota(jnp.int32, (c.shape[0], LANES), 1) < A_HEAD_DIM
    slots = []
    for j in range(c.shape[1] // LANES):
        tile = c[:, j * LANES:(j + 1) * LANES]
        slots += [jnp.where(low, tile, 0.0), jnp.where(low, pltpu.roll(tile, A_HEAD_DIM, 1), 0.0)]
    return jnp.concatenate(slots, axis=1)


def _sigmoid(x):
    return 1.0 / (1.0 + jnp.exp(-x))


_GELU_C = math.sqrt(2.0 / math.pi)


def _gelu_and_grad(x):
    a = _GELU_C + (_GELU_C * 0.044715) * (x * x)
    th = jnp.tanh(x * a)
    hx = 0.5 * x
    p1 = 1.0 + th
    gel = hx * p1
    dgel = 0.5 * p1 + (hx * (1.0 - th * th)) * (3.0 * a - 2.0 * _GELU_C)
    return gel, dgel


def _conv_taps(up, h6, h7):
    r1 = pltpu.roll(up, 1, 0)
    r2 = pltpu.roll(up, 2, 0)
    rows = lax.broadcasted_iota(jnp.int32, (8, up.shape[1]), 0)
    xm1 = jnp.concatenate([jnp.where(rows == 0, h7, r1[0:8]), r1[8:]], axis=0)
    xm2 = jnp.concatenate([jnp.where(rows == 0, h6, jnp.where(rows == 1, h7, r2[0:8])), r2[8:]], axis=0)
    return xm1, xm2


def _conv_taps_next(du, n0, n1):
    tm = du.shape[0]
    r1 = pltpu.roll(du, tm - 1, 0)
    r2 = pltpu.roll(du, tm - 2, 0)
    rows = lax.broadcasted_iota(jnp.int32, (8, du.shape[1]), 0)
    xp1 = jnp.concatenate([r1[:tm - 8], jnp.where(rows == 7, n0, r1[tm - 8:])], axis=0)
    xp2 = jnp.concatenate([r2[:tm - 8], jnp.where(rows == 6, n0, jnp.where(rows == 7, n1, r2[tm - 8:]))], axis=0)
    return xp1, xp2


def _row(tm, n):
    return pl.BlockSpec((tm, n), lambda i: (i, 0))


def _full(shape):
    nd = len(shape)
    return pl.BlockSpec(tuple(shape), lambda i: (0,) * nd)


def _resident(shape):
    nd = len(shape)
    return pl.BlockSpec(tuple(shape), lambda i: (0,) * nd, pipeline_mode=pl.Buffered(1))


def _heads(tm, h):
    return pl.BlockSpec((h, tm, LANES), lambda i: (0, i, 0))


def _rows_call(name, body, t_rows, tm, ins, outs, scratch=()):
    return pl.pallas_call(
        body, name=name, grid=(t_rows // tm,),
        in_specs=[s for _, s in ins],
        out_specs=[s for _, s in outs],
        out_shape=[s for s, _ in outs],
        scratch_shapes=list(scratch),
        compiler_params=pltpu.CompilerParams(dimension_semantics=("arbitrary",), vmem_limit_bytes=VMEM_LIMIT),
    )(*[a for a, _ in ins])


def _sds(shape, dtype):
    return jax.ShapeDtypeStruct(tuple(shape), dtype)


def _rope_consts():
    c = np.zeros((16, LANES), np.float32)
    lane = np.arange(LANES)
    inv_a = (ROPE_THETA ** (-(np.arange(0, A_HEAD_DIM, 2, dtype=np.float32) / A_HEAD_DIM))).astype(np.float32)
    in_a = lane < A_HEAD_DIM
    c[0, in_a] = inv_a[lane[in_a] % (A_HEAD_DIM // 2)]
    c[1, in_a] = 1.0
    c[2, lane < A_HEAD_DIM // 2] = -1.0
    c[3, (lane >= A_HEAD_DIM // 2) & in_a] = 1.0
    inv_b = (ROPE_THETA ** (-(np.arange(0, ROPE_DIM, 2, dtype=np.float32) / ROPE_DIM))).astype(np.float32)
    pe = (lane >= NOPE_DIM) & (lane < NOPE_DIM + ROPE_DIM)
    c[5, pe] = inv_b[(lane[pe] - NOPE_DIM) % (ROPE_DIM // 2)]
    c[6, pe] = 1.0
    c[7, (lane >= NOPE_DIM) & (lane < NOPE_DIM + ROPE_DIM // 2)] = -1.0
    c[8, (lane >= NOPE_DIM + ROPE_DIM // 2) & (lane < NOPE_DIM + ROPE_DIM)] = 1.0
    c[9, lane < NOPE_DIM] = 1.0
    c[10, pe] = 1.0
    return jnp.asarray(c)


def _rope_tables(pos_f, consts, tm):
    t_rows = pos_f.shape[0]

    def body(pos_ref, c_ref, ca, sa1, sa2, cb, sb1, sb2):
        ang = pos_ref[...] * (c_ref[0:1, :] + c_ref[5:6, :])
        cs, sn = jnp.cos(ang), jnp.sin(ang)
        for ref, row in ((ca, 1), (sa1, 2), (sa2, 3)):
            half = (cs if row == 1 else sn) * c_ref[row:row + 1, :]
            ref[...] = half + pltpu.roll(half, A_HEAD_DIM, 1)
        cb[...] = cs * c_ref[6:7, :] + c_ref[9:10, :]
        sb1[...] = sn * c_ref[7:8, :]
        sb2[...] = sn * c_ref[8:9, :]

    tab = (_sds((t_rows, LANES), F32), _row(tm, LANES))
    return _rows_call("rope_tables", body, t_rows, tm,
                      [(pos_f, _row(tm, 1)), (consts, _full(consts.shape))], [tab] * 6)


def _fwd_in(x, g1, win, bg, gq, gkv, wuq, wk, wv, eq, ek, tabs, tm):
    t_rows = x.shape[0]

    def body(x_ref, g1_ref, win_ref, bg_ref, gq_ref, gkv_ref, wuq_ref, wk_ref, wv_ref, eq_ref, ek_ref,
             ca, sa1, sa2, cb, sb1, sb2,
             h1_ref, qs_ref, ks_ref, vs_ref, cq_ref, cqn_ref, ckv_ref, ckvn_ref, qm_ref, km_ref, vm_ref, gate_ref):
        xn, _ = _rms_stats(x_ref[...])
        hb = (xn * g1_ref[...]).astype(BF16)
        h1_ref[...] = hb
        ta = (ca[...], sa1[...], sa2[...])
        tb = (cb[...], sb1[...], sb2[...])
        cq = _dot(hb, win_ref[:, Z_CQ:Z_CKV])
        ckv = _dot(hb, win_ref[:, Z_CKV:Z_KR])
        z_qa = _dot(hb, win_ref[:, Z_QA:Z_KA])
        z_ka = _dot(hb, win_ref[:, Z_KA:Z_VA])
        z_va = _dot(hb, win_ref[:, Z_VA:Z_CQ])
        z_kr = _dot(hb, win_ref[:, Z_KR:Z_GATE])
        cq_ref[...] = cq
        cqn, _ = _rms_stats(cq)
        cqb = (cqn * gq_ref[...]).astype(BF16)
        cqn_ref[...] = cqb
        ckv_ref[...] = ckv
        ckvn, _ = _rms_stats(ckv)
        ckvb = (ckvn * gkv_ref[...]).astype(BF16)
        ckvn_ref[...] = ckvb
        z_qm = _dot(cqb, wuq_ref[...])
        z_km = _dot(ckvb, wk_ref[...])
        z_vm = _dot(ckvb, wv_ref[...])
        z_gate = _dot(hb, win_ref[:, Z_GATE:ZW])
        qs_ref[...] = _dot((_rope(z_qa, *ta, A_HEAD_DIM // 2) * SCALE_A).astype(BF16), eq_ref[...]).astype(BF16)
        ks_ref[...] = _dot(_rope(z_ka, *ta, A_HEAD_DIM // 2).astype(BF16), ek_ref[...]).astype(BF16)
        vs_ref[...] = _dot(z_va.astype(BF16), ek_ref[...]).astype(BF16)
        qm_ref[...] = (_rope(z_qm, *tb, ROPE_DIM // 2) * (SCALE_B * LOG2E)).astype(BF16)
        km_ref[...] = (z_km + _tile_lanes(_rope(z_kr, *tb, ROPE_DIM // 2), HEADS)).astype(BF16)
        vm_ref[...] = z_vm.astype(BF16)
        gate_ref[...] = _sigmoid(z_gate + bg_ref[...]).astype(BF16)

    def o(n, dt):
        return (_sds((t_rows, n), dt), _row(tm, n))

    ins = [(x, _row(tm, D_MODEL)), (g1, _full(g1.shape)), (win, _resident(win.shape)), (bg, _full(bg.shape)),
           (gq, _full(gq.shape)), (gkv, _full(gkv.shape)), (wuq, _full(wuq.shape)), (wk, _full(wk.shape)),
           (wv, _full(wv.shape)), (eq, _full(eq.shape)), (ek, _full(ek.shape))] + [(t, _row(tm, LANES)) for t in tabs]
    outs = [o(1024, BF16), o(1024, BF16), o(256, BF16), o(256, BF16), o(256, F32), o(256, BF16), o(128, F32),
            o(128, BF16), o(1024, BF16), o(1024, BF16), o(1024, BF16), o(2048, BF16)]
    return _rows_call("fwd_in", body, t_rows, tm, ins, outs)


def _attn_tile(t_rows):
    return min(512, t_rows)


MLA_HEADS_PER_STEP = 4
MLA_FWD_HEADS_PER_STEP = 8


def _causal_pairs(nq, by_kv):
    if by_kv:
        pairs = [(i, j) for j in range(nq) for i in range(j, nq)]
    else:
        pairs = [(i, j) for i in range(nq) for j in range(i + 1)]
    return (jnp.asarray([p[0] for p in pairs], jnp.int32), jnp.asarray([p[1] for p in pairs], jnp.int32))


def _mla_fwd(q, k, v):
    t_rows = q.shape[0]
    t = _attn_tile(t_rows)
    hp = MLA_FWD_HEADS_PER_STEP
    w = hp * LANES
    ii, jj = _causal_pairs(t_rows // t, by_kv=False)

    def body(i_ref, j_ref, q_ref, k_ref, v_ref, o_ref, lse_ref, m_s, l_s, acc_s):
        i = i_ref[pl.program_id(1)]
        j = j_ref[pl.program_id(1)]

        @pl.when(j == 0)
        def _():
            m_s[...] = jnp.full(m_s.shape, NEG, F32)
            l_s[...] = jnp.zeros(l_s.shape, F32)
            acc_s[...] = jnp.zeros(acc_s.shape, F32)

        def step(diagonal):
            sls = [slice(hh * LANES, (hh + 1) * LANES) for hh in range(hp)]
            scores = [_dot_nt(k_ref[:, sl], q_ref[:, sl]) for sl in sls]
            if diagonal:
                valid = (lax.broadcasted_iota(jnp.int32, (t, t), 0) <= lax.broadcasted_iota(jnp.int32, (t, t), 1))
                scores = [jnp.where(valid, s, NEG) for s in scores]
            stats = []
            for hh, s in enumerate(scores):
                m_prev = m_s[hh]
                m_new = jnp.maximum(m_prev, jnp.max(s, axis=0, keepdims=True))
                p = jnp.exp2(s - m_new)
                alpha = jnp.exp2(m_prev - m_new)
                stats.append((m_new, alpha, alpha * l_s[hh] + jnp.sum(p, axis=0, keepdims=True), p.astype(BF16)))
            for hh, (m_new, alpha, l_new, p) in enumerate(stats):
                sl = sls[hh]
                acc = alpha * acc_s[hh] + _dot_tn(v_ref[:, sl], p)
                if diagonal:
                    o_ref[:, sl] = (acc / l_new).T.astype(o_ref.dtype)
                    lse_ref[hh] = m_new + jnp.log2(l_new)
                else:
                    m_s[hh] = m_new
                    l_s[hh] = l_new
                    acc_s[hh] = acc

        pl.when(j < i)(lambda: step(False))
        pl.when(j == i)(lambda: step(True))

    grid_spec = pltpu.PrefetchScalarGridSpec(
        num_scalar_prefetch=2, grid=(HEADS // hp, ii.shape[0]),
        in_specs=[pl.BlockSpec((t, w), lambda hb, s, ir, jr: (ir[s], hb)),
                  pl.BlockSpec((t, w), lambda hb, s, ir, jr: (jr[s], hb)),
                  pl.BlockSpec((t, w), lambda hb, s, ir, jr: (jr[s], hb))],
        out_specs=[pl.BlockSpec((t, w), lambda hb, s, ir, jr: (ir[s], hb)),
                   pl.BlockSpec((hp, 1, t), lambda hb, s, ir, jr: (hb, 0, ir[s]))],
        scratch_shapes=[pltpu.VMEM((hp, 1, t), F32), pltpu.VMEM((hp, 1, t), F32), pltpu.VMEM((hp, LANES, t), F32)])
    return pl.pallas_call(
        body, name="mla_fwd", grid_spec=grid_spec,
        out_shape=[_sds((t_rows, HEADS * LANES), BF16), _sds((HEADS, 1, t_rows), F32)],
        compiler_params=pltpu.CompilerParams(dimension_semantics=("arbitrary",) * 2, vmem_limit_bytes=VMEM_LIMIT),
    )(ii, jj, q, k, v)


def _mla_bwd(q, k, v, do, lse, delta):
    t_rows = q.shape[0]
    t = _attn_tile(t_rows)
    hp = MLA_HEADS_PER_STEP
    w = hp * LANES
    ii, jj = _causal_pairs(t_rows // t, by_kv=True)

    def body(i_ref, j_ref, q_ref, k_ref, v_ref, do_ref, lse_ref, dl_ref, dq_ref, dk_ref, dv_ref):
        i = i_ref[pl.program_id(1)]
        j = j_ref[pl.program_id(1)]

        @pl.when(pl.program_id(1) == 0)
        def _():
            dq_ref[...] = jnp.zeros(dq_ref.shape, F32)

        def step(diagonal):
            r0 = pl.multiple_of(i * t, t)
            sls = [slice(hh * LANES, (hh + 1) * LANES) for hh in range(hp)]
            scores = [_dot_nt(k_ref[:, sl], q_ref[:, sl]) for sl in sls]
            if diagonal:
                valid = (lax.broadcasted_iota(jnp.int32, (t, t), 0) <= lax.broadcasted_iota(jnp.int32, (t, t), 1))
                scores = [jnp.where(valid, s, NEG) for s in scores]
            dps = [_dot_nt(v_ref[:, sl], do_ref[:, sl]) for sl in sls]
            ps = [jnp.exp2(s - lse_ref[hh]) for hh, s in enumerate(scores)]
            dss = [(p * (dp - dl_ref[hh])).astype(BF16) for hh, (p, dp) in enumerate(zip(ps, dps))]
            for hh, sl in enumerate(sls):
                dv = _dot(ps[hh].astype(BF16), do_ref[:, sl])
                dk = _dot(dss[hh], q_ref[:, sl]) * (1.0 / LOG2E)
                if diagonal:
                    dv_ref[:, sl] = dv
                    dk_ref[:, sl] = dk
                else:
                    dv_ref[:, sl] += dv
                    dk_ref[:, sl] += dk
                dq_ref[hh, pl.ds(r0, t), :] += _dot_tn(dss[hh], k_ref[:, sl])

        pl.when(i > j)(lambda: step(False))
        pl.when(i == j)(lambda: step(True))

    def qmap(hb, s, ir, jr):
        return (ir[s], hb)

    def kvmap(hb, s, ir, jr):
        return (jr[s], hb)

    def rowmap(hb, s, ir, jr):
        return (hb, 0, ir[s])

    grid_spec = pltpu.PrefetchScalarGridSpec(
        num_scalar_prefetch=2, grid=(HEADS // hp, ii.shape[0]),
        in_specs=[pl.BlockSpec((t, w), qmap), pl.BlockSpec((t, w), kvmap), pl.BlockSpec((t, w), kvmap),
                  pl.BlockSpec((t, w), qmap), pl.BlockSpec((hp, 1, t), rowmap), pl.BlockSpec((hp, 1, t), rowmap)],
        out_specs=[pl.BlockSpec((hp, t_rows, LANES), lambda hb, s, ir, jr: (hb, 0, 0)),
                   pl.BlockSpec((t, w), kvmap), pl.BlockSpec((t, w), kvmap)])
    return pl.pallas_call(
        body, name="mla_bwd", grid_spec=grid_spec,
        out_shape=[_sds((HEADS, t_rows, LANES), F32), _sds((t_rows, HEADS * LANES), F32),
                   _sds((t_rows, HEADS * LANES), F32)],
        compiler_params=pltpu.CompilerParams(dimension_semantics=("arbitrary",) * 2, vmem_limit_bytes=VMEM_LIMIT),
    )(ii, jj, q, k, v, do, lse, delta)


SWA_TILE = 2 * SWA_WINDOW
SWA_GROUP = HEADS // A_KV_HEADS


def _swa_bias(tq):
    koff = lax.broadcasted_iota(jnp.int32, (tq + SWA_WINDOW, SWA_GROUP * tq), 0) - SWA_WINDOW
    qoff = (lax.broadcasted_iota(jnp.int32, (tq + SWA_WINDOW, SWA_GROUP * tq), 1) % tq)
    band = (koff <= qoff) & (qoff - koff < SWA_WINDOW)
    return jnp.stack([jnp.where(band & (koff >= 0), 0.0, NEG), jnp.where(band, 0.0, NEG)]).astype(F32)


def _swa_specs(tq, nq):
    wb = tq // SWA_WINDOW
    kvw = A_KV_HEADS * LANES

    def qi(i):
        return jnp.minimum(i, nq - 1)

    q = pl.BlockSpec((tq, HEADS * LANES), lambda i: (qi(i), 0))
    cur = pl.BlockSpec((tq, kvw), lambda i: (qi(i), 0))
    prev = pl.BlockSpec((SWA_WINDOW, kvw), lambda i: (jnp.maximum(qi(i) * wb - 1, 0), 0))
    bias = pl.BlockSpec((1, tq + SWA_WINDOW, SWA_GROUP * tq), lambda i: (jnp.minimum(i, 1), 0, 0))
    rows = pl.BlockSpec((A_KV_HEADS, 1, 1, SWA_GROUP * tq), lambda i: (0, qi(i), 0, 0))
    sink = pl.BlockSpec((A_KV_HEADS, 1, SWA_GROUP * tq), lambda i: (0, 0, 0))
    return q, cur, prev, bias, rows, sink


def _stack_heads(ref, kvh):
    base = kvh * SWA_GROUP
    return jnp.concatenate([ref[:, (base + g) * LANES:(base + g + 1) * LANES] for g in range(SWA_GROUP)], axis=0)


def _unstack_heads(ref, kvh, val, tq):
    base = kvh * SWA_GROUP
    for g in range(SWA_GROUP):
        ref[:, (base + g) * LANES:(base + g + 1) * LANES] = val[g * tq:(g + 1) * tq].astype(ref.dtype)


def _kv_window(prev_ref, cur_ref, kvh):
    sl = slice(kvh * LANES, (kvh + 1) * LANES)
    return jnp.concatenate([prev_ref[:, sl], cur_ref[:, sl]], axis=0)


def _swa_fwd(q, k, v, bias, sink_rows):
    t_rows = q.shape[0]
    tq = min(SWA_TILE, t_rows)
    nq = t_rows // tq
    qs_, cur, prev, bs, rows, sk = _swa_specs(tq, nq)
    kvhs = range(A_KV_HEADS)

    def body(q_ref, kc_ref, kp_ref, vc_ref, vp_ref, b_ref, sink_ref, o_ref, lse_ref):
        scores = [_dot_nt(_kv_window(kp_ref, kc_ref, h), _stack_heads(q_ref, h)) + b_ref[0] for h in kvhs]
        stats = []
        for h, s in zip(kvhs, scores):
            sink = sink_ref[h]
            m = jnp.maximum(jnp.max(s, axis=0, keepdims=True), sink)
            p = jnp.exp(s - m)
            l = jnp.sum(p, axis=0, keepdims=True) + jnp.exp(sink - m)
            lse_ref[h, 0] = m + jnp.log(l)
            stats.append((p.astype(BF16), l))
        for h, (p, l) in zip(kvhs, stats):
            _unstack_heads(o_ref, h, (_dot_tn(_kv_window(vp_ref, vc_ref, h), p) / l).T, tq)

    return pl.pallas_call(
        body, name="swa_fwd", grid=(nq,),
        in_specs=[qs_, cur, prev, cur, prev, bs, sk],
        out_specs=[qs_, rows],
        out_shape=[_sds((t_rows, HEADS * LANES), BF16), _sds((A_KV_HEADS, nq, 1, SWA_GROUP * tq), F32)],
        compiler_params=pltpu.CompilerParams(dimension_semantics=("arbitrary",), vmem_limit_bytes=VMEM_LIMIT),
    )(q, k, k, v, v, bias, sink_rows)


def _swa_bwd(q, k, v, o, do, lse, bias, sink_rows):
    t_rows = q.shape[0]
    tq = min(SWA_TILE, t_rows)
    nq = t_rows // tq
    qs_, cur, prev, bs, rows, sk = _swa_specs(tq, nq)
    hw = SWA_WINDOW
    kvhs = range(A_KV_HEADS)
    kvw = A_KV_HEADS * LANES

    def body(q_ref, kc_ref, kp_ref, vc_ref, vp_ref, o_ref, do_ref, lse_ref, b_ref, sink_ref,
             dq_ref, dk_ref, dv_ref, dsink_ref, ck, cv, dsa):
        i = pl.program_id(0)

        @pl.when(i == 0)
        def _():
            dsa[...] = jnp.zeros(dsa.shape, F32)

        @pl.when(i < nq)
        def _():
            qs = [_stack_heads(q_ref, h) for h in kvhs]
            dos = [_stack_heads(do_ref, h) for h in kvhs]
            kks = [_kv_window(kp_ref, kc_ref, h) for h in kvhs]
            scores = [_dot_nt(kks[h], qs[h]) for h in kvhs]
            dps = [_dot_nt(_kv_window(vp_ref, vc_ref, h), dos[h]) for h in kvhs]
            ps, dss = [], []
            for h in kvhs:
                lse = lse_ref[h, 0]
                p = jnp.exp(scores[h] + b_ref[0] - lse)
                delta = jnp.sum((_stack_heads(o_ref, h).astype(F32) * dos[h].astype(F32)).T, axis=0, keepdims=True)
                dsa[h] += -jnp.exp(sink_ref[h] - lse) * delta
                ps.append(p.astype(BF16))
                dss.append((p * (dps[h] - delta)).astype(BF16))
            for h in kvhs:
                sl = slice(h * LANES, (h + 1) * LANES)
                dv = _dot(ps[h], dos[h])
                dk = _dot(dss[h], qs[h])
                _unstack_heads(dq_ref, h, _dot_tn(dss[h], kks[h]), tq)

                @pl.when(i > 0)
                def _():
                    dk_ref[0:tq - hw, sl] = ck[0:tq - hw, sl]
                    dk_ref[tq - hw:tq, sl] = ck[tq - hw:tq, sl] + dk[0:hw]
                    dv_ref[0:tq - hw, sl] = cv[0:tq - hw, sl]
                    dv_ref[tq - hw:tq, sl] = cv[tq - hw:tq, sl] + dv[0:hw]

                ck[:, sl] = dk[hw:hw + tq]
                cv[:, sl] = dv[hw:hw + tq]

        @pl.when(i == nq)
        def _():
            dk_ref[...] = ck[...]
            dv_ref[...] = cv[...]
            dsink_ref[...] = jnp.zeros(dsink_ref.shape, F32)
            for h in kvhs:
                for g in range(SWA_GROUP):
                    tot = jnp.sum(dsa[h, :, g * tq:(g + 1) * tq], axis=1, keepdims=True)
                    dsink_ref[h, g:g + 1, :] = jnp.zeros((1, LANES), F32) + tot

    kv_out = pl.BlockSpec((tq, kvw), lambda i: (jnp.maximum(i - 1, 0), 0))
    return pl.pallas_call(
        body, name="swa_bwd", grid=(nq + 1,),
        in_specs=[qs_, cur, prev, cur, prev, qs_, qs_, rows, bs, sk],
        out_specs=[qs_, kv_out, kv_out, pl.BlockSpec((A_KV_HEADS, 8, LANES), lambda i: (0, 0, 0))],
        out_shape=[_sds((t_rows, HEADS * LANES), F32), _sds((t_rows, kvw), F32), _sds((t_rows, kvw), F32),
                   _sds((A_KV_HEADS, 8, LANES), F32)],
        scratch_shapes=[pltpu.VMEM((tq, kvw), F32), pltpu.VMEM((tq, kvw), F32),
                        pltpu.VMEM((A_KV_HEADS, 1, SWA_GROUP * tq), F32)],
        compiler_params=pltpu.CompilerParams(dimension_semantics=("arbitrary",), vmem_limit_bytes=VMEM_LIMIT),
    )(q, k, k, v, v, o, do, lse, bias, sink_rows)


def _fwd_mix(x, ya, yb, gate, wba, wbb, wout, g2, g3, tm):
    t_rows = x.shape[0]

    def body(x_ref, ya_ref, yb_ref, gate_ref, wba_ref, wbb_ref, wout_ref, g2_ref, g3_ref,
             pa_ref, pb_ref, mixed_ref, o_ref, x1_ref, h2_ref, yac_ref, ybc_ref):
        yac = _fold_slots(ya_ref[...].astype(F32)).astype(BF16)
        ybc = _fold_slots(yb_ref[...].astype(F32)).astype(BF16)
        yac_ref[...] = yac
        ybc_ref[...] = ybc
        pa = _dot(yac, wba_ref[...])
        pb = _dot(ybc, wbb_ref[...])
        pa_ref[...] = pa.astype(BF16)
        pb_ref[...] = pb.astype(BF16)
        mixed = (gate_ref[:, 0:D_MODEL].astype(F32) * pa
                 + gate_ref[:, D_MODEL:2 * D_MODEL].astype(F32) * pb).astype(BF16)
        mixed_ref[...] = mixed
        o = _dot(mixed, wout_ref[...])
        o_ref[...] = o
        on, _ = _rms_stats(o)
        x1 = x_ref[...] + on * g2_ref[...]
        x1_ref[...] = x1
        x1n, _ = _rms_stats(x1)
        h2_ref[...] = (x1n * g3_ref[...]).astype(BF16)

    def o_(dt):
        return (_sds((t_rows, D_MODEL), dt), _row(tm, D_MODEL))

    ins = [(x, _row(tm, D_MODEL)), (ya, _row(tm, 1024)), (yb, _row(tm, 1024)), (gate, _row(tm, 2048)),
           (wba, _resident(wba.shape)), (wbb, _resident(wbb.shape)), (wout, _resident(wout.shape)),
           (g2, _full(g2.shape)), (g3, _full(g3.shape))]
    half = (_sds((t_rows, D_MODEL // 2), BF16), _row(tm, D_MODEL // 2))
    return _rows_call("fwd_mix", body, t_rows, tm, ins,
                      [o_(BF16), o_(BF16), o_(BF16), o_(F32), o_(F32), o_(BF16), half, half])


CONV_CHUNK = 1408


def _fwd_up(h2, wup, convw8, convb, tm):
    t_rows = h2.shape[0]
    cdim = 2 * D_FF

    def body(h2_ref, wup_ref, cw_ref, cb_ref, up_ref, a_ref, u_ref, carry):
        i = pl.program_id(0)

        @pl.when(i == 0)
        def _():
            carry[...] = jnp.zeros(carry.shape, F32)

        hb = h2_ref[...]
        ups = [_dot(hb, wup_ref[s]) for s in range(cdim // CONV_CHUNK)]

        def conv(c0):
            sl = slice(c0, c0 + CONV_CHUNK)
            up = ups[c0 // CONV_CHUNK]
            up_ref[:, sl] = up
            xm1, xm2 = _conv_taps(up, carry[6:7, sl], carry[7:8, sl])
            u = cw_ref[0:1, sl] * xm2 + cw_ref[1:2, sl] * xm1 + cw_ref[2:3, sl] * up + cb_ref[:, sl]
            u_ref[:, sl] = u.astype(BF16)
            carry[:, sl] = up[tm - 8:tm, :]
            return u

        for c0 in range(0, D_FF, CONV_CHUNK):
            ug = conv(c0)
            uv = conv(D_FF + c0)
            gel, _ = _gelu_and_grad(ug)
            a_ref[:, c0:c0 + CONV_CHUNK] = (gel * uv).astype(BF16)

    ins = [(h2, _row(tm, D_MODEL)), (wup, _resident(wup.shape)), (convw8, _full(convw8.shape)),
           (convb, _full(convb.shape))]
    outs = [(_sds((t_rows, cdim), F32), _row(tm, cdim)), (_sds((t_rows, D_FF), BF16), _row(tm, D_FF)),
            (_sds((t_rows, cdim), BF16), _row(tm, cdim))]
    return _rows_call("fwd_up", body, t_rows, tm, ins, outs, scratch=[pltpu.VMEM((8, cdim), F32)])


def _fwd_out(a, wdown, x1, g4, p, wple, g5, wpg, tgt, tm):
    t_rows = a.shape[0]

    def body(a_ref, wdown_ref, x1_ref, g4_ref, p_ref, wple_ref, g5_ref, wpg_ref, tgt_ref,
             ff_ref, x2_ref, e_ref, n5_ref, sg_ref, dx3_ref, loss_ref):
        i = pl.program_id(0)
        ff = _dot(a_ref[...], wdown_ref[...])
        e = _dot(p_ref[...].astype(BF16), wple_ref[...])
        ff_ref[...] = ff
        ffn, _ = _rms_stats(ff)
        x2 = x1_ref[...] + ffn * g4_ref[...]
        x2_ref[...] = x2
        e_ref[...] = e.astype(BF16)
        x2n, _ = _rms_stats(x2)
        n5 = (x2n * g5_ref[...]).astype(BF16)
        n5_ref[...] = n5
        sg = _sigmoid(_dot(n5, wpg_ref[...]))
        sg_ref[...] = sg.astype(BF16)
        d = x2 + sg * e - tgt_ref[...]
        dx3_ref[...] = d * (1.0 / D_MODEL)

        @pl.when(i == 0)
        def _():
            loss_ref[...] = jnp.zeros((1, 1), F32)

        loss_ref[...] += 0.5 * jnp.sum(jnp.sum(d * d, axis=1, keepdims=True), axis=0, keepdims=True) * (1.0 / D_MODEL)

    def o_(dt):
        return (_sds((t_rows, D_MODEL), dt), _row(tm, D_MODEL))

    ins = [(a, _row(tm, D_FF)), (wdown, _resident(wdown.shape)), (x1, _row(tm, D_MODEL)), (g4, _full(g4.shape)),
           (p, _row(tm, PLE_DIM)), (wple, _full(wple.shape)), (g5, _full(g5.shape)), (wpg, _resident(wpg.shape)),
           (tgt, _row(tm, D_MODEL))]
    outs = [o_(F32), o_(F32), o_(BF16), o_(BF16), o_(BF16), o_(F32), (_sds((1, 1), F32), _full((1, 1)))]
    return _rows_call("fwd_out", body, t_rows, tm, ins, outs)


def _bwd_out(dx3, e, sg, x2, ff, g5, g4, wpg, wdown, up, u, tm):
    t_rows = dx3.shape[0]
    cdim = 2 * D_FF
    hb = tm // 8

    def body(dx3_ref, e_ref, sg_ref, x2_ref, ff_ref, g5_ref, g4_ref, wpg_ref, wdown_ref, up_ref, halo_ref, u_ref,
             dpre_ref, de_ref, dx2_ref, dff_ref, du_ref, dg5_ref, dg4_ref, dcb_ref, dcw_ref):
        i = pl.program_id(0)

        @pl.when(i == 0)
        def _():
            dg5_ref[...] = jnp.zeros(dg5_ref.shape, F32)
            dg4_ref[...] = jnp.zeros(dg4_ref.shape, F32)
            dcb_ref[...] = jnp.zeros(dcb_ref.shape, F32)
            dcw_ref[...] = jnp.zeros(dcw_ref.shape, F32)

        dx3 = dx3_ref[...]
        sg = sg_ref[...].astype(F32)
        dpre = (dx3 * e_ref[...].astype(F32) * sg * (1.0 - sg)).astype(BF16)
        dpre_ref[...] = dpre
        de_ref[...] = (dx3 * sg).astype(BF16)
        dn5 = _dot_nt(dpre, wpg_ref[...])
        x2n, r5 = _rms_stats(x2_ref[...])
        d2, dg5 = _rms_bwd(dn5, x2n, r5, g5_ref[...])
        dx2 = dx3 + d2
        dx2_ref[...] = dx2
        dg5_ref[...] += dg5
        ffn, r4 = _rms_stats(ff_ref[...])
        dff, dg4 = _rms_bwd(dx2, ffn, r4, g4_ref[...])
        dg4_ref[...] += dg4
        dffb = dff.astype(BF16)
        dff_ref[...] = dffb
        keep = jnp.where(i > 0, 1.0, 0.0)

        def conv(c0):
            sl = slice(c0, c0 + CONV_CHUNK)
            up = up_ref[:, sl]
            xm1, xm2 = _conv_taps(up, halo_ref[6:7, sl] * keep, halo_ref[7:8, sl] * keep)
            return u_ref[:, sl].astype(F32), up, xm1, xm2

        def grads(c0, du, up, xm1, xm2):
            sl = slice(c0, c0 + CONV_CHUNK)
            du_ref[:, sl] = du.astype(BF16)
            dcb_ref[:, sl] += jnp.sum(du, axis=0, keepdims=True)
            dcw_ref[0:1, sl] += jnp.sum(du * xm2, axis=0, keepdims=True)
            dcw_ref[1:2, sl] += jnp.sum(du * xm1, axis=0, keepdims=True)
            dcw_ref[2:3, sl] += jnp.sum(du * up, axis=0, keepdims=True)

        for c0 in range(0, D_FF, CONV_CHUNK):
            da = _dot_nt(dffb, wdown_ref[c0:c0 + CONV_CHUNK, :])
            ug, *rg = conv(c0)
            uv, *rv = conv(D_FF + c0)
            gel, dgel = _gelu_and_grad(ug)
            grads(c0, da * uv * dgel, *rg)
            grads(D_FF + c0, da * gel, *rv)

    def o_(n, dt):
        return (_sds((t_rows, n), dt), _row(tm, n))

    def acc(r, n):
        return (_sds((r, n), F32), _full((r, n)))

    halo = pl.BlockSpec((8, cdim), lambda i: (jnp.maximum(i * hb - 1, 0), 0))
    ins = [(dx3, _row(tm, D_MODEL)), (e, _row(tm, D_MODEL)), (sg, _row(tm, D_MODEL)), (x2, _row(tm, D_MODEL)),
           (ff, _row(tm, D_MODEL)), (g5, _full(g5.shape)), (g4, _full(g4.shape)), (wpg, _resident(wpg.shape)),
           (wdown, _resident(wdown.shape)), (up, _row(tm, cdim)), (up, halo), (u, _row(tm, cdim))]
    outs = [o_(D_MODEL, BF16), o_(D_MODEL, BF16), o_(D_MODEL, F32), o_(D_MODEL, BF16), o_(cdim, BF16),
            acc(1, D_MODEL), acc(1, D_MODEL), acc(1, cdim), acc(8, cdim)]
    return _rows_call("bwd_out", body, t_rows, tm, ins, outs)


def _bwd_mid(du, convw8, wup, dx2, x1, g3, o, g2, wout, gate, pa, pb, wba, wbb, yb, tm):
    t_rows = du.shape[0]
    cdim = 2 * D_FF
    halo_rows = 16
    hb = tm // halo_rows
    last_blk = t_rows // halo_rows - 1
    n_tiles = t_rows // tm

    def body(du_ref, halo_ref, cw_ref, wup_ref, dx2_ref, x1_ref, g3_ref, o_ref, g2_ref, wout_ref, gate_ref, pa_ref,
             pb_ref, wba_ref, wbb_ref, yb_ref,
             dup_ref, dx1_ref, do_ref, dpa_ref, dpb_ref, dgt_ref, dya_ref, dyb_ref, dl_ref, dg3_ref, dg2_ref, dbg_ref):
        i = pl.program_id(0)

        @pl.when(i == 0)
        def _():
            dg3_ref[...] = jnp.zeros(dg3_ref.shape, F32)
            dg2_ref[...] = jnp.zeros(dg2_ref.shape, F32)
            dbg_ref[...] = jnp.zeros(dbg_ref.shape, F32)

        keep = jnp.where(i < n_tiles - 1, 1.0, 0.0)
        dh2 = jnp.zeros((tm, D_MODEL), F32)
        dups = []
        for c0 in range(0, cdim, CONV_CHUNK):
            sl = slice(c0, c0 + CONV_CHUNK)
            du = du_ref[:, sl].astype(F32)
            nxt = halo_ref[:, sl].astype(F32)
            xp1, xp2 = _conv_taps_next(du, nxt[0:1] * keep, nxt[1:2] * keep)
            dups.append((cw_ref[2:3, sl] * du + cw_ref[1:2, sl] * xp1 + cw_ref[0:1, sl] * xp2).astype(BF16))
            dup_ref[:, sl] = dups[-1]
            if len(dups) > 1:
                dh2 = dh2 + _dot_nt(dups[-2], wup_ref[len(dups) - 2])
        dh2 = dh2 + _dot_nt(dups[-1], wup_ref[len(dups) - 1])
        x1n, r3 = _rms_stats(x1_ref[...])
        d1, dg3 = _rms_bwd(dh2, x1n, r3, g3_ref[...])
        dx1 = dx2_ref[...] + d1
        dx1_ref[...] = dx1
        dg3_ref[...] += dg3
        on, r2 = _rms_stats(o_ref[...])
        do, dg2 = _rms_bwd(dx1, on, r2, g2_ref[...])
        dg2_ref[...] += dg2
        dob = do.astype(BF16)
        do_ref[...] = dob
        dmixed = _dot_nt(dob, wout_ref[...])
        ga = gate_ref[:, 0:D_MODEL].astype(F32)
        gb = gate_ref[:, D_MODEL:2 * D_MODEL].astype(F32)
        dpa = (dmixed * ga).astype(BF16)
        dpb = (dmixed * gb).astype(BF16)
        dpa_ref[...] = dpa
        dpb_ref[...] = dpb
        dga = dmixed * pa_ref[...].astype(F32) * ga * (1.0 - ga)
        dgb = dmixed * pb_ref[...].astype(F32) * gb * (1.0 - gb)
        dgt_ref[:, 0:D_MODEL] = dga.astype(BF16)
        dgt_ref[:, D_MODEL:2 * D_MODEL] = dgb.astype(BF16)
        dbg_ref[:, 0:D_MODEL] += jnp.sum(dga, axis=0, keepdims=True)
        dbg_ref[:, D_MODEL:2 * D_MODEL] += jnp.sum(dgb, axis=0, keepdims=True)
        dya_ref[...] = _spread_slots(_dot_nt(dpa, wba_ref[...])).astype(BF16)
        dyb = _dot_nt(dpb, wbb_ref[...]).astype(BF16)
        dyb_ref[...] = _spread_slots(dyb.astype(F32)).astype(BF16)
        prod = yb_ref[...].astype(F32) * dyb.astype(F32)
        width = HEADS * V_DIM
        lane_head = lax.broadcasted_iota(jnp.int32, (HEADS, width), 1) // V_DIM
        sel = (lane_head == lax.broadcasted_iota(jnp.int32, (HEADS, width), 0)).astype(BF16)
        hi = prod.astype(BF16)
        lo = (prod - hi.astype(F32)).astype(BF16)
        dl_ref[...] = _dot_nt(sel, hi) + _dot_nt(sel, lo)

    def o_(n, dt):
        return (_sds((t_rows, n), dt), _row(tm, n))

    def acc(r, n):
        return (_sds((r, n), F32), _full((r, n)))

    halo = pl.BlockSpec((halo_rows, cdim), lambda i: (jnp.minimum((i + 1) * hb, last_blk), 0))
    ins = [(du, _row(tm, cdim)), (du, halo), (convw8, _full(convw8.shape)), (wup, _resident(wup.shape)),
           (dx2, _row(tm, D_MODEL)), (x1, _row(tm, D_MODEL)), (g3, _full(g3.shape)), (o, _row(tm, D_MODEL)),
           (g2, _full(g2.shape)), (wout, _resident(wout.shape)), (gate, _row(tm, 2048)), (pa, _row(tm, D_MODEL)),
           (pb, _row(tm, D_MODEL)), (wba, _resident(wba.shape)), (wbb, _resident(wbb.shape)),
           (yb, _row(tm, D_MODEL // 2))]
    outs = [o_(cdim, BF16), o_(D_MODEL, F32), o_(D_MODEL, BF16), o_(D_MODEL, BF16), o_(D_MODEL, BF16),
            o_(2048, BF16), o_(1024, BF16), o_(1024, BF16),
            (_sds((HEADS, t_rows), F32), pl.BlockSpec((HEADS, tm), lambda i: (0, i))),
            acc(1, D_MODEL), acc(1, D_MODEL), acc(1, 2048)]
    return _rows_call("bwd_mid", body, t_rows, tm, ins, outs)


def _bwd_in(dqs, dks, dvs, dqm, dkm, dvm, tabs, consts, cq, ckv, gq, gkv, wuq, wk, wv, dgates, win, x, g1, dx1, tm):
    t_rows = x.shape[0]

    def body(dqs_ref, dks_ref, dvs_ref, dqm_ref, dkm_ref, dvm_ref, ca, sa1, sa2, cb, sb1, sb2, c_ref, cq_ref,
             ckv_ref, gq_ref, gkv_ref, wuq_ref, wk_ref, wv_ref, dgt_ref, win_ref, x_ref, g1_ref, dx1_ref,
             dz_ref, dqb_ref, dx_ref, dgq_ref, dgkv_ref, dg1_ref):
        i = pl.program_id(0)

        @pl.when(i == 0)
        def _():
            dgq_ref[...] = jnp.zeros(dgq_ref.shape, F32)
            dgkv_ref[...] = jnp.zeros(dgkv_ref.shape, F32)
            dg1_ref[...] = jnp.zeros(dg1_ref.shape, F32)

        ta = (ca[...], sa1[...], sa2[...])
        tb = (cb[...], sb1[...], sb2[...])

        def piece(lo, hi, val):
            dz_ref[:, lo:hi] = val
            return _dot_nt(val, win_ref[:, lo:hi])

        dh1 = piece(Z_GATE, ZW, dgt_ref[...])
        dkm = dkm_ref[...]
        dckvn = _dot_nt(dkm.astype(BF16), wk_ref[...]) + _dot_nt(dvm_ref[...].astype(BF16), wv_ref[...])
        dh1 = dh1 + piece(Z_VA, Z_CQ, _fold_slots(dvs_ref[...]).astype(BF16))
        dqm = jnp.concatenate([dqm_ref[h] for h in range(HEADS)], axis=1)
        dqb = _rope_t(dqm * SCALE_B, *tb, ROPE_DIM // 2).astype(BF16)
        dqb_ref[...] = dqb
        dcqn = _dot_nt(dqb, wuq_ref[...])
        dqa = _rope_t(_fold_slots(dqs_ref[...]) * SCALE_A, *ta, A_HEAD_DIM // 2)
        dh1 = dh1 + piece(Z_QA, Z_KA, dqa.astype(BF16))
        dh1 = dh1 + piece(Z_KA, Z_VA, _rope_t(_fold_slots(dks_ref[...]), *ta, A_HEAD_DIM // 2).astype(BF16))
        ckvn, rkv = _rms_stats(ckv_ref[...])
        dckv, dgkv = _rms_bwd(dckvn, ckvn, rkv, gkv_ref[...])
        dgkv_ref[...] += dgkv
        dh1 = dh1 + piece(Z_CKV, Z_KR, dckv.astype(BF16))
        dslot = dkm[:, 0:LANES]
        for h in range(1, HEADS):
            dslot = dslot + dkm[:, h * LANES:(h + 1) * LANES]
        dh1 = dh1 + piece(Z_KR, Z_GATE, _rope_t(dslot * c_ref[10:11, :], *tb, ROPE_DIM // 2).astype(BF16))
        cqn, rq = _rms_stats(cq_ref[...])
        dcq, dgq = _rms_bwd(dcqn, cqn, rq, gq_ref[...])
        dgq_ref[...] += dgq
        dh1 = dh1 + piece(Z_CQ, Z_CKV, dcq.astype(BF16))
        xn, r1 = _rms_stats(x_ref[...])
        d0, dg1 = _rms_bwd(dh1, xn, r1, g1_ref[...])
        dg1_ref[...] += dg1
        dx_ref[...] = dx1_ref[...] + d0

    def acc(n):
        return (_sds((1, n), F32), _full((1, n)))

    ins = [(dqs, _row(tm, 1024)), (dks, _row(tm, 256)), (dvs, _row(tm, 256)), (dqm, _heads(tm, HEADS)),
           (dkm, _row(tm, 1024)), (dvm, _row(tm, 1024))] + [(t, _row(tm, LANES)) for t in tabs] + [
           (consts, _full(consts.shape)), (cq, _row(tm, 256)), (ckv, _row(tm, 128)), (gq, _full(gq.shape)),
           (gkv, _full(gkv.shape)), (wuq, _full(wuq.shape)), (wk, _full(wk.shape)), (wv, _full(wv.shape)),
           (dgates, _row(tm, 2048)), (win, _resident(win.shape)), (x, _row(tm, D_MODEL)), (g1, _full(g1.shape)),
           (dx1, _row(tm, D_MODEL))]
    outs = [(_sds((t_rows, ZW), BF16), _row(tm, ZW)), (_sds((t_rows, 1024), BF16), _row(tm, 1024)),
            (_sds((t_rows, D_MODEL), F32), _row(tm, D_MODEL)), acc(256), acc(128), acc(D_MODEL)]
    return _rows_call("bwd_in", body, t_rows, tm, ins, outs)


def _pick_cols(n):
    best = LANES
    for d in range(LANES, min(n, 1664) + 1, LANES):
        if n % d == 0:
            best = d
    return best


def _mm_tn(name, a, b, column_shards=1, after=None):
    t_rows, m = a.shape
    n = b.shape[1]
    bk = min(2048, t_rows)
    bm, bn = _pick_cols(m), _pick_cols(n // column_shards)
    per_shard = n // column_shards // bn
    extra = () if after is None else (after,)

    def body(a_ref, b_ref, *rest):
        o_ref = rest[-1]

        @pl.when(pl.program_id(2) == 0)
        def _():
            o_ref[...] = jnp.zeros((bm, bn), F32)

        o_ref[...] += _dot_tn(a_ref[...].astype(BF16), b_ref[...].astype(BF16))

    return pl.pallas_call(
        body, name=name, grid=(m // bm, n // bn, t_rows // bk),
        in_specs=[pl.BlockSpec((bk, bm), lambda i, j, k: (k, i)), pl.BlockSpec((bk, bn), lambda i, j, k: (k, j))]
        + [pl.BlockSpec((8, LANES), lambda i, j, k: (0, 0))] * len(extra),
        out_specs=(pl.BlockSpec((bm, bn), lambda i, j, k: (i, j)) if column_shards == 1 else
                   pl.BlockSpec((None, bm, bn), lambda i, j, k: (j // per_shard, i, j % per_shard))),
        out_shape=_sds((m, n) if column_shards == 1 else (column_shards, m, n // column_shards), F32),
        compiler_params=pltpu.CompilerParams(dimension_semantics=("arbitrary",) * 3, vmem_limit_bytes=VMEM_LIMIT),
    )(a, b, *extra)


PACK_ROWS = 512


ADD_TILE_ELEMS = 1 << 17


def _add_rows(rows, cols):
    best = 16
    for d in range(16, rows + 1, 16):
        if rows % d == 0 and d * cols <= ADD_TILE_ELEMS:
            best = d
    assert rows % best == 0
    return best


def _add_pair(name, g, recv, half):
    _, _, rows, cols = g.shape
    t = _add_rows(rows, cols)

    def body(h_ref, g_ref, r_ref, o_ref):
        o_ref[...] = (g_ref[:, 0] + r_ref[...]).astype(BF16)

    spec = pl.BlockSpec((4, t, cols), lambda i, h: (0, i, 0))
    grid_spec = pltpu.PrefetchScalarGridSpec(
        num_scalar_prefetch=1, grid=(rows // t,),
        in_specs=[pl.BlockSpec((4, 1, t, cols), lambda i, h: (0, h[0], i, 0)), spec], out_specs=spec)
    return pl.pallas_call(body, name=name, grid_spec=grid_spec,
                          out_shape=_sds(recv.shape, BF16))(jnp.reshape(half, (1,)).astype(jnp.int32), g, recv)


def _add_chips(name, parts):
    _, rows, cols = parts.shape
    t = _add_rows(rows, cols)

    def body(p_ref, o_ref):
        acc = p_ref[0].astype(F32)
        for j in range(1, 4):
            acc = acc + p_ref[j].astype(F32)
        o_ref[...] = acc

    return pl.pallas_call(body, name=name, grid=(rows // t,),
                          in_specs=[pl.BlockSpec((4, t, cols), lambda i: (0, i, 0))],
                          out_specs=pl.BlockSpec((t, cols), lambda i: (i, 0)),
                          out_shape=_sds((rows, cols), F32))(parts)


def _add_devices(parts):
    n, rows, _ = parts.shape

    def body(p_ref, o_ref):
        acc = p_ref[0]
        for j in range(1, n):
            acc = acc + p_ref[j]
        o_ref[...] = acc

    return pl.pallas_call(body, name="small_add", grid=(1,),
                          in_specs=[pl.BlockSpec((n, rows, LANES), lambda i: (0, 0, 0))],
                          out_specs=pl.BlockSpec((rows, LANES), lambda i: (0, 0)),
                          out_shape=_sds((rows, LANES), F32))(parts)


def _adam_rows(k, n):
    target = max(8, (1 << 20) // (4 * n))
    if k <= target:
        return k
    best = None
    for d in range(8, target + 1, 8):
        if k % d == 0:
            best = d
    return best if best is not None else k


def _adam_update(w, g, m, v):
    m_ = ADAM_B1 * m + (1.0 - ADAM_B1) * g
    v_ = ADAM_B2 * v + (1.0 - ADAM_B2) * (g * g)
    delta = -ADAM_LR * ((m_ / (1.0 - ADAM_B1 ** ADAM_STEP)) / (jnp.sqrt(v_ / (1.0 - ADAM_B2 ** ADAM_STEP)) + ADAM_EPS)
                        + ADAM_WD * w)
    return delta, m_, v_


def _adamw_many(name, ws, gs, ms, vs):
    n = len(ws)

    def body(*refs):
        for i in range(n):
            w_ref, g_ref, m_ref, v_ref = (refs[k * n + i] for k in range(4))
            d_ref, mo_ref, vo_ref = (refs[(4 + k) * n + i] for k in range(3))
            d_ref[...], mo_ref[...], vo_ref[...] = _adam_update(w_ref[...], g_ref[...], m_ref[...], v_ref[...])

    specs = [pl.BlockSpec(w.shape, lambda i: (0, 0)) for w in ws]
    out = pl.pallas_call(body, name=name, grid=(1,), in_specs=specs * 4, out_specs=specs * 3,
                         out_shape=[_sds(w.shape, F32) for w in ws] * 3)(*ws, *gs, *ms, *vs)
    return [(gs[i], out[i], out[n + i], out[2 * n + i]) for i in range(n)]


def _adamw_halves(name, w, mine, theirs, m, v, half):
    k, n = w.shape
    bk = _adam_rows(k // 2, n)
    nb = k // 2 // bk

    def body(h_ref, w_ref, mine_ref, theirs_ref, m_ref, v_ref, g_ref, d_ref, mo_ref, vo_ref):
        g = jnp.where(pl.program_id(0) == h_ref[0], mine_ref[...], theirs_ref[...])
        g_ref[...] = g
        d_ref[...], mo_ref[...], vo_ref[...] = _adam_update(w_ref[...], g, m_ref[...], v_ref[...])

    full = pl.BlockSpec((bk, n), lambda h, i, c: (h * nb + i, 0))
    part = pl.BlockSpec((bk, n), lambda h, i, c: (i, 0))
    grid_spec = pltpu.PrefetchScalarGridSpec(num_scalar_prefetch=1, grid=(2, nb),
                                             in_specs=[full, part, part, full, full], out_specs=[full] * 4)
    return tuple(pl.pallas_call(
        body, name=name, grid_spec=grid_spec, out_shape=[_sds((k, n), F32)] * 4,
        compiler_params=pltpu.CompilerParams(vmem_limit_bytes=VMEM_LIMIT),
    )(jnp.reshape(half, (1,)).astype(jnp.int32), w, mine, theirs, m, v))


_HBM = pl.BlockSpec(memory_space=pltpu.HBM)


def _me():
    return lax.axis_index("x"), lax.axis_index("y"), lax.axis_index("c")


def _other_chips(x, y):
    return [(1 - x, y), (x, 1 - y), (1 - x, 1 - y)]


def _pass_to_sibling(zones):
    n = len(zones)

    def body(*refs):
        in_refs, out_refs = refs[:n], refs[n:2 * n]
        send_sems, recv_sems = refs[2 * n:]
        x, y, c = _me()
        sent = []
        for a, (in_ref, out_ref) in enumerate(zip(in_refs, out_refs)):
            for j, (cx, cy) in enumerate(_other_chips(x, y)):
                mine, theirs = (2 * cx + cy, c), (2 * cx + cy, 1 - c)
                sems = dict(send_sem=send_sems.at[3 * a + j], recv_sem=recv_sems.at[3 * a + j],
                            device_id=(x, y, 1 - c), device_id_type=MESH)
                sent.append(tuple(pltpu.make_async_remote_copy(src_ref=in_ref.at[part], dst_ref=out_ref.at[part], **sems)
                                  for part in (mine, theirs)))
        for send, _ in sent:
            send.start()
        for _, recv in sent:
            recv.wait_recv()
        for send, _ in sent:
            send.wait_send()

    return pl.pallas_call(
        body, name="pass_to_sibling", out_shape=[_sds(z.shape, z.dtype) for z in zones],
        in_specs=[_HBM] * n, out_specs=[_HBM] * n, input_output_aliases={i: i for i in range(n)},
        scratch_shapes=[pltpu.SemaphoreType.DMA((3 * n,)), pltpu.SemaphoreType.DMA((3 * n,))],
    )(*zones)


def _swap_sibling(name, vs, other_half=False):
    n = len(vs)

    def body(*refs):
        v_refs, out_refs = refs[:n], refs[n:2 * n]
        send_sems, recv_sems = refs[2 * n:]
        x, y, c = _me()
        cps = [pltpu.make_async_remote_copy(src_ref=v_ref.at[:, 1 - c] if other_half else v_ref, dst_ref=out_ref,
                                            send_sem=send_sems.at[a], recv_sem=recv_sems.at[a],
                                            device_id=(x, y, 1 - c), device_id_type=MESH)
               for a, (v_ref, out_ref) in enumerate(zip(v_refs, out_refs))]
        for cp in cps:
            cp.start()
        for cp in cps:
            cp.wait()

    def landing(v):
        return _sds((v.shape[0],) + v.shape[2:] if other_half else v.shape, v.dtype)

    return pl.pallas_call(
        body, name=name, out_shape=[landing(v) for v in vs], in_specs=[_HBM] * n, out_specs=[_HBM] * n,
        scratch_shapes=[pltpu.SemaphoreType.DMA((n,)), pltpu.SemaphoreType.DMA((n,))],
    )(*vs)


_SEM = pl.BlockSpec(memory_space=pltpu.SEMAPHORE)
_EFFECT = pltpu.SideEffectType.DATAFLOW_SIDE_EFFECTING
WHOLE = "whole"
PIECE = "piece"
SIBLING_HALF = "sibling"
MY_HALF = "half"
EVERYONE = "everyone"
_COPIES = {WHOLE: 3, PIECE: 3, MY_HALF: 3, SIBLING_HALF: 1, EVERYONE: 7}


def _landing_shape(v, mode):
    return {WHOLE: (4,) + v.shape, MY_HALF: (4,) + v.shape, PIECE: v.shape, EVERYONE: (8,) + v.shape,
            SIBLING_HALF: (v.shape[0],) + v.shape[2:]}[mode]


def _chip_copies(v_ref, land_ref, send_sems, recv_sems, mode, sem0=0):
    x, y, c = _me()
    if mode == SIBLING_HALF:
        cp = pltpu.make_async_remote_copy(src_ref=v_ref.at[:, 1 - c], dst_ref=land_ref, send_sem=send_sems.at[sem0],
                                          recv_sem=recv_sems.at[sem0], device_id=(x, y, 1 - c), device_id_type=MESH)
        return [(cp, cp)]
    if mode == EVERYONE:
        out = []
        for f in range(1, 8):
            px, py, pc = (1 - x if f & 4 else x), (1 - y if f & 2 else y), (1 - c if f & 1 else c)
            sems = dict(send_sem=send_sems.at[sem0 + f - 1], recv_sem=recv_sems.at[sem0 + f - 1],
                        device_id=(px, py, pc), device_id_type=MESH)
            out.append((pltpu.make_async_remote_copy(src_ref=v_ref, dst_ref=land_ref.at[4 * x + 2 * y + c], **sems),
                        pltpu.make_async_remote_copy(src_ref=v_ref, dst_ref=land_ref.at[4 * px + 2 * py + pc], **sems)))
        return out
    k = 2 * x + y
    out = []
    for j, (cx, cy) in enumerate(_other_chips(x, y)):
        if mode == MY_HALF:
            src, mine, theirs = v_ref.at[c], land_ref.at[k, c], land_ref.at[2 * cx + cy, c]
        else:
            src = v_ref.at[2 * cx + cy] if mode == PIECE else v_ref
            mine, theirs = land_ref.at[k], land_ref.at[2 * cx + cy]
        sems = dict(send_sem=send_sems.at[sem0 + j], recv_sem=recv_sems.at[sem0 + j], device_id=(cx, cy, c),
                    device_id_type=MESH)
        send = pltpu.make_async_remote_copy(src_ref=src, dst_ref=mine, **sems)
        recv = pltpu.make_async_remote_copy(src_ref=src, dst_ref=theirs, **sems)
        out.append((send, recv))
    return out


def _chips_start(name, vs, mode, after=None):
    n = len(vs)
    lands = [_landing_shape(v, mode) for v in vs]

    def body(*refs):
        v_refs, land_refs = refs[:n], refs[n:2 * n]
        send_sems, recv_sems = refs[-2 * n - 3], refs[-2 * n - 2]
        token = refs[-1]
        for a in range(n):
            for send, _ in _chip_copies(v_refs[a], land_refs[a], send_sems, recv_sems, mode, _COPIES[mode] * a):
                send.start()
        token[...] = jnp.zeros_like(token)

    extra = () if after is None else (after,)
    hbm = [pltpu.with_memory_space_constraint(v, pltpu.HBM) for v in vs]
    zones = [pltpu.with_memory_space_constraint(lax.empty(s, v.dtype), pltpu.HBM) for s, v in zip(lands, vs)]
    out = pl.pallas_call(
        body, name=name,
        out_shape=(pltpu.SemaphoreType.DMA((_COPIES[mode] * n,)), pltpu.SemaphoreType.DMA((_COPIES[mode] * n,)),
                   *[pltpu.HBM(v.shape, v.dtype) for v in vs], *[pltpu.HBM(s, v.dtype) for s, v in zip(lands, vs)],
                   _sds((8, LANES), F32)),
        in_specs=(_HBM,) * (2 * n) + (pl.BlockSpec(memory_space=pl.ANY),) * len(extra),
        out_specs=(_SEM, _SEM) + (_HBM,) * (2 * n) + (pl.BlockSpec(memory_space=pltpu.VMEM),),
        input_output_aliases={i: 2 + i for i in range(2 * n)},
        compiler_params=pltpu.CompilerParams(has_side_effects=_EFFECT),
    )(*hbm, *zones, *extra)
    return out[0], out[1], list(out[2:2 + n]), list(out[2 + n:2 + 2 * n]), out[-1]


def _chips_wait(name, send_sems, recv_sems, v_thru, land_thru, mode, after):
    n = len(v_thru)

    def body(*refs):
        v_refs, land_refs = refs[:n], refs[n:2 * n]
        send_sems, recv_sems = refs[2 * n], refs[2 * n + 1]
        for a in range(n):
            for send, recv in _chip_copies(v_refs[a], land_refs[a], send_sems, recv_sems, mode, _COPIES[mode] * a):
                send.wait_send()
                recv.wait_recv()

    out = pl.pallas_call(
        body, name=name,
        out_shape=tuple(pltpu.HBM(a.shape, a.dtype) for a in list(v_thru) + list(land_thru)),
        in_specs=(_HBM,) * (2 * n) + (_SEM, _SEM, pl.BlockSpec(memory_space=pl.ANY)), out_specs=(_HBM,) * (2 * n),
        input_output_aliases={i: i for i in range(2 * n)},
        compiler_params=pltpu.CompilerParams(has_side_effects=_EFFECT),
    )(*v_thru, *land_thru, send_sems, recv_sems, after)
    return list(out[:n]), list(out[n:])


_BIG = (("w_in", (1024, 3232), 1), ("w_uq", (256, 768), 1), ("w_ukv", (128, 1024), 1), ("w_branch_a", (512, 1024), 1),
        ("w_branch_b", (512, 1024), 1), ("w_out", (1024, 1024), 0), ("w_up", (1024, 5632), 1),
        ("w_down", (2816, 1024), 0), ("w_ple_gate", (1024, 1024), 0), ("w_ple", (256, 1024), 1))


def _shard_shape(shape, axis):
    return (shape[0] // 4, shape[1]) if axis == 0 else (shape[0], shape[1] // 4)


def _half_rows(shape, axis):
    k, n = _shard_shape(shape, axis)
    return k * n // (2 * LANES)


_EARLY = ("w_in", "w_uq", "w_ukv")
_LATE = ("w_branch_a", "w_branch_b", "w_out", "w_up", "w_down", "w_ple_gate", "w_ple")
_NATURAL = ("w_in", "w_up", "w_down", "w_out", "w_ple_gate")
_EARLY_PACKED = tuple(b for b in _BIG if b[0] in _EARLY and b[0] not in _NATURAL)
_LATE_PACKED = tuple(b for b in _BIG if b[0] in _LATE and b[0] not in _NATURAL)


def _halves(a):
    return a.reshape(a.shape[:-2] + (2, a.shape[-2] // 2, a.shape[-1]))


def _rows_joined(a):
    return a.reshape(a.shape[:-3] + (a.shape[-3] * a.shape[-2], a.shape[-1]))


def _pack_pad(group):
    return -sum(_half_rows(shape, axis) for _, shape, axis in group) % PACK_ROWS


def _pack_shards(shards, dtype, group):
    parts = [shards[name].astype(dtype).reshape(2, _half_rows(shape, axis), LANES) for name, shape, axis in group]
    return jnp.concatenate(parts + [jnp.zeros((2, _pack_pad(group), LANES), dtype)], axis=1)


def _unpack_gathered(g, group):
    out, off = {}, 0
    for name, shape, axis in group:
        r = _half_rows(shape, axis)
        k, n = _shard_shape(shape, axis)
        w = g[:, :, off:off + r, :].reshape(4, k, n)
        out[name] = w.reshape(shape) if axis == 0 else w.transpose(1, 0, 2).reshape(shape)
        off += r
    return out


def _pack_grads(grads, group):
    parts = []
    for name, shape, axis in group:
        k, n = _shard_shape(shape, axis)
        g = grads[name]
        g4 = g.reshape(4, k, n) if axis == 0 else g.reshape(k, 4, n).transpose(1, 0, 2)
        parts.append(g4.reshape(4, 2, _half_rows(shape, axis), LANES))
    return jnp.concatenate(parts + [jnp.zeros((4, 2, _pack_pad(group), LANES), F32)], axis=2)


def _unpack_shard_grads(f, group):
    out, off = {}, 0
    for name, shape, axis in group:
        r = _half_rows(shape, axis)
        out[name] = f[:, off:off + r, :].reshape(_shard_shape(shape, axis))
        off += r
    return out


def _pad_slots(w, heads, dim):
    k = w.shape[0]
    return jnp.pad(w.reshape(k, heads, dim), ((0, 0), (0, 0), (0, LANES - dim))).reshape(k, heads * LANES)


def _unpad_slots(w, heads, dim):
    k = w.shape[0]
    return w.reshape(k, heads, LANES)[:, :, :dim].reshape(k, heads * dim)


def _pad_w_in(w):
    kr = jnp.pad(w[:, Z_KR:Z_KR + ROPE_DIM], ((0, 0), (NOPE_DIM, LANES - NOPE_DIM - ROPE_DIM)))
    return jnp.concatenate([w[:, :Z_KR], kr, w[:, Z_KR + ROPE_DIM:]], axis=1)


def _unpad_w_in(w):
    return jnp.concatenate([w[:, :Z_KR], w[:, Z_KR + NOPE_DIM:Z_KR + NOPE_DIM + ROPE_DIM], w[:, Z_GATE:ZW]], axis=1)


def _spread_matrix(heads, dim):
    row = lax.broadcasted_iota(jnp.int32, (heads * dim, heads * LANES), 0)
    col = lax.broadcasted_iota(jnp.int32, (heads * dim, heads * LANES), 1)
    return (col == (row // dim) * LANES + row % dim).astype(BF16)


_SMALL = (("attn_pre_norm", 1024), ("attn_post_norm", 1024), ("b_gate", 2048), ("sinks", 8), ("q_a_norm", 256),
          ("kv_a_norm", 128), ("mlp_pre_norm", 1024), ("mlp_post_norm", 1024), ("conv_b", 5632), ("ple_norm", 1024),
          ("conv_w", 3 * 5632), ("loss", 1))


def _small_rows(n):
    return 8 * -(-n // (8 * LANES))


def _pack_small(vals):
    parts = []
    for name, n in _SMALL:
        r = _small_rows(n)
        parts.append(jnp.pad(vals[name].reshape(-1), (0, r * LANES - n)).reshape(r, LANES))
    return jnp.concatenate(parts, axis=0)


def _unpack_small(buf):
    out, off = {}, 0
    for name, n in _SMALL:
        r = _small_rows(n)
        out[name] = buf[off:off + r].reshape(-1)[:n]
        off += r
    return out


def kernel(x, p, positions, attn_pre_norm, attn_post_norm, w_in, b_gate, sinks, q_a_norm, w_uq, kv_a_norm, w_ukv, w_branch_a, w_branch_b, w_out, mlp_pre_norm, mlp_post_norm, w_up, conv_w, conv_b, w_down, ple_norm, w_ple_gate, w_ple, loss_target, m_attn_pre_norm, m_attn_post_norm, m_w_in, m_b_gate, m_sinks, m_q_a_norm, m_w_uq, m_kv_a_norm, m_w_ukv, m_w_branch_a, m_w_branch_b, m_w_out, m_mlp_pre_norm, m_mlp_post_norm, m_w_up, m_conv_w, m_conv_b, m_w_down, m_ple_norm, m_w_ple_gate, m_w_ple, v_attn_pre_norm, v_attn_post_norm, v_w_in, v_b_gate, v_sinks, v_q_a_norm, v_w_uq, v_kv_a_norm, v_w_ukv, v_w_branch_a, v_w_branch_b, v_w_out, v_mlp_pre_norm, v_mlp_post_norm, v_w_up, v_conv_w, v_conv_b, v_w_down, v_ple_norm, v_w_ple_gate, v_w_ple):
    names = ["attn_pre_norm", "attn_post_norm", "w_in", "b_gate", "sinks", "q_a_norm", "w_uq", "kv_a_norm", "w_ukv",
             "w_branch_a", "w_branch_b", "w_out", "mlp_pre_norm", "mlp_post_norm", "w_up", "conv_w", "conv_b",
             "w_down", "ple_norm", "w_ple_gate", "w_ple"]
    wts = dict(zip(names, [attn_pre_norm, attn_post_norm, w_in, b_gate, sinks, q_a_norm, w_uq, kv_a_norm, w_ukv,
                           w_branch_a, w_branch_b, w_out, mlp_pre_norm, mlp_post_norm, w_up, conv_w, conv_b, w_down,
                           ple_norm, w_ple_gate, w_ple]))
    moms = dict(zip(names, [m_attn_pre_norm, m_attn_post_norm, m_w_in, m_b_gate, m_sinks, m_q_a_norm, m_w_uq,
                            m_kv_a_norm, m_w_ukv, m_w_branch_a, m_w_branch_b, m_w_out, m_mlp_pre_norm,
                            m_mlp_post_norm, m_w_up, m_conv_w, m_conv_b, m_w_down, m_ple_norm, m_w_ple_gate, m_w_ple]))
    vars_ = dict(zip(names, [v_attn_pre_norm, v_attn_post_norm, v_w_in, v_b_gate, v_sinks, v_q_a_norm, v_w_uq,
                             v_kv_a_norm, v_w_ukv, v_w_branch_a, v_w_branch_b, v_w_out, v_mlp_pre_norm,
                             v_mlp_post_norm, v_w_up, v_conv_w, v_conv_b, v_w_down, v_ple_norm, v_w_ple_gate, v_w_ple]))
    w2 = {n: a.reshape(a.shape[-2:]) for n, a in wts.items()}
    m2 = {n: a.reshape(a.shape[-2:]) for n, a in moms.items()}
    v2 = {n: a.reshape(a.shape[-2:]) for n, a in vars_.items()}

    t_rows = x.shape[-2]
    tm = min(256, t_rows)
    tm_wide = min(512, t_rows)
    xc, yc, cc = lax.axis_index("x"), lax.axis_index("y"), lax.axis_index("c")
    chip = 2 * xc + yc

    x2d = x.reshape(t_rows, D_MODEL)
    p2d = p.reshape(t_rows, PLE_DIM)
    tgt = loss_target.reshape(t_rows, D_MODEL)
    pos_f = positions.reshape(t_rows, 1).astype(F32)

    def own_slot_filled(gathered, mine):
        return [lax.dynamic_update_slice(g, m[None], (chip, 0, 0, 0)) for g, m in zip(gathered, mine)]

    def shard_lists(group, packed_group, token=0.0):
        ws = {n: w2[n] + token for n in group}
        return [_halves(ws[n].astype(BF16)) for n in group if n in _NATURAL] + [_pack_shards(ws, BF16, packed_group)]

    cw_rows = 3 * 1408 // LANES
    conv_mine = jnp.pad(w2["conv_w"].reshape(cw_rows, LANES), ((0, 48 - cw_rows), (0, 0))).reshape(2, 24, LANES)
    early_mine = shard_lists(_EARLY, _EARLY_PACKED) + [conv_mine]
    early_sems = _chips_start("gather_early_start", early_mine, MY_HALF)
    early_token = early_sems[4][0:1, 0:1]
    consts = _rope_consts()
    tabs = _rope_tables(pos_f + early_token, consts, tm)
    late_mine = shard_lists(_LATE, _LATE_PACKED, early_token)
    both_done = tabs[0][0:1, 0:1] + sum(m[0, 0:1, 0:1].astype(F32) for m in late_mine)
    early_sent, early_landed = _chips_wait("gather_early_wait", *early_sems[:4], MY_HALF, after=both_done)
    early = own_slot_filled(_pass_to_sibling(early_landed), early_sent)
    late_names = [n for n in _LATE if n in _NATURAL]
    first = [late_names.index("w_out"), len(late_names)]
    late_a = [late_mine[i] for i in first]
    late_b = [m for i, m in enumerate(late_mine) if i not in first]
    late_a_sems = _chips_start("gather_late_a_start", late_a, WHOLE, after=early[0])
    late_b_sems = _chips_start("gather_late_b_start", late_b, WHOLE, after=late_a_sems[4])
    late_token = late_b_sems[4][0:1, 0:1]
    full = _unpack_gathered(early[1], _EARLY_PACKED)
    full["w_in"] = _rows_joined(early[0]).transpose(1, 0, 2).reshape(D_MODEL, 3232)
    conv_full = early[2].reshape(4, 48, LANES)[:, :cw_rows].reshape(4, 3, 1408).transpose(1, 0, 2).reshape(3, 2 * D_FF)
    convw8 = jnp.pad(conv_full, ((0, 5), (0, 0)))

    win = _pad_w_in(full["w_in"])
    wuq = _pad_slots(full["w_uq"], HEADS, NOPE_DIM + ROPE_DIM)
    ukv = full["w_ukv"].reshape(KV_LORA, HEADS, NOPE_DIM + V_DIM)
    wk = _pad_slots(ukv[:, :, :NOPE_DIM].reshape(KV_LORA, HEADS * NOPE_DIM), HEADS, NOPE_DIM)
    wv = _pad_slots(ukv[:, :, NOPE_DIM:].reshape(KV_LORA, HEADS * V_DIM), HEADS, V_DIM)
    g1, g2, g3, g4, g5 = (w2["attn_pre_norm"], w2["attn_post_norm"], w2["mlp_pre_norm"], w2["mlp_post_norm"],
                          w2["ple_norm"])
    gq, gkv, bg, convb = w2["q_a_norm"], w2["kv_a_norm"], w2["b_gate"], w2["conv_b"]
    swa_tile = min(SWA_TILE, t_rows)
    sink_rows = jnp.repeat(w2["sinks"].reshape(A_KV_HEADS, SWA_GROUP, 1), swa_tile, axis=2).reshape(
        A_KV_HEADS, 1, SWA_GROUP * swa_tile)
    swa_bias = _swa_bias(swa_tile)
    spread_q = _spread_matrix(HEADS, A_HEAD_DIM)
    spread_kv = _spread_matrix(A_KV_HEADS, A_HEAD_DIM)

    h1, qs, ks, vs, cq, cqn, ckv, ckvn, qm, km, vm, gate = _fwd_in(x2d, g1, win, bg + late_token, gq, gkv, wuq, wk, wv,
                                                                   spread_q, spread_kv, tabs, tm_wide)
    ya, lse_a = _swa_fwd(qs, ks, vs, swa_bias, sink_rows)
    yb, lse_b = _mla_fwd(qm, km, vm)
    late_sent, late_landed = _chips_wait("gather_late_a_wait", *late_a_sems[:4], WHOLE, after=yb)
    wout_g, packed_g = own_slot_filled(late_landed, late_sent)
    full = _unpack_gathered(packed_g, _LATE_PACKED)
    wba, wbb = full["w_branch_a"], full["w_branch_b"]
    wple = full["w_ple"]
    wout = _rows_joined(wout_g).reshape(-1, D_MODEL)
    pa, pb, mixed, o, x1, h2, ya_c, yb_c = _fwd_mix(x2d, ya, yb, gate, wba, wbb, wout, g2, g3, tm_wide)
    late_sent, late_landed = _chips_wait("gather_late_b_wait", *late_b_sems[:4], WHOLE, after=pa)
    natural = dict(zip([n for n in late_names if n != "w_out"], own_slot_filled(late_landed, late_sent)))
    wup = _rows_joined(natural["w_up"])
    wdown, wpg = (_rows_joined(natural[n]).reshape(-1, D_MODEL) for n in ("w_down", "w_ple_gate"))
    up, a, u = _fwd_up(h2, wup, convw8, convb, tm)
    ff, x2, e, n5, sg, dx3, loss_part = _fwd_out(a, wdown, x1, g4, p2d, wple, g5, wpg, tgt, tm_wide)

    dpre, de, dx2, dff, du, dg5, dg4, dconvb, dconvw8 = _bwd_out(dx3, e, sg, x2, ff, g5, g4, wpg, wdown, up, u, tm)
    dup, dx1, do, dpa, dpb, dgates, dya, dyb, delta_b, dg3, dg2, dbg = _bwd_mid(
        du, convw8, wup, dx2, x1, g3, o, g2, wout, gate, pa, pb, wba, wbb, yb_c, tm)
    late_grads = {
        "w_branch_a": _mm_tn("dw_branch_a", ya_c, dpa),
        "w_branch_b": _mm_tn("dw_branch_b", yb_c, dpb),
        "w_out": _mm_tn("dw_out", mixed, do).reshape(4, D_MODEL // 4, D_MODEL),
        "w_up": _mm_tn("dw_up", h2, dup, column_shards=4),
        "w_down": _mm_tn("dw_down", a, dff).reshape(4, D_FF // 4, D_MODEL),
        "w_ple_gate": _mm_tn("dw_ple_gate", n5, dpre).reshape(4, D_MODEL // 4, D_MODEL),
        "w_ple": _mm_tn("dw_ple", p2d, de),
    }

    def grad_views(grads, group, packed_group):
        return [_halves(grads[n]) for n in group if n in _NATURAL] + [_pack_grads(grads, packed_group)]

    def pair_sums(tag, views, theirs):
        return [_add_pair("rs_%s_add_pair_%d" % (tag, i), g, r, cc) for i, (g, r) in enumerate(zip(views, theirs))]

    swap_sems = _chips_start("swap_late_start", grad_views(late_grads, _LATE, _LATE_PACKED), SIBLING_HALF)
    dqs, dks, dvs, dsink_rows = _swa_bwd(qs, ks, vs, ya, dya, lse_a, swa_bias, sink_rows + swap_sems[4][0:1, 0:1])
    dsink = dsink_rows[:, 0:SWA_GROUP, 0]
    late_views, late_theirs = _chips_wait("swap_late_wait", *swap_sems[:4], SIBLING_HALF, after=dqs)
    rs_sems = _chips_start("scatter_late_start", pair_sums("late", late_views, late_theirs), PIECE)
    dqm, dkm, dvm = _mla_bwd(qm, km, vm, dyb, lse_b, delta_b.reshape(HEADS, 1, t_rows) + rs_sems[4][0:1, 0:1])
    dz, dqb, dx, dgq, dgkv, dg1 = _bwd_in(dqs, dks, dvs, dqm, dkm, dvm, tabs, consts, cq, ckv, gq, gkv, wuq, wk, wv,
                                           dgates, win, x2d, g1, dx1, tm)

    small = {"attn_pre_norm": dg1, "attn_post_norm": dg2, "b_gate": dbg, "sinks": dsink, "q_a_norm": dgq,
             "kv_a_norm": dgkv, "mlp_pre_norm": dg3, "mlp_post_norm": dg4, "conv_b": dconvb, "ple_norm": dg5,
             "conv_w": dconvw8[0:3], "loss": loss_part}
    small_sems = _chips_start("gather_small_start", [_pack_small(small)], EVERYONE)
    small_token = small_sems[4]

    dwk = _unpad_slots(_mm_tn("dw_k", ckvn, dkm, after=small_token), HEADS, NOPE_DIM).reshape(
        KV_LORA, HEADS, NOPE_DIM)
    dwv = _unpad_slots(_mm_tn("dw_v", ckvn, dvm, after=small_token), HEADS, V_DIM).reshape(KV_LORA, HEADS, V_DIM)
    early_grads = {
        "w_in": _unpad_w_in(_mm_tn("dw_in", h1, dz, after=small_token)).reshape(D_MODEL, 4, 808).transpose(1, 0, 2),
        "w_uq": _unpad_slots(_mm_tn("dw_uq", cqn, dqb, after=small_token), HEADS, NOPE_DIM + ROPE_DIM),
        "w_ukv": jnp.concatenate([dwk, dwv], axis=2).reshape(KV_LORA, HEADS * (NOPE_DIM + V_DIM)),
    }

    def finish(tag, pairs, landed, group, packed_group):
        reduced = []
        for i, (pair, land) in enumerate(zip(pairs, landed)):
            own = lax.dynamic_index_in_dim(pair, chip, 0, keepdims=True)
            reduced.append(_add_chips("rs_%s_add_chips_%d" % (tag, i),
                                      lax.dynamic_update_slice(land, own, (chip, 0, 0))))
        others = _swap_sibling("swap_%s_reduced_halves" % tag, reduced)
        r, o = reduced[-1], others[-1]
        packed = jnp.where(cc == 0, jnp.stack([r, o]), jnp.stack([o, r]))
        shards = _unpack_shard_grads(packed, packed_group)
        updates.update(zip(shards, _adamw_many("adamw_%s_packed" % tag, [w2[n] for n in shards], list(shards.values()),
                                               [m2[n] for n in shards], [v2[n] for n in shards])))
        for n, r, o in zip([n for n in group if n in _NATURAL], reduced, others):
            updates[n] = _adamw_halves("adamw_" + n, w2[n], r, o, m2[n], v2[n], cc)

    updates = {}

    early_views = grad_views(early_grads, _EARLY, _EARLY_PACKED)
    early_theirs = _swap_sibling("swap_early_grad_halves", early_views, other_half=True)
    small_sent, small_landed = _chips_wait("gather_small_wait", *small_sems[:4], EVERYONE, after=early_theirs[0])
    small_all = lax.dynamic_update_slice(small_landed[0], small_sent[0][None], (4 * xc + 2 * yc + cc, 0, 0))
    early_sems = _chips_start("scatter_early_start", pair_sums("early", early_views, early_theirs), PIECE,
                              after=small_all)
    late_pairs, late_landed = _chips_wait("scatter_late_wait", *rs_sems[:4], PIECE, after=early_sems[4])
    finish("late", late_pairs, late_landed, _LATE, _LATE_PACKED)
    early_pairs, early_landed = _chips_wait("scatter_early_wait", *early_sems[:4], PIECE,
                                            after=updates[_LATE[-1]][1])
    finish("early", early_pairs, early_landed, _EARLY, _EARLY_PACKED)

    small_sum = _unpack_small(_add_devices(small_all))
    small_names = [n for n in names if n in small_sum]
    small_grads = [lax.dynamic_index_in_dim(small_sum[n].reshape(3, 4, 1408), chip, 1, keepdims=False)
                   if n == "conv_w" else small_sum[n].reshape(w2[n].shape) for n in small_names]
    updates.update(zip(small_names, _adamw_many("adamw_small", [w2[n] for n in small_names], small_grads,
                                                [m2[n] for n in small_names], [v2[n] for n in small_names])))
    loss = small_sum["loss"][0]

    outs = [[updates[n][i].reshape(wts[n].shape) for n in names] for i in range(4)]
    return (loss, dx.reshape(x.shape), *outs[0], *outs[1], *outs[2], *outs[3])
```

```python
import math

import numpy as np
import jax
import jax.numpy as jnp
from jax import lax
from jax.experimental import pallas as pl
from jax.experimental.pallas import tpu as pltpu

F32 = jnp.float32
BF16 = jnp.bfloat16

D_MODEL = 1024
D_FF = 2816
PLE_DIM = 256
ROPE_THETA = 10000.0
RMS_EPS = 1e-6
SWA_WINDOW = 128
HEADS = 8
A_KV_HEADS = 2
A_HEAD_DIM = 64
KV_LORA = 128
NOPE_DIM = 64
ROPE_DIM = 32
V_DIM = 64
LANES = 128
ZW = 3328
NEG = -1e30
SCALE_A = A_HEAD_DIM ** -0.5
SCALE_B = (NOPE_DIM + ROPE_DIM) ** -0.5
LOG2E = math.log2(math.e)

ADAM_LR = 0.001
ADAM_B1 = 0.9
ADAM_B2 = 0.999
ADAM_EPS = 1e-08
ADAM_WD = 0.01
ADAM_STEP = 10

VMEM_LIMIT = 60 * 1024 * 1024
MESH = pl.DeviceIdType.MESH

Z_QA, Z_KA, Z_VA, Z_CQ, Z_CKV, Z_KR, Z_GATE = 0, 512, 640, 768, 1024, 1152, 1280


def _dot(a, b):
    return jnp.dot(a, b, preferred_element_type=F32)


def _dot_nt(a, b):
    return lax.dot_general(a, b, (((1,), (1,)), ((), ())), preferred_element_type=F32)


def _dot_tn(a, b):
    return lax.dot_general(a, b, (((0,), (0,)), ((), ())), preferred_element_type=F32)


def _rms_stats(x):
    r = lax.rsqrt(jnp.mean(x * x, axis=-1, keepdims=True) + RMS_EPS)
    return x * r, r


def _rms_bwd(dy, xn, r, g):
    dxn = dy * g
    dx = r * (dxn - xn * jnp.mean(dxn * xn, axis=-1, keepdims=True))
    dg = jnp.sum(dy * xn, axis=0, keepdims=True)
    return dx, dg


def _tile_lanes(t, n):
    return t if n == 1 else jnp.concatenate([t] * n, axis=1)


def _rope(x, c, s1, s2, half):
    w = x.shape[1]
    n = w // LANES
    return (x * _tile_lanes(c, n) + pltpu.roll(x, w - half, 1) * _tile_lanes(s1, n)
            + pltpu.roll(x, half, 1) * _tile_lanes(s2, n))


def _rope_t(dy, c, s1, s2, half):
    w = dy.shape[1]
    n = w // LANES
    return (dy * _tile_lanes(c, n) + pltpu.roll(dy * _tile_lanes(s1, n), half, 1)
            + pltpu.roll(dy * _tile_lanes(s2, n), w - half, 1))


def _fold_slots(d):
    tiles = []
    for j in range(d.shape[1] // (2 * LANES)):
        even = d[:, 2 * j * LANES:(2 * j + 1) * LANES]
        odd = d[:, (2 * j + 1) * LANES:(2 * j + 2) * LANES]
        tiles.append(even + pltpu.roll(odd, A_HEAD_DIM, 1))
    return tiles[0] if len(tiles) == 1 else jnp.concatenate(tiles, axis=1)


def _spread_slots(c):
    low = lax.broadcasted_iota(jnp.int32, (c.shape[0], LANES), 1) < A_HEAD_DIM
    slots = []
    for j in range(c.shape[1] // LANES):
        tile = c[:, j * LANES:(j + 1) * LANES]
        slots += [jnp.where(low, tile, 0.0), jnp.where(low, pltpu.roll(tile, A_HEAD_DIM, 1), 0.0)]
    return jnp.concatenate(slots, axis=1)


def _sigmoid(x):
    return 0.5 * jnp.tanh(0.5 * x) + 0.5


_GELU_C = math.sqrt(2.0 / math.pi)


def _gelu_and_grad(x):
    a = _GELU_C + (_GELU_C * 0.044715) * (x * x)
    th = jnp.tanh(x * a)
    hx = 0.5 * x
    p1 = 1.0 + th
    gel = hx * p1
    dgel = 0.5 * p1 + (hx * (1.0 - th * th)) * (3.0 * a - 2.0 * _GELU_C)
    return gel, dgel


def _conv_taps(up, h6, h7):
    r1 = pltpu.roll(up, 1, 0)
    r2 = pltpu.roll(up, 2, 0)
    rows = lax.broadcasted_iota(jnp.int32, (8, up.shape[1]), 0)
    xm1 = jnp.concatenate([jnp.where(rows == 0, h7, r1[0:8]), r1[8:]], axis=0)
    xm2 = jnp.concatenate([jnp.where(rows == 0, h6, jnp.where(rows == 1, h7, r2[0:8])), r2[8:]], axis=0)
    return xm1, xm2


def _conv_taps_next(du, n0, n1):
    tm = du.shape[0]
    r1 = pltpu.roll(du, tm - 1, 0)
    r2 = pltpu.roll(du, tm - 2, 0)
    rows = lax.broadcasted_iota(jnp.int32, (8, du.shape[1]), 0)
    xp1 = jnp.concatenate([r1[:tm - 8], jnp.where(rows == 7, n0, r1[tm - 8:])], axis=0)
    xp2 = jnp.concatenate([r2[:tm - 8], jnp.where(rows == 6, n0, jnp.where(rows == 7, n1, r2[tm - 8:]))], axis=0)
    return xp1, xp2


def _row(tm, n):
    return pl.BlockSpec((tm, n), lambda i: (i, 0))


def _full(shape):
    nd = len(shape)
    return pl.BlockSpec(tuple(shape), lambda i: (0,) * nd)


def _resident(shape):
    nd = len(shape)
    return pl.BlockSpec(tuple(shape), lambda i: (0,) * nd, pipeline_mode=pl.Buffered(1))


def _heads(tm, h):
    return pl.BlockSpec((h, tm, LANES), lambda i: (0, i, 0))


def _rows_call(name, body, t_rows, tm, ins, outs, scratch=()):
    return pl.pallas_call(
        body, name=name, grid=(t_rows // tm,),
        in_specs=[s for _, s in ins],
        out_specs=[s for _, s in outs],
        out_shape=[s for s, _ in outs],
        scratch_shapes=list(scratch),
        compiler_params=pltpu.CompilerParams(dimension_semantics=("arbitrary",), vmem_limit_bytes=VMEM_LIMIT),
    )(*[a for a, _ in ins])


def _sds(shape, dtype):
    return jax.ShapeDtypeStruct(tuple(shape), dtype)


def _rope_consts():
    c = np.zeros((16, LANES), np.float32)
    lane = np.arange(LANES)
    inv_a = (ROPE_THETA ** (-(np.arange(0, A_HEAD_DIM, 2, dtype=np.float32) / A_HEAD_DIM))).astype(np.float32)
    in_a = lane < A_HEAD_DIM
    c[0, in_a] = inv_a[lane[in_a] % (A_HEAD_DIM // 2)]
    c[1, in_a] = 1.0
    c[2, lane < A_HEAD_DIM // 2] = -1.0
    c[3, (lane >= A_HEAD_DIM // 2) & in_a] = 1.0
    inv_b = (ROPE_THETA ** (-(np.arange(0, ROPE_DIM, 2, dtype=np.float32) / ROPE_DIM))).astype(np.float32)
    pe = (lane >= NOPE_DIM) & (lane < NOPE_DIM + ROPE_DIM)
    c[5, pe] = inv_b[(lane[pe] - NOPE_DIM) % (ROPE_DIM // 2)]
    c[6, pe] = 1.0
    c[7, (lane >= NOPE_DIM) & (lane < NOPE_DIM + ROPE_DIM // 2)] = -1.0
    c[8, (lane >= NOPE_DIM + ROPE_DIM // 2) & (lane < NOPE_DIM + ROPE_DIM)] = 1.0
    c[9, lane < NOPE_DIM] = 1.0
    c[10, pe] = 1.0
    return jnp.asarray(c)


def _rope_tables(pos_f, consts, tm):
    t_rows = pos_f.shape[0]

    def body(pos_ref, c_ref, ca, sa1, sa2, cb, sb1, sb2):
        ang = pos_ref[...] * (c_ref[0:1, :] + c_ref[5:6, :])
        cs, sn = jnp.cos(ang), jnp.sin(ang)
        for ref, row in ((ca, 1), (sa1, 2), (sa2, 3)):
            half = (cs if row == 1 else sn) * c_ref[row:row + 1, :]
            ref[...] = half + pltpu.roll(half, A_HEAD_DIM, 1)
        cb[...] = cs * c_ref[6:7, :] + c_ref[9:10, :]
        sb1[...] = sn * c_ref[7:8, :]
        sb2[...] = sn * c_ref[8:9, :]

    tab = (_sds((t_rows, LANES), F32), _row(tm, LANES))
    return _rows_call("rope_tables", body, t_rows, tm,
                      [(pos_f, _row(tm, 1)), (consts, _full(consts.shape))], [tab] * 6)


def _fwd_in(x, g1, win, bg, gq, gkv, wuq, wk, wv, eq, ek, tabs, tm):
    t_rows = x.shape[0]

    def body(x_ref, g1_ref, win_ref, bg_ref, gq_ref, gkv_ref, wuq_ref, wk_ref, wv_ref, eq_ref, ek_ref,
             ca, sa1, sa2, cb, sb1, sb2,
             h1_ref, qs_ref, ks_ref, vs_ref, cq_ref, cqn_ref, ckv_ref, ckvn_ref, qm_ref, km_ref, vm_ref, gate_ref):
        xn, _ = _rms_stats(x_ref[...])
        hb = (xn * g1_ref[...]).astype(BF16)
        h1_ref[...] = hb
        ta = (ca[...], sa1[...], sa2[...])
        tb = (cb[...], sb1[...], sb2[...])
        cq = _dot(hb, win_ref[:, Z_CQ:Z_CKV])
        ckv = _dot(hb, win_ref[:, Z_CKV:Z_KR])
        z_qa = _dot(hb, win_ref[:, Z_QA:Z_KA])
        z_ka = _dot(hb, win_ref[:, Z_KA:Z_VA])
        z_va = _dot(hb, win_ref[:, Z_VA:Z_CQ])
        z_kr = _dot(hb, win_ref[:, Z_KR:Z_GATE])
        cq_ref[...] = cq
        cqn, _ = _rms_stats(cq)
        cqb = (cqn * gq_ref[...]).astype(BF16)
        cqn_ref[...] = cqb
        ckv_ref[...] = ckv
        ckvn, _ = _rms_stats(ckv)
        ckvb = (ckvn * gkv_ref[...]).astype(BF16)
        ckvn_ref[...] = ckvb
        z_qm = _dot(cqb, wuq_ref[...])
        z_km = _dot(ckvb, wk_ref[...])
        z_vm = _dot(ckvb, wv_ref[...])
        z_gate = _dot(hb, win_ref[:, Z_GATE:ZW])
        qs_ref[...] = _dot((_rope(z_qa, *ta, A_HEAD_DIM // 2) * SCALE_A).astype(BF16), eq_ref[...]).astype(BF16)
        ks_ref[...] = _dot(_rope(z_ka, *ta, A_HEAD_DIM // 2).astype(BF16), ek_ref[...]).astype(BF16)
        vs_ref[...] = _dot(z_va.astype(BF16), ek_ref[...]).astype(BF16)
        qm_ref[...] = (_rope(z_qm, *tb, ROPE_DIM // 2) * (SCALE_B * LOG2E)).astype(BF16)
        km_ref[...] = (z_km + _tile_lanes(_rope(z_kr, *tb, ROPE_DIM // 2), HEADS)).astype(BF16)
        vm_ref[...] = z_vm.astype(BF16)
        gate_ref[...] = _sigmoid(z_gate + bg_ref[...]).astype(BF16)

    def o(n, dt):
        return (_sds((t_rows, n), dt), _row(tm, n))

    ins = [(x, _row(tm, D_MODEL)), (g1, _full(g1.shape)), (win, _resident(win.shape)), (bg, _full(bg.shape)),
           (gq, _full(gq.shape)), (gkv, _full(gkv.shape)), (wuq, _full(wuq.shape)), (wk, _full(wk.shape)),
           (wv, _full(wv.shape)), (eq, _full(eq.shape)), (ek, _full(ek.shape))] + [(t, _row(tm, LANES)) for t in tabs]
    outs = [o(1024, BF16), o(1024, BF16), o(256, BF16), o(256, BF16), o(256, F32), o(256, BF16), o(128, F32),
            o(128, BF16), o(1024, BF16), o(1024, BF16), o(1024, BF16), o(2048, BF16)]
    return _rows_call("fwd_in", body, t_rows, tm, ins, outs)


def _attn_tile(t_rows):
    return min(512, t_rows)


MLA_HEADS_PER_STEP = 4
MLA_FWD_HEADS_PER_STEP = 8


def _causal_pairs(nq, by_kv):
    if by_kv:
        pairs = [(i, j) for j in range(nq) for i in range(j, nq)]
    else:
        pairs = [(i, j) for i in range(nq) for j in range(i + 1)]
    return (jnp.asarray([p[0] for p in pairs], jnp.int32), jnp.asarray([p[1] for p in pairs], jnp.int32))


def _mla_fwd(q, k, v):
    t_rows = q.shape[0]
    t = _attn_tile(t_rows)
    hp = MLA_FWD_HEADS_PER_STEP
    w = hp * LANES
    ii, jj = _causal_pairs(t_rows // t, by_kv=False)

    def body(i_ref, j_ref, q_ref, k_ref, v_ref, o_ref, lse_ref, m_s, l_s, acc_s):
        i = i_ref[pl.program_id(1)]
        j = j_ref[pl.program_id(1)]

        @pl.when(j == 0)
        def _():
            m_s[...] = jnp.full(m_s.shape, NEG, F32)
            l_s[...] = jnp.zeros(l_s.shape, F32)
            acc_s[...] = jnp.zeros(acc_s.shape, F32)

        def step(diagonal):
            sls = [slice(hh * LANES, (hh + 1) * LANES) for hh in range(hp)]
            scores = [_dot_nt(k_ref[:, sl], q_ref[:, sl]) for sl in sls]
            if diagonal:
                valid = (lax.broadcasted_iota(jnp.int32, (t, t), 0) <= lax.broadcasted_iota(jnp.int32, (t, t), 1))
                scores = [jnp.where(valid, s, NEG) for s in scores]
            stats = []
            for hh, s in enumerate(scores):
                m_prev = m_s[hh]
                m_new = jnp.maximum(m_prev, jnp.max(s, axis=0, keepdims=True))
                p = jnp.exp2(s - m_new)
                alpha = jnp.exp2(m_prev - m_new)
                stats.append((m_new, alpha, alpha * l_s[hh] + jnp.sum(p, axis=0, keepdims=True), p.astype(BF16)))
            for hh, (m_new, alpha, l_new, p) in enumerate(stats):
                sl = sls[hh]
                acc = alpha * acc_s[hh] + _dot_tn(v_ref[:, sl], p)
                if diagonal:
                    o_ref[:, sl] = (acc / l_new).T.astype(o_ref.dtype)
                    lse_ref[hh] = m_new + jnp.log2(l_new)
                else:
                    m_s[hh] = m_new
                    l_s[hh] = l_new
                    acc_s[hh] = acc

        pl.when(j < i)(lambda: step(False))
        pl.when(j == i)(lambda: step(True))

    grid_spec = pltpu.PrefetchScalarGridSpec(
        num_scalar_prefetch=2, grid=(HEADS // hp, ii.shape[0]),
        in_specs=[pl.BlockSpec((t, w), lambda hb, s, ir, jr: (ir[s], hb)),
                  pl.BlockSpec((t, w), lambda hb, s, ir, jr: (jr[s], hb)),
                  pl.BlockSpec((t, w), lambda hb, s, ir, jr: (jr[s], hb))],
        out_specs=[pl.BlockSpec((t, w), lambda hb, s, ir, jr: (ir[s], hb)),
                   pl.BlockSpec((hp, 1, t), lambda hb, s, ir, jr: (hb, 0, ir[s]))],
        scratch_shapes=[pltpu.VMEM((hp, 1, t), F32), pltpu.VMEM((hp, 1, t), F32), pltpu.VMEM((hp, LANES, t), F32)])
    return pl.pallas_call(
        body, name="mla_fwd", grid_spec=grid_spec,
        out_shape=[_sds((t_rows, HEADS * LANES), BF16), _sds((HEADS, 1, t_rows), F32)],
        compiler_params=pltpu.CompilerParams(dimension_semantics=("arbitrary",) * 2, vmem_limit_bytes=VMEM_LIMIT),
    )(ii, jj, q, k, v)


def _mla_bwd(q, k, v, do, lse, delta):
    t_rows = q.shape[0]
    t = _attn_tile(t_rows)
    hp = MLA_HEADS_PER_STEP
    w = hp * LANES
    ii, jj = _causal_pairs(t_rows // t, by_kv=True)

    def body(i_ref, j_ref, q_ref, k_ref, v_ref, do_ref, lse_ref, dl_ref, dq_ref, dk_ref, dv_ref):
        i = i_ref[pl.program_id(1)]
        j = j_ref[pl.program_id(1)]

        @pl.when(pl.program_id(1) == 0)
        def _():
            dq_ref[...] = jnp.zeros(dq_ref.shape, F32)

        def step(diagonal):
            r0 = pl.multiple_of(i * t, t)
            sls = [slice(hh * LANES, (hh + 1) * LANES) for hh in range(hp)]
            scores = [_dot_nt(k_ref[:, sl], q_ref[:, sl]) for sl in sls]
            if diagonal:
                valid = (lax.broadcasted_iota(jnp.int32, (t, t), 0) <= lax.broadcasted_iota(jnp.int32, (t, t), 1))
                scores = [jnp.where(valid, s, NEG) for s in scores]
            dps = [_dot_nt(v_ref[:, sl], do_ref[:, sl]) for sl in sls]
            ps = [jnp.exp2(s - lse_ref[hh]) for hh, s in enumerate(scores)]
            dss = [(p * (dp - dl_ref[hh])).astype(BF16) for hh, (p, dp) in enumerate(zip(ps, dps))]
            for hh, sl in enumerate(sls):
                dv = _dot(ps[hh].astype(BF16), do_ref[:, sl])
                dk = _dot(dss[hh], q_ref[:, sl]) * (1.0 / LOG2E)
                if diagonal:
                    dv_ref[:, sl] = dv
                    dk_ref[:, sl] = dk
                else:
                    dv_ref[:, sl] += dv
                    dk_ref[:, sl] += dk
                dq_ref[hh, pl.ds(r0, t), :] += _dot_tn(dss[hh], k_ref[:, sl])

        pl.when(i > j)(lambda: step(False))
        pl.when(i == j)(lambda: step(True))

    def qmap(hb, s, ir, jr):
        return (ir[s], hb)

    def kvmap(hb, s, ir, jr):
        return (jr[s], hb)

    def rowmap(hb, s, ir, jr):
        return (hb, 0, ir[s])

    grid_spec = pltpu.PrefetchScalarGridSpec(
        num_scalar_prefetch=2, grid=(HEADS // hp, ii.shape[0]),
        in_specs=[pl.BlockSpec((t, w), qmap), pl.BlockSpec((t, w), kvmap), pl.BlockSpec((t, w), kvmap),
                  pl.BlockSpec((t, w), qmap), pl.BlockSpec((hp, 1, t), rowmap), pl.BlockSpec((hp, 1, t), rowmap)],
        out_specs=[pl.BlockSpec((hp, t_rows, LANES), lambda hb, s, ir, jr: (hb, 0, 0)),
                   pl.BlockSpec((t, w), kvmap), pl.BlockSpec((t, w), kvmap)])
    return pl.pallas_call(
        body, name="mla_bwd", grid_spec=grid_spec,
        out_shape=[_sds((HEADS, t_rows, LANES), F32), _sds((t_rows, HEADS * LANES), F32),
                   _sds((t_rows, HEADS * LANES), F32)],
        compiler_params=pltpu.CompilerParams(dimension_semantics=("arbitrary",) * 2, vmem_limit_bytes=VMEM_LIMIT),
    )(ii, jj, q, k, v, do, lse, delta)


SWA_TILE = 2 * SWA_WINDOW
SWA_GROUP = HEADS // A_KV_HEADS


def _swa_bias(tq):
    koff = lax.broadcasted_iota(jnp.int32, (tq + SWA_WINDOW, SWA_GROUP * tq), 0) - SWA_WINDOW
    qoff = (lax.broadcasted_iota(jnp.int32, (tq + SWA_WINDOW, SWA_GROUP * tq), 1) % tq)
    band = (koff <= qoff) & (qoff - koff < SWA_WINDOW)
    return jnp.stack([jnp.where(band & (koff >= 0), 0.0, NEG), jnp.where(band, 0.0, NEG)]).astype(F32)


def _swa_specs(tq, nq):
    wb = tq // SWA_WINDOW
    kvw = A_KV_HEADS * LANES

    def qi(i):
        return jnp.minimum(i, nq - 1)

    q = pl.BlockSpec((tq, HEADS * LANES), lambda i: (qi(i), 0))
    cur = pl.BlockSpec((tq, kvw), lambda i: (qi(i), 0))
    prev = pl.BlockSpec((SWA_WINDOW, kvw), lambda i: (jnp.maximum(qi(i) * wb - 1, 0), 0))
    bias = pl.BlockSpec((1, tq + SWA_WINDOW, SWA_GROUP * tq), lambda i: (jnp.minimum(i, 1), 0, 0))
    rows = pl.BlockSpec((A_KV_HEADS, 1, 1, SWA_GROUP * tq), lambda i: (0, qi(i), 0, 0))
    sink = pl.BlockSpec((A_KV_HEADS, 1, SWA_GROUP * tq), lambda i: (0, 0, 0))
    return q, cur, prev, bias, rows, sink


def _stack_heads(ref, kvh):
    base = kvh * SWA_GROUP
    return jnp.concatenate([ref[:, (base + g) * LANES:(base + g + 1) * LANES] for g in range(SWA_GROUP)], axis=0)


def _unstack_heads(ref, kvh, val, tq):
    base = kvh * SWA_GROUP
    for g in range(SWA_GROUP):
        ref[:, (base + g) * LANES:(base + g + 1) * LANES] = val[g * tq:(g + 1) * tq].astype(ref.dtype)


def _kv_window(prev_ref, cur_ref, kvh):
    sl = slice(kvh * LANES, (kvh + 1) * LANES)
    return jnp.concatenate([prev_ref[:, sl], cur_ref[:, sl]], axis=0)


def _swa_fwd(q, k, v, bias, sink_rows):
    t_rows = q.shape[0]
    tq = min(SWA_TILE, t_rows)
    nq = t_rows // tq
    qs_, cur, prev, bs, rows, sk = _swa_specs(tq, nq)
    kvhs = range(A_KV_HEADS)

    def body(q_ref, kc_ref, kp_ref, vc_ref, vp_ref, b_ref, sink_ref, o_ref, lse_ref):
        scores = [_dot_nt(_kv_window(kp_ref, kc_ref, h), _stack_heads(q_ref, h)) + b_ref[0] for h in kvhs]
        stats = []
        for h, s in zip(kvhs, scores):
            sink = sink_ref[h]
            m = jnp.maximum(jnp.max(s, axis=0, keepdims=True), sink)
            p = jnp.exp(s - m)
            l = jnp.sum(p, axis=0, keepdims=True) + jnp.exp(sink - m)
            lse_ref[h, 0] = m + jnp.log(l)
            stats.append((p.astype(BF16), l))
        for h, (p, l) in zip(kvhs, stats):
            _unstack_heads(o_ref, h, (_dot_tn(_kv_window(vp_ref, vc_ref, h), p) / l).T, tq)

    return pl.pallas_call(
        body, name="swa_fwd", grid=(nq,),
        in_specs=[qs_, cur, prev, cur, prev, bs, sk],
        out_specs=[qs_, rows],
        out_shape=[_sds((t_rows, HEADS * LANES), BF16), _sds((A_KV_HEADS, nq, 1, SWA_GROUP * tq), F32)],
        compiler_params=pltpu.CompilerParams(dimension_semantics=("arbitrary",), vmem_limit_bytes=VMEM_LIMIT),
    )(q, k, k, v, v, bias, sink_rows)


def _swa_bwd(q, k, v, o, do, lse, bias, sink_rows):
    t_rows = q.shape[0]
    tq = min(SWA_TILE, t_rows)
    nq = t_rows // tq
    qs_, cur, prev, bs, rows, sk = _swa_specs(tq, nq)
    hw = SWA_WINDOW
    kvhs = range(A_KV_HEADS)
    kvw = A_KV_HEADS * LANES

    def body(q_ref, kc_ref, kp_ref, vc_ref, vp_ref, o_ref, do_ref, lse_ref, b_ref, sink_ref,
             dq_ref, dk_ref, dv_ref, dsink_ref, ck, cv, dsa):
        i = pl.program_id(0)

        @pl.when(i == 0)
        def _():
            dsa[...] = jnp.zeros(dsa.shape, F32)

        @pl.when(i < nq)
        def _():
            qs = [_stack_heads(q_ref, h) for h in kvhs]
            dos = [_stack_heads(do_ref, h) for h in kvhs]
            kks = [_kv_window(kp_ref, kc_ref, h) for h in kvhs]
            scores = [_dot_nt(kks[h], qs[h]) for h in kvhs]
            dps = [_dot_nt(_kv_window(vp_ref, vc_ref, h), dos[h]) for h in kvhs]
            ps, dss = [], []
            for h in kvhs:
                lse = lse_ref[h, 0]
                p = jnp.exp(scores[h] + b_ref[0] - lse)
                delta = jnp.sum((_stack_heads(o_ref, h).astype(F32) * dos[h].astype(F32)).T, axis=0, keepdims=True)
                dsa[h] += -jnp.exp(sink_ref[h] - lse) * delta
                ps.append(p.astype(BF16))
                dss.append((p * (dps[h] - delta)).astype(BF16))
            for h in kvhs:
                sl = slice(h * LANES, (h + 1) * LANES)
                dv = _dot(ps[h], dos[h])
                dk = _dot(dss[h], qs[h])
                _unstack_heads(dq_ref, h, _dot_tn(dss[h], kks[h]), tq)

                @pl.when(i > 0)
                def _():
                    dk_ref[0:tq - hw, sl] = ck[0:tq - hw, sl]
                    dk_ref[tq - hw:tq, sl] = ck[tq - hw:tq, sl] + dk[0:hw]
                    dv_ref[0:tq - hw, sl] = cv[0:tq - hw, sl]
                    dv_ref[tq - hw:tq, sl] = cv[tq - hw:tq, sl] + dv[0:hw]

                ck[:, sl] = dk[hw:hw + tq]
                cv[:, sl] = dv[hw:hw + tq]

        @pl.when(i == nq)
        def _():
            dk_ref[...] = ck[...]
            dv_ref[...] = cv[...]
            dsink_ref[...] = jnp.zeros(dsink_ref.shape, F32)
            for h in kvhs:
                for g in range(SWA_GROUP):
                    tot = jnp.sum(dsa[h, :, g * tq:(g + 1) * tq], axis=1, keepdims=True)
                    dsink_ref[h, g:g + 1, :] = jnp.zeros((1, LANES), F32) + tot

    kv_out = pl.BlockSpec((tq, kvw), lambda i: (jnp.maximum(i - 1, 0), 0))
    return pl.pallas_call(
        body, name="swa_bwd", grid=(nq + 1,),
        in_specs=[qs_, cur, prev, cur, prev, qs_, qs_, rows, bs, sk],
        out_specs=[qs_, kv_out, kv_out, pl.BlockSpec((A_KV_HEADS, 8, LANES), lambda i: (0, 0, 0))],
        out_shape=[_sds((t_rows, HEADS * LANES), F32), _sds((t_rows, kvw), F32), _sds((t_rows, kvw), F32),
                   _sds((A_KV_HEADS, 8, LANES), F32)],
        scratch_shapes=[pltpu.VMEM((tq, kvw), F32), pltpu.VMEM((tq, kvw), F32),
                        pltpu.VMEM((A_KV_HEADS, 1, SWA_GROUP * tq), F32)],
        compiler_params=pltpu.CompilerParams(dimension_semantics=("arbitrary",), vmem_limit_bytes=VMEM_LIMIT),
    )(q, k, k, v, v, o, do, lse, bias, sink_rows)


def _fwd_mix(x, ya, yb, gate, wba, wbb, wout, g2, g3, tm):
    t_rows = x.shape[0]

    def body(x_ref, ya_ref, yb_ref, gate_ref, wba_ref, wbb_ref, wout_ref, g2_ref, g3_ref,
             pa_ref, pb_ref, mixed_ref, o_ref, x1_ref, h2_ref, yac_ref, ybc_ref):
        yac = _fold_slots(ya_ref[...].astype(F32)).astype(BF16)
        ybc = _fold_slots(yb_ref[...].astype(F32)).astype(BF16)
        yac_ref[...] = yac
        ybc_ref[...] = ybc
        pa = _dot(yac, wba_ref[...])
        pb = _dot(ybc, wbb_ref[...])
        pa_ref[...] = pa.astype(BF16)
        pb_ref[...] = pb.astype(BF16)
        mixed = (gate_ref[:, 0:D_MODEL].astype(F32) * pa
                 + gate_ref[:, D_MODEL:2 * D_MODEL].astype(F32) * pb).astype(BF16)
        mixed_ref[...] = mixed
        o = _dot(mixed, wout_ref[...])
        o_ref[...] = o
        on, _ = _rms_stats(o)
        x1 = x_ref[...] + on * g2_ref[...]
        x1_ref[...] = x1
        x1n, _ = _rms_stats(x1)
        h2_ref[...] = (x1n * g3_ref[...]).astype(BF16)

    def o_(dt):
        return (_sds((t_rows, D_MODEL), dt), _row(tm, D_MODEL))

    ins = [(x, _row(tm, D_MODEL)), (ya, _row(tm, 1024)), (yb, _row(tm, 1024)), (gate, _row(tm, 2048)),
           (wba, _resident(wba.shape)), (wbb, _resident(wbb.shape)), (wout, _resident(wout.shape)),
           (g2, _full(g2.shape)), (g3, _full(g3.shape))]
    half = (_sds((t_rows, D_MODEL // 2), BF16), _row(tm, D_MODEL // 2))
    return _rows_call("fwd_mix", body, t_rows, tm, ins,
                      [o_(BF16), o_(BF16), o_(BF16), o_(F32), o_(F32), o_(BF16), half, half])


CONV_CHUNK = 1408


def _fwd_up(h2, wup, convw8, convb, tm):
    t_rows = h2.shape[0]
    cdim = 2 * D_FF

    def body(h2_ref, wup_ref, cw_ref, cb_ref, up_ref, a_ref, u_ref, carry):
        i = pl.program_id(0)

        @pl.when(i == 0)
        def _():
            carry[...] = jnp.zeros(carry.shape, F32)

        hb = h2_ref[...]
        ups = [_dot(hb, wup_ref[s]) for s in range(cdim // CONV_CHUNK)]

        def conv(c0):
            sl = slice(c0, c0 + CONV_CHUNK)
            up = ups[c0 // CONV_CHUNK]
            up_ref[:, sl] = up
            xm1, xm2 = _conv_taps(up, carry[6:7, sl], carry[7:8, sl])
            u = cw_ref[0:1, sl] * xm2 + cw_ref[1:2, sl] * xm1 + cw_ref[2:3, sl] * up + cb_ref[:, sl]
            u_ref[:, sl] = u.astype(BF16)
            carry[:, sl] = up[tm - 8:tm, :]
            return u

        for c0 in range(0, D_FF, CONV_CHUNK):
            ug = conv(c0)
            uv = conv(D_FF + c0)
            gel, _ = _gelu_and_grad(ug)
            a_ref[:, c0:c0 + CONV_CHUNK] = (gel * uv).astype(BF16)

    ins = [(h2, _row(tm, D_MODEL)), (wup, _resident(wup.shape)), (convw8, _full(convw8.shape)),
           (convb, _full(convb.shape))]
    outs = [(_sds((t_rows, cdim), F32), _row(tm, cdim)), (_sds((t_rows, D_FF), BF16), _row(tm, D_FF)),
            (_sds((t_rows, cdim), BF16), _row(tm, cdim))]
    return _rows_call("fwd_up", body, t_rows, tm, ins, outs, scratch=[pltpu.VMEM((8, cdim), F32)])


def _fwd_out(a, wdown, x1, g4, p, wple, g5, wpg, tgt, tm):
    t_rows = a.shape[0]

    def body(a_ref, wdown_ref, x1_ref, g4_ref, p_ref, wple_ref, g5_ref, wpg_ref, tgt_ref,
             ff_ref, x2_ref, e_ref, n5_ref, sg_ref, dx3_ref, loss_ref):
        i = pl.program_id(0)
        ff = _dot(a_ref[...], wdown_ref[...])
        e = _dot(p_ref[...].astype(BF16), wple_ref[...])
        ff_ref[...] = ff
        ffn, _ = _rms_stats(ff)
        x2 = x1_ref[...] + ffn * g4_ref[...]
        x2_ref[...] = x2
        e_ref[...] = e.astype(BF16)
        x2n, _ = _rms_stats(x2)
        n5 = (x2n * g5_ref[...]).astype(BF16)
        n5_ref[...] = n5
        sg = _sigmoid(_dot(n5, wpg_ref[...]))
        sg_ref[...] = sg.astype(BF16)
        d = x2 + sg * e - tgt_ref[...]
        dx3_ref[...] = d * (1.0 / D_MODEL)

        @pl.when(i == 0)
        def _():
            loss_ref[...] = jnp.zeros((1, 1), F32)

        loss_ref[...] += 0.5 * jnp.sum(jnp.sum(d * d, axis=1, keepdims=True), axis=0, keepdims=True) * (1.0 / D_MODEL)

    def o_(dt):
        return (_sds((t_rows, D_MODEL), dt), _row(tm, D_MODEL))

    ins = [(a, _row(tm, D_FF)), (wdown, _resident(wdown.shape)), (x1, _row(tm, D_MODEL)), (g4, _full(g4.shape)),
           (p, _row(tm, PLE_DIM)), (wple, _full(wple.shape)), (g5, _full(g5.shape)), (wpg, _resident(wpg.shape)),
           (tgt, _row(tm, D_MODEL))]
    outs = [o_(F32), o_(F32), o_(BF16), o_(BF16), o_(BF16), o_(F32), (_sds((1, 1), F32), _full((1, 1)))]
    return _rows_call("fwd_out", body, t_rows, tm, ins, outs)


def _bwd_out(dx3, e, sg, x2, ff, g5, g4, wpg, wdown, up, u, tm):
    t_rows = dx3.shape[0]
    cdim = 2 * D_FF
    hb = tm // 8

    def body(dx3_ref, e_ref, sg_ref, x2_ref, ff_ref, g5_ref, g4_ref, wpg_ref, wdown_ref, up_ref, halo_ref, u_ref,
             dpre_ref, de_ref, dx2_ref, dff_ref, du_ref, dg5_ref, dg4_ref, dcb_ref, dcw_ref):
        i = pl.program_id(0)

        @pl.when(i == 0)
        def _():
            dg5_ref[...] = jnp.zeros(dg5_ref.shape, F32)
            dg4_ref[...] = jnp.zeros(dg4_ref.shape, F32)
            dcb_ref[...] = jnp.zeros(dcb_ref.shape, F32)
            dcw_ref[...] = jnp.zeros(dcw_ref.shape, F32)

        dx3 = dx3_ref[...]
        sg = sg_ref[...].astype(F32)
        dpre = (dx3 * e_ref[...].astype(F32) * sg * (1.0 - sg)).astype(BF16)
        dpre_ref[...] = dpre
        de_ref[...] = (dx3 * sg).astype(BF16)
        dn5 = _dot_nt(dpre, wpg_ref[...])
        x2n, r5 = _rms_stats(x2_ref[...])
        d2, dg5 = _rms_bwd(dn5, x2n, r5, g5_ref[...])
        dx2 = dx3 + d2
        dx2_ref[...] = dx2
        dg5_ref[...] += dg5
        ffn, r4 = _rms_stats(ff_ref[...])
        dff, dg4 = _rms_bwd(dx2, ffn, r4, g4_ref[...])
        dg4_ref[...] += dg4
        dffb = dff.astype(BF16)
        dff_ref[...] = dffb
        keep = jnp.where(i > 0, 1.0, 0.0)

        def conv(c0):
            sl = slice(c0, c0 + CONV_CHUNK)
            up = up_ref[:, sl]
            xm1, xm2 = _conv_taps(up, halo_ref[6:7, sl] * keep, halo_ref[7:8, sl] * keep)
            return u_ref[:, sl].astype(F32), up, xm1, xm2

        def grads(c0, du, up, xm1, xm2):
            sl = slice(c0, c0 + CONV_CHUNK)
            du_ref[:, sl] = du.astype(BF16)
            dcb_ref[:, sl] += jnp.sum(du, axis=0, keepdims=True)
            dcw_ref[0:1, sl] += jnp.sum(du * xm2, axis=0, keepdims=True)
            dcw_ref[1:2, sl] += jnp.sum(du * xm1, axis=0, keepdims=True)
            dcw_ref[2:3, sl] += jnp.sum(du * up, axis=0, keepdims=True)

        for c0 in range(0, D_FF, CONV_CHUNK):
            da = _dot_nt(dffb, wdown_ref[c0:c0 + CONV_CHUNK, :])
            ug, *rg = conv(c0)
            uv, *rv = conv(D_FF + c0)
            gel, dgel = _gelu_and_grad(ug)
            grads(c0, da * uv * dgel, *rg)
            grads(D_FF + c0, da * gel, *rv)

    def o_(n, dt):
        return (_sds((t_rows, n), dt), _row(tm, n))

    def acc(r, n):
        return (_sds((r, n), F32), _full((r, n)))

    halo = pl.BlockSpec((8, cdim), lambda i: (jnp.maximum(i * hb - 1, 0), 0))
    ins = [(dx3, _row(tm, D_MODEL)), (e, _row(tm, D_MODEL)), (sg, _row(tm, D_MODEL)), (x2, _row(tm, D_MODEL)),
           (ff, _row(tm, D_MODEL)), (g5, _full(g5.shape)), (g4, _full(g4.shape)), (wpg, _resident(wpg.shape)),
           (wdown, _resident(wdown.shape)), (up, _row(tm, cdim)), (up, halo), (u, _row(tm, cdim))]
    outs = [o_(D_MODEL, BF16), o_(D_MODEL, BF16), o_(D_MODEL, F32), o_(D_MODEL, BF16), o_(cdim, BF16),
            acc(1, D_MODEL), acc(1, D_MODEL), acc(1, cdim), acc(8, cdim)]
    return _rows_call("bwd_out", body, t_rows, tm, ins, outs)


def _bwd_mid(du, convw8, wup, dx2, x1, g3, o, g2, wout, gate, pa, pb, wba, wbb, yb, tm):
    t_rows = du.shape[0]
    cdim = 2 * D_FF
    halo_rows = 16
    hb = tm // halo_rows
    last_blk = t_rows // halo_rows - 1
    n_tiles = t_rows // tm

    def body(du_ref, halo_ref, cw_ref, wup_ref, dx2_ref, x1_ref, g3_ref, o_ref, g2_ref, wout_ref, gate_ref, pa_ref,
             pb_ref, wba_ref, wbb_ref, yb_ref,
             dup_ref, dx1_ref, do_ref, dpa_ref, dpb_ref, dgt_ref, dya_ref, dyb_ref, dl_ref, dg3_ref, dg2_ref, dbg_ref):
        i = pl.program_id(0)

        @pl.when(i == 0)
        def _():
            dg3_ref[...] = jnp.zeros(dg3_ref.shape, F32)
            dg2_ref[...] = jnp.zeros(dg2_ref.shape, F32)
            dbg_ref[...] = jnp.zeros(dbg_ref.shape, F32)

        keep = jnp.where(i < n_tiles - 1, 1.0, 0.0)
        dh2 = jnp.zeros((tm, D_MODEL), F32)
        dups = []
        for c0 in range(0, cdim, CONV_CHUNK):
            sl = slice(c0, c0 + CONV_CHUNK)
            du = du_ref[:, sl].astype(F32)
            nxt = halo_ref[:, sl].astype(F32)
            xp1, xp2 = _conv_taps_next(du, nxt[0:1] * keep, nxt[1:2] * keep)
            dups.append((cw_ref[2:3, sl] * du + cw_ref[1:2, sl] * xp1 + cw_ref[0:1, sl] * xp2).astype(BF16))
            dup_ref[:, sl] = dups[-1]
            if len(dups) > 1:
                dh2 = dh2 + _dot_nt(dups[-2], wup_ref[len(dups) - 2])
        dh2 = dh2 + _dot_nt(dups[-1], wup_ref[len(dups) - 1])
        x1n, r3 = _rms_stats(x1_ref[...])
        d1, dg3 = _rms_bwd(dh2, x1n, r3, g3_ref[...])
        dx1 = dx2_ref[...] + d1
        dx1_ref[...] = dx1
        dg3_ref[...] += dg3
        on, r2 = _rms_stats(o_ref[...])
        do, dg2 = _rms_bwd(dx1, on, r2, g2_ref[...])
        dg2_ref[...] += dg2
        dob = do.astype(BF16)
        do_ref[...] = dob
        dmixed = _dot_nt(dob, wout_ref[...])
        ga = gate_ref[:, 0:D_MODEL].astype(F32)
        gb = gate_ref[:, D_MODEL:2 * D_MODEL].astype(F32)
        dpa = (dmixed * ga).astype(BF16)
        dpb = (dmixed * gb).astype(BF16)
        dpa_ref[...] = dpa
        dpb_ref[...] = dpb
        dga = dmixed * pa_ref[...].astype(F32) * ga * (1.0 - ga)
        dgb = dmixed * pb_ref[...].astype(F32) * gb * (1.0 - gb)
        dgt_ref[:, 0:D_MODEL] = dga.astype(BF16)
        dgt_ref[:, D_MODEL:2 * D_MODEL] = dgb.astype(BF16)
        dbg_ref[:, 0:D_MODEL] += jnp.sum(dga, axis=0, keepdims=True)
        dbg_ref[:, D_MODEL:2 * D_MODEL] += jnp.sum(dgb, axis=0, keepdims=True)
        dya_ref[...] = _spread_slots(_dot_nt(dpa, wba_ref[...])).astype(BF16)
        dyb = _dot_nt(dpb, wbb_ref[...]).astype(BF16)
        dyb_ref[...] = _spread_slots(dyb.astype(F32)).astype(BF16)
        prod = yb_ref[...].astype(F32) * dyb.astype(F32)
        width = HEADS * V_DIM
        lane_head = lax.broadcasted_iota(jnp.int32, (HEADS, width), 1) // V_DIM
        sel = (lane_head == lax.broadcasted_iota(jnp.int32, (HEADS, width), 0)).astype(BF16)
        hi = prod.astype(BF16)
        lo = (prod - hi.astype(F32)).astype(BF16)
        dl_ref[...] = _dot_nt(sel, hi) + _dot_nt(sel, lo)

    def o_(n, dt):
        return (_sds((t_rows, n), dt), _row(tm, n))

    def acc(r, n):
        return (_sds((r, n), F32), _full((r, n)))

    halo = pl.BlockSpec((halo_rows, cdim), lambda i: (jnp.minimum((i + 1) * hb, last_blk), 0))
    ins = [(du, _row(tm, cdim)), (du, halo), (convw8, _full(convw8.shape)), (wup, _resident(wup.shape)),
           (dx2, _row(tm, D_MODEL)), (x1, _row(tm, D_MODEL)), (g3, _full(g3.shape)), (o, _row(tm, D_MODEL)),
           (g2, _full(g2.shape)), (wout, _resident(wout.shape)), (gate, _row(tm, 2048)), (pa, _row(tm, D_MODEL)),
           (pb, _row(tm, D_MODEL)), (wba, _resident(wba.shape)), (wbb, _resident(wbb.shape)),
           (yb, _row(tm, D_MODEL // 2))]
    outs = [o_(cdim, BF16), o_(D_MODEL, F32), o_(D_MODEL, BF16), o_(D_MODEL, BF16), o_(D_MODEL, BF16),
            o_(2048, BF16), o_(1024, BF16), o_(1024, BF16),
            (_sds((HEADS, t_rows), F32), pl.BlockSpec((HEADS, tm), lambda i: (0, i))),
            acc(1, D_MODEL), acc(1, D_MODEL), acc(1, 2048)]
    return _rows_call("bwd_mid", body, t_rows, tm, ins, outs)


def _bwd_in(dqs, dks, dvs, dqm, dkm, dvm, tabs, consts, cq, ckv, gq, gkv, wuq, wk, wv, dgates, win, x, g1, dx1, tm):
    t_rows = x.shape[0]

    def body(dqs_ref, dks_ref, dvs_ref, dqm_ref, dkm_ref, dvm_ref, ca, sa1, sa2, cb, sb1, sb2, c_ref, cq_ref,
             ckv_ref, gq_ref, gkv_ref, wuq_ref, wk_ref, wv_ref, dgt_ref, win_ref, x_ref, g1_ref, dx1_ref,
             dz_ref, dqb_ref, dx_ref, dgq_ref, dgkv_ref, dg1_ref):
        i = pl.program_id(0)

        @pl.when(i == 0)
        def _():
            dgq_ref[...] = jnp.zeros(dgq_ref.shape, F32)
            dgkv_ref[...] = jnp.zeros(dgkv_ref.shape, F32)
            dg1_ref[...] = jnp.zeros(dg1_ref.shape, F32)

        ta = (ca[...], sa1[...], sa2[...])
        tb = (cb[...], sb1[...], sb2[...])

        def piece(lo, hi, val):
            dz_ref[:, lo:hi] = val
            return _dot_nt(val, win_ref[:, lo:hi])

        dh1 = piece(Z_GATE, ZW, dgt_ref[...])
        dkm = dkm_ref[...]
        dckvn = _dot_nt(dkm.astype(BF16), wk_ref[...]) + _dot_nt(dvm_ref[...].astype(BF16), wv_ref[...])
        dh1 = dh1 + piece(Z_VA, Z_CQ, _fold_slots(dvs_ref[...]).astype(BF16))
        dqm = jnp.concatenate([dqm_ref[h] for h in range(HEADS)], axis=1)
        dqb = _rope_t(dqm * SCALE_B, *tb, ROPE_DIM // 2).astype(BF16)
        dqb_ref[...] = dqb
        dcqn = _dot_nt(dqb, wuq_ref[...])
        dqa = _rope_t(_fold_slots(dqs_ref[...]) * SCALE_A, *ta, A_HEAD_DIM // 2)
        dh1 = dh1 + piece(Z_QA, Z_KA, dqa.astype(BF16))
        dh1 = dh1 + piece(Z_KA, Z_VA, _rope_t(_fold_slots(dks_ref[...]), *ta, A_HEAD_DIM // 2).astype(BF16))
        ckvn, rkv = _rms_stats(ckv_ref[...])
        dckv, dgkv = _rms_bwd(dckvn, ckvn, rkv, gkv_ref[...])
        dgkv_ref[...] += dgkv
        dh1 = dh1 + piece(Z_CKV, Z_KR, dckv.astype(BF16))
        dslot = dkm[:, 0:LANES]
        for h in range(1, HEADS):
            dslot = dslot + dkm[:, h * LANES:(h + 1) * LANES]
        dh1 = dh1 + piece(Z_KR, Z_GATE, _rope_t(dslot * c_ref[10:11, :], *tb, ROPE_DIM // 2).astype(BF16))
        cqn, rq = _rms_stats(cq_ref[...])
        dcq, dgq = _rms_bwd(dcqn, cqn, rq, gq_ref[...])
        dgq_ref[...] += dgq
        dh1 = dh1 + piece(Z_CQ, Z_CKV, dcq.astype(BF16))
        xn, r1 = _rms_stats(x_ref[...])
        d0, dg1 = _rms_bwd(dh1, xn, r1, g1_ref[...])
        dg1_ref[...] += dg1
        dx_ref[...] = dx1_ref[...] + d0

    def acc(n):
        return (_sds((1, n), F32), _full((1, n)))

    ins = [(dqs, _row(tm, 1024)), (dks, _row(tm, 256)), (dvs, _row(tm, 256)), (dqm, _heads(tm, HEADS)),
           (dkm, _row(tm, 1024)), (dvm, _row(tm, 1024))] + [(t, _row(tm, LANES)) for t in tabs] + [
           (consts, _full(consts.shape)), (cq, _row(tm, 256)), (ckv, _row(tm, 128)), (gq, _full(gq.shape)),
           (gkv, _full(gkv.shape)), (wuq, _full(wuq.shape)), (wk, _full(wk.shape)), (wv, _full(wv.shape)),
           (dgates, _row(tm, 2048)), (win, _resident(win.shape)), (x, _row(tm, D_MODEL)), (g1, _full(g1.shape)),
           (dx1, _row(tm, D_MODEL))]
    outs = [(_sds((t_rows, ZW), BF16), _row(tm, ZW)), (_sds((t_rows, 1024), BF16), _row(tm, 1024)),
            (_sds((t_rows, D_MODEL), F32), _row(tm, D_MODEL)), acc(256), acc(128), acc(D_MODEL)]
    return _rows_call("bwd_in", body, t_rows, tm, ins, outs)


def _pick_cols(n):
    best = LANES
    for d in range(LANES, min(n, 1664) + 1, LANES):
        if n % d == 0:
            best = d
    return best


def _mm_tn(name, a, b, column_shards=1, after=None):
    t_rows, m = a.shape
    n = b.shape[1]
    bk = min(2048, t_rows)
    bm, bn = _pick_cols(m), _pick_cols(n // column_shards)
    per_shard = n // column_shards // bn
    extra = () if after is None else (after,)

    def body(a_ref, b_ref, *rest):
        o_ref = rest[-1]

        @pl.when(pl.program_id(2) == 0)
        def _():
            o_ref[...] = jnp.zeros((bm, bn), F32)

        o_ref[...] += _dot_tn(a_ref[...].astype(BF16), b_ref[...].astype(BF16))

    return pl.pallas_call(
        body, name=name, grid=(m // bm, n // bn, t_rows // bk),
        in_specs=[pl.BlockSpec((bk, bm), lambda i, j, k: (k, i)), pl.BlockSpec((bk, bn), lambda i, j, k: (k, j))]
        + [pl.BlockSpec((8, LANES), lambda i, j, k: (0, 0))] * len(extra),
        out_specs=(pl.BlockSpec((bm, bn), lambda i, j, k: (i, j)) if column_shards == 1 else
                   pl.BlockSpec((None, bm, bn), lambda i, j, k: (j // per_shard, i, j % per_shard))),
        out_shape=_sds((m, n) if column_shards == 1 else (column_shards, m, n // column_shards), F32),
        compiler_params=pltpu.CompilerParams(dimension_semantics=("arbitrary",) * 3, vmem_limit_bytes=VMEM_LIMIT),
    )(a, b, *extra)


PACK_ROWS = 512


ADD_TILE_ELEMS = 1 << 17


def _add_rows(rows, cols):
    best = 16
    for d in range(16, rows + 1, 16):
        if rows % d == 0 and d * cols <= ADD_TILE_ELEMS:
            best = d
    assert rows % best == 0
    return best


def _add_pair(name, g, recv, half):
    _, _, rows, cols = g.shape
    t = _add_rows(rows, cols)

    def body(h_ref, g_ref, r_ref, o_ref):
        o_ref[...] = (g_ref[:, 0] + r_ref[...]).astype(BF16)

    spec = pl.BlockSpec((4, t, cols), lambda i, h: (0, i, 0))
    grid_spec = pltpu.PrefetchScalarGridSpec(
        num_scalar_prefetch=1, grid=(rows // t,),
        in_specs=[pl.BlockSpec((4, 1, t, cols), lambda i, h: (0, h[0], i, 0)), spec], out_specs=spec)
    return pl.pallas_call(body, name=name, grid_spec=grid_spec,
                          out_shape=_sds(recv.shape, BF16))(jnp.reshape(half, (1,)).astype(jnp.int32), g, recv)


def _add_chips(name, parts):
    _, rows, cols = parts.shape
    t = _add_rows(rows, cols)

    def body(p_ref, o_ref):
        acc = p_ref[0].astype(F32)
        for j in range(1, 4):
            acc = acc + p_ref[j].astype(F32)
        o_ref[...] = acc

    return pl.pallas_call(body, name=name, grid=(rows // t,),
                          in_specs=[pl.BlockSpec((4, t, cols), lambda i: (0, i, 0))],
                          out_specs=pl.BlockSpec((t, cols), lambda i: (i, 0)),
                          out_shape=_sds((rows, cols), F32))(parts)


def _add_devices(parts):
    n, rows, _ = parts.shape

    def body(p_ref, o_ref):
        acc = p_ref[0]
        for j in range(1, n):
            acc = acc + p_ref[j]
        o_ref[...] = acc

    return pl.pallas_call(body, name="small_add", grid=(1,),
                          in_specs=[pl.BlockSpec((n, rows, LANES), lambda i: (0, 0, 0))],
                          out_specs=pl.BlockSpec((rows, LANES), lambda i: (0, 0)),
                          out_shape=_sds((rows, LANES), F32))(parts)


def _adam_rows(k, n):
    target = max(8, (1 << 20) // (4 * n))
    if k <= target:
        return k
    best = None
    for d in range(8, target + 1, 8):
        if k % d == 0:
            best = d
    return best if best is not None else k


def _adam_update(w, g, m, v):
    m_ = ADAM_B1 * m + (1.0 - ADAM_B1) * g
    v_ = ADAM_B2 * v + (1.0 - ADAM_B2) * (g * g)
    delta = -ADAM_LR * ((m_ / (1.0 - ADAM_B1 ** ADAM_STEP)) / (jnp.sqrt(v_ / (1.0 - ADAM_B2 ** ADAM_STEP)) + ADAM_EPS)
                        + ADAM_WD * w)
    return delta, m_, v_


def _adamw_many(name, ws, gs, ms, vs):
    n = len(ws)

    def body(*refs):
        for i in range(n):
            w_ref, g_ref, m_ref, v_ref = (refs[k * n + i] for k in range(4))
            d_ref, mo_ref, vo_ref = (refs[(4 + k) * n + i] for k in range(3))
            d_ref[...], mo_ref[...], vo_ref[...] = _adam_update(w_ref[...], g_ref[...], m_ref[...], v_ref[...])

    specs = [pl.BlockSpec(w.shape, lambda i: (0, 0)) for w in ws]
    out = pl.pallas_call(body, name=name, grid=(1,), in_specs=specs * 4, out_specs=specs * 3,
                         out_shape=[_sds(w.shape, F32) for w in ws] * 3)(*ws, *gs, *ms, *vs)
    return [(gs[i], out[i], out[n + i], out[2 * n + i]) for i in range(n)]


def _adamw_halves(name, w, mine, theirs, m, v, half):
    k, n = w.shape
    bk = _adam_rows(k // 2, n)
    nb = k // 2 // bk

    def body(h_ref, w_ref, mine_ref, theirs_ref, m_ref, v_ref, g_ref, d_ref, mo_ref, vo_ref):
        g = jnp.where(pl.program_id(0) == h_ref[0], mine_ref[...], theirs_ref[...])
        g_ref[...] = g
        d_ref[...], mo_ref[...], vo_ref[...] = _adam_update(w_ref[...], g, m_ref[...], v_ref[...])

    full = pl.BlockSpec((bk, n), lambda h, i, c: (h * nb + i, 0))
    part = pl.BlockSpec((bk, n), lambda h, i, c: (i, 0))
    grid_spec = pltpu.PrefetchScalarGridSpec(num_scalar_prefetch=1, grid=(2, nb),
                                             in_specs=[full, part, part, full, full], out_specs=[full] * 4)
    return tuple(pl.pallas_call(
        body, name=name, grid_spec=grid_spec, out_shape=[_sds((k, n), F32)] * 4,
        compiler_params=pltpu.CompilerParams(vmem_limit_bytes=VMEM_LIMIT),
    )(jnp.reshape(half, (1,)).astype(jnp.int32), w, mine, theirs, m, v))


_HBM = pl.BlockSpec(memory_space=pltpu.HBM)


def _me():
    return lax.axis_index("x"), lax.axis_index("y"), lax.axis_index("c")


def _other_chips(x, y):
    return [(1 - x, y), (x, 1 - y), (1 - x, 1 - y)]


def _pass_to_sibling(zones):
    n = len(zones)

    def body(*refs):
        in_refs, out_refs = refs[:n], refs[n:2 * n]
        send_sems, recv_sems = refs[2 * n:]
        x, y, c = _me()
        sent = []
        for a, (in_ref, out_ref) in enumerate(zip(in_refs, out_refs)):
            for j, (cx, cy) in enumerate(_other_chips(x, y)):
                mine, theirs = (2 * cx + cy, c), (2 * cx + cy, 1 - c)
                sems = dict(send_sem=send_sems.at[3 * a + j], recv_sem=recv_sems.at[3 * a + j],
                            device_id=(x, y, 1 - c), device_id_type=MESH)
                sent.append(tuple(pltpu.make_async_remote_copy(src_ref=in_ref.at[part], dst_ref=out_ref.at[part], **sems)
                                  for part in (mine, theirs)))
        for send, _ in sent:
            send.start()
        for _, recv in sent:
            recv.wait_recv()
        for send, _ in sent:
            send.wait_send()

    return pl.pallas_call(
        body, name="pass_to_sibling", out_shape=[_sds(z.shape, z.dtype) for z in zones],
        in_specs=[_HBM] * n, out_specs=[_HBM] * n, input_output_aliases={i: i for i in range(n)},
        scratch_shapes=[pltpu.SemaphoreType.DMA((3 * n,)), pltpu.SemaphoreType.DMA((3 * n,))],
    )(*zones)


def _swap_sibling(name, vs, other_half=False):
    n = len(vs)

    def body(*refs):
        v_refs, out_refs = refs[:n], refs[n:2 * n]
        send_sems, recv_sems = refs[2 * n:]
        x, y, c = _me()
        cps = [pltpu.make_async_remote_copy(src_ref=v_ref.at[:, 1 - c] if other_half else v_ref, dst_ref=out_ref,
                                            send_sem=send_sems.at[a], recv_sem=recv_sems.at[a],
                                            device_id=(x, y, 1 - c), device_id_type=MESH)
               for a, (v_ref, out_ref) in enumerate(zip(v_refs, out_refs))]
        for cp in cps:
            cp.start()
        for cp in cps:
            cp.wait()

    def landing(v):
        return _sds((v.shape[0],) + v.shape[2:] if other_half else v.shape, v.dtype)

    return pl.pallas_call(
        body, name=name, out_shape=[landing(v) for v in vs], in_specs=[_HBM] * n, out_specs=[_HBM] * n,
        scratch_shapes=[pltpu.SemaphoreType.DMA((n,)), pltpu.SemaphoreType.DMA((n,))],
    )(*vs)


_SEM = pl.BlockSpec(memory_space=pltpu.SEMAPHORE)
_EFFECT = pltpu.SideEffectType.DATAFLOW_SIDE_EFFECTING
WHOLE = "whole"
PIECE = "piece"
SIBLING_HALF = "sibling"
MY_HALF = "half"
EVERYONE = "everyone"
_COPIES = {WHOLE: 3, PIECE: 3, MY_HALF: 3, SIBLING_HALF: 1, EVERYONE: 7}


def _landing_shape(v, mode):
    return {WHOLE: (4,) + v.shape, MY_HALF: (4,) + v.shape, PIECE: v.shape, EVERYONE: (8,) + v.shape,
            SIBLING_HALF: (v.shape[0],) + v.shape[2:]}[mode]


def _chip_copies(v_ref, land_ref, send_sems, recv_sems, mode, sem0=0):
    x, y, c = _me()
    if mode == SIBLING_HALF:
        cp = pltpu.make_async_remote_copy(src_ref=v_ref.at[:, 1 - c], dst_ref=land_ref, send_sem=send_sems.at[sem0],
                                          recv_sem=recv_sems.at[sem0], device_id=(x, y, 1 - c), device_id_type=MESH)
        return [(cp, cp)]
    if mode == EVERYONE:
        out = []
        for f in range(1, 8):
            px, py, pc = (1 - x if f & 4 else x), (1 - y if f & 2 else y), (1 - c if f & 1 else c)
            sems = dict(send_sem=send_sems.at[sem0 + f - 1], recv_sem=recv_sems.at[sem0 + f - 1],
                        device_id=(px, py, pc), device_id_type=MESH)
            out.append((pltpu.make_async_remote_copy(src_ref=v_ref, dst_ref=land_ref.at[4 * x + 2 * y + c], **sems),
                        pltpu.make_async_remote_copy(src_ref=v_ref, dst_ref=land_ref.at[4 * px + 2 * py + pc], **sems)))
        return out
    k = 2 * x + y
    out = []
    for j, (cx, cy) in enumerate(_other_chips(x, y)):
        if mode == MY_HALF:
            src, mine, theirs = v_ref.at[c], land_ref.at[k, c], land_ref.at[2 * cx + cy, c]
        else:
            src = v_ref.at[2 * cx + cy] if mode == PIECE else v_ref
            mine, theirs = land_ref.at[k], land_ref.at[2 * cx + cy]
        sems = dict(send_sem=send_sems.at[sem0 + j], recv_sem=recv_sems.at[sem0 + j], device_id=(cx, cy, c),
                    device_id_type=MESH)
        send = pltpu.make_async_remote_copy(src_ref=src, dst_ref=mine, **sems)
        recv = pltpu.make_async_remote_copy(src_ref=src, dst_ref=theirs, **sems)
        out.append((send, recv))
    return out


def _chips_start(name, vs, mode, after=None):
    n = len(vs)
    lands = [_landing_shape(v, mode) for v in vs]

    def body(*refs):
        v_refs, land_refs = refs[:n], refs[n:2 * n]
        send_sems, recv_sems = refs[-2 * n - 3], refs[-2 * n - 2]
        token = refs[-1]
        for a in range(n):
            for send, _ in _chip_copies(v_refs[a], land_refs[a], send_sems, recv_sems, mode, _COPIES[mode] * a):
                send.start()
        token[...] = jnp.zeros_like(token)

    extra = () if after is None else (after,)
    hbm = [pltpu.with_memory_space_constraint(v, pltpu.HBM) for v in vs]
    zones = [pltpu.with_memory_space_constraint(lax.empty(s, v.dtype), pltpu.HBM) for s, v in zip(lands, vs)]
    out = pl.pallas_call(
        body, name=name,
        out_shape=(pltpu.SemaphoreType.DMA((_COPIES[mode] * n,)), pltpu.SemaphoreType.DMA((_COPIES[mode] * n,)),
                   *[pltpu.HBM(v.shape, v.dtype) for v in vs], *[pltpu.HBM(s, v.dtype) for s, v in zip(lands, vs)],
                   _sds((8, LANES), F32)),
        in_specs=(_HBM,) * (2 * n) + (pl.BlockSpec(memory_space=pl.ANY),) * len(extra),
        out_specs=(_SEM, _SEM) + (_HBM,) * (2 * n) + (pl.BlockSpec(memory_space=pltpu.VMEM),),
        input_output_aliases={i: 2 + i for i in range(2 * n)},
        compiler_params=pltpu.CompilerParams(has_side_effects=_EFFECT),
    )(*hbm, *zones, *extra)
    return out[0], out[1], list(out[2:2 + n]), list(out[2 + n:2 + 2 * n]), out[-1]


def _chips_wait(name, send_sems, recv_sems, v_thru, land_thru, mode, after):
    n = len(v_thru)

    def body(*refs):
        v_refs, land_refs = refs[:n], refs[n:2 * n]
        send_sems, recv_sems = refs[2 * n], refs[2 * n + 1]
        for a in range(n):
            for send, recv in _chip_copies(v_refs[a], land_refs[a], send_sems, recv_sems, mode, _COPIES[mode] * a):
                send.wait_send()
                recv.wait_recv()

    out = pl.pallas_call(
        body, name=name,
        out_shape=tuple(pltpu.HBM(a.shape, a.dtype) for a in list(v_thru) + list(land_thru)),
        in_specs=(_HBM,) * (2 * n) + (_SEM, _SEM, pl.BlockSpec(memory_space=pl.ANY)), out_specs=(_HBM,) * (2 * n),
        input_output_aliases={i: i for i in range(2 * n)},
        compiler_params=pltpu.CompilerParams(has_side_effects=_EFFECT),
    )(*v_thru, *land_thru, send_sems, recv_sems, after)
    return list(out[:n]), list(out[n:])


_BIG = (("w_in", (1024, 3232), 1), ("w_uq", (256, 768), 1), ("w_ukv", (128, 1024), 1), ("w_branch_a", (512, 1024), 1),
        ("w_branch_b", (512, 1024), 1), ("w_out", (1024, 1024), 0), ("w_up", (1024, 5632), 1),
        ("w_down", (2816, 1024), 0), ("w_ple_gate", (1024, 1024), 0), ("w_ple", (256, 1024), 1))


def _shard_shape(shape, axis):
    return (shape[0] // 4, shape[1]) if axis == 0 else (shape[0], shape[1] // 4)


def _half_rows(shape, axis):
    k, n = _shard_shape(shape, axis)
    return k * n // (2 * LANES)


_EARLY = ("w_in", "w_uq", "w_ukv")
_LATE = ("w_branch_a", "w_branch_b", "w_out", "w_up", "w_down", "w_ple_gate", "w_ple")
_NATURAL = ("w_in", "w_up", "w_down", "w_out", "w_ple_gate")
_EARLY_PACKED = tuple(b for b in _BIG if b[0] in _EARLY and b[0] not in _NATURAL)
_LATE_PACKED = tuple(b for b in _BIG if b[0] in _LATE and b[0] not in _NATURAL)


def _halves(a):
    return a.reshape(a.shape[:-2] + (2, a.shape[-2] // 2, a.shape[-1]))


def _rows_joined(a):
    return a.reshape(a.shape[:-3] + (a.shape[-3] * a.shape[-2], a.shape[-1]))


def _pack_pad(group):
    return -sum(_half_rows(shape, axis) for _, shape, axis in group) % PACK_ROWS


def _pack_shards(shards, dtype, group):
    parts = [shards[name].astype(dtype).reshape(2, _half_rows(shape, axis), LANES) for name, shape, axis in group]
    return jnp.concatenate(parts + [jnp.zeros((2, _pack_pad(group), LANES), dtype)], axis=1)


def _unpack_gathered(g, group):
    out, off = {}, 0
    for name, shape, axis in group:
        r = _half_rows(shape, axis)
        k, n = _shard_shape(shape, axis)
        w = g[:, :, off:off + r, :].reshape(4, k, n)
        out[name] = w.reshape(shape) if axis == 0 else w.transpose(1, 0, 2).reshape(shape)
        off += r
    return out


def _pack_grads(grads, group):
    parts = []
    for name, shape, axis in group:
        k, n = _shard_shape(shape, axis)
        g = grads[name]
        g4 = g.reshape(4, k, n) if axis == 0 else g.reshape(k, 4, n).transpose(1, 0, 2)
        parts.append(g4.reshape(4, 2, _half_rows(shape, axis), LANES))
    return jnp.concatenate(parts + [jnp.zeros((4, 2, _pack_pad(group), LANES), F32)], axis=2)


def _unpack_shard_grads(f, group):
    out, off = {}, 0
    for name, shape, axis in group:
        r = _half_rows(shape, axis)
        out[name] = f[:, off:off + r, :].reshape(_shard_shape(shape, axis))
        off += r
    return out


def _pad_slots(w, heads, dim):
    k = w.shape[0]
    return jnp.pad(w.reshape(k, heads, dim), ((0, 0), (0, 0), (0, LANES - dim))).reshape(k, heads * LANES)


def _unpad_slots(w, heads, dim):
    k = w.shape[0]
    return w.reshape(k, heads, LANES)[:, :, :dim].reshape(k, heads * dim)


def _pad_w_in(w):
    kr = jnp.pad(w[:, Z_KR:Z_KR + ROPE_DIM], ((0, 0), (NOPE_DIM, LANES - NOPE_DIM - ROPE_DIM)))
    return jnp.concatenate([w[:, :Z_KR], kr, w[:, Z_KR + ROPE_DIM:]], axis=1)


def _unpad_w_in(w):
    return jnp.concatenate([w[:, :Z_KR], w[:, Z_KR + NOPE_DIM:Z_KR + NOPE_DIM + ROPE_DIM], w[:, Z_GATE:ZW]], axis=1)


def _spread_matrix(heads, dim):
    row = lax.broadcasted_iota(jnp.int32, (heads * dim, heads * LANES), 0)
    col = lax.broadcasted_iota(jnp.int32, (heads * dim, heads * LANES), 1)
    return (col == (row // dim) * LANES + row % dim).astype(BF16)


_SMALL = (("attn_pre_norm", 1024), ("attn_post_norm", 1024), ("b_gate", 2048), ("sinks", 8), ("q_a_norm", 256),
          ("kv_a_norm", 128), ("mlp_pre_norm", 1024), ("mlp_post_norm", 1024), ("conv_b", 5632), ("ple_norm", 1024),
          ("conv_w", 3 * 5632), ("loss", 1))


def _small_rows(n):
    return 8 * -(-n // (8 * LANES))


def _pack_small(vals):
    parts = []
    for name, n in _SMALL:
        r = _small_rows(n)
        parts.append(jnp.pad(vals[name].reshape(-1), (0, r * LANES - n)).reshape(r, LANES))
    return jnp.concatenate(parts, axis=0)


def _unpack_small(buf):
    out, off = {}, 0
    for name, n in _SMALL:
        r = _small_rows(n)
        out[name] = buf[off:off + r].reshape(-1)[:n]
        off += r
    return out


def kernel(x, p, positions, attn_pre_norm, attn_post_norm, w_in, b_gate, sinks, q_a_norm, w_uq, kv_a_norm, w_ukv, w_branch_a, w_branch_b, w_out, mlp_pre_norm, mlp_post_norm, w_up, conv_w, conv_b, w_down, ple_norm, w_ple_gate, w_ple, loss_target, m_attn_pre_norm, m_attn_post_norm, m_w_in, m_b_gate, m_sinks, m_q_a_norm, m_w_uq, m_kv_a_norm, m_w_ukv, m_w_branch_a, m_w_branch_b, m_w_out, m_mlp_pre_norm, m_mlp_post_norm, m_w_up, m_conv_w, m_conv_b, m_w_down, m_ple_norm, m_w_ple_gate, m_w_ple, v_attn_pre_norm, v_attn_post_norm, v_w_in, v_b_gate, v_sinks, v_q_a_norm, v_w_uq, v_kv_a_norm, v_w_ukv, v_w_branch_a, v_w_branch_b, v_w_out, v_mlp_pre_norm, v_mlp_post_norm, v_w_up, v_conv_w, v_conv_b, v_w_down, v_ple_norm, v_w_ple_gate, v_w_ple):
    names = ["attn_pre_norm", "attn_post_norm", "w_in", "b_gate", "sinks", "q_a_norm", "w_uq", "kv_a_norm", "w_ukv",
             "w_branch_a", "w_branch_b", "w_out", "mlp_pre_norm", "mlp_post_norm", "w_up", "conv_w", "conv_b",
             "w_down", "ple_norm", "w_ple_gate", "w_ple"]
    wts = dict(zip(names, [attn_pre_norm, attn_post_norm, w_in, b_gate, sinks, q_a_norm, w_uq, kv_a_norm, w_ukv,
                           w_branch_a, w_branch_b, w_out, mlp_pre_norm, mlp_post_norm, w_up, conv_w, conv_b, w_down,
                           ple_norm, w_ple_gate, w_ple]))
    moms = dict(zip(names, [m_attn_pre_norm, m_attn_post_norm, m_w_in, m_b_gate, m_sinks, m_q_a_norm, m_w_uq,
                            m_kv_a_norm, m_w_ukv, m_w_branch_a, m_w_branch_b, m_w_out, m_mlp_pre_norm,
                            m_mlp_post_norm, m_w_up, m_conv_w, m_conv_b, m_w_down, m_ple_norm, m_w_ple_gate, m_w_ple]))
    vars_ = dict(zip(names, [v_attn_pre_norm, v_attn_post_norm, v_w_in, v_b_gate, v_sinks, v_q_a_norm, v_w_uq,
                             v_kv_a_norm, v_w_ukv, v_w_branch_a, v_w_branch_b, v_w_out, v_mlp_pre_norm,
                             v_mlp_post_norm, v_w_up, v_conv_w, v_conv_b, v_w_down, v_ple_norm, v_w_ple_gate, v_w_ple]))
    w2 = {n: a.reshape(a.shape[-2:]) for n, a in wts.items()}
    m2 = {n: a.reshape(a.shape[-2:]) for n, a in moms.items()}
    v2 = {n: a.reshape(a.shape[-2:]) for n, a in vars_.items()}

    t_rows = x.shape[-2]
    tm = min(256, t_rows)
    tm_wide = min(512, t_rows)
    xc, yc, cc = lax.axis_index("x"), lax.axis_index("y"), lax.axis_index("c")
    chip = 2 * xc + yc

    x2d = x.reshape(t_rows, D_MODEL)
    p2d = p.reshape(t_rows, PLE_DIM)
    tgt = loss_target.reshape(t_rows, D_MODEL)
    pos_f = positions.reshape(t_rows, 1).astype(F32)

    def own_slot_filled(gathered, mine):
        return [lax.dynamic_update_slice(g, m[None], (chip, 0, 0, 0)) for g, m in zip(gathered, mine)]

    def shard_lists(group, packed_group, token=0.0):
        ws = {n: w2[n] + token for n in group}
        return [_halves(ws[n].astype(BF16)) for n in group if n in _NATURAL] + [_pack_shards(ws, BF16, packed_group)]

    cw_rows = 3 * 1408 // LANES
    conv_mine = jnp.pad(w2["conv_w"].reshape(cw_rows, LANES), ((0, 48 - cw_rows), (0, 0))).reshape(2, 24, LANES)
    early_mine = shard_lists(_EARLY, _EARLY_PACKED) + [conv_mine]
    early_sems = _chips_start("gather_early_start", early_mine, MY_HALF)
    early_token = early_sems[4][0:1, 0:1]
    consts = _rope_consts()
    tabs = _rope_tables(pos_f + early_token, consts, tm)
    late_mine = shard_lists(_LATE, _LATE_PACKED, early_token)
    both_done = tabs[0][0:1, 0:1] + sum(m[0, 0:1, 0:1].astype(F32) for m in late_mine)
    early_sent, early_landed = _chips_wait("gather_early_wait", *early_sems[:4], MY_HALF, after=both_done)
    early = own_slot_filled(_pass_to_sibling(early_landed), early_sent)
    late_names = [n for n in _LATE if n in _NATURAL]
    first = [late_names.index("w_out"), len(late_names)]
    late_a = [late_mine[i] for i in first]
    late_b = [m for i, m in enumerate(late_mine) if i not in first]
    late_a_sems = _chips_start("gather_late_a_start", late_a, WHOLE, after=early[0])
    late_b_sems = _chips_start("gather_late_b_start", late_b, WHOLE, after=late_a_sems[4])
    late_token = late_b_sems[4][0:1, 0:1]
    full = _unpack_gathered(early[1], _EARLY_PACKED)
    full["w_in"] = _rows_joined(early[0]).transpose(1, 0, 2).reshape(D_MODEL, 3232)
    conv_full = early[2].reshape(4, 48, LANES)[:, :cw_rows].reshape(4, 3, 1408).transpose(1, 0, 2).reshape(3, 2 * D_FF)
    convw8 = jnp.pad(conv_full, ((0, 5), (0, 0)))

    win = _pad_w_in(full["w_in"])
    wuq = _pad_slots(full["w_uq"], HEADS, NOPE_DIM + ROPE_DIM)
    ukv = full["w_ukv"].reshape(KV_LORA, HEADS, NOPE_DIM + V_DIM)
    wk = _pad_slots(ukv[:, :, :NOPE_DIM].reshape(KV_LORA, HEADS * NOPE_DIM), HEADS, NOPE_DIM)
    wv = _pad_slots(ukv[:, :, NOPE_DIM:].reshape(KV_LORA, HEADS * V_DIM), HEADS, V_DIM)
    g1, g2, g3, g4, g5 = (w2["attn_pre_norm"], w2["attn_post_norm"], w2["mlp_pre_norm"], w2["mlp_post_norm"],
                          w2["ple_norm"])
    gq, gkv, bg, convb = w2["q_a_norm"], w2["kv_a_norm"], w2["b_gate"], w2["conv_b"]
    swa_tile = min(SWA_TILE, t_rows)
    sink_rows = jnp.repeat(w2["sinks"].reshape(A_KV_HEADS, SWA_GROUP, 1), swa_tile, axis=2).reshape(
        A_KV_HEADS, 1, SWA_GROUP * swa_tile)
    swa_bias = _swa_bias(swa_tile)
    spread_q = _spread_matrix(HEADS, A_HEAD_DIM)
    spread_kv = _spread_matrix(A_KV_HEADS, A_HEAD_DIM)

    h1, qs, ks, vs, cq, cqn, ckv, ckvn, qm, km, vm, gate = _fwd_in(x2d, g1, win, bg + late_token, gq, gkv, wuq, wk, wv,
                                                                   spread_q, spread_kv, tabs, tm_wide)
    ya, lse_a = _swa_fwd(qs, ks, vs, swa_bias, sink_rows)
    yb, lse_b = _mla_fwd(qm, km, vm)
    late_sent, late_landed = _chips_wait("gather_late_a_wait", *late_a_sems[:4], WHOLE, after=yb)
    wout_g, packed_g = own_slot_filled(late_landed, late_sent)
    full = _unpack_gathered(packed_g, _LATE_PACKED)
    wba, wbb = full["w_branch_a"], full["w_branch_b"]
    wple = full["w_ple"]
    wout = _rows_joined(wout_g).reshape(-1, D_MODEL)
    pa, pb, mixed, o, x1, h2, ya_c, yb_c = _fwd_mix(x2d, ya, yb, gate, wba, wbb, wout, g2, g3, tm_wide)
    late_sent, late_landed = _chips_wait("gather_late_b_wait", *late_b_sems[:4], WHOLE, after=pa)
    natural = dict(zip([n for n in late_names if n != "w_out"], own_slot_filled(late_landed, late_sent)))
    wup = _rows_joined(natural["w_up"])
    wdown, wpg = (_rows_joined(natural[n]).reshape(-1, D_MODEL) for n in ("w_down", "w_ple_gate"))
    up, a, u = _fwd_up(h2, wup, convw8, convb, tm)
    ff, x2, e, n5, sg, dx3, loss_part = _fwd_out(a, wdown, x1, g4, p2d, wple, g5, wpg, tgt, tm_wide)

    dpre, de, dx2, dff, du, dg5, dg4, dconvb, dconvw8 = _bwd_out(dx3, e, sg, x2, ff, g5, g4, wpg, wdown, up, u, tm)
    dup, dx1, do, dpa, dpb, dgates, dya, dyb, delta_b, dg3, dg2, dbg = _bwd_mid(
        du, convw8, wup, dx2, x1, g3, o, g2, wout, gate, pa, pb, wba, wbb, yb_c, tm)
    late_grads = {
        "w_branch_a": _mm_tn("dw_branch_a", ya_c, dpa),
        "w_branch_b": _mm_tn("dw_branch_b", yb_c, dpb),
        "w_out": _mm_tn("dw_out", mixed, do).reshape(4, D_MODEL // 4, D_MODEL),
        "w_up": _mm_tn("dw_up", h2, dup, column_shards=4),
        "w_down": _mm_tn("dw_down", a, dff).reshape(4, D_FF // 4, D_MODEL),
        "w_ple_gate": _mm_tn("dw_ple_gate", n5, dpre).reshape(4, D_MODEL // 4, D_MODEL),
        "w_ple": _mm_tn("dw_ple", p2d, de),
    }

    def grad_views(grads, group, packed_group):
        return [_halves(grads[n]) for n in group if n in _NATURAL] + [_pack_grads(grads, packed_group)]

    def pair_sums(tag, views, theirs):
        return [_add_pair("rs_%s_add_pair_%d" % (tag, i), g, r, cc) for i, (g, r) in enumerate(zip(views, theirs))]

    swap_sems = _chips_start("swap_late_start", grad_views(late_grads, _LATE, _LATE_PACKED), SIBLING_HALF)
    dqs, dks, dvs, dsink_rows = _swa_bwd(qs, ks, vs, ya, dya, lse_a, swa_bias, sink_rows + swap_sems[4][0:1, 0:1])
    dsink = dsink_rows[:, 0:SWA_GROUP, 0]
    late_views, late_theirs = _chips_wait("swap_late_wait", *swap_sems[:4], SIBLING_HALF, after=dqs)
    rs_sems = _chips_start("scatter_late_start", pair_sums("late", late_views, late_theirs), PIECE)
    dqm, dkm, dvm = _mla_bwd(qm, km, vm, dyb, lse_b, delta_b.reshape(HEADS, 1, t_rows) + rs_sems[4][0:1, 0:1])
    dz, dqb, dx, dgq, dgkv, dg1 = _bwd_in(dqs, dks, dvs, dqm, dkm, dvm, tabs, consts, cq, ckv, gq, gkv, wuq, wk, wv,
                                           dgates, win, x2d, g1, dx1, tm)

    small = {"attn_pre_norm": dg1, "attn_post_norm": dg2, "b_gate": dbg, "sinks": dsink, "q_a_norm": dgq,
             "kv_a_norm": dgkv, "mlp_pre_norm": dg3, "mlp_post_norm": dg4, "conv_b": dconvb, "ple_norm": dg5,
             "conv_w": dconvw8[0:3], "loss": loss_part}
    small_sems = _chips_start("gather_small_start", [_pack_small(small)], EVERYONE)
    small_token = small_sems[4]

    dwk = _unpad_slots(_mm_tn("dw_k", ckvn, dkm, after=small_token), HEADS, NOPE_DIM).reshape(
        KV_LORA, HEADS, NOPE_DIM)
    dwv = _unpad_slots(_mm_tn("dw_v", ckvn, dvm, after=small_token), HEADS, V_DIM).reshape(KV_LORA, HEADS, V_DIM)
    early_grads = {
        "w_in": _unpad_w_in(_mm_tn("dw_in", h1, dz, after=small_token)).reshape(D_MODEL, 4, 808).transpose(1, 0, 2),
        "w_uq": _unpad_slots(_mm_tn("dw_uq", cqn, dqb, after=small_token), HEADS, NOPE_DIM + ROPE_DIM),
        "w_ukv": jnp.concatenate([dwk, dwv], axis=2).reshape(KV_LORA, HEADS * (NOPE_DIM + V_DIM)),
    }

    def finish(tag, pairs, landed, group, packed_group):
        reduced = []
        for i, (pair, land) in enumerate(zip(pairs, landed)):
            own = lax.dynamic_index_in_dim(pair, chip, 0, keepdims=True)
            reduced.append(_add_chips("rs_%s_add_chips_%d" % (tag, i),
                                      lax.dynamic_update_slice(land, own, (chip, 0, 0))))
        others = _swap_sibling("swap_%s_reduced_halves" % tag, reduced)
        r, o = reduced[-1], others[-1]
        packed = jnp.where(cc == 0, jnp.stack([r, o]), jnp.stack([o, r]))
        shards = _unpack_shard_grads(packed, packed_group)
        updates.update(zip(shards, _adamw_many("adamw_%s_packed" % tag, [w2[n] for n in shards], list(shards.values()),
                                               [m2[n] for n in shards], [v2[n] for n in shards])))
        for n, r, o in zip([n for n in group if n in _NATURAL], reduced, others):
            updates[n] = _adamw_halves("adamw_" + n, w2[n], r, o, m2[n], v2[n], cc)

    updates = {}

    early_views = grad_views(early_grads, _EARLY, _EARLY_PACKED)
    early_theirs = _swap_sibling("swap_early_grad_halves", early_views, other_half=True)
    small_sent, small_landed = _chips_wait("gather_small_wait", *small_sems[:4], EVERYONE, after=early_theirs[0])
    small_all = lax.dynamic_update_slice(small_landed[0], small_sent[0][None], (4 * xc + 2 * yc + cc, 0, 0))
    early_sems = _chips_start("scatter_early_start", pair_sums("early", early_views, early_theirs), PIECE,
                              after=small_all)
    late_pairs, late_landed = _chips_wait("scatter_late_wait", *rs_sems[:4], PIECE, after=early_sems[4])
    finish("late", late_pairs, late_landed, _LATE, _LATE_PACKED)
    early_pairs, early_landed = _chips_wait("scatter_early_wait", *early_sems[:4], PIECE,
                                            after=updates[_LATE[-1]][1])
    finish("early", early_pairs, early_landed, _EARLY, _EARLY_PACKED)

    small_sum = _unpack_small(_add_devices(small_all))
    small_names = [n for n in names if n in small_sum]
    small_grads = [lax.dynamic_index_in_dim(small_sum[n].reshape(3, 4, 1408), chip, 1, keepdims=False)
                   if n == "conv_w" else small_sum[n].reshape(w2[n].shape) for n in small_names]
    updates.update(zip(small_names, _adamw_many("adamw_small", [w2[n] for n in small_names], small_grads,
                                                [m2[n] for n in small_names], [v2[n] for n in small_names])))
    loss = small_sum["loss"][0]

    outs = [[updates[n][i].reshape(wts[n].shape) for n in names] for i in range(4)]
    return (loss, dx.reshape(x.shape), *outs[0], *outs[1], *outs[2], *outs[3])
```

```python
import math

import numpy as np
import jax
import jax.numpy as jnp
from jax import lax
from jax.experimental import pallas as pl
from jax.experimental.pallas import tpu as pltpu

F32 = jnp.float32
BF16 = jnp.bfloat16

D_MODEL = 1024
D_FF = 2816
PLE_DIM = 256
ROPE_THETA = 10000.0
RMS_EPS = 1e-6
SWA_WINDOW = 128
HEADS = 8
A_KV_HEADS = 2
A_HEAD_DIM = 64
KV_LORA = 128
NOPE_DIM = 64
ROPE_DIM = 32
V_DIM = 64
LANES = 128
ZW = 3328
NEG = -1e30
SCALE_A = A_HEAD_DIM ** -0.5
SCALE_B = (NOPE_DIM + ROPE_DIM) ** -0.5
LOG2E = math.log2(math.e)

ADAM_LR = 0.001
ADAM_B1 = 0.9
ADAM_B2 = 0.999
ADAM_EPS = 1e-08
ADAM_WD = 0.01
ADAM_STEP = 10

VMEM_LIMIT = 60 * 1024 * 1024
MESH = pl.DeviceIdType.MESH

Z_QA, Z_KA, Z_VA, Z_CQ, Z_CKV, Z_KR, Z_GATE = 0, 512, 640, 768, 1024, 1152, 1280


def _dot(a, b):
    return jnp.dot(a, b, preferred_element_type=F32)


def _dot_nt(a, b):
    return lax.dot_general(a, b, (((1,), (1,)), ((), ())), preferred_element_type=F32)


def _dot_tn(a, b):
    return lax.dot_general(a, b, (((0,), (0,)), ((), ())), preferred_element_type=F32)


def _rms_stats(x):
    r = lax.rsqrt(jnp.mean(x * x, axis=-1, keepdims=True) + RMS_EPS)
    return x * r, r


def _rms_bwd(dy, xn, r, g):
    dxn = dy * g
    dx = r * (dxn - xn * jnp.mean(dxn * xn, axis=-1, keepdims=True))
    dg = jnp.sum(dy * xn, axis=0, keepdims=True)
    return dx, dg


def _tile_lanes(t, n):
    return t if n == 1 else jnp.concatenate([t] * n, axis=1)


def _rope(x, c, s1, s2, half):
    w = x.shape[1]
    n = w // LANES
    return (x * _tile_lanes(c, n) + pltpu.roll(x, w - half, 1) * _tile_lanes(s1, n)
            + pltpu.roll(x, half, 1) * _tile_lanes(s2, n))


def _rope_t(dy, c, s1, s2, half):
    w = dy.shape[1]
    n = w // LANES
    return (dy * _tile_lanes(c, n) + pltpu.roll(dy * _tile_lanes(s1, n), half, 1)
            + pltpu.roll(dy * _tile_lanes(s2, n), w - half, 1))


def _fold_slots(d):
    tiles = []
    for j in range(d.shape[1] // (2 * LANES)):
        even = d[:, 2 * j * LANES:(2 * j + 1) * LANES]
        odd = d[:, (2 * j + 1) * LANES:(2 * j + 2) * LANES]
        tiles.append(even + pltpu.roll(odd, A_HEAD_DIM, 1))
    return tiles[0] if len(tiles) == 1 else jnp.concatenate(tiles, axis=1)


def _spread_slots(c):
    low = lax.broadcasted_iota(jnp.int32, (c.shape[0], LANES), 1) < A_HEAD_DIM
    slots = []
    for j in range(c.shape[1] // LANES):
        tile = c[:, j * LANES:(j + 1) * LANES]
        slots += [jnp.where(low, tile, 0.0), jnp.where(low, pltpu.roll(tile, A_HEAD_DIM, 1), 0.0)]
    return jnp.concatenate(slots, axis=1)


def _sigmoid(x):
    return 0.5 * jnp.tanh(0.5 * x) + 0.5


_GELU_C = math.sqrt(2.0 / math.pi)


def _gelu_and_grad(x):
    a = _GELU_C + (_GELU_C * 0.044715) * (x * x)
    th = jnp.tanh(x * a)
    hx = 0.5 * x
    p1 = 1.0 + th
    gel = hx * p1
    dgel = 0.5 * p1 + (hx * (1.0 - th * th)) * (3.0 * a - 2.0 * _GELU_C)
    return gel, dgel


def _conv_taps(up, h6, h7):
    r1 = pltpu.roll(up, 1, 0)
    r2 = pltpu.roll(up, 2, 0)
    rows = lax.broadcasted_iota(jnp.int32, (8, up.shape[1]), 0)
    xm1 = jnp.concatenate([jnp.where(rows == 0, h7, r1[0:8]), r1[8:]], axis=0)
    xm2 = jnp.concatenate([jnp.where(rows == 0, h6, jnp.where(rows == 1, h7, r2[0:8])), r2[8:]], axis=0)
    return xm1, xm2


def _conv_taps_next(du, n0, n1):
    tm = du.shape[0]
    r1 = pltpu.roll(du, tm - 1, 0)
    r2 = pltpu.roll(du, tm - 2, 0)
    rows = lax.broadcasted_iota(jnp.int32, (8, du.shape[1]), 0)
    xp1 = jnp.concatenate([r1[:tm - 8], jnp.where(rows == 7, n0, r1[tm - 8:])], axis=0)
    xp2 = jnp.concatenate([r2[:tm - 8], jnp.where(rows == 6, n0, jnp.where(rows == 7, n1, r2[tm - 8:]))], axis=0)
    return xp1, xp2


def _row(tm, n):
    return pl.BlockSpec((tm, n), lambda i: (i, 0))


def _full(shape):
    nd = len(shape)
    return pl.BlockSpec(tuple(shape), lambda i: (0,) * nd)


def _resident(shape):
    nd = len(shape)
    return pl.BlockSpec(tuple(shape), lambda i: (0,) * nd, pipeline_mode=pl.Buffered(1))


def _heads(tm, h):
    return pl.BlockSpec((h, tm, LANES), lambda i: (0, i, 0))


def _rows_call(name, body, t_rows, tm, ins, outs, scratch=()):
    return pl.pallas_call(
        body, name=name, grid=(t_rows // tm,),
        in_specs=[s for _, s in ins],
        out_specs=[s for _, s in outs],
        out_shape=[s for s, _ in outs],
        scratch_shapes=list(scratch),
        compiler_params=pltpu.CompilerParams(dimension_semantics=("arbitrary",), vmem_limit_bytes=VMEM_LIMIT),
    )(*[a for a, _ in ins])


def _sds(shape, dtype):
    return jax.ShapeDtypeStruct(tuple(shape), dtype)


def _rope_consts():
    c = np.zeros((16, LANES), np.float32)
    lane = np.arange(LANES)
    inv_a = (ROPE_THETA ** (-(np.arange(0, A_HEAD_DIM, 2, dtype=np.float32) / A_HEAD_DIM))).astype(np.float32)
    in_a = lane < A_HEAD_DIM
    c[0, in_a] = inv_a[lane[in_a] % (A_HEAD_DIM // 2)]
    c[1, in_a] = 1.0
    c[2, lane < A_HEAD_DIM // 2] = -1.0
    c[3, (lane >= A_HEAD_DIM // 2) & in_a] = 1.0
    inv_b = (ROPE_THETA ** (-(np.arange(0, ROPE_DIM, 2, dtype=np.float32) / ROPE_DIM))).astype(np.float32)
    pe = (lane >= NOPE_DIM) & (lane < NOPE_DIM + ROPE_DIM)
    c[5, pe] = inv_b[(lane[pe] - NOPE_DIM) % (ROPE_DIM // 2)]
    c[6, pe] = 1.0
    c[7, (lane >= NOPE_DIM) & (lane < NOPE_DIM + ROPE_DIM // 2)] = -1.0
    c[8, (lane >= NOPE_DIM + ROPE_DIM // 2) & (lane < NOPE_DIM + ROPE_DIM)] = 1.0
    c[9, lane < NOPE_DIM] = 1.0
    c[10, pe] = 1.0
    return jnp.asarray(c)


def _rope_tables(pos_f, consts, tm):
    t_rows = pos_f.shape[0]

    def body(pos_ref, c_ref, ca, sa1, sa2, cb, sb1, sb2):
        ang = pos_ref[...] * (c_ref[0:1, :] + c_ref[5:6, :])
        cs, sn = jnp.cos(ang), jnp.sin(ang)
        for ref, row in ((ca, 1), (sa1, 2), (sa2, 3)):
            half = (cs if row == 1 else sn) * c_ref[row:row + 1, :]
            ref[...] = half + pltpu.roll(half, A_HEAD_DIM, 1)
        cb[...] = cs * c_ref[6:7, :] + c_ref[9:10, :]
        sb1[...] = sn * c_ref[7:8, :]
        sb2[...] = sn * c_ref[8:9, :]

    tab = (_sds((t_rows, LANES), F32), _row(tm, LANES))
    return _rows_call("rope_tables", body, t_rows, tm,
                      [(pos_f, _row(tm, 1)), (consts, _full(consts.shape))], [tab] * 6)


def _fwd_in(x, g1, win, bg, gq, gkv, wuq, wk, wv, eq, ek, tabs, tm):
    t_rows = x.shape[0]

    def body(x_ref, g1_ref, win_ref, bg_ref, gq_ref, gkv_ref, wuq_ref, wk_ref, wv_ref, eq_ref, ek_ref,
             ca, sa1, sa2, cb, sb1, sb2,
             h1_ref, qs_ref, ks_ref, vs_ref, cq_ref, cqn_ref, ckv_ref, ckvn_ref, qm_ref, km_ref, vm_ref, gate_ref):
        xn, _ = _rms_stats(x_ref[...])
        hb = (xn * g1_ref[...]).astype(BF16)
        h1_ref[...] = hb
        ta = (ca[...], sa1[...], sa2[...])
        tb = (cb[...], sb1[...], sb2[...])
        cq = _dot(hb, win_ref[:, Z_CQ:Z_CKV])
        ckv = _dot(hb, win_ref[:, Z_CKV:Z_KR])
        z_qa = _dot(hb, win_ref[:, Z_QA:Z_KA])
        z_ka = _dot(hb, win_ref[:, Z_KA:Z_VA])
        z_va = _dot(hb, win_ref[:, Z_VA:Z_CQ])
        z_kr = _dot(hb, win_ref[:, Z_KR:Z_GATE])
        cq_ref[...] = cq
        cqn, _ = _rms_stats(cq)
        cqb = (cqn * gq_ref[...]).astype(BF16)
        cqn_ref[...] = cqb
        ckv_ref[...] = ckv
        ckvn, _ = _rms_stats(ckv)
        ckvb = (ckvn * gkv_ref[...]).astype(BF16)
        ckvn_ref[...] = ckvb
        z_qm = _dot(cqb, wuq_ref[...])
        z_km = _dot(ckvb, wk_ref[...])
        z_vm = _dot(ckvb, wv_ref[...])
        z_gate = _dot(hb, win_ref[:, Z_GATE:ZW])
        qs_ref[...] = _dot((_rope(z_qa, *ta, A_HEAD_DIM // 2) * (SCALE_A * LOG2E)).astype(BF16), eq_ref[...]).astype(BF16)
        ks_ref[...] = _dot(_rope(z_ka, *ta, A_HEAD_DIM // 2).astype(BF16), ek_ref[...]).astype(BF16)
        vs_ref[...] = _dot(z_va.astype(BF16), ek_ref[...]).astype(BF16)
        qm_ref[...] = (_rope(z_qm, *tb, ROPE_DIM // 2) * (SCALE_B * LOG2E)).astype(BF16)
        km_ref[...] = (z_km + _tile_lanes(_rope(z_kr, *tb, ROPE_DIM // 2), HEADS)).astype(BF16)
        vm_ref[...] = z_vm.astype(BF16)
        gate_ref[...] = _sigmoid(z_gate + bg_ref[...]).astype(BF16)

    def o(n, dt):
        return (_sds((t_rows, n), dt), _row(tm, n))

    ins = [(x, _row(tm, D_MODEL)), (g1, _full(g1.shape)), (win, _resident(win.shape)), (bg, _full(bg.shape)),
           (gq, _full(gq.shape)), (gkv, _full(gkv.shape)), (wuq, _full(wuq.shape)), (wk, _full(wk.shape)),
           (wv, _full(wv.shape)), (eq, _full(eq.shape)), (ek, _full(ek.shape))] + [(t, _row(tm, LANES)) for t in tabs]
    outs = [o(1024, BF16), o(1024, BF16), o(256, BF16), o(256, BF16), o(256, F32), o(256, BF16), o(128, F32),
            o(128, BF16), o(1024, BF16), o(1024, BF16), o(1024, BF16), o(2048, BF16)]
    return _rows_call("fwd_in", body, t_rows, tm, ins, outs)


def _attn_tile(t_rows):
    return min(512, t_rows)


MLA_HEADS_PER_STEP = 4
MLA_FWD_HEADS_PER_STEP = 8


def _causal_pairs(nq, by_kv):
    if by_kv:
        pairs = [(i, j) for j in range(nq) for i in range(j, nq)]
    else:
        pairs = [(i, j) for i in range(nq) for j in range(i + 1)]
    return (jnp.asarray([p[0] for p in pairs], jnp.int32), jnp.asarray([p[1] for p in pairs], jnp.int32))


def _mla_fwd(q, k, v):
    t_rows = q.shape[0]
    t = _attn_tile(t_rows)
    hp = MLA_FWD_HEADS_PER_STEP
    w = hp * LANES
    ii, jj = _causal_pairs(t_rows // t, by_kv=False)

    def body(i_ref, j_ref, q_ref, k_ref, v_ref, o_ref, lse_ref, m_s, l_s, acc_s):
        i = i_ref[pl.program_id(1)]
        j = j_ref[pl.program_id(1)]

        @pl.when(j == 0)
        def _():
            m_s[...] = jnp.full(m_s.shape, NEG, F32)
            l_s[...] = jnp.zeros(l_s.shape, F32)
            acc_s[...] = jnp.zeros(acc_s.shape, F32)

        def step(diagonal):
            sls = [slice(hh * LANES, (hh + 1) * LANES) for hh in range(hp)]
            scores = [_dot_nt(k_ref[:, sl], q_ref[:, sl]) for sl in sls]
            if diagonal:
                valid = (lax.broadcasted_iota(jnp.int32, (t, t), 0) <= lax.broadcasted_iota(jnp.int32, (t, t), 1))
                scores = [jnp.where(valid, s, NEG) for s in scores]
            stats = []
            for hh, s in enumerate(scores):
                m_prev = m_s[hh]
                m_new = jnp.maximum(m_prev, jnp.max(s, axis=0, keepdims=True))
                p = jnp.exp2(s - m_new)
                alpha = jnp.exp2(m_prev - m_new)
                stats.append((m_new, alpha, alpha * l_s[hh] + jnp.sum(p, axis=0, keepdims=True), p.astype(BF16)))
            for hh, (m_new, alpha, l_new, p) in enumerate(stats):
                sl = sls[hh]
                acc = alpha * acc_s[hh] + _dot_tn(v_ref[:, sl], p)
                if diagonal:
                    o_ref[:, sl] = (acc / l_new).T.astype(o_ref.dtype)
                    lse_ref[hh] = m_new + jnp.log2(l_new)
                else:
                    m_s[hh] = m_new
                    l_s[hh] = l_new
                    acc_s[hh] = acc

        pl.when(j < i)(lambda: step(False))
        pl.when(j == i)(lambda: step(True))

    grid_spec = pltpu.PrefetchScalarGridSpec(
        num_scalar_prefetch=2, grid=(HEADS // hp, ii.shape[0]),
        in_specs=[pl.BlockSpec((t, w), lambda hb, s, ir, jr: (ir[s], hb)),
                  pl.BlockSpec((t, w), lambda hb, s, ir, jr: (jr[s], hb)),
                  pl.BlockSpec((t, w), lambda hb, s, ir, jr: (jr[s], hb))],
        out_specs=[pl.BlockSpec((t, w), lambda hb, s, ir, jr: (ir[s], hb)),
                   pl.BlockSpec((hp, 1, t), lambda hb, s, ir, jr: (hb, 0, ir[s]))],
        scratch_shapes=[pltpu.VMEM((hp, 1, t), F32), pltpu.VMEM((hp, 1, t), F32), pltpu.VMEM((hp, LANES, t), F32)])
    return pl.pallas_call(
        body, name="mla_fwd", grid_spec=grid_spec,
        out_shape=[_sds((t_rows, HEADS * LANES), BF16), _sds((HEADS, 1, t_rows), F32)],
        compiler_params=pltpu.CompilerParams(dimension_semantics=("arbitrary",) * 2, vmem_limit_bytes=VMEM_LIMIT),
    )(ii, jj, q, k, v)


def _mla_bwd(q, k, v, do, lse, delta):
    t_rows = q.shape[0]
    t = _attn_tile(t_rows)
    hp = MLA_HEADS_PER_STEP
    w = hp * LANES
    ii, jj = _causal_pairs(t_rows // t, by_kv=True)

    def body(i_ref, j_ref, q_ref, k_ref, v_ref, do_ref, lse_ref, dl_ref, dq_ref, dk_ref, dv_ref):
        i = i_ref[pl.program_id(1)]
        j = j_ref[pl.program_id(1)]

        @pl.when(pl.program_id(1) == 0)
        def _():
            dq_ref[...] = jnp.zeros(dq_ref.shape, F32)

        def step(diagonal):
            r0 = pl.multiple_of(i * t, t)
            sls = [slice(hh * LANES, (hh + 1) * LANES) for hh in range(hp)]
            scores = [_dot_nt(k_ref[:, sl], q_ref[:, sl]) for sl in sls]
            if diagonal:
                valid = (lax.broadcasted_iota(jnp.int32, (t, t), 0) <= lax.broadcasted_iota(jnp.int32, (t, t), 1))
                scores = [jnp.where(valid, s, NEG) for s in scores]
            dps = [_dot_nt(v_ref[:, sl], do_ref[:, sl]) for sl in sls]
            ps = [jnp.exp2(s - lse_ref[hh]) for hh, s in enumerate(scores)]
            dss = [(p * (dp - dl_ref[hh])).astype(BF16) for hh, (p, dp) in enumerate(zip(ps, dps))]
            for hh, sl in enumerate(sls):
                dv = _dot(ps[hh].astype(BF16), do_ref[:, sl])
                dk = _dot(dss[hh], q_ref[:, sl]) * (1.0 / LOG2E)
                if diagonal:
                    dv_ref[:, sl] = dv
                    dk_ref[:, sl] = dk
                else:
                    dv_ref[:, sl] += dv
                    dk_ref[:, sl] += dk
                dq_ref[hh, pl.ds(r0, t), :] += _dot_tn(dss[hh], k_ref[:, sl])

        pl.when(i > j)(lambda: step(False))
        pl.when(i == j)(lambda: step(True))

    def qmap(hb, s, ir, jr):
        return (ir[s], hb)

    def kvmap(hb, s, ir, jr):
        return (jr[s], hb)

    def rowmap(hb, s, ir, jr):
        return (hb, 0, ir[s])

    grid_spec = pltpu.PrefetchScalarGridSpec(
        num_scalar_prefetch=2, grid=(HEADS // hp, ii.shape[0]),
        in_specs=[pl.BlockSpec((t, w), qmap), pl.BlockSpec((t, w), kvmap), pl.BlockSpec((t, w), kvmap),
                  pl.BlockSpec((t, w), qmap), pl.BlockSpec((hp, 1, t), rowmap), pl.BlockSpec((hp, 1, t), rowmap)],
        out_specs=[pl.BlockSpec((hp, t_rows, LANES), lambda hb, s, ir, jr: (hb, 0, 0)),
                   pl.BlockSpec((t, w), kvmap), pl.BlockSpec((t, w), kvmap)])
    return pl.pallas_call(
        body, name="mla_bwd", grid_spec=grid_spec,
        out_shape=[_sds((HEADS, t_rows, LANES), F32), _sds((t_rows, HEADS * LANES), F32),
                   _sds((t_rows, HEADS * LANES), F32)],
        compiler_params=pltpu.CompilerParams(dimension_semantics=("arbitrary",) * 2, vmem_limit_bytes=VMEM_LIMIT),
    )(ii, jj, q, k, v, do, lse, delta)


SWA_TILE = 2 * SWA_WINDOW
SWA_GROUP = HEADS // A_KV_HEADS


def _swa_bias(tq):
    koff = lax.broadcasted_iota(jnp.int32, (tq + SWA_WINDOW, SWA_GROUP * tq), 0) - SWA_WINDOW
    qoff = (lax.broadcasted_iota(jnp.int32, (tq + SWA_WINDOW, SWA_GROUP * tq), 1) % tq)
    band = (koff <= qoff) & (qoff - koff < SWA_WINDOW)
    return jnp.stack([jnp.where(band & (koff >= 0), 0.0, NEG), jnp.where(band, 0.0, NEG)]).astype(F32)


def _swa_specs(tq, nq):
    wb = tq // SWA_WINDOW
    kvw = A_KV_HEADS * LANES

    def qi(i):
        return jnp.minimum(i, nq - 1)

    q = pl.BlockSpec((tq, HEADS * LANES), lambda i: (qi(i), 0))
    cur = pl.BlockSpec((tq, kvw), lambda i: (qi(i), 0))
    prev = pl.BlockSpec((SWA_WINDOW, kvw), lambda i: (jnp.maximum(qi(i) * wb - 1, 0), 0))
    bias = pl.BlockSpec((1, tq + SWA_WINDOW, SWA_GROUP * tq), lambda i: (jnp.minimum(i, 1), 0, 0))
    rows = pl.BlockSpec((A_KV_HEADS, 1, 1, SWA_GROUP * tq), lambda i: (0, qi(i), 0, 0))
    sink = pl.BlockSpec((A_KV_HEADS, 1, SWA_GROUP * tq), lambda i: (0, 0, 0))
    return q, cur, prev, bias, rows, sink


def _stack_heads(ref, kvh):
    base = kvh * SWA_GROUP
    return jnp.concatenate([ref[:, (base + g) * LANES:(base + g + 1) * LANES] for g in range(SWA_GROUP)], axis=0)


def _unstack_heads(ref, kvh, val, tq):
    base = kvh * SWA_GROUP
    for g in range(SWA_GROUP):
        ref[:, (base + g) * LANES:(base + g + 1) * LANES] = val[g * tq:(g + 1) * tq].astype(ref.dtype)


def _kv_window(prev_ref, cur_ref, kvh):
    sl = slice(kvh * LANES, (kvh + 1) * LANES)
    return jnp.concatenate([prev_ref[:, sl], cur_ref[:, sl]], axis=0)


def _swa_fwd(q, k, v, bias, sink_rows):
    t_rows = q.shape[0]
    tq = min(SWA_TILE, t_rows)
    nq = t_rows // tq
    qs_, cur, prev, bs, rows, sk = _swa_specs(tq, nq)
    kvhs = range(A_KV_HEADS)

    def body(q_ref, kc_ref, kp_ref, vc_ref, vp_ref, b_ref, sink_ref, o_ref, lse_ref):
        scores = [_dot_nt(_kv_window(kp_ref, kc_ref, h), _stack_heads(q_ref, h)) + b_ref[0] for h in kvhs]
        stats = []
        for h, s in zip(kvhs, scores):
            sink = sink_ref[h] * LOG2E
            m = jnp.maximum(jnp.max(s, axis=0, keepdims=True), sink)
            p = jnp.exp2(s - m)
            l = jnp.sum(p, axis=0, keepdims=True) + jnp.exp2(sink - m)
            lse_ref[h, 0] = m + jnp.log2(l)
            stats.append((p.astype(BF16), l))
        for h, (p, l) in zip(kvhs, stats):
            _unstack_heads(o_ref, h, (_dot_tn(_kv_window(vp_ref, vc_ref, h), p) / l).T, tq)

    return pl.pallas_call(
        body, name="swa_fwd", grid=(nq,),
        in_specs=[qs_, cur, prev, cur, prev, bs, sk],
        out_specs=[qs_, rows],
        out_shape=[_sds((t_rows, HEADS * LANES), BF16), _sds((A_KV_HEADS, nq, 1, SWA_GROUP * tq), F32)],
        compiler_params=pltpu.CompilerParams(dimension_semantics=("arbitrary",), vmem_limit_bytes=VMEM_LIMIT),
    )(q, k, k, v, v, bias, sink_rows)


def _swa_bwd(q, k, v, o, do, lse, bias, sink_rows):
    t_rows = q.shape[0]
    tq = min(SWA_TILE, t_rows)
    nq = t_rows // tq
    qs_, cur, prev, bs, rows, sk = _swa_specs(tq, nq)
    hw = SWA_WINDOW
    kvhs = range(A_KV_HEADS)
    kvw = A_KV_HEADS * LANES

    def body(q_ref, kc_ref, kp_ref, vc_ref, vp_ref, o_ref, do_ref, lse_ref, b_ref, sink_ref,
             dq_ref, dk_ref, dv_ref, dsink_ref, ck, cv, dsa):
        i = pl.program_id(0)

        @pl.when(i == 0)
        def _():
            dsa[...] = jnp.zeros(dsa.shape, F32)

        @pl.when(i < nq)
        def _():
            qs = [_stack_heads(q_ref, h) for h in kvhs]
            dos = [_stack_heads(do_ref, h) for h in kvhs]
            kks = [_kv_window(kp_ref, kc_ref, h) for h in kvhs]
            scores = [_dot_nt(kks[h], qs[h]) for h in kvhs]
            dps = [_dot_nt(_kv_window(vp_ref, vc_ref, h), dos[h]) for h in kvhs]
            ps, dss = [], []
            for h in kvhs:
                lse = lse_ref[h, 0]
                p = jnp.exp2(scores[h] + b_ref[0] - lse)
                delta = jnp.sum((_stack_heads(o_ref, h).astype(F32) * dos[h].astype(F32)).T, axis=0, keepdims=True)
                dsa[h] += -jnp.exp2(sink_ref[h] * LOG2E - lse) * delta
                ps.append(p.astype(BF16))
                dss.append((p * (dps[h] - delta)).astype(BF16))
            for h in kvhs:
                sl = slice(h * LANES, (h + 1) * LANES)
                dv = _dot(ps[h], dos[h])
                dk = _dot(dss[h], qs[h]) * (1.0 / LOG2E)
                _unstack_heads(dq_ref, h, _dot_tn(dss[h], kks[h]), tq)

                @pl.when(i > 0)
                def _():
                    dk_ref[0:tq - hw, sl] = ck[0:tq - hw, sl]
                    dk_ref[tq - hw:tq, sl] = ck[tq - hw:tq, sl] + dk[0:hw]
                    dv_ref[0:tq - hw, sl] = cv[0:tq - hw, sl]
                    dv_ref[tq - hw:tq, sl] = cv[tq - hw:tq, sl] + dv[0:hw]

                ck[:, sl] = dk[hw:hw + tq]
                cv[:, sl] = dv[hw:hw + tq]

        @pl.when(i == nq)
        def _():
            dk_ref[...] = ck[...]
            dv_ref[...] = cv[...]
            dsink_ref[...] = jnp.zeros(dsink_ref.shape, F32)
            for h in kvhs:
                for g in range(SWA_GROUP):
                    tot = jnp.sum(dsa[h, :, g * tq:(g + 1) * tq], axis=1, keepdims=True)
                    dsink_ref[h, g:g + 1, :] = jnp.zeros((1, LANES), F32) + tot

    kv_out = pl.BlockSpec((tq, kvw), lambda i: (jnp.maximum(i - 1, 0), 0))
    return pl.pallas_call(
        body, name="swa_bwd", grid=(nq + 1,),
        in_specs=[qs_, cur, prev, cur, prev, qs_, qs_, rows, bs, sk],
        out_specs=[qs_, kv_out, kv_out, pl.BlockSpec((A_KV_HEADS, 8, LANES), lambda i: (0, 0, 0))],
        out_shape=[_sds((t_rows, HEADS * LANES), F32), _sds((t_rows, kvw), F32), _sds((t_rows, kvw), F32),
                   _sds((A_KV_HEADS, 8, LANES), F32)],
        scratch_shapes=[pltpu.VMEM((tq, kvw), F32), pltpu.VMEM((tq, kvw), F32),
                        pltpu.VMEM((A_KV_HEADS, 1, SWA_GROUP * tq), F32)],
        compiler_params=pltpu.CompilerParams(dimension_semantics=("arbitrary",), vmem_limit_bytes=VMEM_LIMIT),
    )(q, k, k, v, v, o, do, lse, bias, sink_rows)


def _fwd_mix(x, ya, yb, gate, wba, wbb, wout, g2, g3, tm):
    t_rows = x.shape[0]

    def body(x_ref, ya_ref, yb_ref, gate_ref, wba_ref, wbb_ref, wout_ref, g2_ref, g3_ref,
             pa_ref, pb_ref, mixed_ref, o_ref, x1_ref, h2_ref, yac_ref, ybc_ref):
        yac = _fold_slots(ya_ref[...].astype(F32)).astype(BF16)
        ybc = _fold_slots(yb_ref[...].astype(F32)).astype(BF16)
        yac_ref[...] = yac
        ybc_ref[...] = ybc
        pa = _dot(yac, wba_ref[...])
        pb = _dot(ybc, wbb_ref[...])
        pa_ref[...] = pa.astype(BF16)
        pb_ref[...] = pb.astype(BF16)
        mixed = (gate_ref[:, 0:D_MODEL].astype(F32) * pa
                 + gate_ref[:, D_MODEL:2 * D_MODEL].astype(F32) * pb).astype(BF16)
        mixed_ref[...] = mixed
        o = _dot(mixed, wout_ref[...])
        o_ref[...] = o
        on, _ = _rms_stats(o)
        x1 = x_ref[...] + on * g2_ref[...]
        x1_ref[...] = x1
        x1n, _ = _rms_stats(x1)
        h2_ref[...] = (x1n * g3_ref[...]).astype(BF16)

    def o_(dt):
        return (_sds((t_rows, D_MODEL), dt), _row(tm, D_MODEL))

    ins = [(x, _row(tm, D_MODEL)), (ya, _row(tm, 1024)), (yb, _row(tm, 1024)), (gate, _row(tm, 2048)),
           (wba, _resident(wba.shape)), (wbb, _resident(wbb.shape)), (wout, _resident(wout.shape)),
           (g2, _full(g2.shape)), (g3, _full(g3.shape))]
    half = (_sds((t_rows, D_MODEL // 2), BF16), _row(tm, D_MODEL // 2))
    return _rows_call("fwd_mix", body, t_rows, tm, ins,
                      [o_(BF16), o_(BF16), o_(BF16), o_(F32), o_(F32), o_(BF16), half, half])


CONV_CHUNK = 1408


def _fwd_up(h2, wup, convw8, convb, tm):
    t_rows = h2.shape[0]
    cdim = 2 * D_FF

    def body(h2_ref, wup_ref, cw_ref, cb_ref, up_ref, a_ref, u_ref, carry):
        i = pl.program_id(0)

        @pl.when(i == 0)
        def _():
            carry[...] = jnp.zeros(carry.shape, F32)

        hb = h2_ref[...]
        ups = [_dot(hb, wup_ref[s]) for s in range(cdim // CONV_CHUNK)]

        def conv(c0):
            sl = slice(c0, c0 + CONV_CHUNK)
            up = ups[c0 // CONV_CHUNK]
            up_ref[:, sl] = up
            xm1, xm2 = _conv_taps(up, carry[6:7, sl], carry[7:8, sl])
            u = cw_ref[0:1, sl] * xm2 + cw_ref[1:2, sl] * xm1 + cw_ref[2:3, sl] * up + cb_ref[:, sl]
            u_ref[:, sl] = u.astype(BF16)
            carry[:, sl] = up[tm - 8:tm, :]
            return u

        for c0 in range(0, D_FF, CONV_CHUNK):
            ug = conv(c0)
            uv = conv(D_FF + c0)
            gel, _ = _gelu_and_grad(ug)
            a_ref[:, c0:c0 + CONV_CHUNK] = (gel * uv).astype(BF16)

    ins = [(h2, _row(tm, D_MODEL)), (wup, _resident(wup.shape)), (convw8, _full(convw8.shape)),
           (convb, _full(convb.shape))]
    outs = [(_sds((t_rows, cdim), F32), _row(tm, cdim)), (_sds((t_rows, D_FF), BF16), _row(tm, D_FF)),
            (_sds((t_rows, cdim), BF16), _row(tm, cdim))]
    return _rows_call("fwd_up", body, t_rows, tm, ins, outs, scratch=[pltpu.VMEM((8, cdim), F32)])


def _fwd_out(a, wdown, x1, g4, p, wple, g5, wpg, tgt, tm):
    t_rows = a.shape[0]

    def body(a_ref, wdown_ref, x1_ref, g4_ref, p_ref, wple_ref, g5_ref, wpg_ref, tgt_ref,
             ff_ref, x2_ref, e_ref, n5_ref, sg_ref, dx3_ref, loss_ref):
        i = pl.program_id(0)
        ff = _dot(a_ref[...], wdown_ref[...])
        e = _dot(p_ref[...].astype(BF16), wple_ref[...])
        ff_ref[...] = ff
        ffn, _ = _rms_stats(ff)
        x2 = x1_ref[...] + ffn * g4_ref[...]
        x2_ref[...] = x2
        e_ref[...] = e.astype(BF16)
        x2n, _ = _rms_stats(x2)
        n5 = (x2n * g5_ref[...]).astype(BF16)
        n5_ref[...] = n5
        sg = _sigmoid(_dot(n5, wpg_ref[...]))
        sg_ref[...] = sg.astype(BF16)
        d = x2 + sg * e - tgt_ref[...]
        dx3_ref[...] = d * (1.0 / D_MODEL)

        @pl.when(i == 0)
        def _():
            loss_ref[...] = jnp.zeros((1, 1), F32)

        loss_ref[...] += 0.5 * jnp.sum(jnp.sum(d * d, axis=1, keepdims=True), axis=0, keepdims=True) * (1.0 / D_MODEL)

    def o_(dt):
        return (_sds((t_rows, D_MODEL), dt), _row(tm, D_MODEL))

    ins = [(a, _row(tm, D_FF)), (wdown, _resident(wdown.shape)), (x1, _row(tm, D_MODEL)), (g4, _full(g4.shape)),
           (p, _row(tm, PLE_DIM)), (wple, _full(wple.shape)), (g5, _full(g5.shape)), (wpg, _resident(wpg.shape)),
           (tgt, _row(tm, D_MODEL))]
    outs = [o_(F32), o_(F32), o_(BF16), o_(BF16), o_(BF16), o_(F32), (_sds((1, 1), F32), _full((1, 1)))]
    return _rows_call("fwd_out", body, t_rows, tm, ins, outs)


def _bwd_out(dx3, e, sg, x2, ff, g5, g4, wpg, wdown, up, u, tm):
    t_rows = dx3.shape[0]
    cdim = 2 * D_FF
    hb = tm // 8

    def body(dx3_ref, e_ref, sg_ref, x2_ref, ff_ref, g5_ref, g4_ref, wpg_ref, wdown_ref, up_ref, halo_ref, u_ref,
             dpre_ref, de_ref, dx2_ref, dff_ref, du_ref, dg5_ref, dg4_ref, dcb_ref, dcw_ref):
        i = pl.program_id(0)

        @pl.when(i == 0)
        def _():
            dg5_ref[...] = jnp.zeros(dg5_ref.shape, F32)
            dg4_ref[...] = jnp.zeros(dg4_ref.shape, F32)
            dcb_ref[...] = jnp.zeros(dcb_ref.shape, F32)
            dcw_ref[...] = jnp.zeros(dcw_ref.shape, F32)

        dx3 = dx3_ref[...]
        sg = sg_ref[...].astype(F32)
        dpre = (dx3 * e_ref[...].astype(F32) * sg * (1.0 - sg)).astype(BF16)
        dpre_ref[...] = dpre
        de_ref[...] = (dx3 * sg).astype(BF16)
        dn5 = _dot_nt(dpre, wpg_ref[...])
        x2n, r5 = _rms_stats(x2_ref[...])
        d2, dg5 = _rms_bwd(dn5, x2n, r5, g5_ref[...])
        dx2 = dx3 + d2
        dx2_ref[...] = dx2
        dg5_ref[...] += dg5
        ffn, r4 = _rms_stats(ff_ref[...])
        dff, dg4 = _rms_bwd(dx2, ffn, r4, g4_ref[...])
        dg4_ref[...] += dg4
        dffb = dff.astype(BF16)
        dff_ref[...] = dffb
        keep = jnp.where(i > 0, 1.0, 0.0)

        def conv(c0):
            sl = slice(c0, c0 + CONV_CHUNK)
            up = up_ref[:, sl]
            xm1, xm2 = _conv_taps(up, halo_ref[6:7, sl] * keep, halo_ref[7:8, sl] * keep)
            return u_ref[:, sl].astype(F32), up, xm1, xm2

        def grads(c0, du, up, xm1, xm2):
            sl = slice(c0, c0 + CONV_CHUNK)
            du_ref[:, sl] = du.astype(BF16)
            dcb_ref[:, sl] += jnp.sum(du, axis=0, keepdims=True)
            dcw_ref[0:1, sl] += jnp.sum(du * xm2, axis=0, keepdims=True)
            dcw_ref[1:2, sl] += jnp.sum(du * xm1, axis=0, keepdims=True)
            dcw_ref[2:3, sl] += jnp.sum(du * up, axis=0, keepdims=True)

        for c0 in range(0, D_FF, CONV_CHUNK):
            da = _dot_nt(dffb, wdown_ref[c0:c0 + CONV_CHUNK, :])
            ug, *rg = conv(c0)
            uv, *rv = conv(D_FF + c0)
            gel, dgel = _gelu_and_grad(ug)
            grads(c0, da * uv * dgel, *rg)
            grads(D_FF + c0, da * gel, *rv)

    def o_(n, dt):
        return (_sds((t_rows, n), dt), _row(tm, n))

    def acc(r, n):
        return (_sds((r, n), F32), _full((r, n)))

    halo = pl.BlockSpec((8, cdim), lambda i: (jnp.maximum(i * hb - 1, 0), 0))
    ins = [(dx3, _row(tm, D_MODEL)), (e, _row(tm, D_MODEL)), (sg, _row(tm, D_MODEL)), (x2, _row(tm, D_MODEL)),
           (ff, _row(tm, D_MODEL)), (g5, _full(g5.shape)), (g4, _full(g4.shape)), (wpg, _resident(wpg.shape)),
           (wdown, _resident(wdown.shape)), (up, _row(tm, cdim)), (up, halo), (u, _row(tm, cdim))]
    outs = [o_(D_MODEL, BF16), o_(D_MODEL, BF16), o_(D_MODEL, F32), o_(D_MODEL, BF16), o_(cdim, BF16),
            acc(1, D_MODEL), acc(1, D_MODEL), acc(1, cdim), acc(8, cdim)]
    return _rows_call("bwd_out", body, t_rows, tm, ins, outs)


def _bwd_mid(du, convw8, wup, dx2, x1, g3, o, g2, wout, gate, pa, pb, wba, wbb, yb, tm):
    t_rows = du.shape[0]
    cdim = 2 * D_FF
    halo_rows = 16
    hb = tm // halo_rows
    last_blk = t_rows // halo_rows - 1
    n_tiles = t_rows // tm

    def body(du_ref, halo_ref, cw_ref, wup_ref, dx2_ref, x1_ref, g3_ref, o_ref, g2_ref, wout_ref, gate_ref, pa_ref,
             pb_ref, wba_ref, wbb_ref, yb_ref,
             dup_ref, dx1_ref, do_ref, dpa_ref, dpb_ref, dgt_ref, dya_ref, dyb_ref, dl_ref, dg3_ref, dg2_ref, dbg_ref):
        i = pl.program_id(0)

        @pl.when(i == 0)
        def _():
            dg3_ref[...] = jnp.zeros(dg3_ref.shape, F32)
            dg2_ref[...] = jnp.zeros(dg2_ref.shape, F32)
            dbg_ref[...] = jnp.zeros(dbg_ref.shape, F32)

        keep = jnp.where(i < n_tiles - 1, 1.0, 0.0)
        dh2 = jnp.zeros((tm, D_MODEL), F32)
        dups = []
        for c0 in range(0, cdim, CONV_CHUNK):
            sl = slice(c0, c0 + CONV_CHUNK)
            du = du_ref[:, sl].astype(F32)
            nxt = halo_ref[:, sl].astype(F32)
            xp1, xp2 = _conv_taps_next(du, nxt[0:1] * keep, nxt[1:2] * keep)
            dups.append((cw_ref[2:3, sl] * du + cw_ref[1:2, sl] * xp1 + cw_ref[0:1, sl] * xp2).astype(BF16))
            dup_ref[:, sl] = dups[-1]
            if len(dups) > 1:
                dh2 = dh2 + _dot_nt(dups[-2], wup_ref[len(dups) - 2])
        dh2 = dh2 + _dot_nt(dups[-1], wup_ref[len(dups) - 1])
        x1n, r3 = _rms_stats(x1_ref[...])
        d1, dg3 = _rms_bwd(dh2, x1n, r3, g3_ref[...])
        dx1 = dx2_ref[...] + d1
        dx1_ref[...] = dx1
        dg3_ref[...] += dg3
        on, r2 = _rms_stats(o_ref[...])
        do, dg2 = _rms_bwd(dx1, on, r2, g2_ref[...])
        dg2_ref[...] += dg2
        dob = do.astype(BF16)
        do_ref[...] = dob
        dmixed = _dot_nt(dob, wout_ref[...])
        ga = gate_ref[:, 0:D_MODEL].astype(F32)
        gb = gate_ref[:, D_MODEL:2 * D_MODEL].astype(F32)
        dpa = (dmixed * ga).astype(BF16)
        dpb = (dmixed * gb).astype(BF16)
        dpa_ref[...] = dpa
        dpb_ref[...] = dpb
        dga = dmixed * pa_ref[...].astype(F32) * ga * (1.0 - ga)
        dgb = dmixed * pb_ref[...].astype(F32) * gb * (1.0 - gb)
        dgt_ref[:, 0:D_MODEL] = dga.astype(BF16)
        dgt_ref[:, D_MODEL:2 * D_MODEL] = dgb.astype(BF16)
        dbg_ref[:, 0:D_MODEL] += jnp.sum(dga, axis=0, keepdims=True)
        dbg_ref[:, D_MODEL:2 * D_MODEL] += jnp.sum(dgb, axis=0, keepdims=True)
        dya_ref[...] = _spread_slots(_dot_nt(dpa, wba_ref[...])).astype(BF16)
        dyb = _dot_nt(dpb, wbb_ref[...]).astype(BF16)
        dyb_ref[...] = _spread_slots(dyb.astype(F32)).astype(BF16)
        prod = yb_ref[...].astype(F32) * dyb.astype(F32)
        width = HEADS * V_DIM
        lane_head = lax.broadcasted_iota(jnp.int32, (HEADS, width), 1) // V_DIM
        sel = (lane_head == lax.broadcasted_iota(jnp.int32, (HEADS, width), 0)).astype(BF16)
        hi = prod.astype(BF16)
        lo = (prod - hi.astype(F32)).astype(BF16)
        dl_ref[...] = _dot_nt(sel, hi) + _dot_nt(sel, lo)

    def o_(n, dt):
        return (_sds((t_rows, n), dt), _row(tm, n))

    def acc(r, n):
        return (_sds((r, n), F32), _full((r, n)))

    halo = pl.BlockSpec((halo_rows, cdim), lambda i: (jnp.minimum((i + 1) * hb, last_blk), 0))
    ins = [(du, _row(tm, cdim)), (du, halo), (convw8, _full(convw8.shape)), (wup, _resident(wup.shape)),
           (dx2, _row(tm, D_MODEL)), (x1, _row(tm, D_MODEL)), (g3, _full(g3.shape)), (o, _row(tm, D_MODEL)),
           (g2, _full(g2.shape)), (wout, _resident(wout.shape)), (gate, _row(tm, 2048)), (pa, _row(tm, D_MODEL)),
           (pb, _row(tm, D_MODEL)), (wba, _resident(wba.shape)), (wbb, _resident(wbb.shape)),
           (yb, _row(tm, D_MODEL // 2))]
    outs = [o_(cdim, BF16), o_(D_MODEL, F32), o_(D_MODEL, BF16), o_(D_MODEL, BF16), o_(D_MODEL, BF16),
            o_(2048, BF16), o_(1024, BF16), o_(1024, BF16),
            (_sds((HEADS, t_rows), F32), pl.BlockSpec((HEADS, tm), lambda i: (0, i))),
            acc(1, D_MODEL), acc(1, D_MODEL), acc(1, 2048)]
    return _rows_call("bwd_mid", body, t_rows, tm, ins, outs)


def _bwd_in(dqs, dks, dvs, dqm, dkm, dvm, tabs, consts, cq, ckv, gq, gkv, wuq, wk, wv, dgates, win, x, g1, dx1, tm):
    t_rows = x.shape[0]

    def body(dqs_ref, dks_ref, dvs_ref, dqm_ref, dkm_ref, dvm_ref, ca, sa1, sa2, cb, sb1, sb2, c_ref, cq_ref,
             ckv_ref, gq_ref, gkv_ref, wuq_ref, wk_ref, wv_ref, dgt_ref, win_ref, x_ref, g1_ref, dx1_ref,
             dz_ref, dqb_ref, dx_ref, dgq_ref, dgkv_ref, dg1_ref):
        i = pl.program_id(0)

        @pl.when(i == 0)
        def _():
            dgq_ref[...] = jnp.zeros(dgq_ref.shape, F32)
            dgkv_ref[...] = jnp.zeros(dgkv_ref.shape, F32)
            dg1_ref[...] = jnp.zeros(dg1_ref.shape, F32)

        ta = (ca[...], sa1[...], sa2[...])
        tb = (cb[...], sb1[...], sb2[...])

        def piece(lo, hi, val):
            dz_ref[:, lo:hi] = val
            return _dot_nt(val, win_ref[:, lo:hi])

        dh1 = piece(Z_GATE, ZW, dgt_ref[...])
        dkm = dkm_ref[...]
        dckvn = _dot_nt(dkm.astype(BF16), wk_ref[...]) + _dot_nt(dvm_ref[...].astype(BF16), wv_ref[...])
        dh1 = dh1 + piece(Z_VA, Z_CQ, _fold_slots(dvs_ref[...]).astype(BF16))
        dqm = jnp.concatenate([dqm_ref[h] for h in range(HEADS)], axis=1)
        dqb = _rope_t(dqm * SCALE_B, *tb, ROPE_DIM // 2).astype(BF16)
        dqb_ref[...] = dqb
        dcqn = _dot_nt(dqb, wuq_ref[...])
        dqa = _rope_t(_fold_slots(dqs_ref[...]) * SCALE_A, *ta, A_HEAD_DIM // 2)
        dh1 = dh1 + piece(Z_QA, Z_KA, dqa.astype(BF16))
        dh1 = dh1 + piece(Z_KA, Z_VA, _rope_t(_fold_slots(dks_ref[...]), *ta, A_HEAD_DIM // 2).astype(BF16))
        ckvn, rkv = _rms_stats(ckv_ref[...])
        dckv, dgkv = _rms_bwd(dckvn, ckvn, rkv, gkv_ref[...])
        dgkv_ref[...] += dgkv
        dh1 = dh1 + piece(Z_CKV, Z_KR, dckv.astype(BF16))
        dslot = dkm[:, 0:LANES]
        for h in range(1, HEADS):
            dslot = dslot + dkm[:, h * LANES:(h + 1) * LANES]
        dh1 = dh1 + piece(Z_KR, Z_GATE, _rope_t(dslot * c_ref[10:11, :], *tb, ROPE_DIM // 2).astype(BF16))
        cqn, rq = _rms_stats(cq_ref[...])
        dcq, dgq = _rms_bwd(dcqn, cqn, rq, gq_ref[...])
        dgq_ref[...] += dgq
        dh1 = dh1 + piece(Z_CQ, Z_CKV, dcq.astype(BF16))
        xn, r1 = _rms_stats(x_ref[...])
        d0, dg1 = _rms_bwd(dh1, xn, r1, g1_ref[...])
        dg1_ref[...] += dg1
        dx_ref[...] = dx1_ref[...] + d0

    def acc(n):
        return (_sds((1, n), F32), _full((1, n)))

    ins = [(dqs, _row(tm, 1024)), (dks, _row(tm, 256)), (dvs, _row(tm, 256)), (dqm, _heads(tm, HEADS)),
           (dkm, _row(tm, 1024)), (dvm, _row(tm, 1024))] + [(t, _row(tm, LANES)) for t in tabs] + [
           (consts, _full(consts.shape)), (cq, _row(tm, 256)), (ckv, _row(tm, 128)), (gq, _full(gq.shape)),
           (gkv, _full(gkv.shape)), (wuq, _full(wuq.shape)), (wk, _full(wk.shape)), (wv, _full(wv.shape)),
           (dgates, _row(tm, 2048)), (win, _resident(win.shape)), (x, _row(tm, D_MODEL)), (g1, _full(g1.shape)),
           (dx1, _row(tm, D_MODEL))]
    outs = [(_sds((t_rows, ZW), BF16), _row(tm, ZW)), (_sds((t_rows, 1024), BF16), _row(tm, 1024)),
            (_sds((t_rows, D_MODEL), F32), _row(tm, D_MODEL)), acc(256), acc(128), acc(D_MODEL)]
    return _rows_call("bwd_in", body, t_rows, tm, ins, outs)


def _pick_cols(n):
    best = LANES
    for d in range(LANES, min(n, 1664) + 1, LANES):
        if n % d == 0:
            best = d
    return best


def _mm_tn(name, a, b, column_shards=1, after=None):
    t_rows, m = a.shape
    n = b.shape[1]
    bk = min(2048, t_rows)
    bm, bn = _pick_cols(m), _pick_cols(n // column_shards)
    per_shard = n // column_shards // bn
    extra = () if after is None else (after,)

    def body(a_ref, b_ref, *rest):
        o_ref = rest[-1]

        @pl.when(pl.program_id(2) == 0)
        def _():
            o_ref[...] = jnp.zeros((bm, bn), F32)

        o_ref[...] += _dot_tn(a_ref[...].astype(BF16), b_ref[...].astype(BF16))

    return pl.pallas_call(
        body, name=name, grid=(m // bm, n // bn, t_rows // bk),
        in_specs=[pl.BlockSpec((bk, bm), lambda i, j, k: (k, i)), pl.BlockSpec((bk, bn), lambda i, j, k: (k, j))]
        + [pl.BlockSpec((8, LANES), lambda i, j, k: (0, 0))] * len(extra),
        out_specs=(pl.BlockSpec((bm, bn), lambda i, j, k: (i, j)) if column_shards == 1 else
                   pl.BlockSpec((None, bm, bn), lambda i, j, k: (j // per_shard, i, j % per_shard))),
        out_shape=_sds((m, n) if column_shards == 1 else (column_shards, m, n // column_shards), F32),
        compiler_params=pltpu.CompilerParams(dimension_semantics=("arbitrary",) * 3, vmem_limit_bytes=VMEM_LIMIT),
    )(a, b, *extra)


PACK_ROWS = 512


ADD_TILE_ELEMS = 1 << 17


def _add_rows(rows, cols):
    best = 16
    for d in range(16, rows + 1, 16):
        if rows % d == 0 and d * cols <= ADD_TILE_ELEMS:
            best = d
    assert rows % best == 0
    return best


def _add_pair(name, g, recv, half):
    _, _, rows, cols = g.shape
    t = _add_rows(rows, cols)

    def body(h_ref, g_ref, r_ref, o_ref):
        o_ref[...] = (g_ref[:, 0] + r_ref[...]).astype(BF16)

    spec = pl.BlockSpec((4, t, cols), lambda i, h: (0, i, 0))
    grid_spec = pltpu.PrefetchScalarGridSpec(
        num_scalar_prefetch=1, grid=(rows // t,),
        in_specs=[pl.BlockSpec((4, 1, t, cols), lambda i, h: (0, h[0], i, 0)), spec], out_specs=spec)
    return pl.pallas_call(body, name=name, grid_spec=grid_spec,
                          out_shape=_sds(recv.shape, BF16))(jnp.reshape(half, (1,)).astype(jnp.int32), g, recv)


def _add_chips(name, parts):
    _, rows, cols = parts.shape
    t = _add_rows(rows, cols)

    def body(p_ref, o_ref):
        acc = p_ref[0].astype(F32)
        for j in range(1, 4):
            acc = acc + p_ref[j].astype(F32)
        o_ref[...] = acc

    return pl.pallas_call(body, name=name, grid=(rows // t,),
                          in_specs=[pl.BlockSpec((4, t, cols), lambda i: (0, i, 0))],
                          out_specs=pl.BlockSpec((t, cols), lambda i: (i, 0)),
                          out_shape=_sds((rows, cols), F32))(parts)


def _add_devices(parts):
    n, rows, _ = parts.shape

    def body(p_ref, o_ref):
        acc = p_ref[0]
        for j in range(1, n):
            acc = acc + p_ref[j]
        o_ref[...] = acc

    return pl.pallas_call(body, name="small_add", grid=(1,),
                          in_specs=[pl.BlockSpec((n, rows, LANES), lambda i: (0, 0, 0))],
                          out_specs=pl.BlockSpec((rows, LANES), lambda i: (0, 0)),
                          out_shape=_sds((rows, LANES), F32))(parts)


def _adam_rows(k, n):
    target = max(8, (1 << 20) // (4 * n))
    if k <= target:
        return k
    best = None
    for d in range(8, target + 1, 8):
        if k % d == 0:
            best = d
    return best if best is not None else k


def _adam_update(w, g, m, v):
    m_ = ADAM_B1 * m + (1.0 - ADAM_B1) * g
    v_ = ADAM_B2 * v + (1.0 - ADAM_B2) * (g * g)
    delta = -ADAM_LR * ((m_ / (1.0 - ADAM_B1 ** ADAM_STEP)) / (jnp.sqrt(v_ / (1.0 - ADAM_B2 ** ADAM_STEP)) + ADAM_EPS)
                        + ADAM_WD * w)
    return delta, m_, v_


def _adamw_many(name, ws, gs, ms, vs):
    n = len(ws)

    def body(*refs):
        for i in range(n):
            w_ref, g_ref, m_ref, v_ref = (refs[k * n + i] for k in range(4))
            d_ref, mo_ref, vo_ref = (refs[(4 + k) * n + i] for k in range(3))
            d_ref[...], mo_ref[...], vo_ref[...] = _adam_update(w_ref[...], g_ref[...], m_ref[...], v_ref[...])

    specs = [pl.BlockSpec(w.shape, lambda i: (0, 0)) for w in ws]
    out = pl.pallas_call(body, name=name, grid=(1,), in_specs=specs * 4, out_specs=specs * 3,
                         out_shape=[_sds(w.shape, F32) for w in ws] * 3)(*ws, *gs, *ms, *vs)
    return [(gs[i], out[i], out[n + i], out[2 * n + i]) for i in range(n)]


def _adamw_halves(name, w, mine, theirs, m, v, half):
    k, n = w.shape
    bk = _adam_rows(k // 2, n)
    nb = k // 2 // bk

    def body(h_ref, w_ref, mine_ref, theirs_ref, m_ref, v_ref, g_ref, d_ref, mo_ref, vo_ref):
        g = jnp.where(pl.program_id(0) == h_ref[0], mine_ref[...], theirs_ref[...])
        g_ref[...] = g
        d_ref[...], mo_ref[...], vo_ref[...] = _adam_update(w_ref[...], g, m_ref[...], v_ref[...])

    full = pl.BlockSpec((bk, n), lambda h, i, c: (h * nb + i, 0))
    part = pl.BlockSpec((bk, n), lambda h, i, c: (i, 0))
    grid_spec = pltpu.PrefetchScalarGridSpec(num_scalar_prefetch=1, grid=(2, nb),
                                             in_specs=[full, part, part, full, full], out_specs=[full] * 4)
    return tuple(pl.pallas_call(
        body, name=name, grid_spec=grid_spec, out_shape=[_sds((k, n), F32)] * 4,
        compiler_params=pltpu.CompilerParams(vmem_limit_bytes=VMEM_LIMIT),
    )(jnp.reshape(half, (1,)).astype(jnp.int32), w, mine, theirs, m, v))


_HBM = pl.BlockSpec(memory_space=pltpu.HBM)


def _me():
    return lax.axis_index("x"), lax.axis_index("y"), lax.axis_index("c")


def _other_chips(x, y):
    return [(1 - x, y), (x, 1 - y), (1 - x, 1 - y)]


def _pass_to_sibling(zones):
    n = len(zones)

    def body(*refs):
        in_refs, out_refs = refs[:n], refs[n:2 * n]
        send_sems, recv_sems = refs[2 * n:]
        x, y, c = _me()
        sent = []
        for a, (in_ref, out_ref) in enumerate(zip(in_refs, out_refs)):
            for j, (cx, cy) in enumerate(_other_chips(x, y)):
                mine, theirs = (2 * cx + cy, c), (2 * cx + cy, 1 - c)
                sems = dict(send_sem=send_sems.at[3 * a + j], recv_sem=recv_sems.at[3 * a + j],
                            device_id=(x, y, 1 - c), device_id_type=MESH)
                sent.append(tuple(pltpu.make_async_remote_copy(src_ref=in_ref.at[part], dst_ref=out_ref.at[part], **sems)
                                  for part in (mine, theirs)))
        for send, _ in sent:
            send.start()
        for _, recv in sent:
            recv.wait_recv()
        for send, _ in sent:
            send.wait_send()

    return pl.pallas_call(
        body, name="pass_to_sibling", out_shape=[_sds(z.shape, z.dtype) for z in zones],
        in_specs=[_HBM] * n, out_specs=[_HBM] * n, input_output_aliases={i: i for i in range(n)},
        scratch_shapes=[pltpu.SemaphoreType.DMA((3 * n,)), pltpu.SemaphoreType.DMA((3 * n,))],
    )(*zones)


def _swap_sibling(name, vs, other_half=False):
    n = len(vs)

    def body(*refs):
        v_refs, out_refs = refs[:n], refs[n:2 * n]
        send_sems, recv_sems = refs[2 * n:]
        x, y, c = _me()
        cps = [pltpu.make_async_remote_copy(src_ref=v_ref.at[:, 1 - c] if other_half else v_ref, dst_ref=out_ref,
                                            send_sem=send_sems.at[a], recv_sem=recv_sems.at[a],
                                            device_id=(x, y, 1 - c), device_id_type=MESH)
               for a, (v_ref, out_ref) in enumerate(zip(v_refs, out_refs))]
        for cp in cps:
            cp.start()
        for cp in cps:
            cp.wait()

    def landing(v):
        return _sds((v.shape[0],) + v.shape[2:] if other_half else v.shape, v.dtype)

    return pl.pallas_call(
        body, name=name, out_shape=[landing(v) for v in vs], in_specs=[_HBM] * n, out_specs=[_HBM] * n,
        scratch_shapes=[pltpu.SemaphoreType.DMA((n,)), pltpu.SemaphoreType.DMA((n,))],
    )(*vs)


_SEM = pl.BlockSpec(memory_space=pltpu.SEMAPHORE)
_EFFECT = pltpu.SideEffectType.DATAFLOW_SIDE_EFFECTING
WHOLE = "whole"
PIECE = "piece"
SIBLING_HALF = "sibling"
MY_HALF = "half"
EVERYONE = "everyone"
_COPIES = {WHOLE: 3, PIECE: 3, MY_HALF: 3, SIBLING_HALF: 1, EVERYONE: 7}


def _landing_shape(v, mode):
    return {WHOLE: (4,) + v.shape, MY_HALF: (4,) + v.shape, PIECE: v.shape, EVERYONE: (8,) + v.shape,
            SIBLING_HALF: (v.shape[0],) + v.shape[2:]}[mode]


def _chip_copies(v_ref, land_ref, send_sems, recv_sems, mode, sem0=0):
    x, y, c = _me()
    if mode == SIBLING_HALF:
        cp = pltpu.make_async_remote_copy(src_ref=v_ref.at[:, 1 - c], dst_ref=land_ref, send_sem=send_sems.at[sem0],
                                          recv_sem=recv_sems.at[sem0], device_id=(x, y, 1 - c), device_id_type=MESH)
        return [(cp, cp)]
    if mode == EVERYONE:
        out = []
        for f in range(1, 8):
            px, py, pc = (1 - x if f & 4 else x), (1 - y if f & 2 else y), (1 - c if f & 1 else c)
            sems = dict(send_sem=send_sems.at[sem0 + f - 1], recv_sem=recv_sems.at[sem0 + f - 1],
                        device_id=(px, py, pc), device_id_type=MESH)
            out.append((pltpu.make_async_remote_copy(src_ref=v_ref, dst_ref=land_ref.at[4 * x + 2 * y + c], **sems),
                        pltpu.make_async_remote_copy(src_ref=v_ref, dst_ref=land_ref.at[4 * px + 2 * py + pc], **sems)))
        return out
    k = 2 * x + y
    out = []
    for j, (cx, cy) in enumerate(_other_chips(x, y)):
        if mode == MY_HALF:
            src, mine, theirs = v_ref.at[c], land_ref.at[k, c], land_ref.at[2 * cx + cy, c]
        else:
            src = v_ref.at[2 * cx + cy] if mode == PIECE else v_ref
            mine, theirs = land_ref.at[k], land_ref.at[2 * cx + cy]
        sems = dict(send_sem=send_sems.at[sem0 + j], recv_sem=recv_sems.at[sem0 + j], device_id=(cx, cy, c),
                    device_id_type=MESH)
        send = pltpu.make_async_remote_copy(src_ref=src, dst_ref=mine, **sems)
        recv = pltpu.make_async_remote_copy(src_ref=src, dst_ref=theirs, **sems)
        out.append((send, recv))
    return out


def _chips_start(name, vs, mode, after=None):
    n = len(vs)
    lands = [_landing_shape(v, mode) for v in vs]

    def body(*refs):
        v_refs, land_refs = refs[:n], refs[n:2 * n]
        send_sems, recv_sems = refs[-2 * n - 3], refs[-2 * n - 2]
        token = refs[-1]
        for a in range(n):
            for send, _ in _chip_copies(v_refs[a], land_refs[a], send_sems, recv_sems, mode, _COPIES[mode] * a):
                send.start()
        token[...] = jnp.zeros_like(token)

    extra = () if after is None else (after,)
    hbm = [pltpu.with_memory_space_constraint(v, pltpu.HBM) for v in vs]
    zones = [pltpu.with_memory_space_constraint(lax.empty(s, v.dtype), pltpu.HBM) for s, v in zip(lands, vs)]
    out = pl.pallas_call(
        body, name=name,
        out_shape=(pltpu.SemaphoreType.DMA((_COPIES[mode] * n,)), pltpu.SemaphoreType.DMA((_COPIES[mode] * n,)),
                   *[pltpu.HBM(v.shape, v.dtype) for v in vs], *[pltpu.HBM(s, v.dtype) for s, v in zip(lands, vs)],
                   _sds((8, LANES), F32)),
        in_specs=(_HBM,) * (2 * n) + (pl.BlockSpec(memory_space=pl.ANY),) * len(extra),
        out_specs=(_SEM, _SEM) + (_HBM,) * (2 * n) + (pl.BlockSpec(memory_space=pltpu.VMEM),),
        input_output_aliases={i: 2 + i for i in range(2 * n)},
        compiler_params=pltpu.CompilerParams(has_side_effects=_EFFECT),
    )(*hbm, *zones, *extra)
    return out[0], out[1], list(out[2:2 + n]), list(out[2 + n:2 + 2 * n]), out[-1]


def _chips_wait(name, send_sems, recv_sems, v_thru, land_thru, mode, after):
    n = len(v_thru)

    def body(*refs):
        v_refs, land_refs = refs[:n], refs[n:2 * n]
        send_sems, recv_sems = refs[2 * n], refs[2 * n + 1]
        for a in range(n):
            for send, recv in _chip_copies(v_refs[a], land_refs[a], send_sems, recv_sems, mode, _COPIES[mode] * a):
                send.wait_send()
                recv.wait_recv()

    out = pl.pallas_call(
        body, name=name,
        out_shape=tuple(pltpu.HBM(a.shape, a.dtype) for a in list(v_thru) + list(land_thru)),
        in_specs=(_HBM,) * (2 * n) + (_SEM, _SEM, pl.BlockSpec(memory_space=pl.ANY)), out_specs=(_HBM,) * (2 * n),
        input_output_aliases={i: i for i in range(2 * n)},
        compiler_params=pltpu.CompilerParams(has_side_effects=_EFFECT),
    )(*v_thru, *land_thru, send_sems, recv_sems, after)
    return list(out[:n]), list(out[n:])


_BIG = (("w_in", (1024, 3232), 1), ("w_uq", (256, 768), 1), ("w_ukv", (128, 1024), 1), ("w_branch_a", (512, 1024), 1),
        ("w_branch_b", (512, 1024), 1), ("w_out", (1024, 1024), 0), ("w_up", (1024, 5632), 1),
        ("w_down", (2816, 1024), 0), ("w_ple_gate", (1024, 1024), 0), ("w_ple", (256, 1024), 1))


def _shard_shape(shape, axis):
    return (shape[0] // 4, shape[1]) if axis == 0 else (shape[0], shape[1] // 4)


def _half_rows(shape, axis):
    k, n = _shard_shape(shape, axis)
    return k * n // (2 * LANES)


_EARLY = ("w_in", "w_uq", "w_ukv")
_LATE = ("w_branch_a", "w_branch_b", "w_out", "w_up", "w_down", "w_ple_gate", "w_ple")
_NATURAL = ("w_in", "w_up", "w_down", "w_out", "w_ple_gate")
_EARLY_PACKED = tuple(b for b in _BIG if b[0] in _EARLY and b[0] not in _NATURAL)
_LATE_PACKED = tuple(b for b in _BIG if b[0] in _LATE and b[0] not in _NATURAL)


def _halves(a):
    return a.reshape(a.shape[:-2] + (2, a.shape[-2] // 2, a.shape[-1]))


def _rows_joined(a):
    return a.reshape(a.shape[:-3] + (a.shape[-3] * a.shape[-2], a.shape[-1]))


def _pack_pad(group):
    return -sum(_half_rows(shape, axis) for _, shape, axis in group) % PACK_ROWS


def _pack_shards(shards, dtype, group):
    parts = [shards[name].astype(dtype).reshape(2, _half_rows(shape, axis), LANES) for name, shape, axis in group]
    return jnp.concatenate(parts + [jnp.zeros((2, _pack_pad(group), LANES), dtype)], axis=1)


def _unpack_gathered(g, group):
    out, off = {}, 0
    for name, shape, axis in group:
        r = _half_rows(shape, axis)
        k, n = _shard_shape(shape, axis)
        w = g[:, :, off:off + r, :].reshape(4, k, n)
        out[name] = w.reshape(shape) if axis == 0 else w.transpose(1, 0, 2).reshape(shape)
        off += r
    return out


def _pack_grads(grads, group):
    parts = []
    for name, shape, axis in group:
        k, n = _shard_shape(shape, axis)
        g = grads[name]
        g4 = g.reshape(4, k, n) if axis == 0 else g.reshape(k, 4, n).transpose(1, 0, 2)
        parts.append(g4.reshape(4, 2, _half_rows(shape, axis), LANES))
    return jnp.concatenate(parts + [jnp.zeros((4, 2, _pack_pad(group), LANES), F32)], axis=2)


def _unpack_shard_grads(f, group):
    out, off = {}, 0
    for name, shape, axis in group:
        r = _half_rows(shape, axis)
        out[name] = f[:, off:off + r, :].reshape(_shard_shape(shape, axis))
        off += r
    return out


def _pad_slots(w, heads, dim):
    k = w.shape[0]
    return jnp.pad(w.reshape(k, heads, dim), ((0, 0), (0, 0), (0, LANES - dim))).reshape(k, heads * LANES)


def _unpad_slots(w, heads, dim):
    k = w.shape[0]
    return w.reshape(k, heads, LANES)[:, :, :dim].reshape(k, heads * dim)


def _pad_w_in(w):
    kr = jnp.pad(w[:, Z_KR:Z_KR + ROPE_DIM], ((0, 0), (NOPE_DIM, LANES - NOPE_DIM - ROPE_DIM)))
    return jnp.concatenate([w[:, :Z_KR], kr, w[:, Z_KR + ROPE_DIM:]], axis=1)


def _unpad_w_in(w):
    return jnp.concatenate([w[:, :Z_KR], w[:, Z_KR + NOPE_DIM:Z_KR + NOPE_DIM + ROPE_DIM], w[:, Z_GATE:ZW]], axis=1)


def _spread_matrix(heads, dim):
    row = lax.broadcasted_iota(jnp.int32, (heads * dim, heads * LANES), 0)
    col = lax.broadcasted_iota(jnp.int32, (heads * dim, heads * LANES), 1)
    return (col == (row // dim) * LANES + row % dim).astype(BF16)


_SMALL = (("attn_pre_norm", 1024), ("attn_post_norm", 1024), ("b_gate", 2048), ("sinks", 8), ("q_a_norm", 256),
          ("kv_a_norm", 128), ("mlp_pre_norm", 1024), ("mlp_post_norm", 1024), ("conv_b", 5632), ("ple_norm", 1024),
          ("conv_w", 3 * 5632), ("loss", 1))


def _small_rows(n):
    return 8 * -(-n // (8 * LANES))


def _pack_small(vals):
    parts = []
    for name, n in _SMALL:
        r = _small_rows(n)
        parts.append(jnp.pad(vals[name].reshape(-1), (0, r * LANES - n)).reshape(r, LANES))
    return jnp.concatenate(parts, axis=0)


def _unpack_small(buf):
    out, off = {}, 0
    for name, n in _SMALL:
        r = _small_rows(n)
        out[name] = buf[off:off + r].reshape(-1)[:n]
        off += r
    return out


def kernel(x, p, positions, attn_pre_norm, attn_post_norm, w_in, b_gate, sinks, q_a_norm, w_uq, kv_a_norm, w_ukv, w_branch_a, w_branch_b, w_out, mlp_pre_norm, mlp_post_norm, w_up, conv_w, conv_b, w_down, ple_norm, w_ple_gate, w_ple, loss_target, m_attn_pre_norm, m_attn_post_norm, m_w_in, m_b_gate, m_sinks, m_q_a_norm, m_w_uq, m_kv_a_norm, m_w_ukv, m_w_branch_a, m_w_branch_b, m_w_out, m_mlp_pre_norm, m_mlp_post_norm, m_w_up, m_conv_w, m_conv_b, m_w_down, m_ple_norm, m_w_ple_gate, m_w_ple, v_attn_pre_norm, v_attn_post_norm, v_w_in, v_b_gate, v_sinks, v_q_a_norm, v_w_uq, v_kv_a_norm, v_w_ukv, v_w_branch_a, v_w_branch_b, v_w_out, v_mlp_pre_norm, v_mlp_post_norm, v_w_up, v_conv_w, v_conv_b, v_w_down, v_ple_norm, v_w_ple_gate, v_w_ple):
    names = ["attn_pre_norm", "attn_post_norm", "w_in", "b_gate", "sinks", "q_a_norm", "w_uq", "kv_a_norm", "w_ukv",
             "w_branch_a", "w_branch_b", "w_out", "mlp_pre_norm", "mlp_post_norm", "w_up", "conv_w", "conv_b",
             "w_down", "ple_norm", "w_ple_gate", "w_ple"]
    wts = dict(zip(names, [attn_pre_norm, attn_post_norm, w_in, b_gate, sinks, q_a_norm, w_uq, kv_a_norm, w_ukv,
                           w_branch_a, w_branch_b, w_out, mlp_pre_norm, mlp_post_norm, w_up, conv_w, conv_b, w_down,
                           ple_norm, w_ple_gate, w_ple]))
    moms = dict(zip(names, [m_attn_pre_norm, m_attn_post_norm, m_w_in, m_b_gate, m_sinks, m_q_a_norm, m_w_uq,
                            m_kv_a_norm, m_w_ukv, m_w_branch_a, m_w_branch_b, m_w_out, m_mlp_pre_norm,
                            m_mlp_post_norm, m_w_up, m_conv_w, m_conv_b, m_w_down, m_ple_norm, m_w_ple_gate, m_w_ple]))
    vars_ = dict(zip(names, [v_attn_pre_norm, v_attn_post_norm, v_w_in, v_b_gate, v_sinks, v_q_a_norm, v_w_uq,
                             v_kv_a_norm, v_w_ukv, v_w_branch_a, v_w_branch_b, v_w_out, v_mlp_pre_norm,
                             v_mlp_post_norm, v_w_up, v_conv_w, v_conv_b, v_w_down, v_ple_norm, v_w_ple_gate, v_w_ple]))
    w2 = {n: a.reshape(a.shape[-2:]) for n, a in wts.items()}
    m2 = {n: a.reshape(a.shape[-2:]) for n, a in moms.items()}
    v2 = {n: a.reshape(a.shape[-2:]) for n, a in vars_.items()}

    t_rows = x.shape[-2]
    tm = min(256, t_rows)
    tm_wide = min(512, t_rows)
    xc, yc, cc = lax.axis_index("x"), lax.axis_index("y"), lax.axis_index("c")
    chip = 2 * xc + yc

    x2d = x.reshape(t_rows, D_MODEL)
    p2d = p.reshape(t_rows, PLE_DIM)
    tgt = loss_target.reshape(t_rows, D_MODEL)
    pos_f = positions.reshape(t_rows, 1).astype(F32)

    def own_slot_filled(gathered, mine):
        return [lax.dynamic_update_slice(g, m[None], (chip, 0, 0, 0)) for g, m in zip(gathered, mine)]

    def shard_lists(group, packed_group, token=0.0):
        ws = {n: w2[n] + token for n in group}
        return [_halves(ws[n].astype(BF16)) for n in group if n in _NATURAL] + [_pack_shards(ws, BF16, packed_group)]

    cw_rows = 3 * 1408 // LANES
    conv_mine = jnp.pad(w2["conv_w"].reshape(cw_rows, LANES), ((0, 48 - cw_rows), (0, 0))).reshape(2, 24, LANES)
    early_mine = shard_lists(_EARLY, _EARLY_PACKED) + [conv_mine]
    early_sems = _chips_start("gather_early_start", early_mine, MY_HALF)
    early_token = early_sems[4][0:1, 0:1]
    consts = _rope_consts()
    tabs = _rope_tables(pos_f + early_token, consts, tm)
    late_mine = shard_lists(_LATE, _LATE_PACKED, early_token)
    both_done = tabs[0][0:1, 0:1] + sum(m[0, 0:1, 0:1].astype(F32) for m in late_mine)
    early_sent, early_landed = _chips_wait("gather_early_wait", *early_sems[:4], MY_HALF, after=both_done)
    early = own_slot_filled(_pass_to_sibling(early_landed), early_sent)
    late_names = [n for n in _LATE if n in _NATURAL]
    first = [late_names.index("w_out"), len(late_names)]
    late_a = [late_mine[i] for i in first]
    late_b = [m for i, m in enumerate(late_mine) if i not in first]
    late_a_sems = _chips_start("gather_late_a_start", late_a, WHOLE, after=early[0])
    late_b_sems = _chips_start("gather_late_b_start", late_b, WHOLE, after=late_a_sems[4])
    late_token = late_b_sems[4][0:1, 0:1]
    full = _unpack_gathered(early[1], _EARLY_PACKED)
    full["w_in"] = _rows_joined(early[0]).transpose(1, 0, 2).reshape(D_MODEL, 3232)
    conv_full = early[2].reshape(4, 48, LANES)[:, :cw_rows].reshape(4, 3, 1408).transpose(1, 0, 2).reshape(3, 2 * D_FF)
    convw8 = jnp.pad(conv_full, ((0, 5), (0, 0)))

    win = _pad_w_in(full["w_in"])
    wuq = _pad_slots(full["w_uq"], HEADS, NOPE_DIM + ROPE_DIM)
    ukv = full["w_ukv"].reshape(KV_LORA, HEADS, NOPE_DIM + V_DIM)
    wk = _pad_slots(ukv[:, :, :NOPE_DIM].reshape(KV_LORA, HEADS * NOPE_DIM), HEADS, NOPE_DIM)
    wv = _pad_slots(ukv[:, :, NOPE_DIM:].reshape(KV_LORA, HEADS * V_DIM), HEADS, V_DIM)
    g1, g2, g3, g4, g5 = (w2["attn_pre_norm"], w2["attn_post_norm"], w2["mlp_pre_norm"], w2["mlp_post_norm"],
                          w2["ple_norm"])
    gq, gkv, bg, convb = w2["q_a_norm"], w2["kv_a_norm"], w2["b_gate"], w2["conv_b"]
    swa_tile = min(SWA_TILE, t_rows)
    sink_rows = jnp.repeat(w2["sinks"].reshape(A_KV_HEADS, SWA_GROUP, 1), swa_tile, axis=2).reshape(
        A_KV_HEADS, 1, SWA_GROUP * swa_tile)
    swa_bias = _swa_bias(swa_tile)
    spread_q = _spread_matrix(HEADS, A_HEAD_DIM)
    spread_kv = _spread_matrix(A_KV_HEADS, A_HEAD_DIM)

    h1, qs, ks, vs, cq, cqn, ckv, ckvn, qm, km, vm, gate = _fwd_in(x2d, g1, win, bg + late_token, gq, gkv, wuq, wk, wv,
                                                                   spread_q, spread_kv, tabs, tm_wide)
    ya, lse_a = _swa_fwd(qs, ks, vs, swa_bias, sink_rows)
    yb, lse_b = _mla_fwd(qm, km, vm)
    late_sent, late_landed = _chips_wait("gather_late_a_wait", *late_a_sems[:4], WHOLE, after=yb)
    wout_g, packed_g = own_slot_filled(late_landed, late_sent)
    full = _unpack_gathered(packed_g, _LATE_PACKED)
    wba, wbb = full["w_branch_a"], full["w_branch_b"]
    wple = full["w_ple"]
    wout = _rows_joined(wout_g).reshape(-1, D_MODEL)
    pa, pb, mixed, o, x1, h2, ya_c, yb_c = _fwd_mix(x2d, ya, yb, gate, wba, wbb, wout, g2, g3, tm_wide)
    late_sent, late_landed = _chips_wait("gather_late_b_wait", *late_b_sems[:4], WHOLE, after=pa)
    natural = dict(zip([n for n in late_names if n != "w_out"], own_slot_filled(late_landed, late_sent)))
    wup = _rows_joined(natural["w_up"])
    wdown, wpg = (_rows_joined(natural[n]).reshape(-1, D_MODEL) for n in ("w_down", "w_ple_gate"))
    up, a, u = _fwd_up(h2, wup, convw8, convb, tm)
    ff, x2, e, n5, sg, dx3, loss_part = _fwd_out(a, wdown, x1, g4, p2d, wple, g5, wpg, tgt, tm_wide)

    dpre, de, dx2, dff, du, dg5, dg4, dconvb, dconvw8 = _bwd_out(dx3, e, sg, x2, ff, g5, g4, wpg, wdown, up, u, tm)
    dup, dx1, do, dpa, dpb, dgates, dya, dyb, delta_b, dg3, dg2, dbg = _bwd_mid(
        du, convw8, wup, dx2, x1, g3, o, g2, wout, gate, pa, pb, wba, wbb, yb_c, tm)
    late_grads = {
        "w_branch_a": _mm_tn("dw_branch_a", ya_c, dpa),
        "w_branch_b": _mm_tn("dw_branch_b", yb_c, dpb),
        "w_out": _mm_tn("dw_out", mixed, do).reshape(4, D_MODEL // 4, D_MODEL),
        "w_up": _mm_tn("dw_up", h2, dup, column_shards=4),
        "w_down": _mm_tn("dw_down", a, dff).reshape(4, D_FF // 4, D_MODEL),
        "w_ple_gate": _mm_tn("dw_ple_gate", n5, dpre).reshape(4, D_MODEL // 4, D_MODEL),
        "w_ple": _mm_tn("dw_ple", p2d, de),
    }

    def grad_views(grads, group, packed_group):
        return [_halves(grads[n]) for n in group if n in _NATURAL] + [_pack_grads(grads, packed_group)]

    def pair_sums(tag, views, theirs):
        return [_add_pair("rs_%s_add_pair_%d" % (tag, i), g, r, cc) for i, (g, r) in enumerate(zip(views, theirs))]

    swap_sems = _chips_start("swap_late_start", grad_views(late_grads, _LATE, _LATE_PACKED), SIBLING_HALF)
    dqs, dks, dvs, dsink_rows = _swa_bwd(qs, ks, vs, ya, dya, lse_a, swa_bias, sink_rows + swap_sems[4][0:1, 0:1])
    dsink = dsink_rows[:, 0:SWA_GROUP, 0]
    late_views, late_theirs = _chips_wait("swap_late_wait", *swap_sems[:4], SIBLING_HALF, after=dqs)
    rs_sems = _chips_start("scatter_late_start", pair_sums("late", late_views, late_theirs), PIECE)
    dqm, dkm, dvm = _mla_bwd(qm, km, vm, dyb, lse_b, delta_b.reshape(HEADS, 1, t_rows) + rs_sems[4][0:1, 0:1])
    dz, dqb, dx, dgq, dgkv, dg1 = _bwd_in(dqs, dks, dvs, dqm, dkm, dvm, tabs, consts, cq, ckv, gq, gkv, wuq, wk, wv,
                                           dgates, win, x2d, g1, dx1, tm)

    small = {"attn_pre_norm": dg1, "attn_post_norm": dg2, "b_gate": dbg, "sinks": dsink, "q_a_norm": dgq,
             "kv_a_norm": dgkv, "mlp_pre_norm": dg3, "mlp_post_norm": dg4, "conv_b": dconvb, "ple_norm": dg5,
             "conv_w": dconvw8[0:3], "loss": loss_part}
    small_sems = _chips_start("gather_small_start", [_pack_small(small)], EVERYONE)
    small_token = small_sems[4]

    dwk = _unpad_slots(_mm_tn("dw_k", ckvn, dkm, after=small_token), HEADS, NOPE_DIM).reshape(
        KV_LORA, HEADS, NOPE_DIM)
    dwv = _unpad_slots(_mm_tn("dw_v", ckvn, dvm, after=small_token), HEADS, V_DIM).reshape(KV_LORA, HEADS, V_DIM)
    early_grads = {
        "w_in": _unpad_w_in(_mm_tn("dw_in", h1, dz, after=small_token)).reshape(D_MODEL, 4, 808).transpose(1, 0, 2),
        "w_uq": _unpad_slots(_mm_tn("dw_uq", cqn, dqb, after=small_token), HEADS, NOPE_DIM + ROPE_DIM),
        "w_ukv": jnp.concatenate([dwk, dwv], axis=2).reshape(KV_LORA, HEADS * (NOPE_DIM + V_DIM)),
    }

    def finish(tag, pairs, landed, group, packed_group):
        reduced = []
        for i, (pair, land) in enumerate(zip(pairs, landed)):
            own = lax.dynamic_index_in_dim(pair, chip, 0, keepdims=True)
            reduced.append(_add_chips("rs_%s_add_chips_%d" % (tag, i),
                                      lax.dynamic_update_slice(land, own, (chip, 0, 0))))
        others = _swap_sibling("swap_%s_reduced_halves" % tag, reduced)
        r, o = reduced[-1], others[-1]
        packed = jnp.where(cc == 0, jnp.stack([r, o]), jnp.stack([o, r]))
        shards = _unpack_shard_grads(packed, packed_group)
        updates.update(zip(shards, _adamw_many("adamw_%s_packed" % tag, [w2[n] for n in shards], list(shards.values()),
                                               [m2[n] for n in shards], [v2[n] for n in shards])))
        for n, r, o in zip([n for n in group if n in _NATURAL], reduced, others):
            updates[n] = _adamw_halves("adamw_" + n, w2[n], r, o, m2[n], v2[n], cc)

    updates = {}

    early_views = grad_views(early_grads, _EARLY, _EARLY_PACKED)
    early_theirs = _swap_sibling("swap_early_grad_halves", early_views, other_half=True)
    small_sent, small_landed = _chips_wait("gather_small_wait", *small_sems[:4], EVERYONE, after=early_theirs[0])
    small_all = lax.dynamic_update_slice(small_landed[0], small_sent[0][None], (4 * xc + 2 * yc + cc, 0, 0))
    early_sems = _chips_start("scatter_early_start", pair_sums("early", early_views, early_theirs), PIECE,
                              after=small_all)
    late_pairs, late_landed = _chips_wait("scatter_late_wait", *rs_sems[:4], PIECE, after=early_sems[4])
    finish("late", late_pairs, late_landed, _LATE, _LATE_PACKED)
    early_pairs, early_landed = _chips_wait("scatter_early_wait", *early_sems[:4], PIECE,
                                            after=updates[_LATE[-1]][1])
    finish("early", early_pairs, early_landed, _EARLY, _EARLY_PACKED)

    small_sum = _unpack_small(_add_devices(small_all))
    small_names = [n for n in names if n in small_sum]
    small_grads = [lax.dynamic_index_in_dim(small_sum[n].reshape(3, 4, 1408), chip, 1, keepdims=False)
                   if n == "conv_w" else small_sum[n].reshape(w2[n].shape) for n in small_names]
    updates.update(zip(small_names, _adamw_many("adamw_small", [w2[n] for n in small_names], small_grads,
                                                [m2[n] for n in small_names], [v2[n] for n in small_names])))
    loss = small_sum["loss"][0]

    outs = [[updates[n][i].reshape(wts[n].shape) for n in names] for i in range(4)]
    return (loss, dx.reshape(x.shape), *outs[0], *outs[1], *outs[2], *outs[3])
```
